```python
import jax, jax.numpy as jnp
from jax import lax
import numpy as np

D_MODEL = 1024
BATCH = 16
SEQ = 4096
DEPTH = 1

N_META = 16
EXPAND = 2
D_MIX = EXPAND * D_MODEL
POOL_WINDOWS = (2, 4, 8, 16)
N_POOL_GROUPS = len(POOL_WINDOWS)
D_POOL = D_MIX // 4
POOL_GROUP = D_POOL // N_POOL_GROUPS
MAX_WINDOW = max(POOL_WINDOWS)
D_SSM = D_MIX - D_POOL
SSM_HEAD_DIM = 64
N_SSM_HEADS = D_SSM // SSM_HEAD_DIM
N_SSM_GROUPS = 4
HEADS_PER_GROUP = N_SSM_HEADS // N_SSM_GROUPS
D_STATE = 128
CONV_WIDTH = 4
CHUNK = 128
D_XBC = D_SSM + 2 * N_SSM_GROUPS * D_STATE
D_IN_PROJ = D_POOL + D_SSM + D_XBC + N_SSM_HEADS
META_PAD = (-N_META) % CHUNK
D_FF = 4 * D_MODEL
EPS = 1e-5

kernel_name = "hymba_pool_ssd_hybrid"


def rms_norm(x, w):
    x32 = x.astype(jnp.float32)
    y = x32 * lax.rsqrt(jnp.mean(x32 * x32, axis=-1, keepdims=True) + EPS)
    return (y * w.astype(jnp.float32)).astype(x.dtype)


def pool_mixer(u, pool_w, pool_scale):
    bsz, L, _ = u.shape
    ug = u.astype(jnp.float32).reshape(bsz, L, N_POOL_GROUPS, POOL_GROUP)
    csum = jnp.cumsum(ug, axis=1)
    cp = jnp.pad(csum, ((0, 0), (MAX_WINDOW, 0), (0, 0), (0, 0)))
    pos = jnp.arange(L)
    pooled = []
    for gi, w in enumerate(POOL_WINDOWS):
        win_sum = cp[:, MAX_WINDOW:, gi] - cp[:, MAX_WINDOW - w:MAX_WINDOW - w + L, gi]
        count = jnp.minimum(pos + 1, w).astype(jnp.float32)[None, :, None]
        pooled.append(win_sum / count - ug[:, :, gi])
    pooled = jnp.stack(pooled, axis=2)
    mixed = jnp.einsum('blgc,gcd->blgd', pooled, pool_w.astype(jnp.float32))
    out = mixed.reshape(bsz, L, D_POOL) * pool_scale.astype(jnp.float32)
    return out.astype(u.dtype)


def causal_dwconv(x, w, b):
    y = lax.conv_general_dilated(
        x, w[:, None, :].astype(x.dtype), window_strides=(1,),
        padding=[(CONV_WIDTH - 1, 0)], dimension_numbers=('NWC', 'WIO', 'NWC'),
        feature_group_count=x.shape[-1])
    return y + b.astype(x.dtype)


def ssd_chunked(xs, dt, a, bs, cs):
    bsz, T = xs.shape[:2]
    nc = T // CHUNK
    G, R, P, N = N_SSM_GROUPS, HEADS_PER_GROUP, SSM_HEAD_DIM, D_STATE
    x = xs.reshape(bsz, nc, CHUNK, G, R, P)
    d = dt.reshape(bsz, nc, CHUNK, G, R)
    bc = bs.reshape(bsz, nc, CHUNK, G, N)
    cc = cs.reshape(bsz, nc, CHUNK, G, N)
    a_cs = jnp.cumsum(d * a, axis=2)
    xdt = x * d[..., None]
    causal = jnp.tril(jnp.ones((CHUNK, CHUNK), dtype=bool))
    seg = a_cs[:, :, :, None] - a_cs[:, :, None, :]
    decay = jnp.exp(jnp.where(causal[None, None, :, :, None, None], seg, -jnp.inf))
    cb = jnp.einsum('bclgn,bcsgn->bclsg', cc, bc)
    y_diag = jnp.einsum('bclsgr,bcsgrp->bclgrp', cb[..., None] * decay, xdt)
    decay_to_end = jnp.exp(a_cs[:, :, -1:] - a_cs)
    states = jnp.einsum('bclgn,bclgr,bclgrp->bcgrpn', bc, decay_to_end, xdt)
    chunk_decay = jnp.exp(a_cs[:, :, -1])

    def step(carry, inp):
        st, dec = inp
        return carry * dec[..., None, None] + st, carry

    init = jnp.zeros((bsz, G, R, P, N), dtype=xs.dtype)
    _, prev = lax.scan(step, init, (jnp.moveaxis(states, 1, 0), jnp.moveaxis(chunk_decay, 1, 0)))
    prev = jnp.moveaxis(prev, 0, 1)
    y_off = jnp.einsum('bclgn,bcgrpn,bclgr->bclgrp', cc, prev, jnp.exp(a_cs))
    return (y_diag + y_off).reshape(bsz, T, G, R, P)


def ssd_mixer(z, xbc, dt_raw, conv_w, conv_b, dt_bias, a_log, d_skip, ssm_norm_w):
    bsz, L, _ = z.shape
    G, R, P, N = N_SSM_GROUPS, HEADS_PER_GROUP, SSM_HEAD_DIM, D_STATE
    T = L + META_PAD
    xbc_p = jnp.pad(xbc, ((0, 0), (META_PAD, 0), (0, 0)))
    dt_p = jnp.pad(dt_raw, ((0, 0), (META_PAD, 0), (0, 0)))
    xbc_c = jax.nn.silu(causal_dwconv(xbc_p, conv_w, conv_b)).astype(jnp.float32)
    valid = (jnp.arange(T) >= META_PAD)[None, :, None]
    dt = jnp.where(valid, jax.nn.softplus(dt_p.astype(jnp.float32) + dt_bias.astype(jnp.float32)), 0.0)
    xs = xbc_c[..., :D_SSM].reshape(bsz, T, G, R, P)
    bs = xbc_c[..., D_SSM:D_SSM + G * N].reshape(bsz, T, G, N)
    cs = xbc_c[..., D_SSM + G * N:].reshape(bsz, T, G, N)
    a = -jnp.exp(a_log.astype(jnp.float32)).reshape(G, R)
    y = ssd_chunked(xs, dt.reshape(bsz, T, G, R), a, bs, cs)
    y = y + d_skip.astype(jnp.float32).reshape(G, R)[..., None] * xs
    y = y[:, META_PAD:].reshape(bsz, L, G, R * P)
    y = y * jax.nn.silu(z.astype(jnp.float32).reshape(bsz, L, G, R * P))
    y = y * lax.rsqrt(jnp.mean(y * y, axis=-1, keepdims=True) + EPS)
    y = y * ssm_norm_w.astype(jnp.float32).reshape(G, R * P)
    return y.reshape(bsz, L, D_SSM).astype(z.dtype)


def _fwd_setup_inputs(seed: int = 0) -> dict:
    key = jax.random.key(seed)
    ks = jax.random.split(key, 20)
    f32 = jnp.float32
    nrm = lambda k, shape, s: jax.random.normal(k, shape, f32) * s
    dt0 = jnp.exp(jax.random.uniform(ks[8], (DEPTH, N_SSM_HEADS), f32, np.log(1e-3), np.log(1e-1)))
    return {
        "x": nrm(ks[0], (BATCH, SEQ, D_MODEL), 1.0),
        "meta": nrm(ks[1], (N_META, D_MODEL), 1.0),
        "norm_mix_w": 1.0 + nrm(ks[2], (DEPTH, D_MODEL), 0.05),
        "w_in": nrm(ks[3], (DEPTH, D_MODEL, D_IN_PROJ), D_MODEL ** -0.5),
        "pool_w": nrm(ks[4], (DEPTH, N_POOL_GROUPS, POOL_GROUP, POOL_GROUP), POOL_GROUP ** -0.5),
        "pool_scale": 1.0 + nrm(ks[5], (DEPTH, D_POOL), 0.1),
        "conv_w": nrm(ks[6], (DEPTH, CONV_WIDTH, D_XBC), CONV_WIDTH ** -0.5),
        "conv_b": nrm(ks[7], (DEPTH, D_XBC), 0.02),
        "dt_bias": dt0 + jnp.log(-jnp.expm1(-dt0)),
        "a_log": jnp.log(jax.random.uniform(ks[9], (DEPTH, N_SSM_HEADS), f32, 1.0, 16.0)),
        "d_skip": 1.0 + nrm(ks[10], (DEPTH, N_SSM_HEADS), 0.1),
        "ssm_norm_w": 1.0 + nrm(ks[11], (DEPTH, D_SSM), 0.05),
        "w_out": nrm(ks[12], (DEPTH, D_MIX, D_MODEL), D_MIX ** -0.5),
        "norm_ffn_w": 1.0 + nrm(ks[13], (DEPTH, D_MODEL), 0.05),
        "w_ff1": nrm(ks[14], (DEPTH, D_MODEL, D_FF), D_MODEL ** -0.5),
        "w_ff2": nrm(ks[15], (DEPTH, D_FF, D_MODEL), D_FF ** -0.5),
        "norm_f_w": 1.0 + nrm(ks[16], (D_MODEL,), 0.05),
    }


def _fwd_reference(x, meta, norm_mix_w, w_in, pool_w, pool_scale, conv_w, conv_b, dt_bias, a_log,
              d_skip, ssm_norm_w, w_out, norm_ffn_w, w_ff1, w_ff2, norm_f_w):
    bsz = x.shape[0]
    h = jnp.concatenate(
        [jnp.broadcast_to(meta[None].astype(x.dtype), (bsz, N_META, D_MODEL)), x], axis=1)
    for i in range(DEPTH):
        hn = rms_norm(h, norm_mix_w[i])
        proj = jnp.einsum('bld,de->ble', hn, w_in[i])
        u = proj[..., :D_POOL]
        z = proj[..., D_POOL:D_POOL + D_SSM]
        xbc = proj[..., D_POOL + D_SSM:D_POOL + D_SSM + D_XBC]
        dt_raw = proj[..., D_POOL + D_SSM + D_XBC:]
        y_pool = pool_mixer(u, pool_w[i], pool_scale[i])
        y_ssm = ssd_mixer(z, xbc, dt_raw, conv_w[i], conv_b[i], dt_bias[i], a_log[i],
                          d_skip[i], ssm_norm_w[i])
        y = jnp.concatenate([y_pool, y_ssm], axis=-1)
        h = h + jnp.einsum('ble,ed->bld', y, w_out[i])
        hn = rms_norm(h, norm_ffn_w[i])
        ff = jnp.square(jax.nn.relu(jnp.einsum('bld,df->blf', hn, w_ff1[i])))
        h = h + jnp.einsum('blf,fd->bld', ff, w_ff2[i])
    out = rms_norm(h, norm_f_w)
    return out[:, N_META:]


import jax as _jax
import jax.numpy as _jnp

TWIN_FORMAT = 'train_step'
FWD_PARAMS = ['x', 'meta', 'norm_mix_w', 'w_in', 'pool_w', 'pool_scale', 'conv_w', 'conv_b', 'dt_bias', 'a_log', 'd_skip', 'ssm_norm_w', 'w_out', 'norm_ffn_w', 'w_ff1', 'w_ff2', 'norm_f_w']
TWIN_WEIGHTS = ['meta', 'norm_mix_w', 'w_in', 'pool_w', 'pool_scale', 'conv_w', 'conv_b', 'dt_bias', 'a_log', 'd_skip', 'ssm_norm_w', 'w_out', 'norm_ffn_w', 'w_ff1', 'w_ff2', 'norm_f_w']
TWIN_DIFF_INPUT = 'x'
TWIN_INPUTS = ['x', 'meta', 'norm_mix_w', 'w_in', 'pool_w', 'pool_scale', 'conv_w', 'conv_b', 'dt_bias', 'a_log', 'd_skip', 'ssm_norm_w', 'w_out', 'norm_ffn_w', 'w_ff1', 'w_ff2', 'norm_f_w', 'loss_target', 'm_meta', 'm_norm_mix_w', 'm_w_in', 'm_pool_w', 'm_pool_scale', 'm_conv_w', 'm_conv_b', 'm_dt_bias', 'm_a_log', 'm_d_skip', 'm_ssm_norm_w', 'm_w_out', 'm_norm_ffn_w', 'm_w_ff1', 'm_w_ff2', 'm_norm_f_w', 'v_meta', 'v_norm_mix_w', 'v_w_in', 'v_pool_w', 'v_pool_scale', 'v_conv_w', 'v_conv_b', 'v_dt_bias', 'v_a_log', 'v_d_skip', 'v_ssm_norm_w', 'v_w_out', 'v_norm_ffn_w', 'v_w_ff1', 'v_w_ff2', 'v_norm_f_w']
TWIN_OUTPUTS = ['loss', 'grad_x', 'grad_meta', 'grad_norm_mix_w', 'grad_w_in', 'grad_pool_w', 'grad_pool_scale', 'grad_conv_w', 'grad_conv_b', 'grad_dt_bias', 'grad_a_log', 'grad_d_skip', 'grad_ssm_norm_w', 'grad_w_out', 'grad_norm_ffn_w', 'grad_w_ff1', 'grad_w_ff2', 'grad_norm_f_w', 'delta_meta', 'delta_norm_mix_w', 'delta_w_in', 'delta_pool_w', 'delta_pool_scale', 'delta_conv_w', 'delta_conv_b', 'delta_dt_bias', 'delta_a_log', 'delta_d_skip', 'delta_ssm_norm_w', 'delta_w_out', 'delta_norm_ffn_w', 'delta_w_ff1', 'delta_w_ff2', 'delta_norm_f_w', 'new_m_meta', 'new_m_norm_mix_w', 'new_m_w_in', 'new_m_pool_w', 'new_m_pool_scale', 'new_m_conv_w', 'new_m_conv_b', 'new_m_dt_bias', 'new_m_a_log', 'new_m_d_skip', 'new_m_ssm_norm_w', 'new_m_w_out', 'new_m_norm_ffn_w', 'new_m_w_ff1', 'new_m_w_ff2', 'new_m_norm_f_w', 'new_v_meta', 'new_v_norm_mix_w', 'new_v_w_in', 'new_v_pool_w', 'new_v_pool_scale', 'new_v_conv_w', 'new_v_conv_b', 'new_v_dt_bias', 'new_v_a_log', 'new_v_d_skip', 'new_v_ssm_norm_w', 'new_v_w_out', 'new_v_norm_ffn_w', 'new_v_w_ff1', 'new_v_w_ff2', 'new_v_norm_f_w']
TWIN_LEAF_KINDS = {'loss': 'loss', 'grad_x': 'grad_x', 'grad_meta': 'grad_w', 'grad_norm_mix_w': 'grad_w', 'grad_w_in': 'grad_w', 'grad_pool_w': 'grad_w', 'grad_pool_scale': 'grad_w', 'grad_conv_w': 'grad_w', 'grad_conv_b': 'grad_w', 'grad_dt_bias': 'grad_w', 'grad_a_log': 'grad_w', 'grad_d_skip': 'grad_w', 'grad_ssm_norm_w': 'grad_w', 'grad_w_out': 'grad_w', 'grad_norm_ffn_w': 'grad_w', 'grad_w_ff1': 'grad_w', 'grad_w_ff2': 'grad_w', 'grad_norm_f_w': 'grad_w', 'delta_meta': 'delta_w', 'delta_norm_mix_w': 'delta_w', 'delta_w_in': 'delta_w', 'delta_pool_w': 'delta_w', 'delta_pool_scale': 'delta_w', 'delta_conv_w': 'delta_w', 'delta_conv_b': 'delta_w', 'delta_dt_bias': 'delta_w', 'delta_a_log': 'delta_w', 'delta_d_skip': 'delta_w', 'delta_ssm_norm_w': 'delta_w', 'delta_w_out': 'delta_w', 'delta_norm_ffn_w': 'delta_w', 'delta_w_ff1': 'delta_w', 'delta_w_ff2': 'delta_w', 'delta_norm_f_w': 'delta_w', 'new_m_meta': 'new_m', 'new_m_norm_mix_w': 'new_m', 'new_m_w_in': 'new_m', 'new_m_pool_w': 'new_m', 'new_m_pool_scale': 'new_m', 'new_m_conv_w': 'new_m', 'new_m_conv_b': 'new_m', 'new_m_dt_bias': 'new_m', 'new_m_a_log': 'new_m', 'new_m_d_skip': 'new_m', 'new_m_ssm_norm_w': 'new_m', 'new_m_w_out': 'new_m', 'new_m_norm_ffn_w': 'new_m', 'new_m_w_ff1': 'new_m', 'new_m_w_ff2': 'new_m', 'new_m_norm_f_w': 'new_m', 'new_v_meta': 'new_v', 'new_v_norm_mix_w': 'new_v', 'new_v_w_in': 'new_v', 'new_v_pool_w': 'new_v', 'new_v_pool_scale': 'new_v', 'new_v_conv_w': 'new_v', 'new_v_conv_b': 'new_v', 'new_v_dt_bias': 'new_v', 'new_v_a_log': 'new_v', 'new_v_d_skip': 'new_v', 'new_v_ssm_norm_w': 'new_v', 'new_v_w_out': 'new_v', 'new_v_norm_ffn_w': 'new_v', 'new_v_w_ff1': 'new_v', 'new_v_w_ff2': 'new_v', 'new_v_norm_f_w': 'new_v'}


def _forward(args):
    return _fwd_reference(*[args[k] for k in FWD_PARAMS])


def _output_shape():
    out = _jax.eval_shape(lambda: _forward(_fwd_setup_inputs(0)))
    return out.shape, out.dtype

N_MICROBATCH = 1
ADAM_LR = 0.001
ADAM_B1 = 0.9
ADAM_B2 = 0.999
ADAM_EPS = 1e-08
ADAM_WD = 0.01
ADAM_STEP = 10
PER_EXAMPLE_BATCH_AXIS = {'x': 0, 'loss_target': 0}
SHARED_INPUTS = []
_WEIGHT_DTYPES = {'meta': _jnp.float32, 'norm_mix_w': _jnp.float32, 'w_in': _jnp.float32, 'pool_w': _jnp.float32, 'pool_scale': _jnp.float32, 'conv_w': _jnp.float32, 'conv_b': _jnp.float32, 'dt_bias': _jnp.float32, 'a_log': _jnp.float32, 'd_skip': _jnp.float32, 'ssm_norm_w': _jnp.float32, 'w_out': _jnp.float32, 'norm_ffn_w': _jnp.float32, 'w_ff1': _jnp.float32, 'w_ff2': _jnp.float32, 'norm_f_w': _jnp.float32}
MOMENT_SCALE = {'meta': 4.178860e-03, 'norm_mix_w': 2.582019e-01, 'w_in': 1.236155e-01, 'pool_w': 1.241161e-01, 'pool_scale': 1.324668e-01, 'conv_w': 1.189168e-01, 'conv_b': 2.305440e-01, 'dt_bias': 2.531806e-01, 'a_log': 4.280830e-01, 'd_skip': 1.728481e+00, 'ssm_norm_w': 1.577881e-01, 'w_out': 2.088703e-01, 'norm_ffn_w': 2.100496e-01, 'w_ff1': 1.008856e-01, 'w_ff2': 3.008027e-01, 'norm_f_w': 6.445708e+01}


def _to_microbatches(a, axis):
    t = _jnp.moveaxis(a, axis, 0)
    t = t.reshape((N_MICROBATCH, t.shape[0] // N_MICROBATCH) + t.shape[1:])
    return _jnp.moveaxis(t, 1, axis + 1)


def setup_inputs(seed: int = 0) -> dict:
    inp = _fwd_setup_inputs(seed)
    key = _jax.random.fold_in(_jax.random.key(seed), 7919)
    shape, _ = _output_shape()
    out = dict(inp)
    out["loss_target"] = _jax.random.normal(_jax.random.fold_in(key, 0), shape, _jnp.float32)
    for i, name in enumerate(TWIN_WEIGHTS):
        w = inp[name].astype(_jnp.float32)
        if MOMENT_SCALE is None:
            s = _jnp.sqrt(_jnp.mean(_jnp.square(w)) + 1e-30)
        else:
            s = MOMENT_SCALE[name]
        km, kv = _jax.random.split(_jax.random.fold_in(key, i + 1))
        out[name] = w
        out["m_" + name] = s * _jax.random.normal(km, w.shape, _jnp.float32)
        out["v_" + name] = (s * s) * _jax.random.uniform(kv, w.shape, _jnp.float32, 0.5, 1.5)
    if N_MICROBATCH > 1:
        for name, axis in PER_EXAMPLE_BATCH_AXIS.items():
            out[name] = _to_microbatches(out[name], axis)
    return {'x': out['x'], 'meta': out['meta'], 'norm_mix_w': out['norm_mix_w'], 'w_in': out['w_in'], 'pool_w': out['pool_w'], 'pool_scale': out['pool_scale'], 'conv_w': out['conv_w'], 'conv_b': out['conv_b'], 'dt_bias': out['dt_bias'], 'a_log': out['a_log'], 'd_skip': out['d_skip'], 'ssm_norm_w': out['ssm_norm_w'], 'w_out': out['w_out'], 'norm_ffn_w': out['norm_ffn_w'], 'w_ff1': out['w_ff1'], 'w_ff2': out['w_ff2'], 'norm_f_w': out['norm_f_w'], 'loss_target': out['loss_target'], 'm_meta': out['m_meta'], 'm_norm_mix_w': out['m_norm_mix_w'], 'm_w_in': out['m_w_in'], 'm_pool_w': out['m_pool_w'], 'm_pool_scale': out['m_pool_scale'], 'm_conv_w': out['m_conv_w'], 'm_conv_b': out['m_conv_b'], 'm_dt_bias': out['m_dt_bias'], 'm_a_log': out['m_a_log'], 'm_d_skip': out['m_d_skip'], 'm_ssm_norm_w': out['m_ssm_norm_w'], 'm_w_out': out['m_w_out'], 'm_norm_ffn_w': out['m_norm_ffn_w'], 'm_w_ff1': out['m_w_ff1'], 'm_w_ff2': out['m_w_ff2'], 'm_norm_f_w': out['m_norm_f_w'], 'v_meta': out['v_meta'], 'v_norm_mix_w': out['v_norm_mix_w'], 'v_w_in': out['v_w_in'], 'v_pool_w': out['v_pool_w'], 'v_pool_scale': out['v_pool_scale'], 'v_conv_w': out['v_conv_w'], 'v_conv_b': out['v_conv_b'], 'v_dt_bias': out['v_dt_bias'], 'v_a_log': out['v_a_log'], 'v_d_skip': out['v_d_skip'], 'v_ssm_norm_w': out['v_ssm_norm_w'], 'v_w_out': out['v_w_out'], 'v_norm_ffn_w': out['v_norm_ffn_w'], 'v_w_ff1': out['v_w_ff1'], 'v_w_ff2': out['v_w_ff2'], 'v_norm_f_w': out['v_norm_f_w']}


def _loss(weights, diff, rest, loss_target):
    with _jax.named_scope("forward"):
        args = {**rest, TWIN_DIFF_INPUT: diff, **{k: w.astype(_WEIGHT_DTYPES[k]) for k, w in weights.items()}}
        y = _forward(args)
    with _jax.named_scope("loss_head"):
        err = _jnp.square(y.astype(_jnp.float32) - loss_target)
        return 0.5 * _jnp.sum(_jnp.mean(err, axis=-1)) if err.ndim else 0.5 * err


def _adamw(w, g, m, v):
    m = ADAM_B1 * m + (1.0 - ADAM_B1) * g
    v = ADAM_B2 * v + (1.0 - ADAM_B2) * _jnp.square(g)
    m_hat = m / (1.0 - ADAM_B1 ** ADAM_STEP)
    v_hat = v / (1.0 - ADAM_B2 ** ADAM_STEP)
    delta = -ADAM_LR * (m_hat / (_jnp.sqrt(v_hat) + ADAM_EPS) + ADAM_WD * w)
    return delta, m, v


def reference(x, meta, norm_mix_w, w_in, pool_w, pool_scale, conv_w, conv_b, dt_bias, a_log, d_skip, ssm_norm_w, w_out, norm_ffn_w, w_ff1, w_ff2, norm_f_w, loss_target, m_meta, m_norm_mix_w, m_w_in, m_pool_w, m_pool_scale, m_conv_w, m_conv_b, m_dt_bias, m_a_log, m_d_skip, m_ssm_norm_w, m_w_out, m_norm_ffn_w, m_w_ff1, m_w_ff2, m_norm_f_w, v_meta, v_norm_mix_w, v_w_in, v_pool_w, v_pool_scale, v_conv_w, v_conv_b, v_dt_bias, v_a_log, v_d_skip, v_ssm_norm_w, v_w_out, v_norm_ffn_w, v_w_ff1, v_w_ff2, v_norm_f_w):
    given = dict(x=x, meta=meta, norm_mix_w=norm_mix_w, w_in=w_in, pool_w=pool_w, pool_scale=pool_scale, conv_w=conv_w, conv_b=conv_b, dt_bias=dt_bias, a_log=a_log, d_skip=d_skip, ssm_norm_w=ssm_norm_w, w_out=w_out, norm_ffn_w=norm_ffn_w, w_ff1=w_ff1, w_ff2=w_ff2, norm_f_w=norm_f_w, loss_target=loss_target, m_meta=m_meta, m_norm_mix_w=m_norm_mix_w, m_w_in=m_w_in, m_pool_w=m_pool_w, m_pool_scale=m_pool_scale, m_conv_w=m_conv_w, m_conv_b=m_conv_b, m_dt_bias=m_dt_bias, m_a_log=m_a_log, m_d_skip=m_d_skip, m_ssm_norm_w=m_ssm_norm_w, m_w_out=m_w_out, m_norm_ffn_w=m_norm_ffn_w, m_w_ff1=m_w_ff1, m_w_ff2=m_w_ff2, m_norm_f_w=m_norm_f_w, v_meta=v_meta, v_norm_mix_w=v_norm_mix_w, v_w_in=v_w_in, v_pool_w=v_pool_w, v_pool_scale=v_pool_scale, v_conv_w=v_conv_w, v_conv_b=v_conv_b, v_dt_bias=v_dt_bias, v_a_log=v_a_log, v_d_skip=v_d_skip, v_ssm_norm_w=v_ssm_norm_w, v_w_out=v_w_out, v_norm_ffn_w=v_norm_ffn_w, v_w_ff1=v_w_ff1, v_w_ff2=v_w_ff2, v_norm_f_w=v_norm_f_w)
    weights = {n: given[n] for n in TWIN_WEIGHTS}
    shared = {n: given[n] for n in SHARED_INPUTS}
    per_example = {n: given[n] for n in ['x']}
    grad_fn = _jax.value_and_grad(_loss, argnums=(0, 1))

    def one_microbatch(ex, loss_target):
        ex = dict(ex)
        diff = ex.pop(TWIN_DIFF_INPUT)
        return grad_fn(weights, diff, {**shared, **ex}, loss_target)

    if N_MICROBATCH == 1:
        loss, (grad_w, grad_x) = one_microbatch(per_example, given["loss_target"])
    else:
        def body(carry, xs):
            loss_sum, grad_sum = carry
            l_k, (gw_k, gx_k) = one_microbatch(xs[0], xs[1])
            with _jax.named_scope("update"):
                return (loss_sum + l_k, _jax.tree.map(_jnp.add, grad_sum, gw_k)), gx_k

        init = (_jnp.zeros((), _jnp.float32), _jax.tree.map(_jnp.zeros_like, weights))
        (loss, grad_w), grad_x = _jax.lax.scan(body, init, (per_example, given["loss_target"]))
    with _jax.named_scope("update"):
        delta_w, new_m, new_v = {}, {}, {}
        for n in TWIN_WEIGHTS:
            delta_w[n], new_m[n], new_v[n] = _adamw(weights[n], grad_w[n], given["m_" + n], given["v_" + n])
    return (loss, grad_x, *[grad_w[n] for n in TWIN_WEIGHTS], *[delta_w[n] for n in TWIN_WEIGHTS],
            *[new_m[n] for n in TWIN_WEIGHTS], *[new_v[n] for n in TWIN_WEIGHTS])
```

```python
import functools

import jax
import jax.numpy as jnp
from jax import lax
from jax.experimental import pallas as pl
from jax.experimental.pallas import tpu as pltpu

F32 = jnp.float32
BF16 = jnp.bfloat16
MESH = pl.DeviceIdType.MESH
ANY = pl.BlockSpec(memory_space=pl.ANY)

D_MODEL = 1024
N_META = 16
CHUNK = 128
PAD = CHUNK - N_META
POOL_WINDOWS = (2, 4, 8, 16)
D_POOL = 512
POOL_GROUP = 128
D_SSM = 1536
N_HEADS = 24
N_GROUPS = 4
HPG = 6
HEAD_DIM = 64
D_STATE = 128
GW = HPG * HEAD_DIM
D_XBC = D_SSM + 2 * N_GROUPS * D_STATE
D_DT = N_GROUPS * 128
D_FF = 4096
CONV_W = 4
EPS = 1e-5
LANES = 128
VMEM_LIMIT = 56 * 1024 * 1024

ADAM_LR, ADAM_B1, ADAM_B2, ADAM_EPS, ADAM_WD, ADAM_STEP = 0.001, 0.9, 0.999, 1e-08, 0.01, 10


def _params(*sem):
    return pltpu.CompilerParams(dimension_semantics=sem, vmem_limit_bytes=VMEM_LIMIT)


def _pick(n, cands):
    for c in cands:
        if n % c == 0:
            return c
    raise ValueError(f"no block size for {n}")


def _dot(a, b):
    return jnp.dot(a.astype(BF16), b.astype(BF16), preferred_element_type=F32)


def _dot_nt(a, b):
    return lax.dot_general(a.astype(BF16), b.astype(BF16), (((1,), (1,)), ((), ())), preferred_element_type=F32)


def _dot_tn(a, b):
    return lax.dot_general(a.astype(BF16), b.astype(BF16), (((0,), (0,)), ((), ())), preferred_element_type=F32)


def _dot_exact(mask, x):
    m = mask.astype(BF16)
    hi = x.astype(BF16)
    r1 = x - hi.astype(F32)
    mid = r1.astype(BF16)
    lo = (r1 - mid.astype(F32)).astype(BF16)
    dot = lambda t: jnp.dot(m, t, preferred_element_type=F32)
    return dot(hi) + dot(mid) + dot(lo)


def _sigmoid(x):
    return 1.0 / (1.0 + jnp.exp(-x))


def _softplus(x):
    return jnp.maximum(x, 0.0) + jnp.log1p(jnp.exp(-jnp.abs(x)))


def _sum_all(x):
    return jnp.sum(jnp.sum(x, axis=1, keepdims=True), axis=0, keepdims=True)


def _mm(a, w, *, name, tm, tn=512, nt=False, pre=None, post=None, extras=(), out_dtype=F32):
    n, k = a.shape
    m = w.shape[0] if nt else w.shape[1]
    tn = min(tn, m)
    n_ex = len(extras)

    def body(a_ref, w_ref, *rest):
        o_ref = rest[n_ex]
        av = a_ref[...]
        if pre is not None:
            av = pre(av)
        r = _dot_nt(av, w_ref[...]) if nt else _dot(av, w_ref[...])
        if post is not None:
            r = post(r, *[e[...] for e in rest[:n_ex]])
        o_ref[...] = r.astype(out_dtype)

    w_spec = pl.BlockSpec((tn, k), lambda i, j: (j, 0)) if nt else pl.BlockSpec((k, tn), lambda i, j: (0, j))
    blk = pl.BlockSpec((tm, tn), lambda i, j: (i, j))
    return pl.pallas_call(
        body, name=name, grid=(n // tm, m // tn),
        in_specs=[pl.BlockSpec((tm, k), lambda i, j: (i, 0)), w_spec] + [blk] * n_ex,
        out_specs=blk, out_shape=jax.ShapeDtypeStruct((n, m), out_dtype),
        compiler_params=_params("parallel", "parallel"),
    )(a, w, *extras)


def _mm_tn(a, g, *, name, tk, tn, tm, pre=None):
    n, k = a.shape
    m = g.shape[1]
    tk, tn = min(tk, k), min(tn, m)

    def body(a_ref, g_ref, o_ref):
        @pl.when(pl.program_id(2) == 0)
        def _():
            o_ref[...] = jnp.zeros_like(o_ref)

        av = a_ref[...]
        if pre is not None:
            av = pre(av)
        o_ref[...] += _dot_tn(av, g_ref[...])

    return pl.pallas_call(
        body, name=name, grid=(k // tk, m // tn, n // tm),
        in_specs=[pl.BlockSpec((tm, tk), lambda i, j, r: (r, i)), pl.BlockSpec((tm, tn), lambda i, j, r: (r, j))],
        out_specs=pl.BlockSpec((tk, tn), lambda i, j, r: (i, j)),
        out_shape=jax.ShapeDtypeStruct((k, m), F32),
        compiler_params=_params("parallel", "parallel", "arbitrary"),
    )(a, g)


def _rms_fwd(h, w, *, name):
    n, d = h.shape
    tm = _pick(n, (768, 512, 256, 128))

    def body(h_ref, w_ref, o_ref):
        x = h_ref[...]
        r = lax.rsqrt(jnp.mean(x * x, axis=-1, keepdims=True) + EPS)
        o_ref[...] = (x * r * w_ref[...]).astype(BF16)

    return pl.pallas_call(
        body, name=name, grid=(n // tm,),
        in_specs=[pl.BlockSpec((tm, d), lambda i: (i, 0)), pl.BlockSpec((1, d), lambda i: (0, 0))],
        out_specs=pl.BlockSpec((tm, d), lambda i: (i, 0)), out_shape=jax.ShapeDtypeStruct((n, d), BF16),
        compiler_params=_params("parallel"),
    )(h, w)


def _rms_bwd(dy, h, w, dres, *, name):
    n, d = h.shape
    tm = _pick(n, (768, 512, 256, 128))

    def body(dy_ref, h_ref, w_ref, dres_ref, dx_ref, dw_ref):
        @pl.when(pl.program_id(0) == 0)
        def _():
            dw_ref[...] = jnp.zeros_like(dw_ref)

        x, dyv = h_ref[...], dy_ref[...]
        r = lax.rsqrt(jnp.mean(x * x, axis=-1, keepdims=True) + EPS)
        g = dyv * w_ref[...]
        dx_ref[...] = r * (g - x * (r * r) * jnp.mean(g * x, axis=-1, keepdims=True)) + dres_ref[...]
        dw_ref[...] += jnp.sum(dyv * x * r, axis=0, keepdims=True)

    row = pl.BlockSpec((tm, d), lambda i: (i, 0))
    vec = pl.BlockSpec((1, d), lambda i: (0, 0))
    return pl.pallas_call(
        body, name=name, grid=(n // tm,), in_specs=[row, row, vec, row], out_specs=[row, vec],
        out_shape=[jax.ShapeDtypeStruct((n, d), F32), jax.ShapeDtypeStruct((1, d), F32)],
        compiler_params=_params("arbitrary"),
    )(dy, h, w, dres)


def _final_norm_loss(h2, target, w, *, name):
    bsz, t, d = h2.shape
    nc = t // CHUNK

    def body(h_ref, t_ref, w_ref, dh_ref, loss_ref, dw_ref):
        j = pl.program_id(1)

        @pl.when((pl.program_id(0) == 0) & (j == 0))
        def _():
            loss_ref[...] = jnp.zeros_like(loss_ref)
            dw_ref[...] = jnp.zeros_like(dw_ref)

        x, wv = h_ref[0], w_ref[...]
        r = lax.rsqrt(jnp.mean(x * x, axis=-1, keepdims=True) + EPS)
        diff = jnp.where(j > 0, x * r * wv - t_ref[0], 0.0)
        loss_ref[...] += _sum_all(diff * diff) * (0.5 / d)
        dy = diff * (1.0 / d)
        g = dy * wv
        dh_ref[0] = r * (g - x * (r * r) * jnp.mean(g * x, axis=-1, keepdims=True))
        dw_ref[...] += jnp.sum(dy * x * r, axis=0, keepdims=True)

    return pl.pallas_call(
        body, name=name, grid=(bsz, nc),
        in_specs=[pl.BlockSpec((1, CHUNK, d), lambda b, j: (b, j, 0)),
                  pl.BlockSpec((1, CHUNK, d), lambda b, j: (b, jnp.maximum(j - 1, 0), 0)),
                  pl.BlockSpec((1, d), lambda b, j: (0, 0))],
        out_specs=[pl.BlockSpec((1, CHUNK, d), lambda b, j: (b, j, 0)),
                   pl.BlockSpec((8, LANES), lambda b, j: (0, 0)),
                   pl.BlockSpec((1, d), lambda b, j: (0, 0))],
        out_shape=[jax.ShapeDtypeStruct((bsz, t, d), F32), jax.ShapeDtypeStruct((8, LANES), F32),
                   jax.ShapeDtypeStruct((1, d), F32)],
        compiler_params=_params("arbitrary", "arbitrary"),
    )(h2, target, w)


def _pool_masks(j, transposed):
    r = lax.broadcasted_iota(jnp.int32, (CHUNK, 2 * CHUNK), 0)
    c = lax.broadcasted_iota(jnp.int32, (CHUNK, 2 * CHUNK), 1)
    masks = []
    for w in POOL_WINDOWS:
        if transposed:
            m = (c >= r) & (c < r + w)
        else:
            s = c - CHUNK
            m = (s <= r) & (s > r - w) & (s + j * CHUNK >= 0)
        masks.append(m.astype(F32))
    return masks


def _pool_count(t_global, w):
    return jnp.clip(t_global - PAD + 1, 1, w).astype(F32)


def _pool_fwd(u, pool_w, pool_scale, *, name):
    bsz, t, _ = u.shape
    nc = t // CHUNK

    def body(prev_ref, cur_ref, pw_ref, sc_ref, o_ref):
        j = pl.program_id(1)
        masks = _pool_masks(j, False)
        tg = j * CHUNK + lax.broadcasted_iota(jnp.int32, (CHUNK, 1), 0)
        for gi, w in enumerate(POOL_WINDOWS):
            sl = pl.ds(gi * POOL_GROUP, POOL_GROUP)
            cur = cur_ref[0, :, sl]
            both = jnp.concatenate([prev_ref[0, :, sl], cur], axis=0)
            pooled = _dot_exact(masks[gi], both) / _pool_count(tg, w) - cur
            o_ref[0, :, sl] = (_dot(pooled, pw_ref[gi]) * sc_ref[:, sl]).astype(BF16)

    blk = lambda f: pl.BlockSpec((1, CHUNK, D_POOL), f)
    return pl.pallas_call(
        body, name=name, grid=(bsz, nc),
        in_specs=[blk(lambda b, j: (b, jnp.maximum(j - 1, 0), 0)), blk(lambda b, j: (b, j, 0)),
                  pl.BlockSpec((4, POOL_GROUP, POOL_GROUP), lambda b, j: (0, 0, 0)),
                  pl.BlockSpec((1, D_POOL), lambda b, j: (0, 0))],
        out_specs=blk(lambda b, j: (b, j, 0)), out_shape=jax.ShapeDtypeStruct(u.shape, BF16),
        compiler_params=_params("parallel", "parallel"),
    )(u, u, pool_w, pool_scale)


def _pool_bwd(u, dyp, pool_w, pool_scale, *, name):
    bsz, t, _ = u.shape
    nc = t // CHUNK

    def body(prev_ref, cur_ref, dy_ref, dyn_ref, pw_ref, sc_ref, du_ref, dpw_ref, dsc_ref):
        j = pl.program_id(1)

        @pl.when(j == 0)
        def _():
            dpw_ref[...] = jnp.zeros_like(dpw_ref)
            dsc_ref[...] = jnp.zeros_like(dsc_ref)

        fwd = _pool_masks(j, False)
        bwd = _pool_masks(j, True)
        tg = j * CHUNK + lax.broadcasted_iota(jnp.int32, (CHUNK, 1), 0)
        has_next = j < nc - 1
        for gi, w in enumerate(POOL_WINDOWS):
            sl = pl.ds(gi * POOL_GROUP, POOL_GROUP)
            cur, pw, sc = cur_ref[0, :, sl], pw_ref[gi], sc_ref[:, sl]
            both = jnp.concatenate([prev_ref[0, :, sl], cur], axis=0)
            pooled = _dot_exact(fwd[gi], both) / _pool_count(tg, w) - cur
            dy = dy_ref[0, :, sl]
            dsc_ref[0, :, sl] += jnp.sum(dy * _dot(pooled, pw), axis=0, keepdims=True)
            dm = dy * sc
            dpw_ref[0, gi] += _dot_tn(pooled, dm)
            dpooled = _dot_nt(dm, pw)
            dpooled_next = _dot_nt(jnp.where(has_next, dyn_ref[0, :, sl], 0.0) * sc, pw)
            e = jnp.concatenate([dpooled / _pool_count(tg, w), dpooled_next / _pool_count(tg + CHUNK, w)], axis=0)
            du_ref[0, :, sl] = _dot_exact(bwd[gi], e) - dpooled

    blk = lambda f: pl.BlockSpec((1, CHUNK, D_POOL), f)
    return pl.pallas_call(
        body, name=name, grid=(bsz, nc),
        in_specs=[blk(lambda b, j: (b, jnp.maximum(j - 1, 0), 0)), blk(lambda b, j: (b, j, 0)),
                  blk(lambda b, j: (b, j, 0)), blk(lambda b, j: (b, jnp.minimum(j + 1, nc - 1), 0)),
                  pl.BlockSpec((4, POOL_GROUP, POOL_GROUP), lambda b, j: (0, 0, 0)),
                  pl.BlockSpec((1, D_POOL), lambda b, j: (0, 0))],
        out_specs=[blk(lambda b, j: (b, j, 0)),
                   pl.BlockSpec((1, 4, POOL_GROUP, POOL_GROUP), lambda b, j: (b, 0, 0, 0)),
                   pl.BlockSpec((1, 1, D_POOL), lambda b, j: (b, 0, 0))],
        out_shape=[jax.ShapeDtypeStruct(u.shape, F32), jax.ShapeDtypeStruct((bsz, 4, POOL_GROUP, POOL_GROUP), F32),
                   jax.ShapeDtypeStruct((bsz, 1, D_POOL), F32)],
        compiler_params=_params("parallel", "arbitrary"),
    )(u, u, dyp, dyp, pool_w, pool_scale)


def _conv_taps(buf_ref, w_ref):
    acc = None
    for k in range(CONV_W):
        term = w_ref[k:k + 1, :] * buf_ref[pl.ds(8 - (CONV_W - 1) + k, CHUNK), :]
        acc = term if acc is None else acc + term
    return acc


def _conv_fwd(xbc, conv_w, conv_b, *, name):
    bsz, t, c = xbc.shape
    nc = t // CHUNK

    def body(tail_ref, cur_ref, w_ref, b_ref, o_ref, buf_ref):
        j = pl.program_id(1)
        buf_ref[pl.ds(0, 8), :] = jnp.where(j > 0, tail_ref[0], 0.0)
        buf_ref[pl.ds(8, CHUNK), :] = cur_ref[0]
        pre = _conv_taps(buf_ref, w_ref) + b_ref[...]
        o_ref[0] = pre * _sigmoid(pre)

    return pl.pallas_call(
        body, name=name, grid=(bsz, nc),
        in_specs=[pl.BlockSpec((1, 8, c), lambda b, j: (b, jnp.maximum(j * (CHUNK // 8) - 1, 0), 0)),
                  pl.BlockSpec((1, CHUNK, c), lambda b, j: (b, j, 0)),
                  pl.BlockSpec((CONV_W, c), lambda b, j: (0, 0)), pl.BlockSpec((1, c), lambda b, j: (0, 0))],
        out_specs=pl.BlockSpec((1, CHUNK, c), lambda b, j: (b, j, 0)), out_shape=jax.ShapeDtypeStruct(xbc.shape, F32),
        scratch_shapes=[pltpu.VMEM((CHUNK + 8, c), F32)],
        compiler_params=_params("parallel", "parallel"),
    )(xbc, xbc, conv_w, conv_b)


def _conv_bwd_pre(xbc, dxs, db, dc, conv_w, conv_b, *, name):
    bsz, t, c = xbc.shape
    nc = t // CHUNK

    def body(tail_ref, cur_ref, dxs_ref, db_ref, dc_ref, w_ref, b_ref, dpre_ref, dwb_ref, buf_ref):
        j = pl.program_id(1)

        @pl.when(j == 0)
        def _():
            dwb_ref[...] = jnp.zeros_like(dwb_ref)

        buf_ref[pl.ds(0, 8), :] = jnp.where(j > 0, tail_ref[0], 0.0)
        buf_ref[pl.ds(8, CHUNK), :] = cur_ref[0]
        pre = _conv_taps(buf_ref, w_ref) + b_ref[...]
        s = _sigmoid(pre)
        dsilu = s * (1.0 + pre * (1.0 - s))
        dpre_ref[0, :, pl.ds(0, D_SSM)] = dxs_ref[0] * dsilu[:, :D_SSM]
        dpre_ref[0, :, pl.ds(D_SSM, D_POOL)] = db_ref[0] * dsilu[:, D_SSM:D_SSM + D_POOL]
        dpre_ref[0, :, pl.ds(D_SSM + D_POOL, D_POOL)] = dc_ref[0] * dsilu[:, D_SSM + D_POOL:]
        dpre = dpre_ref[0]
        for k in range(CONV_W):
            shifted = buf_ref[pl.ds(8 - (CONV_W - 1) + k, CHUNK), :]
            dwb_ref[0, k:k + 1, :] += jnp.sum(dpre * shifted, axis=0, keepdims=True)
        dwb_ref[0, CONV_W:CONV_W + 1, :] += jnp.sum(dpre, axis=0, keepdims=True)

    row = lambda width: pl.BlockSpec((1, CHUNK, width), lambda b, j: (b, j, 0))
    return pl.pallas_call(
        body, name=name, grid=(bsz, nc),
        in_specs=[pl.BlockSpec((1, 8, c), lambda b, j: (b, jnp.maximum(j * (CHUNK // 8) - 1, 0), 0)),
                  row(c), row(D_SSM), row(D_POOL), row(D_POOL),
                  pl.BlockSpec((CONV_W, c), lambda b, j: (0, 0)), pl.BlockSpec((1, c), lambda b, j: (0, 0))],
        out_specs=[row(c), pl.BlockSpec((1, 8, c), lambda b, j: (b, 0, 0))],
        out_shape=[jax.ShapeDtypeStruct(xbc.shape, F32), jax.ShapeDtypeStruct((bsz, 8, c), F32)],
        scratch_shapes=[pltpu.VMEM((CHUNK + 8, c), F32)],
        compiler_params=_params("parallel", "arbitrary"),
    )(xbc, xbc, dxs, db, dc, conv_w, conv_b)


def _conv_bwd_in(dpre, conv_w, *, name):
    bsz, t, c = dpre.shape
    nc = t // CHUNK

    def body(cur_ref, head_ref, w_ref, o_ref, buf_ref):
        j = pl.program_id(1)
        buf_ref[pl.ds(0, CHUNK), :] = cur_ref[0]
        buf_ref[pl.ds(CHUNK, 8), :] = jnp.where(j < nc - 1, head_ref[0], 0.0)
        acc = None
        for k in range(CONV_W):
            term = w_ref[k:k + 1, :] * buf_ref[pl.ds(CONV_W - 1 - k, CHUNK), :]
            acc = term if acc is None else acc + term
        o_ref[0] = acc

    return pl.pallas_call(
        body, name=name, grid=(bsz, nc),
        in_specs=[pl.BlockSpec((1, CHUNK, c), lambda b, j: (b, j, 0)),
                  pl.BlockSpec((1, 8, c), lambda b, j: (b, jnp.minimum((j + 1) * (CHUNK // 8), t // 8 - 1), 0)),
                  pl.BlockSpec((CONV_W, c), lambda b, j: (0, 0))],
        out_specs=pl.BlockSpec((1, CHUNK, c), lambda b, j: (b, j, 0)), out_shape=jax.ShapeDtypeStruct(dpre.shape, F32),
        scratch_shapes=[pltpu.VMEM((CHUNK + 8, c), F32)],
        compiler_params=_params("parallel", "parallel"),
    )(dpre, dpre, conv_w)


def _ssd_common(j, dtr_ref, dtb_ref, alog_ref):
    lane = lax.broadcasted_iota(jnp.int32, (CHUNK, LANES), 1)
    row = lax.broadcasted_iota(jnp.int32, (CHUNK, LANES), 0)
    raw = dtr_ref[0] + dtb_ref[0]
    valid = (lane < HPG) & ((j > 0) | (row >= PAD))
    dt = jnp.where(valid, _softplus(raw), 0.0)
    a = -jnp.exp(alog_ref[0])
    tril = (row >= lane).astype(F32)
    acs = _dot_exact(tril, dt * a)
    return dict(lane=lane, row=row, raw=raw, valid=valid, dt=dt, a=a, causal=row >= lane,
                acs=acs, acs_t=acs.T, dt_t=dt.T, aend=acs[CHUNK - 1:CHUNK, :])


def _ssd_specs(nc, rev):
    ch = (lambda j: nc - 1 - j) if rev else (lambda j: j)
    return dict(
        xs=pl.BlockSpec((1, CHUNK, GW), lambda b, g, j: (b, ch(j), g)),
        bm=pl.BlockSpec((1, CHUNK, D_STATE), lambda b, g, j: (b, ch(j), D_SSM // D_STATE + g)),
        cm=pl.BlockSpec((1, CHUNK, D_STATE), lambda b, g, j: (b, ch(j), D_SSM // D_STATE + N_GROUPS + g)),
        lane_blk=pl.BlockSpec((1, CHUNK, LANES), lambda b, g, j: (b, ch(j), g)),
        grp_const=pl.BlockSpec((1, 1, LANES), lambda b, g, j: (g, 0, 0)),
        grp_vec=pl.BlockSpec((1, GW), lambda b, g, j: (0, g)),
        state=pl.BlockSpec((1, 1, D_STATE, GW), lambda b, g, j: (b, ch(j), 0, g)),
    )


def _ssd_fwd(xc, dtr, z, dtb, alog, dskip, normw, *, name):
    bsz, t, _ = xc.shape
    nc = t // CHUNK
    sp = _ssd_specs(nc, False)

    def body(xs_ref, b_ref, c_ref, dtr_ref, z_ref, dtb_ref, alog_ref, dsk_ref, nw_ref, yn_ref, y_ref, sp_ref, s_ref):
        j = pl.program_id(2)

        @pl.when(j == 0)
        def _():
            s_ref[...] = jnp.zeros_like(s_ref)

        q = _ssd_common(j, dtr_ref, dtb_ref, alog_ref)
        sp_ref[0, 0] = s_ref[...]
        bm, cm = b_ref[0], c_ref[0]
        cb = _dot_nt(cm, bm)
        low = q["lane"] < HEAD_DIM
        y_even = s_even = None
        for r in range(HPG):
            pair = pl.ds((r // 2) * LANES, LANES)
            col, aend = q["acs"][:, r:r + 1], q["aend"][:, r:r + 1]
            decay = jnp.exp(jnp.where(q["causal"], col - q["acs_t"][r:r + 1, :], -jnp.inf))
            xp, s_old = xs_ref[0, :, pair], s_ref[:, pair]
            y_r = _dot(cb * decay * q["dt_t"][r:r + 1, :], xp) + _dot(cm * jnp.exp(col), s_old)
            k = jnp.exp(aend - col) * q["dt"][:, r:r + 1]
            s_r = jnp.exp(aend) * s_old + _dot_tn(bm * k, xp)
            if r % 2 == 0:
                y_even, s_even = y_r, s_r
            else:
                y_ref[0, :, pair] = jnp.where(low, y_even, y_r)
                s_ref[:, pair] = jnp.where(low, s_even, s_r)
        y = y_ref[0] + dsk_ref[...] * xs_ref[0]
        y_ref[0] = y
        zz = z_ref[0]
        yg = y * (zz * _sigmoid(zz))
        rstd = lax.rsqrt(jnp.mean(yg * yg, axis=-1, keepdims=True) + EPS)
        yn_ref[0] = (yg * rstd * nw_ref[...]).astype(BF16)

    return pl.pallas_call(
        body, name=name, grid=(bsz, N_GROUPS, nc),
        in_specs=[sp["xs"], sp["bm"], sp["cm"], sp["lane_blk"], sp["xs"], sp["grp_const"], sp["grp_const"],
                  sp["grp_vec"], sp["grp_vec"]],
        out_specs=[sp["xs"], sp["xs"], sp["state"]],
        out_shape=[jax.ShapeDtypeStruct((bsz, t, D_SSM), BF16), jax.ShapeDtypeStruct((bsz, t, D_SSM), F32),
                   jax.ShapeDtypeStruct((bsz, nc, D_STATE, D_SSM), F32)],
        scratch_shapes=[pltpu.VMEM((D_STATE, GW), F32)],
        compiler_params=_params("parallel", "parallel", "arbitrary"),
    )(xc, xc, xc, dtr, z, dtb, alog, dskip, normw)


def _ssd_bwd(xc, dtr, z, ypre, sprev, dyn, dtb, alog, dskip, normw, *, name):
    bsz, t, _ = xc.shape
    nc = t // CHUNK
    sp = _ssd_specs(nc, True)

    def body(xs_ref, b_ref, c_ref, dtr_ref, z_ref, y_ref, sp_ref, dyn_ref, dtb_ref, alog_ref, dsk_ref, nw_ref,
             dz_ref, dxs_ref, db_ref, dc_ref, ddt_ref, dnw_ref, dsm_ref, ds_ref):
        j = pl.program_id(2)

        @pl.when(j == 0)
        def _():
            ds_ref[...] = jnp.zeros_like(ds_ref)
            dnw_ref[...] = jnp.zeros_like(dnw_ref)
            dsm_ref[...] = jnp.zeros_like(dsm_ref)

        q = _ssd_common(nc - 1 - j, dtr_ref, dtb_ref, alog_ref)
        lane, row = q["lane"], q["row"]
        lane1 = lane[0:1, :]
        y, zz, nw = y_ref[0], z_ref[0], nw_ref[...]
        sz = _sigmoid(zz)
        sil = zz * sz
        yg = y * sil
        rstd = lax.rsqrt(jnp.mean(yg * yg, axis=-1, keepdims=True) + EPS)
        dyn = dyn_ref[0]
        gn = dyn * nw
        dyg = rstd * (gn - yg * (rstd * rstd) * jnp.mean(gn * yg, axis=-1, keepdims=True))
        dnw_ref[0] += jnp.sum(dyn * yg * rstd, axis=0, keepdims=True)
        dy = dyg * sil
        dz_ref[0] = dyg * y * (sz * (1.0 + zz * (1.0 - sz)))
        xs = xs_ref[0]
        dskip_cols = jnp.sum(dy * xs, axis=0, keepdims=True)

        bm, cm = b_ref[0], c_ref[0]
        cb = _dot_nt(cm, bm)
        zero = jnp.zeros((CHUNK, LANES), F32)
        dcb, dc_acc, db_acc = zero, zero, zero
        dacs, dacs_t, ddt, ddt_t = zero, zero, zero, zero
        dskip_row = jnp.zeros((1, LANES), F32)
        dx_pair = ds_pair = None
        for r in range(HPG):
            jj = r // 2
            pair = pl.ds(jj * LANES, LANES)
            half = (lane < HEAD_DIM) if r % 2 == 0 else (lane >= HEAD_DIM)
            xp = xs[:, jj * LANES:(jj + 1) * LANES]
            dym = jnp.where(half, dy[:, jj * LANES:(jj + 1) * LANES], 0.0)
            xm = jnp.where(half, xp, 0.0)
            s_old = sp_ref[0, 0, :, pair]
            dsm = jnp.where(half, ds_ref[:, pair], 0.0)
            col, aend, dt_col = q["acs"][:, r:r + 1], q["aend"][:, r:r + 1], q["dt"][:, r:r + 1]
            dt_row = q["dt_t"][r:r + 1, :]
            decay = jnp.exp(jnp.where(q["causal"], col - q["acs_t"][r:r + 1, :], -jnp.inf))
            gmat = _dot_nt(dym, xp)
            dx_r = _dot_tn(cb * decay * dt_row, dym)
            dcb = dcb + gmat * decay * dt_row
            w0 = gmat * cb * decay
            cs0 = jnp.sum(w0, axis=0, keepdims=True)
            rs = jnp.sum(w0 * dt_row, axis=1, keepdims=True)
            ea = jnp.exp(col)
            ds_r = _dot_tn(cm * ea, dym)
            t1 = _dot_nt(dym, s_old)
            dc_acc = dc_acc + ea * t1
            qv = jnp.sum(cm * t1, axis=1, keepdims=True)
            dte = jnp.exp(aend - col)
            k = dte * dt_col
            ed = jnp.exp(aend)
            ds_r = ds_r + ed * dsm
            d_aend = _sum_all(dsm * s_old) * ed
            dx_r = dx_r + _dot(bm * k, dsm)
            dbs = _dot_nt(xm, dsm)
            db_acc = db_acc + k * dbs
            dk = jnp.sum(bm * dbs, axis=1, keepdims=True)
            ddte = dk * dt_col
            d_aend = d_aend + _sum_all(ddte * dte)
            dacs_col = rs + qv * ea - ddte * dte + jnp.where(row[:, 0:1] == CHUNK - 1, d_aend, 0.0)
            dacs = jnp.where(lane == r, dacs_col, dacs)
            ddt = jnp.where(lane == r, dk * dte, ddt)
            dacs_t = jnp.where(row == r, -cs0 * dt_row, dacs_t)
            ddt_t = jnp.where(row == r, cs0, ddt_t)
            dsk = _sum_all(jnp.where(half[0:1, :], dskip_cols[:, jj * LANES:(jj + 1) * LANES], 0.0))
            dskip_row = dskip_row + jnp.where(lane1 == r, dsk, 0.0)
            if r % 2 == 0:
                dx_pair, ds_pair = dx_r, ds_r
            else:
                dxs_ref[0, :, pair] = dx_pair + dx_r + dy[:, jj * LANES:(jj + 1) * LANES] * dsk_ref[:, pair]
                ds_ref[:, pair] = ds_pair + ds_r
        dacs = dacs + dacs_t.T
        ddt = ddt + ddt_t.T
        triu = (lane >= row).astype(F32)
        dda = _dot_exact(triu, dacs)
        ddt = ddt + dda * q["a"]
        da = jnp.sum(dda * q["dt"], axis=0, keepdims=True)
        draw = jnp.where(q["valid"], ddt * _sigmoid(q["raw"]), 0.0)
        ddt_ref[0] = draw
        dsm_ref[0, 0, 0:1, :] += dskip_row
        dsm_ref[0, 0, 1:2, :] += da * q["a"]
        dsm_ref[0, 0, 2:3, :] += jnp.sum(draw, axis=0, keepdims=True)
        dc_ref[0] = dc_acc + _dot(dcb, bm)
        db_ref[0] = db_acc + _dot_tn(dcb, cm)

    grp_out = pl.BlockSpec((1, CHUNK, D_STATE), lambda b, g, j: (b, nc - 1 - j, g))
    return pl.pallas_call(
        body, name=name, grid=(bsz, N_GROUPS, nc),
        in_specs=[sp["xs"], sp["bm"], sp["cm"], sp["lane_blk"], sp["xs"], sp["xs"], sp["state"], sp["xs"],
                  sp["grp_const"], sp["grp_const"], sp["grp_vec"], sp["grp_vec"]],
        out_specs=[sp["xs"], sp["xs"], grp_out, grp_out, sp["lane_blk"],
                   pl.BlockSpec((1, 1, GW), lambda b, g, j: (b, 0, g)),
                   pl.BlockSpec((1, 1, 8, LANES), lambda b, g, j: (b, g, 0, 0))],
        out_shape=[jax.ShapeDtypeStruct((bsz, t, D_SSM), F32), jax.ShapeDtypeStruct((bsz, t, D_SSM), F32),
                   jax.ShapeDtypeStruct((bsz, t, N_GROUPS * D_STATE), F32),
                   jax.ShapeDtypeStruct((bsz, t, N_GROUPS * D_STATE), F32),
                   jax.ShapeDtypeStruct((bsz, t, D_DT), F32), jax.ShapeDtypeStruct((bsz, 1, D_SSM), F32),
                   jax.ShapeDtypeStruct((bsz, N_GROUPS, 8, LANES), F32)],
        scratch_shapes=[pltpu.VMEM((D_STATE, GW), F32)],
        compiler_params=_params("parallel", "parallel", "arbitrary"),
    )(xc, xc, xc, dtr, z, ypre, sprev, dyn, dtb, alog, dskip, normw)


def _input_grad(dhn, h0, w, dres, seq, *, name):
    bsz, t, d = h0.shape
    nc = t // CHUNK

    def body(dy_ref, h_ref, w_ref, dres_ref, gx_ref, head_ref, dw_ref):
        j = pl.program_id(1)

        @pl.when((pl.program_id(0) == 0) & (j == 0))
        def _():
            dw_ref[...] = jnp.zeros_like(dw_ref)

        x, dyv = h_ref[0], dy_ref[0]
        r = lax.rsqrt(jnp.mean(x * x, axis=-1, keepdims=True) + EPS)
        g = dyv * w_ref[...]
        dx = r * (g - x * (r * r) * jnp.mean(g * x, axis=-1, keepdims=True)) + dres_ref[0]
        dw_ref[...] += jnp.sum(dyv * x * r, axis=0, keepdims=True)

        @pl.when(j == 0)
        def _():
            head_ref[0] = dx

        gx_ref[0] = dx

    row = pl.BlockSpec((1, CHUNK, d), lambda b, j: (b, j, 0))
    return pl.pallas_call(
        body, name=name, grid=(bsz, nc),
        in_specs=[row, row, pl.BlockSpec((1, d), lambda b, j: (0, 0)), row],
        out_specs=[pl.BlockSpec((1, CHUNK, d), lambda b, j: (b, jnp.maximum(j - 1, 0), 0)),
                   pl.BlockSpec((1, CHUNK, d), lambda b, j: (b, 0, 0)), pl.BlockSpec((1, d), lambda b, j: (0, 0))],
        out_shape=[jax.ShapeDtypeStruct((bsz, seq, d), F32), jax.ShapeDtypeStruct((bsz, CHUNK, d), F32),
                   jax.ShapeDtypeStruct((1, d), F32)],
        compiler_params=_params("arbitrary", "arbitrary"),
    )(dhn, h0, w, dres)


def _chip_peers():
    x, y, c = lax.axis_index("x"), lax.axis_index("y"), lax.axis_index("c")
    return x, y, c, [(1 - x, y), (x, 1 - y), (1 - x, 1 - y)]


def _gather_chips(arrs, *, name):
    n = len(arrs)

    def body(*refs):
        ins, outs = refs[:n], refs[n:2 * n]
        send_sems, recv_sems, loc_sems = refs[2 * n:]
        x, y, c, chips = _chip_peers()
        me = 2 * x + y
        local, sends = [], []
        for i in range(n):
            cp = pltpu.make_async_copy(ins[i], outs[i].at[me], loc_sems.at[i])
            cp.start()
            local.append(cp)
            for k, (px, py) in enumerate(chips):
                cp = pltpu.make_async_remote_copy(
                    src_ref=ins[i], dst_ref=outs[i].at[me], send_sem=send_sems.at[3 * i + k],
                    recv_sem=recv_sems.at[3 * i + k], device_id=(px, py, c), device_id_type=MESH)
                cp.start()
                sends.append(cp)
        for i in range(n):
            for k, (px, py) in enumerate(chips):
                pltpu.make_async_remote_copy(
                    src_ref=ins[i], dst_ref=outs[i].at[2 * px + py], send_sem=send_sems.at[3 * i + k],
                    recv_sem=recv_sems.at[3 * i + k], device_id=(px, py, c), device_id_type=MESH).wait_recv()
        for cp in sends:
            cp.wait_send()
        for cp in local:
            cp.wait()

    return pl.pallas_call(
        body, name=name, in_specs=[ANY] * n, out_specs=[ANY] * n,
        out_shape=[jax.ShapeDtypeStruct((4,) + a.shape, a.dtype) for a in arrs],
        scratch_shapes=[pltpu.SemaphoreType.DMA((3 * n,)), pltpu.SemaphoreType.DMA((3 * n,)),
                        pltpu.SemaphoreType.DMA((n,))],
    )(*arrs)


def _exchange_grads(big, small, *, name):
    n = len(big)
    flips = [(fx, fy, fc) for fx in (0, 1) for fy in (0, 1) for fc in (0, 1)][1:]

    def body(*refs):
        ins, small_ref = refs[:n], refs[n]
        outs, small_out = refs[n + 1:2 * n + 1], refs[2 * n + 1]
        send_sems, recv_sems, s_send, s_recv, loc_sem = refs[2 * n + 2:]
        x, y, c, chips = _chip_peers()
        me, me8 = 2 * x + y, 4 * x + 2 * y + c
        own = pltpu.make_async_copy(small_ref, small_out.at[me8], loc_sem)
        own.start()
        sends = []
        for k, (fx, fy, fc) in enumerate(flips):
            cp = pltpu.make_async_remote_copy(
                src_ref=small_ref, dst_ref=small_out.at[me8], send_sem=s_send.at[k], recv_sem=s_recv.at[k],
                device_id=(x ^ fx, y ^ fy, c ^ fc), device_id_type=MESH)
            cp.start()
            sends.append(cp)
        for i in range(n):
            for k, (px, py) in enumerate(chips):
                cp = pltpu.make_async_remote_copy(
                    src_ref=ins[i].at[2 * px + py], dst_ref=outs[i].at[me], send_sem=send_sems.at[3 * i + k],
                    recv_sem=recv_sems.at[3 * i + k], device_id=(px, py, c), device_id_type=MESH)
                cp.start()
                sends.append(cp)
        for k, (fx, fy, fc) in enumerate(flips):
            peer8 = 4 * (x ^ fx) + 2 * (y ^ fy) + (c ^ fc)
            pltpu.make_async_remote_copy(
                src_ref=small_ref, dst_ref=small_out.at[peer8], send_sem=s_send.at[k], recv_sem=s_recv.at[k],
                device_id=(x ^ fx, y ^ fy, c ^ fc), device_id_type=MESH).wait_recv()
        for i in range(n):
            for k, (px, py) in enumerate(chips):
                pltpu.make_async_remote_copy(
                    src_ref=ins[i].at[me], dst_ref=outs[i].at[2 * px + py], send_sem=send_sems.at[3 * i + k],
                    recv_sem=recv_sems.at[3 * i + k], device_id=(px, py, c), device_id_type=MESH).wait_recv()
        for cp in sends:
            cp.wait_send()
        own.wait()

    return pl.pallas_call(
        body, name=name, in_specs=[ANY] * (n + 1), out_specs=[ANY] * (n + 1),
        out_shape=[jax.ShapeDtypeStruct(a.shape, a.dtype) for a in big]
        + [jax.ShapeDtypeStruct((8,) + small.shape, small.dtype)],
        scratch_shapes=[pltpu.SemaphoreType.DMA((3 * n,)), pltpu.SemaphoreType.DMA((3 * n,)),
                        pltpu.SemaphoreType.DMA((7,)), pltpu.SemaphoreType.DMA((7,)), pltpu.SemaphoreType.DMA(())],
    )(*big, small)


def _swap_cores(arrs, *, name):
    n = len(arrs)

    def body(*refs):
        ins, outs = refs[:n], refs[n:2 * n]
        send_sems, recv_sems = refs[2 * n:]
        x, y, c = lax.axis_index("x"), lax.axis_index("y"), lax.axis_index("c")
        copies = []
        for i in range(n):
            cp = pltpu.make_async_remote_copy(
                src_ref=ins[i], dst_ref=outs[i], send_sem=send_sems.at[i], recv_sem=recv_sems.at[i],
                device_id=(x, y, 1 - c), device_id_type=MESH)
            cp.start()
            copies.append(cp)
        for cp in copies:
            cp.wait_recv()
        for cp in copies:
            cp.wait_send()

    return pl.pallas_call(
        body, name=name, in_specs=[ANY] * n, out_specs=[ANY] * n,
        out_shape=[jax.ShapeDtypeStruct(a.shape, a.dtype) for a in arrs],
        scratch_shapes=[pltpu.SemaphoreType.DMA((n,)), pltpu.SemaphoreType.DMA((n,))],
    )(*arrs)


def _chip_sum(own, landed, *, name):
    r, c = own.shape
    tm = _pick(r, (256, 128, 64, 8))

    def body(own_ref, land_ref, o_ref):
        me = 2 * lax.axis_index("x") + lax.axis_index("y")
        acc = None
        for jchip in range(4):
            term = jnp.where(me == jchip, own_ref[...], land_ref[jchip].astype(F32))
            acc = term if acc is None else acc + term
        o_ref[...] = acc

    return pl.pallas_call(
        body, name=name, grid=(r // tm,),
        in_specs=[pl.BlockSpec((tm, c), lambda i: (i, 0)), pl.BlockSpec((4, tm, c), lambda i: (0, i, 0))],
        out_specs=pl.BlockSpec((tm, c), lambda i: (i, 0)), out_shape=jax.ShapeDtypeStruct((r, c), F32),
        compiler_params=_params("parallel"),
    )(own, landed)


def _device_sum(parts, *, name):
    _, r, c = parts.shape

    def body(p_ref, o_ref):
        acc = p_ref[0]
        for d in range(1, 8):
            acc = acc + p_ref[d]
        o_ref[...] = acc

    return pl.pallas_call(body, name=name, out_shape=jax.ShapeDtypeStruct((r, c), F32))(parts)


def _adamw_math(w, g, m, v):
    m = ADAM_B1 * m + (1.0 - ADAM_B1) * g
    v = ADAM_B2 * v + (1.0 - ADAM_B2) * (g * g)
    m_hat = m / (1.0 - ADAM_B1 ** ADAM_STEP)
    v_hat = v / (1.0 - ADAM_B2 ** ADAM_STEP)
    return -ADAM_LR * (m_hat / (jnp.sqrt(v_hat) + ADAM_EPS) + ADAM_WD * w), m, v


def _adamw(w, g_parts, m, v, *, name):
    r, c = w.shape
    tm = _pick(r, (256, 128, 64, 16, 8, 4, 1)) if r * c > 65536 else r
    n_g = len(g_parts)

    def body(*refs):
        w_ref, m_ref, v_ref = refs[n_g:n_g + 3]
        g_ref, d_ref, nm_ref, nv_ref = refs[n_g + 3:]
        g = refs[0][...]
        for p in refs[1:n_g]:
            g = g + p[...]
        g_ref[...] = g
        d_ref[...], nm_ref[...], nv_ref[...] = _adamw_math(w_ref[...], g, m_ref[...], v_ref[...])

    blk = pl.BlockSpec((tm, c), lambda i: (i, 0))
    return pl.pallas_call(
        body, name=name, grid=(r // tm,), in_specs=[blk] * (n_g + 3), out_specs=[blk] * 4,
        out_shape=[jax.ShapeDtypeStruct((r, c), F32)] * 4, compiler_params=_params("parallel"),
    )(*g_parts, w, m, v)


def _pad_heads(v):
    return jnp.pad(v.reshape(N_GROUPS, 1, HPG), ((0, 0), (0, 0), (0, LANES - HPG)))


def _unpad_heads(v):
    return v[:, :HPG].reshape(1, N_HEADS)


_SMALL = [("norm_mix_w", (1, 1024)), ("pool_w", (512, 128)), ("pool_scale", (1, 512)), ("conv_w", (4, D_XBC)),
          ("conv_b", (1, D_XBC)), ("dt_bias", (1, N_HEADS)), ("a_log", (1, N_HEADS)), ("d_skip", (1, N_HEADS)),
          ("ssm_norm_w", (1, D_SSM)), ("norm_ffn_w", (1, 1024)), ("norm_f_w", (1, 1024)), ("meta", (N_META, 1024))]


def _pack_small(grads):
    rows = []
    for nm, shape in _SMALL:
        flat = grads[nm].reshape(-1)
        rows.append(jnp.pad(flat, (0, (-flat.size) % LANES)).reshape(-1, LANES))
    packed = jnp.concatenate(rows, axis=0)
    return jnp.pad(packed, ((0, (-packed.shape[0]) % 8), (0, 0)))


def _unpack_small(packed):
    out, r0 = {}, 0
    for nm, shape in _SMALL:
        size = shape[0] * shape[1]
        nrow = -(-size // LANES)
        out[nm] = packed[r0:r0 + nrow].reshape(-1)[:size].reshape(shape)
        r0 += nrow
    return out


def kernel(x, meta, norm_mix_w, w_in, pool_w, pool_scale, conv_w, conv_b, dt_bias, a_log, d_skip, ssm_norm_w, w_out, norm_ffn_w, w_ff1, w_ff2, norm_f_w, loss_target, m_meta, m_norm_mix_w, m_w_in, m_pool_w, m_pool_scale, m_conv_w, m_conv_b, m_dt_bias, m_a_log, m_d_skip, m_ssm_norm_w, m_w_out, m_norm_ffn_w, m_w_ff1, m_w_ff2, m_norm_f_w, v_meta, v_norm_mix_w, v_w_in, v_pool_w, v_pool_scale, v_conv_w, v_conv_b, v_dt_bias, v_a_log, v_d_skip, v_ssm_norm_w, v_w_out, v_norm_ffn_w, v_w_ff1, v_w_ff2, v_norm_f_w):
    bsz, seq, d = x.shape
    t = seq + CHUNK
    n = bsz * t
    chip = 2 * lax.axis_index("x") + lax.axis_index("y")
    d_in = w_in.shape[2] * 4

    g_in, g_out, g_ff1, g_ff2, g_conv, g_meta = _gather_chips(
        [w_in[0].astype(BF16), w_out[0].astype(BF16), w_ff1[0].astype(BF16), w_ff2[0].astype(BF16), conv_w[0], meta],
        name="gather_weights")
    win = g_in.transpose(1, 0, 2).reshape(d, d_in)
    wu, wz = win[:, :D_POOL], win[:, D_POOL:D_POOL + D_SSM]
    wx = win[:, D_POOL + D_SSM:D_POOL + D_SSM + D_XBC]
    wdt = jnp.pad(win[:, D_POOL + D_SSM + D_XBC:].reshape(d, N_GROUPS, HPG),
                  ((0, 0), (0, 0), (0, LANES - HPG))).reshape(d, D_DT)
    wo = g_out.reshape(D_POOL + D_SSM, d)
    wo_p, wo_s = wo[:D_POOL], wo[D_POOL:]
    w1 = g_ff1.transpose(1, 0, 2).reshape(d, D_FF)
    w2 = g_ff2.reshape(D_FF, d)
    convw = g_conv.transpose(1, 0, 2).reshape(CONV_W, D_XBC)
    meta_full = g_meta.transpose(1, 0, 2).reshape(N_META, d)
    dtb, alog = _pad_heads(dt_bias), _pad_heads(a_log)
    dskip = jnp.repeat(d_skip, HEAD_DIM, axis=1)
    poolw = pool_w[0]

    h0 = jnp.concatenate([jnp.zeros((bsz, PAD, d), F32), jnp.broadcast_to(meta_full[None], (bsz, N_META, d)), x], axis=1)
    h0f = h0.reshape(n, d)
    tm = _pick(n, (768, 512, 256, 128))
    tm_wide = _pick(n, (384, 256, 128))
    hn1 = _rms_fwd(h0f, norm_mix_w, name="norm_mix")
    u = _mm(hn1, wu, name="proj_u", tm=tm)
    z = _mm(hn1, wz, name="proj_z", tm=tm)
    xbc = _mm(hn1, wx, name="proj_xbc", tm=tm)
    dtr = _mm(hn1, wdt, name="proj_dt", tm=tm)
    ypool = _pool_fwd(u.reshape(bsz, t, D_POOL), poolw, pool_scale, name="pool_fwd")
    xbc3 = xbc.reshape(bsz, t, D_XBC)
    xc = _conv_fwd(xbc3, convw, conv_b, name="conv_fwd")
    z3, dtr3 = z.reshape(bsz, t, D_SSM), dtr.reshape(bsz, t, D_DT)
    yn, ypre, sprev = _ssd_fwd(xc, dtr3, z3, dtb, alog, dskip, ssm_norm_w, name="ssd_fwd")
    ypool_f, yn_f = ypool.reshape(n, D_POOL), yn.reshape(n, D_SSM)
    add = lambda r, e: r + e
    h1 = _mm(ypool_f, wo_p, name="out_pool", tm=tm, post=add, extras=(h0f,))
    h1 = _mm(yn_f, wo_s, name="out_ssm", tm=tm, post=add, extras=(h1,))
    hn2 = _rms_fwd(h1, norm_ffn_w, name="norm_ffn")
    act = _mm(hn2, w1, name="ff1", tm=tm)
    relu2 = lambda a: jnp.square(jnp.maximum(a, 0.0))
    h2 = _mm(act, w2, name="ff2", tm=tm_wide, pre=relu2, post=add, extras=(h1,))
    dh2, loss_acc, d_norm_f = _final_norm_loss(h2.reshape(bsz, t, d), loss_target, norm_f_w.reshape(1, d), name="loss")
    loss = lax.psum(loss_acc[0, 0], ("x", "y", "c"))

    dh2f = dh2.reshape(n, d)
    dact = _mm(dh2f, w2, name="ff2_bwd", tm=tm, nt=True, post=lambda r, a: r * (2.0 * jnp.maximum(a, 0.0)),
               extras=(act,), out_dtype=BF16)
    d_w2 = _mm_tn(act, dh2f, name="ff2_dw", tk=1024, tn=1024, tm=tm_wide, pre=relu2)
    d_w1 = _mm_tn(hn2, dact, name="ff1_dw", tk=1024, tn=1024, tm=tm)
    dhn2 = _mm(dact, w1, name="ff1_bwd", tm=tm_wide, nt=True)
    dh1, d_norm_ffn = _rms_bwd(dhn2, h1, norm_ffn_w, dh2f, name="norm_ffn_bwd")
    dypool = _mm(dh1, wo_p, name="out_pool_bwd", tm=tm, nt=True)
    dyn = _mm(dh1, wo_s, name="out_ssm_bwd", tm=tm, nt=True)
    d_wo_p = _mm_tn(ypool_f, dh1, name="out_pool_dw", tk=512, tn=1024, tm=tm)
    d_wo_s = _mm_tn(yn_f, dh1, name="out_ssm_dw", tk=512, tn=1024, tm=tm)
    dz, dxs, dbm, dcm, ddtr, d_nw, d_heads = _ssd_bwd(
        xc, dtr3, z3, ypre, sprev, dyn.reshape(bsz, t, D_SSM), dtb, alog, dskip, ssm_norm_w, name="ssd_bwd")
    dpre, d_convwb = _conv_bwd_pre(xbc3, dxs, dbm, dcm, convw, conv_b, name="conv_bwd_pre")
    dxbc = _conv_bwd_in(dpre, convw, name="conv_bwd_in")
    du, d_poolw, d_poolsc = _pool_bwd(u.reshape(bsz, t, D_POOL), dypool.reshape(bsz, t, D_POOL), poolw, pool_scale,
                                      name="pool_bwd")
    duf, dzf, dxbcf, ddtrf = du.reshape(n, D_POOL), dz.reshape(n, D_SSM), dxbc.reshape(n, D_XBC), ddtr.reshape(n, D_DT)
    dhn1 = _mm(duf, wu, name="proj_u_bwd", tm=tm, nt=True)
    dhn1 = _mm(dzf, wz, name="proj_z_bwd", tm=tm, nt=True, post=add, extras=(dhn1,))
    dhn1 = _mm(dxbcf, wx, name="proj_xbc_bwd", tm=tm_wide, nt=True, post=add, extras=(dhn1,))
    dhn1 = _mm(ddtrf, wdt, name="proj_dt_bwd", tm=tm, nt=True, post=add, extras=(dhn1,))
    d_wu = _mm_tn(hn1, duf, name="proj_u_dw", tk=1024, tn=512, tm=tm)
    d_wz = _mm_tn(hn1, dzf, name="proj_z_dw", tk=1024, tn=512, tm=tm)
    d_wx = _mm_tn(hn1, dxbcf, name="proj_xbc_dw", tk=1024, tn=512, tm=tm)
    d_wdt = _mm_tn(hn1, ddtrf, name="proj_dt_dw", tk=1024, tn=512, tm=tm)
    grad_x, d_head_rows, d_norm_mix = _input_grad(
        dhn1.reshape(bsz, t, d), h0, norm_mix_w, dh1.reshape(bsz, t, d), seq, name="input_grad")

    d_win = jnp.concatenate([d_wu, d_wz, d_wx, d_wdt.reshape(d, N_GROUPS, LANES)[:, :, :HPG].reshape(d, N_HEADS)], axis=1)
    big = [d_win.reshape(d, 4, d_in // 4).transpose(1, 0, 2),
           jnp.concatenate([d_wo_p, d_wo_s], axis=0).reshape(4, (D_POOL + D_SSM) // 4, d),
           d_w1.reshape(d, 4, D_FF // 4).transpose(1, 0, 2), d_w2.reshape(4, D_FF // 4, d)]
    heads = jnp.sum(d_heads, axis=0)
    small = _pack_small({
        "norm_mix_w": d_norm_mix, "pool_w": jnp.sum(d_poolw, axis=0), "pool_scale": jnp.sum(d_poolsc, axis=0),
        "conv_w": jnp.sum(d_convwb[:, :CONV_W], axis=0), "conv_b": jnp.sum(d_convwb[:, CONV_W:CONV_W + 1], axis=0),
        "dt_bias": _unpad_heads(heads[:, 2]), "a_log": _unpad_heads(heads[:, 1]), "d_skip": _unpad_heads(heads[:, 0]),
        "ssm_norm_w": jnp.sum(d_nw, axis=0), "norm_ffn_w": d_norm_ffn, "norm_f_w": d_norm_f,
        "meta": jnp.sum(d_head_rows[:, PAD:], axis=0)})
    *landed, small_all = _exchange_grads([b.astype(BF16) for b in big], small, name="exchange_grads")
    own = [lax.dynamic_index_in_dim(b, chip, 0, keepdims=False) for b in big]
    mine = [_chip_sum(o, l, name=f"chip_sum_{i}") for i, (o, l) in enumerate(zip(own, landed))]
    theirs = _swap_cores(mine, name="swap_cores")
    gsmall = _unpack_small(_device_sum(small_all, name="device_sum"))
    gsmall["conv_w"] = lax.dynamic_slice_in_dim(gsmall["conv_w"], chip * (D_XBC // 4), D_XBC // 4, axis=1)
    gsmall["meta"] = lax.dynamic_slice_in_dim(gsmall["meta"], chip * (d // 4), d // 4, axis=1)

    given = dict(meta=(meta, m_meta, v_meta), norm_mix_w=(norm_mix_w, m_norm_mix_w, v_norm_mix_w),
                 w_in=(w_in, m_w_in, v_w_in), pool_w=(pool_w, m_pool_w, v_pool_w),
                 pool_scale=(pool_scale, m_pool_scale, v_pool_scale), conv_w=(conv_w, m_conv_w, v_conv_w),
                 conv_b=(conv_b, m_conv_b, v_conv_b), dt_bias=(dt_bias, m_dt_bias, v_dt_bias),
                 a_log=(a_log, m_a_log, v_a_log), d_skip=(d_skip, m_d_skip, v_d_skip),
                 ssm_norm_w=(ssm_norm_w, m_ssm_norm_w, v_ssm_norm_w), w_out=(w_out, m_w_out, v_w_out),
                 norm_ffn_w=(norm_ffn_w, m_norm_ffn_w, v_norm_ffn_w), w_ff1=(w_ff1, m_w_ff1, v_w_ff1),
                 w_ff2=(w_ff2, m_w_ff2, v_w_ff2), norm_f_w=(norm_f_w, m_norm_f_w, v_norm_f_w))
    big_names = ["w_in", "w_out", "w_ff1", "w_ff2"]
    results = {}
    for nm, (w, m, v) in given.items():
        if nm in big_names:
            i = big_names.index(nm)
            parts, shape2 = (mine[i], theirs[i]), mine[i].shape
        else:
            parts, shape2 = (gsmall[nm],), gsmall[nm].shape
        outs = _adamw(w.reshape(shape2), parts, m.reshape(shape2), v.reshape(shape2), name=f"adamw_{nm}")
        results[nm] = [o.reshape(w.shape) for o in outs]
    order = list(given)
    return (loss, grad_x, *[results[nm][0] for nm in order], *[results[nm][1] for nm in order],
            *[results[nm][2] for nm in order], *[results[nm][3] for nm in order])
```

```python
import functools

import jax
import jax.numpy as jnp
from jax import lax
from jax.experimental import pallas as pl
from jax.experimental.pallas import tpu as pltpu

F32 = jnp.float32
BF16 = jnp.bfloat16
MESH = pl.DeviceIdType.MESH
ANY = pl.BlockSpec(memory_space=pl.ANY)

D_MODEL = 1024
N_META = 16
CHUNK = 128
PAD = CHUNK - N_META
POOL_WINDOWS = (2, 4, 8, 16)
D_POOL = 512
POOL_GROUP = 128
D_SSM = 1536
N_HEADS = 24
N_GROUPS = 4
HPG = 6
HEAD_DIM = 64
D_STATE = 128
GW = HPG * HEAD_DIM
D_XBC = D_SSM + 2 * N_GROUPS * D_STATE
D_DT = N_GROUPS * 128
D_FF = 4096
CONV_W = 4
EPS = 1e-5
LANES = 128
VMEM_LIMIT = 56 * 1024 * 1024

ADAM_LR, ADAM_B1, ADAM_B2, ADAM_EPS, ADAM_WD, ADAM_STEP = 0.001, 0.9, 0.999, 1e-08, 0.01, 10


def _params(*sem):
    return pltpu.CompilerParams(dimension_semantics=sem, vmem_limit_bytes=VMEM_LIMIT)


def _pick(n, cands):
    for c in cands:
        if n % c == 0:
            return c
    raise ValueError(f"no block size for {n}")


def _dot(a, b):
    return jnp.dot(a.astype(BF16), b.astype(BF16), preferred_element_type=F32)


def _dot_nt(a, b):
    return lax.dot_general(a.astype(BF16), b.astype(BF16), (((1,), (1,)), ((), ())), preferred_element_type=F32)


def _dot_tn(a, b):
    return lax.dot_general(a.astype(BF16), b.astype(BF16), (((0,), (0,)), ((), ())), preferred_element_type=F32)


def _dot_exact(mask, x):
    m = mask.astype(BF16)
    hi = x.astype(BF16)
    r1 = x - hi.astype(F32)
    mid = r1.astype(BF16)
    lo = (r1 - mid.astype(F32)).astype(BF16)
    dot = lambda t: jnp.dot(m, t, preferred_element_type=F32)
    return dot(hi) + dot(mid) + dot(lo)


def _sigmoid(x):
    return 1.0 / (1.0 + jnp.exp(-x))


def _softplus(x):
    return jnp.maximum(x, 0.0) + jnp.log1p(jnp.exp(-jnp.abs(x)))


def _sum_all(x):
    return jnp.sum(jnp.sum(x, axis=1, keepdims=True), axis=0, keepdims=True)


def _mm(a, w, *, name, tm, tn=512, nt=False, pre=None, post=None, extras=(), out_dtype=F32, rider=None):
    n, k = a.shape
    m = w.shape[0] if nt else w.shape[1]
    tn = min(tn, m)
    n_ex = len(extras)

    def body(a_ref, w_ref, *rest):
        o_ref = rest[n_ex]
        av = a_ref[...]
        if pre is not None:
            av = pre(av)
        r = _dot_nt(av, w_ref[...]) if nt else _dot(av, w_ref[...])
        if post is not None:
            r = post(r, *[e[...] for e in rest[:n_ex]])
        o_ref[...] = r.astype(out_dtype)

    w_spec = pl.BlockSpec((tn, k), lambda i, j: (j, 0)) if nt else pl.BlockSpec((k, tn), lambda i, j: (0, j))
    blk = pl.BlockSpec((tm, tn), lambda i, j: (i, j))
    grid = (n // tm, m // tn)
    ride = _Ride(rider, body, 2 + n_ex, 1, 0, grid)
    outs = pl.pallas_call(
        ride.body, name=name, grid=grid,
        in_specs=[pl.BlockSpec((tm, k), lambda i, j: (i, 0)), w_spec] + [blk] * n_ex + ride.in_specs,
        out_specs=[blk] + ride.out_specs, out_shape=[jax.ShapeDtypeStruct((n, m), out_dtype)] + ride.out_shape,
        scratch_shapes=ride.scratch, compiler_params=_params(*ride.semantics(("parallel", "parallel"))),
    )(a, w, *extras, *ride.args)
    return (outs[0], outs[1:]) if rider else outs[0]


def _mm_tn(a, g, *, name, tk, tn, tm, pre=None):
    n, k = a.shape
    m = g.shape[1]
    tk, tn = min(tk, k), min(tn, m)

    def body(a_ref, g_ref, o_ref):
        @pl.when(pl.program_id(2) == 0)
        def _():
            o_ref[...] = jnp.zeros_like(o_ref)

        av = a_ref[...]
        if pre is not None:
            av = pre(av)
        o_ref[...] += _dot_tn(av, g_ref[...])

    return pl.pallas_call(
        body, name=name, grid=(k // tk, m // tn, n // tm),
        in_specs=[pl.BlockSpec((tm, tk), lambda i, j, r: (r, i)), pl.BlockSpec((tm, tn), lambda i, j, r: (r, j))],
        out_specs=pl.BlockSpec((tk, tn), lambda i, j, r: (i, j)),
        out_shape=jax.ShapeDtypeStruct((k, m), F32),
        compiler_params=_params("parallel", "parallel", "arbitrary"),
    )(a, g)


def _rms_fwd(h, w, *, name):
    n, d = h.shape
    tm = _pick(n, (768, 512, 256, 128))

    def body(h_ref, w_ref, o_ref):
        x = h_ref[...]
        r = lax.rsqrt(jnp.mean(x * x, axis=-1, keepdims=True) + EPS)
        o_ref[...] = (x * r * w_ref[...]).astype(BF16)

    return pl.pallas_call(
        body, name=name, grid=(n // tm,),
        in_specs=[pl.BlockSpec((tm, d), lambda i: (i, 0)), pl.BlockSpec((1, d), lambda i: (0, 0))],
        out_specs=pl.BlockSpec((tm, d), lambda i: (i, 0)), out_shape=jax.ShapeDtypeStruct((n, d), BF16),
        compiler_params=_params("parallel"),
    )(h, w)


def _rms_bwd(dy, h, w, dres, *, name):
    n, d = h.shape
    tm = _pick(n, (768, 512, 256, 128))

    def body(dy_ref, h_ref, w_ref, dres_ref, dx_ref, dw_ref):
        @pl.when(pl.program_id(0) == 0)
        def _():
            dw_ref[...] = jnp.zeros_like(dw_ref)

        x, dyv = h_ref[...], dy_ref[...]
        r = lax.rsqrt(jnp.mean(x * x, axis=-1, keepdims=True) + EPS)
        g = dyv * w_ref[...]
        dx_ref[...] = r * (g - x * (r * r) * jnp.mean(g * x, axis=-1, keepdims=True)) + dres_ref[...]
        dw_ref[...] += jnp.sum(dyv * x * r, axis=0, keepdims=True)

    row = pl.BlockSpec((tm, d), lambda i: (i, 0))
    vec = pl.BlockSpec((1, d), lambda i: (0, 0))
    return pl.pallas_call(
        body, name=name, grid=(n // tm,), in_specs=[row, row, vec, row], out_specs=[row, vec],
        out_shape=[jax.ShapeDtypeStruct((n, d), F32), jax.ShapeDtypeStruct((1, d), F32)],
        compiler_params=_params("arbitrary"),
    )(dy, h, w, dres)


def _final_norm_loss(h2, target, w, *, name):
    bsz, t, d = h2.shape
    nc = t // CHUNK

    def body(h_ref, t_ref, w_ref, dh_ref, loss_ref, dw_ref):
        j = pl.program_id(1)

        @pl.when((pl.program_id(0) == 0) & (j == 0))
        def _():
            loss_ref[...] = jnp.zeros_like(loss_ref)
            dw_ref[...] = jnp.zeros_like(dw_ref)

        x, wv = h_ref[0], w_ref[...]
        r = lax.rsqrt(jnp.mean(x * x, axis=-1, keepdims=True) + EPS)
        diff = jnp.where(j > 0, x * r * wv - t_ref[0], 0.0)
        loss_ref[...] += _sum_all(diff * diff) * (0.5 / d)
        dy = diff * (1.0 / d)
        g = dy * wv
        dh_ref[0] = r * (g - x * (r * r) * jnp.mean(g * x, axis=-1, keepdims=True))
        dw_ref[...] += jnp.sum(dy * x * r, axis=0, keepdims=True)

    return pl.pallas_call(
        body, name=name, grid=(bsz, nc),
        in_specs=[pl.BlockSpec((1, CHUNK, d), lambda b, j: (b, j, 0)),
                  pl.BlockSpec((1, CHUNK, d), lambda b, j: (b, jnp.maximum(j - 1, 0), 0)),
                  pl.BlockSpec((1, d), lambda b, j: (0, 0))],
        out_specs=[pl.BlockSpec((1, CHUNK, d), lambda b, j: (b, j, 0)),
                   pl.BlockSpec((8, LANES), lambda b, j: (0, 0)),
                   pl.BlockSpec((1, d), lambda b, j: (0, 0))],
        out_shape=[jax.ShapeDtypeStruct((bsz, t, d), F32), jax.ShapeDtypeStruct((8, LANES), F32),
                   jax.ShapeDtypeStruct((1, d), F32)],
        compiler_params=_params("arbitrary", "arbitrary"),
    )(h2, target, w)


def _pool_masks(j, transposed):
    r = lax.broadcasted_iota(jnp.int32, (CHUNK, 2 * CHUNK), 0)
    c = lax.broadcasted_iota(jnp.int32, (CHUNK, 2 * CHUNK), 1)
    masks = []
    for w in POOL_WINDOWS:
        if transposed:
            m = (c >= r) & (c < r + w)
        else:
            s = c - CHUNK
            m = (s <= r) & (s > r - w) & (s + j * CHUNK >= 0)
        masks.append(m.astype(F32))
    return masks


def _pool_count(t_global, w):
    return jnp.clip(t_global - PAD + 1, 1, w).astype(F32)


def _pool_fwd(u, pool_w, pool_scale, *, name):
    bsz, t, _ = u.shape
    nc = t // CHUNK

    def body(prev_ref, cur_ref, pw_ref, sc_ref, o_ref):
        j = pl.program_id(1)
        masks = _pool_masks(j, False)
        tg = j * CHUNK + lax.broadcasted_iota(jnp.int32, (CHUNK, 1), 0)
        for gi, w in enumerate(POOL_WINDOWS):
            sl = pl.ds(gi * POOL_GROUP, POOL_GROUP)
            cur = cur_ref[0, :, sl]
            both = jnp.concatenate([prev_ref[0, :, sl], cur], axis=0)
            pooled = _dot_exact(masks[gi], both) / _pool_count(tg, w) - cur
            o_ref[0, :, sl] = (_dot(pooled, pw_ref[gi]) * sc_ref[:, sl]).astype(BF16)

    blk = lambda f: pl.BlockSpec((1, CHUNK, D_POOL), f)
    return pl.pallas_call(
        body, name=name, grid=(bsz, nc),
        in_specs=[blk(lambda b, j: (b, jnp.maximum(j - 1, 0), 0)), blk(lambda b, j: (b, j, 0)),
                  pl.BlockSpec((4, POOL_GROUP, POOL_GROUP), lambda b, j: (0, 0, 0)),
                  pl.BlockSpec((1, D_POOL), lambda b, j: (0, 0))],
        out_specs=blk(lambda b, j: (b, j, 0)), out_shape=jax.ShapeDtypeStruct(u.shape, BF16),
        compiler_params=_params("parallel", "parallel"),
    )(u, u, pool_w, pool_scale)


def _pool_bwd(u, dyp, pool_w, pool_scale, *, name):
    bsz, t, _ = u.shape
    nc = t // CHUNK

    def body(prev_ref, cur_ref, dy_ref, dyn_ref, pw_ref, sc_ref, du_ref, dpw_ref, dsc_ref):
        j = pl.program_id(1)

        @pl.when(j == 0)
        def _():
            dpw_ref[...] = jnp.zeros_like(dpw_ref)
            dsc_ref[...] = jnp.zeros_like(dsc_ref)

        fwd = _pool_masks(j, False)
        bwd = _pool_masks(j, True)
        tg = j * CHUNK + lax.broadcasted_iota(jnp.int32, (CHUNK, 1), 0)
        has_next = j < nc - 1
        for gi, w in enumerate(POOL_WINDOWS):
            sl = pl.ds(gi * POOL_GROUP, POOL_GROUP)
            cur, pw, sc = cur_ref[0, :, sl], pw_ref[gi], sc_ref[:, sl]
            both = jnp.concatenate([prev_ref[0, :, sl], cur], axis=0)
            pooled = _dot_exact(fwd[gi], both) / _pool_count(tg, w) - cur
            dy = dy_ref[0, :, sl]
            dsc_ref[0, :, sl] += jnp.sum(dy * _dot(pooled, pw), axis=0, keepdims=True)
            dm = dy * sc
            dpw_ref[0, gi] += _dot_tn(pooled, dm)
            dpooled = _dot_nt(dm, pw)
            dpooled_next = _dot_nt(jnp.where(has_next, dyn_ref[0, :, sl], 0.0) * sc, pw)
            e = jnp.concatenate([dpooled / _pool_count(tg, w), dpooled_next / _pool_count(tg + CHUNK, w)], axis=0)
            du_ref[0, :, sl] = _dot_exact(bwd[gi], e) - dpooled

    blk = lambda f: pl.BlockSpec((1, CHUNK, D_POOL), f)
    return pl.pallas_call(
        body, name=name, grid=(bsz, nc),
        in_specs=[blk(lambda b, j: (b, jnp.maximum(j - 1, 0), 0)), blk(lambda b, j: (b, j, 0)),
                  blk(lambda b, j: (b, j, 0)), blk(lambda b, j: (b, jnp.minimum(j + 1, nc - 1), 0)),
                  pl.BlockSpec((4, POOL_GROUP, POOL_GROUP), lambda b, j: (0, 0, 0)),
                  pl.BlockSpec((1, D_POOL), lambda b, j: (0, 0))],
        out_specs=[blk(lambda b, j: (b, j, 0)),
                   pl.BlockSpec((1, 4, POOL_GROUP, POOL_GROUP), lambda b, j: (b, 0, 0, 0)),
                   pl.BlockSpec((1, 1, D_POOL), lambda b, j: (b, 0, 0))],
        out_shape=[jax.ShapeDtypeStruct(u.shape, F32), jax.ShapeDtypeStruct((bsz, 4, POOL_GROUP, POOL_GROUP), F32),
                   jax.ShapeDtypeStruct((bsz, 1, D_POOL), F32)],
        compiler_params=_params("parallel", "arbitrary"),
    )(u, u, dyp, dyp, pool_w, pool_scale)


def _conv_taps(buf_ref, w_ref):
    acc = None
    for k in range(CONV_W):
        term = w_ref[k:k + 1, :] * buf_ref[pl.ds(8 - (CONV_W - 1) + k, CHUNK), :]
        acc = term if acc is None else acc + term
    return acc


def _conv_fwd(xbc, conv_w, conv_b, *, name):
    bsz, t, c = xbc.shape
    nc = t // CHUNK

    def body(tail_ref, cur_ref, w_ref, b_ref, o_ref, buf_ref):
        j = pl.program_id(1)
        buf_ref[pl.ds(0, 8), :] = jnp.where(j > 0, tail_ref[0], 0.0)
        buf_ref[pl.ds(8, CHUNK), :] = cur_ref[0]
        pre = _conv_taps(buf_ref, w_ref) + b_ref[...]
        o_ref[0] = pre * _sigmoid(pre)

    return pl.pallas_call(
        body, name=name, grid=(bsz, nc),
        in_specs=[pl.BlockSpec((1, 8, c), lambda b, j: (b, jnp.maximum(j * (CHUNK // 8) - 1, 0), 0)),
                  pl.BlockSpec((1, CHUNK, c), lambda b, j: (b, j, 0)),
                  pl.BlockSpec((CONV_W, c), lambda b, j: (0, 0)), pl.BlockSpec((1, c), lambda b, j: (0, 0))],
        out_specs=pl.BlockSpec((1, CHUNK, c), lambda b, j: (b, j, 0)), out_shape=jax.ShapeDtypeStruct(xbc.shape, F32),
        scratch_shapes=[pltpu.VMEM((CHUNK + 8, c), F32)],
        compiler_params=_params("parallel", "parallel"),
    )(xbc, xbc, conv_w, conv_b)


def _conv_bwd_pre(xbc, dxs, db, dc, conv_w, conv_b, *, name):
    bsz, t, c = xbc.shape
    nc = t // CHUNK

    def body(tail_ref, cur_ref, dxs_ref, db_ref, dc_ref, w_ref, b_ref, dpre_ref, dwb_ref, buf_ref):
        j = pl.program_id(1)

        @pl.when(j == 0)
        def _():
            dwb_ref[...] = jnp.zeros_like(dwb_ref)

        buf_ref[pl.ds(0, 8), :] = jnp.where(j > 0, tail_ref[0], 0.0)
        buf_ref[pl.ds(8, CHUNK), :] = cur_ref[0]
        pre = _conv_taps(buf_ref, w_ref) + b_ref[...]
        s = _sigmoid(pre)
        dsilu = s * (1.0 + pre * (1.0 - s))
        dpre_ref[0, :, pl.ds(0, D_SSM)] = dxs_ref[0] * dsilu[:, :D_SSM]
        dpre_ref[0, :, pl.ds(D_SSM, D_POOL)] = db_ref[0] * dsilu[:, D_SSM:D_SSM + D_POOL]
        dpre_ref[0, :, pl.ds(D_SSM + D_POOL, D_POOL)] = dc_ref[0] * dsilu[:, D_SSM + D_POOL:]
        dpre = dpre_ref[0]
        for k in range(CONV_W):
            shifted = buf_ref[pl.ds(8 - (CONV_W - 1) + k, CHUNK), :]
            dwb_ref[0, k:k + 1, :] += jnp.sum(dpre * shifted, axis=0, keepdims=True)
        dwb_ref[0, CONV_W:CONV_W + 1, :] += jnp.sum(dpre, axis=0, keepdims=True)

    row = lambda width: pl.BlockSpec((1, CHUNK, width), lambda b, j: (b, j, 0))
    return pl.pallas_call(
        body, name=name, grid=(bsz, nc),
        in_specs=[pl.BlockSpec((1, 8, c), lambda b, j: (b, jnp.maximum(j * (CHUNK // 8) - 1, 0), 0)),
                  row(c), row(D_SSM), row(D_POOL), row(D_POOL),
                  pl.BlockSpec((CONV_W, c), lambda b, j: (0, 0)), pl.BlockSpec((1, c), lambda b, j: (0, 0))],
        out_specs=[row(c), pl.BlockSpec((1, 8, c), lambda b, j: (b, 0, 0))],
        out_shape=[jax.ShapeDtypeStruct(xbc.shape, F32), jax.ShapeDtypeStruct((bsz, 8, c), F32)],
        scratch_shapes=[pltpu.VMEM((CHUNK + 8, c), F32)],
        compiler_params=_params("parallel", "arbitrary"),
    )(xbc, xbc, dxs, db, dc, conv_w, conv_b)


def _conv_bwd_in(dpre, conv_w, *, name):
    bsz, t, c = dpre.shape
    nc = t // CHUNK

    def body(cur_ref, head_ref, w_ref, o_ref, buf_ref):
        j = pl.program_id(1)
        buf_ref[pl.ds(0, CHUNK), :] = cur_ref[0]
        buf_ref[pl.ds(CHUNK, 8), :] = jnp.where(j < nc - 1, head_ref[0], 0.0)
        acc = None
        for k in range(CONV_W):
            term = w_ref[k:k + 1, :] * buf_ref[pl.ds(CONV_W - 1 - k, CHUNK), :]
            acc = term if acc is None else acc + term
        o_ref[0] = acc

    return pl.pallas_call(
        body, name=name, grid=(bsz, nc),
        in_specs=[pl.BlockSpec((1, CHUNK, c), lambda b, j: (b, j, 0)),
                  pl.BlockSpec((1, 8, c), lambda b, j: (b, jnp.minimum((j + 1) * (CHUNK // 8), t // 8 - 1), 0)),
                  pl.BlockSpec((CONV_W, c), lambda b, j: (0, 0))],
        out_specs=pl.BlockSpec((1, CHUNK, c), lambda b, j: (b, j, 0)), out_shape=jax.ShapeDtypeStruct(dpre.shape, F32),
        scratch_shapes=[pltpu.VMEM((CHUNK + 8, c), F32)],
        compiler_params=_params("parallel", "parallel"),
    )(dpre, dpre, conv_w)


def _ssd_common(j, dtr_ref, dtb_ref, alog_ref):
    lane = lax.broadcasted_iota(jnp.int32, (CHUNK, LANES), 1)
    row = lax.broadcasted_iota(jnp.int32, (CHUNK, LANES), 0)
    raw = dtr_ref[0] + dtb_ref[0]
    valid = (lane < HPG) & ((j > 0) | (row >= PAD))
    dt = jnp.where(valid, _softplus(raw), 0.0)
    a = -jnp.exp(alog_ref[0])
    tril = (row >= lane).astype(F32)
    acs = _dot_exact(tril, dt * a)
    return dict(lane=lane, row=row, raw=raw, valid=valid, dt=dt, a=a, causal=row >= lane,
                acs=acs, acs_t=acs.T, dt_t=dt.T, aend=acs[CHUNK - 1:CHUNK, :])


def _ssd_specs(nc, rev):
    ch = (lambda j: nc - 1 - j) if rev else (lambda j: j)
    return dict(
        xs=pl.BlockSpec((1, CHUNK, GW), lambda b, g, j: (b, ch(j), g)),
        bm=pl.BlockSpec((1, CHUNK, D_STATE), lambda b, g, j: (b, ch(j), D_SSM // D_STATE + g)),
        cm=pl.BlockSpec((1, CHUNK, D_STATE), lambda b, g, j: (b, ch(j), D_SSM // D_STATE + N_GROUPS + g)),
        lane_blk=pl.BlockSpec((1, CHUNK, LANES), lambda b, g, j: (b, ch(j), g)),
        grp_const=pl.BlockSpec((1, 1, LANES), lambda b, g, j: (g, 0, 0)),
        grp_vec=pl.BlockSpec((1, GW), lambda b, g, j: (0, g)),
        state=pl.BlockSpec((1, 1, D_STATE, GW), lambda b, g, j: (b, ch(j), 0, g)),
    )


def _ssd_fwd(xc, dtr, z, dtb, alog, dskip, normw, *, name, rider=None):
    bsz, t, _ = xc.shape
    nc = t // CHUNK
    sp = _ssd_specs(nc, False)

    def body(xs_ref, b_ref, c_ref, dtr_ref, z_ref, dtb_ref, alog_ref, dsk_ref, nw_ref, yn_ref, y_ref, sp_ref, s_ref):
        j = pl.program_id(2)

        @pl.when(j == 0)
        def _():
            s_ref[...] = jnp.zeros_like(s_ref)

        q = _ssd_common(j, dtr_ref, dtb_ref, alog_ref)
        sp_ref[0, 0] = s_ref[...]
        bm, cm = b_ref[0], c_ref[0]
        cb = _dot_nt(cm, bm)
        low = q["lane"] < HEAD_DIM
        y_even = s_even = None
        for r in range(HPG):
            pair = pl.ds((r // 2) * LANES, LANES)
            col, aend = q["acs"][:, r:r + 1], q["aend"][:, r:r + 1]
            decay = jnp.exp(jnp.where(q["causal"], col - q["acs_t"][r:r + 1, :], -jnp.inf))
            xp, s_old = xs_ref[0, :, pair], s_ref[:, pair]
            y_r = _dot(cb * decay * q["dt_t"][r:r + 1, :], xp) + _dot(cm * jnp.exp(col), s_old)
            k = jnp.exp(aend - col) * q["dt"][:, r:r + 1]
            s_r = jnp.exp(aend) * s_old + _dot_tn(bm * k, xp)
            if r % 2 == 0:
                y_even, s_even = y_r, s_r
            else:
                y_ref[0, :, pair] = jnp.where(low, y_even, y_r)
                s_ref[:, pair] = jnp.where(low, s_even, s_r)
        y = y_ref[0] + dsk_ref[...] * xs_ref[0]
        y_ref[0] = y
        zz = z_ref[0]
        yg = y * (zz * _sigmoid(zz))
        rstd = lax.rsqrt(jnp.mean(yg * yg, axis=-1, keepdims=True) + EPS)
        yn_ref[0] = (yg * rstd * nw_ref[...]).astype(BF16)

    grid = (bsz, N_GROUPS, nc)
    ride = _Ride(rider, body, 9, 3, 1, grid)
    outs = pl.pallas_call(
        ride.body, name=name, grid=grid,
        in_specs=[sp["xs"], sp["bm"], sp["cm"], sp["lane_blk"], sp["xs"], sp["grp_const"], sp["grp_const"],
                  sp["grp_vec"], sp["grp_vec"]] + ride.in_specs,
        out_specs=[sp["xs"], sp["xs"], sp["state"]] + ride.out_specs,
        out_shape=[jax.ShapeDtypeStruct((bsz, t, D_SSM), BF16), jax.ShapeDtypeStruct((bsz, t, D_SSM), F32),
                   jax.ShapeDtypeStruct((bsz, nc, D_STATE, D_SSM), F32)] + ride.out_shape,
        scratch_shapes=[pltpu.VMEM((D_STATE, GW), F32)] + ride.scratch,
        compiler_params=_params(*ride.semantics(("parallel", "parallel", "arbitrary"))),
    )(xc, xc, xc, dtr, z, dtb, alog, dskip, normw, *ride.args)
    return outs[:3], outs[3:]


def _ssd_bwd(xc, dtr, z, ypre, sprev, dyn, dtb, alog, dskip, normw, *, name, rider=None):
    bsz, t, _ = xc.shape
    nc = t // CHUNK
    sp = _ssd_specs(nc, True)

    def body(xs_ref, b_ref, c_ref, dtr_ref, z_ref, y_ref, sp_ref, dyn_ref, dtb_ref, alog_ref, dsk_ref, nw_ref,
             dz_ref, dxs_ref, db_ref, dc_ref, ddt_ref, dnw_ref, dsm_ref, ds_ref):
        j = pl.program_id(2)

        @pl.when(j == 0)
        def _():
            ds_ref[...] = jnp.zeros_like(ds_ref)
            dnw_ref[...] = jnp.zeros_like(dnw_ref)
            dsm_ref[...] = jnp.zeros_like(dsm_ref)

        q = _ssd_common(nc - 1 - j, dtr_ref, dtb_ref, alog_ref)
        lane, row = q["lane"], q["row"]
        lane1 = lane[0:1, :]
        y, zz, nw = y_ref[0], z_ref[0], nw_ref[...]
        sz = _sigmoid(zz)
        sil = zz * sz
        yg = y * sil
        rstd = lax.rsqrt(jnp.mean(yg * yg, axis=-1, keepdims=True) + EPS)
        dyn = dyn_ref[0]
        gn = dyn * nw
        dyg = rstd * (gn - yg * (rstd * rstd) * jnp.mean(gn * yg, axis=-1, keepdims=True))
        dnw_ref[0] += jnp.sum(dyn * yg * rstd, axis=0, keepdims=True)
        dy = dyg * sil
        dz_ref[0] = dyg * y * (sz * (1.0 + zz * (1.0 - sz)))
        xs = xs_ref[0]
        dskip_cols = jnp.sum(dy * xs, axis=0, keepdims=True)

        bm, cm = b_ref[0], c_ref[0]
        cb = _dot_nt(cm, bm)
        zero = jnp.zeros((CHUNK, LANES), F32)
        dcb, dc_acc, db_acc = zero, zero, zero
        dacs, dacs_t, ddt, ddt_t = zero, zero, zero, zero
        dskip_row = jnp.zeros((1, LANES), F32)
        dx_pair = ds_pair = None
        for r in range(HPG):
            jj = r // 2
            pair = pl.ds(jj * LANES, LANES)
            half = (lane < HEAD_DIM) if r % 2 == 0 else (lane >= HEAD_DIM)
            xp = xs[:, jj * LANES:(jj + 1) * LANES]
            dym = jnp.where(half, dy[:, jj * LANES:(jj + 1) * LANES], 0.0)
            xm = jnp.where(half, xp, 0.0)
            s_old = sp_ref[0, 0, :, pair]
            dsm = jnp.where(half, ds_ref[:, pair], 0.0)
            col, aend, dt_col = q["acs"][:, r:r + 1], q["aend"][:, r:r + 1], q["dt"][:, r:r + 1]
            dt_row = q["dt_t"][r:r + 1, :]
            decay = jnp.exp(jnp.where(q["causal"], col - q["acs_t"][r:r + 1, :], -jnp.inf))
            gmat = _dot_nt(dym, xp)
            dx_r = _dot_tn(cb * decay * dt_row, dym)
            dcb = dcb + gmat * decay * dt_row
            w0 = gmat * cb * decay
            cs0 = jnp.sum(w0, axis=0, keepdims=True)
            rs = jnp.sum(w0 * dt_row, axis=1, keepdims=True)
            ea = jnp.exp(col)
            ds_r = _dot_tn(cm * ea, dym)
            t1 = _dot_nt(dym, s_old)
            dc_acc = dc_acc + ea * t1
            qv = jnp.sum(cm * t1, axis=1, keepdims=True)
            dte = jnp.exp(aend - col)
            k = dte * dt_col
            ed = jnp.exp(aend)
            ds_r = ds_r + ed * dsm
            d_aend = _sum_all(dsm * s_old) * ed
            dx_r = dx_r + _dot(bm * k, dsm)
            dbs = _dot_nt(xm, dsm)
            db_acc = db_acc + k * dbs
            dk = jnp.sum(bm * dbs, axis=1, keepdims=True)
            ddte = dk * dt_col
            d_aend = d_aend + _sum_all(ddte * dte)
            dacs_col = rs + qv * ea - ddte * dte + jnp.where(row[:, 0:1] == CHUNK - 1, d_aend, 0.0)
            dacs = jnp.where(lane == r, dacs_col, dacs)
            ddt = jnp.where(lane == r, dk * dte, ddt)
            dacs_t = jnp.where(row == r, -cs0 * dt_row, dacs_t)
            ddt_t = jnp.where(row == r, cs0, ddt_t)
            dsk = _sum_all(jnp.where(half[0:1, :], dskip_cols[:, jj * LANES:(jj + 1) * LANES], 0.0))
            dskip_row = dskip_row + jnp.where(lane1 == r, dsk, 0.0)
            if r % 2 == 0:
                dx_pair, ds_pair = dx_r, ds_r
            else:
                dxs_ref[0, :, pair] = dx_pair + dx_r + dy[:, jj * LANES:(jj + 1) * LANES] * dsk_ref[:, pair]
                ds_ref[:, pair] = ds_pair + ds_r
        dacs = dacs + dacs_t.T
        ddt = ddt + ddt_t.T
        triu = (lane >= row).astype(F32)
        dda = _dot_exact(triu, dacs)
        ddt = ddt + dda * q["a"]
        da = jnp.sum(dda * q["dt"], axis=0, keepdims=True)
        draw = jnp.where(q["valid"], ddt * _sigmoid(q["raw"]), 0.0)
        ddt_ref[0] = draw
        dsm_ref[0, 0, 0:1, :] += dskip_row
        dsm_ref[0, 0, 1:2, :] += da * q["a"]
        dsm_ref[0, 0, 2:3, :] += jnp.sum(draw, axis=0, keepdims=True)
        dc_ref[0] = dc_acc + _dot(dcb, bm)
        db_ref[0] = db_acc + _dot_tn(dcb, cm)

    grp_out = pl.BlockSpec((1, CHUNK, D_STATE), lambda b, g, j: (b, nc - 1 - j, g))
    grid = (bsz, N_GROUPS, nc)
    ride = _Ride(rider, body, 12, 7, 1, grid)
    outs = pl.pallas_call(
        ride.body, name=name, grid=grid,
        in_specs=[sp["xs"], sp["bm"], sp["cm"], sp["lane_blk"], sp["xs"], sp["xs"], sp["state"], sp["xs"],
                  sp["grp_const"], sp["grp_const"], sp["grp_vec"], sp["grp_vec"]] + ride.in_specs,
        out_specs=[sp["xs"], sp["xs"], grp_out, grp_out, sp["lane_blk"],
                   pl.BlockSpec((1, 1, GW), lambda b, g, j: (b, 0, g)),
                   pl.BlockSpec((1, 1, 8, LANES), lambda b, g, j: (b, g, 0, 0))] + ride.out_specs,
        out_shape=[jax.ShapeDtypeStruct((bsz, t, D_SSM), F32), jax.ShapeDtypeStruct((bsz, t, D_SSM), F32),
                   jax.ShapeDtypeStruct((bsz, t, N_GROUPS * D_STATE), F32),
                   jax.ShapeDtypeStruct((bsz, t, N_GROUPS * D_STATE), F32),
                   jax.ShapeDtypeStruct((bsz, t, D_DT), F32), jax.ShapeDtypeStruct((bsz, 1, D_SSM), F32),
                   jax.ShapeDtypeStruct((bsz, N_GROUPS, 8, LANES), F32)] + ride.out_shape,
        scratch_shapes=[pltpu.VMEM((D_STATE, GW), F32)] + ride.scratch,
        compiler_params=_params(*ride.semantics(("parallel", "parallel", "arbitrary"))),
    )(xc, xc, xc, dtr, z, ypre, sprev, dyn, dtb, alog, dskip, normw, *ride.args)
    return outs[:7], outs[7:]


def _input_grad(dhn, h0, w, dres, seq, *, name):
    bsz, t, d = h0.shape
    nc = t // CHUNK

    def body(dy_ref, h_ref, w_ref, dres_ref, gx_ref, head_ref, dw_ref):
        j = pl.program_id(1)

        @pl.when((pl.program_id(0) == 0) & (j == 0))
        def _():
            dw_ref[...] = jnp.zeros_like(dw_ref)

        x, dyv = h_ref[0], dy_ref[0]
        r = lax.rsqrt(jnp.mean(x * x, axis=-1, keepdims=True) + EPS)
        g = dyv * w_ref[...]
        dx = r * (g - x * (r * r) * jnp.mean(g * x, axis=-1, keepdims=True)) + dres_ref[0]
        dw_ref[...] += jnp.sum(dyv * x * r, axis=0, keepdims=True)

        @pl.when(j == 0)
        def _():
            head_ref[0] = dx

        gx_ref[0] = dx

    row = pl.BlockSpec((1, CHUNK, d), lambda b, j: (b, j, 0))
    return pl.pallas_call(
        body, name=name, grid=(bsz, nc),
        in_specs=[row, row, pl.BlockSpec((1, d), lambda b, j: (0, 0)), row],
        out_specs=[pl.BlockSpec((1, CHUNK, d), lambda b, j: (b, jnp.maximum(j - 1, 0), 0)),
                   pl.BlockSpec((1, CHUNK, d), lambda b, j: (b, 0, 0)), pl.BlockSpec((1, d), lambda b, j: (0, 0))],
        out_shape=[jax.ShapeDtypeStruct((bsz, seq, d), F32), jax.ShapeDtypeStruct((bsz, CHUNK, d), F32),
                   jax.ShapeDtypeStruct((1, d), F32)],
        compiler_params=_params("arbitrary", "arbitrary"),
    )(dhn, h0, w, dres)


def _remote(src, dst, send_sem, recv_sem, dev):
    return pltpu.make_async_remote_copy(src_ref=src, dst_ref=dst, send_sem=send_sem, recv_sem=recv_sem,
                                        device_id=dev, device_id_type=MESH)


def _position():
    return lax.axis_index("x"), lax.axis_index("y"), lax.axis_index("c")


def _other_chips(pos):
    x, y, _ = pos
    return [(1 - x, y), (x, 1 - y), (1 - x, 1 - y)]


class _Gather:
    def __init__(self, arrs):
        n = len(arrs)
        self.args, self.n_in, self.n_out = list(arrs), n, n
        self.out_shape = [jax.ShapeDtypeStruct((4,) + a.shape, a.dtype) for a in arrs]
        self.scratch = [pltpu.SemaphoreType.DMA((3 * n,)), pltpu.SemaphoreType.DMA((3 * n,)),
                        pltpu.SemaphoreType.DMA((n,))]

    def _copies(self, pos, ins, outs, sems):
        send_sems, recv_sems, loc_sems = sems
        x, y, c = pos
        me = 2 * x + y
        local = [pltpu.make_async_copy(ins[i], outs[i].at[me], loc_sems.at[i]) for i in range(self.n_in)]
        sends, recvs = [], []
        for i in range(self.n_in):
            for k, (px, py) in enumerate(_other_chips(pos)):
                sems_k = (send_sems.at[3 * i + k], recv_sems.at[3 * i + k], (px, py, c))
                sends.append(_remote(ins[i], outs[i].at[me], *sems_k))
                recvs.append(_remote(ins[i], outs[i].at[2 * px + py], *sems_k))
        return local, sends, recvs

    def start(self, pos, ins, outs, sems):
        local, sends, _ = self._copies(pos, ins, outs, sems)
        for cp in local + sends:
            cp.start()

    def finish(self, pos, ins, outs, sems):
        local, sends, recvs = self._copies(pos, ins, outs, sems)
        for cp in recvs:
            cp.wait_recv()
        for cp in sends:
            cp.wait_send()
        for cp in local:
            cp.wait()


class _Exchange:
    FLIPS = [(fx, fy, fc) for fx in (0, 1) for fy in (0, 1) for fc in (0, 1)][1:]

    def __init__(self, big, small=None):
        n = len(big)
        self.n_big, self.has_small = n, small is not None
        self.args = list(big) + ([small] if self.has_small else [])
        self.n_in = self.n_out = len(self.args)
        self.out_shape = [jax.ShapeDtypeStruct(a.shape, a.dtype) for a in big]
        self.scratch = [pltpu.SemaphoreType.DMA((max(3 * n, 1),)), pltpu.SemaphoreType.DMA((max(3 * n, 1),))]
        if self.has_small:
            self.out_shape.append(jax.ShapeDtypeStruct((8,) + small.shape, small.dtype))
            self.scratch += [pltpu.SemaphoreType.DMA((7,)), pltpu.SemaphoreType.DMA((7,)), pltpu.SemaphoreType.DMA((1,))]

    def _copies(self, pos, ins, outs, sems):
        x, y, c = pos
        me, me8 = 2 * x + y, 4 * x + 2 * y + c
        local, sends, recvs = [], [], []
        for i in range(self.n_big):
            for k, (px, py) in enumerate(_other_chips(pos)):
                sems_k = (sems[0].at[3 * i + k], sems[1].at[3 * i + k], (px, py, c))
                sends.append(_remote(ins[i].at[2 * px + py], outs[i].at[me], *sems_k))
                recvs.append(_remote(ins[i].at[me], outs[i].at[2 * px + py], *sems_k))
        if self.has_small:
            small, landed = ins[self.n_big], outs[self.n_big]
            local.append(pltpu.make_async_copy(small, landed.at[me8], sems[4].at[0]))
            for k, (fx, fy, fc) in enumerate(self.FLIPS):
                peer = (x ^ fx, y ^ fy, c ^ fc)
                sems_k = (sems[2].at[k], sems[3].at[k], peer)
                sends.append(_remote(small, landed.at[me8], *sems_k))
                recvs.append(_remote(small, landed.at[4 * peer[0] + 2 * peer[1] + peer[2]], *sems_k))
        return local, sends, recvs

    start = _Gather.start
    finish = _Gather.finish


class _Swap:
    def __init__(self, arrs):
        n = len(arrs)
        self.args, self.n_in, self.n_out = list(arrs), n, n
        self.out_shape = [jax.ShapeDtypeStruct(a.shape, a.dtype) for a in arrs]
        self.scratch = [pltpu.SemaphoreType.DMA((n,)), pltpu.SemaphoreType.DMA((n,))]

    def _copies(self, pos, ins, outs, sems):
        x, y, c = pos
        both = [_remote(ins[i], outs[i], sems[0].at[i], sems[1].at[i], (x, y, 1 - c)) for i in range(self.n_in)]
        return [], both, both

    start = _Gather.start
    finish = _Gather.finish


def _comm(rider, *, name):
    a, b = rider.n_in, rider.n_in + rider.n_out

    def body(*refs):
        pos = _position()
        rider.start(pos, refs[:a], refs[a:b], refs[b:])
        rider.finish(pos, refs[:a], refs[a:b], refs[b:])

    return pl.pallas_call(body, name=name, in_specs=[ANY] * rider.n_in, out_specs=[ANY] * rider.n_out,
                          out_shape=rider.out_shape, scratch_shapes=rider.scratch)(*rider.args)


class _Ride:
    def __init__(self, rider, body, n_in, n_out, n_scratch, grid):
        self.rider = rider
        self.args = rider.args if rider else []
        self.in_specs = [ANY] * rider.n_in if rider else []
        self.out_specs = [ANY] * rider.n_out if rider else []
        self.out_shape = rider.out_shape if rider else []
        self.scratch = rider.scratch if rider else []
        self.body = self._wrap(body, n_in, n_out, n_scratch, grid) if rider else body

    def semantics(self, sem):
        return ("arbitrary",) * len(sem) if self.rider else sem

    def _wrap(self, body, n_in, n_out, n_scratch, grid):
        rider = self.rider
        a = n_in
        b = a + rider.n_in
        c = b + n_out
        d = c + rider.n_out
        e = d + n_scratch

        def wrapped(*refs):
            pos = _position()
            ids = [pl.program_id(i) for i in range(len(grid))]
            first = functools.reduce(jnp.logical_and, [i == 0 for i in ids])
            last = functools.reduce(jnp.logical_and, [i == g - 1 for i, g in zip(ids, grid)])

            @pl.when(first)
            def _():
                rider.start(pos, refs[a:b], refs[c:d], refs[e:])

            body(*refs[:a], *refs[b:c], *refs[d:e])

            @pl.when(last)
            def _():
                rider.finish(pos, refs[a:b], refs[c:d], refs[e:])

        return wrapped


def _chip_sum(own, landed, *, name):
    r, c = own.shape
    tm = _pick(r, (256, 128, 64, 8))

    def body(own_ref, land_ref, o_ref):
        me = 2 * lax.axis_index("x") + lax.axis_index("y")
        acc = None
        for jchip in range(4):
            term = jnp.where(me == jchip, own_ref[...], land_ref[jchip].astype(F32))
            acc = term if acc is None else acc + term
        o_ref[...] = acc

    return pl.pallas_call(
        body, name=name, grid=(r // tm,),
        in_specs=[pl.BlockSpec((tm, c), lambda i: (i, 0)), pl.BlockSpec((4, tm, c), lambda i: (0, i, 0))],
        out_specs=pl.BlockSpec((tm, c), lambda i: (i, 0)), out_shape=jax.ShapeDtypeStruct((r, c), F32),
        compiler_params=_params("parallel"),
    )(own, landed)


def _device_sum(parts, *, name):
    _, r, c = parts.shape

    def body(p_ref, o_ref):
        acc = p_ref[0]
        for d in range(1, 8):
            acc = acc + p_ref[d]
        o_ref[...] = acc

    return pl.pallas_call(body, name=name, out_shape=jax.ShapeDtypeStruct((r, c), F32))(parts)


def _adamw_math(w, g, m, v):
    m = ADAM_B1 * m + (1.0 - ADAM_B1) * g
    v = ADAM_B2 * v + (1.0 - ADAM_B2) * (g * g)
    m_hat = m / (1.0 - ADAM_B1 ** ADAM_STEP)
    v_hat = v / (1.0 - ADAM_B2 ** ADAM_STEP)
    return -ADAM_LR * (m_hat / (jnp.sqrt(v_hat) + ADAM_EPS) + ADAM_WD * w), m, v


def _adamw(w, g_parts, m, v, *, name):
    r, c = w.shape
    tm = _pick(r, (256, 128, 64, 16, 8, 4, 1)) if r * c > 65536 else r
    n_g = len(g_parts)

    def body(*refs):
        w_ref, m_ref, v_ref = refs[n_g:n_g + 3]
        g_ref, d_ref, nm_ref, nv_ref = refs[n_g + 3:]
        g = refs[0][...]
        for p in refs[1:n_g]:
            g = g + p[...]
        g_ref[...] = g
        d_ref[...], nm_ref[...], nv_ref[...] = _adamw_math(w_ref[...], g, m_ref[...], v_ref[...])

    blk = pl.BlockSpec((tm, c), lambda i: (i, 0))
    return pl.pallas_call(
        body, name=name, grid=(r // tm,), in_specs=[blk] * (n_g + 3), out_specs=[blk] * 4,
        out_shape=[jax.ShapeDtypeStruct((r, c), F32)] * 4, compiler_params=_params("parallel"),
    )(*g_parts, w, m, v)


def _pad_heads(v):
    return jnp.pad(v.reshape(N_GROUPS, 1, HPG), ((0, 0), (0, 0), (0, LANES - HPG)))


def _unpad_heads(v):
    return v[:, :HPG].reshape(1, N_HEADS)


_SMALL = [("norm_mix_w", (1, 1024)), ("pool_w", (512, 128)), ("pool_scale", (1, 512)), ("conv_w", (4, D_XBC)),
          ("conv_b", (1, D_XBC)), ("dt_bias", (1, N_HEADS)), ("a_log", (1, N_HEADS)), ("d_skip", (1, N_HEADS)),
          ("ssm_norm_w", (1, D_SSM)), ("norm_ffn_w", (1, 1024)), ("norm_f_w", (1, 1024)), ("meta", (N_META, 1024))]


def _pack_small(grads):
    rows = []
    for nm, shape in _SMALL:
        flat = grads[nm].reshape(-1)
        rows.append(jnp.pad(flat, (0, (-flat.size) % LANES)).reshape(-1, LANES))
    packed = jnp.concatenate(rows, axis=0)
    return jnp.pad(packed, ((0, (-packed.shape[0]) % 8), (0, 0)))


def _unpack_small(packed):
    out, r0 = {}, 0
    for nm, shape in _SMALL:
        size = shape[0] * shape[1]
        nrow = -(-size // LANES)
        out[nm] = packed[r0:r0 + nrow].reshape(-1)[:size].reshape(shape)
        r0 += nrow
    return out


def kernel(x, meta, norm_mix_w, w_in, pool_w, pool_scale, conv_w, conv_b, dt_bias, a_log, d_skip, ssm_norm_w, w_out, norm_ffn_w, w_ff1, w_ff2, norm_f_w, loss_target, m_meta, m_norm_mix_w, m_w_in, m_pool_w, m_pool_scale, m_conv_w, m_conv_b, m_dt_bias, m_a_log, m_d_skip, m_ssm_norm_w, m_w_out, m_norm_ffn_w, m_w_ff1, m_w_ff2, m_norm_f_w, v_meta, v_norm_mix_w, v_w_in, v_pool_w, v_pool_scale, v_conv_w, v_conv_b, v_dt_bias, v_a_log, v_d_skip, v_ssm_norm_w, v_w_out, v_norm_ffn_w, v_w_ff1, v_w_ff2, v_norm_f_w):
    bsz, seq, d = x.shape
    t = seq + CHUNK
    n = bsz * t
    chip = 2 * lax.axis_index("x") + lax.axis_index("y")
    d_in = w_in.shape[2] * 4

    g_in, g_conv, g_meta = _comm(_Gather([w_in[0].astype(BF16), conv_w[0], meta]), name="gather_in")
    late_weights = _Gather([w_out[0].astype(BF16), w_ff1[0].astype(BF16), w_ff2[0].astype(BF16)])
    win = g_in.transpose(1, 0, 2).reshape(d, d_in)
    wu, wz = win[:, :D_POOL], win[:, D_POOL:D_POOL + D_SSM]
    wx = win[:, D_POOL + D_SSM:D_POOL + D_SSM + D_XBC]
    wdt = jnp.pad(win[:, D_POOL + D_SSM + D_XBC:].reshape(d, N_GROUPS, HPG),
                  ((0, 0), (0, 0), (0, LANES - HPG))).reshape(d, D_DT)
    convw = g_conv.transpose(1, 0, 2).reshape(CONV_W, D_XBC)
    meta_full = g_meta.transpose(1, 0, 2).reshape(N_META, d)
    dtb, alog = _pad_heads(dt_bias), _pad_heads(a_log)
    dskip = jnp.repeat(d_skip, HEAD_DIM, axis=1)
    poolw = pool_w[0]

    h0 = jnp.concatenate([jnp.zeros((bsz, PAD, d), F32), jnp.broadcast_to(meta_full[None], (bsz, N_META, d)), x], axis=1)
    h0f = h0.reshape(n, d)
    tm = _pick(n, (768, 512, 256, 128))
    tm_wide = _pick(n, (384, 256, 128))
    hn1 = _rms_fwd(h0f, norm_mix_w, name="norm_mix")
    u = _mm(hn1, wu, name="proj_u", tm=tm)
    z = _mm(hn1, wz, name="proj_z", tm=tm)
    xbc = _mm(hn1, wx, name="proj_xbc", tm=tm)
    dtr = _mm(hn1, wdt, name="proj_dt", tm=tm)
    ypool = _pool_fwd(u.reshape(bsz, t, D_POOL), poolw, pool_scale, name="pool_fwd")
    xbc3 = xbc.reshape(bsz, t, D_XBC)
    xc = _conv_fwd(xbc3, convw, conv_b, name="conv_fwd")
    z3, dtr3 = z.reshape(bsz, t, D_SSM), dtr.reshape(bsz, t, D_DT)
    (yn, ypre, sprev), (g_out, g_ff1, g_ff2) = _ssd_fwd(xc, dtr3, z3, dtb, alog, dskip, ssm_norm_w, name="ssd_fwd",
                                                        rider=late_weights)
    wo = g_out.reshape(D_POOL + D_SSM, d)
    wo_p, wo_s = wo[:D_POOL], wo[D_POOL:]
    w1 = g_ff1.transpose(1, 0, 2).reshape(d, D_FF)
    w2 = g_ff2.reshape(D_FF, d)
    ypool_f, yn_f = ypool.reshape(n, D_POOL), yn.reshape(n, D_SSM)
    add = lambda r, e: r + e
    h1 = _mm(ypool_f, wo_p, name="out_pool", tm=tm, post=add, extras=(h0f,))
    h1 = _mm(yn_f, wo_s, name="out_ssm", tm=tm, post=add, extras=(h1,))
    hn2 = _rms_fwd(h1, norm_ffn_w, name="norm_ffn")
    act = _mm(hn2, w1, name="ff1", tm=tm)
    relu2 = lambda a: jnp.square(jnp.maximum(a, 0.0))
    h2 = _mm(act, w2, name="ff2", tm=tm_wide, pre=relu2, post=add, extras=(h1,))
    dh2, loss_acc, d_norm_f = _final_norm_loss(h2.reshape(bsz, t, d), loss_target, norm_f_w.reshape(1, d), name="loss")
    loss = lax.psum(loss_acc[0, 0], ("x", "y", "c"))

    dh2f = dh2.reshape(n, d)
    dact = _mm(dh2f, w2, name="ff2_bwd", tm=tm, nt=True, post=lambda r, a: r * (2.0 * jnp.maximum(a, 0.0)),
               extras=(act,), out_dtype=BF16)
    d_w2 = _mm_tn(act, dh2f, name="ff2_dw", tk=1024, tn=1024, tm=tm_wide, pre=relu2)
    d_w1 = _mm_tn(hn2, dact, name="ff1_dw", tk=1024, tn=1024, tm=tm)
    dhn2 = _mm(dact, w1, name="ff1_bwd", tm=tm_wide, nt=True)
    dh1, d_norm_ffn = _rms_bwd(dhn2, h1, norm_ffn_w, dh2f, name="norm_ffn_bwd")
    dypool = _mm(dh1, wo_p, name="out_pool_bwd", tm=tm, nt=True)
    dyn = _mm(dh1, wo_s, name="out_ssm_bwd", tm=tm, nt=True)
    d_wo_p = _mm_tn(ypool_f, dh1, name="out_pool_dw", tk=512, tn=1024, tm=tm)
    d_wo_s = _mm_tn(yn_f, dh1, name="out_ssm_dw", tk=512, tn=1024, tm=tm)
    big_late = [jnp.concatenate([d_wo_p, d_wo_s], axis=0).reshape(4, (D_POOL + D_SSM) // 4, d),
                d_w1.reshape(d, 4, D_FF // 4).transpose(1, 0, 2), d_w2.reshape(4, D_FF // 4, d)]
    (dz, dxs, dbm, dcm, ddtr, d_nw, d_heads), landed_late = _ssd_bwd(
        xc, dtr3, z3, ypre, sprev, dyn.reshape(bsz, t, D_SSM), dtb, alog, dskip, ssm_norm_w, name="ssd_bwd",
        rider=_Exchange([b.astype(BF16) for b in big_late]))
    dpre, d_convwb = _conv_bwd_pre(xbc3, dxs, dbm, dcm, convw, conv_b, name="conv_bwd_pre")
    dxbc = _conv_bwd_in(dpre, convw, name="conv_bwd_in")
    du, d_poolw, d_poolsc = _pool_bwd(u.reshape(bsz, t, D_POOL), dypool.reshape(bsz, t, D_POOL), poolw, pool_scale,
                                      name="pool_bwd")
    duf, dzf, dxbcf, ddtrf = du.reshape(n, D_POOL), dz.reshape(n, D_SSM), dxbc.reshape(n, D_XBC), ddtr.reshape(n, D_DT)
    d_wu = _mm_tn(hn1, duf, name="proj_u_dw", tk=1024, tn=512, tm=tm)
    d_wz = _mm_tn(hn1, dzf, name="proj_z_dw", tk=1024, tn=512, tm=tm)
    d_wx = _mm_tn(hn1, dxbcf, name="proj_xbc_dw", tk=1024, tn=512, tm=tm)
    d_wdt = _mm_tn(hn1, ddtrf, name="proj_dt_dw", tk=1024, tn=512, tm=tm)
    d_win = jnp.concatenate([d_wu, d_wz, d_wx, d_wdt.reshape(d, N_GROUPS, LANES)[:, :, :HPG].reshape(d, N_HEADS)], axis=1)
    big_in = d_win.reshape(d, 4, d_in // 4).transpose(1, 0, 2)
    dhn1 = _mm(duf, wu, name="proj_u_bwd", tm=tm, nt=True)
    dhn1 = _mm(dzf, wz, name="proj_z_bwd", tm=tm, nt=True, post=add, extras=(dhn1,))
    dhn1, landed_in = _mm(dxbcf, wx, name="proj_xbc_bwd", tm=tm_wide, nt=True, post=add, extras=(dhn1,),
                          rider=_Exchange([big_in.astype(BF16)]))
    dhn1 = _mm(ddtrf, wdt, name="proj_dt_bwd", tm=tm, nt=True, post=add, extras=(dhn1,))
    grad_x, d_head_rows, d_norm_mix = _input_grad(
        dhn1.reshape(bsz, t, d), h0, norm_mix_w, dh1.reshape(bsz, t, d), seq, name="input_grad")

    big = [big_in] + big_late
    landed = list(landed_in) + list(landed_late)
    heads = jnp.sum(d_heads, axis=0)
    small = _pack_small({
        "norm_mix_w": d_norm_mix, "pool_w": jnp.sum(d_poolw, axis=0), "pool_scale": jnp.sum(d_poolsc, axis=0),
        "conv_w": jnp.sum(d_convwb[:, :CONV_W], axis=0), "conv_b": jnp.sum(d_convwb[:, CONV_W:CONV_W + 1], axis=0),
        "dt_bias": _unpad_heads(heads[:, 2]), "a_log": _unpad_heads(heads[:, 1]), "d_skip": _unpad_heads(heads[:, 0]),
        "ssm_norm_w": jnp.sum(d_nw, axis=0), "norm_ffn_w": d_norm_ffn, "norm_f_w": d_norm_f,
        "meta": jnp.sum(d_head_rows[:, PAD:], axis=0)})
    (small_all,) = _comm(_Exchange([], small), name="exchange_small")
    own = [lax.dynamic_index_in_dim(b, chip, 0, keepdims=False) for b in big]
    mine = [_chip_sum(o, l, name=f"chip_sum_{i}") for i, (o, l) in enumerate(zip(own, landed))]
    theirs = _comm(_Swap(mine), name="swap_cores")
    gsmall = _unpack_small(_device_sum(small_all, name="device_sum"))
    gsmall["conv_w"] = lax.dynamic_slice_in_dim(gsmall["conv_w"], chip * (D_XBC // 4), D_XBC // 4, axis=1)
    gsmall["meta"] = lax.dynamic_slice_in_dim(gsmall["meta"], chip * (d // 4), d // 4, axis=1)

    given = dict(meta=(meta, m_meta, v_meta), norm_mix_w=(norm_mix_w, m_norm_mix_w, v_norm_mix_w),
                 w_in=(w_in, m_w_in, v_w_in), pool_w=(pool_w, m_pool_w, v_pool_w),
                 pool_scale=(pool_scale, m_pool_scale, v_pool_scale), conv_w=(conv_w, m_conv_w, v_conv_w),
                 conv_b=(conv_b, m_conv_b, v_conv_b), dt_bias=(dt_bias, m_dt_bias, v_dt_bias),
                 a_log=(a_log, m_a_log, v_a_log), d_skip=(d_skip, m_d_skip, v_d_skip),
                 ssm_norm_w=(ssm_norm_w, m_ssm_norm_w, v_ssm_norm_w), w_out=(w_out, m_w_out, v_w_out),
                 norm_ffn_w=(norm_ffn_w, m_norm_ffn_w, v_norm_ffn_w), w_ff1=(w_ff1, m_w_ff1, v_w_ff1),
                 w_ff2=(w_ff2, m_w_ff2, v_w_ff2), norm_f_w=(norm_f_w, m_norm_f_w, v_norm_f_w))
    big_names = ["w_in", "w_out", "w_ff1", "w_ff2"]
    results = {}
    for nm, (w, m, v) in given.items():
        if nm in big_names:
            i = big_names.index(nm)
            parts, shape2 = (mine[i], theirs[i]), mine[i].shape
        else:
            parts, shape2 = (gsmall[nm],), gsmall[nm].shape
        outs = _adamw(w.reshape(shape2), parts, m.reshape(shape2), v.reshape(shape2), name=f"adamw_{nm}")
        results[nm] = [o.reshape(w.shape) for o in outs]
    order = list(given)
    return (loss, grad_x, *[results[nm][0] for nm in order], *[results[nm][1] for nm in order],
            *[results[nm][2] for nm in order], *[results[nm][3] for nm in order])
```

```python
import functools

import jax
import jax.numpy as jnp
from jax import lax
from jax.experimental import pallas as pl
from jax.experimental.pallas import tpu as pltpu

F32 = jnp.float32
BF16 = jnp.bfloat16
MESH = pl.DeviceIdType.MESH
ANY = pl.BlockSpec(memory_space=pl.ANY)

D_MODEL = 1024
N_META = 16
CHUNK = 128
PAD = CHUNK - N_META
POOL_WINDOWS = (2, 4, 8, 16)
D_POOL = 512
POOL_GROUP = 128
D_SSM = 1536
N_HEADS = 24
N_GROUPS = 4
HPG = 6
HEAD_DIM = 64
D_STATE = 128
GW = HPG * HEAD_DIM
D_XBC = D_SSM + 2 * N_GROUPS * D_STATE
D_DT = N_GROUPS * 128
D_FF = 4096
CONV_W = 4
EPS = 1e-5
LANES = 128
VMEM_LIMIT = 56 * 1024 * 1024

ADAM_LR, ADAM_B1, ADAM_B2, ADAM_EPS, ADAM_WD, ADAM_STEP = 0.001, 0.9, 0.999, 1e-08, 0.01, 10


def _params(*sem):
    return pltpu.CompilerParams(dimension_semantics=sem, vmem_limit_bytes=VMEM_LIMIT)


def _pick(n, cands):
    for c in cands:
        if n % c == 0:
            return c
    raise ValueError(f"no block size for {n}")


def _dot(a, b):
    return jnp.dot(a.astype(BF16), b.astype(BF16), preferred_element_type=F32)


def _dot_nt(a, b):
    return lax.dot_general(a.astype(BF16), b.astype(BF16), (((1,), (1,)), ((), ())), preferred_element_type=F32)


def _dot_tn(a, b):
    return lax.dot_general(a.astype(BF16), b.astype(BF16), (((0,), (0,)), ((), ())), preferred_element_type=F32)


def _dot_exact(mask, x):
    m = mask.astype(BF16)
    hi = x.astype(BF16)
    r1 = x - hi.astype(F32)
    mid = r1.astype(BF16)
    lo = (r1 - mid.astype(F32)).astype(BF16)
    dot = lambda t: jnp.dot(m, t, preferred_element_type=F32)
    return dot(hi) + dot(mid) + dot(lo)


def _sigmoid(x):
    return 1.0 / (1.0 + jnp.exp(-x))


def _softplus(x):
    return jnp.maximum(x, 0.0) + jnp.log1p(jnp.exp(-jnp.abs(x)))


def _sum_all(x):
    return jnp.sum(jnp.sum(x, axis=1, keepdims=True), axis=0, keepdims=True)


def _mm(a, w, *, name, tm, tn=512, nt=False, pre=None, post=None, extras=(), out_dtype=F32, rider=None):
    n, k = a.shape
    m = w.shape[0] if nt else w.shape[1]
    tn = min(tn, m)
    n_ex = len(extras)

    def body(a_ref, w_ref, *rest):
        o_ref = rest[n_ex]
        av = a_ref[...]
        if pre is not None:
            av = pre(av)
        r = _dot_nt(av, w_ref[...]) if nt else _dot(av, w_ref[...])
        if post is not None:
            r = post(r, *[e[...] for e in rest[:n_ex]])
        o_ref[...] = r.astype(out_dtype)

    w_spec = pl.BlockSpec((tn, k), lambda i, j: (j, 0)) if nt else pl.BlockSpec((k, tn), lambda i, j: (0, j))
    blk = pl.BlockSpec((tm, tn), lambda i, j: (i, j))
    grid = (n // tm, m // tn)
    ride = _Ride(rider, body, 2 + n_ex, 1, 0, grid)
    outs = pl.pallas_call(
        ride.body, name=name, grid=grid,
        in_specs=[pl.BlockSpec((tm, k), lambda i, j: (i, 0)), w_spec] + [blk] * n_ex + ride.in_specs,
        out_specs=[blk] + ride.out_specs, out_shape=[jax.ShapeDtypeStruct((n, m), out_dtype)] + ride.out_shape,
        scratch_shapes=ride.scratch, compiler_params=_params(*ride.semantics(("parallel", "parallel"))),
    )(a, w, *extras, *ride.args)
    return (outs[0], outs[1:]) if rider else outs[0]


def _mm_tn(a, g, *, name, tk, tn, tm, pre=None):
    n, k = a.shape
    m = g.shape[1]
    tk, tn = min(tk, k), min(tn, m)

    def body(a_ref, g_ref, o_ref):
        @pl.when(pl.program_id(2) == 0)
        def _():
            o_ref[...] = jnp.zeros_like(o_ref)

        av = a_ref[...]
        if pre is not None:
            av = pre(av)
        o_ref[...] += _dot_tn(av, g_ref[...])

    return pl.pallas_call(
        body, name=name, grid=(k // tk, m // tn, n // tm),
        in_specs=[pl.BlockSpec((tm, tk), lambda i, j, r: (r, i)), pl.BlockSpec((tm, tn), lambda i, j, r: (r, j))],
        out_specs=pl.BlockSpec((tk, tn), lambda i, j, r: (i, j)),
        out_shape=jax.ShapeDtypeStruct((k, m), F32),
        compiler_params=_params("parallel", "parallel", "arbitrary"),
    )(a, g)


def _rms_fwd(h, w, *, name):
    n, d = h.shape
    tm = _pick(n, (768, 512, 256, 128))

    def body(h_ref, w_ref, o_ref):
        x = h_ref[...]
        r = lax.rsqrt(jnp.mean(x * x, axis=-1, keepdims=True) + EPS)
        o_ref[...] = (x * r * w_ref[...]).astype(BF16)

    return pl.pallas_call(
        body, name=name, grid=(n // tm,),
        in_specs=[pl.BlockSpec((tm, d), lambda i: (i, 0)), pl.BlockSpec((1, d), lambda i: (0, 0))],
        out_specs=pl.BlockSpec((tm, d), lambda i: (i, 0)), out_shape=jax.ShapeDtypeStruct((n, d), BF16),
        compiler_params=_params("parallel"),
    )(h, w)


def _rms_bwd(dy, h, w, dres, *, name):
    n, d = h.shape
    tm = _pick(n, (768, 512, 256, 128))

    def body(dy_ref, h_ref, w_ref, dres_ref, dx_ref, dw_ref):
        @pl.when(pl.program_id(0) == 0)
        def _():
            dw_ref[...] = jnp.zeros_like(dw_ref)

        x, dyv = h_ref[...], dy_ref[...]
        r = lax.rsqrt(jnp.mean(x * x, axis=-1, keepdims=True) + EPS)
        g = dyv * w_ref[...]
        dx_ref[...] = r * (g - x * (r * r) * jnp.mean(g * x, axis=-1, keepdims=True)) + dres_ref[...]
        dw_ref[...] += jnp.sum(dyv * x * r, axis=0, keepdims=True)

    row = pl.BlockSpec((tm, d), lambda i: (i, 0))
    vec = pl.BlockSpec((1, d), lambda i: (0, 0))
    return pl.pallas_call(
        body, name=name, grid=(n // tm,), in_specs=[row, row, vec, row], out_specs=[row, vec],
        out_shape=[jax.ShapeDtypeStruct((n, d), F32), jax.ShapeDtypeStruct((1, d), F32)],
        compiler_params=_params("arbitrary"),
    )(dy, h, w, dres)


def _final_norm_loss(h2, target, w, *, name):
    bsz, t, d = h2.shape
    nc = t // CHUNK

    def body(h_ref, t_ref, w_ref, dh_ref, loss_ref, dw_ref):
        j = pl.program_id(1)

        @pl.when((pl.program_id(0) == 0) & (j == 0))
        def _():
            loss_ref[...] = jnp.zeros_like(loss_ref)
            dw_ref[...] = jnp.zeros_like(dw_ref)

        x, wv = h_ref[0], w_ref[...]
        r = lax.rsqrt(jnp.mean(x * x, axis=-1, keepdims=True) + EPS)
        diff = jnp.where(j > 0, x * r * wv - t_ref[0], 0.0)
        loss_ref[...] += _sum_all(diff * diff) * (0.5 / d)
        dy = diff * (1.0 / d)
        g = dy * wv
        dh_ref[0] = r * (g - x * (r * r) * jnp.mean(g * x, axis=-1, keepdims=True))
        dw_ref[...] += jnp.sum(dy * x * r, axis=0, keepdims=True)

    return pl.pallas_call(
        body, name=name, grid=(bsz, nc),
        in_specs=[pl.BlockSpec((1, CHUNK, d), lambda b, j: (b, j, 0)),
                  pl.BlockSpec((1, CHUNK, d), lambda b, j: (b, jnp.maximum(j - 1, 0), 0)),
                  pl.BlockSpec((1, d), lambda b, j: (0, 0))],
        out_specs=[pl.BlockSpec((1, CHUNK, d), lambda b, j: (b, j, 0)),
                   pl.BlockSpec((8, LANES), lambda b, j: (0, 0)),
                   pl.BlockSpec((1, d), lambda b, j: (0, 0))],
        out_shape=[jax.ShapeDtypeStruct((bsz, t, d), F32), jax.ShapeDtypeStruct((8, LANES), F32),
                   jax.ShapeDtypeStruct((1, d), F32)],
        compiler_params=_params("arbitrary", "arbitrary"),
    )(h2, target, w)


def _pool_masks(j, transposed):
    r = lax.broadcasted_iota(jnp.int32, (CHUNK, 2 * CHUNK), 0)
    c = lax.broadcasted_iota(jnp.int32, (CHUNK, 2 * CHUNK), 1)
    masks = []
    for w in POOL_WINDOWS:
        if transposed:
            m = (c >= r) & (c < r + w)
        else:
            s = c - CHUNK
            m = (s <= r) & (s > r - w) & (s + j * CHUNK >= 0)
        masks.append(m.astype(F32))
    return masks


def _pool_count(t_global, w):
    return jnp.clip(t_global - PAD + 1, 1, w).astype(F32)


def _pool_fwd(u, pool_w, pool_scale, *, name):
    bsz, t, _ = u.shape
    nc = t // CHUNK

    def body(prev_ref, cur_ref, pw_ref, sc_ref, o_ref):
        j = pl.program_id(1)
        masks = _pool_masks(j, False)
        tg = j * CHUNK + lax.broadcasted_iota(jnp.int32, (CHUNK, 1), 0)
        for gi, w in enumerate(POOL_WINDOWS):
            sl = pl.ds(gi * POOL_GROUP, POOL_GROUP)
            cur = cur_ref[0, :, sl]
            both = jnp.concatenate([prev_ref[0, :, sl], cur], axis=0)
            pooled = _dot_exact(masks[gi], both) / _pool_count(tg, w) - cur
            o_ref[0, :, sl] = (_dot(pooled, pw_ref[gi]) * sc_ref[:, sl]).astype(BF16)

    blk = lambda f: pl.BlockSpec((1, CHUNK, D_POOL), f)
    return pl.pallas_call(
        body, name=name, grid=(bsz, nc),
        in_specs=[blk(lambda b, j: (b, jnp.maximum(j - 1, 0), 0)), blk(lambda b, j: (b, j, 0)),
                  pl.BlockSpec((4, POOL_GROUP, POOL_GROUP), lambda b, j: (0, 0, 0)),
                  pl.BlockSpec((1, D_POOL), lambda b, j: (0, 0))],
        out_specs=blk(lambda b, j: (b, j, 0)), out_shape=jax.ShapeDtypeStruct(u.shape, BF16),
        compiler_params=_params("parallel", "parallel"),
    )(u, u, pool_w, pool_scale)


def _pool_bwd(u, dyp, pool_w, pool_scale, *, name):
    bsz, t, _ = u.shape
    nc = t // CHUNK

    def body(prev_ref, cur_ref, dy_ref, dyn_ref, pw_ref, sc_ref, du_ref, dpw_ref, dsc_ref):
        j = pl.program_id(1)

        @pl.when(j == 0)
        def _():
            dpw_ref[...] = jnp.zeros_like(dpw_ref)
            dsc_ref[...] = jnp.zeros_like(dsc_ref)

        fwd = _pool_masks(j, False)
        bwd = _pool_masks(j, True)
        tg = j * CHUNK + lax.broadcasted_iota(jnp.int32, (CHUNK, 1), 0)
        has_next = j < nc - 1
        for gi, w in enumerate(POOL_WINDOWS):
            sl = pl.ds(gi * POOL_GROUP, POOL_GROUP)
            cur, pw, sc = cur_ref[0, :, sl], pw_ref[gi], sc_ref[:, sl]
            both = jnp.concatenate([prev_ref[0, :, sl], cur], axis=0)
            pooled = _dot_exact(fwd[gi], both) / _pool_count(tg, w) - cur
            dy = dy_ref[0, :, sl]
            dsc_ref[0, :, sl] += jnp.sum(dy * _dot(pooled, pw), axis=0, keepdims=True)
            dm = dy * sc
            dpw_ref[0, gi] += _dot_tn(pooled, dm)
            dpooled = _dot_nt(dm, pw)
            dpooled_next = _dot_nt(jnp.where(has_next, dyn_ref[0, :, sl], 0.0) * sc, pw)
            e = jnp.concatenate([dpooled / _pool_count(tg, w), dpooled_next / _pool_count(tg + CHUNK, w)], axis=0)
            du_ref[0, :, sl] = _dot_exact(bwd[gi], e) - dpooled

    blk = lambda f: pl.BlockSpec((1, CHUNK, D_POOL), f)
    return pl.pallas_call(
        body, name=name, grid=(bsz, nc),
        in_specs=[blk(lambda b, j: (b, jnp.maximum(j - 1, 0), 0)), blk(lambda b, j: (b, j, 0)),
                  blk(lambda b, j: (b, j, 0)), blk(lambda b, j: (b, jnp.minimum(j + 1, nc - 1), 0)),
                  pl.BlockSpec((4, POOL_GROUP, POOL_GROUP), lambda b, j: (0, 0, 0)),
                  pl.BlockSpec((1, D_POOL), lambda b, j: (0, 0))],
        out_specs=[blk(lambda b, j: (b, j, 0)),
                   pl.BlockSpec((1, 4, POOL_GROUP, POOL_GROUP), lambda b, j: (b, 0, 0, 0)),
                   pl.BlockSpec((1, 1, D_POOL), lambda b, j: (b, 0, 0))],
        out_shape=[jax.ShapeDtypeStruct(u.shape, F32), jax.ShapeDtypeStruct((bsz, 4, POOL_GROUP, POOL_GROUP), F32),
                   jax.ShapeDtypeStruct((bsz, 1, D_POOL), F32)],
        compiler_params=_params("parallel", "arbitrary"),
    )(u, u, dyp, dyp, pool_w, pool_scale)


def _conv_taps(buf_ref, w_ref):
    acc = None
    for k in range(CONV_W):
        term = w_ref[k:k + 1, :] * buf_ref[pl.ds(8 - (CONV_W - 1) + k, CHUNK), :]
        acc = term if acc is None else acc + term
    return acc


def _conv_fwd(xbc, conv_w, conv_b, *, name):
    bsz, t, c = xbc.shape
    nc = t // CHUNK

    def body(tail_ref, cur_ref, w_ref, b_ref, o_ref, buf_ref):
        j = pl.program_id(1)
        buf_ref[pl.ds(0, 8), :] = jnp.where(j > 0, tail_ref[0], 0.0)
        buf_ref[pl.ds(8, CHUNK), :] = cur_ref[0]
        pre = _conv_taps(buf_ref, w_ref) + b_ref[...]
        o_ref[0] = pre * _sigmoid(pre)

    return pl.pallas_call(
        body, name=name, grid=(bsz, nc),
        in_specs=[pl.BlockSpec((1, 8, c), lambda b, j: (b, jnp.maximum(j * (CHUNK // 8) - 1, 0), 0)),
                  pl.BlockSpec((1, CHUNK, c), lambda b, j: (b, j, 0)),
                  pl.BlockSpec((CONV_W, c), lambda b, j: (0, 0)), pl.BlockSpec((1, c), lambda b, j: (0, 0))],
        out_specs=pl.BlockSpec((1, CHUNK, c), lambda b, j: (b, j, 0)), out_shape=jax.ShapeDtypeStruct(xbc.shape, F32),
        scratch_shapes=[pltpu.VMEM((CHUNK + 8, c), F32)],
        compiler_params=_params("parallel", "parallel"),
    )(xbc, xbc, conv_w, conv_b)


def _conv_bwd_pre(xbc, dxs, db, dc, conv_w, conv_b, *, name):
    bsz, t, c = xbc.shape
    nc = t // CHUNK

    def body(tail_ref, cur_ref, dxs_ref, db_ref, dc_ref, w_ref, b_ref, dpre_ref, dwb_ref, buf_ref):
        j = pl.program_id(1)

        @pl.when(j == 0)
        def _():
            dwb_ref[...] = jnp.zeros_like(dwb_ref)

        buf_ref[pl.ds(0, 8), :] = jnp.where(j > 0, tail_ref[0], 0.0)
        buf_ref[pl.ds(8, CHUNK), :] = cur_ref[0]
        pre = _conv_taps(buf_ref, w_ref) + b_ref[...]
        s = _sigmoid(pre)
        dsilu = s * (1.0 + pre * (1.0 - s))
        dpre_ref[0, :, pl.ds(0, D_SSM)] = dxs_ref[0] * dsilu[:, :D_SSM]
        dpre_ref[0, :, pl.ds(D_SSM, D_POOL)] = db_ref[0] * dsilu[:, D_SSM:D_SSM + D_POOL]
        dpre_ref[0, :, pl.ds(D_SSM + D_POOL, D_POOL)] = dc_ref[0] * dsilu[:, D_SSM + D_POOL:]
        dpre = dpre_ref[0]
        for k in range(CONV_W):
            shifted = buf_ref[pl.ds(8 - (CONV_W - 1) + k, CHUNK), :]
            dwb_ref[0, k:k + 1, :] += jnp.sum(dpre * shifted, axis=0, keepdims=True)
        dwb_ref[0, CONV_W:CONV_W + 1, :] += jnp.sum(dpre, axis=0, keepdims=True)

    row = lambda width: pl.BlockSpec((1, CHUNK, width), lambda b, j: (b, j, 0))
    return pl.pallas_call(
        body, name=name, grid=(bsz, nc),
        in_specs=[pl.BlockSpec((1, 8, c), lambda b, j: (b, jnp.maximum(j * (CHUNK // 8) - 1, 0), 0)),
                  row(c), row(D_SSM), row(D_POOL), row(D_POOL),
                  pl.BlockSpec((CONV_W, c), lambda b, j: (0, 0)), pl.BlockSpec((1, c), lambda b, j: (0, 0))],
        out_specs=[row(c), pl.BlockSpec((1, 8, c), lambda b, j: (b, 0, 0))],
        out_shape=[jax.ShapeDtypeStruct(xbc.shape, F32), jax.ShapeDtypeStruct((bsz, 8, c), F32)],
        scratch_shapes=[pltpu.VMEM((CHUNK + 8, c), F32)],
        compiler_params=_params("parallel", "arbitrary"),
    )(xbc, xbc, dxs, db, dc, conv_w, conv_b)


def _conv_bwd_in(dpre, conv_w, *, name):
    bsz, t, c = dpre.shape
    nc = t // CHUNK

    def body(cur_ref, head_ref, w_ref, o_ref, buf_ref):
        j = pl.program_id(1)
        buf_ref[pl.ds(0, CHUNK), :] = cur_ref[0]
        buf_ref[pl.ds(CHUNK, 8), :] = jnp.where(j < nc - 1, head_ref[0], 0.0)
        acc = None
        for k in range(CONV_W):
            term = w_ref[k:k + 1, :] * buf_ref[pl.ds(CONV_W - 1 - k, CHUNK), :]
            acc = term if acc is None else acc + term
        o_ref[0] = acc

    return pl.pallas_call(
        body, name=name, grid=(bsz, nc),
        in_specs=[pl.BlockSpec((1, CHUNK, c), lambda b, j: (b, j, 0)),
                  pl.BlockSpec((1, 8, c), lambda b, j: (b, jnp.minimum((j + 1) * (CHUNK // 8), t // 8 - 1), 0)),
                  pl.BlockSpec((CONV_W, c), lambda b, j: (0, 0))],
        out_specs=pl.BlockSpec((1, CHUNK, c), lambda b, j: (b, j, 0)), out_shape=jax.ShapeDtypeStruct(dpre.shape, F32),
        scratch_shapes=[pltpu.VMEM((CHUNK + 8, c), F32)],
        compiler_params=_params("parallel", "parallel"),
    )(dpre, dpre, conv_w)


def _ssd_common(j, dtr, dtb, alog):
    lane = lax.broadcasted_iota(jnp.int32, (CHUNK, LANES), 1)
    row = lax.broadcasted_iota(jnp.int32, (CHUNK, LANES), 0)
    raw = dtr + dtb
    valid = (lane < HPG) & ((j > 0) | (row >= PAD))
    dt = jnp.where(valid, _softplus(raw), 0.0)
    a = -jnp.exp(alog)
    tril = (row >= lane).astype(F32)
    acs = _dot_exact(tril, dt * a)
    return dict(lane=lane, row=row, raw=raw, valid=valid, dt=dt, a=a, causal=row >= lane,
                acs=acs, acs_t=acs.T, dt_t=dt.T, aend=acs[CHUNK - 1:CHUNK, :])


def _ssd_specs(bsz, nc, rev):
    ch = (lambda j: nc - 1 - j) if rev else (lambda j: j)
    return dict(
        xs=pl.BlockSpec((bsz, CHUNK, GW), lambda g, j: (0, ch(j), g)),
        bm=pl.BlockSpec((bsz, CHUNK, D_STATE), lambda g, j: (0, ch(j), D_SSM // D_STATE + g)),
        cm=pl.BlockSpec((bsz, CHUNK, D_STATE), lambda g, j: (0, ch(j), D_SSM // D_STATE + N_GROUPS + g)),
        lane_blk=pl.BlockSpec((bsz, CHUNK, LANES), lambda g, j: (0, ch(j), g)),
        grp_const=pl.BlockSpec((1, 1, LANES), lambda g, j: (g, 0, 0)),
        grp_vec=pl.BlockSpec((1, GW), lambda g, j: (0, g)),
        state=pl.BlockSpec((bsz, 1, D_STATE, GW), lambda g, j: (0, ch(j), 0, g)),
    )


def _ssd_fwd(xc, dtr, z, dtb, alog, dskip, normw, *, name, rider=None):
    bsz, t, _ = xc.shape
    nc = t // CHUNK
    sp = _ssd_specs(bsz, nc, False)

    def body(xs_ref, b_ref, c_ref, dtr_ref, z_ref, dtb_ref, alog_ref, dsk_ref, nw_ref, yn_ref, y_ref, sp_ref, s_ref):
        j = pl.program_id(1)

        @pl.when(j == 0)
        def _():
            s_ref[...] = jnp.zeros_like(s_ref)

        ex = range(bsz)
        units = [(e, r) for e in ex for r in range(HPG)]
        full = lambda v: jnp.broadcast_to(v, (CHUNK, LANES))
        pair = lambda r: pl.ds((r // 2) * LANES, LANES)
        q = [_ssd_common(j, dtr_ref[e], dtb_ref[0], alog_ref[0]) for e in ex]
        for e in ex:
            sp_ref[e, 0] = s_ref[e]
        bm, cm = [b_ref[e] for e in ex], [c_ref[e] for e in ex]
        cb = [_dot_nt(cm[e], bm[e]) for e in ex]
        low = q[0]["lane"] < HEAD_DIM
        col = {(e, r): full(q[e]["acs"][:, r:r + 1]) for e, r in units}
        aend = {(e, r): q[e]["aend"][:, r:r + 1] for e, r in units}
        decay = {(e, r): jnp.exp(jnp.where(q[e]["causal"], col[e, r] - q[e]["acs_t"][r:r + 1, :], -jnp.inf))
                 for e, r in units}
        mp = {(e, r): cb[e] * decay[e, r] * q[e]["dt_t"][r:r + 1, :] for e, r in units}
        ce = {(e, r): cm[e] * jnp.exp(col[e, r]) for e, r in units}
        bk = {(e, r): bm[e] * (jnp.exp(aend[e, r] - col[e, r]) * full(q[e]["dt"][:, r:r + 1])) for e, r in units}
        xp = {(e, r): xs_ref[e, :, pair(r)] for e, r in units}
        s_old = {(e, r): s_ref[e, :, pair(r)] for e, r in units}
        y_h = {u: _dot(mp[u], xp[u]) + _dot(ce[u], s_old[u]) for u in units}
        s_h = {u: jnp.exp(aend[u]) * s_old[u] + _dot_tn(bk[u], xp[u]) for u in units}
        for e in ex:
            for r in range(0, HPG, 2):
                y_ref[e, :, pair(r)] = jnp.where(low, y_h[e, r], y_h[e, r + 1])
                s_ref[e, :, pair(r)] = jnp.where(low, s_h[e, r], s_h[e, r + 1])
        y = [y_ref[e] + dsk_ref[...] * xs_ref[e] for e in ex]
        zz = [z_ref[e] for e in ex]
        yg = [y[e] * (zz[e] * _sigmoid(zz[e])) for e in ex]
        rstd = [lax.rsqrt(jnp.mean(yg[e] * yg[e], axis=-1, keepdims=True) + EPS) for e in ex]
        for e in ex:
            y_ref[e] = y[e]
            yn_ref[e] = (yg[e] * rstd[e] * nw_ref[...]).astype(BF16)

    grid = (N_GROUPS, nc)
    ride = _Ride(rider, body, 9, 3, 1, grid)
    outs = pl.pallas_call(
        ride.body, name=name, grid=grid,
        in_specs=[sp["xs"], sp["bm"], sp["cm"], sp["lane_blk"], sp["xs"], sp["grp_const"], sp["grp_const"],
                  sp["grp_vec"], sp["grp_vec"]] + ride.in_specs,
        out_specs=[sp["xs"], sp["xs"], sp["state"]] + ride.out_specs,
        out_shape=[jax.ShapeDtypeStruct((bsz, t, D_SSM), BF16), jax.ShapeDtypeStruct((bsz, t, D_SSM), F32),
                   jax.ShapeDtypeStruct((bsz, nc, D_STATE, D_SSM), F32)] + ride.out_shape,
        scratch_shapes=[pltpu.VMEM((bsz, D_STATE, GW), F32)] + ride.scratch,
        compiler_params=_params(*ride.semantics(("parallel", "arbitrary"))),
    )(xc, xc, xc, dtr, z, dtb, alog, dskip, normw, *ride.args)
    return outs[:3], outs[3:]


def _ssd_bwd(xc, dtr, z, ypre, sprev, dyn, dtb, alog, dskip, normw, *, name, rider=None):
    bsz, t, _ = xc.shape
    nc = t // CHUNK
    sp = _ssd_specs(bsz, nc, True)

    def body(xs_ref, b_ref, c_ref, dtr_ref, z_ref, y_ref, sp_ref, dyn_ref, dtb_ref, alog_ref, dsk_ref, nw_ref,
             dz_ref, dxs_ref, db_ref, dc_ref, ddt_ref, dnw_ref, dsm_ref, ds_ref):
        j = pl.program_id(1)

        @pl.when(j == 0)
        def _():
            ds_ref[...] = jnp.zeros_like(ds_ref)
            dnw_ref[...] = jnp.zeros_like(dnw_ref)
            dsm_ref[...] = jnp.zeros_like(dsm_ref)

        ex = range(bsz)
        heads = range(HPG)
        units = [(e, r) for e in ex for r in heads]
        q = [_ssd_common(nc - 1 - j, dtr_ref[e], dtb_ref[0], alog_ref[0]) for e in ex]
        lane, row = q[0]["lane"], q[0]["row"]
        lane1 = lane[0:1, :]
        nw = nw_ref[...]
        y, zz, dyn = [y_ref[e] for e in ex], [z_ref[e] for e in ex], [dyn_ref[e] for e in ex]
        sz = [_sigmoid(zz[e]) for e in ex]
        sil = [zz[e] * sz[e] for e in ex]
        yg = [y[e] * sil[e] for e in ex]
        rstd = [lax.rsqrt(jnp.mean(yg[e] * yg[e], axis=-1, keepdims=True) + EPS) for e in ex]
        gn = [dyn[e] * nw for e in ex]
        dyg = [rstd[e] * (gn[e] - yg[e] * (rstd[e] * rstd[e]) * jnp.mean(gn[e] * yg[e], axis=-1, keepdims=True))
               for e in ex]
        dy = [dyg[e] * sil[e] for e in ex]
        xs = [xs_ref[e] for e in ex]
        for e in ex:
            dnw_ref[e] += jnp.sum(dyn[e] * yg[e] * rstd[e], axis=0, keepdims=True)
            dz_ref[e] = dyg[e] * y[e] * (sz[e] * (1.0 + zz[e] * (1.0 - sz[e])))
        dskip_cols = [jnp.sum(dy[e] * xs[e], axis=0, keepdims=True) for e in ex]

        bm, cm = [b_ref[e] for e in ex], [c_ref[e] for e in ex]
        cb = [_dot_nt(cm[e], bm[e]) for e in ex]
        zero = jnp.zeros((CHUNK, LANES), F32)
        full = lambda v: jnp.broadcast_to(v, (CHUNK, LANES))
        low = lane < HEAD_DIM
        half = [low if r % 2 == 0 else ~low for r in heads]
        sl = lambda v, r: v[:, (r // 2) * LANES:(r // 2 + 1) * LANES]
        pair = lambda r: pl.ds((r // 2) * LANES, LANES)
        col = {(e, r): full(q[e]["acs"][:, r:r + 1]) for e, r in units}
        dt_col = {(e, r): full(q[e]["dt"][:, r:r + 1]) for e, r in units}
        aend = {(e, r): q[e]["aend"][:, r:r + 1] for e, r in units}
        dt_row = {(e, r): q[e]["dt_t"][r:r + 1, :] for e, r in units}
        decay = {(e, r): jnp.exp(jnp.where(q[e]["causal"], col[e, r] - q[e]["acs_t"][r:r + 1, :], -jnp.inf))
                 for e, r in units}
        ea = {u: jnp.exp(col[u]) for u in units}
        dte = {u: jnp.exp(aend[u] - col[u]) for u in units}
        ed = {u: jnp.exp(aend[u]) for u in units}
        k = {u: dte[u] * dt_col[u] for u in units}
        mp = {(e, r): cb[e] * decay[e, r] * dt_row[e, r] for e, r in units}
        xp = {(e, r): sl(xs[e], r) for e, r in units}
        dym = {(e, r): jnp.where(half[r], sl(dy[e], r), 0.0) for e, r in units}
        xm = {(e, r): jnp.where(half[r], xp[e, r], 0.0) for e, r in units}
        s_old = {(e, r): sp_ref[e, 0, :, pair(r)] for e, r in units}
        dsm = {(e, r): jnp.where(half[r], ds_ref[e, :, pair(r)], 0.0) for e, r in units}
        gmat = {u: _dot_nt(dym[u], xp[u]) for u in units}
        t1 = {u: _dot_nt(dym[u], s_old[u]) for u in units}
        dbs = {u: _dot_nt(xm[u], dsm[u]) for u in units}
        dx = {(e, r): _dot_tn(mp[e, r], dym[e, r]) + _dot(bm[e] * k[e, r], dsm[e, r]) for e, r in units}
        ds = {(e, r): _dot_tn(cm[e] * ea[e, r], dym[e, r]) + ed[e, r] * dsm[e, r] for e, r in units}
        w0 = {(e, r): gmat[e, r] * cb[e] * decay[e, r] for e, r in units}
        cs0 = {u: jnp.sum(w0[u], axis=0, keepdims=True) for u in units}
        rs = {u: jnp.sum(w0[u] * dt_row[u], axis=1, keepdims=True) for u in units}
        qv = {(e, r): jnp.sum(cm[e] * t1[e, r], axis=1, keepdims=True) for e, r in units}
        dk = {(e, r): jnp.sum(bm[e] * dbs[e, r], axis=1, keepdims=True) for e, r in units}
        ddte = {u: dk[u] * dt_col[u] for u in units}
        d_aend = {u: _sum_all(dsm[u] * s_old[u]) * ed[u] + _sum_all(ddte[u][:, 0:1] * dte[u][:, 0:1]) for u in units}
        last_row = row == CHUNK - 1
        dacs_col = {u: rs[u] + qv[u] * ea[u] - ddte[u] * dte[u] + jnp.where(last_row, d_aend[u], 0.0) for u in units}
        triu = (lane >= row).astype(F32)
        for e in ex:
            dcb, dc_acc, db_acc = zero, zero, zero
            dacs, dacs_t, ddt, ddt_t = zero, zero, zero, zero
            dskip_row = jnp.zeros((1, LANES), F32)
            for r in heads:
                u = (e, r)
                dcb = dcb + gmat[u] * decay[u] * dt_row[u]
                dc_acc = dc_acc + ea[u] * t1[u]
                db_acc = db_acc + k[u] * dbs[u]
                dacs = jnp.where(lane == r, dacs_col[u], dacs)
                ddt = jnp.where(lane == r, dk[u] * dte[u], ddt)
                dacs_t = jnp.where(row == r, -cs0[u] * dt_row[u], dacs_t)
                ddt_t = jnp.where(row == r, cs0[u], ddt_t)
                dsk = _sum_all(jnp.where(half[r][0:1, :], sl(dskip_cols[e], r), 0.0))
                dskip_row = dskip_row + jnp.where(lane1 == r, dsk, 0.0)
            for r in range(0, HPG, 2):
                dxs_ref[e, :, pair(r)] = dx[e, r] + dx[e, r + 1] + sl(dy[e], r) * dsk_ref[:, pair(r)]
                ds_ref[e, :, pair(r)] = ds[e, r] + ds[e, r + 1]
            dacs = dacs + dacs_t.T
            ddt = ddt + ddt_t.T
            dda = _dot_exact(triu, dacs)
            ddt = ddt + dda * q[e]["a"]
            da = jnp.sum(dda * q[e]["dt"], axis=0, keepdims=True)
            draw = jnp.where(q[e]["valid"], ddt * _sigmoid(q[e]["raw"]), 0.0)
            ddt_ref[e] = draw
            dsm_ref[e, 0, 0:1, :] += dskip_row
            dsm_ref[e, 0, 1:2, :] += da * q[e]["a"]
            dsm_ref[e, 0, 2:3, :] += jnp.sum(draw, axis=0, keepdims=True)
            dc_ref[e] = dc_acc + _dot(dcb, bm[e])
            db_ref[e] = db_acc + _dot_tn(dcb, cm[e])

    grp_out = pl.BlockSpec((bsz, CHUNK, D_STATE), lambda g, j: (0, nc - 1 - j, g))
    grid = (N_GROUPS, nc)
    ride = _Ride(rider, body, 12, 7, 1, grid)
    outs = pl.pallas_call(
        ride.body, name=name, grid=grid,
        in_specs=[sp["xs"], sp["bm"], sp["cm"], sp["lane_blk"], sp["xs"], sp["xs"], sp["state"], sp["xs"],
                  sp["grp_const"], sp["grp_const"], sp["grp_vec"], sp["grp_vec"]] + ride.in_specs,
        out_specs=[sp["xs"], sp["xs"], grp_out, grp_out, sp["lane_blk"],
                   pl.BlockSpec((bsz, 1, GW), lambda g, j: (0, 0, g)),
                   pl.BlockSpec((bsz, 1, 8, LANES), lambda g, j: (0, g, 0, 0))] + ride.out_specs,
        out_shape=[jax.ShapeDtypeStruct((bsz, t, D_SSM), F32), jax.ShapeDtypeStruct((bsz, t, D_SSM), F32),
                   jax.ShapeDtypeStruct((bsz, t, N_GROUPS * D_STATE), F32),
                   jax.ShapeDtypeStruct((bsz, t, N_GROUPS * D_STATE), F32),
                   jax.ShapeDtypeStruct((bsz, t, D_DT), F32), jax.ShapeDtypeStruct((bsz, 1, D_SSM), F32),
                   jax.ShapeDtypeStruct((bsz, N_GROUPS, 8, LANES), F32)] + ride.out_shape,
        scratch_shapes=[pltpu.VMEM((bsz, D_STATE, GW), F32)] + ride.scratch,
        compiler_params=_params(*ride.semantics(("parallel", "arbitrary"))),
    )(xc, xc, xc, dtr, z, ypre, sprev, dyn, dtb, alog, dskip, normw, *ride.args)
    return outs[:7], outs[7:]


def _input_grad(dhn, h0, w, dres, seq, *, name):
    bsz, t, d = h0.shape
    nc = t // CHUNK

    def body(dy_ref, h_ref, w_ref, dres_ref, gx_ref, head_ref, dw_ref):
        j = pl.program_id(1)

        @pl.when((pl.program_id(0) == 0) & (j == 0))
        def _():
            dw_ref[...] = jnp.zeros_like(dw_ref)

        x, dyv = h_ref[0], dy_ref[0]
        r = lax.rsqrt(jnp.mean(x * x, axis=-1, keepdims=True) + EPS)
        g = dyv * w_ref[...]
        dx = r * (g - x * (r * r) * jnp.mean(g * x, axis=-1, keepdims=True)) + dres_ref[0]
        dw_ref[...] += jnp.sum(dyv * x * r, axis=0, keepdims=True)

        @pl.when(j == 0)
        def _():
            head_ref[0] = dx

        gx_ref[0] = dx

    row = pl.BlockSpec((1, CHUNK, d), lambda b, j: (b, j, 0))
    return pl.pallas_call(
        body, name=name, grid=(bsz, nc),
        in_specs=[row, row, pl.BlockSpec((1, d), lambda b, j: (0, 0)), row],
        out_specs=[pl.BlockSpec((1, CHUNK, d), lambda b, j: (b, jnp.maximum(j - 1, 0), 0)),
                   pl.BlockSpec((1, CHUNK, d), lambda b, j: (b, 0, 0)), pl.BlockSpec((1, d), lambda b, j: (0, 0))],
        out_shape=[jax.ShapeDtypeStruct((bsz, seq, d), F32), jax.ShapeDtypeStruct((bsz, CHUNK, d), F32),
                   jax.ShapeDtypeStruct((1, d), F32)],
        compiler_params=_params("arbitrary", "arbitrary"),
    )(dhn, h0, w, dres)


def _remote(src, dst, send_sem, recv_sem, dev):
    return pltpu.make_async_remote_copy(src_ref=src, dst_ref=dst, send_sem=send_sem, recv_sem=recv_sem,
                                        device_id=dev, device_id_type=MESH)


def _position():
    return lax.axis_index("x"), lax.axis_index("y"), lax.axis_index("c")


def _other_chips(pos):
    x, y, _ = pos
    return [(1 - x, y), (x, 1 - y), (1 - x, 1 - y)]


class _Gather:
    def __init__(self, arrs):
        n = len(arrs)
        self.args, self.n_in, self.n_out = list(arrs), n, n
        self.out_shape = [jax.ShapeDtypeStruct((4,) + a.shape, a.dtype) for a in arrs]
        self.scratch = [pltpu.SemaphoreType.DMA((3 * n,)), pltpu.SemaphoreType.DMA((3 * n,)),
                        pltpu.SemaphoreType.DMA((n,))]

    def _copies(self, pos, ins, outs, sems):
        send_sems, recv_sems, loc_sems = sems
        x, y, c = pos
        me = 2 * x + y
        local = [pltpu.make_async_copy(ins[i], outs[i].at[me], loc_sems.at[i]) for i in range(self.n_in)]
        sends, recvs = [], []
        for i in range(self.n_in):
            for k, (px, py) in enumerate(_other_chips(pos)):
                sems_k = (send_sems.at[3 * i + k], recv_sems.at[3 * i + k], (px, py, c))
                sends.append(_remote(ins[i], outs[i].at[me], *sems_k))
                recvs.append(_remote(ins[i], outs[i].at[2 * px + py], *sems_k))
        return local, sends, recvs

    def start(self, pos, ins, outs, sems):
        local, sends, _ = self._copies(pos, ins, outs, sems)
        for cp in local + sends:
            cp.start()

    def finish(self, pos, ins, outs, sems):
        local, sends, recvs = self._copies(pos, ins, outs, sems)
        for cp in recvs:
            cp.wait_recv()
        for cp in sends:
            cp.wait_send()
        for cp in local:
            cp.wait()


class _Exchange:
    FLIPS = [(fx, fy, fc) for fx in (0, 1) for fy in (0, 1) for fc in (0, 1)][1:]

    def __init__(self, big, small=None):
        n = len(big)
        self.n_big, self.has_small = n, small is not None
        self.args = list(big) + ([small] if self.has_small else [])
        self.n_in = self.n_out = len(self.args)
        self.out_shape = [jax.ShapeDtypeStruct(a.shape, a.dtype) for a in big]
        self.scratch = [pltpu.SemaphoreType.DMA((max(3 * n, 1),)), pltpu.SemaphoreType.DMA((max(3 * n, 1),))]
        if self.has_small:
            self.out_shape.append(jax.ShapeDtypeStruct((8,) + small.shape, small.dtype))
            self.scratch += [pltpu.SemaphoreType.DMA((7,)), pltpu.SemaphoreType.DMA((7,)), pltpu.SemaphoreType.DMA((1,))]

    def _copies(self, pos, ins, outs, sems):
        x, y, c = pos
        me, me8 = 2 * x + y, 4 * x + 2 * y + c
        local, sends, recvs = [], [], []
        for i in range(self.n_big):
            for k, (px, py) in enumerate(_other_chips(pos)):
                sems_k = (sems[0].at[3 * i + k], sems[1].at[3 * i + k], (px, py, c))
                sends.append(_remote(ins[i].at[2 * px + py], outs[i].at[me], *sems_k))
                recvs.append(_remote(ins[i].at[me], outs[i].at[2 * px + py], *sems_k))
        if self.has_small:
            small, landed = ins[self.n_big], outs[self.n_big]
            local.append(pltpu.make_async_copy(small, landed.at[me8], sems[4].at[0]))
            for k, (fx, fy, fc) in enumerate(self.FLIPS):
                peer = (x ^ fx, y ^ fy, c ^ fc)
                sems_k = (sems[2].at[k], sems[3].at[k], peer)
                sends.append(_remote(small, landed.at[me8], *sems_k))
                recvs.append(_remote(small, landed.at[4 * peer[0] + 2 * peer[1] + peer[2]], *sems_k))
        return local, sends, recvs

    start = _Gather.start
    finish = _Gather.finish


class _Swap:
    def __init__(self, arrs):
        n = len(arrs)
        self.args, self.n_in, self.n_out = list(arrs), n, n
        self.out_shape = [jax.ShapeDtypeStruct(a.shape, a.dtype) for a in arrs]
        self.scratch = [pltpu.SemaphoreType.DMA((n,)), pltpu.SemaphoreType.DMA((n,))]

    def _copies(self, pos, ins, outs, sems):
        x, y, c = pos
        both = [_remote(ins[i], outs[i], sems[0].at[i], sems[1].at[i], (x, y, 1 - c)) for i in range(self.n_in)]
        return [], both, both

    start = _Gather.start
    finish = _Gather.finish


def _comm(rider, *, name):
    a, b = rider.n_in, rider.n_in + rider.n_out

    def body(*refs):
        pos = _position()
        rider.start(pos, refs[:a], refs[a:b], refs[b:])
        rider.finish(pos, refs[:a], refs[a:b], refs[b:])

    return pl.pallas_call(body, name=name, in_specs=[ANY] * rider.n_in, out_specs=[ANY] * rider.n_out,
                          out_shape=rider.out_shape, scratch_shapes=rider.scratch)(*rider.args)


class _Ride:
    def __init__(self, rider, body, n_in, n_out, n_scratch, grid):
        self.rider = rider
        self.args = rider.args if rider else []
        self.in_specs = [ANY] * rider.n_in if rider else []
        self.out_specs = [ANY] * rider.n_out if rider else []
        self.out_shape = rider.out_shape if rider else []
        self.scratch = rider.scratch if rider else []
        self.body = self._wrap(body, n_in, n_out, n_scratch, grid) if rider else body

    def semantics(self, sem):
        return ("arbitrary",) * len(sem) if self.rider else sem

    def _wrap(self, body, n_in, n_out, n_scratch, grid):
        rider = self.rider
        a = n_in
        b = a + rider.n_in
        c = b + n_out
        d = c + rider.n_out
        e = d + n_scratch

        def wrapped(*refs):
            pos = _position()
            ids = [pl.program_id(i) for i in range(len(grid))]
            first = functools.reduce(jnp.logical_and, [i == 0 for i in ids])
            last = functools.reduce(jnp.logical_and, [i == g - 1 for i, g in zip(ids, grid)])

            @pl.when(first)
            def _():
                rider.start(pos, refs[a:b], refs[c:d], refs[e:])

            body(*refs[:a], *refs[b:c], *refs[d:e])

            @pl.when(last)
            def _():
                rider.finish(pos, refs[a:b], refs[c:d], refs[e:])

        return wrapped


def _chip_sum(own, landed, *, name):
    r, c = own.shape
    tm = _pick(r, (256, 128, 64, 8))

    def body(own_ref, land_ref, o_ref):
        me = 2 * lax.axis_index("x") + lax.axis_index("y")
        acc = None
        for jchip in range(4):
            term = jnp.where(me == jchip, own_ref[...], land_ref[jchip].astype(F32))
            acc = term if acc is None else acc + term
        o_ref[...] = acc

    return pl.pallas_call(
        body, name=name, grid=(r // tm,),
        in_specs=[pl.BlockSpec((tm, c), lambda i: (i, 0)), pl.BlockSpec((4, tm, c), lambda i: (0, i, 0))],
        out_specs=pl.BlockSpec((tm, c), lambda i: (i, 0)), out_shape=jax.ShapeDtypeStruct((r, c), F32),
        compiler_params=_params("parallel"),
    )(own, landed)


def _device_sum(parts, *, name):
    _, r, c = parts.shape

    def body(p_ref, o_ref):
        acc = p_ref[0]
        for d in range(1, 8):
            acc = acc + p_ref[d]
        o_ref[...] = acc

    return pl.pallas_call(body, name=name, out_shape=jax.ShapeDtypeStruct((r, c), F32))(parts)


def _adamw_math(w, g, m, v):
    m = ADAM_B1 * m + (1.0 - ADAM_B1) * g
    v = ADAM_B2 * v + (1.0 - ADAM_B2) * (g * g)
    m_hat = m / (1.0 - ADAM_B1 ** ADAM_STEP)
    v_hat = v / (1.0 - ADAM_B2 ** ADAM_STEP)
    return -ADAM_LR * (m_hat / (jnp.sqrt(v_hat) + ADAM_EPS) + ADAM_WD * w), m, v


def _adamw(w, g_parts, m, v, *, name):
    r, c = w.shape
    tm = _pick(r, (256, 128, 64, 16, 8, 4, 1)) if r * c > 65536 else r
    n_g = len(g_parts)

    def body(*refs):
        w_ref, m_ref, v_ref = refs[n_g:n_g + 3]
        g_ref, d_ref, nm_ref, nv_ref = refs[n_g + 3:]
        g = refs[0][...]
        for p in refs[1:n_g]:
            g = g + p[...]
        g_ref[...] = g
        d_ref[...], nm_ref[...], nv_ref[...] = _adamw_math(w_ref[...], g, m_ref[...], v_ref[...])

    blk = pl.BlockSpec((tm, c), lambda i: (i, 0))
    return pl.pallas_call(
        body, name=name, grid=(r // tm,), in_specs=[blk] * (n_g + 3), out_specs=[blk] * 4,
        out_shape=[jax.ShapeDtypeStruct((r, c), F32)] * 4, compiler_params=_params("parallel"),
    )(*g_parts, w, m, v)


def _pad_heads(v):
    return jnp.pad(v.reshape(N_GROUPS, 1, HPG), ((0, 0), (0, 0), (0, LANES - HPG)))


def _unpad_heads(v):
    return v[:, :HPG].reshape(1, N_HEADS)


_SMALL = [("norm_mix_w", (1, 1024)), ("pool_w", (512, 128)), ("pool_scale", (1, 512)), ("conv_w", (4, D_XBC)),
          ("conv_b", (1, D_XBC)), ("dt_bias", (1, N_HEADS)), ("a_log", (1, N_HEADS)), ("d_skip", (1, N_HEADS)),
          ("ssm_norm_w", (1, D_SSM)), ("norm_ffn_w", (1, 1024)), ("norm_f_w", (1, 1024)), ("meta", (N_META, 1024))]


def _pack_small(grads):
    rows = []
    for nm, shape in _SMALL:
        flat = grads[nm].reshape(-1)
        rows.append(jnp.pad(flat, (0, (-flat.size) % LANES)).reshape(-1, LANES))
    packed = jnp.concatenate(rows, axis=0)
    return jnp.pad(packed, ((0, (-packed.shape[0]) % 8), (0, 0)))


def _unpack_small(packed):
    out, r0 = {}, 0
    for nm, shape in _SMALL:
        size = shape[0] * shape[1]
        nrow = -(-size // LANES)
        out[nm] = packed[r0:r0 + nrow].reshape(-1)[:size].reshape(shape)
        r0 += nrow
    return out


def kernel(x, meta, norm_mix_w, w_in, pool_w, pool_scale, conv_w, conv_b, dt_bias, a_log, d_skip, ssm_norm_w, w_out, norm_ffn_w, w_ff1, w_ff2, norm_f_w, loss_target, m_meta, m_norm_mix_w, m_w_in, m_pool_w, m_pool_scale, m_conv_w, m_conv_b, m_dt_bias, m_a_log, m_d_skip, m_ssm_norm_w, m_w_out, m_norm_ffn_w, m_w_ff1, m_w_ff2, m_norm_f_w, v_meta, v_norm_mix_w, v_w_in, v_pool_w, v_pool_scale, v_conv_w, v_conv_b, v_dt_bias, v_a_log, v_d_skip, v_ssm_norm_w, v_w_out, v_norm_ffn_w, v_w_ff1, v_w_ff2, v_norm_f_w):
    bsz, seq, d = x.shape
    t = seq + CHUNK
    n = bsz * t
    chip = 2 * lax.axis_index("x") + lax.axis_index("y")
    d_in = w_in.shape[2] * 4

    g_in, g_conv, g_meta = _comm(_Gather([w_in[0].astype(BF16), conv_w[0], meta]), name="gather_in")
    late_weights = _Gather([w_out[0].astype(BF16), w_ff1[0].astype(BF16), w_ff2[0].astype(BF16)])
    win = g_in.transpose(1, 0, 2).reshape(d, d_in)
    wu, wz = win[:, :D_POOL], win[:, D_POOL:D_POOL + D_SSM]
    wx = win[:, D_POOL + D_SSM:D_POOL + D_SSM + D_XBC]
    wdt = jnp.pad(win[:, D_POOL + D_SSM + D_XBC:].reshape(d, N_GROUPS, HPG),
                  ((0, 0), (0, 0), (0, LANES - HPG))).reshape(d, D_DT)
    convw = g_conv.transpose(1, 0, 2).reshape(CONV_W, D_XBC)
    meta_full = g_meta.transpose(1, 0, 2).reshape(N_META, d)
    dtb, alog = _pad_heads(dt_bias), _pad_heads(a_log)
    dskip = jnp.repeat(d_skip, HEAD_DIM, axis=1)
    poolw = pool_w[0]

    h0 = jnp.concatenate([jnp.zeros((bsz, PAD, d), F32), jnp.broadcast_to(meta_full[None], (bsz, N_META, d)), x], axis=1)
    h0f = h0.reshape(n, d)
    tm = _pick(n, (768, 512, 256, 128))
    tm_wide = _pick(n, (384, 256, 128))
    hn1 = _rms_fwd(h0f, norm_mix_w, name="norm_mix")
    u = _mm(hn1, wu, name="proj_u", tm=tm)
    z = _mm(hn1, wz, name="proj_z", tm=tm)
    xbc = _mm(hn1, wx, name="proj_xbc", tm=tm)
    dtr = _mm(hn1, wdt, name="proj_dt", tm=tm)
    ypool = _pool_fwd(u.reshape(bsz, t, D_POOL), poolw, pool_scale, name="pool_fwd")
    xbc3 = xbc.reshape(bsz, t, D_XBC)
    xc = _conv_fwd(xbc3, convw, conv_b, name="conv_fwd")
    z3, dtr3 = z.reshape(bsz, t, D_SSM), dtr.reshape(bsz, t, D_DT)
    (yn, ypre, sprev), (g_out, g_ff1, g_ff2) = _ssd_fwd(xc, dtr3, z3, dtb, alog, dskip, ssm_norm_w, name="ssd_fwd",
                                                        rider=late_weights)
    wo = g_out.reshape(D_POOL + D_SSM, d)
    wo_p, wo_s = wo[:D_POOL], wo[D_POOL:]
    w1 = g_ff1.transpose(1, 0, 2).reshape(d, D_FF)
    w2 = g_ff2.reshape(D_FF, d)
    ypool_f, yn_f = ypool.reshape(n, D_POOL), yn.reshape(n, D_SSM)
    add = lambda r, e: r + e
    h1 = _mm(ypool_f, wo_p, name="out_pool", tm=tm, post=add, extras=(h0f,))
    h1 = _mm(yn_f, wo_s, name="out_ssm", tm=tm, post=add, extras=(h1,))
    hn2 = _rms_fwd(h1, norm_ffn_w, name="norm_ffn")
    act = _mm(hn2, w1, name="ff1", tm=tm)
    relu2 = lambda a: jnp.square(jnp.maximum(a, 0.0))
    h2 = _mm(act, w2, name="ff2", tm=tm_wide, pre=relu2, post=add, extras=(h1,))
    dh2, loss_acc, d_norm_f = _final_norm_loss(h2.reshape(bsz, t, d), loss_target, norm_f_w.reshape(1, d), name="loss")
    loss = lax.psum(loss_acc[0, 0], ("x", "y", "c"))

    dh2f = dh2.reshape(n, d)
    dact = _mm(dh2f, w2, name="ff2_bwd", tm=tm, nt=True, post=lambda r, a: r * (2.0 * jnp.maximum(a, 0.0)),
               extras=(act,), out_dtype=BF16)
    d_w2 = _mm_tn(act, dh2f, name="ff2_dw", tk=1024, tn=1024, tm=tm_wide, pre=relu2)
    d_w1 = _mm_tn(hn2, dact, name="ff1_dw", tk=1024, tn=1024, tm=tm)
    dhn2 = _mm(dact, w1, name="ff1_bwd", tm=tm_wide, nt=True)
    dh1, d_norm_ffn = _rms_bwd(dhn2, h1, norm_ffn_w, dh2f, name="norm_ffn_bwd")
    dypool = _mm(dh1, wo_p, name="out_pool_bwd", tm=tm, nt=True)
    dyn = _mm(dh1, wo_s, name="out_ssm_bwd", tm=tm, nt=True)
    d_wo_p = _mm_tn(ypool_f, dh1, name="out_pool_dw", tk=512, tn=1024, tm=tm)
    d_wo_s = _mm_tn(yn_f, dh1, name="out_ssm_dw", tk=512, tn=1024, tm=tm)
    big_late = [jnp.concatenate([d_wo_p, d_wo_s], axis=0).reshape(4, (D_POOL + D_SSM) // 4, d),
                d_w1.reshape(d, 4, D_FF // 4).transpose(1, 0, 2), d_w2.reshape(4, D_FF // 4, d)]
    (dz, dxs, dbm, dcm, ddtr, d_nw, d_heads), landed_late = _ssd_bwd(
        xc, dtr3, z3, ypre, sprev, dyn.reshape(bsz, t, D_SSM), dtb, alog, dskip, ssm_norm_w, name="ssd_bwd",
        rider=_Exchange([b.astype(BF16) for b in big_late]))
    dpre, d_convwb = _conv_bwd_pre(xbc3, dxs, dbm, dcm, convw, conv_b, name="conv_bwd_pre")
    dxbc = _conv_bwd_in(dpre, convw, name="conv_bwd_in")
    du, d_poolw, d_poolsc = _pool_bwd(u.reshape(bsz, t, D_POOL), dypool.reshape(bsz, t, D_POOL), poolw, pool_scale,
                                      name="pool_bwd")
    duf, dzf, dxbcf, ddtrf = du.reshape(n, D_POOL), dz.reshape(n, D_SSM), dxbc.reshape(n, D_XBC), ddtr.reshape(n, D_DT)
    d_wu = _mm_tn(hn1, duf, name="proj_u_dw", tk=1024, tn=512, tm=tm)
    d_wz = _mm_tn(hn1, dzf, name="proj_z_dw", tk=1024, tn=512, tm=tm)
    d_wx = _mm_tn(hn1, dxbcf, name="proj_xbc_dw", tk=1024, tn=512, tm=tm)
    d_wdt = _mm_tn(hn1, ddtrf, name="proj_dt_dw", tk=1024, tn=512, tm=tm)
    d_win = jnp.concatenate([d_wu, d_wz, d_wx, d_wdt.reshape(d, N_GROUPS, LANES)[:, :, :HPG].reshape(d, N_HEADS)], axis=1)
    big_in = d_win.reshape(d, 4, d_in // 4).transpose(1, 0, 2)
    dhn1 = _mm(duf, wu, name="proj_u_bwd", tm=tm, nt=True)
    dhn1 = _mm(dzf, wz, name="proj_z_bwd", tm=tm, nt=True, post=add, extras=(dhn1,))
    dhn1, landed_in = _mm(dxbcf, wx, name="proj_xbc_bwd", tm=tm_wide, nt=True, post=add, extras=(dhn1,),
                          rider=_Exchange([big_in.astype(BF16)]))
    dhn1 = _mm(ddtrf, wdt, name="proj_dt_bwd", tm=tm, nt=True, post=add, extras=(dhn1,))
    grad_x, d_head_rows, d_norm_mix = _input_grad(
        dhn1.reshape(bsz, t, d), h0, norm_mix_w, dh1.reshape(bsz, t, d), seq, name="input_grad")

    big = [big_in] + big_late
    landed = list(landed_in) + list(landed_late)
    heads = jnp.sum(d_heads, axis=0)
    small = _pack_small({
        "norm_mix_w": d_norm_mix, "pool_w": jnp.sum(d_poolw, axis=0), "pool_scale": jnp.sum(d_poolsc, axis=0),
        "conv_w": jnp.sum(d_convwb[:, :CONV_W], axis=0), "conv_b": jnp.sum(d_convwb[:, CONV_W:CONV_W + 1], axis=0),
        "dt_bias": _unpad_heads(heads[:, 2]), "a_log": _unpad_heads(heads[:, 1]), "d_skip": _unpad_heads(heads[:, 0]),
        "ssm_norm_w": jnp.sum(d_nw, axis=0), "norm_ffn_w": d_norm_ffn, "norm_f_w": d_norm_f,
        "meta": jnp.sum(d_head_rows[:, PAD:], axis=0)})
    (small_all,) = _comm(_Exchange([], small), name="exchange_small")
    own = [lax.dynamic_index_in_dim(b, chip, 0, keepdims=False) for b in big]
    mine = [_chip_sum(o, l, name=f"chip_sum_{i}") for i, (o, l) in enumerate(zip(own, landed))]
    theirs = _comm(_Swap(mine), name="swap_cores")
    gsmall = _unpack_small(_device_sum(small_all, name="device_sum"))
    gsmall["conv_w"] = lax.dynamic_slice_in_dim(gsmall["conv_w"], chip * (D_XBC // 4), D_XBC // 4, axis=1)
    gsmall["meta"] = lax.dynamic_slice_in_dim(gsmall["meta"], chip * (d // 4), d // 4, axis=1)

    given = dict(meta=(meta, m_meta, v_meta), norm_mix_w=(norm_mix_w, m_norm_mix_w, v_norm_mix_w),
                 w_in=(w_in, m_w_in, v_w_in), pool_w=(pool_w, m_pool_w, v_pool_w),
                 pool_scale=(pool_scale, m_pool_scale, v_pool_scale), conv_w=(conv_w, m_conv_w, v_conv_w),
                 conv_b=(conv_b, m_conv_b, v_conv_b), dt_bias=(dt_bias, m_dt_bias, v_dt_bias),
                 a_log=(a_log, m_a_log, v_a_log), d_skip=(d_skip, m_d_skip, v_d_skip),
                 ssm_norm_w=(ssm_norm_w, m_ssm_norm_w, v_ssm_norm_w), w_out=(w_out, m_w_out, v_w_out),
                 norm_ffn_w=(norm_ffn_w, m_norm_ffn_w, v_norm_ffn_w), w_ff1=(w_ff1, m_w_ff1, v_w_ff1),
                 w_ff2=(w_ff2, m_w_ff2, v_w_ff2), norm_f_w=(norm_f_w, m_norm_f_w, v_norm_f_w))
    big_names = ["w_in", "w_out", "w_ff1", "w_ff2"]
    results = {}
    for nm, (w, m, v) in given.items():
        if nm in big_names:
            i = big_names.index(nm)
            parts, shape2 = (mine[i], theirs[i]), mine[i].shape
        else:
            parts, shape2 = (gsmall[nm],), gsmall[nm].shape
        outs = _adamw(w.reshape(shape2), parts, m.reshape(shape2), v.reshape(shape2), name=f"adamw_{nm}")
        results[nm] = [o.reshape(w.shape) for o in outs]
    order = list(given)
    return (loss, grad_x, *[results[nm][0] for nm in order], *[results[nm][1] for nm in order],
            *[results[nm][2] for nm in order], *[results[nm][3] for nm in order])
```

```python
import functools

import jax
import jax.numpy as jnp
from jax import lax
from jax.experimental import pallas as pl
from jax.experimental.pallas import tpu as pltpu

F32 = jnp.float32
BF16 = jnp.bfloat16
MESH = pl.DeviceIdType.MESH
ANY = pl.BlockSpec(memory_space=pl.ANY)

D_MODEL = 1024
N_META = 16
CHUNK = 128
PAD = CHUNK - N_META
POOL_WINDOWS = (2, 4, 8, 16)
D_POOL = 512
POOL_GROUP = 128
D_SSM = 1536
N_HEADS = 24
N_GROUPS = 4
HPG = 6
HEAD_DIM = 64
D_STATE = 128
GW = HPG * HEAD_DIM
D_XBC = D_SSM + 2 * N_GROUPS * D_STATE
D_DT = N_GROUPS * 128
D_FF = 4096
CONV_W = 4
EPS = 1e-5
LANES = 128
VMEM_LIMIT = 56 * 1024 * 1024

ADAM_LR, ADAM_B1, ADAM_B2, ADAM_EPS, ADAM_WD, ADAM_STEP = 0.001, 0.9, 0.999, 1e-08, 0.01, 10


def _params(*sem):
    return pltpu.CompilerParams(dimension_semantics=sem, vmem_limit_bytes=VMEM_LIMIT)


def _pick(n, cands):
    for c in cands:
        if n % c == 0:
            return c
    raise ValueError(f"no block size for {n}")


def _dot(a, b):
    return jnp.dot(a.astype(BF16), b.astype(BF16), preferred_element_type=F32)


def _dot_nt(a, b):
    return lax.dot_general(a.astype(BF16), b.astype(BF16), (((1,), (1,)), ((), ())), preferred_element_type=F32)


def _dot_tn(a, b):
    return lax.dot_general(a.astype(BF16), b.astype(BF16), (((0,), (0,)), ((), ())), preferred_element_type=F32)


def _dot_exact(mask, x):
    m = mask.astype(BF16)
    hi = x.astype(BF16)
    r1 = x - hi.astype(F32)
    mid = r1.astype(BF16)
    lo = (r1 - mid.astype(F32)).astype(BF16)
    dot = lambda t: jnp.dot(m, t, preferred_element_type=F32)
    return dot(hi) + dot(mid) + dot(lo)


def _sigmoid(x):
    return 1.0 / (1.0 + jnp.exp(-x))


def _softplus(x):
    return jnp.maximum(x, 0.0) + jnp.log1p(jnp.exp(-jnp.abs(x)))


def _sum_all(x):
    return jnp.sum(jnp.sum(x, axis=1, keepdims=True), axis=0, keepdims=True)


ROW_TILES = (4224, 2816, 2112, 1408, 1056, 768, 704, 512, 384, 256, 128)
TILE_BUDGET = 28 * 1024 * 1024


def _row_tile(n, bytes_per_row, fixed_bytes):
    for tm in ROW_TILES:
        if n % tm == 0 and 2 * (tm * bytes_per_row + fixed_bytes) <= TILE_BUDGET:
            return tm
    raise ValueError(f"no row tile for {n}")


def _mm(a, w, *, name, tn=512, nt=False, pre=None, post=None, extras=(), out_dtype=F32, rider=None):
    a_list = list(a) if isinstance(a, (list, tuple)) else [a]
    w_list = list(w) if isinstance(w, (list, tuple)) else [w]
    n_a, n_ex = len(a_list), len(extras)
    n = a_list[0].shape[0]
    m = w_list[0].shape[0] if nt else w_list[0].shape[1]
    tn = min(tn, m)
    size = lambda dt: jnp.dtype(dt).itemsize
    per_row = (sum(x.shape[1] * size(x.dtype) for x in a_list) + tn * size(out_dtype)
               + sum(tn * size(e.dtype) for e in extras))
    tm = _row_tile(n, per_row, sum(x.shape[1 if nt else 0] * tn * size(x.dtype) for x in w_list))

    def body(*refs):
        a_refs, w_refs, ex_refs, o_ref = refs[:n_a], refs[n_a:2 * n_a], refs[2 * n_a:2 * n_a + n_ex], refs[2 * n_a + n_ex]
        r = None
        for a_ref, w_ref in zip(a_refs, w_refs):
            av = a_ref[...]
            if pre is not None:
                av = pre(av)
            term = _dot_nt(av, w_ref[...]) if nt else _dot(av, w_ref[...])
            r = term if r is None else r + term
        if post is not None:
            r = post(r, *[e[...] for e in ex_refs])
        o_ref[...] = r.astype(out_dtype)

    a_specs = [pl.BlockSpec((tm, x.shape[1]), lambda i, j: (i, 0)) for x in a_list]
    w_specs = [pl.BlockSpec((tn, x.shape[1]), lambda i, j: (j, 0)) if nt else pl.BlockSpec((x.shape[0], tn), lambda i, j: (0, j))
               for x in w_list]
    blk = pl.BlockSpec((tm, tn), lambda i, j: (i, j))
    grid = (n // tm, m // tn)
    ride = _Ride(rider, body, 2 * n_a + n_ex, 1, 0, grid)
    outs = pl.pallas_call(
        ride.body, name=name, grid=grid,
        in_specs=a_specs + w_specs + [blk] * n_ex + ride.in_specs,
        out_specs=[blk] + ride.out_specs, out_shape=[jax.ShapeDtypeStruct((n, m), out_dtype)] + ride.out_shape,
        scratch_shapes=ride.scratch, compiler_params=_params(*ride.semantics(("parallel", "parallel"))),
    )(*a_list, *w_list, *extras, *ride.args)
    return (outs[0], outs[1:]) if rider else outs[0]


def _mm_tn(a, g, *, name, tk, tn, pre=None):
    n, k = a.shape
    m = g.shape[1]
    tk, tn = min(tk, k), min(tn, m)
    tm = _row_tile(n, tk * jnp.dtype(a.dtype).itemsize + tn * jnp.dtype(g.dtype).itemsize, tk * tn * 4)

    def body(a_ref, g_ref, o_ref):
        @pl.when(pl.program_id(2) == 0)
        def _():
            o_ref[...] = jnp.zeros_like(o_ref)

        av = a_ref[...]
        if pre is not None:
            av = pre(av)
        o_ref[...] += _dot_tn(av, g_ref[...])

    return pl.pallas_call(
        body, name=name, grid=(k // tk, m // tn, n // tm),
        in_specs=[pl.BlockSpec((tm, tk), lambda i, j, r: (r, i)), pl.BlockSpec((tm, tn), lambda i, j, r: (r, j))],
        out_specs=pl.BlockSpec((tk, tn), lambda i, j, r: (i, j)),
        out_shape=jax.ShapeDtypeStruct((k, m), F32),
        compiler_params=_params("parallel", "parallel", "arbitrary"),
    )(a, g)


def _rms_fwd(h, w, *, name):
    n, d = h.shape
    tm = _pick(n, (768, 512, 256, 128))

    def body(h_ref, w_ref, o_ref):
        x = h_ref[...]
        r = lax.rsqrt(jnp.mean(x * x, axis=-1, keepdims=True) + EPS)
        o_ref[...] = (x * r * w_ref[...]).astype(BF16)

    return pl.pallas_call(
        body, name=name, grid=(n // tm,),
        in_specs=[pl.BlockSpec((tm, d), lambda i: (i, 0)), pl.BlockSpec((1, d), lambda i: (0, 0))],
        out_specs=pl.BlockSpec((tm, d), lambda i: (i, 0)), out_shape=jax.ShapeDtypeStruct((n, d), BF16),
        compiler_params=_params("parallel"),
    )(h, w)


def _rms_bwd(dy, h, w, dres, *, name):
    n, d = h.shape
    tm = _pick(n, (768, 512, 256, 128))

    def body(dy_ref, h_ref, w_ref, dres_ref, dx_ref, dxb_ref, dw_ref):
        @pl.when(pl.program_id(0) == 0)
        def _():
            dw_ref[...] = jnp.zeros_like(dw_ref)

        x, dyv = h_ref[...], dy_ref[...]
        r = lax.rsqrt(jnp.mean(x * x, axis=-1, keepdims=True) + EPS)
        g = dyv * w_ref[...]
        dx = r * (g - x * (r * r) * jnp.mean(g * x, axis=-1, keepdims=True)) + dres_ref[...]
        dx_ref[...] = dx
        dxb_ref[...] = dx.astype(BF16)
        dw_ref[...] += jnp.sum(dyv * x * r, axis=0, keepdims=True)

    row = pl.BlockSpec((tm, d), lambda i: (i, 0))
    vec = pl.BlockSpec((1, d), lambda i: (0, 0))
    return pl.pallas_call(
        body, name=name, grid=(n // tm,), in_specs=[row, row, vec, row], out_specs=[row, row, vec],
        out_shape=[jax.ShapeDtypeStruct((n, d), F32), jax.ShapeDtypeStruct((n, d), BF16), jax.ShapeDtypeStruct((1, d), F32)],
        compiler_params=_params("arbitrary"),
    )(dy, h, w, dres)


def _final_norm_loss(h2, target, w, *, name):
    bsz, t, d = h2.shape
    nc = t // CHUNK

    def body(h_ref, t_ref, w_ref, dh_ref, dhb_ref, loss_ref, dw_ref):
        j = pl.program_id(1)

        @pl.when((pl.program_id(0) == 0) & (j == 0))
        def _():
            loss_ref[...] = jnp.zeros_like(loss_ref)
            dw_ref[...] = jnp.zeros_like(dw_ref)

        x, wv = h_ref[0], w_ref[...]
        r = lax.rsqrt(jnp.mean(x * x, axis=-1, keepdims=True) + EPS)
        diff = jnp.where(j > 0, x * r * wv - t_ref[0], 0.0)
        loss_ref[...] += _sum_all(diff * diff) * (0.5 / d)
        dy = diff * (1.0 / d)
        g = dy * wv
        dh = r * (g - x * (r * r) * jnp.mean(g * x, axis=-1, keepdims=True))
        dh_ref[0] = dh
        dhb_ref[0] = dh.astype(BF16)
        dw_ref[...] += jnp.sum(dy * x * r, axis=0, keepdims=True)

    row = pl.BlockSpec((1, CHUNK, d), lambda b, j: (b, j, 0))
    return pl.pallas_call(
        body, name=name, grid=(bsz, nc),
        in_specs=[row, pl.BlockSpec((1, CHUNK, d), lambda b, j: (b, jnp.maximum(j - 1, 0), 0)),
                  pl.BlockSpec((1, d), lambda b, j: (0, 0))],
        out_specs=[row, row, pl.BlockSpec((8, LANES), lambda b, j: (0, 0)), pl.BlockSpec((1, d), lambda b, j: (0, 0))],
        out_shape=[jax.ShapeDtypeStruct((bsz, t, d), F32), jax.ShapeDtypeStruct((bsz, t, d), BF16),
                   jax.ShapeDtypeStruct((8, LANES), F32), jax.ShapeDtypeStruct((1, d), F32)],
        compiler_params=_params("arbitrary", "arbitrary"),
    )(h2, target, w)


def _pool_masks(j, transposed):
    r = lax.broadcasted_iota(jnp.int32, (CHUNK, 2 * CHUNK), 0)
    c = lax.broadcasted_iota(jnp.int32, (CHUNK, 2 * CHUNK), 1)
    masks = []
    for w in POOL_WINDOWS:
        if transposed:
            m = (c >= r) & (c < r + w)
        else:
            s = c - CHUNK
            m = (s <= r) & (s > r - w) & (s + j * CHUNK >= 0)
        masks.append(m.astype(F32))
    return masks


def _pool_count(t_global, w):
    return jnp.clip(t_global - PAD + 1, 1, w).astype(F32)


def _pool_fwd(u, pool_w, pool_scale, *, name):
    bsz, t, _ = u.shape
    nc = t // CHUNK

    def body(prev_ref, cur_ref, pw_ref, sc_ref, o_ref):
        j = pl.program_id(1)
        masks = _pool_masks(j, False)
        tg = j * CHUNK + lax.broadcasted_iota(jnp.int32, (CHUNK, 1), 0)
        for gi, w in enumerate(POOL_WINDOWS):
            sl = pl.ds(gi * POOL_GROUP, POOL_GROUP)
            cur = cur_ref[0, :, sl]
            both = jnp.concatenate([prev_ref[0, :, sl], cur], axis=0)
            pooled = _dot_exact(masks[gi], both) / _pool_count(tg, w) - cur
            o_ref[0, :, sl] = (_dot(pooled, pw_ref[gi]) * sc_ref[:, sl]).astype(BF16)

    blk = lambda f: pl.BlockSpec((1, CHUNK, D_POOL), f)
    return pl.pallas_call(
        body, name=name, grid=(bsz, nc),
        in_specs=[blk(lambda b, j: (b, jnp.maximum(j - 1, 0), 0)), blk(lambda b, j: (b, j, 0)),
                  pl.BlockSpec((4, POOL_GROUP, POOL_GROUP), lambda b, j: (0, 0, 0)),
                  pl.BlockSpec((1, D_POOL), lambda b, j: (0, 0))],
        out_specs=blk(lambda b, j: (b, j, 0)), out_shape=jax.ShapeDtypeStruct(u.shape, BF16),
        compiler_params=_params("parallel", "parallel"),
    )(u, u, pool_w, pool_scale)


def _pool_bwd(u, dyp, pool_w, pool_scale, *, name):
    bsz, t, _ = u.shape
    nc = t // CHUNK

    def body(prev_ref, cur_ref, dy_ref, dyn_ref, pw_ref, sc_ref, du_ref, dpw_ref, dsc_ref):
        j = pl.program_id(1)

        @pl.when(j == 0)
        def _():
            dpw_ref[...] = jnp.zeros_like(dpw_ref)
            dsc_ref[...] = jnp.zeros_like(dsc_ref)

        fwd = _pool_masks(j, False)
        bwd = _pool_masks(j, True)
        tg = j * CHUNK + lax.broadcasted_iota(jnp.int32, (CHUNK, 1), 0)
        has_next = j < nc - 1
        for gi, w in enumerate(POOL_WINDOWS):
            sl = pl.ds(gi * POOL_GROUP, POOL_GROUP)
            cur, pw, sc = cur_ref[0, :, sl], pw_ref[gi], sc_ref[:, sl]
            both = jnp.concatenate([prev_ref[0, :, sl], cur], axis=0)
            pooled = _dot_exact(fwd[gi], both) / _pool_count(tg, w) - cur
            dy = dy_ref[0, :, sl]
            dsc_ref[0, :, sl] += jnp.sum(dy * _dot(pooled, pw), axis=0, keepdims=True)
            dm = dy * sc
            dpw_ref[0, gi] += _dot_tn(pooled, dm)
            dpooled = _dot_nt(dm, pw)
            dpooled_next = _dot_nt(jnp.where(has_next, dyn_ref[0, :, sl], 0.0) * sc, pw)
            e = jnp.concatenate([dpooled / _pool_count(tg, w), dpooled_next / _pool_count(tg + CHUNK, w)], axis=0)
            du_ref[0, :, sl] = (_dot_exact(bwd[gi], e) - dpooled).astype(BF16)

    blk = lambda f: pl.BlockSpec((1, CHUNK, D_POOL), f)
    return pl.pallas_call(
        body, name=name, grid=(bsz, nc),
        in_specs=[blk(lambda b, j: (b, jnp.maximum(j - 1, 0), 0)), blk(lambda b, j: (b, j, 0)),
                  blk(lambda b, j: (b, j, 0)), blk(lambda b, j: (b, jnp.minimum(j + 1, nc - 1), 0)),
                  pl.BlockSpec((4, POOL_GROUP, POOL_GROUP), lambda b, j: (0, 0, 0)),
                  pl.BlockSpec((1, D_POOL), lambda b, j: (0, 0))],
        out_specs=[blk(lambda b, j: (b, j, 0)),
                   pl.BlockSpec((1, 4, POOL_GROUP, POOL_GROUP), lambda b, j: (b, 0, 0, 0)),
                   pl.BlockSpec((1, 1, D_POOL), lambda b, j: (b, 0, 0))],
        out_shape=[jax.ShapeDtypeStruct(u.shape, BF16), jax.ShapeDtypeStruct((bsz, 4, POOL_GROUP, POOL_GROUP), F32),
                   jax.ShapeDtypeStruct((bsz, 1, D_POOL), F32)],
        compiler_params=_params("parallel", "arbitrary"),
    )(u, u, dyp, dyp, pool_w, pool_scale)


def _conv_taps(buf_ref, w_ref):
    acc = None
    for k in range(CONV_W):
        term = w_ref[k:k + 1, :] * buf_ref[pl.ds(8 - (CONV_W - 1) + k, CHUNK), :]
        acc = term if acc is None else acc + term
    return acc


def _conv_fwd(xbc, conv_w, conv_b, *, name):
    bsz, t, c = xbc.shape
    nc = t // CHUNK

    def body(tail_ref, cur_ref, w_ref, b_ref, o_ref, buf_ref):
        j = pl.program_id(1)
        buf_ref[pl.ds(0, 8), :] = jnp.where(j > 0, tail_ref[0], 0.0)
        buf_ref[pl.ds(8, CHUNK), :] = cur_ref[0]
        pre = _conv_taps(buf_ref, w_ref) + b_ref[...]
        o_ref[0] = pre * _sigmoid(pre)

    return pl.pallas_call(
        body, name=name, grid=(bsz, nc),
        in_specs=[pl.BlockSpec((1, 8, c), lambda b, j: (b, jnp.maximum(j * (CHUNK // 8) - 1, 0), 0)),
                  pl.BlockSpec((1, CHUNK, c), lambda b, j: (b, j, 0)),
                  pl.BlockSpec((CONV_W, c), lambda b, j: (0, 0)), pl.BlockSpec((1, c), lambda b, j: (0, 0))],
        out_specs=pl.BlockSpec((1, CHUNK, c), lambda b, j: (b, j, 0)), out_shape=jax.ShapeDtypeStruct(xbc.shape, F32),
        scratch_shapes=[pltpu.VMEM((CHUNK + 8, c), F32)],
        compiler_params=_params("parallel", "parallel"),
    )(xbc, xbc, conv_w, conv_b)


def _conv_bwd_pre(xbc, dxs, db, dc, conv_w, conv_b, *, name):
    bsz, t, c = xbc.shape
    nc = t // CHUNK

    def body(tail_ref, cur_ref, dxs_ref, db_ref, dc_ref, w_ref, b_ref, dpre_ref, dwb_ref, buf_ref):
        j = pl.program_id(1)

        @pl.when(j == 0)
        def _():
            dwb_ref[...] = jnp.zeros_like(dwb_ref)

        buf_ref[pl.ds(0, 8), :] = jnp.where(j > 0, tail_ref[0], 0.0)
        buf_ref[pl.ds(8, CHUNK), :] = cur_ref[0]
        pre = _conv_taps(buf_ref, w_ref) + b_ref[...]
        s = _sigmoid(pre)
        dsilu = s * (1.0 + pre * (1.0 - s))
        dpre_ref[0, :, pl.ds(0, D_SSM)] = dxs_ref[0] * dsilu[:, :D_SSM]
        dpre_ref[0, :, pl.ds(D_SSM, D_POOL)] = db_ref[0] * dsilu[:, D_SSM:D_SSM + D_POOL]
        dpre_ref[0, :, pl.ds(D_SSM + D_POOL, D_POOL)] = dc_ref[0] * dsilu[:, D_SSM + D_POOL:]
        dpre = dpre_ref[0]
        for k in range(CONV_W):
            shifted = buf_ref[pl.ds(8 - (CONV_W - 1) + k, CHUNK), :]
            dwb_ref[0, k:k + 1, :] += jnp.sum(dpre * shifted, axis=0, keepdims=True)
        dwb_ref[0, CONV_W:CONV_W + 1, :] += jnp.sum(dpre, axis=0, keepdims=True)

    row = lambda width: pl.BlockSpec((1, CHUNK, width), lambda b, j: (b, j, 0))
    return pl.pallas_call(
        body, name=name, grid=(bsz, nc),
        in_specs=[pl.BlockSpec((1, 8, c), lambda b, j: (b, jnp.maximum(j * (CHUNK // 8) - 1, 0), 0)),
                  row(c), row(D_SSM), row(D_POOL), row(D_POOL),
                  pl.BlockSpec((CONV_W, c), lambda b, j: (0, 0)), pl.BlockSpec((1, c), lambda b, j: (0, 0))],
        out_specs=[row(c), pl.BlockSpec((1, 8, c), lambda b, j: (b, 0, 0))],
        out_shape=[jax.ShapeDtypeStruct(xbc.shape, F32), jax.ShapeDtypeStruct((bsz, 8, c), F32)],
        scratch_shapes=[pltpu.VMEM((CHUNK + 8, c), F32)],
        compiler_params=_params("parallel", "arbitrary"),
    )(xbc, xbc, dxs, db, dc, conv_w, conv_b)


def _conv_bwd_in(dpre, conv_w, *, name):
    bsz, t, c = dpre.shape
    nc = t // CHUNK

    def body(cur_ref, head_ref, w_ref, o_ref, buf_ref):
        j = pl.program_id(1)
        buf_ref[pl.ds(0, CHUNK), :] = cur_ref[0]
        buf_ref[pl.ds(CHUNK, 8), :] = jnp.where(j < nc - 1, head_ref[0], 0.0)
        acc = None
        for k in range(CONV_W):
            term = w_ref[k:k + 1, :] * buf_ref[pl.ds(CONV_W - 1 - k, CHUNK), :]
            acc = term if acc is None else acc + term
        o_ref[0] = acc.astype(BF16)

    return pl.pallas_call(
        body, name=name, grid=(bsz, nc),
        in_specs=[pl.BlockSpec((1, CHUNK, c), lambda b, j: (b, j, 0)),
                  pl.BlockSpec((1, 8, c), lambda b, j: (b, jnp.minimum((j + 1) * (CHUNK // 8), t // 8 - 1), 0)),
                  pl.BlockSpec((CONV_W, c), lambda b, j: (0, 0))],
        out_specs=pl.BlockSpec((1, CHUNK, c), lambda b, j: (b, j, 0)), out_shape=jax.ShapeDtypeStruct(dpre.shape, BF16),
        scratch_shapes=[pltpu.VMEM((CHUNK + 8, c), F32)],
        compiler_params=_params("parallel", "parallel"),
    )(dpre, dpre, conv_w)


def _ssd_common(j, dtr, dtb, alog):
    lane = lax.broadcasted_iota(jnp.int32, (CHUNK, LANES), 1)
    row = lax.broadcasted_iota(jnp.int32, (CHUNK, LANES), 0)
    raw = dtr + dtb
    valid = (lane < HPG) & ((j > 0) | (row >= PAD))
    dt = jnp.where(valid, _softplus(raw), 0.0)
    a = -jnp.exp(alog)
    tril = (row >= lane).astype(F32)
    acs = _dot_exact(tril, dt * a)
    return dict(lane=lane, row=row, raw=raw, valid=valid, dt=dt, a=a, causal=row >= lane,
                acs=acs, acs_t=acs.T, dt_t=dt.T, aend=acs[CHUNK - 1:CHUNK, :])


def _ssd_specs(bsz, nc, rev):
    ch = (lambda j: nc - 1 - j) if rev else (lambda j: j)
    return dict(
        xs=pl.BlockSpec((bsz, CHUNK, GW), lambda g, j: (0, ch(j), g)),
        bm=pl.BlockSpec((bsz, CHUNK, D_STATE), lambda g, j: (0, ch(j), D_SSM // D_STATE + g)),
        cm=pl.BlockSpec((bsz, CHUNK, D_STATE), lambda g, j: (0, ch(j), D_SSM // D_STATE + N_GROUPS + g)),
        lane_blk=pl.BlockSpec((bsz, CHUNK, LANES), lambda g, j: (0, ch(j), g)),
        grp_const=pl.BlockSpec((1, 1, LANES), lambda g, j: (g, 0, 0)),
        grp_vec=pl.BlockSpec((1, GW), lambda g, j: (0, g)),
        state=pl.BlockSpec((bsz, 1, D_STATE, GW), lambda g, j: (0, ch(j), 0, g)),
    )


def _ssd_fwd(xc, dtr, z, dtb, alog, dskip, normw, *, name, rider=None):
    bsz, t, _ = xc.shape
    nc = t // CHUNK
    sp = _ssd_specs(bsz, nc, False)

    def body(xs_ref, b_ref, c_ref, dtr_ref, z_ref, dtb_ref, alog_ref, dsk_ref, nw_ref, yn_ref, y_ref, sp_ref, s_ref):
        j = pl.program_id(1)

        @pl.when(j == 0)
        def _():
            s_ref[...] = jnp.zeros_like(s_ref)

        ex = range(bsz)
        units = [(e, r) for e in ex for r in range(HPG)]
        full = lambda v: jnp.broadcast_to(v, (CHUNK, LANES))
        pair = lambda r: pl.ds((r // 2) * LANES, LANES)
        q = [_ssd_common(j, dtr_ref[e], dtb_ref[0], alog_ref[0]) for e in ex]
        for e in ex:
            sp_ref[e, 0] = s_ref[e]
        bm, cm = [b_ref[e] for e in ex], [c_ref[e] for e in ex]
        cb = [_dot_nt(cm[e], bm[e]) for e in ex]
        low = q[0]["lane"] < HEAD_DIM
        col = {(e, r): full(q[e]["acs"][:, r:r + 1]) for e, r in units}
        aend = {(e, r): q[e]["aend"][:, r:r + 1] for e, r in units}
        decay = {(e, r): jnp.exp(jnp.where(q[e]["causal"], col[e, r] - q[e]["acs_t"][r:r + 1, :], -jnp.inf))
                 for e, r in units}
        mp = {(e, r): cb[e] * decay[e, r] * q[e]["dt_t"][r:r + 1, :] for e, r in units}
        ce = {(e, r): cm[e] * jnp.exp(col[e, r]) for e, r in units}
        bk = {(e, r): bm[e] * (jnp.exp(aend[e, r] - col[e, r]) * full(q[e]["dt"][:, r:r + 1])) for e, r in units}
        xp = {(e, r): xs_ref[e, :, pair(r)] for e, r in units}
        s_old = {(e, r): s_ref[e, :, pair(r)] for e, r in units}
        y_h = {u: _dot(mp[u], xp[u]) + _dot(ce[u], s_old[u]) for u in units}
        s_h = {u: jnp.exp(aend[u]) * s_old[u] + _dot_tn(bk[u], xp[u]) for u in units}
        for e in ex:
            for r in range(0, HPG, 2):
                y_ref[e, :, pair(r)] = jnp.where(low, y_h[e, r], y_h[e, r + 1])
                s_ref[e, :, pair(r)] = jnp.where(low, s_h[e, r], s_h[e, r + 1])
        y = [y_ref[e] + dsk_ref[...] * xs_ref[e] for e in ex]
        zz = [z_ref[e] for e in ex]
        yg = [y[e] * (zz[e] * _sigmoid(zz[e])) for e in ex]
        rstd = [lax.rsqrt(jnp.mean(yg[e] * yg[e], axis=-1, keepdims=True) + EPS) for e in ex]
        for e in ex:
            y_ref[e] = y[e]
            yn_ref[e] = (yg[e] * rstd[e] * nw_ref[...]).astype(BF16)

    grid = (N_GROUPS, nc)
    ride = _Ride(rider, body, 9, 3, 1, grid)
    outs = pl.pallas_call(
        ride.body, name=name, grid=grid,
        in_specs=[sp["xs"], sp["bm"], sp["cm"], sp["lane_blk"], sp["xs"], sp["grp_const"], sp["grp_const"],
                  sp["grp_vec"], sp["grp_vec"]] + ride.in_specs,
        out_specs=[sp["xs"], sp["xs"], sp["state"]] + ride.out_specs,
        out_shape=[jax.ShapeDtypeStruct((bsz, t, D_SSM), BF16), jax.ShapeDtypeStruct((bsz, t, D_SSM), F32),
                   jax.ShapeDtypeStruct((bsz, nc, D_STATE, D_SSM), F32)] + ride.out_shape,
        scratch_shapes=[pltpu.VMEM((bsz, D_STATE, GW), F32)] + ride.scratch,
        compiler_params=_params(*ride.semantics(("parallel", "arbitrary"))),
    )(xc, xc, xc, dtr, z, dtb, alog, dskip, normw, *ride.args)
    return outs[:3], outs[3:]


def _ssd_bwd(xc, dtr, z, ypre, sprev, dyn, dtb, alog, dskip, normw, *, name, rider=None):
    bsz, t, _ = xc.shape
    nc = t // CHUNK
    sp = _ssd_specs(bsz, nc, True)

    def body(xs_ref, b_ref, c_ref, dtr_ref, z_ref, y_ref, sp_ref, dyn_ref, dtb_ref, alog_ref, dsk_ref, nw_ref,
             dz_ref, dxs_ref, db_ref, dc_ref, ddt_ref, dnw_ref, dsm_ref, ds_ref):
        j = pl.program_id(1)

        @pl.when(j == 0)
        def _():
            ds_ref[...] = jnp.zeros_like(ds_ref)
            dnw_ref[...] = jnp.zeros_like(dnw_ref)
            dsm_ref[...] = jnp.zeros_like(dsm_ref)

        ex = range(bsz)
        heads = range(HPG)
        units = [(e, r) for e in ex for r in heads]
        q = [_ssd_common(nc - 1 - j, dtr_ref[e], dtb_ref[0], alog_ref[0]) for e in ex]
        lane, row = q[0]["lane"], q[0]["row"]
        lane1 = lane[0:1, :]
        nw = nw_ref[...]
        y, zz, dyn = [y_ref[e] for e in ex], [z_ref[e] for e in ex], [dyn_ref[e] for e in ex]
        sz = [_sigmoid(zz[e]) for e in ex]
        sil = [zz[e] * sz[e] for e in ex]
        yg = [y[e] * sil[e] for e in ex]
        rstd = [lax.rsqrt(jnp.mean(yg[e] * yg[e], axis=-1, keepdims=True) + EPS) for e in ex]
        gn = [dyn[e] * nw for e in ex]
        dyg = [rstd[e] * (gn[e] - yg[e] * (rstd[e] * rstd[e]) * jnp.mean(gn[e] * yg[e], axis=-1, keepdims=True))
               for e in ex]
        dy = [dyg[e] * sil[e] for e in ex]
        xs = [xs_ref[e] for e in ex]
        for e in ex:
            dnw_ref[e] += jnp.sum(dyn[e] * yg[e] * rstd[e], axis=0, keepdims=True)
            dz_ref[e] = (dyg[e] * y[e] * (sz[e] * (1.0 + zz[e] * (1.0 - sz[e])))).astype(BF16)
        dskip_cols = [jnp.sum(dy[e] * xs[e], axis=0, keepdims=True) for e in ex]

        bm, cm = [b_ref[e] for e in ex], [c_ref[e] for e in ex]
        cb = [_dot_nt(cm[e], bm[e]) for e in ex]
        zero = jnp.zeros((CHUNK, LANES), F32)
        full = lambda v: jnp.broadcast_to(v, (CHUNK, LANES))
        low = lane < HEAD_DIM
        half = [low if r % 2 == 0 else ~low for r in heads]
        sl = lambda v, r: v[:, (r // 2) * LANES:(r // 2 + 1) * LANES]
        pair = lambda r: pl.ds((r // 2) * LANES, LANES)
        col = {(e, r): full(q[e]["acs"][:, r:r + 1]) for e, r in units}
        dt_col = {(e, r): full(q[e]["dt"][:, r:r + 1]) for e, r in units}
        aend = {(e, r): q[e]["aend"][:, r:r + 1] for e, r in units}
        dt_row = {(e, r): q[e]["dt_t"][r:r + 1, :] for e, r in units}
        decay = {(e, r): jnp.exp(jnp.where(q[e]["causal"], col[e, r] - q[e]["acs_t"][r:r + 1, :], -jnp.inf))
                 for e, r in units}
        ea = {u: jnp.exp(col[u]) for u in units}
        dte = {u: jnp.exp(aend[u] - col[u]) for u in units}
        ed = {u: jnp.exp(aend[u]) for u in units}
        k = {u: dte[u] * dt_col[u] for u in units}
        mp = {(e, r): cb[e] * decay[e, r] * dt_row[e, r] for e, r in units}
        xp = {(e, r): sl(xs[e], r) for e, r in units}
        dym = {(e, r): jnp.where(half[r], sl(dy[e], r), 0.0) for e, r in units}
        xm = {(e, r): jnp.where(half[r], xp[e, r], 0.0) for e, r in units}
        s_old = {(e, r): sp_ref[e, 0, :, pair(r)] for e, r in units}
        dsm = {(e, r): jnp.where(half[r], ds_ref[e, :, pair(r)], 0.0) for e, r in units}
        gmat = {u: _dot_nt(dym[u], xp[u]) for u in units}
        t1 = {u: _dot_nt(dym[u], s_old[u]) for u in units}
        dbs = {u: _dot_nt(xm[u], dsm[u]) for u in units}
        dx = {(e, r): _dot_tn(mp[e, r], dym[e, r]) + _dot(bm[e] * k[e, r], dsm[e, r]) for e, r in units}
        ds = {(e, r): _dot_tn(cm[e] * ea[e, r], dym[e, r]) + ed[e, r] * dsm[e, r] for e, r in units}
        w0 = {(e, r): gmat[e, r] * cb[e] * decay[e, r] for e, r in units}
        cs0 = {u: jnp.sum(w0[u], axis=0, keepdims=True) for u in units}
        rs = {u: jnp.sum(w0[u] * dt_row[u], axis=1, keepdims=True) for u in units}
        qv = {(e, r): jnp.sum(cm[e] * t1[e, r], axis=1, keepdims=True) for e, r in units}
        dk = {(e, r): jnp.sum(bm[e] * dbs[e, r], axis=1, keepdims=True) for e, r in units}
        ddte = {u: dk[u] * dt_col[u] for u in units}
        d_aend = {u: _sum_all(dsm[u] * s_old[u]) * ed[u] + _sum_all(ddte[u][:, 0:1] * dte[u][:, 0:1]) for u in units}
        last_row = row == CHUNK - 1
        dacs_col = {u: rs[u] + qv[u] * ea[u] - ddte[u] * dte[u] + jnp.where(last_row, d_aend[u], 0.0) for u in units}
        triu = (lane >= row).astype(F32)
        for e in ex:
            dcb, dc_acc, db_acc = zero, zero, zero
            dacs, dacs_t, ddt, ddt_t = zero, zero, zero, zero
            dskip_row = jnp.zeros((1, LANES), F32)
            for r in heads:
                u = (e, r)
                dcb = dcb + gmat[u] * decay[u] * dt_row[u]
                dc_acc = dc_acc + ea[u] * t1[u]
                db_acc = db_acc + k[u] * dbs[u]
                dacs = jnp.where(lane == r, dacs_col[u], dacs)
                ddt = jnp.where(lane == r, dk[u] * dte[u], ddt)
                dacs_t = jnp.where(row == r, -cs0[u] * dt_row[u], dacs_t)
                ddt_t = jnp.where(row == r, cs0[u], ddt_t)
                dsk = _sum_all(jnp.where(half[r][0:1, :], sl(dskip_cols[e], r), 0.0))
                dskip_row = dskip_row + jnp.where(lane1 == r, dsk, 0.0)
            for r in range(0, HPG, 2):
                dxs_ref[e, :, pair(r)] = dx[e, r] + dx[e, r + 1] + sl(dy[e], r) * dsk_ref[:, pair(r)]
                ds_ref[e, :, pair(r)] = ds[e, r] + ds[e, r + 1]
            dacs = dacs + dacs_t.T
            ddt = ddt + ddt_t.T
            dda = _dot_exact(triu, dacs)
            ddt = ddt + dda * q[e]["a"]
            da = jnp.sum(dda * q[e]["dt"], axis=0, keepdims=True)
            draw = jnp.where(q[e]["valid"], ddt * _sigmoid(q[e]["raw"]), 0.0)
            ddt_ref[e] = draw.astype(BF16)
            dsm_ref[e, 0, 0:1, :] += dskip_row
            dsm_ref[e, 0, 1:2, :] += da * q[e]["a"]
            dsm_ref[e, 0, 2:3, :] += jnp.sum(draw, axis=0, keepdims=True)
            dc_ref[e] = dc_acc + _dot(dcb, bm[e])
            db_ref[e] = db_acc + _dot_tn(dcb, cm[e])

    grp_out = pl.BlockSpec((bsz, CHUNK, D_STATE), lambda g, j: (0, nc - 1 - j, g))
    grid = (N_GROUPS, nc)
    ride = _Ride(rider, body, 12, 7, 1, grid)
    outs = pl.pallas_call(
        ride.body, name=name, grid=grid,
        in_specs=[sp["xs"], sp["bm"], sp["cm"], sp["lane_blk"], sp["xs"], sp["xs"], sp["state"], sp["xs"],
                  sp["grp_const"], sp["grp_const"], sp["grp_vec"], sp["grp_vec"]] + ride.in_specs,
        out_specs=[sp["xs"], sp["xs"], grp_out, grp_out, sp["lane_blk"],
                   pl.BlockSpec((bsz, 1, GW), lambda g, j: (0, 0, g)),
                   pl.BlockSpec((bsz, 1, 8, LANES), lambda g, j: (0, g, 0, 0))] + ride.out_specs,
        out_shape=[jax.ShapeDtypeStruct((bsz, t, D_SSM), BF16), jax.ShapeDtypeStruct((bsz, t, D_SSM), F32),
                   jax.ShapeDtypeStruct((bsz, t, N_GROUPS * D_STATE), F32),
                   jax.ShapeDtypeStruct((bsz, t, N_GROUPS * D_STATE), F32),
                   jax.ShapeDtypeStruct((bsz, t, D_DT), BF16), jax.ShapeDtypeStruct((bsz, 1, D_SSM), F32),
                   jax.ShapeDtypeStruct((bsz, N_GROUPS, 8, LANES), F32)] + ride.out_shape,
        scratch_shapes=[pltpu.VMEM((bsz, D_STATE, GW), F32)] + ride.scratch,
        compiler_params=_params(*ride.semantics(("parallel", "arbitrary"))),
    )(xc, xc, xc, dtr, z, ypre, sprev, dyn, dtb, alog, dskip, normw, *ride.args)
    return outs[:7], outs[7:]


def _input_grad(dhn, h0, w, dres, seq, *, name):
    bsz, t, d = h0.shape
    nc = t // CHUNK

    def body(dy_ref, h_ref, w_ref, dres_ref, gx_ref, head_ref, dw_ref):
        j = pl.program_id(1)

        @pl.when((pl.program_id(0) == 0) & (j == 0))
        def _():
            dw_ref[...] = jnp.zeros_like(dw_ref)

        x, dyv = h_ref[0], dy_ref[0]
        r = lax.rsqrt(jnp.mean(x * x, axis=-1, keepdims=True) + EPS)
        g = dyv * w_ref[...]
        dx = r * (g - x * (r * r) * jnp.mean(g * x, axis=-1, keepdims=True)) + dres_ref[0]
        dw_ref[...] += jnp.sum(dyv * x * r, axis=0, keepdims=True)

        @pl.when(j == 0)
        def _():
            head_ref[0] = dx

        gx_ref[0] = dx

    row = pl.BlockSpec((1, CHUNK, d), lambda b, j: (b, j, 0))
    return pl.pallas_call(
        body, name=name, grid=(bsz, nc),
        in_specs=[row, row, pl.BlockSpec((1, d), lambda b, j: (0, 0)), row],
        out_specs=[pl.BlockSpec((1, CHUNK, d), lambda b, j: (b, jnp.maximum(j - 1, 0), 0)),
                   pl.BlockSpec((1, CHUNK, d), lambda b, j: (b, 0, 0)), pl.BlockSpec((1, d), lambda b, j: (0, 0))],
        out_shape=[jax.ShapeDtypeStruct((bsz, seq, d), F32), jax.ShapeDtypeStruct((bsz, CHUNK, d), F32),
                   jax.ShapeDtypeStruct((1, d), F32)],
        compiler_params=_params("arbitrary", "arbitrary"),
    )(dhn, h0, w, dres)


def _remote(src, dst, send_sem, recv_sem, dev):
    return pltpu.make_async_remote_copy(src_ref=src, dst_ref=dst, send_sem=send_sem, recv_sem=recv_sem,
                                        device_id=dev, device_id_type=MESH)


def _position():
    return lax.axis_index("x"), lax.axis_index("y"), lax.axis_index("c")


def _other_chips(pos):
    x, y, _ = pos
    return [(1 - x, y), (x, 1 - y), (1 - x, 1 - y)]


class _Gather:
    def __init__(self, arrs):
        n = len(arrs)
        self.args, self.n_in, self.n_out = list(arrs), n, n
        self.out_shape = [jax.ShapeDtypeStruct((4,) + a.shape, a.dtype) for a in arrs]
        self.scratch = [pltpu.SemaphoreType.DMA((3 * n,)), pltpu.SemaphoreType.DMA((3 * n,)),
                        pltpu.SemaphoreType.DMA((n,))]

    def _copies(self, pos, ins, outs, sems):
        send_sems, recv_sems, loc_sems = sems
        x, y, c = pos
        me = 2 * x + y
        local = [pltpu.make_async_copy(ins[i], outs[i].at[me], loc_sems.at[i]) for i in range(self.n_in)]
        sends, recvs = [], []
        for i in range(self.n_in):
            for k, (px, py) in enumerate(_other_chips(pos)):
                sems_k = (send_sems.at[3 * i + k], recv_sems.at[3 * i + k], (px, py, c))
                sends.append(_remote(ins[i], outs[i].at[me], *sems_k))
                recvs.append(_remote(ins[i], outs[i].at[2 * px + py], *sems_k))
        return local, sends, recvs

    def start(self, pos, ins, outs, sems):
        local, sends, _ = self._copies(pos, ins, outs, sems)
        for cp in local + sends:
            cp.start()

    def finish(self, pos, ins, outs, sems):
        local, sends, recvs = self._copies(pos, ins, outs, sems)
        for cp in recvs:
            cp.wait_recv()
        for cp in sends:
            cp.wait_send()
        for cp in local:
            cp.wait()


class _Exchange:
    FLIPS = [(fx, fy, fc) for fx in (0, 1) for fy in (0, 1) for fc in (0, 1)][1:]

    def __init__(self, big, small=None):
        n = len(big)
        self.n_big, self.has_small = n, small is not None
        self.args = list(big) + ([small] if self.has_small else [])
        self.n_in = self.n_out = len(self.args)
        self.out_shape = [jax.ShapeDtypeStruct(a.shape, a.dtype) for a in big]
        self.scratch = [pltpu.SemaphoreType.DMA((max(3 * n, 1),)), pltpu.SemaphoreType.DMA((max(3 * n, 1),))]
        if self.has_small:
            self.out_shape.append(jax.ShapeDtypeStruct((8,) + small.shape, small.dtype))
            self.scratch += [pltpu.SemaphoreType.DMA((7,)), pltpu.SemaphoreType.DMA((7,)), pltpu.SemaphoreType.DMA((1,))]

    def _copies(self, pos, ins, outs, sems):
        x, y, c = pos
        me, me8 = 2 * x + y, 4 * x + 2 * y + c
        local, sends, recvs = [], [], []
        for i in range(self.n_big):
            for k, (px, py) in enumerate(_other_chips(pos)):
                sems_k = (sems[0].at[3 * i + k], sems[1].at[3 * i + k], (px, py, c))
                sends.append(_remote(ins[i].at[2 * px + py], outs[i].at[me], *sems_k))
                recvs.append(_remote(ins[i].at[me], outs[i].at[2 * px + py], *sems_k))
        if self.has_small:
            small, landed = ins[self.n_big], outs[self.n_big]
            local.append(pltpu.make_async_copy(small, landed.at[me8], sems[4].at[0]))
            for k, (fx, fy, fc) in enumerate(self.FLIPS):
                peer = (x ^ fx, y ^ fy, c ^ fc)
                sems_k = (sems[2].at[k], sems[3].at[k], peer)
                sends.append(_remote(small, landed.at[me8], *sems_k))
                recvs.append(_remote(small, landed.at[4 * peer[0] + 2 * peer[1] + peer[2]], *sems_k))
        return local, sends, recvs

    start = _Gather.start
    finish = _Gather.finish


class _Swap:
    def __init__(self, arrs):
        n = len(arrs)
        self.args, self.n_in, self.n_out = list(arrs), n, n
        self.out_shape = [jax.ShapeDtypeStruct(a.shape, a.dtype) for a in arrs]
        self.scratch = [pltpu.SemaphoreType.DMA((n,)), pltpu.SemaphoreType.DMA((n,))]

    def _copies(self, pos, ins, outs, sems):
        x, y, c = pos
        both = [_remote(ins[i], outs[i], sems[0].at[i], sems[1].at[i], (x, y, 1 - c)) for i in range(self.n_in)]
        return [], both, both

    start = _Gather.start
    finish = _Gather.finish


def _comm(rider, *, name):
    a, b = rider.n_in, rider.n_in + rider.n_out

    def body(*refs):
        pos = _position()
        rider.start(pos, refs[:a], refs[a:b], refs[b:])
        rider.finish(pos, refs[:a], refs[a:b], refs[b:])

    return pl.pallas_call(body, name=name, in_specs=[ANY] * rider.n_in, out_specs=[ANY] * rider.n_out,
                          out_shape=rider.out_shape, scratch_shapes=rider.scratch)(*rider.args)


class _Ride:
    def __init__(self, rider, body, n_in, n_out, n_scratch, grid):
        self.rider = rider
        self.args = rider.args if rider else []
        self.in_specs = [ANY] * rider.n_in if rider else []
        self.out_specs = [ANY] * rider.n_out if rider else []
        self.out_shape = rider.out_shape if rider else []
        self.scratch = rider.scratch if rider else []
        self.body = self._wrap(body, n_in, n_out, n_scratch, grid) if rider else body

    def semantics(self, sem):
        return ("arbitrary",) * len(sem) if self.rider else sem

    def _wrap(self, body, n_in, n_out, n_scratch, grid):
        rider = self.rider
        a = n_in
        b = a + rider.n_in
        c = b + n_out
        d = c + rider.n_out
        e = d + n_scratch

        def wrapped(*refs):
            pos = _position()
            ids = [pl.program_id(i) for i in range(len(grid))]
            first = functools.reduce(jnp.logical_and, [i == 0 for i in ids])
            last = functools.reduce(jnp.logical_and, [i == g - 1 for i, g in zip(ids, grid)])

            @pl.when(first)
            def _():
                rider.start(pos, refs[a:b], refs[c:d], refs[e:])

            body(*refs[:a], *refs[b:c], *refs[d:e])

            @pl.when(last)
            def _():
                rider.finish(pos, refs[a:b], refs[c:d], refs[e:])

        return wrapped


def _chip_sum(own, landed, *, name):
    r, c = own.shape
    tm = _pick(r, (256, 128, 64, 8))

    def body(own_ref, land_ref, o_ref):
        me = 2 * lax.axis_index("x") + lax.axis_index("y")
        acc = None
        for jchip in range(4):
            term = jnp.where(me == jchip, own_ref[...], land_ref[jchip].astype(F32))
            acc = term if acc is None else acc + term
        o_ref[...] = acc

    return pl.pallas_call(
        body, name=name, grid=(r // tm,),
        in_specs=[pl.BlockSpec((tm, c), lambda i: (i, 0)), pl.BlockSpec((4, tm, c), lambda i: (0, i, 0))],
        out_specs=pl.BlockSpec((tm, c), lambda i: (i, 0)), out_shape=jax.ShapeDtypeStruct((r, c), F32),
        compiler_params=_params("parallel"),
    )(own, landed)


def _device_sum(parts, *, name):
    _, r, c = parts.shape

    def body(p_ref, o_ref):
        acc = p_ref[0]
        for d in range(1, 8):
            acc = acc + p_ref[d]
        o_ref[...] = acc

    return pl.pallas_call(body, name=name, out_shape=jax.ShapeDtypeStruct((r, c), F32))(parts)


def _adamw_math(w, g, m, v):
    m = ADAM_B1 * m + (1.0 - ADAM_B1) * g
    v = ADAM_B2 * v + (1.0 - ADAM_B2) * (g * g)
    m_hat = m / (1.0 - ADAM_B1 ** ADAM_STEP)
    v_hat = v / (1.0 - ADAM_B2 ** ADAM_STEP)
    return -ADAM_LR * (m_hat / (jnp.sqrt(v_hat) + ADAM_EPS) + ADAM_WD * w), m, v


def _adamw(w, g_parts, m, v, *, name):
    r, c = w.shape
    tm = _pick(r, (256, 128, 64, 16, 8, 4, 1)) if r * c > 65536 else r
    n_g = len(g_parts)

    def body(*refs):
        w_ref, m_ref, v_ref = refs[n_g:n_g + 3]
        g_ref, d_ref, nm_ref, nv_ref = refs[n_g + 3:]
        g = refs[0][...]
        for p in refs[1:n_g]:
            g = g + p[...]
        g_ref[...] = g
        d_ref[...], nm_ref[...], nv_ref[...] = _adamw_math(w_ref[...], g, m_ref[...], v_ref[...])

    blk = pl.BlockSpec((tm, c), lambda i: (i, 0))
    return pl.pallas_call(
        body, name=name, grid=(r // tm,), in_specs=[blk] * (n_g + 3), out_specs=[blk] * 4,
        out_shape=[jax.ShapeDtypeStruct((r, c), F32)] * 4, compiler_params=_params("parallel"),
    )(*g_parts, w, m, v)


def _pad_heads(v):
    return jnp.pad(v.reshape(N_GROUPS, 1, HPG), ((0, 0), (0, 0), (0, LANES - HPG)))


def _unpad_heads(v):
    return v[:, :HPG].reshape(1, N_HEADS)


_SMALL = [("norm_mix_w", (1, 1024)), ("pool_w", (512, 128)), ("pool_scale", (1, 512)), ("conv_w", (4, D_XBC)),
          ("conv_b", (1, D_XBC)), ("dt_bias", (1, N_HEADS)), ("a_log", (1, N_HEADS)), ("d_skip", (1, N_HEADS)),
          ("ssm_norm_w", (1, D_SSM)), ("norm_ffn_w", (1, 1024)), ("norm_f_w", (1, 1024)), ("meta", (N_META, 1024))]


def _pack_small(grads):
    rows = []
    for nm, shape in _SMALL:
        flat = grads[nm].reshape(-1)
        rows.append(jnp.pad(flat, (0, (-flat.size) % LANES)).reshape(-1, LANES))
    packed = jnp.concatenate(rows, axis=0)
    return jnp.pad(packed, ((0, (-packed.shape[0]) % 8), (0, 0)))


def _unpack_small(packed):
    out, r0 = {}, 0
    for nm, shape in _SMALL:
        size = shape[0] * shape[1]
        nrow = -(-size // LANES)
        out[nm] = packed[r0:r0 + nrow].reshape(-1)[:size].reshape(shape)
        r0 += nrow
    return out


def kernel(x, meta, norm_mix_w, w_in, pool_w, pool_scale, conv_w, conv_b, dt_bias, a_log, d_skip, ssm_norm_w, w_out, norm_ffn_w, w_ff1, w_ff2, norm_f_w, loss_target, m_meta, m_norm_mix_w, m_w_in, m_pool_w, m_pool_scale, m_conv_w, m_conv_b, m_dt_bias, m_a_log, m_d_skip, m_ssm_norm_w, m_w_out, m_norm_ffn_w, m_w_ff1, m_w_ff2, m_norm_f_w, v_meta, v_norm_mix_w, v_w_in, v_pool_w, v_pool_scale, v_conv_w, v_conv_b, v_dt_bias, v_a_log, v_d_skip, v_ssm_norm_w, v_w_out, v_norm_ffn_w, v_w_ff1, v_w_ff2, v_norm_f_w):
    bsz, seq, d = x.shape
    t = seq + CHUNK
    n = bsz * t
    chip = 2 * lax.axis_index("x") + lax.axis_index("y")
    d_in = w_in.shape[2] * 4

    g_in, g_conv, g_meta = _comm(_Gather([w_in[0].astype(BF16), conv_w[0], meta]), name="gather_in")
    late_weights = _Gather([w_out[0].astype(BF16), w_ff1[0].astype(BF16), w_ff2[0].astype(BF16)])
    win = g_in.transpose(1, 0, 2).reshape(d, d_in)
    wu, wz = win[:, :D_POOL], win[:, D_POOL:D_POOL + D_SSM]
    wx = win[:, D_POOL + D_SSM:D_POOL + D_SSM + D_XBC]
    wdt = jnp.pad(win[:, D_POOL + D_SSM + D_XBC:].reshape(d, N_GROUPS, HPG),
                  ((0, 0), (0, 0), (0, LANES - HPG))).reshape(d, D_DT)
    convw = g_conv.transpose(1, 0, 2).reshape(CONV_W, D_XBC)
    meta_full = g_meta.transpose(1, 0, 2).reshape(N_META, d)
    dtb, alog = _pad_heads(dt_bias), _pad_heads(a_log)
    dskip = jnp.repeat(d_skip, HEAD_DIM, axis=1)
    poolw = pool_w[0]

    h0 = jnp.concatenate([jnp.zeros((bsz, PAD, d), F32), jnp.broadcast_to(meta_full[None], (bsz, N_META, d)), x], axis=1)
    h0f = h0.reshape(n, d)
    hn1 = _rms_fwd(h0f, norm_mix_w, name="norm_mix")
    u = _mm(hn1, wu, name="proj_u")
    z = _mm(hn1, wz, name="proj_z")
    xbc = _mm(hn1, wx, name="proj_xbc")
    dtr = _mm(hn1, wdt, name="proj_dt")
    ypool = _pool_fwd(u.reshape(bsz, t, D_POOL), poolw, pool_scale, name="pool_fwd")
    xbc3 = xbc.reshape(bsz, t, D_XBC)
    xc = _conv_fwd(xbc3, convw, conv_b, name="conv_fwd")
    z3, dtr3 = z.reshape(bsz, t, D_SSM), dtr.reshape(bsz, t, D_DT)
    (yn, ypre, sprev), (g_out, g_ff1, g_ff2) = _ssd_fwd(xc, dtr3, z3, dtb, alog, dskip, ssm_norm_w, name="ssd_fwd",
                                                        rider=late_weights)
    wo = g_out.reshape(D_POOL + D_SSM, d)
    wo_p, wo_s = wo[:D_POOL], wo[D_POOL:]
    w1 = g_ff1.transpose(1, 0, 2).reshape(d, D_FF)
    w2 = g_ff2.reshape(D_FF, d)
    ypool_f, yn_f = ypool.reshape(n, D_POOL), yn.reshape(n, D_SSM)
    add = lambda r, e: r + e
    h1 = _mm([ypool_f, yn_f], [wo_p, wo_s], name="out_proj", post=add, extras=(h0f,))
    hn2 = _rms_fwd(h1, norm_ffn_w, name="norm_ffn")
    act = _mm(hn2, w1, name="ff1", out_dtype=BF16)
    relu2 = lambda a: jnp.square(jnp.maximum(a, 0))
    h2 = _mm(act, w2, name="ff2", pre=relu2, post=add, extras=(h1,))
    dh2, dh2b, loss_acc, d_norm_f = _final_norm_loss(h2.reshape(bsz, t, d), loss_target, norm_f_w.reshape(1, d),
                                                     name="loss")
    loss = lax.psum(loss_acc[0, 0], ("x", "y", "c"))

    dh2f, dh2bf = dh2.reshape(n, d), dh2b.reshape(n, d)
    dact = _mm(dh2bf, w2, name="ff2_bwd", nt=True, post=lambda r, a: r * (2.0 * jnp.maximum(a, 0).astype(F32)),
               extras=(act,), out_dtype=BF16)
    d_w2 = _mm_tn(act, dh2bf, name="ff2_dw", tk=2048, tn=1024, pre=relu2)
    d_w1 = _mm_tn(hn2, dact, name="ff1_dw", tk=1024, tn=2048)
    dhn2 = _mm(dact, w1, name="ff1_bwd", nt=True)
    dh1, dh1b, d_norm_ffn = _rms_bwd(dhn2, h1, norm_ffn_w, dh2f, name="norm_ffn_bwd")
    dypool = _mm(dh1b, wo_p, name="out_pool_bwd", nt=True)
    dyn = _mm(dh1b, wo_s, name="out_ssm_bwd", nt=True)
    d_wo_p = _mm_tn(ypool_f, dh1b, name="out_pool_dw", tk=512, tn=1024)
    d_wo_s = _mm_tn(yn_f, dh1b, name="out_ssm_dw", tk=1536, tn=1024)
    big_late = [jnp.concatenate([d_wo_p, d_wo_s], axis=0).reshape(4, (D_POOL + D_SSM) // 4, d),
                d_w1.reshape(d, 4, D_FF // 4).transpose(1, 0, 2), d_w2.reshape(4, D_FF // 4, d)]
    (dz, dxs, dbm, dcm, ddtr, d_nw, d_heads), landed_late = _ssd_bwd(
        xc, dtr3, z3, ypre, sprev, dyn.reshape(bsz, t, D_SSM), dtb, alog, dskip, ssm_norm_w, name="ssd_bwd",
        rider=_Exchange([b.astype(BF16) for b in big_late]))
    dpre, d_convwb = _conv_bwd_pre(xbc3, dxs, dbm, dcm, convw, conv_b, name="conv_bwd_pre")
    dxbc = _conv_bwd_in(dpre, convw, name="conv_bwd_in")
    du, d_poolw, d_poolsc = _pool_bwd(u.reshape(bsz, t, D_POOL), dypool.reshape(bsz, t, D_POOL), poolw, pool_scale,
                                      name="pool_bwd")
    duf, dzf, dxbcf, ddtrf = du.reshape(n, D_POOL), dz.reshape(n, D_SSM), dxbc.reshape(n, D_XBC), ddtr.reshape(n, D_DT)
    d_wu = _mm_tn(hn1, duf, name="proj_u_dw", tk=1024, tn=512)
    d_wz = _mm_tn(hn1, dzf, name="proj_z_dw", tk=1024, tn=1536)
    d_wx = _mm_tn(hn1, dxbcf, name="proj_xbc_dw", tk=1024, tn=1280)
    d_wdt = _mm_tn(hn1, ddtrf, name="proj_dt_dw", tk=1024, tn=512)
    d_win = jnp.concatenate([d_wu, d_wz, d_wx, d_wdt.reshape(d, N_GROUPS, LANES)[:, :, :HPG].reshape(d, N_HEADS)], axis=1)
    big_in = d_win.reshape(d, 4, d_in // 4).transpose(1, 0, 2)
    dhn1, landed_in = _mm([duf, dzf, dxbcf, ddtrf], [wu, wz, wx, wdt], name="proj_bwd", nt=True,
                          rider=_Exchange([big_in.astype(BF16)]))
    grad_x, d_head_rows, d_norm_mix = _input_grad(
        dhn1.reshape(bsz, t, d), h0, norm_mix_w, dh1.reshape(bsz, t, d), seq, name="input_grad")

    big = [big_in] + big_late
    landed = list(landed_in) + list(landed_late)
    heads = jnp.sum(d_heads, axis=0)
    small = _pack_small({
        "norm_mix_w": d_norm_mix, "pool_w": jnp.sum(d_poolw, axis=0), "pool_scale": jnp.sum(d_poolsc, axis=0),
        "conv_w": jnp.sum(d_convwb[:, :CONV_W], axis=0), "conv_b": jnp.sum(d_convwb[:, CONV_W:CONV_W + 1], axis=0),
        "dt_bias": _unpad_heads(heads[:, 2]), "a_log": _unpad_heads(heads[:, 1]), "d_skip": _unpad_heads(heads[:, 0]),
        "ssm_norm_w": jnp.sum(d_nw, axis=0), "norm_ffn_w": d_norm_ffn, "norm_f_w": d_norm_f,
        "meta": jnp.sum(d_head_rows[:, PAD:], axis=0)})
    (small_all,) = _comm(_Exchange([], small), name="exchange_small")
    own = [lax.dynamic_index_in_dim(b, chip, 0, keepdims=False) for b in big]
    mine = [_chip_sum(o, l, name=f"chip_sum_{i}") for i, (o, l) in enumerate(zip(own, landed))]
    theirs = _comm(_Swap(mine), name="swap_cores")
    gsmall = _unpack_small(_device_sum(small_all, name="device_sum"))
    gsmall["conv_w"] = lax.dynamic_slice_in_dim(gsmall["conv_w"], chip * (D_XBC // 4), D_XBC // 4, axis=1)
    gsmall["meta"] = lax.dynamic_slice_in_dim(gsmall["meta"], chip * (d // 4), d // 4, axis=1)

    given = dict(meta=(meta, m_meta, v_meta), norm_mix_w=(norm_mix_w, m_norm_mix_w, v_norm_mix_w),
                 w_in=(w_in, m_w_in, v_w_in), pool_w=(pool_w, m_pool_w, v_pool_w),
                 pool_scale=(pool_scale, m_pool_scale, v_pool_scale), conv_w=(conv_w, m_conv_w, v_conv_w),
                 conv_b=(conv_b, m_conv_b, v_conv_b), dt_bias=(dt_bias, m_dt_bias, v_dt_bias),
                 a_log=(a_log, m_a_log, v_a_log), d_skip=(d_skip, m_d_skip, v_d_skip),
                 ssm_norm_w=(ssm_norm_w, m_ssm_norm_w, v_ssm_norm_w), w_out=(w_out, m_w_out, v_w_out),
                 norm_ffn_w=(norm_ffn_w, m_norm_ffn_w, v_norm_ffn_w), w_ff1=(w_ff1, m_w_ff1, v_w_ff1),
                 w_ff2=(w_ff2, m_w_ff2, v_w_ff2), norm_f_w=(norm_f_w, m_norm_f_w, v_norm_f_w))
    big_names = ["w_in", "w_out", "w_ff1", "w_ff2"]
    results = {}
    for nm, (w, m, v) in given.items():
        if nm in big_names:
            i = big_names.index(nm)
            parts, shape2 = (mine[i], theirs[i]), mine[i].shape
        else:
            parts, shape2 = (gsmall[nm],), gsmall[nm].shape
        outs = _adamw(w.reshape(shape2), parts, m.reshape(shape2), v.reshape(shape2), name=f"adamw_{nm}")
        results[nm] = [o.reshape(w.shape) for o in outs]
    order = list(given)
    return (loss, grad_x, *[results[nm][0] for nm in order], *[results[nm][1] for nm in order],
            *[results[nm][2] for nm in order], *[results[nm][3] for nm in order])
```

```python
import functools

import jax
import jax.numpy as jnp
from jax import lax
from jax.experimental import pallas as pl
from jax.experimental.pallas import tpu as pltpu

F32 = jnp.float32
BF16 = jnp.bfloat16
MESH = pl.DeviceIdType.MESH
ANY = pl.BlockSpec(memory_space=pl.ANY)

D_MODEL = 1024
N_META = 16
CHUNK = 128
PAD = CHUNK - N_META
POOL_WINDOWS = (2, 4, 8, 16)
D_POOL = 512
POOL_GROUP = 128
D_SSM = 1536
N_HEADS = 24
N_GROUPS = 4
HPG = 6
HEAD_DIM = 64
D_STATE = 128
GW = HPG * HEAD_DIM
D_XBC = D_SSM + 2 * N_GROUPS * D_STATE
D_DT = N_GROUPS * 128
D_FF = 4096
CONV_W = 4
EPS = 1e-5
LANES = 128
VMEM_LIMIT = 56 * 1024 * 1024

ADAM_LR, ADAM_B1, ADAM_B2, ADAM_EPS, ADAM_WD, ADAM_STEP = 0.001, 0.9, 0.999, 1e-08, 0.01, 10


def _params(*sem):
    return pltpu.CompilerParams(dimension_semantics=sem, vmem_limit_bytes=VMEM_LIMIT)


def _pick(n, cands):
    for c in cands:
        if n % c == 0:
            return c
    raise ValueError(f"no block size for {n}")


def _dot(a, b):
    return jnp.dot(a.astype(BF16), b.astype(BF16), preferred_element_type=F32)


def _dot_nt(a, b):
    return lax.dot_general(a.astype(BF16), b.astype(BF16), (((1,), (1,)), ((), ())), preferred_element_type=F32)


def _dot_tn(a, b):
    return lax.dot_general(a.astype(BF16), b.astype(BF16), (((0,), (0,)), ((), ())), preferred_element_type=F32)


def _dot_exact(mask, x):
    m = mask.astype(BF16)
    hi = x.astype(BF16)
    r1 = x - hi.astype(F32)
    mid = r1.astype(BF16)
    lo = (r1 - mid.astype(F32)).astype(BF16)
    dot = lambda t: jnp.dot(m, t, preferred_element_type=F32)
    return dot(hi) + dot(mid) + dot(lo)


def _sigmoid(x):
    return 1.0 / (1.0 + jnp.exp(-x))


def _softplus(x):
    return jnp.maximum(x, 0.0) + jnp.log1p(jnp.exp(-jnp.abs(x)))


def _sum_all(x):
    return jnp.sum(jnp.sum(x, axis=1, keepdims=True), axis=0, keepdims=True)


ROW_TILES = (4224, 2816, 2112, 1408, 1056, 768, 704, 512, 384, 256, 128)
TILE_BUDGET = 28 * 1024 * 1024


def _row_tile(n, bytes_per_row, fixed_bytes):
    for tm in ROW_TILES:
        if n % tm == 0 and 2 * (tm * bytes_per_row + fixed_bytes) <= TILE_BUDGET:
            return tm
    raise ValueError(f"no row tile for {n}")


def _mm(a, w, *, name, tn=512, nt=False, pre=None, post=None, extras=(), out_dtype=F32, rider=None):
    a_list = list(a) if isinstance(a, (list, tuple)) else [a]
    w_list = list(w) if isinstance(w, (list, tuple)) else [w]
    n_a, n_ex = len(a_list), len(extras)
    n = a_list[0].shape[0]
    m = w_list[0].shape[0] if nt else w_list[0].shape[1]
    tn = min(tn, m)
    size = lambda dt: jnp.dtype(dt).itemsize
    per_row = (sum(x.shape[1] * size(x.dtype) for x in a_list) + tn * size(out_dtype)
               + sum(tn * size(e.dtype) for e in extras))
    tm = _row_tile(n, per_row, sum(x.shape[1 if nt else 0] * tn * size(x.dtype) for x in w_list))

    def body(*refs):
        a_refs, w_refs, ex_refs, o_ref = refs[:n_a], refs[n_a:2 * n_a], refs[2 * n_a:2 * n_a + n_ex], refs[2 * n_a + n_ex]
        r = None
        for a_ref, w_ref in zip(a_refs, w_refs):
            av = a_ref[...]
            if pre is not None:
                av = pre(av)
            term = _dot_nt(av, w_ref[...]) if nt else _dot(av, w_ref[...])
            r = term if r is None else r + term
        if post is not None:
            r = post(r, *[e[...] for e in ex_refs])
        o_ref[...] = r.astype(out_dtype)

    a_specs = [pl.BlockSpec((tm, x.shape[1]), lambda i, j: (i, 0)) for x in a_list]
    w_specs = [pl.BlockSpec((tn, x.shape[1]), lambda i, j: (j, 0)) if nt else pl.BlockSpec((x.shape[0], tn), lambda i, j: (0, j))
               for x in w_list]
    blk = pl.BlockSpec((tm, tn), lambda i, j: (i, j))
    grid = (n // tm, m // tn)
    ride = _Ride(rider, body, 2 * n_a + n_ex, 1, 0, grid)
    outs = pl.pallas_call(
        ride.body, name=name, grid=grid,
        in_specs=a_specs + w_specs + [blk] * n_ex + ride.in_specs,
        out_specs=[blk] + ride.out_specs, out_shape=[jax.ShapeDtypeStruct((n, m), out_dtype)] + ride.out_shape,
        scratch_shapes=ride.scratch, compiler_params=_params(*ride.semantics(("parallel", "parallel"))),
    )(*a_list, *w_list, *extras, *ride.args)
    return (outs[0], outs[1:]) if rider else outs[0]


def _mm_tn(a, g, *, name, tk, tn, pre=None):
    n, k = a.shape
    m = g.shape[1]
    tk, tn = min(tk, k), min(tn, m)
    tm = _row_tile(n, tk * jnp.dtype(a.dtype).itemsize + tn * jnp.dtype(g.dtype).itemsize, tk * tn * 4)

    def body(a_ref, g_ref, o_ref):
        @pl.when(pl.program_id(2) == 0)
        def _():
            o_ref[...] = jnp.zeros_like(o_ref)

        av = a_ref[...]
        if pre is not None:
            av = pre(av)
        o_ref[...] += _dot_tn(av, g_ref[...])

    return pl.pallas_call(
        body, name=name, grid=(k // tk, m // tn, n // tm),
        in_specs=[pl.BlockSpec((tm, tk), lambda i, j, r: (r, i)), pl.BlockSpec((tm, tn), lambda i, j, r: (r, j))],
        out_specs=pl.BlockSpec((tk, tn), lambda i, j, r: (i, j)),
        out_shape=jax.ShapeDtypeStruct((k, m), F32),
        compiler_params=_params("parallel", "parallel", "arbitrary"),
    )(a, g)


def _rms_fwd(h, w, *, name):
    n, d = h.shape
    tm = _pick(n, (768, 512, 256, 128))

    def body(h_ref, w_ref, o_ref):
        x = h_ref[...]
        r = lax.rsqrt(jnp.mean(x * x, axis=-1, keepdims=True) + EPS)
        o_ref[...] = (x * r * w_ref[...]).astype(BF16)

    return pl.pallas_call(
        body, name=name, grid=(n // tm,),
        in_specs=[pl.BlockSpec((tm, d), lambda i: (i, 0)), pl.BlockSpec((1, d), lambda i: (0, 0))],
        out_specs=pl.BlockSpec((tm, d), lambda i: (i, 0)), out_shape=jax.ShapeDtypeStruct((n, d), BF16),
        compiler_params=_params("parallel"),
    )(h, w)


def _rms_bwd(dy, h, w, dres, *, name):
    n, d = h.shape
    tm = _pick(n, (768, 512, 256, 128))

    def body(dy_ref, h_ref, w_ref, dres_ref, dx_ref, dxb_ref, dw_ref):
        @pl.when(pl.program_id(0) == 0)
        def _():
            dw_ref[...] = jnp.zeros_like(dw_ref)

        x, dyv = h_ref[...], dy_ref[...]
        r = lax.rsqrt(jnp.mean(x * x, axis=-1, keepdims=True) + EPS)
        g = dyv * w_ref[...]
        dx = r * (g - x * (r * r) * jnp.mean(g * x, axis=-1, keepdims=True)) + dres_ref[...]
        dx_ref[...] = dx
        dxb_ref[...] = dx.astype(BF16)
        dw_ref[...] += jnp.sum(dyv * x * r, axis=0, keepdims=True)

    row = pl.BlockSpec((tm, d), lambda i: (i, 0))
    vec = pl.BlockSpec((1, d), lambda i: (0, 0))
    return pl.pallas_call(
        body, name=name, grid=(n // tm,), in_specs=[row, row, vec, row], out_specs=[row, row, vec],
        out_shape=[jax.ShapeDtypeStruct((n, d), F32), jax.ShapeDtypeStruct((n, d), BF16), jax.ShapeDtypeStruct((1, d), F32)],
        compiler_params=_params("arbitrary"),
    )(dy, h, w, dres)


def _final_norm_loss(h2, target, w, *, name):
    bsz, t, d = h2.shape
    nc = t // CHUNK

    def body(h_ref, t_ref, w_ref, dh_ref, dhb_ref, loss_ref, dw_ref):
        j = pl.program_id(1)

        @pl.when((pl.program_id(0) == 0) & (j == 0))
        def _():
            loss_ref[...] = jnp.zeros_like(loss_ref)
            dw_ref[...] = jnp.zeros_like(dw_ref)

        x, wv = h_ref[0], w_ref[...]
        r = lax.rsqrt(jnp.mean(x * x, axis=-1, keepdims=True) + EPS)
        diff = jnp.where(j > 0, x * r * wv - t_ref[0], 0.0)
        loss_ref[...] += _sum_all(diff * diff) * (0.5 / d)
        dy = diff * (1.0 / d)
        g = dy * wv
        dh = r * (g - x * (r * r) * jnp.mean(g * x, axis=-1, keepdims=True))
        dh_ref[0] = dh
        dhb_ref[0] = dh.astype(BF16)
        dw_ref[...] += jnp.sum(dy * x * r, axis=0, keepdims=True)

    row = pl.BlockSpec((1, CHUNK, d), lambda b, j: (b, j, 0))
    return pl.pallas_call(
        body, name=name, grid=(bsz, nc),
        in_specs=[row, pl.BlockSpec((1, CHUNK, d), lambda b, j: (b, jnp.maximum(j - 1, 0), 0)),
                  pl.BlockSpec((1, d), lambda b, j: (0, 0))],
        out_specs=[row, row, pl.BlockSpec((8, LANES), lambda b, j: (0, 0)), pl.BlockSpec((1, d), lambda b, j: (0, 0))],
        out_shape=[jax.ShapeDtypeStruct((bsz, t, d), F32), jax.ShapeDtypeStruct((bsz, t, d), BF16),
                   jax.ShapeDtypeStruct((8, LANES), F32), jax.ShapeDtypeStruct((1, d), F32)],
        compiler_params=_params("arbitrary", "arbitrary"),
    )(h2, target, w)


def _pool_masks(j, transposed):
    r = lax.broadcasted_iota(jnp.int32, (CHUNK, 2 * CHUNK), 0)
    c = lax.broadcasted_iota(jnp.int32, (CHUNK, 2 * CHUNK), 1)
    masks = []
    for w in POOL_WINDOWS:
        if transposed:
            m = (c >= r) & (c < r + w)
        else:
            s = c - CHUNK
            m = (s <= r) & (s > r - w) & (s + j * CHUNK >= 0)
        masks.append(m.astype(F32))
    return masks


def _pool_count(t_global, w):
    return jnp.clip(t_global - PAD + 1, 1, w).astype(F32)


def _pool_fwd(u, pool_w, pool_scale, *, name):
    bsz, t, _ = u.shape
    nc = t // CHUNK

    def body(prev_ref, cur_ref, pw_ref, sc_ref, o_ref):
        j = pl.program_id(0)
        masks = _pool_masks(j, False)
        tg = j * CHUNK + lax.broadcasted_iota(jnp.int32, (CHUNK, 1), 0)
        count = [_pool_count(tg, w) for w in POOL_WINDOWS]
        units = [(e, gi) for e in range(bsz) for gi in range(len(POOL_WINDOWS))]
        sl = lambda gi: pl.ds(gi * POOL_GROUP, POOL_GROUP)
        cur = {(e, gi): cur_ref[e, :, sl(gi)] for e, gi in units}
        both = {(e, gi): jnp.concatenate([prev_ref[e, :, sl(gi)], cur[e, gi]], axis=0) for e, gi in units}
        win = {(e, gi): _dot_exact(masks[gi], both[e, gi]) for e, gi in units}
        pooled = {(e, gi): win[e, gi] / count[gi] - cur[e, gi] for e, gi in units}
        mixed = {(e, gi): _dot(pooled[e, gi], pw_ref[gi]) for e, gi in units}
        for e, gi in units:
            o_ref[e, :, sl(gi)] = (mixed[e, gi] * sc_ref[:, sl(gi)]).astype(BF16)

    blk = lambda f: pl.BlockSpec((bsz, CHUNK, D_POOL), f)
    return pl.pallas_call(
        body, name=name, grid=(nc,),
        in_specs=[blk(lambda j: (0, jnp.maximum(j - 1, 0), 0)), blk(lambda j: (0, j, 0)),
                  pl.BlockSpec((4, POOL_GROUP, POOL_GROUP), lambda j: (0, 0, 0)),
                  pl.BlockSpec((1, D_POOL), lambda j: (0, 0))],
        out_specs=blk(lambda j: (0, j, 0)), out_shape=jax.ShapeDtypeStruct(u.shape, BF16),
        compiler_params=_params("parallel"),
    )(u, u, pool_w, pool_scale)


def _pool_bwd(u, dyp, pool_w, pool_scale, *, name):
    bsz, t, _ = u.shape
    nc = t // CHUNK

    def body(prev_ref, cur_ref, dy_ref, dyn_ref, pw_ref, sc_ref, du_ref, dpw_ref, dsc_ref):
        j = pl.program_id(0)

        @pl.when(j == 0)
        def _():
            dpw_ref[...] = jnp.zeros_like(dpw_ref)
            dsc_ref[...] = jnp.zeros_like(dsc_ref)

        fwd = _pool_masks(j, False)
        bwd = _pool_masks(j, True)
        tg = j * CHUNK + lax.broadcasted_iota(jnp.int32, (CHUNK, 1), 0)
        count = [_pool_count(tg, w) for w in POOL_WINDOWS]
        count_next = [_pool_count(tg + CHUNK, w) for w in POOL_WINDOWS]
        has_next = j < nc - 1
        groups = range(len(POOL_WINDOWS))
        units = [(e, gi) for e in range(bsz) for gi in groups]
        sl = lambda gi: pl.ds(gi * POOL_GROUP, POOL_GROUP)
        cur = {(e, gi): cur_ref[e, :, sl(gi)] for e, gi in units}
        both = {(e, gi): jnp.concatenate([prev_ref[e, :, sl(gi)], cur[e, gi]], axis=0) for e, gi in units}
        win = {(e, gi): _dot_exact(fwd[gi], both[e, gi]) for e, gi in units}
        pooled = {(e, gi): win[e, gi] / count[gi] - cur[e, gi] for e, gi in units}
        dy = {(e, gi): dy_ref[e, :, sl(gi)] for e, gi in units}
        mixed = {(e, gi): _dot(pooled[e, gi], pw_ref[gi]) for e, gi in units}
        dm = {(e, gi): dy[e, gi] * sc_ref[:, sl(gi)] for e, gi in units}
        dm_next = {(e, gi): jnp.where(has_next, dyn_ref[e, :, sl(gi)], 0.0) * sc_ref[:, sl(gi)] for e, gi in units}
        dpw = {(e, gi): _dot_tn(pooled[e, gi], dm[e, gi]) for e, gi in units}
        dpooled = {(e, gi): _dot_nt(dm[e, gi], pw_ref[gi]) for e, gi in units}
        dpooled_next = {(e, gi): _dot_nt(dm_next[e, gi], pw_ref[gi]) for e, gi in units}
        spread = {(e, gi): jnp.concatenate([dpooled[e, gi] / count[gi], dpooled_next[e, gi] / count_next[gi]], axis=0)
                  for e, gi in units}
        back = {(e, gi): _dot_exact(bwd[gi], spread[e, gi]) for e, gi in units}
        for e, gi in units:
            du_ref[e, :, sl(gi)] = (back[e, gi] - dpooled[e, gi]).astype(BF16)
        for gi in groups:
            dsc, dw = None, None
            for e in range(bsz):
                term = jnp.sum(dy[e, gi] * mixed[e, gi], axis=0, keepdims=True)
                dsc = term if dsc is None else dsc + term
                dw = dpw[e, gi] if dw is None else dw + dpw[e, gi]
            dsc_ref[:, sl(gi)] += dsc
            dpw_ref[gi] += dw

    blk = lambda f: pl.BlockSpec((bsz, CHUNK, D_POOL), f)
    return pl.pallas_call(
        body, name=name, grid=(nc,),
        in_specs=[blk(lambda j: (0, jnp.maximum(j - 1, 0), 0)), blk(lambda j: (0, j, 0)),
                  blk(lambda j: (0, j, 0)), blk(lambda j: (0, jnp.minimum(j + 1, nc - 1), 0)),
                  pl.BlockSpec((4, POOL_GROUP, POOL_GROUP), lambda j: (0, 0, 0)),
                  pl.BlockSpec((1, D_POOL), lambda j: (0, 0))],
        out_specs=[blk(lambda j: (0, j, 0)), pl.BlockSpec((4, POOL_GROUP, POOL_GROUP), lambda j: (0, 0, 0)),
                   pl.BlockSpec((1, D_POOL), lambda j: (0, 0))],
        out_shape=[jax.ShapeDtypeStruct(u.shape, BF16), jax.ShapeDtypeStruct((4, POOL_GROUP, POOL_GROUP), F32),
                   jax.ShapeDtypeStruct((1, D_POOL), F32)],
        compiler_params=_params("arbitrary"),
    )(u, u, dyp, dyp, pool_w, pool_scale)


CONV_SLAB = 512


def _conv_taps(tail, cur, keep_tail):
    ext = jnp.concatenate([jnp.where(keep_tail, tail, 0.0), cur], axis=0)
    return [(pltpu.roll(ext, CONV_W - 1 - k, 0) if k < CONV_W - 1 else ext)[8:] for k in range(CONV_W)]


def _conv_pre(taps, w_ref, b_ref, sl):
    acc = b_ref[:, sl]
    for k in range(CONV_W):
        acc = acc + w_ref[k:k + 1, sl] * taps[k]
    return acc


def _conv_fwd(xbc, conv_w, conv_b, *, name):
    bsz, t, c = xbc.shape
    nc = t // CHUNK

    def body(tail_ref, cur_ref, w_ref, b_ref, o_ref):
        keep = pl.program_id(1) > 0
        for c0 in range(0, c, CONV_SLAB):
            sl = pl.ds(c0, CONV_SLAB)
            pre = _conv_pre(_conv_taps(tail_ref[0, :, sl], cur_ref[0, :, sl], keep), w_ref, b_ref, sl)
            o_ref[0, :, sl] = pre * _sigmoid(pre)

    return pl.pallas_call(
        body, name=name, grid=(bsz, nc),
        in_specs=[pl.BlockSpec((1, 8, c), lambda b, j: (b, jnp.maximum(j * (CHUNK // 8) - 1, 0), 0)),
                  pl.BlockSpec((1, CHUNK, c), lambda b, j: (b, j, 0)),
                  pl.BlockSpec((CONV_W, c), lambda b, j: (0, 0)), pl.BlockSpec((1, c), lambda b, j: (0, 0))],
        out_specs=pl.BlockSpec((1, CHUNK, c), lambda b, j: (b, j, 0)), out_shape=jax.ShapeDtypeStruct(xbc.shape, F32),
        compiler_params=_params("parallel", "parallel"),
    )(xbc, xbc, conv_w, conv_b)


def _conv_bwd_pre(xbc, dxs, db, dc, conv_w, conv_b, *, name):
    bsz, t, c = xbc.shape
    nc = t // CHUNK

    def body(tail_ref, cur_ref, dxs_ref, db_ref, dc_ref, w_ref, b_ref, dpre_ref, dwb_ref):
        j = pl.program_id(1)

        @pl.when(j == 0)
        def _():
            dwb_ref[...] = jnp.zeros_like(dwb_ref)

        for c0 in range(0, c, CONV_SLAB):
            sl = pl.ds(c0, CONV_SLAB)
            if c0 < D_SSM:
                dxc = dxs_ref[0, :, sl]
            else:
                dxc = (db_ref if c0 < D_SSM + D_POOL else dc_ref)[0]
            taps = _conv_taps(tail_ref[0, :, sl], cur_ref[0, :, sl], j > 0)
            pre = _conv_pre(taps, w_ref, b_ref, sl)
            s = _sigmoid(pre)
            dpre = dxc * (s * (1.0 + pre * (1.0 - s)))
            dpre_ref[0, :, sl] = dpre
            for k in range(CONV_W):
                dwb_ref[0, k:k + 1, sl] += jnp.sum(dpre * taps[k], axis=0, keepdims=True)
            dwb_ref[0, CONV_W:CONV_W + 1, sl] += jnp.sum(dpre, axis=0, keepdims=True)

    assert CONV_SLAB == D_POOL and D_SSM % CONV_SLAB == 0
    row = lambda width: pl.BlockSpec((1, CHUNK, width), lambda b, j: (b, j, 0))
    return pl.pallas_call(
        body, name=name, grid=(bsz, nc),
        in_specs=[pl.BlockSpec((1, 8, c), lambda b, j: (b, jnp.maximum(j * (CHUNK // 8) - 1, 0), 0)),
                  row(c), row(D_SSM), row(D_POOL), row(D_POOL),
                  pl.BlockSpec((CONV_W, c), lambda b, j: (0, 0)), pl.BlockSpec((1, c), lambda b, j: (0, 0))],
        out_specs=[row(c), pl.BlockSpec((1, 8, c), lambda b, j: (b, 0, 0))],
        out_shape=[jax.ShapeDtypeStruct(xbc.shape, F32), jax.ShapeDtypeStruct((bsz, 8, c), F32)],
        compiler_params=_params("parallel", "arbitrary"),
    )(xbc, xbc, dxs, db, dc, conv_w, conv_b)


def _conv_bwd_in(dpre, conv_w, *, name):
    bsz, t, c = dpre.shape
    nc = t // CHUNK

    def body(cur_ref, head_ref, w_ref, o_ref):
        keep = pl.program_id(1) < nc - 1
        for c0 in range(0, c, CONV_SLAB):
            sl = pl.ds(c0, CONV_SLAB)
            ext = jnp.concatenate([cur_ref[0, :, sl], jnp.where(keep, head_ref[0, :, sl], 0.0)], axis=0)
            acc = w_ref[CONV_W - 1:CONV_W, sl] * ext[:CHUNK]
            for k in range(CONV_W - 1):
                up = CONV_W - 1 - k
                acc = acc + w_ref[k:k + 1, sl] * pltpu.roll(ext, CHUNK + 8 - up, 0)[:CHUNK]
            o_ref[0, :, sl] = acc.astype(BF16)

    return pl.pallas_call(
        body, name=name, grid=(bsz, nc),
        in_specs=[pl.BlockSpec((1, CHUNK, c), lambda b, j: (b, j, 0)),
                  pl.BlockSpec((1, 8, c), lambda b, j: (b, jnp.minimum((j + 1) * (CHUNK // 8), t // 8 - 1), 0)),
                  pl.BlockSpec((CONV_W, c), lambda b, j: (0, 0))],
        out_specs=pl.BlockSpec((1, CHUNK, c), lambda b, j: (b, j, 0)), out_shape=jax.ShapeDtypeStruct(dpre.shape, BF16),
        compiler_params=_params("parallel", "parallel"),
    )(dpre, dpre, conv_w)


def _ssd_common(j, dtr, dtb, alog):
    lane = lax.broadcasted_iota(jnp.int32, (CHUNK, LANES), 1)
    row = lax.broadcasted_iota(jnp.int32, (CHUNK, LANES), 0)
    raw = dtr + dtb
    valid = (lane < HPG) & ((j > 0) | (row >= PAD))
    dt = jnp.where(valid, _softplus(raw), 0.0)
    a = -jnp.exp(alog)
    tril = (row >= lane).astype(F32)
    acs = _dot_exact(tril, dt * a)
    return dict(lane=lane, row=row, raw=raw, valid=valid, dt=dt, a=a, causal=row >= lane,
                acs=acs, acs_t=acs.T, dt_t=dt.T, aend=acs[CHUNK - 1:CHUNK, :])


def _ssd_specs(bsz, nc, rev):
    ch = (lambda j: nc - 1 - j) if rev else (lambda j: j)
    return dict(
        xs=pl.BlockSpec((bsz, CHUNK, GW), lambda g, j: (0, ch(j), g)),
        bm=pl.BlockSpec((bsz, CHUNK, D_STATE), lambda g, j: (0, ch(j), D_SSM // D_STATE + g)),
        cm=pl.BlockSpec((bsz, CHUNK, D_STATE), lambda g, j: (0, ch(j), D_SSM // D_STATE + N_GROUPS + g)),
        lane_blk=pl.BlockSpec((bsz, CHUNK, LANES), lambda g, j: (0, ch(j), g)),
        grp_const=pl.BlockSpec((1, 1, LANES), lambda g, j: (g, 0, 0)),
        grp_vec=pl.BlockSpec((1, GW), lambda g, j: (0, g)),
        state=pl.BlockSpec((bsz, 1, D_STATE, GW), lambda g, j: (0, ch(j), 0, g)),
    )


def _ssd_fwd(xc, dtr, z, dtb, alog, dskip, normw, *, name, rider=None):
    bsz, t, _ = xc.shape
    nc = t // CHUNK
    sp = _ssd_specs(bsz, nc, False)

    def body(xs_ref, b_ref, c_ref, dtr_ref, z_ref, dtb_ref, alog_ref, dsk_ref, nw_ref, yn_ref, y_ref, sp_ref, s_ref):
        j = pl.program_id(1)

        @pl.when(j == 0)
        def _():
            s_ref[...] = jnp.zeros_like(s_ref)

        ex = range(bsz)
        units = [(e, r) for e in ex for r in range(HPG)]
        full = lambda v: jnp.broadcast_to(v, (CHUNK, LANES))
        pair = lambda r: pl.ds((r // 2) * LANES, LANES)
        q = [_ssd_common(j, dtr_ref[e], dtb_ref[0], alog_ref[0]) for e in ex]
        for e in ex:
            sp_ref[e, 0] = s_ref[e]
        bm, cm = [b_ref[e] for e in ex], [c_ref[e] for e in ex]
        cb = [_dot_nt(cm[e], bm[e]) for e in ex]
        low = q[0]["lane"] < HEAD_DIM
        col = {(e, r): full(q[e]["acs"][:, r:r + 1]) for e, r in units}
        aend = {(e, r): q[e]["aend"][:, r:r + 1] for e, r in units}
        decay = {(e, r): jnp.exp(jnp.where(q[e]["causal"], col[e, r] - q[e]["acs_t"][r:r + 1, :], -jnp.inf))
                 for e, r in units}
        mp = {(e, r): cb[e] * decay[e, r] * q[e]["dt_t"][r:r + 1, :] for e, r in units}
        ce = {(e, r): cm[e] * jnp.exp(col[e, r]) for e, r in units}
        bk = {(e, r): bm[e] * (jnp.exp(aend[e, r] - col[e, r]) * full(q[e]["dt"][:, r:r + 1])) for e, r in units}
        xp = {(e, r): xs_ref[e, :, pair(r)] for e, r in units}
        s_old = {(e, r): s_ref[e, :, pair(r)] for e, r in units}
        y_h = {u: _dot(mp[u], xp[u]) + _dot(ce[u], s_old[u]) for u in units}
        s_h = {u: jnp.exp(aend[u]) * s_old[u] + _dot_tn(bk[u], xp[u]) for u in units}
        for e in ex:
            for r in range(0, HPG, 2):
                y_ref[e, :, pair(r)] = jnp.where(low, y_h[e, r], y_h[e, r + 1])
                s_ref[e, :, pair(r)] = jnp.where(low, s_h[e, r], s_h[e, r + 1])
        y = [y_ref[e] + dsk_ref[...] * xs_ref[e] for e in ex]
        zz = [z_ref[e] for e in ex]
        yg = [y[e] * (zz[e] * _sigmoid(zz[e])) for e in ex]
        rstd = [lax.rsqrt(jnp.mean(yg[e] * yg[e], axis=-1, keepdims=True) + EPS) for e in ex]
        for e in ex:
            y_ref[e] = y[e]
            yn_ref[e] = (yg[e] * rstd[e] * nw_ref[...]).astype(BF16)

    grid = (N_GROUPS, nc)
    ride = _Ride(rider, body, 9, 3, 1, grid)
    outs = pl.pallas_call(
        ride.body, name=name, grid=grid,
        in_specs=[sp["xs"], sp["bm"], sp["cm"], sp["lane_blk"], sp["xs"], sp["grp_const"], sp["grp_const"],
                  sp["grp_vec"], sp["grp_vec"]] + ride.in_specs,
        out_specs=[sp["xs"], sp["xs"], sp["state"]] + ride.out_specs,
        out_shape=[jax.ShapeDtypeStruct((bsz, t, D_SSM), BF16), jax.ShapeDtypeStruct((bsz, t, D_SSM), F32),
                   jax.ShapeDtypeStruct((bsz, nc, D_STATE, D_SSM), F32)] + ride.out_shape,
        scratch_shapes=[pltpu.VMEM((bsz, D_STATE, GW), F32)] + ride.scratch,
        compiler_params=_params(*ride.semantics(("parallel", "arbitrary"))),
    )(xc, xc, xc, dtr, z, dtb, alog, dskip, normw, *ride.args)
    return outs[:3], outs[3:]


def _ssd_bwd(xc, dtr, z, ypre, sprev, dyn, dtb, alog, dskip, normw, *, name, rider=None):
    bsz, t, _ = xc.shape
    nc = t // CHUNK
    sp = _ssd_specs(bsz, nc, True)

    def body(xs_ref, b_ref, c_ref, dtr_ref, z_ref, y_ref, sp_ref, dyn_ref, dtb_ref, alog_ref, dsk_ref, nw_ref,
             dz_ref, dxs_ref, db_ref, dc_ref, ddt_ref, dnw_ref, dsm_ref, ds_ref):
        j = pl.program_id(1)

        @pl.when(j == 0)
        def _():
            ds_ref[...] = jnp.zeros_like(ds_ref)
            dnw_ref[...] = jnp.zeros_like(dnw_ref)
            dsm_ref[...] = jnp.zeros_like(dsm_ref)

        ex = range(bsz)
        heads = range(HPG)
        units = [(e, r) for e in ex for r in heads]
        q = [_ssd_common(nc - 1 - j, dtr_ref[e], dtb_ref[0], alog_ref[0]) for e in ex]
        lane, row = q[0]["lane"], q[0]["row"]
        lane1 = lane[0:1, :]
        nw = nw_ref[...]
        y, zz, dyn = [y_ref[e] for e in ex], [z_ref[e] for e in ex], [dyn_ref[e] for e in ex]
        sz = [_sigmoid(zz[e]) for e in ex]
        sil = [zz[e] * sz[e] for e in ex]
        yg = [y[e] * sil[e] for e in ex]
        rstd = [lax.rsqrt(jnp.mean(yg[e] * yg[e], axis=-1, keepdims=True) + EPS) for e in ex]
        gn = [dyn[e] * nw for e in ex]
        dyg = [rstd[e] * (gn[e] - yg[e] * (rstd[e] * rstd[e]) * jnp.mean(gn[e] * yg[e], axis=-1, keepdims=True))
               for e in ex]
        dy = [dyg[e] * sil[e] for e in ex]
        xs = [xs_ref[e] for e in ex]
        for e in ex:
            dnw_ref[e] += jnp.sum(dyn[e] * yg[e] * rstd[e], axis=0, keepdims=True)
            dz_ref[e] = (dyg[e] * y[e] * (sz[e] * (1.0 + zz[e] * (1.0 - sz[e])))).astype(BF16)
        dskip_cols = [jnp.sum(dy[e] * xs[e], axis=0, keepdims=True) for e in ex]

        bm, cm = [b_ref[e] for e in ex], [c_ref[e] for e in ex]
        cb = [_dot_nt(cm[e], bm[e]) for e in ex]
        zero = jnp.zeros((CHUNK, LANES), F32)
        full = lambda v: jnp.broadcast_to(v, (CHUNK, LANES))
        low = lane < HEAD_DIM
        half = [low if r % 2 == 0 else ~low for r in heads]
        sl = lambda v, r: v[:, (r // 2) * LANES:(r // 2 + 1) * LANES]
        pair = lambda r: pl.ds((r // 2) * LANES, LANES)
        col = {(e, r): full(q[e]["acs"][:, r:r + 1]) for e, r in units}
        dt_col = {(e, r): full(q[e]["dt"][:, r:r + 1]) for e, r in units}
        aend = {(e, r): q[e]["aend"][:, r:r + 1] for e, r in units}
        dt_row = {(e, r): q[e]["dt_t"][r:r + 1, :] for e, r in units}
        decay = {(e, r): jnp.exp(jnp.where(q[e]["causal"], col[e, r] - q[e]["acs_t"][r:r + 1, :], -jnp.inf))
                 for e, r in units}
        ea = {u: jnp.exp(col[u]) for u in units}
        dte = {u: jnp.exp(aend[u] - col[u]) for u in units}
        ed = {u: jnp.exp(aend[u]) for u in units}
        k = {u: dte[u] * dt_col[u] for u in units}
        mp = {(e, r): cb[e] * decay[e, r] * dt_row[e, r] for e, r in units}
        xp = {(e, r): sl(xs[e], r) for e, r in units}
        dym = {(e, r): jnp.where(half[r], sl(dy[e], r), 0.0) for e, r in units}
        xm = {(e, r): jnp.where(half[r], xp[e, r], 0.0) for e, r in units}
        s_old = {(e, r): sp_ref[e, 0, :, pair(r)] for e, r in units}
        dsm = {(e, r): jnp.where(half[r], ds_ref[e, :, pair(r)], 0.0) for e, r in units}
        gmat = {u: _dot_nt(dym[u], xp[u]) for u in units}
        t1 = {u: _dot_nt(dym[u], s_old[u]) for u in units}
        dbs = {u: _dot_nt(xm[u], dsm[u]) for u in units}
        dx = {(e, r): _dot_tn(mp[e, r], dym[e, r]) + _dot(bm[e] * k[e, r], dsm[e, r]) for e, r in units}
        ds = {(e, r): _dot_tn(cm[e] * ea[e, r], dym[e, r]) + ed[e, r] * dsm[e, r] for e, r in units}
        w0 = {(e, r): gmat[e, r] * cb[e] * decay[e, r] for e, r in units}
        cs0 = {u: jnp.sum(w0[u], axis=0, keepdims=True) for u in units}
        rs = {u: jnp.sum(w0[u] * dt_row[u], axis=1, keepdims=True) for u in units}
        qv = {(e, r): jnp.sum(cm[e] * t1[e, r], axis=1, keepdims=True) for e, r in units}
        dk = {(e, r): jnp.sum(bm[e] * dbs[e, r], axis=1, keepdims=True) for e, r in units}
        ddte = {u: dk[u] * dt_col[u] for u in units}
        d_aend = {u: _sum_all(dsm[u] * s_old[u]) * ed[u] + _sum_all(ddte[u][:, 0:1] * dte[u][:, 0:1]) for u in units}
        last_row = row == CHUNK - 1
        dacs_col = {u: rs[u] + qv[u] * ea[u] - ddte[u] * dte[u] + jnp.where(last_row, d_aend[u], 0.0) for u in units}
        triu = (lane >= row).astype(F32)
        for e in ex:
            dcb, dc_acc, db_acc = zero, zero, zero
            dacs, dacs_t, ddt, ddt_t = zero, zero, zero, zero
            dskip_row = jnp.zeros((1, LANES), F32)
            for r in heads:
                u = (e, r)
                dcb = dcb + gmat[u] * decay[u] * dt_row[u]
                dc_acc = dc_acc + ea[u] * t1[u]
                db_acc = db_acc + k[u] * dbs[u]
                dacs = jnp.where(lane == r, dacs_col[u], dacs)
                ddt = jnp.where(lane == r, dk[u] * dte[u], ddt)
                dacs_t = jnp.where(row == r, -cs0[u] * dt_row[u], dacs_t)
                ddt_t = jnp.where(row == r, cs0[u], ddt_t)
                dsk = _sum_all(jnp.where(half[r][0:1, :], sl(dskip_cols[e], r), 0.0))
                dskip_row = dskip_row + jnp.where(lane1 == r, dsk, 0.0)
            for r in range(0, HPG, 2):
                dxs_ref[e, :, pair(r)] = dx[e, r] + dx[e, r + 1] + sl(dy[e], r) * dsk_ref[:, pair(r)]
                ds_ref[e, :, pair(r)] = ds[e, r] + ds[e, r + 1]
            dacs = dacs + dacs_t.T
            ddt = ddt + ddt_t.T
            dda = _dot_exact(triu, dacs)
            ddt = ddt + dda * q[e]["a"]
            da = jnp.sum(dda * q[e]["dt"], axis=0, keepdims=True)
            draw = jnp.where(q[e]["valid"], ddt * _sigmoid(q[e]["raw"]), 0.0)
            ddt_ref[e] = draw.astype(BF16)
            dsm_ref[e, 0, 0:1, :] += dskip_row
            dsm_ref[e, 0, 1:2, :] += da * q[e]["a"]
            dsm_ref[e, 0, 2:3, :] += jnp.sum(draw, axis=0, keepdims=True)
            dc_ref[e] = dc_acc + _dot(dcb, bm[e])
            db_ref[e] = db_acc + _dot_tn(dcb, cm[e])

    grp_out = pl.BlockSpec((bsz, CHUNK, D_STATE), lambda g, j: (0, nc - 1 - j, g))
    grid = (N_GROUPS, nc)
    ride = _Ride(rider, body, 12, 7, 1, grid)
    outs = pl.pallas_call(
        ride.body, name=name, grid=grid,
        in_specs=[sp["xs"], sp["bm"], sp["cm"], sp["lane_blk"], sp["xs"], sp["xs"], sp["state"], sp["xs"],
                  sp["grp_const"], sp["grp_const"], sp["grp_vec"], sp["grp_vec"]] + ride.in_specs,
        out_specs=[sp["xs"], sp["xs"], grp_out, grp_out, sp["lane_blk"],
                   pl.BlockSpec((bsz, 1, GW), lambda g, j: (0, 0, g)),
                   pl.BlockSpec((bsz, 1, 8, LANES), lambda g, j: (0, g, 0, 0))] + ride.out_specs,
        out_shape=[jax.ShapeDtypeStruct((bsz, t, D_SSM), BF16), jax.ShapeDtypeStruct((bsz, t, D_SSM), F32),
                   jax.ShapeDtypeStruct((bsz, t, N_GROUPS * D_STATE), F32),
                   jax.ShapeDtypeStruct((bsz, t, N_GROUPS * D_STATE), F32),
                   jax.ShapeDtypeStruct((bsz, t, D_DT), BF16), jax.ShapeDtypeStruct((bsz, 1, D_SSM), F32),
                   jax.ShapeDtypeStruct((bsz, N_GROUPS, 8, LANES), F32)] + ride.out_shape,
        scratch_shapes=[pltpu.VMEM((bsz, D_STATE, GW), F32)] + ride.scratch,
        compiler_params=_params(*ride.semantics(("parallel", "arbitrary"))),
    )(xc, xc, xc, dtr, z, ypre, sprev, dyn, dtb, alog, dskip, normw, *ride.args)
    return outs[:7], outs[7:]


def _input_grad(dhn, h0, w, dres, seq, *, name):
    bsz, t, d = h0.shape
    nc = t // CHUNK

    def body(dy_ref, h_ref, w_ref, dres_ref, gx_ref, head_ref, dw_ref):
        j = pl.program_id(1)

        @pl.when((pl.program_id(0) == 0) & (j == 0))
        def _():
            dw_ref[...] = jnp.zeros_like(dw_ref)

        x, dyv = h_ref[0], dy_ref[0]
        r = lax.rsqrt(jnp.mean(x * x, axis=-1, keepdims=True) + EPS)
        g = dyv * w_ref[...]
        dx = r * (g - x * (r * r) * jnp.mean(g * x, axis=-1, keepdims=True)) + dres_ref[0]
        dw_ref[...] += jnp.sum(dyv * x * r, axis=0, keepdims=True)

        @pl.when(j == 0)
        def _():
            head_ref[0] = dx

        gx_ref[0] = dx

    row = pl.BlockSpec((1, CHUNK, d), lambda b, j: (b, j, 0))
    return pl.pallas_call(
        body, name=name, grid=(bsz, nc),
        in_specs=[row, row, pl.BlockSpec((1, d), lambda b, j: (0, 0)), row],
        out_specs=[pl.BlockSpec((1, CHUNK, d), lambda b, j: (b, jnp.maximum(j - 1, 0), 0)),
                   pl.BlockSpec((1, CHUNK, d), lambda b, j: (b, 0, 0)), pl.BlockSpec((1, d), lambda b, j: (0, 0))],
        out_shape=[jax.ShapeDtypeStruct((bsz, seq, d), F32), jax.ShapeDtypeStruct((bsz, CHUNK, d), F32),
                   jax.ShapeDtypeStruct((1, d), F32)],
        compiler_params=_params("arbitrary", "arbitrary"),
    )(dhn, h0, w, dres)


def _remote(src, dst, send_sem, recv_sem, dev):
    return pltpu.make_async_remote_copy(src_ref=src, dst_ref=dst, send_sem=send_sem, recv_sem=recv_sem,
                                        device_id=dev, device_id_type=MESH)


def _position():
    return lax.axis_index("x"), lax.axis_index("y"), lax.axis_index("c")


def _other_chips(pos):
    x, y, _ = pos
    return [(1 - x, y), (x, 1 - y), (1 - x, 1 - y)]


class _Gather:
    def __init__(self, arrs):
        n = len(arrs)
        self.args, self.n_in, self.n_out = list(arrs), n, n
        self.out_shape = [jax.ShapeDtypeStruct((4,) + a.shape, a.dtype) for a in arrs]
        self.scratch = [pltpu.SemaphoreType.DMA((3 * n,)), pltpu.SemaphoreType.DMA((3 * n,)),
                        pltpu.SemaphoreType.DMA((n,))]

    def _copies(self, pos, ins, outs, sems):
        send_sems, recv_sems, loc_sems = sems
        x, y, c = pos
        me = 2 * x + y
        local = [pltpu.make_async_copy(ins[i], outs[i].at[me], loc_sems.at[i]) for i in range(self.n_in)]
        sends, recvs = [], []
        for i in range(self.n_in):
            for k, (px, py) in enumerate(_other_chips(pos)):
                sems_k = (send_sems.at[3 * i + k], recv_sems.at[3 * i + k], (px, py, c))
                sends.append(_remote(ins[i], outs[i].at[me], *sems_k))
                recvs.append(_remote(ins[i], outs[i].at[2 * px + py], *sems_k))
        return local, sends, recvs

    def start(self, pos, ins, outs, sems):
        local, sends, _ = self._copies(pos, ins, outs, sems)
        for cp in local + sends:
            cp.start()

    def finish(self, pos, ins, outs, sems):
        local, sends, recvs = self._copies(pos, ins, outs, sems)
        for cp in recvs:
            cp.wait_recv()
        for cp in sends:
            cp.wait_send()
        for cp in local:
            cp.wait()


class _Exchange:
    FLIPS = [(fx, fy, fc) for fx in (0, 1) for fy in (0, 1) for fc in (0, 1)][1:]

    def __init__(self, big, small=None):
        n = len(big)
        self.n_big, self.has_small = n, small is not None
        self.args = list(big) + ([small] if self.has_small else [])
        self.n_in = self.n_out = len(self.args)
        self.out_shape = [jax.ShapeDtypeStruct(a.shape, a.dtype) for a in big]
        self.scratch = [pltpu.SemaphoreType.DMA((max(3 * n, 1),)), pltpu.SemaphoreType.DMA((max(3 * n, 1),))]
        if self.has_small:
            self.out_shape.append(jax.ShapeDtypeStruct((8,) + small.shape, small.dtype))
            self.scratch += [pltpu.SemaphoreType.DMA((7,)), pltpu.SemaphoreType.DMA((7,)), pltpu.SemaphoreType.DMA((1,))]

    def _copies(self, pos, ins, outs, sems):
        x, y, c = pos
        me, me8 = 2 * x + y, 4 * x + 2 * y + c
        local, sends, recvs = [], [], []
        for i in range(self.n_big):
            for k, (px, py) in enumerate(_other_chips(pos)):
                sems_k = (sems[0].at[3 * i + k], sems[1].at[3 * i + k], (px, py, c))
                sends.append(_remote(ins[i].at[2 * px + py], outs[i].at[me], *sems_k))
                recvs.append(_remote(ins[i].at[me], outs[i].at[2 * px + py], *sems_k))
        if self.has_small:
            small, landed = ins[self.n_big], outs[self.n_big]
            local.append(pltpu.make_async_copy(small, landed.at[me8], sems[4].at[0]))
            for k, (fx, fy, fc) in enumerate(self.FLIPS):
                peer = (x ^ fx, y ^ fy, c ^ fc)
                sems_k = (sems[2].at[k], sems[3].at[k], peer)
                sends.append(_remote(small, landed.at[me8], *sems_k))
                recvs.append(_remote(small, landed.at[4 * peer[0] + 2 * peer[1] + peer[2]], *sems_k))
        return local, sends, recvs

    start = _Gather.start
    finish = _Gather.finish


class _Swap:
    def __init__(self, arrs):
        n = len(arrs)
        self.args, self.n_in, self.n_out = list(arrs), n, n
        self.out_shape = [jax.ShapeDtypeStruct(a.shape, a.dtype) for a in arrs]
        self.scratch = [pltpu.SemaphoreType.DMA((n,)), pltpu.SemaphoreType.DMA((n,))]

    def _copies(self, pos, ins, outs, sems):
        x, y, c = pos
        both = [_remote(ins[i], outs[i], sems[0].at[i], sems[1].at[i], (x, y, 1 - c)) for i in range(self.n_in)]
        return [], both, both

    start = _Gather.start
    finish = _Gather.finish


def _comm(rider, *, name):
    a, b = rider.n_in, rider.n_in + rider.n_out

    def body(*refs):
        pos = _position()
        rider.start(pos, refs[:a], refs[a:b], refs[b:])
        rider.finish(pos, refs[:a], refs[a:b], refs[b:])

    return pl.pallas_call(body, name=name, in_specs=[ANY] * rider.n_in, out_specs=[ANY] * rider.n_out,
                          out_shape=rider.out_shape, scratch_shapes=rider.scratch)(*rider.args)


class _Ride:
    def __init__(self, rider, body, n_in, n_out, n_scratch, grid):
        self.rider = rider
        self.args = rider.args if rider else []
        self.in_specs = [ANY] * rider.n_in if rider else []
        self.out_specs = [ANY] * rider.n_out if rider else []
        self.out_shape = rider.out_shape if rider else []
        self.scratch = rider.scratch if rider else []
        self.body = self._wrap(body, n_in, n_out, n_scratch, grid) if rider else body

    def semantics(self, sem):
        return ("arbitrary",) * len(sem) if self.rider else sem

    def _wrap(self, body, n_in, n_out, n_scratch, grid):
        rider = self.rider
        a = n_in
        b = a + rider.n_in
        c = b + n_out
        d = c + rider.n_out
        e = d + n_scratch

        def wrapped(*refs):
            pos = _position()
            ids = [pl.program_id(i) for i in range(len(grid))]
            first = functools.reduce(jnp.logical_and, [i == 0 for i in ids])
            last = functools.reduce(jnp.logical_and, [i == g - 1 for i, g in zip(ids, grid)])

            @pl.when(first)
            def _():
                rider.start(pos, refs[a:b], refs[c:d], refs[e:])

            body(*refs[:a], *refs[b:c], *refs[d:e])

            @pl.when(last)
            def _():
                rider.finish(pos, refs[a:b], refs[c:d], refs[e:])

        return wrapped


def _chip_sum(own, landed, *, name):
    r, c = own.shape
    tm = _pick(r, (256, 128, 64, 8))

    def body(own_ref, land_ref, o_ref):
        me = 2 * lax.axis_index("x") + lax.axis_index("y")
        acc = None
        for jchip in range(4):
            term = jnp.where(me == jchip, own_ref[...], land_ref[jchip].astype(F32))
            acc = term if acc is None else acc + term
        o_ref[...] = acc

    return pl.pallas_call(
        body, name=name, grid=(r // tm,),
        in_specs=[pl.BlockSpec((tm, c), lambda i: (i, 0)), pl.BlockSpec((4, tm, c), lambda i: (0, i, 0))],
        out_specs=pl.BlockSpec((tm, c), lambda i: (i, 0)), out_shape=jax.ShapeDtypeStruct((r, c), F32),
        compiler_params=_params("parallel"),
    )(own, landed)


def _device_sum(parts, *, name):
    _, r, c = parts.shape

    def body(p_ref, o_ref):
        acc = p_ref[0]
        for d in range(1, 8):
            acc = acc + p_ref[d]
        o_ref[...] = acc

    return pl.pallas_call(body, name=name, out_shape=jax.ShapeDtypeStruct((r, c), F32))(parts)


def _adamw_math(w, g, m, v):
    m = ADAM_B1 * m + (1.0 - ADAM_B1) * g
    v = ADAM_B2 * v + (1.0 - ADAM_B2) * (g * g)
    m_hat = m / (1.0 - ADAM_B1 ** ADAM_STEP)
    v_hat = v / (1.0 - ADAM_B2 ** ADAM_STEP)
    return -ADAM_LR * (m_hat / (jnp.sqrt(v_hat) + ADAM_EPS) + ADAM_WD * w), m, v


def _adamw(w, g_parts, m, v, *, name):
    r, c = w.shape
    tm = _pick(r, (256, 128, 64, 16, 8, 4, 1)) if r * c > 65536 else r
    n_g = len(g_parts)

    def body(*refs):
        w_ref, m_ref, v_ref = refs[n_g:n_g + 3]
        g_ref, d_ref, nm_ref, nv_ref = refs[n_g + 3:]
        g = refs[0][...]
        for p in refs[1:n_g]:
            g = g + p[...]
        g_ref[...] = g
        d_ref[...], nm_ref[...], nv_ref[...] = _adamw_math(w_ref[...], g, m_ref[...], v_ref[...])

    blk = pl.BlockSpec((tm, c), lambda i: (i, 0))
    return pl.pallas_call(
        body, name=name, grid=(r // tm,), in_specs=[blk] * (n_g + 3), out_specs=[blk] * 4,
        out_shape=[jax.ShapeDtypeStruct((r, c), F32)] * 4, compiler_params=_params("parallel"),
    )(*g_parts, w, m, v)


def _pad_heads(v):
    return jnp.pad(v.reshape(N_GROUPS, 1, HPG), ((0, 0), (0, 0), (0, LANES - HPG)))


def _unpad_heads(v):
    return v[:, :HPG].reshape(1, N_HEADS)


_SMALL = [("norm_mix_w", (1, 1024)), ("pool_w", (512, 128)), ("pool_scale", (1, 512)), ("conv_w", (4, D_XBC)),
          ("conv_b", (1, D_XBC)), ("dt_bias", (1, N_HEADS)), ("a_log", (1, N_HEADS)), ("d_skip", (1, N_HEADS)),
          ("ssm_norm_w", (1, D_SSM)), ("norm_ffn_w", (1, 1024)), ("norm_f_w", (1, 1024)), ("meta", (N_META, 1024))]


def _pack_small(grads):
    rows = []
    for nm, shape in _SMALL:
        flat = grads[nm].reshape(-1)
        rows.append(jnp.pad(flat, (0, (-flat.size) % LANES)).reshape(-1, LANES))
    packed = jnp.concatenate(rows, axis=0)
    return jnp.pad(packed, ((0, (-packed.shape[0]) % 8), (0, 0)))


def _unpack_small(packed):
    out, r0 = {}, 0
    for nm, shape in _SMALL:
        size = shape[0] * shape[1]
        nrow = -(-size // LANES)
        out[nm] = packed[r0:r0 + nrow].reshape(-1)[:size].reshape(shape)
        r0 += nrow
    return out


def kernel(x, meta, norm_mix_w, w_in, pool_w, pool_scale, conv_w, conv_b, dt_bias, a_log, d_skip, ssm_norm_w, w_out, norm_ffn_w, w_ff1, w_ff2, norm_f_w, loss_target, m_meta, m_norm_mix_w, m_w_in, m_pool_w, m_pool_scale, m_conv_w, m_conv_b, m_dt_bias, m_a_log, m_d_skip, m_ssm_norm_w, m_w_out, m_norm_ffn_w, m_w_ff1, m_w_ff2, m_norm_f_w, v_meta, v_norm_mix_w, v_w_in, v_pool_w, v_pool_scale, v_conv_w, v_conv_b, v_dt_bias, v_a_log, v_d_skip, v_ssm_norm_w, v_w_out, v_norm_ffn_w, v_w_ff1, v_w_ff2, v_norm_f_w):
    bsz, seq, d = x.shape
    t = seq + CHUNK
    n = bsz * t
    chip = 2 * lax.axis_index("x") + lax.axis_index("y")
    d_in = w_in.shape[2] * 4

    g_in, g_conv, g_meta = _comm(_Gather([w_in[0].astype(BF16), conv_w[0], meta]), name="gather_in")
    late_weights = _Gather([w_out[0].astype(BF16), w_ff1[0].astype(BF16), w_ff2[0].astype(BF16)])
    win = g_in.transpose(1, 0, 2).reshape(d, d_in)
    wu, wz = win[:, :D_POOL], win[:, D_POOL:D_POOL + D_SSM]
    wx = win[:, D_POOL + D_SSM:D_POOL + D_SSM + D_XBC]
    wdt = jnp.pad(win[:, D_POOL + D_SSM + D_XBC:].reshape(d, N_GROUPS, HPG),
                  ((0, 0), (0, 0), (0, LANES - HPG))).reshape(d, D_DT)
    convw = g_conv.transpose(1, 0, 2).reshape(CONV_W, D_XBC)
    meta_full = g_meta.transpose(1, 0, 2).reshape(N_META, d)
    dtb, alog = _pad_heads(dt_bias), _pad_heads(a_log)
    dskip = jnp.repeat(d_skip, HEAD_DIM, axis=1)
    poolw = pool_w[0]

    h0 = jnp.concatenate([jnp.zeros((bsz, PAD, d), F32), jnp.broadcast_to(meta_full[None], (bsz, N_META, d)), x], axis=1)
    h0f = h0.reshape(n, d)
    hn1 = _rms_fwd(h0f, norm_mix_w, name="norm_mix")
    u = _mm(hn1, wu, name="proj_u")
    z = _mm(hn1, wz, name="proj_z")
    xbc = _mm(hn1, wx, name="proj_xbc")
    dtr = _mm(hn1, wdt, name="proj_dt")
    ypool = _pool_fwd(u.reshape(bsz, t, D_POOL), poolw, pool_scale, name="pool_fwd")
    xbc3 = xbc.reshape(bsz, t, D_XBC)
    xc = _conv_fwd(xbc3, convw, conv_b, name="conv_fwd")
    z3, dtr3 = z.reshape(bsz, t, D_SSM), dtr.reshape(bsz, t, D_DT)
    (yn, ypre, sprev), (g_out, g_ff1, g_ff2) = _ssd_fwd(xc, dtr3, z3, dtb, alog, dskip, ssm_norm_w, name="ssd_fwd",
                                                        rider=late_weights)
    wo = g_out.reshape(D_POOL + D_SSM, d)
    wo_p, wo_s = wo[:D_POOL], wo[D_POOL:]
    w1 = g_ff1.transpose(1, 0, 2).reshape(d, D_FF)
    w2 = g_ff2.reshape(D_FF, d)
    ypool_f, yn_f = ypool.reshape(n, D_POOL), yn.reshape(n, D_SSM)
    add = lambda r, e: r + e
    h1 = _mm([ypool_f, yn_f], [wo_p, wo_s], name="out_proj", post=add, extras=(h0f,))
    hn2 = _rms_fwd(h1, norm_ffn_w, name="norm_ffn")
    act = _mm(hn2, w1, name="ff1", out_dtype=BF16)
    relu2 = lambda a: jnp.square(jnp.maximum(a, 0))
    h2 = _mm(act, w2, name="ff2", pre=relu2, post=add, extras=(h1,))
    dh2, dh2b, loss_acc, d_norm_f = _final_norm_loss(h2.reshape(bsz, t, d), loss_target, norm_f_w.reshape(1, d),
                                                     name="loss")
    loss = lax.psum(loss_acc[0, 0], ("x", "y", "c"))

    dh2f, dh2bf = dh2.reshape(n, d), dh2b.reshape(n, d)
    dact = _mm(dh2bf, w2, name="ff2_bwd", nt=True, post=lambda r, a: r * (2.0 * jnp.maximum(a, 0).astype(F32)),
               extras=(act,), out_dtype=BF16)
    d_w2 = _mm_tn(act, dh2bf, name="ff2_dw", tk=2048, tn=1024, pre=relu2)
    d_w1 = _mm_tn(hn2, dact, name="ff1_dw", tk=1024, tn=2048)
    dhn2 = _mm(dact, w1, name="ff1_bwd", nt=True)
    dh1, dh1b, d_norm_ffn = _rms_bwd(dhn2, h1, norm_ffn_w, dh2f, name="norm_ffn_bwd")
    dypool = _mm(dh1b, wo_p, name="out_pool_bwd", nt=True)
    dyn = _mm(dh1b, wo_s, name="out_ssm_bwd", nt=True)
    d_wo_p = _mm_tn(ypool_f, dh1b, name="out_pool_dw", tk=512, tn=1024)
    d_wo_s = _mm_tn(yn_f, dh1b, name="out_ssm_dw", tk=1536, tn=1024)
    big_late = [jnp.concatenate([d_wo_p, d_wo_s], axis=0).reshape(4, (D_POOL + D_SSM) // 4, d),
                d_w1.reshape(d, 4, D_FF // 4).transpose(1, 0, 2), d_w2.reshape(4, D_FF // 4, d)]
    (dz, dxs, dbm, dcm, ddtr, d_nw, d_heads), landed_late = _ssd_bwd(
        xc, dtr3, z3, ypre, sprev, dyn.reshape(bsz, t, D_SSM), dtb, alog, dskip, ssm_norm_w, name="ssd_bwd",
        rider=_Exchange([b.astype(BF16) for b in big_late]))
    dpre, d_convwb = _conv_bwd_pre(xbc3, dxs, dbm, dcm, convw, conv_b, name="conv_bwd_pre")
    dxbc = _conv_bwd_in(dpre, convw, name="conv_bwd_in")
    du, d_poolw, d_poolsc = _pool_bwd(u.reshape(bsz, t, D_POOL), dypool.reshape(bsz, t, D_POOL), poolw, pool_scale,
                                      name="pool_bwd")
    duf, dzf, dxbcf, ddtrf = du.reshape(n, D_POOL), dz.reshape(n, D_SSM), dxbc.reshape(n, D_XBC), ddtr.reshape(n, D_DT)
    d_wu = _mm_tn(hn1, duf, name="proj_u_dw", tk=1024, tn=512)
    d_wz = _mm_tn(hn1, dzf, name="proj_z_dw", tk=1024, tn=1536)
    d_wx = _mm_tn(hn1, dxbcf, name="proj_xbc_dw", tk=1024, tn=1280)
    d_wdt = _mm_tn(hn1, ddtrf, name="proj_dt_dw", tk=1024, tn=512)
    d_win = jnp.concatenate([d_wu, d_wz, d_wx, d_wdt.reshape(d, N_GROUPS, LANES)[:, :, :HPG].reshape(d, N_HEADS)], axis=1)
    big_in = d_win.reshape(d, 4, d_in // 4).transpose(1, 0, 2)
    dhn1, landed_in = _mm([duf, dzf, dxbcf, ddtrf], [wu, wz, wx, wdt], name="proj_bwd", nt=True,
                          rider=_Exchange([big_in.astype(BF16)]))
    grad_x, d_head_rows, d_norm_mix = _input_grad(
        dhn1.reshape(bsz, t, d), h0, norm_mix_w, dh1.reshape(bsz, t, d), seq, name="input_grad")

    big = [big_in] + big_late
    landed = list(landed_in) + list(landed_late)
    heads = jnp.sum(d_heads, axis=0)
    small = _pack_small({
        "norm_mix_w": d_norm_mix, "pool_w": d_poolw, "pool_scale": d_poolsc,
        "conv_w": jnp.sum(d_convwb[:, :CONV_W], axis=0), "conv_b": jnp.sum(d_convwb[:, CONV_W:CONV_W + 1], axis=0),
        "dt_bias": _unpad_heads(heads[:, 2]), "a_log": _unpad_heads(heads[:, 1]), "d_skip": _unpad_heads(heads[:, 0]),
        "ssm_norm_w": jnp.sum(d_nw, axis=0), "norm_ffn_w": d_norm_ffn, "norm_f_w": d_norm_f,
        "meta": jnp.sum(d_head_rows[:, PAD:], axis=0)})
    (small_all,) = _comm(_Exchange([], small), name="exchange_small")
    own = [lax.dynamic_index_in_dim(b, chip, 0, keepdims=False) for b in big]
    mine = [_chip_sum(o, l, name=f"chip_sum_{i}") for i, (o, l) in enumerate(zip(own, landed))]
    theirs = _comm(_Swap(mine), name="swap_cores")
    gsmall = _unpack_small(_device_sum(small_all, name="device_sum"))
    gsmall["conv_w"] = lax.dynamic_slice_in_dim(gsmall["conv_w"], chip * (D_XBC // 4), D_XBC // 4, axis=1)
    gsmall["meta"] = lax.dynamic_slice_in_dim(gsmall["meta"], chip * (d // 4), d // 4, axis=1)

    given = dict(meta=(meta, m_meta, v_meta), norm_mix_w=(norm_mix_w, m_norm_mix_w, v_norm_mix_w),
                 w_in=(w_in, m_w_in, v_w_in), pool_w=(pool_w, m_pool_w, v_pool_w),
                 pool_scale=(pool_scale, m_pool_scale, v_pool_scale), conv_w=(conv_w, m_conv_w, v_conv_w),
                 conv_b=(conv_b, m_conv_b, v_conv_b), dt_bias=(dt_bias, m_dt_bias, v_dt_bias),
                 a_log=(a_log, m_a_log, v_a_log), d_skip=(d_skip, m_d_skip, v_d_skip),
                 ssm_norm_w=(ssm_norm_w, m_ssm_norm_w, v_ssm_norm_w), w_out=(w_out, m_w_out, v_w_out),
                 norm_ffn_w=(norm_ffn_w, m_norm_ffn_w, v_norm_ffn_w), w_ff1=(w_ff1, m_w_ff1, v_w_ff1),
                 w_ff2=(w_ff2, m_w_ff2, v_w_ff2), norm_f_w=(norm_f_w, m_norm_f_w, v_norm_f_w))
    big_names = ["w_in", "w_out", "w_ff1", "w_ff2"]
    results = {}
    for nm, (w, m, v) in given.items():
        if nm in big_names:
            i = big_names.index(nm)
            parts, shape2 = (mine[i], theirs[i]), mine[i].shape
        else:
            parts, shape2 = (gsmall[nm],), gsmall[nm].shape
        outs = _adamw(w.reshape(shape2), parts, m.reshape(shape2), v.reshape(shape2), name=f"adamw_{nm}")
        results[nm] = [o.reshape(w.shape) for o in outs]
    order = list(given)
    return (loss, grad_x, *[results[nm][0] for nm in order], *[results[nm][1] for nm in order],
            *[results[nm][2] for nm in order], *[results[nm][3] for nm in order])
```

```python
import functools

import jax
import jax.numpy as jnp
from jax import lax
from jax.experimental import pallas as pl
from jax.experimental.pallas import tpu as pltpu

F32 = jnp.float32
BF16 = jnp.bfloat16
MESH = pl.DeviceIdType.MESH
ANY = pl.BlockSpec(memory_space=pl.ANY)

D_MODEL = 1024
N_META = 16
CHUNK = 128
PAD = CHUNK - N_META
POOL_WINDOWS = (2, 4, 8, 16)
D_POOL = 512
POOL_GROUP = 128
D_SSM = 1536
N_HEADS = 24
N_GROUPS = 4
HPG = 6
HEAD_DIM = 64
D_STATE = 128
GW = HPG * HEAD_DIM
D_XBC = D_SSM + 2 * N_GROUPS * D_STATE
D_DT = N_GROUPS * 128
D_FF = 4096
CONV_W = 4
EPS = 1e-5
LANES = 128
VMEM_LIMIT = 56 * 1024 * 1024

ADAM_LR, ADAM_B1, ADAM_B2, ADAM_EPS, ADAM_WD, ADAM_STEP = 0.001, 0.9, 0.999, 1e-08, 0.01, 10


def _params(*sem):
    return pltpu.CompilerParams(dimension_semantics=sem, vmem_limit_bytes=VMEM_LIMIT)


def _pick(n, cands):
    for c in cands:
        if n % c == 0:
            return c
    raise ValueError(f"no block size for {n}")


def _dot(a, b):
    return jnp.dot(a.astype(BF16), b.astype(BF16), preferred_element_type=F32)


def _dot_nt(a, b):
    return lax.dot_general(a.astype(BF16), b.astype(BF16), (((1,), (1,)), ((), ())), preferred_element_type=F32)


def _dot_tn(a, b):
    return lax.dot_general(a.astype(BF16), b.astype(BF16), (((0,), (0,)), ((), ())), preferred_element_type=F32)


def _dot_exact(mask, x):
    m = mask.astype(BF16)
    hi = x.astype(BF16)
    r1 = x - hi.astype(F32)
    mid = r1.astype(BF16)
    lo = (r1 - mid.astype(F32)).astype(BF16)
    dot = lambda t: jnp.dot(m, t, preferred_element_type=F32)
    return dot(hi) + dot(mid) + dot(lo)


def _sigmoid(x):
    return 1.0 / (1.0 + jnp.exp(-x))


def _softplus(x):
    return jnp.maximum(x, 0.0) + jnp.log1p(jnp.exp(-jnp.abs(x)))


def _sum_all(x):
    return jnp.sum(jnp.sum(x, axis=1, keepdims=True), axis=0, keepdims=True)


ROW_TILES = (4224, 2816, 2112, 1408, 1056, 768, 704, 512, 384, 256, 128)
TILE_BUDGET = 28 * 1024 * 1024


def _row_tile(n, bytes_per_row, fixed_bytes):
    for tm in ROW_TILES:
        if n % tm == 0 and 2 * (tm * bytes_per_row + fixed_bytes) <= TILE_BUDGET:
            return tm
    raise ValueError(f"no row tile for {n}")


def _mm(a, w, *, name, tn=512, nt=False, pre=None, post=None, extras=(), out_dtype=F32, rider=None):
    a_list = list(a) if isinstance(a, (list, tuple)) else [a]
    w_list = list(w) if isinstance(w, (list, tuple)) else [w]
    n_a, n_ex = len(a_list), len(extras)
    n = a_list[0].shape[0]
    m = w_list[0].shape[0] if nt else w_list[0].shape[1]
    tn = min(tn, m)
    size = lambda dt: jnp.dtype(dt).itemsize
    per_row = (sum(x.shape[1] * size(x.dtype) for x in a_list) + tn * size(out_dtype)
               + sum(tn * size(e.dtype) for e in extras))
    tm = _row_tile(n, per_row, sum(x.shape[1 if nt else 0] * tn * size(x.dtype) for x in w_list))

    def body(*refs):
        a_refs, w_refs, ex_refs, o_ref = refs[:n_a], refs[n_a:2 * n_a], refs[2 * n_a:2 * n_a + n_ex], refs[2 * n_a + n_ex]
        r = None
        for a_ref, w_ref in zip(a_refs, w_refs):
            av = a_ref[...]
            if pre is not None:
                av = pre(av)
            term = _dot_nt(av, w_ref[...]) if nt else _dot(av, w_ref[...])
            r = term if r is None else r + term
        if post is not None:
            r = post(r, *[e[...] for e in ex_refs])
        o_ref[...] = r.astype(out_dtype)

    a_specs = [pl.BlockSpec((tm, x.shape[1]), lambda i, j: (i, 0)) for x in a_list]
    w_specs = [pl.BlockSpec((tn, x.shape[1]), lambda i, j: (j, 0)) if nt else pl.BlockSpec((x.shape[0], tn), lambda i, j: (0, j))
               for x in w_list]
    blk = pl.BlockSpec((tm, tn), lambda i, j: (i, j))
    grid = (n // tm, m // tn)
    ride = _Ride(rider, body, 2 * n_a + n_ex, 1, 0, grid)
    outs = pl.pallas_call(
        ride.body, name=name, grid=grid,
        in_specs=a_specs + w_specs + [blk] * n_ex + ride.in_specs,
        out_specs=[blk] + ride.out_specs, out_shape=[jax.ShapeDtypeStruct((n, m), out_dtype)] + ride.out_shape,
        scratch_shapes=ride.scratch, compiler_params=_params(*ride.semantics(("parallel", "parallel"))),
    )(*a_list, *w_list, *extras, *ride.args)
    return (outs[0], outs[1:]) if rider else outs[0]


def _mm_tn(a, g, *, name, tk, tn, pre=None):
    n, k = a.shape
    m = g.shape[1]
    tk, tn = min(tk, k), min(tn, m)
    tm = _row_tile(n, tk * jnp.dtype(a.dtype).itemsize + tn * jnp.dtype(g.dtype).itemsize, tk * tn * 4)

    def body(a_ref, g_ref, o_ref):
        @pl.when(pl.program_id(2) == 0)
        def _():
            o_ref[...] = jnp.zeros_like(o_ref)

        av = a_ref[...]
        if pre is not None:
            av = pre(av)
        o_ref[...] += _dot_tn(av, g_ref[...])

    return pl.pallas_call(
        body, name=name, grid=(k // tk, m // tn, n // tm),
        in_specs=[pl.BlockSpec((tm, tk), lambda i, j, r: (r, i)), pl.BlockSpec((tm, tn), lambda i, j, r: (r, j))],
        out_specs=pl.BlockSpec((tk, tn), lambda i, j, r: (i, j)),
        out_shape=jax.ShapeDtypeStruct((k, m), F32),
        compiler_params=_params("parallel", "parallel", "arbitrary"),
    )(a, g)


def _rms_fwd(h, w, *, name):
    n, d = h.shape
    tm = _pick(n, (768, 512, 256, 128))

    def body(h_ref, w_ref, o_ref):
        x = h_ref[...]
        r = lax.rsqrt(jnp.mean(x * x, axis=-1, keepdims=True) + EPS)
        o_ref[...] = (x * r * w_ref[...]).astype(BF16)

    return pl.pallas_call(
        body, name=name, grid=(n // tm,),
        in_specs=[pl.BlockSpec((tm, d), lambda i: (i, 0)), pl.BlockSpec((1, d), lambda i: (0, 0))],
        out_specs=pl.BlockSpec((tm, d), lambda i: (i, 0)), out_shape=jax.ShapeDtypeStruct((n, d), BF16),
        compiler_params=_params("parallel"),
    )(h, w)


def _rms_bwd(dy, h, w, dres, *, name):
    n, d = h.shape
    tm = _pick(n, (768, 512, 256, 128))

    def body(dy_ref, h_ref, w_ref, dres_ref, dx_ref, dxb_ref, dw_ref):
        @pl.when(pl.program_id(0) == 0)
        def _():
            dw_ref[...] = jnp.zeros_like(dw_ref)

        x, dyv = h_ref[...], dy_ref[...]
        r = lax.rsqrt(jnp.mean(x * x, axis=-1, keepdims=True) + EPS)
        g = dyv * w_ref[...]
        dx = r * (g - x * (r * r) * jnp.mean(g * x, axis=-1, keepdims=True)) + dres_ref[...]
        dx_ref[...] = dx
        dxb_ref[...] = dx.astype(BF16)
        dw_ref[...] += jnp.sum(dyv * x * r, axis=0, keepdims=True)

    row = pl.BlockSpec((tm, d), lambda i: (i, 0))
    vec = pl.BlockSpec((1, d), lambda i: (0, 0))
    return pl.pallas_call(
        body, name=name, grid=(n // tm,), in_specs=[row, row, vec, row], out_specs=[row, row, vec],
        out_shape=[jax.ShapeDtypeStruct((n, d), F32), jax.ShapeDtypeStruct((n, d), BF16), jax.ShapeDtypeStruct((1, d), F32)],
        compiler_params=_params("arbitrary"),
    )(dy, h, w, dres)


def _final_norm_loss(h2, target, w, *, name):
    bsz, t, d = h2.shape
    nc = t // CHUNK

    def body(h_ref, t_ref, w_ref, dh_ref, dhb_ref, loss_ref, dw_ref):
        j = pl.program_id(1)

        @pl.when((pl.program_id(0) == 0) & (j == 0))
        def _():
            loss_ref[...] = jnp.zeros_like(loss_ref)
            dw_ref[...] = jnp.zeros_like(dw_ref)

        x, wv = h_ref[0], w_ref[...]
        r = lax.rsqrt(jnp.mean(x * x, axis=-1, keepdims=True) + EPS)
        diff = jnp.where(j > 0, x * r * wv - t_ref[0], 0.0)
        loss_ref[...] += _sum_all(diff * diff) * (0.5 / d)
        dy = diff * (1.0 / d)
        g = dy * wv
        dh = r * (g - x * (r * r) * jnp.mean(g * x, axis=-1, keepdims=True))
        dh_ref[0] = dh
        dhb_ref[0] = dh.astype(BF16)
        dw_ref[...] += jnp.sum(dy * x * r, axis=0, keepdims=True)

    row = pl.BlockSpec((1, CHUNK, d), lambda b, j: (b, j, 0))
    return pl.pallas_call(
        body, name=name, grid=(bsz, nc),
        in_specs=[row, pl.BlockSpec((1, CHUNK, d), lambda b, j: (b, jnp.maximum(j - 1, 0), 0)),
                  pl.BlockSpec((1, d), lambda b, j: (0, 0))],
        out_specs=[row, row, pl.BlockSpec((8, LANES), lambda b, j: (0, 0)), pl.BlockSpec((1, d), lambda b, j: (0, 0))],
        out_shape=[jax.ShapeDtypeStruct((bsz, t, d), F32), jax.ShapeDtypeStruct((bsz, t, d), BF16),
                   jax.ShapeDtypeStruct((8, LANES), F32), jax.ShapeDtypeStruct((1, d), F32)],
        compiler_params=_params("arbitrary", "arbitrary"),
    )(h2, target, w)


def _pool_masks(j, transposed):
    r = lax.broadcasted_iota(jnp.int32, (CHUNK, 2 * CHUNK), 0)
    c = lax.broadcasted_iota(jnp.int32, (CHUNK, 2 * CHUNK), 1)
    masks = []
    for w in POOL_WINDOWS:
        if transposed:
            m = (c >= r) & (c < r + w)
        else:
            s = c - CHUNK
            m = (s <= r) & (s > r - w) & (s + j * CHUNK >= 0)
        masks.append(m.astype(F32))
    return masks


def _pool_count(t_global, w):
    return jnp.clip(t_global - PAD + 1, 1, w).astype(F32)


def _pool_fwd(u, pool_w, pool_scale, *, name):
    bsz, t, _ = u.shape
    nc = t // CHUNK

    def body(prev_ref, cur_ref, pw_ref, sc_ref, o_ref):
        j = pl.program_id(0)
        masks = _pool_masks(j, False)
        tg = j * CHUNK + lax.broadcasted_iota(jnp.int32, (CHUNK, 1), 0)
        count = [_pool_count(tg, w) for w in POOL_WINDOWS]
        units = [(e, gi) for e in range(bsz) for gi in range(len(POOL_WINDOWS))]
        sl = lambda gi: pl.ds(gi * POOL_GROUP, POOL_GROUP)
        cur = {(e, gi): cur_ref[e, :, sl(gi)] for e, gi in units}
        both = {(e, gi): jnp.concatenate([prev_ref[e, :, sl(gi)], cur[e, gi]], axis=0) for e, gi in units}
        win = {(e, gi): _dot_exact(masks[gi], both[e, gi]) for e, gi in units}
        pooled = {(e, gi): win[e, gi] / count[gi] - cur[e, gi] for e, gi in units}
        mixed = {(e, gi): _dot(pooled[e, gi], pw_ref[gi]) for e, gi in units}
        for e, gi in units:
            o_ref[e, :, sl(gi)] = (mixed[e, gi] * sc_ref[:, sl(gi)]).astype(BF16)

    blk = lambda f: pl.BlockSpec((bsz, CHUNK, D_POOL), f)
    return pl.pallas_call(
        body, name=name, grid=(nc,),
        in_specs=[blk(lambda j: (0, jnp.maximum(j - 1, 0), 0)), blk(lambda j: (0, j, 0)),
                  pl.BlockSpec((4, POOL_GROUP, POOL_GROUP), lambda j: (0, 0, 0)),
                  pl.BlockSpec((1, D_POOL), lambda j: (0, 0))],
        out_specs=blk(lambda j: (0, j, 0)), out_shape=jax.ShapeDtypeStruct(u.shape, BF16),
        compiler_params=_params("parallel"),
    )(u, u, pool_w, pool_scale)


def _pool_bwd(u, dyp, pool_w, pool_scale, *, name):
    bsz, t, _ = u.shape
    nc = t // CHUNK

    def body(prev_ref, cur_ref, dy_ref, dyn_ref, pw_ref, sc_ref, du_ref, dpw_ref, dsc_ref):
        j = pl.program_id(0)

        @pl.when(j == 0)
        def _():
            dpw_ref[...] = jnp.zeros_like(dpw_ref)
            dsc_ref[...] = jnp.zeros_like(dsc_ref)

        fwd = _pool_masks(j, False)
        bwd = _pool_masks(j, True)
        tg = j * CHUNK + lax.broadcasted_iota(jnp.int32, (CHUNK, 1), 0)
        count = [_pool_count(tg, w) for w in POOL_WINDOWS]
        count_next = [_pool_count(tg + CHUNK, w) for w in POOL_WINDOWS]
        has_next = j < nc - 1
        groups = range(len(POOL_WINDOWS))
        units = [(e, gi) for e in range(bsz) for gi in groups]
        sl = lambda gi: pl.ds(gi * POOL_GROUP, POOL_GROUP)
        cur = {(e, gi): cur_ref[e, :, sl(gi)] for e, gi in units}
        both = {(e, gi): jnp.concatenate([prev_ref[e, :, sl(gi)], cur[e, gi]], axis=0) for e, gi in units}
        win = {(e, gi): _dot_exact(fwd[gi], both[e, gi]) for e, gi in units}
        pooled = {(e, gi): win[e, gi] / count[gi] - cur[e, gi] for e, gi in units}
        dy = {(e, gi): dy_ref[e, :, sl(gi)] for e, gi in units}
        mixed = {(e, gi): _dot(pooled[e, gi], pw_ref[gi]) for e, gi in units}
        dm = {(e, gi): dy[e, gi] * sc_ref[:, sl(gi)] for e, gi in units}
        dm_next = {(e, gi): jnp.where(has_next, dyn_ref[e, :, sl(gi)], 0.0) * sc_ref[:, sl(gi)] for e, gi in units}
        dpw = {(e, gi): _dot_tn(pooled[e, gi], dm[e, gi]) for e, gi in units}
        dpooled = {(e, gi): _dot_nt(dm[e, gi], pw_ref[gi]) for e, gi in units}
        dpooled_next = {(e, gi): _dot_nt(dm_next[e, gi], pw_ref[gi]) for e, gi in units}
        spread = {(e, gi): jnp.concatenate([dpooled[e, gi] / count[gi], dpooled_next[e, gi] / count_next[gi]], axis=0)
                  for e, gi in units}
        back = {(e, gi): _dot_exact(bwd[gi], spread[e, gi]) for e, gi in units}
        for e, gi in units:
            du_ref[e, :, sl(gi)] = (back[e, gi] - dpooled[e, gi]).astype(BF16)
        for gi in groups:
            dsc, dw = None, None
            for e in range(bsz):
                term = jnp.sum(dy[e, gi] * mixed[e, gi], axis=0, keepdims=True)
                dsc = term if dsc is None else dsc + term
                dw = dpw[e, gi] if dw is None else dw + dpw[e, gi]
            dsc_ref[:, sl(gi)] += dsc
            dpw_ref[gi] += dw

    blk = lambda f: pl.BlockSpec((bsz, CHUNK, D_POOL), f)
    return pl.pallas_call(
        body, name=name, grid=(nc,),
        in_specs=[blk(lambda j: (0, jnp.maximum(j - 1, 0), 0)), blk(lambda j: (0, j, 0)),
                  blk(lambda j: (0, j, 0)), blk(lambda j: (0, jnp.minimum(j + 1, nc - 1), 0)),
                  pl.BlockSpec((4, POOL_GROUP, POOL_GROUP), lambda j: (0, 0, 0)),
                  pl.BlockSpec((1, D_POOL), lambda j: (0, 0))],
        out_specs=[blk(lambda j: (0, j, 0)), pl.BlockSpec((4, POOL_GROUP, POOL_GROUP), lambda j: (0, 0, 0)),
                   pl.BlockSpec((1, D_POOL), lambda j: (0, 0))],
        out_shape=[jax.ShapeDtypeStruct(u.shape, BF16), jax.ShapeDtypeStruct((4, POOL_GROUP, POOL_GROUP), F32),
                   jax.ShapeDtypeStruct((1, D_POOL), F32)],
        compiler_params=_params("arbitrary"),
    )(u, u, dyp, dyp, pool_w, pool_scale)


CONV_SLAB = 512


def _conv_taps(tail, cur, keep_tail):
    ext = jnp.concatenate([jnp.where(keep_tail, tail, 0.0), cur], axis=0)
    return [(pltpu.roll(ext, CONV_W - 1 - k, 0) if k < CONV_W - 1 else ext)[8:] for k in range(CONV_W)]


def _conv_pre(taps, w_ref, b_ref, sl):
    acc = b_ref[:, sl]
    for k in range(CONV_W):
        acc = acc + w_ref[k:k + 1, sl] * taps[k]
    return acc


def _conv_fwd(xbc, conv_w, conv_b, *, name):
    bsz, t, c = xbc.shape
    nc = t // CHUNK

    def body(tail_ref, cur_ref, w_ref, b_ref, o_ref):
        keep = pl.program_id(1) > 0
        for c0 in range(0, c, CONV_SLAB):
            sl = pl.ds(c0, CONV_SLAB)
            pre = _conv_pre(_conv_taps(tail_ref[0, :, sl], cur_ref[0, :, sl], keep), w_ref, b_ref, sl)
            o_ref[0, :, sl] = (pre * _sigmoid(pre)).astype(BF16)

    return pl.pallas_call(
        body, name=name, grid=(bsz, nc),
        in_specs=[pl.BlockSpec((1, 8, c), lambda b, j: (b, jnp.maximum(j * (CHUNK // 8) - 1, 0), 0)),
                  pl.BlockSpec((1, CHUNK, c), lambda b, j: (b, j, 0)),
                  pl.BlockSpec((CONV_W, c), lambda b, j: (0, 0)), pl.BlockSpec((1, c), lambda b, j: (0, 0))],
        out_specs=pl.BlockSpec((1, CHUNK, c), lambda b, j: (b, j, 0)), out_shape=jax.ShapeDtypeStruct(xbc.shape, BF16),
        compiler_params=_params("parallel", "parallel"),
    )(xbc, xbc, conv_w, conv_b)


def _conv_bwd_pre(xbc, dxs, db, dc, conv_w, conv_b, *, name):
    bsz, t, c = xbc.shape
    nc = t // CHUNK

    def body(tail_ref, cur_ref, dxs_ref, db_ref, dc_ref, w_ref, b_ref, dpre_ref, dwb_ref):
        j = pl.program_id(1)

        @pl.when(j == 0)
        def _():
            dwb_ref[...] = jnp.zeros_like(dwb_ref)

        for c0 in range(0, c, CONV_SLAB):
            sl = pl.ds(c0, CONV_SLAB)
            if c0 < D_SSM:
                dxc = dxs_ref[0, :, sl]
            else:
                dxc = (db_ref if c0 < D_SSM + D_POOL else dc_ref)[0]
            taps = _conv_taps(tail_ref[0, :, sl], cur_ref[0, :, sl], j > 0)
            pre = _conv_pre(taps, w_ref, b_ref, sl)
            s = _sigmoid(pre)
            dpre = dxc * (s * (1.0 + pre * (1.0 - s)))
            dpre_ref[0, :, sl] = dpre.astype(BF16)
            for k in range(CONV_W):
                dwb_ref[0, k:k + 1, sl] += jnp.sum(dpre * taps[k], axis=0, keepdims=True)
            dwb_ref[0, CONV_W:CONV_W + 1, sl] += jnp.sum(dpre, axis=0, keepdims=True)

    assert CONV_SLAB == D_POOL and D_SSM % CONV_SLAB == 0
    row = lambda width: pl.BlockSpec((1, CHUNK, width), lambda b, j: (b, j, 0))
    return pl.pallas_call(
        body, name=name, grid=(bsz, nc),
        in_specs=[pl.BlockSpec((1, 8, c), lambda b, j: (b, jnp.maximum(j * (CHUNK // 8) - 1, 0), 0)),
                  row(c), row(D_SSM), row(D_POOL), row(D_POOL),
                  pl.BlockSpec((CONV_W, c), lambda b, j: (0, 0)), pl.BlockSpec((1, c), lambda b, j: (0, 0))],
        out_specs=[row(c), pl.BlockSpec((1, 8, c), lambda b, j: (b, 0, 0))],
        out_shape=[jax.ShapeDtypeStruct(xbc.shape, BF16), jax.ShapeDtypeStruct((bsz, 8, c), F32)],
        compiler_params=_params("parallel", "arbitrary"),
    )(xbc, xbc, dxs, db, dc, conv_w, conv_b)


def _conv_bwd_in(dpre, conv_w, *, name):
    bsz, t, c = dpre.shape
    nc = t // CHUNK

    halo = 16

    def body(cur_ref, head_ref, w_ref, o_ref):
        keep = pl.program_id(1) < nc - 1
        for c0 in range(0, c, CONV_SLAB):
            sl = pl.ds(c0, CONV_SLAB)
            ext = jnp.concatenate([cur_ref[0, :, sl].astype(F32),
                                   jnp.where(keep, head_ref[0, :, sl].astype(F32), 0.0)], axis=0)
            acc = w_ref[CONV_W - 1:CONV_W, sl] * ext[:CHUNK]
            for k in range(CONV_W - 1):
                up = CONV_W - 1 - k
                acc = acc + w_ref[k:k + 1, sl] * pltpu.roll(ext, CHUNK + halo - up, 0)[:CHUNK]
            o_ref[0, :, sl] = acc.astype(BF16)

    return pl.pallas_call(
        body, name=name, grid=(bsz, nc),
        in_specs=[pl.BlockSpec((1, CHUNK, c), lambda b, j: (b, j, 0)),
                  pl.BlockSpec((1, halo, c), lambda b, j: (b, jnp.minimum((j + 1) * (CHUNK // halo), t // halo - 1), 0)),
                  pl.BlockSpec((CONV_W, c), lambda b, j: (0, 0))],
        out_specs=pl.BlockSpec((1, CHUNK, c), lambda b, j: (b, j, 0)), out_shape=jax.ShapeDtypeStruct(dpre.shape, BF16),
        compiler_params=_params("parallel", "parallel"),
    )(dpre, dpre, conv_w)


def _ssd_common(j, dtr, dtb, alog):
    lane = lax.broadcasted_iota(jnp.int32, (CHUNK, LANES), 1)
    row = lax.broadcasted_iota(jnp.int32, (CHUNK, LANES), 0)
    raw = dtr + dtb
    valid = (lane < HPG) & ((j > 0) | (row >= PAD))
    dt = jnp.where(valid, _softplus(raw), 0.0)
    a = -jnp.exp(alog)
    tril = (row >= lane).astype(F32)
    acs = _dot_exact(tril, dt * a)
    return dict(lane=lane, row=row, raw=raw, valid=valid, dt=dt, a=a, causal=row >= lane,
                acs=acs, acs_t=acs.T, dt_t=dt.T, aend=acs[CHUNK - 1:CHUNK, :])


def _ssd_specs(bsz, nc, rev):
    ch = (lambda j: nc - 1 - j) if rev else (lambda j: j)
    return dict(
        xs=pl.BlockSpec((bsz, CHUNK, GW), lambda g, j: (0, ch(j), g)),
        bm=pl.BlockSpec((bsz, CHUNK, D_STATE), lambda g, j: (0, ch(j), D_SSM // D_STATE + g)),
        cm=pl.BlockSpec((bsz, CHUNK, D_STATE), lambda g, j: (0, ch(j), D_SSM // D_STATE + N_GROUPS + g)),
        lane_blk=pl.BlockSpec((bsz, CHUNK, LANES), lambda g, j: (0, ch(j), g)),
        grp_const=pl.BlockSpec((1, 1, LANES), lambda g, j: (g, 0, 0)),
        grp_vec=pl.BlockSpec((1, GW), lambda g, j: (0, g)),
        state=pl.BlockSpec((bsz, 1, D_STATE, GW), lambda g, j: (0, ch(j), 0, g)),
    )


def _ssd_fwd(xc, dtr, z, dtb, alog, dskip, normw, *, name, rider=None):
    bsz, t, _ = xc.shape
    nc = t // CHUNK
    sp = _ssd_specs(bsz, nc, False)

    def body(xs_ref, b_ref, c_ref, dtr_ref, z_ref, dtb_ref, alog_ref, dsk_ref, nw_ref, yn_ref, y_ref, sp_ref, s_ref):
        j = pl.program_id(1)

        @pl.when(j == 0)
        def _():
            s_ref[...] = jnp.zeros_like(s_ref)

        ex = range(bsz)
        units = [(e, r) for e in ex for r in range(HPG)]
        full = lambda v: jnp.broadcast_to(v, (CHUNK, LANES))
        pair = lambda r: pl.ds((r // 2) * LANES, LANES)
        q = [_ssd_common(j, dtr_ref[e], dtb_ref[0], alog_ref[0]) for e in ex]
        for e in ex:
            sp_ref[e, 0] = s_ref[e]
        bm, cm = [b_ref[e] for e in ex], [c_ref[e] for e in ex]
        cb = [_dot_nt(cm[e], bm[e]) for e in ex]
        low = q[0]["lane"] < HEAD_DIM
        col = {(e, r): full(q[e]["acs"][:, r:r + 1]) for e, r in units}
        aend = {(e, r): q[e]["aend"][:, r:r + 1] for e, r in units}
        decay = {(e, r): jnp.exp(jnp.where(q[e]["causal"], col[e, r] - q[e]["acs_t"][r:r + 1, :], -jnp.inf))
                 for e, r in units}
        mp = {(e, r): cb[e] * decay[e, r] * q[e]["dt_t"][r:r + 1, :] for e, r in units}
        ce = {(e, r): cm[e] * jnp.exp(col[e, r]) for e, r in units}
        bk = {(e, r): bm[e] * (jnp.exp(aend[e, r] - col[e, r]) * full(q[e]["dt"][:, r:r + 1])) for e, r in units}
        xp = {(e, r): xs_ref[e, :, pair(r)] for e, r in units}
        s_old = {(e, r): s_ref[e, :, pair(r)] for e, r in units}
        y_h = {u: _dot(mp[u], xp[u]) + _dot(ce[u], s_old[u]) for u in units}
        s_h = {u: jnp.exp(aend[u]) * s_old[u] + _dot_tn(bk[u], xp[u]) for u in units}
        for e in ex:
            for r in range(0, HPG, 2):
                y_ref[e, :, pair(r)] = jnp.where(low, y_h[e, r], y_h[e, r + 1])
                s_ref[e, :, pair(r)] = jnp.where(low, s_h[e, r], s_h[e, r + 1])
        y = [y_ref[e] + dsk_ref[...] * xs_ref[e] for e in ex]
        zz = [z_ref[e] for e in ex]
        yg = [y[e] * (zz[e] * _sigmoid(zz[e])) for e in ex]
        rstd = [lax.rsqrt(jnp.mean(yg[e] * yg[e], axis=-1, keepdims=True) + EPS) for e in ex]
        for e in ex:
            y_ref[e] = y[e]
            yn_ref[e] = (yg[e] * rstd[e] * nw_ref[...]).astype(BF16)

    grid = (N_GROUPS, nc)
    ride = _Ride(rider, body, 9, 3, 1, grid)
    outs = pl.pallas_call(
        ride.body, name=name, grid=grid,
        in_specs=[sp["xs"], sp["bm"], sp["cm"], sp["lane_blk"], sp["xs"], sp["grp_const"], sp["grp_const"],
                  sp["grp_vec"], sp["grp_vec"]] + ride.in_specs,
        out_specs=[sp["xs"], sp["xs"], sp["state"]] + ride.out_specs,
        out_shape=[jax.ShapeDtypeStruct((bsz, t, D_SSM), BF16), jax.ShapeDtypeStruct((bsz, t, D_SSM), F32),
                   jax.ShapeDtypeStruct((bsz, nc, D_STATE, D_SSM), F32)] + ride.out_shape,
        scratch_shapes=[pltpu.VMEM((bsz, D_STATE, GW), F32)] + ride.scratch,
        compiler_params=_params(*ride.semantics(("parallel", "arbitrary"))),
    )(xc, xc, xc, dtr, z, dtb, alog, dskip, normw, *ride.args)
    return outs[:3], outs[3:]


def _ssd_bwd(xc, dtr, z, ypre, sprev, dyn, dtb, alog, dskip, normw, *, name, rider=None):
    bsz, t, _ = xc.shape
    nc = t // CHUNK
    sp = _ssd_specs(bsz, nc, True)

    def body(xs_ref, b_ref, c_ref, dtr_ref, z_ref, y_ref, sp_ref, dyn_ref, dtb_ref, alog_ref, dsk_ref, nw_ref,
             dz_ref, dxs_ref, db_ref, dc_ref, ddt_ref, dnw_ref, dsm_ref, ds_ref):
        j = pl.program_id(1)

        @pl.when(j == 0)
        def _():
            ds_ref[...] = jnp.zeros_like(ds_ref)
            dnw_ref[...] = jnp.zeros_like(dnw_ref)
            dsm_ref[...] = jnp.zeros_like(dsm_ref)

        ex = range(bsz)
        heads = range(HPG)
        units = [(e, r) for e in ex for r in heads]
        q = [_ssd_common(nc - 1 - j, dtr_ref[e], dtb_ref[0], alog_ref[0]) for e in ex]
        lane, row = q[0]["lane"], q[0]["row"]
        lane1 = lane[0:1, :]
        nw = nw_ref[...]
        y, zz, dyn = [y_ref[e] for e in ex], [z_ref[e] for e in ex], [dyn_ref[e] for e in ex]
        sz = [_sigmoid(zz[e]) for e in ex]
        sil = [zz[e] * sz[e] for e in ex]
        yg = [y[e] * sil[e] for e in ex]
        rstd = [lax.rsqrt(jnp.mean(yg[e] * yg[e], axis=-1, keepdims=True) + EPS) for e in ex]
        gn = [dyn[e] * nw for e in ex]
        dyg = [rstd[e] * (gn[e] - yg[e] * (rstd[e] * rstd[e]) * jnp.mean(gn[e] * yg[e], axis=-1, keepdims=True))
               for e in ex]
        dy = [dyg[e] * sil[e] for e in ex]
        xs = [xs_ref[e] for e in ex]
        for e in ex:
            dnw_ref[e] += jnp.sum(dyn[e] * yg[e] * rstd[e], axis=0, keepdims=True)
            dz_ref[e] = (dyg[e] * y[e] * (sz[e] * (1.0 + zz[e] * (1.0 - sz[e])))).astype(BF16)
        dskip_cols = [jnp.sum(dy[e] * xs[e], axis=0, keepdims=True) for e in ex]

        bm, cm = [b_ref[e] for e in ex], [c_ref[e] for e in ex]
        cb = [_dot_nt(cm[e], bm[e]) for e in ex]
        zero = jnp.zeros((CHUNK, LANES), F32)
        full = lambda v: jnp.broadcast_to(v, (CHUNK, LANES))
        low = lane < HEAD_DIM
        half = [low if r % 2 == 0 else ~low for r in heads]
        sl = lambda v, r: v[:, (r // 2) * LANES:(r // 2 + 1) * LANES]
        pair = lambda r: pl.ds((r // 2) * LANES, LANES)
        col = {(e, r): full(q[e]["acs"][:, r:r + 1]) for e, r in units}
        dt_col = {(e, r): full(q[e]["dt"][:, r:r + 1]) for e, r in units}
        aend = {(e, r): q[e]["aend"][:, r:r + 1] for e, r in units}
        dt_row = {(e, r): q[e]["dt_t"][r:r + 1, :] for e, r in units}
        decay = {(e, r): jnp.exp(jnp.where(q[e]["causal"], col[e, r] - q[e]["acs_t"][r:r + 1, :], -jnp.inf))
                 for e, r in units}
        ea = {u: jnp.exp(col[u]) for u in units}
        dte = {u: jnp.exp(aend[u] - col[u]) for u in units}
        ed = {u: jnp.exp(aend[u]) for u in units}
        k = {u: dte[u] * dt_col[u] for u in units}
        mp = {(e, r): cb[e] * decay[e, r] * dt_row[e, r] for e, r in units}
        xp = {(e, r): sl(xs[e], r) for e, r in units}
        dym = {(e, r): jnp.where(half[r], sl(dy[e], r), 0.0) for e, r in units}
        xm = {(e, r): jnp.where(half[r], xp[e, r], 0.0) for e, r in units}
        s_old = {(e, r): sp_ref[e, 0, :, pair(r)] for e, r in units}
        dsm = {(e, r): jnp.where(half[r], ds_ref[e, :, pair(r)], 0.0) for e, r in units}
        gmat = {u: _dot_nt(dym[u], xp[u]) for u in units}
        t1 = {u: _dot_nt(dym[u], s_old[u]) for u in units}
        dbs = {u: _dot_nt(xm[u], dsm[u]) for u in units}
        dx = {(e, r): _dot_tn(mp[e, r], dym[e, r]) + _dot(bm[e] * k[e, r], dsm[e, r]) for e, r in units}
        ds = {(e, r): _dot_tn(cm[e] * ea[e, r], dym[e, r]) + ed[e, r] * dsm[e, r] for e, r in units}
        w0 = {(e, r): gmat[e, r] * cb[e] * decay[e, r] for e, r in units}
        cs0 = {u: jnp.sum(w0[u], axis=0, keepdims=True) for u in units}
        rs = {u: jnp.sum(w0[u] * dt_row[u], axis=1, keepdims=True) for u in units}
        qv = {(e, r): jnp.sum(cm[e] * t1[e, r], axis=1, keepdims=True) for e, r in units}
        dk = {(e, r): jnp.sum(bm[e] * dbs[e, r], axis=1, keepdims=True) for e, r in units}
        ddte = {u: dk[u] * dt_col[u] for u in units}
        d_aend = {u: _sum_all(dsm[u] * s_old[u]) * ed[u] + _sum_all(ddte[u][:, 0:1] * dte[u][:, 0:1]) for u in units}
        last_row = row == CHUNK - 1
        dacs_col = {u: rs[u] + qv[u] * ea[u] - ddte[u] * dte[u] + jnp.where(last_row, d_aend[u], 0.0) for u in units}
        triu = (lane >= row).astype(F32)
        for e in ex:
            dcb, dc_acc, db_acc = zero, zero, zero
            dacs, dacs_t, ddt, ddt_t = zero, zero, zero, zero
            dskip_row = jnp.zeros((1, LANES), F32)
            for r in heads:
                u = (e, r)
                dcb = dcb + gmat[u] * decay[u] * dt_row[u]
                dc_acc = dc_acc + ea[u] * t1[u]
                db_acc = db_acc + k[u] * dbs[u]
                dacs = jnp.where(lane == r, dacs_col[u], dacs)
                ddt = jnp.where(lane == r, dk[u] * dte[u], ddt)
                dacs_t = jnp.where(row == r, -cs0[u] * dt_row[u], dacs_t)
                ddt_t = jnp.where(row == r, cs0[u], ddt_t)
                dsk = _sum_all(jnp.where(half[r][0:1, :], sl(dskip_cols[e], r), 0.0))
                dskip_row = dskip_row + jnp.where(lane1 == r, dsk, 0.0)
            for r in range(0, HPG, 2):
                dxs_ref[e, :, pair(r)] = (dx[e, r] + dx[e, r + 1] + sl(dy[e], r) * dsk_ref[:, pair(r)]).astype(BF16)
                ds_ref[e, :, pair(r)] = ds[e, r] + ds[e, r + 1]
            dacs = dacs + dacs_t.T
            ddt = ddt + ddt_t.T
            dda = _dot_exact(triu, dacs)
            ddt = ddt + dda * q[e]["a"]
            da = jnp.sum(dda * q[e]["dt"], axis=0, keepdims=True)
            draw = jnp.where(q[e]["valid"], ddt * _sigmoid(q[e]["raw"]), 0.0)
            ddt_ref[e] = draw.astype(BF16)
            dsm_ref[e, 0, 0:1, :] += dskip_row
            dsm_ref[e, 0, 1:2, :] += da * q[e]["a"]
            dsm_ref[e, 0, 2:3, :] += jnp.sum(draw, axis=0, keepdims=True)
            dc_ref[e] = (dc_acc + _dot(dcb, bm[e])).astype(BF16)
            db_ref[e] = (db_acc + _dot_tn(dcb, cm[e])).astype(BF16)

    grp_out = pl.BlockSpec((bsz, CHUNK, D_STATE), lambda g, j: (0, nc - 1 - j, g))
    grid = (N_GROUPS, nc)
    ride = _Ride(rider, body, 12, 7, 1, grid)
    outs = pl.pallas_call(
        ride.body, name=name, grid=grid,
        in_specs=[sp["xs"], sp["bm"], sp["cm"], sp["lane_blk"], sp["xs"], sp["xs"], sp["state"], sp["xs"],
                  sp["grp_const"], sp["grp_const"], sp["grp_vec"], sp["grp_vec"]] + ride.in_specs,
        out_specs=[sp["xs"], sp["xs"], grp_out, grp_out, sp["lane_blk"],
                   pl.BlockSpec((bsz, 1, GW), lambda g, j: (0, 0, g)),
                   pl.BlockSpec((bsz, 1, 8, LANES), lambda g, j: (0, g, 0, 0))] + ride.out_specs,
        out_shape=[jax.ShapeDtypeStruct((bsz, t, D_SSM), BF16), jax.ShapeDtypeStruct((bsz, t, D_SSM), BF16),
                   jax.ShapeDtypeStruct((bsz, t, N_GROUPS * D_STATE), BF16),
                   jax.ShapeDtypeStruct((bsz, t, N_GROUPS * D_STATE), BF16),
                   jax.ShapeDtypeStruct((bsz, t, D_DT), BF16), jax.ShapeDtypeStruct((bsz, 1, D_SSM), F32),
                   jax.ShapeDtypeStruct((bsz, N_GROUPS, 8, LANES), F32)] + ride.out_shape,
        scratch_shapes=[pltpu.VMEM((bsz, D_STATE, GW), F32)] + ride.scratch,
        compiler_params=_params(*ride.semantics(("parallel", "arbitrary"))),
    )(xc, xc, xc, dtr, z, ypre, sprev, dyn, dtb, alog, dskip, normw, *ride.args)
    return outs[:7], outs[7:]


def _input_grad(dhn, h0, w, dres, seq, *, name):
    bsz, t, d = h0.shape
    nc = t // CHUNK

    def body(dy_ref, h_ref, w_ref, dres_ref, gx_ref, head_ref, dw_ref):
        j = pl.program_id(1)

        @pl.when((pl.program_id(0) == 0) & (j == 0))
        def _():
            dw_ref[...] = jnp.zeros_like(dw_ref)

        x, dyv = h_ref[0], dy_ref[0]
        r = lax.rsqrt(jnp.mean(x * x, axis=-1, keepdims=True) + EPS)
        g = dyv * w_ref[...]
        dx = r * (g - x * (r * r) * jnp.mean(g * x, axis=-1, keepdims=True)) + dres_ref[0]
        dw_ref[...] += jnp.sum(dyv * x * r, axis=0, keepdims=True)

        @pl.when(j == 0)
        def _():
            head_ref[0] = dx

        gx_ref[0] = dx

    row = pl.BlockSpec((1, CHUNK, d), lambda b, j: (b, j, 0))
    return pl.pallas_call(
        body, name=name, grid=(bsz, nc),
        in_specs=[row, row, pl.BlockSpec((1, d), lambda b, j: (0, 0)), row],
        out_specs=[pl.BlockSpec((1, CHUNK, d), lambda b, j: (b, jnp.maximum(j - 1, 0), 0)),
                   pl.BlockSpec((1, CHUNK, d), lambda b, j: (b, 0, 0)), pl.BlockSpec((1, d), lambda b, j: (0, 0))],
        out_shape=[jax.ShapeDtypeStruct((bsz, seq, d), F32), jax.ShapeDtypeStruct((bsz, CHUNK, d), F32),
                   jax.ShapeDtypeStruct((1, d), F32)],
        compiler_params=_params("arbitrary", "arbitrary"),
    )(dhn, h0, w, dres)


def _remote(src, dst, send_sem, recv_sem, dev):
    return pltpu.make_async_remote_copy(src_ref=src, dst_ref=dst, send_sem=send_sem, recv_sem=recv_sem,
                                        device_id=dev, device_id_type=MESH)


def _position():
    return lax.axis_index("x"), lax.axis_index("y"), lax.axis_index("c")


def _other_chips(pos):
    x, y, _ = pos
    return [(1 - x, y), (x, 1 - y), (1 - x, 1 - y)]


class _Gather:
    def __init__(self, arrs):
        n = len(arrs)
        self.args, self.n_in, self.n_out = list(arrs), n, n
        self.out_shape = [jax.ShapeDtypeStruct((4,) + a.shape, a.dtype) for a in arrs]
        self.scratch = [pltpu.SemaphoreType.DMA((3 * n,)), pltpu.SemaphoreType.DMA((3 * n,)),
                        pltpu.SemaphoreType.DMA((n,))]

    def _copies(self, pos, ins, outs, sems):
        send_sems, recv_sems, loc_sems = sems
        x, y, c = pos
        me = 2 * x + y
        local = [pltpu.make_async_copy(ins[i], outs[i].at[me], loc_sems.at[i]) for i in range(self.n_in)]
        sends, recvs = [], []
        for i in range(self.n_in):
            for k, (px, py) in enumerate(_other_chips(pos)):
                sems_k = (send_sems.at[3 * i + k], recv_sems.at[3 * i + k], (px, py, c))
                sends.append(_remote(ins[i], outs[i].at[me], *sems_k))
                recvs.append(_remote(ins[i], outs[i].at[2 * px + py], *sems_k))
        return local, sends, recvs

    def start(self, pos, ins, outs, sems):
        local, sends, _ = self._copies(pos, ins, outs, sems)
        for cp in local + sends:
            cp.start()

    def finish(self, pos, ins, outs, sems):
        local, sends, recvs = self._copies(pos, ins, outs, sems)
        for cp in recvs:
            cp.wait_recv()
        for cp in sends:
            cp.wait_send()
        for cp in local:
            cp.wait()


class _Exchange:
    FLIPS = [(fx, fy, fc) for fx in (0, 1) for fy in (0, 1) for fc in (0, 1)][1:]

    def __init__(self, big, small=None):
        n = len(big)
        self.n_big, self.has_small = n, small is not None
        self.args = list(big) + ([small] if self.has_small else [])
        self.n_in = self.n_out = len(self.args)
        self.out_shape = [jax.ShapeDtypeStruct(a.shape, a.dtype) for a in big]
        self.scratch = [pltpu.SemaphoreType.DMA((max(3 * n, 1),)), pltpu.SemaphoreType.DMA((max(3 * n, 1),))]
        if self.has_small:
            self.out_shape.append(jax.ShapeDtypeStruct((8,) + small.shape, small.dtype))
            self.scratch += [pltpu.SemaphoreType.DMA((7,)), pltpu.SemaphoreType.DMA((7,)), pltpu.SemaphoreType.DMA((1,))]

    def _copies(self, pos, ins, outs, sems):
        x, y, c = pos
        me, me8 = 2 * x + y, 4 * x + 2 * y + c
        local, sends, recvs = [], [], []
        for i in range(self.n_big):
            for k, (px, py) in enumerate(_other_chips(pos)):
                sems_k = (sems[0].at[3 * i + k], sems[1].at[3 * i + k], (px, py, c))
                sends.append(_remote(ins[i].at[2 * px + py], outs[i].at[me], *sems_k))
                recvs.append(_remote(ins[i].at[me], outs[i].at[2 * px + py], *sems_k))
        if self.has_small:
            small, landed = ins[self.n_big], outs[self.n_big]
            local.append(pltpu.make_async_copy(small, landed.at[me8], sems[4].at[0]))
            for k, (fx, fy, fc) in enumerate(self.FLIPS):
                peer = (x ^ fx, y ^ fy, c ^ fc)
                sems_k = (sems[2].at[k], sems[3].at[k], peer)
                sends.append(_remote(small, landed.at[me8], *sems_k))
                recvs.append(_remote(small, landed.at[4 * peer[0] + 2 * peer[1] + peer[2]], *sems_k))
        return local, sends, recvs

    start = _Gather.start
    finish = _Gather.finish


class _Swap:
    def __init__(self, arrs):
        n = len(arrs)
        self.args, self.n_in, self.n_out = list(arrs), n, n
        self.out_shape = [jax.ShapeDtypeStruct(a.shape, a.dtype) for a in arrs]
        self.scratch = [pltpu.SemaphoreType.DMA((n,)), pltpu.SemaphoreType.DMA((n,))]

    def _copies(self, pos, ins, outs, sems):
        x, y, c = pos
        both = [_remote(ins[i], outs[i], sems[0].at[i], sems[1].at[i], (x, y, 1 - c)) for i in range(self.n_in)]
        return [], both, both

    start = _Gather.start
    finish = _Gather.finish


def _comm(rider, *, name):
    a, b = rider.n_in, rider.n_in + rider.n_out

    def body(*refs):
        pos = _position()
        rider.start(pos, refs[:a], refs[a:b], refs[b:])
        rider.finish(pos, refs[:a], refs[a:b], refs[b:])

    return pl.pallas_call(body, name=name, in_specs=[ANY] * rider.n_in, out_specs=[ANY] * rider.n_out,
                          out_shape=rider.out_shape, scratch_shapes=rider.scratch)(*rider.args)


class _Ride:
    def __init__(self, rider, body, n_in, n_out, n_scratch, grid):
        self.rider = rider
        self.args = rider.args if rider else []
        self.in_specs = [ANY] * rider.n_in if rider else []
        self.out_specs = [ANY] * rider.n_out if rider else []
        self.out_shape = rider.out_shape if rider else []
        self.scratch = rider.scratch if rider else []
        self.body = self._wrap(body, n_in, n_out, n_scratch, grid) if rider else body

    def semantics(self, sem):
        return ("arbitrary",) * len(sem) if self.rider else sem

    def _wrap(self, body, n_in, n_out, n_scratch, grid):
        rider = self.rider
        a = n_in
        b = a + rider.n_in
        c = b + n_out
        d = c + rider.n_out
        e = d + n_scratch

        def wrapped(*refs):
            pos = _position()
            ids = [pl.program_id(i) for i in range(len(grid))]
            first = functools.reduce(jnp.logical_and, [i == 0 for i in ids])
            last = functools.reduce(jnp.logical_and, [i == g - 1 for i, g in zip(ids, grid)])

            @pl.when(first)
            def _():
                rider.start(pos, refs[a:b], refs[c:d], refs[e:])

            body(*refs[:a], *refs[b:c], *refs[d:e])

            @pl.when(last)
            def _():
                rider.finish(pos, refs[a:b], refs[c:d], refs[e:])

        return wrapped


def _elementwise_tiles(r, c):
    if r % 8 == 0 and r * c > 65536:
        tm = _pick(r, (256, 128, 64, 16, 8))
        return (tm, c), r // tm, lambda i: (i, 0)
    if r % 8 and c % 256 == 0 and r * c > 65536:
        return (r, 256), c // 256, lambda i: (0, i)
    return (r, c), 1, lambda i: (0, 0)


def _chip_sum(own, landed, *, name):
    r, c = own.shape
    blk, steps, at = _elementwise_tiles(r, c)

    def body(own_ref, land_ref, o_ref):
        me = 2 * lax.axis_index("x") + lax.axis_index("y")
        acc = None
        for jchip in range(4):
            term = jnp.where(me == jchip, own_ref[...], land_ref[jchip].astype(F32))
            acc = term if acc is None else acc + term
        o_ref[...] = acc

    return pl.pallas_call(
        body, name=name, grid=(steps,),
        in_specs=[pl.BlockSpec(blk, at), pl.BlockSpec((4,) + blk, lambda i: (0,) + at(i))],
        out_specs=pl.BlockSpec(blk, at), out_shape=jax.ShapeDtypeStruct((r, c), F32),
        compiler_params=_params("parallel"),
    )(own, landed)


def _device_sum(parts, *, name):
    _, r, c = parts.shape

    def body(p_ref, o_ref):
        acc = p_ref[0]
        for d in range(1, 8):
            acc = acc + p_ref[d]
        o_ref[...] = acc

    return pl.pallas_call(body, name=name, out_shape=jax.ShapeDtypeStruct((r, c), F32))(parts)


def _adamw_math(w, g, m, v):
    m = ADAM_B1 * m + (1.0 - ADAM_B1) * g
    v = ADAM_B2 * v + (1.0 - ADAM_B2) * (g * g)
    m_hat = m / (1.0 - ADAM_B1 ** ADAM_STEP)
    v_hat = v / (1.0 - ADAM_B2 ** ADAM_STEP)
    return -ADAM_LR * (m_hat / (jnp.sqrt(v_hat) + ADAM_EPS) + ADAM_WD * w), m, v


def _adamw(w, g_parts, m, v, *, name):
    r, c = w.shape
    shape, steps, at = _elementwise_tiles(r, c)
    n_g = len(g_parts)

    def body(*refs):
        w_ref, m_ref, v_ref = refs[n_g:n_g + 3]
        g_ref, d_ref, nm_ref, nv_ref = refs[n_g + 3:]
        g = refs[0][...]
        for p in refs[1:n_g]:
            g = g + p[...]
        g_ref[...] = g
        d_ref[...], nm_ref[...], nv_ref[...] = _adamw_math(w_ref[...], g, m_ref[...], v_ref[...])

    blk = pl.BlockSpec(shape, at)
    return pl.pallas_call(
        body, name=name, grid=(steps,), in_specs=[blk] * (n_g + 3), out_specs=[blk] * 4,
        out_shape=[jax.ShapeDtypeStruct((r, c), F32)] * 4, compiler_params=_params("parallel"),
    )(*g_parts, w, m, v)


def _pad_heads(v):
    return jnp.pad(v.reshape(N_GROUPS, 1, HPG), ((0, 0), (0, 0), (0, LANES - HPG)))


def _unpad_heads(v):
    return v[:, :HPG].reshape(1, N_HEADS)


_SMALL = [("norm_mix_w", (1, 1024)), ("pool_w", (512, 128)), ("pool_scale", (1, 512)), ("conv_w", (4, D_XBC)),
          ("conv_b", (1, D_XBC)), ("dt_bias", (1, N_HEADS)), ("a_log", (1, N_HEADS)), ("d_skip", (1, N_HEADS)),
          ("ssm_norm_w", (1, D_SSM)), ("norm_ffn_w", (1, 1024)), ("norm_f_w", (1, 1024)), ("meta", (N_META, 1024))]


def _pack_small(grads):
    rows = []
    for nm, shape in _SMALL:
        flat = grads[nm].reshape(-1)
        rows.append(jnp.pad(flat, (0, (-flat.size) % LANES)).reshape(-1, LANES))
    packed = jnp.concatenate(rows, axis=0)
    return jnp.pad(packed, ((0, (-packed.shape[0]) % 8), (0, 0)))


def _unpack_small(packed):
    out, r0 = {}, 0
    for nm, shape in _SMALL:
        size = shape[0] * shape[1]
        nrow = -(-size // LANES)
        out[nm] = packed[r0:r0 + nrow].reshape(-1)[:size].reshape(shape)
        r0 += nrow
    return out


def kernel(x, meta, norm_mix_w, w_in, pool_w, pool_scale, conv_w, conv_b, dt_bias, a_log, d_skip, ssm_norm_w, w_out, norm_ffn_w, w_ff1, w_ff2, norm_f_w, loss_target, m_meta, m_norm_mix_w, m_w_in, m_pool_w, m_pool_scale, m_conv_w, m_conv_b, m_dt_bias, m_a_log, m_d_skip, m_ssm_norm_w, m_w_out, m_norm_ffn_w, m_w_ff1, m_w_ff2, m_norm_f_w, v_meta, v_norm_mix_w, v_w_in, v_pool_w, v_pool_scale, v_conv_w, v_conv_b, v_dt_bias, v_a_log, v_d_skip, v_ssm_norm_w, v_w_out, v_norm_ffn_w, v_w_ff1, v_w_ff2, v_norm_f_w):
    bsz, seq, d = x.shape
    t = seq + CHUNK
    n = bsz * t
    chip = 2 * lax.axis_index("x") + lax.axis_index("y")
    d_in = w_in.shape[2] * 4

    g_in, g_conv, g_meta = _comm(_Gather([w_in[0].T.astype(BF16), conv_w[0], meta]), name="gather_in")
    late_weights = _Gather([w_out[0].astype(BF16), w_ff1[0].astype(BF16), w_ff2[0].astype(BF16)])
    win = g_in.reshape(d_in, d)
    wu, wz = win[:D_POOL], win[D_POOL:D_POOL + D_SSM]
    wx = win[D_POOL + D_SSM:D_POOL + D_SSM + D_XBC]
    wdt = jnp.pad(win[D_POOL + D_SSM + D_XBC:].reshape(N_GROUPS, HPG, d),
                  ((0, 0), (0, LANES - HPG), (0, 0))).reshape(D_DT, d)
    convw = g_conv.transpose(1, 0, 2).reshape(CONV_W, D_XBC)
    meta_full = g_meta.transpose(1, 0, 2).reshape(N_META, d)
    dtb, alog = _pad_heads(dt_bias), _pad_heads(a_log)
    dskip = jnp.repeat(d_skip, HEAD_DIM, axis=1)
    poolw = pool_w[0]

    h0 = jnp.concatenate([jnp.zeros((bsz, PAD, d), F32), jnp.broadcast_to(meta_full[None], (bsz, N_META, d)), x], axis=1)
    h0f = h0.reshape(n, d)
    hn1 = _rms_fwd(h0f, norm_mix_w, name="norm_mix")
    u = _mm(hn1, wu, name="proj_u", nt=True)
    z = _mm(hn1, wz, name="proj_z", nt=True)
    xbc = _mm(hn1, wx, name="proj_xbc", nt=True)
    dtr = _mm(hn1, wdt, name="proj_dt", nt=True)
    ypool = _pool_fwd(u.reshape(bsz, t, D_POOL), poolw, pool_scale, name="pool_fwd")
    xbc3 = xbc.reshape(bsz, t, D_XBC)
    xc = _conv_fwd(xbc3, convw, conv_b, name="conv_fwd")
    z3, dtr3 = z.reshape(bsz, t, D_SSM), dtr.reshape(bsz, t, D_DT)
    (yn, ypre, sprev), (g_out, g_ff1, g_ff2) = _ssd_fwd(xc, dtr3, z3, dtb, alog, dskip, ssm_norm_w, name="ssd_fwd",
                                                        rider=late_weights)
    wo = g_out.reshape(D_POOL + D_SSM, d)
    wo_p, wo_s = wo[:D_POOL], wo[D_POOL:]
    w1 = g_ff1.transpose(1, 0, 2).reshape(d, D_FF)
    w2 = g_ff2.reshape(D_FF, d)
    ypool_f, yn_f = ypool.reshape(n, D_POOL), yn.reshape(n, D_SSM)
    add = lambda r, e: r + e
    h1 = _mm([ypool_f, yn_f], [wo_p, wo_s], name="out_proj", post=add, extras=(h0f,))
    hn2 = _rms_fwd(h1, norm_ffn_w, name="norm_ffn")
    act = _mm(hn2, w1, name="ff1", out_dtype=BF16)
    relu2 = lambda a: jnp.square(jnp.maximum(a, 0))
    h2 = _mm(act, w2, name="ff2", pre=relu2, post=add, extras=(h1,))
    dh2, dh2b, loss_acc, d_norm_f = _final_norm_loss(h2.reshape(bsz, t, d), loss_target, norm_f_w.reshape(1, d),
                                                     name="loss")
    loss = lax.psum(loss_acc[0, 0], ("x", "y", "c"))

    dh2f, dh2bf = dh2.reshape(n, d), dh2b.reshape(n, d)
    dact = _mm(dh2bf, w2, name="ff2_bwd", nt=True, post=lambda r, a: r * (2.0 * jnp.maximum(a, 0).astype(F32)),
               extras=(act,), out_dtype=BF16)
    d_w2 = _mm_tn(act, dh2bf, name="ff2_dw", tk=2048, tn=1024, pre=relu2)
    d_w1 = _mm_tn(hn2, dact, name="ff1_dw", tk=1024, tn=2048)
    dhn2 = _mm(dact, w1, name="ff1_bwd", nt=True)
    dh1, dh1b, d_norm_ffn = _rms_bwd(dhn2, h1, norm_ffn_w, dh2f, name="norm_ffn_bwd")
    dypool = _mm(dh1b, wo_p, name="out_pool_bwd", nt=True)
    dyn = _mm(dh1b, wo_s, name="out_ssm_bwd", nt=True)
    d_wo_p = _mm_tn(ypool_f, dh1b, name="out_pool_dw", tk=512, tn=1024)
    d_wo_s = _mm_tn(yn_f, dh1b, name="out_ssm_dw", tk=1536, tn=1024)
    big_late = [jnp.concatenate([d_wo_p, d_wo_s], axis=0).reshape(4, (D_POOL + D_SSM) // 4, d),
                d_w1.reshape(d, 4, D_FF // 4).transpose(1, 0, 2), d_w2.reshape(4, D_FF // 4, d)]
    (dz, dxs, dbm, dcm, ddtr, d_nw, d_heads), landed_late = _ssd_bwd(
        xc, dtr3, z3, ypre, sprev, dyn.reshape(bsz, t, D_SSM), dtb, alog, dskip, ssm_norm_w, name="ssd_bwd",
        rider=_Exchange([b.astype(BF16) for b in big_late]))
    dpre, d_convwb = _conv_bwd_pre(xbc3, dxs, dbm, dcm, convw, conv_b, name="conv_bwd_pre")
    dxbc = _conv_bwd_in(dpre, convw, name="conv_bwd_in")
    du, d_poolw, d_poolsc = _pool_bwd(u.reshape(bsz, t, D_POOL), dypool.reshape(bsz, t, D_POOL), poolw, pool_scale,
                                      name="pool_bwd")
    duf, dzf, dxbcf, ddtrf = du.reshape(n, D_POOL), dz.reshape(n, D_SSM), dxbc.reshape(n, D_XBC), ddtr.reshape(n, D_DT)
    d_wu = _mm_tn(duf, hn1, name="proj_u_dw", tk=512, tn=1024)
    d_wz = _mm_tn(dzf, hn1, name="proj_z_dw", tk=1536, tn=1024)
    d_wx = _mm_tn(dxbcf, hn1, name="proj_xbc_dw", tk=1280, tn=1024)
    d_wdt = _mm_tn(ddtrf, hn1, name="proj_dt_dw", tk=512, tn=1024)
    d_win = jnp.concatenate([d_wu, d_wz, d_wx, d_wdt.reshape(N_GROUPS, LANES, d)[:, :HPG].reshape(N_HEADS, d)], axis=0)
    big_in = d_win.reshape(4, d_in // 4, d)
    dhn1, landed_in = _mm([duf, dzf, dxbcf, ddtrf], [wu, wz, wx, wdt], name="proj_bwd",
                          rider=_Exchange([big_in.astype(BF16)]))
    grad_x, d_head_rows, d_norm_mix = _input_grad(
        dhn1.reshape(bsz, t, d), h0, norm_mix_w, dh1.reshape(bsz, t, d), seq, name="input_grad")

    big = [big_in] + big_late
    landed = list(landed_in) + list(landed_late)
    heads = jnp.sum(d_heads, axis=0)
    small = _pack_small({
        "norm_mix_w": d_norm_mix, "pool_w": d_poolw, "pool_scale": d_poolsc,
        "conv_w": jnp.sum(d_convwb[:, :CONV_W], axis=0), "conv_b": jnp.sum(d_convwb[:, CONV_W:CONV_W + 1], axis=0),
        "dt_bias": _unpad_heads(heads[:, 2]), "a_log": _unpad_heads(heads[:, 1]), "d_skip": _unpad_heads(heads[:, 0]),
        "ssm_norm_w": jnp.sum(d_nw, axis=0), "norm_ffn_w": d_norm_ffn, "norm_f_w": d_norm_f,
        "meta": jnp.sum(d_head_rows[:, PAD:], axis=0)})
    (small_all,) = _comm(_Exchange([], small), name="exchange_small")
    own = [lax.dynamic_index_in_dim(b, chip, 0, keepdims=False) for b in big]
    mine = [_chip_sum(o, l, name=f"chip_sum_{i}") for i, (o, l) in enumerate(zip(own, landed))]
    theirs = _comm(_Swap(mine), name="swap_cores")
    gsmall = _unpack_small(_device_sum(small_all, name="device_sum"))
    gsmall["conv_w"] = lax.dynamic_slice_in_dim(gsmall["conv_w"], chip * (D_XBC // 4), D_XBC // 4, axis=1)
    gsmall["meta"] = lax.dynamic_slice_in_dim(gsmall["meta"], chip * (d // 4), d // 4, axis=1)

    given = dict(meta=(meta, m_meta, v_meta), norm_mix_w=(norm_mix_w, m_norm_mix_w, v_norm_mix_w),
                 w_in=(w_in, m_w_in, v_w_in), pool_w=(pool_w, m_pool_w, v_pool_w),
                 pool_scale=(pool_scale, m_pool_scale, v_pool_scale), conv_w=(conv_w, m_conv_w, v_conv_w),
                 conv_b=(conv_b, m_conv_b, v_conv_b), dt_bias=(dt_bias, m_dt_bias, v_dt_bias),
                 a_log=(a_log, m_a_log, v_a_log), d_skip=(d_skip, m_d_skip, v_d_skip),
                 ssm_norm_w=(ssm_norm_w, m_ssm_norm_w, v_ssm_norm_w), w_out=(w_out, m_w_out, v_w_out),
                 norm_ffn_w=(norm_ffn_w, m_norm_ffn_w, v_norm_ffn_w), w_ff1=(w_ff1, m_w_ff1, v_w_ff1),
                 w_ff2=(w_ff2, m_w_ff2, v_w_ff2), norm_f_w=(norm_f_w, m_norm_f_w, v_norm_f_w))
    big_names = ["w_in", "w_out", "w_ff1", "w_ff2"]
    results = {}
    for nm, (w, m, v) in given.items():
        if nm in big_names:
            i = big_names.index(nm)
            parts, shape2 = (mine[i], theirs[i]), mine[i].shape
        else:
            parts, shape2 = (gsmall[nm],), gsmall[nm].shape
        if nm == "w_in":
            outs = _adamw(w[0].T, parts, m[0].T, v[0].T, name=f"adamw_{nm}")
            results[nm] = [o.T[None] for o in outs]
        else:
            outs = _adamw(w.reshape(shape2), parts, m.reshape(shape2), v.reshape(shape2), name=f"adamw_{nm}")
            results[nm] = [o.reshape(w.shape) for o in outs]
    order = list(given)
    return (loss, grad_x, *[results[nm][0] for nm in order], *[results[nm][1] for nm in order],
            *[results[nm][2] for nm in order], *[results[nm][3] for nm in order])
```

```python
import functools

import jax
import jax.numpy as jnp
from jax import lax
from jax.experimental import pallas as pl
from jax.experimental.pallas import tpu as pltpu

F32 = jnp.float32
BF16 = jnp.bfloat16
MESH = pl.DeviceIdType.MESH
ANY = pl.BlockSpec(memory_space=pl.ANY)

D_MODEL = 1024
N_META = 16
CHUNK = 128
PAD = CHUNK - N_META
POOL_WINDOWS = (2, 4, 8, 16)
D_POOL = 512
POOL_GROUP = 128
D_SSM = 1536
N_HEADS = 24
N_GROUPS = 4
HPG = 6
HEAD_DIM = 64
D_STATE = 128
GW = HPG * HEAD_DIM
D_XBC = D_SSM + 2 * N_GROUPS * D_STATE
D_DT = N_GROUPS * 128
D_FF = 4096
CONV_W = 4
EPS = 1e-5
LANES = 128
VMEM_LIMIT = 56 * 1024 * 1024

ADAM_LR, ADAM_B1, ADAM_B2, ADAM_EPS, ADAM_WD, ADAM_STEP = 0.001, 0.9, 0.999, 1e-08, 0.01, 10


def _params(*sem):
    return pltpu.CompilerParams(dimension_semantics=sem, vmem_limit_bytes=VMEM_LIMIT)


def _pick(n, cands):
    for c in cands:
        if n % c == 0:
            return c
    raise ValueError(f"no block size for {n}")


def _dot(a, b):
    return jnp.dot(a.astype(BF16), b.astype(BF16), preferred_element_type=F32)


def _dot_nt(a, b):
    return lax.dot_general(a.astype(BF16), b.astype(BF16), (((1,), (1,)), ((), ())), preferred_element_type=F32)


def _dot_tn(a, b):
    return lax.dot_general(a.astype(BF16), b.astype(BF16), (((0,), (0,)), ((), ())), preferred_element_type=F32)


def _dot_exact(mask, x):
    m = mask.astype(BF16)
    hi = x.astype(BF16)
    r1 = x - hi.astype(F32)
    mid = r1.astype(BF16)
    lo = (r1 - mid.astype(F32)).astype(BF16)
    dot = lambda t: jnp.dot(m, t, preferred_element_type=F32)
    return dot(hi) + dot(mid) + dot(lo)


def _sigmoid(x):
    return 1.0 / (1.0 + jnp.exp(-x))


def _softplus(x):
    return jnp.maximum(x, 0.0) + jnp.log1p(jnp.exp(-jnp.abs(x)))


def _sum_all(x):
    return jnp.sum(jnp.sum(x, axis=1, keepdims=True), axis=0, keepdims=True)


ROW_TILES = (4224, 2816, 2112, 1408, 1056, 768, 704, 512, 384, 256, 128)
TILE_BUDGET = 28 * 1024 * 1024


def _row_tile(n, bytes_per_row, fixed_bytes):
    for tm in ROW_TILES:
        if n % tm == 0 and 2 * (tm * bytes_per_row + fixed_bytes) <= TILE_BUDGET:
            return tm
    raise ValueError(f"no row tile for {n}")


def _mm(a, w, *, name, tn=512, nt=False, pre=None, post=None, extras=(), out_dtype=F32, rider=None):
    a_list = list(a) if isinstance(a, (list, tuple)) else [a]
    w_list = list(w) if isinstance(w, (list, tuple)) else [w]
    n_a, n_ex = len(a_list), len(extras)
    n = a_list[0].shape[0]
    m = w_list[0].shape[0] if nt else w_list[0].shape[1]
    tn = min(tn, m)
    size = lambda dt: jnp.dtype(dt).itemsize
    per_row = (sum(x.shape[1] * size(x.dtype) for x in a_list) + tn * size(out_dtype)
               + sum(tn * size(e.dtype) for e in extras))
    tm = _row_tile(n, per_row, sum(x.shape[1 if nt else 0] * tn * size(x.dtype) for x in w_list))

    def body(*refs):
        a_refs, w_refs, ex_refs, o_ref = refs[:n_a], refs[n_a:2 * n_a], refs[2 * n_a:2 * n_a + n_ex], refs[2 * n_a + n_ex]
        r = None
        for a_ref, w_ref in zip(a_refs, w_refs):
            av = a_ref[...]
            if pre is not None:
                av = pre(av)
            term = _dot_nt(av, w_ref[...]) if nt else _dot(av, w_ref[...])
            r = term if r is None else r + term
        if post is not None:
            r = post(r, *[e[...] for e in ex_refs])
        o_ref[...] = r.astype(out_dtype)

    a_specs = [pl.BlockSpec((tm, x.shape[1]), lambda i, j: (i, 0)) for x in a_list]
    w_specs = [pl.BlockSpec((tn, x.shape[1]), lambda i, j: (j, 0)) if nt else pl.BlockSpec((x.shape[0], tn), lambda i, j: (0, j))
               for x in w_list]
    blk = pl.BlockSpec((tm, tn), lambda i, j: (i, j))
    grid = (n // tm, m // tn)
    ride = _Ride(rider, body, 2 * n_a + n_ex, 1, 0, grid)
    outs = pl.pallas_call(
        ride.body, name=name, grid=grid,
        in_specs=a_specs + w_specs + [blk] * n_ex + ride.in_specs,
        out_specs=[blk] + ride.out_specs, out_shape=[jax.ShapeDtypeStruct((n, m), out_dtype)] + ride.out_shape,
        scratch_shapes=ride.scratch, compiler_params=_params(*ride.semantics(("parallel", "parallel"))),
    )(*a_list, *w_list, *extras, *ride.args)
    return (outs[0], outs[1:]) if rider else outs[0]


def _mm_tn(a, g, *, name, tk, tn, pre=None):
    n, k = a.shape
    m = g.shape[1]
    tk, tn = min(tk, k), min(tn, m)
    tm = _row_tile(n, tk * jnp.dtype(a.dtype).itemsize + tn * jnp.dtype(g.dtype).itemsize, tk * tn * 4)

    def body(a_ref, g_ref, o_ref):
        @pl.when(pl.program_id(2) == 0)
        def _():
            o_ref[...] = jnp.zeros_like(o_ref)

        av = a_ref[...]
        if pre is not None:
            av = pre(av)
        o_ref[...] += _dot_tn(av, g_ref[...])

    return pl.pallas_call(
        body, name=name, grid=(k // tk, m // tn, n // tm),
        in_specs=[pl.BlockSpec((tm, tk), lambda i, j, r: (r, i)), pl.BlockSpec((tm, tn), lambda i, j, r: (r, j))],
        out_specs=pl.BlockSpec((tk, tn), lambda i, j, r: (i, j)),
        out_shape=jax.ShapeDtypeStruct((k, m), F32),
        compiler_params=_params("parallel", "parallel", "arbitrary"),
    )(a, g)


def _rms_fwd(h, w, *, name):
    n, d = h.shape
    tm = _pick(n, (768, 512, 256, 128))

    def body(h_ref, w_ref, o_ref):
        x = h_ref[...]
        r = lax.rsqrt(jnp.mean(x * x, axis=-1, keepdims=True) + EPS)
        o_ref[...] = (x * r * w_ref[...]).astype(BF16)

    return pl.pallas_call(
        body, name=name, grid=(n // tm,),
        in_specs=[pl.BlockSpec((tm, d), lambda i: (i, 0)), pl.BlockSpec((1, d), lambda i: (0, 0))],
        out_specs=pl.BlockSpec((tm, d), lambda i: (i, 0)), out_shape=jax.ShapeDtypeStruct((n, d), BF16),
        compiler_params=_params("parallel"),
    )(h, w)


def _rms_bwd(dy, h, w, dres, *, name):
    n, d = h.shape
    tm = _pick(n, (768, 512, 256, 128))

    def body(dy_ref, h_ref, w_ref, dres_ref, dx_ref, dxb_ref, dw_ref):
        @pl.when(pl.program_id(0) == 0)
        def _():
            dw_ref[...] = jnp.zeros_like(dw_ref)

        x, dyv = h_ref[...], dy_ref[...]
        r = lax.rsqrt(jnp.mean(x * x, axis=-1, keepdims=True) + EPS)
        g = dyv * w_ref[...]
        dx = r * (g - x * (r * r) * jnp.mean(g * x, axis=-1, keepdims=True)) + dres_ref[...]
        dx_ref[...] = dx
        dxb_ref[...] = dx.astype(BF16)
        dw_ref[...] += jnp.sum(dyv * x * r, axis=0, keepdims=True)

    row = pl.BlockSpec((tm, d), lambda i: (i, 0))
    vec = pl.BlockSpec((1, d), lambda i: (0, 0))
    return pl.pallas_call(
        body, name=name, grid=(n // tm,), in_specs=[row, row, vec, row], out_specs=[row, row, vec],
        out_shape=[jax.ShapeDtypeStruct((n, d), F32), jax.ShapeDtypeStruct((n, d), BF16), jax.ShapeDtypeStruct((1, d), F32)],
        compiler_params=_params("arbitrary"),
    )(dy, h, w, dres)


def _final_norm_loss(h2, target, w, *, name):
    bsz, t, d = h2.shape
    nc = t // CHUNK

    def body(h_ref, t_ref, w_ref, dh_ref, dhb_ref, loss_ref, dw_ref):
        j = pl.program_id(1)

        @pl.when((pl.program_id(0) == 0) & (j == 0))
        def _():
            loss_ref[...] = jnp.zeros_like(loss_ref)
            dw_ref[...] = jnp.zeros_like(dw_ref)

        x, wv = h_ref[0], w_ref[...]
        r = lax.rsqrt(jnp.mean(x * x, axis=-1, keepdims=True) + EPS)
        diff = jnp.where(j > 0, x * r * wv - t_ref[0], 0.0)
        loss_ref[...] += _sum_all(diff * diff) * (0.5 / d)
        dy = diff * (1.0 / d)
        g = dy * wv
        dh = r * (g - x * (r * r) * jnp.mean(g * x, axis=-1, keepdims=True))
        dh_ref[0] = dh
        dhb_ref[0] = dh.astype(BF16)
        dw_ref[...] += jnp.sum(dy * x * r, axis=0, keepdims=True)

    row = pl.BlockSpec((1, CHUNK, d), lambda b, j: (b, j, 0))
    return pl.pallas_call(
        body, name=name, grid=(bsz, nc),
        in_specs=[row, pl.BlockSpec((1, CHUNK, d), lambda b, j: (b, jnp.maximum(j - 1, 0), 0)),
                  pl.BlockSpec((1, d), lambda b, j: (0, 0))],
        out_specs=[row, row, pl.BlockSpec((8, LANES), lambda b, j: (0, 0)), pl.BlockSpec((1, d), lambda b, j: (0, 0))],
        out_shape=[jax.ShapeDtypeStruct((bsz, t, d), F32), jax.ShapeDtypeStruct((bsz, t, d), BF16),
                   jax.ShapeDtypeStruct((8, LANES), F32), jax.ShapeDtypeStruct((1, d), F32)],
        compiler_params=_params("arbitrary", "arbitrary"),
    )(h2, target, w)


def _pool_masks(j, transposed):
    r = lax.broadcasted_iota(jnp.int32, (CHUNK, 2 * CHUNK), 0)
    c = lax.broadcasted_iota(jnp.int32, (CHUNK, 2 * CHUNK), 1)
    masks = []
    for w in POOL_WINDOWS:
        if transposed:
            m = (c >= r) & (c < r + w)
        else:
            s = c - CHUNK
            m = (s <= r) & (s > r - w) & (s + j * CHUNK >= 0)
        masks.append(m.astype(F32))
    return masks


def _pool_count(t_global, w):
    return jnp.clip(t_global - PAD + 1, 1, w).astype(F32)


def _pool_fwd(u, pool_w, pool_scale, *, name):
    bsz, t, _ = u.shape
    nc = t // CHUNK

    def body(prev_ref, cur_ref, pw_ref, sc_ref, o_ref):
        j = pl.program_id(0)
        masks = _pool_masks(j, False)
        tg = j * CHUNK + lax.broadcasted_iota(jnp.int32, (CHUNK, 1), 0)
        count = [_pool_count(tg, w) for w in POOL_WINDOWS]
        units = [(e, gi) for e in range(bsz) for gi in range(len(POOL_WINDOWS))]
        sl = lambda gi: pl.ds(gi * POOL_GROUP, POOL_GROUP)
        cur = {(e, gi): cur_ref[e, :, sl(gi)] for e, gi in units}
        both = {(e, gi): jnp.concatenate([prev_ref[e, :, sl(gi)], cur[e, gi]], axis=0) for e, gi in units}
        win = {(e, gi): _dot_exact(masks[gi], both[e, gi]) for e, gi in units}
        pooled = {(e, gi): win[e, gi] / count[gi] - cur[e, gi] for e, gi in units}
        mixed = {(e, gi): _dot(pooled[e, gi], pw_ref[gi]) for e, gi in units}
        for e, gi in units:
            o_ref[e, :, sl(gi)] = (mixed[e, gi] * sc_ref[:, sl(gi)]).astype(BF16)

    blk = lambda f: pl.BlockSpec((bsz, CHUNK, D_POOL), f)
    return pl.pallas_call(
        body, name=name, grid=(nc,),
        in_specs=[blk(lambda j: (0, jnp.maximum(j - 1, 0), 0)), blk(lambda j: (0, j, 0)),
                  pl.BlockSpec((4, POOL_GROUP, POOL_GROUP), lambda j: (0, 0, 0)),
                  pl.BlockSpec((1, D_POOL), lambda j: (0, 0))],
        out_specs=blk(lambda j: (0, j, 0)), out_shape=jax.ShapeDtypeStruct(u.shape, BF16),
        compiler_params=_params("parallel"),
    )(u, u, pool_w, pool_scale)


def _pool_bwd(u, dyp, pool_w, pool_scale, *, name):
    bsz, t, _ = u.shape
    nc = t // CHUNK

    def body(prev_ref, cur_ref, dy_ref, dyn_ref, pw_ref, sc_ref, du_ref, dpw_ref, dsc_ref):
        j = pl.program_id(0)

        @pl.when(j == 0)
        def _():
            dpw_ref[...] = jnp.zeros_like(dpw_ref)
            dsc_ref[...] = jnp.zeros_like(dsc_ref)

        fwd = _pool_masks(j, False)
        bwd = _pool_masks(j, True)
        tg = j * CHUNK + lax.broadcasted_iota(jnp.int32, (CHUNK, 1), 0)
        count = [_pool_count(tg, w) for w in POOL_WINDOWS]
        count_next = [_pool_count(tg + CHUNK, w) for w in POOL_WINDOWS]
        has_next = j < nc - 1
        groups = range(len(POOL_WINDOWS))
        units = [(e, gi) for e in range(bsz) for gi in groups]
        sl = lambda gi: pl.ds(gi * POOL_GROUP, POOL_GROUP)
        cur = {(e, gi): cur_ref[e, :, sl(gi)] for e, gi in units}
        both = {(e, gi): jnp.concatenate([prev_ref[e, :, sl(gi)], cur[e, gi]], axis=0) for e, gi in units}
        win = {(e, gi): _dot_exact(fwd[gi], both[e, gi]) for e, gi in units}
        pooled = {(e, gi): win[e, gi] / count[gi] - cur[e, gi] for e, gi in units}
        dy = {(e, gi): dy_ref[e, :, sl(gi)] for e, gi in units}
        mixed = {(e, gi): _dot(pooled[e, gi], pw_ref[gi]) for e, gi in units}
        dm = {(e, gi): dy[e, gi] * sc_ref[:, sl(gi)] for e, gi in units}
        dm_next = {(e, gi): jnp.where(has_next, dyn_ref[e, :, sl(gi)], 0.0) * sc_ref[:, sl(gi)] for e, gi in units}
        dpw = {(e, gi): _dot_tn(pooled[e, gi], dm[e, gi]) for e, gi in units}
        dpooled = {(e, gi): _dot_nt(dm[e, gi], pw_ref[gi]) for e, gi in units}
        dpooled_next = {(e, gi): _dot_nt(dm_next[e, gi], pw_ref[gi]) for e, gi in units}
        spread = {(e, gi): jnp.concatenate([dpooled[e, gi] / count[gi], dpooled_next[e, gi] / count_next[gi]], axis=0)
                  for e, gi in units}
        back = {(e, gi): _dot_exact(bwd[gi], spread[e, gi]) for e, gi in units}
        for e, gi in units:
            du_ref[e, :, sl(gi)] = (back[e, gi] - dpooled[e, gi]).astype(BF16)
        for gi in groups:
            dsc, dw = None, None
            for e in range(bsz):
                term = jnp.sum(dy[e, gi] * mixed[e, gi], axis=0, keepdims=True)
                dsc = term if dsc is None else dsc + term
                dw = dpw[e, gi] if dw is None else dw + dpw[e, gi]
            dsc_ref[:, sl(gi)] += dsc
            dpw_ref[gi] += dw

    blk = lambda f: pl.BlockSpec((bsz, CHUNK, D_POOL), f)
    return pl.pallas_call(
        body, name=name, grid=(nc,),
        in_specs=[blk(lambda j: (0, jnp.maximum(j - 1, 0), 0)), blk(lambda j: (0, j, 0)),
                  blk(lambda j: (0, j, 0)), blk(lambda j: (0, jnp.minimum(j + 1, nc - 1), 0)),
                  pl.BlockSpec((4, POOL_GROUP, POOL_GROUP), lambda j: (0, 0, 0)),
                  pl.BlockSpec((1, D_POOL), lambda j: (0, 0))],
        out_specs=[blk(lambda j: (0, j, 0)), pl.BlockSpec((4, POOL_GROUP, POOL_GROUP), lambda j: (0, 0, 0)),
                   pl.BlockSpec((1, D_POOL), lambda j: (0, 0))],
        out_shape=[jax.ShapeDtypeStruct(u.shape, BF16), jax.ShapeDtypeStruct((4, POOL_GROUP, POOL_GROUP), F32),
                   jax.ShapeDtypeStruct((1, D_POOL), F32)],
        compiler_params=_params("arbitrary"),
    )(u, u, dyp, dyp, pool_w, pool_scale)


CONV_SLAB = 512


def _conv_taps(tail, cur, keep_tail):
    ext = jnp.concatenate([jnp.where(keep_tail, tail, 0.0), cur], axis=0)
    return [(pltpu.roll(ext, CONV_W - 1 - k, 0) if k < CONV_W - 1 else ext)[8:] for k in range(CONV_W)]


def _conv_pre(taps, w_ref, b_ref, sl):
    acc = b_ref[:, sl]
    for k in range(CONV_W):
        acc = acc + w_ref[k:k + 1, sl] * taps[k]
    return acc


def _conv_fwd(xbc, conv_w, conv_b, *, name):
    bsz, t, c = xbc.shape
    nc = t // CHUNK

    def body(tail_ref, cur_ref, w_ref, b_ref, o_ref):
        keep = pl.program_id(1) > 0
        for c0 in range(0, c, CONV_SLAB):
            sl = pl.ds(c0, CONV_SLAB)
            pre = _conv_pre(_conv_taps(tail_ref[0, :, sl], cur_ref[0, :, sl], keep), w_ref, b_ref, sl)
            o_ref[0, :, sl] = (pre * _sigmoid(pre)).astype(BF16)

    return pl.pallas_call(
        body, name=name, grid=(bsz, nc),
        in_specs=[pl.BlockSpec((1, 8, c), lambda b, j: (b, jnp.maximum(j * (CHUNK // 8) - 1, 0), 0)),
                  pl.BlockSpec((1, CHUNK, c), lambda b, j: (b, j, 0)),
                  pl.BlockSpec((CONV_W, c), lambda b, j: (0, 0)), pl.BlockSpec((1, c), lambda b, j: (0, 0))],
        out_specs=pl.BlockSpec((1, CHUNK, c), lambda b, j: (b, j, 0)), out_shape=jax.ShapeDtypeStruct(xbc.shape, BF16),
        compiler_params=_params("parallel", "parallel"),
    )(xbc, xbc, conv_w, conv_b)


def _conv_bwd_pre(xbc, dxs, db, dc, conv_w, conv_b, *, name):
    bsz, t, c = xbc.shape
    nc = t // CHUNK

    def body(tail_ref, cur_ref, dxs_ref, db_ref, dc_ref, w_ref, b_ref, dpre_ref, dwb_ref):
        j = pl.program_id(1)

        @pl.when(j == 0)
        def _():
            dwb_ref[...] = jnp.zeros_like(dwb_ref)

        for c0 in range(0, c, CONV_SLAB):
            sl = pl.ds(c0, CONV_SLAB)
            if c0 < D_SSM:
                dxc = dxs_ref[0, :, sl]
            else:
                dxc = (db_ref if c0 < D_SSM + D_POOL else dc_ref)[0]
            taps = _conv_taps(tail_ref[0, :, sl], cur_ref[0, :, sl], j > 0)
            pre = _conv_pre(taps, w_ref, b_ref, sl)
            s = _sigmoid(pre)
            dpre = dxc * (s * (1.0 + pre * (1.0 - s)))
            dpre_ref[0, :, sl] = dpre.astype(BF16)
            for k in range(CONV_W):
                dwb_ref[0, k:k + 1, sl] += jnp.sum(dpre * taps[k], axis=0, keepdims=True)
            dwb_ref[0, CONV_W:CONV_W + 1, sl] += jnp.sum(dpre, axis=0, keepdims=True)

    assert CONV_SLAB == D_POOL and D_SSM % CONV_SLAB == 0
    row = lambda width: pl.BlockSpec((1, CHUNK, width), lambda b, j: (b, j, 0))
    return pl.pallas_call(
        body, name=name, grid=(bsz, nc),
        in_specs=[pl.BlockSpec((1, 8, c), lambda b, j: (b, jnp.maximum(j * (CHUNK // 8) - 1, 0), 0)),
                  row(c), row(D_SSM), row(D_POOL), row(D_POOL),
                  pl.BlockSpec((CONV_W, c), lambda b, j: (0, 0)), pl.BlockSpec((1, c), lambda b, j: (0, 0))],
        out_specs=[row(c), pl.BlockSpec((1, 8, c), lambda b, j: (b, 0, 0))],
        out_shape=[jax.ShapeDtypeStruct(xbc.shape, BF16), jax.ShapeDtypeStruct((bsz, 8, c), F32)],
        compiler_params=_params("parallel", "arbitrary"),
    )(xbc, xbc, dxs, db, dc, conv_w, conv_b)


def _conv_bwd_in(dpre, conv_w, *, name):
    bsz, t, c = dpre.shape
    nc = t // CHUNK

    halo = 16

    def body(cur_ref, head_ref, w_ref, o_ref):
        keep = pl.program_id(1) < nc - 1
        for c0 in range(0, c, CONV_SLAB):
            sl = pl.ds(c0, CONV_SLAB)
            ext = jnp.concatenate([cur_ref[0, :, sl].astype(F32),
                                   jnp.where(keep, head_ref[0, :, sl].astype(F32), 0.0)], axis=0)
            acc = w_ref[CONV_W - 1:CONV_W, sl] * ext[:CHUNK]
            for k in range(CONV_W - 1):
                up = CONV_W - 1 - k
                acc = acc + w_ref[k:k + 1, sl] * pltpu.roll(ext, CHUNK + halo - up, 0)[:CHUNK]
            o_ref[0, :, sl] = acc.astype(BF16)

    return pl.pallas_call(
        body, name=name, grid=(bsz, nc),
        in_specs=[pl.BlockSpec((1, CHUNK, c), lambda b, j: (b, j, 0)),
                  pl.BlockSpec((1, halo, c), lambda b, j: (b, jnp.minimum((j + 1) * (CHUNK // halo), t // halo - 1), 0)),
                  pl.BlockSpec((CONV_W, c), lambda b, j: (0, 0))],
        out_specs=pl.BlockSpec((1, CHUNK, c), lambda b, j: (b, j, 0)), out_shape=jax.ShapeDtypeStruct(dpre.shape, BF16),
        compiler_params=_params("parallel", "parallel"),
    )(dpre, dpre, conv_w)


def _dt_valid(j):
    lane = lax.broadcasted_iota(jnp.int32, (CHUNK, LANES), 1)
    row = lax.broadcasted_iota(jnp.int32, (CHUNK, LANES), 0)
    return (lane < HPG) & ((j > 0) | (row >= PAD))


def _ssd_prep(dtr, dtb, alog, *, name):
    bsz, t, _ = dtr.shape
    nc = t // CHUNK

    def body(dtr_ref, dtb_ref, alog_ref, dt_ref, acs_ref, tr_ref):
        j = pl.program_id(0)
        valid = _dt_valid(j)
        row = lax.broadcasted_iota(jnp.int32, (CHUNK, LANES), 0)
        lane = lax.broadcasted_iota(jnp.int32, (CHUNK, LANES), 1)
        tril = (row >= lane).astype(F32)
        units = [(e, g) for e in range(bsz) for g in range(N_GROUPS)]
        sl = lambda g: pl.ds(g * LANES, LANES)
        dt = {(e, g): jnp.where(valid, _softplus(dtr_ref[e, :, sl(g)] + dtb_ref[g]), 0.0) for e, g in units}
        acs = {(e, g): _dot_exact(tril, dt[e, g] * -jnp.exp(alog_ref[g])) for e, g in units}
        for e, g in units:
            dt_ref[e, :, sl(g)] = dt[e, g]
            acs_ref[e, :, sl(g)] = acs[e, g]
            tr_ref[e, 0, g, 0:8, :] = dt[e, g].T[0:8]
            tr_ref[e, 0, g, 8:16, :] = acs[e, g].T[0:8]

    blk = pl.BlockSpec((bsz, CHUNK, D_DT), lambda j: (0, j, 0))
    const = pl.BlockSpec((N_GROUPS, 1, LANES), lambda j: (0, 0, 0))
    return pl.pallas_call(
        body, name=name, grid=(nc,), in_specs=[blk, const, const],
        out_specs=[blk, blk, pl.BlockSpec((bsz, 1, N_GROUPS, 16, LANES), lambda j: (0, j, 0, 0, 0))],
        out_shape=[jax.ShapeDtypeStruct(dtr.shape, F32), jax.ShapeDtypeStruct(dtr.shape, F32),
                   jax.ShapeDtypeStruct((bsz, nc, N_GROUPS, 16, LANES), F32)],
        compiler_params=_params("parallel"),
    )(dtr, dtb, alog)


def _ssd_decay(dt, acs, tr):
    lane = lax.broadcasted_iota(jnp.int32, (CHUNK, LANES), 1)
    row = lax.broadcasted_iota(jnp.int32, (CHUNK, LANES), 0)
    return dict(lane=lane, row=row, dt=dt, causal=row >= lane, acs=acs, acs_t=tr[8:16], dt_t=tr[0:8],
                aend=acs[CHUNK - 1:CHUNK, :])


def _ssd_specs(bsz, nc, rev):
    ch = (lambda j: nc - 1 - j) if rev else (lambda j: j)
    return dict(
        xs=pl.BlockSpec((bsz, CHUNK, GW), lambda g, j: (0, ch(j), g)),
        bm=pl.BlockSpec((bsz, CHUNK, D_STATE), lambda g, j: (0, ch(j), D_SSM // D_STATE + g)),
        cm=pl.BlockSpec((bsz, CHUNK, D_STATE), lambda g, j: (0, ch(j), D_SSM // D_STATE + N_GROUPS + g)),
        lane_blk=pl.BlockSpec((bsz, CHUNK, LANES), lambda g, j: (0, ch(j), g)),
        grp_const=pl.BlockSpec((1, 1, LANES), lambda g, j: (g, 0, 0)),
        grp_vec=pl.BlockSpec((1, GW), lambda g, j: (0, g)),
        state=pl.BlockSpec((bsz, 1, D_STATE, GW), lambda g, j: (0, ch(j), 0, g)),
        tr=pl.BlockSpec((bsz, 1, 1, 16, LANES), lambda g, j: (0, ch(j), g, 0, 0)),
    )


def _ssd_fwd(xc, dt, acs, tr, z, dskip, normw, *, name, rider=None):
    bsz, t, _ = xc.shape
    nc = t // CHUNK
    sp = _ssd_specs(bsz, nc, False)

    def body(xs_ref, b_ref, c_ref, dt_ref, acs_ref, tr_ref, z_ref, dsk_ref, nw_ref, yn_ref, y_ref, sp_ref, s_ref):
        j = pl.program_id(1)

        @pl.when(j == 0)
        def _():
            s_ref[...] = jnp.zeros_like(s_ref)

        ex = range(bsz)
        units = [(e, r) for e in ex for r in range(HPG)]
        full = lambda v: jnp.broadcast_to(v, (CHUNK, LANES))
        pair = lambda r: pl.ds((r // 2) * LANES, LANES)
        q = [_ssd_decay(dt_ref[e], acs_ref[e], tr_ref[e, 0, 0]) for e in ex]
        for e in ex:
            sp_ref[e, 0] = s_ref[e]
        bm, cm = [b_ref[e] for e in ex], [c_ref[e] for e in ex]
        cb = [_dot_nt(cm[e], bm[e]) for e in ex]
        low = q[0]["lane"] < HEAD_DIM
        col = {(e, r): full(q[e]["acs"][:, r:r + 1]) for e, r in units}
        aend = {(e, r): q[e]["aend"][:, r:r + 1] for e, r in units}
        decay = {(e, r): jnp.exp(jnp.where(q[e]["causal"], col[e, r] - q[e]["acs_t"][r:r + 1, :], -jnp.inf))
                 for e, r in units}
        mp = {(e, r): cb[e] * decay[e, r] * q[e]["dt_t"][r:r + 1, :] for e, r in units}
        ce = {(e, r): cm[e] * jnp.exp(col[e, r]) for e, r in units}
        bk = {(e, r): bm[e] * (jnp.exp(aend[e, r] - col[e, r]) * full(q[e]["dt"][:, r:r + 1])) for e, r in units}
        xp = {(e, r): xs_ref[e, :, pair(r)] for e, r in units}
        s_old = {(e, r): s_ref[e, :, pair(r)] for e, r in units}
        y_h = {u: _dot(mp[u], xp[u]) + _dot(ce[u], s_old[u]) for u in units}
        s_h = {u: jnp.exp(aend[u]) * s_old[u] + _dot_tn(bk[u], xp[u]) for u in units}
        for e in ex:
            for r in range(0, HPG, 2):
                y_ref[e, :, pair(r)] = jnp.where(low, y_h[e, r], y_h[e, r + 1])
                s_ref[e, :, pair(r)] = jnp.where(low, s_h[e, r], s_h[e, r + 1])
        y = [y_ref[e] + dsk_ref[...] * xs_ref[e] for e in ex]
        zz = [z_ref[e] for e in ex]
        yg = [y[e] * (zz[e] * _sigmoid(zz[e])) for e in ex]
        rstd = [lax.rsqrt(jnp.mean(yg[e] * yg[e], axis=-1, keepdims=True) + EPS) for e in ex]
        for e in ex:
            y_ref[e] = y[e]
            yn_ref[e] = (yg[e] * rstd[e] * nw_ref[...]).astype(BF16)

    grid = (N_GROUPS, nc)
    ride = _Ride(rider, body, 9, 3, 1, grid)
    outs = pl.pallas_call(
        ride.body, name=name, grid=grid,
        in_specs=[sp["xs"], sp["bm"], sp["cm"], sp["lane_blk"], sp["lane_blk"], sp["tr"], sp["xs"],
                  sp["grp_vec"], sp["grp_vec"]] + ride.in_specs,
        out_specs=[sp["xs"], sp["xs"], sp["state"]] + ride.out_specs,
        out_shape=[jax.ShapeDtypeStruct((bsz, t, D_SSM), BF16), jax.ShapeDtypeStruct((bsz, t, D_SSM), F32),
                   jax.ShapeDtypeStruct((bsz, nc, D_STATE, D_SSM), F32)] + ride.out_shape,
        scratch_shapes=[pltpu.VMEM((bsz, D_STATE, GW), F32)] + ride.scratch,
        compiler_params=_params(*ride.semantics(("parallel", "arbitrary"))),
    )(xc, xc, xc, dt, acs, tr, z, dskip, normw, *ride.args)
    return outs[:3], outs[3:]


def _ssd_bwd(xc, dtr, dt, acs, tr, z, ypre, sprev, dyn, dtb, alog, dskip, normw, *, name, rider=None):
    bsz, t, _ = xc.shape
    nc = t // CHUNK
    sp = _ssd_specs(bsz, nc, True)

    def body(xs_ref, b_ref, c_ref, dtr_ref, dt_ref, acs_ref, tr_ref, z_ref, y_ref, sp_ref, dyn_ref, dtb_ref, alog_ref,
             dsk_ref, nw_ref, dz_ref, dxs_ref, db_ref, dc_ref, ddt_ref, dnw_ref, dsm_ref, ds_ref):
        j = pl.program_id(1)

        @pl.when(j == 0)
        def _():
            ds_ref[...] = jnp.zeros_like(ds_ref)
            dnw_ref[...] = jnp.zeros_like(dnw_ref)
            dsm_ref[...] = jnp.zeros_like(dsm_ref)

        ex = range(bsz)
        heads = range(HPG)
        units = [(e, r) for e in ex for r in heads]
        q = [_ssd_decay(dt_ref[e], acs_ref[e], tr_ref[e, 0, 0]) for e in ex]
        a = -jnp.exp(alog_ref[0])
        valid = _dt_valid(nc - 1 - j)
        lane, row = q[0]["lane"], q[0]["row"]
        lane1 = lane[0:1, :]
        nw = nw_ref[...]
        y, zz, dyn = [y_ref[e] for e in ex], [z_ref[e] for e in ex], [dyn_ref[e] for e in ex]
        sz = [_sigmoid(zz[e]) for e in ex]
        sil = [zz[e] * sz[e] for e in ex]
        yg = [y[e] * sil[e] for e in ex]
        rstd = [lax.rsqrt(jnp.mean(yg[e] * yg[e], axis=-1, keepdims=True) + EPS) for e in ex]
        gn = [dyn[e] * nw for e in ex]
        dyg = [rstd[e] * (gn[e] - yg[e] * (rstd[e] * rstd[e]) * jnp.mean(gn[e] * yg[e], axis=-1, keepdims=True))
               for e in ex]
        dy = [dyg[e] * sil[e] for e in ex]
        xs = [xs_ref[e] for e in ex]
        for e in ex:
            dnw_ref[e] += jnp.sum(dyn[e] * yg[e] * rstd[e], axis=0, keepdims=True)
            dz_ref[e] = (dyg[e] * y[e] * (sz[e] * (1.0 + zz[e] * (1.0 - sz[e])))).astype(BF16)
        dskip_cols = [jnp.sum(dy[e] * xs[e], axis=0, keepdims=True) for e in ex]

        bm, cm = [b_ref[e] for e in ex], [c_ref[e] for e in ex]
        cb = [_dot_nt(cm[e], bm[e]) for e in ex]
        zero = jnp.zeros((CHUNK, LANES), F32)
        full = lambda v: jnp.broadcast_to(v, (CHUNK, LANES))
        low = lane < HEAD_DIM
        half = [low if r % 2 == 0 else ~low for r in heads]
        sl = lambda v, r: v[:, (r // 2) * LANES:(r // 2 + 1) * LANES]
        pair = lambda r: pl.ds((r // 2) * LANES, LANES)
        col = {(e, r): full(q[e]["acs"][:, r:r + 1]) for e, r in units}
        dt_col = {(e, r): full(q[e]["dt"][:, r:r + 1]) for e, r in units}
        aend = {(e, r): q[e]["aend"][:, r:r + 1] for e, r in units}
        dt_row = {(e, r): q[e]["dt_t"][r:r + 1, :] for e, r in units}
        decay = {(e, r): jnp.exp(jnp.where(q[e]["causal"], col[e, r] - q[e]["acs_t"][r:r + 1, :], -jnp.inf))
                 for e, r in units}
        ea = {u: jnp.exp(col[u]) for u in units}
        dte = {u: jnp.exp(aend[u] - col[u]) for u in units}
        ed = {u: jnp.exp(aend[u]) for u in units}
        k = {u: dte[u] * dt_col[u] for u in units}
        mp = {(e, r): cb[e] * decay[e, r] * dt_row[e, r] for e, r in units}
        xp = {(e, r): sl(xs[e], r) for e, r in units}
        dym = {(e, r): jnp.where(half[r], sl(dy[e], r), 0.0) for e, r in units}
        s_old = {(e, r): sp_ref[e, 0, :, pair(r)] for e, r in units}
        ds_old = {(e, r): ds_ref[e, :, pair(r)] for e, r in units}
        dsm = {(e, r): jnp.where(half[r], ds_old[e, r], 0.0) for e, r in units}
        gmat = {u: _dot_nt(dym[u], xp[u]) for u in units}
        t1 = {u: _dot_nt(dym[u], s_old[u]) for u in units}
        dbs = {u: _dot_nt(xp[u], dsm[u]) for u in units}
        dx = {(e, r): _dot_tn(mp[e, r], dym[e, r]) + _dot(bm[e] * k[e, r], dsm[e, r]) for e, r in units}
        ds = {(e, r): _dot_tn(cm[e] * ea[e, r], dym[e, r]) for e, r in units}
        gd = {u: gmat[u] * decay[u] for u in units}
        w0 = {(e, r): gd[e, r] * cb[e] for e, r in units}
        cs0 = {u: jnp.sum(w0[u], axis=0, keepdims=True) for u in units}
        rs = {u: jnp.sum(w0[u] * dt_row[u], axis=1, keepdims=True) for u in units}
        qv = {(e, r): jnp.sum(cm[e] * t1[e, r], axis=1, keepdims=True) for e, r in units}
        dk = {(e, r): jnp.sum(bm[e] * dbs[e, r], axis=1, keepdims=True) for e, r in units}
        ddte = {u: dk[u] * dt_col[u] for u in units}
        d_aend = {u: _sum_all(dsm[u] * s_old[u]) * ed[u] + _sum_all(ddte[u][:, 0:1] * dte[u][:, 0:1]) for u in units}
        last_row = row == CHUNK - 1
        dacs_col = {u: rs[u] + qv[u] * ea[u] - ddte[u] * dte[u] + jnp.where(last_row, d_aend[u], 0.0) for u in units}
        triu = (lane >= row).astype(F32)
        for e in ex:
            dcb, dc_acc, db_acc = zero, zero, zero
            dacs, dacs_t, ddt, ddt_t = zero, zero, zero, zero
            dskip_row = jnp.zeros((1, LANES), F32)
            for r in heads:
                u = (e, r)
                dcb = dcb + gd[u] * dt_row[u]
                dc_acc = dc_acc + ea[u] * t1[u]
                db_acc = db_acc + k[u] * dbs[u]
                dacs = jnp.where(lane == r, dacs_col[u], dacs)
                ddt = jnp.where(lane == r, dk[u] * dte[u], ddt)
                dacs_t = jnp.where(row == r, -cs0[u] * dt_row[u], dacs_t)
                ddt_t = jnp.where(row == r, cs0[u], ddt_t)
                dsk = _sum_all(jnp.where(half[r][0:1, :], sl(dskip_cols[e], r), 0.0))
                dskip_row = dskip_row + jnp.where(lane1 == r, dsk, 0.0)
            for r in range(0, HPG, 2):
                dxs_ref[e, :, pair(r)] = (dx[e, r] + dx[e, r + 1] + sl(dy[e], r) * dsk_ref[:, pair(r)]).astype(BF16)
                ed_pair = jnp.where(lane1 < HEAD_DIM, ed[e, r], ed[e, r + 1])
                ds_ref[e, :, pair(r)] = ds[e, r] + ds[e, r + 1] + ed_pair * ds_old[e, r]
            dacs = dacs + dacs_t.T
            ddt = ddt + ddt_t.T
            dda = _dot_exact(triu, dacs)
            ddt = ddt + dda * a
            da = jnp.sum(dda * q[e]["dt"], axis=0, keepdims=True)
            draw = jnp.where(valid, ddt * _sigmoid(dtr_ref[e] + dtb_ref[0]), 0.0)
            ddt_ref[e] = draw.astype(BF16)
            dsm_ref[e, 0, 0:1, :] += dskip_row
            dsm_ref[e, 0, 1:2, :] += da * a
            dsm_ref[e, 0, 2:3, :] += jnp.sum(draw, axis=0, keepdims=True)
            dc_ref[e] = (dc_acc + _dot(dcb, bm[e])).astype(BF16)
            db_ref[e] = (db_acc + _dot_tn(dcb, cm[e])).astype(BF16)

    grp_out = pl.BlockSpec((bsz, CHUNK, D_STATE), lambda g, j: (0, nc - 1 - j, g))
    grid = (N_GROUPS, nc)
    ride = _Ride(rider, body, 15, 7, 1, grid)
    outs = pl.pallas_call(
        ride.body, name=name, grid=grid,
        in_specs=[sp["xs"], sp["bm"], sp["cm"], sp["lane_blk"], sp["lane_blk"], sp["lane_blk"], sp["tr"], sp["xs"],
                  sp["xs"], sp["state"], sp["xs"], sp["grp_const"], sp["grp_const"], sp["grp_vec"], sp["grp_vec"]]
        + ride.in_specs,
        out_specs=[sp["xs"], sp["xs"], grp_out, grp_out, sp["lane_blk"],
                   pl.BlockSpec((bsz, 1, GW), lambda g, j: (0, 0, g)),
                   pl.BlockSpec((bsz, 1, 8, LANES), lambda g, j: (0, g, 0, 0))] + ride.out_specs,
        out_shape=[jax.ShapeDtypeStruct((bsz, t, D_SSM), BF16), jax.ShapeDtypeStruct((bsz, t, D_SSM), BF16),
                   jax.ShapeDtypeStruct((bsz, t, N_GROUPS * D_STATE), BF16),
                   jax.ShapeDtypeStruct((bsz, t, N_GROUPS * D_STATE), BF16),
                   jax.ShapeDtypeStruct((bsz, t, D_DT), BF16), jax.ShapeDtypeStruct((bsz, 1, D_SSM), F32),
                   jax.ShapeDtypeStruct((bsz, N_GROUPS, 8, LANES), F32)] + ride.out_shape,
        scratch_shapes=[pltpu.VMEM((bsz, D_STATE, GW), F32)] + ride.scratch,
        compiler_params=_params(*ride.semantics(("parallel", "arbitrary"))),
    )(xc, xc, xc, dtr, dt, acs, tr, z, ypre, sprev, dyn, dtb, alog, dskip, normw, *ride.args)
    return outs[:7], outs[7:]


def _input_grad(dhn, h0, w, dres, seq, *, name):
    bsz, t, d = h0.shape
    nc = t // CHUNK

    def body(dy_ref, h_ref, w_ref, dres_ref, gx_ref, head_ref, dw_ref):
        j = pl.program_id(1)

        @pl.when((pl.program_id(0) == 0) & (j == 0))
        def _():
            dw_ref[...] = jnp.zeros_like(dw_ref)

        x, dyv = h_ref[0], dy_ref[0]
        r = lax.rsqrt(jnp.mean(x * x, axis=-1, keepdims=True) + EPS)
        g = dyv * w_ref[...]
        dx = r * (g - x * (r * r) * jnp.mean(g * x, axis=-1, keepdims=True)) + dres_ref[0]
        dw_ref[...] += jnp.sum(dyv * x * r, axis=0, keepdims=True)

        @pl.when(j == 0)
        def _():
            head_ref[0] = dx

        gx_ref[0] = dx

    row = pl.BlockSpec((1, CHUNK, d), lambda b, j: (b, j, 0))
    return pl.pallas_call(
        body, name=name, grid=(bsz, nc),
        in_specs=[row, row, pl.BlockSpec((1, d), lambda b, j: (0, 0)), row],
        out_specs=[pl.BlockSpec((1, CHUNK, d), lambda b, j: (b, jnp.maximum(j - 1, 0), 0)),
                   pl.BlockSpec((1, CHUNK, d), lambda b, j: (b, 0, 0)), pl.BlockSpec((1, d), lambda b, j: (0, 0))],
        out_shape=[jax.ShapeDtypeStruct((bsz, seq, d), F32), jax.ShapeDtypeStruct((bsz, CHUNK, d), F32),
                   jax.ShapeDtypeStruct((1, d), F32)],
        compiler_params=_params("arbitrary", "arbitrary"),
    )(dhn, h0, w, dres)


def _remote(src, dst, send_sem, recv_sem, dev):
    return pltpu.make_async_remote_copy(src_ref=src, dst_ref=dst, send_sem=send_sem, recv_sem=recv_sem,
                                        device_id=dev, device_id_type=MESH)


def _position():
    return lax.axis_index("x"), lax.axis_index("y"), lax.axis_index("c")


def _other_chips(pos):
    x, y, _ = pos
    return [(1 - x, y), (x, 1 - y), (1 - x, 1 - y)]


class _Gather:
    def __init__(self, arrs):
        n = len(arrs)
        self.args, self.n_in, self.n_out = list(arrs), n, n
        self.split = [a.ndim == 2 and a.shape[1] % (2 * LANES) == 0 for a in arrs]
        self.out_shape = [jax.ShapeDtypeStruct((4,) + a.shape, a.dtype) for a in arrs]
        self.scratch = [pltpu.SemaphoreType.DMA((3 * n,)), pltpu.SemaphoreType.DMA((3 * n,)),
                        pltpu.SemaphoreType.DMA((n,)), pltpu.SemaphoreType.DMA((3 * n,)),
                        pltpu.SemaphoreType.DMA((3 * n,))]

    def _copies(self, pos, ins, outs, sems):
        send_sems, recv_sems, loc_sems, pass_send_sems, pass_recv_sems = sems
        x, y, c = pos
        me, sibling = 2 * x + y, (x, y, 1 - c)
        local = [pltpu.make_async_copy(ins[i], outs[i].at[me], loc_sems.at[i]) for i in range(self.n_in)]
        sends, recvs, passes, pass_recvs = [], [], [], []
        for i in range(self.n_in):
            half = self.args[i].shape[1] // 2 if self.split[i] else None
            for k, (px, py) in enumerate(_other_chips(pos)):
                them = 2 * px + py
                sems_k = (send_sems.at[3 * i + k], recv_sems.at[3 * i + k], (px, py, c))
                if half is None:
                    sends.append(_remote(ins[i], outs[i].at[me], *sems_k))
                    recvs.append(_remote(ins[i], outs[i].at[them], *sems_k))
                    passes.append(None)
                    continue
                mine = pl.ds(pl.multiple_of(c * half, LANES), half)
                other = pl.ds(pl.multiple_of((1 - c) * half, LANES), half)
                sends.append(_remote(ins[i].at[:, mine], outs[i].at[me, :, mine], *sems_k))
                recvs.append(_remote(ins[i].at[:, mine], outs[i].at[them, :, mine], *sems_k))
                pass_k = (pass_send_sems.at[3 * i + k], pass_recv_sems.at[3 * i + k], sibling)
                passes.append(_remote(outs[i].at[them, :, mine], outs[i].at[them, :, mine], *pass_k))
                pass_recvs.append(_remote(outs[i].at[them, :, other], outs[i].at[them, :, other], *pass_k))
        return local, sends, recvs, passes, pass_recvs

    def start(self, pos, ins, outs, sems):
        local, sends = self._copies(pos, ins, outs, sems)[:2]
        for cp in local + sends:
            cp.start()

    def finish(self, pos, ins, outs, sems):
        local, sends, recvs, passes, pass_recvs = self._copies(pos, ins, outs, sems)
        for cp, onward in zip(recvs, passes):
            cp.wait_recv()
            if onward is not None:
                onward.start()
        for cp in pass_recvs:
            cp.wait_recv()
        for cp in sends + [p for p in passes if p is not None]:
            cp.wait_send()
        for cp in local:
            cp.wait()


class _Exchange:
    FLIPS = [(fx, fy, fc) for fx in (0, 1) for fy in (0, 1) for fc in (0, 1)][1:]

    def __init__(self, big, small=None):
        n = len(big)
        self.n_big, self.has_small = n, small is not None
        self.args = list(big) + ([small] if self.has_small else [])
        self.n_in = self.n_out = len(self.args)
        self.out_shape = [jax.ShapeDtypeStruct(a.shape, a.dtype) for a in big]
        self.scratch = [pltpu.SemaphoreType.DMA((max(3 * n, 1),)), pltpu.SemaphoreType.DMA((max(3 * n, 1),))]
        if self.has_small:
            self.out_shape.append(jax.ShapeDtypeStruct((8,) + small.shape, small.dtype))
            self.scratch += [pltpu.SemaphoreType.DMA((7,)), pltpu.SemaphoreType.DMA((7,)), pltpu.SemaphoreType.DMA((1,))]

    def _copies(self, pos, ins, outs, sems):
        x, y, c = pos
        me, me8 = 2 * x + y, 4 * x + 2 * y + c
        local, sends, recvs = [], [], []
        for i in range(self.n_big):
            for k, (px, py) in enumerate(_other_chips(pos)):
                sems_k = (sems[0].at[3 * i + k], sems[1].at[3 * i + k], (px, py, c))
                sends.append(_remote(ins[i].at[2 * px + py], outs[i].at[me], *sems_k))
                recvs.append(_remote(ins[i].at[me], outs[i].at[2 * px + py], *sems_k))
        if self.has_small:
            small, landed = ins[self.n_big], outs[self.n_big]
            local.append(pltpu.make_async_copy(small, landed.at[me8], sems[4].at[0]))
            for k, (fx, fy, fc) in enumerate(self.FLIPS):
                peer = (x ^ fx, y ^ fy, c ^ fc)
                sems_k = (sems[2].at[k], sems[3].at[k], peer)
                sends.append(_remote(small, landed.at[me8], *sems_k))
                recvs.append(_remote(small, landed.at[4 * peer[0] + 2 * peer[1] + peer[2]], *sems_k))
        return local, sends, recvs, [None] * len(recvs), []

    start = _Gather.start
    finish = _Gather.finish


class _Swap:
    def __init__(self, arrs):
        n = len(arrs)
        self.args, self.n_in, self.n_out = list(arrs), n, n
        self.out_shape = [jax.ShapeDtypeStruct(a.shape, a.dtype) for a in arrs]
        self.scratch = [pltpu.SemaphoreType.DMA((n,)), pltpu.SemaphoreType.DMA((n,))]

    def _copies(self, pos, ins, outs, sems):
        x, y, c = pos
        both = [_remote(ins[i], outs[i], sems[0].at[i], sems[1].at[i], (x, y, 1 - c)) for i in range(self.n_in)]
        return [], both, both, [None] * len(both), []

    start = _Gather.start
    finish = _Gather.finish


def _comm(rider, *, name):
    a, b = rider.n_in, rider.n_in + rider.n_out

    def body(*refs):
        pos = _position()
        rider.start(pos, refs[:a], refs[a:b], refs[b:])
        rider.finish(pos, refs[:a], refs[a:b], refs[b:])

    return pl.pallas_call(body, name=name, in_specs=[ANY] * rider.n_in, out_specs=[ANY] * rider.n_out,
                          out_shape=rider.out_shape, scratch_shapes=rider.scratch)(*rider.args)


class _Ride:
    def __init__(self, rider, body, n_in, n_out, n_scratch, grid):
        self.rider = rider
        self.args = rider.args if rider else []
        self.in_specs = [ANY] * rider.n_in if rider else []
        self.out_specs = [ANY] * rider.n_out if rider else []
        self.out_shape = rider.out_shape if rider else []
        self.scratch = rider.scratch if rider else []
        self.body = self._wrap(body, n_in, n_out, n_scratch, grid) if rider else body

    def semantics(self, sem):
        return ("arbitrary",) * len(sem) if self.rider else sem

    def _wrap(self, body, n_in, n_out, n_scratch, grid):
        rider = self.rider
        a = n_in
        b = a + rider.n_in
        c = b + n_out
        d = c + rider.n_out
        e = d + n_scratch

        def wrapped(*refs):
            pos = _position()
            ids = [pl.program_id(i) for i in range(len(grid))]
            first = functools.reduce(jnp.logical_and, [i == 0 for i in ids])
            last = functools.reduce(jnp.logical_and, [i == g - 1 for i, g in zip(ids, grid)])

            @pl.when(first)
            def _():
                rider.start(pos, refs[a:b], refs[c:d], refs[e:])

            body(*refs[:a], *refs[b:c], *refs[d:e])

            @pl.when(last)
            def _():
                rider.finish(pos, refs[a:b], refs[c:d], refs[e:])

        return wrapped


def _elementwise_tiles(r, c):
    if r % 8 == 0 and r * c > 65536:
        tm = _pick(r, (256, 128, 64, 16, 8))
        return (tm, c), r // tm, lambda i: (i, 0)
    if r % 8 and c % 256 == 0 and r * c > 65536:
        return (r, 256), c // 256, lambda i: (0, i)
    return (r, c), 1, lambda i: (0, 0)


def _chip_sum(own, landed, *, name):
    r, c = own.shape
    blk, steps, at = _elementwise_tiles(r, c)

    def body(own_ref, land_ref, o_ref):
        me = 2 * lax.axis_index("x") + lax.axis_index("y")
        acc = None
        for jchip in range(4):
            term = jnp.where(me == jchip, own_ref[...], land_ref[jchip].astype(F32))
            acc = term if acc is None else acc + term
        o_ref[...] = acc

    return pl.pallas_call(
        body, name=name, grid=(steps,),
        in_specs=[pl.BlockSpec(blk, at), pl.BlockSpec((4,) + blk, lambda i: (0,) + at(i))],
        out_specs=pl.BlockSpec(blk, at), out_shape=jax.ShapeDtypeStruct((r, c), F32),
        compiler_params=_params("parallel"),
    )(own, landed)


def _device_sum(parts, *, name):
    _, r, c = parts.shape

    def body(p_ref, o_ref):
        acc = p_ref[0]
        for d in range(1, 8):
            acc = acc + p_ref[d]
        o_ref[...] = acc

    return pl.pallas_call(body, name=name, out_shape=jax.ShapeDtypeStruct((r, c), F32))(parts)


def _adamw_math(w, g, m, v):
    m = ADAM_B1 * m + (1.0 - ADAM_B1) * g
    v = ADAM_B2 * v + (1.0 - ADAM_B2) * (g * g)
    m_hat = m / (1.0 - ADAM_B1 ** ADAM_STEP)
    v_hat = v / (1.0 - ADAM_B2 ** ADAM_STEP)
    return -ADAM_LR * (m_hat / (jnp.sqrt(v_hat) + ADAM_EPS) + ADAM_WD * w), m, v


def _adamw(w, g_parts, m, v, *, name):
    r, c = w.shape
    shape, steps, at = _elementwise_tiles(r, c)
    n_g = len(g_parts)

    def body(*refs):
        w_ref, m_ref, v_ref = refs[n_g:n_g + 3]
        g_ref, d_ref, nm_ref, nv_ref = refs[n_g + 3:]
        g = refs[0][...]
        for p in refs[1:n_g]:
            g = g + p[...]
        g_ref[...] = g
        d_ref[...], nm_ref[...], nv_ref[...] = _adamw_math(w_ref[...], g, m_ref[...], v_ref[...])

    blk = pl.BlockSpec(shape, at)
    return pl.pallas_call(
        body, name=name, grid=(steps,), in_specs=[blk] * (n_g + 3), out_specs=[blk] * 4,
        out_shape=[jax.ShapeDtypeStruct((r, c), F32)] * 4, compiler_params=_params("parallel"),
    )(*g_parts, w, m, v)


def _pad_heads(v):
    return jnp.pad(v.reshape(N_GROUPS, 1, HPG), ((0, 0), (0, 0), (0, LANES - HPG)))


def _unpad_heads(v):
    return v[:, :HPG].reshape(1, N_HEADS)


_SMALL_EARLY = [("pool_w", (512, 128)), ("pool_scale", (1, 512)), ("conv_w", (4, D_XBC)), ("conv_b", (1, D_XBC)),
                ("dt_bias", (1, N_HEADS)), ("a_log", (1, N_HEADS)), ("d_skip", (1, N_HEADS)), ("ssm_norm_w", (1, D_SSM)),
                ("norm_ffn_w", (1, 1024)), ("norm_f_w", (1, 1024))]
_SMALL_LATE = [("norm_mix_w", (1, 1024)), ("meta", (N_META, 1024))]


def _pack_small(grads, layout):
    rows = []
    for nm, shape in layout:
        flat = grads[nm].reshape(-1)
        rows.append(jnp.pad(flat, (0, (-flat.size) % LANES)).reshape(-1, LANES))
    packed = jnp.concatenate(rows, axis=0)
    return jnp.pad(packed, ((0, (-packed.shape[0]) % 8), (0, 0)))


def _unpack_small(packed, layout):
    out, r0 = {}, 0
    for nm, shape in layout:
        size = shape[0] * shape[1]
        nrow = -(-size // LANES)
        out[nm] = packed[r0:r0 + nrow].reshape(-1)[:size].reshape(shape)
        r0 += nrow
    return out


def kernel(x, meta, norm_mix_w, w_in, pool_w, pool_scale, conv_w, conv_b, dt_bias, a_log, d_skip, ssm_norm_w, w_out, norm_ffn_w, w_ff1, w_ff2, norm_f_w, loss_target, m_meta, m_norm_mix_w, m_w_in, m_pool_w, m_pool_scale, m_conv_w, m_conv_b, m_dt_bias, m_a_log, m_d_skip, m_ssm_norm_w, m_w_out, m_norm_ffn_w, m_w_ff1, m_w_ff2, m_norm_f_w, v_meta, v_norm_mix_w, v_w_in, v_pool_w, v_pool_scale, v_conv_w, v_conv_b, v_dt_bias, v_a_log, v_d_skip, v_ssm_norm_w, v_w_out, v_norm_ffn_w, v_w_ff1, v_w_ff2, v_norm_f_w):
    bsz, seq, d = x.shape
    t = seq + CHUNK
    n = bsz * t
    chip = 2 * lax.axis_index("x") + lax.axis_index("y")
    d_in = w_in.shape[2] * 4

    g_in, g_conv, g_meta = _comm(_Gather([w_in[0].T.astype(BF16), conv_w[0], meta]), name="gather_in")
    late_weights = _Gather([w_out[0].astype(BF16), w_ff1[0].astype(BF16), w_ff2[0].astype(BF16)])
    win = g_in.reshape(d_in, d)
    wu, wz = win[:D_POOL], win[D_POOL:D_POOL + D_SSM]
    wx = win[D_POOL + D_SSM:D_POOL + D_SSM + D_XBC]
    wdt = jnp.pad(win[D_POOL + D_SSM + D_XBC:].reshape(N_GROUPS, HPG, d),
                  ((0, 0), (0, LANES - HPG), (0, 0))).reshape(D_DT, d)
    convw = g_conv.transpose(1, 0, 2).reshape(CONV_W, D_XBC)
    meta_full = g_meta.transpose(1, 0, 2).reshape(N_META, d)
    dtb, alog = _pad_heads(dt_bias), _pad_heads(a_log)
    dskip = jnp.repeat(d_skip, HEAD_DIM, axis=1)
    poolw = pool_w[0]

    h0 = jnp.concatenate([jnp.zeros((bsz, PAD, d), F32), jnp.broadcast_to(meta_full[None], (bsz, N_META, d)), x], axis=1)
    h0f = h0.reshape(n, d)
    hn1 = _rms_fwd(h0f, norm_mix_w, name="norm_mix")
    u = _mm(hn1, wu, name="proj_u", nt=True)
    z = _mm(hn1, wz, name="proj_z", nt=True)
    xbc = _mm(hn1, wx, name="proj_xbc", nt=True)
    dtr = _mm(hn1, wdt, name="proj_dt", nt=True)
    ypool = _pool_fwd(u.reshape(bsz, t, D_POOL), poolw, pool_scale, name="pool_fwd")
    xbc3 = xbc.reshape(bsz, t, D_XBC)
    xc = _conv_fwd(xbc3, convw, conv_b, name="conv_fwd")
    z3, dtr3 = z.reshape(bsz, t, D_SSM), dtr.reshape(bsz, t, D_DT)
    dt3, acs3, tr3 = _ssd_prep(dtr3, dtb, alog, name="ssd_prep")
    (yn, ypre, sprev), (g_out, g_ff1, g_ff2) = _ssd_fwd(xc, dt3, acs3, tr3, z3, dskip, ssm_norm_w, name="ssd_fwd",
                                                        rider=late_weights)
    wo = g_out.reshape(D_POOL + D_SSM, d)
    wo_p, wo_s = wo[:D_POOL], wo[D_POOL:]
    w1 = g_ff1.transpose(1, 0, 2).reshape(d, D_FF)
    w2 = g_ff2.reshape(D_FF, d)
    ypool_f, yn_f = ypool.reshape(n, D_POOL), yn.reshape(n, D_SSM)
    add = lambda r, e: r + e
    h1 = _mm([ypool_f, yn_f], [wo_p, wo_s], name="out_proj", post=add, extras=(h0f,))
    hn2 = _rms_fwd(h1, norm_ffn_w, name="norm_ffn")
    act = _mm(hn2, w1, name="ff1", out_dtype=BF16)
    relu2 = lambda a: jnp.square(jnp.maximum(a, 0))
    h2 = _mm(act, w2, name="ff2", pre=relu2, post=add, extras=(h1,))
    dh2, dh2b, loss_acc, d_norm_f = _final_norm_loss(h2.reshape(bsz, t, d), loss_target, norm_f_w.reshape(1, d),
                                                     name="loss")
    loss = lax.psum(loss_acc[0, 0], ("x", "y", "c"))

    dh2f, dh2bf = dh2.reshape(n, d), dh2b.reshape(n, d)
    dact = _mm(dh2bf, w2, name="ff2_bwd", nt=True, post=lambda r, a: r * (2.0 * jnp.maximum(a, 0).astype(F32)),
               extras=(act,), out_dtype=BF16)
    d_w2 = _mm_tn(act, dh2bf, name="ff2_dw", tk=2048, tn=1024, pre=relu2)
    d_w1 = _mm_tn(hn2, dact, name="ff1_dw", tk=1024, tn=2048)
    dhn2 = _mm(dact, w1, name="ff1_bwd", nt=True)
    dh1, dh1b, d_norm_ffn = _rms_bwd(dhn2, h1, norm_ffn_w, dh2f, name="norm_ffn_bwd")
    dypool = _mm(dh1b, wo_p, name="out_pool_bwd", nt=True)
    dyn = _mm(dh1b, wo_s, name="out_ssm_bwd", nt=True)
    d_wo_p = _mm_tn(ypool_f, dh1b, name="out_pool_dw", tk=512, tn=1024)
    d_wo_s = _mm_tn(yn_f, dh1b, name="out_ssm_dw", tk=1536, tn=1024)
    big_late = [jnp.concatenate([d_wo_p, d_wo_s], axis=0).reshape(4, (D_POOL + D_SSM) // 4, d),
                d_w1.reshape(d, 4, D_FF // 4).transpose(1, 0, 2), d_w2.reshape(4, D_FF // 4, d)]
    (dz, dxs, dbm, dcm, ddtr, d_nw, d_heads), landed_late = _ssd_bwd(
        xc, dtr3, dt3, acs3, tr3, z3, ypre, sprev, dyn.reshape(bsz, t, D_SSM), dtb, alog, dskip, ssm_norm_w, name="ssd_bwd",
        rider=_Exchange([b.astype(BF16) for b in big_late]))
    dpre, d_convwb = _conv_bwd_pre(xbc3, dxs, dbm, dcm, convw, conv_b, name="conv_bwd_pre")
    dxbc = _conv_bwd_in(dpre, convw, name="conv_bwd_in")
    du, d_poolw, d_poolsc = _pool_bwd(u.reshape(bsz, t, D_POOL), dypool.reshape(bsz, t, D_POOL), poolw, pool_scale,
                                      name="pool_bwd")
    duf, dzf, dxbcf, ddtrf = du.reshape(n, D_POOL), dz.reshape(n, D_SSM), dxbc.reshape(n, D_XBC), ddtr.reshape(n, D_DT)
    d_wu = _mm_tn(duf, hn1, name="proj_u_dw", tk=512, tn=1024)
    d_wz = _mm_tn(dzf, hn1, name="proj_z_dw", tk=1536, tn=1024)
    d_wx = _mm_tn(dxbcf, hn1, name="proj_xbc_dw", tk=1280, tn=1024)
    d_wdt = _mm_tn(ddtrf, hn1, name="proj_dt_dw", tk=512, tn=1024)
    d_win = jnp.concatenate([d_wu, d_wz, d_wx, d_wdt.reshape(N_GROUPS, LANES, d)[:, :HPG].reshape(N_HEADS, d)], axis=0)
    big_in = d_win.reshape(4, d_in // 4, d)
    heads = jnp.sum(d_heads, axis=0)
    small_early = _pack_small({
        "pool_w": d_poolw, "pool_scale": d_poolsc,
        "conv_w": jnp.sum(d_convwb[:, :CONV_W], axis=0), "conv_b": jnp.sum(d_convwb[:, CONV_W:CONV_W + 1], axis=0),
        "dt_bias": _unpad_heads(heads[:, 2]), "a_log": _unpad_heads(heads[:, 1]), "d_skip": _unpad_heads(heads[:, 0]),
        "ssm_norm_w": jnp.sum(d_nw, axis=0), "norm_ffn_w": d_norm_ffn, "norm_f_w": d_norm_f}, _SMALL_EARLY)
    dhn1, (landed_in, early_all) = _mm([duf, dzf, dxbcf, ddtrf], [wu, wz, wx, wdt], name="proj_bwd",
                                       rider=_Exchange([big_in.astype(BF16)], small_early))
    grad_x, d_head_rows, d_norm_mix = _input_grad(
        dhn1.reshape(bsz, t, d), h0, norm_mix_w, dh1.reshape(bsz, t, d), seq, name="input_grad")

    big = [big_in] + big_late
    landed = [landed_in] + list(landed_late)
    small_late = _pack_small({"norm_mix_w": d_norm_mix, "meta": jnp.sum(d_head_rows[:, PAD:], axis=0)}, _SMALL_LATE)
    (late_all,) = _comm(_Exchange([], small_late), name="exchange_small")
    own = [lax.dynamic_index_in_dim(b, chip, 0, keepdims=False) for b in big]
    mine = [_chip_sum(o, l, name=f"chip_sum_{i}") for i, (o, l) in enumerate(zip(own, landed))]
    theirs = _comm(_Swap(mine), name="swap_cores")
    gsmall = {**_unpack_small(_device_sum(early_all, name="device_sum_early"), _SMALL_EARLY),
              **_unpack_small(_device_sum(late_all, name="device_sum_late"), _SMALL_LATE)}
    gsmall["conv_w"] = lax.dynamic_slice_in_dim(gsmall["conv_w"], chip * (D_XBC // 4), D_XBC // 4, axis=1)
    gsmall["meta"] = lax.dynamic_slice_in_dim(gsmall["meta"], chip * (d // 4), d // 4, axis=1)

    given = dict(meta=(meta, m_meta, v_meta), norm_mix_w=(norm_mix_w, m_norm_mix_w, v_norm_mix_w),
                 w_in=(w_in, m_w_in, v_w_in), pool_w=(pool_w, m_pool_w, v_pool_w),
                 pool_scale=(pool_scale, m_pool_scale, v_pool_scale), conv_w=(conv_w, m_conv_w, v_conv_w),
                 conv_b=(conv_b, m_conv_b, v_conv_b), dt_bias=(dt_bias, m_dt_bias, v_dt_bias),
                 a_log=(a_log, m_a_log, v_a_log), d_skip=(d_skip, m_d_skip, v_d_skip),
                 ssm_norm_w=(ssm_norm_w, m_ssm_norm_w, v_ssm_norm_w), w_out=(w_out, m_w_out, v_w_out),
                 norm_ffn_w=(norm_ffn_w, m_norm_ffn_w, v_norm_ffn_w), w_ff1=(w_ff1, m_w_ff1, v_w_ff1),
                 w_ff2=(w_ff2, m_w_ff2, v_w_ff2), norm_f_w=(norm_f_w, m_norm_f_w, v_norm_f_w))
    big_names = ["w_in", "w_out", "w_ff1", "w_ff2"]
    results = {}
    for nm, (w, m, v) in given.items():
        if nm in big_names:
            i = big_names.index(nm)
            parts, shape2 = (mine[i], theirs[i]), mine[i].shape
        else:
            parts, shape2 = (gsmall[nm],), gsmall[nm].shape
        if nm == "w_in":
            outs = _adamw(w[0].T, parts, m[0].T, v[0].T, name=f"adamw_{nm}")
            results[nm] = [o.T[None] for o in outs]
        else:
            outs = _adamw(w.reshape(shape2), parts, m.reshape(shape2), v.reshape(shape2), name=f"adamw_{nm}")
            results[nm] = [o.reshape(w.shape) for o in outs]
    order = list(given)
    return (loss, grad_x, *[results[nm][0] for nm in order], *[results[nm][1] for nm in order],
            *[results[nm][2] for nm in order], *[results[nm][3] for nm in order])
```

```python
import functools

import jax
import jax.numpy as jnp
from jax import lax
from jax.experimental import pallas as pl
from jax.experimental.pallas import tpu as pltpu

F32 = jnp.float32
BF16 = jnp.bfloat16
MESH = pl.DeviceIdType.MESH
ANY = pl.BlockSpec(memory_space=pl.ANY)

D_MODEL = 1024
N_META = 16
CHUNK = 128
PAD = CHUNK - N_META
POOL_WINDOWS = (2, 4, 8, 16)
D_POOL = 512
POOL_GROUP = 128
D_SSM = 1536
N_HEADS = 24
N_GROUPS = 4
HPG = 6
HEAD_DIM = 64
D_STATE = 128
GW = HPG * HEAD_DIM
D_XBC = D_SSM + 2 * N_GROUPS * D_STATE
D_DT = N_GROUPS * 128
D_FF = 4096
CONV_W = 4
EPS = 1e-5
LANES = 128
VMEM_LIMIT = 56 * 1024 * 1024

ADAM_LR, ADAM_B1, ADAM_B2, ADAM_EPS, ADAM_WD, ADAM_STEP = 0.001, 0.9, 0.999, 1e-08, 0.01, 10


def _params(*sem):
    return pltpu.CompilerParams(dimension_semantics=sem, vmem_limit_bytes=VMEM_LIMIT)


def _pick(n, cands):
    for c in cands:
        if n % c == 0:
            return c
    raise ValueError(f"no block size for {n}")


def _dot(a, b):
    return jnp.dot(a.astype(BF16), b.astype(BF16), preferred_element_type=F32)


def _dot_nt(a, b):
    return lax.dot_general(a.astype(BF16), b.astype(BF16), (((1,), (1,)), ((), ())), preferred_element_type=F32)


def _dot_tn(a, b):
    return lax.dot_general(a.astype(BF16), b.astype(BF16), (((0,), (0,)), ((), ())), preferred_element_type=F32)


def _dot_exact(mask, x):
    m = mask.astype(BF16)
    hi = x.astype(BF16)
    r1 = x - hi.astype(F32)
    mid = r1.astype(BF16)
    lo = (r1 - mid.astype(F32)).astype(BF16)
    dot = lambda t: jnp.dot(m, t, preferred_element_type=F32)
    return dot(hi) + dot(mid) + dot(lo)


def _sigmoid(x):
    return 1.0 / (1.0 + jnp.exp(-x))


def _softplus(x):
    return jnp.maximum(x, 0.0) + jnp.log1p(jnp.exp(-jnp.abs(x)))


def _sum_all(x):
    return jnp.sum(jnp.sum(x, axis=1, keepdims=True), axis=0, keepdims=True)


ROW_TILES = (4224, 2816, 2112, 1408, 1056, 768, 704, 512, 384, 256, 128)
TILE_BUDGET = 28 * 1024 * 1024


def _row_tile(n, bytes_per_row, fixed_bytes):
    for tm in ROW_TILES:
        if n % tm == 0 and 2 * (tm * bytes_per_row + fixed_bytes) <= TILE_BUDGET:
            return tm
    raise ValueError(f"no row tile for {n}")


def _mm(a, w, *, name, tn=512, nt=False, pre=None, post=None, extras=(), out_dtype=F32, rider=None):
    a_list = list(a) if isinstance(a, (list, tuple)) else [a]
    w_list = list(w) if isinstance(w, (list, tuple)) else [w]
    n_a, n_ex = len(a_list), len(extras)
    n = a_list[0].shape[0]
    m = w_list[0].shape[0] if nt else w_list[0].shape[1]
    tn = min(tn, m)
    size = lambda dt: jnp.dtype(dt).itemsize
    per_row = (sum(x.shape[1] * size(x.dtype) for x in a_list) + tn * size(out_dtype)
               + sum(tn * size(e.dtype) for e in extras))
    tm = _row_tile(n, per_row, sum(x.shape[1 if nt else 0] * tn * size(x.dtype) for x in w_list))

    def body(*refs):
        a_refs, w_refs, ex_refs, o_ref = refs[:n_a], refs[n_a:2 * n_a], refs[2 * n_a:2 * n_a + n_ex], refs[2 * n_a + n_ex]
        r = None
        for a_ref, w_ref in zip(a_refs, w_refs):
            av = a_ref[...]
            if pre is not None:
                av = pre(av)
            term = _dot_nt(av, w_ref[...]) if nt else _dot(av, w_ref[...])
            r = term if r is None else r + term
        if post is not None:
            r = post(r, *[e[...] for e in ex_refs])
        o_ref[...] = r.astype(out_dtype)

    a_specs = [pl.BlockSpec((tm, x.shape[1]), lambda i, j: (i, 0)) for x in a_list]
    w_specs = [pl.BlockSpec((tn, x.shape[1]), lambda i, j: (j, 0)) if nt else pl.BlockSpec((x.shape[0], tn), lambda i, j: (0, j))
               for x in w_list]
    blk = pl.BlockSpec((tm, tn), lambda i, j: (i, j))
    grid = (n // tm, m // tn)
    ride = _Ride(rider, body, 2 * n_a + n_ex, 1, 0, grid)
    outs = pl.pallas_call(
        ride.body, name=name, grid=grid,
        in_specs=a_specs + w_specs + [blk] * n_ex + ride.in_specs,
        out_specs=[blk] + ride.out_specs, out_shape=[jax.ShapeDtypeStruct((n, m), out_dtype)] + ride.out_shape,
        scratch_shapes=ride.scratch, compiler_params=_params(*ride.semantics(("parallel", "parallel"))),
    )(*a_list, *w_list, *extras, *ride.args)
    return (outs[0], outs[1:]) if rider else outs[0]


def _mm_tn(a, g, *, name, tk, tn, pre=None):
    n, k = a.shape
    m = g.shape[1]
    tk, tn = min(tk, k), min(tn, m)
    tm = _row_tile(n, tk * jnp.dtype(a.dtype).itemsize + tn * jnp.dtype(g.dtype).itemsize, tk * tn * 4)

    def body(a_ref, g_ref, o_ref):
        @pl.when(pl.program_id(2) == 0)
        def _():
            o_ref[...] = jnp.zeros_like(o_ref)

        av = a_ref[...]
        if pre is not None:
            av = pre(av)
        o_ref[...] += _dot_tn(av, g_ref[...])

    return pl.pallas_call(
        body, name=name, grid=(k // tk, m // tn, n // tm),
        in_specs=[pl.BlockSpec((tm, tk), lambda i, j, r: (r, i)), pl.BlockSpec((tm, tn), lambda i, j, r: (r, j))],
        out_specs=pl.BlockSpec((tk, tn), lambda i, j, r: (i, j)),
        out_shape=jax.ShapeDtypeStruct((k, m), F32),
        compiler_params=_params("parallel", "parallel", "arbitrary"),
    )(a, g)


def _mm_rms_bwd(a, w, h, w_norm, dres, *, name, tk=1024):
    n, k = a.shape
    d = h.shape[1]
    tm = _row_tile(n, tk * jnp.dtype(a.dtype).itemsize + d * (4 + 4 + 4 + 2 + 4), d * tk * 2)
    last = k // tk - 1

    def body(a_ref, w_ref, h_ref, wn_ref, dres_ref, dx_ref, dxb_ref, dw_ref, acc_ref):
        i, kk = pl.program_id(0), pl.program_id(1)

        @pl.when((i == 0) & (kk == 0))
        def _():
            dw_ref[...] = jnp.zeros_like(dw_ref)

        part = _dot_nt(a_ref[...], w_ref[...])

        @pl.when(kk == 0)
        def _():
            acc_ref[...] = part

        @pl.when(kk > 0)
        def _():
            acc_ref[...] += part

        @pl.when(kk == last)
        def _():
            x, dyv = h_ref[...], acc_ref[...]
            r = lax.rsqrt(jnp.mean(x * x, axis=-1, keepdims=True) + EPS)
            g = dyv * wn_ref[...]
            dx = r * (g - x * (r * r) * jnp.mean(g * x, axis=-1, keepdims=True)) + dres_ref[...]
            dx_ref[...] = dx
            dxb_ref[...] = dx.astype(BF16)
            dw_ref[...] += jnp.sum(dyv * x * r, axis=0, keepdims=True)

    row = pl.BlockSpec((tm, d), lambda i, kk: (i, 0))
    vec = pl.BlockSpec((1, d), lambda i, kk: (0, 0))
    return pl.pallas_call(
        body, name=name, grid=(n // tm, k // tk),
        in_specs=[pl.BlockSpec((tm, tk), lambda i, kk: (i, kk)), pl.BlockSpec((d, tk), lambda i, kk: (0, kk)), row, vec, row],
        out_specs=[row, row, vec],
        out_shape=[jax.ShapeDtypeStruct((n, d), F32), jax.ShapeDtypeStruct((n, d), BF16), jax.ShapeDtypeStruct((1, d), F32)],
        scratch_shapes=[pltpu.VMEM((tm, d), F32)],
        compiler_params=_params("arbitrary", "arbitrary"),
    )(a, w, h, w_norm, dres)


def _rms_fwd(h, w, *, name):
    n, d = h.shape
    tm = _pick(n, (768, 512, 256, 128))

    def body(h_ref, w_ref, o_ref):
        x = h_ref[...]
        r = lax.rsqrt(jnp.mean(x * x, axis=-1, keepdims=True) + EPS)
        o_ref[...] = (x * r * w_ref[...]).astype(BF16)

    return pl.pallas_call(
        body, name=name, grid=(n // tm,),
        in_specs=[pl.BlockSpec((tm, d), lambda i: (i, 0)), pl.BlockSpec((1, d), lambda i: (0, 0))],
        out_specs=pl.BlockSpec((tm, d), lambda i: (i, 0)), out_shape=jax.ShapeDtypeStruct((n, d), BF16),
        compiler_params=_params("parallel"),
    )(h, w)


def _final_norm_loss(h2, target, w, *, name):
    bsz, t, d = h2.shape
    nc = t // CHUNK

    def body(h_ref, t_ref, w_ref, dh_ref, dhb_ref, loss_ref, dw_ref):
        j = pl.program_id(1)

        @pl.when((pl.program_id(0) == 0) & (j == 0))
        def _():
            loss_ref[...] = jnp.zeros_like(loss_ref)
            dw_ref[...] = jnp.zeros_like(dw_ref)

        x, wv = h_ref[0], w_ref[...]
        r = lax.rsqrt(jnp.mean(x * x, axis=-1, keepdims=True) + EPS)
        diff = jnp.where(j > 0, x * r * wv - t_ref[0], 0.0)
        loss_ref[...] += _sum_all(diff * diff) * (0.5 / d)
        dy = diff * (1.0 / d)
        g = dy * wv
        dh = r * (g - x * (r * r) * jnp.mean(g * x, axis=-1, keepdims=True))
        dh_ref[0] = dh
        dhb_ref[0] = dh.astype(BF16)
        dw_ref[...] += jnp.sum(dy * x * r, axis=0, keepdims=True)

    row = pl.BlockSpec((1, CHUNK, d), lambda b, j: (b, j, 0))
    return pl.pallas_call(
        body, name=name, grid=(bsz, nc),
        in_specs=[row, pl.BlockSpec((1, CHUNK, d), lambda b, j: (b, jnp.maximum(j - 1, 0), 0)),
                  pl.BlockSpec((1, d), lambda b, j: (0, 0))],
        out_specs=[row, row, pl.BlockSpec((8, LANES), lambda b, j: (0, 0)), pl.BlockSpec((1, d), lambda b, j: (0, 0))],
        out_shape=[jax.ShapeDtypeStruct((bsz, t, d), F32), jax.ShapeDtypeStruct((bsz, t, d), BF16),
                   jax.ShapeDtypeStruct((8, LANES), F32), jax.ShapeDtypeStruct((1, d), F32)],
        compiler_params=_params("arbitrary", "arbitrary"),
    )(h2, target, w)


def _pool_masks(j, transposed):
    r = lax.broadcasted_iota(jnp.int32, (CHUNK, 2 * CHUNK), 0)
    c = lax.broadcasted_iota(jnp.int32, (CHUNK, 2 * CHUNK), 1)
    masks = []
    for w in POOL_WINDOWS:
        if transposed:
            m = (c >= r) & (c < r + w)
        else:
            s = c - CHUNK
            m = (s <= r) & (s > r - w) & (s + j * CHUNK >= 0)
        masks.append(m.astype(F32))
    return masks


def _pool_count(t_global, w):
    return jnp.clip(t_global - PAD + 1, 1, w).astype(F32)


def _pool_fwd(u, pool_w, pool_scale, *, name):
    bsz, t, _ = u.shape
    nc = t // CHUNK

    def body(prev_ref, cur_ref, pw_ref, sc_ref, o_ref):
        j = pl.program_id(0)
        masks = _pool_masks(j, False)
        tg = j * CHUNK + lax.broadcasted_iota(jnp.int32, (CHUNK, 1), 0)
        count = [_pool_count(tg, w) for w in POOL_WINDOWS]
        units = [(e, gi) for e in range(bsz) for gi in range(len(POOL_WINDOWS))]
        sl = lambda gi: pl.ds(gi * POOL_GROUP, POOL_GROUP)
        cur = {(e, gi): cur_ref[e, :, sl(gi)] for e, gi in units}
        both = {(e, gi): jnp.concatenate([prev_ref[e, :, sl(gi)], cur[e, gi]], axis=0) for e, gi in units}
        win = {(e, gi): _dot_exact(masks[gi], both[e, gi]) for e, gi in units}
        pooled = {(e, gi): win[e, gi] / count[gi] - cur[e, gi] for e, gi in units}
        mixed = {(e, gi): _dot(pooled[e, gi], pw_ref[gi]) for e, gi in units}
        for e, gi in units:
            o_ref[e, :, sl(gi)] = (mixed[e, gi] * sc_ref[:, sl(gi)]).astype(BF16)

    blk = lambda f: pl.BlockSpec((bsz, CHUNK, D_POOL), f)
    return pl.pallas_call(
        body, name=name, grid=(nc,),
        in_specs=[blk(lambda j: (0, jnp.maximum(j - 1, 0), 0)), blk(lambda j: (0, j, 0)),
                  pl.BlockSpec((4, POOL_GROUP, POOL_GROUP), lambda j: (0, 0, 0)),
                  pl.BlockSpec((1, D_POOL), lambda j: (0, 0))],
        out_specs=blk(lambda j: (0, j, 0)), out_shape=jax.ShapeDtypeStruct(u.shape, BF16),
        compiler_params=_params("parallel"),
    )(u, u, pool_w, pool_scale)


def _pool_bwd(u, dyp, pool_w, pool_scale, *, name):
    bsz, t, _ = u.shape
    nc = t // CHUNK

    def body(prev_ref, cur_ref, dy_ref, dyn_ref, pw_ref, sc_ref, du_ref, dpw_ref, dsc_ref):
        j = pl.program_id(0)

        @pl.when(j == 0)
        def _():
            dpw_ref[...] = jnp.zeros_like(dpw_ref)
            dsc_ref[...] = jnp.zeros_like(dsc_ref)

        fwd = _pool_masks(j, False)
        bwd = _pool_masks(j, True)
        tg = j * CHUNK + lax.broadcasted_iota(jnp.int32, (CHUNK, 1), 0)
        count = [_pool_count(tg, w) for w in POOL_WINDOWS]
        count_next = [_pool_count(tg + CHUNK, w) for w in POOL_WINDOWS]
        has_next = j < nc - 1
        groups = range(len(POOL_WINDOWS))
        units = [(e, gi) for e in range(bsz) for gi in groups]
        sl = lambda gi: pl.ds(gi * POOL_GROUP, POOL_GROUP)
        cur = {(e, gi): cur_ref[e, :, sl(gi)] for e, gi in units}
        both = {(e, gi): jnp.concatenate([prev_ref[e, :, sl(gi)], cur[e, gi]], axis=0) for e, gi in units}
        win = {(e, gi): _dot_exact(fwd[gi], both[e, gi]) for e, gi in units}
        pooled = {(e, gi): win[e, gi] / count[gi] - cur[e, gi] for e, gi in units}
        dy = {(e, gi): dy_ref[e, :, sl(gi)] for e, gi in units}
        mixed = {(e, gi): _dot(pooled[e, gi], pw_ref[gi]) for e, gi in units}
        dm = {(e, gi): dy[e, gi] * sc_ref[:, sl(gi)] for e, gi in units}
        dm_next = {(e, gi): jnp.where(has_next, dyn_ref[e, :, sl(gi)], 0.0) * sc_ref[:, sl(gi)] for e, gi in units}
        dpw = {(e, gi): _dot_tn(pooled[e, gi], dm[e, gi]) for e, gi in units}
        dpooled = {(e, gi): _dot_nt(dm[e, gi], pw_ref[gi]) for e, gi in units}
        dpooled_next = {(e, gi): _dot_nt(dm_next[e, gi], pw_ref[gi]) for e, gi in units}
        spread = {(e, gi): jnp.concatenate([dpooled[e, gi] / count[gi], dpooled_next[e, gi] / count_next[gi]], axis=0)
                  for e, gi in units}
        back = {(e, gi): _dot_exact(bwd[gi], spread[e, gi]) for e, gi in units}
        for e, gi in units:
            du_ref[e, :, sl(gi)] = (back[e, gi] - dpooled[e, gi]).astype(BF16)
        for gi in groups:
            dsc, dw = None, None
            for e in range(bsz):
                term = jnp.sum(dy[e, gi] * mixed[e, gi], axis=0, keepdims=True)
                dsc = term if dsc is None else dsc + term
                dw = dpw[e, gi] if dw is None else dw + dpw[e, gi]
            dsc_ref[:, sl(gi)] += dsc
            dpw_ref[gi] += dw

    blk = lambda f: pl.BlockSpec((bsz, CHUNK, D_POOL), f)
    return pl.pallas_call(
        body, name=name, grid=(nc,),
        in_specs=[blk(lambda j: (0, jnp.maximum(j - 1, 0), 0)), blk(lambda j: (0, j, 0)),
                  blk(lambda j: (0, j, 0)), blk(lambda j: (0, jnp.minimum(j + 1, nc - 1), 0)),
                  pl.BlockSpec((4, POOL_GROUP, POOL_GROUP), lambda j: (0, 0, 0)),
                  pl.BlockSpec((1, D_POOL), lambda j: (0, 0))],
        out_specs=[blk(lambda j: (0, j, 0)), pl.BlockSpec((4, POOL_GROUP, POOL_GROUP), lambda j: (0, 0, 0)),
                   pl.BlockSpec((1, D_POOL), lambda j: (0, 0))],
        out_shape=[jax.ShapeDtypeStruct(u.shape, BF16), jax.ShapeDtypeStruct((4, POOL_GROUP, POOL_GROUP), F32),
                   jax.ShapeDtypeStruct((1, D_POOL), F32)],
        compiler_params=_params("arbitrary"),
    )(u, u, dyp, dyp, pool_w, pool_scale)


CONV_SLAB = 512


def _conv_taps(tail, cur, keep_tail):
    ext = jnp.concatenate([jnp.where(keep_tail, tail, 0.0), cur], axis=0)
    return [(pltpu.roll(ext, CONV_W - 1 - k, 0) if k < CONV_W - 1 else ext)[8:] for k in range(CONV_W)]


def _conv_pre(taps, w_ref, b_ref, sl):
    acc = b_ref[:, sl]
    for k in range(CONV_W):
        acc = acc + w_ref[k:k + 1, sl] * taps[k]
    return acc


def _conv_fwd(xbc, conv_w, conv_b, *, name):
    bsz, t, c = xbc.shape
    nc = t // CHUNK

    def body(tail_ref, cur_ref, w_ref, b_ref, o_ref):
        keep = pl.program_id(1) > 0
        for c0 in range(0, c, CONV_SLAB):
            sl = pl.ds(c0, CONV_SLAB)
            pre = _conv_pre(_conv_taps(tail_ref[0, :, sl], cur_ref[0, :, sl], keep), w_ref, b_ref, sl)
            o_ref[0, :, sl] = (pre * _sigmoid(pre)).astype(BF16)

    return pl.pallas_call(
        body, name=name, grid=(bsz, nc),
        in_specs=[pl.BlockSpec((1, 8, c), lambda b, j: (b, jnp.maximum(j * (CHUNK // 8) - 1, 0), 0)),
                  pl.BlockSpec((1, CHUNK, c), lambda b, j: (b, j, 0)),
                  pl.BlockSpec((CONV_W, c), lambda b, j: (0, 0)), pl.BlockSpec((1, c), lambda b, j: (0, 0))],
        out_specs=pl.BlockSpec((1, CHUNK, c), lambda b, j: (b, j, 0)), out_shape=jax.ShapeDtypeStruct(xbc.shape, BF16),
        compiler_params=_params("parallel", "parallel"),
    )(xbc, xbc, conv_w, conv_b)


def _conv_bwd(xbc, dxs, db, dc, conv_w, conv_b, *, name):
    bsz, t, c = xbc.shape
    nc = t // CHUNK
    halo = 16
    rows = CHUNK + halo

    def body(tail_ref, cur_ref, head_ref, dxs_ref, db_ref, dc_ref, dxs_head, db_head, dc_head, w_ref, b_ref,
             dx_ref, dwb_ref):
        j = pl.program_id(1)

        @pl.when(j == 0)
        def _():
            dwb_ref[...] = jnp.zeros_like(dwb_ref)

        has_prev, has_next = j > 0, j < nc - 1
        for c0 in range(0, c, CONV_SLAB):
            sl = pl.ds(c0, CONV_SLAB)
            if c0 < D_SSM:
                dxc, dxc_next = dxs_ref[0, :, sl], dxs_head[0, :, sl]
            elif c0 < D_SSM + D_POOL:
                dxc, dxc_next = db_ref[0], db_head[0]
            else:
                dxc, dxc_next = dc_ref[0], dc_head[0]
            dxc = jnp.concatenate([dxc.astype(F32), jnp.where(has_next, dxc_next.astype(F32), 0.0)], axis=0)
            ext = jnp.concatenate([jnp.where(has_prev, tail_ref[0, :, sl], 0.0), cur_ref[0, :, sl],
                                   jnp.where(has_next, head_ref[0, :, sl], 0.0)], axis=0)
            taps = [(pltpu.roll(ext, CONV_W - 1 - k, 0) if k < CONV_W - 1 else ext)[8:] for k in range(CONV_W)]
            pre = _conv_pre(taps, w_ref, b_ref, sl)
            s = _sigmoid(pre)
            dpre = dxc * (s * (1.0 + pre * (1.0 - s)))
            acc = w_ref[CONV_W - 1:CONV_W, sl] * dpre[:CHUNK]
            for k in range(CONV_W - 1):
                up = CONV_W - 1 - k
                acc = acc + w_ref[k:k + 1, sl] * pltpu.roll(dpre, rows - up, 0)[:CHUNK]
            dx_ref[0, :, sl] = acc.astype(BF16)
            for k in range(CONV_W):
                dwb_ref[0, k:k + 1, sl] += jnp.sum(dpre[:CHUNK] * taps[k][:CHUNK], axis=0, keepdims=True)
            dwb_ref[0, CONV_W:CONV_W + 1, sl] += jnp.sum(dpre[:CHUNK], axis=0, keepdims=True)

    assert CONV_SLAB == D_POOL and D_SSM % CONV_SLAB == 0
    row = lambda width: pl.BlockSpec((1, CHUNK, width), lambda b, j: (b, j, 0))
    nxt = lambda width: pl.BlockSpec(
        (1, halo, width), lambda b, j: (b, jnp.minimum((j + 1) * (CHUNK // halo), t // halo - 1), 0))
    return pl.pallas_call(
        body, name=name, grid=(bsz, nc),
        in_specs=[pl.BlockSpec((1, 8, c), lambda b, j: (b, jnp.maximum(j * (CHUNK // 8) - 1, 0), 0)), row(c), nxt(c),
                  row(D_SSM), row(D_POOL), row(D_POOL), nxt(D_SSM), nxt(D_POOL), nxt(D_POOL),
                  pl.BlockSpec((CONV_W, c), lambda b, j: (0, 0)), pl.BlockSpec((1, c), lambda b, j: (0, 0))],
        out_specs=[row(c), pl.BlockSpec((1, 8, c), lambda b, j: (b, 0, 0))],
        out_shape=[jax.ShapeDtypeStruct(xbc.shape, BF16), jax.ShapeDtypeStruct((bsz, 8, c), F32)],
        compiler_params=_params("parallel", "arbitrary"),
    )(xbc, xbc, xbc, dxs, db, dc, dxs, db, dc, conv_w, conv_b)


def _dt_valid(j):
    lane = lax.broadcasted_iota(jnp.int32, (CHUNK, LANES), 1)
    row = lax.broadcasted_iota(jnp.int32, (CHUNK, LANES), 0)
    return (lane < HPG) & ((j > 0) | (row >= PAD))


def _ssd_prep(dtr, dtb, alog, *, name):
    bsz, t, _ = dtr.shape
    nc = t // CHUNK

    def body(dtr_ref, dtb_ref, alog_ref, dt_ref, acs_ref, tr_ref):
        j = pl.program_id(0)
        valid = _dt_valid(j)
        row = lax.broadcasted_iota(jnp.int32, (CHUNK, LANES), 0)
        lane = lax.broadcasted_iota(jnp.int32, (CHUNK, LANES), 1)
        tril = (row >= lane).astype(F32)
        units = [(e, g) for e in range(bsz) for g in range(N_GROUPS)]
        sl = lambda g: pl.ds(g * LANES, LANES)
        dt = {(e, g): jnp.where(valid, _softplus(dtr_ref[e, :, sl(g)] + dtb_ref[g]), 0.0) for e, g in units}
        acs = {(e, g): _dot_exact(tril, dt[e, g] * -jnp.exp(alog_ref[g])) for e, g in units}
        for e, g in units:
            dt_ref[e, :, sl(g)] = dt[e, g]
            acs_ref[e, :, sl(g)] = acs[e, g]
            tr_ref[e, 0, g, 0:8, :] = dt[e, g].T[0:8]
            tr_ref[e, 0, g, 8:16, :] = acs[e, g].T[0:8]

    blk = pl.BlockSpec((bsz, CHUNK, D_DT), lambda j: (0, j, 0))
    const = pl.BlockSpec((N_GROUPS, 1, LANES), lambda j: (0, 0, 0))
    return pl.pallas_call(
        body, name=name, grid=(nc,), in_specs=[blk, const, const],
        out_specs=[blk, blk, pl.BlockSpec((bsz, 1, N_GROUPS, 16, LANES), lambda j: (0, j, 0, 0, 0))],
        out_shape=[jax.ShapeDtypeStruct(dtr.shape, F32), jax.ShapeDtypeStruct(dtr.shape, F32),
                   jax.ShapeDtypeStruct((bsz, nc, N_GROUPS, 16, LANES), F32)],
        compiler_params=_params("parallel"),
    )(dtr, dtb, alog)


def _ssd_decay(dt, acs, tr):
    lane = lax.broadcasted_iota(jnp.int32, (CHUNK, LANES), 1)
    row = lax.broadcasted_iota(jnp.int32, (CHUNK, LANES), 0)
    return dict(lane=lane, row=row, dt=dt, causal=row >= lane, acs=acs, acs_t=tr[8:16], dt_t=tr[0:8],
                aend=acs[CHUNK - 1:CHUNK, :])


def _ssd_specs(bsz, nc, rev):
    ch = (lambda j: nc - 1 - j) if rev else (lambda j: j)
    return dict(
        xs=pl.BlockSpec((bsz, CHUNK, GW), lambda g, j: (0, ch(j), g)),
        bm=pl.BlockSpec((bsz, CHUNK, D_STATE), lambda g, j: (0, ch(j), D_SSM // D_STATE + g)),
        cm=pl.BlockSpec((bsz, CHUNK, D_STATE), lambda g, j: (0, ch(j), D_SSM // D_STATE + N_GROUPS + g)),
        lane_blk=pl.BlockSpec((bsz, CHUNK, LANES), lambda g, j: (0, ch(j), g)),
        grp_const=pl.BlockSpec((1, 1, LANES), lambda g, j: (g, 0, 0)),
        grp_vec=pl.BlockSpec((1, GW), lambda g, j: (0, g)),
        state=pl.BlockSpec((bsz, 1, D_STATE, GW), lambda g, j: (0, ch(j), 0, g)),
        tr=pl.BlockSpec((bsz, 1, 1, 16, LANES), lambda g, j: (0, ch(j), g, 0, 0)),
    )


def _ssd_fwd(xc, dt, acs, tr, z, dskip, normw, *, name, rider=None):
    bsz, t, _ = xc.shape
    nc = t // CHUNK
    sp = _ssd_specs(bsz, nc, False)

    def body(xs_ref, b_ref, c_ref, dt_ref, acs_ref, tr_ref, z_ref, dsk_ref, nw_ref, yn_ref, y_ref, sp_ref, s_ref):
        j = pl.program_id(1)

        @pl.when(j == 0)
        def _():
            s_ref[...] = jnp.zeros_like(s_ref)

        ex = range(bsz)
        units = [(e, r) for e in ex for r in range(HPG)]
        full = lambda v: jnp.broadcast_to(v, (CHUNK, LANES))
        pair = lambda r: pl.ds((r // 2) * LANES, LANES)
        q = [_ssd_decay(dt_ref[e], acs_ref[e], tr_ref[e, 0, 0]) for e in ex]
        for e in ex:
            sp_ref[e, 0] = s_ref[e]
        bm, cm = [b_ref[e] for e in ex], [c_ref[e] for e in ex]
        cb = [_dot_nt(cm[e], bm[e]) for e in ex]
        low = q[0]["lane"] < HEAD_DIM
        col = {(e, r): full(q[e]["acs"][:, r:r + 1]) for e, r in units}
        aend = {(e, r): q[e]["aend"][:, r:r + 1] for e, r in units}
        decay = {(e, r): jnp.exp(jnp.where(q[e]["causal"], col[e, r] - q[e]["acs_t"][r:r + 1, :], -jnp.inf))
                 for e, r in units}
        mp = {(e, r): cb[e] * decay[e, r] * q[e]["dt_t"][r:r + 1, :] for e, r in units}
        ce = {(e, r): cm[e] * jnp.exp(col[e, r]) for e, r in units}
        bk = {(e, r): bm[e] * (jnp.exp(aend[e, r] - col[e, r]) * full(q[e]["dt"][:, r:r + 1])) for e, r in units}
        xp = {(e, r): xs_ref[e, :, pair(r)] for e, r in units}
        s_old = {(e, r): s_ref[e, :, pair(r)] for e, r in units}
        y_h = {u: _dot(mp[u], xp[u]) + _dot(ce[u], s_old[u]) for u in units}
        s_h = {u: jnp.exp(aend[u]) * s_old[u] + _dot_tn(bk[u], xp[u]) for u in units}
        for e in ex:
            for r in range(0, HPG, 2):
                y_ref[e, :, pair(r)] = jnp.where(low, y_h[e, r], y_h[e, r + 1])
                s_ref[e, :, pair(r)] = jnp.where(low, s_h[e, r], s_h[e, r + 1])
        y = [y_ref[e] + dsk_ref[...] * xs_ref[e] for e in ex]
        zz = [z_ref[e] for e in ex]
        yg = [y[e] * (zz[e] * _sigmoid(zz[e])) for e in ex]
        rstd = [lax.rsqrt(jnp.mean(yg[e] * yg[e], axis=-1, keepdims=True) + EPS) for e in ex]
        for e in ex:
            y_ref[e] = y[e]
            yn_ref[e] = (yg[e] * rstd[e] * nw_ref[...]).astype(BF16)

    grid = (N_GROUPS, nc)
    ride = _Ride(rider, body, 9, 3, 1, grid)
    outs = pl.pallas_call(
        ride.body, name=name, grid=grid,
        in_specs=[sp["xs"], sp["bm"], sp["cm"], sp["lane_blk"], sp["lane_blk"], sp["tr"], sp["xs"],
                  sp["grp_vec"], sp["grp_vec"]] + ride.in_specs,
        out_specs=[sp["xs"], sp["xs"], sp["state"]] + ride.out_specs,
        out_shape=[jax.ShapeDtypeStruct((bsz, t, D_SSM), BF16), jax.ShapeDtypeStruct((bsz, t, D_SSM), F32),
                   jax.ShapeDtypeStruct((bsz, nc, D_STATE, D_SSM), F32)] + ride.out_shape,
        scratch_shapes=[pltpu.VMEM((bsz, D_STATE, GW), F32)] + ride.scratch,
        compiler_params=_params(*ride.semantics(("parallel", "arbitrary"))),
    )(xc, xc, xc, dt, acs, tr, z, dskip, normw, *ride.args)
    return outs[:3], outs[3:]


def _ssd_bwd(xc, dtr, dt, acs, tr, z, ypre, sprev, dyn, dtb, alog, dskip, normw, *, name, rider=None):
    bsz, t, _ = xc.shape
    nc = t // CHUNK
    sp = _ssd_specs(bsz, nc, True)

    def body(xs_ref, b_ref, c_ref, dtr_ref, dt_ref, acs_ref, tr_ref, z_ref, y_ref, sp_ref, dyn_ref, dtb_ref, alog_ref,
             dsk_ref, nw_ref, dz_ref, dxs_ref, db_ref, dc_ref, ddt_ref, dnw_ref, dsm_ref, ds_ref):
        j = pl.program_id(1)

        @pl.when(j == 0)
        def _():
            ds_ref[...] = jnp.zeros_like(ds_ref)
            dnw_ref[...] = jnp.zeros_like(dnw_ref)
            dsm_ref[...] = jnp.zeros_like(dsm_ref)

        ex = range(bsz)
        heads = range(HPG)
        units = [(e, r) for e in ex for r in heads]
        q = [_ssd_decay(dt_ref[e], acs_ref[e], tr_ref[e, 0, 0]) for e in ex]
        a = -jnp.exp(alog_ref[0])
        valid = _dt_valid(nc - 1 - j)
        lane, row = q[0]["lane"], q[0]["row"]
        lane1 = lane[0:1, :]
        nw = nw_ref[...]
        y, zz, dyn = [y_ref[e] for e in ex], [z_ref[e] for e in ex], [dyn_ref[e] for e in ex]
        sz = [_sigmoid(zz[e]) for e in ex]
        sil = [zz[e] * sz[e] for e in ex]
        yg = [y[e] * sil[e] for e in ex]
        rstd = [lax.rsqrt(jnp.mean(yg[e] * yg[e], axis=-1, keepdims=True) + EPS) for e in ex]
        gn = [dyn[e] * nw for e in ex]
        dyg = [rstd[e] * (gn[e] - yg[e] * (rstd[e] * rstd[e]) * jnp.mean(gn[e] * yg[e], axis=-1, keepdims=True))
               for e in ex]
        dy = [dyg[e] * sil[e] for e in ex]
        xs = [xs_ref[e] for e in ex]
        for e in ex:
            dnw_ref[e] += jnp.sum(dyn[e] * yg[e] * rstd[e], axis=0, keepdims=True)
            dz_ref[e] = (dyg[e] * y[e] * (sz[e] * (1.0 + zz[e] * (1.0 - sz[e])))).astype(BF16)
        dskip_cols = [jnp.sum(dy[e] * xs[e], axis=0, keepdims=True) for e in ex]

        bm, cm = [b_ref[e] for e in ex], [c_ref[e] for e in ex]
        cb = [_dot_nt(cm[e], bm[e]) for e in ex]
        zero = jnp.zeros((CHUNK, LANES), F32)
        full = lambda v: jnp.broadcast_to(v, (CHUNK, LANES))
        low = lane < HEAD_DIM
        half = [low if r % 2 == 0 else ~low for r in heads]
        sl = lambda v, r: v[:, (r // 2) * LANES:(r // 2 + 1) * LANES]
        pair = lambda r: pl.ds((r // 2) * LANES, LANES)
        col = {(e, r): full(q[e]["acs"][:, r:r + 1]) for e, r in units}
        dt_col = {(e, r): full(q[e]["dt"][:, r:r + 1]) for e, r in units}
        aend = {(e, r): q[e]["aend"][:, r:r + 1] for e, r in units}
        dt_row = {(e, r): q[e]["dt_t"][r:r + 1, :] for e, r in units}
        decay = {(e, r): jnp.exp(jnp.where(q[e]["causal"], col[e, r] - q[e]["acs_t"][r:r + 1, :], -jnp.inf))
                 for e, r in units}
        ea = {u: jnp.exp(col[u]) for u in units}
        dte = {u: jnp.exp(aend[u] - col[u]) for u in units}
        ed = {u: jnp.exp(aend[u]) for u in units}
        k = {u: dte[u] * dt_col[u] for u in units}
        mp = {(e, r): cb[e] * decay[e, r] * dt_row[e, r] for e, r in units}
        xp = {(e, r): sl(xs[e], r) for e, r in units}
        dym = {(e, r): jnp.where(half[r], sl(dy[e], r), 0.0) for e, r in units}
        s_old = {(e, r): sp_ref[e, 0, :, pair(r)] for e, r in units}
        ds_old = {(e, r): ds_ref[e, :, pair(r)] for e, r in units}
        dsm = {(e, r): jnp.where(half[r], ds_old[e, r], 0.0) for e, r in units}
        gmat = {u: _dot_nt(dym[u], xp[u]) for u in units}
        t1 = {u: _dot_nt(dym[u], s_old[u]) for u in units}
        dbs = {u: _dot_nt(xp[u], dsm[u]) for u in units}
        dx = {(e, r): _dot_tn(mp[e, r], dym[e, r]) + _dot(bm[e] * k[e, r], dsm[e, r]) for e, r in units}
        ds = {(e, r): _dot_tn(cm[e] * ea[e, r], dym[e, r]) for e, r in units}
        gd = {u: gmat[u] * decay[u] for u in units}
        w0 = {(e, r): gd[e, r] * cb[e] for e, r in units}
        cs0 = {u: jnp.sum(w0[u], axis=0, keepdims=True) for u in units}
        rs = {u: jnp.sum(w0[u] * dt_row[u], axis=1, keepdims=True) for u in units}
        qv = {(e, r): jnp.sum(cm[e] * t1[e, r], axis=1, keepdims=True) for e, r in units}
        dk = {(e, r): jnp.sum(bm[e] * dbs[e, r], axis=1, keepdims=True) for e, r in units}
        ddte = {u: dk[u] * dt_col[u] for u in units}
        d_aend = {u: _sum_all(dsm[u] * s_old[u]) * ed[u] + _sum_all(ddte[u][:, 0:1] * dte[u][:, 0:1]) for u in units}
        last_row = row == CHUNK - 1
        dacs_col = {u: rs[u] + qv[u] * ea[u] - ddte[u] * dte[u] + jnp.where(last_row, d_aend[u], 0.0) for u in units}
        triu = (lane >= row).astype(F32)
        for e in ex:
            dcb, dc_acc, db_acc = zero, zero, zero
            dacs, dacs_t, ddt, ddt_t = zero, zero, zero, zero
            dskip_row = jnp.zeros((1, LANES), F32)
            for r in heads:
                u = (e, r)
                dcb = dcb + gd[u] * dt_row[u]
                dc_acc = dc_acc + ea[u] * t1[u]
                db_acc = db_acc + k[u] * dbs[u]
                dacs = jnp.where(lane == r, dacs_col[u], dacs)
                ddt = jnp.where(lane == r, dk[u] * dte[u], ddt)
                dacs_t = jnp.where(row == r, -cs0[u] * dt_row[u], dacs_t)
                ddt_t = jnp.where(row == r, cs0[u], ddt_t)
                dsk = _sum_all(jnp.where(half[r][0:1, :], sl(dskip_cols[e], r), 0.0))
                dskip_row = dskip_row + jnp.where(lane1 == r, dsk, 0.0)
            for r in range(0, HPG, 2):
                dxs_ref[e, :, pair(r)] = (dx[e, r] + dx[e, r + 1] + sl(dy[e], r) * dsk_ref[:, pair(r)]).astype(BF16)
                ed_pair = jnp.where(lane1 < HEAD_DIM, ed[e, r], ed[e, r + 1])
                ds_ref[e, :, pair(r)] = ds[e, r] + ds[e, r + 1] + ed_pair * ds_old[e, r]
            dacs = dacs + dacs_t.T
            ddt = ddt + ddt_t.T
            dda = _dot_exact(triu, dacs)
            ddt = ddt + dda * a
            da = jnp.sum(dda * q[e]["dt"], axis=0, keepdims=True)
            draw = jnp.where(valid, ddt * _sigmoid(dtr_ref[e] + dtb_ref[0]), 0.0)
            ddt_ref[e] = draw.astype(BF16)
            dsm_ref[e, 0, 0:1, :] += dskip_row
            dsm_ref[e, 0, 1:2, :] += da * a
            dsm_ref[e, 0, 2:3, :] += jnp.sum(draw, axis=0, keepdims=True)
            dc_ref[e] = (dc_acc + _dot(dcb, bm[e])).astype(BF16)
            db_ref[e] = (db_acc + _dot_tn(dcb, cm[e])).astype(BF16)

    grp_out = pl.BlockSpec((bsz, CHUNK, D_STATE), lambda g, j: (0, nc - 1 - j, g))
    grid = (N_GROUPS, nc)
    ride = _Ride(rider, body, 15, 7, 1, grid)
    outs = pl.pallas_call(
        ride.body, name=name, grid=grid,
        in_specs=[sp["xs"], sp["bm"], sp["cm"], sp["lane_blk"], sp["lane_blk"], sp["lane_blk"], sp["tr"], sp["xs"],
                  sp["xs"], sp["state"], sp["xs"], sp["grp_const"], sp["grp_const"], sp["grp_vec"], sp["grp_vec"]]
        + ride.in_specs,
        out_specs=[sp["xs"], sp["xs"], grp_out, grp_out, sp["lane_blk"],
                   pl.BlockSpec((bsz, 1, GW), lambda g, j: (0, 0, g)),
                   pl.BlockSpec((bsz, 1, 8, LANES), lambda g, j: (0, g, 0, 0))] + ride.out_specs,
        out_shape=[jax.ShapeDtypeStruct((bsz, t, D_SSM), BF16), jax.ShapeDtypeStruct((bsz, t, D_SSM), BF16),
                   jax.ShapeDtypeStruct((bsz, t, N_GROUPS * D_STATE), BF16),
                   jax.ShapeDtypeStruct((bsz, t, N_GROUPS * D_STATE), BF16),
                   jax.ShapeDtypeStruct((bsz, t, D_DT), BF16), jax.ShapeDtypeStruct((bsz, 1, D_SSM), F32),
                   jax.ShapeDtypeStruct((bsz, N_GROUPS, 8, LANES), F32)] + ride.out_shape,
        scratch_shapes=[pltpu.VMEM((bsz, D_STATE, GW), F32)] + ride.scratch,
        compiler_params=_params(*ride.semantics(("parallel", "arbitrary"))),
    )(xc, xc, xc, dtr, dt, acs, tr, z, ypre, sprev, dyn, dtb, alog, dskip, normw, *ride.args)
    return outs[:7], outs[7:]


def _input_grad(dhn, h0, w, dres, seq, *, name):
    bsz, t, d = h0.shape
    nc = t // CHUNK

    def body(dy_ref, h_ref, w_ref, dres_ref, gx_ref, head_ref, dw_ref):
        j = pl.program_id(1)

        @pl.when((pl.program_id(0) == 0) & (j == 0))
        def _():
            dw_ref[...] = jnp.zeros_like(dw_ref)

        x, dyv = h_ref[0], dy_ref[0]
        r = lax.rsqrt(jnp.mean(x * x, axis=-1, keepdims=True) + EPS)
        g = dyv * w_ref[...]
        dx = r * (g - x * (r * r) * jnp.mean(g * x, axis=-1, keepdims=True)) + dres_ref[0]
        dw_ref[...] += jnp.sum(dyv * x * r, axis=0, keepdims=True)

        @pl.when(j == 0)
        def _():
            head_ref[0] = dx

        gx_ref[0] = dx

    row = pl.BlockSpec((1, CHUNK, d), lambda b, j: (b, j, 0))
    return pl.pallas_call(
        body, name=name, grid=(bsz, nc),
        in_specs=[row, row, pl.BlockSpec((1, d), lambda b, j: (0, 0)), row],
        out_specs=[pl.BlockSpec((1, CHUNK, d), lambda b, j: (b, jnp.maximum(j - 1, 0), 0)),
                   pl.BlockSpec((1, CHUNK, d), lambda b, j: (b, 0, 0)), pl.BlockSpec((1, d), lambda b, j: (0, 0))],
        out_shape=[jax.ShapeDtypeStruct((bsz, seq, d), F32), jax.ShapeDtypeStruct((bsz, CHUNK, d), F32),
                   jax.ShapeDtypeStruct((1, d), F32)],
        compiler_params=_params("arbitrary", "arbitrary"),
    )(dhn, h0, w, dres)


def _remote(src, dst, send_sem, recv_sem, dev):
    return pltpu.make_async_remote_copy(src_ref=src, dst_ref=dst, send_sem=send_sem, recv_sem=recv_sem,
                                        device_id=dev, device_id_type=MESH)


def _position():
    return lax.axis_index("x"), lax.axis_index("y"), lax.axis_index("c")


def _other_chips(pos):
    x, y, _ = pos
    return [(1 - x, y), (x, 1 - y), (1 - x, 1 - y)]


class _Gather:
    def __init__(self, arrs):
        n = len(arrs)
        self.args, self.n_in, self.n_out = list(arrs), n, n
        self.split = [a.ndim == 2 and a.shape[1] % (2 * LANES) == 0 for a in arrs]
        self.out_shape = [jax.ShapeDtypeStruct((4,) + a.shape, a.dtype) for a in arrs]
        self.scratch = [pltpu.SemaphoreType.DMA((3 * n,)), pltpu.SemaphoreType.DMA((3 * n,)),
                        pltpu.SemaphoreType.DMA((n,)), pltpu.SemaphoreType.DMA((3 * n,)),
                        pltpu.SemaphoreType.DMA((3 * n,))]

    def _copies(self, pos, ins, outs, sems):
        send_sems, recv_sems, loc_sems, pass_send_sems, pass_recv_sems = sems
        x, y, c = pos
        me, sibling = 2 * x + y, (x, y, 1 - c)
        local = [pltpu.make_async_copy(ins[i], outs[i].at[me], loc_sems.at[i]) for i in range(self.n_in)]
        sends, recvs, passes, pass_recvs = [], [], [], []
        for i in range(self.n_in):
            half = self.args[i].shape[1] // 2 if self.split[i] else None
            for k, (px, py) in enumerate(_other_chips(pos)):
                them = 2 * px + py
                sems_k = (send_sems.at[3 * i + k], recv_sems.at[3 * i + k], (px, py, c))
                if half is None:
                    sends.append(_remote(ins[i], outs[i].at[me], *sems_k))
                    recvs.append(_remote(ins[i], outs[i].at[them], *sems_k))
                    passes.append(None)
                    continue
                mine = pl.ds(pl.multiple_of(c * half, LANES), half)
                other = pl.ds(pl.multiple_of((1 - c) * half, LANES), half)
                sends.append(_remote(ins[i].at[:, mine], outs[i].at[me, :, mine], *sems_k))
                recvs.append(_remote(ins[i].at[:, mine], outs[i].at[them, :, mine], *sems_k))
                pass_k = (pass_send_sems.at[3 * i + k], pass_recv_sems.at[3 * i + k], sibling)
                passes.append(_remote(outs[i].at[them, :, mine], outs[i].at[them, :, mine], *pass_k))
                pass_recvs.append(_remote(outs[i].at[them, :, other], outs[i].at[them, :, other], *pass_k))
        return local, sends, recvs, passes, pass_recvs

    def start(self, pos, ins, outs, sems):
        local, sends = self._copies(pos, ins, outs, sems)[:2]
        for cp in local + sends:
            cp.start()

    def finish(self, pos, ins, outs, sems):
        local, sends, recvs, passes, pass_recvs = self._copies(pos, ins, outs, sems)
        for cp, onward in zip(recvs, passes):
            cp.wait_recv()
            if onward is not None:
                onward.start()
        for cp in pass_recvs:
            cp.wait_recv()
        for cp in sends + [p for p in passes if p is not None]:
            cp.wait_send()
        for cp in local:
            cp.wait()


class _Exchange:
    FLIPS = [(fx, fy, fc) for fx in (0, 1) for fy in (0, 1) for fc in (0, 1)][1:]

    def __init__(self, big, small=None):
        n = len(big)
        self.n_big, self.has_small = n, small is not None
        self.args = list(big) + ([small] if self.has_small else [])
        self.n_in = self.n_out = len(self.args)
        self.out_shape = [jax.ShapeDtypeStruct(a.shape, a.dtype) for a in big]
        self.scratch = [pltpu.SemaphoreType.DMA((max(3 * n, 1),)), pltpu.SemaphoreType.DMA((max(3 * n, 1),))]
        if self.has_small:
            self.out_shape.append(jax.ShapeDtypeStruct((8,) + small.shape, small.dtype))
            self.scratch += [pltpu.SemaphoreType.DMA((7,)), pltpu.SemaphoreType.DMA((7,)), pltpu.SemaphoreType.DMA((1,))]

    def _copies(self, pos, ins, outs, sems):
        x, y, c = pos
        me, me8 = 2 * x + y, 4 * x + 2 * y + c
        local, sends, recvs = [], [], []
        for i in range(self.n_big):
            for k, (px, py) in enumerate(_other_chips(pos)):
                sems_k = (sems[0].at[3 * i + k], sems[1].at[3 * i + k], (px, py, c))
                sends.append(_remote(ins[i].at[2 * px + py], outs[i].at[me], *sems_k))
                recvs.append(_remote(ins[i].at[me], outs[i].at[2 * px + py], *sems_k))
        if self.has_small:
            small, landed = ins[self.n_big], outs[self.n_big]
            local.append(pltpu.make_async_copy(small, landed.at[me8], sems[4].at[0]))
            for k, (fx, fy, fc) in enumerate(self.FLIPS):
                peer = (x ^ fx, y ^ fy, c ^ fc)
                sems_k = (sems[2].at[k], sems[3].at[k], peer)
                sends.append(_remote(small, landed.at[me8], *sems_k))
                recvs.append(_remote(small, landed.at[4 * peer[0] + 2 * peer[1] + peer[2]], *sems_k))
        return local, sends, recvs, [None] * len(recvs), []

    start = _Gather.start
    finish = _Gather.finish


class _Swap:
    def __init__(self, arrs):
        n = len(arrs)
        self.args, self.n_in, self.n_out = list(arrs), n, n
        self.out_shape = [jax.ShapeDtypeStruct(a.shape, a.dtype) for a in arrs]
        self.scratch = [pltpu.SemaphoreType.DMA((n,)), pltpu.SemaphoreType.DMA((n,))]

    def _copies(self, pos, ins, outs, sems):
        x, y, c = pos
        both = [_remote(ins[i], outs[i], sems[0].at[i], sems[1].at[i], (x, y, 1 - c)) for i in range(self.n_in)]
        return [], both, both, [None] * len(both), []

    start = _Gather.start
    finish = _Gather.finish


def _comm(rider, *, name):
    a, b = rider.n_in, rider.n_in + rider.n_out

    def body(*refs):
        pos = _position()
        rider.start(pos, refs[:a], refs[a:b], refs[b:])
        rider.finish(pos, refs[:a], refs[a:b], refs[b:])

    return pl.pallas_call(body, name=name, in_specs=[ANY] * rider.n_in, out_specs=[ANY] * rider.n_out,
                          out_shape=rider.out_shape, scratch_shapes=rider.scratch)(*rider.args)


class _Ride:
    def __init__(self, rider, body, n_in, n_out, n_scratch, grid):
        self.rider = rider
        self.args = rider.args if rider else []
        self.in_specs = [ANY] * rider.n_in if rider else []
        self.out_specs = [ANY] * rider.n_out if rider else []
        self.out_shape = rider.out_shape if rider else []
        self.scratch = rider.scratch if rider else []
        self.body = self._wrap(body, n_in, n_out, n_scratch, grid) if rider else body

    def semantics(self, sem):
        return ("arbitrary",) * len(sem) if self.rider else sem

    def _wrap(self, body, n_in, n_out, n_scratch, grid):
        rider = self.rider
        a = n_in
        b = a + rider.n_in
        c = b + n_out
        d = c + rider.n_out
        e = d + n_scratch

        def wrapped(*refs):
            pos = _position()
            ids = [pl.program_id(i) for i in range(len(grid))]
            first = functools.reduce(jnp.logical_and, [i == 0 for i in ids])
            last = functools.reduce(jnp.logical_and, [i == g - 1 for i, g in zip(ids, grid)])

            @pl.when(first)
            def _():
                rider.start(pos, refs[a:b], refs[c:d], refs[e:])

            body(*refs[:a], *refs[b:c], *refs[d:e])

            @pl.when(last)
            def _():
                rider.finish(pos, refs[a:b], refs[c:d], refs[e:])

        return wrapped


def _elementwise_tiles(r, c):
    if r % 8 == 0 and r * c > 65536:
        tm = _pick(r, (256, 128, 64, 16, 8))
        return (tm, c), r // tm, lambda i: (i, 0)
    if r % 8 and c % 256 == 0 and r * c > 65536:
        return (r, 256), c // 256, lambda i: (0, i)
    return (r, c), 1, lambda i: (0, 0)


def _chip_sum(own, landed, *, name):
    r, c = own.shape
    blk, steps, at = _elementwise_tiles(r, c)

    def body(own_ref, land_ref, o_ref):
        me = 2 * lax.axis_index("x") + lax.axis_index("y")
        acc = None
        for jchip in range(4):
            term = jnp.where(me == jchip, own_ref[...], land_ref[jchip].astype(F32))
            acc = term if acc is None else acc + term
        o_ref[...] = acc

    return pl.pallas_call(
        body, name=name, grid=(steps,),
        in_specs=[pl.BlockSpec(blk, at), pl.BlockSpec((4,) + blk, lambda i: (0,) + at(i))],
        out_specs=pl.BlockSpec(blk, at), out_shape=jax.ShapeDtypeStruct((r, c), F32),
        compiler_params=_params("parallel"),
    )(own, landed)


def _device_sum(parts, *, name):
    _, r, c = parts.shape

    def body(p_ref, o_ref):
        acc = p_ref[0]
        for d in range(1, 8):
            acc = acc + p_ref[d]
        o_ref[...] = acc

    return pl.pallas_call(body, name=name, out_shape=jax.ShapeDtypeStruct((r, c), F32))(parts)


def _adamw_math(w, g, m, v):
    m = ADAM_B1 * m + (1.0 - ADAM_B1) * g
    v = ADAM_B2 * v + (1.0 - ADAM_B2) * (g * g)
    m_hat = m / (1.0 - ADAM_B1 ** ADAM_STEP)
    v_hat = v / (1.0 - ADAM_B2 ** ADAM_STEP)
    return -ADAM_LR * (m_hat / (jnp.sqrt(v_hat) + ADAM_EPS) + ADAM_WD * w), m, v


def _adamw(w, g_parts, m, v, *, name):
    r, c = w.shape
    shape, steps, at = _elementwise_tiles(r, c)
    n_g = len(g_parts)

    def body(*refs):
        w_ref, m_ref, v_ref = refs[n_g:n_g + 3]
        g_ref, d_ref, nm_ref, nv_ref = refs[n_g + 3:]
        g = refs[0][...]
        for p in refs[1:n_g]:
            g = g + p[...]
        g_ref[...] = g
        d_ref[...], nm_ref[...], nv_ref[...] = _adamw_math(w_ref[...], g, m_ref[...], v_ref[...])

    blk = pl.BlockSpec(shape, at)
    return pl.pallas_call(
        body, name=name, grid=(steps,), in_specs=[blk] * (n_g + 3), out_specs=[blk] * 4,
        out_shape=[jax.ShapeDtypeStruct((r, c), F32)] * 4, compiler_params=_params("parallel"),
    )(*g_parts, w, m, v)


def _pad_heads(v):
    return jnp.pad(v.reshape(N_GROUPS, 1, HPG), ((0, 0), (0, 0), (0, LANES - HPG)))


def _unpad_heads(v):
    return v[:, :HPG].reshape(1, N_HEADS)


_SMALL_EARLY = [("pool_w", (512, 128)), ("pool_scale", (1, 512)), ("conv_w", (4, D_XBC)), ("conv_b", (1, D_XBC)),
                ("dt_bias", (1, N_HEADS)), ("a_log", (1, N_HEADS)), ("d_skip", (1, N_HEADS)), ("ssm_norm_w", (1, D_SSM)),
                ("norm_ffn_w", (1, 1024)), ("norm_f_w", (1, 1024))]
_SMALL_LATE = [("norm_mix_w", (1, 1024)), ("meta", (N_META, 1024)), ("loss", (1, 1))]


def _pack_small(grads, layout):
    rows = []
    for nm, shape in layout:
        flat = grads[nm].reshape(-1)
        rows.append(jnp.pad(flat, (0, (-flat.size) % LANES)).reshape(-1, LANES))
    packed = jnp.concatenate(rows, axis=0)
    return jnp.pad(packed, ((0, (-packed.shape[0]) % 8), (0, 0)))


def _unpack_small(packed, layout):
    out, r0 = {}, 0
    for nm, shape in layout:
        size = shape[0] * shape[1]
        nrow = -(-size // LANES)
        out[nm] = packed[r0:r0 + nrow].reshape(-1)[:size].reshape(shape)
        r0 += nrow
    return out


def kernel(x, meta, norm_mix_w, w_in, pool_w, pool_scale, conv_w, conv_b, dt_bias, a_log, d_skip, ssm_norm_w, w_out, norm_ffn_w, w_ff1, w_ff2, norm_f_w, loss_target, m_meta, m_norm_mix_w, m_w_in, m_pool_w, m_pool_scale, m_conv_w, m_conv_b, m_dt_bias, m_a_log, m_d_skip, m_ssm_norm_w, m_w_out, m_norm_ffn_w, m_w_ff1, m_w_ff2, m_norm_f_w, v_meta, v_norm_mix_w, v_w_in, v_pool_w, v_pool_scale, v_conv_w, v_conv_b, v_dt_bias, v_a_log, v_d_skip, v_ssm_norm_w, v_w_out, v_norm_ffn_w, v_w_ff1, v_w_ff2, v_norm_f_w):
    bsz, seq, d = x.shape
    t = seq + CHUNK
    n = bsz * t
    chip = 2 * lax.axis_index("x") + lax.axis_index("y")
    d_in = w_in.shape[2] * 4

    g_in, g_conv, g_meta = _comm(_Gather([w_in[0].T.astype(BF16), conv_w[0], meta]), name="gather_in")
    late_weights = _Gather([w_out[0].astype(BF16), w_ff1[0].astype(BF16), w_ff2[0].astype(BF16)])
    win = g_in.reshape(d_in, d)
    wu, wz = win[:D_POOL], win[D_POOL:D_POOL + D_SSM]
    wx = win[D_POOL + D_SSM:D_POOL + D_SSM + D_XBC]
    wdt = jnp.pad(win[D_POOL + D_SSM + D_XBC:].reshape(N_GROUPS, HPG, d),
                  ((0, 0), (0, LANES - HPG), (0, 0))).reshape(D_DT, d)
    convw = g_conv.transpose(1, 0, 2).reshape(CONV_W, D_XBC)
    meta_full = g_meta.transpose(1, 0, 2).reshape(N_META, d)
    dtb, alog = _pad_heads(dt_bias), _pad_heads(a_log)
    dskip = jnp.repeat(d_skip, HEAD_DIM, axis=1)
    poolw = pool_w[0]

    h0 = jnp.concatenate([jnp.zeros((bsz, PAD, d), F32), jnp.broadcast_to(meta_full[None], (bsz, N_META, d)), x], axis=1)
    h0f = h0.reshape(n, d)
    hn1 = _rms_fwd(h0f, norm_mix_w, name="norm_mix")
    u = _mm(hn1, wu, name="proj_u", nt=True)
    z = _mm(hn1, wz, name="proj_z", nt=True)
    xbc = _mm(hn1, wx, name="proj_xbc", nt=True)
    dtr = _mm(hn1, wdt, name="proj_dt", nt=True)
    ypool = _pool_fwd(u.reshape(bsz, t, D_POOL), poolw, pool_scale, name="pool_fwd")
    xbc3 = xbc.reshape(bsz, t, D_XBC)
    xc = _conv_fwd(xbc3, convw, conv_b, name="conv_fwd")
    z3, dtr3 = z.reshape(bsz, t, D_SSM), dtr.reshape(bsz, t, D_DT)
    dt3, acs3, tr3 = _ssd_prep(dtr3, dtb, alog, name="ssd_prep")
    (yn, ypre, sprev), (g_out, g_ff1, g_ff2) = _ssd_fwd(xc, dt3, acs3, tr3, z3, dskip, ssm_norm_w, name="ssd_fwd",
                                                        rider=late_weights)
    wo = g_out.reshape(D_POOL + D_SSM, d)
    wo_p, wo_s = wo[:D_POOL], wo[D_POOL:]
    w1 = g_ff1.transpose(1, 0, 2).reshape(d, D_FF)
    w2 = g_ff2.reshape(D_FF, d)
    ypool_f, yn_f = ypool.reshape(n, D_POOL), yn.reshape(n, D_SSM)
    add = lambda r, e: r + e
    h1 = _mm([ypool_f, yn_f], [wo_p, wo_s], name="out_proj", post=add, extras=(h0f,))
    hn2 = _rms_fwd(h1, norm_ffn_w, name="norm_ffn")
    act = _mm(hn2, w1, name="ff1", out_dtype=BF16)
    relu2 = lambda a: jnp.square(jnp.maximum(a, 0))
    h2 = _mm(act, w2, name="ff2", pre=relu2, post=add, extras=(h1,))
    dh2, dh2b, loss_acc, d_norm_f = _final_norm_loss(h2.reshape(bsz, t, d), loss_target, norm_f_w.reshape(1, d),
                                                     name="loss")

    dh2f, dh2bf = dh2.reshape(n, d), dh2b.reshape(n, d)
    dact = _mm(dh2bf, w2, name="ff2_bwd", nt=True, post=lambda r, a: r * (2.0 * jnp.maximum(a, 0).astype(F32)),
               extras=(act,), out_dtype=BF16)
    d_w2 = _mm_tn(act, dh2bf, name="ff2_dw", tk=2048, tn=1024, pre=relu2)
    d_w1 = _mm_tn(hn2, dact, name="ff1_dw", tk=1024, tn=2048)
    dh1, dh1b, d_norm_ffn = _mm_rms_bwd(dact, w1, h1, norm_ffn_w, dh2f, name="ff1_bwd")
    dypool = _mm(dh1b, wo_p, name="out_pool_bwd", nt=True)
    dyn = _mm(dh1b, wo_s, name="out_ssm_bwd", nt=True)
    d_wo_p = _mm_tn(ypool_f, dh1b, name="out_pool_dw", tk=512, tn=1024)
    d_wo_s = _mm_tn(yn_f, dh1b, name="out_ssm_dw", tk=1536, tn=1024)
    big_late = [jnp.concatenate([d_wo_p, d_wo_s], axis=0).reshape(4, (D_POOL + D_SSM) // 4, d),
                d_w1.reshape(d, 4, D_FF // 4).transpose(1, 0, 2), d_w2.reshape(4, D_FF // 4, d)]
    (dz, dxs, dbm, dcm, ddtr, d_nw, d_heads), landed_late = _ssd_bwd(
        xc, dtr3, dt3, acs3, tr3, z3, ypre, sprev, dyn.reshape(bsz, t, D_SSM), dtb, alog, dskip, ssm_norm_w, name="ssd_bwd",
        rider=_Exchange([b.astype(BF16) for b in big_late]))
    dxbc, d_convwb = _conv_bwd(xbc3, dxs, dbm, dcm, convw, conv_b, name="conv_bwd")
    du, d_poolw, d_poolsc = _pool_bwd(u.reshape(bsz, t, D_POOL), dypool.reshape(bsz, t, D_POOL), poolw, pool_scale,
                                      name="pool_bwd")
    duf, dzf, dxbcf, ddtrf = du.reshape(n, D_POOL), dz.reshape(n, D_SSM), dxbc.reshape(n, D_XBC), ddtr.reshape(n, D_DT)
    d_wu = _mm_tn(duf, hn1, name="proj_u_dw", tk=512, tn=1024)
    d_wz = _mm_tn(dzf, hn1, name="proj_z_dw", tk=1536, tn=1024)
    d_wx = _mm_tn(dxbcf, hn1, name="proj_xbc_dw", tk=1280, tn=1024)
    d_wdt = _mm_tn(ddtrf, hn1, name="proj_dt_dw", tk=512, tn=1024)
    d_win = jnp.concatenate([d_wu, d_wz, d_wx, d_wdt.reshape(N_GROUPS, LANES, d)[:, :HPG].reshape(N_HEADS, d)], axis=0)
    big_in = d_win.reshape(4, d_in // 4, d)
    heads = jnp.sum(d_heads, axis=0)
    small_early = _pack_small({
        "pool_w": d_poolw, "pool_scale": d_poolsc,
        "conv_w": jnp.sum(d_convwb[:, :CONV_W], axis=0), "conv_b": jnp.sum(d_convwb[:, CONV_W:CONV_W + 1], axis=0),
        "dt_bias": _unpad_heads(heads[:, 2]), "a_log": _unpad_heads(heads[:, 1]), "d_skip": _unpad_heads(heads[:, 0]),
        "ssm_norm_w": jnp.sum(d_nw, axis=0), "norm_ffn_w": d_norm_ffn, "norm_f_w": d_norm_f}, _SMALL_EARLY)
    dhn1, (landed_in, early_all) = _mm([duf, dzf, dxbcf, ddtrf], [wu, wz, wx, wdt], name="proj_bwd",
                                       rider=_Exchange([big_in.astype(BF16)], small_early))
    grad_x, d_head_rows, d_norm_mix = _input_grad(
        dhn1.reshape(bsz, t, d), h0, norm_mix_w, dh1.reshape(bsz, t, d), seq, name="input_grad")

    big = [big_in] + big_late
    landed = [landed_in] + list(landed_late)
    small_late = _pack_small({"norm_mix_w": d_norm_mix, "meta": jnp.sum(d_head_rows[:, PAD:], axis=0),
                              "loss": loss_acc[0:1, 0:1]}, _SMALL_LATE)
    (late_all,) = _comm(_Exchange([], small_late), name="exchange_small")
    own = [lax.dynamic_index_in_dim(b, chip, 0, keepdims=False) for b in big]
    mine = [_chip_sum(o, l, name=f"chip_sum_{i}") for i, (o, l) in enumerate(zip(own, landed))]
    theirs = _comm(_Swap(mine), name="swap_cores")
    gsmall = {**_unpack_small(_device_sum(early_all, name="device_sum_early"), _SMALL_EARLY),
              **_unpack_small(_device_sum(late_all, name="device_sum_late"), _SMALL_LATE)}
    gsmall["conv_w"] = lax.dynamic_slice_in_dim(gsmall["conv_w"], chip * (D_XBC // 4), D_XBC // 4, axis=1)
    gsmall["meta"] = lax.dynamic_slice_in_dim(gsmall["meta"], chip * (d // 4), d // 4, axis=1)
    loss = gsmall["loss"][0, 0]

    given = dict(meta=(meta, m_meta, v_meta), norm_mix_w=(norm_mix_w, m_norm_mix_w, v_norm_mix_w),
                 w_in=(w_in, m_w_in, v_w_in), pool_w=(pool_w, m_pool_w, v_pool_w),
                 pool_scale=(pool_scale, m_pool_scale, v_pool_scale), conv_w=(conv_w, m_conv_w, v_conv_w),
                 conv_b=(conv_b, m_conv_b, v_conv_b), dt_bias=(dt_bias, m_dt_bias, v_dt_bias),
                 a_log=(a_log, m_a_log, v_a_log), d_skip=(d_skip, m_d_skip, v_d_skip),
                 ssm_norm_w=(ssm_norm_w, m_ssm_norm_w, v_ssm_norm_w), w_out=(w_out, m_w_out, v_w_out),
                 norm_ffn_w=(norm_ffn_w, m_norm_ffn_w, v_norm_ffn_w), w_ff1=(w_ff1, m_w_ff1, v_w_ff1),
                 w_ff2=(w_ff2, m_w_ff2, v_w_ff2), norm_f_w=(norm_f_w, m_norm_f_w, v_norm_f_w))
    big_names = ["w_in", "w_out", "w_ff1", "w_ff2"]
    results = {}
    for nm, (w, m, v) in given.items():
        if nm in big_names:
            i = big_names.index(nm)
            parts, shape2 = (mine[i], theirs[i]), mine[i].shape
        else:
            parts, shape2 = (gsmall[nm],), gsmall[nm].shape
        if nm == "w_in":
            outs = _adamw(w[0].T, parts, m[0].T, v[0].T, name=f"adamw_{nm}")
            results[nm] = [o.T[None] for o in outs]
        else:
            outs = _adamw(w.reshape(shape2), parts, m.reshape(shape2), v.reshape(shape2), name=f"adamw_{nm}")
            results[nm] = [o.reshape(w.shape) for o in outs]
    order = list(given)
    return (loss, grad_x, *[results[nm][0] for nm in order], *[results[nm][1] for nm in order],
            *[results[nm][2] for nm in order], *[results[nm][3] for nm in order])
```

```python
import functools

import jax
import jax.numpy as jnp
from jax import lax
from jax.experimental import pallas as pl
from jax.experimental.pallas import tpu as pltpu

F32 = jnp.float32
BF16 = jnp.bfloat16
MESH = pl.DeviceIdType.MESH
ANY = pl.BlockSpec(memory_space=pl.ANY)

D_MODEL = 1024
N_META = 16
CHUNK = 128
PAD = CHUNK - N_META
POOL_WINDOWS = (2, 4, 8, 16)
D_POOL = 512
POOL_GROUP = 128
D_SSM = 1536
N_HEADS = 24
N_GROUPS = 4
HPG = 6
HEAD_DIM = 64
D_STATE = 128
GW = HPG * HEAD_DIM
D_XBC = D_SSM + 2 * N_GROUPS * D_STATE
D_DT = N_GROUPS * 128
D_FF = 4096
CONV_W = 4
EPS = 1e-5
LANES = 128
VMEM_LIMIT = 56 * 1024 * 1024

ADAM_LR, ADAM_B1, ADAM_B2, ADAM_EPS, ADAM_WD, ADAM_STEP = 0.001, 0.9, 0.999, 1e-08, 0.01, 10


def _params(*sem):
    return pltpu.CompilerParams(dimension_semantics=sem, vmem_limit_bytes=VMEM_LIMIT)


def _pick(n, cands):
    for c in cands:
        if n % c == 0:
            return c
    raise ValueError(f"no block size for {n}")


def _dot(a, b):
    return jnp.dot(a.astype(BF16), b.astype(BF16), preferred_element_type=F32)


def _dot_nt(a, b):
    return lax.dot_general(a.astype(BF16), b.astype(BF16), (((1,), (1,)), ((), ())), preferred_element_type=F32)


def _dot_tn(a, b):
    return lax.dot_general(a.astype(BF16), b.astype(BF16), (((0,), (0,)), ((), ())), preferred_element_type=F32)


def _dot_exact(mask, x):
    m = mask.astype(BF16)
    hi = x.astype(BF16)
    r1 = x - hi.astype(F32)
    mid = r1.astype(BF16)
    lo = (r1 - mid.astype(F32)).astype(BF16)
    dot = lambda t: jnp.dot(m, t, preferred_element_type=F32)
    return dot(hi) + dot(mid) + dot(lo)


def _sigmoid(x):
    return 1.0 / (1.0 + jnp.exp(-x))


def _softplus(x):
    return jnp.maximum(x, 0.0) + jnp.log1p(jnp.exp(-jnp.abs(x)))


def _sum_all(x):
    return jnp.sum(jnp.sum(x, axis=1, keepdims=True), axis=0, keepdims=True)


ROW_TILES = (2816, 2112, 1408, 1056, 768, 704, 512, 384, 256, 128)
TILE_BUDGET = 28 * 1024 * 1024


def _row_tile(n, bytes_per_row, fixed_bytes, budget=TILE_BUDGET):
    for tm in ROW_TILES:
        if n % tm == 0 and 2 * (tm * bytes_per_row + fixed_bytes) <= budget:
            return tm
    raise ValueError(f"no row tile for {n}")


WIDE_BUDGET = 38 * 1024 * 1024


def _mm(a, w, *, name, tn=512, nt=False, pre=None, post=None, extras=(), out_dtype=F32, rider=None):
    a_list = list(a) if isinstance(a, (list, tuple)) else [a]
    w_list = list(w) if isinstance(w, (list, tuple)) else [w]
    n_a, n_ex = len(a_list), len(extras)
    n = a_list[0].shape[0]
    m = w_list[0].shape[0] if nt else w_list[0].shape[1]
    tn = min(tn, m)
    size = lambda dt: jnp.dtype(dt).itemsize
    per_row = (sum(x.shape[1] * size(x.dtype) for x in a_list) + m * size(out_dtype)
               + sum(m * size(e.dtype) for e in extras))
    tm = _row_tile(n, per_row, sum(x.size * size(x.dtype) for x in w_list) // 2, WIDE_BUDGET)

    def body(*refs):
        a_refs, w_refs, ex_refs, o_ref = refs[:n_a], refs[n_a:2 * n_a], refs[2 * n_a:2 * n_a + n_ex], refs[2 * n_a + n_ex]
        avs = [(a_ref[...] if pre is None else pre(a_ref[...])).astype(BF16) for a_ref in a_refs]
        for c0 in range(0, m, tn):
            r = None
            for av, w_ref in zip(avs, w_refs):
                term = _dot_nt(av, w_ref[c0:c0 + tn, :]) if nt else _dot(av, w_ref[:, c0:c0 + tn])
                r = term if r is None else r + term
            if post is not None:
                r = post(r, *[e[:, c0:c0 + tn] for e in ex_refs])
            o_ref[:, c0:c0 + tn] = r.astype(out_dtype)

    a_specs = [pl.BlockSpec((tm, x.shape[1]), lambda i: (i, 0)) for x in a_list]
    w_specs = [pl.BlockSpec(x.shape, lambda i: (0, 0), pipeline_mode=pl.Buffered(1)) for x in w_list]
    blk = pl.BlockSpec((tm, m), lambda i: (i, 0))
    grid = (n // tm,)
    ride = _Ride(rider, body, 2 * n_a + n_ex, 1, 0, grid)
    outs = pl.pallas_call(
        ride.body, name=name, grid=grid,
        in_specs=a_specs + w_specs + [blk] * n_ex + ride.in_specs,
        out_specs=[blk] + ride.out_specs, out_shape=[jax.ShapeDtypeStruct((n, m), out_dtype)] + ride.out_shape,
        scratch_shapes=ride.scratch, compiler_params=_params(*ride.semantics(("parallel",))),
    )(*a_list, *w_list, *extras, *ride.args)
    return (outs[0], outs[1:]) if rider else outs[0]


def _mm_tn(a, g, *, name, tk, tn, pre=None):
    n, k = a.shape
    m = g.shape[1]
    tk, tn = min(tk, k), min(tn, m)
    tm = _row_tile(n, tk * jnp.dtype(a.dtype).itemsize + tn * jnp.dtype(g.dtype).itemsize, tk * tn * 4)

    def body(a_ref, g_ref, o_ref):
        @pl.when(pl.program_id(2) == 0)
        def _():
            o_ref[...] = jnp.zeros_like(o_ref)

        av = a_ref[...]
        if pre is not None:
            av = pre(av)
        o_ref[...] += _dot_tn(av, g_ref[...])

    return pl.pallas_call(
        body, name=name, grid=(k // tk, m // tn, n // tm),
        in_specs=[pl.BlockSpec((tm, tk), lambda i, j, r: (r, i)), pl.BlockSpec((tm, tn), lambda i, j, r: (r, j))],
        out_specs=pl.BlockSpec((tk, tn), lambda i, j, r: (i, j)),
        out_shape=jax.ShapeDtypeStruct((k, m), F32),
        compiler_params=_params("parallel", "parallel", "arbitrary"),
    )(a, g)


def _mm_rms_bwd(a, w, h, w_norm, dres, *, name, tk):
    n, k = a.shape
    d = h.shape[1]
    single = tk == k
    tm = _row_tile(n, tk * jnp.dtype(a.dtype).itemsize + d * (4 + 4 + 4 + 2), d * tk * (1 if single else 2), WIDE_BUDGET)
    last = k // tk - 1

    def body(a_ref, w_ref, h_ref, wn_ref, dres_ref, dx_ref, dxb_ref, dw_ref, acc_ref):
        i, kk = pl.program_id(0), pl.program_id(1)

        @pl.when((i == 0) & (kk == 0))
        def _():
            dw_ref[...] = jnp.zeros_like(dw_ref)

        part = _dot_nt(a_ref[...], w_ref[...])

        @pl.when(kk == 0)
        def _():
            acc_ref[...] = part

        @pl.when(kk > 0)
        def _():
            acc_ref[...] += part

        @pl.when(kk == last)
        def _():
            x, dyv = h_ref[...], acc_ref[...]
            r = lax.rsqrt(jnp.mean(x * x, axis=-1, keepdims=True) + EPS)
            g = dyv * wn_ref[...]
            dx = r * (g - x * (r * r) * jnp.mean(g * x, axis=-1, keepdims=True)) + dres_ref[...]
            dx_ref[...] = dx
            dxb_ref[...] = dx.astype(BF16)
            dw_ref[...] += jnp.sum(dyv * x * r, axis=0, keepdims=True)

    row = pl.BlockSpec((tm, d), lambda i, kk: (i, 0))
    vec = pl.BlockSpec((1, d), lambda i, kk: (0, 0))
    return pl.pallas_call(
        body, name=name, grid=(n // tm, k // tk),
        in_specs=[pl.BlockSpec((tm, tk), lambda i, kk: (i, kk)),
                  pl.BlockSpec((d, tk), lambda i, kk: (0, kk), pipeline_mode=pl.Buffered(1) if single else None),
                  row, vec, row],
        out_specs=[row, row, vec],
        out_shape=[jax.ShapeDtypeStruct((n, d), F32), jax.ShapeDtypeStruct((n, d), BF16), jax.ShapeDtypeStruct((1, d), F32)],
        scratch_shapes=[pltpu.VMEM((tm, d), F32)],
        compiler_params=_params("arbitrary", "arbitrary"),
    )(a, w, h, w_norm, dres)


def _rms_fwd(h, w, *, name):
    n, d = h.shape
    tm = _pick(n, (768, 512, 256, 128))

    def body(h_ref, w_ref, o_ref):
        x = h_ref[...]
        r = lax.rsqrt(jnp.mean(x * x, axis=-1, keepdims=True) + EPS)
        o_ref[...] = (x * r * w_ref[...]).astype(BF16)

    return pl.pallas_call(
        body, name=name, grid=(n // tm,),
        in_specs=[pl.BlockSpec((tm, d), lambda i: (i, 0)), pl.BlockSpec((1, d), lambda i: (0, 0))],
        out_specs=pl.BlockSpec((tm, d), lambda i: (i, 0)), out_shape=jax.ShapeDtypeStruct((n, d), BF16),
        compiler_params=_params("parallel"),
    )(h, w)


def _final_norm_loss(h2, target, w, *, name):
    bsz, t, d = h2.shape
    nc = t // CHUNK

    def body(h_ref, t_ref, w_ref, dh_ref, dhb_ref, loss_ref, dw_ref):
        j = pl.program_id(1)

        @pl.when((pl.program_id(0) == 0) & (j == 0))
        def _():
            loss_ref[...] = jnp.zeros_like(loss_ref)
            dw_ref[...] = jnp.zeros_like(dw_ref)

        x, wv = h_ref[0], w_ref[...]
        r = lax.rsqrt(jnp.mean(x * x, axis=-1, keepdims=True) + EPS)
        diff = jnp.where(j > 0, x * r * wv - t_ref[0], 0.0)
        loss_ref[...] += _sum_all(diff * diff) * (0.5 / d)
        dy = diff * (1.0 / d)
        g = dy * wv
        dh = r * (g - x * (r * r) * jnp.mean(g * x, axis=-1, keepdims=True))
        dh_ref[0] = dh
        dhb_ref[0] = dh.astype(BF16)
        dw_ref[...] += jnp.sum(dy * x * r, axis=0, keepdims=True)

    row = pl.BlockSpec((1, CHUNK, d), lambda b, j: (b, j, 0))
    return pl.pallas_call(
        body, name=name, grid=(bsz, nc),
        in_specs=[row, pl.BlockSpec((1, CHUNK, d), lambda b, j: (b, jnp.maximum(j - 1, 0), 0)),
                  pl.BlockSpec((1, d), lambda b, j: (0, 0))],
        out_specs=[row, row, pl.BlockSpec((8, LANES), lambda b, j: (0, 0)), pl.BlockSpec((1, d), lambda b, j: (0, 0))],
        out_shape=[jax.ShapeDtypeStruct((bsz, t, d), F32), jax.ShapeDtypeStruct((bsz, t, d), BF16),
                   jax.ShapeDtypeStruct((8, LANES), F32), jax.ShapeDtypeStruct((1, d), F32)],
        compiler_params=_params("arbitrary", "arbitrary"),
    )(h2, target, w)


def _pool_masks(j, transposed):
    r = lax.broadcasted_iota(jnp.int32, (CHUNK, 2 * CHUNK), 0)
    c = lax.broadcasted_iota(jnp.int32, (CHUNK, 2 * CHUNK), 1)
    masks = []
    for w in POOL_WINDOWS:
        if transposed:
            m = (c >= r) & (c < r + w)
        else:
            s = c - CHUNK
            m = (s <= r) & (s > r - w) & (s + j * CHUNK >= 0)
        masks.append(m.astype(F32))
    return masks


def _pool_count(t_global, w):
    return jnp.clip(t_global - PAD + 1, 1, w).astype(F32)


def _pool_fwd(u, pool_w, pool_scale, *, name):
    bsz, t, _ = u.shape
    nc = t // CHUNK

    def body(prev_ref, cur_ref, pw_ref, sc_ref, o_ref):
        j = pl.program_id(0)
        masks = _pool_masks(j, False)
        tg = j * CHUNK + lax.broadcasted_iota(jnp.int32, (CHUNK, 1), 0)
        count = [_pool_count(tg, w) for w in POOL_WINDOWS]
        units = [(e, gi) for e in range(bsz) for gi in range(len(POOL_WINDOWS))]
        sl = lambda gi: pl.ds(gi * POOL_GROUP, POOL_GROUP)
        cur = {(e, gi): cur_ref[e, :, sl(gi)] for e, gi in units}
        both = {(e, gi): jnp.concatenate([prev_ref[e, :, sl(gi)], cur[e, gi]], axis=0) for e, gi in units}
        win = {(e, gi): _dot_exact(masks[gi], both[e, gi]) for e, gi in units}
        pooled = {(e, gi): win[e, gi] / count[gi] - cur[e, gi] for e, gi in units}
        mixed = {(e, gi): _dot(pooled[e, gi], pw_ref[gi]) for e, gi in units}
        for e, gi in units:
            o_ref[e, :, sl(gi)] = (mixed[e, gi] * sc_ref[:, sl(gi)]).astype(BF16)

    blk = lambda f: pl.BlockSpec((bsz, CHUNK, D_POOL), f)
    return pl.pallas_call(
        body, name=name, grid=(nc,),
        in_specs=[blk(lambda j: (0, jnp.maximum(j - 1, 0), 0)), blk(lambda j: (0, j, 0)),
                  pl.BlockSpec((4, POOL_GROUP, POOL_GROUP), lambda j: (0, 0, 0)),
                  pl.BlockSpec((1, D_POOL), lambda j: (0, 0))],
        out_specs=blk(lambda j: (0, j, 0)), out_shape=jax.ShapeDtypeStruct(u.shape, BF16),
        compiler_params=_params("parallel"),
    )(u, u, pool_w, pool_scale)


def _pool_bwd(u, dyp, pool_w, pool_scale, *, name):
    bsz, t, _ = u.shape
    nc = t // CHUNK

    def body(prev_ref, cur_ref, dy_ref, dyn_ref, pw_ref, sc_ref, du_ref, dpw_ref, dsc_ref):
        j = pl.program_id(0)

        @pl.when(j == 0)
        def _():
            dpw_ref[...] = jnp.zeros_like(dpw_ref)
            dsc_ref[...] = jnp.zeros_like(dsc_ref)

        fwd = _pool_masks(j, False)
        bwd = _pool_masks(j, True)
        tg = j * CHUNK + lax.broadcasted_iota(jnp.int32, (CHUNK, 1), 0)
        count = [_pool_count(tg, w) for w in POOL_WINDOWS]
        count_next = [_pool_count(tg + CHUNK, w) for w in POOL_WINDOWS]
        has_next = j < nc - 1
        groups = range(len(POOL_WINDOWS))
        units = [(e, gi) for e in range(bsz) for gi in groups]
        sl = lambda gi: pl.ds(gi * POOL_GROUP, POOL_GROUP)
        cur = {(e, gi): cur_ref[e, :, sl(gi)] for e, gi in units}
        both = {(e, gi): jnp.concatenate([prev_ref[e, :, sl(gi)], cur[e, gi]], axis=0) for e, gi in units}
        win = {(e, gi): _dot_exact(fwd[gi], both[e, gi]) for e, gi in units}
        pooled = {(e, gi): win[e, gi] / count[gi] - cur[e, gi] for e, gi in units}
        dy = {(e, gi): dy_ref[e, :, sl(gi)] for e, gi in units}
        mixed = {(e, gi): _dot(pooled[e, gi], pw_ref[gi]) for e, gi in units}
        dm = {(e, gi): dy[e, gi] * sc_ref[:, sl(gi)] for e, gi in units}
        dm_next = {(e, gi): jnp.where(has_next, dyn_ref[e, :, sl(gi)], 0.0) * sc_ref[:, sl(gi)] for e, gi in units}
        dpw = {(e, gi): _dot_tn(pooled[e, gi], dm[e, gi]) for e, gi in units}
        dpooled = {(e, gi): _dot_nt(dm[e, gi], pw_ref[gi]) for e, gi in units}
        dpooled_next = {(e, gi): _dot_nt(dm_next[e, gi], pw_ref[gi]) for e, gi in units}
        spread = {(e, gi): jnp.concatenate([dpooled[e, gi] / count[gi], dpooled_next[e, gi] / count_next[gi]], axis=0)
                  for e, gi in units}
        back = {(e, gi): _dot_exact(bwd[gi], spread[e, gi]) for e, gi in units}
        for e, gi in units:
            du_ref[e, :, sl(gi)] = (back[e, gi] - dpooled[e, gi]).astype(BF16)
        for gi in groups:
            dsc, dw = None, None
            for e in range(bsz):
                term = jnp.sum(dy[e, gi] * mixed[e, gi], axis=0, keepdims=True)
                dsc = term if dsc is None else dsc + term
                dw = dpw[e, gi] if dw is None else dw + dpw[e, gi]
            dsc_ref[:, sl(gi)] += dsc
            dpw_ref[gi] += dw

    blk = lambda f: pl.BlockSpec((bsz, CHUNK, D_POOL), f)
    return pl.pallas_call(
        body, name=name, grid=(nc,),
        in_specs=[blk(lambda j: (0, jnp.maximum(j - 1, 0), 0)), blk(lambda j: (0, j, 0)),
                  blk(lambda j: (0, j, 0)), blk(lambda j: (0, jnp.minimum(j + 1, nc - 1), 0)),
                  pl.BlockSpec((4, POOL_GROUP, POOL_GROUP), lambda j: (0, 0, 0)),
                  pl.BlockSpec((1, D_POOL), lambda j: (0, 0))],
        out_specs=[blk(lambda j: (0, j, 0)), pl.BlockSpec((4, POOL_GROUP, POOL_GROUP), lambda j: (0, 0, 0)),
                   pl.BlockSpec((1, D_POOL), lambda j: (0, 0))],
        out_shape=[jax.ShapeDtypeStruct(u.shape, BF16), jax.ShapeDtypeStruct((4, POOL_GROUP, POOL_GROUP), F32),
                   jax.ShapeDtypeStruct((1, D_POOL), F32)],
        compiler_params=_params("arbitrary"),
    )(u, u, dyp, dyp, pool_w, pool_scale)


CONV_SLAB = 512


def _conv_taps(tail, cur, keep_tail):
    ext = jnp.concatenate([jnp.where(keep_tail, tail, 0.0), cur], axis=0)
    return [(pltpu.roll(ext, CONV_W - 1 - k, 0) if k < CONV_W - 1 else ext)[8:] for k in range(CONV_W)]


def _conv_pre(taps, w_ref, b_ref, sl):
    acc = b_ref[:, sl]
    for k in range(CONV_W):
        acc = acc + w_ref[k:k + 1, sl] * taps[k]
    return acc


def _conv_fwd(xbc, conv_w, conv_b, *, name):
    bsz, t, c = xbc.shape
    nc = t // CHUNK

    def body(tail_ref, cur_ref, w_ref, b_ref, o_ref):
        keep = pl.program_id(1) > 0
        for c0 in range(0, c, CONV_SLAB):
            sl = pl.ds(c0, CONV_SLAB)
            pre = _conv_pre(_conv_taps(tail_ref[0, :, sl], cur_ref[0, :, sl], keep), w_ref, b_ref, sl)
            o_ref[0, :, sl] = (pre * _sigmoid(pre)).astype(BF16)

    return pl.pallas_call(
        body, name=name, grid=(bsz, nc),
        in_specs=[pl.BlockSpec((1, 8, c), lambda b, j: (b, jnp.maximum(j * (CHUNK // 8) - 1, 0), 0)),
                  pl.BlockSpec((1, CHUNK, c), lambda b, j: (b, j, 0)),
                  pl.BlockSpec((CONV_W, c), lambda b, j: (0, 0)), pl.BlockSpec((1, c), lambda b, j: (0, 0))],
        out_specs=pl.BlockSpec((1, CHUNK, c), lambda b, j: (b, j, 0)), out_shape=jax.ShapeDtypeStruct(xbc.shape, BF16),
        compiler_params=_params("parallel", "parallel"),
    )(xbc, xbc, conv_w, conv_b)


def _conv_bwd(xbc, dxs, db, dc, conv_w, conv_b, *, name):
    bsz, t, c = xbc.shape
    nc = t // CHUNK
    halo = 16
    rows = CHUNK + halo

    def body(tail_ref, cur_ref, head_ref, dxs_ref, db_ref, dc_ref, dxs_head, db_head, dc_head, w_ref, b_ref,
             dx_ref, dwb_ref):
        j = pl.program_id(1)

        @pl.when(j == 0)
        def _():
            dwb_ref[...] = jnp.zeros_like(dwb_ref)

        has_prev, has_next = j > 0, j < nc - 1
        for c0 in range(0, c, CONV_SLAB):
            sl = pl.ds(c0, CONV_SLAB)
            if c0 < D_SSM:
                dxc, dxc_next = dxs_ref[0, :, sl], dxs_head[0, :, sl]
            elif c0 < D_SSM + D_POOL:
                dxc, dxc_next = db_ref[0], db_head[0]
            else:
                dxc, dxc_next = dc_ref[0], dc_head[0]
            dxc = jnp.concatenate([dxc.astype(F32), jnp.where(has_next, dxc_next.astype(F32), 0.0)], axis=0)
            ext = jnp.concatenate([jnp.where(has_prev, tail_ref[0, :, sl], 0.0), cur_ref[0, :, sl],
                                   jnp.where(has_next, head_ref[0, :, sl], 0.0)], axis=0)
            taps = [(pltpu.roll(ext, CONV_W - 1 - k, 0) if k < CONV_W - 1 else ext)[8:] for k in range(CONV_W)]
            pre = _conv_pre(taps, w_ref, b_ref, sl)
            s = _sigmoid(pre)
            dpre = dxc * (s * (1.0 + pre * (1.0 - s)))
            acc = w_ref[CONV_W - 1:CONV_W, sl] * dpre[:CHUNK]
            for k in range(CONV_W - 1):
                up = CONV_W - 1 - k
                acc = acc + w_ref[k:k + 1, sl] * pltpu.roll(dpre, rows - up, 0)[:CHUNK]
            dx_ref[0, :, sl] = acc.astype(BF16)
            for k in range(CONV_W):
                dwb_ref[0, k:k + 1, sl] += jnp.sum(dpre[:CHUNK] * taps[k][:CHUNK], axis=0, keepdims=True)
            dwb_ref[0, CONV_W:CONV_W + 1, sl] += jnp.sum(dpre[:CHUNK], axis=0, keepdims=True)

    assert CONV_SLAB == D_POOL and D_SSM % CONV_SLAB == 0
    row = lambda width: pl.BlockSpec((1, CHUNK, width), lambda b, j: (b, j, 0))
    nxt = lambda width: pl.BlockSpec(
        (1, halo, width), lambda b, j: (b, jnp.minimum((j + 1) * (CHUNK // halo), t // halo - 1), 0))
    return pl.pallas_call(
        body, name=name, grid=(bsz, nc),
        in_specs=[pl.BlockSpec((1, 8, c), lambda b, j: (b, jnp.maximum(j * (CHUNK // 8) - 1, 0), 0)), row(c), nxt(c),
                  row(D_SSM), row(D_POOL), row(D_POOL), nxt(D_SSM), nxt(D_POOL), nxt(D_POOL),
                  pl.BlockSpec((CONV_W, c), lambda b, j: (0, 0)), pl.BlockSpec((1, c), lambda b, j: (0, 0))],
        out_specs=[row(c), pl.BlockSpec((1, 8, c), lambda b, j: (b, 0, 0))],
        out_shape=[jax.ShapeDtypeStruct(xbc.shape, BF16), jax.ShapeDtypeStruct((bsz, 8, c), F32)],
        compiler_params=_params("parallel", "arbitrary"),
    )(xbc, xbc, xbc, dxs, db, dc, dxs, db, dc, conv_w, conv_b)


def _dt_valid(j):
    lane = lax.broadcasted_iota(jnp.int32, (CHUNK, LANES), 1)
    row = lax.broadcasted_iota(jnp.int32, (CHUNK, LANES), 0)
    return (lane < HPG) & ((j > 0) | (row >= PAD))


def _ssd_prep(dtr, dtb, alog, *, name):
    bsz, t, _ = dtr.shape
    nc = t // CHUNK

    def body(dtr_ref, dtb_ref, alog_ref, dt_ref, acs_ref, tr_ref):
        j = pl.program_id(0)
        valid = _dt_valid(j)
        row = lax.broadcasted_iota(jnp.int32, (CHUNK, LANES), 0)
        lane = lax.broadcasted_iota(jnp.int32, (CHUNK, LANES), 1)
        tril = (row >= lane).astype(F32)
        units = [(e, g) for e in range(bsz) for g in range(N_GROUPS)]
        sl = lambda g: pl.ds(g * LANES, LANES)
        dt = {(e, g): jnp.where(valid, _softplus(dtr_ref[e, :, sl(g)] + dtb_ref[g]), 0.0) for e, g in units}
        acs = {(e, g): _dot_exact(tril, dt[e, g] * -jnp.exp(alog_ref[g])) for e, g in units}
        for e, g in units:
            dt_ref[e, :, sl(g)] = dt[e, g]
            acs_ref[e, :, sl(g)] = acs[e, g]
            tr_ref[e, 0, g, 0:8, :] = dt[e, g].T[0:8]
            tr_ref[e, 0, g, 8:16, :] = acs[e, g].T[0:8]

    blk = pl.BlockSpec((bsz, CHUNK, D_DT), lambda j: (0, j, 0))
    const = pl.BlockSpec((N_GROUPS, 1, LANES), lambda j: (0, 0, 0))
    return pl.pallas_call(
        body, name=name, grid=(nc,), in_specs=[blk, const, const],
        out_specs=[blk, blk, pl.BlockSpec((bsz, 1, N_GROUPS, 16, LANES), lambda j: (0, j, 0, 0, 0))],
        out_shape=[jax.ShapeDtypeStruct(dtr.shape, F32), jax.ShapeDtypeStruct(dtr.shape, F32),
                   jax.ShapeDtypeStruct((bsz, nc, N_GROUPS, 16, LANES), F32)],
        compiler_params=_params("parallel"),
    )(dtr, dtb, alog)


def _ssd_decay(dt, acs, tr):
    lane = lax.broadcasted_iota(jnp.int32, (CHUNK, LANES), 1)
    row = lax.broadcasted_iota(jnp.int32, (CHUNK, LANES), 0)
    return dict(lane=lane, row=row, dt=dt, causal=row >= lane, acs=acs, acs_t=tr[8:16], dt_t=tr[0:8],
                aend=acs[CHUNK - 1:CHUNK, :])


def _ssd_specs(bsz, nc, rev):
    ch = (lambda j: nc - 1 - j) if rev else (lambda j: j)
    return dict(
        xs=pl.BlockSpec((bsz, CHUNK, GW), lambda g, j: (0, ch(j), g)),
        bm=pl.BlockSpec((bsz, CHUNK, D_STATE), lambda g, j: (0, ch(j), D_SSM // D_STATE + g)),
        cm=pl.BlockSpec((bsz, CHUNK, D_STATE), lambda g, j: (0, ch(j), D_SSM // D_STATE + N_GROUPS + g)),
        lane_blk=pl.BlockSpec((bsz, CHUNK, LANES), lambda g, j: (0, ch(j), g)),
        grp_const=pl.BlockSpec((1, 1, LANES), lambda g, j: (g, 0, 0)),
        grp_vec=pl.BlockSpec((1, GW), lambda g, j: (0, g)),
        state=pl.BlockSpec((bsz, 1, D_STATE, GW), lambda g, j: (0, ch(j), 0, g)),
        tr=pl.BlockSpec((bsz, 1, 1, 16, LANES), lambda g, j: (0, ch(j), g, 0, 0)),
    )


def _ssd_fwd(xc, dt, acs, tr, z, dskip, normw, *, name, rider=None):
    bsz, t, _ = xc.shape
    nc = t // CHUNK
    sp = _ssd_specs(bsz, nc, False)

    def body(xs_ref, b_ref, c_ref, dt_ref, acs_ref, tr_ref, z_ref, dsk_ref, nw_ref, yn_ref, y_ref, sp_ref, s_ref):
        j = pl.program_id(1)

        @pl.when(j == 0)
        def _():
            s_ref[...] = jnp.zeros_like(s_ref)

        ex = range(bsz)
        units = [(e, r) for e in ex for r in range(HPG)]
        full = lambda v: jnp.broadcast_to(v, (CHUNK, LANES))
        pair = lambda r: pl.ds((r // 2) * LANES, LANES)
        q = [_ssd_decay(dt_ref[e], acs_ref[e], tr_ref[e, 0, 0]) for e in ex]
        for e in ex:
            sp_ref[e, 0] = s_ref[e]
        bm, cm = [b_ref[e] for e in ex], [c_ref[e] for e in ex]
        cb = [_dot_nt(cm[e], bm[e]) for e in ex]
        low = q[0]["lane"] < HEAD_DIM
        col = {(e, r): full(q[e]["acs"][:, r:r + 1]) for e, r in units}
        aend = {(e, r): q[e]["aend"][:, r:r + 1] for e, r in units}
        decay = {(e, r): jnp.exp(jnp.where(q[e]["causal"], col[e, r] - q[e]["acs_t"][r:r + 1, :], -jnp.inf))
                 for e, r in units}
        mp = {(e, r): cb[e] * decay[e, r] * q[e]["dt_t"][r:r + 1, :] for e, r in units}
        ce = {(e, r): cm[e] * jnp.exp(col[e, r]) for e, r in units}
        bk = {(e, r): bm[e] * (jnp.exp(aend[e, r] - col[e, r]) * full(q[e]["dt"][:, r:r + 1])) for e, r in units}
        xp = {(e, r): xs_ref[e, :, pair(r)] for e, r in units}
        s_old = {(e, r): s_ref[e, :, pair(r)] for e, r in units}
        y_h = {u: _dot(mp[u], xp[u]) + _dot(ce[u], s_old[u]) for u in units}
        s_h = {u: jnp.exp(aend[u]) * s_old[u] + _dot_tn(bk[u], xp[u]) for u in units}
        for e in ex:
            for r in range(0, HPG, 2):
                y_ref[e, :, pair(r)] = jnp.where(low, y_h[e, r], y_h[e, r + 1])
                s_ref[e, :, pair(r)] = jnp.where(low, s_h[e, r], s_h[e, r + 1])
        y = [y_ref[e] + dsk_ref[...] * xs_ref[e] for e in ex]
        zz = [z_ref[e] for e in ex]
        yg = [y[e] * (zz[e] * _sigmoid(zz[e])) for e in ex]
        rstd = [lax.rsqrt(jnp.mean(yg[e] * yg[e], axis=-1, keepdims=True) + EPS) for e in ex]
        for e in ex:
            y_ref[e] = y[e]
            yn_ref[e] = (yg[e] * rstd[e] * nw_ref[...]).astype(BF16)

    grid = (N_GROUPS, nc)
    ride = _Ride(rider, body, 9, 3, 1, grid)
    outs = pl.pallas_call(
        ride.body, name=name, grid=grid,
        in_specs=[sp["xs"], sp["bm"], sp["cm"], sp["lane_blk"], sp["lane_blk"], sp["tr"], sp["xs"],
                  sp["grp_vec"], sp["grp_vec"]] + ride.in_specs,
        out_specs=[sp["xs"], sp["xs"], sp["state"]] + ride.out_specs,
        out_shape=[jax.ShapeDtypeStruct((bsz, t, D_SSM), BF16), jax.ShapeDtypeStruct((bsz, t, D_SSM), F32),
                   jax.ShapeDtypeStruct((bsz, nc, D_STATE, D_SSM), F32)] + ride.out_shape,
        scratch_shapes=[pltpu.VMEM((bsz, D_STATE, GW), F32)] + ride.scratch,
        compiler_params=_params(*ride.semantics(("parallel", "arbitrary"))),
    )(xc, xc, xc, dt, acs, tr, z, dskip, normw, *ride.args)
    return outs[:3], outs[3:]


def _ssd_bwd(xc, dtr, dt, acs, tr, z, ypre, sprev, dyn, dtb, alog, dskip, normw, *, name, rider=None):
    bsz, t, _ = xc.shape
    nc = t // CHUNK
    sp = _ssd_specs(bsz, nc, True)

    def body(xs_ref, b_ref, c_ref, dtr_ref, dt_ref, acs_ref, tr_ref, z_ref, y_ref, sp_ref, dyn_ref, dtb_ref, alog_ref,
             dsk_ref, nw_ref, dz_ref, dxs_ref, db_ref, dc_ref, ddt_ref, dnw_ref, dsm_ref, ds_ref):
        j = pl.program_id(1)

        @pl.when(j == 0)
        def _():
            ds_ref[...] = jnp.zeros_like(ds_ref)
            dnw_ref[...] = jnp.zeros_like(dnw_ref)
            dsm_ref[...] = jnp.zeros_like(dsm_ref)

        ex = range(bsz)
        heads = range(HPG)
        units = [(e, r) for e in ex for r in heads]
        q = [_ssd_decay(dt_ref[e], acs_ref[e], tr_ref[e, 0, 0]) for e in ex]
        a = -jnp.exp(alog_ref[0])
        valid = _dt_valid(nc - 1 - j)
        lane, row = q[0]["lane"], q[0]["row"]
        lane1 = lane[0:1, :]
        nw = nw_ref[...]
        y, zz, dyn = [y_ref[e] for e in ex], [z_ref[e] for e in ex], [dyn_ref[e] for e in ex]
        sz = [_sigmoid(zz[e]) for e in ex]
        sil = [zz[e] * sz[e] for e in ex]
        yg = [y[e] * sil[e] for e in ex]
        rstd = [lax.rsqrt(jnp.mean(yg[e] * yg[e], axis=-1, keepdims=True) + EPS) for e in ex]
        gn = [dyn[e] * nw for e in ex]
        dyg = [rstd[e] * (gn[e] - yg[e] * (rstd[e] * rstd[e]) * jnp.mean(gn[e] * yg[e], axis=-1, keepdims=True))
               for e in ex]
        dy = [dyg[e] * sil[e] for e in ex]
        xs = [xs_ref[e] for e in ex]
        for e in ex:
            dnw_ref[e] += jnp.sum(dyn[e] * yg[e] * rstd[e], axis=0, keepdims=True)
            dz_ref[e] = (dyg[e] * y[e] * (sz[e] * (1.0 + zz[e] * (1.0 - sz[e])))).astype(BF16)
        dskip_cols = [jnp.sum(dy[e] * xs[e], axis=0, keepdims=True) for e in ex]

        bm, cm = [b_ref[e] for e in ex], [c_ref[e] for e in ex]
        cb = [_dot_nt(cm[e], bm[e]) for e in ex]
        zero = jnp.zeros((CHUNK, LANES), F32)
        full = lambda v: jnp.broadcast_to(v, (CHUNK, LANES))
        low = lane < HEAD_DIM
        half = [low if r % 2 == 0 else ~low for r in heads]
        sl = lambda v, r: v[:, (r // 2) * LANES:(r // 2 + 1) * LANES]
        pair = lambda r: pl.ds((r // 2) * LANES, LANES)
        col = {(e, r): full(q[e]["acs"][:, r:r + 1]) for e, r in units}
        dt_col = {(e, r): full(q[e]["dt"][:, r:r + 1]) for e, r in units}
        aend = {(e, r): q[e]["aend"][:, r:r + 1] for e, r in units}
        dt_row = {(e, r): q[e]["dt_t"][r:r + 1, :] for e, r in units}
        decay = {(e, r): jnp.exp(jnp.where(q[e]["causal"], col[e, r] - q[e]["acs_t"][r:r + 1, :], -jnp.inf))
                 for e, r in units}
        ea = {u: jnp.exp(col[u]) for u in units}
        dte = {u: jnp.exp(aend[u] - col[u]) for u in units}
        ed = {u: jnp.exp(aend[u]) for u in units}
        k = {u: dte[u] * dt_col[u] for u in units}
        mp = {(e, r): cb[e] * decay[e, r] * dt_row[e, r] for e, r in units}
        xp = {(e, r): sl(xs[e], r) for e, r in units}
        dym = {(e, r): jnp.where(half[r], sl(dy[e], r), 0.0) for e, r in units}
        s_old = {(e, r): sp_ref[e, 0, :, pair(r)] for e, r in units}
        ds_old = {(e, r): ds_ref[e, :, pair(r)] for e, r in units}
        dsm = {(e, r): jnp.where(half[r], ds_old[e, r], 0.0) for e, r in units}
        gmat = {u: _dot_nt(dym[u], xp[u]) for u in units}
        t1 = {u: _dot_nt(dym[u], s_old[u]) for u in units}
        dbs = {u: _dot_nt(xp[u], dsm[u]) for u in units}
        dx = {(e, r): _dot_tn(mp[e, r], dym[e, r]) + _dot(bm[e] * k[e, r], dsm[e, r]) for e, r in units}
        ds = {(e, r): _dot_tn(cm[e] * ea[e, r], dym[e, r]) for e, r in units}
        gd = {u: gmat[u] * decay[u] for u in units}
        w0 = {(e, r): gd[e, r] * cb[e] for e, r in units}
        cs0 = {u: jnp.sum(w0[u], axis=0, keepdims=True) for u in units}
        rs = {u: jnp.sum(w0[u] * dt_row[u], axis=1, keepdims=True) for u in units}
        qv = {(e, r): jnp.sum(cm[e] * t1[e, r], axis=1, keepdims=True) for e, r in units}
        dk = {(e, r): jnp.sum(bm[e] * dbs[e, r], axis=1, keepdims=True) for e, r in units}
        ddte = {u: dk[u] * dt_col[u] for u in units}
        d_aend = {u: _sum_all(dsm[u] * s_old[u]) * ed[u] + _sum_all(ddte[u][:, 0:1] * dte[u][:, 0:1]) for u in units}
        last_row = row == CHUNK - 1
        dacs_col = {u: rs[u] + qv[u] * ea[u] - ddte[u] * dte[u] + jnp.where(last_row, d_aend[u], 0.0) for u in units}
        triu = (lane >= row).astype(F32)
        for e in ex:
            dcb, dc_acc, db_acc = zero, zero, zero
            dacs, dacs_t, ddt, ddt_t = zero, zero, zero, zero
            dskip_row = jnp.zeros((1, LANES), F32)
            for r in heads:
                u = (e, r)
                dcb = dcb + gd[u] * dt_row[u]
                dc_acc = dc_acc + ea[u] * t1[u]
                db_acc = db_acc + k[u] * dbs[u]
                dacs = jnp.where(lane == r, dacs_col[u], dacs)
                ddt = jnp.where(lane == r, dk[u] * dte[u], ddt)
                dacs_t = jnp.where(row == r, -cs0[u] * dt_row[u], dacs_t)
                ddt_t = jnp.where(row == r, cs0[u], ddt_t)
                dsk = _sum_all(jnp.where(half[r][0:1, :], sl(dskip_cols[e], r), 0.0))
                dskip_row = dskip_row + jnp.where(lane1 == r, dsk, 0.0)
            for r in range(0, HPG, 2):
                dxs_ref[e, :, pair(r)] = (dx[e, r] + dx[e, r + 1] + sl(dy[e], r) * dsk_ref[:, pair(r)]).astype(BF16)
                ed_pair = jnp.where(lane1 < HEAD_DIM, ed[e, r], ed[e, r + 1])
                ds_ref[e, :, pair(r)] = ds[e, r] + ds[e, r + 1] + ed_pair * ds_old[e, r]
            dacs = dacs + dacs_t.T
            ddt = ddt + ddt_t.T
            dda = _dot_exact(triu, dacs)
            ddt = ddt + dda * a
            da = jnp.sum(dda * q[e]["dt"], axis=0, keepdims=True)
            draw = jnp.where(valid, ddt * _sigmoid(dtr_ref[e] + dtb_ref[0]), 0.0)
            ddt_ref[e] = draw.astype(BF16)
            dsm_ref[e, 0, 0:1, :] += dskip_row
            dsm_ref[e, 0, 1:2, :] += da * a
            dsm_ref[e, 0, 2:3, :] += jnp.sum(draw, axis=0, keepdims=True)
            dc_ref[e] = (dc_acc + _dot(dcb, bm[e])).astype(BF16)
            db_ref[e] = (db_acc + _dot_tn(dcb, cm[e])).astype(BF16)

    grp_out = pl.BlockSpec((bsz, CHUNK, D_STATE), lambda g, j: (0, nc - 1 - j, g))
    grid = (N_GROUPS, nc)
    ride = _Ride(rider, body, 15, 7, 1, grid)
    outs = pl.pallas_call(
        ride.body, name=name, grid=grid,
        in_specs=[sp["xs"], sp["bm"], sp["cm"], sp["lane_blk"], sp["lane_blk"], sp["lane_blk"], sp["tr"], sp["xs"],
                  sp["xs"], sp["state"], sp["xs"], sp["grp_const"], sp["grp_const"], sp["grp_vec"], sp["grp_vec"]]
        + ride.in_specs,
        out_specs=[sp["xs"], sp["xs"], grp_out, grp_out, sp["lane_blk"],
                   pl.BlockSpec((bsz, 1, GW), lambda g, j: (0, 0, g)),
                   pl.BlockSpec((bsz, 1, 8, LANES), lambda g, j: (0, g, 0, 0))] + ride.out_specs,
        out_shape=[jax.ShapeDtypeStruct((bsz, t, D_SSM), BF16), jax.ShapeDtypeStruct((bsz, t, D_SSM), BF16),
                   jax.ShapeDtypeStruct((bsz, t, N_GROUPS * D_STATE), BF16),
                   jax.ShapeDtypeStruct((bsz, t, N_GROUPS * D_STATE), BF16),
                   jax.ShapeDtypeStruct((bsz, t, D_DT), BF16), jax.ShapeDtypeStruct((bsz, 1, D_SSM), F32),
                   jax.ShapeDtypeStruct((bsz, N_GROUPS, 8, LANES), F32)] + ride.out_shape,
        scratch_shapes=[pltpu.VMEM((bsz, D_STATE, GW), F32)] + ride.scratch,
        compiler_params=_params(*ride.semantics(("parallel", "arbitrary"))),
    )(xc, xc, xc, dtr, dt, acs, tr, z, ypre, sprev, dyn, dtb, alog, dskip, normw, *ride.args)
    return outs[:7], outs[7:]


def _input_grad(dhn, h0, w, dres, seq, *, name):
    bsz, t, d = h0.shape
    nc = t // CHUNK

    def body(dy_ref, h_ref, w_ref, dres_ref, gx_ref, head_ref, dw_ref):
        j = pl.program_id(1)

        @pl.when((pl.program_id(0) == 0) & (j == 0))
        def _():
            dw_ref[...] = jnp.zeros_like(dw_ref)

        x, dyv = h_ref[0], dy_ref[0]
        r = lax.rsqrt(jnp.mean(x * x, axis=-1, keepdims=True) + EPS)
        g = dyv * w_ref[...]
        dx = r * (g - x * (r * r) * jnp.mean(g * x, axis=-1, keepdims=True)) + dres_ref[0]
        dw_ref[...] += jnp.sum(dyv * x * r, axis=0, keepdims=True)

        @pl.when(j == 0)
        def _():
            head_ref[0] = dx

        gx_ref[0] = dx

    row = pl.BlockSpec((1, CHUNK, d), lambda b, j: (b, j, 0))
    return pl.pallas_call(
        body, name=name, grid=(bsz, nc),
        in_specs=[row, row, pl.BlockSpec((1, d), lambda b, j: (0, 0)), row],
        out_specs=[pl.BlockSpec((1, CHUNK, d), lambda b, j: (b, jnp.maximum(j - 1, 0), 0)),
                   pl.BlockSpec((1, CHUNK, d), lambda b, j: (b, 0, 0)), pl.BlockSpec((1, d), lambda b, j: (0, 0))],
        out_shape=[jax.ShapeDtypeStruct((bsz, seq, d), F32), jax.ShapeDtypeStruct((bsz, CHUNK, d), F32),
                   jax.ShapeDtypeStruct((1, d), F32)],
        compiler_params=_params("arbitrary", "arbitrary"),
    )(dhn, h0, w, dres)


def _remote(src, dst, send_sem, recv_sem, dev):
    return pltpu.make_async_remote_copy(src_ref=src, dst_ref=dst, send_sem=send_sem, recv_sem=recv_sem,
                                        device_id=dev, device_id_type=MESH)


def _position():
    return lax.axis_index("x"), lax.axis_index("y"), lax.axis_index("c")


def _other_chips(pos):
    x, y, _ = pos
    return [(1 - x, y), (x, 1 - y), (1 - x, 1 - y)]


class _Gather:
    def __init__(self, arrs):
        n = len(arrs)
        self.args, self.n_in, self.n_out = list(arrs), n, n
        self.split = [a.ndim == 2 and a.shape[1] % (2 * LANES) == 0 for a in arrs]
        self.out_shape = [jax.ShapeDtypeStruct((4,) + a.shape, a.dtype) for a in arrs]
        self.scratch = [pltpu.SemaphoreType.DMA((3 * n,)), pltpu.SemaphoreType.DMA((3 * n,)),
                        pltpu.SemaphoreType.DMA((n,)), pltpu.SemaphoreType.DMA((3 * n,)),
                        pltpu.SemaphoreType.DMA((3 * n,))]

    def _copies(self, pos, ins, outs, sems):
        send_sems, recv_sems, loc_sems, pass_send_sems, pass_recv_sems = sems
        x, y, c = pos
        me, sibling = 2 * x + y, (x, y, 1 - c)
        local = [pltpu.make_async_copy(ins[i], outs[i].at[me], loc_sems.at[i]) for i in range(self.n_in)]
        sends, recvs, passes, pass_recvs = [], [], [], []
        for i in range(self.n_in):
            half = self.args[i].shape[1] // 2 if self.split[i] else None
            for k, (px, py) in enumerate(_other_chips(pos)):
                them = 2 * px + py
                sems_k = (send_sems.at[3 * i + k], recv_sems.at[3 * i + k], (px, py, c))
                if half is None:
                    sends.append(_remote(ins[i], outs[i].at[me], *sems_k))
                    recvs.append(_remote(ins[i], outs[i].at[them], *sems_k))
                    passes.append(None)
                    continue
                mine = pl.ds(pl.multiple_of(c * half, LANES), half)
                other = pl.ds(pl.multiple_of((1 - c) * half, LANES), half)
                sends.append(_remote(ins[i].at[:, mine], outs[i].at[me, :, mine], *sems_k))
                recvs.append(_remote(ins[i].at[:, mine], outs[i].at[them, :, mine], *sems_k))
                pass_k = (pass_send_sems.at[3 * i + k], pass_recv_sems.at[3 * i + k], sibling)
                passes.append(_remote(outs[i].at[them, :, mine], outs[i].at[them, :, mine], *pass_k))
                pass_recvs.append(_remote(outs[i].at[them, :, other], outs[i].at[them, :, other], *pass_k))
        return local, sends, recvs, passes, pass_recvs

    def start(self, pos, ins, outs, sems):
        local, sends = self._copies(pos, ins, outs, sems)[:2]
        for cp in local + sends:
            cp.start()

    def finish(self, pos, ins, outs, sems):
        local, sends, recvs, passes, pass_recvs = self._copies(pos, ins, outs, sems)
        for cp, onward in zip(recvs, passes):
            cp.wait_recv()
            if onward is not None:
                onward.start()
        for cp in pass_recvs:
            cp.wait_recv()
        for cp in sends + [p for p in passes if p is not None]:
            cp.wait_send()
        for cp in local:
            cp.wait()


class _Exchange:
    FLIPS = [(fx, fy, fc) for fx in (0, 1) for fy in (0, 1) for fc in (0, 1)][1:]

    def __init__(self, big, small=None):
        n = len(big)
        self.n_big, self.has_small = n, small is not None
        self.args = list(big) + ([small] if self.has_small else [])
        self.n_in = self.n_out = len(self.args)
        self.out_shape = [jax.ShapeDtypeStruct(a.shape, a.dtype) for a in big]
        self.scratch = [pltpu.SemaphoreType.DMA((max(3 * n, 1),)), pltpu.SemaphoreType.DMA((max(3 * n, 1),))]
        if self.has_small:
            self.out_shape.append(jax.ShapeDtypeStruct((8,) + small.shape, small.dtype))
            self.scratch += [pltpu.SemaphoreType.DMA((7,)), pltpu.SemaphoreType.DMA((7,)), pltpu.SemaphoreType.DMA((1,))]

    def _copies(self, pos, ins, outs, sems):
        x, y, c = pos
        me, me8 = 2 * x + y, 4 * x + 2 * y + c
        local, sends, recvs = [], [], []
        for i in range(self.n_big):
            for k, (px, py) in enumerate(_other_chips(pos)):
                sems_k = (sems[0].at[3 * i + k], sems[1].at[3 * i + k], (px, py, c))
                sends.append(_remote(ins[i].at[2 * px + py], outs[i].at[me], *sems_k))
                recvs.append(_remote(ins[i].at[me], outs[i].at[2 * px + py], *sems_k))
        if self.has_small:
            small, landed = ins[self.n_big], outs[self.n_big]
            local.append(pltpu.make_async_copy(small, landed.at[me8], sems[4].at[0]))
            for k, (fx, fy, fc) in enumerate(self.FLIPS):
                peer = (x ^ fx, y ^ fy, c ^ fc)
                sems_k = (sems[2].at[k], sems[3].at[k], peer)
                sends.append(_remote(small, landed.at[me8], *sems_k))
                recvs.append(_remote(small, landed.at[4 * peer[0] + 2 * peer[1] + peer[2]], *sems_k))
        return local, sends, recvs, [None] * len(recvs), []

    start = _Gather.start
    finish = _Gather.finish


class _Swap:
    def __init__(self, arrs):
        n = len(arrs)
        self.args, self.n_in, self.n_out = list(arrs), n, n
        self.out_shape = [jax.ShapeDtypeStruct(a.shape, a.dtype) for a in arrs]
        self.scratch = [pltpu.SemaphoreType.DMA((n,)), pltpu.SemaphoreType.DMA((n,))]

    def _copies(self, pos, ins, outs, sems):
        x, y, c = pos
        both = [_remote(ins[i], outs[i], sems[0].at[i], sems[1].at[i], (x, y, 1 - c)) for i in range(self.n_in)]
        return [], both, both, [None] * len(both), []

    start = _Gather.start
    finish = _Gather.finish


def _comm(rider, *, name):
    a, b = rider.n_in, rider.n_in + rider.n_out

    def body(*refs):
        pos = _position()
        rider.start(pos, refs[:a], refs[a:b], refs[b:])
        rider.finish(pos, refs[:a], refs[a:b], refs[b:])

    return pl.pallas_call(body, name=name, in_specs=[ANY] * rider.n_in, out_specs=[ANY] * rider.n_out,
                          out_shape=rider.out_shape, scratch_shapes=rider.scratch)(*rider.args)


class _Ride:
    def __init__(self, rider, body, n_in, n_out, n_scratch, grid):
        self.rider = rider
        self.args = rider.args if rider else []
        self.in_specs = [ANY] * rider.n_in if rider else []
        self.out_specs = [ANY] * rider.n_out if rider else []
        self.out_shape = rider.out_shape if rider else []
        self.scratch = rider.scratch if rider else []
        self.body = self._wrap(body, n_in, n_out, n_scratch, grid) if rider else body

    def semantics(self, sem):
        return ("arbitrary",) * len(sem) if self.rider else sem

    def _wrap(self, body, n_in, n_out, n_scratch, grid):
        rider = self.rider
        a = n_in
        b = a + rider.n_in
        c = b + n_out
        d = c + rider.n_out
        e = d + n_scratch

        def wrapped(*refs):
            pos = _position()
            ids = [pl.program_id(i) for i in range(len(grid))]
            first = functools.reduce(jnp.logical_and, [i == 0 for i in ids])
            last = functools.reduce(jnp.logical_and, [i == g - 1 for i, g in zip(ids, grid)])

            @pl.when(first)
            def _():
                rider.start(pos, refs[a:b], refs[c:d], refs[e:])

            body(*refs[:a], *refs[b:c], *refs[d:e])

            @pl.when(last)
            def _():
                rider.finish(pos, refs[a:b], refs[c:d], refs[e:])

        return wrapped


def _elementwise_tiles(r, c):
    if r % 8 == 0 and r * c > 65536:
        tm = _pick(r, (256, 128, 64, 16, 8))
        return (tm, c), r // tm, lambda i: (i, 0)
    if r % 8 and c % 256 == 0 and r * c > 65536:
        return (r, 256), c // 256, lambda i: (0, i)
    return (r, c), 1, lambda i: (0, 0)


def _chip_sum(own, landed, *, name):
    r, c = own.shape
    blk, steps, at = _elementwise_tiles(r, c)

    def body(own_ref, land_ref, o_ref):
        me = 2 * lax.axis_index("x") + lax.axis_index("y")
        acc = None
        for jchip in range(4):
            term = jnp.where(me == jchip, own_ref[...], land_ref[jchip].astype(F32))
            acc = term if acc is None else acc + term
        o_ref[...] = acc

    return pl.pallas_call(
        body, name=name, grid=(steps,),
        in_specs=[pl.BlockSpec(blk, at), pl.BlockSpec((4,) + blk, lambda i: (0,) + at(i))],
        out_specs=pl.BlockSpec(blk, at), out_shape=jax.ShapeDtypeStruct((r, c), F32),
        compiler_params=_params("parallel"),
    )(own, landed)


def _device_sum(parts, *, name):
    _, r, c = parts.shape

    def body(p_ref, o_ref):
        acc = p_ref[0]
        for d in range(1, 8):
            acc = acc + p_ref[d]
        o_ref[...] = acc

    return pl.pallas_call(body, name=name, out_shape=jax.ShapeDtypeStruct((r, c), F32))(parts)


def _adamw_math(w, g, m, v):
    m = ADAM_B1 * m + (1.0 - ADAM_B1) * g
    v = ADAM_B2 * v + (1.0 - ADAM_B2) * (g * g)
    m_hat = m / (1.0 - ADAM_B1 ** ADAM_STEP)
    v_hat = v / (1.0 - ADAM_B2 ** ADAM_STEP)
    return -ADAM_LR * (m_hat / (jnp.sqrt(v_hat) + ADAM_EPS) + ADAM_WD * w), m, v


def _adamw(w, g_parts, m, v, *, name):
    r, c = w.shape
    shape, steps, at = _elementwise_tiles(r, c)
    n_g = len(g_parts)

    def body(*refs):
        w_ref, m_ref, v_ref = refs[n_g:n_g + 3]
        g_ref, d_ref, nm_ref, nv_ref = refs[n_g + 3:]
        g = refs[0][...]
        for p in refs[1:n_g]:
            g = g + p[...]
        g_ref[...] = g
        d_ref[...], nm_ref[...], nv_ref[...] = _adamw_math(w_ref[...], g, m_ref[...], v_ref[...])

    blk = pl.BlockSpec(shape, at)
    return pl.pallas_call(
        body, name=name, grid=(steps,), in_specs=[blk] * (n_g + 3), out_specs=[blk] * 4,
        out_shape=[jax.ShapeDtypeStruct((r, c), F32)] * 4, compiler_params=_params("parallel"),
    )(*g_parts, w, m, v)


def _pad_heads(v):
    return jnp.pad(v.reshape(N_GROUPS, 1, HPG), ((0, 0), (0, 0), (0, LANES - HPG)))


def _unpad_heads(v):
    return v[:, :HPG].reshape(1, N_HEADS)


_SMALL_EARLY = [("pool_w", (512, 128)), ("pool_scale", (1, 512)), ("conv_w", (4, D_XBC)), ("conv_b", (1, D_XBC)),
                ("dt_bias", (1, N_HEADS)), ("a_log", (1, N_HEADS)), ("d_skip", (1, N_HEADS)), ("ssm_norm_w", (1, D_SSM)),
                ("norm_ffn_w", (1, 1024)), ("norm_f_w", (1, 1024))]
_SMALL_LATE = [("norm_mix_w", (1, 1024)), ("meta", (N_META, 1024)), ("loss", (1, 1))]


def _pack_small(grads, layout):
    rows = []
    for nm, shape in layout:
        flat = grads[nm].reshape(-1)
        rows.append(jnp.pad(flat, (0, (-flat.size) % LANES)).reshape(-1, LANES))
    packed = jnp.concatenate(rows, axis=0)
    return jnp.pad(packed, ((0, (-packed.shape[0]) % 8), (0, 0)))


def _unpack_small(packed, layout):
    out, r0 = {}, 0
    for nm, shape in layout:
        size = shape[0] * shape[1]
        nrow = -(-size // LANES)
        out[nm] = packed[r0:r0 + nrow].reshape(-1)[:size].reshape(shape)
        r0 += nrow
    return out


def kernel(x, meta, norm_mix_w, w_in, pool_w, pool_scale, conv_w, conv_b, dt_bias, a_log, d_skip, ssm_norm_w, w_out, norm_ffn_w, w_ff1, w_ff2, norm_f_w, loss_target, m_meta, m_norm_mix_w, m_w_in, m_pool_w, m_pool_scale, m_conv_w, m_conv_b, m_dt_bias, m_a_log, m_d_skip, m_ssm_norm_w, m_w_out, m_norm_ffn_w, m_w_ff1, m_w_ff2, m_norm_f_w, v_meta, v_norm_mix_w, v_w_in, v_pool_w, v_pool_scale, v_conv_w, v_conv_b, v_dt_bias, v_a_log, v_d_skip, v_ssm_norm_w, v_w_out, v_norm_ffn_w, v_w_ff1, v_w_ff2, v_norm_f_w):
    bsz, seq, d = x.shape
    t = seq + CHUNK
    n = bsz * t
    chip = 2 * lax.axis_index("x") + lax.axis_index("y")
    d_in = w_in.shape[2] * 4

    g_in, g_conv, g_meta = _comm(_Gather([w_in[0].T.astype(BF16), conv_w[0], meta]), name="gather_in")
    late_weights = _Gather([w_out[0].astype(BF16), w_ff1[0].astype(BF16), w_ff2[0].astype(BF16)])
    win = g_in.reshape(d_in, d)
    wu, wz = win[:D_POOL], win[D_POOL:D_POOL + D_SSM]
    wx = win[D_POOL + D_SSM:D_POOL + D_SSM + D_XBC]
    wdt = jnp.pad(win[D_POOL + D_SSM + D_XBC:].reshape(N_GROUPS, HPG, d),
                  ((0, 0), (0, LANES - HPG), (0, 0))).reshape(D_DT, d)
    convw = g_conv.transpose(1, 0, 2).reshape(CONV_W, D_XBC)
    meta_full = g_meta.transpose(1, 0, 2).reshape(N_META, d)
    dtb, alog = _pad_heads(dt_bias), _pad_heads(a_log)
    dskip = jnp.repeat(d_skip, HEAD_DIM, axis=1)
    poolw = pool_w[0]

    h0 = jnp.concatenate([jnp.zeros((bsz, PAD, d), F32), jnp.broadcast_to(meta_full[None], (bsz, N_META, d)), x], axis=1)
    h0f = h0.reshape(n, d)
    hn1 = _rms_fwd(h0f, norm_mix_w, name="norm_mix")
    u = _mm(hn1, wu, name="proj_u", nt=True)
    z = _mm(hn1, wz, name="proj_z", nt=True)
    xbc = _mm(hn1, wx, name="proj_xbc", nt=True)
    dtr = _mm(hn1, wdt, name="proj_dt", nt=True)
    ypool = _pool_fwd(u.reshape(bsz, t, D_POOL), poolw, pool_scale, name="pool_fwd")
    xbc3 = xbc.reshape(bsz, t, D_XBC)
    xc = _conv_fwd(xbc3, convw, conv_b, name="conv_fwd")
    z3, dtr3 = z.reshape(bsz, t, D_SSM), dtr.reshape(bsz, t, D_DT)
    dt3, acs3, tr3 = _ssd_prep(dtr3, dtb, alog, name="ssd_prep")
    (yn, ypre, sprev), (g_out, g_ff1, g_ff2) = _ssd_fwd(xc, dt3, acs3, tr3, z3, dskip, ssm_norm_w, name="ssd_fwd",
                                                        rider=late_weights)
    wo = g_out.reshape(D_POOL + D_SSM, d)
    wo_p, wo_s = wo[:D_POOL], wo[D_POOL:]
    w1 = g_ff1.transpose(1, 0, 2).reshape(d, D_FF)
    w2 = g_ff2.reshape(D_FF, d)
    ypool_f, yn_f = ypool.reshape(n, D_POOL), yn.reshape(n, D_SSM)
    add = lambda r, e: r + e
    h1 = _mm([ypool_f, yn_f], [wo_p, wo_s], name="out_proj", post=add, extras=(h0f,))
    hn2 = _rms_fwd(h1, norm_ffn_w, name="norm_ffn")
    act = _mm(hn2, w1, name="ff1", out_dtype=BF16)
    relu2 = lambda a: jnp.square(jnp.maximum(a, 0))
    h2 = _mm(act, w2, name="ff2", pre=relu2, post=add, extras=(h1,))
    dh2, dh2b, loss_acc, d_norm_f = _final_norm_loss(h2.reshape(bsz, t, d), loss_target, norm_f_w.reshape(1, d),
                                                     name="loss")

    dh2f, dh2bf = dh2.reshape(n, d), dh2b.reshape(n, d)
    dact = _mm(dh2bf, w2, name="ff2_bwd", nt=True, post=lambda r, a: r * (2.0 * jnp.maximum(a, 0).astype(F32)),
               extras=(act,), out_dtype=BF16)
    d_w2 = _mm_tn(act, dh2bf, name="ff2_dw", tk=2048, tn=1024, pre=relu2)
    d_w1 = _mm_tn(hn2, dact, name="ff1_dw", tk=1024, tn=2048)
    dh1, dh1b, d_norm_ffn = _mm_rms_bwd(dact, w1, h1, norm_ffn_w, dh2f, name="ff1_bwd", tk=D_FF)
    dypool = _mm(dh1b, wo_p, name="out_pool_bwd", nt=True)
    dyn = _mm(dh1b, wo_s, name="out_ssm_bwd", nt=True)
    d_wo_p = _mm_tn(ypool_f, dh1b, name="out_pool_dw", tk=512, tn=1024)
    d_wo_s = _mm_tn(yn_f, dh1b, name="out_ssm_dw", tk=1536, tn=1024)
    big_late = [jnp.concatenate([d_wo_p, d_wo_s], axis=0).reshape(4, (D_POOL + D_SSM) // 4, d),
                d_w1.reshape(d, 4, D_FF // 4).transpose(1, 0, 2), d_w2.reshape(4, D_FF // 4, d)]
    (dz, dxs, dbm, dcm, ddtr, d_nw, d_heads), landed_late = _ssd_bwd(
        xc, dtr3, dt3, acs3, tr3, z3, ypre, sprev, dyn.reshape(bsz, t, D_SSM), dtb, alog, dskip, ssm_norm_w, name="ssd_bwd",
        rider=_Exchange([b.astype(BF16) for b in big_late]))
    dxbc, d_convwb = _conv_bwd(xbc3, dxs, dbm, dcm, convw, conv_b, name="conv_bwd")
    du, d_poolw, d_poolsc = _pool_bwd(u.reshape(bsz, t, D_POOL), dypool.reshape(bsz, t, D_POOL), poolw, pool_scale,
                                      name="pool_bwd")
    duf, dzf, dxbcf, ddtrf = du.reshape(n, D_POOL), dz.reshape(n, D_SSM), dxbc.reshape(n, D_XBC), ddtr.reshape(n, D_DT)
    d_wu = _mm_tn(duf, hn1, name="proj_u_dw", tk=512, tn=1024)
    d_wz = _mm_tn(dzf, hn1, name="proj_z_dw", tk=1536, tn=1024)
    d_wx = _mm_tn(dxbcf, hn1, name="proj_xbc_dw", tk=1280, tn=1024)
    d_wdt = _mm_tn(ddtrf, hn1, name="proj_dt_dw", tk=512, tn=1024)
    d_win = jnp.concatenate([d_wu, d_wz, d_wx, d_wdt.reshape(N_GROUPS, LANES, d)[:, :HPG].reshape(N_HEADS, d)], axis=0)
    big_in = d_win.reshape(4, d_in // 4, d)
    heads = jnp.sum(d_heads, axis=0)
    small_early = _pack_small({
        "pool_w": d_poolw, "pool_scale": d_poolsc,
        "conv_w": jnp.sum(d_convwb[:, :CONV_W], axis=0), "conv_b": jnp.sum(d_convwb[:, CONV_W:CONV_W + 1], axis=0),
        "dt_bias": _unpad_heads(heads[:, 2]), "a_log": _unpad_heads(heads[:, 1]), "d_skip": _unpad_heads(heads[:, 0]),
        "ssm_norm_w": jnp.sum(d_nw, axis=0), "norm_ffn_w": d_norm_ffn, "norm_f_w": d_norm_f}, _SMALL_EARLY)
    dhn1, (landed_in, early_all) = _mm([duf, dzf, dxbcf, ddtrf], [wu, wz, wx, wdt], name="proj_bwd",
                                       rider=_Exchange([big_in.astype(BF16)], small_early))
    grad_x, d_head_rows, d_norm_mix = _input_grad(
        dhn1.reshape(bsz, t, d), h0, norm_mix_w, dh1.reshape(bsz, t, d), seq, name="input_grad")

    big = [big_in] + big_late
    landed = [landed_in] + list(landed_late)
    small_late = _pack_small({"norm_mix_w": d_norm_mix, "meta": jnp.sum(d_head_rows[:, PAD:], axis=0),
                              "loss": loss_acc[0:1, 0:1]}, _SMALL_LATE)
    (late_all,) = _comm(_Exchange([], small_late), name="exchange_small")
    own = [lax.dynamic_index_in_dim(b, chip, 0, keepdims=False) for b in big]
    mine = [_chip_sum(o, l, name=f"chip_sum_{i}") for i, (o, l) in enumerate(zip(own, landed))]
    theirs = _comm(_Swap(mine), name="swap_cores")
    gsmall = {**_unpack_small(_device_sum(early_all, name="device_sum_early"), _SMALL_EARLY),
              **_unpack_small(_device_sum(late_all, name="device_sum_late"), _SMALL_LATE)}
    gsmall["conv_w"] = lax.dynamic_slice_in_dim(gsmall["conv_w"], chip * (D_XBC // 4), D_XBC // 4, axis=1)
    gsmall["meta"] = lax.dynamic_slice_in_dim(gsmall["meta"], chip * (d // 4), d // 4, axis=1)
    loss = gsmall["loss"][0, 0]

    given = dict(meta=(meta, m_meta, v_meta), norm_mix_w=(norm_mix_w, m_norm_mix_w, v_norm_mix_w),
                 w_in=(w_in, m_w_in, v_w_in), pool_w=(pool_w, m_pool_w, v_pool_w),
                 pool_scale=(pool_scale, m_pool_scale, v_pool_scale), conv_w=(conv_w, m_conv_w, v_conv_w),
                 conv_b=(conv_b, m_conv_b, v_conv_b), dt_bias=(dt_bias, m_dt_bias, v_dt_bias),
                 a_log=(a_log, m_a_log, v_a_log), d_skip=(d_skip, m_d_skip, v_d_skip),
                 ssm_norm_w=(ssm_norm_w, m_ssm_norm_w, v_ssm_norm_w), w_out=(w_out, m_w_out, v_w_out),
                 norm_ffn_w=(norm_ffn_w, m_norm_ffn_w, v_norm_ffn_w), w_ff1=(w_ff1, m_w_ff1, v_w_ff1),
                 w_ff2=(w_ff2, m_w_ff2, v_w_ff2), norm_f_w=(norm_f_w, m_norm_f_w, v_norm_f_w))
    big_names = ["w_in", "w_out", "w_ff1", "w_ff2"]
    results = {}
    for nm, (w, m, v) in given.items():
        if nm in big_names:
            i = big_names.index(nm)
            parts, shape2 = (mine[i], theirs[i]), mine[i].shape
        else:
            parts, shape2 = (gsmall[nm],), gsmall[nm].shape
        if nm == "w_in":
            outs = _adamw(w[0].T, parts, m[0].T, v[0].T, name=f"adamw_{nm}")
            results[nm] = [o.T[None] for o in outs]
        else:
            outs = _adamw(w.reshape(shape2), parts, m.reshape(shape2), v.reshape(shape2), name=f"adamw_{nm}")
            results[nm] = [o.reshape(w.shape) for o in outs]
    order = list(given)
    return (loss, grad_x, *[results[nm][0] for nm in order], *[results[nm][1] for nm in order],
            *[results[nm][2] for nm in order], *[results[nm][3] for nm in order])
```

```python
import functools

import jax
import jax.numpy as jnp
from jax import lax
from jax.experimental import pallas as pl
from jax.experimental.pallas import tpu as pltpu

F32 = jnp.float32
BF16 = jnp.bfloat16
MESH = pl.DeviceIdType.MESH
ANY = pl.BlockSpec(memory_space=pl.ANY)

D_MODEL = 1024
N_META = 16
CHUNK = 128
PAD = CHUNK - N_META
POOL_WINDOWS = (2, 4, 8, 16)
D_POOL = 512
POOL_GROUP = 128
D_SSM = 1536
N_HEADS = 24
N_GROUPS = 4
HPG = 6
HEAD_DIM = 64
D_STATE = 128
GW = HPG * HEAD_DIM
D_XBC = D_SSM + 2 * N_GROUPS * D_STATE
D_DT = N_GROUPS * 128
D_FF = 4096
CONV_W = 4
EPS = 1e-5
LANES = 128
VMEM_LIMIT = 56 * 1024 * 1024

ADAM_LR, ADAM_B1, ADAM_B2, ADAM_EPS, ADAM_WD, ADAM_STEP = 0.001, 0.9, 0.999, 1e-08, 0.01, 10


def _params(*sem):
    return pltpu.CompilerParams(dimension_semantics=sem, vmem_limit_bytes=VMEM_LIMIT)


def _pick(n, cands):
    for c in cands:
        if n % c == 0:
            return c
    raise ValueError(f"no block size for {n}")


def _dot(a, b):
    return jnp.dot(a.astype(BF16), b.astype(BF16), preferred_element_type=F32)


def _dot_nt(a, b):
    return lax.dot_general(a.astype(BF16), b.astype(BF16), (((1,), (1,)), ((), ())), preferred_element_type=F32)


def _dot_tn(a, b):
    return lax.dot_general(a.astype(BF16), b.astype(BF16), (((0,), (0,)), ((), ())), preferred_element_type=F32)


def _dot_exact(mask, x):
    m = mask.astype(BF16)
    hi = x.astype(BF16)
    r1 = x - hi.astype(F32)
    mid = r1.astype(BF16)
    lo = (r1 - mid.astype(F32)).astype(BF16)
    dot = lambda t: jnp.dot(m, t, preferred_element_type=F32)
    return dot(hi) + dot(mid) + dot(lo)


def _sigmoid(x):
    return 1.0 / (1.0 + jnp.exp(-x))


def _softplus(x):
    return jnp.maximum(x, 0.0) + jnp.log1p(jnp.exp(-jnp.abs(x)))


def _sum_all(x):
    return jnp.sum(jnp.sum(x, axis=1, keepdims=True), axis=0, keepdims=True)


ROW_TILES = (1056, 768, 704, 512, 384, 256, 128)
TILE_BUDGET = 28 * 1024 * 1024


def _row_tile(n, bytes_per_row, fixed_bytes, budget=TILE_BUDGET):
    for tm in ROW_TILES:
        if n % tm == 0 and 2 * (tm * bytes_per_row + fixed_bytes) <= budget:
            return tm
    raise ValueError(f"no row tile for {n}")


WIDE_BUDGET = 38 * 1024 * 1024


def _mm(a, w, *, name, tn=512, nt=False, pre=None, post=None, extras=(), out_dtype=F32, rider=None):
    a_list = list(a) if isinstance(a, (list, tuple)) else [a]
    w_list = list(w) if isinstance(w, (list, tuple)) else [w]
    n_a, n_ex = len(a_list), len(extras)
    n = a_list[0].shape[0]
    shard = w_list[0].shape[2] if w_list[0].ndim == 3 else None
    assert shard is None or (not nt and n_a == 1 and shard % tn == 0)
    m = w_list[0].shape[0] * shard if shard else w_list[0].shape[0] if nt else w_list[0].shape[1]
    tn = min(tn, m)
    size = lambda dt: jnp.dtype(dt).itemsize
    per_row = (sum(x.shape[1] * size(x.dtype) for x in a_list) + m * size(out_dtype)
               + sum(m * size(e.dtype) for e in extras))
    tm = _row_tile(n, per_row, sum(x.size * size(x.dtype) for x in w_list) // 2, WIDE_BUDGET)

    def body(*refs):
        a_refs, w_refs, ex_refs, o_ref = refs[:n_a], refs[n_a:2 * n_a], refs[2 * n_a:2 * n_a + n_ex], refs[2 * n_a + n_ex]
        avs = [(a_ref[...] if pre is None else pre(a_ref[...])).astype(BF16) for a_ref in a_refs]
        for c0 in range(0, m, tn):
            r = None
            for av, w_ref in zip(avs, w_refs):
                if shard:
                    term = _dot(av, w_ref[c0 // shard, :, c0 % shard:c0 % shard + tn])
                else:
                    term = _dot_nt(av, w_ref[c0:c0 + tn, :]) if nt else _dot(av, w_ref[:, c0:c0 + tn])
                r = term if r is None else r + term
            if post is not None:
                r = post(r, *[e[:, c0:c0 + tn] for e in ex_refs])
            o_ref[:, c0:c0 + tn] = r.astype(out_dtype)

    a_specs = [pl.BlockSpec((tm, x.shape[1]), lambda i: (i, 0)) for x in a_list]
    w_specs = [pl.BlockSpec(x.shape, lambda i, nd=x.ndim: (0,) * nd, pipeline_mode=pl.Buffered(1)) for x in w_list]
    blk = pl.BlockSpec((tm, m), lambda i: (i, 0))
    grid = (n // tm,)
    ride = _Ride(rider, body, 2 * n_a + n_ex, 1, 0, grid)
    outs = pl.pallas_call(
        ride.body, name=name, grid=grid,
        in_specs=a_specs + w_specs + [blk] * n_ex + ride.in_specs,
        out_specs=[blk] + ride.out_specs, out_shape=[jax.ShapeDtypeStruct((n, m), out_dtype)] + ride.out_shape,
        scratch_shapes=ride.scratch, compiler_params=_params(*ride.semantics(("parallel",))),
    )(*a_list, *w_list, *extras, *ride.args)
    return (outs[0], outs[1:]) if rider else outs[0]


def _mm_tn(a, g, *, name, tk, tn, pre=None, column_shards=False, rider=None):
    n, k = a.shape
    m = g.shape[1]
    tk, tn = min(tk, k), min(tn, m)
    tm = _row_tile(n, tk * jnp.dtype(a.dtype).itemsize + tn * jnp.dtype(g.dtype).itemsize, tk * tn * 4)

    def body(a_ref, g_ref, o_ref):
        @pl.when(pl.program_id(2) == 0)
        def _():
            o_ref[...] = jnp.zeros_like(o_ref)

        av = a_ref[...]
        if pre is not None:
            av = pre(av)
        o_ref[...] += _dot_tn(av, g_ref[...])

    if column_shards:
        out_spec = pl.BlockSpec((None, tk, tn), lambda i, j, r: (j, i, 0))
        out_shape = jax.ShapeDtypeStruct((m // tn, k, tn), F32)
    else:
        out_spec = pl.BlockSpec((tk, tn), lambda i, j, r: (i, j))
        out_shape = jax.ShapeDtypeStruct((k, m), F32)
    grid = (k // tk, m // tn, n // tm)
    ride = _Ride(rider, body, 2, 1, 0, grid)
    outs = pl.pallas_call(
        ride.body, name=name, grid=grid,
        in_specs=[pl.BlockSpec((tm, tk), lambda i, j, r: (r, i)), pl.BlockSpec((tm, tn), lambda i, j, r: (r, j))]
        + ride.in_specs,
        out_specs=[out_spec] + ride.out_specs, out_shape=[out_shape] + ride.out_shape, scratch_shapes=ride.scratch,
        compiler_params=_params(*ride.semantics(("parallel", "parallel", "arbitrary"))),
    )(a, g, *ride.args)
    return (outs[0], outs[1:]) if rider else outs[0]


def _mm_rms_bwd(a, w, h, w_norm, dres, *, name):
    n, k = a.shape
    d = h.shape[1]
    slabs, _, ks = w.shape
    tm = _row_tile(n, k * jnp.dtype(a.dtype).itemsize + d * (4 + 4 + 4 + 2), d * k, WIDE_BUDGET)

    def body(a_ref, w_ref, h_ref, wn_ref, dres_ref, dx_ref, dxb_ref, dw_ref):
        @pl.when(pl.program_id(0) == 0)
        def _():
            dw_ref[...] = jnp.zeros_like(dw_ref)

        dyv = None
        for s in range(slabs):
            part = _dot_nt(a_ref[:, s * ks:(s + 1) * ks], w_ref[s])
            dyv = part if dyv is None else dyv + part
        x = h_ref[...]
        r = lax.rsqrt(jnp.mean(x * x, axis=-1, keepdims=True) + EPS)
        g = dyv * wn_ref[...]
        dx = r * (g - x * (r * r) * jnp.mean(g * x, axis=-1, keepdims=True)) + dres_ref[...]
        dx_ref[...] = dx
        dxb_ref[...] = dx.astype(BF16)
        dw_ref[...] += jnp.sum(dyv * x * r, axis=0, keepdims=True)

    row = pl.BlockSpec((tm, d), lambda i: (i, 0))
    vec = pl.BlockSpec((1, d), lambda i: (0, 0))
    return pl.pallas_call(
        body, name=name, grid=(n // tm,),
        in_specs=[pl.BlockSpec((tm, k), lambda i: (i, 0)),
                  pl.BlockSpec(w.shape, lambda i: (0, 0, 0), pipeline_mode=pl.Buffered(1)), row, vec, row],
        out_specs=[row, row, vec],
        out_shape=[jax.ShapeDtypeStruct((n, d), F32), jax.ShapeDtypeStruct((n, d), BF16), jax.ShapeDtypeStruct((1, d), F32)],
        compiler_params=_params("arbitrary"),
    )(a, w, h, w_norm, dres)


def _rms_fwd(h, w, *, name):
    n, d = h.shape
    tm = _pick(n, (768, 512, 256, 128))

    def body(h_ref, w_ref, o_ref):
        x = h_ref[...]
        r = lax.rsqrt(jnp.mean(x * x, axis=-1, keepdims=True) + EPS)
        o_ref[...] = (x * r * w_ref[...]).astype(BF16)

    return pl.pallas_call(
        body, name=name, grid=(n // tm,),
        in_specs=[pl.BlockSpec((tm, d), lambda i: (i, 0)), pl.BlockSpec((1, d), lambda i: (0, 0))],
        out_specs=pl.BlockSpec((tm, d), lambda i: (i, 0)), out_shape=jax.ShapeDtypeStruct((n, d), BF16),
        compiler_params=_params("parallel"),
    )(h, w)


def _final_norm_loss(h2, target, w, *, name):
    bsz, t, d = h2.shape
    nc = t // CHUNK

    def body(h_ref, t_ref, w_ref, dh_ref, dhb_ref, loss_ref, dw_ref):
        j = pl.program_id(1)

        @pl.when((pl.program_id(0) == 0) & (j == 0))
        def _():
            loss_ref[...] = jnp.zeros_like(loss_ref)
            dw_ref[...] = jnp.zeros_like(dw_ref)

        x, wv = h_ref[0], w_ref[...]
        r = lax.rsqrt(jnp.mean(x * x, axis=-1, keepdims=True) + EPS)
        diff = jnp.where(j > 0, x * r * wv - t_ref[0], 0.0)
        loss_ref[...] += _sum_all(diff * diff) * (0.5 / d)
        dy = diff * (1.0 / d)
        g = dy * wv
        dh = r * (g - x * (r * r) * jnp.mean(g * x, axis=-1, keepdims=True))
        dh_ref[0] = dh
        dhb_ref[0] = dh.astype(BF16)
        dw_ref[...] += jnp.sum(dy * x * r, axis=0, keepdims=True)

    row = pl.BlockSpec((1, CHUNK, d), lambda b, j: (b, j, 0))
    return pl.pallas_call(
        body, name=name, grid=(bsz, nc),
        in_specs=[row, pl.BlockSpec((1, CHUNK, d), lambda b, j: (b, jnp.maximum(j - 1, 0), 0)),
                  pl.BlockSpec((1, d), lambda b, j: (0, 0))],
        out_specs=[row, row, pl.BlockSpec((8, LANES), lambda b, j: (0, 0)), pl.BlockSpec((1, d), lambda b, j: (0, 0))],
        out_shape=[jax.ShapeDtypeStruct((bsz, t, d), F32), jax.ShapeDtypeStruct((bsz, t, d), BF16),
                   jax.ShapeDtypeStruct((8, LANES), F32), jax.ShapeDtypeStruct((1, d), F32)],
        compiler_params=_params("arbitrary", "arbitrary"),
    )(h2, target, w)


def _pool_masks(j, transposed):
    r = lax.broadcasted_iota(jnp.int32, (CHUNK, 2 * CHUNK), 0)
    c = lax.broadcasted_iota(jnp.int32, (CHUNK, 2 * CHUNK), 1)
    masks = []
    for w in POOL_WINDOWS:
        if transposed:
            m = (c >= r) & (c < r + w)
        else:
            s = c - CHUNK
            m = (s <= r) & (s > r - w) & (s + j * CHUNK >= 0)
        masks.append(m.astype(F32))
    return masks


def _pool_count(t_global, w):
    return jnp.clip(t_global - PAD + 1, 1, w).astype(F32)


def _pool_fwd(u, pool_w, pool_scale, *, name):
    bsz, t, _ = u.shape
    nc = t // CHUNK

    def body(prev_ref, cur_ref, pw_ref, sc_ref, o_ref):
        j = pl.program_id(0)
        masks = _pool_masks(j, False)
        tg = j * CHUNK + lax.broadcasted_iota(jnp.int32, (CHUNK, 1), 0)
        count = [_pool_count(tg, w) for w in POOL_WINDOWS]
        units = [(e, gi) for e in range(bsz) for gi in range(len(POOL_WINDOWS))]
        sl = lambda gi: pl.ds(gi * POOL_GROUP, POOL_GROUP)
        cur = {(e, gi): cur_ref[e, :, sl(gi)] for e, gi in units}
        both = {(e, gi): jnp.concatenate([prev_ref[e, :, sl(gi)], cur[e, gi]], axis=0) for e, gi in units}
        win = {(e, gi): _dot_exact(masks[gi], both[e, gi]) for e, gi in units}
        pooled = {(e, gi): win[e, gi] / count[gi] - cur[e, gi] for e, gi in units}
        mixed = {(e, gi): _dot(pooled[e, gi], pw_ref[gi]) for e, gi in units}
        for e, gi in units:
            o_ref[e, :, sl(gi)] = (mixed[e, gi] * sc_ref[:, sl(gi)]).astype(BF16)

    blk = lambda f: pl.BlockSpec((bsz, CHUNK, D_POOL), f)
    return pl.pallas_call(
        body, name=name, grid=(nc,),
        in_specs=[blk(lambda j: (0, jnp.maximum(j - 1, 0), 0)), blk(lambda j: (0, j, 0)),
                  pl.BlockSpec((4, POOL_GROUP, POOL_GROUP), lambda j: (0, 0, 0)),
                  pl.BlockSpec((1, D_POOL), lambda j: (0, 0))],
        out_specs=blk(lambda j: (0, j, 0)), out_shape=jax.ShapeDtypeStruct(u.shape, BF16),
        compiler_params=_params("parallel"),
    )(u, u, pool_w, pool_scale)


def _pool_bwd(u, dyp, pool_w, pool_scale, *, name):
    bsz, t, _ = u.shape
    nc = t // CHUNK

    def body(prev_ref, cur_ref, dy_ref, dyn_ref, pw_ref, sc_ref, du_ref, dpw_ref, dsc_ref):
        j = pl.program_id(0)

        @pl.when(j == 0)
        def _():
            dpw_ref[...] = jnp.zeros_like(dpw_ref)
            dsc_ref[...] = jnp.zeros_like(dsc_ref)

        fwd = _pool_masks(j, False)
        bwd = _pool_masks(j, True)
        tg = j * CHUNK + lax.broadcasted_iota(jnp.int32, (CHUNK, 1), 0)
        count = [_pool_count(tg, w) for w in POOL_WINDOWS]
        count_next = [_pool_count(tg + CHUNK, w) for w in POOL_WINDOWS]
        has_next = j < nc - 1
        groups = range(len(POOL_WINDOWS))
        units = [(e, gi) for e in range(bsz) for gi in groups]
        sl = lambda gi: pl.ds(gi * POOL_GROUP, POOL_GROUP)
        cur = {(e, gi): cur_ref[e, :, sl(gi)] for e, gi in units}
        both = {(e, gi): jnp.concatenate([prev_ref[e, :, sl(gi)], cur[e, gi]], axis=0) for e, gi in units}
        win = {(e, gi): _dot_exact(fwd[gi], both[e, gi]) for e, gi in units}
        pooled = {(e, gi): win[e, gi] / count[gi] - cur[e, gi] for e, gi in units}
        dy = {(e, gi): dy_ref[e, :, sl(gi)] for e, gi in units}
        mixed = {(e, gi): _dot(pooled[e, gi], pw_ref[gi]) for e, gi in units}
        dm = {(e, gi): dy[e, gi] * sc_ref[:, sl(gi)] for e, gi in units}
        dm_next = {(e, gi): jnp.where(has_next, dyn_ref[e, :, sl(gi)], 0.0) * sc_ref[:, sl(gi)] for e, gi in units}
        dpw = {(e, gi): _dot_tn(pooled[e, gi], dm[e, gi]) for e, gi in units}
        dpooled = {(e, gi): _dot_nt(dm[e, gi], pw_ref[gi]) for e, gi in units}
        dpooled_next = {(e, gi): _dot_nt(dm_next[e, gi], pw_ref[gi]) for e, gi in units}
        spread = {(e, gi): jnp.concatenate([dpooled[e, gi] / count[gi], dpooled_next[e, gi] / count_next[gi]], axis=0)
                  for e, gi in units}
        back = {(e, gi): _dot_exact(bwd[gi], spread[e, gi]) for e, gi in units}
        for e, gi in units:
            du_ref[e, :, sl(gi)] = (back[e, gi] - dpooled[e, gi]).astype(BF16)
        for gi in groups:
            dsc, dw = None, None
            for e in range(bsz):
                term = jnp.sum(dy[e, gi] * mixed[e, gi], axis=0, keepdims=True)
                dsc = term if dsc is None else dsc + term
                dw = dpw[e, gi] if dw is None else dw + dpw[e, gi]
            dsc_ref[:, sl(gi)] += dsc
            dpw_ref[gi] += dw

    blk = lambda f: pl.BlockSpec((bsz, CHUNK, D_POOL), f)
    return pl.pallas_call(
        body, name=name, grid=(nc,),
        in_specs=[blk(lambda j: (0, jnp.maximum(j - 1, 0), 0)), blk(lambda j: (0, j, 0)),
                  blk(lambda j: (0, j, 0)), blk(lambda j: (0, jnp.minimum(j + 1, nc - 1), 0)),
                  pl.BlockSpec((4, POOL_GROUP, POOL_GROUP), lambda j: (0, 0, 0)),
                  pl.BlockSpec((1, D_POOL), lambda j: (0, 0))],
        out_specs=[blk(lambda j: (0, j, 0)), pl.BlockSpec((4, POOL_GROUP, POOL_GROUP), lambda j: (0, 0, 0)),
                   pl.BlockSpec((1, D_POOL), lambda j: (0, 0))],
        out_shape=[jax.ShapeDtypeStruct(u.shape, BF16), jax.ShapeDtypeStruct((4, POOL_GROUP, POOL_GROUP), F32),
                   jax.ShapeDtypeStruct((1, D_POOL), F32)],
        compiler_params=_params("arbitrary"),
    )(u, u, dyp, dyp, pool_w, pool_scale)


CONV_SLAB = 512


def _conv_taps(tail, cur, keep_tail):
    ext = jnp.concatenate([jnp.where(keep_tail, tail, 0.0), cur], axis=0)
    return [(pltpu.roll(ext, CONV_W - 1 - k, 0) if k < CONV_W - 1 else ext)[8:] for k in range(CONV_W)]


def _conv_pre(taps, w_ref, b_ref, sl):
    acc = b_ref[:, sl]
    for k in range(CONV_W):
        acc = acc + w_ref[k:k + 1, sl] * taps[k]
    return acc


def _conv_fwd(xbc, conv_w, conv_b, *, name):
    bsz, t, c = xbc.shape
    nc = t // CHUNK

    def body(tail_ref, cur_ref, w_ref, b_ref, o_ref):
        keep = pl.program_id(1) > 0
        for c0 in range(0, c, CONV_SLAB):
            sl = pl.ds(c0, CONV_SLAB)
            pre = _conv_pre(_conv_taps(tail_ref[0, :, sl], cur_ref[0, :, sl], keep), w_ref, b_ref, sl)
            o_ref[0, :, sl] = (pre * _sigmoid(pre)).astype(BF16)

    return pl.pallas_call(
        body, name=name, grid=(bsz, nc),
        in_specs=[pl.BlockSpec((1, 8, c), lambda b, j: (b, jnp.maximum(j * (CHUNK // 8) - 1, 0), 0)),
                  pl.BlockSpec((1, CHUNK, c), lambda b, j: (b, j, 0)),
                  pl.BlockSpec((CONV_W, c), lambda b, j: (0, 0)), pl.BlockSpec((1, c), lambda b, j: (0, 0))],
        out_specs=pl.BlockSpec((1, CHUNK, c), lambda b, j: (b, j, 0)), out_shape=jax.ShapeDtypeStruct(xbc.shape, BF16),
        compiler_params=_params("parallel", "parallel"),
    )(xbc, xbc, conv_w, conv_b)


def _conv_bwd(xbc, dxs, db, dc, conv_w, conv_b, *, name):
    bsz, t, c = xbc.shape
    nc = t // CHUNK
    halo = 16
    rows = CHUNK + halo

    def body(tail_ref, cur_ref, head_ref, dxs_ref, db_ref, dc_ref, dxs_head, db_head, dc_head, w_ref, b_ref,
             dx_ref, dwb_ref):
        j = pl.program_id(1)

        @pl.when(j == 0)
        def _():
            dwb_ref[...] = jnp.zeros_like(dwb_ref)

        has_prev, has_next = j > 0, j < nc - 1
        for c0 in range(0, c, CONV_SLAB):
            sl = pl.ds(c0, CONV_SLAB)
            if c0 < D_SSM:
                dxc, dxc_next = dxs_ref[0, :, sl], dxs_head[0, :, sl]
            elif c0 < D_SSM + D_POOL:
                dxc, dxc_next = db_ref[0], db_head[0]
            else:
                dxc, dxc_next = dc_ref[0], dc_head[0]
            dxc = jnp.concatenate([dxc.astype(F32), jnp.where(has_next, dxc_next.astype(F32), 0.0)], axis=0)
            ext = jnp.concatenate([jnp.where(has_prev, tail_ref[0, :, sl], 0.0), cur_ref[0, :, sl],
                                   jnp.where(has_next, head_ref[0, :, sl], 0.0)], axis=0)
            taps = [(pltpu.roll(ext, CONV_W - 1 - k, 0) if k < CONV_W - 1 else ext)[8:] for k in range(CONV_W)]
            pre = _conv_pre(taps, w_ref, b_ref, sl)
            s = _sigmoid(pre)
            dpre = dxc * (s * (1.0 + pre * (1.0 - s)))
            acc = w_ref[CONV_W - 1:CONV_W, sl] * dpre[:CHUNK]
            for k in range(CONV_W - 1):
                up = CONV_W - 1 - k
                acc = acc + w_ref[k:k + 1, sl] * pltpu.roll(dpre, rows - up, 0)[:CHUNK]
            dx_ref[0, :, sl] = acc.astype(BF16)
            for k in range(CONV_W):
                dwb_ref[0, k:k + 1, sl] += jnp.sum(dpre[:CHUNK] * taps[k][:CHUNK], axis=0, keepdims=True)
            dwb_ref[0, CONV_W:CONV_W + 1, sl] += jnp.sum(dpre[:CHUNK], axis=0, keepdims=True)

    assert CONV_SLAB == D_POOL and D_SSM % CONV_SLAB == 0
    row = lambda width: pl.BlockSpec((1, CHUNK, width), lambda b, j: (b, j, 0))
    nxt = lambda width: pl.BlockSpec(
        (1, halo, width), lambda b, j: (b, jnp.minimum((j + 1) * (CHUNK // halo), t // halo - 1), 0))
    return pl.pallas_call(
        body, name=name, grid=(bsz, nc),
        in_specs=[pl.BlockSpec((1, 8, c), lambda b, j: (b, jnp.maximum(j * (CHUNK // 8) - 1, 0), 0)), row(c), nxt(c),
                  row(D_SSM), row(D_POOL), row(D_POOL), nxt(D_SSM), nxt(D_POOL), nxt(D_POOL),
                  pl.BlockSpec((CONV_W, c), lambda b, j: (0, 0)), pl.BlockSpec((1, c), lambda b, j: (0, 0))],
        out_specs=[row(c), pl.BlockSpec((1, 8, c), lambda b, j: (b, 0, 0))],
        out_shape=[jax.ShapeDtypeStruct(xbc.shape, BF16), jax.ShapeDtypeStruct((bsz, 8, c), F32)],
        compiler_params=_params("parallel", "arbitrary"),
    )(xbc, xbc, xbc, dxs, db, dc, dxs, db, dc, conv_w, conv_b)


def _dt_valid(j):
    lane = lax.broadcasted_iota(jnp.int32, (CHUNK, LANES), 1)
    row = lax.broadcasted_iota(jnp.int32, (CHUNK, LANES), 0)
    return (lane < HPG) & ((j > 0) | (row >= PAD))


def _ssd_prep(dtr, dtb, alog, *, name):
    bsz, t, _ = dtr.shape
    nc = t // CHUNK

    def body(dtr_ref, dtb_ref, alog_ref, dt_ref, acs_ref, tr_ref):
        j = pl.program_id(0)
        valid = _dt_valid(j)
        row = lax.broadcasted_iota(jnp.int32, (CHUNK, LANES), 0)
        lane = lax.broadcasted_iota(jnp.int32, (CHUNK, LANES), 1)
        tril = (row >= lane).astype(F32)
        units = [(e, g) for e in range(bsz) for g in range(N_GROUPS)]
        sl = lambda g: pl.ds(g * LANES, LANES)
        dt = {(e, g): jnp.where(valid, _softplus(dtr_ref[e, :, sl(g)] + dtb_ref[g]), 0.0) for e, g in units}
        acs = {(e, g): _dot_exact(tril, dt[e, g] * -jnp.exp(alog_ref[g])) for e, g in units}
        for e, g in units:
            dt_ref[e, :, sl(g)] = dt[e, g]
            acs_ref[e, :, sl(g)] = acs[e, g]
            tr_ref[e, 0, g, 0:8, :] = dt[e, g].T[0:8]
            tr_ref[e, 0, g, 8:16, :] = acs[e, g].T[0:8]

    blk = pl.BlockSpec((bsz, CHUNK, D_DT), lambda j: (0, j, 0))
    const = pl.BlockSpec((N_GROUPS, 1, LANES), lambda j: (0, 0, 0))
    return pl.pallas_call(
        body, name=name, grid=(nc,), in_specs=[blk, const, const],
        out_specs=[blk, blk, pl.BlockSpec((bsz, 1, N_GROUPS, 16, LANES), lambda j: (0, j, 0, 0, 0))],
        out_shape=[jax.ShapeDtypeStruct(dtr.shape, F32), jax.ShapeDtypeStruct(dtr.shape, F32),
                   jax.ShapeDtypeStruct((bsz, nc, N_GROUPS, 16, LANES), F32)],
        compiler_params=_params("parallel"),
    )(dtr, dtb, alog)


def _ssd_decay(dt, acs, tr):
    lane = lax.broadcasted_iota(jnp.int32, (CHUNK, LANES), 1)
    row = lax.broadcasted_iota(jnp.int32, (CHUNK, LANES), 0)
    return dict(lane=lane, row=row, dt=dt, causal=row >= lane, acs=acs, acs_t=tr[8:16], dt_t=tr[0:8],
                aend=acs[CHUNK - 1:CHUNK, :])


def _ssd_specs(bsz, nc, rev):
    ch = (lambda j: nc - 1 - j) if rev else (lambda j: j)
    return dict(
        xs=pl.BlockSpec((bsz, CHUNK, GW), lambda g, j: (0, ch(j), g)),
        bm=pl.BlockSpec((bsz, CHUNK, D_STATE), lambda g, j: (0, ch(j), D_SSM // D_STATE + g)),
        cm=pl.BlockSpec((bsz, CHUNK, D_STATE), lambda g, j: (0, ch(j), D_SSM // D_STATE + N_GROUPS + g)),
        lane_blk=pl.BlockSpec((bsz, CHUNK, LANES), lambda g, j: (0, ch(j), g)),
        grp_const=pl.BlockSpec((1, 1, LANES), lambda g, j: (g, 0, 0)),
        grp_vec=pl.BlockSpec((1, GW), lambda g, j: (0, g)),
        state=pl.BlockSpec((bsz, 1, D_STATE, GW), lambda g, j: (0, ch(j), 0, g)),
        tr=pl.BlockSpec((bsz, 1, 1, 16, LANES), lambda g, j: (0, ch(j), g, 0, 0)),
    )


def _ssd_fwd(xc, dt, acs, tr, z, dskip, normw, *, name, rider=None):
    bsz, t, _ = xc.shape
    nc = t // CHUNK
    sp = _ssd_specs(bsz, nc, False)

    def body(xs_ref, b_ref, c_ref, dt_ref, acs_ref, tr_ref, z_ref, dsk_ref, nw_ref, yn_ref, y_ref, sp_ref, s_ref):
        j = pl.program_id(1)

        @pl.when(j == 0)
        def _():
            s_ref[...] = jnp.zeros_like(s_ref)

        ex = range(bsz)
        units = [(e, r) for e in ex for r in range(HPG)]
        full = lambda v: jnp.broadcast_to(v, (CHUNK, LANES))
        pair = lambda r: pl.ds((r // 2) * LANES, LANES)
        q = [_ssd_decay(dt_ref[e], acs_ref[e], tr_ref[e, 0, 0]) for e in ex]
        for e in ex:
            sp_ref[e, 0] = s_ref[e]
        bm, cm = [b_ref[e] for e in ex], [c_ref[e] for e in ex]
        cb = [_dot_nt(cm[e], bm[e]) for e in ex]
        low = q[0]["lane"] < HEAD_DIM
        col = {(e, r): full(q[e]["acs"][:, r:r + 1]) for e, r in units}
        aend = {(e, r): q[e]["aend"][:, r:r + 1] for e, r in units}
        decay = {(e, r): jnp.exp(jnp.where(q[e]["causal"], col[e, r] - q[e]["acs_t"][r:r + 1, :], -jnp.inf))
                 for e, r in units}
        mp = {(e, r): cb[e] * decay[e, r] * q[e]["dt_t"][r:r + 1, :] for e, r in units}
        ce = {(e, r): cm[e] * jnp.exp(col[e, r]) for e, r in units}
        bk = {(e, r): bm[e] * (jnp.exp(aend[e, r] - col[e, r]) * full(q[e]["dt"][:, r:r + 1])) for e, r in units}
        xp = {(e, r): xs_ref[e, :, pair(r)] for e, r in units}
        s_old = {(e, r): s_ref[e, :, pair(r)] for e, r in units}
        y_h = {u: _dot(mp[u], xp[u]) + _dot(ce[u], s_old[u]) for u in units}
        s_h = {u: jnp.exp(aend[u]) * s_old[u] + _dot_tn(bk[u], xp[u]) for u in units}
        for e in ex:
            for r in range(0, HPG, 2):
                y_ref[e, :, pair(r)] = jnp.where(low, y_h[e, r], y_h[e, r + 1])
                s_ref[e, :, pair(r)] = jnp.where(low, s_h[e, r], s_h[e, r + 1])
        y = [y_ref[e] + dsk_ref[...] * xs_ref[e] for e in ex]
        zz = [z_ref[e] for e in ex]
        yg = [y[e] * (zz[e] * _sigmoid(zz[e])) for e in ex]
        rstd = [lax.rsqrt(jnp.mean(yg[e] * yg[e], axis=-1, keepdims=True) + EPS) for e in ex]
        for e in ex:
            y_ref[e] = y[e]
            yn_ref[e] = (yg[e] * rstd[e] * nw_ref[...]).astype(BF16)

    grid = (N_GROUPS, nc)
    ride = _Ride(rider, body, 9, 3, 1, grid)
    outs = pl.pallas_call(
        ride.body, name=name, grid=grid,
        in_specs=[sp["xs"], sp["bm"], sp["cm"], sp["lane_blk"], sp["lane_blk"], sp["tr"], sp["xs"],
                  sp["grp_vec"], sp["grp_vec"]] + ride.in_specs,
        out_specs=[sp["xs"], sp["xs"], sp["state"]] + ride.out_specs,
        out_shape=[jax.ShapeDtypeStruct((bsz, t, D_SSM), BF16), jax.ShapeDtypeStruct((bsz, t, D_SSM), F32),
                   jax.ShapeDtypeStruct((bsz, nc, D_STATE, D_SSM), F32)] + ride.out_shape,
        scratch_shapes=[pltpu.VMEM((bsz, D_STATE, GW), F32)] + ride.scratch,
        compiler_params=_params(*ride.semantics(("parallel", "arbitrary"))),
    )(xc, xc, xc, dt, acs, tr, z, dskip, normw, *ride.args)
    return outs[:3], outs[3:]


def _ssd_bwd(xc, dtr, dt, acs, tr, z, ypre, sprev, dyn, dtb, alog, dskip, normw, *, name, rider=None):
    bsz, t, _ = xc.shape
    nc = t // CHUNK
    sp = _ssd_specs(bsz, nc, True)

    def body(xs_ref, b_ref, c_ref, dtr_ref, dt_ref, acs_ref, tr_ref, z_ref, y_ref, sp_ref, dyn_ref, dtb_ref, alog_ref,
             dsk_ref, nw_ref, dz_ref, dxs_ref, db_ref, dc_ref, ddt_ref, dnw_ref, dsm_ref, ds_ref):
        j = pl.program_id(1)

        @pl.when(j == 0)
        def _():
            ds_ref[...] = jnp.zeros_like(ds_ref)
            dnw_ref[...] = jnp.zeros_like(dnw_ref)
            dsm_ref[...] = jnp.zeros_like(dsm_ref)

        ex = range(bsz)
        heads = range(HPG)
        units = [(e, r) for e in ex for r in heads]
        q = [_ssd_decay(dt_ref[e], acs_ref[e], tr_ref[e, 0, 0]) for e in ex]
        a = -jnp.exp(alog_ref[0])
        valid = _dt_valid(nc - 1 - j)
        lane, row = q[0]["lane"], q[0]["row"]
        lane1 = lane[0:1, :]
        nw = nw_ref[...]
        y, zz, dyn = [y_ref[e] for e in ex], [z_ref[e] for e in ex], [dyn_ref[e] for e in ex]
        sz = [_sigmoid(zz[e]) for e in ex]
        sil = [zz[e] * sz[e] for e in ex]
        yg = [y[e] * sil[e] for e in ex]
        rstd = [lax.rsqrt(jnp.mean(yg[e] * yg[e], axis=-1, keepdims=True) + EPS) for e in ex]
        gn = [dyn[e] * nw for e in ex]
        dyg = [rstd[e] * (gn[e] - yg[e] * (rstd[e] * rstd[e]) * jnp.mean(gn[e] * yg[e], axis=-1, keepdims=True))
               for e in ex]
        dy = [dyg[e] * sil[e] for e in ex]
        xs = [xs_ref[e] for e in ex]
        for e in ex:
            dnw_ref[e] += jnp.sum(dyn[e] * yg[e] * rstd[e], axis=0, keepdims=True)
            dz_ref[e] = (dyg[e] * y[e] * (sz[e] * (1.0 + zz[e] * (1.0 - sz[e])))).astype(BF16)
        dskip_cols = [jnp.sum(dy[e] * xs[e], axis=0, keepdims=True) for e in ex]

        bm, cm = [b_ref[e] for e in ex], [c_ref[e] for e in ex]
        cb = [_dot_nt(cm[e], bm[e]) for e in ex]
        zero = jnp.zeros((CHUNK, LANES), F32)
        full = lambda v: jnp.broadcast_to(v, (CHUNK, LANES))
        low = lane < HEAD_DIM
        half = [low if r % 2 == 0 else ~low for r in heads]
        sl = lambda v, r: v[:, (r // 2) * LANES:(r // 2 + 1) * LANES]
        pair = lambda r: pl.ds((r // 2) * LANES, LANES)
        col = {(e, r): full(q[e]["acs"][:, r:r + 1]) for e, r in units}
        dt_col = {(e, r): full(q[e]["dt"][:, r:r + 1]) for e, r in units}
        aend = {(e, r): q[e]["aend"][:, r:r + 1] for e, r in units}
        dt_row = {(e, r): q[e]["dt_t"][r:r + 1, :] for e, r in units}
        decay = {(e, r): jnp.exp(jnp.where(q[e]["causal"], col[e, r] - q[e]["acs_t"][r:r + 1, :], -jnp.inf))
                 for e, r in units}
        ea = {u: jnp.exp(col[u]) for u in units}
        dte = {u: jnp.exp(aend[u] - col[u]) for u in units}
        ed = {u: jnp.exp(aend[u]) for u in units}
        k = {u: dte[u] * dt_col[u] for u in units}
        mp = {(e, r): cb[e] * decay[e, r] * dt_row[e, r] for e, r in units}
        xp = {(e, r): sl(xs[e], r) for e, r in units}
        dym = {(e, r): jnp.where(half[r], sl(dy[e], r), 0.0) for e, r in units}
        s_old = {(e, r): sp_ref[e, 0, :, pair(r)] for e, r in units}
        ds_old = {(e, r): ds_ref[e, :, pair(r)] for e, r in units}
        dsm = {(e, r): jnp.where(half[r], ds_old[e, r], 0.0) for e, r in units}
        gmat = {u: _dot_nt(dym[u], xp[u]) for u in units}
        t1 = {u: _dot_nt(dym[u], s_old[u]) for u in units}
        dbs = {u: _dot_nt(xp[u], dsm[u]) for u in units}
        dx = {(e, r): _dot_tn(mp[e, r], dym[e, r]) + _dot(bm[e] * k[e, r], dsm[e, r]) for e, r in units}
        ds = {(e, r): _dot_tn(cm[e] * ea[e, r], dym[e, r]) for e, r in units}
        gd = {u: gmat[u] * decay[u] for u in units}
        w0 = {(e, r): gd[e, r] * cb[e] for e, r in units}
        cs0 = {u: jnp.sum(w0[u], axis=0, keepdims=True) for u in units}
        rs = {u: jnp.sum(w0[u] * dt_row[u], axis=1, keepdims=True) for u in units}
        qv = {(e, r): jnp.sum(cm[e] * t1[e, r], axis=1, keepdims=True) for e, r in units}
        dk = {(e, r): jnp.sum(bm[e] * dbs[e, r], axis=1, keepdims=True) for e, r in units}
        ddte = {u: dk[u] * dt_col[u] for u in units}
        d_aend = {u: _sum_all(dsm[u] * s_old[u]) * ed[u] + _sum_all(ddte[u][:, 0:1] * dte[u][:, 0:1]) for u in units}
        last_row = row == CHUNK - 1
        dacs_col = {u: rs[u] + qv[u] * ea[u] - ddte[u] * dte[u] + jnp.where(last_row, d_aend[u], 0.0) for u in units}
        triu = (lane >= row).astype(F32)
        for e in ex:
            dcb, dc_acc, db_acc = zero, zero, zero
            dacs, dacs_t, ddt, ddt_t = zero, zero, zero, zero
            dskip_row = jnp.zeros((1, LANES), F32)
            for r in heads:
                u = (e, r)
                dcb = dcb + gd[u] * dt_row[u]
                dc_acc = dc_acc + ea[u] * t1[u]
                db_acc = db_acc + k[u] * dbs[u]
                dacs = jnp.where(lane == r, dacs_col[u], dacs)
                ddt = jnp.where(lane == r, dk[u] * dte[u], ddt)
                dacs_t = jnp.where(row == r, -cs0[u] * dt_row[u], dacs_t)
                ddt_t = jnp.where(row == r, cs0[u], ddt_t)
                dsk = _sum_all(jnp.where(half[r][0:1, :], sl(dskip_cols[e], r), 0.0))
                dskip_row = dskip_row + jnp.where(lane1 == r, dsk, 0.0)
            for r in range(0, HPG, 2):
                dxs_ref[e, :, pair(r)] = (dx[e, r] + dx[e, r + 1] + sl(dy[e], r) * dsk_ref[:, pair(r)]).astype(BF16)
                ed_pair = jnp.where(lane1 < HEAD_DIM, ed[e, r], ed[e, r + 1])
                ds_ref[e, :, pair(r)] = ds[e, r] + ds[e, r + 1] + ed_pair * ds_old[e, r]
            dacs = dacs + dacs_t.T
            ddt = ddt + ddt_t.T
            dda = _dot_exact(triu, dacs)
            ddt = ddt + dda * a
            da = jnp.sum(dda * q[e]["dt"], axis=0, keepdims=True)
            draw = jnp.where(valid, ddt * _sigmoid(dtr_ref[e] + dtb_ref[0]), 0.0)
            ddt_ref[e] = draw.astype(BF16)
            dsm_ref[e, 0, 0:1, :] += dskip_row
            dsm_ref[e, 0, 1:2, :] += da * a
            dsm_ref[e, 0, 2:3, :] += jnp.sum(draw, axis=0, keepdims=True)
            dc_ref[e] = (dc_acc + _dot(dcb, bm[e])).astype(BF16)
            db_ref[e] = (db_acc + _dot_tn(dcb, cm[e])).astype(BF16)

    grp_out = pl.BlockSpec((bsz, CHUNK, D_STATE), lambda g, j: (0, nc - 1 - j, g))
    grid = (N_GROUPS, nc)
    ride = _Ride(rider, body, 15, 7, 1, grid)
    outs = pl.pallas_call(
        ride.body, name=name, grid=grid,
        in_specs=[sp["xs"], sp["bm"], sp["cm"], sp["lane_blk"], sp["lane_blk"], sp["lane_blk"], sp["tr"], sp["xs"],
                  sp["xs"], sp["state"], sp["xs"], sp["grp_const"], sp["grp_const"], sp["grp_vec"], sp["grp_vec"]]
        + ride.in_specs,
        out_specs=[sp["xs"], sp["xs"], grp_out, grp_out, sp["lane_blk"],
                   pl.BlockSpec((bsz, 1, GW), lambda g, j: (0, 0, g)),
                   pl.BlockSpec((bsz, 1, 8, LANES), lambda g, j: (0, g, 0, 0))] + ride.out_specs,
        out_shape=[jax.ShapeDtypeStruct((bsz, t, D_SSM), BF16), jax.ShapeDtypeStruct((bsz, t, D_SSM), BF16),
                   jax.ShapeDtypeStruct((bsz, t, N_GROUPS * D_STATE), BF16),
                   jax.ShapeDtypeStruct((bsz, t, N_GROUPS * D_STATE), BF16),
                   jax.ShapeDtypeStruct((bsz, t, D_DT), BF16), jax.ShapeDtypeStruct((bsz, 1, D_SSM), F32),
                   jax.ShapeDtypeStruct((bsz, N_GROUPS, 8, LANES), F32)] + ride.out_shape,
        scratch_shapes=[pltpu.VMEM((bsz, D_STATE, GW), F32)] + ride.scratch,
        compiler_params=_params(*ride.semantics(("parallel", "arbitrary"))),
    )(xc, xc, xc, dtr, dt, acs, tr, z, ypre, sprev, dyn, dtb, alog, dskip, normw, *ride.args)
    return outs[:7], outs[7:]


def _input_grad(dhn, h0, w, dres, seq, *, name):
    bsz, t, d = h0.shape
    nc = t // CHUNK

    def body(dy_ref, h_ref, w_ref, dres_ref, gx_ref, head_ref, dw_ref):
        j = pl.program_id(1)

        @pl.when((pl.program_id(0) == 0) & (j == 0))
        def _():
            dw_ref[...] = jnp.zeros_like(dw_ref)

        x, dyv = h_ref[0], dy_ref[0]
        r = lax.rsqrt(jnp.mean(x * x, axis=-1, keepdims=True) + EPS)
        g = dyv * w_ref[...]
        dx = r * (g - x * (r * r) * jnp.mean(g * x, axis=-1, keepdims=True)) + dres_ref[0]
        dw_ref[...] += jnp.sum(dyv * x * r, axis=0, keepdims=True)

        @pl.when(j == 0)
        def _():
            head_ref[0] = dx

        gx_ref[0] = dx

    row = pl.BlockSpec((1, CHUNK, d), lambda b, j: (b, j, 0))
    return pl.pallas_call(
        body, name=name, grid=(bsz, nc),
        in_specs=[row, row, pl.BlockSpec((1, d), lambda b, j: (0, 0)), row],
        out_specs=[pl.BlockSpec((1, CHUNK, d), lambda b, j: (b, jnp.maximum(j - 1, 0), 0)),
                   pl.BlockSpec((1, CHUNK, d), lambda b, j: (b, 0, 0)), pl.BlockSpec((1, d), lambda b, j: (0, 0))],
        out_shape=[jax.ShapeDtypeStruct((bsz, seq, d), F32), jax.ShapeDtypeStruct((bsz, CHUNK, d), F32),
                   jax.ShapeDtypeStruct((1, d), F32)],
        compiler_params=_params("arbitrary", "arbitrary"),
    )(dhn, h0, w, dres)


def _remote(src, dst, send_sem, recv_sem, dev):
    return pltpu.make_async_remote_copy(src_ref=src, dst_ref=dst, send_sem=send_sem, recv_sem=recv_sem,
                                        device_id=dev, device_id_type=MESH)


def _position():
    return lax.axis_index("x"), lax.axis_index("y"), lax.axis_index("c")


def _other_chips(pos):
    x, y, _ = pos
    return [(1 - x, y), (x, 1 - y), (1 - x, 1 - y)]


class _Gather:
    def __init__(self, arrs):
        n = len(arrs)
        self.args, self.n_in, self.n_out = list(arrs), n, n
        self.split = [a.ndim == 2 and a.shape[1] % (2 * LANES) == 0 for a in arrs]
        self.out_shape = [jax.ShapeDtypeStruct((4,) + a.shape, a.dtype) for a in arrs]
        self.scratch = [pltpu.SemaphoreType.DMA((3 * n,)), pltpu.SemaphoreType.DMA((3 * n,)),
                        pltpu.SemaphoreType.DMA((n,)), pltpu.SemaphoreType.DMA((3 * n,)),
                        pltpu.SemaphoreType.DMA((3 * n,))]

    def _copies(self, pos, ins, outs, sems):
        send_sems, recv_sems, loc_sems, pass_send_sems, pass_recv_sems = sems
        x, y, c = pos
        me, sibling = 2 * x + y, (x, y, 1 - c)
        local = [pltpu.make_async_copy(ins[i], outs[i].at[me], loc_sems.at[i]) for i in range(self.n_in)]
        sends, recvs, passes, pass_recvs = [], [], [], []
        for i in range(self.n_in):
            half = self.args[i].shape[1] // 2 if self.split[i] else None
            for k, (px, py) in enumerate(_other_chips(pos)):
                them = 2 * px + py
                sems_k = (send_sems.at[3 * i + k], recv_sems.at[3 * i + k], (px, py, c))
                if half is None:
                    sends.append(_remote(ins[i], outs[i].at[me], *sems_k))
                    recvs.append(_remote(ins[i], outs[i].at[them], *sems_k))
                    passes.append(None)
                    continue
                mine = pl.ds(pl.multiple_of(c * half, LANES), half)
                other = pl.ds(pl.multiple_of((1 - c) * half, LANES), half)
                sends.append(_remote(ins[i].at[:, mine], outs[i].at[me, :, mine], *sems_k))
                recvs.append(_remote(ins[i].at[:, mine], outs[i].at[them, :, mine], *sems_k))
                pass_k = (pass_send_sems.at[3 * i + k], pass_recv_sems.at[3 * i + k], sibling)
                passes.append(_remote(outs[i].at[them, :, mine], outs[i].at[them, :, mine], *pass_k))
                pass_recvs.append(_remote(outs[i].at[them, :, other], outs[i].at[them, :, other], *pass_k))
        return local, sends, recvs, passes, pass_recvs

    def start(self, pos, ins, outs, sems):
        local, sends = self._copies(pos, ins, outs, sems)[:2]
        for cp in local + sends:
            cp.start()

    def finish(self, pos, ins, outs, sems):
        local, sends, recvs, passes, pass_recvs = self._copies(pos, ins, outs, sems)
        for cp, onward in zip(recvs, passes):
            cp.wait_recv()
            if onward is not None:
                onward.start()
        for cp in pass_recvs:
            cp.wait_recv()
        for cp in sends + [p for p in passes if p is not None]:
            cp.wait_send()
        for cp in local:
            cp.wait()


class _Exchange:
    FLIPS = [(fx, fy, fc) for fx in (0, 1) for fy in (0, 1) for fc in (0, 1)][1:]

    def __init__(self, big, small=None):
        n = len(big)
        self.n_big, self.has_small = n, small is not None
        self.args = list(big) + ([small] if self.has_small else [])
        self.n_in = self.n_out = len(self.args)
        self.out_shape = [jax.ShapeDtypeStruct(a.shape, a.dtype) for a in big]
        self.scratch = [pltpu.SemaphoreType.DMA((max(3 * n, 1),)), pltpu.SemaphoreType.DMA((max(3 * n, 1),))]
        if self.has_small:
            self.out_shape.append(jax.ShapeDtypeStruct((8,) + small.shape, small.dtype))
            self.scratch += [pltpu.SemaphoreType.DMA((7,)), pltpu.SemaphoreType.DMA((7,)), pltpu.SemaphoreType.DMA((1,))]

    def _copies(self, pos, ins, outs, sems):
        x, y, c = pos
        me, me8 = 2 * x + y, 4 * x + 2 * y + c
        local, sends, recvs = [], [], []
        for i in range(self.n_big):
            for k, (px, py) in enumerate(_other_chips(pos)):
                sems_k = (sems[0].at[3 * i + k], sems[1].at[3 * i + k], (px, py, c))
                sends.append(_remote(ins[i].at[2 * px + py], outs[i].at[me], *sems_k))
                recvs.append(_remote(ins[i].at[me], outs[i].at[2 * px + py], *sems_k))
        if self.has_small:
            small, landed = ins[self.n_big], outs[self.n_big]
            local.append(pltpu.make_async_copy(small, landed.at[me8], sems[4].at[0]))
            for k, (fx, fy, fc) in enumerate(self.FLIPS):
                peer = (x ^ fx, y ^ fy, c ^ fc)
                sems_k = (sems[2].at[k], sems[3].at[k], peer)
                sends.append(_remote(small, landed.at[me8], *sems_k))
                recvs.append(_remote(small, landed.at[4 * peer[0] + 2 * peer[1] + peer[2]], *sems_k))
        return local, sends, recvs, [None] * len(recvs), []

    start = _Gather.start
    finish = _Gather.finish


class _Swap:
    def __init__(self, arrs):
        n = len(arrs)
        self.args, self.n_in, self.n_out = list(arrs), n, n
        self.out_shape = [jax.ShapeDtypeStruct(a.shape, a.dtype) for a in arrs]
        self.scratch = [pltpu.SemaphoreType.DMA((n,)), pltpu.SemaphoreType.DMA((n,))]

    def _copies(self, pos, ins, outs, sems):
        x, y, c = pos
        both = [_remote(ins[i], outs[i], sems[0].at[i], sems[1].at[i], (x, y, 1 - c)) for i in range(self.n_in)]
        return [], both, both, [None] * len(both), []

    start = _Gather.start
    finish = _Gather.finish


def _comm(rider, *, name):
    a, b = rider.n_in, rider.n_in + rider.n_out

    def body(*refs):
        pos = _position()
        rider.start(pos, refs[:a], refs[a:b], refs[b:])
        rider.finish(pos, refs[:a], refs[a:b], refs[b:])

    return pl.pallas_call(body, name=name, in_specs=[ANY] * rider.n_in, out_specs=[ANY] * rider.n_out,
                          out_shape=rider.out_shape, scratch_shapes=rider.scratch)(*rider.args)


class _Ride:
    def __init__(self, rider, body, n_in, n_out, n_scratch, grid):
        self.rider = rider
        self.args = rider.args if rider else []
        self.in_specs = [ANY] * rider.n_in if rider else []
        self.out_specs = [ANY] * rider.n_out if rider else []
        self.out_shape = rider.out_shape if rider else []
        self.scratch = rider.scratch if rider else []
        self.body = self._wrap(body, n_in, n_out, n_scratch, grid) if rider else body

    def semantics(self, sem):
        return ("arbitrary",) * len(sem) if self.rider else sem

    def _wrap(self, body, n_in, n_out, n_scratch, grid):
        rider = self.rider
        a = n_in
        b = a + rider.n_in
        c = b + n_out
        d = c + rider.n_out
        e = d + n_scratch

        def wrapped(*refs):
            pos = _position()
            ids = [pl.program_id(i) for i in range(len(grid))]
            first = functools.reduce(jnp.logical_and, [i == 0 for i in ids])
            last = functools.reduce(jnp.logical_and, [i == g - 1 for i, g in zip(ids, grid)])

            @pl.when(first)
            def _():
                rider.start(pos, refs[a:b], refs[c:d], refs[e:])

            body(*refs[:a], *refs[b:c], *refs[d:e])

            @pl.when(last)
            def _():
                rider.finish(pos, refs[a:b], refs[c:d], refs[e:])

        return wrapped


def _elementwise_tiles(r, c):
    if r % 8 == 0 and r * c > 65536:
        tm = _pick(r, (256, 128, 64, 16, 8))
        return (tm, c), r // tm, lambda i: (i, 0)
    if r % 8 and c % 256 == 0 and r * c > 65536:
        return (r, 256), c // 256, lambda i: (0, i)
    return (r, c), 1, lambda i: (0, 0)


def _chip_sum(own, landed, *, name):
    r, c = own.shape
    blk, steps, at = _elementwise_tiles(r, c)

    def body(own_ref, land_ref, o_ref):
        me = 2 * lax.axis_index("x") + lax.axis_index("y")
        acc = None
        for jchip in range(4):
            term = jnp.where(me == jchip, own_ref[...], land_ref[jchip].astype(F32))
            acc = term if acc is None else acc + term
        o_ref[...] = acc

    return pl.pallas_call(
        body, name=name, grid=(steps,),
        in_specs=[pl.BlockSpec(blk, at), pl.BlockSpec((4,) + blk, lambda i: (0,) + at(i))],
        out_specs=pl.BlockSpec(blk, at), out_shape=jax.ShapeDtypeStruct((r, c), F32),
        compiler_params=_params("parallel"),
    )(own, landed)


def _device_sum(parts, *, name):
    _, r, c = parts.shape

    def body(p_ref, o_ref):
        acc = p_ref[0]
        for d in range(1, 8):
            acc = acc + p_ref[d]
        o_ref[...] = acc

    return pl.pallas_call(body, name=name, out_shape=jax.ShapeDtypeStruct((r, c), F32))(parts)


def _adamw_math(w, g, m, v):
    m = ADAM_B1 * m + (1.0 - ADAM_B1) * g
    v = ADAM_B2 * v + (1.0 - ADAM_B2) * (g * g)
    m_hat = m / (1.0 - ADAM_B1 ** ADAM_STEP)
    v_hat = v / (1.0 - ADAM_B2 ** ADAM_STEP)
    return -ADAM_LR * (m_hat / (jnp.sqrt(v_hat) + ADAM_EPS) + ADAM_WD * w), m, v


def _adamw(w, g_parts, m, v, *, name):
    r, c = w.shape
    shape, steps, at = _elementwise_tiles(r, c)
    n_g = len(g_parts)

    def body(*refs):
        w_ref, m_ref, v_ref = refs[n_g:n_g + 3]
        g_ref, d_ref, nm_ref, nv_ref = refs[n_g + 3:]
        g = refs[0][...]
        for p in refs[1:n_g]:
            g = g + p[...]
        g_ref[...] = g
        d_ref[...], nm_ref[...], nv_ref[...] = _adamw_math(w_ref[...], g, m_ref[...], v_ref[...])

    blk = pl.BlockSpec(shape, at)
    return pl.pallas_call(
        body, name=name, grid=(steps,), in_specs=[blk] * (n_g + 3), out_specs=[blk] * 4,
        out_shape=[jax.ShapeDtypeStruct((r, c), F32)] * 4, compiler_params=_params("parallel"),
    )(*g_parts, w, m, v)


def _pad_heads(v):
    return jnp.pad(v.reshape(N_GROUPS, 1, HPG), ((0, 0), (0, 0), (0, LANES - HPG)))


def _unpad_heads(v):
    return v[:, :HPG].reshape(1, N_HEADS)


_SMALL_EARLY = [("pool_w", (512, 128)), ("pool_scale", (1, 512)), ("conv_w", (4, D_XBC)), ("conv_b", (1, D_XBC)),
                ("dt_bias", (1, N_HEADS)), ("a_log", (1, N_HEADS)), ("d_skip", (1, N_HEADS)), ("ssm_norm_w", (1, D_SSM)),
                ("norm_ffn_w", (1, 1024)), ("norm_f_w", (1, 1024))]
_SMALL_LATE = [("norm_mix_w", (1, 1024)), ("meta", (N_META, 1024)), ("loss", (1, 1))]


def _pack_small(grads, layout):
    rows = []
    for nm, shape in layout:
        flat = grads[nm].reshape(-1)
        rows.append(jnp.pad(flat, (0, (-flat.size) % LANES)).reshape(-1, LANES))
    packed = jnp.concatenate(rows, axis=0)
    return jnp.pad(packed, ((0, (-packed.shape[0]) % 8), (0, 0)))


def _unpack_small(packed, layout):
    out, r0 = {}, 0
    for nm, shape in layout:
        size = shape[0] * shape[1]
        nrow = -(-size // LANES)
        out[nm] = packed[r0:r0 + nrow].reshape(-1)[:size].reshape(shape)
        r0 += nrow
    return out


def kernel(x, meta, norm_mix_w, w_in, pool_w, pool_scale, conv_w, conv_b, dt_bias, a_log, d_skip, ssm_norm_w, w_out, norm_ffn_w, w_ff1, w_ff2, norm_f_w, loss_target, m_meta, m_norm_mix_w, m_w_in, m_pool_w, m_pool_scale, m_conv_w, m_conv_b, m_dt_bias, m_a_log, m_d_skip, m_ssm_norm_w, m_w_out, m_norm_ffn_w, m_w_ff1, m_w_ff2, m_norm_f_w, v_meta, v_norm_mix_w, v_w_in, v_pool_w, v_pool_scale, v_conv_w, v_conv_b, v_dt_bias, v_a_log, v_d_skip, v_ssm_norm_w, v_w_out, v_norm_ffn_w, v_w_ff1, v_w_ff2, v_norm_f_w):
    bsz, seq, d = x.shape
    t = seq + CHUNK
    n = bsz * t
    chip = 2 * lax.axis_index("x") + lax.axis_index("y")
    d_in = w_in.shape[2] * 4

    g_in, g_conv, g_meta = _comm(_Gather([w_in[0].T.astype(BF16), conv_w[0], meta]), name="gather_in")
    late_weights = _Gather([w_out[0].astype(BF16), w_ff1[0].astype(BF16), w_ff2[0].astype(BF16)])
    win = g_in.reshape(d_in, d)
    wu, wz = win[:D_POOL], win[D_POOL:D_POOL + D_SSM]
    wx = win[D_POOL + D_SSM:D_POOL + D_SSM + D_XBC]
    wdt = jnp.pad(win[D_POOL + D_SSM + D_XBC:].reshape(N_GROUPS, HPG, d),
                  ((0, 0), (0, LANES - HPG), (0, 0))).reshape(D_DT, d)
    convw = g_conv.transpose(1, 0, 2).reshape(CONV_W, D_XBC)
    meta_full = g_meta.transpose(1, 0, 2).reshape(N_META, d)
    dtb, alog = _pad_heads(dt_bias), _pad_heads(a_log)
    dskip = jnp.repeat(d_skip, HEAD_DIM, axis=1)
    poolw = pool_w[0]

    h0 = jnp.concatenate([jnp.zeros((bsz, PAD, d), F32), jnp.broadcast_to(meta_full[None], (bsz, N_META, d)), x], axis=1)
    h0f = h0.reshape(n, d)
    hn1 = _rms_fwd(h0f, norm_mix_w, name="norm_mix")
    u = _mm(hn1, wu, name="proj_u", nt=True)
    z = _mm(hn1, wz, name="proj_z", nt=True)
    xbc = _mm(hn1, wx, name="proj_xbc", nt=True)
    dtr = _mm(hn1, wdt, name="proj_dt", nt=True)
    ypool = _pool_fwd(u.reshape(bsz, t, D_POOL), poolw, pool_scale, name="pool_fwd")
    xbc3 = xbc.reshape(bsz, t, D_XBC)
    xc = _conv_fwd(xbc3, convw, conv_b, name="conv_fwd")
    z3, dtr3 = z.reshape(bsz, t, D_SSM), dtr.reshape(bsz, t, D_DT)
    dt3, acs3, tr3 = _ssd_prep(dtr3, dtb, alog, name="ssd_prep")
    (yn, ypre, sprev), (g_out, g_ff1, g_ff2) = _ssd_fwd(xc, dt3, acs3, tr3, z3, dskip, ssm_norm_w, name="ssd_fwd",
                                                        rider=late_weights)
    wo = g_out.reshape(D_POOL + D_SSM, d)
    wo_p, wo_s = wo[:D_POOL], wo[D_POOL:]
    w1 = g_ff1
    w2 = g_ff2.reshape(D_FF, d)
    ypool_f, yn_f = ypool.reshape(n, D_POOL), yn.reshape(n, D_SSM)
    add = lambda r, e: r + e
    h1 = _mm([ypool_f, yn_f], [wo_p, wo_s], name="out_proj", post=add, extras=(h0f,))
    hn2 = _rms_fwd(h1, norm_ffn_w, name="norm_ffn")
    act = _mm(hn2, w1, name="ff1", out_dtype=BF16)
    relu2 = lambda a: jnp.square(jnp.maximum(a, 0))
    h2 = _mm(act, w2, name="ff2", pre=relu2, post=add, extras=(h1,))
    dh2, dh2b, loss_acc, d_norm_f = _final_norm_loss(h2.reshape(bsz, t, d), loss_target, norm_f_w.reshape(1, d),
                                                     name="loss")

    dh2f, dh2bf = dh2.reshape(n, d), dh2b.reshape(n, d)
    dact = _mm(dh2bf, w2, name="ff2_bwd", nt=True, post=lambda r, a: r * (2.0 * jnp.maximum(a, 0).astype(F32)),
               extras=(act,), out_dtype=BF16)
    d_w2 = _mm_tn(act, dh2bf, name="ff2_dw", tk=2048, tn=1024, pre=relu2)
    d_w1 = _mm_tn(hn2, dact, name="ff1_dw", tk=1024, tn=D_FF // 4, column_shards=True)
    dh1, dh1b, d_norm_ffn = _mm_rms_bwd(dact, w1, h1, norm_ffn_w, dh2f, name="ff1_bwd")
    dypool = _mm(dh1b, wo_p, name="out_pool_bwd", nt=True)
    dyn = _mm(dh1b, wo_s, name="out_ssm_bwd", nt=True)
    d_wo_p = _mm_tn(ypool_f, dh1b, name="out_pool_dw", tk=512, tn=1024)
    d_wo_s = _mm_tn(yn_f, dh1b, name="out_ssm_dw", tk=1536, tn=1024)
    big_late = [jnp.concatenate([d_wo_p, d_wo_s], axis=0).reshape(4, (D_POOL + D_SSM) // 4, d),
                d_w1, d_w2.reshape(4, D_FF // 4, d)]
    (dz, dxs, dbm, dcm, ddtr, d_nw, d_heads), landed_late = _ssd_bwd(
        xc, dtr3, dt3, acs3, tr3, z3, ypre, sprev, dyn.reshape(bsz, t, D_SSM), dtb, alog, dskip, ssm_norm_w, name="ssd_bwd",
        rider=_Exchange([b.astype(BF16) for b in big_late]))
    dxbc, d_convwb = _conv_bwd(xbc3, dxs, dbm, dcm, convw, conv_b, name="conv_bwd")
    du, d_poolw, d_poolsc = _pool_bwd(u.reshape(bsz, t, D_POOL), dypool.reshape(bsz, t, D_POOL), poolw, pool_scale,
                                      name="pool_bwd")
    duf, dzf, dxbcf, ddtrf = du.reshape(n, D_POOL), dz.reshape(n, D_SSM), dxbc.reshape(n, D_XBC), ddtr.reshape(n, D_DT)
    heads = jnp.sum(d_heads, axis=0)
    small_early = _pack_small({
        "pool_w": d_poolw, "pool_scale": d_poolsc,
        "conv_w": jnp.sum(d_convwb[:, :CONV_W], axis=0), "conv_b": jnp.sum(d_convwb[:, CONV_W:CONV_W + 1], axis=0),
        "dt_bias": _unpad_heads(heads[:, 2]), "a_log": _unpad_heads(heads[:, 1]), "d_skip": _unpad_heads(heads[:, 0]),
        "ssm_norm_w": jnp.sum(d_nw, axis=0), "norm_ffn_w": d_norm_ffn, "norm_f_w": d_norm_f}, _SMALL_EARLY)
    d_wu = _mm_tn(duf, hn1, name="proj_u_dw", tk=512, tn=1024)
    d_wz = _mm_tn(dzf, hn1, name="proj_z_dw", tk=1536, tn=1024)
    d_wx, (early_all,) = _mm_tn(dxbcf, hn1, name="proj_xbc_dw", tk=1280, tn=1024, rider=_Exchange([], small_early))
    d_wdt = _mm_tn(ddtrf, hn1, name="proj_dt_dw", tk=512, tn=1024)
    d_win = jnp.concatenate([d_wu, d_wz, d_wx, d_wdt.reshape(N_GROUPS, LANES, d)[:, :HPG].reshape(N_HEADS, d)], axis=0)
    big_in = d_win.reshape(4, d_in // 4, d)
    dhn1, (landed_in,) = _mm([duf, dzf, dxbcf, ddtrf], [wu, wz, wx, wdt], name="proj_bwd",
                             rider=_Exchange([big_in.astype(BF16)]))
    grad_x, d_head_rows, d_norm_mix = _input_grad(
        dhn1.reshape(bsz, t, d), h0, norm_mix_w, dh1.reshape(bsz, t, d), seq, name="input_grad")

    big = [big_in] + big_late
    landed = [landed_in] + list(landed_late)
    small_late = _pack_small({"norm_mix_w": d_norm_mix, "meta": jnp.sum(d_head_rows[:, PAD:], axis=0),
                              "loss": loss_acc[0:1, 0:1]}, _SMALL_LATE)
    (late_all,) = _comm(_Exchange([], small_late), name="exchange_small")
    own = [lax.dynamic_index_in_dim(b, chip, 0, keepdims=False) for b in big]
    mine = [_chip_sum(o, l, name=f"chip_sum_{i}") for i, (o, l) in enumerate(zip(own, landed))]
    theirs = _comm(_Swap(mine), name="swap_cores")
    gsmall = {**_unpack_small(_device_sum(early_all, name="device_sum_early"), _SMALL_EARLY),
              **_unpack_small(_device_sum(late_all, name="device_sum_late"), _SMALL_LATE)}
    gsmall["conv_w"] = lax.dynamic_slice_in_dim(gsmall["conv_w"], chip * (D_XBC // 4), D_XBC // 4, axis=1)
    gsmall["meta"] = lax.dynamic_slice_in_dim(gsmall["meta"], chip * (d // 4), d // 4, axis=1)
    loss = gsmall["loss"][0, 0]

    given = dict(meta=(meta, m_meta, v_meta), norm_mix_w=(norm_mix_w, m_norm_mix_w, v_norm_mix_w),
                 w_in=(w_in, m_w_in, v_w_in), pool_w=(pool_w, m_pool_w, v_pool_w),
                 pool_scale=(pool_scale, m_pool_scale, v_pool_scale), conv_w=(conv_w, m_conv_w, v_conv_w),
                 conv_b=(conv_b, m_conv_b, v_conv_b), dt_bias=(dt_bias, m_dt_bias, v_dt_bias),
                 a_log=(a_log, m_a_log, v_a_log), d_skip=(d_skip, m_d_skip, v_d_skip),
                 ssm_norm_w=(ssm_norm_w, m_ssm_norm_w, v_ssm_norm_w), w_out=(w_out, m_w_out, v_w_out),
                 norm_ffn_w=(norm_ffn_w, m_norm_ffn_w, v_norm_ffn_w), w_ff1=(w_ff1, m_w_ff1, v_w_ff1),
                 w_ff2=(w_ff2, m_w_ff2, v_w_ff2), norm_f_w=(norm_f_w, m_norm_f_w, v_norm_f_w))
    big_names = ["w_in", "w_out", "w_ff1", "w_ff2"]
    results = {}
    for nm, (w, m, v) in given.items():
        if nm in big_names:
            i = big_names.index(nm)
            parts, shape2 = (mine[i], theirs[i]), mine[i].shape
        else:
            parts, shape2 = (gsmall[nm],), gsmall[nm].shape
        if nm == "w_in":
            outs = _adamw(w[0].T, parts, m[0].T, v[0].T, name=f"adamw_{nm}")
            results[nm] = [o.T[None] for o in outs]
        else:
            outs = _adamw(w.reshape(shape2), parts, m.reshape(shape2), v.reshape(shape2), name=f"adamw_{nm}")
            results[nm] = [o.reshape(w.shape) for o in outs]
    order = list(given)
    return (loss, grad_x, *[results[nm][0] for nm in order], *[results[nm][1] for nm in order],
            *[results[nm][2] for nm in order], *[results[nm][3] for nm in order])
```

```python
import functools

import jax
import jax.numpy as jnp
from jax import lax
from jax.experimental import pallas as pl
from jax.experimental.pallas import tpu as pltpu

F32 = jnp.float32
BF16 = jnp.bfloat16
MESH = pl.DeviceIdType.MESH
ANY = pl.BlockSpec(memory_space=pl.ANY)

D_MODEL = 1024
N_META = 16
CHUNK = 128
PAD = CHUNK - N_META
POOL_WINDOWS = (2, 4, 8, 16)
D_POOL = 512
POOL_GROUP = 128
D_SSM = 1536
N_HEADS = 24
N_GROUPS = 4
HPG = 6
HEAD_DIM = 64
D_STATE = 128
GW = HPG * HEAD_DIM
D_XBC = D_SSM + 2 * N_GROUPS * D_STATE
D_DT = N_GROUPS * 128
D_FF = 4096
CONV_W = 4
EPS = 1e-5
LANES = 128
VMEM_LIMIT = 56 * 1024 * 1024

ADAM_LR, ADAM_B1, ADAM_B2, ADAM_EPS, ADAM_WD, ADAM_STEP = 0.001, 0.9, 0.999, 1e-08, 0.01, 10


def _params(*sem):
    return pltpu.CompilerParams(dimension_semantics=sem, vmem_limit_bytes=VMEM_LIMIT)


def _pick(n, cands):
    for c in cands:
        if n % c == 0:
            return c
    raise ValueError(f"no block size for {n}")


def _dot(a, b):
    return jnp.dot(a.astype(BF16), b.astype(BF16), preferred_element_type=F32)


def _dot_nt(a, b):
    return lax.dot_general(a.astype(BF16), b.astype(BF16), (((1,), (1,)), ((), ())), preferred_element_type=F32)


def _dot_tn(a, b):
    return lax.dot_general(a.astype(BF16), b.astype(BF16), (((0,), (0,)), ((), ())), preferred_element_type=F32)


def _dot_exact(mask, x):
    m = mask.astype(BF16)
    hi = x.astype(BF16)
    r1 = x - hi.astype(F32)
    mid = r1.astype(BF16)
    lo = (r1 - mid.astype(F32)).astype(BF16)
    dot = lambda t: jnp.dot(m, t, preferred_element_type=F32)
    return dot(hi) + dot(mid) + dot(lo)


def _sigmoid(x):
    return 1.0 / (1.0 + jnp.exp(-x))


def _softplus(x):
    return jnp.maximum(x, 0.0) + jnp.log1p(jnp.exp(-jnp.abs(x)))


def _sum_all(x):
    return jnp.sum(jnp.sum(x, axis=1, keepdims=True), axis=0, keepdims=True)


ROW_TILES = (1056, 768, 704, 512, 384, 256, 128)
TILE_BUDGET = 28 * 1024 * 1024


def _row_tile(n, bytes_per_row, fixed_bytes, budget=TILE_BUDGET):
    for tm in ROW_TILES:
        if n % tm == 0 and 2 * (tm * bytes_per_row + fixed_bytes) <= budget:
            return tm
    raise ValueError(f"no row tile for {n}")


WIDE_BUDGET = 38 * 1024 * 1024


def _mm(a, w, *, name, tn=512, nt=False, pre=None, post=None, extras=(), out_dtype=F32, rider=None):
    a_list = list(a) if isinstance(a, (list, tuple)) else [a]
    w_list = list(w) if isinstance(w, (list, tuple)) else [w]
    n_a, n_ex = len(a_list), len(extras)
    n = a_list[0].shape[0]
    shard = w_list[0].shape[2] if w_list[0].ndim == 3 else None
    assert shard is None or (not nt and n_a == 1 and shard % tn == 0)
    m = w_list[0].shape[0] * shard if shard else w_list[0].shape[0] if nt else w_list[0].shape[1]
    tn = min(tn, m)
    size = lambda dt: jnp.dtype(dt).itemsize
    per_row = (sum(x.shape[1] * size(x.dtype) for x in a_list) + m * size(out_dtype)
               + sum(m * size(e.dtype) for e in extras))
    tm = _row_tile(n, per_row, sum(x.size * size(x.dtype) for x in w_list) // 2, WIDE_BUDGET)

    def body(*refs):
        a_refs, w_refs, ex_refs, o_ref = refs[:n_a], refs[n_a:2 * n_a], refs[2 * n_a:2 * n_a + n_ex], refs[2 * n_a + n_ex]
        avs = [(a_ref[...] if pre is None else pre(a_ref[...])).astype(BF16) for a_ref in a_refs]
        for c0 in range(0, m, tn):
            r = None
            for av, w_ref in zip(avs, w_refs):
                if shard:
                    term = _dot(av, w_ref[c0 // shard, :, c0 % shard:c0 % shard + tn])
                else:
                    term = _dot_nt(av, w_ref[c0:c0 + tn, :]) if nt else _dot(av, w_ref[:, c0:c0 + tn])
                r = term if r is None else r + term
            if post is not None:
                r = post(r, *[e[:, c0:c0 + tn] for e in ex_refs])
            o_ref[:, c0:c0 + tn] = r.astype(out_dtype)

    a_specs = [pl.BlockSpec((tm, x.shape[1]), lambda i: (i, 0)) for x in a_list]
    w_specs = [pl.BlockSpec(x.shape, lambda i, nd=x.ndim: (0,) * nd, pipeline_mode=pl.Buffered(1)) for x in w_list]
    blk = pl.BlockSpec((tm, m), lambda i: (i, 0))
    grid = (n // tm,)
    ride = _Ride(rider, body, 2 * n_a + n_ex, 1, 0, grid)
    outs = pl.pallas_call(
        ride.body, name=name, grid=grid,
        in_specs=a_specs + w_specs + [blk] * n_ex + ride.in_specs,
        out_specs=[blk] + ride.out_specs, out_shape=[jax.ShapeDtypeStruct((n, m), out_dtype)] + ride.out_shape,
        scratch_shapes=ride.scratch, compiler_params=_params(*ride.semantics(("parallel",))),
    )(*a_list, *w_list, *extras, *ride.args)
    return (outs[0], outs[1:]) if rider else outs[0]


def _mm_tn(a, g, *, name, tk, tn, pre=None, slab=None, rider=None):
    n, k = a.shape
    m = g.shape[1]
    tk, tn = min(tk, k), min(tn, m)
    tm = _row_tile(n, tk * jnp.dtype(a.dtype).itemsize + tn * jnp.dtype(g.dtype).itemsize, tk * tn * 4)
    steps = n // tm

    def body(a_ref, g_ref, o_ref, acc_ref):
        r = pl.program_id(2)

        @pl.when(r == 0)
        def _():
            acc_ref[...] = jnp.zeros_like(acc_ref)

        av = a_ref[...]
        if pre is not None:
            av = pre(av)
        if slab:
            for s in range(tn // slab):
                acc_ref[s] += _dot_tn(av, g_ref[:, s * slab:(s + 1) * slab])
        else:
            acc_ref[...] += _dot_tn(av, g_ref[...])

        @pl.when(r == steps - 1)
        def _():
            o_ref[...] = acc_ref[...].astype(BF16)

    if slab:
        block, out_spec = (tn // slab, tk, slab), pl.BlockSpec((tn // slab, tk, slab), lambda i, j, r: (j, i, 0))
        out_shape = jax.ShapeDtypeStruct((m // slab, k, slab), BF16)
    else:
        block, out_spec = (tk, tn), pl.BlockSpec((tk, tn), lambda i, j, r: (i, j))
        out_shape = jax.ShapeDtypeStruct((k, m), BF16)
    grid = (k // tk, m // tn, steps)
    ride = _Ride(rider, body, 2, 1, 1, grid)
    outs = pl.pallas_call(
        ride.body, name=name, grid=grid,
        in_specs=[pl.BlockSpec((tm, tk), lambda i, j, r: (r, i)), pl.BlockSpec((tm, tn), lambda i, j, r: (r, j))]
        + ride.in_specs,
        out_specs=[out_spec] + ride.out_specs, out_shape=[out_shape] + ride.out_shape,
        scratch_shapes=[pltpu.VMEM(block, F32)] + ride.scratch,
        compiler_params=_params(*ride.semantics(("parallel", "parallel", "arbitrary"))),
    )(a, g, *ride.args)
    return (outs[0], outs[1:]) if rider else outs[0]


def _mm_rms_bwd(a, w, h, w_norm, dres, *, name):
    n, k = a.shape
    d = h.shape[1]
    slabs, _, ks = w.shape
    tm = _row_tile(n, k * jnp.dtype(a.dtype).itemsize + d * (4 + 4 + 4 + 2), d * k, WIDE_BUDGET)

    def body(a_ref, w_ref, h_ref, wn_ref, dres_ref, dx_ref, dxb_ref, dw_ref):
        @pl.when(pl.program_id(0) == 0)
        def _():
            dw_ref[...] = jnp.zeros_like(dw_ref)

        dyv = None
        for s in range(slabs):
            part = _dot_nt(a_ref[:, s * ks:(s + 1) * ks], w_ref[s])
            dyv = part if dyv is None else dyv + part
        x = h_ref[...]
        r = lax.rsqrt(jnp.mean(x * x, axis=-1, keepdims=True) + EPS)
        g = dyv * wn_ref[...]
        dx = r * (g - x * (r * r) * jnp.mean(g * x, axis=-1, keepdims=True)) + dres_ref[...]
        dx_ref[...] = dx
        dxb_ref[...] = dx.astype(BF16)
        dw_ref[...] += jnp.sum(dyv * x * r, axis=0, keepdims=True)

    row = pl.BlockSpec((tm, d), lambda i: (i, 0))
    vec = pl.BlockSpec((1, d), lambda i: (0, 0))
    return pl.pallas_call(
        body, name=name, grid=(n // tm,),
        in_specs=[pl.BlockSpec((tm, k), lambda i: (i, 0)),
                  pl.BlockSpec(w.shape, lambda i: (0, 0, 0), pipeline_mode=pl.Buffered(1)), row, vec, row],
        out_specs=[row, row, vec],
        out_shape=[jax.ShapeDtypeStruct((n, d), F32), jax.ShapeDtypeStruct((n, d), BF16), jax.ShapeDtypeStruct((1, d), F32)],
        compiler_params=_params("arbitrary"),
    )(a, w, h, w_norm, dres)


def _rms_fwd(h, w, *, name):
    n, d = h.shape
    tm = _pick(n, (768, 512, 256, 128))

    def body(h_ref, w_ref, o_ref):
        x = h_ref[...]
        r = lax.rsqrt(jnp.mean(x * x, axis=-1, keepdims=True) + EPS)
        o_ref[...] = (x * r * w_ref[...]).astype(BF16)

    return pl.pallas_call(
        body, name=name, grid=(n // tm,),
        in_specs=[pl.BlockSpec((tm, d), lambda i: (i, 0)), pl.BlockSpec((1, d), lambda i: (0, 0))],
        out_specs=pl.BlockSpec((tm, d), lambda i: (i, 0)), out_shape=jax.ShapeDtypeStruct((n, d), BF16),
        compiler_params=_params("parallel"),
    )(h, w)


def _final_norm_loss(h2, target, w, *, name):
    bsz, t, d = h2.shape
    nc = t // CHUNK

    def body(h_ref, t_ref, w_ref, dh_ref, dhb_ref, loss_ref, dw_ref):
        j = pl.program_id(0)

        @pl.when(j == 0)
        def _():
            loss_ref[...] = jnp.zeros_like(loss_ref)
            dw_ref[...] = jnp.zeros_like(dw_ref)

        wv = w_ref[...]
        for e in range(bsz):
            x = h_ref[e]
            r = lax.rsqrt(jnp.mean(x * x, axis=-1, keepdims=True) + EPS)
            diff = jnp.where(j > 0, x * r * wv - t_ref[e], 0.0)
            loss_ref[...] += _sum_all(diff * diff) * (0.5 / d)
            dy = diff * (1.0 / d)
            g = dy * wv
            dh = r * (g - x * (r * r) * jnp.mean(g * x, axis=-1, keepdims=True))
            dh_ref[e] = dh
            dhb_ref[e] = dh.astype(BF16)
            dw_ref[...] += jnp.sum(dy * x * r, axis=0, keepdims=True)

    row = pl.BlockSpec((bsz, CHUNK, d), lambda j: (0, j, 0))
    return pl.pallas_call(
        body, name=name, grid=(nc,),
        in_specs=[row, pl.BlockSpec((bsz, CHUNK, d), lambda j: (0, jnp.maximum(j - 1, 0), 0)),
                  pl.BlockSpec((1, d), lambda j: (0, 0))],
        out_specs=[row, row, pl.BlockSpec((8, LANES), lambda j: (0, 0)), pl.BlockSpec((1, d), lambda j: (0, 0))],
        out_shape=[jax.ShapeDtypeStruct((bsz, t, d), F32), jax.ShapeDtypeStruct((bsz, t, d), BF16),
                   jax.ShapeDtypeStruct((8, LANES), F32), jax.ShapeDtypeStruct((1, d), F32)],
        compiler_params=_params("arbitrary"),
    )(h2, target, w)


def _pool_masks(j, transposed):
    r = lax.broadcasted_iota(jnp.int32, (CHUNK, 2 * CHUNK), 0)
    c = lax.broadcasted_iota(jnp.int32, (CHUNK, 2 * CHUNK), 1)
    masks = []
    for w in POOL_WINDOWS:
        if transposed:
            m = (c >= r) & (c < r + w)
        else:
            s = c - CHUNK
            m = (s <= r) & (s > r - w) & (s + j * CHUNK >= 0)
        masks.append(m.astype(F32))
    return masks


def _pool_count(t_global, w):
    return jnp.clip(t_global - PAD + 1, 1, w).astype(F32)


def _pool_fwd(u, pool_w, pool_scale, *, name):
    bsz, t, _ = u.shape
    nc = t // CHUNK

    def body(prev_ref, cur_ref, pw_ref, sc_ref, o_ref):
        j = pl.program_id(0)
        masks = _pool_masks(j, False)
        tg = j * CHUNK + lax.broadcasted_iota(jnp.int32, (CHUNK, 1), 0)
        count = [_pool_count(tg, w) for w in POOL_WINDOWS]
        units = [(e, gi) for e in range(bsz) for gi in range(len(POOL_WINDOWS))]
        sl = lambda gi: pl.ds(gi * POOL_GROUP, POOL_GROUP)
        cur = {(e, gi): cur_ref[e, :, sl(gi)] for e, gi in units}
        both = {(e, gi): jnp.concatenate([prev_ref[e, :, sl(gi)], cur[e, gi]], axis=0) for e, gi in units}
        win = {(e, gi): _dot_exact(masks[gi], both[e, gi]) for e, gi in units}
        pooled = {(e, gi): win[e, gi] / count[gi] - cur[e, gi] for e, gi in units}
        mixed = {(e, gi): _dot(pooled[e, gi], pw_ref[gi]) for e, gi in units}
        for e, gi in units:
            o_ref[e, :, sl(gi)] = (mixed[e, gi] * sc_ref[:, sl(gi)]).astype(BF16)

    blk = lambda f: pl.BlockSpec((bsz, CHUNK, D_POOL), f)
    return pl.pallas_call(
        body, name=name, grid=(nc,),
        in_specs=[blk(lambda j: (0, jnp.maximum(j - 1, 0), 0)), blk(lambda j: (0, j, 0)),
                  pl.BlockSpec((4, POOL_GROUP, POOL_GROUP), lambda j: (0, 0, 0)),
                  pl.BlockSpec((1, D_POOL), lambda j: (0, 0))],
        out_specs=blk(lambda j: (0, j, 0)), out_shape=jax.ShapeDtypeStruct(u.shape, BF16),
        compiler_params=_params("parallel"),
    )(u, u, pool_w, pool_scale)


def _pool_bwd(u, dyp, pool_w, pool_scale, *, name):
    bsz, t, _ = u.shape
    nc = t // CHUNK

    def body(prev_ref, cur_ref, dy_ref, dyn_ref, pw_ref, sc_ref, du_ref, dpw_ref, dsc_ref):
        j = pl.program_id(0)

        @pl.when(j == 0)
        def _():
            dpw_ref[...] = jnp.zeros_like(dpw_ref)
            dsc_ref[...] = jnp.zeros_like(dsc_ref)

        fwd = _pool_masks(j, False)
        bwd = _pool_masks(j, True)
        tg = j * CHUNK + lax.broadcasted_iota(jnp.int32, (CHUNK, 1), 0)
        count = [_pool_count(tg, w) for w in POOL_WINDOWS]
        count_next = [_pool_count(tg + CHUNK, w) for w in POOL_WINDOWS]
        has_next = j < nc - 1
        groups = range(len(POOL_WINDOWS))
        units = [(e, gi) for e in range(bsz) for gi in groups]
        sl = lambda gi: pl.ds(gi * POOL_GROUP, POOL_GROUP)
        cur = {(e, gi): cur_ref[e, :, sl(gi)] for e, gi in units}
        both = {(e, gi): jnp.concatenate([prev_ref[e, :, sl(gi)], cur[e, gi]], axis=0) for e, gi in units}
        win = {(e, gi): _dot_exact(fwd[gi], both[e, gi]) for e, gi in units}
        pooled = {(e, gi): win[e, gi] / count[gi] - cur[e, gi] for e, gi in units}
        dy = {(e, gi): dy_ref[e, :, sl(gi)] for e, gi in units}
        mixed = {(e, gi): _dot(pooled[e, gi], pw_ref[gi]) for e, gi in units}
        dm = {(e, gi): dy[e, gi] * sc_ref[:, sl(gi)] for e, gi in units}
        dm_next = {(e, gi): jnp.where(has_next, dyn_ref[e, :, sl(gi)], 0.0) * sc_ref[:, sl(gi)] for e, gi in units}
        dpw = {(e, gi): _dot_tn(pooled[e, gi], dm[e, gi]) for e, gi in units}
        dpooled = {(e, gi): _dot_nt(dm[e, gi], pw_ref[gi]) for e, gi in units}
        dpooled_next = {(e, gi): _dot_nt(dm_next[e, gi], pw_ref[gi]) for e, gi in units}
        spread = {(e, gi): jnp.concatenate([dpooled[e, gi] / count[gi], dpooled_next[e, gi] / count_next[gi]], axis=0)
                  for e, gi in units}
        back = {(e, gi): _dot_exact(bwd[gi], spread[e, gi]) for e, gi in units}
        for e, gi in units:
            du_ref[e, :, sl(gi)] = (back[e, gi] - dpooled[e, gi]).astype(BF16)
        for gi in groups:
            dsc, dw = None, None
            for e in range(bsz):
                term = jnp.sum(dy[e, gi] * mixed[e, gi], axis=0, keepdims=True)
                dsc = term if dsc is None else dsc + term
                dw = dpw[e, gi] if dw is None else dw + dpw[e, gi]
            dsc_ref[:, sl(gi)] += dsc
            dpw_ref[gi] += dw

    blk = lambda f: pl.BlockSpec((bsz, CHUNK, D_POOL), f)
    return pl.pallas_call(
        body, name=name, grid=(nc,),
        in_specs=[blk(lambda j: (0, jnp.maximum(j - 1, 0), 0)), blk(lambda j: (0, j, 0)),
                  blk(lambda j: (0, j, 0)), blk(lambda j: (0, jnp.minimum(j + 1, nc - 1), 0)),
                  pl.BlockSpec((4, POOL_GROUP, POOL_GROUP), lambda j: (0, 0, 0)),
                  pl.BlockSpec((1, D_POOL), lambda j: (0, 0))],
        out_specs=[blk(lambda j: (0, j, 0)), pl.BlockSpec((4, POOL_GROUP, POOL_GROUP), lambda j: (0, 0, 0)),
                   pl.BlockSpec((1, D_POOL), lambda j: (0, 0))],
        out_shape=[jax.ShapeDtypeStruct(u.shape, BF16), jax.ShapeDtypeStruct((4, POOL_GROUP, POOL_GROUP), F32),
                   jax.ShapeDtypeStruct((1, D_POOL), F32)],
        compiler_params=_params("arbitrary"),
    )(u, u, dyp, dyp, pool_w, pool_scale)


CONV_SLAB = 512


def _conv_taps(tail, cur, keep_tail):
    ext = jnp.concatenate([jnp.where(keep_tail, tail, 0.0), cur], axis=0)
    return [(pltpu.roll(ext, CONV_W - 1 - k, 0) if k < CONV_W - 1 else ext)[8:] for k in range(CONV_W)]


def _conv_pre(taps, w_ref, b_ref, sl):
    acc = b_ref[:, sl]
    for k in range(CONV_W):
        acc = acc + w_ref[k:k + 1, sl] * taps[k]
    return acc


def _conv_fwd(xbc, conv_w, conv_b, *, name):
    bsz, t, c = xbc.shape
    nc = t // CHUNK

    def body(tail_ref, cur_ref, w_ref, b_ref, o_ref):
        keep = pl.program_id(1) > 0
        for c0 in range(0, c, CONV_SLAB):
            sl = pl.ds(c0, CONV_SLAB)
            pre = _conv_pre(_conv_taps(tail_ref[0, :, sl], cur_ref[0, :, sl], keep), w_ref, b_ref, sl)
            o_ref[0, :, sl] = (pre * _sigmoid(pre)).astype(BF16)

    return pl.pallas_call(
        body, name=name, grid=(bsz, nc),
        in_specs=[pl.BlockSpec((1, 8, c), lambda b, j: (b, jnp.maximum(j * (CHUNK // 8) - 1, 0), 0)),
                  pl.BlockSpec((1, CHUNK, c), lambda b, j: (b, j, 0)),
                  pl.BlockSpec((CONV_W, c), lambda b, j: (0, 0)), pl.BlockSpec((1, c), lambda b, j: (0, 0))],
        out_specs=pl.BlockSpec((1, CHUNK, c), lambda b, j: (b, j, 0)), out_shape=jax.ShapeDtypeStruct(xbc.shape, BF16),
        compiler_params=_params("parallel", "parallel"),
    )(xbc, xbc, conv_w, conv_b)


def _conv_bwd(xbc, dxs, db, dc, conv_w, conv_b, *, name):
    bsz, t, c = xbc.shape
    nc = t // CHUNK
    halo = 16
    rows = CHUNK + halo

    def body(tail_ref, cur_ref, head_ref, dxs_ref, db_ref, dc_ref, dxs_head, db_head, dc_head, w_ref, b_ref,
             dx_ref, dwb_ref):
        j = pl.program_id(1)

        @pl.when(j == 0)
        def _():
            dwb_ref[...] = jnp.zeros_like(dwb_ref)

        has_prev, has_next = j > 0, j < nc - 1
        for c0 in range(0, c, CONV_SLAB):
            sl = pl.ds(c0, CONV_SLAB)
            if c0 < D_SSM:
                dxc, dxc_next = dxs_ref[0, :, sl], dxs_head[0, :, sl]
            elif c0 < D_SSM + D_POOL:
                dxc, dxc_next = db_ref[0], db_head[0]
            else:
                dxc, dxc_next = dc_ref[0], dc_head[0]
            dxc = jnp.concatenate([dxc.astype(F32), jnp.where(has_next, dxc_next.astype(F32), 0.0)], axis=0)
            ext = jnp.concatenate([jnp.where(has_prev, tail_ref[0, :, sl], 0.0), cur_ref[0, :, sl],
                                   jnp.where(has_next, head_ref[0, :, sl], 0.0)], axis=0)
            taps = [(pltpu.roll(ext, CONV_W - 1 - k, 0) if k < CONV_W - 1 else ext)[8:] for k in range(CONV_W)]
            pre = _conv_pre(taps, w_ref, b_ref, sl)
            s = _sigmoid(pre)
            dpre = dxc * (s * (1.0 + pre * (1.0 - s)))
            acc = w_ref[CONV_W - 1:CONV_W, sl] * dpre[:CHUNK]
            for k in range(CONV_W - 1):
                up = CONV_W - 1 - k
                acc = acc + w_ref[k:k + 1, sl] * pltpu.roll(dpre, rows - up, 0)[:CHUNK]
            dx_ref[0, :, sl] = acc.astype(BF16)
            for k in range(CONV_W):
                dwb_ref[0, k:k + 1, sl] += jnp.sum(dpre[:CHUNK] * taps[k][:CHUNK], axis=0, keepdims=True)
            dwb_ref[0, CONV_W:CONV_W + 1, sl] += jnp.sum(dpre[:CHUNK], axis=0, keepdims=True)

    assert CONV_SLAB == D_POOL and D_SSM % CONV_SLAB == 0
    row = lambda width: pl.BlockSpec((1, CHUNK, width), lambda b, j: (b, j, 0))
    nxt = lambda width: pl.BlockSpec(
        (1, halo, width), lambda b, j: (b, jnp.minimum((j + 1) * (CHUNK // halo), t // halo - 1), 0))
    return pl.pallas_call(
        body, name=name, grid=(bsz, nc),
        in_specs=[pl.BlockSpec((1, 8, c), lambda b, j: (b, jnp.maximum(j * (CHUNK // 8) - 1, 0), 0)), row(c), nxt(c),
                  row(D_SSM), row(D_POOL), row(D_POOL), nxt(D_SSM), nxt(D_POOL), nxt(D_POOL),
                  pl.BlockSpec((CONV_W, c), lambda b, j: (0, 0)), pl.BlockSpec((1, c), lambda b, j: (0, 0))],
        out_specs=[row(c), pl.BlockSpec((1, 8, c), lambda b, j: (b, 0, 0))],
        out_shape=[jax.ShapeDtypeStruct(xbc.shape, BF16), jax.ShapeDtypeStruct((bsz, 8, c), F32)],
        compiler_params=_params("parallel", "arbitrary"),
    )(xbc, xbc, xbc, dxs, db, dc, dxs, db, dc, conv_w, conv_b)


def _dt_valid(j):
    lane = lax.broadcasted_iota(jnp.int32, (CHUNK, LANES), 1)
    row = lax.broadcasted_iota(jnp.int32, (CHUNK, LANES), 0)
    return (lane < HPG) & ((j > 0) | (row >= PAD))


def _ssd_prep(dtr, dtb, alog, *, name):
    bsz, t, _ = dtr.shape
    nc = t // CHUNK

    def body(dtr_ref, dtb_ref, alog_ref, dt_ref, acs_ref, tr_ref):
        j = pl.program_id(0)
        valid = _dt_valid(j)
        row = lax.broadcasted_iota(jnp.int32, (CHUNK, LANES), 0)
        lane = lax.broadcasted_iota(jnp.int32, (CHUNK, LANES), 1)
        tril = (row >= lane).astype(F32)
        units = [(e, g) for e in range(bsz) for g in range(N_GROUPS)]
        sl = lambda g: pl.ds(g * LANES, LANES)
        dt = {(e, g): jnp.where(valid, _softplus(dtr_ref[e, :, sl(g)] + dtb_ref[g]), 0.0) for e, g in units}
        acs = {(e, g): _dot_exact(tril, dt[e, g] * -jnp.exp(alog_ref[g])) for e, g in units}
        for e, g in units:
            dt_ref[e, :, sl(g)] = dt[e, g]
            acs_ref[e, :, sl(g)] = acs[e, g]
            tr_ref[e, 0, g, 0:8, :] = dt[e, g].T[0:8]
            tr_ref[e, 0, g, 8:16, :] = acs[e, g].T[0:8]

    blk = pl.BlockSpec((bsz, CHUNK, D_DT), lambda j: (0, j, 0))
    const = pl.BlockSpec((N_GROUPS, 1, LANES), lambda j: (0, 0, 0))
    return pl.pallas_call(
        body, name=name, grid=(nc,), in_specs=[blk, const, const],
        out_specs=[blk, blk, pl.BlockSpec((bsz, 1, N_GROUPS, 16, LANES), lambda j: (0, j, 0, 0, 0))],
        out_shape=[jax.ShapeDtypeStruct(dtr.shape, F32), jax.ShapeDtypeStruct(dtr.shape, F32),
                   jax.ShapeDtypeStruct((bsz, nc, N_GROUPS, 16, LANES), F32)],
        compiler_params=_params("parallel"),
    )(dtr, dtb, alog)


def _ssd_decay(dt, acs, tr):
    lane = lax.broadcasted_iota(jnp.int32, (CHUNK, LANES), 1)
    row = lax.broadcasted_iota(jnp.int32, (CHUNK, LANES), 0)
    return dict(lane=lane, row=row, dt=dt, causal=row >= lane, acs=acs, acs_t=tr[8:16], dt_t=tr[0:8],
                aend=acs[CHUNK - 1:CHUNK, :])


def _ssd_specs(bsz, nc, rev):
    ch = (lambda j: nc - 1 - j) if rev else (lambda j: j)
    return dict(
        xs=pl.BlockSpec((bsz, CHUNK, GW), lambda g, j: (0, ch(j), g)),
        bm=pl.BlockSpec((bsz, CHUNK, D_STATE), lambda g, j: (0, ch(j), D_SSM // D_STATE + g)),
        cm=pl.BlockSpec((bsz, CHUNK, D_STATE), lambda g, j: (0, ch(j), D_SSM // D_STATE + N_GROUPS + g)),
        lane_blk=pl.BlockSpec((bsz, CHUNK, LANES), lambda g, j: (0, ch(j), g)),
        grp_const=pl.BlockSpec((1, 1, LANES), lambda g, j: (g, 0, 0)),
        grp_vec=pl.BlockSpec((1, GW), lambda g, j: (0, g)),
        state=pl.BlockSpec((bsz, 1, D_STATE, GW), lambda g, j: (0, ch(j), 0, g)),
        tr=pl.BlockSpec((bsz, 1, 1, 16, LANES), lambda g, j: (0, ch(j), g, 0, 0)),
    )


def _ssd_fwd(xc, dt, acs, tr, z, dskip, normw, *, name, rider=None):
    bsz, t, _ = xc.shape
    nc = t // CHUNK
    sp = _ssd_specs(bsz, nc, False)

    def body(xs_ref, b_ref, c_ref, dt_ref, acs_ref, tr_ref, z_ref, dsk_ref, nw_ref, yn_ref, y_ref, sp_ref, s_ref):
        j = pl.program_id(1)

        @pl.when(j == 0)
        def _():
            s_ref[...] = jnp.zeros_like(s_ref)

        ex = range(bsz)
        units = [(e, r) for e in ex for r in range(HPG)]
        full = lambda v: jnp.broadcast_to(v, (CHUNK, LANES))
        pair = lambda r: pl.ds((r // 2) * LANES, LANES)
        q = [_ssd_decay(dt_ref[e], acs_ref[e], tr_ref[e, 0, 0]) for e in ex]
        for e in ex:
            sp_ref[e, 0] = s_ref[e]
        bm, cm = [b_ref[e] for e in ex], [c_ref[e] for e in ex]
        cb = [_dot_nt(cm[e], bm[e]) for e in ex]
        low = q[0]["lane"] < HEAD_DIM
        col = {(e, r): full(q[e]["acs"][:, r:r + 1]) for e, r in units}
        aend = {(e, r): q[e]["aend"][:, r:r + 1] for e, r in units}
        decay = {(e, r): jnp.exp(jnp.where(q[e]["causal"], col[e, r] - q[e]["acs_t"][r:r + 1, :], -jnp.inf))
                 for e, r in units}
        mp = {(e, r): cb[e] * decay[e, r] * q[e]["dt_t"][r:r + 1, :] for e, r in units}
        ce = {(e, r): cm[e] * jnp.exp(col[e, r]) for e, r in units}
        bk = {(e, r): bm[e] * (jnp.exp(aend[e, r] - col[e, r]) * full(q[e]["dt"][:, r:r + 1])) for e, r in units}
        xp = {(e, r): xs_ref[e, :, pair(r)] for e, r in units}
        s_old = {(e, r): s_ref[e, :, pair(r)] for e, r in units}
        y_h = {u: _dot(mp[u], xp[u]) + _dot(ce[u], s_old[u]) for u in units}
        s_h = {u: jnp.exp(aend[u]) * s_old[u] + _dot_tn(bk[u], xp[u]) for u in units}
        for e in ex:
            for r in range(0, HPG, 2):
                y_ref[e, :, pair(r)] = jnp.where(low, y_h[e, r], y_h[e, r + 1])
                s_ref[e, :, pair(r)] = jnp.where(low, s_h[e, r], s_h[e, r + 1])
        y = [y_ref[e] + dsk_ref[...] * xs_ref[e] for e in ex]
        zz = [z_ref[e] for e in ex]
        yg = [y[e] * (zz[e] * _sigmoid(zz[e])) for e in ex]
        rstd = [lax.rsqrt(jnp.mean(yg[e] * yg[e], axis=-1, keepdims=True) + EPS) for e in ex]
        for e in ex:
            y_ref[e] = y[e]
            yn_ref[e] = (yg[e] * rstd[e] * nw_ref[...]).astype(BF16)

    grid = (N_GROUPS, nc)
    ride = _Ride(rider, body, 9, 3, 1, grid)
    outs = pl.pallas_call(
        ride.body, name=name, grid=grid,
        in_specs=[sp["xs"], sp["bm"], sp["cm"], sp["lane_blk"], sp["lane_blk"], sp["tr"], sp["xs"],
                  sp["grp_vec"], sp["grp_vec"]] + ride.in_specs,
        out_specs=[sp["xs"], sp["xs"], sp["state"]] + ride.out_specs,
        out_shape=[jax.ShapeDtypeStruct((bsz, t, D_SSM), BF16), jax.ShapeDtypeStruct((bsz, t, D_SSM), F32),
                   jax.ShapeDtypeStruct((bsz, nc, D_STATE, D_SSM), F32)] + ride.out_shape,
        scratch_shapes=[pltpu.VMEM((bsz, D_STATE, GW), F32)] + ride.scratch,
        compiler_params=_params(*ride.semantics(("parallel", "arbitrary"))),
    )(xc, xc, xc, dt, acs, tr, z, dskip, normw, *ride.args)
    return outs[:3], outs[3:]


def _ssd_bwd(xc, dtr, dt, acs, tr, z, ypre, sprev, dyn, dtb, alog, dskip, normw, *, name, rider=None):
    bsz, t, _ = xc.shape
    nc = t // CHUNK
    sp = _ssd_specs(bsz, nc, True)

    def body(xs_ref, b_ref, c_ref, dtr_ref, dt_ref, acs_ref, tr_ref, z_ref, y_ref, sp_ref, dyn_ref, dtb_ref, alog_ref,
             dsk_ref, nw_ref, dz_ref, dxs_ref, db_ref, dc_ref, ddt_ref, dnw_ref, dsm_ref, ds_ref):
        j = pl.program_id(1)

        @pl.when(j == 0)
        def _():
            ds_ref[...] = jnp.zeros_like(ds_ref)
            dnw_ref[...] = jnp.zeros_like(dnw_ref)
            dsm_ref[...] = jnp.zeros_like(dsm_ref)

        ex = range(bsz)
        heads = range(HPG)
        units = [(e, r) for e in ex for r in heads]
        q = [_ssd_decay(dt_ref[e], acs_ref[e], tr_ref[e, 0, 0]) for e in ex]
        a = -jnp.exp(alog_ref[0])
        valid = _dt_valid(nc - 1 - j)
        lane, row = q[0]["lane"], q[0]["row"]
        lane1 = lane[0:1, :]
        nw = nw_ref[...]
        y, zz, dyn = [y_ref[e] for e in ex], [z_ref[e] for e in ex], [dyn_ref[e] for e in ex]
        sz = [_sigmoid(zz[e]) for e in ex]
        sil = [zz[e] * sz[e] for e in ex]
        yg = [y[e] * sil[e] for e in ex]
        rstd = [lax.rsqrt(jnp.mean(yg[e] * yg[e], axis=-1, keepdims=True) + EPS) for e in ex]
        gn = [dyn[e] * nw for e in ex]
        dyg = [rstd[e] * (gn[e] - yg[e] * (rstd[e] * rstd[e]) * jnp.mean(gn[e] * yg[e], axis=-1, keepdims=True))
               for e in ex]
        dy = [dyg[e] * sil[e] for e in ex]
        xs = [xs_ref[e] for e in ex]
        for e in ex:
            dnw_ref[e] += jnp.sum(dyn[e] * yg[e] * rstd[e], axis=0, keepdims=True)
            dz_ref[e] = (dyg[e] * y[e] * (sz[e] * (1.0 + zz[e] * (1.0 - sz[e])))).astype(BF16)
        dskip_cols = [jnp.sum(dy[e] * xs[e], axis=0, keepdims=True) for e in ex]

        bm, cm = [b_ref[e] for e in ex], [c_ref[e] for e in ex]
        cb = [_dot_nt(cm[e], bm[e]) for e in ex]
        zero = jnp.zeros((CHUNK, LANES), F32)
        full = lambda v: jnp.broadcast_to(v, (CHUNK, LANES))
        low = lane < HEAD_DIM
        half = [low if r % 2 == 0 else ~low for r in heads]
        sl = lambda v, r: v[:, (r // 2) * LANES:(r // 2 + 1) * LANES]
        pair = lambda r: pl.ds((r // 2) * LANES, LANES)
        col = {(e, r): full(q[e]["acs"][:, r:r + 1]) for e, r in units}
        dt_col = {(e, r): full(q[e]["dt"][:, r:r + 1]) for e, r in units}
        aend = {(e, r): q[e]["aend"][:, r:r + 1] for e, r in units}
        dt_row = {(e, r): q[e]["dt_t"][r:r + 1, :] for e, r in units}
        decay = {(e, r): jnp.exp(jnp.where(q[e]["causal"], col[e, r] - q[e]["acs_t"][r:r + 1, :], -jnp.inf))
                 for e, r in units}
        ea = {u: jnp.exp(col[u]) for u in units}
        dte = {u: jnp.exp(aend[u] - col[u]) for u in units}
        ed = {u: jnp.exp(aend[u]) for u in units}
        k = {u: dte[u] * dt_col[u] for u in units}
        mp = {(e, r): cb[e] * decay[e, r] * dt_row[e, r] for e, r in units}
        xp = {(e, r): sl(xs[e], r) for e, r in units}
        dym = {(e, r): jnp.where(half[r], sl(dy[e], r), 0.0) for e, r in units}
        s_old = {(e, r): sp_ref[e, 0, :, pair(r)] for e, r in units}
        ds_old = {(e, r): ds_ref[e, :, pair(r)] for e, r in units}
        dsm = {(e, r): jnp.where(half[r], ds_old[e, r], 0.0) for e, r in units}
        gmat = {u: _dot_nt(dym[u], xp[u]) for u in units}
        t1 = {u: _dot_nt(dym[u], s_old[u]) for u in units}
        dbs = {u: _dot_nt(xp[u], dsm[u]) for u in units}
        dx = {(e, r): _dot_tn(mp[e, r], dym[e, r]) + _dot(bm[e] * k[e, r], dsm[e, r]) for e, r in units}
        ds = {(e, r): _dot_tn(cm[e] * ea[e, r], dym[e, r]) for e, r in units}
        gd = {u: gmat[u] * decay[u] for u in units}
        w0 = {(e, r): gd[e, r] * cb[e] for e, r in units}
        cs0 = {u: jnp.sum(w0[u], axis=0, keepdims=True) for u in units}
        rs = {u: jnp.sum(w0[u] * dt_row[u], axis=1, keepdims=True) for u in units}
        qv = {(e, r): jnp.sum(cm[e] * t1[e, r], axis=1, keepdims=True) for e, r in units}
        dk = {(e, r): jnp.sum(bm[e] * dbs[e, r], axis=1, keepdims=True) for e, r in units}
        ddte = {u: dk[u] * dt_col[u] for u in units}
        d_aend = {u: _sum_all(dsm[u] * s_old[u]) * ed[u] + _sum_all(ddte[u][:, 0:1] * dte[u][:, 0:1]) for u in units}
        last_row = row == CHUNK - 1
        dacs_col = {u: rs[u] + qv[u] * ea[u] - ddte[u] * dte[u] + jnp.where(last_row, d_aend[u], 0.0) for u in units}
        triu = (lane >= row).astype(F32)
        for e in ex:
            dcb, dc_acc, db_acc = zero, zero, zero
            dacs, dacs_t, ddt, ddt_t = zero, zero, zero, zero
            dskip_row = jnp.zeros((1, LANES), F32)
            for r in heads:
                u = (e, r)
                dcb = dcb + gd[u] * dt_row[u]
                dc_acc = dc_acc + ea[u] * t1[u]
                db_acc = db_acc + k[u] * dbs[u]
                dacs = jnp.where(lane == r, dacs_col[u], dacs)
                ddt = jnp.where(lane == r, dk[u] * dte[u], ddt)
                dacs_t = jnp.where(row == r, -cs0[u] * dt_row[u], dacs_t)
                ddt_t = jnp.where(row == r, cs0[u], ddt_t)
                dsk = _sum_all(jnp.where(half[r][0:1, :], sl(dskip_cols[e], r), 0.0))
                dskip_row = dskip_row + jnp.where(lane1 == r, dsk, 0.0)
            for r in range(0, HPG, 2):
                dxs_ref[e, :, pair(r)] = (dx[e, r] + dx[e, r + 1] + sl(dy[e], r) * dsk_ref[:, pair(r)]).astype(BF16)
                ed_pair = jnp.where(lane1 < HEAD_DIM, ed[e, r], ed[e, r + 1])
                ds_ref[e, :, pair(r)] = ds[e, r] + ds[e, r + 1] + ed_pair * ds_old[e, r]
            dacs = dacs + dacs_t.T
            ddt = ddt + ddt_t.T
            dda = _dot_exact(triu, dacs)
            ddt = ddt + dda * a
            da = jnp.sum(dda * q[e]["dt"], axis=0, keepdims=True)
            draw = jnp.where(valid, ddt * _sigmoid(dtr_ref[e] + dtb_ref[0]), 0.0)
            ddt_ref[e] = draw.astype(BF16)
            dsm_ref[e, 0, 0:1, :] += dskip_row
            dsm_ref[e, 0, 1:2, :] += da * a
            dsm_ref[e, 0, 2:3, :] += jnp.sum(draw, axis=0, keepdims=True)
            dc_ref[e] = (dc_acc + _dot(dcb, bm[e])).astype(BF16)
            db_ref[e] = (db_acc + _dot_tn(dcb, cm[e])).astype(BF16)

    grp_out = pl.BlockSpec((bsz, CHUNK, D_STATE), lambda g, j: (0, nc - 1 - j, g))
    grid = (N_GROUPS, nc)
    ride = _Ride(rider, body, 15, 7, 1, grid)
    outs = pl.pallas_call(
        ride.body, name=name, grid=grid,
        in_specs=[sp["xs"], sp["bm"], sp["cm"], sp["lane_blk"], sp["lane_blk"], sp["lane_blk"], sp["tr"], sp["xs"],
                  sp["xs"], sp["state"], sp["xs"], sp["grp_const"], sp["grp_const"], sp["grp_vec"], sp["grp_vec"]]
        + ride.in_specs,
        out_specs=[sp["xs"], sp["xs"], grp_out, grp_out, sp["lane_blk"],
                   pl.BlockSpec((bsz, 1, GW), lambda g, j: (0, 0, g)),
                   pl.BlockSpec((bsz, 1, 8, LANES), lambda g, j: (0, g, 0, 0))] + ride.out_specs,
        out_shape=[jax.ShapeDtypeStruct((bsz, t, D_SSM), BF16), jax.ShapeDtypeStruct((bsz, t, D_SSM), BF16),
                   jax.ShapeDtypeStruct((bsz, t, N_GROUPS * D_STATE), BF16),
                   jax.ShapeDtypeStruct((bsz, t, N_GROUPS * D_STATE), BF16),
                   jax.ShapeDtypeStruct((bsz, t, D_DT), BF16), jax.ShapeDtypeStruct((bsz, 1, D_SSM), F32),
                   jax.ShapeDtypeStruct((bsz, N_GROUPS, 8, LANES), F32)] + ride.out_shape,
        scratch_shapes=[pltpu.VMEM((bsz, D_STATE, GW), F32)] + ride.scratch,
        compiler_params=_params(*ride.semantics(("parallel", "arbitrary"))),
    )(xc, xc, xc, dtr, dt, acs, tr, z, ypre, sprev, dyn, dtb, alog, dskip, normw, *ride.args)
    return outs[:7], outs[7:]


def _input_grad(dhn, h0, w, dres, seq, *, name):
    bsz, t, d = h0.shape
    nc = t // CHUNK

    def body(dy_ref, h_ref, w_ref, dres_ref, gx_ref, head_ref, dw_ref):
        j = pl.program_id(0)

        @pl.when(j == 0)
        def _():
            dw_ref[...] = jnp.zeros_like(dw_ref)

        for e in range(bsz):
            x, dyv = h_ref[e], dy_ref[e]
            r = lax.rsqrt(jnp.mean(x * x, axis=-1, keepdims=True) + EPS)
            g = dyv * w_ref[...]
            dx = r * (g - x * (r * r) * jnp.mean(g * x, axis=-1, keepdims=True)) + dres_ref[e]
            dw_ref[...] += jnp.sum(dyv * x * r, axis=0, keepdims=True)
            gx_ref[e] = dx

        @pl.when(j == 0)
        def _():
            head_ref[...] = gx_ref[...]

    row = pl.BlockSpec((bsz, CHUNK, d), lambda j: (0, j, 0))
    return pl.pallas_call(
        body, name=name, grid=(nc,),
        in_specs=[row, row, pl.BlockSpec((1, d), lambda j: (0, 0)), row],
        out_specs=[pl.BlockSpec((bsz, CHUNK, d), lambda j: (0, jnp.maximum(j - 1, 0), 0)),
                   pl.BlockSpec((bsz, CHUNK, d), lambda j: (0, 0, 0)), pl.BlockSpec((1, d), lambda j: (0, 0))],
        out_shape=[jax.ShapeDtypeStruct((bsz, seq, d), F32), jax.ShapeDtypeStruct((bsz, CHUNK, d), F32),
                   jax.ShapeDtypeStruct((1, d), F32)],
        compiler_params=_params("arbitrary"),
    )(dhn, h0, w, dres)


def _remote(src, dst, send_sem, recv_sem, dev):
    return pltpu.make_async_remote_copy(src_ref=src, dst_ref=dst, send_sem=send_sem, recv_sem=recv_sem,
                                        device_id=dev, device_id_type=MESH)


def _position():
    return lax.axis_index("x"), lax.axis_index("y"), lax.axis_index("c")


def _other_chips(pos):
    x, y, _ = pos
    return [(1 - x, y), (x, 1 - y), (1 - x, 1 - y)]


class _Gather:
    def __init__(self, arrs):
        n = len(arrs)
        self.args, self.n_in, self.n_out = list(arrs), n, n
        self.split = [a.ndim == 2 and a.shape[1] % (2 * LANES) == 0 for a in arrs]
        self.out_shape = [jax.ShapeDtypeStruct((4,) + a.shape, a.dtype) for a in arrs]
        self.scratch = [pltpu.SemaphoreType.DMA((3 * n,)), pltpu.SemaphoreType.DMA((3 * n,)),
                        pltpu.SemaphoreType.DMA((n,)), pltpu.SemaphoreType.DMA((3 * n,)),
                        pltpu.SemaphoreType.DMA((3 * n,))]

    def _copies(self, pos, ins, outs, sems):
        send_sems, recv_sems, loc_sems, pass_send_sems, pass_recv_sems = sems
        x, y, c = pos
        me, sibling = 2 * x + y, (x, y, 1 - c)
        local = [pltpu.make_async_copy(ins[i], outs[i].at[me], loc_sems.at[i]) for i in range(self.n_in)]
        sends, recvs, passes, pass_recvs = [], [], [], []
        for i in range(self.n_in):
            half = self.args[i].shape[1] // 2 if self.split[i] else None
            for k, (px, py) in enumerate(_other_chips(pos)):
                them = 2 * px + py
                sems_k = (send_sems.at[3 * i + k], recv_sems.at[3 * i + k], (px, py, c))
                if half is None:
                    sends.append(_remote(ins[i], outs[i].at[me], *sems_k))
                    recvs.append(_remote(ins[i], outs[i].at[them], *sems_k))
                    passes.append(None)
                    continue
                mine = pl.ds(pl.multiple_of(c * half, LANES), half)
                other = pl.ds(pl.multiple_of((1 - c) * half, LANES), half)
                sends.append(_remote(ins[i].at[:, mine], outs[i].at[me, :, mine], *sems_k))
                recvs.append(_remote(ins[i].at[:, mine], outs[i].at[them, :, mine], *sems_k))
                pass_k = (pass_send_sems.at[3 * i + k], pass_recv_sems.at[3 * i + k], sibling)
                passes.append(_remote(outs[i].at[them, :, mine], outs[i].at[them, :, mine], *pass_k))
                pass_recvs.append(_remote(outs[i].at[them, :, other], outs[i].at[them, :, other], *pass_k))
        return local, sends, recvs, passes, pass_recvs

    def start(self, pos, ins, outs, sems):
        local, sends = self._copies(pos, ins, outs, sems)[:2]
        for cp in local + sends:
            cp.start()

    def finish(self, pos, ins, outs, sems):
        local, sends, recvs, passes, pass_recvs = self._copies(pos, ins, outs, sems)
        for cp, onward in zip(recvs, passes):
            cp.wait_recv()
            if onward is not None:
                onward.start()
        for cp in pass_recvs:
            cp.wait_recv()
        for cp in sends + [p for p in passes if p is not None]:
            cp.wait_send()
        for cp in local:
            cp.wait()


class _Exchange:
    FLIPS = [(fx, fy, fc) for fx in (0, 1) for fy in (0, 1) for fc in (0, 1)][1:]

    def __init__(self, big, small=None):
        n = len(big)
        self.n_big, self.has_small = n, small is not None
        self.args = list(big) + ([small] if self.has_small else [])
        self.n_in = self.n_out = len(self.args)
        self.out_shape = [jax.ShapeDtypeStruct(a.shape, a.dtype) for a in big]
        self.scratch = [pltpu.SemaphoreType.DMA((max(3 * n, 1),)), pltpu.SemaphoreType.DMA((max(3 * n, 1),)),
                        pltpu.SemaphoreType.DMA((n + 1,))]
        if self.has_small:
            self.out_shape.append(jax.ShapeDtypeStruct((8,) + small.shape, small.dtype))
            self.scratch += [pltpu.SemaphoreType.DMA((7,)), pltpu.SemaphoreType.DMA((7,))]

    def _copies(self, pos, ins, outs, sems):
        x, y, c = pos
        me, me8 = 2 * x + y, 4 * x + 2 * y + c
        local, sends, recvs = [], [], []
        for i in range(self.n_big):
            local.append(pltpu.make_async_copy(ins[i].at[me], outs[i].at[me], sems[2].at[i]))
            for k, (px, py) in enumerate(_other_chips(pos)):
                sems_k = (sems[0].at[3 * i + k], sems[1].at[3 * i + k], (px, py, c))
                sends.append(_remote(ins[i].at[2 * px + py], outs[i].at[me], *sems_k))
                recvs.append(_remote(ins[i].at[me], outs[i].at[2 * px + py], *sems_k))
        if self.has_small:
            small, landed = ins[self.n_big], outs[self.n_big]
            local.append(pltpu.make_async_copy(small, landed.at[me8], sems[2].at[self.n_big]))
            for k, (fx, fy, fc) in enumerate(self.FLIPS):
                peer = (x ^ fx, y ^ fy, c ^ fc)
                sems_k = (sems[3].at[k], sems[4].at[k], peer)
                sends.append(_remote(small, landed.at[me8], *sems_k))
                recvs.append(_remote(small, landed.at[4 * peer[0] + 2 * peer[1] + peer[2]], *sems_k))
        return local, sends, recvs, [None] * len(recvs), []

    start = _Gather.start
    finish = _Gather.finish


class _Swap:
    def __init__(self, arrs):
        n = len(arrs)
        self.args, self.n_in, self.n_out = list(arrs), n, n
        self.out_shape = [jax.ShapeDtypeStruct(a.shape, a.dtype) for a in arrs]
        self.scratch = [pltpu.SemaphoreType.DMA((n,)), pltpu.SemaphoreType.DMA((n,))]

    def _copies(self, pos, ins, outs, sems):
        x, y, c = pos
        both = [_remote(ins[i], outs[i], sems[0].at[i], sems[1].at[i], (x, y, 1 - c)) for i in range(self.n_in)]
        return [], both, both, [None] * len(both), []

    start = _Gather.start
    finish = _Gather.finish


def _comm(rider, *, name):
    a, b = rider.n_in, rider.n_in + rider.n_out

    def body(*refs):
        pos = _position()
        rider.start(pos, refs[:a], refs[a:b], refs[b:])
        rider.finish(pos, refs[:a], refs[a:b], refs[b:])

    return pl.pallas_call(body, name=name, in_specs=[ANY] * rider.n_in, out_specs=[ANY] * rider.n_out,
                          out_shape=rider.out_shape, scratch_shapes=rider.scratch)(*rider.args)


class _Ride:
    def __init__(self, rider, body, n_in, n_out, n_scratch, grid):
        self.rider = rider
        self.args = rider.args if rider else []
        self.in_specs = [ANY] * rider.n_in if rider else []
        self.out_specs = [ANY] * rider.n_out if rider else []
        self.out_shape = rider.out_shape if rider else []
        self.scratch = rider.scratch if rider else []
        self.body = self._wrap(body, n_in, n_out, n_scratch, grid) if rider else body

    def semantics(self, sem):
        return ("arbitrary",) * len(sem) if self.rider else sem

    def _wrap(self, body, n_in, n_out, n_scratch, grid):
        rider = self.rider
        a = n_in
        b = a + rider.n_in
        c = b + n_out
        d = c + rider.n_out
        e = d + n_scratch

        def wrapped(*refs):
            pos = _position()
            ids = [pl.program_id(i) for i in range(len(grid))]
            first = functools.reduce(jnp.logical_and, [i == 0 for i in ids])
            last = functools.reduce(jnp.logical_and, [i == g - 1 for i, g in zip(ids, grid)])

            @pl.when(first)
            def _():
                rider.start(pos, refs[a:b], refs[c:d], refs[e:])

            body(*refs[:a], *refs[b:c], *refs[d:e])

            @pl.when(last)
            def _():
                rider.finish(pos, refs[a:b], refs[c:d], refs[e:])

        return wrapped


def _elementwise_tiles(r, c):
    if r % 8 == 0 and r * c > 65536:
        tm = _pick(r, (256, 128, 64, 16, 8))
        return (tm, c), r // tm, lambda i: (i, 0)
    if r % 8 and c % 256 == 0 and r * c > 65536:
        return (r, 256), c // 256, lambda i: (0, i)
    return (r, c), 1, lambda i: (0, 0)


def _chip_sum(landed, *, name):
    _, r, c = landed.shape
    blk, steps, at = _elementwise_tiles(r, c)

    def body(land_ref, o_ref):
        acc = land_ref[0].astype(F32)
        for jchip in range(1, 4):
            acc = acc + land_ref[jchip].astype(F32)
        o_ref[...] = acc

    return pl.pallas_call(
        body, name=name, grid=(steps,), in_specs=[pl.BlockSpec((4,) + blk, lambda i: (0,) + at(i))],
        out_specs=pl.BlockSpec(blk, at), out_shape=jax.ShapeDtypeStruct((r, c), F32),
        compiler_params=_params("parallel"),
    )(landed)


def _device_sum(parts, *, name):
    _, r, c = parts.shape

    def body(p_ref, o_ref):
        acc = p_ref[0]
        for d in range(1, 8):
            acc = acc + p_ref[d]
        o_ref[...] = acc

    return pl.pallas_call(body, name=name, out_shape=jax.ShapeDtypeStruct((r, c), F32))(parts)


def _adamw_math(w, g, m, v):
    m = ADAM_B1 * m + (1.0 - ADAM_B1) * g
    v = ADAM_B2 * v + (1.0 - ADAM_B2) * (g * g)
    m_hat = m / (1.0 - ADAM_B1 ** ADAM_STEP)
    v_hat = v / (1.0 - ADAM_B2 ** ADAM_STEP)
    return -ADAM_LR * (m_hat / (jnp.sqrt(v_hat) + ADAM_EPS) + ADAM_WD * w), m, v


def _adamw(w, g_parts, m, v, *, name):
    r, c = w.shape
    shape, steps, at = _elementwise_tiles(r, c)
    n_g = len(g_parts)

    def body(*refs):
        w_ref, m_ref, v_ref = refs[n_g:n_g + 3]
        g_ref, d_ref, nm_ref, nv_ref = refs[n_g + 3:]
        g = refs[0][...]
        for p in refs[1:n_g]:
            g = g + p[...]
        g_ref[...] = g
        d_ref[...], nm_ref[...], nv_ref[...] = _adamw_math(w_ref[...], g, m_ref[...], v_ref[...])

    blk = pl.BlockSpec(shape, at)
    return pl.pallas_call(
        body, name=name, grid=(steps,), in_specs=[blk] * (n_g + 3), out_specs=[blk] * 4,
        out_shape=[jax.ShapeDtypeStruct((r, c), F32)] * 4, compiler_params=_params("parallel"),
    )(*g_parts, w, m, v)


def _pad_heads(v):
    return jnp.pad(v.reshape(N_GROUPS, 1, HPG), ((0, 0), (0, 0), (0, LANES - HPG)))


def _unpad_heads(v):
    return v[:, :HPG].reshape(1, N_HEADS)


_SMALL_EARLY = [("pool_w", (512, 128)), ("pool_scale", (1, 512)), ("conv_w", (4, D_XBC)), ("conv_b", (1, D_XBC)),
                ("dt_bias", (1, N_HEADS)), ("a_log", (1, N_HEADS)), ("d_skip", (1, N_HEADS)), ("ssm_norm_w", (1, D_SSM)),
                ("norm_ffn_w", (1, 1024)), ("norm_f_w", (1, 1024))]
_SMALL_LATE = [("norm_mix_w", (1, 1024)), ("meta", (N_META, 1024)), ("loss", (1, 1))]


def _pack_small(grads, layout):
    rows = []
    for nm, shape in layout:
        flat = grads[nm].reshape(-1)
        rows.append(jnp.pad(flat, (0, (-flat.size) % LANES)).reshape(-1, LANES))
    packed = jnp.concatenate(rows, axis=0)
    return jnp.pad(packed, ((0, (-packed.shape[0]) % 8), (0, 0)))


def _unpack_small(packed, layout):
    out, r0 = {}, 0
    for nm, shape in layout:
        size = shape[0] * shape[1]
        nrow = -(-size // LANES)
        out[nm] = packed[r0:r0 + nrow].reshape(-1)[:size].reshape(shape)
        r0 += nrow
    return out


def kernel(x, meta, norm_mix_w, w_in, pool_w, pool_scale, conv_w, conv_b, dt_bias, a_log, d_skip, ssm_norm_w, w_out, norm_ffn_w, w_ff1, w_ff2, norm_f_w, loss_target, m_meta, m_norm_mix_w, m_w_in, m_pool_w, m_pool_scale, m_conv_w, m_conv_b, m_dt_bias, m_a_log, m_d_skip, m_ssm_norm_w, m_w_out, m_norm_ffn_w, m_w_ff1, m_w_ff2, m_norm_f_w, v_meta, v_norm_mix_w, v_w_in, v_pool_w, v_pool_scale, v_conv_w, v_conv_b, v_dt_bias, v_a_log, v_d_skip, v_ssm_norm_w, v_w_out, v_norm_ffn_w, v_w_ff1, v_w_ff2, v_norm_f_w):
    bsz, seq, d = x.shape
    t = seq + CHUNK
    n = bsz * t
    chip = 2 * lax.axis_index("x") + lax.axis_index("y")
    d_in = w_in.shape[2] * 4

    g_in, g_conv, g_meta = _comm(_Gather([w_in[0].T.astype(BF16), conv_w[0], meta]), name="gather_in")
    late_weights = _Gather([w_out[0].astype(BF16), w_ff1[0].astype(BF16), w_ff2[0].astype(BF16)])
    win = g_in.reshape(d_in, d)
    wu, wz = win[:D_POOL], win[D_POOL:D_POOL + D_SSM]
    wx = win[D_POOL + D_SSM:D_POOL + D_SSM + D_XBC]
    wdt = jnp.pad(win[D_POOL + D_SSM + D_XBC:].reshape(N_GROUPS, HPG, d),
                  ((0, 0), (0, LANES - HPG), (0, 0))).reshape(D_DT, d)
    convw = g_conv.transpose(1, 0, 2).reshape(CONV_W, D_XBC)
    meta_full = g_meta.transpose(1, 0, 2).reshape(N_META, d)
    dtb, alog = _pad_heads(dt_bias), _pad_heads(a_log)
    dskip = jnp.repeat(d_skip, HEAD_DIM, axis=1)
    poolw = pool_w[0]

    h0 = jnp.concatenate([jnp.zeros((bsz, PAD, d), F32), jnp.broadcast_to(meta_full[None], (bsz, N_META, d)), x], axis=1)
    h0f = h0.reshape(n, d)
    hn1 = _rms_fwd(h0f, norm_mix_w, name="norm_mix")
    u = _mm(hn1, wu, name="proj_u", nt=True)
    z = _mm(hn1, wz, name="proj_z", nt=True)
    xbc = _mm(hn1, wx, name="proj_xbc", nt=True)
    dtr = _mm(hn1, wdt, name="proj_dt", nt=True)
    ypool = _pool_fwd(u.reshape(bsz, t, D_POOL), poolw, pool_scale, name="pool_fwd")
    xbc3 = xbc.reshape(bsz, t, D_XBC)
    xc = _conv_fwd(xbc3, convw, conv_b, name="conv_fwd")
    z3, dtr3 = z.reshape(bsz, t, D_SSM), dtr.reshape(bsz, t, D_DT)
    dt3, acs3, tr3 = _ssd_prep(dtr3, dtb, alog, name="ssd_prep")
    (yn, ypre, sprev), (g_out, g_ff1, g_ff2) = _ssd_fwd(xc, dt3, acs3, tr3, z3, dskip, ssm_norm_w, name="ssd_fwd",
                                                        rider=late_weights)
    wo = g_out.reshape(D_POOL + D_SSM, d)
    wo_p, wo_s = wo[:D_POOL], wo[D_POOL:]
    w1 = g_ff1
    w2 = g_ff2.reshape(D_FF, d)
    ypool_f, yn_f = ypool.reshape(n, D_POOL), yn.reshape(n, D_SSM)
    add = lambda r, e: r + e
    h1 = _mm([ypool_f, yn_f], [wo_p, wo_s], name="out_proj", post=add, extras=(h0f,))
    hn2 = _rms_fwd(h1, norm_ffn_w, name="norm_ffn")
    act = _mm(hn2, w1, name="ff1", out_dtype=BF16)
    relu2 = lambda a: jnp.square(jnp.maximum(a, 0))
    h2 = _mm(act, w2, name="ff2", pre=relu2, post=add, extras=(h1,))
    dh2, dh2b, loss_acc, d_norm_f = _final_norm_loss(h2.reshape(bsz, t, d), loss_target, norm_f_w.reshape(1, d),
                                                     name="loss")

    dh2f, dh2bf = dh2.reshape(n, d), dh2b.reshape(n, d)
    dact = _mm(dh2bf, w2, name="ff2_bwd", nt=True, post=lambda r, a: r * (2.0 * jnp.maximum(a, 0).astype(F32)),
               extras=(act,), out_dtype=BF16)
    d_w2 = _mm_tn(act, dh2bf, name="ff2_dw", tk=2048, tn=1024, pre=relu2)
    d_w1 = _mm_tn(hn2, dact, name="ff1_dw", tk=1024, tn=2048, slab=D_FF // 4)
    dh1, dh1b, d_norm_ffn = _mm_rms_bwd(dact, w1, h1, norm_ffn_w, dh2f, name="ff1_bwd")
    dypool = _mm(dh1b, wo_p, name="out_pool_bwd", nt=True)
    dyn = _mm(dh1b, wo_s, name="out_ssm_bwd", nt=True)
    d_wo_p = _mm_tn(ypool_f, dh1b, name="out_pool_dw", tk=512, tn=1024)
    d_wo_s = _mm_tn(yn_f, dh1b, name="out_ssm_dw", tk=1536, tn=1024)
    big_late = [jnp.concatenate([d_wo_p, d_wo_s], axis=0).reshape(4, (D_POOL + D_SSM) // 4, d),
                d_w1, d_w2.reshape(4, D_FF // 4, d)]
    (dz, dxs, dbm, dcm, ddtr, d_nw, d_heads), landed_late = _ssd_bwd(
        xc, dtr3, dt3, acs3, tr3, z3, ypre, sprev, dyn.reshape(bsz, t, D_SSM), dtb, alog, dskip, ssm_norm_w, name="ssd_bwd",
        rider=_Exchange(big_late))
    dxbc, d_convwb = _conv_bwd(xbc3, dxs, dbm, dcm, convw, conv_b, name="conv_bwd")
    du, d_poolw, d_poolsc = _pool_bwd(u.reshape(bsz, t, D_POOL), dypool.reshape(bsz, t, D_POOL), poolw, pool_scale,
                                      name="pool_bwd")
    duf, dzf, dxbcf, ddtrf = du.reshape(n, D_POOL), dz.reshape(n, D_SSM), dxbc.reshape(n, D_XBC), ddtr.reshape(n, D_DT)
    heads = jnp.sum(d_heads, axis=0)
    small_early = _pack_small({
        "pool_w": d_poolw, "pool_scale": d_poolsc,
        "conv_w": jnp.sum(d_convwb[:, :CONV_W], axis=0), "conv_b": jnp.sum(d_convwb[:, CONV_W:CONV_W + 1], axis=0),
        "dt_bias": _unpad_heads(heads[:, 2]), "a_log": _unpad_heads(heads[:, 1]), "d_skip": _unpad_heads(heads[:, 0]),
        "ssm_norm_w": jnp.sum(d_nw, axis=0), "norm_ffn_w": d_norm_ffn, "norm_f_w": d_norm_f}, _SMALL_EARLY)
    d_wu = _mm_tn(duf, hn1, name="proj_u_dw", tk=512, tn=1024)
    d_wz = _mm_tn(dzf, hn1, name="proj_z_dw", tk=1536, tn=1024)
    d_wx, (early_all,) = _mm_tn(dxbcf, hn1, name="proj_xbc_dw", tk=1280, tn=1024, rider=_Exchange([], small_early))
    d_wdt = _mm_tn(ddtrf, hn1, name="proj_dt_dw", tk=512, tn=1024)
    d_win = jnp.concatenate([d_wu, d_wz, d_wx, d_wdt.reshape(N_GROUPS, LANES, d)[:, :HPG].reshape(N_HEADS, d)], axis=0)
    big_in = d_win.reshape(4, d_in // 4, d)
    dhn1, (landed_in,) = _mm([duf, dzf, dxbcf, ddtrf], [wu, wz, wx, wdt], name="proj_bwd",
                             rider=_Exchange([big_in]))
    grad_x, d_head_rows, d_norm_mix = _input_grad(
        dhn1.reshape(bsz, t, d), h0, norm_mix_w, dh1.reshape(bsz, t, d), seq, name="input_grad")

    landed = [landed_in] + list(landed_late)
    small_late = _pack_small({"norm_mix_w": d_norm_mix, "meta": jnp.sum(d_head_rows[:, PAD:], axis=0),
                              "loss": loss_acc[0:1, 0:1]}, _SMALL_LATE)
    (late_all,) = _comm(_Exchange([], small_late), name="exchange_small")
    mine = [_chip_sum(l, name=f"chip_sum_{i}") for i, l in enumerate(landed)]
    theirs = _comm(_Swap(mine), name="swap_cores")
    gsmall = {**_unpack_small(_device_sum(early_all, name="device_sum_early"), _SMALL_EARLY),
              **_unpack_small(_device_sum(late_all, name="device_sum_late"), _SMALL_LATE)}
    gsmall["conv_w"] = lax.dynamic_slice_in_dim(gsmall["conv_w"], chip * (D_XBC // 4), D_XBC // 4, axis=1)
    gsmall["meta"] = lax.dynamic_slice_in_dim(gsmall["meta"], chip * (d // 4), d // 4, axis=1)
    loss = gsmall["loss"][0, 0]

    given = dict(meta=(meta, m_meta, v_meta), norm_mix_w=(norm_mix_w, m_norm_mix_w, v_norm_mix_w),
                 w_in=(w_in, m_w_in, v_w_in), pool_w=(pool_w, m_pool_w, v_pool_w),
                 pool_scale=(pool_scale, m_pool_scale, v_pool_scale), conv_w=(conv_w, m_conv_w, v_conv_w),
                 conv_b=(conv_b, m_conv_b, v_conv_b), dt_bias=(dt_bias, m_dt_bias, v_dt_bias),
                 a_log=(a_log, m_a_log, v_a_log), d_skip=(d_skip, m_d_skip, v_d_skip),
                 ssm_norm_w=(ssm_norm_w, m_ssm_norm_w, v_ssm_norm_w), w_out=(w_out, m_w_out, v_w_out),
                 norm_ffn_w=(norm_ffn_w, m_norm_ffn_w, v_norm_ffn_w), w_ff1=(w_ff1, m_w_ff1, v_w_ff1),
                 w_ff2=(w_ff2, m_w_ff2, v_w_ff2), norm_f_w=(norm_f_w, m_norm_f_w, v_norm_f_w))
    big_names = ["w_in", "w_out", "w_ff1", "w_ff2"]
    results = {}
    for nm, (w, m, v) in given.items():
        if nm in big_names:
            i = big_names.index(nm)
            parts, shape2 = (mine[i], theirs[i]), mine[i].shape
        else:
            parts, shape2 = (gsmall[nm],), gsmall[nm].shape
        if nm == "w_in":
            outs = _adamw(w[0].T, parts, m[0].T, v[0].T, name=f"adamw_{nm}")
            results[nm] = [o.T[None] for o in outs]
        else:
            outs = _adamw(w.reshape(shape2), parts, m.reshape(shape2), v.reshape(shape2), name=f"adamw_{nm}")
            results[nm] = [o.reshape(w.shape) for o in outs]
    order = list(given)
    return (loss, grad_x, *[results[nm][0] for nm in order], *[results[nm][1] for nm in order],
            *[results[nm][2] for nm in order], *[results[nm][3] for nm in order])
```

```python
import functools

import jax
import jax.numpy as jnp
from jax import lax
from jax.experimental import pallas as pl
from jax.experimental.pallas import tpu as pltpu

F32 = jnp.float32
BF16 = jnp.bfloat16
MESH = pl.DeviceIdType.MESH
ANY = pl.BlockSpec(memory_space=pl.ANY)

D_MODEL = 1024
N_META = 16
CHUNK = 128
PAD = CHUNK - N_META
POOL_WINDOWS = (2, 4, 8, 16)
D_POOL = 512
POOL_GROUP = 128
D_SSM = 1536
N_HEADS = 24
N_GROUPS = 4
HPG = 6
HEAD_DIM = 64
D_STATE = 128
GW = HPG * HEAD_DIM
D_XBC = D_SSM + 2 * N_GROUPS * D_STATE
D_DT = N_GROUPS * 128
D_FF = 4096
CONV_W = 4
EPS = 1e-5
LANES = 128
VMEM_LIMIT = 56 * 1024 * 1024

ADAM_LR, ADAM_B1, ADAM_B2, ADAM_EPS, ADAM_WD, ADAM_STEP = 0.001, 0.9, 0.999, 1e-08, 0.01, 10


def _params(*sem):
    return pltpu.CompilerParams(dimension_semantics=sem, vmem_limit_bytes=VMEM_LIMIT)


def _pick(n, cands):
    for c in cands:
        if n % c == 0:
            return c
    raise ValueError(f"no block size for {n}")


def _dot(a, b):
    return jnp.dot(a.astype(BF16), b.astype(BF16), preferred_element_type=F32)


def _dot_nt(a, b):
    return lax.dot_general(a.astype(BF16), b.astype(BF16), (((1,), (1,)), ((), ())), preferred_element_type=F32)


def _dot_tn(a, b):
    return lax.dot_general(a.astype(BF16), b.astype(BF16), (((0,), (0,)), ((), ())), preferred_element_type=F32)


def _dot_exact(mask, x):
    m = mask.astype(BF16)
    hi = x.astype(BF16)
    r1 = x - hi.astype(F32)
    mid = r1.astype(BF16)
    lo = (r1 - mid.astype(F32)).astype(BF16)
    dot = lambda t: jnp.dot(m, t, preferred_element_type=F32)
    return dot(hi) + dot(mid) + dot(lo)


def _sigmoid(x):
    return 1.0 / (1.0 + jnp.exp(-x))


def _softplus(x):
    return jnp.maximum(x, 0.0) + jnp.log1p(jnp.exp(-jnp.abs(x)))


def _sum_all(x):
    return jnp.sum(jnp.sum(x, axis=1, keepdims=True), axis=0, keepdims=True)


ROW_TILES = (1056, 768, 704, 512, 384, 256, 128)
TILE_BUDGET = 28 * 1024 * 1024


def _row_tile(n, bytes_per_row, fixed_bytes, budget=TILE_BUDGET):
    for tm in ROW_TILES:
        if n % tm == 0 and 2 * (tm * bytes_per_row + fixed_bytes) <= budget:
            return tm
    raise ValueError(f"no row tile for {n}")


WIDE_BUDGET = 38 * 1024 * 1024


def _mm(a, w, *, name, tn=512, nt=False, pre=None, post=None, extras=(), out_dtype=F32, rider=None):
    a_list = list(a) if isinstance(a, (list, tuple)) else [a]
    w_list = list(w) if isinstance(w, (list, tuple)) else [w]
    n_a, n_ex = len(a_list), len(extras)
    n = a_list[0].shape[0]
    shard = w_list[0].shape[2] if w_list[0].ndim == 3 else None
    assert shard is None or (not nt and n_a == 1 and shard % tn == 0)
    m = w_list[0].shape[0] * shard if shard else w_list[0].shape[0] if nt else w_list[0].shape[1]
    tn = min(tn, m)
    size = lambda dt: jnp.dtype(dt).itemsize
    per_row = (sum(x.shape[1] * size(x.dtype) for x in a_list) + m * size(out_dtype)
               + sum(m * size(e.dtype) for e in extras))
    tm = _row_tile(n, per_row, sum(x.size * size(x.dtype) for x in w_list) // 2, WIDE_BUDGET)

    def body(*refs):
        a_refs, w_refs, ex_refs, o_ref = refs[:n_a], refs[n_a:2 * n_a], refs[2 * n_a:2 * n_a + n_ex], refs[2 * n_a + n_ex]
        avs = [(a_ref[...] if pre is None else pre(a_ref[...])).astype(BF16) for a_ref in a_refs]
        for c0 in range(0, m, tn):
            r = None
            for av, w_ref in zip(avs, w_refs):
                if shard:
                    term = _dot(av, w_ref[c0 // shard, :, c0 % shard:c0 % shard + tn])
                else:
                    term = _dot_nt(av, w_ref[c0:c0 + tn, :]) if nt else _dot(av, w_ref[:, c0:c0 + tn])
                r = term if r is None else r + term
            if post is not None:
                r = post(r, *[e[:, c0:c0 + tn] for e in ex_refs])
            o_ref[:, c0:c0 + tn] = r.astype(out_dtype)

    a_specs = [pl.BlockSpec((tm, x.shape[1]), lambda i: (i, 0)) for x in a_list]
    w_specs = [pl.BlockSpec(x.shape, lambda i, nd=x.ndim: (0,) * nd, pipeline_mode=pl.Buffered(1)) for x in w_list]
    blk = pl.BlockSpec((tm, m), lambda i: (i, 0))
    grid = (n // tm,)
    ride = _Ride(rider, body, 2 * n_a + n_ex, 1, 0, grid)
    outs = pl.pallas_call(
        ride.body, name=name, grid=grid,
        in_specs=a_specs + w_specs + [blk] * n_ex + ride.in_specs,
        out_specs=[blk] + ride.out_specs, out_shape=[jax.ShapeDtypeStruct((n, m), out_dtype)] + ride.out_shape,
        scratch_shapes=ride.scratch, compiler_params=_params(*ride.semantics(("parallel",))),
    )(*a_list, *w_list, *extras, *ride.args)
    return (outs[0], outs[1:]) if rider else outs[0]


def _mm_tn(a, g, *, name, tk, tn, pre=None, slab=None, rider=None):
    n, k = a.shape
    m = g.shape[1]
    tk, tn = min(tk, k), min(tn, m)
    tm = _row_tile(n, tk * jnp.dtype(a.dtype).itemsize + tn * jnp.dtype(g.dtype).itemsize, tk * tn * 4)
    steps = n // tm

    def body(a_ref, g_ref, o_ref, acc_ref):
        r = pl.program_id(2)

        @pl.when(r == 0)
        def _():
            acc_ref[...] = jnp.zeros_like(acc_ref)

        av = a_ref[...]
        if pre is not None:
            av = pre(av)
        if slab:
            for s in range(tn // slab):
                acc_ref[s] += _dot_tn(av, g_ref[:, s * slab:(s + 1) * slab])
        else:
            acc_ref[...] += _dot_tn(av, g_ref[...])

        @pl.when(r == steps - 1)
        def _():
            o_ref[...] = acc_ref[...].astype(BF16)

    if slab:
        block, out_spec = (tn // slab, tk, slab), pl.BlockSpec((tn // slab, tk, slab), lambda i, j, r: (j, i, 0))
        out_shape = jax.ShapeDtypeStruct((m // slab, k, slab), BF16)
    else:
        block, out_spec = (tk, tn), pl.BlockSpec((tk, tn), lambda i, j, r: (i, j))
        out_shape = jax.ShapeDtypeStruct((k, m), BF16)
    grid = (k // tk, m // tn, steps)
    ride = _Ride(rider, body, 2, 1, 1, grid)
    outs = pl.pallas_call(
        ride.body, name=name, grid=grid,
        in_specs=[pl.BlockSpec((tm, tk), lambda i, j, r: (r, i)), pl.BlockSpec((tm, tn), lambda i, j, r: (r, j))]
        + ride.in_specs,
        out_specs=[out_spec] + ride.out_specs, out_shape=[out_shape] + ride.out_shape,
        scratch_shapes=[pltpu.VMEM(block, F32)] + ride.scratch,
        compiler_params=_params(*ride.semantics(("parallel", "parallel", "arbitrary"))),
    )(a, g, *ride.args)
    return (outs[0], outs[1:]) if rider else outs[0]


def _mm_rms_bwd(a, w, h, w_norm, dres, *, name):
    n, k = a.shape
    d = h.shape[1]
    slabs, _, ks = w.shape
    tm = _row_tile(n, k * jnp.dtype(a.dtype).itemsize + d * (4 + 4 + 4 + 2), d * k, WIDE_BUDGET)

    def body(a_ref, w_ref, h_ref, wn_ref, dres_ref, dx_ref, dxb_ref, dw_ref):
        @pl.when(pl.program_id(0) == 0)
        def _():
            dw_ref[...] = jnp.zeros_like(dw_ref)

        dyv = None
        for s in range(slabs):
            part = _dot_nt(a_ref[:, s * ks:(s + 1) * ks], w_ref[s])
            dyv = part if dyv is None else dyv + part
        x = h_ref[...]
        r = lax.rsqrt(jnp.mean(x * x, axis=-1, keepdims=True) + EPS)
        g = dyv * wn_ref[...]
        dx = r * (g - x * (r * r) * jnp.mean(g * x, axis=-1, keepdims=True)) + dres_ref[...]
        dx_ref[...] = dx
        dxb_ref[...] = dx.astype(BF16)
        dw_ref[...] += jnp.sum(dyv * x * r, axis=0, keepdims=True)

    row = pl.BlockSpec((tm, d), lambda i: (i, 0))
    vec = pl.BlockSpec((1, d), lambda i: (0, 0))
    return pl.pallas_call(
        body, name=name, grid=(n // tm,),
        in_specs=[pl.BlockSpec((tm, k), lambda i: (i, 0)),
                  pl.BlockSpec(w.shape, lambda i: (0, 0, 0), pipeline_mode=pl.Buffered(1)), row, vec, row],
        out_specs=[row, row, vec],
        out_shape=[jax.ShapeDtypeStruct((n, d), F32), jax.ShapeDtypeStruct((n, d), BF16), jax.ShapeDtypeStruct((1, d), F32)],
        compiler_params=_params("arbitrary"),
    )(a, w, h, w_norm, dres)


def _rms_fwd(h, w, *, name):
    n, d = h.shape
    tm = _pick(n, (768, 512, 256, 128))

    def body(h_ref, w_ref, o_ref):
        x = h_ref[...]
        r = lax.rsqrt(jnp.mean(x * x, axis=-1, keepdims=True) + EPS)
        o_ref[...] = (x * r * w_ref[...]).astype(BF16)

    return pl.pallas_call(
        body, name=name, grid=(n // tm,),
        in_specs=[pl.BlockSpec((tm, d), lambda i: (i, 0)), pl.BlockSpec((1, d), lambda i: (0, 0))],
        out_specs=pl.BlockSpec((tm, d), lambda i: (i, 0)), out_shape=jax.ShapeDtypeStruct((n, d), BF16),
        compiler_params=_params("parallel"),
    )(h, w)


def _final_norm_loss(h2, target, w, *, name):
    bsz, t, d = h2.shape
    nc = t // CHUNK

    def body(h_ref, t_ref, w_ref, dh_ref, dhb_ref, loss_ref, dw_ref):
        j = pl.program_id(0)

        @pl.when(j == 0)
        def _():
            loss_ref[...] = jnp.zeros_like(loss_ref)
            dw_ref[...] = jnp.zeros_like(dw_ref)

        wv = w_ref[...]
        for e in range(bsz):
            x = h_ref[e]
            r = lax.rsqrt(jnp.mean(x * x, axis=-1, keepdims=True) + EPS)
            diff = jnp.where(j > 0, x * r * wv - t_ref[e], 0.0)
            loss_ref[...] += _sum_all(diff * diff) * (0.5 / d)
            dy = diff * (1.0 / d)
            g = dy * wv
            dh = r * (g - x * (r * r) * jnp.mean(g * x, axis=-1, keepdims=True))
            dh_ref[e] = dh
            dhb_ref[e] = dh.astype(BF16)
            dw_ref[...] += jnp.sum(dy * x * r, axis=0, keepdims=True)

    row = pl.BlockSpec((bsz, CHUNK, d), lambda j: (0, j, 0))
    return pl.pallas_call(
        body, name=name, grid=(nc,),
        in_specs=[row, pl.BlockSpec((bsz, CHUNK, d), lambda j: (0, jnp.maximum(j - 1, 0), 0)),
                  pl.BlockSpec((1, d), lambda j: (0, 0))],
        out_specs=[row, row, pl.BlockSpec((8, LANES), lambda j: (0, 0)), pl.BlockSpec((1, d), lambda j: (0, 0))],
        out_shape=[jax.ShapeDtypeStruct((bsz, t, d), F32), jax.ShapeDtypeStruct((bsz, t, d), BF16),
                   jax.ShapeDtypeStruct((8, LANES), F32), jax.ShapeDtypeStruct((1, d), F32)],
        compiler_params=_params("arbitrary"),
    )(h2, target, w)


def _pool_masks(j, transposed):
    r = lax.broadcasted_iota(jnp.int32, (CHUNK, 2 * CHUNK), 0)
    c = lax.broadcasted_iota(jnp.int32, (CHUNK, 2 * CHUNK), 1)
    masks = []
    for w in POOL_WINDOWS:
        if transposed:
            m = (c >= r) & (c < r + w)
        else:
            s = c - CHUNK
            m = (s <= r) & (s > r - w) & (s + j * CHUNK >= 0)
        masks.append(m.astype(F32))
    return masks


def _pool_count(t_global, w):
    return jnp.clip(t_global - PAD + 1, 1, w).astype(F32)


def _pool_fwd(u, pool_w, pool_scale, *, name):
    bsz, t, _ = u.shape
    nc = t // CHUNK

    def body(prev_ref, cur_ref, pw_ref, sc_ref, o_ref):
        j = pl.program_id(0)
        masks = _pool_masks(j, False)
        tg = j * CHUNK + lax.broadcasted_iota(jnp.int32, (CHUNK, 1), 0)
        count = [_pool_count(tg, w) for w in POOL_WINDOWS]
        units = [(e, gi) for e in range(bsz) for gi in range(len(POOL_WINDOWS))]
        sl = lambda gi: pl.ds(gi * POOL_GROUP, POOL_GROUP)
        cur = {(e, gi): cur_ref[e, :, sl(gi)] for e, gi in units}
        both = {(e, gi): jnp.concatenate([prev_ref[e, :, sl(gi)], cur[e, gi]], axis=0) for e, gi in units}
        win = {(e, gi): _dot_exact(masks[gi], both[e, gi]) for e, gi in units}
        pooled = {(e, gi): win[e, gi] / count[gi] - cur[e, gi] for e, gi in units}
        mixed = {(e, gi): _dot(pooled[e, gi], pw_ref[gi]) for e, gi in units}
        for e, gi in units:
            o_ref[e, :, sl(gi)] = (mixed[e, gi] * sc_ref[:, sl(gi)]).astype(BF16)

    blk = lambda f: pl.BlockSpec((bsz, CHUNK, D_POOL), f)
    return pl.pallas_call(
        body, name=name, grid=(nc,),
        in_specs=[blk(lambda j: (0, jnp.maximum(j - 1, 0), 0)), blk(lambda j: (0, j, 0)),
                  pl.BlockSpec((4, POOL_GROUP, POOL_GROUP), lambda j: (0, 0, 0)),
                  pl.BlockSpec((1, D_POOL), lambda j: (0, 0))],
        out_specs=blk(lambda j: (0, j, 0)), out_shape=jax.ShapeDtypeStruct(u.shape, BF16),
        compiler_params=_params("parallel"),
    )(u, u, pool_w, pool_scale)


def _pool_bwd(u, dyp, pool_w, pool_scale, *, name):
    bsz, t, _ = u.shape
    nc = t // CHUNK

    def body(prev_ref, cur_ref, dy_ref, dyn_ref, pw_ref, sc_ref, du_ref, dpw_ref, dsc_ref):
        j = pl.program_id(0)

        @pl.when(j == 0)
        def _():
            dpw_ref[...] = jnp.zeros_like(dpw_ref)
            dsc_ref[...] = jnp.zeros_like(dsc_ref)

        fwd = _pool_masks(j, False)
        bwd = _pool_masks(j, True)
        tg = j * CHUNK + lax.broadcasted_iota(jnp.int32, (CHUNK, 1), 0)
        count = [_pool_count(tg, w) for w in POOL_WINDOWS]
        count_next = [_pool_count(tg + CHUNK, w) for w in POOL_WINDOWS]
        has_next = j < nc - 1
        groups = range(len(POOL_WINDOWS))
        units = [(e, gi) for e in range(bsz) for gi in groups]
        sl = lambda gi: pl.ds(gi * POOL_GROUP, POOL_GROUP)
        cur = {(e, gi): cur_ref[e, :, sl(gi)] for e, gi in units}
        both = {(e, gi): jnp.concatenate([prev_ref[e, :, sl(gi)], cur[e, gi]], axis=0) for e, gi in units}
        win = {(e, gi): _dot_exact(fwd[gi], both[e, gi]) for e, gi in units}
        pooled = {(e, gi): win[e, gi] / count[gi] - cur[e, gi] for e, gi in units}
        dy = {(e, gi): dy_ref[e, :, sl(gi)] for e, gi in units}
        mixed = {(e, gi): _dot(pooled[e, gi], pw_ref[gi]) for e, gi in units}
        dm = {(e, gi): dy[e, gi] * sc_ref[:, sl(gi)] for e, gi in units}
        dm_next = {(e, gi): jnp.where(has_next, dyn_ref[e, :, sl(gi)], 0.0) * sc_ref[:, sl(gi)] for e, gi in units}
        dpw = {(e, gi): _dot_tn(pooled[e, gi], dm[e, gi]) for e, gi in units}
        dpooled = {(e, gi): _dot_nt(dm[e, gi], pw_ref[gi]) for e, gi in units}
        dpooled_next = {(e, gi): _dot_nt(dm_next[e, gi], pw_ref[gi]) for e, gi in units}
        spread = {(e, gi): jnp.concatenate([dpooled[e, gi] / count[gi], dpooled_next[e, gi] / count_next[gi]], axis=0)
                  for e, gi in units}
        back = {(e, gi): _dot_exact(bwd[gi], spread[e, gi]) for e, gi in units}
        for e, gi in units:
            du_ref[e, :, sl(gi)] = (back[e, gi] - dpooled[e, gi]).astype(BF16)
        for gi in groups:
            dsc, dw = None, None
            for e in range(bsz):
                term = jnp.sum(dy[e, gi] * mixed[e, gi], axis=0, keepdims=True)
                dsc = term if dsc is None else dsc + term
                dw = dpw[e, gi] if dw is None else dw + dpw[e, gi]
            dsc_ref[:, sl(gi)] += dsc
            dpw_ref[gi] += dw

    blk = lambda f: pl.BlockSpec((bsz, CHUNK, D_POOL), f)
    return pl.pallas_call(
        body, name=name, grid=(nc,),
        in_specs=[blk(lambda j: (0, jnp.maximum(j - 1, 0), 0)), blk(lambda j: (0, j, 0)),
                  blk(lambda j: (0, j, 0)), blk(lambda j: (0, jnp.minimum(j + 1, nc - 1), 0)),
                  pl.BlockSpec((4, POOL_GROUP, POOL_GROUP), lambda j: (0, 0, 0)),
                  pl.BlockSpec((1, D_POOL), lambda j: (0, 0))],
        out_specs=[blk(lambda j: (0, j, 0)), pl.BlockSpec((4, POOL_GROUP, POOL_GROUP), lambda j: (0, 0, 0)),
                   pl.BlockSpec((1, D_POOL), lambda j: (0, 0))],
        out_shape=[jax.ShapeDtypeStruct(u.shape, BF16), jax.ShapeDtypeStruct((4, POOL_GROUP, POOL_GROUP), F32),
                   jax.ShapeDtypeStruct((1, D_POOL), F32)],
        compiler_params=_params("arbitrary"),
    )(u, u, dyp, dyp, pool_w, pool_scale)


CONV_SLAB = 512


def _conv_taps(tail, cur, keep_tail):
    ext = jnp.concatenate([jnp.where(keep_tail, tail, 0.0), cur], axis=0)
    return [(pltpu.roll(ext, CONV_W - 1 - k, 0) if k < CONV_W - 1 else ext)[8:] for k in range(CONV_W)]


def _conv_pre(taps, w_ref, b_ref, sl):
    acc = b_ref[:, sl]
    for k in range(CONV_W):
        acc = acc + w_ref[k:k + 1, sl] * taps[k]
    return acc


def _proj_conv(hn, w, conv_w, conv_b, *, name):
    n, d = hn.shape
    c = w.shape[0]
    assert PAD >= CONV_W - 1
    tm = _row_tile(n, d * 2 + c * (4 + 2), c * d, WIDE_BUDGET)

    def body(hn_ref, w_ref, cw_ref, cb_ref, xbc_ref, xc_ref, tail_ref):
        @pl.when(pl.program_id(0) == 0)
        def _():
            tail_ref[...] = jnp.zeros_like(tail_ref)

        av = hn_ref[...]
        starts = list(range(0, c, CONV_SLAB))

        def project(c0):
            xbc_ref[:, pl.ds(c0, CONV_SLAB)] = _dot_nt(av, w_ref[c0:c0 + CONV_SLAB, :])

        def convolve(c0):
            sl = pl.ds(c0, CONV_SLAB)
            xb = xbc_ref[:, sl]
            pre = _conv_pre(_conv_taps(tail_ref[:, sl], xb, True), cw_ref, cb_ref, sl)
            xc_ref[:, sl] = (pre * _sigmoid(pre)).astype(BF16)
            tail_ref[:, sl] = xb[tm - 8:, :]

        project(starts[0])
        for c0, c_next in zip(starts, starts[1:] + [None]):
            if c_next is not None:
                project(c_next)
            convolve(c0)

    row = lambda width: pl.BlockSpec((tm, width), lambda i: (i, 0))
    return pl.pallas_call(
        body, name=name, grid=(n // tm,),
        in_specs=[row(d), pl.BlockSpec(w.shape, lambda i: (0, 0), pipeline_mode=pl.Buffered(1)),
                  pl.BlockSpec((CONV_W, c), lambda i: (0, 0)), pl.BlockSpec((1, c), lambda i: (0, 0))],
        out_specs=[row(c), row(c)],
        out_shape=[jax.ShapeDtypeStruct((n, c), F32), jax.ShapeDtypeStruct((n, c), BF16)],
        scratch_shapes=[pltpu.VMEM((8, c), F32)],
        compiler_params=_params("arbitrary"),
    )(hn, w, conv_w, conv_b)


def _conv_bwd(xbc, dxs, db, dc, conv_w, conv_b, *, name):
    bsz, t, c = xbc.shape
    nc = t // CHUNK
    halo = 16
    rows = CHUNK + halo

    def body(tail_ref, cur_ref, head_ref, dxs_ref, db_ref, dc_ref, dxs_head, db_head, dc_head, w_ref, b_ref,
             dx_ref, dwb_ref):
        j = pl.program_id(1)

        @pl.when(j == 0)
        def _():
            dwb_ref[...] = jnp.zeros_like(dwb_ref)

        has_prev, has_next = j > 0, j < nc - 1
        for c0 in range(0, c, CONV_SLAB):
            sl = pl.ds(c0, CONV_SLAB)
            if c0 < D_SSM:
                dxc, dxc_next = dxs_ref[0, :, sl], dxs_head[0, :, sl]
            elif c0 < D_SSM + D_POOL:
                dxc, dxc_next = db_ref[0], db_head[0]
            else:
                dxc, dxc_next = dc_ref[0], dc_head[0]
            dxc = jnp.concatenate([dxc.astype(F32), jnp.where(has_next, dxc_next.astype(F32), 0.0)], axis=0)
            ext = jnp.concatenate([jnp.where(has_prev, tail_ref[0, :, sl], 0.0), cur_ref[0, :, sl],
                                   jnp.where(has_next, head_ref[0, :, sl], 0.0)], axis=0)
            taps = [(pltpu.roll(ext, CONV_W - 1 - k, 0) if k < CONV_W - 1 else ext)[8:] for k in range(CONV_W)]
            pre = _conv_pre(taps, w_ref, b_ref, sl)
            s = _sigmoid(pre)
            dpre = dxc * (s * (1.0 + pre * (1.0 - s)))
            acc = w_ref[CONV_W - 1:CONV_W, sl] * dpre[:CHUNK]
            for k in range(CONV_W - 1):
                up = CONV_W - 1 - k
                acc = acc + w_ref[k:k + 1, sl] * pltpu.roll(dpre, rows - up, 0)[:CHUNK]
            dx_ref[0, :, sl] = acc.astype(BF16)
            for k in range(CONV_W):
                dwb_ref[0, k:k + 1, sl] += jnp.sum(dpre[:CHUNK] * taps[k][:CHUNK], axis=0, keepdims=True)
            dwb_ref[0, CONV_W:CONV_W + 1, sl] += jnp.sum(dpre[:CHUNK], axis=0, keepdims=True)

    assert CONV_SLAB == D_POOL and D_SSM % CONV_SLAB == 0
    row = lambda width: pl.BlockSpec((1, CHUNK, width), lambda b, j: (b, j, 0))
    nxt = lambda width: pl.BlockSpec(
        (1, halo, width), lambda b, j: (b, jnp.minimum((j + 1) * (CHUNK // halo), t // halo - 1), 0))
    return pl.pallas_call(
        body, name=name, grid=(bsz, nc),
        in_specs=[pl.BlockSpec((1, 8, c), lambda b, j: (b, jnp.maximum(j * (CHUNK // 8) - 1, 0), 0)), row(c), nxt(c),
                  row(D_SSM), row(D_POOL), row(D_POOL), nxt(D_SSM), nxt(D_POOL), nxt(D_POOL),
                  pl.BlockSpec((CONV_W, c), lambda b, j: (0, 0)), pl.BlockSpec((1, c), lambda b, j: (0, 0))],
        out_specs=[row(c), pl.BlockSpec((1, 8, c), lambda b, j: (b, 0, 0))],
        out_shape=[jax.ShapeDtypeStruct(xbc.shape, BF16), jax.ShapeDtypeStruct((bsz, 8, c), F32)],
        compiler_params=_params("parallel", "arbitrary"),
    )(xbc, xbc, xbc, dxs, db, dc, dxs, db, dc, conv_w, conv_b)


def _dt_valid(j):
    lane = lax.broadcasted_iota(jnp.int32, (CHUNK, LANES), 1)
    row = lax.broadcasted_iota(jnp.int32, (CHUNK, LANES), 0)
    return (lane < HPG) & ((j > 0) | (row >= PAD))


def _ssd_prep(dtr, dtb, alog, *, name):
    bsz, t, _ = dtr.shape
    nc = t // CHUNK

    def body(dtr_ref, dtb_ref, alog_ref, dt_ref, acs_ref, tr_ref):
        j = pl.program_id(0)
        valid = _dt_valid(j)
        row = lax.broadcasted_iota(jnp.int32, (CHUNK, LANES), 0)
        lane = lax.broadcasted_iota(jnp.int32, (CHUNK, LANES), 1)
        tril = (row >= lane).astype(F32)
        units = [(e, g) for e in range(bsz) for g in range(N_GROUPS)]
        sl = lambda g: pl.ds(g * LANES, LANES)
        dt = {(e, g): jnp.where(valid, _softplus(dtr_ref[e, :, sl(g)] + dtb_ref[g]), 0.0) for e, g in units}
        acs = {(e, g): _dot_exact(tril, dt[e, g] * -jnp.exp(alog_ref[g])) for e, g in units}
        for e, g in units:
            dt_ref[e, :, sl(g)] = dt[e, g]
            acs_ref[e, :, sl(g)] = acs[e, g]
            tr_ref[e, 0, g, 0:8, :] = dt[e, g].T[0:8]
            tr_ref[e, 0, g, 8:16, :] = acs[e, g].T[0:8]

    blk = pl.BlockSpec((bsz, CHUNK, D_DT), lambda j: (0, j, 0))
    const = pl.BlockSpec((N_GROUPS, 1, LANES), lambda j: (0, 0, 0))
    return pl.pallas_call(
        body, name=name, grid=(nc,), in_specs=[blk, const, const],
        out_specs=[blk, blk, pl.BlockSpec((bsz, 1, N_GROUPS, 16, LANES), lambda j: (0, j, 0, 0, 0))],
        out_shape=[jax.ShapeDtypeStruct(dtr.shape, F32), jax.ShapeDtypeStruct(dtr.shape, F32),
                   jax.ShapeDtypeStruct((bsz, nc, N_GROUPS, 16, LANES), F32)],
        compiler_params=_params("parallel"),
    )(dtr, dtb, alog)


def _ssd_decay(dt, acs, tr):
    lane = lax.broadcasted_iota(jnp.int32, (CHUNK, LANES), 1)
    row = lax.broadcasted_iota(jnp.int32, (CHUNK, LANES), 0)
    return dict(lane=lane, row=row, dt=dt, causal=row >= lane, acs=acs, acs_t=tr[8:16], dt_t=tr[0:8],
                aend=acs[CHUNK - 1:CHUNK, :])


def _ssd_specs(bsz, nc, rev):
    ch = (lambda j: nc - 1 - j) if rev else (lambda j: j)
    return dict(
        xs=pl.BlockSpec((bsz, CHUNK, GW), lambda g, j: (0, ch(j), g)),
        bm=pl.BlockSpec((bsz, CHUNK, D_STATE), lambda g, j: (0, ch(j), D_SSM // D_STATE + g)),
        cm=pl.BlockSpec((bsz, CHUNK, D_STATE), lambda g, j: (0, ch(j), D_SSM // D_STATE + N_GROUPS + g)),
        lane_blk=pl.BlockSpec((bsz, CHUNK, LANES), lambda g, j: (0, ch(j), g)),
        grp_const=pl.BlockSpec((1, 1, LANES), lambda g, j: (g, 0, 0)),
        grp_vec=pl.BlockSpec((1, GW), lambda g, j: (0, g)),
        state=pl.BlockSpec((bsz, 1, D_STATE, GW), lambda g, j: (0, ch(j), 0, g)),
        tr=pl.BlockSpec((bsz, 1, 1, 16, LANES), lambda g, j: (0, ch(j), g, 0, 0)),
    )


def _ssd_fwd(xc, dt, acs, tr, z, dskip, normw, *, name, rider=None):
    bsz, t, _ = xc.shape
    nc = t // CHUNK
    sp = _ssd_specs(bsz, nc, False)

    def body(xs_ref, b_ref, c_ref, dt_ref, acs_ref, tr_ref, z_ref, dsk_ref, nw_ref, yn_ref, y_ref, sp_ref, s_ref):
        j = pl.program_id(1)

        @pl.when(j == 0)
        def _():
            s_ref[...] = jnp.zeros_like(s_ref)

        ex = range(bsz)
        units = [(e, r) for e in ex for r in range(HPG)]
        full = lambda v: jnp.broadcast_to(v, (CHUNK, LANES))
        pair = lambda r: pl.ds((r // 2) * LANES, LANES)
        q = [_ssd_decay(dt_ref[e], acs_ref[e], tr_ref[e, 0, 0]) for e in ex]
        for e in ex:
            sp_ref[e, 0] = s_ref[e]
        bm, cm = [b_ref[e] for e in ex], [c_ref[e] for e in ex]
        cb = [_dot_nt(cm[e], bm[e]) for e in ex]
        low = q[0]["lane"] < HEAD_DIM
        col = {(e, r): full(q[e]["acs"][:, r:r + 1]) for e, r in units}
        aend = {(e, r): q[e]["aend"][:, r:r + 1] for e, r in units}
        decay = {(e, r): jnp.exp(jnp.where(q[e]["causal"], col[e, r] - q[e]["acs_t"][r:r + 1, :], -jnp.inf))
                 for e, r in units}
        mp = {(e, r): cb[e] * decay[e, r] * q[e]["dt_t"][r:r + 1, :] for e, r in units}
        ce = {(e, r): cm[e] * jnp.exp(col[e, r]) for e, r in units}
        bk = {(e, r): bm[e] * (jnp.exp(aend[e, r] - col[e, r]) * full(q[e]["dt"][:, r:r + 1])) for e, r in units}
        xp = {(e, r): xs_ref[e, :, pair(r)] for e, r in units}
        s_old = {(e, r): s_ref[e, :, pair(r)] for e, r in units}
        y_h = {u: _dot(mp[u], xp[u]) + _dot(ce[u], s_old[u]) for u in units}
        s_h = {u: jnp.exp(aend[u]) * s_old[u] + _dot_tn(bk[u], xp[u]) for u in units}
        for e in ex:
            for r in range(0, HPG, 2):
                y_ref[e, :, pair(r)] = jnp.where(low, y_h[e, r], y_h[e, r + 1])
                s_ref[e, :, pair(r)] = jnp.where(low, s_h[e, r], s_h[e, r + 1])
        y = [y_ref[e] + dsk_ref[...] * xs_ref[e] for e in ex]
        zz = [z_ref[e] for e in ex]
        yg = [y[e] * (zz[e] * _sigmoid(zz[e])) for e in ex]
        rstd = [lax.rsqrt(jnp.mean(yg[e] * yg[e], axis=-1, keepdims=True) + EPS) for e in ex]
        for e in ex:
            y_ref[e] = y[e]
            yn_ref[e] = (yg[e] * rstd[e] * nw_ref[...]).astype(BF16)

    grid = (N_GROUPS, nc)
    ride = _Ride(rider, body, 9, 3, 1, grid)
    outs = pl.pallas_call(
        ride.body, name=name, grid=grid,
        in_specs=[sp["xs"], sp["bm"], sp["cm"], sp["lane_blk"], sp["lane_blk"], sp["tr"], sp["xs"],
                  sp["grp_vec"], sp["grp_vec"]] + ride.in_specs,
        out_specs=[sp["xs"], sp["xs"], sp["state"]] + ride.out_specs,
        out_shape=[jax.ShapeDtypeStruct((bsz, t, D_SSM), BF16), jax.ShapeDtypeStruct((bsz, t, D_SSM), F32),
                   jax.ShapeDtypeStruct((bsz, nc, D_STATE, D_SSM), F32)] + ride.out_shape,
        scratch_shapes=[pltpu.VMEM((bsz, D_STATE, GW), F32)] + ride.scratch,
        compiler_params=_params(*ride.semantics(("parallel", "arbitrary"))),
    )(xc, xc, xc, dt, acs, tr, z, dskip, normw, *ride.args)
    return outs[:3], outs[3:]


def _ssd_bwd(xc, dtr, dt, acs, tr, z, ypre, sprev, dyn, dtb, alog, dskip, normw, *, name, rider=None):
    bsz, t, _ = xc.shape
    nc = t // CHUNK
    sp = _ssd_specs(bsz, nc, True)

    def body(xs_ref, b_ref, c_ref, dtr_ref, dt_ref, acs_ref, tr_ref, z_ref, y_ref, sp_ref, dyn_ref, dtb_ref, alog_ref,
             dsk_ref, nw_ref, dz_ref, dxs_ref, db_ref, dc_ref, ddt_ref, dnw_ref, dsm_ref, ds_ref):
        j = pl.program_id(1)

        @pl.when(j == 0)
        def _():
            ds_ref[...] = jnp.zeros_like(ds_ref)
            dnw_ref[...] = jnp.zeros_like(dnw_ref)
            dsm_ref[...] = jnp.zeros_like(dsm_ref)

        ex = range(bsz)
        heads = range(HPG)
        units = [(e, r) for e in ex for r in heads]
        q = [_ssd_decay(dt_ref[e], acs_ref[e], tr_ref[e, 0, 0]) for e in ex]
        a = -jnp.exp(alog_ref[0])
        valid = _dt_valid(nc - 1 - j)
        lane, row = q[0]["lane"], q[0]["row"]
        lane1 = lane[0:1, :]
        nw = nw_ref[...]
        y, zz, dyn = [y_ref[e] for e in ex], [z_ref[e] for e in ex], [dyn_ref[e] for e in ex]
        sz = [_sigmoid(zz[e]) for e in ex]
        sil = [zz[e] * sz[e] for e in ex]
        yg = [y[e] * sil[e] for e in ex]
        rstd = [lax.rsqrt(jnp.mean(yg[e] * yg[e], axis=-1, keepdims=True) + EPS) for e in ex]
        gn = [dyn[e] * nw for e in ex]
        dyg = [rstd[e] * (gn[e] - yg[e] * (rstd[e] * rstd[e]) * jnp.mean(gn[e] * yg[e], axis=-1, keepdims=True))
               for e in ex]
        dy = [dyg[e] * sil[e] for e in ex]
        xs = [xs_ref[e] for e in ex]
        for e in ex:
            dnw_ref[e] += jnp.sum(dyn[e] * yg[e] * rstd[e], axis=0, keepdims=True)
            dz_ref[e] = (dyg[e] * y[e] * (sz[e] * (1.0 + zz[e] * (1.0 - sz[e])))).astype(BF16)
        dskip_cols = [jnp.sum(dy[e] * xs[e], axis=0, keepdims=True) for e in ex]

        bm, cm = [b_ref[e] for e in ex], [c_ref[e] for e in ex]
        cb = [_dot_nt(cm[e], bm[e]) for e in ex]
        zero = jnp.zeros((CHUNK, LANES), F32)
        full = lambda v: jnp.broadcast_to(v, (CHUNK, LANES))
        low = lane < HEAD_DIM
        half = [low if r % 2 == 0 else ~low for r in heads]
        sl = lambda v, r: v[:, (r // 2) * LANES:(r // 2 + 1) * LANES]
        pair = lambda r: pl.ds((r // 2) * LANES, LANES)
        col = {(e, r): full(q[e]["acs"][:, r:r + 1]) for e, r in units}
        dt_col = {(e, r): full(q[e]["dt"][:, r:r + 1]) for e, r in units}
        aend = {(e, r): q[e]["aend"][:, r:r + 1] for e, r in units}
        dt_row = {(e, r): q[e]["dt_t"][r:r + 1, :] for e, r in units}
        decay = {(e, r): jnp.exp(jnp.where(q[e]["causal"], col[e, r] - q[e]["acs_t"][r:r + 1, :], -jnp.inf))
                 for e, r in units}
        ea = {u: jnp.exp(col[u]) for u in units}
        dte = {u: jnp.exp(aend[u] - col[u]) for u in units}
        ed = {u: jnp.exp(aend[u]) for u in units}
        k = {u: dte[u] * dt_col[u] for u in units}
        mp = {(e, r): cb[e] * decay[e, r] * dt_row[e, r] for e, r in units}
        xp = {(e, r): sl(xs[e], r) for e, r in units}
        dym = {(e, r): jnp.where(half[r], sl(dy[e], r), 0.0) for e, r in units}
        s_old = {(e, r): sp_ref[e, 0, :, pair(r)] for e, r in units}
        ds_old = {(e, r): ds_ref[e, :, pair(r)] for e, r in units}
        dsm = {(e, r): jnp.where(half[r], ds_old[e, r], 0.0) for e, r in units}
        gmat = {u: _dot_nt(dym[u], xp[u]) for u in units}
        t1 = {u: _dot_nt(dym[u], s_old[u]) for u in units}
        dbs = {u: _dot_nt(xp[u], dsm[u]) for u in units}
        dx = {(e, r): _dot_tn(mp[e, r], dym[e, r]) + _dot(bm[e] * k[e, r], dsm[e, r]) for e, r in units}
        ds = {(e, r): _dot_tn(cm[e] * ea[e, r], dym[e, r]) for e, r in units}
        gd = {u: gmat[u] * decay[u] for u in units}
        w0 = {(e, r): gd[e, r] * cb[e] for e, r in units}
        cs0 = {u: jnp.sum(w0[u], axis=0, keepdims=True) for u in units}
        rs = {u: jnp.sum(w0[u] * dt_row[u], axis=1, keepdims=True) for u in units}
        qv = {(e, r): jnp.sum(cm[e] * t1[e, r], axis=1, keepdims=True) for e, r in units}
        dk = {(e, r): jnp.sum(bm[e] * dbs[e, r], axis=1, keepdims=True) for e, r in units}
        ddte = {u: dk[u] * dt_col[u] for u in units}
        d_aend = {u: _sum_all(dsm[u] * s_old[u]) * ed[u] + _sum_all(ddte[u][:, 0:1] * dte[u][:, 0:1]) for u in units}
        last_row = row == CHUNK - 1
        dacs_col = {u: rs[u] + qv[u] * ea[u] - ddte[u] * dte[u] + jnp.where(last_row, d_aend[u], 0.0) for u in units}
        triu = (lane >= row).astype(F32)
        for e in ex:
            dcb, dc_acc, db_acc = zero, zero, zero
            dacs, dacs_t, ddt, ddt_t = zero, zero, zero, zero
            dskip_row = jnp.zeros((1, LANES), F32)
            for r in heads:
                u = (e, r)
                dcb = dcb + gd[u] * dt_row[u]
                dc_acc = dc_acc + ea[u] * t1[u]
                db_acc = db_acc + k[u] * dbs[u]
                dacs = jnp.where(lane == r, dacs_col[u], dacs)
                ddt = jnp.where(lane == r, dk[u] * dte[u], ddt)
                dacs_t = jnp.where(row == r, -cs0[u] * dt_row[u], dacs_t)
                ddt_t = jnp.where(row == r, cs0[u], ddt_t)
                dsk = _sum_all(jnp.where(half[r][0:1, :], sl(dskip_cols[e], r), 0.0))
                dskip_row = dskip_row + jnp.where(lane1 == r, dsk, 0.0)
            for r in range(0, HPG, 2):
                dxs_ref[e, :, pair(r)] = (dx[e, r] + dx[e, r + 1] + sl(dy[e], r) * dsk_ref[:, pair(r)]).astype(BF16)
                ed_pair = jnp.where(lane1 < HEAD_DIM, ed[e, r], ed[e, r + 1])
                ds_ref[e, :, pair(r)] = ds[e, r] + ds[e, r + 1] + ed_pair * ds_old[e, r]
            dacs = dacs + dacs_t.T
            ddt = ddt + ddt_t.T
            dda = _dot_exact(triu, dacs)
            ddt = ddt + dda * a
            da = jnp.sum(dda * q[e]["dt"], axis=0, keepdims=True)
            draw = jnp.where(valid, ddt * _sigmoid(dtr_ref[e] + dtb_ref[0]), 0.0)
            ddt_ref[e] = draw.astype(BF16)
            dsm_ref[e, 0, 0:1, :] += dskip_row
            dsm_ref[e, 0, 1:2, :] += da * a
            dsm_ref[e, 0, 2:3, :] += jnp.sum(draw, axis=0, keepdims=True)
            dc_ref[e] = (dc_acc + _dot(dcb, bm[e])).astype(BF16)
            db_ref[e] = (db_acc + _dot_tn(dcb, cm[e])).astype(BF16)

    grp_out = pl.BlockSpec((bsz, CHUNK, D_STATE), lambda g, j: (0, nc - 1 - j, g))
    grid = (N_GROUPS, nc)
    ride = _Ride(rider, body, 15, 7, 1, grid)
    outs = pl.pallas_call(
        ride.body, name=name, grid=grid,
        in_specs=[sp["xs"], sp["bm"], sp["cm"], sp["lane_blk"], sp["lane_blk"], sp["lane_blk"], sp["tr"], sp["xs"],
                  sp["xs"], sp["state"], sp["xs"], sp["grp_const"], sp["grp_const"], sp["grp_vec"], sp["grp_vec"]]
        + ride.in_specs,
        out_specs=[sp["xs"], sp["xs"], grp_out, grp_out, sp["lane_blk"],
                   pl.BlockSpec((bsz, 1, GW), lambda g, j: (0, 0, g)),
                   pl.BlockSpec((bsz, 1, 8, LANES), lambda g, j: (0, g, 0, 0))] + ride.out_specs,
        out_shape=[jax.ShapeDtypeStruct((bsz, t, D_SSM), BF16), jax.ShapeDtypeStruct((bsz, t, D_SSM), BF16),
                   jax.ShapeDtypeStruct((bsz, t, N_GROUPS * D_STATE), BF16),
                   jax.ShapeDtypeStruct((bsz, t, N_GROUPS * D_STATE), BF16),
                   jax.ShapeDtypeStruct((bsz, t, D_DT), BF16), jax.ShapeDtypeStruct((bsz, 1, D_SSM), F32),
                   jax.ShapeDtypeStruct((bsz, N_GROUPS, 8, LANES), F32)] + ride.out_shape,
        scratch_shapes=[pltpu.VMEM((bsz, D_STATE, GW), F32)] + ride.scratch,
        compiler_params=_params(*ride.semantics(("parallel", "arbitrary"))),
    )(xc, xc, xc, dtr, dt, acs, tr, z, ypre, sprev, dyn, dtb, alog, dskip, normw, *ride.args)
    return outs[:7], outs[7:]


def _input_grad(dhn, h0, w, dres, seq, *, name):
    bsz, t, d = h0.shape
    nc = t // CHUNK

    def body(dy_ref, h_ref, w_ref, dres_ref, gx_ref, head_ref, dw_ref):
        j = pl.program_id(0)

        @pl.when(j == 0)
        def _():
            dw_ref[...] = jnp.zeros_like(dw_ref)

        for e in range(bsz):
            x, dyv = h_ref[e], dy_ref[e]
            r = lax.rsqrt(jnp.mean(x * x, axis=-1, keepdims=True) + EPS)
            g = dyv * w_ref[...]
            dx = r * (g - x * (r * r) * jnp.mean(g * x, axis=-1, keepdims=True)) + dres_ref[e]
            dw_ref[...] += jnp.sum(dyv * x * r, axis=0, keepdims=True)
            gx_ref[e] = dx

        @pl.when(j == 0)
        def _():
            head_ref[...] = gx_ref[...]

    row = pl.BlockSpec((bsz, CHUNK, d), lambda j: (0, j, 0))
    return pl.pallas_call(
        body, name=name, grid=(nc,),
        in_specs=[row, row, pl.BlockSpec((1, d), lambda j: (0, 0)), row],
        out_specs=[pl.BlockSpec((bsz, CHUNK, d), lambda j: (0, jnp.maximum(j - 1, 0), 0)),
                   pl.BlockSpec((bsz, CHUNK, d), lambda j: (0, 0, 0)), pl.BlockSpec((1, d), lambda j: (0, 0))],
        out_shape=[jax.ShapeDtypeStruct((bsz, seq, d), F32), jax.ShapeDtypeStruct((bsz, CHUNK, d), F32),
                   jax.ShapeDtypeStruct((1, d), F32)],
        compiler_params=_params("arbitrary"),
    )(dhn, h0, w, dres)


def _remote(src, dst, send_sem, recv_sem, dev):
    return pltpu.make_async_remote_copy(src_ref=src, dst_ref=dst, send_sem=send_sem, recv_sem=recv_sem,
                                        device_id=dev, device_id_type=MESH)


def _position():
    return lax.axis_index("x"), lax.axis_index("y"), lax.axis_index("c")


def _other_chips(pos):
    x, y, _ = pos
    return [(1 - x, y), (x, 1 - y), (1 - x, 1 - y)]


class _Gather:
    def __init__(self, arrs):
        n = len(arrs)
        self.args, self.n_in, self.n_out = list(arrs), n, n
        self.split = [a.ndim == 2 and a.shape[1] % (2 * LANES) == 0 for a in arrs]
        self.out_shape = [jax.ShapeDtypeStruct((4,) + a.shape, a.dtype) for a in arrs]
        self.scratch = [pltpu.SemaphoreType.DMA((3 * n,)), pltpu.SemaphoreType.DMA((3 * n,)),
                        pltpu.SemaphoreType.DMA((n,)), pltpu.SemaphoreType.DMA((3 * n,)),
                        pltpu.SemaphoreType.DMA((3 * n,))]

    def _copies(self, pos, ins, outs, sems):
        send_sems, recv_sems, loc_sems, pass_send_sems, pass_recv_sems = sems
        x, y, c = pos
        me, sibling = 2 * x + y, (x, y, 1 - c)
        local = [pltpu.make_async_copy(ins[i], outs[i].at[me], loc_sems.at[i]) for i in range(self.n_in)]
        sends, recvs, passes, pass_recvs = [], [], [], []
        for i in range(self.n_in):
            half = self.args[i].shape[1] // 2 if self.split[i] else None
            for k, (px, py) in enumerate(_other_chips(pos)):
                them = 2 * px + py
                sems_k = (send_sems.at[3 * i + k], recv_sems.at[3 * i + k], (px, py, c))
                if half is None:
                    sends.append(_remote(ins[i], outs[i].at[me], *sems_k))
                    recvs.append(_remote(ins[i], outs[i].at[them], *sems_k))
                    passes.append(None)
                    continue
                mine = pl.ds(pl.multiple_of(c * half, LANES), half)
                other = pl.ds(pl.multiple_of((1 - c) * half, LANES), half)
                sends.append(_remote(ins[i].at[:, mine], outs[i].at[me, :, mine], *sems_k))
                recvs.append(_remote(ins[i].at[:, mine], outs[i].at[them, :, mine], *sems_k))
                pass_k = (pass_send_sems.at[3 * i + k], pass_recv_sems.at[3 * i + k], sibling)
                passes.append(_remote(outs[i].at[them, :, mine], outs[i].at[them, :, mine], *pass_k))
                pass_recvs.append(_remote(outs[i].at[them, :, other], outs[i].at[them, :, other], *pass_k))
        return local, sends, recvs, passes, pass_recvs

    def start(self, pos, ins, outs, sems):
        local, sends = self._copies(pos, ins, outs, sems)[:2]
        for cp in local + sends:
            cp.start()

    def finish(self, pos, ins, outs, sems):
        local, sends, recvs, passes, pass_recvs = self._copies(pos, ins, outs, sems)
        for cp, onward in zip(recvs, passes):
            cp.wait_recv()
            if onward is not None:
                onward.start()
        for cp in pass_recvs:
            cp.wait_recv()
        for cp in sends + [p for p in passes if p is not None]:
            cp.wait_send()
        for cp in local:
            cp.wait()


class _Exchange:
    FLIPS = [(fx, fy, fc) for fx in (0, 1) for fy in (0, 1) for fc in (0, 1)][1:]

    def __init__(self, big, small=None):
        n = len(big)
        self.n_big, self.has_small = n, small is not None
        self.args = list(big) + ([small] if self.has_small else [])
        self.n_in = self.n_out = len(self.args)
        self.out_shape = [jax.ShapeDtypeStruct(a.shape, a.dtype) for a in big]
        self.scratch = [pltpu.SemaphoreType.DMA((max(3 * n, 1),)), pltpu.SemaphoreType.DMA((max(3 * n, 1),)),
                        pltpu.SemaphoreType.DMA((n + 1,))]
        if self.has_small:
            self.out_shape.append(jax.ShapeDtypeStruct((8,) + small.shape, small.dtype))
            self.scratch += [pltpu.SemaphoreType.DMA((7,)), pltpu.SemaphoreType.DMA((7,))]

    def _copies(self, pos, ins, outs, sems):
        x, y, c = pos
        me, me8 = 2 * x + y, 4 * x + 2 * y + c
        local, sends, recvs = [], [], []
        for i in range(self.n_big):
            local.append(pltpu.make_async_copy(ins[i].at[me], outs[i].at[me], sems[2].at[i]))
            for k, (px, py) in enumerate(_other_chips(pos)):
                sems_k = (sems[0].at[3 * i + k], sems[1].at[3 * i + k], (px, py, c))
                sends.append(_remote(ins[i].at[2 * px + py], outs[i].at[me], *sems_k))
                recvs.append(_remote(ins[i].at[me], outs[i].at[2 * px + py], *sems_k))
        if self.has_small:
            small, landed = ins[self.n_big], outs[self.n_big]
            local.append(pltpu.make_async_copy(small, landed.at[me8], sems[2].at[self.n_big]))
            for k, (fx, fy, fc) in enumerate(self.FLIPS):
                peer = (x ^ fx, y ^ fy, c ^ fc)
                sems_k = (sems[3].at[k], sems[4].at[k], peer)
                sends.append(_remote(small, landed.at[me8], *sems_k))
                recvs.append(_remote(small, landed.at[4 * peer[0] + 2 * peer[1] + peer[2]], *sems_k))
        return local, sends, recvs, [None] * len(recvs), []

    start = _Gather.start
    finish = _Gather.finish


class _Swap:
    def __init__(self, arrs):
        n = len(arrs)
        self.args, self.n_in, self.n_out = list(arrs), n, n
        self.out_shape = [jax.ShapeDtypeStruct(a.shape, a.dtype) for a in arrs]
        self.scratch = [pltpu.SemaphoreType.DMA((n,)), pltpu.SemaphoreType.DMA((n,))]

    def _copies(self, pos, ins, outs, sems):
        x, y, c = pos
        both = [_remote(ins[i], outs[i], sems[0].at[i], sems[1].at[i], (x, y, 1 - c)) for i in range(self.n_in)]
        return [], both, both, [None] * len(both), []

    start = _Gather.start
    finish = _Gather.finish


def _comm(rider, *, name):
    a, b = rider.n_in, rider.n_in + rider.n_out

    def body(*refs):
        pos = _position()
        rider.start(pos, refs[:a], refs[a:b], refs[b:])
        rider.finish(pos, refs[:a], refs[a:b], refs[b:])

    return pl.pallas_call(body, name=name, in_specs=[ANY] * rider.n_in, out_specs=[ANY] * rider.n_out,
                          out_shape=rider.out_shape, scratch_shapes=rider.scratch)(*rider.args)


class _Ride:
    def __init__(self, rider, body, n_in, n_out, n_scratch, grid):
        self.rider = rider
        self.args = rider.args if rider else []
        self.in_specs = [ANY] * rider.n_in if rider else []
        self.out_specs = [ANY] * rider.n_out if rider else []
        self.out_shape = rider.out_shape if rider else []
        self.scratch = rider.scratch if rider else []
        self.body = self._wrap(body, n_in, n_out, n_scratch, grid) if rider else body

    def semantics(self, sem):
        return ("arbitrary",) * len(sem) if self.rider else sem

    def _wrap(self, body, n_in, n_out, n_scratch, grid):
        rider = self.rider
        a = n_in
        b = a + rider.n_in
        c = b + n_out
        d = c + rider.n_out
        e = d + n_scratch

        def wrapped(*refs):
            pos = _position()
            ids = [pl.program_id(i) for i in range(len(grid))]
            first = functools.reduce(jnp.logical_and, [i == 0 for i in ids])
            last = functools.reduce(jnp.logical_and, [i == g - 1 for i, g in zip(ids, grid)])

            @pl.when(first)
            def _():
                rider.start(pos, refs[a:b], refs[c:d], refs[e:])

            body(*refs[:a], *refs[b:c], *refs[d:e])

            @pl.when(last)
            def _():
                rider.finish(pos, refs[a:b], refs[c:d], refs[e:])

        return wrapped


def _elementwise_tiles(r, c):
    if r % 8 == 0 and r * c > 65536:
        tm = _pick(r, (256, 128, 64, 16, 8))
        return (tm, c), r // tm, lambda i: (i, 0)
    if r % 8 and c % 256 == 0 and r * c > 65536:
        return (r, 256), c // 256, lambda i: (0, i)
    return (r, c), 1, lambda i: (0, 0)


def _chip_sum(landed, *, name):
    _, r, c = landed.shape
    blk, steps, at = _elementwise_tiles(r, c)

    def body(land_ref, o_ref):
        acc = land_ref[0].astype(F32)
        for jchip in range(1, 4):
            acc = acc + land_ref[jchip].astype(F32)
        o_ref[...] = acc

    return pl.pallas_call(
        body, name=name, grid=(steps,), in_specs=[pl.BlockSpec((4,) + blk, lambda i: (0,) + at(i))],
        out_specs=pl.BlockSpec(blk, at), out_shape=jax.ShapeDtypeStruct((r, c), F32),
        compiler_params=_params("parallel"),
    )(landed)


def _device_sum(parts, *, name):
    _, r, c = parts.shape

    def body(p_ref, o_ref):
        acc = p_ref[0]
        for d in range(1, 8):
            acc = acc + p_ref[d]
        o_ref[...] = acc

    return pl.pallas_call(body, name=name, out_shape=jax.ShapeDtypeStruct((r, c), F32))(parts)


def _adamw_math(w, g, m, v):
    m = ADAM_B1 * m + (1.0 - ADAM_B1) * g
    v = ADAM_B2 * v + (1.0 - ADAM_B2) * (g * g)
    m_hat = m / (1.0 - ADAM_B1 ** ADAM_STEP)
    v_hat = v / (1.0 - ADAM_B2 ** ADAM_STEP)
    return -ADAM_LR * (m_hat / (jnp.sqrt(v_hat) + ADAM_EPS) + ADAM_WD * w), m, v


def _adamw(w, g_parts, m, v, *, name):
    r, c = w.shape
    shape, steps, at = _elementwise_tiles(r, c)
    n_g = len(g_parts)

    def body(*refs):
        w_ref, m_ref, v_ref = refs[n_g:n_g + 3]
        g_ref, d_ref, nm_ref, nv_ref = refs[n_g + 3:]
        g = refs[0][...]
        for p in refs[1:n_g]:
            g = g + p[...]
        g_ref[...] = g
        d_ref[...], nm_ref[...], nv_ref[...] = _adamw_math(w_ref[...], g, m_ref[...], v_ref[...])

    blk = pl.BlockSpec(shape, at)
    return pl.pallas_call(
        body, name=name, grid=(steps,), in_specs=[blk] * (n_g + 3), out_specs=[blk] * 4,
        out_shape=[jax.ShapeDtypeStruct((r, c), F32)] * 4, compiler_params=_params("parallel"),
    )(*g_parts, w, m, v)


def _pad_heads(v):
    return jnp.pad(v.reshape(N_GROUPS, 1, HPG), ((0, 0), (0, 0), (0, LANES - HPG)))


def _unpad_heads(v):
    return v[:, :HPG].reshape(1, N_HEADS)


_SMALL_EARLY = [("pool_w", (512, 128)), ("pool_scale", (1, 512)), ("conv_w", (4, D_XBC)), ("conv_b", (1, D_XBC)),
                ("dt_bias", (1, N_HEADS)), ("a_log", (1, N_HEADS)), ("d_skip", (1, N_HEADS)), ("ssm_norm_w", (1, D_SSM)),
                ("norm_ffn_w", (1, 1024)), ("norm_f_w", (1, 1024))]
_SMALL_LATE = [("norm_mix_w", (1, 1024)), ("meta", (N_META, 1024)), ("loss", (1, 1))]


def _pack_small(grads, layout):
    rows = []
    for nm, shape in layout:
        flat = grads[nm].reshape(-1)
        rows.append(jnp.pad(flat, (0, (-flat.size) % LANES)).reshape(-1, LANES))
    packed = jnp.concatenate(rows, axis=0)
    return jnp.pad(packed, ((0, (-packed.shape[0]) % 8), (0, 0)))


def _unpack_small(packed, layout):
    out, r0 = {}, 0
    for nm, shape in layout:
        size = shape[0] * shape[1]
        nrow = -(-size // LANES)
        out[nm] = packed[r0:r0 + nrow].reshape(-1)[:size].reshape(shape)
        r0 += nrow
    return out


def kernel(x, meta, norm_mix_w, w_in, pool_w, pool_scale, conv_w, conv_b, dt_bias, a_log, d_skip, ssm_norm_w, w_out, norm_ffn_w, w_ff1, w_ff2, norm_f_w, loss_target, m_meta, m_norm_mix_w, m_w_in, m_pool_w, m_pool_scale, m_conv_w, m_conv_b, m_dt_bias, m_a_log, m_d_skip, m_ssm_norm_w, m_w_out, m_norm_ffn_w, m_w_ff1, m_w_ff2, m_norm_f_w, v_meta, v_norm_mix_w, v_w_in, v_pool_w, v_pool_scale, v_conv_w, v_conv_b, v_dt_bias, v_a_log, v_d_skip, v_ssm_norm_w, v_w_out, v_norm_ffn_w, v_w_ff1, v_w_ff2, v_norm_f_w):
    bsz, seq, d = x.shape
    t = seq + CHUNK
    n = bsz * t
    chip = 2 * lax.axis_index("x") + lax.axis_index("y")
    d_in = w_in.shape[2] * 4

    g_in, g_conv, g_meta = _comm(_Gather([w_in[0].T.astype(BF16), conv_w[0], meta]), name="gather_in")
    late_weights = _Gather([w_out[0].astype(BF16), w_ff1[0].astype(BF16), w_ff2[0].astype(BF16)])
    win = g_in.reshape(d_in, d)
    wu, wz = win[:D_POOL], win[D_POOL:D_POOL + D_SSM]
    wx = win[D_POOL + D_SSM:D_POOL + D_SSM + D_XBC]
    wdt = jnp.pad(win[D_POOL + D_SSM + D_XBC:].reshape(N_GROUPS, HPG, d),
                  ((0, 0), (0, LANES - HPG), (0, 0))).reshape(D_DT, d)
    convw = g_conv.transpose(1, 0, 2).reshape(CONV_W, D_XBC)
    meta_full = g_meta.transpose(1, 0, 2).reshape(N_META, d)
    dtb, alog = _pad_heads(dt_bias), _pad_heads(a_log)
    dskip = jnp.repeat(d_skip, HEAD_DIM, axis=1)
    poolw = pool_w[0]

    h0 = jnp.concatenate([jnp.zeros((bsz, PAD, d), F32), jnp.broadcast_to(meta_full[None], (bsz, N_META, d)), x], axis=1)
    h0f = h0.reshape(n, d)
    hn1 = _rms_fwd(h0f, norm_mix_w, name="norm_mix")
    u = _mm(hn1, wu, name="proj_u", nt=True)
    z = _mm(hn1, wz, name="proj_z", nt=True)
    xbc, xc = _proj_conv(hn1, wx, convw, conv_b, name="proj_xbc")
    dtr = _mm(hn1, wdt, name="proj_dt", nt=True)
    ypool = _pool_fwd(u.reshape(bsz, t, D_POOL), poolw, pool_scale, name="pool_fwd")
    xbc3 = xbc.reshape(bsz, t, D_XBC)
    xc = xc.reshape(bsz, t, D_XBC)
    z3, dtr3 = z.reshape(bsz, t, D_SSM), dtr.reshape(bsz, t, D_DT)
    dt3, acs3, tr3 = _ssd_prep(dtr3, dtb, alog, name="ssd_prep")
    (yn, ypre, sprev), (g_out, g_ff1, g_ff2) = _ssd_fwd(xc, dt3, acs3, tr3, z3, dskip, ssm_norm_w, name="ssd_fwd",
                                                        rider=late_weights)
    wo = g_out.reshape(D_POOL + D_SSM, d)
    wo_p, wo_s = wo[:D_POOL], wo[D_POOL:]
    w1 = g_ff1
    w2 = g_ff2.reshape(D_FF, d)
    ypool_f, yn_f = ypool.reshape(n, D_POOL), yn.reshape(n, D_SSM)
    add = lambda r, e: r + e
    h1 = _mm([ypool_f, yn_f], [wo_p, wo_s], name="out_proj", post=add, extras=(h0f,))
    hn2 = _rms_fwd(h1, norm_ffn_w, name="norm_ffn")
    act = _mm(hn2, w1, name="ff1", out_dtype=BF16)
    relu2 = lambda a: jnp.square(jnp.maximum(a, 0))
    h2 = _mm(act, w2, name="ff2", pre=relu2, post=add, extras=(h1,))
    dh2, dh2b, loss_acc, d_norm_f = _final_norm_loss(h2.reshape(bsz, t, d), loss_target, norm_f_w.reshape(1, d),
                                                     name="loss")

    dh2f, dh2bf = dh2.reshape(n, d), dh2b.reshape(n, d)
    dact = _mm(dh2bf, w2, name="ff2_bwd", nt=True, post=lambda r, a: r * (2.0 * jnp.maximum(a, 0).astype(F32)),
               extras=(act,), out_dtype=BF16)
    d_w2 = _mm_tn(act, dh2bf, name="ff2_dw", tk=2048, tn=1024, pre=relu2)
    d_w1 = _mm_tn(hn2, dact, name="ff1_dw", tk=1024, tn=2048, slab=D_FF // 4)
    dh1, dh1b, d_norm_ffn = _mm_rms_bwd(dact, w1, h1, norm_ffn_w, dh2f, name="ff1_bwd")
    dypool = _mm(dh1b, wo_p, name="out_pool_bwd", nt=True)
    dyn = _mm(dh1b, wo_s, name="out_ssm_bwd", nt=True)
    d_wo_p = _mm_tn(ypool_f, dh1b, name="out_pool_dw", tk=512, tn=1024)
    d_wo_s = _mm_tn(yn_f, dh1b, name="out_ssm_dw", tk=1536, tn=1024)
    big_late = [jnp.concatenate([d_wo_p, d_wo_s], axis=0).reshape(4, (D_POOL + D_SSM) // 4, d),
                d_w1, d_w2.reshape(4, D_FF // 4, d)]
    (dz, dxs, dbm, dcm, ddtr, d_nw, d_heads), landed_late = _ssd_bwd(
        xc, dtr3, dt3, acs3, tr3, z3, ypre, sprev, dyn.reshape(bsz, t, D_SSM), dtb, alog, dskip, ssm_norm_w, name="ssd_bwd",
        rider=_Exchange(big_late))
    dxbc, d_convwb = _conv_bwd(xbc3, dxs, dbm, dcm, convw, conv_b, name="conv_bwd")
    du, d_poolw, d_poolsc = _pool_bwd(u.reshape(bsz, t, D_POOL), dypool.reshape(bsz, t, D_POOL), poolw, pool_scale,
                                      name="pool_bwd")
    duf, dzf, dxbcf, ddtrf = du.reshape(n, D_POOL), dz.reshape(n, D_SSM), dxbc.reshape(n, D_XBC), ddtr.reshape(n, D_DT)
    heads = jnp.sum(d_heads, axis=0)
    small_early = _pack_small({
        "pool_w": d_poolw, "pool_scale": d_poolsc,
        "conv_w": jnp.sum(d_convwb[:, :CONV_W], axis=0), "conv_b": jnp.sum(d_convwb[:, CONV_W:CONV_W + 1], axis=0),
        "dt_bias": _unpad_heads(heads[:, 2]), "a_log": _unpad_heads(heads[:, 1]), "d_skip": _unpad_heads(heads[:, 0]),
        "ssm_norm_w": jnp.sum(d_nw, axis=0), "norm_ffn_w": d_norm_ffn, "norm_f_w": d_norm_f}, _SMALL_EARLY)
    d_wu = _mm_tn(duf, hn1, name="proj_u_dw", tk=512, tn=1024)
    d_wz = _mm_tn(dzf, hn1, name="proj_z_dw", tk=1536, tn=1024)
    d_wx, (early_all,) = _mm_tn(dxbcf, hn1, name="proj_xbc_dw", tk=1280, tn=1024, rider=_Exchange([], small_early))
    d_wdt = _mm_tn(ddtrf, hn1, name="proj_dt_dw", tk=512, tn=1024)
    d_win = jnp.concatenate([d_wu, d_wz, d_wx, d_wdt.reshape(N_GROUPS, LANES, d)[:, :HPG].reshape(N_HEADS, d)], axis=0)
    big_in = d_win.reshape(4, d_in // 4, d)
    dhn1, (landed_in,) = _mm([duf, dzf, dxbcf, ddtrf], [wu, wz, wx, wdt], name="proj_bwd",
                             rider=_Exchange([big_in]))
    grad_x, d_head_rows, d_norm_mix = _input_grad(
        dhn1.reshape(bsz, t, d), h0, norm_mix_w, dh1.reshape(bsz, t, d), seq, name="input_grad")

    landed = [landed_in] + list(landed_late)
    small_late = _pack_small({"norm_mix_w": d_norm_mix, "meta": jnp.sum(d_head_rows[:, PAD:], axis=0),
                              "loss": loss_acc[0:1, 0:1]}, _SMALL_LATE)
    (late_all,) = _comm(_Exchange([], small_late), name="exchange_small")
    mine = [_chip_sum(l, name=f"chip_sum_{i}") for i, l in enumerate(landed)]
    theirs = _comm(_Swap(mine), name="swap_cores")
    gsmall = {**_unpack_small(_device_sum(early_all, name="device_sum_early"), _SMALL_EARLY),
              **_unpack_small(_device_sum(late_all, name="device_sum_late"), _SMALL_LATE)}
    gsmall["conv_w"] = lax.dynamic_slice_in_dim(gsmall["conv_w"], chip * (D_XBC // 4), D_XBC // 4, axis=1)
    gsmall["meta"] = lax.dynamic_slice_in_dim(gsmall["meta"], chip * (d // 4), d // 4, axis=1)
    loss = gsmall["loss"][0, 0]

    given = dict(meta=(meta, m_meta, v_meta), norm_mix_w=(norm_mix_w, m_norm_mix_w, v_norm_mix_w),
                 w_in=(w_in, m_w_in, v_w_in), pool_w=(pool_w, m_pool_w, v_pool_w),
                 pool_scale=(pool_scale, m_pool_scale, v_pool_scale), conv_w=(conv_w, m_conv_w, v_conv_w),
                 conv_b=(conv_b, m_conv_b, v_conv_b), dt_bias=(dt_bias, m_dt_bias, v_dt_bias),
                 a_log=(a_log, m_a_log, v_a_log), d_skip=(d_skip, m_d_skip, v_d_skip),
                 ssm_norm_w=(ssm_norm_w, m_ssm_norm_w, v_ssm_norm_w), w_out=(w_out, m_w_out, v_w_out),
                 norm_ffn_w=(norm_ffn_w, m_norm_ffn_w, v_norm_ffn_w), w_ff1=(w_ff1, m_w_ff1, v_w_ff1),
                 w_ff2=(w_ff2, m_w_ff2, v_w_ff2), norm_f_w=(norm_f_w, m_norm_f_w, v_norm_f_w))
    big_names = ["w_in", "w_out", "w_ff1", "w_ff2"]
    results = {}
    for nm, (w, m, v) in given.items():
        if nm in big_names:
            i = big_names.index(nm)
            parts, shape2 = (mine[i], theirs[i]), mine[i].shape
        else:
            parts, shape2 = (gsmall[nm],), gsmall[nm].shape
        if nm == "w_in":
            outs = _adamw(w[0].T, parts, m[0].T, v[0].T, name=f"adamw_{nm}")
            results[nm] = [o.T[None] for o in outs]
        else:
            outs = _adamw(w.reshape(shape2), parts, m.reshape(shape2), v.reshape(shape2), name=f"adamw_{nm}")
            results[nm] = [o.reshape(w.shape) for o in outs]
    order = list(given)
    return (loss, grad_x, *[results[nm][0] for nm in order], *[results[nm][1] for nm in order],
            *[results[nm][2] for nm in order], *[results[nm][3] for nm in order])
```

```python
import functools

import jax
import jax.numpy as jnp
from jax import lax
from jax.experimental import pallas as pl
from jax.experimental.pallas import tpu as pltpu

F32 = jnp.float32
BF16 = jnp.bfloat16
MESH = pl.DeviceIdType.MESH
ANY = pl.BlockSpec(memory_space=pl.ANY)

D_MODEL = 1024
N_META = 16
CHUNK = 128
PAD = CHUNK - N_META
POOL_WINDOWS = (2, 4, 8, 16)
D_POOL = 512
POOL_GROUP = 128
D_SSM = 1536
N_HEADS = 24
N_GROUPS = 4
HPG = 6
HEAD_DIM = 64
D_STATE = 128
GW = HPG * HEAD_DIM
D_XBC = D_SSM + 2 * N_GROUPS * D_STATE
D_DT = N_GROUPS * 128
D_FF = 4096
CONV_W = 4
EPS = 1e-5
LANES = 128
VMEM_LIMIT = 56 * 1024 * 1024

ADAM_LR, ADAM_B1, ADAM_B2, ADAM_EPS, ADAM_WD, ADAM_STEP = 0.001, 0.9, 0.999, 1e-08, 0.01, 10


def _params(*sem):
    return pltpu.CompilerParams(dimension_semantics=sem, vmem_limit_bytes=VMEM_LIMIT)


def _pick(n, cands):
    for c in cands:
        if n % c == 0:
            return c
    raise ValueError(f"no block size for {n}")


def _dot(a, b):
    return jnp.dot(a.astype(BF16), b.astype(BF16), preferred_element_type=F32)


def _dot_nt(a, b):
    return lax.dot_general(a.astype(BF16), b.astype(BF16), (((1,), (1,)), ((), ())), preferred_element_type=F32)


def _dot_tn(a, b):
    return lax.dot_general(a.astype(BF16), b.astype(BF16), (((0,), (0,)), ((), ())), preferred_element_type=F32)


def _dot_exact(mask, x):
    m = mask.astype(BF16)
    hi = x.astype(BF16)
    r1 = x - hi.astype(F32)
    mid = r1.astype(BF16)
    lo = (r1 - mid.astype(F32)).astype(BF16)
    dot = lambda t: jnp.dot(m, t, preferred_element_type=F32)
    return dot(hi) + dot(mid) + dot(lo)


def _sigmoid(x):
    return 1.0 / (1.0 + jnp.exp(-x))


def _softplus(x):
    return jnp.maximum(x, 0.0) + jnp.log1p(jnp.exp(-jnp.abs(x)))


def _sum_all(x):
    return jnp.sum(jnp.sum(x, axis=1, keepdims=True), axis=0, keepdims=True)


ROW_TILES = (1056, 768, 704, 512, 384, 256, 128)
TILE_BUDGET = 28 * 1024 * 1024


def _row_tile(n, bytes_per_row, fixed_bytes, budget=TILE_BUDGET):
    for tm in ROW_TILES:
        if n % tm == 0 and 2 * (tm * bytes_per_row + fixed_bytes) <= budget:
            return tm
    raise ValueError(f"no row tile for {n}")


WIDE_BUDGET = 38 * 1024 * 1024


def _mm(a, w, *, name, tn=512, nt=False, pre=None, post=None, extras=(), out_dtype=F32, rider=None):
    a_list = list(a) if isinstance(a, (list, tuple)) else [a]
    w_list = list(w) if isinstance(w, (list, tuple)) else [w]
    n_a, n_ex = len(a_list), len(extras)
    n = a_list[0].shape[0]
    shard = w_list[0].shape[2] if w_list[0].ndim == 3 else None
    assert shard is None or (not nt and n_a == 1 and shard % tn == 0)
    m = w_list[0].shape[0] * shard if shard else w_list[0].shape[0] if nt else w_list[0].shape[1]
    tn = min(tn, m)
    size = lambda dt: jnp.dtype(dt).itemsize
    per_row = (sum(x.shape[1] * size(x.dtype) for x in a_list) + m * size(out_dtype)
               + sum(m * size(e.dtype) for e in extras))
    tm = _row_tile(n, per_row, sum(x.size * size(x.dtype) for x in w_list) // 2, WIDE_BUDGET)

    def body(*refs):
        a_refs, w_refs, ex_refs, o_ref = refs[:n_a], refs[n_a:2 * n_a], refs[2 * n_a:2 * n_a + n_ex], refs[2 * n_a + n_ex]
        avs = [(a_ref[...] if pre is None else pre(a_ref[...])).astype(BF16) for a_ref in a_refs]
        for c0 in range(0, m, tn):
            r = None
            for av, w_ref in zip(avs, w_refs):
                if shard:
                    term = _dot(av, w_ref[c0 // shard, :, c0 % shard:c0 % shard + tn])
                else:
                    term = _dot_nt(av, w_ref[c0:c0 + tn, :]) if nt else _dot(av, w_ref[:, c0:c0 + tn])
                r = term if r is None else r + term
            if post is not None:
                r = post(r, *[e[:, c0:c0 + tn] for e in ex_refs])
            o_ref[:, c0:c0 + tn] = r.astype(out_dtype)

    a_specs = [pl.BlockSpec((tm, x.shape[1]), lambda i: (i, 0)) for x in a_list]
    w_specs = [pl.BlockSpec(x.shape, lambda i, nd=x.ndim: (0,) * nd, pipeline_mode=pl.Buffered(1)) for x in w_list]
    blk = pl.BlockSpec((tm, m), lambda i: (i, 0))
    grid = (n // tm,)
    ride = _Ride(rider, body, 2 * n_a + n_ex, 1, 0, grid)
    outs = pl.pallas_call(
        ride.body, name=name, grid=grid,
        in_specs=a_specs + w_specs + [blk] * n_ex + ride.in_specs,
        out_specs=[blk] + ride.out_specs, out_shape=[jax.ShapeDtypeStruct((n, m), out_dtype)] + ride.out_shape,
        scratch_shapes=ride.scratch, compiler_params=_params(*ride.semantics(("parallel",))),
    )(*a_list, *w_list, *extras, *ride.args)
    return (outs[0], outs[1:]) if rider else outs[0]


def _mm_tn(a, g, *, name, tk, tn, pre=None, slab=None, rider=None):
    n, k = a.shape
    m = g.shape[1]
    tk, tn = min(tk, k), min(tn, m)
    tm = _row_tile(n, tk * jnp.dtype(a.dtype).itemsize + tn * jnp.dtype(g.dtype).itemsize, tk * tn * 4)
    steps = n // tm

    def body(a_ref, g_ref, o_ref, acc_ref):
        r = pl.program_id(2)

        @pl.when(r == 0)
        def _():
            acc_ref[...] = jnp.zeros_like(acc_ref)

        av = a_ref[...]
        if pre is not None:
            av = pre(av)
        if slab:
            for s in range(tn // slab):
                acc_ref[s] += _dot_tn(av, g_ref[:, s * slab:(s + 1) * slab])
        else:
            acc_ref[...] += _dot_tn(av, g_ref[...])

        @pl.when(r == steps - 1)
        def _():
            o_ref[...] = acc_ref[...].astype(BF16)

    if slab:
        block, out_spec = (tn // slab, tk, slab), pl.BlockSpec((tn // slab, tk, slab), lambda i, j, r: (j, i, 0))
        out_shape = jax.ShapeDtypeStruct((m // slab, k, slab), BF16)
    else:
        block, out_spec = (tk, tn), pl.BlockSpec((tk, tn), lambda i, j, r: (i, j))
        out_shape = jax.ShapeDtypeStruct((k, m), BF16)
    grid = (k // tk, m // tn, steps)
    ride = _Ride(rider, body, 2, 1, 1, grid)
    outs = pl.pallas_call(
        ride.body, name=name, grid=grid,
        in_specs=[pl.BlockSpec((tm, tk), lambda i, j, r: (r, i)), pl.BlockSpec((tm, tn), lambda i, j, r: (r, j))]
        + ride.in_specs,
        out_specs=[out_spec] + ride.out_specs, out_shape=[out_shape] + ride.out_shape,
        scratch_shapes=[pltpu.VMEM(block, F32)] + ride.scratch,
        compiler_params=_params(*ride.semantics(("parallel", "parallel", "arbitrary"))),
    )(a, g, *ride.args)
    return (outs[0], outs[1:]) if rider else outs[0]


def _mm_rms_bwd(a, w, h, w_norm, dres, *, name):
    n, k = a.shape
    d = h.shape[1]
    slabs, _, ks = w.shape
    tm = _row_tile(n, k * jnp.dtype(a.dtype).itemsize + d * (4 + 4 + 4 + 2), d * k, WIDE_BUDGET)

    def body(a_ref, w_ref, h_ref, wn_ref, dres_ref, dx_ref, dxb_ref, dw_ref):
        @pl.when(pl.program_id(0) == 0)
        def _():
            dw_ref[...] = jnp.zeros_like(dw_ref)

        dyv = None
        for s in range(slabs):
            part = _dot_nt(a_ref[:, s * ks:(s + 1) * ks], w_ref[s])
            dyv = part if dyv is None else dyv + part
        x = h_ref[...]
        r = lax.rsqrt(jnp.mean(x * x, axis=-1, keepdims=True) + EPS)
        g = dyv * wn_ref[...]
        dx = r * (g - x * (r * r) * jnp.mean(g * x, axis=-1, keepdims=True)) + dres_ref[...]
        dx_ref[...] = dx
        dxb_ref[...] = dx.astype(BF16)
        dw_ref[...] += jnp.sum(dyv * x * r, axis=0, keepdims=True)

    row = pl.BlockSpec((tm, d), lambda i: (i, 0))
    vec = pl.BlockSpec((1, d), lambda i: (0, 0))
    return pl.pallas_call(
        body, name=name, grid=(n // tm,),
        in_specs=[pl.BlockSpec((tm, k), lambda i: (i, 0)),
                  pl.BlockSpec(w.shape, lambda i: (0, 0, 0), pipeline_mode=pl.Buffered(1)), row, vec, row],
        out_specs=[row, row, vec],
        out_shape=[jax.ShapeDtypeStruct((n, d), F32), jax.ShapeDtypeStruct((n, d), BF16), jax.ShapeDtypeStruct((1, d), F32)],
        compiler_params=_params("arbitrary"),
    )(a, w, h, w_norm, dres)


def _rms_fwd(h, w, *, name):
    n, d = h.shape
    tm = _pick(n, (768, 512, 256, 128))

    def body(h_ref, w_ref, o_ref):
        x = h_ref[...]
        r = lax.rsqrt(jnp.mean(x * x, axis=-1, keepdims=True) + EPS)
        o_ref[...] = (x * r * w_ref[...]).astype(BF16)

    return pl.pallas_call(
        body, name=name, grid=(n // tm,),
        in_specs=[pl.BlockSpec((tm, d), lambda i: (i, 0)), pl.BlockSpec((1, d), lambda i: (0, 0))],
        out_specs=pl.BlockSpec((tm, d), lambda i: (i, 0)), out_shape=jax.ShapeDtypeStruct((n, d), BF16),
        compiler_params=_params("parallel"),
    )(h, w)


def _embed_norm(x, meta, w, *, name, rider=None):
    bsz, seq, d = x.shape
    t = seq + CHUNK
    nc = t // CHUNK

    def body(x_ref, meta_ref, w_ref, h_ref, hn_ref):
        j = pl.program_id(0)
        first = jnp.concatenate([jnp.zeros((PAD, d), F32), meta_ref[...]], axis=0)
        for e in range(bsz):
            h = jnp.where(j == 0, first, x_ref[e])
            r = lax.rsqrt(jnp.mean(h * h, axis=-1, keepdims=True) + EPS)
            h_ref[e] = h
            hn_ref[e] = (h * r * w_ref[...]).astype(BF16)

    row = pl.BlockSpec((bsz, CHUNK, d), lambda j: (0, j, 0))
    grid = (nc,)
    ride = _Ride(rider, body, 3, 2, 0, grid)
    outs = pl.pallas_call(
        ride.body, name=name, grid=grid,
        in_specs=[pl.BlockSpec((bsz, CHUNK, d), lambda j: (0, jnp.maximum(j - 1, 0), 0)),
                  pl.BlockSpec((N_META, d), lambda j: (0, 0)), pl.BlockSpec((1, d), lambda j: (0, 0))] + ride.in_specs,
        out_specs=[row, row] + ride.out_specs,
        out_shape=[jax.ShapeDtypeStruct((bsz, t, d), F32), jax.ShapeDtypeStruct((bsz, t, d), BF16)] + ride.out_shape,
        scratch_shapes=ride.scratch, compiler_params=_params(*ride.semantics(("parallel",))),
    )(x, meta, w, *ride.args)
    return outs[:2], outs[2:]


def _final_norm_loss(h2, target, w, *, name):
    bsz, t, d = h2.shape
    nc = t // CHUNK

    def body(h_ref, t_ref, w_ref, dh_ref, dhb_ref, loss_ref, dw_ref):
        j = pl.program_id(0)

        @pl.when(j == 0)
        def _():
            loss_ref[...] = jnp.zeros_like(loss_ref)
            dw_ref[...] = jnp.zeros_like(dw_ref)

        wv = w_ref[...]
        for e in range(bsz):
            x = h_ref[e]
            r = lax.rsqrt(jnp.mean(x * x, axis=-1, keepdims=True) + EPS)
            diff = jnp.where(j > 0, x * r * wv - t_ref[e], 0.0)
            loss_ref[...] += _sum_all(diff * diff) * (0.5 / d)
            dy = diff * (1.0 / d)
            g = dy * wv
            dh = r * (g - x * (r * r) * jnp.mean(g * x, axis=-1, keepdims=True))
            dh_ref[e] = dh
            dhb_ref[e] = dh.astype(BF16)
            dw_ref[...] += jnp.sum(dy * x * r, axis=0, keepdims=True)

    row = pl.BlockSpec((bsz, CHUNK, d), lambda j: (0, j, 0))
    return pl.pallas_call(
        body, name=name, grid=(nc,),
        in_specs=[row, pl.BlockSpec((bsz, CHUNK, d), lambda j: (0, jnp.maximum(j - 1, 0), 0)),
                  pl.BlockSpec((1, d), lambda j: (0, 0))],
        out_specs=[row, row, pl.BlockSpec((8, LANES), lambda j: (0, 0)), pl.BlockSpec((1, d), lambda j: (0, 0))],
        out_shape=[jax.ShapeDtypeStruct((bsz, t, d), F32), jax.ShapeDtypeStruct((bsz, t, d), BF16),
                   jax.ShapeDtypeStruct((8, LANES), F32), jax.ShapeDtypeStruct((1, d), F32)],
        compiler_params=_params("arbitrary"),
    )(h2, target, w)


def _pool_masks(j, transposed):
    r = lax.broadcasted_iota(jnp.int32, (CHUNK, 2 * CHUNK), 0)
    c = lax.broadcasted_iota(jnp.int32, (CHUNK, 2 * CHUNK), 1)
    masks = []
    for w in POOL_WINDOWS:
        if transposed:
            m = (c >= r) & (c < r + w)
        else:
            s = c - CHUNK
            m = (s <= r) & (s > r - w) & (s + j * CHUNK >= 0)
        masks.append(m.astype(F32))
    return masks


def _pool_count(t_global, w):
    return jnp.clip(t_global - PAD + 1, 1, w).astype(F32)


def _pool_fwd(u, pool_w, pool_scale, *, name):
    bsz, t, _ = u.shape
    nc = t // CHUNK

    def body(prev_ref, cur_ref, pw_ref, sc_ref, o_ref):
        j = pl.program_id(0)
        masks = _pool_masks(j, False)
        tg = j * CHUNK + lax.broadcasted_iota(jnp.int32, (CHUNK, 1), 0)
        count = [_pool_count(tg, w) for w in POOL_WINDOWS]
        units = [(e, gi) for e in range(bsz) for gi in range(len(POOL_WINDOWS))]
        sl = lambda gi: pl.ds(gi * POOL_GROUP, POOL_GROUP)
        cur = {(e, gi): cur_ref[e, :, sl(gi)] for e, gi in units}
        both = {(e, gi): jnp.concatenate([prev_ref[e, :, sl(gi)], cur[e, gi]], axis=0) for e, gi in units}
        win = {(e, gi): _dot_exact(masks[gi], both[e, gi]) for e, gi in units}
        pooled = {(e, gi): win[e, gi] / count[gi] - cur[e, gi] for e, gi in units}
        mixed = {(e, gi): _dot(pooled[e, gi], pw_ref[gi]) for e, gi in units}
        for e, gi in units:
            o_ref[e, :, sl(gi)] = (mixed[e, gi] * sc_ref[:, sl(gi)]).astype(BF16)

    blk = lambda f: pl.BlockSpec((bsz, CHUNK, D_POOL), f)
    return pl.pallas_call(
        body, name=name, grid=(nc,),
        in_specs=[blk(lambda j: (0, jnp.maximum(j - 1, 0), 0)), blk(lambda j: (0, j, 0)),
                  pl.BlockSpec((4, POOL_GROUP, POOL_GROUP), lambda j: (0, 0, 0)),
                  pl.BlockSpec((1, D_POOL), lambda j: (0, 0))],
        out_specs=blk(lambda j: (0, j, 0)), out_shape=jax.ShapeDtypeStruct(u.shape, BF16),
        compiler_params=_params("parallel"),
    )(u, u, pool_w, pool_scale)


def _pool_bwd(u, dyp, pool_w, pool_scale, *, name):
    bsz, t, _ = u.shape
    nc = t // CHUNK

    def body(prev_ref, cur_ref, dy_ref, dyn_ref, pw_ref, sc_ref, du_ref, dpw_ref, dsc_ref):
        j = pl.program_id(0)

        @pl.when(j == 0)
        def _():
            dpw_ref[...] = jnp.zeros_like(dpw_ref)
            dsc_ref[...] = jnp.zeros_like(dsc_ref)

        fwd = _pool_masks(j, False)
        bwd = _pool_masks(j, True)
        tg = j * CHUNK + lax.broadcasted_iota(jnp.int32, (CHUNK, 1), 0)
        count = [_pool_count(tg, w) for w in POOL_WINDOWS]
        count_next = [_pool_count(tg + CHUNK, w) for w in POOL_WINDOWS]
        has_next = j < nc - 1
        groups = range(len(POOL_WINDOWS))
        units = [(e, gi) for e in range(bsz) for gi in groups]
        sl = lambda gi: pl.ds(gi * POOL_GROUP, POOL_GROUP)
        cur = {(e, gi): cur_ref[e, :, sl(gi)] for e, gi in units}
        both = {(e, gi): jnp.concatenate([prev_ref[e, :, sl(gi)], cur[e, gi]], axis=0) for e, gi in units}
        win = {(e, gi): _dot_exact(fwd[gi], both[e, gi]) for e, gi in units}
        pooled = {(e, gi): win[e, gi] / count[gi] - cur[e, gi] for e, gi in units}
        dy = {(e, gi): dy_ref[e, :, sl(gi)] for e, gi in units}
        mixed = {(e, gi): _dot(pooled[e, gi], pw_ref[gi]) for e, gi in units}
        dm = {(e, gi): dy[e, gi] * sc_ref[:, sl(gi)] for e, gi in units}
        dm_next = {(e, gi): jnp.where(has_next, dyn_ref[e, :, sl(gi)], 0.0) * sc_ref[:, sl(gi)] for e, gi in units}
        dpw = {(e, gi): _dot_tn(pooled[e, gi], dm[e, gi]) for e, gi in units}
        dpooled = {(e, gi): _dot_nt(dm[e, gi], pw_ref[gi]) for e, gi in units}
        dpooled_next = {(e, gi): _dot_nt(dm_next[e, gi], pw_ref[gi]) for e, gi in units}
        spread = {(e, gi): jnp.concatenate([dpooled[e, gi] / count[gi], dpooled_next[e, gi] / count_next[gi]], axis=0)
                  for e, gi in units}
        back = {(e, gi): _dot_exact(bwd[gi], spread[e, gi]) for e, gi in units}
        for e, gi in units:
            du_ref[e, :, sl(gi)] = (back[e, gi] - dpooled[e, gi]).astype(BF16)
        for gi in groups:
            dsc, dw = None, None
            for e in range(bsz):
                term = jnp.sum(dy[e, gi] * mixed[e, gi], axis=0, keepdims=True)
                dsc = term if dsc is None else dsc + term
                dw = dpw[e, gi] if dw is None else dw + dpw[e, gi]
            dsc_ref[:, sl(gi)] += dsc
            dpw_ref[gi] += dw

    blk = lambda f: pl.BlockSpec((bsz, CHUNK, D_POOL), f)
    return pl.pallas_call(
        body, name=name, grid=(nc,),
        in_specs=[blk(lambda j: (0, jnp.maximum(j - 1, 0), 0)), blk(lambda j: (0, j, 0)),
                  blk(lambda j: (0, j, 0)), blk(lambda j: (0, jnp.minimum(j + 1, nc - 1), 0)),
                  pl.BlockSpec((4, POOL_GROUP, POOL_GROUP), lambda j: (0, 0, 0)),
                  pl.BlockSpec((1, D_POOL), lambda j: (0, 0))],
        out_specs=[blk(lambda j: (0, j, 0)), pl.BlockSpec((4, POOL_GROUP, POOL_GROUP), lambda j: (0, 0, 0)),
                   pl.BlockSpec((1, D_POOL), lambda j: (0, 0))],
        out_shape=[jax.ShapeDtypeStruct(u.shape, BF16), jax.ShapeDtypeStruct((4, POOL_GROUP, POOL_GROUP), F32),
                   jax.ShapeDtypeStruct((1, D_POOL), F32)],
        compiler_params=_params("arbitrary"),
    )(u, u, dyp, dyp, pool_w, pool_scale)


CONV_SLAB = 512


def _conv_taps(tail, cur, keep_tail):
    ext = jnp.concatenate([jnp.where(keep_tail, tail, 0.0), cur], axis=0)
    return [(pltpu.roll(ext, CONV_W - 1 - k, 0) if k < CONV_W - 1 else ext)[8:] for k in range(CONV_W)]


def _conv_pre(taps, w_ref, b_ref, sl):
    acc = b_ref[:, sl]
    for k in range(CONV_W):
        acc = acc + w_ref[k:k + 1, sl] * taps[k]
    return acc


def _proj_conv(hn, w, conv_w, conv_b, *, name):
    n, d = hn.shape
    c = w.shape[0]
    assert PAD >= CONV_W - 1
    tm = _row_tile(n, d * 2 + c * (4 + 2), c * d, WIDE_BUDGET)

    def body(hn_ref, w_ref, cw_ref, cb_ref, xbc_ref, xc_ref, tail_ref):
        @pl.when(pl.program_id(0) == 0)
        def _():
            tail_ref[...] = jnp.zeros_like(tail_ref)

        av = hn_ref[...]
        starts = list(range(0, c, CONV_SLAB))

        def project(c0):
            xbc_ref[:, pl.ds(c0, CONV_SLAB)] = _dot_nt(av, w_ref[c0:c0 + CONV_SLAB, :])

        def convolve(c0):
            sl = pl.ds(c0, CONV_SLAB)
            xb = xbc_ref[:, sl]
            pre = _conv_pre(_conv_taps(tail_ref[:, sl], xb, True), cw_ref, cb_ref, sl)
            xc_ref[:, sl] = (pre * _sigmoid(pre)).astype(BF16)
            tail_ref[:, sl] = xb[tm - 8:, :]

        project(starts[0])
        for c0, c_next in zip(starts, starts[1:] + [None]):
            if c_next is not None:
                project(c_next)
            convolve(c0)

    row = lambda width: pl.BlockSpec((tm, width), lambda i: (i, 0))
    return pl.pallas_call(
        body, name=name, grid=(n // tm,),
        in_specs=[row(d), pl.BlockSpec(w.shape, lambda i: (0, 0), pipeline_mode=pl.Buffered(1)),
                  pl.BlockSpec((CONV_W, c), lambda i: (0, 0)), pl.BlockSpec((1, c), lambda i: (0, 0))],
        out_specs=[row(c), row(c)],
        out_shape=[jax.ShapeDtypeStruct((n, c), F32), jax.ShapeDtypeStruct((n, c), BF16)],
        scratch_shapes=[pltpu.VMEM((8, c), F32)],
        compiler_params=_params("arbitrary"),
    )(hn, w, conv_w, conv_b)


def _conv_bwd(xbc, dxs, db, dc, conv_w, conv_b, *, name):
    bsz, t, c = xbc.shape
    nc = t // CHUNK
    halo = 16
    rows = CHUNK + halo

    def body(tail_ref, cur_ref, head_ref, dxs_ref, db_ref, dc_ref, dxs_head, db_head, dc_head, w_ref, b_ref,
             dx_ref, dwb_ref):
        j = pl.program_id(1)

        @pl.when(j == 0)
        def _():
            dwb_ref[...] = jnp.zeros_like(dwb_ref)

        has_prev, has_next = j > 0, j < nc - 1
        for c0 in range(0, c, CONV_SLAB):
            sl = pl.ds(c0, CONV_SLAB)
            if c0 < D_SSM:
                dxc, dxc_next = dxs_ref[0, :, sl], dxs_head[0, :, sl]
            elif c0 < D_SSM + D_POOL:
                dxc, dxc_next = db_ref[0], db_head[0]
            else:
                dxc, dxc_next = dc_ref[0], dc_head[0]
            dxc = jnp.concatenate([dxc.astype(F32), jnp.where(has_next, dxc_next.astype(F32), 0.0)], axis=0)
            ext = jnp.concatenate([jnp.where(has_prev, tail_ref[0, :, sl], 0.0), cur_ref[0, :, sl],
                                   jnp.where(has_next, head_ref[0, :, sl], 0.0)], axis=0)
            taps = [(pltpu.roll(ext, CONV_W - 1 - k, 0) if k < CONV_W - 1 else ext)[8:] for k in range(CONV_W)]
            pre = _conv_pre(taps, w_ref, b_ref, sl)
            s = _sigmoid(pre)
            dpre = dxc * (s * (1.0 + pre * (1.0 - s)))
            acc = w_ref[CONV_W - 1:CONV_W, sl] * dpre[:CHUNK]
            for k in range(CONV_W - 1):
                up = CONV_W - 1 - k
                acc = acc + w_ref[k:k + 1, sl] * pltpu.roll(dpre, rows - up, 0)[:CHUNK]
            dx_ref[0, :, sl] = acc.astype(BF16)
            for k in range(CONV_W):
                dwb_ref[0, k:k + 1, sl] += jnp.sum(dpre[:CHUNK] * taps[k][:CHUNK], axis=0, keepdims=True)
            dwb_ref[0, CONV_W:CONV_W + 1, sl] += jnp.sum(dpre[:CHUNK], axis=0, keepdims=True)

    assert CONV_SLAB == D_POOL and D_SSM % CONV_SLAB == 0
    row = lambda width: pl.BlockSpec((1, CHUNK, width), lambda b, j: (b, j, 0))
    nxt = lambda width: pl.BlockSpec(
        (1, halo, width), lambda b, j: (b, jnp.minimum((j + 1) * (CHUNK // halo), t // halo - 1), 0))
    return pl.pallas_call(
        body, name=name, grid=(bsz, nc),
        in_specs=[pl.BlockSpec((1, 8, c), lambda b, j: (b, jnp.maximum(j * (CHUNK // 8) - 1, 0), 0)), row(c), nxt(c),
                  row(D_SSM), row(D_POOL), row(D_POOL), nxt(D_SSM), nxt(D_POOL), nxt(D_POOL),
                  pl.BlockSpec((CONV_W, c), lambda b, j: (0, 0)), pl.BlockSpec((1, c), lambda b, j: (0, 0))],
        out_specs=[row(c), pl.BlockSpec((1, 8, c), lambda b, j: (b, 0, 0))],
        out_shape=[jax.ShapeDtypeStruct(xbc.shape, BF16), jax.ShapeDtypeStruct((bsz, 8, c), F32)],
        compiler_params=_params("parallel", "arbitrary"),
    )(xbc, xbc, xbc, dxs, db, dc, dxs, db, dc, conv_w, conv_b)


def _dt_valid(j):
    lane = lax.broadcasted_iota(jnp.int32, (CHUNK, LANES), 1)
    row = lax.broadcasted_iota(jnp.int32, (CHUNK, LANES), 0)
    return (lane < HPG) & ((j > 0) | (row >= PAD))


def _ssd_prep(dtr, dtb, alog, *, name):
    bsz, t, _ = dtr.shape
    nc = t // CHUNK

    def body(dtr_ref, dtb_ref, alog_ref, dt_ref, acs_ref, tr_ref):
        j = pl.program_id(0)
        valid = _dt_valid(j)
        row = lax.broadcasted_iota(jnp.int32, (CHUNK, LANES), 0)
        lane = lax.broadcasted_iota(jnp.int32, (CHUNK, LANES), 1)
        tril = (row >= lane).astype(F32)
        units = [(e, g) for e in range(bsz) for g in range(N_GROUPS)]
        sl = lambda g: pl.ds(g * LANES, LANES)
        dt = {(e, g): jnp.where(valid, _softplus(dtr_ref[e, :, sl(g)] + dtb_ref[g]), 0.0) for e, g in units}
        acs = {(e, g): _dot_exact(tril, dt[e, g] * -jnp.exp(alog_ref[g])) for e, g in units}
        for e, g in units:
            dt_ref[e, :, sl(g)] = dt[e, g]
            acs_ref[e, :, sl(g)] = acs[e, g]
            tr_ref[e, 0, g, 0:8, :] = dt[e, g].T[0:8]
            tr_ref[e, 0, g, 8:16, :] = acs[e, g].T[0:8]

    blk = pl.BlockSpec((bsz, CHUNK, D_DT), lambda j: (0, j, 0))
    const = pl.BlockSpec((N_GROUPS, 1, LANES), lambda j: (0, 0, 0))
    return pl.pallas_call(
        body, name=name, grid=(nc,), in_specs=[blk, const, const],
        out_specs=[blk, blk, pl.BlockSpec((bsz, 1, N_GROUPS, 16, LANES), lambda j: (0, j, 0, 0, 0))],
        out_shape=[jax.ShapeDtypeStruct(dtr.shape, F32), jax.ShapeDtypeStruct(dtr.shape, F32),
                   jax.ShapeDtypeStruct((bsz, nc, N_GROUPS, 16, LANES), F32)],
        compiler_params=_params("parallel"),
    )(dtr, dtb, alog)


def _ssd_decay(dt, acs, tr):
    lane = lax.broadcasted_iota(jnp.int32, (CHUNK, LANES), 1)
    row = lax.broadcasted_iota(jnp.int32, (CHUNK, LANES), 0)
    return dict(lane=lane, row=row, dt=dt, causal=row >= lane, acs=acs, acs_t=tr[8:16], dt_t=tr[0:8],
                aend=acs[CHUNK - 1:CHUNK, :])


def _ssd_specs(bsz, nc, rev):
    ch = (lambda j: nc - 1 - j) if rev else (lambda j: j)
    return dict(
        xs=pl.BlockSpec((bsz, CHUNK, GW), lambda g, j: (0, ch(j), g)),
        bm=pl.BlockSpec((bsz, CHUNK, D_STATE), lambda g, j: (0, ch(j), D_SSM // D_STATE + g)),
        cm=pl.BlockSpec((bsz, CHUNK, D_STATE), lambda g, j: (0, ch(j), D_SSM // D_STATE + N_GROUPS + g)),
        lane_blk=pl.BlockSpec((bsz, CHUNK, LANES), lambda g, j: (0, ch(j), g)),
        grp_const=pl.BlockSpec((1, 1, LANES), lambda g, j: (g, 0, 0)),
        grp_vec=pl.BlockSpec((1, GW), lambda g, j: (0, g)),
        state=pl.BlockSpec((bsz, 1, D_STATE, GW), lambda g, j: (0, ch(j), 0, g)),
        tr=pl.BlockSpec((bsz, 1, 1, 16, LANES), lambda g, j: (0, ch(j), g, 0, 0)),
    )


def _ssd_fwd(xc, dt, acs, tr, z, dskip, normw, *, name, rider=None):
    bsz, t, _ = xc.shape
    nc = t // CHUNK
    sp = _ssd_specs(bsz, nc, False)

    def body(xs_ref, b_ref, c_ref, dt_ref, acs_ref, tr_ref, z_ref, dsk_ref, nw_ref, yn_ref, y_ref, sp_ref, s_ref):
        j = pl.program_id(1)

        @pl.when(j == 0)
        def _():
            s_ref[...] = jnp.zeros_like(s_ref)

        ex = range(bsz)
        units = [(e, r) for e in ex for r in range(HPG)]
        full = lambda v: jnp.broadcast_to(v, (CHUNK, LANES))
        pair = lambda r: pl.ds((r // 2) * LANES, LANES)
        q = [_ssd_decay(dt_ref[e], acs_ref[e], tr_ref[e, 0, 0]) for e in ex]
        for e in ex:
            sp_ref[e, 0] = s_ref[e]
        bm, cm = [b_ref[e] for e in ex], [c_ref[e] for e in ex]
        cb = [_dot_nt(cm[e], bm[e]) for e in ex]
        low = q[0]["lane"] < HEAD_DIM
        col = {(e, r): full(q[e]["acs"][:, r:r + 1]) for e, r in units}
        aend = {(e, r): q[e]["aend"][:, r:r + 1] for e, r in units}
        decay = {(e, r): jnp.exp(jnp.where(q[e]["causal"], col[e, r] - q[e]["acs_t"][r:r + 1, :], -jnp.inf))
                 for e, r in units}
        mp = {(e, r): cb[e] * decay[e, r] * q[e]["dt_t"][r:r + 1, :] for e, r in units}
        ce = {(e, r): cm[e] * jnp.exp(col[e, r]) for e, r in units}
        bk = {(e, r): bm[e] * (jnp.exp(aend[e, r] - col[e, r]) * full(q[e]["dt"][:, r:r + 1])) for e, r in units}
        xp = {(e, r): xs_ref[e, :, pair(r)] for e, r in units}
        s_old = {(e, r): s_ref[e, :, pair(r)] for e, r in units}
        y_h = {u: _dot(mp[u], xp[u]) + _dot(ce[u], s_old[u]) for u in units}
        s_h = {u: jnp.exp(aend[u]) * s_old[u] + _dot_tn(bk[u], xp[u]) for u in units}
        for e in ex:
            for r in range(0, HPG, 2):
                y_ref[e, :, pair(r)] = jnp.where(low, y_h[e, r], y_h[e, r + 1])
                s_ref[e, :, pair(r)] = jnp.where(low, s_h[e, r], s_h[e, r + 1])
        y = [y_ref[e] + dsk_ref[...] * xs_ref[e] for e in ex]
        zz = [z_ref[e] for e in ex]
        yg = [y[e] * (zz[e] * _sigmoid(zz[e])) for e in ex]
        rstd = [lax.rsqrt(jnp.mean(yg[e] * yg[e], axis=-1, keepdims=True) + EPS) for e in ex]
        for e in ex:
            y_ref[e] = y[e]
            yn_ref[e] = (yg[e] * rstd[e] * nw_ref[...]).astype(BF16)

    grid = (N_GROUPS, nc)
    ride = _Ride(rider, body, 9, 3, 1, grid)
    outs = pl.pallas_call(
        ride.body, name=name, grid=grid,
        in_specs=[sp["xs"], sp["bm"], sp["cm"], sp["lane_blk"], sp["lane_blk"], sp["tr"], sp["xs"],
                  sp["grp_vec"], sp["grp_vec"]] + ride.in_specs,
        out_specs=[sp["xs"], sp["xs"], sp["state"]] + ride.out_specs,
        out_shape=[jax.ShapeDtypeStruct((bsz, t, D_SSM), BF16), jax.ShapeDtypeStruct((bsz, t, D_SSM), F32),
                   jax.ShapeDtypeStruct((bsz, nc, D_STATE, D_SSM), F32)] + ride.out_shape,
        scratch_shapes=[pltpu.VMEM((bsz, D_STATE, GW), F32)] + ride.scratch,
        compiler_params=_params(*ride.semantics(("parallel", "arbitrary"))),
    )(xc, xc, xc, dt, acs, tr, z, dskip, normw, *ride.args)
    return outs[:3], outs[3:]


def _ssd_bwd(xc, dtr, dt, acs, tr, z, ypre, sprev, dyn, dtb, alog, dskip, normw, *, name, rider=None):
    bsz, t, _ = xc.shape
    nc = t // CHUNK
    sp = _ssd_specs(bsz, nc, True)

    def body(xs_ref, b_ref, c_ref, dtr_ref, dt_ref, acs_ref, tr_ref, z_ref, y_ref, sp_ref, dyn_ref, dtb_ref, alog_ref,
             dsk_ref, nw_ref, dz_ref, dxs_ref, db_ref, dc_ref, ddt_ref, dnw_ref, dsm_ref, ds_ref):
        j = pl.program_id(1)

        @pl.when(j == 0)
        def _():
            ds_ref[...] = jnp.zeros_like(ds_ref)
            dnw_ref[...] = jnp.zeros_like(dnw_ref)
            dsm_ref[...] = jnp.zeros_like(dsm_ref)

        ex = range(bsz)
        heads = range(HPG)
        units = [(e, r) for e in ex for r in heads]
        q = [_ssd_decay(dt_ref[e], acs_ref[e], tr_ref[e, 0, 0]) for e in ex]
        a = -jnp.exp(alog_ref[0])
        valid = _dt_valid(nc - 1 - j)
        lane, row = q[0]["lane"], q[0]["row"]
        lane1 = lane[0:1, :]
        nw = nw_ref[...]
        y, zz, dyn = [y_ref[e] for e in ex], [z_ref[e] for e in ex], [dyn_ref[e] for e in ex]
        sz = [_sigmoid(zz[e]) for e in ex]
        sil = [zz[e] * sz[e] for e in ex]
        yg = [y[e] * sil[e] for e in ex]
        rstd = [lax.rsqrt(jnp.mean(yg[e] * yg[e], axis=-1, keepdims=True) + EPS) for e in ex]
        gn = [dyn[e] * nw for e in ex]
        dyg = [rstd[e] * (gn[e] - yg[e] * (rstd[e] * rstd[e]) * jnp.mean(gn[e] * yg[e], axis=-1, keepdims=True))
               for e in ex]
        dy = [dyg[e] * sil[e] for e in ex]
        xs = [xs_ref[e] for e in ex]
        for e in ex:
            dnw_ref[e] += jnp.sum(dyn[e] * yg[e] * rstd[e], axis=0, keepdims=True)
            dz_ref[e] = (dyg[e] * y[e] * (sz[e] * (1.0 + zz[e] * (1.0 - sz[e])))).astype(BF16)
        dskip_cols = [jnp.sum(dy[e] * xs[e], axis=0, keepdims=True) for e in ex]

        bm, cm = [b_ref[e] for e in ex], [c_ref[e] for e in ex]
        cb = [_dot_nt(cm[e], bm[e]) for e in ex]
        zero = jnp.zeros((CHUNK, LANES), F32)
        full = lambda v: jnp.broadcast_to(v, (CHUNK, LANES))
        low = lane < HEAD_DIM
        half = [low if r % 2 == 0 else ~low for r in heads]
        sl = lambda v, r: v[:, (r // 2) * LANES:(r // 2 + 1) * LANES]
        pair = lambda r: pl.ds((r // 2) * LANES, LANES)
        col = {(e, r): full(q[e]["acs"][:, r:r + 1]) for e, r in units}
        dt_col = {(e, r): full(q[e]["dt"][:, r:r + 1]) for e, r in units}
        aend = {(e, r): q[e]["aend"][:, r:r + 1] for e, r in units}
        dt_row = {(e, r): q[e]["dt_t"][r:r + 1, :] for e, r in units}
        decay = {(e, r): jnp.exp(jnp.where(q[e]["causal"], col[e, r] - q[e]["acs_t"][r:r + 1, :], -jnp.inf))
                 for e, r in units}
        ea = {u: jnp.exp(col[u]) for u in units}
        dte = {u: jnp.exp(aend[u] - col[u]) for u in units}
        ed = {u: jnp.exp(aend[u]) for u in units}
        k = {u: dte[u] * dt_col[u] for u in units}
        mp = {(e, r): cb[e] * decay[e, r] * dt_row[e, r] for e, r in units}
        xp = {(e, r): sl(xs[e], r) for e, r in units}
        dym = {(e, r): jnp.where(half[r], sl(dy[e], r), 0.0) for e, r in units}
        s_old = {(e, r): sp_ref[e, 0, :, pair(r)] for e, r in units}
        ds_old = {(e, r): ds_ref[e, :, pair(r)] for e, r in units}
        dsm = {(e, r): jnp.where(half[r], ds_old[e, r], 0.0) for e, r in units}
        gmat = {u: _dot_nt(dym[u], xp[u]) for u in units}
        t1 = {u: _dot_nt(dym[u], s_old[u]) for u in units}
        dbs = {u: _dot_nt(xp[u], dsm[u]) for u in units}
        dx = {(e, r): _dot_tn(mp[e, r], dym[e, r]) + _dot(bm[e] * k[e, r], dsm[e, r]) for e, r in units}
        ds = {(e, r): _dot_tn(cm[e] * ea[e, r], dym[e, r]) for e, r in units}
        gd = {u: gmat[u] * decay[u] for u in units}
        w0 = {(e, r): gd[e, r] * cb[e] for e, r in units}
        cs0 = {u: jnp.sum(w0[u], axis=0, keepdims=True) for u in units}
        rs = {u: jnp.sum(w0[u] * dt_row[u], axis=1, keepdims=True) for u in units}
        qv = {(e, r): jnp.sum(cm[e] * t1[e, r], axis=1, keepdims=True) for e, r in units}
        dk = {(e, r): jnp.sum(bm[e] * dbs[e, r], axis=1, keepdims=True) for e, r in units}
        ddte = {u: dk[u] * dt_col[u] for u in units}
        d_aend = {u: _sum_all(dsm[u] * s_old[u]) * ed[u] + _sum_all(ddte[u][:, 0:1] * dte[u][:, 0:1]) for u in units}
        last_row = row == CHUNK - 1
        dacs_col = {u: rs[u] + qv[u] * ea[u] - ddte[u] * dte[u] + jnp.where(last_row, d_aend[u], 0.0) for u in units}
        triu = (lane >= row).astype(F32)
        for e in ex:
            dcb, dc_acc, db_acc = zero, zero, zero
            dacs, dacs_t, ddt, ddt_t = zero, zero, zero, zero
            dskip_row = jnp.zeros((1, LANES), F32)
            for r in heads:
                u = (e, r)
                dcb = dcb + gd[u] * dt_row[u]
                dc_acc = dc_acc + ea[u] * t1[u]
                db_acc = db_acc + k[u] * dbs[u]
                dacs = jnp.where(lane == r, dacs_col[u], dacs)
                ddt = jnp.where(lane == r, dk[u] * dte[u], ddt)
                dacs_t = jnp.where(row == r, -cs0[u] * dt_row[u], dacs_t)
                ddt_t = jnp.where(row == r, cs0[u], ddt_t)
                dsk = _sum_all(jnp.where(half[r][0:1, :], sl(dskip_cols[e], r), 0.0))
                dskip_row = dskip_row + jnp.where(lane1 == r, dsk, 0.0)
            for r in range(0, HPG, 2):
                dxs_ref[e, :, pair(r)] = (dx[e, r] + dx[e, r + 1] + sl(dy[e], r) * dsk_ref[:, pair(r)]).astype(BF16)
                ed_pair = jnp.where(lane1 < HEAD_DIM, ed[e, r], ed[e, r + 1])
                ds_ref[e, :, pair(r)] = ds[e, r] + ds[e, r + 1] + ed_pair * ds_old[e, r]
            dacs = dacs + dacs_t.T
            ddt = ddt + ddt_t.T
            dda = _dot_exact(triu, dacs)
            ddt = ddt + dda * a
            da = jnp.sum(dda * q[e]["dt"], axis=0, keepdims=True)
            draw = jnp.where(valid, ddt * _sigmoid(dtr_ref[e] + dtb_ref[0]), 0.0)
            ddt_ref[e] = draw.astype(BF16)
            dsm_ref[e, 0, 0:1, :] += dskip_row
            dsm_ref[e, 0, 1:2, :] += da * a
            dsm_ref[e, 0, 2:3, :] += jnp.sum(draw, axis=0, keepdims=True)
            dc_ref[e] = (dc_acc + _dot(dcb, bm[e])).astype(BF16)
            db_ref[e] = (db_acc + _dot_tn(dcb, cm[e])).astype(BF16)

    grp_out = pl.BlockSpec((bsz, CHUNK, D_STATE), lambda g, j: (0, nc - 1 - j, g))
    grid = (N_GROUPS, nc)
    ride = _Ride(rider, body, 15, 7, 1, grid)
    outs = pl.pallas_call(
        ride.body, name=name, grid=grid,
        in_specs=[sp["xs"], sp["bm"], sp["cm"], sp["lane_blk"], sp["lane_blk"], sp["lane_blk"], sp["tr"], sp["xs"],
                  sp["xs"], sp["state"], sp["xs"], sp["grp_const"], sp["grp_const"], sp["grp_vec"], sp["grp_vec"]]
        + ride.in_specs,
        out_specs=[sp["xs"], sp["xs"], grp_out, grp_out, sp["lane_blk"],
                   pl.BlockSpec((bsz, 1, GW), lambda g, j: (0, 0, g)),
                   pl.BlockSpec((bsz, 1, 8, LANES), lambda g, j: (0, g, 0, 0))] + ride.out_specs,
        out_shape=[jax.ShapeDtypeStruct((bsz, t, D_SSM), BF16), jax.ShapeDtypeStruct((bsz, t, D_SSM), BF16),
                   jax.ShapeDtypeStruct((bsz, t, N_GROUPS * D_STATE), BF16),
                   jax.ShapeDtypeStruct((bsz, t, N_GROUPS * D_STATE), BF16),
                   jax.ShapeDtypeStruct((bsz, t, D_DT), BF16), jax.ShapeDtypeStruct((bsz, 1, D_SSM), F32),
                   jax.ShapeDtypeStruct((bsz, N_GROUPS, 8, LANES), F32)] + ride.out_shape,
        scratch_shapes=[pltpu.VMEM((bsz, D_STATE, GW), F32)] + ride.scratch,
        compiler_params=_params(*ride.semantics(("parallel", "arbitrary"))),
    )(xc, xc, xc, dtr, dt, acs, tr, z, ypre, sprev, dyn, dtb, alog, dskip, normw, *ride.args)
    return outs[:7], outs[7:]


def _input_grad(dhn, h0, w, dres, seq, *, name):
    bsz, t, d = h0.shape
    nc = t // CHUNK

    def body(dy_ref, h_ref, w_ref, dres_ref, gx_ref, head_ref, dw_ref):
        j = pl.program_id(0)

        @pl.when(j == 0)
        def _():
            dw_ref[...] = jnp.zeros_like(dw_ref)

        for e in range(bsz):
            x, dyv = h_ref[e], dy_ref[e]
            r = lax.rsqrt(jnp.mean(x * x, axis=-1, keepdims=True) + EPS)
            g = dyv * w_ref[...]
            dx = r * (g - x * (r * r) * jnp.mean(g * x, axis=-1, keepdims=True)) + dres_ref[e]
            dw_ref[...] += jnp.sum(dyv * x * r, axis=0, keepdims=True)
            gx_ref[e] = dx

        @pl.when(j == 0)
        def _():
            head_ref[...] = gx_ref[...]

    row = pl.BlockSpec((bsz, CHUNK, d), lambda j: (0, j, 0))
    return pl.pallas_call(
        body, name=name, grid=(nc,),
        in_specs=[row, row, pl.BlockSpec((1, d), lambda j: (0, 0)), row],
        out_specs=[pl.BlockSpec((bsz, CHUNK, d), lambda j: (0, jnp.maximum(j - 1, 0), 0)),
                   pl.BlockSpec((bsz, CHUNK, d), lambda j: (0, 0, 0)), pl.BlockSpec((1, d), lambda j: (0, 0))],
        out_shape=[jax.ShapeDtypeStruct((bsz, seq, d), F32), jax.ShapeDtypeStruct((bsz, CHUNK, d), F32),
                   jax.ShapeDtypeStruct((1, d), F32)],
        compiler_params=_params("arbitrary"),
    )(dhn, h0, w, dres)


def _remote(src, dst, send_sem, recv_sem, dev):
    return pltpu.make_async_remote_copy(src_ref=src, dst_ref=dst, send_sem=send_sem, recv_sem=recv_sem,
                                        device_id=dev, device_id_type=MESH)


def _position():
    return lax.axis_index("x"), lax.axis_index("y"), lax.axis_index("c")


def _other_chips(pos):
    x, y, _ = pos
    return [(1 - x, y), (x, 1 - y), (1 - x, 1 - y)]


class _Gather:
    def __init__(self, arrs):
        n = len(arrs)
        self.args, self.n_in, self.n_out = list(arrs), n, n
        self.split = [a.ndim == 2 and a.shape[1] % (2 * LANES) == 0 for a in arrs]
        self.out_shape = [jax.ShapeDtypeStruct((4,) + a.shape, a.dtype) for a in arrs]
        self.scratch = [pltpu.SemaphoreType.DMA((3 * n,)), pltpu.SemaphoreType.DMA((3 * n,)),
                        pltpu.SemaphoreType.DMA((n,)), pltpu.SemaphoreType.DMA((3 * n,)),
                        pltpu.SemaphoreType.DMA((3 * n,))]

    def _copies(self, pos, ins, outs, sems):
        send_sems, recv_sems, loc_sems, pass_send_sems, pass_recv_sems = sems
        x, y, c = pos
        me, sibling = 2 * x + y, (x, y, 1 - c)
        local = [pltpu.make_async_copy(ins[i], outs[i].at[me], loc_sems.at[i]) for i in range(self.n_in)]
        sends, recvs, passes, pass_recvs = [], [], [], []
        for i in range(self.n_in):
            half = self.args[i].shape[1] // 2 if self.split[i] else None
            for k, (px, py) in enumerate(_other_chips(pos)):
                them = 2 * px + py
                sems_k = (send_sems.at[3 * i + k], recv_sems.at[3 * i + k], (px, py, c))
                if half is None:
                    sends.append(_remote(ins[i], outs[i].at[me], *sems_k))
                    recvs.append(_remote(ins[i], outs[i].at[them], *sems_k))
                    passes.append(None)
                    continue
                mine = pl.ds(pl.multiple_of(c * half, LANES), half)
                other = pl.ds(pl.multiple_of((1 - c) * half, LANES), half)
                sends.append(_remote(ins[i].at[:, mine], outs[i].at[me, :, mine], *sems_k))
                recvs.append(_remote(ins[i].at[:, mine], outs[i].at[them, :, mine], *sems_k))
                pass_k = (pass_send_sems.at[3 * i + k], pass_recv_sems.at[3 * i + k], sibling)
                passes.append(_remote(outs[i].at[them, :, mine], outs[i].at[them, :, mine], *pass_k))
                pass_recvs.append(_remote(outs[i].at[them, :, other], outs[i].at[them, :, other], *pass_k))
        return local, sends, recvs, passes, pass_recvs

    def start(self, pos, ins, outs, sems):
        local, sends = self._copies(pos, ins, outs, sems)[:2]
        for cp in local + sends:
            cp.start()

    def finish(self, pos, ins, outs, sems):
        local, sends, recvs, passes, pass_recvs = self._copies(pos, ins, outs, sems)
        for cp, onward in zip(recvs, passes):
            cp.wait_recv()
            if onward is not None:
                onward.start()
        for cp in pass_recvs:
            cp.wait_recv()
        for cp in sends + [p for p in passes if p is not None]:
            cp.wait_send()
        for cp in local:
            cp.wait()


class _Exchange:
    FLIPS = [(fx, fy, fc) for fx in (0, 1) for fy in (0, 1) for fc in (0, 1)][1:]

    def __init__(self, big, small=None):
        n = len(big)
        self.n_big, self.has_small = n, small is not None
        self.args = list(big) + ([small] if self.has_small else [])
        self.n_in = self.n_out = len(self.args)
        self.out_shape = [jax.ShapeDtypeStruct(a.shape, a.dtype) for a in big]
        self.scratch = [pltpu.SemaphoreType.DMA((max(3 * n, 1),)), pltpu.SemaphoreType.DMA((max(3 * n, 1),)),
                        pltpu.SemaphoreType.DMA((n + 1,))]
        if self.has_small:
            self.out_shape.append(jax.ShapeDtypeStruct((8,) + small.shape, small.dtype))
            self.scratch += [pltpu.SemaphoreType.DMA((7,)), pltpu.SemaphoreType.DMA((7,))]

    def _copies(self, pos, ins, outs, sems):
        x, y, c = pos
        me, me8 = 2 * x + y, 4 * x + 2 * y + c
        local, sends, recvs = [], [], []
        for i in range(self.n_big):
            local.append(pltpu.make_async_copy(ins[i].at[me], outs[i].at[me], sems[2].at[i]))
            for k, (px, py) in enumerate(_other_chips(pos)):
                sems_k = (sems[0].at[3 * i + k], sems[1].at[3 * i + k], (px, py, c))
                sends.append(_remote(ins[i].at[2 * px + py], outs[i].at[me], *sems_k))
                recvs.append(_remote(ins[i].at[me], outs[i].at[2 * px + py], *sems_k))
        if self.has_small:
            small, landed = ins[self.n_big], outs[self.n_big]
            local.append(pltpu.make_async_copy(small, landed.at[me8], sems[2].at[self.n_big]))
            for k, (fx, fy, fc) in enumerate(self.FLIPS):
                peer = (x ^ fx, y ^ fy, c ^ fc)
                sems_k = (sems[3].at[k], sems[4].at[k], peer)
                sends.append(_remote(small, landed.at[me8], *sems_k))
                recvs.append(_remote(small, landed.at[4 * peer[0] + 2 * peer[1] + peer[2]], *sems_k))
        return local, sends, recvs, [None] * len(recvs), []

    start = _Gather.start
    finish = _Gather.finish


class _Swap:
    def __init__(self, arrs):
        n = len(arrs)
        self.args, self.n_in, self.n_out = list(arrs), n, n
        self.out_shape = [jax.ShapeDtypeStruct(a.shape, a.dtype) for a in arrs]
        self.scratch = [pltpu.SemaphoreType.DMA((n,)), pltpu.SemaphoreType.DMA((n,))]

    def _copies(self, pos, ins, outs, sems):
        x, y, c = pos
        both = [_remote(ins[i], outs[i], sems[0].at[i], sems[1].at[i], (x, y, 1 - c)) for i in range(self.n_in)]
        return [], both, both, [None] * len(both), []

    start = _Gather.start
    finish = _Gather.finish


def _comm(rider, *, name):
    a, b = rider.n_in, rider.n_in + rider.n_out

    def body(*refs):
        pos = _position()
        rider.start(pos, refs[:a], refs[a:b], refs[b:])
        rider.finish(pos, refs[:a], refs[a:b], refs[b:])

    return pl.pallas_call(body, name=name, in_specs=[ANY] * rider.n_in, out_specs=[ANY] * rider.n_out,
                          out_shape=rider.out_shape, scratch_shapes=rider.scratch)(*rider.args)


class _Ride:
    def __init__(self, rider, body, n_in, n_out, n_scratch, grid):
        self.rider = rider
        self.args = rider.args if rider else []
        self.in_specs = [ANY] * rider.n_in if rider else []
        self.out_specs = [ANY] * rider.n_out if rider else []
        self.out_shape = rider.out_shape if rider else []
        self.scratch = rider.scratch if rider else []
        self.body = self._wrap(body, n_in, n_out, n_scratch, grid) if rider else body

    def semantics(self, sem):
        return ("arbitrary",) * len(sem) if self.rider else sem

    def _wrap(self, body, n_in, n_out, n_scratch, grid):
        rider = self.rider
        a = n_in
        b = a + rider.n_in
        c = b + n_out
        d = c + rider.n_out
        e = d + n_scratch

        def wrapped(*refs):
            pos = _position()
            ids = [pl.program_id(i) for i in range(len(grid))]
            first = functools.reduce(jnp.logical_and, [i == 0 for i in ids])
            last = functools.reduce(jnp.logical_and, [i == g - 1 for i, g in zip(ids, grid)])

            @pl.when(first)
            def _():
                rider.start(pos, refs[a:b], refs[c:d], refs[e:])

            body(*refs[:a], *refs[b:c], *refs[d:e])

            @pl.when(last)
            def _():
                rider.finish(pos, refs[a:b], refs[c:d], refs[e:])

        return wrapped


def _elementwise_tiles(r, c):
    if r % 8 == 0 and r * c > 65536:
        tm = _pick(r, (256, 128, 64, 16, 8))
        return (tm, c), r // tm, lambda i: (i, 0)
    if r % 8 and c % 256 == 0 and r * c > 65536:
        return (r, 256), c // 256, lambda i: (0, i)
    return (r, c), 1, lambda i: (0, 0)


def _chip_sum(landed, *, name):
    _, r, c = landed.shape
    blk, steps, at = _elementwise_tiles(r, c)

    def body(land_ref, o_ref):
        acc = land_ref[0].astype(F32)
        for jchip in range(1, 4):
            acc = acc + land_ref[jchip].astype(F32)
        o_ref[...] = acc

    return pl.pallas_call(
        body, name=name, grid=(steps,), in_specs=[pl.BlockSpec((4,) + blk, lambda i: (0,) + at(i))],
        out_specs=pl.BlockSpec(blk, at), out_shape=jax.ShapeDtypeStruct((r, c), F32),
        compiler_params=_params("parallel"),
    )(landed)


def _device_sum(parts, *, name):
    _, r, c = parts.shape

    def body(p_ref, o_ref):
        acc = p_ref[0]
        for d in range(1, 8):
            acc = acc + p_ref[d]
        o_ref[...] = acc

    return pl.pallas_call(body, name=name, out_shape=jax.ShapeDtypeStruct((r, c), F32))(parts)


def _adamw_math(w, g, m, v):
    m = ADAM_B1 * m + (1.0 - ADAM_B1) * g
    v = ADAM_B2 * v + (1.0 - ADAM_B2) * (g * g)
    m_hat = m / (1.0 - ADAM_B1 ** ADAM_STEP)
    v_hat = v / (1.0 - ADAM_B2 ** ADAM_STEP)
    return -ADAM_LR * (m_hat / (jnp.sqrt(v_hat) + ADAM_EPS) + ADAM_WD * w), m, v


def _adamw(w, g_parts, m, v, *, name):
    r, c = w.shape
    shape, steps, at = _elementwise_tiles(r, c)
    n_g = len(g_parts)

    def body(*refs):
        w_ref, m_ref, v_ref = refs[n_g:n_g + 3]
        g_ref, d_ref, nm_ref, nv_ref = refs[n_g + 3:]
        g = refs[0][...]
        for p in refs[1:n_g]:
            g = g + p[...]
        g_ref[...] = g
        d_ref[...], nm_ref[...], nv_ref[...] = _adamw_math(w_ref[...], g, m_ref[...], v_ref[...])

    blk = pl.BlockSpec(shape, at)
    return pl.pallas_call(
        body, name=name, grid=(steps,), in_specs=[blk] * (n_g + 3), out_specs=[blk] * 4,
        out_shape=[jax.ShapeDtypeStruct((r, c), F32)] * 4, compiler_params=_params("parallel"),
    )(*g_parts, w, m, v)


def _pad_heads(v):
    return jnp.pad(v.reshape(N_GROUPS, 1, HPG), ((0, 0), (0, 0), (0, LANES - HPG)))


def _unpad_heads(v):
    return v[:, :HPG].reshape(1, N_HEADS)


_SMALL_EARLY = [("pool_w", (512, 128)), ("pool_scale", (1, 512)), ("conv_w", (4, D_XBC)), ("conv_b", (1, D_XBC)),
                ("dt_bias", (1, N_HEADS)), ("a_log", (1, N_HEADS)), ("d_skip", (1, N_HEADS)), ("ssm_norm_w", (1, D_SSM)),
                ("norm_ffn_w", (1, 1024)), ("norm_f_w", (1, 1024))]
_SMALL_LATE = [("norm_mix_w", (1, 1024)), ("meta", (N_META, 1024)), ("loss", (1, 1))]


def _pack_small(grads, layout):
    rows = []
    for nm, shape in layout:
        flat = grads[nm].reshape(-1)
        rows.append(jnp.pad(flat, (0, (-flat.size) % LANES)).reshape(-1, LANES))
    packed = jnp.concatenate(rows, axis=0)
    return jnp.pad(packed, ((0, (-packed.shape[0]) % 8), (0, 0)))


def _unpack_small(packed, layout):
    out, r0 = {}, 0
    for nm, shape in layout:
        size = shape[0] * shape[1]
        nrow = -(-size // LANES)
        out[nm] = packed[r0:r0 + nrow].reshape(-1)[:size].reshape(shape)
        r0 += nrow
    return out


def kernel(x, meta, norm_mix_w, w_in, pool_w, pool_scale, conv_w, conv_b, dt_bias, a_log, d_skip, ssm_norm_w, w_out, norm_ffn_w, w_ff1, w_ff2, norm_f_w, loss_target, m_meta, m_norm_mix_w, m_w_in, m_pool_w, m_pool_scale, m_conv_w, m_conv_b, m_dt_bias, m_a_log, m_d_skip, m_ssm_norm_w, m_w_out, m_norm_ffn_w, m_w_ff1, m_w_ff2, m_norm_f_w, v_meta, v_norm_mix_w, v_w_in, v_pool_w, v_pool_scale, v_conv_w, v_conv_b, v_dt_bias, v_a_log, v_d_skip, v_ssm_norm_w, v_w_out, v_norm_ffn_w, v_w_ff1, v_w_ff2, v_norm_f_w):
    bsz, seq, d = x.shape
    t = seq + CHUNK
    n = bsz * t
    chip = 2 * lax.axis_index("x") + lax.axis_index("y")
    d_in = w_in.shape[2] * 4

    g_conv, g_meta = _comm(_Gather([conv_w[0], meta]), name="gather_small")
    convw = g_conv.transpose(1, 0, 2).reshape(CONV_W, D_XBC)
    meta_full = g_meta.transpose(1, 0, 2).reshape(N_META, d)
    (h0, hn1), (g_in,) = _embed_norm(x, meta_full, norm_mix_w, name="embed_norm",
                                     rider=_Gather([w_in[0].T.astype(BF16)]))
    h0f, hn1 = h0.reshape(n, d), hn1.reshape(n, d)
    late_weights = _Gather([w_out[0].astype(BF16), w_ff1[0].astype(BF16), w_ff2[0].astype(BF16)])
    win = g_in.reshape(d_in, d)
    wu, wz = win[:D_POOL], win[D_POOL:D_POOL + D_SSM]
    wx = win[D_POOL + D_SSM:D_POOL + D_SSM + D_XBC]
    wdt = jnp.pad(win[D_POOL + D_SSM + D_XBC:].reshape(N_GROUPS, HPG, d),
                  ((0, 0), (0, LANES - HPG), (0, 0))).reshape(D_DT, d)
    dtb, alog = _pad_heads(dt_bias), _pad_heads(a_log)
    dskip = jnp.repeat(d_skip, HEAD_DIM, axis=1)
    poolw = pool_w[0]

    u = _mm(hn1, wu, name="proj_u", nt=True)
    z = _mm(hn1, wz, name="proj_z", nt=True)
    xbc, xc = _proj_conv(hn1, wx, convw, conv_b, name="proj_xbc")
    dtr = _mm(hn1, wdt, name="proj_dt", nt=True)
    ypool = _pool_fwd(u.reshape(bsz, t, D_POOL), poolw, pool_scale, name="pool_fwd")
    xbc3 = xbc.reshape(bsz, t, D_XBC)
    xc = xc.reshape(bsz, t, D_XBC)
    z3, dtr3 = z.reshape(bsz, t, D_SSM), dtr.reshape(bsz, t, D_DT)
    dt3, acs3, tr3 = _ssd_prep(dtr3, dtb, alog, name="ssd_prep")
    (yn, ypre, sprev), (g_out, g_ff1, g_ff2) = _ssd_fwd(xc, dt3, acs3, tr3, z3, dskip, ssm_norm_w, name="ssd_fwd",
                                                        rider=late_weights)
    wo = g_out.reshape(D_POOL + D_SSM, d)
    wo_p, wo_s = wo[:D_POOL], wo[D_POOL:]
    w1 = g_ff1
    w2 = g_ff2.reshape(D_FF, d)
    ypool_f, yn_f = ypool.reshape(n, D_POOL), yn.reshape(n, D_SSM)
    add = lambda r, e: r + e
    h1 = _mm([ypool_f, yn_f], [wo_p, wo_s], name="out_proj", post=add, extras=(h0f,))
    hn2 = _rms_fwd(h1, norm_ffn_w, name="norm_ffn")
    act = _mm(hn2, w1, name="ff1", out_dtype=BF16)
    relu2 = lambda a: jnp.square(jnp.maximum(a, 0))
    h2 = _mm(act, w2, name="ff2", pre=relu2, post=add, extras=(h1,))
    dh2, dh2b, loss_acc, d_norm_f = _final_norm_loss(h2.reshape(bsz, t, d), loss_target, norm_f_w.reshape(1, d),
                                                     name="loss")

    dh2f, dh2bf = dh2.reshape(n, d), dh2b.reshape(n, d)
    dact = _mm(dh2bf, w2, name="ff2_bwd", nt=True, post=lambda r, a: r * (2.0 * jnp.maximum(a, 0).astype(F32)),
               extras=(act,), out_dtype=BF16)
    d_w2 = _mm_tn(act, dh2bf, name="ff2_dw", tk=2048, tn=1024, pre=relu2)
    d_w1 = _mm_tn(hn2, dact, name="ff1_dw", tk=1024, tn=2048, slab=D_FF // 4)
    dh1, dh1b, d_norm_ffn = _mm_rms_bwd(dact, w1, h1, norm_ffn_w, dh2f, name="ff1_bwd")
    dypool = _mm(dh1b, wo_p, name="out_pool_bwd", nt=True)
    dyn = _mm(dh1b, wo_s, name="out_ssm_bwd", nt=True)
    d_wo_p = _mm_tn(ypool_f, dh1b, name="out_pool_dw", tk=512, tn=1024)
    d_wo_s = _mm_tn(yn_f, dh1b, name="out_ssm_dw", tk=1536, tn=1024)
    big_late = [jnp.concatenate([d_wo_p, d_wo_s], axis=0).reshape(4, (D_POOL + D_SSM) // 4, d),
                d_w1, d_w2.reshape(4, D_FF // 4, d)]
    (dz, dxs, dbm, dcm, ddtr, d_nw, d_heads), landed_late = _ssd_bwd(
        xc, dtr3, dt3, acs3, tr3, z3, ypre, sprev, dyn.reshape(bsz, t, D_SSM), dtb, alog, dskip, ssm_norm_w, name="ssd_bwd",
        rider=_Exchange(big_late))
    dxbc, d_convwb = _conv_bwd(xbc3, dxs, dbm, dcm, convw, conv_b, name="conv_bwd")
    du, d_poolw, d_poolsc = _pool_bwd(u.reshape(bsz, t, D_POOL), dypool.reshape(bsz, t, D_POOL), poolw, pool_scale,
                                      name="pool_bwd")
    duf, dzf, dxbcf, ddtrf = du.reshape(n, D_POOL), dz.reshape(n, D_SSM), dxbc.reshape(n, D_XBC), ddtr.reshape(n, D_DT)
    heads = jnp.sum(d_heads, axis=0)
    small_early = _pack_small({
        "pool_w": d_poolw, "pool_scale": d_poolsc,
        "conv_w": jnp.sum(d_convwb[:, :CONV_W], axis=0), "conv_b": jnp.sum(d_convwb[:, CONV_W:CONV_W + 1], axis=0),
        "dt_bias": _unpad_heads(heads[:, 2]), "a_log": _unpad_heads(heads[:, 1]), "d_skip": _unpad_heads(heads[:, 0]),
        "ssm_norm_w": jnp.sum(d_nw, axis=0), "norm_ffn_w": d_norm_ffn, "norm_f_w": d_norm_f}, _SMALL_EARLY)
    d_wu = _mm_tn(duf, hn1, name="proj_u_dw", tk=512, tn=1024)
    d_wz = _mm_tn(dzf, hn1, name="proj_z_dw", tk=1536, tn=1024)
    d_wx, (early_all,) = _mm_tn(dxbcf, hn1, name="proj_xbc_dw", tk=1280, tn=1024, rider=_Exchange([], small_early))
    d_wdt = _mm_tn(ddtrf, hn1, name="proj_dt_dw", tk=512, tn=1024)
    d_win = jnp.concatenate([d_wu, d_wz, d_wx, d_wdt.reshape(N_GROUPS, LANES, d)[:, :HPG].reshape(N_HEADS, d)], axis=0)
    big_in = d_win.reshape(4, d_in // 4, d)
    dhn1, (landed_in,) = _mm([duf, dzf, dxbcf, ddtrf], [wu, wz, wx, wdt], name="proj_bwd",
                             rider=_Exchange([big_in]))
    grad_x, d_head_rows, d_norm_mix = _input_grad(
        dhn1.reshape(bsz, t, d), h0, norm_mix_w, dh1.reshape(bsz, t, d), seq, name="input_grad")

    landed = [landed_in] + list(landed_late)
    small_late = _pack_small({"norm_mix_w": d_norm_mix, "meta": jnp.sum(d_head_rows[:, PAD:], axis=0),
                              "loss": loss_acc[0:1, 0:1]}, _SMALL_LATE)
    (late_all,) = _comm(_Exchange([], small_late), name="exchange_small")
    mine = [_chip_sum(l, name=f"chip_sum_{i}") for i, l in enumerate(landed)]
    theirs = _comm(_Swap(mine), name="swap_cores")
    gsmall = {**_unpack_small(_device_sum(early_all, name="device_sum_early"), _SMALL_EARLY),
              **_unpack_small(_device_sum(late_all, name="device_sum_late"), _SMALL_LATE)}
    gsmall["conv_w"] = lax.dynamic_slice_in_dim(gsmall["conv_w"], chip * (D_XBC // 4), D_XBC // 4, axis=1)
    gsmall["meta"] = lax.dynamic_slice_in_dim(gsmall["meta"], chip * (d // 4), d // 4, axis=1)
    loss = gsmall["loss"][0, 0]

    given = dict(meta=(meta, m_meta, v_meta), norm_mix_w=(norm_mix_w, m_norm_mix_w, v_norm_mix_w),
                 w_in=(w_in, m_w_in, v_w_in), pool_w=(pool_w, m_pool_w, v_pool_w),
                 pool_scale=(pool_scale, m_pool_scale, v_pool_scale), conv_w=(conv_w, m_conv_w, v_conv_w),
                 conv_b=(conv_b, m_conv_b, v_conv_b), dt_bias=(dt_bias, m_dt_bias, v_dt_bias),
                 a_log=(a_log, m_a_log, v_a_log), d_skip=(d_skip, m_d_skip, v_d_skip),
                 ssm_norm_w=(ssm_norm_w, m_ssm_norm_w, v_ssm_norm_w), w_out=(w_out, m_w_out, v_w_out),
                 norm_ffn_w=(norm_ffn_w, m_norm_ffn_w, v_norm_ffn_w), w_ff1=(w_ff1, m_w_ff1, v_w_ff1),
                 w_ff2=(w_ff2, m_w_ff2, v_w_ff2), norm_f_w=(norm_f_w, m_norm_f_w, v_norm_f_w))
    big_names = ["w_in", "w_out", "w_ff1", "w_ff2"]
    results = {}
    for nm, (w, m, v) in given.items():
        if nm in big_names:
            i = big_names.index(nm)
            parts, shape2 = (mine[i], theirs[i]), mine[i].shape
        else:
            parts, shape2 = (gsmall[nm],), gsmall[nm].shape
        if nm == "w_in":
            outs = _adamw(w[0].T, parts, m[0].T, v[0].T, name=f"adamw_{nm}")
            results[nm] = [o.T[None] for o in outs]
        else:
            outs = _adamw(w.reshape(shape2), parts, m.reshape(shape2), v.reshape(shape2), name=f"adamw_{nm}")
            results[nm] = [o.reshape(w.shape) for o in outs]
    order = list(given)
    return (loss, grad_x, *[results[nm][0] for nm in order], *[results[nm][1] for nm in order],
            *[results[nm][2] for nm in order], *[results[nm][3] for nm in order])
```

```python
import jax
import jax.numpy as jnp
from jax import lax
from jax.experimental import pallas as pl
from jax.experimental.pallas import tpu as pltpu

F32 = jnp.float32
BF16 = jnp.bfloat16
MESH = pl.DeviceIdType.MESH
ANY = pl.BlockSpec(memory_space=pl.ANY)

D_MODEL = 1024
N_META = 16
CHUNK = 128
PAD = CHUNK - N_META
POOL_WINDOWS = (2, 4, 8, 16)
D_POOL = 512
POOL_GROUP = 128
D_SSM = 1536
N_HEADS = 24
N_GROUPS = 4
HPG = 6
HEAD_DIM = 64
D_STATE = 128
GW = HPG * HEAD_DIM
D_XBC = D_SSM + 2 * N_GROUPS * D_STATE
D_DT = N_GROUPS * 128
D_FF = 4096
CONV_W = 4
EPS = 1e-5
LANES = 128
VMEM_LIMIT = 56 * 1024 * 1024

ADAM_LR, ADAM_B1, ADAM_B2, ADAM_EPS, ADAM_WD, ADAM_STEP = 0.001, 0.9, 0.999, 1e-08, 0.01, 10


def _params(*sem):
    return pltpu.CompilerParams(dimension_semantics=sem, vmem_limit_bytes=VMEM_LIMIT)


def _pick(n, cands):
    for c in cands:
        if n % c == 0:
            return c
    raise ValueError(f"no block size for {n}")


def _dot(a, b):
    return jnp.dot(a.astype(BF16), b.astype(BF16), preferred_element_type=F32)


def _dot_nt(a, b):
    return lax.dot_general(a.astype(BF16), b.astype(BF16), (((1,), (1,)), ((), ())), preferred_element_type=F32)


def _dot_tn(a, b):
    return lax.dot_general(a.astype(BF16), b.astype(BF16), (((0,), (0,)), ((), ())), preferred_element_type=F32)


def _dot_exact(mask, x):
    m = mask.astype(BF16)
    hi = x.astype(BF16)
    r1 = x - hi.astype(F32)
    mid = r1.astype(BF16)
    lo = (r1 - mid.astype(F32)).astype(BF16)
    dot = lambda t: jnp.dot(m, t, preferred_element_type=F32)
    return dot(hi) + dot(mid) + dot(lo)


def _sigmoid(x):
    return 1.0 / (1.0 + jnp.exp(-x))


def _softplus(x):
    return jnp.maximum(x, 0.0) + jnp.log1p(jnp.exp(-jnp.abs(x)))


def _sum_all(x):
    return jnp.sum(jnp.sum(x, axis=1, keepdims=True), axis=0, keepdims=True)


ROW_TILES = (1056, 768, 704, 512, 384, 256, 128)
TILE_BUDGET = 28 * 1024 * 1024


def _row_tile(n, bytes_per_row, fixed_bytes, budget=TILE_BUDGET):
    for tm in ROW_TILES:
        if n % tm == 0 and 2 * (tm * bytes_per_row + fixed_bytes) <= budget:
            return tm
    raise ValueError(f"no row tile for {n}")


WIDE_BUDGET = 38 * 1024 * 1024


def _mm(a, w, *, name, tn=512, nt=False, pre=None, post=None, extras=(), out_dtype=F32, norm_w=None, rider=None):
    assert norm_w is None or (rider is None and out_dtype == F32)
    a_list = list(a) if isinstance(a, (list, tuple)) else [a]
    w_list = list(w) if isinstance(w, (list, tuple)) else [w]
    n_a, n_ex = len(a_list), len(extras)
    n = a_list[0].shape[0]
    shard = w_list[0].shape[2] if w_list[0].ndim == 3 else None
    assert shard is None or (not nt and n_a == 1 and shard % tn == 0)
    m = w_list[0].shape[0] * shard if shard else w_list[0].shape[0] if nt else w_list[0].shape[1]
    tn = min(tn, m)
    size = lambda dt: jnp.dtype(dt).itemsize
    per_row = (sum(x.shape[1] * size(x.dtype) for x in a_list) + m * size(out_dtype)
               + sum(m * size(e.dtype) for e in extras) + (2 * m if norm_w is not None else 0))
    tm = _row_tile(n, per_row, sum(x.size * size(x.dtype) for x in w_list) // 2, WIDE_BUDGET)
    n_norm = 0 if norm_w is None else 1

    def body(*refs):
        a_refs, w_refs, ex_refs = refs[:n_a], refs[n_a:2 * n_a], refs[2 * n_a:2 * n_a + n_ex]
        o_ref = refs[2 * n_a + n_ex + n_norm]
        avs = [(a_ref[...] if pre is None else pre(a_ref[...])).astype(BF16) for a_ref in a_refs]
        for c0 in range(0, m, tn):
            r = None
            for av, w_ref in zip(avs, w_refs):
                if shard:
                    term = _dot(av, w_ref[c0 // shard, :, c0 % shard:c0 % shard + tn])
                else:
                    term = _dot_nt(av, w_ref[c0:c0 + tn, :]) if nt else _dot(av, w_ref[:, c0:c0 + tn])
                r = term if r is None else r + term
            if post is not None:
                r = post(r, *[e[:, c0:c0 + tn] for e in ex_refs])
            o_ref[:, c0:c0 + tn] = r.astype(out_dtype)
        if n_norm:
            x = o_ref[...]
            scale = lax.rsqrt(jnp.mean(x * x, axis=-1, keepdims=True) + EPS)
            refs[2 * n_a + n_ex + 2][...] = (x * scale * refs[2 * n_a + n_ex][...]).astype(BF16)

    a_specs = [pl.BlockSpec((tm, x.shape[1]), lambda i: (i, 0)) for x in a_list]
    w_specs = [pl.BlockSpec(x.shape, lambda i, nd=x.ndim: (0,) * nd, pipeline_mode=pl.Buffered(1)) for x in w_list]
    blk = pl.BlockSpec((tm, m), lambda i: (i, 0))
    vec = [pl.BlockSpec((1, m), lambda i: (0, 0))] * n_norm
    grid = (n // tm,)
    ride = _Ride(rider, body, 2 * n_a + n_ex + n_norm, 1 + n_norm, 0, grid)
    outs = pl.pallas_call(
        ride.body, name=name, grid=grid,
        in_specs=a_specs + w_specs + [blk] * n_ex + vec + ride.in_specs,
        out_specs=[blk] * (1 + n_norm) + ride.out_specs,
        out_shape=[jax.ShapeDtypeStruct((n, m), out_dtype)] + [jax.ShapeDtypeStruct((n, m), BF16)] * n_norm + ride.out_shape,
        scratch_shapes=ride.scratch, compiler_params=_params(*ride.semantics(("parallel",))),
    )(*a_list, *w_list, *extras, *([norm_w] * n_norm), *ride.args)
    if n_norm:
        return outs[0], outs[1]
    return (outs[0], outs[1:]) if rider else outs[0]


def _mm_tn(a, g, *, name, tk, tn, pre=None, slab=None, rider=None):
    n, k = a.shape
    m = g.shape[1]
    tk, tn = min(tk, k), min(tn, m)
    tm = _row_tile(n, tk * jnp.dtype(a.dtype).itemsize + tn * jnp.dtype(g.dtype).itemsize, tk * tn * 4)
    steps = n // tm

    def body(a_ref, g_ref, o_ref, acc_ref):
        r = pl.program_id(2)

        @pl.when(r == 0)
        def _():
            acc_ref[...] = jnp.zeros_like(acc_ref)

        av = a_ref[...]
        if pre is not None:
            av = pre(av)
        if slab:
            for s in range(tn // slab):
                acc_ref[s] += _dot_tn(av, g_ref[:, s * slab:(s + 1) * slab])
        else:
            acc_ref[...] += _dot_tn(av, g_ref[...])

        @pl.when(r == steps - 1)
        def _():
            o_ref[...] = acc_ref[...].astype(BF16)

    if slab:
        block, out_spec = (tn // slab, tk, slab), pl.BlockSpec((tn // slab, tk, slab), lambda i, j, r: (j, i, 0))
        out_shape = jax.ShapeDtypeStruct((m // slab, k, slab), BF16)
    else:
        block, out_spec = (tk, tn), pl.BlockSpec((tk, tn), lambda i, j, r: (i, j))
        out_shape = jax.ShapeDtypeStruct((k, m), BF16)
    grid = (k // tk, m // tn, steps)
    ride = _Ride(rider, body, 2, 1, 1, grid)
    outs = pl.pallas_call(
        ride.body, name=name, grid=grid,
        in_specs=[pl.BlockSpec((tm, tk), lambda i, j, r: (r, i)), pl.BlockSpec((tm, tn), lambda i, j, r: (r, j))]
        + ride.in_specs,
        out_specs=[out_spec] + ride.out_specs, out_shape=[out_shape] + ride.out_shape,
        scratch_shapes=[pltpu.VMEM(block, F32)] + ride.scratch,
        compiler_params=_params(*ride.semantics(("parallel", "parallel", "arbitrary"))),
    )(a, g, *ride.args)
    return (outs[0], outs[1:]) if rider else outs[0]


def _mm_rms_bwd(a, w, h, w_norm, dres, *, name):
    n, k = a.shape
    d = h.shape[1]
    slabs, _, ks = w.shape
    tm = _row_tile(n, k * jnp.dtype(a.dtype).itemsize + d * (4 + 4 + 4 + 2), d * k, WIDE_BUDGET)

    def body(a_ref, w_ref, h_ref, wn_ref, dres_ref, dx_ref, dxb_ref, dw_ref):
        @pl.when(pl.program_id(0) == 0)
        def _():
            dw_ref[...] = jnp.zeros_like(dw_ref)

        dyv = None
        for s in range(slabs):
            part = _dot_nt(a_ref[:, s * ks:(s + 1) * ks], w_ref[s])
            dyv = part if dyv is None else dyv + part
        x = h_ref[...]
        r = lax.rsqrt(jnp.mean(x * x, axis=-1, keepdims=True) + EPS)
        g = dyv * wn_ref[...]
        dx = r * (g - x * (r * r) * jnp.mean(g * x, axis=-1, keepdims=True)) + dres_ref[...]
        dx_ref[...] = dx
        dxb_ref[...] = dx.astype(BF16)
        dw_ref[...] += jnp.sum(dyv * x * r, axis=0, keepdims=True)

    row = pl.BlockSpec((tm, d), lambda i: (i, 0))
    vec = pl.BlockSpec((1, d), lambda i: (0, 0))
    return pl.pallas_call(
        body, name=name, grid=(n // tm,),
        in_specs=[pl.BlockSpec((tm, k), lambda i: (i, 0)),
                  pl.BlockSpec(w.shape, lambda i: (0, 0, 0), pipeline_mode=pl.Buffered(1)), row, vec, row],
        out_specs=[row, row, vec],
        out_shape=[jax.ShapeDtypeStruct((n, d), F32), jax.ShapeDtypeStruct((n, d), BF16), jax.ShapeDtypeStruct((1, d), F32)],
        compiler_params=_params("arbitrary"),
    )(a, w, h, w_norm, dres)


def _embed_norm(x, meta, w, *, name, rider=None):
    bsz, seq, d = x.shape
    t = seq + CHUNK
    nc = t // CHUNK

    def body(x_ref, meta_ref, w_ref, h_ref, hn_ref):
        j = pl.program_id(0)
        first = jnp.concatenate([jnp.zeros((PAD, d), F32), meta_ref[...]], axis=0)
        for e in range(bsz):
            h = jnp.where(j == 0, first, x_ref[e])
            r = lax.rsqrt(jnp.mean(h * h, axis=-1, keepdims=True) + EPS)
            h_ref[e] = h
            hn_ref[e] = (h * r * w_ref[...]).astype(BF16)

    row = pl.BlockSpec((bsz, CHUNK, d), lambda j: (0, j, 0))
    grid = (nc,)
    ride = _Ride(rider, body, 3, 2, 0, grid)
    outs = pl.pallas_call(
        ride.body, name=name, grid=grid,
        in_specs=[pl.BlockSpec((bsz, CHUNK, d), lambda j: (0, jnp.maximum(j - 1, 0), 0)),
                  pl.BlockSpec((N_META, d), lambda j: (0, 0)), pl.BlockSpec((1, d), lambda j: (0, 0))] + ride.in_specs,
        out_specs=[row, row] + ride.out_specs,
        out_shape=[jax.ShapeDtypeStruct((bsz, t, d), F32), jax.ShapeDtypeStruct((bsz, t, d), BF16)] + ride.out_shape,
        scratch_shapes=ride.scratch, compiler_params=_params(*ride.semantics(("parallel",))),
    )(x, meta, w, *ride.args)
    return outs[:2], outs[2:]


def _final_norm_loss(h2, target, w, *, name):
    bsz, t, d = h2.shape
    nc = t // CHUNK

    def body(h_ref, t_ref, w_ref, dh_ref, dhb_ref, loss_ref, dw_ref):
        j = pl.program_id(0)

        @pl.when(j == 0)
        def _():
            loss_ref[...] = jnp.zeros_like(loss_ref)
            dw_ref[...] = jnp.zeros_like(dw_ref)

        wv = w_ref[...]
        for e in range(bsz):
            x = h_ref[e]
            r = lax.rsqrt(jnp.mean(x * x, axis=-1, keepdims=True) + EPS)
            diff = jnp.where(j > 0, x * r * wv - t_ref[e], 0.0)
            loss_ref[...] += _sum_all(diff * diff) * (0.5 / d)
            dy = diff * (1.0 / d)
            g = dy * wv
            dh = r * (g - x * (r * r) * jnp.mean(g * x, axis=-1, keepdims=True))
            dh_ref[e] = dh
            dhb_ref[e] = dh.astype(BF16)
            dw_ref[...] += jnp.sum(dy * x * r, axis=0, keepdims=True)

    row = pl.BlockSpec((bsz, CHUNK, d), lambda j: (0, j, 0))
    return pl.pallas_call(
        body, name=name, grid=(nc,),
        in_specs=[row, pl.BlockSpec((bsz, CHUNK, d), lambda j: (0, jnp.maximum(j - 1, 0), 0)),
                  pl.BlockSpec((1, d), lambda j: (0, 0))],
        out_specs=[row, row, pl.BlockSpec((8, LANES), lambda j: (0, 0)), pl.BlockSpec((1, d), lambda j: (0, 0))],
        out_shape=[jax.ShapeDtypeStruct((bsz, t, d), F32), jax.ShapeDtypeStruct((bsz, t, d), BF16),
                   jax.ShapeDtypeStruct((8, LANES), F32), jax.ShapeDtypeStruct((1, d), F32)],
        compiler_params=_params("arbitrary"),
    )(h2, target, w)


def _pool_masks(j, transposed):
    r = lax.broadcasted_iota(jnp.int32, (CHUNK, 2 * CHUNK), 0)
    c = lax.broadcasted_iota(jnp.int32, (CHUNK, 2 * CHUNK), 1)
    masks = []
    for w in POOL_WINDOWS:
        if transposed:
            m = (c >= r) & (c < r + w)
        else:
            s = c - CHUNK
            m = (s <= r) & (s > r - w) & (s + j * CHUNK >= 0)
        masks.append(m.astype(F32))
    return masks


def _pool_count(t_global, w):
    return jnp.clip(t_global - PAD + 1, 1, w).astype(F32)


def _pool_fwd(u, pool_w, pool_scale, *, name):
    bsz, t, _ = u.shape
    nc = t // CHUNK

    def body(prev_ref, cur_ref, pw_ref, sc_ref, o_ref):
        j = pl.program_id(0)
        masks = _pool_masks(j, False)
        tg = j * CHUNK + lax.broadcasted_iota(jnp.int32, (CHUNK, 1), 0)
        count = [_pool_count(tg, w) for w in POOL_WINDOWS]
        units = [(e, gi) for e in range(bsz) for gi in range(len(POOL_WINDOWS))]
        sl = lambda gi: pl.ds(gi * POOL_GROUP, POOL_GROUP)
        cur = {(e, gi): cur_ref[e, :, sl(gi)] for e, gi in units}
        both = {(e, gi): jnp.concatenate([prev_ref[e, :, sl(gi)], cur[e, gi]], axis=0) for e, gi in units}
        win = {(e, gi): _dot_exact(masks[gi], both[e, gi]) for e, gi in units}
        pooled = {(e, gi): win[e, gi] / count[gi] - cur[e, gi] for e, gi in units}
        mixed = {(e, gi): _dot(pooled[e, gi], pw_ref[gi]) for e, gi in units}
        for e, gi in units:
            o_ref[e, :, sl(gi)] = (mixed[e, gi] * sc_ref[:, sl(gi)]).astype(BF16)

    blk = lambda f: pl.BlockSpec((bsz, CHUNK, D_POOL), f)
    return pl.pallas_call(
        body, name=name, grid=(nc,),
        in_specs=[blk(lambda j: (0, jnp.maximum(j - 1, 0), 0)), blk(lambda j: (0, j, 0)),
                  pl.BlockSpec((4, POOL_GROUP, POOL_GROUP), lambda j: (0, 0, 0)),
                  pl.BlockSpec((1, D_POOL), lambda j: (0, 0))],
        out_specs=blk(lambda j: (0, j, 0)), out_shape=jax.ShapeDtypeStruct(u.shape, BF16),
        compiler_params=_params("parallel"),
    )(u, u, pool_w, pool_scale)


def _pool_bwd(u, dyp, pool_w, pool_scale, *, name):
    bsz, t, _ = u.shape
    nc = t // CHUNK

    def body(prev_ref, cur_ref, dy_ref, dyn_ref, pw_ref, sc_ref, du_ref, dpw_ref, dsc_ref):
        j = pl.program_id(0)

        @pl.when(j == 0)
        def _():
            dpw_ref[...] = jnp.zeros_like(dpw_ref)
            dsc_ref[...] = jnp.zeros_like(dsc_ref)

        fwd = _pool_masks(j, False)
        bwd = _pool_masks(j, True)
        tg = j * CHUNK + lax.broadcasted_iota(jnp.int32, (CHUNK, 1), 0)
        count = [_pool_count(tg, w) for w in POOL_WINDOWS]
        count_next = [_pool_count(tg + CHUNK, w) for w in POOL_WINDOWS]
        has_next = j < nc - 1
        groups = range(len(POOL_WINDOWS))
        units = [(e, gi) for e in range(bsz) for gi in groups]
        sl = lambda gi: pl.ds(gi * POOL_GROUP, POOL_GROUP)
        cur = {(e, gi): cur_ref[e, :, sl(gi)] for e, gi in units}
        both = {(e, gi): jnp.concatenate([prev_ref[e, :, sl(gi)], cur[e, gi]], axis=0) for e, gi in units}
        win = {(e, gi): _dot_exact(fwd[gi], both[e, gi]) for e, gi in units}
        pooled = {(e, gi): win[e, gi] / count[gi] - cur[e, gi] for e, gi in units}
        dy = {(e, gi): dy_ref[e, :, sl(gi)] for e, gi in units}
        mixed = {(e, gi): _dot(pooled[e, gi], pw_ref[gi]) for e, gi in units}
        dm = {(e, gi): dy[e, gi] * sc_ref[:, sl(gi)] for e, gi in units}
        dm_next = {(e, gi): jnp.where(has_next, dyn_ref[e, :, sl(gi)], 0.0) * sc_ref[:, sl(gi)] for e, gi in units}
        dpw = {(e, gi): _dot_tn(pooled[e, gi], dm[e, gi]) for e, gi in units}
        dpooled = {(e, gi): _dot_nt(dm[e, gi], pw_ref[gi]) for e, gi in units}
        dpooled_next = {(e, gi): _dot_nt(dm_next[e, gi], pw_ref[gi]) for e, gi in units}
        spread = {(e, gi): jnp.concatenate([dpooled[e, gi] / count[gi], dpooled_next[e, gi] / count_next[gi]], axis=0)
                  for e, gi in units}
        back = {(e, gi): _dot_exact(bwd[gi], spread[e, gi]) for e, gi in units}
        for e, gi in units:
            du_ref[e, :, sl(gi)] = (back[e, gi] - dpooled[e, gi]).astype(BF16)
        for gi in groups:
            dsc, dw = None, None
            for e in range(bsz):
                term = jnp.sum(dy[e, gi] * mixed[e, gi], axis=0, keepdims=True)
                dsc = term if dsc is None else dsc + term
                dw = dpw[e, gi] if dw is None else dw + dpw[e, gi]
            dsc_ref[:, sl(gi)] += dsc
            dpw_ref[gi] += dw

    blk = lambda f: pl.BlockSpec((bsz, CHUNK, D_POOL), f)
    return pl.pallas_call(
        body, name=name, grid=(nc,),
        in_specs=[blk(lambda j: (0, jnp.maximum(j - 1, 0), 0)), blk(lambda j: (0, j, 0)),
                  blk(lambda j: (0, j, 0)), blk(lambda j: (0, jnp.minimum(j + 1, nc - 1), 0)),
                  pl.BlockSpec((4, POOL_GROUP, POOL_GROUP), lambda j: (0, 0, 0)),
                  pl.BlockSpec((1, D_POOL), lambda j: (0, 0))],
        out_specs=[blk(lambda j: (0, j, 0)), pl.BlockSpec((4, POOL_GROUP, POOL_GROUP), lambda j: (0, 0, 0)),
                   pl.BlockSpec((1, D_POOL), lambda j: (0, 0))],
        out_shape=[jax.ShapeDtypeStruct(u.shape, BF16), jax.ShapeDtypeStruct((4, POOL_GROUP, POOL_GROUP), F32),
                   jax.ShapeDtypeStruct((1, D_POOL), F32)],
        compiler_params=_params("arbitrary"),
    )(u, u, dyp, dyp, pool_w, pool_scale)


CONV_SLAB = 512


def _conv_taps(tail, cur, keep_tail):
    ext = jnp.concatenate([jnp.where(keep_tail, tail, 0.0), cur], axis=0)
    return [(pltpu.roll(ext, CONV_W - 1 - k, 0) if k < CONV_W - 1 else ext)[8:] for k in range(CONV_W)]


def _conv_pre(taps, w_ref, b_ref, sl):
    acc = b_ref[:, sl]
    for k in range(CONV_W):
        acc = acc + w_ref[k:k + 1, sl] * taps[k]
    return acc


def _proj_conv(hn, w, conv_w, conv_b, *, name):
    n, d = hn.shape
    c = w.shape[0]
    assert PAD >= CONV_W - 1
    tm = _row_tile(n, d * 2 + c * (4 + 2), c * d, WIDE_BUDGET)

    def body(hn_ref, w_ref, cw_ref, cb_ref, xbc_ref, xc_ref, tail_ref):
        @pl.when(pl.program_id(0) == 0)
        def _():
            tail_ref[...] = jnp.zeros_like(tail_ref)

        av = hn_ref[...]
        starts = list(range(0, c, CONV_SLAB))

        def project(c0):
            xbc_ref[:, pl.ds(c0, CONV_SLAB)] = _dot_nt(av, w_ref[c0:c0 + CONV_SLAB, :])

        def convolve(c0):
            sl = pl.ds(c0, CONV_SLAB)
            xb = xbc_ref[:, sl]
            pre = _conv_pre(_conv_taps(tail_ref[:, sl], xb, True), cw_ref, cb_ref, sl)
            xc_ref[:, sl] = (pre * _sigmoid(pre)).astype(BF16)
            tail_ref[:, sl] = xb[tm - 8:, :]

        project(starts[0])
        for c0, c_next in zip(starts, starts[1:] + [None]):
            if c_next is not None:
                project(c_next)
            convolve(c0)

    row = lambda width: pl.BlockSpec((tm, width), lambda i: (i, 0))
    return pl.pallas_call(
        body, name=name, grid=(n // tm,),
        in_specs=[row(d), pl.BlockSpec(w.shape, lambda i: (0, 0), pipeline_mode=pl.Buffered(1)),
                  pl.BlockSpec((CONV_W, c), lambda i: (0, 0)), pl.BlockSpec((1, c), lambda i: (0, 0))],
        out_specs=[row(c), row(c)],
        out_shape=[jax.ShapeDtypeStruct((n, c), F32), jax.ShapeDtypeStruct((n, c), BF16)],
        scratch_shapes=[pltpu.VMEM((8, c), F32)],
        compiler_params=_params("arbitrary"),
    )(hn, w, conv_w, conv_b)


def _conv_bwd(xbc, dxs, db, dc, conv_w, conv_b, *, name):
    bsz, t, c = xbc.shape
    nc = t // CHUNK
    halo = 16
    rows = CHUNK + halo

    def body(tail_ref, cur_ref, head_ref, dxs_ref, db_ref, dc_ref, dxs_head, db_head, dc_head, w_ref, b_ref,
             dx_ref, dwb_ref):
        j = pl.program_id(1)

        @pl.when(j == 0)
        def _():
            dwb_ref[...] = jnp.zeros_like(dwb_ref)

        has_prev, has_next = j > 0, j < nc - 1
        for c0 in range(0, c, CONV_SLAB):
            sl = pl.ds(c0, CONV_SLAB)
            if c0 < D_SSM:
                dxc, dxc_next = dxs_ref[0, :, sl], dxs_head[0, :, sl]
            elif c0 < D_SSM + D_POOL:
                dxc, dxc_next = db_ref[0], db_head[0]
            else:
                dxc, dxc_next = dc_ref[0], dc_head[0]
            dxc = jnp.concatenate([dxc.astype(F32), jnp.where(has_next, dxc_next.astype(F32), 0.0)], axis=0)
            ext = jnp.concatenate([jnp.where(has_prev, tail_ref[0, :, sl], 0.0), cur_ref[0, :, sl],
                                   jnp.where(has_next, head_ref[0, :, sl], 0.0)], axis=0)
            taps = [(pltpu.roll(ext, CONV_W - 1 - k, 0) if k < CONV_W - 1 else ext)[8:] for k in range(CONV_W)]
            pre = _conv_pre(taps, w_ref, b_ref, sl)
            s = _sigmoid(pre)
            dpre = dxc * (s * (1.0 + pre * (1.0 - s)))
            acc = w_ref[CONV_W - 1:CONV_W, sl] * dpre[:CHUNK]
            for k in range(CONV_W - 1):
                up = CONV_W - 1 - k
                acc = acc + w_ref[k:k + 1, sl] * pltpu.roll(dpre, rows - up, 0)[:CHUNK]
            dx_ref[0, :, sl] = acc.astype(BF16)
            for k in range(CONV_W):
                dwb_ref[0, k:k + 1, sl] += jnp.sum(dpre[:CHUNK] * taps[k][:CHUNK], axis=0, keepdims=True)
            dwb_ref[0, CONV_W:CONV_W + 1, sl] += jnp.sum(dpre[:CHUNK], axis=0, keepdims=True)

    assert CONV_SLAB == D_POOL and D_SSM % CONV_SLAB == 0
    row = lambda width: pl.BlockSpec((1, CHUNK, width), lambda b, j: (b, j, 0))
    nxt = lambda width: pl.BlockSpec(
        (1, halo, width), lambda b, j: (b, jnp.minimum((j + 1) * (CHUNK // halo), t // halo - 1), 0))
    return pl.pallas_call(
        body, name=name, grid=(bsz, nc),
        in_specs=[pl.BlockSpec((1, 8, c), lambda b, j: (b, jnp.maximum(j * (CHUNK // 8) - 1, 0), 0)), row(c), nxt(c),
                  row(D_SSM), row(D_POOL), row(D_POOL), nxt(D_SSM), nxt(D_POOL), nxt(D_POOL),
                  pl.BlockSpec((CONV_W, c), lambda b, j: (0, 0)), pl.BlockSpec((1, c), lambda b, j: (0, 0))],
        out_specs=[row(c), pl.BlockSpec((1, 8, c), lambda b, j: (b, 0, 0))],
        out_shape=[jax.ShapeDtypeStruct(xbc.shape, BF16), jax.ShapeDtypeStruct((bsz, 8, c), F32)],
        compiler_params=_params("parallel", "arbitrary"),
    )(xbc, xbc, xbc, dxs, db, dc, dxs, db, dc, conv_w, conv_b)


def _dt_valid(j):
    lane = lax.broadcasted_iota(jnp.int32, (CHUNK, LANES), 1)
    row = lax.broadcasted_iota(jnp.int32, (CHUNK, LANES), 0)
    return (lane < HPG) & ((j > 0) | (row >= PAD))


def _ssd_prep(dtr, dtb, alog, *, name):
    bsz, t, _ = dtr.shape
    nc = t // CHUNK

    def body(dtr_ref, dtb_ref, alog_ref, dt_ref, acs_ref, tr_ref):
        j = pl.program_id(0)
        valid = _dt_valid(j)
        row = lax.broadcasted_iota(jnp.int32, (CHUNK, LANES), 0)
        lane = lax.broadcasted_iota(jnp.int32, (CHUNK, LANES), 1)
        tril = (row >= lane).astype(F32)
        units = [(e, g) for e in range(bsz) for g in range(N_GROUPS)]
        sl = lambda g: pl.ds(g * LANES, LANES)
        dt = {(e, g): jnp.where(valid, _softplus(dtr_ref[e, :, sl(g)] + dtb_ref[g]), 0.0) for e, g in units}
        acs = {(e, g): _dot_exact(tril, dt[e, g] * -jnp.exp(alog_ref[g])) for e, g in units}
        for e, g in units:
            dt_ref[e, :, sl(g)] = dt[e, g]
            acs_ref[e, :, sl(g)] = acs[e, g]
            tr_ref[e, 0, g, 0:8, :] = dt[e, g].T[0:8]
            tr_ref[e, 0, g, 8:16, :] = acs[e, g].T[0:8]

    blk = pl.BlockSpec((bsz, CHUNK, D_DT), lambda j: (0, j, 0))
    const = pl.BlockSpec((N_GROUPS, 1, LANES), lambda j: (0, 0, 0))
    return pl.pallas_call(
        body, name=name, grid=(nc,), in_specs=[blk, const, const],
        out_specs=[blk, blk, pl.BlockSpec((bsz, 1, N_GROUPS, 16, LANES), lambda j: (0, j, 0, 0, 0))],
        out_shape=[jax.ShapeDtypeStruct(dtr.shape, F32), jax.ShapeDtypeStruct(dtr.shape, F32),
                   jax.ShapeDtypeStruct((bsz, nc, N_GROUPS, 16, LANES), F32)],
        compiler_params=_params("parallel"),
    )(dtr, dtb, alog)


def _ssd_decay(dt, acs, tr):
    lane = lax.broadcasted_iota(jnp.int32, (CHUNK, LANES), 1)
    row = lax.broadcasted_iota(jnp.int32, (CHUNK, LANES), 0)
    return dict(lane=lane, row=row, dt=dt, causal=row >= lane, acs=acs, acs_t=tr[8:16], dt_t=tr[0:8],
                aend=acs[CHUNK - 1:CHUNK, :])


def _ssd_specs(bsz, nc, rev):
    ch = (lambda j: nc - 1 - j) if rev else (lambda j: j)
    return dict(
        xs=pl.BlockSpec((bsz, CHUNK, GW), lambda g, j: (0, ch(j), g)),
        bm=pl.BlockSpec((bsz, CHUNK, D_STATE), lambda g, j: (0, ch(j), D_SSM // D_STATE + g)),
        cm=pl.BlockSpec((bsz, CHUNK, D_STATE), lambda g, j: (0, ch(j), D_SSM // D_STATE + N_GROUPS + g)),
        lane_blk=pl.BlockSpec((bsz, CHUNK, LANES), lambda g, j: (0, ch(j), g)),
        grp_const=pl.BlockSpec((1, 1, LANES), lambda g, j: (g, 0, 0)),
        grp_vec=pl.BlockSpec((1, GW), lambda g, j: (0, g)),
        state=pl.BlockSpec((bsz, 1, D_STATE, GW), lambda g, j: (0, ch(j), 0, g)),
        tr=pl.BlockSpec((bsz, 1, 1, 16, LANES), lambda g, j: (0, ch(j), g, 0, 0)),
    )


def _ssd_fwd(xc, dt, acs, tr, z, dskip, normw, *, name, rider=None):
    bsz, t, _ = xc.shape
    nc = t // CHUNK
    sp = _ssd_specs(bsz, nc, False)

    def body(xs_ref, b_ref, c_ref, dt_ref, acs_ref, tr_ref, z_ref, dsk_ref, nw_ref, yn_ref, y_ref, sp_ref, s_ref):
        j = pl.program_id(1)

        @pl.when(j == 0)
        def _():
            s_ref[...] = jnp.zeros_like(s_ref)

        ex = range(bsz)
        units = [(e, r) for e in ex for r in range(HPG)]
        full = lambda v: jnp.broadcast_to(v, (CHUNK, LANES))
        pair = lambda r: pl.ds((r // 2) * LANES, LANES)
        q = [_ssd_decay(dt_ref[e], acs_ref[e], tr_ref[e, 0, 0]) for e in ex]
        for e in ex:
            sp_ref[e, 0] = s_ref[e]
        bm, cm = [b_ref[e] for e in ex], [c_ref[e] for e in ex]
        cb = [_dot_nt(cm[e], bm[e]) for e in ex]
        low = q[0]["lane"] < HEAD_DIM
        col = {(e, r): full(q[e]["acs"][:, r:r + 1]) for e, r in units}
        aend = {(e, r): q[e]["aend"][:, r:r + 1] for e, r in units}
        decay = {(e, r): jnp.exp(jnp.where(q[e]["causal"], col[e, r] - q[e]["acs_t"][r:r + 1, :], -jnp.inf))
                 for e, r in units}
        mp = {(e, r): cb[e] * decay[e, r] * q[e]["dt_t"][r:r + 1, :] for e, r in units}
        ce = {(e, r): cm[e] * jnp.exp(col[e, r]) for e, r in units}
        bk = {(e, r): bm[e] * (jnp.exp(aend[e, r] - col[e, r]) * full(q[e]["dt"][:, r:r + 1])) for e, r in units}
        xp = {(e, r): xs_ref[e, :, pair(r)] for e, r in units}
        s_old = {(e, r): s_ref[e, :, pair(r)] for e, r in units}
        y_h = {u: _dot(mp[u], xp[u]) + _dot(ce[u], s_old[u]) for u in units}
        s_h = {u: jnp.exp(aend[u]) * s_old[u] + _dot_tn(bk[u], xp[u]) for u in units}
        for e in ex:
            for r in range(0, HPG, 2):
                y_ref[e, :, pair(r)] = jnp.where(low, y_h[e, r], y_h[e, r + 1])
                s_ref[e, :, pair(r)] = jnp.where(low, s_h[e, r], s_h[e, r + 1])
        y = [y_ref[e] + dsk_ref[...] * xs_ref[e] for e in ex]
        zz = [z_ref[e] for e in ex]
        yg = [y[e] * (zz[e] * _sigmoid(zz[e])) for e in ex]
        rstd = [lax.rsqrt(jnp.mean(yg[e] * yg[e], axis=-1, keepdims=True) + EPS) for e in ex]
        for e in ex:
            y_ref[e] = y[e]
            yn_ref[e] = (yg[e] * rstd[e] * nw_ref[...]).astype(BF16)

    grid = (N_GROUPS, nc)
    ride = _Ride(rider, body, 9, 3, 1, grid)
    outs = pl.pallas_call(
        ride.body, name=name, grid=grid,
        in_specs=[sp["xs"], sp["bm"], sp["cm"], sp["lane_blk"], sp["lane_blk"], sp["tr"], sp["xs"],
                  sp["grp_vec"], sp["grp_vec"]] + ride.in_specs,
        out_specs=[sp["xs"], sp["xs"], sp["state"]] + ride.out_specs,
        out_shape=[jax.ShapeDtypeStruct((bsz, t, D_SSM), BF16), jax.ShapeDtypeStruct((bsz, t, D_SSM), F32),
                   jax.ShapeDtypeStruct((bsz, nc, D_STATE, D_SSM), F32)] + ride.out_shape,
        scratch_shapes=[pltpu.VMEM((bsz, D_STATE, GW), F32)] + ride.scratch,
        compiler_params=_params(*ride.semantics(("parallel", "arbitrary"))),
    )(xc, xc, xc, dt, acs, tr, z, dskip, normw, *ride.args)
    return outs[:3], outs[3:]


def _ssd_bwd(xc, dtr, dt, acs, tr, z, ypre, sprev, dyn, dtb, alog, dskip, normw, *, name, rider=None):
    bsz, t, _ = xc.shape
    nc = t // CHUNK
    sp = _ssd_specs(bsz, nc, True)

    def body(xs_ref, b_ref, c_ref, dtr_ref, dt_ref, acs_ref, tr_ref, z_ref, y_ref, sp_ref, dyn_ref, dtb_ref, alog_ref,
             dsk_ref, nw_ref, dz_ref, dxs_ref, db_ref, dc_ref, ddt_ref, dnw_ref, dsm_ref, ds_ref):
        j = pl.program_id(1)

        @pl.when(j == 0)
        def _():
            ds_ref[...] = jnp.zeros_like(ds_ref)
            dnw_ref[...] = jnp.zeros_like(dnw_ref)
            dsm_ref[...] = jnp.zeros_like(dsm_ref)

        ex = range(bsz)
        heads = range(HPG)
        units = [(e, r) for e in ex for r in heads]
        q = [_ssd_decay(dt_ref[e], acs_ref[e], tr_ref[e, 0, 0]) for e in ex]
        a = -jnp.exp(alog_ref[0])
        valid = _dt_valid(nc - 1 - j)
        lane, row = q[0]["lane"], q[0]["row"]
        lane1 = lane[0:1, :]
        nw = nw_ref[...]
        y, zz, dyn = [y_ref[e] for e in ex], [z_ref[e] for e in ex], [dyn_ref[e] for e in ex]
        sz = [_sigmoid(zz[e]) for e in ex]
        sil = [zz[e] * sz[e] for e in ex]
        yg = [y[e] * sil[e] for e in ex]
        rstd = [lax.rsqrt(jnp.mean(yg[e] * yg[e], axis=-1, keepdims=True) + EPS) for e in ex]
        gn = [dyn[e] * nw for e in ex]
        dyg = [rstd[e] * (gn[e] - yg[e] * (rstd[e] * rstd[e]) * jnp.mean(gn[e] * yg[e], axis=-1, keepdims=True))
               for e in ex]
        dy = [dyg[e] * sil[e] for e in ex]
        xs = [xs_ref[e] for e in ex]
        for e in ex:
            dnw_ref[e] += jnp.sum(dyn[e] * yg[e] * rstd[e], axis=0, keepdims=True)
            dz_ref[e] = (dyg[e] * y[e] * (sz[e] * (1.0 + zz[e] * (1.0 - sz[e])))).astype(BF16)
        dskip_cols = [jnp.sum(dy[e] * xs[e], axis=0, keepdims=True) for e in ex]

        bm, cm = [b_ref[e] for e in ex], [c_ref[e] for e in ex]
        cb = [_dot_nt(cm[e], bm[e]) for e in ex]
        zero = jnp.zeros((CHUNK, LANES), F32)
        full = lambda v: jnp.broadcast_to(v, (CHUNK, LANES))
        low = lane < HEAD_DIM
        half = [low if r % 2 == 0 else ~low for r in heads]
        sl = lambda v, r: v[:, (r // 2) * LANES:(r // 2 + 1) * LANES]
        pair = lambda r: pl.ds((r // 2) * LANES, LANES)
        col = {(e, r): full(q[e]["acs"][:, r:r + 1]) for e, r in units}
        dt_col = {(e, r): full(q[e]["dt"][:, r:r + 1]) for e, r in units}
        aend = {(e, r): q[e]["aend"][:, r:r + 1] for e, r in units}
        dt_row = {(e, r): q[e]["dt_t"][r:r + 1, :] for e, r in units}
        decay = {(e, r): jnp.exp(jnp.where(q[e]["causal"], col[e, r] - q[e]["acs_t"][r:r + 1, :], -jnp.inf))
                 for e, r in units}
        ea = {u: jnp.exp(col[u]) for u in units}
        dte = {u: jnp.exp(aend[u] - col[u]) for u in units}
        ed = {u: jnp.exp(aend[u]) for u in units}
        k = {u: dte[u] * dt_col[u] for u in units}
        mp = {(e, r): cb[e] * decay[e, r] * dt_row[e, r] for e, r in units}
        xp = {(e, r): sl(xs[e], r) for e, r in units}
        dym = {(e, r): jnp.where(half[r], sl(dy[e], r), 0.0) for e, r in units}
        s_old = {(e, r): sp_ref[e, 0, :, pair(r)] for e, r in units}
        ds_old = {(e, r): ds_ref[e, :, pair(r)] for e, r in units}
        dsm = {(e, r): jnp.where(half[r], ds_old[e, r], 0.0) for e, r in units}
        gmat = {u: _dot_nt(dym[u], xp[u]) for u in units}
        t1 = {u: _dot_nt(dym[u], s_old[u]) for u in units}
        dbs = {u: _dot_nt(xp[u], dsm[u]) for u in units}
        dx = {(e, r): _dot_tn(mp[e, r], dym[e, r]) + _dot(bm[e] * k[e, r], dsm[e, r]) for e, r in units}
        ds = {(e, r): _dot_tn(cm[e] * ea[e, r], dym[e, r]) for e, r in units}
        gd = {u: gmat[u] * decay[u] for u in units}
        w0 = {(e, r): gd[e, r] * cb[e] for e, r in units}
        cs0 = {u: jnp.sum(w0[u], axis=0, keepdims=True) for u in units}
        rs = {u: jnp.sum(w0[u] * dt_row[u], axis=1, keepdims=True) for u in units}
        qv = {(e, r): jnp.sum(cm[e] * t1[e, r], axis=1, keepdims=True) for e, r in units}
        dk = {(e, r): jnp.sum(bm[e] * dbs[e, r], axis=1, keepdims=True) for e, r in units}
        ddte = {u: dk[u] * dt_col[u] for u in units}
        d_aend = {u: _sum_all(dsm[u] * s_old[u]) * ed[u] + _sum_all(ddte[u][:, 0:1] * dte[u][:, 0:1]) for u in units}
        last_row = row == CHUNK - 1
        dacs_col = {u: rs[u] + qv[u] * ea[u] - ddte[u] * dte[u] + jnp.where(last_row, d_aend[u], 0.0) for u in units}
        triu = (lane >= row).astype(F32)
        for e in ex:
            dcb, dc_acc, db_acc = zero, zero, zero
            dacs, dacs_t, ddt, ddt_t = zero, zero, zero, zero
            dskip_row = jnp.zeros((1, LANES), F32)
            for r in heads:
                u = (e, r)
                dcb = dcb + gd[u] * dt_row[u]
                dc_acc = dc_acc + ea[u] * t1[u]
                db_acc = db_acc + k[u] * dbs[u]
                dacs = jnp.where(lane == r, dacs_col[u], dacs)
                ddt = jnp.where(lane == r, dk[u] * dte[u], ddt)
                dacs_t = jnp.where(row == r, -cs0[u] * dt_row[u], dacs_t)
                ddt_t = jnp.where(row == r, cs0[u], ddt_t)
                dsk = _sum_all(jnp.where(half[r][0:1, :], sl(dskip_cols[e], r), 0.0))
                dskip_row = dskip_row + jnp.where(lane1 == r, dsk, 0.0)
            for r in range(0, HPG, 2):
                dxs_ref[e, :, pair(r)] = (dx[e, r] + dx[e, r + 1] + sl(dy[e], r) * dsk_ref[:, pair(r)]).astype(BF16)
                ed_pair = jnp.where(lane1 < HEAD_DIM, ed[e, r], ed[e, r + 1])
                ds_ref[e, :, pair(r)] = ds[e, r] + ds[e, r + 1] + ed_pair * ds_old[e, r]
            dacs = dacs + dacs_t.T
            ddt = ddt + ddt_t.T
            dda = _dot_exact(triu, dacs)
            ddt = ddt + dda * a
            da = jnp.sum(dda * q[e]["dt"], axis=0, keepdims=True)
            draw = jnp.where(valid, ddt * _sigmoid(dtr_ref[e] + dtb_ref[0]), 0.0)
            ddt_ref[e] = draw.astype(BF16)
            dsm_ref[e, 0, 0:1, :] += dskip_row
            dsm_ref[e, 0, 1:2, :] += da * a
            dsm_ref[e, 0, 2:3, :] += jnp.sum(draw, axis=0, keepdims=True)
            dc_ref[e] = (dc_acc + _dot(dcb, bm[e])).astype(BF16)
            db_ref[e] = (db_acc + _dot_tn(dcb, cm[e])).astype(BF16)

    grp_out = pl.BlockSpec((bsz, CHUNK, D_STATE), lambda g, j: (0, nc - 1 - j, g))
    grid = (N_GROUPS, nc)
    ride = _Ride(rider, body, 15, 7, 1, grid)
    outs = pl.pallas_call(
        ride.body, name=name, grid=grid,
        in_specs=[sp["xs"], sp["bm"], sp["cm"], sp["lane_blk"], sp["lane_blk"], sp["lane_blk"], sp["tr"], sp["xs"],
                  sp["xs"], sp["state"], sp["xs"], sp["grp_const"], sp["grp_const"], sp["grp_vec"], sp["grp_vec"]]
        + ride.in_specs,
        out_specs=[sp["xs"], sp["xs"], grp_out, grp_out, sp["lane_blk"],
                   pl.BlockSpec((bsz, 1, GW), lambda g, j: (0, 0, g)),
                   pl.BlockSpec((bsz, 1, 8, LANES), lambda g, j: (0, g, 0, 0))] + ride.out_specs,
        out_shape=[jax.ShapeDtypeStruct((bsz, t, D_SSM), BF16), jax.ShapeDtypeStruct((bsz, t, D_SSM), BF16),
                   jax.ShapeDtypeStruct((bsz, t, N_GROUPS * D_STATE), BF16),
                   jax.ShapeDtypeStruct((bsz, t, N_GROUPS * D_STATE), BF16),
                   jax.ShapeDtypeStruct((bsz, t, D_DT), BF16), jax.ShapeDtypeStruct((bsz, 1, D_SSM), F32),
                   jax.ShapeDtypeStruct((bsz, N_GROUPS, 8, LANES), F32)] + ride.out_shape,
        scratch_shapes=[pltpu.VMEM((bsz, D_STATE, GW), F32)] + ride.scratch,
        compiler_params=_params(*ride.semantics(("parallel", "arbitrary"))),
    )(xc, xc, xc, dtr, dt, acs, tr, z, ypre, sprev, dyn, dtb, alog, dskip, normw, *ride.args)
    return outs[:7], outs[7:]


def _input_grad(dhn, h0, w, dres, seq, *, name):
    bsz, t, d = h0.shape
    nc = t // CHUNK

    def body(dy_ref, h_ref, w_ref, dres_ref, gx_ref, head_ref, dw_ref):
        j = pl.program_id(0)

        @pl.when(j == 0)
        def _():
            dw_ref[...] = jnp.zeros_like(dw_ref)

        for e in range(bsz):
            x, dyv = h_ref[e], dy_ref[e]
            r = lax.rsqrt(jnp.mean(x * x, axis=-1, keepdims=True) + EPS)
            g = dyv * w_ref[...]
            dx = r * (g - x * (r * r) * jnp.mean(g * x, axis=-1, keepdims=True)) + dres_ref[e]
            dw_ref[...] += jnp.sum(dyv * x * r, axis=0, keepdims=True)
            gx_ref[e] = dx

        @pl.when(j == 0)
        def _():
            head_ref[...] = gx_ref[...]

    row = pl.BlockSpec((bsz, CHUNK, d), lambda j: (0, j, 0))
    return pl.pallas_call(
        body, name=name, grid=(nc,),
        in_specs=[row, row, pl.BlockSpec((1, d), lambda j: (0, 0)), row],
        out_specs=[pl.BlockSpec((bsz, CHUNK, d), lambda j: (0, jnp.maximum(j - 1, 0), 0)),
                   pl.BlockSpec((bsz, CHUNK, d), lambda j: (0, 0, 0)), pl.BlockSpec((1, d), lambda j: (0, 0))],
        out_shape=[jax.ShapeDtypeStruct((bsz, seq, d), F32), jax.ShapeDtypeStruct((bsz, CHUNK, d), F32),
                   jax.ShapeDtypeStruct((1, d), F32)],
        compiler_params=_params("arbitrary"),
    )(dhn, h0, w, dres)


def _remote(src, dst, send_sem, recv_sem, dev):
    return pltpu.make_async_remote_copy(src_ref=src, dst_ref=dst, send_sem=send_sem, recv_sem=recv_sem,
                                        device_id=dev, device_id_type=MESH)


def _position():
    return lax.axis_index("x"), lax.axis_index("y"), lax.axis_index("c")


def _other_chips(pos):
    x, y, _ = pos
    return [(1 - x, y), (x, 1 - y), (1 - x, 1 - y)]


class _Gather:
    def __init__(self, arrs):
        n = len(arrs)
        self.args, self.n_in, self.n_out = list(arrs), n, n
        self.split = [a.ndim == 2 and a.shape[1] % (2 * LANES) == 0 for a in arrs]
        self.out_shape = [jax.ShapeDtypeStruct((4,) + a.shape, a.dtype) for a in arrs]
        self.scratch = [pltpu.SemaphoreType.DMA((3 * n,)), pltpu.SemaphoreType.DMA((3 * n,)),
                        pltpu.SemaphoreType.DMA((n,)), pltpu.SemaphoreType.DMA((3 * n,)),
                        pltpu.SemaphoreType.DMA((3 * n,))]

    def _copies(self, pos, ins, outs, sems):
        send_sems, recv_sems, loc_sems, pass_send_sems, pass_recv_sems = sems
        x, y, c = pos
        me, sibling = 2 * x + y, (x, y, 1 - c)
        local = [pltpu.make_async_copy(ins[i], outs[i].at[me], loc_sems.at[i]) for i in range(self.n_in)]
        sends, recvs, passes, pass_recvs = [], [], [], []
        for i in range(self.n_in):
            half = self.args[i].shape[1] // 2 if self.split[i] else None
            for k, (px, py) in enumerate(_other_chips(pos)):
                them = 2 * px + py
                sems_k = (send_sems.at[3 * i + k], recv_sems.at[3 * i + k], (px, py, c))
                if half is None:
                    sends.append(_remote(ins[i], outs[i].at[me], *sems_k))
                    recvs.append(_remote(ins[i], outs[i].at[them], *sems_k))
                    passes.append(None)
                    continue
                mine = pl.ds(pl.multiple_of(c * half, LANES), half)
                other = pl.ds(pl.multiple_of((1 - c) * half, LANES), half)
                sends.append(_remote(ins[i].at[:, mine], outs[i].at[me, :, mine], *sems_k))
                recvs.append(_remote(ins[i].at[:, mine], outs[i].at[them, :, mine], *sems_k))
                pass_k = (pass_send_sems.at[3 * i + k], pass_recv_sems.at[3 * i + k], sibling)
                passes.append(_remote(outs[i].at[them, :, mine], outs[i].at[them, :, mine], *pass_k))
                pass_recvs.append(_remote(outs[i].at[them, :, other], outs[i].at[them, :, other], *pass_k))
        return local, sends, recvs, passes, pass_recvs

    def start(self, pos, ins, outs, sems):
        local, sends = self._copies(pos, ins, outs, sems)[:2]
        for cp in local + sends:
            cp.start()

    def relay(self, pos, ins, outs, sems):
        _, _, recvs, passes, _ = self._copies(pos, ins, outs, sems)
        for cp, onward in zip(recvs, passes):
            if onward is not None:
                cp.wait_recv()
                onward.start()

    def finish(self, pos, ins, outs, sems):
        local, sends, recvs, passes, pass_recvs = self._copies(pos, ins, outs, sems)
        for cp, onward in zip(recvs, passes):
            if onward is None:
                cp.wait_recv()
        for cp in pass_recvs:
            cp.wait_recv()
        for cp in sends + [p for p in passes if p is not None]:
            cp.wait_send()
        for cp in local:
            cp.wait()


class _Exchange:
    FLIPS = [(fx, fy, fc) for fx in (0, 1) for fy in (0, 1) for fc in (0, 1)][1:]

    def __init__(self, big, small=None):
        n = len(big)
        self.n_big, self.has_small = n, small is not None
        self.args = list(big) + ([small] if self.has_small else [])
        self.n_in = self.n_out = len(self.args)
        self.out_shape = [jax.ShapeDtypeStruct(a.shape, a.dtype) for a in big]
        self.scratch = [pltpu.SemaphoreType.DMA((max(3 * n, 1),)), pltpu.SemaphoreType.DMA((max(3 * n, 1),)),
                        pltpu.SemaphoreType.DMA((n + 1,))]
        if self.has_small:
            self.out_shape.append(jax.ShapeDtypeStruct((8,) + small.shape, small.dtype))
            self.scratch += [pltpu.SemaphoreType.DMA((7,)), pltpu.SemaphoreType.DMA((7,))]

    def _copies(self, pos, ins, outs, sems):
        x, y, c = pos
        me, me8 = 2 * x + y, 4 * x + 2 * y + c
        local, sends, recvs = [], [], []
        for i in range(self.n_big):
            local.append(pltpu.make_async_copy(ins[i].at[me], outs[i].at[me], sems[2].at[i]))
            for k, (px, py) in enumerate(_other_chips(pos)):
                sems_k = (sems[0].at[3 * i + k], sems[1].at[3 * i + k], (px, py, c))
                sends.append(_remote(ins[i].at[2 * px + py], outs[i].at[me], *sems_k))
                recvs.append(_remote(ins[i].at[me], outs[i].at[2 * px + py], *sems_k))
        if self.has_small:
            small, landed = ins[self.n_big], outs[self.n_big]
            local.append(pltpu.make_async_copy(small, landed.at[me8], sems[2].at[self.n_big]))
            for k, (fx, fy, fc) in enumerate(self.FLIPS):
                peer = (x ^ fx, y ^ fy, c ^ fc)
                sems_k = (sems[3].at[k], sems[4].at[k], peer)
                sends.append(_remote(small, landed.at[me8], *sems_k))
                recvs.append(_remote(small, landed.at[4 * peer[0] + 2 * peer[1] + peer[2]], *sems_k))
        return local, sends, recvs, [None] * len(recvs), []

    start = _Gather.start
    relay = _Gather.relay
    finish = _Gather.finish


class _Swap:
    def __init__(self, arrs):
        n = len(arrs)
        self.args, self.n_in, self.n_out = list(arrs), n, n
        self.out_shape = [jax.ShapeDtypeStruct(a.shape, a.dtype) for a in arrs]
        self.scratch = [pltpu.SemaphoreType.DMA((n,)), pltpu.SemaphoreType.DMA((n,))]

    def _copies(self, pos, ins, outs, sems):
        x, y, c = pos
        both = [_remote(ins[i], outs[i], sems[0].at[i], sems[1].at[i], (x, y, 1 - c)) for i in range(self.n_in)]
        return [], both, both, [None] * len(both), []

    start = _Gather.start
    relay = _Gather.relay
    finish = _Gather.finish


def _comm(rider, *, name):
    a, b = rider.n_in, rider.n_in + rider.n_out

    def body(*refs):
        pos = _position()
        rider.start(pos, refs[:a], refs[a:b], refs[b:])
        rider.relay(pos, refs[:a], refs[a:b], refs[b:])
        rider.finish(pos, refs[:a], refs[a:b], refs[b:])

    return pl.pallas_call(body, name=name, in_specs=[ANY] * rider.n_in, out_specs=[ANY] * rider.n_out,
                          out_shape=rider.out_shape, scratch_shapes=rider.scratch)(*rider.args)


class _Ride:
    RELAY_AT = 0.8

    def __init__(self, rider, body, n_in, n_out, n_scratch, grid):
        self.rider = rider
        self.args = rider.args if rider else []
        self.in_specs = [ANY] * rider.n_in if rider else []
        self.out_specs = [ANY] * rider.n_out if rider else []
        self.out_shape = rider.out_shape if rider else []
        self.scratch = rider.scratch if rider else []
        self.body = self._wrap(body, n_in, n_out, n_scratch, grid) if rider else body

    def semantics(self, sem):
        return ("arbitrary",) * len(sem) if self.rider else sem

    def _wrap(self, body, n_in, n_out, n_scratch, grid):
        rider = self.rider
        a = n_in
        b = a + rider.n_in
        c = b + n_out
        d = c + rider.n_out
        e = d + n_scratch

        def wrapped(*refs):
            pos = _position()
            ids = [pl.program_id(i) for i in range(len(grid))]
            step, total = 0, 1
            for i, g in zip(ids, grid):
                step, total = step * g + i, total * g

            @pl.when(step == 0)
            def _():
                rider.start(pos, refs[a:b], refs[c:d], refs[e:])

            body(*refs[:a], *refs[b:c], *refs[d:e])

            @pl.when(step == int(self.RELAY_AT * (total - 1)))
            def _():
                rider.relay(pos, refs[a:b], refs[c:d], refs[e:])

            @pl.when(step == total - 1)
            def _():
                rider.finish(pos, refs[a:b], refs[c:d], refs[e:])

        return wrapped


def _elementwise_tiles(r, c):
    if r % 8 == 0 and r * c > 65536:
        tm = _pick(r, (256, 128, 64, 16, 8))
        return (tm, c), r // tm, lambda i: (i, 0)
    if r % 8 and c % 256 == 0 and r * c > 65536:
        return (r, 256), c // 256, lambda i: (0, i)
    return (r, c), 1, lambda i: (0, 0)


def _chip_sum(landed, *, name):
    _, r, c = landed.shape
    blk, steps, at = _elementwise_tiles(r, c)

    def body(land_ref, o_ref):
        acc = land_ref[0].astype(F32)
        for jchip in range(1, 4):
            acc = acc + land_ref[jchip].astype(F32)
        o_ref[...] = acc

    return pl.pallas_call(
        body, name=name, grid=(steps,), in_specs=[pl.BlockSpec((4,) + blk, lambda i: (0,) + at(i))],
        out_specs=pl.BlockSpec(blk, at), out_shape=jax.ShapeDtypeStruct((r, c), F32),
        compiler_params=_params("parallel"),
    )(landed)


def _device_sum(parts, *, name):
    _, r, c = parts.shape

    def body(p_ref, o_ref):
        acc = p_ref[0]
        for d in range(1, 8):
            acc = acc + p_ref[d]
        o_ref[...] = acc

    return pl.pallas_call(body, name=name, out_shape=jax.ShapeDtypeStruct((r, c), F32))(parts)


def _adamw_math(w, g, m, v):
    m = ADAM_B1 * m + (1.0 - ADAM_B1) * g
    v = ADAM_B2 * v + (1.0 - ADAM_B2) * (g * g)
    m_hat = m / (1.0 - ADAM_B1 ** ADAM_STEP)
    v_hat = v / (1.0 - ADAM_B2 ** ADAM_STEP)
    return -ADAM_LR * (m_hat / (jnp.sqrt(v_hat) + ADAM_EPS) + ADAM_WD * w), m, v


def _adamw(w, g_parts, m, v, *, name):
    r, c = w.shape
    shape, steps, at = _elementwise_tiles(r, c)
    n_g = len(g_parts)

    def body(*refs):
        w_ref, m_ref, v_ref = refs[n_g:n_g + 3]
        g_ref, d_ref, nm_ref, nv_ref = refs[n_g + 3:]
        g = refs[0][...]
        for p in refs[1:n_g]:
            g = g + p[...]
        g_ref[...] = g
        d_ref[...], nm_ref[...], nv_ref[...] = _adamw_math(w_ref[...], g, m_ref[...], v_ref[...])

    blk = pl.BlockSpec(shape, at)
    return pl.pallas_call(
        body, name=name, grid=(steps,), in_specs=[blk] * (n_g + 3), out_specs=[blk] * 4,
        out_shape=[jax.ShapeDtypeStruct((r, c), F32)] * 4, compiler_params=_params("parallel"),
    )(*g_parts, w, m, v)


def _pad_heads(v):
    return jnp.pad(v.reshape(N_GROUPS, 1, HPG), ((0, 0), (0, 0), (0, LANES - HPG)))


def _unpad_heads(v):
    return v[:, :HPG].reshape(1, N_HEADS)


_SMALL_EARLY = [("pool_w", (512, 128)), ("pool_scale", (1, 512)), ("conv_w", (4, D_XBC)), ("conv_b", (1, D_XBC)),
                ("dt_bias", (1, N_HEADS)), ("a_log", (1, N_HEADS)), ("d_skip", (1, N_HEADS)), ("ssm_norm_w", (1, D_SSM)),
                ("norm_ffn_w", (1, 1024)), ("norm_f_w", (1, 1024))]
_SMALL_LATE = [("norm_mix_w", (1, 1024)), ("meta", (N_META, 1024)), ("loss", (1, 1))]


def _pack_small(grads, layout):
    rows = []
    for nm, shape in layout:
        flat = grads[nm].reshape(-1)
        rows.append(jnp.pad(flat, (0, (-flat.size) % LANES)).reshape(-1, LANES))
    packed = jnp.concatenate(rows, axis=0)
    return jnp.pad(packed, ((0, (-packed.shape[0]) % 8), (0, 0)))


def _unpack_small(packed, layout):
    out, r0 = {}, 0
    for nm, shape in layout:
        size = shape[0] * shape[1]
        nrow = -(-size // LANES)
        out[nm] = packed[r0:r0 + nrow].reshape(-1)[:size].reshape(shape)
        r0 += nrow
    return out


def kernel(x, meta, norm_mix_w, w_in, pool_w, pool_scale, conv_w, conv_b, dt_bias, a_log, d_skip, ssm_norm_w, w_out, norm_ffn_w, w_ff1, w_ff2, norm_f_w, loss_target, m_meta, m_norm_mix_w, m_w_in, m_pool_w, m_pool_scale, m_conv_w, m_conv_b, m_dt_bias, m_a_log, m_d_skip, m_ssm_norm_w, m_w_out, m_norm_ffn_w, m_w_ff1, m_w_ff2, m_norm_f_w, v_meta, v_norm_mix_w, v_w_in, v_pool_w, v_pool_scale, v_conv_w, v_conv_b, v_dt_bias, v_a_log, v_d_skip, v_ssm_norm_w, v_w_out, v_norm_ffn_w, v_w_ff1, v_w_ff2, v_norm_f_w):
    bsz, seq, d = x.shape
    t = seq + CHUNK
    n = bsz * t
    chip = 2 * lax.axis_index("x") + lax.axis_index("y")
    d_in = w_in.shape[2] * 4

    g_conv, g_meta = _comm(_Gather([conv_w[0], meta]), name="gather_small")
    convw = g_conv.transpose(1, 0, 2).reshape(CONV_W, D_XBC)
    meta_full = g_meta.transpose(1, 0, 2).reshape(N_META, d)
    (h0, hn1), (g_in,) = _embed_norm(x, meta_full, norm_mix_w, name="embed_norm",
                                     rider=_Gather([w_in[0].T.astype(BF16)]))
    h0f, hn1 = h0.reshape(n, d), hn1.reshape(n, d)
    late_weights = _Gather([w_out[0].astype(BF16), w_ff1[0].astype(BF16), w_ff2[0].astype(BF16)])
    win = g_in.reshape(d_in, d)
    wu, wz = win[:D_POOL], win[D_POOL:D_POOL + D_SSM]
    wx = win[D_POOL + D_SSM:D_POOL + D_SSM + D_XBC]
    wdt = jnp.pad(win[D_POOL + D_SSM + D_XBC:].reshape(N_GROUPS, HPG, d),
                  ((0, 0), (0, LANES - HPG), (0, 0))).reshape(D_DT, d)
    dtb, alog = _pad_heads(dt_bias), _pad_heads(a_log)
    dskip = jnp.repeat(d_skip, HEAD_DIM, axis=1)
    poolw = pool_w[0]

    u = _mm(hn1, wu, name="proj_u", nt=True)
    z = _mm(hn1, wz, name="proj_z", nt=True)
    xbc, xc = _proj_conv(hn1, wx, convw, conv_b, name="proj_xbc")
    dtr = _mm(hn1, wdt, name="proj_dt", nt=True)
    ypool = _pool_fwd(u.reshape(bsz, t, D_POOL), poolw, pool_scale, name="pool_fwd")
    xbc3 = xbc.reshape(bsz, t, D_XBC)
    xc = xc.reshape(bsz, t, D_XBC)
    z3, dtr3 = z.reshape(bsz, t, D_SSM), dtr.reshape(bsz, t, D_DT)
    dt3, acs3, tr3 = _ssd_prep(dtr3, dtb, alog, name="ssd_prep")
    (yn, ypre, sprev), (g_out, g_ff1, g_ff2) = _ssd_fwd(xc, dt3, acs3, tr3, z3, dskip, ssm_norm_w, name="ssd_fwd",
                                                        rider=late_weights)
    wo = g_out.reshape(D_POOL + D_SSM, d)
    wo_p, wo_s = wo[:D_POOL], wo[D_POOL:]
    w1 = g_ff1
    w2 = g_ff2.reshape(D_FF, d)
    ypool_f, yn_f = ypool.reshape(n, D_POOL), yn.reshape(n, D_SSM)
    add = lambda r, e: r + e
    h1, hn2 = _mm([ypool_f, yn_f], [wo_p, wo_s], name="out_proj", post=add, extras=(h0f,), norm_w=norm_ffn_w)
    act = _mm(hn2, w1, name="ff1", out_dtype=BF16)
    relu2 = lambda a: jnp.square(jnp.maximum(a, 0))
    h2 = _mm(act, w2, name="ff2", pre=relu2, post=add, extras=(h1,))
    dh2, dh2b, loss_acc, d_norm_f = _final_norm_loss(h2.reshape(bsz, t, d), loss_target, norm_f_w.reshape(1, d),
                                                     name="loss")

    dh2f, dh2bf = dh2.reshape(n, d), dh2b.reshape(n, d)
    dact = _mm(dh2bf, w2, name="ff2_bwd", nt=True, post=lambda r, a: r * (2.0 * jnp.maximum(a, 0).astype(F32)),
               extras=(act,), out_dtype=BF16)
    d_w2 = _mm_tn(act, dh2bf, name="ff2_dw", tk=2048, tn=1024, pre=relu2)
    d_w1 = _mm_tn(hn2, dact, name="ff1_dw", tk=1024, tn=2048, slab=D_FF // 4)
    dh1, dh1b, d_norm_ffn = _mm_rms_bwd(dact, w1, h1, norm_ffn_w, dh2f, name="ff1_bwd")
    dypool = _mm(dh1b, wo_p, name="out_pool_bwd", nt=True)
    dyn = _mm(dh1b, wo_s, name="out_ssm_bwd", nt=True)
    d_wo_p = _mm_tn(ypool_f, dh1b, name="out_pool_dw", tk=512, tn=1024)
    d_wo_s = _mm_tn(yn_f, dh1b, name="out_ssm_dw", tk=1536, tn=1024)
    big_late = [jnp.concatenate([d_wo_p, d_wo_s], axis=0).reshape(4, (D_POOL + D_SSM) // 4, d),
                d_w1, d_w2.reshape(4, D_FF // 4, d)]
    (dz, dxs, dbm, dcm, ddtr, d_nw, d_heads), landed_late = _ssd_bwd(
        xc, dtr3, dt3, acs3, tr3, z3, ypre, sprev, dyn.reshape(bsz, t, D_SSM), dtb, alog, dskip, ssm_norm_w, name="ssd_bwd",
        rider=_Exchange(big_late))
    dxbc, d_convwb = _conv_bwd(xbc3, dxs, dbm, dcm, convw, conv_b, name="conv_bwd")
    du, d_poolw, d_poolsc = _pool_bwd(u.reshape(bsz, t, D_POOL), dypool.reshape(bsz, t, D_POOL), poolw, pool_scale,
                                      name="pool_bwd")
    duf, dzf, dxbcf, ddtrf = du.reshape(n, D_POOL), dz.reshape(n, D_SSM), dxbc.reshape(n, D_XBC), ddtr.reshape(n, D_DT)
    heads = jnp.sum(d_heads, axis=0)
    small_early = _pack_small({
        "pool_w": d_poolw, "pool_scale": d_poolsc,
        "conv_w": jnp.sum(d_convwb[:, :CONV_W], axis=0), "conv_b": jnp.sum(d_convwb[:, CONV_W:CONV_W + 1], axis=0),
        "dt_bias": _unpad_heads(heads[:, 2]), "a_log": _unpad_heads(heads[:, 1]), "d_skip": _unpad_heads(heads[:, 0]),
        "ssm_norm_w": jnp.sum(d_nw, axis=0), "norm_ffn_w": d_norm_ffn, "norm_f_w": d_norm_f}, _SMALL_EARLY)
    d_wu = _mm_tn(duf, hn1, name="proj_u_dw", tk=512, tn=1024)
    d_wz = _mm_tn(dzf, hn1, name="proj_z_dw", tk=1536, tn=1024)
    d_wx, (early_all,) = _mm_tn(dxbcf, hn1, name="proj_xbc_dw", tk=1280, tn=1024, rider=_Exchange([], small_early))
    d_wdt = _mm_tn(ddtrf, hn1, name="proj_dt_dw", tk=512, tn=1024)
    d_win = jnp.concatenate([d_wu, d_wz, d_wx, d_wdt.reshape(N_GROUPS, LANES, d)[:, :HPG].reshape(N_HEADS, d)], axis=0)
    big_in = d_win.reshape(4, d_in // 4, d)
    dhn1, (landed_in,) = _mm([duf, dzf, dxbcf, ddtrf], [wu, wz, wx, wdt], name="proj_bwd",
                             rider=_Exchange([big_in]))
    grad_x, d_head_rows, d_norm_mix = _input_grad(
        dhn1.reshape(bsz, t, d), h0, norm_mix_w, dh1.reshape(bsz, t, d), seq, name="input_grad")

    landed = [landed_in] + list(landed_late)
    small_late = _pack_small({"norm_mix_w": d_norm_mix, "meta": jnp.sum(d_head_rows[:, PAD:], axis=0),
                              "loss": loss_acc[0:1, 0:1]}, _SMALL_LATE)
    (late_all,) = _comm(_Exchange([], small_late), name="exchange_small")
    mine = [_chip_sum(l, name=f"chip_sum_{i}") for i, l in enumerate(landed)]
    theirs = _comm(_Swap(mine), name="swap_cores")
    gsmall = {**_unpack_small(_device_sum(early_all, name="device_sum_early"), _SMALL_EARLY),
              **_unpack_small(_device_sum(late_all, name="device_sum_late"), _SMALL_LATE)}
    gsmall["conv_w"] = lax.dynamic_slice_in_dim(gsmall["conv_w"], chip * (D_XBC // 4), D_XBC // 4, axis=1)
    gsmall["meta"] = lax.dynamic_slice_in_dim(gsmall["meta"], chip * (d // 4), d // 4, axis=1)
    loss = gsmall["loss"][0, 0]

    given = dict(meta=(meta, m_meta, v_meta), norm_mix_w=(norm_mix_w, m_norm_mix_w, v_norm_mix_w),
                 w_in=(w_in, m_w_in, v_w_in), pool_w=(pool_w, m_pool_w, v_pool_w),
                 pool_scale=(pool_scale, m_pool_scale, v_pool_scale), conv_w=(conv_w, m_conv_w, v_conv_w),
                 conv_b=(conv_b, m_conv_b, v_conv_b), dt_bias=(dt_bias, m_dt_bias, v_dt_bias),
                 a_log=(a_log, m_a_log, v_a_log), d_skip=(d_skip, m_d_skip, v_d_skip),
                 ssm_norm_w=(ssm_norm_w, m_ssm_norm_w, v_ssm_norm_w), w_out=(w_out, m_w_out, v_w_out),
                 norm_ffn_w=(norm_ffn_w, m_norm_ffn_w, v_norm_ffn_w), w_ff1=(w_ff1, m_w_ff1, v_w_ff1),
                 w_ff2=(w_ff2, m_w_ff2, v_w_ff2), norm_f_w=(norm_f_w, m_norm_f_w, v_norm_f_w))
    big_names = ["w_in", "w_out", "w_ff1", "w_ff2"]
    results = {}
    for nm, (w, m, v) in given.items():
        if nm in big_names:
            i = big_names.index(nm)
            parts, shape2 = (mine[i], theirs[i]), mine[i].shape
        else:
            parts, shape2 = (gsmall[nm],), gsmall[nm].shape
        if nm == "w_in":
            outs = _adamw(w[0].T, parts, m[0].T, v[0].T, name=f"adamw_{nm}")
            results[nm] = [o.T[None] for o in outs]
        else:
            outs = _adamw(w.reshape(shape2), parts, m.reshape(shape2), v.reshape(shape2), name=f"adamw_{nm}")
            results[nm] = [o.reshape(w.shape) for o in outs]
    order = list(given)
    return (loss, grad_x, *[results[nm][0] for nm in order], *[results[nm][1] for nm in order],
            *[results[nm][2] for nm in order], *[results[nm][3] for nm in order])
```

```python
import jax
import jax.numpy as jnp
from jax import lax
from jax.experimental import pallas as pl
from jax.experimental.pallas import tpu as pltpu

F32 = jnp.float32
BF16 = jnp.bfloat16
MESH = pl.DeviceIdType.MESH
ANY = pl.BlockSpec(memory_space=pl.ANY)

D_MODEL = 1024
N_META = 16
CHUNK = 128
PAD = CHUNK - N_META
POOL_WINDOWS = (2, 4, 8, 16)
D_POOL = 512
POOL_GROUP = 128
D_SSM = 1536
N_HEADS = 24
N_GROUPS = 4
HPG = 6
HEAD_DIM = 64
D_STATE = 128
GW = HPG * HEAD_DIM
D_XBC = D_SSM + 2 * N_GROUPS * D_STATE
D_DT = N_GROUPS * 128
D_FF = 4096
CONV_W = 4
EPS = 1e-5
LANES = 128
VMEM_LIMIT = 56 * 1024 * 1024

ADAM_LR, ADAM_B1, ADAM_B2, ADAM_EPS, ADAM_WD, ADAM_STEP = 0.001, 0.9, 0.999, 1e-08, 0.01, 10


def _params(*sem):
    return pltpu.CompilerParams(dimension_semantics=sem, vmem_limit_bytes=VMEM_LIMIT)


def _pick(n, cands):
    for c in cands:
        if n % c == 0:
            return c
    raise ValueError(f"no block size for {n}")


def _dot(a, b):
    return jnp.dot(a.astype(BF16), b.astype(BF16), preferred_element_type=F32)


def _dot_nt(a, b):
    return lax.dot_general(a.astype(BF16), b.astype(BF16), (((1,), (1,)), ((), ())), preferred_element_type=F32)


def _dot_tn(a, b):
    return lax.dot_general(a.astype(BF16), b.astype(BF16), (((0,), (0,)), ((), ())), preferred_element_type=F32)


def _dot_exact(mask, x):
    m = mask.astype(BF16)
    hi = x.astype(BF16)
    r1 = x - hi.astype(F32)
    mid = r1.astype(BF16)
    lo = (r1 - mid.astype(F32)).astype(BF16)
    dot = lambda t: jnp.dot(m, t, preferred_element_type=F32)
    return dot(hi) + dot(mid) + dot(lo)


def _sigmoid(x):
    return 1.0 / (1.0 + jnp.exp(-x))


def _softplus(x):
    return jnp.maximum(x, 0.0) + jnp.log1p(jnp.exp(-jnp.abs(x)))


def _sum_all(x):
    return jnp.sum(jnp.sum(x, axis=1, keepdims=True), axis=0, keepdims=True)


ROW_TILES = (1056, 768, 704, 512, 384, 256, 128)
TILE_BUDGET = 28 * 1024 * 1024


def _row_tile(n, bytes_per_row, fixed_bytes, budget=TILE_BUDGET):
    for tm in ROW_TILES:
        if n % tm == 0 and 2 * (tm * bytes_per_row + fixed_bytes) <= budget:
            return tm
    raise ValueError(f"no row tile for {n}")


WIDE_BUDGET = 38 * 1024 * 1024


def _mm(a, w, *, name, tn=512, nt=False, pre=None, post=None, extras=(), out_dtype=F32, norm_w=None, rider=None):
    assert norm_w is None or (rider is None and out_dtype == F32)
    a_list = list(a) if isinstance(a, (list, tuple)) else [a]
    w_list = list(w) if isinstance(w, (list, tuple)) else [w]
    n_a, n_ex = len(a_list), len(extras)
    n = a_list[0].shape[0]
    shard = w_list[0].shape[2] if w_list[0].ndim == 3 else None
    assert shard is None or (not nt and n_a == 1 and shard % tn == 0)
    m = w_list[0].shape[0] * shard if shard else w_list[0].shape[0] if nt else w_list[0].shape[1]
    tn = min(tn, m)
    size = lambda dt: jnp.dtype(dt).itemsize
    per_row = (sum(x.shape[1] * size(x.dtype) for x in a_list) + m * size(out_dtype)
               + sum(m * size(e.dtype) for e in extras) + (2 * m if norm_w is not None else 0))
    tm = _row_tile(n, per_row, sum(x.size * size(x.dtype) for x in w_list) // 2, WIDE_BUDGET)
    n_norm = 0 if norm_w is None else 1

    def body(*refs):
        a_refs, w_refs, ex_refs = refs[:n_a], refs[n_a:2 * n_a], refs[2 * n_a:2 * n_a + n_ex]
        o_ref = refs[2 * n_a + n_ex + n_norm]
        avs = [(a_ref[...] if pre is None else pre(a_ref[...])).astype(BF16) for a_ref in a_refs]
        for c0 in range(0, m, tn):
            r = None
            for av, w_ref in zip(avs, w_refs):
                if shard:
                    term = _dot(av, w_ref[c0 // shard, :, c0 % shard:c0 % shard + tn])
                else:
                    term = _dot_nt(av, w_ref[c0:c0 + tn, :]) if nt else _dot(av, w_ref[:, c0:c0 + tn])
                r = term if r is None else r + term
            if post is not None:
                r = post(r, *[e[:, c0:c0 + tn] for e in ex_refs])
            o_ref[:, c0:c0 + tn] = r.astype(out_dtype)
        if n_norm:
            x = o_ref[...]
            scale = lax.rsqrt(jnp.mean(x * x, axis=-1, keepdims=True) + EPS)
            refs[2 * n_a + n_ex + 2][...] = (x * scale * refs[2 * n_a + n_ex][...]).astype(BF16)

    a_specs = [pl.BlockSpec((tm, x.shape[1]), lambda i: (i, 0)) for x in a_list]
    w_specs = [pl.BlockSpec(x.shape, lambda i, nd=x.ndim: (0,) * nd, pipeline_mode=pl.Buffered(1)) for x in w_list]
    blk = pl.BlockSpec((tm, m), lambda i: (i, 0))
    vec = [pl.BlockSpec((1, m), lambda i: (0, 0))] * n_norm
    grid = (n // tm,)
    ride = _Ride(rider, body, 2 * n_a + n_ex + n_norm, 1 + n_norm, 0, grid)
    outs = pl.pallas_call(
        ride.body, name=name, grid=grid,
        in_specs=a_specs + w_specs + [blk] * n_ex + vec + ride.in_specs,
        out_specs=[blk] * (1 + n_norm) + ride.out_specs,
        out_shape=[jax.ShapeDtypeStruct((n, m), out_dtype)] + [jax.ShapeDtypeStruct((n, m), BF16)] * n_norm + ride.out_shape,
        scratch_shapes=ride.scratch, compiler_params=_params(*ride.semantics(("parallel",))),
    )(*a_list, *w_list, *extras, *([norm_w] * n_norm), *ride.args)
    if n_norm:
        return outs[0], outs[1]
    return (outs[0], outs[1:]) if rider else outs[0]


def _mm_fanout(a, ws, *, name, tn=512):
    n, k = a.shape
    ms = [w.shape[0] for w in ws]
    tm = _row_tile(n, k * 2 + 4 * sum(ms), sum(w.size for w in ws), WIDE_BUDGET)
    n_w = len(ws)

    def body(a_ref, *refs):
        av = a_ref[...]
        for w_ref, o_ref, m in zip(refs[:n_w], refs[n_w:], ms):
            for c0 in range(0, m, tn):
                o_ref[:, c0:c0 + tn] = _dot_nt(av, w_ref[c0:c0 + tn, :])

    return pl.pallas_call(
        body, name=name, grid=(n // tm,),
        in_specs=[pl.BlockSpec((tm, k), lambda i: (i, 0))]
        + [pl.BlockSpec(w.shape, lambda i: (0, 0), pipeline_mode=pl.Buffered(1)) for w in ws],
        out_specs=[pl.BlockSpec((tm, m), lambda i: (i, 0)) for m in ms],
        out_shape=[jax.ShapeDtypeStruct((n, m), F32) for m in ms],
        compiler_params=_params("parallel"),
    )(a, *ws)


def _mm_tn(a, g, *, name, tk, tn, pre=None, slab=None, rider=None):
    n, k = a.shape
    m = g.shape[1]
    tk, tn = min(tk, k), min(tn, m)
    tm = _row_tile(n, tk * jnp.dtype(a.dtype).itemsize + tn * jnp.dtype(g.dtype).itemsize, tk * tn * 4)
    steps = n // tm

    def body(a_ref, g_ref, o_ref, acc_ref):
        r = pl.program_id(2)

        @pl.when(r == 0)
        def _():
            acc_ref[...] = jnp.zeros_like(acc_ref)

        av = a_ref[...]
        if pre is not None:
            av = pre(av)
        if slab:
            for s in range(tn // slab):
                acc_ref[s] += _dot_tn(av, g_ref[:, s * slab:(s + 1) * slab])
        else:
            acc_ref[...] += _dot_tn(av, g_ref[...])

        @pl.when(r == steps - 1)
        def _():
            o_ref[...] = acc_ref[...].astype(BF16)

    if slab:
        block, out_spec = (tn // slab, tk, slab), pl.BlockSpec((tn // slab, tk, slab), lambda i, j, r: (j, i, 0))
        out_shape = jax.ShapeDtypeStruct((m // slab, k, slab), BF16)
    else:
        block, out_spec = (tk, tn), pl.BlockSpec((tk, tn), lambda i, j, r: (i, j))
        out_shape = jax.ShapeDtypeStruct((k, m), BF16)
    grid = (k // tk, m // tn, steps)
    ride = _Ride(rider, body, 2, 1, 1, grid)
    outs = pl.pallas_call(
        ride.body, name=name, grid=grid,
        in_specs=[pl.BlockSpec((tm, tk), lambda i, j, r: (r, i)), pl.BlockSpec((tm, tn), lambda i, j, r: (r, j))]
        + ride.in_specs,
        out_specs=[out_spec] + ride.out_specs, out_shape=[out_shape] + ride.out_shape,
        scratch_shapes=[pltpu.VMEM(block, F32)] + ride.scratch,
        compiler_params=_params(*ride.semantics(("parallel", "parallel", "arbitrary"))),
    )(a, g, *ride.args)
    return (outs[0], outs[1:]) if rider else outs[0]


def _mm_rms_bwd(a, w, h, w_norm, dres, *, name):
    n, k = a.shape
    d = h.shape[1]
    slabs, _, ks = w.shape
    tm = _row_tile(n, k * jnp.dtype(a.dtype).itemsize + d * (4 + 4 + 4 + 2), d * k, WIDE_BUDGET)

    def body(a_ref, w_ref, h_ref, wn_ref, dres_ref, dx_ref, dxb_ref, dw_ref):
        @pl.when(pl.program_id(0) == 0)
        def _():
            dw_ref[...] = jnp.zeros_like(dw_ref)

        dyv = None
        for s in range(slabs):
            part = _dot_nt(a_ref[:, s * ks:(s + 1) * ks], w_ref[s])
            dyv = part if dyv is None else dyv + part
        x = h_ref[...]
        r = lax.rsqrt(jnp.mean(x * x, axis=-1, keepdims=True) + EPS)
        g = dyv * wn_ref[...]
        dx = r * (g - x * (r * r) * jnp.mean(g * x, axis=-1, keepdims=True)) + dres_ref[...]
        dx_ref[...] = dx
        dxb_ref[...] = dx.astype(BF16)
        dw_ref[...] += jnp.sum(dyv * x * r, axis=0, keepdims=True)

    row = pl.BlockSpec((tm, d), lambda i: (i, 0))
    vec = pl.BlockSpec((1, d), lambda i: (0, 0))
    return pl.pallas_call(
        body, name=name, grid=(n // tm,),
        in_specs=[pl.BlockSpec((tm, k), lambda i: (i, 0)),
                  pl.BlockSpec(w.shape, lambda i: (0, 0, 0), pipeline_mode=pl.Buffered(1)), row, vec, row],
        out_specs=[row, row, vec],
        out_shape=[jax.ShapeDtypeStruct((n, d), F32), jax.ShapeDtypeStruct((n, d), BF16), jax.ShapeDtypeStruct((1, d), F32)],
        compiler_params=_params("arbitrary"),
    )(a, w, h, w_norm, dres)


def _embed_norm(x, meta, w, *, name, rider=None):
    bsz, seq, d = x.shape
    t = seq + CHUNK
    nc = t // CHUNK

    def body(x_ref, meta_ref, w_ref, h_ref, hn_ref):
        j = pl.program_id(0)
        first = jnp.concatenate([jnp.zeros((PAD, d), F32), meta_ref[...]], axis=0)
        for e in range(bsz):
            h = jnp.where(j == 0, first, x_ref[e])
            r = lax.rsqrt(jnp.mean(h * h, axis=-1, keepdims=True) + EPS)
            h_ref[e] = h
            hn_ref[e] = (h * r * w_ref[...]).astype(BF16)

    row = pl.BlockSpec((bsz, CHUNK, d), lambda j: (0, j, 0))
    grid = (nc,)
    ride = _Ride(rider, body, 3, 2, 0, grid)
    outs = pl.pallas_call(
        ride.body, name=name, grid=grid,
        in_specs=[pl.BlockSpec((bsz, CHUNK, d), lambda j: (0, jnp.maximum(j - 1, 0), 0)),
                  pl.BlockSpec((N_META, d), lambda j: (0, 0)), pl.BlockSpec((1, d), lambda j: (0, 0))] + ride.in_specs,
        out_specs=[row, row] + ride.out_specs,
        out_shape=[jax.ShapeDtypeStruct((bsz, t, d), F32), jax.ShapeDtypeStruct((bsz, t, d), BF16)] + ride.out_shape,
        scratch_shapes=ride.scratch, compiler_params=_params(*ride.semantics(("parallel",))),
    )(x, meta, w, *ride.args)
    return outs[:2], outs[2:]


def _final_norm_loss(h2, target, w, *, name):
    bsz, t, d = h2.shape
    nc = t // CHUNK

    def body(h_ref, t_ref, w_ref, dh_ref, dhb_ref, loss_ref, dw_ref):
        j = pl.program_id(0)

        @pl.when(j == 0)
        def _():
            loss_ref[...] = jnp.zeros_like(loss_ref)
            dw_ref[...] = jnp.zeros_like(dw_ref)

        wv = w_ref[...]
        for e in range(bsz):
            x = h_ref[e]
            r = lax.rsqrt(jnp.mean(x * x, axis=-1, keepdims=True) + EPS)
            diff = jnp.where(j > 0, x * r * wv - t_ref[e], 0.0)
            loss_ref[...] += _sum_all(diff * diff) * (0.5 / d)
            dy = diff * (1.0 / d)
            g = dy * wv
            dh = r * (g - x * (r * r) * jnp.mean(g * x, axis=-1, keepdims=True))
            dh_ref[e] = dh
            dhb_ref[e] = dh.astype(BF16)
            dw_ref[...] += jnp.sum(dy * x * r, axis=0, keepdims=True)

    row = pl.BlockSpec((bsz, CHUNK, d), lambda j: (0, j, 0))
    return pl.pallas_call(
        body, name=name, grid=(nc,),
        in_specs=[row, pl.BlockSpec((bsz, CHUNK, d), lambda j: (0, jnp.maximum(j - 1, 0), 0)),
                  pl.BlockSpec((1, d), lambda j: (0, 0))],
        out_specs=[row, row, pl.BlockSpec((8, LANES), lambda j: (0, 0)), pl.BlockSpec((1, d), lambda j: (0, 0))],
        out_shape=[jax.ShapeDtypeStruct((bsz, t, d), F32), jax.ShapeDtypeStruct((bsz, t, d), BF16),
                   jax.ShapeDtypeStruct((8, LANES), F32), jax.ShapeDtypeStruct((1, d), F32)],
        compiler_params=_params("arbitrary"),
    )(h2, target, w)


def _pool_masks(j, transposed):
    r = lax.broadcasted_iota(jnp.int32, (CHUNK, 2 * CHUNK), 0)
    c = lax.broadcasted_iota(jnp.int32, (CHUNK, 2 * CHUNK), 1)
    masks = []
    for w in POOL_WINDOWS:
        if transposed:
            m = (c >= r) & (c < r + w)
        else:
            s = c - CHUNK
            m = (s <= r) & (s > r - w) & (s + j * CHUNK >= 0)
        masks.append(m.astype(F32))
    return masks


def _pool_count(t_global, w):
    return jnp.clip(t_global - PAD + 1, 1, w).astype(F32)


def _pool_fwd(u, pool_w, pool_scale, *, name):
    bsz, t, _ = u.shape
    nc = t // CHUNK

    def body(prev_ref, cur_ref, pw_ref, sc_ref, o_ref):
        j = pl.program_id(0)
        masks = _pool_masks(j, False)
        tg = j * CHUNK + lax.broadcasted_iota(jnp.int32, (CHUNK, 1), 0)
        count = [_pool_count(tg, w) for w in POOL_WINDOWS]
        units = [(e, gi) for e in range(bsz) for gi in range(len(POOL_WINDOWS))]
        sl = lambda gi: pl.ds(gi * POOL_GROUP, POOL_GROUP)
        cur = {(e, gi): cur_ref[e, :, sl(gi)] for e, gi in units}
        both = {(e, gi): jnp.concatenate([prev_ref[e, :, sl(gi)], cur[e, gi]], axis=0) for e, gi in units}
        win = {(e, gi): _dot_exact(masks[gi], both[e, gi]) for e, gi in units}
        pooled = {(e, gi): win[e, gi] / count[gi] - cur[e, gi] for e, gi in units}
        mixed = {(e, gi): _dot(pooled[e, gi], pw_ref[gi]) for e, gi in units}
        for e, gi in units:
            o_ref[e, :, sl(gi)] = (mixed[e, gi] * sc_ref[:, sl(gi)]).astype(BF16)

    blk = lambda f: pl.BlockSpec((bsz, CHUNK, D_POOL), f)
    return pl.pallas_call(
        body, name=name, grid=(nc,),
        in_specs=[blk(lambda j: (0, jnp.maximum(j - 1, 0), 0)), blk(lambda j: (0, j, 0)),
                  pl.BlockSpec((4, POOL_GROUP, POOL_GROUP), lambda j: (0, 0, 0)),
                  pl.BlockSpec((1, D_POOL), lambda j: (0, 0))],
        out_specs=blk(lambda j: (0, j, 0)), out_shape=jax.ShapeDtypeStruct(u.shape, BF16),
        compiler_params=_params("parallel"),
    )(u, u, pool_w, pool_scale)


def _pool_bwd(u, dyp, pool_w, pool_scale, *, name):
    bsz, t, _ = u.shape
    nc = t // CHUNK

    def body(prev_ref, cur_ref, dy_ref, dyn_ref, pw_ref, sc_ref, du_ref, dpw_ref, dsc_ref):
        j = pl.program_id(0)

        @pl.when(j == 0)
        def _():
            dpw_ref[...] = jnp.zeros_like(dpw_ref)
            dsc_ref[...] = jnp.zeros_like(dsc_ref)

        fwd = _pool_masks(j, False)
        bwd = _pool_masks(j, True)
        tg = j * CHUNK + lax.broadcasted_iota(jnp.int32, (CHUNK, 1), 0)
        count = [_pool_count(tg, w) for w in POOL_WINDOWS]
        count_next = [_pool_count(tg + CHUNK, w) for w in POOL_WINDOWS]
        has_next = j < nc - 1
        groups = range(len(POOL_WINDOWS))
        units = [(e, gi) for e in range(bsz) for gi in groups]
        sl = lambda gi: pl.ds(gi * POOL_GROUP, POOL_GROUP)
        cur = {(e, gi): cur_ref[e, :, sl(gi)] for e, gi in units}
        both = {(e, gi): jnp.concatenate([prev_ref[e, :, sl(gi)], cur[e, gi]], axis=0) for e, gi in units}
        win = {(e, gi): _dot_exact(fwd[gi], both[e, gi]) for e, gi in units}
        pooled = {(e, gi): win[e, gi] / count[gi] - cur[e, gi] for e, gi in units}
        dy = {(e, gi): dy_ref[e, :, sl(gi)] for e, gi in units}
        mixed = {(e, gi): _dot(pooled[e, gi], pw_ref[gi]) for e, gi in units}
        dm = {(e, gi): dy[e, gi] * sc_ref[:, sl(gi)] for e, gi in units}
        dm_next = {(e, gi): jnp.where(has_next, dyn_ref[e, :, sl(gi)], 0.0) * sc_ref[:, sl(gi)] for e, gi in units}
        dpw = {(e, gi): _dot_tn(pooled[e, gi], dm[e, gi]) for e, gi in units}
        dpooled = {(e, gi): _dot_nt(dm[e, gi], pw_ref[gi]) for e, gi in units}
        dpooled_next = {(e, gi): _dot_nt(dm_next[e, gi], pw_ref[gi]) for e, gi in units}
        spread = {(e, gi): jnp.concatenate([dpooled[e, gi] / count[gi], dpooled_next[e, gi] / count_next[gi]], axis=0)
                  for e, gi in units}
        back = {(e, gi): _dot_exact(bwd[gi], spread[e, gi]) for e, gi in units}
        for e, gi in units:
            du_ref[e, :, sl(gi)] = (back[e, gi] - dpooled[e, gi]).astype(BF16)
        for gi in groups:
            dsc, dw = None, None
            for e in range(bsz):
                term = jnp.sum(dy[e, gi] * mixed[e, gi], axis=0, keepdims=True)
                dsc = term if dsc is None else dsc + term
                dw = dpw[e, gi] if dw is None else dw + dpw[e, gi]
            dsc_ref[:, sl(gi)] += dsc
            dpw_ref[gi] += dw

    blk = lambda f: pl.BlockSpec((bsz, CHUNK, D_POOL), f)
    return pl.pallas_call(
        body, name=name, grid=(nc,),
        in_specs=[blk(lambda j: (0, jnp.maximum(j - 1, 0), 0)), blk(lambda j: (0, j, 0)),
                  blk(lambda j: (0, j, 0)), blk(lambda j: (0, jnp.minimum(j + 1, nc - 1), 0)),
                  pl.BlockSpec((4, POOL_GROUP, POOL_GROUP), lambda j: (0, 0, 0)),
                  pl.BlockSpec((1, D_POOL), lambda j: (0, 0))],
        out_specs=[blk(lambda j: (0, j, 0)), pl.BlockSpec((4, POOL_GROUP, POOL_GROUP), lambda j: (0, 0, 0)),
                   pl.BlockSpec((1, D_POOL), lambda j: (0, 0))],
        out_shape=[jax.ShapeDtypeStruct(u.shape, BF16), jax.ShapeDtypeStruct((4, POOL_GROUP, POOL_GROUP), F32),
                   jax.ShapeDtypeStruct((1, D_POOL), F32)],
        compiler_params=_params("arbitrary"),
    )(u, u, dyp, dyp, pool_w, pool_scale)


CONV_SLAB = 512


def _conv_taps(tail, cur, keep_tail):
    ext = jnp.concatenate([jnp.where(keep_tail, tail, 0.0), cur], axis=0)
    return [(pltpu.roll(ext, CONV_W - 1 - k, 0) if k < CONV_W - 1 else ext)[8:] for k in range(CONV_W)]


def _conv_pre(taps, w_ref, b_ref, sl):
    acc = b_ref[:, sl]
    for k in range(CONV_W):
        acc = acc + w_ref[k:k + 1, sl] * taps[k]
    return acc


def _proj_conv(hn, w, conv_w, conv_b, *, name):
    n, d = hn.shape
    c = w.shape[0]
    assert PAD >= CONV_W - 1
    tm = _row_tile(n, d * 2 + c * (4 + 2), c * d, WIDE_BUDGET)

    def body(hn_ref, w_ref, cw_ref, cb_ref, xbc_ref, xc_ref, tail_ref):
        @pl.when(pl.program_id(0) == 0)
        def _():
            tail_ref[...] = jnp.zeros_like(tail_ref)

        av = hn_ref[...]
        starts = list(range(0, c, CONV_SLAB))

        def project(c0):
            xbc_ref[:, pl.ds(c0, CONV_SLAB)] = _dot_nt(av, w_ref[c0:c0 + CONV_SLAB, :])

        def convolve(c0):
            sl = pl.ds(c0, CONV_SLAB)
            xb = xbc_ref[:, sl]
            pre = _conv_pre(_conv_taps(tail_ref[:, sl], xb, True), cw_ref, cb_ref, sl)
            xc_ref[:, sl] = (pre * _sigmoid(pre)).astype(BF16)
            tail_ref[:, sl] = xb[tm - 8:, :]

        project(starts[0])
        for c0, c_next in zip(starts, starts[1:] + [None]):
            if c_next is not None:
                project(c_next)
            convolve(c0)

    row = lambda width: pl.BlockSpec((tm, width), lambda i: (i, 0))
    return pl.pallas_call(
        body, name=name, grid=(n // tm,),
        in_specs=[row(d), pl.BlockSpec(w.shape, lambda i: (0, 0), pipeline_mode=pl.Buffered(1)),
                  pl.BlockSpec((CONV_W, c), lambda i: (0, 0)), pl.BlockSpec((1, c), lambda i: (0, 0))],
        out_specs=[row(c), row(c)],
        out_shape=[jax.ShapeDtypeStruct((n, c), F32), jax.ShapeDtypeStruct((n, c), BF16)],
        scratch_shapes=[pltpu.VMEM((8, c), F32)],
        compiler_params=_params("arbitrary"),
    )(hn, w, conv_w, conv_b)


def _conv_bwd(xbc, dxs, db, dc, conv_w, conv_b, *, name):
    bsz, t, c = xbc.shape
    nc = t // CHUNK
    halo = 16
    rows = CHUNK + halo

    def body(tail_ref, cur_ref, head_ref, dxs_ref, db_ref, dc_ref, dxs_head, db_head, dc_head, w_ref, b_ref,
             dx_ref, dwb_ref):
        j = pl.program_id(1)

        @pl.when(j == 0)
        def _():
            dwb_ref[...] = jnp.zeros_like(dwb_ref)

        has_prev, has_next = j > 0, j < nc - 1
        for c0 in range(0, c, CONV_SLAB):
            sl = pl.ds(c0, CONV_SLAB)
            if c0 < D_SSM:
                dxc, dxc_next = dxs_ref[0, :, sl], dxs_head[0, :, sl]
            elif c0 < D_SSM + D_POOL:
                dxc, dxc_next = db_ref[0], db_head[0]
            else:
                dxc, dxc_next = dc_ref[0], dc_head[0]
            dxc = jnp.concatenate([dxc.astype(F32), jnp.where(has_next, dxc_next.astype(F32), 0.0)], axis=0)
            ext = jnp.concatenate([jnp.where(has_prev, tail_ref[0, :, sl], 0.0), cur_ref[0, :, sl],
                                   jnp.where(has_next, head_ref[0, :, sl], 0.0)], axis=0)
            taps = [(pltpu.roll(ext, CONV_W - 1 - k, 0) if k < CONV_W - 1 else ext)[8:] for k in range(CONV_W)]
            pre = _conv_pre(taps, w_ref, b_ref, sl)
            s = _sigmoid(pre)
            dpre = dxc * (s * (1.0 + pre * (1.0 - s)))
            acc = w_ref[CONV_W - 1:CONV_W, sl] * dpre[:CHUNK]
            for k in range(CONV_W - 1):
                up = CONV_W - 1 - k
                acc = acc + w_ref[k:k + 1, sl] * pltpu.roll(dpre, rows - up, 0)[:CHUNK]
            dx_ref[0, :, sl] = acc.astype(BF16)
            for k in range(CONV_W):
                dwb_ref[0, k:k + 1, sl] += jnp.sum(dpre[:CHUNK] * taps[k][:CHUNK], axis=0, keepdims=True)
            dwb_ref[0, CONV_W:CONV_W + 1, sl] += jnp.sum(dpre[:CHUNK], axis=0, keepdims=True)

    assert CONV_SLAB == D_POOL and D_SSM % CONV_SLAB == 0
    row = lambda width: pl.BlockSpec((1, CHUNK, width), lambda b, j: (b, j, 0))
    nxt = lambda width: pl.BlockSpec(
        (1, halo, width), lambda b, j: (b, jnp.minimum((j + 1) * (CHUNK // halo), t // halo - 1), 0))
    return pl.pallas_call(
        body, name=name, grid=(bsz, nc),
        in_specs=[pl.BlockSpec((1, 8, c), lambda b, j: (b, jnp.maximum(j * (CHUNK // 8) - 1, 0), 0)), row(c), nxt(c),
                  row(D_SSM), row(D_POOL), row(D_POOL), nxt(D_SSM), nxt(D_POOL), nxt(D_POOL),
                  pl.BlockSpec((CONV_W, c), lambda b, j: (0, 0)), pl.BlockSpec((1, c), lambda b, j: (0, 0))],
        out_specs=[row(c), pl.BlockSpec((1, 8, c), lambda b, j: (b, 0, 0))],
        out_shape=[jax.ShapeDtypeStruct(xbc.shape, BF16), jax.ShapeDtypeStruct((bsz, 8, c), F32)],
        compiler_params=_params("parallel", "arbitrary"),
    )(xbc, xbc, xbc, dxs, db, dc, dxs, db, dc, conv_w, conv_b)


def _dt_valid(j):
    lane = lax.broadcasted_iota(jnp.int32, (CHUNK, LANES), 1)
    row = lax.broadcasted_iota(jnp.int32, (CHUNK, LANES), 0)
    return (lane < HPG) & ((j > 0) | (row >= PAD))


def _ssd_prep(dtr, dtb, alog, *, name):
    bsz, t, _ = dtr.shape
    nc = t // CHUNK

    def body(dtr_ref, dtb_ref, alog_ref, dt_ref, acs_ref, tr_ref):
        j = pl.program_id(0)
        valid = _dt_valid(j)
        row = lax.broadcasted_iota(jnp.int32, (CHUNK, LANES), 0)
        lane = lax.broadcasted_iota(jnp.int32, (CHUNK, LANES), 1)
        tril = (row >= lane).astype(F32)
        units = [(e, g) for e in range(bsz) for g in range(N_GROUPS)]
        sl = lambda g: pl.ds(g * LANES, LANES)
        dt = {(e, g): jnp.where(valid, _softplus(dtr_ref[e, :, sl(g)] + dtb_ref[g]), 0.0) for e, g in units}
        acs = {(e, g): _dot_exact(tril, dt[e, g] * -jnp.exp(alog_ref[g])) for e, g in units}
        for e, g in units:
            dt_ref[e, :, sl(g)] = dt[e, g]
            acs_ref[e, :, sl(g)] = acs[e, g]
            tr_ref[e, 0, g, 0:8, :] = dt[e, g].T[0:8]
            tr_ref[e, 0, g, 8:16, :] = acs[e, g].T[0:8]

    blk = pl.BlockSpec((bsz, CHUNK, D_DT), lambda j: (0, j, 0))
    const = pl.BlockSpec((N_GROUPS, 1, LANES), lambda j: (0, 0, 0))
    return pl.pallas_call(
        body, name=name, grid=(nc,), in_specs=[blk, const, const],
        out_specs=[blk, blk, pl.BlockSpec((bsz, 1, N_GROUPS, 16, LANES), lambda j: (0, j, 0, 0, 0))],
        out_shape=[jax.ShapeDtypeStruct(dtr.shape, F32), jax.ShapeDtypeStruct(dtr.shape, F32),
                   jax.ShapeDtypeStruct((bsz, nc, N_GROUPS, 16, LANES), F32)],
        compiler_params=_params("parallel"),
    )(dtr, dtb, alog)


def _ssd_decay(dt, acs, tr):
    lane = lax.broadcasted_iota(jnp.int32, (CHUNK, LANES), 1)
    row = lax.broadcasted_iota(jnp.int32, (CHUNK, LANES), 0)
    return dict(lane=lane, row=row, dt=dt, causal=row >= lane, acs=acs, acs_t=tr[8:16], dt_t=tr[0:8],
                aend=acs[CHUNK - 1:CHUNK, :])


def _ssd_specs(bsz, nc, rev):
    ch = (lambda j: nc - 1 - j) if rev else (lambda j: j)
    return dict(
        xs=pl.BlockSpec((bsz, CHUNK, GW), lambda g, j: (0, ch(j), g)),
        bm=pl.BlockSpec((bsz, CHUNK, D_STATE), lambda g, j: (0, ch(j), D_SSM // D_STATE + g)),
        cm=pl.BlockSpec((bsz, CHUNK, D_STATE), lambda g, j: (0, ch(j), D_SSM // D_STATE + N_GROUPS + g)),
        lane_blk=pl.BlockSpec((bsz, CHUNK, LANES), lambda g, j: (0, ch(j), g)),
        grp_const=pl.BlockSpec((1, 1, LANES), lambda g, j: (g, 0, 0)),
        grp_vec=pl.BlockSpec((1, GW), lambda g, j: (0, g)),
        state=pl.BlockSpec((bsz, 1, D_STATE, GW), lambda g, j: (0, ch(j), 0, g)),
        tr=pl.BlockSpec((bsz, 1, 1, 16, LANES), lambda g, j: (0, ch(j), g, 0, 0)),
    )


def _ssd_fwd(xc, dt, acs, tr, z, dskip, normw, *, name, rider=None):
    bsz, t, _ = xc.shape
    nc = t // CHUNK
    sp = _ssd_specs(bsz, nc, False)

    def body(xs_ref, b_ref, c_ref, dt_ref, acs_ref, tr_ref, z_ref, dsk_ref, nw_ref, yn_ref, y_ref, sp_ref, s_ref):
        j = pl.program_id(1)

        @pl.when(j == 0)
        def _():
            s_ref[...] = jnp.zeros_like(s_ref)

        ex = range(bsz)
        units = [(e, r) for e in ex for r in range(HPG)]
        full = lambda v: jnp.broadcast_to(v, (CHUNK, LANES))
        pair = lambda r: pl.ds((r // 2) * LANES, LANES)
        q = [_ssd_decay(dt_ref[e], acs_ref[e], tr_ref[e, 0, 0]) for e in ex]
        for e in ex:
            sp_ref[e, 0] = s_ref[e]
        bm, cm = [b_ref[e] for e in ex], [c_ref[e] for e in ex]
        cb = [_dot_nt(cm[e], bm[e]) for e in ex]
        low = q[0]["lane"] < HEAD_DIM
        col = {(e, r): full(q[e]["acs"][:, r:r + 1]) for e, r in units}
        aend = {(e, r): q[e]["aend"][:, r:r + 1] for e, r in units}
        decay = {(e, r): jnp.exp(jnp.where(q[e]["causal"], col[e, r] - q[e]["acs_t"][r:r + 1, :], -jnp.inf))
                 for e, r in units}
        mp = {(e, r): cb[e] * decay[e, r] * q[e]["dt_t"][r:r + 1, :] for e, r in units}
        ce = {(e, r): cm[e] * jnp.exp(col[e, r]) for e, r in units}
        bk = {(e, r): bm[e] * (jnp.exp(aend[e, r] - col[e, r]) * full(q[e]["dt"][:, r:r + 1])) for e, r in units}
        xp = {(e, r): xs_ref[e, :, pair(r)] for e, r in units}
        s_old = {(e, r): s_ref[e, :, pair(r)] for e, r in units}
        y_h = {u: _dot(mp[u], xp[u]) + _dot(ce[u], s_old[u]) for u in units}
        s_h = {u: jnp.exp(aend[u]) * s_old[u] + _dot_tn(bk[u], xp[u]) for u in units}
        for e in ex:
            for r in range(0, HPG, 2):
                y_ref[e, :, pair(r)] = jnp.where(low, y_h[e, r], y_h[e, r + 1])
                s_ref[e, :, pair(r)] = jnp.where(low, s_h[e, r], s_h[e, r + 1])
        y = [y_ref[e] + dsk_ref[...] * xs_ref[e] for e in ex]
        zz = [z_ref[e] for e in ex]
        yg = [y[e] * (zz[e] * _sigmoid(zz[e])) for e in ex]
        rstd = [lax.rsqrt(jnp.mean(yg[e] * yg[e], axis=-1, keepdims=True) + EPS) for e in ex]
        for e in ex:
            y_ref[e] = y[e]
            yn_ref[e] = (yg[e] * rstd[e] * nw_ref[...]).astype(BF16)

    grid = (N_GROUPS, nc)
    ride = _Ride(rider, body, 9, 3, 1, grid)
    outs = pl.pallas_call(
        ride.body, name=name, grid=grid,
        in_specs=[sp["xs"], sp["bm"], sp["cm"], sp["lane_blk"], sp["lane_blk"], sp["tr"], sp["xs"],
                  sp["grp_vec"], sp["grp_vec"]] + ride.in_specs,
        out_specs=[sp["xs"], sp["xs"], sp["state"]] + ride.out_specs,
        out_shape=[jax.ShapeDtypeStruct((bsz, t, D_SSM), BF16), jax.ShapeDtypeStruct((bsz, t, D_SSM), F32),
                   jax.ShapeDtypeStruct((bsz, nc, D_STATE, D_SSM), F32)] + ride.out_shape,
        scratch_shapes=[pltpu.VMEM((bsz, D_STATE, GW), F32)] + ride.scratch,
        compiler_params=_params(*ride.semantics(("parallel", "arbitrary"))),
    )(xc, xc, xc, dt, acs, tr, z, dskip, normw, *ride.args)
    return outs[:3], outs[3:]


def _ssd_bwd(xc, dtr, dt, acs, tr, z, ypre, sprev, dyn, dtb, alog, dskip, normw, *, name, rider=None):
    bsz, t, _ = xc.shape
    nc = t // CHUNK
    sp = _ssd_specs(bsz, nc, True)

    def body(xs_ref, b_ref, c_ref, dtr_ref, dt_ref, acs_ref, tr_ref, z_ref, y_ref, sp_ref, dyn_ref, dtb_ref, alog_ref,
             dsk_ref, nw_ref, dz_ref, dxs_ref, db_ref, dc_ref, ddt_ref, dnw_ref, dsm_ref, ds_ref):
        j = pl.program_id(1)

        @pl.when(j == 0)
        def _():
            ds_ref[...] = jnp.zeros_like(ds_ref)
            dnw_ref[...] = jnp.zeros_like(dnw_ref)
            dsm_ref[...] = jnp.zeros_like(dsm_ref)

        ex = range(bsz)
        heads = range(HPG)
        units = [(e, r) for e in ex for r in heads]
        q = [_ssd_decay(dt_ref[e], acs_ref[e], tr_ref[e, 0, 0]) for e in ex]
        a = -jnp.exp(alog_ref[0])
        valid = _dt_valid(nc - 1 - j)
        lane, row = q[0]["lane"], q[0]["row"]
        lane1 = lane[0:1, :]
        nw = nw_ref[...]
        y, zz, dyn = [y_ref[e] for e in ex], [z_ref[e] for e in ex], [dyn_ref[e] for e in ex]
        sz = [_sigmoid(zz[e]) for e in ex]
        sil = [zz[e] * sz[e] for e in ex]
        yg = [y[e] * sil[e] for e in ex]
        rstd = [lax.rsqrt(jnp.mean(yg[e] * yg[e], axis=-1, keepdims=True) + EPS) for e in ex]
        gn = [dyn[e] * nw for e in ex]
        dyg = [rstd[e] * (gn[e] - yg[e] * (rstd[e] * rstd[e]) * jnp.mean(gn[e] * yg[e], axis=-1, keepdims=True))
               for e in ex]
        dy = [dyg[e] * sil[e] for e in ex]
        xs = [xs_ref[e] for e in ex]
        for e in ex:
            dnw_ref[e] += jnp.sum(dyn[e] * yg[e] * rstd[e], axis=0, keepdims=True)
            dz_ref[e] = (dyg[e] * y[e] * (sz[e] * (1.0 + zz[e] * (1.0 - sz[e])))).astype(BF16)
        dskip_cols = [jnp.sum(dy[e] * xs[e], axis=0, keepdims=True) for e in ex]

        bm, cm = [b_ref[e] for e in ex], [c_ref[e] for e in ex]
        cb = [_dot_nt(cm[e], bm[e]) for e in ex]
        zero = jnp.zeros((CHUNK, LANES), F32)
        full = lambda v: jnp.broadcast_to(v, (CHUNK, LANES))
        low = lane < HEAD_DIM
        half = [low if r % 2 == 0 else ~low for r in heads]
        sl = lambda v, r: v[:, (r // 2) * LANES:(r // 2 + 1) * LANES]
        pair = lambda r: pl.ds((r // 2) * LANES, LANES)
        col = {(e, r): full(q[e]["acs"][:, r:r + 1]) for e, r in units}
        dt_col = {(e, r): full(q[e]["dt"][:, r:r + 1]) for e, r in units}
        aend = {(e, r): q[e]["aend"][:, r:r + 1] for e, r in units}
        dt_row = {(e, r): q[e]["dt_t"][r:r + 1, :] for e, r in units}
        decay = {(e, r): jnp.exp(jnp.where(q[e]["causal"], col[e, r] - q[e]["acs_t"][r:r + 1, :], -jnp.inf))
                 for e, r in units}
        ea = {u: jnp.exp(col[u]) for u in units}
        dte = {u: jnp.exp(aend[u] - col[u]) for u in units}
        ed = {u: jnp.exp(aend[u]) for u in units}
        k = {u: dte[u] * dt_col[u] for u in units}
        mp = {(e, r): cb[e] * decay[e, r] * dt_row[e, r] for e, r in units}
        xp = {(e, r): sl(xs[e], r) for e, r in units}
        dym = {(e, r): jnp.where(half[r], sl(dy[e], r), 0.0) for e, r in units}
        s_old = {(e, r): sp_ref[e, 0, :, pair(r)] for e, r in units}
        ds_old = {(e, r): ds_ref[e, :, pair(r)] for e, r in units}
        dsm = {(e, r): jnp.where(half[r], ds_old[e, r], 0.0) for e, r in units}
        gmat = {u: _dot_nt(dym[u], xp[u]) for u in units}
        t1 = {u: _dot_nt(dym[u], s_old[u]) for u in units}
        dbs = {u: _dot_nt(xp[u], dsm[u]) for u in units}
        dx = {(e, r): _dot_tn(mp[e, r], dym[e, r]) + _dot(bm[e] * k[e, r], dsm[e, r]) for e, r in units}
        ds = {(e, r): _dot_tn(cm[e] * ea[e, r], dym[e, r]) for e, r in units}
        gd = {u: gmat[u] * decay[u] for u in units}
        w0 = {(e, r): gd[e, r] * cb[e] for e, r in units}
        cs0 = {u: jnp.sum(w0[u], axis=0, keepdims=True) for u in units}
        rs = {u: jnp.sum(w0[u] * dt_row[u], axis=1, keepdims=True) for u in units}
        qv = {(e, r): jnp.sum(cm[e] * t1[e, r], axis=1, keepdims=True) for e, r in units}
        dk = {(e, r): jnp.sum(bm[e] * dbs[e, r], axis=1, keepdims=True) for e, r in units}
        ddte = {u: dk[u] * dt_col[u] for u in units}
        d_aend = {u: _sum_all(dsm[u] * s_old[u]) * ed[u] + _sum_all(ddte[u][:, 0:1] * dte[u][:, 0:1]) for u in units}
        last_row = row == CHUNK - 1
        dacs_col = {u: rs[u] + qv[u] * ea[u] - ddte[u] * dte[u] + jnp.where(last_row, d_aend[u], 0.0) for u in units}
        triu = (lane >= row).astype(F32)
        for e in ex:
            dcb, dc_acc, db_acc = zero, zero, zero
            dacs, dacs_t, ddt, ddt_t = zero, zero, zero, zero
            dskip_row = jnp.zeros((1, LANES), F32)
            for r in heads:
                u = (e, r)
                dcb = dcb + gd[u] * dt_row[u]
                dc_acc = dc_acc + ea[u] * t1[u]
                db_acc = db_acc + k[u] * dbs[u]
                dacs = jnp.where(lane == r, dacs_col[u], dacs)
                ddt = jnp.where(lane == r, dk[u] * dte[u], ddt)
                dacs_t = jnp.where(row == r, -cs0[u] * dt_row[u], dacs_t)
                ddt_t = jnp.where(row == r, cs0[u], ddt_t)
                dsk = _sum_all(jnp.where(half[r][0:1, :], sl(dskip_cols[e], r), 0.0))
                dskip_row = dskip_row + jnp.where(lane1 == r, dsk, 0.0)
            for r in range(0, HPG, 2):
                dxs_ref[e, :, pair(r)] = (dx[e, r] + dx[e, r + 1] + sl(dy[e], r) * dsk_ref[:, pair(r)]).astype(BF16)
                ed_pair = jnp.where(lane1 < HEAD_DIM, ed[e, r], ed[e, r + 1])
                ds_ref[e, :, pair(r)] = ds[e, r] + ds[e, r + 1] + ed_pair * ds_old[e, r]
            dacs = dacs + dacs_t.T
            ddt = ddt + ddt_t.T
            dda = _dot_exact(triu, dacs)
            ddt = ddt + dda * a
            da = jnp.sum(dda * q[e]["dt"], axis=0, keepdims=True)
            draw = jnp.where(valid, ddt * _sigmoid(dtr_ref[e] + dtb_ref[0]), 0.0)
            ddt_ref[e] = draw.astype(BF16)
            dsm_ref[e, 0, 0:1, :] += dskip_row
            dsm_ref[e, 0, 1:2, :] += da * a
            dsm_ref[e, 0, 2:3, :] += jnp.sum(draw, axis=0, keepdims=True)
            dc_ref[e] = (dc_acc + _dot(dcb, bm[e])).astype(BF16)
            db_ref[e] = (db_acc + _dot_tn(dcb, cm[e])).astype(BF16)

    grp_out = pl.BlockSpec((bsz, CHUNK, D_STATE), lambda g, j: (0, nc - 1 - j, g))
    grid = (N_GROUPS, nc)
    ride = _Ride(rider, body, 15, 7, 1, grid)
    outs = pl.pallas_call(
        ride.body, name=name, grid=grid,
        in_specs=[sp["xs"], sp["bm"], sp["cm"], sp["lane_blk"], sp["lane_blk"], sp["lane_blk"], sp["tr"], sp["xs"],
                  sp["xs"], sp["state"], sp["xs"], sp["grp_const"], sp["grp_const"], sp["grp_vec"], sp["grp_vec"]]
        + ride.in_specs,
        out_specs=[sp["xs"], sp["xs"], grp_out, grp_out, sp["lane_blk"],
                   pl.BlockSpec((bsz, 1, GW), lambda g, j: (0, 0, g)),
                   pl.BlockSpec((bsz, 1, 8, LANES), lambda g, j: (0, g, 0, 0))] + ride.out_specs,
        out_shape=[jax.ShapeDtypeStruct((bsz, t, D_SSM), BF16), jax.ShapeDtypeStruct((bsz, t, D_SSM), BF16),
                   jax.ShapeDtypeStruct((bsz, t, N_GROUPS * D_STATE), BF16),
                   jax.ShapeDtypeStruct((bsz, t, N_GROUPS * D_STATE), BF16),
                   jax.ShapeDtypeStruct((bsz, t, D_DT), BF16), jax.ShapeDtypeStruct((bsz, 1, D_SSM), F32),
                   jax.ShapeDtypeStruct((bsz, N_GROUPS, 8, LANES), F32)] + ride.out_shape,
        scratch_shapes=[pltpu.VMEM((bsz, D_STATE, GW), F32)] + ride.scratch,
        compiler_params=_params(*ride.semantics(("parallel", "arbitrary"))),
    )(xc, xc, xc, dtr, dt, acs, tr, z, ypre, sprev, dyn, dtb, alog, dskip, normw, *ride.args)
    return outs[:7], outs[7:]


def _input_grad(dhn, h0, w, dres, seq, *, name):
    bsz, t, d = h0.shape
    nc = t // CHUNK

    def body(dy_ref, h_ref, w_ref, dres_ref, gx_ref, head_ref, dw_ref):
        j = pl.program_id(0)

        @pl.when(j == 0)
        def _():
            dw_ref[...] = jnp.zeros_like(dw_ref)

        for e in range(bsz):
            x, dyv = h_ref[e], dy_ref[e]
            r = lax.rsqrt(jnp.mean(x * x, axis=-1, keepdims=True) + EPS)
            g = dyv * w_ref[...]
            dx = r * (g - x * (r * r) * jnp.mean(g * x, axis=-1, keepdims=True)) + dres_ref[e]
            dw_ref[...] += jnp.sum(dyv * x * r, axis=0, keepdims=True)
            gx_ref[e] = dx

        @pl.when(j == 0)
        def _():
            head_ref[...] = gx_ref[...]

    row = pl.BlockSpec((bsz, CHUNK, d), lambda j: (0, j, 0))
    return pl.pallas_call(
        body, name=name, grid=(nc,),
        in_specs=[row, row, pl.BlockSpec((1, d), lambda j: (0, 0)), row],
        out_specs=[pl.BlockSpec((bsz, CHUNK, d), lambda j: (0, jnp.maximum(j - 1, 0), 0)),
                   pl.BlockSpec((bsz, CHUNK, d), lambda j: (0, 0, 0)), pl.BlockSpec((1, d), lambda j: (0, 0))],
        out_shape=[jax.ShapeDtypeStruct((bsz, seq, d), F32), jax.ShapeDtypeStruct((bsz, CHUNK, d), F32),
                   jax.ShapeDtypeStruct((1, d), F32)],
        compiler_params=_params("arbitrary"),
    )(dhn, h0, w, dres)


def _remote(src, dst, send_sem, recv_sem, dev):
    return pltpu.make_async_remote_copy(src_ref=src, dst_ref=dst, send_sem=send_sem, recv_sem=recv_sem,
                                        device_id=dev, device_id_type=MESH)


def _position():
    return lax.axis_index("x"), lax.axis_index("y"), lax.axis_index("c")


def _other_chips(pos):
    x, y, _ = pos
    return [(1 - x, y), (x, 1 - y), (1 - x, 1 - y)]


class _Gather:
    def __init__(self, arrs):
        n = len(arrs)
        self.args, self.n_in, self.n_out = list(arrs), n, n
        self.split = [a.ndim == 2 and a.shape[1] % (2 * LANES) == 0 for a in arrs]
        self.out_shape = [jax.ShapeDtypeStruct((4,) + a.shape, a.dtype) for a in arrs]
        self.scratch = [pltpu.SemaphoreType.DMA((3 * n,)), pltpu.SemaphoreType.DMA((3 * n,)),
                        pltpu.SemaphoreType.DMA((n,)), pltpu.SemaphoreType.DMA((3 * n,)),
                        pltpu.SemaphoreType.DMA((3 * n,))]

    def _copies(self, pos, ins, outs, sems):
        send_sems, recv_sems, loc_sems, pass_send_sems, pass_recv_sems = sems
        x, y, c = pos
        me, sibling = 2 * x + y, (x, y, 1 - c)
        local = [pltpu.make_async_copy(ins[i], outs[i].at[me], loc_sems.at[i]) for i in range(self.n_in)]
        sends, recvs, passes, pass_recvs = [], [], [], []
        for i in range(self.n_in):
            half = self.args[i].shape[1] // 2 if self.split[i] else None
            for k, (px, py) in enumerate(_other_chips(pos)):
                them = 2 * px + py
                sems_k = (send_sems.at[3 * i + k], recv_sems.at[3 * i + k], (px, py, c))
                if half is None:
                    sends.append(_remote(ins[i], outs[i].at[me], *sems_k))
                    recvs.append(_remote(ins[i], outs[i].at[them], *sems_k))
                    passes.append(None)
                    continue
                mine = pl.ds(pl.multiple_of(c * half, LANES), half)
                other = pl.ds(pl.multiple_of((1 - c) * half, LANES), half)
                sends.append(_remote(ins[i].at[:, mine], outs[i].at[me, :, mine], *sems_k))
                recvs.append(_remote(ins[i].at[:, mine], outs[i].at[them, :, mine], *sems_k))
                pass_k = (pass_send_sems.at[3 * i + k], pass_recv_sems.at[3 * i + k], sibling)
                passes.append(_remote(outs[i].at[them, :, mine], outs[i].at[them, :, mine], *pass_k))
                pass_recvs.append(_remote(outs[i].at[them, :, other], outs[i].at[them, :, other], *pass_k))
        return local, sends, recvs, passes, pass_recvs

    def start(self, pos, ins, outs, sems):
        local, sends = self._copies(pos, ins, outs, sems)[:2]
        for cp in local + sends:
            cp.start()

    def relay(self, pos, ins, outs, sems):
        _, _, recvs, passes, _ = self._copies(pos, ins, outs, sems)
        for cp, onward in zip(recvs, passes):
            if onward is not None:
                cp.wait_recv()
                onward.start()

    def finish(self, pos, ins, outs, sems):
        local, sends, recvs, passes, pass_recvs = self._copies(pos, ins, outs, sems)
        for cp, onward in zip(recvs, passes):
            if onward is None:
                cp.wait_recv()
        for cp in pass_recvs:
            cp.wait_recv()
        for cp in sends + [p for p in passes if p is not None]:
            cp.wait_send()
        for cp in local:
            cp.wait()


class _Exchange:
    FLIPS = [(fx, fy, fc) for fx in (0, 1) for fy in (0, 1) for fc in (0, 1)][1:]

    def __init__(self, big, small=None):
        n = len(big)
        self.n_big, self.has_small = n, small is not None
        self.args = list(big) + ([small] if self.has_small else [])
        self.n_in = self.n_out = len(self.args)
        self.out_shape = [jax.ShapeDtypeStruct(a.shape, a.dtype) for a in big]
        self.scratch = [pltpu.SemaphoreType.DMA((max(3 * n, 1),)), pltpu.SemaphoreType.DMA((max(3 * n, 1),)),
                        pltpu.SemaphoreType.DMA((n + 1,))]
        if self.has_small:
            self.out_shape.append(jax.ShapeDtypeStruct((8,) + small.shape, small.dtype))
            self.scratch += [pltpu.SemaphoreType.DMA((7,)), pltpu.SemaphoreType.DMA((7,))]

    def _copies(self, pos, ins, outs, sems):
        x, y, c = pos
        me, me8 = 2 * x + y, 4 * x + 2 * y + c
        local, sends, recvs = [], [], []
        for i in range(self.n_big):
            local.append(pltpu.make_async_copy(ins[i].at[me], outs[i].at[me], sems[2].at[i]))
            for k, (px, py) in enumerate(_other_chips(pos)):
                sems_k = (sems[0].at[3 * i + k], sems[1].at[3 * i + k], (px, py, c))
                sends.append(_remote(ins[i].at[2 * px + py], outs[i].at[me], *sems_k))
                recvs.append(_remote(ins[i].at[me], outs[i].at[2 * px + py], *sems_k))
        if self.has_small:
            small, landed = ins[self.n_big], outs[self.n_big]
            local.append(pltpu.make_async_copy(small, landed.at[me8], sems[2].at[self.n_big]))
            for k, (fx, fy, fc) in enumerate(self.FLIPS):
                peer = (x ^ fx, y ^ fy, c ^ fc)
                sems_k = (sems[3].at[k], sems[4].at[k], peer)
                sends.append(_remote(small, landed.at[me8], *sems_k))
                recvs.append(_remote(small, landed.at[4 * peer[0] + 2 * peer[1] + peer[2]], *sems_k))
        return local, sends, recvs, [None] * len(recvs), []

    start = _Gather.start
    relay = _Gather.relay
    finish = _Gather.finish


class _Swap:
    def __init__(self, arrs):
        n = len(arrs)
        self.args, self.n_in, self.n_out = list(arrs), n, n
        self.out_shape = [jax.ShapeDtypeStruct(a.shape, a.dtype) for a in arrs]
        self.scratch = [pltpu.SemaphoreType.DMA((n,)), pltpu.SemaphoreType.DMA((n,))]

    def _copies(self, pos, ins, outs, sems):
        x, y, c = pos
        both = [_remote(ins[i], outs[i], sems[0].at[i], sems[1].at[i], (x, y, 1 - c)) for i in range(self.n_in)]
        return [], both, both, [None] * len(both), []

    start = _Gather.start
    relay = _Gather.relay
    finish = _Gather.finish


def _comm(rider, *, name):
    a, b = rider.n_in, rider.n_in + rider.n_out

    def body(*refs):
        pos = _position()
        rider.start(pos, refs[:a], refs[a:b], refs[b:])
        rider.relay(pos, refs[:a], refs[a:b], refs[b:])
        rider.finish(pos, refs[:a], refs[a:b], refs[b:])

    return pl.pallas_call(body, name=name, in_specs=[ANY] * rider.n_in, out_specs=[ANY] * rider.n_out,
                          out_shape=rider.out_shape, scratch_shapes=rider.scratch)(*rider.args)


class _Ride:
    RELAY_AT = 0.8

    def __init__(self, rider, body, n_in, n_out, n_scratch, grid):
        self.rider = rider
        self.args = rider.args if rider else []
        self.in_specs = [ANY] * rider.n_in if rider else []
        self.out_specs = [ANY] * rider.n_out if rider else []
        self.out_shape = rider.out_shape if rider else []
        self.scratch = rider.scratch if rider else []
        self.body = self._wrap(body, n_in, n_out, n_scratch, grid) if rider else body

    def semantics(self, sem):
        return ("arbitrary",) * len(sem) if self.rider else sem

    def _wrap(self, body, n_in, n_out, n_scratch, grid):
        rider = self.rider
        a = n_in
        b = a + rider.n_in
        c = b + n_out
        d = c + rider.n_out
        e = d + n_scratch

        def wrapped(*refs):
            pos = _position()
            ids = [pl.program_id(i) for i in range(len(grid))]
            step, total = 0, 1
            for i, g in zip(ids, grid):
                step, total = step * g + i, total * g

            @pl.when(step == 0)
            def _():
                rider.start(pos, refs[a:b], refs[c:d], refs[e:])

            body(*refs[:a], *refs[b:c], *refs[d:e])

            @pl.when(step == int(self.RELAY_AT * (total - 1)))
            def _():
                rider.relay(pos, refs[a:b], refs[c:d], refs[e:])

            @pl.when(step == total - 1)
            def _():
                rider.finish(pos, refs[a:b], refs[c:d], refs[e:])

        return wrapped


def _elementwise_tiles(r, c):
    if r % 8 == 0 and r * c > 65536:
        tm = _pick(r, (256, 128, 64, 16, 8))
        return (tm, c), r // tm, lambda i: (i, 0)
    if r % 8 and c % 256 == 0 and r * c > 65536:
        return (r, 256), c // 256, lambda i: (0, i)
    return (r, c), 1, lambda i: (0, 0)


def _chip_sum(landed, *, name):
    _, r, c = landed.shape
    blk, steps, at = _elementwise_tiles(r, c)

    def body(land_ref, o_ref):
        acc = land_ref[0].astype(F32)
        for jchip in range(1, 4):
            acc = acc + land_ref[jchip].astype(F32)
        o_ref[...] = acc

    return pl.pallas_call(
        body, name=name, grid=(steps,), in_specs=[pl.BlockSpec((4,) + blk, lambda i: (0,) + at(i))],
        out_specs=pl.BlockSpec(blk, at), out_shape=jax.ShapeDtypeStruct((r, c), F32),
        compiler_params=_params("parallel"),
    )(landed)


def _device_sum(parts, *, name):
    _, r, c = parts.shape

    def body(p_ref, o_ref):
        acc = p_ref[0]
        for d in range(1, 8):
            acc = acc + p_ref[d]
        o_ref[...] = acc

    return pl.pallas_call(body, name=name, out_shape=jax.ShapeDtypeStruct((r, c), F32))(parts)


def _adamw_math(w, g, m, v):
    m = ADAM_B1 * m + (1.0 - ADAM_B1) * g
    v = ADAM_B2 * v + (1.0 - ADAM_B2) * (g * g)
    m_hat = m / (1.0 - ADAM_B1 ** ADAM_STEP)
    v_hat = v / (1.0 - ADAM_B2 ** ADAM_STEP)
    return -ADAM_LR * (m_hat / (jnp.sqrt(v_hat) + ADAM_EPS) + ADAM_WD * w), m, v


def _adamw(w, g_parts, m, v, *, name):
    r, c = w.shape
    shape, steps, at = _elementwise_tiles(r, c)
    n_g = len(g_parts)

    def body(*refs):
        w_ref, m_ref, v_ref = refs[n_g:n_g + 3]
        g_ref, d_ref, nm_ref, nv_ref = refs[n_g + 3:]
        g = refs[0][...]
        for p in refs[1:n_g]:
            g = g + p[...]
        g_ref[...] = g
        d_ref[...], nm_ref[...], nv_ref[...] = _adamw_math(w_ref[...], g, m_ref[...], v_ref[...])

    blk = pl.BlockSpec(shape, at)
    return pl.pallas_call(
        body, name=name, grid=(steps,), in_specs=[blk] * (n_g + 3), out_specs=[blk] * 4,
        out_shape=[jax.ShapeDtypeStruct((r, c), F32)] * 4, compiler_params=_params("parallel"),
    )(*g_parts, w, m, v)


def _pad_heads(v):
    return jnp.pad(v.reshape(N_GROUPS, 1, HPG), ((0, 0), (0, 0), (0, LANES - HPG)))


def _unpad_heads(v):
    return v[:, :HPG].reshape(1, N_HEADS)


_SMALL_EARLY = [("pool_w", (512, 128)), ("pool_scale", (1, 512)), ("conv_w", (4, D_XBC)), ("conv_b", (1, D_XBC)),
                ("dt_bias", (1, N_HEADS)), ("a_log", (1, N_HEADS)), ("d_skip", (1, N_HEADS)), ("ssm_norm_w", (1, D_SSM)),
                ("norm_ffn_w", (1, 1024)), ("norm_f_w", (1, 1024))]
_SMALL_LATE = [("norm_mix_w", (1, 1024)), ("meta", (N_META, 1024)), ("loss", (1, 1))]


def _pack_small(grads, layout):
    rows = []
    for nm, shape in layout:
        flat = grads[nm].reshape(-1)
        rows.append(jnp.pad(flat, (0, (-flat.size) % LANES)).reshape(-1, LANES))
    packed = jnp.concatenate(rows, axis=0)
    return jnp.pad(packed, ((0, (-packed.shape[0]) % 8), (0, 0)))


def _unpack_small(packed, layout):
    out, r0 = {}, 0
    for nm, shape in layout:
        size = shape[0] * shape[1]
        nrow = -(-size // LANES)
        out[nm] = packed[r0:r0 + nrow].reshape(-1)[:size].reshape(shape)
        r0 += nrow
    return out


def kernel(x, meta, norm_mix_w, w_in, pool_w, pool_scale, conv_w, conv_b, dt_bias, a_log, d_skip, ssm_norm_w, w_out, norm_ffn_w, w_ff1, w_ff2, norm_f_w, loss_target, m_meta, m_norm_mix_w, m_w_in, m_pool_w, m_pool_scale, m_conv_w, m_conv_b, m_dt_bias, m_a_log, m_d_skip, m_ssm_norm_w, m_w_out, m_norm_ffn_w, m_w_ff1, m_w_ff2, m_norm_f_w, v_meta, v_norm_mix_w, v_w_in, v_pool_w, v_pool_scale, v_conv_w, v_conv_b, v_dt_bias, v_a_log, v_d_skip, v_ssm_norm_w, v_w_out, v_norm_ffn_w, v_w_ff1, v_w_ff2, v_norm_f_w):
    bsz, seq, d = x.shape
    t = seq + CHUNK
    n = bsz * t
    chip = 2 * lax.axis_index("x") + lax.axis_index("y")
    d_in = w_in.shape[2] * 4

    g_conv, g_meta = _comm(_Gather([conv_w[0], meta]), name="gather_small")
    convw = g_conv.transpose(1, 0, 2).reshape(CONV_W, D_XBC)
    meta_full = g_meta.transpose(1, 0, 2).reshape(N_META, d)
    (h0, hn1), (g_in,) = _embed_norm(x, meta_full, norm_mix_w, name="embed_norm",
                                     rider=_Gather([w_in[0].T.astype(BF16)]))
    h0f, hn1 = h0.reshape(n, d), hn1.reshape(n, d)
    late_weights = _Gather([w_out[0].astype(BF16), w_ff1[0].astype(BF16), w_ff2[0].astype(BF16)])
    win = g_in.reshape(d_in, d)
    wu, wz = win[:D_POOL], win[D_POOL:D_POOL + D_SSM]
    wx = win[D_POOL + D_SSM:D_POOL + D_SSM + D_XBC]
    wdt = jnp.pad(win[D_POOL + D_SSM + D_XBC:].reshape(N_GROUPS, HPG, d),
                  ((0, 0), (0, LANES - HPG), (0, 0))).reshape(D_DT, d)
    dtb, alog = _pad_heads(dt_bias), _pad_heads(a_log)
    dskip = jnp.repeat(d_skip, HEAD_DIM, axis=1)
    poolw = pool_w[0]

    u, z, dtr = _mm_fanout(hn1, [wu, wz, wdt], name="proj_uzdt")
    xbc, xc = _proj_conv(hn1, wx, convw, conv_b, name="proj_xbc")
    ypool = _pool_fwd(u.reshape(bsz, t, D_POOL), poolw, pool_scale, name="pool_fwd")
    xbc3 = xbc.reshape(bsz, t, D_XBC)
    xc = xc.reshape(bsz, t, D_XBC)
    z3, dtr3 = z.reshape(bsz, t, D_SSM), dtr.reshape(bsz, t, D_DT)
    dt3, acs3, tr3 = _ssd_prep(dtr3, dtb, alog, name="ssd_prep")
    (yn, ypre, sprev), (g_out, g_ff1, g_ff2) = _ssd_fwd(xc, dt3, acs3, tr3, z3, dskip, ssm_norm_w, name="ssd_fwd",
                                                        rider=late_weights)
    wo = g_out.reshape(D_POOL + D_SSM, d)
    wo_p, wo_s = wo[:D_POOL], wo[D_POOL:]
    w1 = g_ff1
    w2 = g_ff2.reshape(D_FF, d)
    ypool_f, yn_f = ypool.reshape(n, D_POOL), yn.reshape(n, D_SSM)
    add = lambda r, e: r + e
    h1, hn2 = _mm([ypool_f, yn_f], [wo_p, wo_s], name="out_proj", post=add, extras=(h0f,), norm_w=norm_ffn_w)
    act = _mm(hn2, w1, name="ff1", out_dtype=BF16)
    relu2 = lambda a: jnp.square(jnp.maximum(a, 0))
    h2 = _mm(act, w2, name="ff2", pre=relu2, post=add, extras=(h1,))
    dh2, dh2b, loss_acc, d_norm_f = _final_norm_loss(h2.reshape(bsz, t, d), loss_target, norm_f_w.reshape(1, d),
                                                     name="loss")

    dh2f, dh2bf = dh2.reshape(n, d), dh2b.reshape(n, d)
    dact = _mm(dh2bf, w2, name="ff2_bwd", nt=True, post=lambda r, a: r * (2.0 * jnp.maximum(a, 0).astype(F32)),
               extras=(act,), out_dtype=BF16)
    d_w2 = _mm_tn(act, dh2bf, name="ff2_dw", tk=2048, tn=1024, pre=relu2)
    d_w1 = _mm_tn(hn2, dact, name="ff1_dw", tk=1024, tn=2048, slab=D_FF // 4)
    dh1, dh1b, d_norm_ffn = _mm_rms_bwd(dact, w1, h1, norm_ffn_w, dh2f, name="ff1_bwd")
    dypool, dyn = _mm_fanout(dh1b, [wo_p, wo_s], name="out_proj_bwd")
    d_wo_p = _mm_tn(ypool_f, dh1b, name="out_pool_dw", tk=512, tn=1024)
    d_wo_s = _mm_tn(yn_f, dh1b, name="out_ssm_dw", tk=1536, tn=1024)
    big_late = [jnp.concatenate([d_wo_p, d_wo_s], axis=0).reshape(4, (D_POOL + D_SSM) // 4, d),
                d_w1, d_w2.reshape(4, D_FF // 4, d)]
    (dz, dxs, dbm, dcm, ddtr, d_nw, d_heads), landed_late = _ssd_bwd(
        xc, dtr3, dt3, acs3, tr3, z3, ypre, sprev, dyn.reshape(bsz, t, D_SSM), dtb, alog, dskip, ssm_norm_w, name="ssd_bwd",
        rider=_Exchange(big_late))
    dxbc, d_convwb = _conv_bwd(xbc3, dxs, dbm, dcm, convw, conv_b, name="conv_bwd")
    du, d_poolw, d_poolsc = _pool_bwd(u.reshape(bsz, t, D_POOL), dypool.reshape(bsz, t, D_POOL), poolw, pool_scale,
                                      name="pool_bwd")
    duf, dzf, dxbcf, ddtrf = du.reshape(n, D_POOL), dz.reshape(n, D_SSM), dxbc.reshape(n, D_XBC), ddtr.reshape(n, D_DT)
    heads = jnp.sum(d_heads, axis=0)
    small_early = _pack_small({
        "pool_w": d_poolw, "pool_scale": d_poolsc,
        "conv_w": jnp.sum(d_convwb[:, :CONV_W], axis=0), "conv_b": jnp.sum(d_convwb[:, CONV_W:CONV_W + 1], axis=0),
        "dt_bias": _unpad_heads(heads[:, 2]), "a_log": _unpad_heads(heads[:, 1]), "d_skip": _unpad_heads(heads[:, 0]),
        "ssm_norm_w": jnp.sum(d_nw, axis=0), "norm_ffn_w": d_norm_ffn, "norm_f_w": d_norm_f}, _SMALL_EARLY)
    d_wu = _mm_tn(duf, hn1, name="proj_u_dw", tk=512, tn=1024)
    d_wz = _mm_tn(dzf, hn1, name="proj_z_dw", tk=1536, tn=1024)
    d_wx, (early_all,) = _mm_tn(dxbcf, hn1, name="proj_xbc_dw", tk=1280, tn=1024, rider=_Exchange([], small_early))
    d_wdt = _mm_tn(ddtrf, hn1, name="proj_dt_dw", tk=512, tn=1024)
    d_win = jnp.concatenate([d_wu, d_wz, d_wx, d_wdt.reshape(N_GROUPS, LANES, d)[:, :HPG].reshape(N_HEADS, d)], axis=0)
    big_in = d_win.reshape(4, d_in // 4, d)
    dhn1, (landed_in,) = _mm([duf, dzf, dxbcf, ddtrf], [wu, wz, wx, wdt], name="proj_bwd",
                             rider=_Exchange([big_in]))
    grad_x, d_head_rows, d_norm_mix = _input_grad(
        dhn1.reshape(bsz, t, d), h0, norm_mix_w, dh1.reshape(bsz, t, d), seq, name="input_grad")

    landed = [landed_in] + list(landed_late)
    small_late = _pack_small({"norm_mix_w": d_norm_mix, "meta": jnp.sum(d_head_rows[:, PAD:], axis=0),
                              "loss": loss_acc[0:1, 0:1]}, _SMALL_LATE)
    (late_all,) = _comm(_Exchange([], small_late), name="exchange_small")
    mine = [_chip_sum(l, name=f"chip_sum_{i}") for i, l in enumerate(landed)]
    theirs = _comm(_Swap(mine), name="swap_cores")
    gsmall = {**_unpack_small(_device_sum(early_all, name="device_sum_early"), _SMALL_EARLY),
              **_unpack_small(_device_sum(late_all, name="device_sum_late"), _SMALL_LATE)}
    gsmall["conv_w"] = lax.dynamic_slice_in_dim(gsmall["conv_w"], chip * (D_XBC // 4), D_XBC // 4, axis=1)
    gsmall["meta"] = lax.dynamic_slice_in_dim(gsmall["meta"], chip * (d // 4), d // 4, axis=1)
    loss = gsmall["loss"][0, 0]

    given = dict(meta=(meta, m_meta, v_meta), norm_mix_w=(norm_mix_w, m_norm_mix_w, v_norm_mix_w),
                 w_in=(w_in, m_w_in, v_w_in), pool_w=(pool_w, m_pool_w, v_pool_w),
                 pool_scale=(pool_scale, m_pool_scale, v_pool_scale), conv_w=(conv_w, m_conv_w, v_conv_w),
                 conv_b=(conv_b, m_conv_b, v_conv_b), dt_bias=(dt_bias, m_dt_bias, v_dt_bias),
                 a_log=(a_log, m_a_log, v_a_log), d_skip=(d_skip, m_d_skip, v_d_skip),
                 ssm_norm_w=(ssm_norm_w, m_ssm_norm_w, v_ssm_norm_w), w_out=(w_out, m_w_out, v_w_out),
                 norm_ffn_w=(norm_ffn_w, m_norm_ffn_w, v_norm_ffn_w), w_ff1=(w_ff1, m_w_ff1, v_w_ff1),
                 w_ff2=(w_ff2, m_w_ff2, v_w_ff2), norm_f_w=(norm_f_w, m_norm_f_w, v_norm_f_w))
    big_names = ["w_in", "w_out", "w_ff1", "w_ff2"]
    results = {}
    for nm, (w, m, v) in given.items():
        if nm in big_names:
            i = big_names.index(nm)
            parts, shape2 = (mine[i], theirs[i]), mine[i].shape
        else:
            parts, shape2 = (gsmall[nm],), gsmall[nm].shape
        if nm == "w_in":
            outs = _adamw(w[0].T, parts, m[0].T, v[0].T, name=f"adamw_{nm}")
            results[nm] = [o.T[None] for o in outs]
        else:
            outs = _adamw(w.reshape(shape2), parts, m.reshape(shape2), v.reshape(shape2), name=f"adamw_{nm}")
            results[nm] = [o.reshape(w.shape) for o in outs]
    order = list(given)
    return (loss, grad_x, *[results[nm][0] for nm in order], *[results[nm][1] for nm in order],
            *[results[nm][2] for nm in order], *[results[nm][3] for nm in order])
```

```python
import jax
import jax.numpy as jnp
from jax import lax
from jax.experimental import pallas as pl
from jax.experimental.pallas import tpu as pltpu

F32 = jnp.float32
BF16 = jnp.bfloat16
MESH = pl.DeviceIdType.MESH
ANY = pl.BlockSpec(memory_space=pl.ANY)

D_MODEL = 1024
N_META = 16
CHUNK = 128
PAD = CHUNK - N_META
POOL_WINDOWS = (2, 4, 8, 16)
D_POOL = 512
POOL_GROUP = 128
D_SSM = 1536
N_HEADS = 24
N_GROUPS = 4
HPG = 6
HEAD_DIM = 64
D_STATE = 128
GW = HPG * HEAD_DIM
D_XBC = D_SSM + 2 * N_GROUPS * D_STATE
D_DT = N_GROUPS * 128
D_FF = 4096
CONV_W = 4
EPS = 1e-5
LANES = 128
VMEM_LIMIT = 56 * 1024 * 1024

ADAM_LR, ADAM_B1, ADAM_B2, ADAM_EPS, ADAM_WD, ADAM_STEP = 0.001, 0.9, 0.999, 1e-08, 0.01, 10


def _params(*sem):
    return pltpu.CompilerParams(dimension_semantics=sem, vmem_limit_bytes=VMEM_LIMIT)


def _pick(n, cands):
    for c in cands:
        if n % c == 0:
            return c
    raise ValueError(f"no block size for {n}")


def _dot(a, b):
    return jnp.dot(a.astype(BF16), b.astype(BF16), preferred_element_type=F32)


def _dot_nt(a, b):
    return lax.dot_general(a.astype(BF16), b.astype(BF16), (((1,), (1,)), ((), ())), preferred_element_type=F32)


def _dot_tn(a, b):
    return lax.dot_general(a.astype(BF16), b.astype(BF16), (((0,), (0,)), ((), ())), preferred_element_type=F32)


def _dot_exact(mask, x):
    m = mask.astype(BF16)
    hi = x.astype(BF16)
    r1 = x - hi.astype(F32)
    mid = r1.astype(BF16)
    lo = (r1 - mid.astype(F32)).astype(BF16)
    dot = lambda t: jnp.dot(m, t, preferred_element_type=F32)
    return dot(hi) + dot(mid) + dot(lo)


def _sigmoid(x):
    return 1.0 / (1.0 + jnp.exp(-x))


def _softplus(x):
    return jnp.maximum(x, 0.0) + jnp.log1p(jnp.exp(-jnp.abs(x)))


def _sum_all(x):
    return jnp.sum(jnp.sum(x, axis=1, keepdims=True), axis=0, keepdims=True)


ROW_TILES = (1056, 768, 704, 512, 384, 256, 128)
TILE_BUDGET = 28 * 1024 * 1024


def _row_tile(n, bytes_per_row, fixed_bytes, budget=TILE_BUDGET):
    for tm in ROW_TILES:
        if n % tm == 0 and 2 * (tm * bytes_per_row + fixed_bytes) <= budget:
            return tm
    raise ValueError(f"no row tile for {n}")


WIDE_BUDGET = 38 * 1024 * 1024


def _mm(a, w, *, name, tn=512, nt=False, pre=None, post=None, extras=(), out_dtype=F32, norm_w=None, rider=None):
    assert norm_w is None or (rider is None and out_dtype == F32)
    a_list = list(a) if isinstance(a, (list, tuple)) else [a]
    w_list = list(w) if isinstance(w, (list, tuple)) else [w]
    n_a, n_ex = len(a_list), len(extras)
    n = a_list[0].shape[0]
    shard = w_list[0].shape[2] if w_list[0].ndim == 3 else None
    assert shard is None or (not nt and n_a == 1 and shard % tn == 0)
    m = w_list[0].shape[0] * shard if shard else w_list[0].shape[0] if nt else w_list[0].shape[1]
    tn = min(tn, m)
    size = lambda dt: jnp.dtype(dt).itemsize
    per_row = (sum(x.shape[1] * size(x.dtype) for x in a_list) + m * size(out_dtype)
               + sum(m * size(e.dtype) for e in extras) + (2 * m if norm_w is not None else 0))
    tm = _row_tile(n, per_row, sum(x.size * size(x.dtype) for x in w_list) // 2, WIDE_BUDGET)
    n_norm = 0 if norm_w is None else 1

    def body(*refs):
        a_refs, w_refs, ex_refs = refs[:n_a], refs[n_a:2 * n_a], refs[2 * n_a:2 * n_a + n_ex]
        o_ref = refs[2 * n_a + n_ex + n_norm]
        avs = [(a_ref[...] if pre is None else pre(a_ref[...])).astype(BF16) for a_ref in a_refs]
        for c0 in range(0, m, tn):
            r = None
            for av, w_ref in zip(avs, w_refs):
                if shard:
                    term = _dot(av, w_ref[c0 // shard, :, c0 % shard:c0 % shard + tn])
                else:
                    term = _dot_nt(av, w_ref[c0:c0 + tn, :]) if nt else _dot(av, w_ref[:, c0:c0 + tn])
                r = term if r is None else r + term
            if post is not None:
                r = post(r, *[e[:, c0:c0 + tn] for e in ex_refs])
            o_ref[:, c0:c0 + tn] = r.astype(out_dtype)
        if n_norm:
            x = o_ref[...]
            scale = lax.rsqrt(jnp.mean(x * x, axis=-1, keepdims=True) + EPS)
            refs[2 * n_a + n_ex + 2][...] = (x * scale * refs[2 * n_a + n_ex][...]).astype(BF16)

    a_specs = [pl.BlockSpec((tm, x.shape[1]), lambda i: (i, 0)) for x in a_list]
    w_specs = [pl.BlockSpec(x.shape, lambda i, nd=x.ndim: (0,) * nd, pipeline_mode=pl.Buffered(1)) for x in w_list]
    blk = pl.BlockSpec((tm, m), lambda i: (i, 0))
    vec = [pl.BlockSpec((1, m), lambda i: (0, 0))] * n_norm
    grid = (n // tm,)
    ride = _Ride(rider, body, 2 * n_a + n_ex + n_norm, 1 + n_norm, 0, grid)
    outs = pl.pallas_call(
        ride.body, name=name, grid=grid,
        in_specs=a_specs + w_specs + [blk] * n_ex + vec + ride.in_specs,
        out_specs=[blk] * (1 + n_norm) + ride.out_specs,
        out_shape=[jax.ShapeDtypeStruct((n, m), out_dtype)] + [jax.ShapeDtypeStruct((n, m), BF16)] * n_norm + ride.out_shape,
        scratch_shapes=ride.scratch, compiler_params=_params(*ride.semantics(("parallel",))),
    )(*a_list, *w_list, *extras, *([norm_w] * n_norm), *ride.args)
    if n_norm:
        return outs[0], outs[1]
    return (outs[0], outs[1:]) if rider else outs[0]


def _mm_fanout(a, ws, *, name, tn=512):
    n, k = a.shape
    ms = [w.shape[0] for w in ws]
    tm = _row_tile(n, k * 2 + 4 * sum(ms), sum(w.size for w in ws), WIDE_BUDGET)
    n_w = len(ws)

    def body(a_ref, *refs):
        av = a_ref[...]
        for w_ref, o_ref, m in zip(refs[:n_w], refs[n_w:], ms):
            for c0 in range(0, m, tn):
                o_ref[:, c0:c0 + tn] = _dot_nt(av, w_ref[c0:c0 + tn, :])

    return pl.pallas_call(
        body, name=name, grid=(n // tm,),
        in_specs=[pl.BlockSpec((tm, k), lambda i: (i, 0))]
        + [pl.BlockSpec(w.shape, lambda i: (0, 0), pipeline_mode=pl.Buffered(1)) for w in ws],
        out_specs=[pl.BlockSpec((tm, m), lambda i: (i, 0)) for m in ms],
        out_shape=[jax.ShapeDtypeStruct((n, m), F32) for m in ms],
        compiler_params=_params("parallel"),
    )(a, *ws)


def _mm_tn(a, g, *, name, tk, tn, pre=None, slab=None, rider=None):
    n, k = a.shape
    m = g.shape[1]
    tk, tn = min(tk, k), min(tn, m)
    tm = _row_tile(n, tk * jnp.dtype(a.dtype).itemsize + tn * jnp.dtype(g.dtype).itemsize, tk * tn * 4)
    steps = n // tm

    def body(a_ref, g_ref, o_ref, acc_ref):
        r = pl.program_id(2)

        @pl.when(r == 0)
        def _():
            acc_ref[...] = jnp.zeros_like(acc_ref)

        av = a_ref[...]
        if pre is not None:
            av = pre(av)
        if slab:
            for s in range(tn // slab):
                acc_ref[s] += _dot_tn(av, g_ref[:, s * slab:(s + 1) * slab])
        else:
            acc_ref[...] += _dot_tn(av, g_ref[...])

        @pl.when(r == steps - 1)
        def _():
            o_ref[...] = acc_ref[...].astype(BF16)

    if slab:
        block, out_spec = (tn // slab, tk, slab), pl.BlockSpec((tn // slab, tk, slab), lambda i, j, r: (j, i, 0))
        out_shape = jax.ShapeDtypeStruct((m // slab, k, slab), BF16)
    else:
        block, out_spec = (tk, tn), pl.BlockSpec((tk, tn), lambda i, j, r: (i, j))
        out_shape = jax.ShapeDtypeStruct((k, m), BF16)
    grid = (k // tk, m // tn, steps)
    ride = _Ride(rider, body, 2, 1, 1, grid)
    outs = pl.pallas_call(
        ride.body, name=name, grid=grid,
        in_specs=[pl.BlockSpec((tm, tk), lambda i, j, r: (r, i)), pl.BlockSpec((tm, tn), lambda i, j, r: (r, j))]
        + ride.in_specs,
        out_specs=[out_spec] + ride.out_specs, out_shape=[out_shape] + ride.out_shape,
        scratch_shapes=[pltpu.VMEM(block, F32)] + ride.scratch,
        compiler_params=_params(*ride.semantics(("parallel", "parallel", "arbitrary"))),
    )(a, g, *ride.args)
    return (outs[0], outs[1:]) if rider else outs[0]


def _mm_tn_cat(a_list, g, *, name):
    n, m = g.shape
    ks = [a.shape[1] for a in a_list]
    size = lambda x: jnp.dtype(x.dtype).itemsize
    tm = _row_tile(n, sum(a.shape[1] * size(a) for a in a_list) + m * size(g), sum(ks) * m * 4)
    steps, n_a = n // tm, len(a_list)

    def body(*refs):
        g_ref, o_ref, acc_ref = refs[n_a], refs[n_a + 1], refs[n_a + 2]
        r = pl.program_id(0)

        @pl.when(r == 0)
        def _():
            acc_ref[...] = jnp.zeros_like(acc_ref)

        gv, k0 = g_ref[...], 0
        for a_ref, k in zip(refs[:n_a], ks):
            acc_ref[k0:k0 + k, :] += _dot_tn(a_ref[...], gv)
            k0 += k

        @pl.when(r == steps - 1)
        def _():
            o_ref[...] = acc_ref[...].astype(BF16)

    return pl.pallas_call(
        body, name=name, grid=(steps,),
        in_specs=[pl.BlockSpec((tm, k), lambda r: (r, 0)) for k in ks] + [pl.BlockSpec((tm, m), lambda r: (r, 0))],
        out_specs=pl.BlockSpec((sum(ks), m), lambda r: (0, 0)),
        out_shape=jax.ShapeDtypeStruct((sum(ks), m), BF16),
        scratch_shapes=[pltpu.VMEM((sum(ks), m), F32)],
        compiler_params=_params("arbitrary"),
    )(*a_list, g)


def _mm_rms_bwd(a, w, h, w_norm, dres, *, name):
    n, k = a.shape
    d = h.shape[1]
    slabs, _, ks = w.shape
    tm = _row_tile(n, k * jnp.dtype(a.dtype).itemsize + d * (4 + 4 + 4 + 2), d * k, WIDE_BUDGET)

    def body(a_ref, w_ref, h_ref, wn_ref, dres_ref, dx_ref, dxb_ref, dw_ref):
        @pl.when(pl.program_id(0) == 0)
        def _():
            dw_ref[...] = jnp.zeros_like(dw_ref)

        dyv = None
        for s in range(slabs):
            part = _dot_nt(a_ref[:, s * ks:(s + 1) * ks], w_ref[s])
            dyv = part if dyv is None else dyv + part
        x = h_ref[...]
        r = lax.rsqrt(jnp.mean(x * x, axis=-1, keepdims=True) + EPS)
        g = dyv * wn_ref[...]
        dx = r * (g - x * (r * r) * jnp.mean(g * x, axis=-1, keepdims=True)) + dres_ref[...]
        dx_ref[...] = dx
        dxb_ref[...] = dx.astype(BF16)
        dw_ref[...] += jnp.sum(dyv * x * r, axis=0, keepdims=True)

    row = pl.BlockSpec((tm, d), lambda i: (i, 0))
    vec = pl.BlockSpec((1, d), lambda i: (0, 0))
    return pl.pallas_call(
        body, name=name, grid=(n // tm,),
        in_specs=[pl.BlockSpec((tm, k), lambda i: (i, 0)),
                  pl.BlockSpec(w.shape, lambda i: (0, 0, 0), pipeline_mode=pl.Buffered(1)), row, vec, row],
        out_specs=[row, row, vec],
        out_shape=[jax.ShapeDtypeStruct((n, d), F32), jax.ShapeDtypeStruct((n, d), BF16), jax.ShapeDtypeStruct((1, d), F32)],
        compiler_params=_params("arbitrary"),
    )(a, w, h, w_norm, dres)


def _embed_norm(x, meta, w, *, name, rider=None):
    bsz, seq, d = x.shape
    t = seq + CHUNK
    nc = t // CHUNK

    def body(x_ref, meta_ref, w_ref, h_ref, hn_ref):
        j = pl.program_id(0)
        first = jnp.concatenate([jnp.zeros((PAD, d), F32), meta_ref[...]], axis=0)
        for e in range(bsz):
            h = jnp.where(j == 0, first, x_ref[e])
            r = lax.rsqrt(jnp.mean(h * h, axis=-1, keepdims=True) + EPS)
            h_ref[e] = h
            hn_ref[e] = (h * r * w_ref[...]).astype(BF16)

    row = pl.BlockSpec((bsz, CHUNK, d), lambda j: (0, j, 0))
    grid = (nc,)
    ride = _Ride(rider, body, 3, 2, 0, grid)
    outs = pl.pallas_call(
        ride.body, name=name, grid=grid,
        in_specs=[pl.BlockSpec((bsz, CHUNK, d), lambda j: (0, jnp.maximum(j - 1, 0), 0)),
                  pl.BlockSpec((N_META, d), lambda j: (0, 0)), pl.BlockSpec((1, d), lambda j: (0, 0))] + ride.in_specs,
        out_specs=[row, row] + ride.out_specs,
        out_shape=[jax.ShapeDtypeStruct((bsz, t, d), F32), jax.ShapeDtypeStruct((bsz, t, d), BF16)] + ride.out_shape,
        scratch_shapes=ride.scratch, compiler_params=_params(*ride.semantics(("parallel",))),
    )(x, meta, w, *ride.args)
    return outs[:2], outs[2:]


def _final_norm_loss(h2, target, w, *, name):
    bsz, t, d = h2.shape
    nc = t // CHUNK

    def body(h_ref, t_ref, w_ref, dh_ref, dhb_ref, loss_ref, dw_ref):
        j = pl.program_id(0)

        @pl.when(j == 0)
        def _():
            loss_ref[...] = jnp.zeros_like(loss_ref)
            dw_ref[...] = jnp.zeros_like(dw_ref)

        wv = w_ref[...]
        for e in range(bsz):
            x = h_ref[e]
            r = lax.rsqrt(jnp.mean(x * x, axis=-1, keepdims=True) + EPS)
            diff = jnp.where(j > 0, x * r * wv - t_ref[e], 0.0)
            loss_ref[...] += _sum_all(diff * diff) * (0.5 / d)
            dy = diff * (1.0 / d)
            g = dy * wv
            dh = r * (g - x * (r * r) * jnp.mean(g * x, axis=-1, keepdims=True))
            dh_ref[e] = dh
            dhb_ref[e] = dh.astype(BF16)
            dw_ref[...] += jnp.sum(dy * x * r, axis=0, keepdims=True)

    row = pl.BlockSpec((bsz, CHUNK, d), lambda j: (0, j, 0))
    return pl.pallas_call(
        body, name=name, grid=(nc,),
        in_specs=[row, pl.BlockSpec((bsz, CHUNK, d), lambda j: (0, jnp.maximum(j - 1, 0), 0)),
                  pl.BlockSpec((1, d), lambda j: (0, 0))],
        out_specs=[row, row, pl.BlockSpec((8, LANES), lambda j: (0, 0)), pl.BlockSpec((1, d), lambda j: (0, 0))],
        out_shape=[jax.ShapeDtypeStruct((bsz, t, d), F32), jax.ShapeDtypeStruct((bsz, t, d), BF16),
                   jax.ShapeDtypeStruct((8, LANES), F32), jax.ShapeDtypeStruct((1, d), F32)],
        compiler_params=_params("arbitrary"),
    )(h2, target, w)


def _pool_masks(j, transposed):
    r = lax.broadcasted_iota(jnp.int32, (CHUNK, 2 * CHUNK), 0)
    c = lax.broadcasted_iota(jnp.int32, (CHUNK, 2 * CHUNK), 1)
    masks = []
    for w in POOL_WINDOWS:
        if transposed:
            m = (c >= r) & (c < r + w)
        else:
            s = c - CHUNK
            m = (s <= r) & (s > r - w) & (s + j * CHUNK >= 0)
        masks.append(m.astype(F32))
    return masks


def _pool_count(t_global, w):
    return jnp.clip(t_global - PAD + 1, 1, w).astype(F32)


def _pool_fwd(u, pool_w, pool_scale, *, name):
    bsz, t, _ = u.shape
    nc = t // CHUNK

    def body(prev_ref, cur_ref, pw_ref, sc_ref, o_ref):
        j = pl.program_id(0)
        masks = _pool_masks(j, False)
        tg = j * CHUNK + lax.broadcasted_iota(jnp.int32, (CHUNK, 1), 0)
        count = [_pool_count(tg, w) for w in POOL_WINDOWS]
        units = [(e, gi) for e in range(bsz) for gi in range(len(POOL_WINDOWS))]
        sl = lambda gi: pl.ds(gi * POOL_GROUP, POOL_GROUP)
        cur = {(e, gi): cur_ref[e, :, sl(gi)] for e, gi in units}
        both = {(e, gi): jnp.concatenate([prev_ref[e, :, sl(gi)], cur[e, gi]], axis=0) for e, gi in units}
        win = {(e, gi): _dot_exact(masks[gi], both[e, gi]) for e, gi in units}
        pooled = {(e, gi): win[e, gi] / count[gi] - cur[e, gi] for e, gi in units}
        mixed = {(e, gi): _dot(pooled[e, gi], pw_ref[gi]) for e, gi in units}
        for e, gi in units:
            o_ref[e, :, sl(gi)] = (mixed[e, gi] * sc_ref[:, sl(gi)]).astype(BF16)

    blk = lambda f: pl.BlockSpec((bsz, CHUNK, D_POOL), f)
    return pl.pallas_call(
        body, name=name, grid=(nc,),
        in_specs=[blk(lambda j: (0, jnp.maximum(j - 1, 0), 0)), blk(lambda j: (0, j, 0)),
                  pl.BlockSpec((4, POOL_GROUP, POOL_GROUP), lambda j: (0, 0, 0)),
                  pl.BlockSpec((1, D_POOL), lambda j: (0, 0))],
        out_specs=blk(lambda j: (0, j, 0)), out_shape=jax.ShapeDtypeStruct(u.shape, BF16),
        compiler_params=_params("parallel"),
    )(u, u, pool_w, pool_scale)


def _pool_bwd(u, dyp, pool_w, pool_scale, *, name):
    bsz, t, _ = u.shape
    nc = t // CHUNK

    def body(prev_ref, cur_ref, dy_ref, dyn_ref, pw_ref, sc_ref, du_ref, dpw_ref, dsc_ref):
        j = pl.program_id(0)

        @pl.when(j == 0)
        def _():
            dpw_ref[...] = jnp.zeros_like(dpw_ref)
            dsc_ref[...] = jnp.zeros_like(dsc_ref)

        fwd = _pool_masks(j, False)
        bwd = _pool_masks(j, True)
        tg = j * CHUNK + lax.broadcasted_iota(jnp.int32, (CHUNK, 1), 0)
        count = [_pool_count(tg, w) for w in POOL_WINDOWS]
        count_next = [_pool_count(tg + CHUNK, w) for w in POOL_WINDOWS]
        has_next = j < nc - 1
        groups = range(len(POOL_WINDOWS))
        units = [(e, gi) for e in range(bsz) for gi in groups]
        sl = lambda gi: pl.ds(gi * POOL_GROUP, POOL_GROUP)
        cur = {(e, gi): cur_ref[e, :, sl(gi)] for e, gi in units}
        both = {(e, gi): jnp.concatenate([prev_ref[e, :, sl(gi)], cur[e, gi]], axis=0) for e, gi in units}
        win = {(e, gi): _dot_exact(fwd[gi], both[e, gi]) for e, gi in units}
        pooled = {(e, gi): win[e, gi] / count[gi] - cur[e, gi] for e, gi in units}
        dy = {(e, gi): dy_ref[e, :, sl(gi)] for e, gi in units}
        mixed = {(e, gi): _dot(pooled[e, gi], pw_ref[gi]) for e, gi in units}
        dm = {(e, gi): dy[e, gi] * sc_ref[:, sl(gi)] for e, gi in units}
        dm_next = {(e, gi): jnp.where(has_next, dyn_ref[e, :, sl(gi)], 0.0) * sc_ref[:, sl(gi)] for e, gi in units}
        dpw = {(e, gi): _dot_tn(pooled[e, gi], dm[e, gi]) for e, gi in units}
        dpooled = {(e, gi): _dot_nt(dm[e, gi], pw_ref[gi]) for e, gi in units}
        dpooled_next = {(e, gi): _dot_nt(dm_next[e, gi], pw_ref[gi]) for e, gi in units}
        spread = {(e, gi): jnp.concatenate([dpooled[e, gi] / count[gi], dpooled_next[e, gi] / count_next[gi]], axis=0)
                  for e, gi in units}
        back = {(e, gi): _dot_exact(bwd[gi], spread[e, gi]) for e, gi in units}
        for e, gi in units:
            du_ref[e, :, sl(gi)] = (back[e, gi] - dpooled[e, gi]).astype(BF16)
        for gi in groups:
            dsc, dw = None, None
            for e in range(bsz):
                term = jnp.sum(dy[e, gi] * mixed[e, gi], axis=0, keepdims=True)
                dsc = term if dsc is None else dsc + term
                dw = dpw[e, gi] if dw is None else dw + dpw[e, gi]
            dsc_ref[:, sl(gi)] += dsc
            dpw_ref[gi] += dw

    blk = lambda f: pl.BlockSpec((bsz, CHUNK, D_POOL), f)
    return pl.pallas_call(
        body, name=name, grid=(nc,),
        in_specs=[blk(lambda j: (0, jnp.maximum(j - 1, 0), 0)), blk(lambda j: (0, j, 0)),
                  blk(lambda j: (0, j, 0)), blk(lambda j: (0, jnp.minimum(j + 1, nc - 1), 0)),
                  pl.BlockSpec((4, POOL_GROUP, POOL_GROUP), lambda j: (0, 0, 0)),
                  pl.BlockSpec((1, D_POOL), lambda j: (0, 0))],
        out_specs=[blk(lambda j: (0, j, 0)), pl.BlockSpec((4, POOL_GROUP, POOL_GROUP), lambda j: (0, 0, 0)),
                   pl.BlockSpec((1, D_POOL), lambda j: (0, 0))],
        out_shape=[jax.ShapeDtypeStruct(u.shape, BF16), jax.ShapeDtypeStruct((4, POOL_GROUP, POOL_GROUP), F32),
                   jax.ShapeDtypeStruct((1, D_POOL), F32)],
        compiler_params=_params("arbitrary"),
    )(u, u, dyp, dyp, pool_w, pool_scale)


CONV_SLAB = 512


def _conv_taps(tail, cur, keep_tail):
    ext = jnp.concatenate([jnp.where(keep_tail, tail, 0.0), cur], axis=0)
    return [(pltpu.roll(ext, CONV_W - 1 - k, 0) if k < CONV_W - 1 else ext)[8:] for k in range(CONV_W)]


def _conv_pre(taps, w_ref, b_ref, sl):
    acc = b_ref[:, sl]
    for k in range(CONV_W):
        acc = acc + w_ref[k:k + 1, sl] * taps[k]
    return acc


def _proj_conv(hn, w, conv_w, conv_b, *, name):
    n, d = hn.shape
    c = w.shape[0]
    assert PAD >= CONV_W - 1
    tm = _row_tile(n, d * 2 + c * (4 + 2), c * d, WIDE_BUDGET)

    def body(hn_ref, w_ref, cw_ref, cb_ref, xbc_ref, xc_ref, tail_ref):
        @pl.when(pl.program_id(0) == 0)
        def _():
            tail_ref[...] = jnp.zeros_like(tail_ref)

        av = hn_ref[...]
        starts = list(range(0, c, CONV_SLAB))

        def project(c0):
            xbc_ref[:, pl.ds(c0, CONV_SLAB)] = _dot_nt(av, w_ref[c0:c0 + CONV_SLAB, :])

        def convolve(c0):
            sl = pl.ds(c0, CONV_SLAB)
            xb = xbc_ref[:, sl]
            pre = _conv_pre(_conv_taps(tail_ref[:, sl], xb, True), cw_ref, cb_ref, sl)
            xc_ref[:, sl] = (pre * _sigmoid(pre)).astype(BF16)
            tail_ref[:, sl] = xb[tm - 8:, :]

        project(starts[0])
        for c0, c_next in zip(starts, starts[1:] + [None]):
            if c_next is not None:
                project(c_next)
            convolve(c0)

    row = lambda width: pl.BlockSpec((tm, width), lambda i: (i, 0))
    return pl.pallas_call(
        body, name=name, grid=(n // tm,),
        in_specs=[row(d), pl.BlockSpec(w.shape, lambda i: (0, 0), pipeline_mode=pl.Buffered(1)),
                  pl.BlockSpec((CONV_W, c), lambda i: (0, 0)), pl.BlockSpec((1, c), lambda i: (0, 0))],
        out_specs=[row(c), row(c)],
        out_shape=[jax.ShapeDtypeStruct((n, c), F32), jax.ShapeDtypeStruct((n, c), BF16)],
        scratch_shapes=[pltpu.VMEM((8, c), F32)],
        compiler_params=_params("arbitrary"),
    )(hn, w, conv_w, conv_b)


def _conv_bwd(xbc, dxs, db, dc, conv_w, conv_b, *, name):
    bsz, t, c = xbc.shape
    nc = t // CHUNK
    halo = 16
    rows = CHUNK + halo

    def body(tail_ref, cur_ref, head_ref, dxs_ref, db_ref, dc_ref, dxs_head, db_head, dc_head, w_ref, b_ref,
             dx_ref, dwb_ref):
        j = pl.program_id(1)

        @pl.when(j == 0)
        def _():
            dwb_ref[...] = jnp.zeros_like(dwb_ref)

        has_prev, has_next = j > 0, j < nc - 1
        for c0 in range(0, c, CONV_SLAB):
            sl = pl.ds(c0, CONV_SLAB)
            if c0 < D_SSM:
                dxc, dxc_next = dxs_ref[0, :, sl], dxs_head[0, :, sl]
            elif c0 < D_SSM + D_POOL:
                dxc, dxc_next = db_ref[0], db_head[0]
            else:
                dxc, dxc_next = dc_ref[0], dc_head[0]
            dxc = jnp.concatenate([dxc.astype(F32), jnp.where(has_next, dxc_next.astype(F32), 0.0)], axis=0)
            ext = jnp.concatenate([jnp.where(has_prev, tail_ref[0, :, sl], 0.0), cur_ref[0, :, sl],
                                   jnp.where(has_next, head_ref[0, :, sl], 0.0)], axis=0)
            taps = [(pltpu.roll(ext, CONV_W - 1 - k, 0) if k < CONV_W - 1 else ext)[8:] for k in range(CONV_W)]
            pre = _conv_pre(taps, w_ref, b_ref, sl)
            s = _sigmoid(pre)
            dpre = dxc * (s * (1.0 + pre * (1.0 - s)))
            acc = w_ref[CONV_W - 1:CONV_W, sl] * dpre[:CHUNK]
            for k in range(CONV_W - 1):
                up = CONV_W - 1 - k
                acc = acc + w_ref[k:k + 1, sl] * pltpu.roll(dpre, rows - up, 0)[:CHUNK]
            dx_ref[0, :, sl] = acc.astype(BF16)
            for k in range(CONV_W):
                dwb_ref[0, k:k + 1, sl] += jnp.sum(dpre[:CHUNK] * taps[k][:CHUNK], axis=0, keepdims=True)
            dwb_ref[0, CONV_W:CONV_W + 1, sl] += jnp.sum(dpre[:CHUNK], axis=0, keepdims=True)

    assert CONV_SLAB == D_POOL and D_SSM % CONV_SLAB == 0
    row = lambda width: pl.BlockSpec((1, CHUNK, width), lambda b, j: (b, j, 0))
    nxt = lambda width: pl.BlockSpec(
        (1, halo, width), lambda b, j: (b, jnp.minimum((j + 1) * (CHUNK // halo), t // halo - 1), 0))
    return pl.pallas_call(
        body, name=name, grid=(bsz, nc),
        in_specs=[pl.BlockSpec((1, 8, c), lambda b, j: (b, jnp.maximum(j * (CHUNK // 8) - 1, 0), 0)), row(c), nxt(c),
                  row(D_SSM), row(D_POOL), row(D_POOL), nxt(D_SSM), nxt(D_POOL), nxt(D_POOL),
                  pl.BlockSpec((CONV_W, c), lambda b, j: (0, 0)), pl.BlockSpec((1, c), lambda b, j: (0, 0))],
        out_specs=[row(c), pl.BlockSpec((1, 8, c), lambda b, j: (b, 0, 0))],
        out_shape=[jax.ShapeDtypeStruct(xbc.shape, BF16), jax.ShapeDtypeStruct((bsz, 8, c), F32)],
        compiler_params=_params("parallel", "arbitrary"),
    )(xbc, xbc, xbc, dxs, db, dc, dxs, db, dc, conv_w, conv_b)


def _dt_valid(j):
    lane = lax.broadcasted_iota(jnp.int32, (CHUNK, LANES), 1)
    row = lax.broadcasted_iota(jnp.int32, (CHUNK, LANES), 0)
    return (lane < HPG) & ((j > 0) | (row >= PAD))


def _ssd_prep(dtr, dtb, alog, *, name):
    bsz, t, _ = dtr.shape
    nc = t // CHUNK

    def body(dtr_ref, dtb_ref, alog_ref, dt_ref, acs_ref, tr_ref):
        j = pl.program_id(0)
        valid = _dt_valid(j)
        row = lax.broadcasted_iota(jnp.int32, (CHUNK, LANES), 0)
        lane = lax.broadcasted_iota(jnp.int32, (CHUNK, LANES), 1)
        tril = (row >= lane).astype(F32)
        units = [(e, g) for e in range(bsz) for g in range(N_GROUPS)]
        sl = lambda g: pl.ds(g * LANES, LANES)
        dt = {(e, g): jnp.where(valid, _softplus(dtr_ref[e, :, sl(g)] + dtb_ref[g]), 0.0) for e, g in units}
        acs = {(e, g): _dot_exact(tril, dt[e, g] * -jnp.exp(alog_ref[g])) for e, g in units}
        for e, g in units:
            dt_ref[e, :, sl(g)] = dt[e, g]
            acs_ref[e, :, sl(g)] = acs[e, g]
            tr_ref[e, 0, g, 0:8, :] = dt[e, g].T[0:8]
            tr_ref[e, 0, g, 8:16, :] = acs[e, g].T[0:8]

    blk = pl.BlockSpec((bsz, CHUNK, D_DT), lambda j: (0, j, 0))
    const = pl.BlockSpec((N_GROUPS, 1, LANES), lambda j: (0, 0, 0))
    return pl.pallas_call(
        body, name=name, grid=(nc,), in_specs=[blk, const, const],
        out_specs=[blk, blk, pl.BlockSpec((bsz, 1, N_GROUPS, 16, LANES), lambda j: (0, j, 0, 0, 0))],
        out_shape=[jax.ShapeDtypeStruct(dtr.shape, F32), jax.ShapeDtypeStruct(dtr.shape, F32),
                   jax.ShapeDtypeStruct((bsz, nc, N_GROUPS, 16, LANES), F32)],
        compiler_params=_params("parallel"),
    )(dtr, dtb, alog)


def _ssd_decay(dt, acs, tr):
    lane = lax.broadcasted_iota(jnp.int32, (CHUNK, LANES), 1)
    row = lax.broadcasted_iota(jnp.int32, (CHUNK, LANES), 0)
    return dict(lane=lane, row=row, dt=dt, causal=row >= lane, acs=acs, acs_t=tr[8:16], dt_t=tr[0:8],
                aend=acs[CHUNK - 1:CHUNK, :])


def _ssd_specs(bsz, nc, rev):
    ch = (lambda j: nc - 1 - j) if rev else (lambda j: j)
    return dict(
        xs=pl.BlockSpec((bsz, CHUNK, GW), lambda g, j: (0, ch(j), g)),
        bm=pl.BlockSpec((bsz, CHUNK, D_STATE), lambda g, j: (0, ch(j), D_SSM // D_STATE + g)),
        cm=pl.BlockSpec((bsz, CHUNK, D_STATE), lambda g, j: (0, ch(j), D_SSM // D_STATE + N_GROUPS + g)),
        lane_blk=pl.BlockSpec((bsz, CHUNK, LANES), lambda g, j: (0, ch(j), g)),
        grp_const=pl.BlockSpec((1, 1, LANES), lambda g, j: (g, 0, 0)),
        grp_vec=pl.BlockSpec((1, GW), lambda g, j: (0, g)),
        state=pl.BlockSpec((bsz, 1, D_STATE, GW), lambda g, j: (0, ch(j), 0, g)),
        tr=pl.BlockSpec((bsz, 1, 1, 16, LANES), lambda g, j: (0, ch(j), g, 0, 0)),
    )


def _ssd_fwd(xc, dt, acs, tr, z, dskip, normw, *, name, rider=None):
    bsz, t, _ = xc.shape
    nc = t // CHUNK
    sp = _ssd_specs(bsz, nc, False)

    def body(xs_ref, b_ref, c_ref, dt_ref, acs_ref, tr_ref, z_ref, dsk_ref, nw_ref, yn_ref, y_ref, sp_ref, s_ref):
        j = pl.program_id(1)

        @pl.when(j == 0)
        def _():
            s_ref[...] = jnp.zeros_like(s_ref)

        ex = range(bsz)
        units = [(e, r) for e in ex for r in range(HPG)]
        full = lambda v: jnp.broadcast_to(v, (CHUNK, LANES))
        pair = lambda r: pl.ds((r // 2) * LANES, LANES)
        q = [_ssd_decay(dt_ref[e], acs_ref[e], tr_ref[e, 0, 0]) for e in ex]
        for e in ex:
            sp_ref[e, 0] = s_ref[e]
        bm, cm = [b_ref[e] for e in ex], [c_ref[e] for e in ex]
        cb = [_dot_nt(cm[e], bm[e]) for e in ex]
        low = q[0]["lane"] < HEAD_DIM
        col = {(e, r): full(q[e]["acs"][:, r:r + 1]) for e, r in units}
        aend = {(e, r): q[e]["aend"][:, r:r + 1] for e, r in units}
        decay = {(e, r): jnp.exp(jnp.where(q[e]["causal"], col[e, r] - q[e]["acs_t"][r:r + 1, :], -jnp.inf))
                 for e, r in units}
        mp = {(e, r): cb[e] * decay[e, r] * q[e]["dt_t"][r:r + 1, :] for e, r in units}
        ce = {(e, r): cm[e] * jnp.exp(col[e, r]) for e, r in units}
        bk = {(e, r): bm[e] * (jnp.exp(aend[e, r] - col[e, r]) * full(q[e]["dt"][:, r:r + 1])) for e, r in units}
        xp = {(e, r): xs_ref[e, :, pair(r)] for e, r in units}
        s_old = {(e, r): s_ref[e, :, pair(r)] for e, r in units}
        y_h = {u: _dot(mp[u], xp[u]) + _dot(ce[u], s_old[u]) for u in units}
        s_h = {u: jnp.exp(aend[u]) * s_old[u] + _dot_tn(bk[u], xp[u]) for u in units}
        for e in ex:
            for r in range(0, HPG, 2):
                y_ref[e, :, pair(r)] = jnp.where(low, y_h[e, r], y_h[e, r + 1])
                s_ref[e, :, pair(r)] = jnp.where(low, s_h[e, r], s_h[e, r + 1])
        y = [y_ref[e] + dsk_ref[...] * xs_ref[e] for e in ex]
        zz = [z_ref[e] for e in ex]
        yg = [y[e] * (zz[e] * _sigmoid(zz[e])) for e in ex]
        rstd = [lax.rsqrt(jnp.mean(yg[e] * yg[e], axis=-1, keepdims=True) + EPS) for e in ex]
        for e in ex:
            y_ref[e] = y[e]
            yn_ref[e] = (yg[e] * rstd[e] * nw_ref[...]).astype(BF16)

    grid = (N_GROUPS, nc)
    ride = _Ride(rider, body, 9, 3, 1, grid)
    outs = pl.pallas_call(
        ride.body, name=name, grid=grid,
        in_specs=[sp["xs"], sp["bm"], sp["cm"], sp["lane_blk"], sp["lane_blk"], sp["tr"], sp["xs"],
                  sp["grp_vec"], sp["grp_vec"]] + ride.in_specs,
        out_specs=[sp["xs"], sp["xs"], sp["state"]] + ride.out_specs,
        out_shape=[jax.ShapeDtypeStruct((bsz, t, D_SSM), BF16), jax.ShapeDtypeStruct((bsz, t, D_SSM), F32),
                   jax.ShapeDtypeStruct((bsz, nc, D_STATE, D_SSM), F32)] + ride.out_shape,
        scratch_shapes=[pltpu.VMEM((bsz, D_STATE, GW), F32)] + ride.scratch,
        compiler_params=_params(*ride.semantics(("parallel", "arbitrary"))),
    )(xc, xc, xc, dt, acs, tr, z, dskip, normw, *ride.args)
    return outs[:3], outs[3:]


def _ssd_bwd(xc, dtr, dt, acs, tr, z, ypre, sprev, dyn, dtb, alog, dskip, normw, *, name, rider=None):
    bsz, t, _ = xc.shape
    nc = t // CHUNK
    sp = _ssd_specs(bsz, nc, True)

    def body(xs_ref, b_ref, c_ref, dtr_ref, dt_ref, acs_ref, tr_ref, z_ref, y_ref, sp_ref, dyn_ref, dtb_ref, alog_ref,
             dsk_ref, nw_ref, dz_ref, dxs_ref, db_ref, dc_ref, ddt_ref, dnw_ref, dsm_ref, ds_ref):
        j = pl.program_id(1)

        @pl.when(j == 0)
        def _():
            ds_ref[...] = jnp.zeros_like(ds_ref)
            dnw_ref[...] = jnp.zeros_like(dnw_ref)
            dsm_ref[...] = jnp.zeros_like(dsm_ref)

        ex = range(bsz)
        heads = range(HPG)
        units = [(e, r) for e in ex for r in heads]
        q = [_ssd_decay(dt_ref[e], acs_ref[e], tr_ref[e, 0, 0]) for e in ex]
        a = -jnp.exp(alog_ref[0])
        valid = _dt_valid(nc - 1 - j)
        lane, row = q[0]["lane"], q[0]["row"]
        lane1 = lane[0:1, :]
        nw = nw_ref[...]
        y, zz, dyn = [y_ref[e] for e in ex], [z_ref[e] for e in ex], [dyn_ref[e] for e in ex]
        sz = [_sigmoid(zz[e]) for e in ex]
        sil = [zz[e] * sz[e] for e in ex]
        yg = [y[e] * sil[e] for e in ex]
        rstd = [lax.rsqrt(jnp.mean(yg[e] * yg[e], axis=-1, keepdims=True) + EPS) for e in ex]
        gn = [dyn[e] * nw for e in ex]
        dyg = [rstd[e] * (gn[e] - yg[e] * (rstd[e] * rstd[e]) * jnp.mean(gn[e] * yg[e], axis=-1, keepdims=True))
               for e in ex]
        dy = [dyg[e] * sil[e] for e in ex]
        xs = [xs_ref[e] for e in ex]
        for e in ex:
            dnw_ref[e] += jnp.sum(dyn[e] * yg[e] * rstd[e], axis=0, keepdims=True)
            dz_ref[e] = (dyg[e] * y[e] * (sz[e] * (1.0 + zz[e] * (1.0 - sz[e])))).astype(BF16)
        dskip_cols = [jnp.sum(dy[e] * xs[e], axis=0, keepdims=True) for e in ex]

        bm, cm = [b_ref[e] for e in ex], [c_ref[e] for e in ex]
        cb = [_dot_nt(cm[e], bm[e]) for e in ex]
        zero = jnp.zeros((CHUNK, LANES), F32)
        full = lambda v: jnp.broadcast_to(v, (CHUNK, LANES))
        low = lane < HEAD_DIM
        half = [low if r % 2 == 0 else ~low for r in heads]
        sl = lambda v, r: v[:, (r // 2) * LANES:(r // 2 + 1) * LANES]
        pair = lambda r: pl.ds((r // 2) * LANES, LANES)
        col = {(e, r): full(q[e]["acs"][:, r:r + 1]) for e, r in units}
        dt_col = {(e, r): full(q[e]["dt"][:, r:r + 1]) for e, r in units}
        aend = {(e, r): q[e]["aend"][:, r:r + 1] for e, r in units}
        dt_row = {(e, r): q[e]["dt_t"][r:r + 1, :] for e, r in units}
        decay = {(e, r): jnp.exp(jnp.where(q[e]["causal"], col[e, r] - q[e]["acs_t"][r:r + 1, :], -jnp.inf))
                 for e, r in units}
        ea = {u: jnp.exp(col[u]) for u in units}
        dte = {u: jnp.exp(aend[u] - col[u]) for u in units}
        ed = {u: jnp.exp(aend[u]) for u in units}
        k = {u: dte[u] * dt_col[u] for u in units}
        mp = {(e, r): cb[e] * decay[e, r] * dt_row[e, r] for e, r in units}
        xp = {(e, r): sl(xs[e], r) for e, r in units}
        dym = {(e, r): jnp.where(half[r], sl(dy[e], r), 0.0) for e, r in units}
        s_old = {(e, r): sp_ref[e, 0, :, pair(r)] for e, r in units}
        ds_old = {(e, r): ds_ref[e, :, pair(r)] for e, r in units}
        dsm = {(e, r): jnp.where(half[r], ds_old[e, r], 0.0) for e, r in units}
        gmat = {u: _dot_nt(dym[u], xp[u]) for u in units}
        t1 = {u: _dot_nt(dym[u], s_old[u]) for u in units}
        dbs = {u: _dot_nt(xp[u], dsm[u]) for u in units}
        dx = {(e, r): _dot_tn(mp[e, r], dym[e, r]) + _dot(bm[e] * k[e, r], dsm[e, r]) for e, r in units}
        ds = {(e, r): _dot_tn(cm[e] * ea[e, r], dym[e, r]) for e, r in units}
        gd = {u: gmat[u] * decay[u] for u in units}
        w0 = {(e, r): gd[e, r] * cb[e] for e, r in units}
        cs0 = {u: jnp.sum(w0[u], axis=0, keepdims=True) for u in units}
        rs = {u: jnp.sum(w0[u] * dt_row[u], axis=1, keepdims=True) for u in units}
        qv = {(e, r): jnp.sum(cm[e] * t1[e, r], axis=1, keepdims=True) for e, r in units}
        dk = {(e, r): jnp.sum(bm[e] * dbs[e, r], axis=1, keepdims=True) for e, r in units}
        ddte = {u: dk[u] * dt_col[u] for u in units}
        d_aend = {u: _sum_all(dsm[u] * s_old[u]) * ed[u] + _sum_all(ddte[u][:, 0:1] * dte[u][:, 0:1]) for u in units}
        last_row = row == CHUNK - 1
        dacs_col = {u: rs[u] + qv[u] * ea[u] - ddte[u] * dte[u] + jnp.where(last_row, d_aend[u], 0.0) for u in units}
        triu = (lane >= row).astype(F32)
        for e in ex:
            dcb, dc_acc, db_acc = zero, zero, zero
            dacs, dacs_t, ddt, ddt_t = zero, zero, zero, zero
            dskip_row = jnp.zeros((1, LANES), F32)
            for r in heads:
                u = (e, r)
                dcb = dcb + gd[u] * dt_row[u]
                dc_acc = dc_acc + ea[u] * t1[u]
                db_acc = db_acc + k[u] * dbs[u]
                dacs = jnp.where(lane == r, dacs_col[u], dacs)
                ddt = jnp.where(lane == r, dk[u] * dte[u], ddt)
                dacs_t = jnp.where(row == r, -cs0[u] * dt_row[u], dacs_t)
                ddt_t = jnp.where(row == r, cs0[u], ddt_t)
                dsk = _sum_all(jnp.where(half[r][0:1, :], sl(dskip_cols[e], r), 0.0))
                dskip_row = dskip_row + jnp.where(lane1 == r, dsk, 0.0)
            for r in range(0, HPG, 2):
                dxs_ref[e, :, pair(r)] = (dx[e, r] + dx[e, r + 1] + sl(dy[e], r) * dsk_ref[:, pair(r)]).astype(BF16)
                ed_pair = jnp.where(lane1 < HEAD_DIM, ed[e, r], ed[e, r + 1])
                ds_ref[e, :, pair(r)] = ds[e, r] + ds[e, r + 1] + ed_pair * ds_old[e, r]
            dacs = dacs + dacs_t.T
            ddt = ddt + ddt_t.T
            dda = _dot_exact(triu, dacs)
            ddt = ddt + dda * a
            da = jnp.sum(dda * q[e]["dt"], axis=0, keepdims=True)
            draw = jnp.where(valid, ddt * _sigmoid(dtr_ref[e] + dtb_ref[0]), 0.0)
            ddt_ref[e] = draw.astype(BF16)
            dsm_ref[e, 0, 0:1, :] += dskip_row
            dsm_ref[e, 0, 1:2, :] += da * a
            dsm_ref[e, 0, 2:3, :] += jnp.sum(draw, axis=0, keepdims=True)
            dc_ref[e] = (dc_acc + _dot(dcb, bm[e])).astype(BF16)
            db_ref[e] = (db_acc + _dot_tn(dcb, cm[e])).astype(BF16)

    grp_out = pl.BlockSpec((bsz, CHUNK, D_STATE), lambda g, j: (0, nc - 1 - j, g))
    grid = (N_GROUPS, nc)
    ride = _Ride(rider, body, 15, 7, 1, grid)
    outs = pl.pallas_call(
        ride.body, name=name, grid=grid,
        in_specs=[sp["xs"], sp["bm"], sp["cm"], sp["lane_blk"], sp["lane_blk"], sp["lane_blk"], sp["tr"], sp["xs"],
                  sp["xs"], sp["state"], sp["xs"], sp["grp_const"], sp["grp_const"], sp["grp_vec"], sp["grp_vec"]]
        + ride.in_specs,
        out_specs=[sp["xs"], sp["xs"], grp_out, grp_out, sp["lane_blk"],
                   pl.BlockSpec((bsz, 1, GW), lambda g, j: (0, 0, g)),
                   pl.BlockSpec((bsz, 1, 8, LANES), lambda g, j: (0, g, 0, 0))] + ride.out_specs,
        out_shape=[jax.ShapeDtypeStruct((bsz, t, D_SSM), BF16), jax.ShapeDtypeStruct((bsz, t, D_SSM), BF16),
                   jax.ShapeDtypeStruct((bsz, t, N_GROUPS * D_STATE), BF16),
                   jax.ShapeDtypeStruct((bsz, t, N_GROUPS * D_STATE), BF16),
                   jax.ShapeDtypeStruct((bsz, t, D_DT), BF16), jax.ShapeDtypeStruct((bsz, 1, D_SSM), F32),
                   jax.ShapeDtypeStruct((bsz, N_GROUPS, 8, LANES), F32)] + ride.out_shape,
        scratch_shapes=[pltpu.VMEM((bsz, D_STATE, GW), F32)] + ride.scratch,
        compiler_params=_params(*ride.semantics(("parallel", "arbitrary"))),
    )(xc, xc, xc, dtr, dt, acs, tr, z, ypre, sprev, dyn, dtb, alog, dskip, normw, *ride.args)
    return outs[:7], outs[7:]


def _input_grad(dhn, h0, w, dres, seq, *, name):
    bsz, t, d = h0.shape
    nc = t // CHUNK

    def body(dy_ref, h_ref, w_ref, dres_ref, gx_ref, head_ref, dw_ref):
        j = pl.program_id(0)

        @pl.when(j == 0)
        def _():
            dw_ref[...] = jnp.zeros_like(dw_ref)

        for e in range(bsz):
            x, dyv = h_ref[e], dy_ref[e]
            r = lax.rsqrt(jnp.mean(x * x, axis=-1, keepdims=True) + EPS)
            g = dyv * w_ref[...]
            dx = r * (g - x * (r * r) * jnp.mean(g * x, axis=-1, keepdims=True)) + dres_ref[e]
            dw_ref[...] += jnp.sum(dyv * x * r, axis=0, keepdims=True)
            gx_ref[e] = dx

        @pl.when(j == 0)
        def _():
            head_ref[...] = gx_ref[...]

    row = pl.BlockSpec((bsz, CHUNK, d), lambda j: (0, j, 0))
    return pl.pallas_call(
        body, name=name, grid=(nc,),
        in_specs=[row, row, pl.BlockSpec((1, d), lambda j: (0, 0)), row],
        out_specs=[pl.BlockSpec((bsz, CHUNK, d), lambda j: (0, jnp.maximum(j - 1, 0), 0)),
                   pl.BlockSpec((bsz, CHUNK, d), lambda j: (0, 0, 0)), pl.BlockSpec((1, d), lambda j: (0, 0))],
        out_shape=[jax.ShapeDtypeStruct((bsz, seq, d), F32), jax.ShapeDtypeStruct((bsz, CHUNK, d), F32),
                   jax.ShapeDtypeStruct((1, d), F32)],
        compiler_params=_params("arbitrary"),
    )(dhn, h0, w, dres)


def _remote(src, dst, send_sem, recv_sem, dev):
    return pltpu.make_async_remote_copy(src_ref=src, dst_ref=dst, send_sem=send_sem, recv_sem=recv_sem,
                                        device_id=dev, device_id_type=MESH)


def _position():
    return lax.axis_index("x"), lax.axis_index("y"), lax.axis_index("c")


def _other_chips(pos):
    x, y, _ = pos
    return [(1 - x, y), (x, 1 - y), (1 - x, 1 - y)]


class _Gather:
    def __init__(self, arrs):
        n = len(arrs)
        self.args, self.n_in, self.n_out = list(arrs), n, n
        self.split = [a.ndim == 2 and a.shape[1] % (2 * LANES) == 0 for a in arrs]
        self.out_shape = [jax.ShapeDtypeStruct((4,) + a.shape, a.dtype) for a in arrs]
        self.scratch = [pltpu.SemaphoreType.DMA((3 * n,)), pltpu.SemaphoreType.DMA((3 * n,)),
                        pltpu.SemaphoreType.DMA((n,)), pltpu.SemaphoreType.DMA((3 * n,)),
                        pltpu.SemaphoreType.DMA((3 * n,))]

    def _copies(self, pos, ins, outs, sems):
        send_sems, recv_sems, loc_sems, pass_send_sems, pass_recv_sems = sems
        x, y, c = pos
        me, sibling = 2 * x + y, (x, y, 1 - c)
        local = [pltpu.make_async_copy(ins[i], outs[i].at[me], loc_sems.at[i]) for i in range(self.n_in)]
        sends, recvs, passes, pass_recvs = [], [], [], []
        for i in range(self.n_in):
            half = self.args[i].shape[1] // 2 if self.split[i] else None
            for k, (px, py) in enumerate(_other_chips(pos)):
                them = 2 * px + py
                sems_k = (send_sems.at[3 * i + k], recv_sems.at[3 * i + k], (px, py, c))
                if half is None:
                    sends.append(_remote(ins[i], outs[i].at[me], *sems_k))
                    recvs.append(_remote(ins[i], outs[i].at[them], *sems_k))
                    passes.append(None)
                    continue
                mine = pl.ds(pl.multiple_of(c * half, LANES), half)
                other = pl.ds(pl.multiple_of((1 - c) * half, LANES), half)
                sends.append(_remote(ins[i].at[:, mine], outs[i].at[me, :, mine], *sems_k))
                recvs.append(_remote(ins[i].at[:, mine], outs[i].at[them, :, mine], *sems_k))
                pass_k = (pass_send_sems.at[3 * i + k], pass_recv_sems.at[3 * i + k], sibling)
                passes.append(_remote(outs[i].at[them, :, mine], outs[i].at[them, :, mine], *pass_k))
                pass_recvs.append(_remote(outs[i].at[them, :, other], outs[i].at[them, :, other], *pass_k))
        return local, sends, recvs, passes, pass_recvs

    def start(self, pos, ins, outs, sems):
        local, sends = self._copies(pos, ins, outs, sems)[:2]
        for cp in local + sends:
            cp.start()

    def relay(self, pos, ins, outs, sems):
        _, _, recvs, passes, _ = self._copies(pos, ins, outs, sems)
        for cp, onward in zip(recvs, passes):
            if onward is not None:
                cp.wait_recv()
                onward.start()

    def finish(self, pos, ins, outs, sems):
        local, sends, recvs, passes, pass_recvs = self._copies(pos, ins, outs, sems)
        for cp, onward in zip(recvs, passes):
            if onward is None:
                cp.wait_recv()
        for cp in pass_recvs:
            cp.wait_recv()
        for cp in sends + [p for p in passes if p is not None]:
            cp.wait_send()
        for cp in local:
            cp.wait()


class _Exchange:
    FLIPS = [(fx, fy, fc) for fx in (0, 1) for fy in (0, 1) for fc in (0, 1)][1:]

    def __init__(self, big, small=None):
        n = len(big)
        self.n_big, self.has_small = n, small is not None
        self.args = list(big) + ([small] if self.has_small else [])
        self.n_in = self.n_out = len(self.args)
        self.out_shape = [jax.ShapeDtypeStruct(a.shape, a.dtype) for a in big]
        self.scratch = [pltpu.SemaphoreType.DMA((max(3 * n, 1),)), pltpu.SemaphoreType.DMA((max(3 * n, 1),)),
                        pltpu.SemaphoreType.DMA((n + 1,))]
        if self.has_small:
            self.out_shape.append(jax.ShapeDtypeStruct((8,) + small.shape, small.dtype))
            self.scratch += [pltpu.SemaphoreType.DMA((7,)), pltpu.SemaphoreType.DMA((7,))]

    def _copies(self, pos, ins, outs, sems):
        x, y, c = pos
        me, me8 = 2 * x + y, 4 * x + 2 * y + c
        local, sends, recvs = [], [], []
        for i in range(self.n_big):
            local.append(pltpu.make_async_copy(ins[i].at[me], outs[i].at[me], sems[2].at[i]))
            for k, (px, py) in enumerate(_other_chips(pos)):
                sems_k = (sems[0].at[3 * i + k], sems[1].at[3 * i + k], (px, py, c))
                sends.append(_remote(ins[i].at[2 * px + py], outs[i].at[me], *sems_k))
                recvs.append(_remote(ins[i].at[me], outs[i].at[2 * px + py], *sems_k))
        if self.has_small:
            small, landed = ins[self.n_big], outs[self.n_big]
            local.append(pltpu.make_async_copy(small, landed.at[me8], sems[2].at[self.n_big]))
            for k, (fx, fy, fc) in enumerate(self.FLIPS):
                peer = (x ^ fx, y ^ fy, c ^ fc)
                sems_k = (sems[3].at[k], sems[4].at[k], peer)
                sends.append(_remote(small, landed.at[me8], *sems_k))
                recvs.append(_remote(small, landed.at[4 * peer[0] + 2 * peer[1] + peer[2]], *sems_k))
        return local, sends, recvs, [None] * len(recvs), []

    start = _Gather.start
    relay = _Gather.relay
    finish = _Gather.finish


class _Swap:
    def __init__(self, arrs):
        n = len(arrs)
        self.args, self.n_in, self.n_out = list(arrs), n, n
        self.out_shape = [jax.ShapeDtypeStruct(a.shape, a.dtype) for a in arrs]
        self.scratch = [pltpu.SemaphoreType.DMA((n,)), pltpu.SemaphoreType.DMA((n,))]

    def _copies(self, pos, ins, outs, sems):
        x, y, c = pos
        both = [_remote(ins[i], outs[i], sems[0].at[i], sems[1].at[i], (x, y, 1 - c)) for i in range(self.n_in)]
        return [], both, both, [None] * len(both), []

    start = _Gather.start
    relay = _Gather.relay
    finish = _Gather.finish


def _comm(rider, *, name):
    a, b = rider.n_in, rider.n_in + rider.n_out

    def body(*refs):
        pos = _position()
        rider.start(pos, refs[:a], refs[a:b], refs[b:])
        rider.relay(pos, refs[:a], refs[a:b], refs[b:])
        rider.finish(pos, refs[:a], refs[a:b], refs[b:])

    return pl.pallas_call(body, name=name, in_specs=[ANY] * rider.n_in, out_specs=[ANY] * rider.n_out,
                          out_shape=rider.out_shape, scratch_shapes=rider.scratch)(*rider.args)


class _Ride:
    RELAY_AT = 0.8

    def __init__(self, rider, body, n_in, n_out, n_scratch, grid):
        self.rider = rider
        self.args = rider.args if rider else []
        self.in_specs = [ANY] * rider.n_in if rider else []
        self.out_specs = [ANY] * rider.n_out if rider else []
        self.out_shape = rider.out_shape if rider else []
        self.scratch = rider.scratch if rider else []
        self.body = self._wrap(body, n_in, n_out, n_scratch, grid) if rider else body

    def semantics(self, sem):
        return ("arbitrary",) * len(sem) if self.rider else sem

    def _wrap(self, body, n_in, n_out, n_scratch, grid):
        rider = self.rider
        a = n_in
        b = a + rider.n_in
        c = b + n_out
        d = c + rider.n_out
        e = d + n_scratch

        def wrapped(*refs):
            pos = _position()
            ids = [pl.program_id(i) for i in range(len(grid))]
            step, total = 0, 1
            for i, g in zip(ids, grid):
                step, total = step * g + i, total * g

            @pl.when(step == 0)
            def _():
                rider.start(pos, refs[a:b], refs[c:d], refs[e:])

            body(*refs[:a], *refs[b:c], *refs[d:e])

            @pl.when(step == int(self.RELAY_AT * (total - 1)))
            def _():
                rider.relay(pos, refs[a:b], refs[c:d], refs[e:])

            @pl.when(step == total - 1)
            def _():
                rider.finish(pos, refs[a:b], refs[c:d], refs[e:])

        return wrapped


def _elementwise_tiles(r, c):
    if r % 8 == 0 and r * c > 65536:
        tm = _pick(r, (256, 128, 64, 16, 8))
        return (tm, c), r // tm, lambda i: (i, 0)
    if r % 8 and c % 256 == 0 and r * c > 65536:
        return (r, 256), c // 256, lambda i: (0, i)
    return (r, c), 1, lambda i: (0, 0)


def _chip_sum(landed, *, name):
    _, r, c = landed.shape
    blk, steps, at = _elementwise_tiles(r, c)

    def body(land_ref, o_ref):
        acc = land_ref[0].astype(F32)
        for jchip in range(1, 4):
            acc = acc + land_ref[jchip].astype(F32)
        o_ref[...] = acc

    return pl.pallas_call(
        body, name=name, grid=(steps,), in_specs=[pl.BlockSpec((4,) + blk, lambda i: (0,) + at(i))],
        out_specs=pl.BlockSpec(blk, at), out_shape=jax.ShapeDtypeStruct((r, c), F32),
        compiler_params=_params("parallel"),
    )(landed)


def _device_sum(parts, *, name):
    _, r, c = parts.shape

    def body(p_ref, o_ref):
        acc = p_ref[0]
        for d in range(1, 8):
            acc = acc + p_ref[d]
        o_ref[...] = acc

    return pl.pallas_call(body, name=name, out_shape=jax.ShapeDtypeStruct((r, c), F32))(parts)


def _adamw_math(w, g, m, v):
    m = ADAM_B1 * m + (1.0 - ADAM_B1) * g
    v = ADAM_B2 * v + (1.0 - ADAM_B2) * (g * g)
    m_hat = m / (1.0 - ADAM_B1 ** ADAM_STEP)
    v_hat = v / (1.0 - ADAM_B2 ** ADAM_STEP)
    return -ADAM_LR * (m_hat / (jnp.sqrt(v_hat) + ADAM_EPS) + ADAM_WD * w), m, v


def _adamw(w, g_parts, m, v, *, name):
    r, c = w.shape
    shape, steps, at = _elementwise_tiles(r, c)
    n_g = len(g_parts)

    def body(*refs):
        w_ref, m_ref, v_ref = refs[n_g:n_g + 3]
        g_ref, d_ref, nm_ref, nv_ref = refs[n_g + 3:]
        g = refs[0][...]
        for p in refs[1:n_g]:
            g = g + p[...]
        g_ref[...] = g
        d_ref[...], nm_ref[...], nv_ref[...] = _adamw_math(w_ref[...], g, m_ref[...], v_ref[...])

    blk = pl.BlockSpec(shape, at)
    return pl.pallas_call(
        body, name=name, grid=(steps,), in_specs=[blk] * (n_g + 3), out_specs=[blk] * 4,
        out_shape=[jax.ShapeDtypeStruct((r, c), F32)] * 4, compiler_params=_params("parallel"),
    )(*g_parts, w, m, v)


def _pad_heads(v):
    return jnp.pad(v.reshape(N_GROUPS, 1, HPG), ((0, 0), (0, 0), (0, LANES - HPG)))


def _unpad_heads(v):
    return v[:, :HPG].reshape(1, N_HEADS)


_SMALL_EARLY = [("pool_w", (512, 128)), ("pool_scale", (1, 512)), ("conv_w", (4, D_XBC)), ("conv_b", (1, D_XBC)),
                ("dt_bias", (1, N_HEADS)), ("a_log", (1, N_HEADS)), ("d_skip", (1, N_HEADS)), ("ssm_norm_w", (1, D_SSM)),
                ("norm_ffn_w", (1, 1024)), ("norm_f_w", (1, 1024))]
_SMALL_LATE = [("norm_mix_w", (1, 1024)), ("meta", (N_META, 1024)), ("loss", (1, 1))]


def _pack_small(grads, layout):
    rows = []
    for nm, shape in layout:
        flat = grads[nm].reshape(-1)
        rows.append(jnp.pad(flat, (0, (-flat.size) % LANES)).reshape(-1, LANES))
    packed = jnp.concatenate(rows, axis=0)
    return jnp.pad(packed, ((0, (-packed.shape[0]) % 8), (0, 0)))


def _unpack_small(packed, layout):
    out, r0 = {}, 0
    for nm, shape in layout:
        size = shape[0] * shape[1]
        nrow = -(-size // LANES)
        out[nm] = packed[r0:r0 + nrow].reshape(-1)[:size].reshape(shape)
        r0 += nrow
    return out


def kernel(x, meta, norm_mix_w, w_in, pool_w, pool_scale, conv_w, conv_b, dt_bias, a_log, d_skip, ssm_norm_w, w_out, norm_ffn_w, w_ff1, w_ff2, norm_f_w, loss_target, m_meta, m_norm_mix_w, m_w_in, m_pool_w, m_pool_scale, m_conv_w, m_conv_b, m_dt_bias, m_a_log, m_d_skip, m_ssm_norm_w, m_w_out, m_norm_ffn_w, m_w_ff1, m_w_ff2, m_norm_f_w, v_meta, v_norm_mix_w, v_w_in, v_pool_w, v_pool_scale, v_conv_w, v_conv_b, v_dt_bias, v_a_log, v_d_skip, v_ssm_norm_w, v_w_out, v_norm_ffn_w, v_w_ff1, v_w_ff2, v_norm_f_w):
    bsz, seq, d = x.shape
    t = seq + CHUNK
    n = bsz * t
    chip = 2 * lax.axis_index("x") + lax.axis_index("y")
    d_in = w_in.shape[2] * 4

    g_conv, g_meta = _comm(_Gather([conv_w[0], meta]), name="gather_small")
    convw = g_conv.transpose(1, 0, 2).reshape(CONV_W, D_XBC)
    meta_full = g_meta.transpose(1, 0, 2).reshape(N_META, d)
    (h0, hn1), (g_in,) = _embed_norm(x, meta_full, norm_mix_w, name="embed_norm",
                                     rider=_Gather([w_in[0].T.astype(BF16)]))
    h0f, hn1 = h0.reshape(n, d), hn1.reshape(n, d)
    late_weights = _Gather([w_out[0].astype(BF16), w_ff1[0].astype(BF16), w_ff2[0].astype(BF16)])
    win = g_in.reshape(d_in, d)
    wu, wz = win[:D_POOL], win[D_POOL:D_POOL + D_SSM]
    wx = win[D_POOL + D_SSM:D_POOL + D_SSM + D_XBC]
    wdt = jnp.pad(win[D_POOL + D_SSM + D_XBC:].reshape(N_GROUPS, HPG, d),
                  ((0, 0), (0, LANES - HPG), (0, 0))).reshape(D_DT, d)
    dtb, alog = _pad_heads(dt_bias), _pad_heads(a_log)
    dskip = jnp.repeat(d_skip, HEAD_DIM, axis=1)
    poolw = pool_w[0]

    u, z, dtr = _mm_fanout(hn1, [wu, wz, wdt], name="proj_uzdt")
    xbc, xc = _proj_conv(hn1, wx, convw, conv_b, name="proj_xbc")
    ypool = _pool_fwd(u.reshape(bsz, t, D_POOL), poolw, pool_scale, name="pool_fwd")
    xbc3 = xbc.reshape(bsz, t, D_XBC)
    xc = xc.reshape(bsz, t, D_XBC)
    z3, dtr3 = z.reshape(bsz, t, D_SSM), dtr.reshape(bsz, t, D_DT)
    dt3, acs3, tr3 = _ssd_prep(dtr3, dtb, alog, name="ssd_prep")
    (yn, ypre, sprev), (g_out, g_ff1, g_ff2) = _ssd_fwd(xc, dt3, acs3, tr3, z3, dskip, ssm_norm_w, name="ssd_fwd",
                                                        rider=late_weights)
    wo = g_out.reshape(D_POOL + D_SSM, d)
    wo_p, wo_s = wo[:D_POOL], wo[D_POOL:]
    w1 = g_ff1
    w2 = g_ff2.reshape(D_FF, d)
    ypool_f, yn_f = ypool.reshape(n, D_POOL), yn.reshape(n, D_SSM)
    add = lambda r, e: r + e
    h1, hn2 = _mm([ypool_f, yn_f], [wo_p, wo_s], name="out_proj", post=add, extras=(h0f,), norm_w=norm_ffn_w)
    act = _mm(hn2, w1, name="ff1", out_dtype=BF16)
    relu2 = lambda a: jnp.square(jnp.maximum(a, 0))
    h2 = _mm(act, w2, name="ff2", pre=relu2, post=add, extras=(h1,))
    dh2, dh2b, loss_acc, d_norm_f = _final_norm_loss(h2.reshape(bsz, t, d), loss_target, norm_f_w.reshape(1, d),
                                                     name="loss")

    dh2f, dh2bf = dh2.reshape(n, d), dh2b.reshape(n, d)
    dact = _mm(dh2bf, w2, name="ff2_bwd", nt=True, post=lambda r, a: r * (2.0 * jnp.maximum(a, 0).astype(F32)),
               extras=(act,), out_dtype=BF16)
    d_w2 = _mm_tn(act, dh2bf, name="ff2_dw", tk=2048, tn=1024, pre=relu2)
    d_w1 = _mm_tn(hn2, dact, name="ff1_dw", tk=1024, tn=2048, slab=D_FF // 4)
    dh1, dh1b, d_norm_ffn = _mm_rms_bwd(dact, w1, h1, norm_ffn_w, dh2f, name="ff1_bwd")
    dypool, dyn = _mm_fanout(dh1b, [wo_p, wo_s], name="out_proj_bwd")
    d_wo = _mm_tn_cat([ypool_f, yn_f], dh1b, name="out_proj_dw")
    big_late = [d_wo.reshape(4, (D_POOL + D_SSM) // 4, d),
                d_w1, d_w2.reshape(4, D_FF // 4, d)]
    (dz, dxs, dbm, dcm, ddtr, d_nw, d_heads), landed_late = _ssd_bwd(
        xc, dtr3, dt3, acs3, tr3, z3, ypre, sprev, dyn.reshape(bsz, t, D_SSM), dtb, alog, dskip, ssm_norm_w, name="ssd_bwd",
        rider=_Exchange(big_late))
    dxbc, d_convwb = _conv_bwd(xbc3, dxs, dbm, dcm, convw, conv_b, name="conv_bwd")
    du, d_poolw, d_poolsc = _pool_bwd(u.reshape(bsz, t, D_POOL), dypool.reshape(bsz, t, D_POOL), poolw, pool_scale,
                                      name="pool_bwd")
    duf, dzf, dxbcf, ddtrf = du.reshape(n, D_POOL), dz.reshape(n, D_SSM), dxbc.reshape(n, D_XBC), ddtr.reshape(n, D_DT)
    heads = jnp.sum(d_heads, axis=0)
    small_early = _pack_small({
        "pool_w": d_poolw, "pool_scale": d_poolsc,
        "conv_w": jnp.sum(d_convwb[:, :CONV_W], axis=0), "conv_b": jnp.sum(d_convwb[:, CONV_W:CONV_W + 1], axis=0),
        "dt_bias": _unpad_heads(heads[:, 2]), "a_log": _unpad_heads(heads[:, 1]), "d_skip": _unpad_heads(heads[:, 0]),
        "ssm_norm_w": jnp.sum(d_nw, axis=0), "norm_ffn_w": d_norm_ffn, "norm_f_w": d_norm_f}, _SMALL_EARLY)
    d_wuzdt = _mm_tn_cat([duf, dzf, ddtrf], hn1, name="proj_uzdt_dw")
    d_wx, (early_all,) = _mm_tn(dxbcf, hn1, name="proj_xbc_dw", tk=1280, tn=1024, rider=_Exchange([], small_early))
    d_wdt = d_wuzdt[D_POOL + D_SSM:].reshape(N_GROUPS, LANES, d)[:, :HPG].reshape(N_HEADS, d)
    d_win = jnp.concatenate([d_wuzdt[:D_POOL + D_SSM], d_wx, d_wdt], axis=0)
    big_in = d_win.reshape(4, d_in // 4, d)
    dhn1, (landed_in,) = _mm([duf, dzf, dxbcf, ddtrf], [wu, wz, wx, wdt], name="proj_bwd",
                             rider=_Exchange([big_in]))
    grad_x, d_head_rows, d_norm_mix = _input_grad(
        dhn1.reshape(bsz, t, d), h0, norm_mix_w, dh1.reshape(bsz, t, d), seq, name="input_grad")

    landed = [landed_in] + list(landed_late)
    small_late = _pack_small({"norm_mix_w": d_norm_mix, "meta": jnp.sum(d_head_rows[:, PAD:], axis=0),
                              "loss": loss_acc[0:1, 0:1]}, _SMALL_LATE)
    (late_all,) = _comm(_Exchange([], small_late), name="exchange_small")
    mine = [_chip_sum(l, name=f"chip_sum_{i}") for i, l in enumerate(landed)]
    theirs = _comm(_Swap(mine), name="swap_cores")
    gsmall = {**_unpack_small(_device_sum(early_all, name="device_sum_early"), _SMALL_EARLY),
              **_unpack_small(_device_sum(late_all, name="device_sum_late"), _SMALL_LATE)}
    gsmall["conv_w"] = lax.dynamic_slice_in_dim(gsmall["conv_w"], chip * (D_XBC // 4), D_XBC // 4, axis=1)
    gsmall["meta"] = lax.dynamic_slice_in_dim(gsmall["meta"], chip * (d // 4), d // 4, axis=1)
    loss = gsmall["loss"][0, 0]

    given = dict(meta=(meta, m_meta, v_meta), norm_mix_w=(norm_mix_w, m_norm_mix_w, v_norm_mix_w),
                 w_in=(w_in, m_w_in, v_w_in), pool_w=(pool_w, m_pool_w, v_pool_w),
                 pool_scale=(pool_scale, m_pool_scale, v_pool_scale), conv_w=(conv_w, m_conv_w, v_conv_w),
                 conv_b=(conv_b, m_conv_b, v_conv_b), dt_bias=(dt_bias, m_dt_bias, v_dt_bias),
                 a_log=(a_log, m_a_log, v_a_log), d_skip=(d_skip, m_d_skip, v_d_skip),
                 ssm_norm_w=(ssm_norm_w, m_ssm_norm_w, v_ssm_norm_w), w_out=(w_out, m_w_out, v_w_out),
                 norm_ffn_w=(norm_ffn_w, m_norm_ffn_w, v_norm_ffn_w), w_ff1=(w_ff1, m_w_ff1, v_w_ff1),
                 w_ff2=(w_ff2, m_w_ff2, v_w_ff2), norm_f_w=(norm_f_w, m_norm_f_w, v_norm_f_w))
    big_names = ["w_in", "w_out", "w_ff1", "w_ff2"]
    results = {}
    for nm, (w, m, v) in given.items():
        if nm in big_names:
            i = big_names.index(nm)
            parts, shape2 = (mine[i], theirs[i]), mine[i].shape
        else:
            parts, shape2 = (gsmall[nm],), gsmall[nm].shape
        if nm == "w_in":
            outs = _adamw(w[0].T, parts, m[0].T, v[0].T, name=f"adamw_{nm}")
            results[nm] = [o.T[None] for o in outs]
        else:
            outs = _adamw(w.reshape(shape2), parts, m.reshape(shape2), v.reshape(shape2), name=f"adamw_{nm}")
            results[nm] = [o.reshape(w.shape) for o in outs]
    order = list(given)
    return (loss, grad_x, *[results[nm][0] for nm in order], *[results[nm][1] for nm in order],
            *[results[nm][2] for nm in order], *[results[nm][3] for nm in order])
```

```python
import jax
import jax.numpy as jnp
from jax import lax
from jax.experimental import pallas as pl
from jax.experimental.pallas import tpu as pltpu

F32 = jnp.float32
BF16 = jnp.bfloat16
MESH = pl.DeviceIdType.MESH
ANY = pl.BlockSpec(memory_space=pl.ANY)

D_MODEL = 1024
N_META = 16
CHUNK = 128
PAD = CHUNK - N_META
POOL_WINDOWS = (2, 4, 8, 16)
D_POOL = 512
POOL_GROUP = 128
D_SSM = 1536
N_HEADS = 24
N_GROUPS = 4
HPG = 6
HEAD_DIM = 64
D_STATE = 128
GW = HPG * HEAD_DIM
D_XBC = D_SSM + 2 * N_GROUPS * D_STATE
D_DT = N_GROUPS * 128
D_FF = 4096
CONV_W = 4
EPS = 1e-5
LANES = 128
VMEM_LIMIT = 56 * 1024 * 1024

ADAM_LR, ADAM_B1, ADAM_B2, ADAM_EPS, ADAM_WD, ADAM_STEP = 0.001, 0.9, 0.999, 1e-08, 0.01, 10


def _params(*sem):
    return pltpu.CompilerParams(dimension_semantics=sem, vmem_limit_bytes=VMEM_LIMIT)


def _pick(n, cands):
    for c in cands:
        if n % c == 0:
            return c
    raise ValueError(f"no block size for {n}")


def _dot(a, b):
    return jnp.dot(a.astype(BF16), b.astype(BF16), preferred_element_type=F32)


def _dot_nt(a, b):
    return lax.dot_general(a.astype(BF16), b.astype(BF16), (((1,), (1,)), ((), ())), preferred_element_type=F32)


def _dot_tn(a, b):
    return lax.dot_general(a.astype(BF16), b.astype(BF16), (((0,), (0,)), ((), ())), preferred_element_type=F32)


def _dot_exact(mask, x):
    m = mask.astype(BF16)
    hi = x.astype(BF16)
    r1 = x - hi.astype(F32)
    mid = r1.astype(BF16)
    lo = (r1 - mid.astype(F32)).astype(BF16)
    dot = lambda t: jnp.dot(m, t, preferred_element_type=F32)
    return dot(hi) + dot(mid) + dot(lo)


def _sigmoid(x):
    return 1.0 / (1.0 + jnp.exp(-x))


def _softplus(x):
    return jnp.maximum(x, 0.0) + jnp.log1p(jnp.exp(-jnp.abs(x)))


def _sum_all(x):
    return jnp.sum(jnp.sum(x, axis=1, keepdims=True), axis=0, keepdims=True)


ROW_TILES = (1056, 768, 704, 512, 384, 256, 128)
TILE_BUDGET = 28 * 1024 * 1024


def _row_tile(n, bytes_per_row, fixed_bytes, budget=TILE_BUDGET):
    for tm in ROW_TILES:
        if n % tm == 0 and 2 * (tm * bytes_per_row + fixed_bytes) <= budget:
            return tm
    raise ValueError(f"no row tile for {n}")


WIDE_BUDGET = 38 * 1024 * 1024


def _mm(a, w, *, name, tn=512, nt=False, pre=None, post=None, extras=(), out_dtype=F32, norm_w=None, rider=None):
    assert norm_w is None or (rider is None and out_dtype == F32)
    a_list = list(a) if isinstance(a, (list, tuple)) else [a]
    w_list = list(w) if isinstance(w, (list, tuple)) else [w]
    n_a, n_ex = len(a_list), len(extras)
    n = a_list[0].shape[0]
    shard = w_list[0].shape[2] if w_list[0].ndim == 3 else None
    assert shard is None or (not nt and n_a == 1 and shard % tn == 0)
    m = w_list[0].shape[0] * shard if shard else w_list[0].shape[0] if nt else w_list[0].shape[1]
    tn = min(tn, m)
    size = lambda dt: jnp.dtype(dt).itemsize
    per_row = (sum(x.shape[1] * size(x.dtype) for x in a_list) + m * size(out_dtype)
               + sum(m * size(e.dtype) for e in extras) + (2 * m if norm_w is not None else 0))
    tm = _row_tile(n, per_row, sum(x.size * size(x.dtype) for x in w_list) // 2, WIDE_BUDGET)
    n_norm = 0 if norm_w is None else 1

    def body(*refs):
        a_refs, w_refs, ex_refs = refs[:n_a], refs[n_a:2 * n_a], refs[2 * n_a:2 * n_a + n_ex]
        o_ref = refs[2 * n_a + n_ex + n_norm]
        avs = [(a_ref[...] if pre is None else pre(a_ref[...])).astype(BF16) for a_ref in a_refs]
        for c0 in range(0, m, tn):
            r = None
            for av, w_ref in zip(avs, w_refs):
                if shard:
                    term = _dot(av, w_ref[c0 // shard, :, c0 % shard:c0 % shard + tn])
                else:
                    term = _dot_nt(av, w_ref[c0:c0 + tn, :]) if nt else _dot(av, w_ref[:, c0:c0 + tn])
                r = term if r is None else r + term
            if post is not None:
                r = post(r, *[e[:, c0:c0 + tn] for e in ex_refs])
            o_ref[:, c0:c0 + tn] = r.astype(out_dtype)
        if n_norm:
            x = o_ref[...]
            scale = lax.rsqrt(jnp.mean(x * x, axis=-1, keepdims=True) + EPS)
            refs[2 * n_a + n_ex + 2][...] = (x * scale * refs[2 * n_a + n_ex][...]).astype(BF16)

    a_specs = [pl.BlockSpec((tm, x.shape[1]), lambda i: (i, 0)) for x in a_list]
    w_specs = [pl.BlockSpec(x.shape, lambda i, nd=x.ndim: (0,) * nd, pipeline_mode=pl.Buffered(1)) for x in w_list]
    blk = pl.BlockSpec((tm, m), lambda i: (i, 0))
    vec = [pl.BlockSpec((1, m), lambda i: (0, 0))] * n_norm
    grid = (n // tm,)
    ride = _Ride(rider, body, 2 * n_a + n_ex + n_norm, 1 + n_norm, 0, grid)
    outs = pl.pallas_call(
        ride.body, name=name, grid=grid,
        in_specs=a_specs + w_specs + [blk] * n_ex + vec + ride.in_specs,
        out_specs=[blk] * (1 + n_norm) + ride.out_specs,
        out_shape=[jax.ShapeDtypeStruct((n, m), out_dtype)] + [jax.ShapeDtypeStruct((n, m), BF16)] * n_norm + ride.out_shape,
        scratch_shapes=ride.scratch, compiler_params=_params(*ride.semantics(("parallel",))),
    )(*a_list, *w_list, *extras, *([norm_w] * n_norm), *ride.args)
    if n_norm:
        return outs[0], outs[1]
    return (outs[0], outs[1:]) if rider else outs[0]


def _mm_fanout(a, ws, *, name, tn=512):
    n, k = a.shape
    ms = [w.shape[0] for w in ws]
    tm = _row_tile(n, k * 2 + 4 * sum(ms), sum(w.size for w in ws), WIDE_BUDGET)
    n_w = len(ws)

    def body(a_ref, *refs):
        av = a_ref[...]
        for w_ref, o_ref, m in zip(refs[:n_w], refs[n_w:], ms):
            for c0 in range(0, m, tn):
                o_ref[:, c0:c0 + tn] = _dot_nt(av, w_ref[c0:c0 + tn, :])

    return pl.pallas_call(
        body, name=name, grid=(n // tm,),
        in_specs=[pl.BlockSpec((tm, k), lambda i: (i, 0))]
        + [pl.BlockSpec(w.shape, lambda i: (0, 0), pipeline_mode=pl.Buffered(1)) for w in ws],
        out_specs=[pl.BlockSpec((tm, m), lambda i: (i, 0)) for m in ms],
        out_shape=[jax.ShapeDtypeStruct((n, m), F32) for m in ms],
        compiler_params=_params("parallel"),
    )(a, *ws)


def _mm_tn(a, g, *, name, tk, tn, pre=None, slab=None, rider=None):
    n, k = a.shape
    m = g.shape[1]
    tk, tn = min(tk, k), min(tn, m)
    tm = _row_tile(n, tk * jnp.dtype(a.dtype).itemsize + tn * jnp.dtype(g.dtype).itemsize, tk * tn * 4)
    steps = n // tm

    def body(a_ref, g_ref, o_ref, acc_ref):
        r = pl.program_id(2)

        @pl.when(r == 0)
        def _():
            acc_ref[...] = jnp.zeros_like(acc_ref)

        av = a_ref[...]
        if pre is not None:
            av = pre(av)
        if slab:
            for s in range(tn // slab):
                acc_ref[s] += _dot_tn(av, g_ref[:, s * slab:(s + 1) * slab])
        else:
            acc_ref[...] += _dot_tn(av, g_ref[...])

        @pl.when(r == steps - 1)
        def _():
            o_ref[...] = acc_ref[...].astype(BF16)

    if slab:
        block, out_spec = (tn // slab, tk, slab), pl.BlockSpec((tn // slab, tk, slab), lambda i, j, r: (j, i, 0))
        out_shape = jax.ShapeDtypeStruct((m // slab, k, slab), BF16)
    else:
        block, out_spec = (tk, tn), pl.BlockSpec((tk, tn), lambda i, j, r: (i, j))
        out_shape = jax.ShapeDtypeStruct((k, m), BF16)
    grid = (k // tk, m // tn, steps)
    ride = _Ride(rider, body, 2, 1, 1, grid)
    outs = pl.pallas_call(
        ride.body, name=name, grid=grid,
        in_specs=[pl.BlockSpec((tm, tk), lambda i, j, r: (r, i)), pl.BlockSpec((tm, tn), lambda i, j, r: (r, j))]
        + ride.in_specs,
        out_specs=[out_spec] + ride.out_specs, out_shape=[out_shape] + ride.out_shape,
        scratch_shapes=[pltpu.VMEM(block, F32)] + ride.scratch,
        compiler_params=_params(*ride.semantics(("parallel", "parallel", "arbitrary"))),
    )(a, g, *ride.args)
    return (outs[0], outs[1:]) if rider else outs[0]


def _mm_tn_cat(a_list, g, *, name):
    n, m = g.shape
    ks = [a.shape[1] for a in a_list]
    size = lambda x: jnp.dtype(x.dtype).itemsize
    tm = _row_tile(n, sum(a.shape[1] * size(a) for a in a_list) + m * size(g), sum(ks) * m * 4)
    steps, n_a = n // tm, len(a_list)

    def body(*refs):
        g_ref, o_ref, acc_ref = refs[n_a], refs[n_a + 1], refs[n_a + 2]
        r = pl.program_id(0)

        @pl.when(r == 0)
        def _():
            acc_ref[...] = jnp.zeros_like(acc_ref)

        gv, k0 = g_ref[...], 0
        for a_ref, k in zip(refs[:n_a], ks):
            acc_ref[k0:k0 + k, :] += _dot_tn(a_ref[...], gv)
            k0 += k

        @pl.when(r == steps - 1)
        def _():
            o_ref[...] = acc_ref[...].astype(BF16)

    return pl.pallas_call(
        body, name=name, grid=(steps,),
        in_specs=[pl.BlockSpec((tm, k), lambda r: (r, 0)) for k in ks] + [pl.BlockSpec((tm, m), lambda r: (r, 0))],
        out_specs=pl.BlockSpec((sum(ks), m), lambda r: (0, 0)),
        out_shape=jax.ShapeDtypeStruct((sum(ks), m), BF16),
        scratch_shapes=[pltpu.VMEM((sum(ks), m), F32)],
        compiler_params=_params("arbitrary"),
    )(*a_list, g)


def _mm_rms_bwd(a, w, h, w_norm, dres, *, name):
    n, k = a.shape
    d = h.shape[1]
    slabs, _, ks = w.shape
    tm = _row_tile(n, k * jnp.dtype(a.dtype).itemsize + d * (4 + 4 + 4 + 2), d * k, WIDE_BUDGET)

    def body(a_ref, w_ref, h_ref, wn_ref, dres_ref, dx_ref, dxb_ref, dw_ref):
        @pl.when(pl.program_id(0) == 0)
        def _():
            dw_ref[...] = jnp.zeros_like(dw_ref)

        dyv = None
        for s in range(slabs):
            part = _dot_nt(a_ref[:, s * ks:(s + 1) * ks], w_ref[s])
            dyv = part if dyv is None else dyv + part
        x = h_ref[...]
        r = lax.rsqrt(jnp.mean(x * x, axis=-1, keepdims=True) + EPS)
        g = dyv * wn_ref[...]
        dx = r * (g - x * (r * r) * jnp.mean(g * x, axis=-1, keepdims=True)) + dres_ref[...]
        dx_ref[...] = dx
        dxb_ref[...] = dx.astype(BF16)
        dw_ref[...] += jnp.sum(dyv * x * r, axis=0, keepdims=True)

    row = pl.BlockSpec((tm, d), lambda i: (i, 0))
    vec = pl.BlockSpec((1, d), lambda i: (0, 0))
    return pl.pallas_call(
        body, name=name, grid=(n // tm,),
        in_specs=[pl.BlockSpec((tm, k), lambda i: (i, 0)),
                  pl.BlockSpec(w.shape, lambda i: (0, 0, 0), pipeline_mode=pl.Buffered(1)), row, vec, row],
        out_specs=[row, row, vec],
        out_shape=[jax.ShapeDtypeStruct((n, d), F32), jax.ShapeDtypeStruct((n, d), BF16), jax.ShapeDtypeStruct((1, d), F32)],
        compiler_params=_params("arbitrary"),
    )(a, w, h, w_norm, dres)


def _embed_norm(x, meta, w, *, name, rider=None):
    bsz, seq, d = x.shape
    t = seq + CHUNK
    nc = t // CHUNK

    def body(x_ref, meta_ref, w_ref, h_ref, hn_ref):
        j = pl.program_id(0)
        first = jnp.concatenate([jnp.zeros((PAD, d), F32), meta_ref[...]], axis=0)
        for e in range(bsz):
            h = jnp.where(j == 0, first, x_ref[e])
            r = lax.rsqrt(jnp.mean(h * h, axis=-1, keepdims=True) + EPS)
            h_ref[e] = h
            hn_ref[e] = (h * r * w_ref[...]).astype(BF16)

    row = pl.BlockSpec((bsz, CHUNK, d), lambda j: (0, j, 0))
    grid = (nc,)
    ride = _Ride(rider, body, 3, 2, 0, grid)
    outs = pl.pallas_call(
        ride.body, name=name, grid=grid,
        in_specs=[pl.BlockSpec((bsz, CHUNK, d), lambda j: (0, jnp.maximum(j - 1, 0), 0)),
                  pl.BlockSpec((N_META, d), lambda j: (0, 0)), pl.BlockSpec((1, d), lambda j: (0, 0))] + ride.in_specs,
        out_specs=[row, row] + ride.out_specs,
        out_shape=[jax.ShapeDtypeStruct((bsz, t, d), F32), jax.ShapeDtypeStruct((bsz, t, d), BF16)] + ride.out_shape,
        scratch_shapes=ride.scratch, compiler_params=_params(*ride.semantics(("parallel",))),
    )(x, meta, w, *ride.args)
    return outs[:2], outs[2:]


def _final_norm_loss(h2, target, w, *, name):
    bsz, t, d = h2.shape
    nc = t // CHUNK

    def body(h_ref, t_ref, w_ref, dh_ref, dhb_ref, loss_ref, dw_ref):
        j = pl.program_id(0)

        @pl.when(j == 0)
        def _():
            loss_ref[...] = jnp.zeros_like(loss_ref)
            dw_ref[...] = jnp.zeros_like(dw_ref)

        wv = w_ref[...]
        for e in range(bsz):
            x = h_ref[e]
            r = lax.rsqrt(jnp.mean(x * x, axis=-1, keepdims=True) + EPS)
            diff = jnp.where(j > 0, x * r * wv - t_ref[e], 0.0)
            loss_ref[...] += _sum_all(diff * diff) * (0.5 / d)
            dy = diff * (1.0 / d)
            g = dy * wv
            dh = r * (g - x * (r * r) * jnp.mean(g * x, axis=-1, keepdims=True))
            dh_ref[e] = dh
            dhb_ref[e] = dh.astype(BF16)
            dw_ref[...] += jnp.sum(dy * x * r, axis=0, keepdims=True)

    row = pl.BlockSpec((bsz, CHUNK, d), lambda j: (0, j, 0))
    return pl.pallas_call(
        body, name=name, grid=(nc,),
        in_specs=[row, pl.BlockSpec((bsz, CHUNK, d), lambda j: (0, jnp.maximum(j - 1, 0), 0)),
                  pl.BlockSpec((1, d), lambda j: (0, 0))],
        out_specs=[row, row, pl.BlockSpec((8, LANES), lambda j: (0, 0)), pl.BlockSpec((1, d), lambda j: (0, 0))],
        out_shape=[jax.ShapeDtypeStruct((bsz, t, d), F32), jax.ShapeDtypeStruct((bsz, t, d), BF16),
                   jax.ShapeDtypeStruct((8, LANES), F32), jax.ShapeDtypeStruct((1, d), F32)],
        compiler_params=_params("arbitrary"),
    )(h2, target, w)


def _pool_masks(j, transposed):
    r = lax.broadcasted_iota(jnp.int32, (CHUNK, 2 * CHUNK), 0)
    c = lax.broadcasted_iota(jnp.int32, (CHUNK, 2 * CHUNK), 1)
    masks = []
    for w in POOL_WINDOWS:
        if transposed:
            m = (c >= r) & (c < r + w)
        else:
            s = c - CHUNK
            m = (s <= r) & (s > r - w) & (s + j * CHUNK >= 0)
        masks.append(m.astype(F32))
    return masks


def _pool_count(t_global, w):
    return jnp.clip(t_global - PAD + 1, 1, w).astype(F32)


def _pool_fwd(u, pool_w, pool_scale, *, name):
    bsz, t, _ = u.shape
    nc = t // CHUNK

    def body(prev_ref, cur_ref, pw_ref, sc_ref, o_ref):
        j = pl.program_id(0)
        masks = _pool_masks(j, False)
        tg = j * CHUNK + lax.broadcasted_iota(jnp.int32, (CHUNK, 1), 0)
        count = [_pool_count(tg, w) for w in POOL_WINDOWS]
        units = [(e, gi) for e in range(bsz) for gi in range(len(POOL_WINDOWS))]
        sl = lambda gi: pl.ds(gi * POOL_GROUP, POOL_GROUP)
        cur = {(e, gi): cur_ref[e, :, sl(gi)] for e, gi in units}
        both = {(e, gi): jnp.concatenate([prev_ref[e, :, sl(gi)], cur[e, gi]], axis=0) for e, gi in units}
        win = {(e, gi): _dot_exact(masks[gi], both[e, gi]) for e, gi in units}
        pooled = {(e, gi): win[e, gi] / count[gi] - cur[e, gi] for e, gi in units}
        mixed = {(e, gi): _dot(pooled[e, gi], pw_ref[gi]) for e, gi in units}
        for e, gi in units:
            o_ref[e, :, sl(gi)] = (mixed[e, gi] * sc_ref[:, sl(gi)]).astype(BF16)

    blk = lambda f: pl.BlockSpec((bsz, CHUNK, D_POOL), f)
    return pl.pallas_call(
        body, name=name, grid=(nc,),
        in_specs=[blk(lambda j: (0, jnp.maximum(j - 1, 0), 0)), blk(lambda j: (0, j, 0)),
                  pl.BlockSpec((4, POOL_GROUP, POOL_GROUP), lambda j: (0, 0, 0)),
                  pl.BlockSpec((1, D_POOL), lambda j: (0, 0))],
        out_specs=blk(lambda j: (0, j, 0)), out_shape=jax.ShapeDtypeStruct(u.shape, BF16),
        compiler_params=_params("parallel"),
    )(u, u, pool_w, pool_scale)


def _pool_bwd(u, dyp, pool_w, pool_scale, *, name):
    bsz, t, _ = u.shape
    nc = t // CHUNK

    def body(prev_ref, cur_ref, dy_ref, dyn_ref, pw_ref, sc_ref, du_ref, dpw_ref, dsc_ref):
        j = pl.program_id(0)

        @pl.when(j == 0)
        def _():
            dpw_ref[...] = jnp.zeros_like(dpw_ref)
            dsc_ref[...] = jnp.zeros_like(dsc_ref)

        fwd = _pool_masks(j, False)
        bwd = _pool_masks(j, True)
        tg = j * CHUNK + lax.broadcasted_iota(jnp.int32, (CHUNK, 1), 0)
        count = [_pool_count(tg, w) for w in POOL_WINDOWS]
        count_next = [_pool_count(tg + CHUNK, w) for w in POOL_WINDOWS]
        has_next = j < nc - 1
        groups = range(len(POOL_WINDOWS))
        units = [(e, gi) for e in range(bsz) for gi in groups]
        sl = lambda gi: pl.ds(gi * POOL_GROUP, POOL_GROUP)
        cur = {(e, gi): cur_ref[e, :, sl(gi)] for e, gi in units}
        both = {(e, gi): jnp.concatenate([prev_ref[e, :, sl(gi)], cur[e, gi]], axis=0) for e, gi in units}
        win = {(e, gi): _dot_exact(fwd[gi], both[e, gi]) for e, gi in units}
        pooled = {(e, gi): win[e, gi] / count[gi] - cur[e, gi] for e, gi in units}
        dy = {(e, gi): dy_ref[e, :, sl(gi)] for e, gi in units}
        mixed = {(e, gi): _dot(pooled[e, gi], pw_ref[gi]) for e, gi in units}
        dm = {(e, gi): dy[e, gi] * sc_ref[:, sl(gi)] for e, gi in units}
        dm_next = {(e, gi): jnp.where(has_next, dyn_ref[e, :, sl(gi)], 0.0) * sc_ref[:, sl(gi)] for e, gi in units}
        dpw = {(e, gi): _dot_tn(pooled[e, gi], dm[e, gi]) for e, gi in units}
        dpooled = {(e, gi): _dot_nt(dm[e, gi], pw_ref[gi]) for e, gi in units}
        dpooled_next = {(e, gi): _dot_nt(dm_next[e, gi], pw_ref[gi]) for e, gi in units}
        spread = {(e, gi): jnp.concatenate([dpooled[e, gi] / count[gi], dpooled_next[e, gi] / count_next[gi]], axis=0)
                  for e, gi in units}
        back = {(e, gi): _dot_exact(bwd[gi], spread[e, gi]) for e, gi in units}
        for e, gi in units:
            du_ref[e, :, sl(gi)] = (back[e, gi] - dpooled[e, gi]).astype(BF16)
        for gi in groups:
            dsc, dw = None, None
            for e in range(bsz):
                term = jnp.sum(dy[e, gi] * mixed[e, gi], axis=0, keepdims=True)
                dsc = term if dsc is None else dsc + term
                dw = dpw[e, gi] if dw is None else dw + dpw[e, gi]
            dsc_ref[:, sl(gi)] += dsc
            dpw_ref[gi] += dw

    blk = lambda f: pl.BlockSpec((bsz, CHUNK, D_POOL), f)
    return pl.pallas_call(
        body, name=name, grid=(nc,),
        in_specs=[blk(lambda j: (0, jnp.maximum(j - 1, 0), 0)), blk(lambda j: (0, j, 0)),
                  blk(lambda j: (0, j, 0)), blk(lambda j: (0, jnp.minimum(j + 1, nc - 1), 0)),
                  pl.BlockSpec((4, POOL_GROUP, POOL_GROUP), lambda j: (0, 0, 0)),
                  pl.BlockSpec((1, D_POOL), lambda j: (0, 0))],
        out_specs=[blk(lambda j: (0, j, 0)), pl.BlockSpec((4, POOL_GROUP, POOL_GROUP), lambda j: (0, 0, 0)),
                   pl.BlockSpec((1, D_POOL), lambda j: (0, 0))],
        out_shape=[jax.ShapeDtypeStruct(u.shape, BF16), jax.ShapeDtypeStruct((4, POOL_GROUP, POOL_GROUP), F32),
                   jax.ShapeDtypeStruct((1, D_POOL), F32)],
        compiler_params=_params("arbitrary"),
    )(u, u, dyp, dyp, pool_w, pool_scale)


CONV_SLAB = 512


def _conv_taps(tail, cur, keep_tail):
    ext = jnp.concatenate([jnp.where(keep_tail, tail, 0.0), cur], axis=0)
    return [(pltpu.roll(ext, CONV_W - 1 - k, 0) if k < CONV_W - 1 else ext)[8:] for k in range(CONV_W)]


def _conv_pre(taps, w_ref, b_ref, sl):
    acc = b_ref[:, sl]
    for k in range(CONV_W):
        acc = acc + w_ref[k:k + 1, sl] * taps[k]
    return acc


def _proj_conv(hn, w, conv_w, conv_b, *, name):
    n, d = hn.shape
    c = w.shape[0]
    assert PAD >= CONV_W - 1
    tm = _row_tile(n, d * 2 + c * (4 + 2), c * d, WIDE_BUDGET)

    def body(hn_ref, w_ref, cw_ref, cb_ref, xbc_ref, xc_ref, tail_ref):
        @pl.when(pl.program_id(0) == 0)
        def _():
            tail_ref[...] = jnp.zeros_like(tail_ref)

        av = hn_ref[...]
        starts = list(range(0, c, CONV_SLAB))

        def project(c0):
            xbc_ref[:, pl.ds(c0, CONV_SLAB)] = _dot_nt(av, w_ref[c0:c0 + CONV_SLAB, :])

        def convolve(c0):
            sl = pl.ds(c0, CONV_SLAB)
            xb = xbc_ref[:, sl]
            pre = _conv_pre(_conv_taps(tail_ref[:, sl], xb, True), cw_ref, cb_ref, sl)
            xc_ref[:, sl] = (pre * _sigmoid(pre)).astype(BF16)
            tail_ref[:, sl] = xb[tm - 8:, :]

        project(starts[0])
        for c0, c_next in zip(starts, starts[1:] + [None]):
            if c_next is not None:
                project(c_next)
            convolve(c0)

    row = lambda width: pl.BlockSpec((tm, width), lambda i: (i, 0))
    return pl.pallas_call(
        body, name=name, grid=(n // tm,),
        in_specs=[row(d), pl.BlockSpec(w.shape, lambda i: (0, 0), pipeline_mode=pl.Buffered(1)),
                  pl.BlockSpec((CONV_W, c), lambda i: (0, 0)), pl.BlockSpec((1, c), lambda i: (0, 0))],
        out_specs=[row(c), row(c)],
        out_shape=[jax.ShapeDtypeStruct((n, c), F32), jax.ShapeDtypeStruct((n, c), BF16)],
        scratch_shapes=[pltpu.VMEM((8, c), F32)],
        compiler_params=_params("arbitrary"),
    )(hn, w, conv_w, conv_b)


def _conv_bwd(xbc, dxs, db, dc, conv_w, conv_b, *, name):
    bsz, t, c = xbc.shape
    nc = t // CHUNK
    halo = 16
    rows = CHUNK + halo

    def body(tail_ref, cur_ref, head_ref, dxs_ref, db_ref, dc_ref, dxs_head, db_head, dc_head, w_ref, b_ref,
             dx_ref, dwb_ref):
        j = pl.program_id(1)

        @pl.when(j == 0)
        def _():
            dwb_ref[...] = jnp.zeros_like(dwb_ref)

        has_prev, has_next = j > 0, j < nc - 1
        for c0 in range(0, c, CONV_SLAB):
            sl = pl.ds(c0, CONV_SLAB)
            if c0 < D_SSM:
                dxc, dxc_next = dxs_ref[0, :, sl], dxs_head[0, :, sl]
            elif c0 < D_SSM + D_POOL:
                dxc, dxc_next = db_ref[0], db_head[0]
            else:
                dxc, dxc_next = dc_ref[0], dc_head[0]
            dxc = jnp.concatenate([dxc.astype(F32), jnp.where(has_next, dxc_next.astype(F32), 0.0)], axis=0)
            ext = jnp.concatenate([jnp.where(has_prev, tail_ref[0, :, sl], 0.0), cur_ref[0, :, sl],
                                   jnp.where(has_next, head_ref[0, :, sl], 0.0)], axis=0)
            taps = [(pltpu.roll(ext, CONV_W - 1 - k, 0) if k < CONV_W - 1 else ext)[8:] for k in range(CONV_W)]
            pre = _conv_pre(taps, w_ref, b_ref, sl)
            s = _sigmoid(pre)
            dpre = dxc * (s * (1.0 + pre * (1.0 - s)))
            acc = w_ref[CONV_W - 1:CONV_W, sl] * dpre[:CHUNK]
            for k in range(CONV_W - 1):
                up = CONV_W - 1 - k
                acc = acc + w_ref[k:k + 1, sl] * pltpu.roll(dpre, rows - up, 0)[:CHUNK]
            dx_ref[0, :, sl] = acc.astype(BF16)
            for k in range(CONV_W):
                dwb_ref[0, k:k + 1, sl] += jnp.sum(dpre[:CHUNK] * taps[k][:CHUNK], axis=0, keepdims=True)
            dwb_ref[0, CONV_W:CONV_W + 1, sl] += jnp.sum(dpre[:CHUNK], axis=0, keepdims=True)

    assert CONV_SLAB == D_POOL and D_SSM % CONV_SLAB == 0
    row = lambda width: pl.BlockSpec((1, CHUNK, width), lambda b, j: (b, j, 0))
    nxt = lambda width: pl.BlockSpec(
        (1, halo, width), lambda b, j: (b, jnp.minimum((j + 1) * (CHUNK // halo), t // halo - 1), 0))
    return pl.pallas_call(
        body, name=name, grid=(bsz, nc),
        in_specs=[pl.BlockSpec((1, 8, c), lambda b, j: (b, jnp.maximum(j * (CHUNK // 8) - 1, 0), 0)), row(c), nxt(c),
                  row(D_SSM), row(D_POOL), row(D_POOL), nxt(D_SSM), nxt(D_POOL), nxt(D_POOL),
                  pl.BlockSpec((CONV_W, c), lambda b, j: (0, 0)), pl.BlockSpec((1, c), lambda b, j: (0, 0))],
        out_specs=[row(c), pl.BlockSpec((1, 8, c), lambda b, j: (b, 0, 0))],
        out_shape=[jax.ShapeDtypeStruct(xbc.shape, BF16), jax.ShapeDtypeStruct((bsz, 8, c), F32)],
        compiler_params=_params("parallel", "arbitrary"),
    )(xbc, xbc, xbc, dxs, db, dc, dxs, db, dc, conv_w, conv_b)


def _dt_valid(j):
    lane = lax.broadcasted_iota(jnp.int32, (CHUNK, LANES), 1)
    row = lax.broadcasted_iota(jnp.int32, (CHUNK, LANES), 0)
    return (lane < HPG) & ((j > 0) | (row >= PAD))


def _proj_uz_dt(hn, wu, wz, wdt, dtb, alog, nc, *, name):
    n, d = hn.shape
    tm = _pick(n, (768, 384, 128))
    per_tile = tm // CHUNK
    widths = (wu.shape[0], wz.shape[0], wdt.shape[0])

    def body(hn_ref, wu_ref, wz_ref, wdt_ref, dtb_ref, alog_ref, u_ref, z_ref, dtr_ref, dt_ref, acs_ref, tr_ref):
        i = pl.program_id(0)
        av = hn_ref[...]
        for w_ref, o_ref, m in zip((wu_ref, wz_ref, wdt_ref), (u_ref, z_ref, dtr_ref), widths):
            for c0 in range(0, m, 512):
                o_ref[:, c0:c0 + 512] = _dot_nt(av, w_ref[c0:c0 + 512, :])
        row = lax.broadcasted_iota(jnp.int32, (CHUNK, LANES), 0)
        lane = lax.broadcasted_iota(jnp.int32, (CHUNK, LANES), 1)
        tril = (row >= lane).astype(F32)
        units = [(cc, g) for cc in range(per_tile) for g in range(N_GROUPS)]
        at = lambda cc, g: (pl.ds(cc * CHUNK, CHUNK), pl.ds(g * LANES, LANES))
        valid = [(lane < HPG) & (((i * per_tile + cc) % nc > 0) | (row >= PAD)) for cc in range(per_tile)]
        dt = {(cc, g): jnp.where(valid[cc], _softplus(dtr_ref[at(cc, g)] + dtb_ref[g]), 0.0) for cc, g in units}
        acs = {(cc, g): _dot_exact(tril, dt[cc, g] * -jnp.exp(alog_ref[g])) for cc, g in units}
        for cc, g in units:
            dt_ref[at(cc, g)] = dt[cc, g]
            acs_ref[at(cc, g)] = acs[cc, g]
            tr_ref[cc, g, 0:8, :] = dt[cc, g].T[0:8]
            tr_ref[cc, g, 8:16, :] = acs[cc, g].T[0:8]

    row_blk = lambda width: pl.BlockSpec((tm, width), lambda i: (i, 0))
    whole = lambda w: pl.BlockSpec(w.shape, lambda i: (0, 0), pipeline_mode=pl.Buffered(1))
    const = pl.BlockSpec((N_GROUPS, 1, LANES), lambda i: (0, 0, 0))
    return pl.pallas_call(
        body, name=name, grid=(n // tm,),
        in_specs=[row_blk(d), whole(wu), whole(wz), whole(wdt), const, const],
        out_specs=[row_blk(widths[0]), row_blk(widths[1])] + [row_blk(D_DT)] * 3
        + [pl.BlockSpec((per_tile, N_GROUPS, 16, LANES), lambda i: (i, 0, 0, 0))],
        out_shape=[jax.ShapeDtypeStruct((n, widths[0]), F32), jax.ShapeDtypeStruct((n, widths[1]), F32)]
        + [jax.ShapeDtypeStruct((n, D_DT), F32)] * 3 + [jax.ShapeDtypeStruct((n // CHUNK, N_GROUPS, 16, LANES), F32)],
        compiler_params=_params("parallel"),
    )(hn, wu, wz, wdt, dtb, alog)


def _ssd_decay(dt, acs, tr):
    lane = lax.broadcasted_iota(jnp.int32, (CHUNK, LANES), 1)
    row = lax.broadcasted_iota(jnp.int32, (CHUNK, LANES), 0)
    return dict(lane=lane, row=row, dt=dt, causal=row >= lane, acs=acs, acs_t=tr[8:16], dt_t=tr[0:8],
                aend=acs[CHUNK - 1:CHUNK, :])


def _ssd_specs(bsz, nc, rev):
    ch = (lambda j: nc - 1 - j) if rev else (lambda j: j)
    return dict(
        xs=pl.BlockSpec((bsz, CHUNK, GW), lambda g, j: (0, ch(j), g)),
        bm=pl.BlockSpec((bsz, CHUNK, D_STATE), lambda g, j: (0, ch(j), D_SSM // D_STATE + g)),
        cm=pl.BlockSpec((bsz, CHUNK, D_STATE), lambda g, j: (0, ch(j), D_SSM // D_STATE + N_GROUPS + g)),
        lane_blk=pl.BlockSpec((bsz, CHUNK, LANES), lambda g, j: (0, ch(j), g)),
        grp_const=pl.BlockSpec((1, 1, LANES), lambda g, j: (g, 0, 0)),
        grp_vec=pl.BlockSpec((1, GW), lambda g, j: (0, g)),
        state=pl.BlockSpec((bsz, 1, D_STATE, GW), lambda g, j: (0, ch(j), 0, g)),
        tr=pl.BlockSpec((bsz, 1, 1, 16, LANES), lambda g, j: (0, ch(j), g, 0, 0)),
    )


def _ssd_fwd(xc, dt, acs, tr, z, dskip, normw, *, name, rider=None):
    bsz, t, _ = xc.shape
    nc = t // CHUNK
    sp = _ssd_specs(bsz, nc, False)

    def body(xs_ref, b_ref, c_ref, dt_ref, acs_ref, tr_ref, z_ref, dsk_ref, nw_ref, yn_ref, y_ref, sp_ref, s_ref):
        j = pl.program_id(1)

        @pl.when(j == 0)
        def _():
            s_ref[...] = jnp.zeros_like(s_ref)

        ex = range(bsz)
        units = [(e, r) for e in ex for r in range(HPG)]
        full = lambda v: jnp.broadcast_to(v, (CHUNK, LANES))
        pair = lambda r: pl.ds((r // 2) * LANES, LANES)
        q = [_ssd_decay(dt_ref[e], acs_ref[e], tr_ref[e, 0, 0]) for e in ex]
        for e in ex:
            sp_ref[e, 0] = s_ref[e]
        bm, cm = [b_ref[e] for e in ex], [c_ref[e] for e in ex]
        cb = [_dot_nt(cm[e], bm[e]) for e in ex]
        low = q[0]["lane"] < HEAD_DIM
        col = {(e, r): full(q[e]["acs"][:, r:r + 1]) for e, r in units}
        aend = {(e, r): q[e]["aend"][:, r:r + 1] for e, r in units}
        decay = {(e, r): jnp.exp(jnp.where(q[e]["causal"], col[e, r] - q[e]["acs_t"][r:r + 1, :], -jnp.inf))
                 for e, r in units}
        mp = {(e, r): cb[e] * decay[e, r] * q[e]["dt_t"][r:r + 1, :] for e, r in units}
        ce = {(e, r): cm[e] * jnp.exp(col[e, r]) for e, r in units}
        bk = {(e, r): bm[e] * (jnp.exp(aend[e, r] - col[e, r]) * full(q[e]["dt"][:, r:r + 1])) for e, r in units}
        xp = {(e, r): xs_ref[e, :, pair(r)] for e, r in units}
        s_old = {(e, r): s_ref[e, :, pair(r)] for e, r in units}
        y_h = {u: _dot(mp[u], xp[u]) + _dot(ce[u], s_old[u]) for u in units}
        s_h = {u: jnp.exp(aend[u]) * s_old[u] + _dot_tn(bk[u], xp[u]) for u in units}
        for e in ex:
            for r in range(0, HPG, 2):
                y_ref[e, :, pair(r)] = jnp.where(low, y_h[e, r], y_h[e, r + 1])
                s_ref[e, :, pair(r)] = jnp.where(low, s_h[e, r], s_h[e, r + 1])
        y = [y_ref[e] + dsk_ref[...] * xs_ref[e] for e in ex]
        zz = [z_ref[e] for e in ex]
        yg = [y[e] * (zz[e] * _sigmoid(zz[e])) for e in ex]
        rstd = [lax.rsqrt(jnp.mean(yg[e] * yg[e], axis=-1, keepdims=True) + EPS) for e in ex]
        for e in ex:
            y_ref[e] = y[e]
            yn_ref[e] = (yg[e] * rstd[e] * nw_ref[...]).astype(BF16)

    grid = (N_GROUPS, nc)
    ride = _Ride(rider, body, 9, 3, 1, grid)
    outs = pl.pallas_call(
        ride.body, name=name, grid=grid,
        in_specs=[sp["xs"], sp["bm"], sp["cm"], sp["lane_blk"], sp["lane_blk"], sp["tr"], sp["xs"],
                  sp["grp_vec"], sp["grp_vec"]] + ride.in_specs,
        out_specs=[sp["xs"], sp["xs"], sp["state"]] + ride.out_specs,
        out_shape=[jax.ShapeDtypeStruct((bsz, t, D_SSM), BF16), jax.ShapeDtypeStruct((bsz, t, D_SSM), F32),
                   jax.ShapeDtypeStruct((bsz, nc, D_STATE, D_SSM), F32)] + ride.out_shape,
        scratch_shapes=[pltpu.VMEM((bsz, D_STATE, GW), F32)] + ride.scratch,
        compiler_params=_params(*ride.semantics(("parallel", "arbitrary"))),
    )(xc, xc, xc, dt, acs, tr, z, dskip, normw, *ride.args)
    return outs[:3], outs[3:]


def _ssd_bwd(xc, dtr, dt, acs, tr, z, ypre, sprev, dyn, dtb, alog, dskip, normw, *, name, rider=None):
    bsz, t, _ = xc.shape
    nc = t // CHUNK
    sp = _ssd_specs(bsz, nc, True)

    def body(xs_ref, b_ref, c_ref, dtr_ref, dt_ref, acs_ref, tr_ref, z_ref, y_ref, sp_ref, dyn_ref, dtb_ref, alog_ref,
             dsk_ref, nw_ref, dz_ref, dxs_ref, db_ref, dc_ref, ddt_ref, dnw_ref, dsm_ref, ds_ref):
        j = pl.program_id(1)

        @pl.when(j == 0)
        def _():
            ds_ref[...] = jnp.zeros_like(ds_ref)
            dnw_ref[...] = jnp.zeros_like(dnw_ref)
            dsm_ref[...] = jnp.zeros_like(dsm_ref)

        ex = range(bsz)
        heads = range(HPG)
        units = [(e, r) for e in ex for r in heads]
        q = [_ssd_decay(dt_ref[e], acs_ref[e], tr_ref[e, 0, 0]) for e in ex]
        a = -jnp.exp(alog_ref[0])
        valid = _dt_valid(nc - 1 - j)
        lane, row = q[0]["lane"], q[0]["row"]
        lane1 = lane[0:1, :]
        nw = nw_ref[...]
        y, zz, dyn = [y_ref[e] for e in ex], [z_ref[e] for e in ex], [dyn_ref[e] for e in ex]
        sz = [_sigmoid(zz[e]) for e in ex]
        sil = [zz[e] * sz[e] for e in ex]
        yg = [y[e] * sil[e] for e in ex]
        rstd = [lax.rsqrt(jnp.mean(yg[e] * yg[e], axis=-1, keepdims=True) + EPS) for e in ex]
        gn = [dyn[e] * nw for e in ex]
        dyg = [rstd[e] * (gn[e] - yg[e] * (rstd[e] * rstd[e]) * jnp.mean(gn[e] * yg[e], axis=-1, keepdims=True))
               for e in ex]
        dy = [dyg[e] * sil[e] for e in ex]
        xs = [xs_ref[e] for e in ex]
        for e in ex:
            dnw_ref[e] += jnp.sum(dyn[e] * yg[e] * rstd[e], axis=0, keepdims=True)
            dz_ref[e] = (dyg[e] * y[e] * (sz[e] * (1.0 + zz[e] * (1.0 - sz[e])))).astype(BF16)
        dskip_cols = [jnp.sum(dy[e] * xs[e], axis=0, keepdims=True) for e in ex]

        bm, cm = [b_ref[e] for e in ex], [c_ref[e] for e in ex]
        cb = [_dot_nt(cm[e], bm[e]) for e in ex]
        zero = jnp.zeros((CHUNK, LANES), F32)
        full = lambda v: jnp.broadcast_to(v, (CHUNK, LANES))
        low = lane < HEAD_DIM
        half = [low if r % 2 == 0 else ~low for r in heads]
        sl = lambda v, r: v[:, (r // 2) * LANES:(r // 2 + 1) * LANES]
        pair = lambda r: pl.ds((r // 2) * LANES, LANES)
        col = {(e, r): full(q[e]["acs"][:, r:r + 1]) for e, r in units}
        dt_col = {(e, r): full(q[e]["dt"][:, r:r + 1]) for e, r in units}
        aend = {(e, r): q[e]["aend"][:, r:r + 1] for e, r in units}
        dt_row = {(e, r): q[e]["dt_t"][r:r + 1, :] for e, r in units}
        decay = {(e, r): jnp.exp(jnp.where(q[e]["causal"], col[e, r] - q[e]["acs_t"][r:r + 1, :], -jnp.inf))
                 for e, r in units}
        ea = {u: jnp.exp(col[u]) for u in units}
        dte = {u: jnp.exp(aend[u] - col[u]) for u in units}
        ed = {u: jnp.exp(aend[u]) for u in units}
        k = {u: dte[u] * dt_col[u] for u in units}
        mp = {(e, r): cb[e] * decay[e, r] * dt_row[e, r] for e, r in units}
        xp = {(e, r): sl(xs[e], r) for e, r in units}
        dym = {(e, r): jnp.where(half[r], sl(dy[e], r), 0.0) for e, r in units}
        s_old = {(e, r): sp_ref[e, 0, :, pair(r)] for e, r in units}
        ds_old = {(e, r): ds_ref[e, :, pair(r)] for e, r in units}
        dsm = {(e, r): jnp.where(half[r], ds_old[e, r], 0.0) for e, r in units}
        gmat = {u: _dot_nt(dym[u], xp[u]) for u in units}
        t1 = {u: _dot_nt(dym[u], s_old[u]) for u in units}
        dbs = {u: _dot_nt(xp[u], dsm[u]) for u in units}
        dx = {(e, r): _dot_tn(mp[e, r], dym[e, r]) + _dot(bm[e] * k[e, r], dsm[e, r]) for e, r in units}
        ds = {(e, r): _dot_tn(cm[e] * ea[e, r], dym[e, r]) for e, r in units}
        gd = {u: gmat[u] * decay[u] for u in units}
        w0 = {(e, r): gd[e, r] * cb[e] for e, r in units}
        cs0 = {u: jnp.sum(w0[u], axis=0, keepdims=True) for u in units}
        rs = {u: jnp.sum(w0[u] * dt_row[u], axis=1, keepdims=True) for u in units}
        qv = {(e, r): jnp.sum(cm[e] * t1[e, r], axis=1, keepdims=True) for e, r in units}
        dk = {(e, r): jnp.sum(bm[e] * dbs[e, r], axis=1, keepdims=True) for e, r in units}
        ddte = {u: dk[u] * dt_col[u] for u in units}
        d_aend = {u: _sum_all(dsm[u] * s_old[u]) * ed[u] + _sum_all(ddte[u][:, 0:1] * dte[u][:, 0:1]) for u in units}
        last_row = row == CHUNK - 1
        dacs_col = {u: rs[u] + qv[u] * ea[u] - ddte[u] * dte[u] + jnp.where(last_row, d_aend[u], 0.0) for u in units}
        triu = (lane >= row).astype(F32)
        for e in ex:
            dcb, dc_acc, db_acc = zero, zero, zero
            dacs, dacs_t, ddt, ddt_t = zero, zero, zero, zero
            dskip_row = jnp.zeros((1, LANES), F32)
            for r in heads:
                u = (e, r)
                dcb = dcb + gd[u] * dt_row[u]
                dc_acc = dc_acc + ea[u] * t1[u]
                db_acc = db_acc + k[u] * dbs[u]
                dacs = jnp.where(lane == r, dacs_col[u], dacs)
                ddt = jnp.where(lane == r, dk[u] * dte[u], ddt)
                dacs_t = jnp.where(row == r, -cs0[u] * dt_row[u], dacs_t)
                ddt_t = jnp.where(row == r, cs0[u], ddt_t)
                dsk = _sum_all(jnp.where(half[r][0:1, :], sl(dskip_cols[e], r), 0.0))
                dskip_row = dskip_row + jnp.where(lane1 == r, dsk, 0.0)
            for r in range(0, HPG, 2):
                dxs_ref[e, :, pair(r)] = (dx[e, r] + dx[e, r + 1] + sl(dy[e], r) * dsk_ref[:, pair(r)]).astype(BF16)
                ed_pair = jnp.where(lane1 < HEAD_DIM, ed[e, r], ed[e, r + 1])
                ds_ref[e, :, pair(r)] = ds[e, r] + ds[e, r + 1] + ed_pair * ds_old[e, r]
            dacs = dacs + dacs_t.T
            ddt = ddt + ddt_t.T
            dda = _dot_exact(triu, dacs)
            ddt = ddt + dda * a
            da = jnp.sum(dda * q[e]["dt"], axis=0, keepdims=True)
            draw = jnp.where(valid, ddt * _sigmoid(dtr_ref[e] + dtb_ref[0]), 0.0)
            ddt_ref[e] = draw.astype(BF16)
            dsm_ref[e, 0, 0:1, :] += dskip_row
            dsm_ref[e, 0, 1:2, :] += da * a
            dsm_ref[e, 0, 2:3, :] += jnp.sum(draw, axis=0, keepdims=True)
            dc_ref[e] = (dc_acc + _dot(dcb, bm[e])).astype(BF16)
            db_ref[e] = (db_acc + _dot_tn(dcb, cm[e])).astype(BF16)

    grp_out = pl.BlockSpec((bsz, CHUNK, D_STATE), lambda g, j: (0, nc - 1 - j, g))
    grid = (N_GROUPS, nc)
    ride = _Ride(rider, body, 15, 7, 1, grid)
    outs = pl.pallas_call(
        ride.body, name=name, grid=grid,
        in_specs=[sp["xs"], sp["bm"], sp["cm"], sp["lane_blk"], sp["lane_blk"], sp["lane_blk"], sp["tr"], sp["xs"],
                  sp["xs"], sp["state"], sp["xs"], sp["grp_const"], sp["grp_const"], sp["grp_vec"], sp["grp_vec"]]
        + ride.in_specs,
        out_specs=[sp["xs"], sp["xs"], grp_out, grp_out, sp["lane_blk"],
                   pl.BlockSpec((bsz, 1, GW), lambda g, j: (0, 0, g)),
                   pl.BlockSpec((bsz, 1, 8, LANES), lambda g, j: (0, g, 0, 0))] + ride.out_specs,
        out_shape=[jax.ShapeDtypeStruct((bsz, t, D_SSM), BF16), jax.ShapeDtypeStruct((bsz, t, D_SSM), BF16),
                   jax.ShapeDtypeStruct((bsz, t, N_GROUPS * D_STATE), BF16),
                   jax.ShapeDtypeStruct((bsz, t, N_GROUPS * D_STATE), BF16),
                   jax.ShapeDtypeStruct((bsz, t, D_DT), BF16), jax.ShapeDtypeStruct((bsz, 1, D_SSM), F32),
                   jax.ShapeDtypeStruct((bsz, N_GROUPS, 8, LANES), F32)] + ride.out_shape,
        scratch_shapes=[pltpu.VMEM((bsz, D_STATE, GW), F32)] + ride.scratch,
        compiler_params=_params(*ride.semantics(("parallel", "arbitrary"))),
    )(xc, xc, xc, dtr, dt, acs, tr, z, ypre, sprev, dyn, dtb, alog, dskip, normw, *ride.args)
    return outs[:7], outs[7:]


def _input_grad(dhn, h0, w, dres, seq, *, name):
    bsz, t, d = h0.shape
    nc = t // CHUNK

    def body(dy_ref, h_ref, w_ref, dres_ref, gx_ref, head_ref, dw_ref):
        j = pl.program_id(0)

        @pl.when(j == 0)
        def _():
            dw_ref[...] = jnp.zeros_like(dw_ref)

        for e in range(bsz):
            x, dyv = h_ref[e], dy_ref[e]
            r = lax.rsqrt(jnp.mean(x * x, axis=-1, keepdims=True) + EPS)
            g = dyv * w_ref[...]
            dx = r * (g - x * (r * r) * jnp.mean(g * x, axis=-1, keepdims=True)) + dres_ref[e]
            dw_ref[...] += jnp.sum(dyv * x * r, axis=0, keepdims=True)
            gx_ref[e] = dx

        @pl.when(j == 0)
        def _():
            head_ref[...] = gx_ref[...]

    row = pl.BlockSpec((bsz, CHUNK, d), lambda j: (0, j, 0))
    return pl.pallas_call(
        body, name=name, grid=(nc,),
        in_specs=[row, row, pl.BlockSpec((1, d), lambda j: (0, 0)), row],
        out_specs=[pl.BlockSpec((bsz, CHUNK, d), lambda j: (0, jnp.maximum(j - 1, 0), 0)),
                   pl.BlockSpec((bsz, CHUNK, d), lambda j: (0, 0, 0)), pl.BlockSpec((1, d), lambda j: (0, 0))],
        out_shape=[jax.ShapeDtypeStruct((bsz, seq, d), F32), jax.ShapeDtypeStruct((bsz, CHUNK, d), F32),
                   jax.ShapeDtypeStruct((1, d), F32)],
        compiler_params=_params("arbitrary"),
    )(dhn, h0, w, dres)


def _remote(src, dst, send_sem, recv_sem, dev):
    return pltpu.make_async_remote_copy(src_ref=src, dst_ref=dst, send_sem=send_sem, recv_sem=recv_sem,
                                        device_id=dev, device_id_type=MESH)


def _position():
    return lax.axis_index("x"), lax.axis_index("y"), lax.axis_index("c")


def _other_chips(pos):
    x, y, _ = pos
    return [(1 - x, y), (x, 1 - y), (1 - x, 1 - y)]


class _Gather:
    def __init__(self, arrs):
        n = len(arrs)
        self.args, self.n_in, self.n_out = list(arrs), n, n
        self.split = [a.ndim == 2 and a.shape[1] % (2 * LANES) == 0 for a in arrs]
        self.out_shape = [jax.ShapeDtypeStruct((4,) + a.shape, a.dtype) for a in arrs]
        self.scratch = [pltpu.SemaphoreType.DMA((3 * n,)), pltpu.SemaphoreType.DMA((3 * n,)),
                        pltpu.SemaphoreType.DMA((n,)), pltpu.SemaphoreType.DMA((3 * n,)),
                        pltpu.SemaphoreType.DMA((3 * n,))]

    def _copies(self, pos, ins, outs, sems):
        send_sems, recv_sems, loc_sems, pass_send_sems, pass_recv_sems = sems
        x, y, c = pos
        me, sibling = 2 * x + y, (x, y, 1 - c)
        local = [pltpu.make_async_copy(ins[i], outs[i].at[me], loc_sems.at[i]) for i in range(self.n_in)]
        sends, recvs, passes, pass_recvs = [], [], [], []
        for i in range(self.n_in):
            half = self.args[i].shape[1] // 2 if self.split[i] else None
            for k, (px, py) in enumerate(_other_chips(pos)):
                them = 2 * px + py
                sems_k = (send_sems.at[3 * i + k], recv_sems.at[3 * i + k], (px, py, c))
                if half is None:
                    sends.append(_remote(ins[i], outs[i].at[me], *sems_k))
                    recvs.append(_remote(ins[i], outs[i].at[them], *sems_k))
                    passes.append(None)
                    continue
                mine = pl.ds(pl.multiple_of(c * half, LANES), half)
                other = pl.ds(pl.multiple_of((1 - c) * half, LANES), half)
                sends.append(_remote(ins[i].at[:, mine], outs[i].at[me, :, mine], *sems_k))
                recvs.append(_remote(ins[i].at[:, mine], outs[i].at[them, :, mine], *sems_k))
                pass_k = (pass_send_sems.at[3 * i + k], pass_recv_sems.at[3 * i + k], sibling)
                passes.append(_remote(outs[i].at[them, :, mine], outs[i].at[them, :, mine], *pass_k))
                pass_recvs.append(_remote(outs[i].at[them, :, other], outs[i].at[them, :, other], *pass_k))
        return local, sends, recvs, passes, pass_recvs

    def start(self, pos, ins, outs, sems):
        local, sends = self._copies(pos, ins, outs, sems)[:2]
        for cp in local + sends:
            cp.start()

    def relay(self, pos, ins, outs, sems):
        _, _, recvs, passes, _ = self._copies(pos, ins, outs, sems)
        for cp, onward in zip(recvs, passes):
            if onward is not None:
                cp.wait_recv()
                onward.start()

    def finish(self, pos, ins, outs, sems):
        local, sends, recvs, passes, pass_recvs = self._copies(pos, ins, outs, sems)
        for cp, onward in zip(recvs, passes):
            if onward is None:
                cp.wait_recv()
        for cp in pass_recvs:
            cp.wait_recv()
        for cp in sends + [p for p in passes if p is not None]:
            cp.wait_send()
        for cp in local:
            cp.wait()


class _Exchange:
    FLIPS = [(fx, fy, fc) for fx in (0, 1) for fy in (0, 1) for fc in (0, 1)][1:]

    def __init__(self, big, small=None):
        n = len(big)
        self.n_big, self.has_small = n, small is not None
        self.args = list(big) + ([small] if self.has_small else [])
        self.n_in = self.n_out = len(self.args)
        self.out_shape = [jax.ShapeDtypeStruct(a.shape, a.dtype) for a in big]
        self.scratch = [pltpu.SemaphoreType.DMA((max(3 * n, 1),)), pltpu.SemaphoreType.DMA((max(3 * n, 1),)),
                        pltpu.SemaphoreType.DMA((n + 1,))]
        if self.has_small:
            self.out_shape.append(jax.ShapeDtypeStruct((8,) + small.shape, small.dtype))
            self.scratch += [pltpu.SemaphoreType.DMA((7,)), pltpu.SemaphoreType.DMA((7,))]

    def _copies(self, pos, ins, outs, sems):
        x, y, c = pos
        me, me8 = 2 * x + y, 4 * x + 2 * y + c
        local, sends, recvs = [], [], []
        for i in range(self.n_big):
            local.append(pltpu.make_async_copy(ins[i].at[me], outs[i].at[me], sems[2].at[i]))
            for k, (px, py) in enumerate(_other_chips(pos)):
                sems_k = (sems[0].at[3 * i + k], sems[1].at[3 * i + k], (px, py, c))
                sends.append(_remote(ins[i].at[2 * px + py], outs[i].at[me], *sems_k))
                recvs.append(_remote(ins[i].at[me], outs[i].at[2 * px + py], *sems_k))
        if self.has_small:
            small, landed = ins[self.n_big], outs[self.n_big]
            local.append(pltpu.make_async_copy(small, landed.at[me8], sems[2].at[self.n_big]))
            for k, (fx, fy, fc) in enumerate(self.FLIPS):
                peer = (x ^ fx, y ^ fy, c ^ fc)
                sems_k = (sems[3].at[k], sems[4].at[k], peer)
                sends.append(_remote(small, landed.at[me8], *sems_k))
                recvs.append(_remote(small, landed.at[4 * peer[0] + 2 * peer[1] + peer[2]], *sems_k))
        return local, sends, recvs, [None] * len(recvs), []

    start = _Gather.start
    relay = _Gather.relay
    finish = _Gather.finish


class _Swap:
    def __init__(self, arrs):
        n = len(arrs)
        self.args, self.n_in, self.n_out = list(arrs), n, n
        self.out_shape = [jax.ShapeDtypeStruct(a.shape, a.dtype) for a in arrs]
        self.scratch = [pltpu.SemaphoreType.DMA((n,)), pltpu.SemaphoreType.DMA((n,))]

    def _copies(self, pos, ins, outs, sems):
        x, y, c = pos
        both = [_remote(ins[i], outs[i], sems[0].at[i], sems[1].at[i], (x, y, 1 - c)) for i in range(self.n_in)]
        return [], both, both, [None] * len(both), []

    start = _Gather.start
    relay = _Gather.relay
    finish = _Gather.finish


def _comm(rider, *, name):
    a, b = rider.n_in, rider.n_in + rider.n_out

    def body(*refs):
        pos = _position()
        rider.start(pos, refs[:a], refs[a:b], refs[b:])
        rider.relay(pos, refs[:a], refs[a:b], refs[b:])
        rider.finish(pos, refs[:a], refs[a:b], refs[b:])

    return pl.pallas_call(body, name=name, in_specs=[ANY] * rider.n_in, out_specs=[ANY] * rider.n_out,
                          out_shape=rider.out_shape, scratch_shapes=rider.scratch)(*rider.args)


class _Ride:
    RELAY_AT = 0.8

    def __init__(self, rider, body, n_in, n_out, n_scratch, grid):
        self.rider = rider
        self.args = rider.args if rider else []
        self.in_specs = [ANY] * rider.n_in if rider else []
        self.out_specs = [ANY] * rider.n_out if rider else []
        self.out_shape = rider.out_shape if rider else []
        self.scratch = rider.scratch if rider else []
        self.body = self._wrap(body, n_in, n_out, n_scratch, grid) if rider else body

    def semantics(self, sem):
        return ("arbitrary",) * len(sem) if self.rider else sem

    def _wrap(self, body, n_in, n_out, n_scratch, grid):
        rider = self.rider
        a = n_in
        b = a + rider.n_in
        c = b + n_out
        d = c + rider.n_out
        e = d + n_scratch

        def wrapped(*refs):
            pos = _position()
            ids = [pl.program_id(i) for i in range(len(grid))]
            step, total = 0, 1
            for i, g in zip(ids, grid):
                step, total = step * g + i, total * g

            @pl.when(step == 0)
            def _():
                rider.start(pos, refs[a:b], refs[c:d], refs[e:])

            body(*refs[:a], *refs[b:c], *refs[d:e])

            @pl.when(step == int(self.RELAY_AT * (total - 1)))
            def _():
                rider.relay(pos, refs[a:b], refs[c:d], refs[e:])

            @pl.when(step == total - 1)
            def _():
                rider.finish(pos, refs[a:b], refs[c:d], refs[e:])

        return wrapped


def _elementwise_tiles(r, c):
    if r % 8 == 0 and r * c > 65536:
        tm = _pick(r, (256, 128, 64, 16, 8))
        return (tm, c), r // tm, lambda i: (i, 0)
    if r % 8 and c % 256 == 0 and r * c > 65536:
        return (r, 256), c // 256, lambda i: (0, i)
    return (r, c), 1, lambda i: (0, 0)


def _chip_sum(landed, *, name):
    _, r, c = landed.shape
    blk, steps, at = _elementwise_tiles(r, c)

    def body(land_ref, o_ref):
        acc = land_ref[0].astype(F32)
        for jchip in range(1, 4):
            acc = acc + land_ref[jchip].astype(F32)
        o_ref[...] = acc

    return pl.pallas_call(
        body, name=name, grid=(steps,), in_specs=[pl.BlockSpec((4,) + blk, lambda i: (0,) + at(i))],
        out_specs=pl.BlockSpec(blk, at), out_shape=jax.ShapeDtypeStruct((r, c), F32),
        compiler_params=_params("parallel"),
    )(landed)


def _device_sum(parts, *, name):
    _, r, c = parts.shape

    def body(p_ref, o_ref):
        acc = p_ref[0]
        for d in range(1, 8):
            acc = acc + p_ref[d]
        o_ref[...] = acc

    return pl.pallas_call(body, name=name, out_shape=jax.ShapeDtypeStruct((r, c), F32))(parts)


def _adamw_math(w, g, m, v):
    m = ADAM_B1 * m + (1.0 - ADAM_B1) * g
    v = ADAM_B2 * v + (1.0 - ADAM_B2) * (g * g)
    m_hat = m / (1.0 - ADAM_B1 ** ADAM_STEP)
    v_hat = v / (1.0 - ADAM_B2 ** ADAM_STEP)
    return -ADAM_LR * (m_hat / (jnp.sqrt(v_hat) + ADAM_EPS) + ADAM_WD * w), m, v


def _adamw(w, g_parts, m, v, *, name):
    r, c = w.shape
    shape, steps, at = _elementwise_tiles(r, c)
    n_g = len(g_parts)

    def body(*refs):
        w_ref, m_ref, v_ref = refs[n_g:n_g + 3]
        g_ref, d_ref, nm_ref, nv_ref = refs[n_g + 3:]
        g = refs[0][...]
        for p in refs[1:n_g]:
            g = g + p[...]
        g_ref[...] = g
        d_ref[...], nm_ref[...], nv_ref[...] = _adamw_math(w_ref[...], g, m_ref[...], v_ref[...])

    blk = pl.BlockSpec(shape, at)
    return pl.pallas_call(
        body, name=name, grid=(steps,), in_specs=[blk] * (n_g + 3), out_specs=[blk] * 4,
        out_shape=[jax.ShapeDtypeStruct((r, c), F32)] * 4, compiler_params=_params("parallel"),
    )(*g_parts, w, m, v)


def _pad_heads(v):
    return jnp.pad(v.reshape(N_GROUPS, 1, HPG), ((0, 0), (0, 0), (0, LANES - HPG)))


def _unpad_heads(v):
    return v[:, :HPG].reshape(1, N_HEADS)


_SMALL_EARLY = [("pool_w", (512, 128)), ("pool_scale", (1, 512)), ("conv_w", (4, D_XBC)), ("conv_b", (1, D_XBC)),
                ("dt_bias", (1, N_HEADS)), ("a_log", (1, N_HEADS)), ("d_skip", (1, N_HEADS)), ("ssm_norm_w", (1, D_SSM)),
                ("norm_ffn_w", (1, 1024)), ("norm_f_w", (1, 1024))]
_SMALL_LATE = [("norm_mix_w", (1, 1024)), ("meta", (N_META, 1024)), ("loss", (1, 1))]


def _pack_small(grads, layout):
    rows = []
    for nm, shape in layout:
        flat = grads[nm].reshape(-1)
        rows.append(jnp.pad(flat, (0, (-flat.size) % LANES)).reshape(-1, LANES))
    packed = jnp.concatenate(rows, axis=0)
    return jnp.pad(packed, ((0, (-packed.shape[0]) % 8), (0, 0)))


def _unpack_small(packed, layout):
    out, r0 = {}, 0
    for nm, shape in layout:
        size = shape[0] * shape[1]
        nrow = -(-size // LANES)
        out[nm] = packed[r0:r0 + nrow].reshape(-1)[:size].reshape(shape)
        r0 += nrow
    return out


def kernel(x, meta, norm_mix_w, w_in, pool_w, pool_scale, conv_w, conv_b, dt_bias, a_log, d_skip, ssm_norm_w, w_out, norm_ffn_w, w_ff1, w_ff2, norm_f_w, loss_target, m_meta, m_norm_mix_w, m_w_in, m_pool_w, m_pool_scale, m_conv_w, m_conv_b, m_dt_bias, m_a_log, m_d_skip, m_ssm_norm_w, m_w_out, m_norm_ffn_w, m_w_ff1, m_w_ff2, m_norm_f_w, v_meta, v_norm_mix_w, v_w_in, v_pool_w, v_pool_scale, v_conv_w, v_conv_b, v_dt_bias, v_a_log, v_d_skip, v_ssm_norm_w, v_w_out, v_norm_ffn_w, v_w_ff1, v_w_ff2, v_norm_f_w):
    bsz, seq, d = x.shape
    t = seq + CHUNK
    n = bsz * t
    chip = 2 * lax.axis_index("x") + lax.axis_index("y")
    d_in = w_in.shape[2] * 4

    g_conv, g_meta = _comm(_Gather([conv_w[0], meta]), name="gather_small")
    convw = g_conv.transpose(1, 0, 2).reshape(CONV_W, D_XBC)
    meta_full = g_meta.transpose(1, 0, 2).reshape(N_META, d)
    (h0, hn1), (g_in,) = _embed_norm(x, meta_full, norm_mix_w, name="embed_norm",
                                     rider=_Gather([w_in[0].T.astype(BF16)]))
    h0f, hn1 = h0.reshape(n, d), hn1.reshape(n, d)
    late_weights = _Gather([w_out[0].astype(BF16), w_ff1[0].astype(BF16), w_ff2[0].astype(BF16)])
    win = g_in.reshape(d_in, d)
    wu, wz = win[:D_POOL], win[D_POOL:D_POOL + D_SSM]
    wx = win[D_POOL + D_SSM:D_POOL + D_SSM + D_XBC]
    wdt = jnp.pad(win[D_POOL + D_SSM + D_XBC:].reshape(N_GROUPS, HPG, d),
                  ((0, 0), (0, LANES - HPG), (0, 0))).reshape(D_DT, d)
    dtb, alog = _pad_heads(dt_bias), _pad_heads(a_log)
    dskip = jnp.repeat(d_skip, HEAD_DIM, axis=1)
    poolw = pool_w[0]

    u, z, dtr, dt_, acs_, tr_ = _proj_uz_dt(hn1, wu, wz, wdt, dtb, alog, t // CHUNK, name="proj_uzdt")
    xbc, xc = _proj_conv(hn1, wx, convw, conv_b, name="proj_xbc")
    ypool = _pool_fwd(u.reshape(bsz, t, D_POOL), poolw, pool_scale, name="pool_fwd")
    xbc3 = xbc.reshape(bsz, t, D_XBC)
    xc = xc.reshape(bsz, t, D_XBC)
    z3, dtr3 = z.reshape(bsz, t, D_SSM), dtr.reshape(bsz, t, D_DT)
    dt3, acs3 = dt_.reshape(bsz, t, D_DT), acs_.reshape(bsz, t, D_DT)
    tr3 = tr_.reshape(bsz, t // CHUNK, N_GROUPS, 16, LANES)
    (yn, ypre, sprev), (g_out, g_ff1, g_ff2) = _ssd_fwd(xc, dt3, acs3, tr3, z3, dskip, ssm_norm_w, name="ssd_fwd",
                                                        rider=late_weights)
    wo = g_out.reshape(D_POOL + D_SSM, d)
    wo_p, wo_s = wo[:D_POOL], wo[D_POOL:]
    w1 = g_ff1
    w2 = g_ff2.reshape(D_FF, d)
    ypool_f, yn_f = ypool.reshape(n, D_POOL), yn.reshape(n, D_SSM)
    add = lambda r, e: r + e
    h1, hn2 = _mm([ypool_f, yn_f], [wo_p, wo_s], name="out_proj", post=add, extras=(h0f,), norm_w=norm_ffn_w)
    act = _mm(hn2, w1, name="ff1", out_dtype=BF16)
    relu2 = lambda a: jnp.square(jnp.maximum(a, 0))
    h2 = _mm(act, w2, name="ff2", pre=relu2, post=add, extras=(h1,))
    dh2, dh2b, loss_acc, d_norm_f = _final_norm_loss(h2.reshape(bsz, t, d), loss_target, norm_f_w.reshape(1, d),
                                                     name="loss")

    dh2f, dh2bf = dh2.reshape(n, d), dh2b.reshape(n, d)
    dact = _mm(dh2bf, w2, name="ff2_bwd", nt=True, post=lambda r, a: r * (2.0 * jnp.maximum(a, 0).astype(F32)),
               extras=(act,), out_dtype=BF16)
    d_w2 = _mm_tn(act, dh2bf, name="ff2_dw", tk=2048, tn=1024, pre=relu2)
    d_w1 = _mm_tn(hn2, dact, name="ff1_dw", tk=1024, tn=2048, slab=D_FF // 4)
    dh1, dh1b, d_norm_ffn = _mm_rms_bwd(dact, w1, h1, norm_ffn_w, dh2f, name="ff1_bwd")
    dypool, dyn = _mm_fanout(dh1b, [wo_p, wo_s], name="out_proj_bwd")
    d_wo = _mm_tn_cat([ypool_f, yn_f], dh1b, name="out_proj_dw")
    big_late = [d_wo.reshape(4, (D_POOL + D_SSM) // 4, d),
                d_w1, d_w2.reshape(4, D_FF // 4, d)]
    (dz, dxs, dbm, dcm, ddtr, d_nw, d_heads), landed_late = _ssd_bwd(
        xc, dtr3, dt3, acs3, tr3, z3, ypre, sprev, dyn.reshape(bsz, t, D_SSM), dtb, alog, dskip, ssm_norm_w, name="ssd_bwd",
        rider=_Exchange(big_late))
    dxbc, d_convwb = _conv_bwd(xbc3, dxs, dbm, dcm, convw, conv_b, name="conv_bwd")
    du, d_poolw, d_poolsc = _pool_bwd(u.reshape(bsz, t, D_POOL), dypool.reshape(bsz, t, D_POOL), poolw, pool_scale,
                                      name="pool_bwd")
    duf, dzf, dxbcf, ddtrf = du.reshape(n, D_POOL), dz.reshape(n, D_SSM), dxbc.reshape(n, D_XBC), ddtr.reshape(n, D_DT)
    heads = jnp.sum(d_heads, axis=0)
    small_early = _pack_small({
        "pool_w": d_poolw, "pool_scale": d_poolsc,
        "conv_w": jnp.sum(d_convwb[:, :CONV_W], axis=0), "conv_b": jnp.sum(d_convwb[:, CONV_W:CONV_W + 1], axis=0),
        "dt_bias": _unpad_heads(heads[:, 2]), "a_log": _unpad_heads(heads[:, 1]), "d_skip": _unpad_heads(heads[:, 0]),
        "ssm_norm_w": jnp.sum(d_nw, axis=0), "norm_ffn_w": d_norm_ffn, "norm_f_w": d_norm_f}, _SMALL_EARLY)
    d_wuzdt = _mm_tn_cat([duf, dzf, ddtrf], hn1, name="proj_uzdt_dw")
    d_wx, (early_all,) = _mm_tn(dxbcf, hn1, name="proj_xbc_dw", tk=1280, tn=1024, rider=_Exchange([], small_early))
    d_wdt = d_wuzdt[D_POOL + D_SSM:].reshape(N_GROUPS, LANES, d)[:, :HPG].reshape(N_HEADS, d)
    d_win = jnp.concatenate([d_wuzdt[:D_POOL + D_SSM], d_wx, d_wdt], axis=0)
    big_in = d_win.reshape(4, d_in // 4, d)
    dhn1, (landed_in,) = _mm([duf, dzf, dxbcf, ddtrf], [wu, wz, wx, wdt], name="proj_bwd",
                             rider=_Exchange([big_in]))
    grad_x, d_head_rows, d_norm_mix = _input_grad(
        dhn1.reshape(bsz, t, d), h0, norm_mix_w, dh1.reshape(bsz, t, d), seq, name="input_grad")

    landed = [landed_in] + list(landed_late)
    small_late = _pack_small({"norm_mix_w": d_norm_mix, "meta": jnp.sum(d_head_rows[:, PAD:], axis=0),
                              "loss": loss_acc[0:1, 0:1]}, _SMALL_LATE)
    (late_all,) = _comm(_Exchange([], small_late), name="exchange_small")
    mine = [_chip_sum(l, name=f"chip_sum_{i}") for i, l in enumerate(landed)]
    theirs = _comm(_Swap(mine), name="swap_cores")
    gsmall = {**_unpack_small(_device_sum(early_all, name="device_sum_early"), _SMALL_EARLY),
              **_unpack_small(_device_sum(late_all, name="device_sum_late"), _SMALL_LATE)}
    gsmall["conv_w"] = lax.dynamic_slice_in_dim(gsmall["conv_w"], chip * (D_XBC // 4), D_XBC // 4, axis=1)
    gsmall["meta"] = lax.dynamic_slice_in_dim(gsmall["meta"], chip * (d // 4), d // 4, axis=1)
    loss = gsmall["loss"][0, 0]

    given = dict(meta=(meta, m_meta, v_meta), norm_mix_w=(norm_mix_w, m_norm_mix_w, v_norm_mix_w),
                 w_in=(w_in, m_w_in, v_w_in), pool_w=(pool_w, m_pool_w, v_pool_w),
                 pool_scale=(pool_scale, m_pool_scale, v_pool_scale), conv_w=(conv_w, m_conv_w, v_conv_w),
                 conv_b=(conv_b, m_conv_b, v_conv_b), dt_bias=(dt_bias, m_dt_bias, v_dt_bias),
                 a_log=(a_log, m_a_log, v_a_log), d_skip=(d_skip, m_d_skip, v_d_skip),
                 ssm_norm_w=(ssm_norm_w, m_ssm_norm_w, v_ssm_norm_w), w_out=(w_out, m_w_out, v_w_out),
                 norm_ffn_w=(norm_ffn_w, m_norm_ffn_w, v_norm_ffn_w), w_ff1=(w_ff1, m_w_ff1, v_w_ff1),
                 w_ff2=(w_ff2, m_w_ff2, v_w_ff2), norm_f_w=(norm_f_w, m_norm_f_w, v_norm_f_w))
    big_names = ["w_in", "w_out", "w_ff1", "w_ff2"]
    results = {}
    for nm, (w, m, v) in given.items():
        if nm in big_names:
            i = big_names.index(nm)
            parts, shape2 = (mine[i], theirs[i]), mine[i].shape
        else:
            parts, shape2 = (gsmall[nm],), gsmall[nm].shape
        if nm == "w_in":
            outs = _adamw(w[0].T, parts, m[0].T, v[0].T, name=f"adamw_{nm}")
            results[nm] = [o.T[None] for o in outs]
        else:
            outs = _adamw(w.reshape(shape2), parts, m.reshape(shape2), v.reshape(shape2), name=f"adamw_{nm}")
            results[nm] = [o.reshape(w.shape) for o in outs]
    order = list(given)
    return (loss, grad_x, *[results[nm][0] for nm in order], *[results[nm][1] for nm in order],
            *[results[nm][2] for nm in order], *[results[nm][3] for nm in order])
```

```python
import jax
import jax.numpy as jnp
from jax import lax
from jax.experimental import pallas as pl
from jax.experimental.pallas import tpu as pltpu

F32 = jnp.float32
BF16 = jnp.bfloat16
MESH = pl.DeviceIdType.MESH
ANY = pl.BlockSpec(memory_space=pl.ANY)

D_MODEL = 1024
N_META = 16
CHUNK = 128
PAD = CHUNK - N_META
POOL_WINDOWS = (2, 4, 8, 16)
D_POOL = 512
POOL_GROUP = 128
D_SSM = 1536
N_HEADS = 24
N_GROUPS = 4
HPG = 6
HEAD_DIM = 64
D_STATE = 128
GW = HPG * HEAD_DIM
D_XBC = D_SSM + 2 * N_GROUPS * D_STATE
D_DT = N_GROUPS * 128
D_FF = 4096
CONV_W = 4
EPS = 1e-5
LANES = 128
VMEM_LIMIT = 56 * 1024 * 1024

ADAM_LR, ADAM_B1, ADAM_B2, ADAM_EPS, ADAM_WD, ADAM_STEP = 0.001, 0.9, 0.999, 1e-08, 0.01, 10


def _params(*sem):
    return pltpu.CompilerParams(dimension_semantics=sem, vmem_limit_bytes=VMEM_LIMIT)


def _pick(n, cands):
    for c in cands:
        if n % c == 0:
            return c
    raise ValueError(f"no block size for {n}")


def _dot(a, b):
    return jnp.dot(a.astype(BF16), b.astype(BF16), preferred_element_type=F32)


def _dot_nt(a, b):
    return lax.dot_general(a.astype(BF16), b.astype(BF16), (((1,), (1,)), ((), ())), preferred_element_type=F32)


def _dot_tn(a, b):
    return lax.dot_general(a.astype(BF16), b.astype(BF16), (((0,), (0,)), ((), ())), preferred_element_type=F32)


def _dot_exact(mask, x, terms=3):
    m = mask.astype(BF16)
    dot = lambda t: jnp.dot(m, t, preferred_element_type=F32)
    hi = x.astype(BF16)
    r1 = x - hi.astype(F32)
    mid = r1.astype(BF16)
    if terms == 2:
        return dot(hi) + dot(mid)
    lo = (r1 - mid.astype(F32)).astype(BF16)
    return dot(hi) + dot(mid) + dot(lo)


def _sigmoid(x):
    return 1.0 / (1.0 + jnp.exp(-x))


def _softplus(x):
    return jnp.maximum(x, 0.0) + jnp.log1p(jnp.exp(-jnp.abs(x)))


def _sum_all(x):
    return jnp.sum(jnp.sum(x, axis=1, keepdims=True), axis=0, keepdims=True)


ROW_TILES = (1056, 768, 704, 512, 384, 256, 128)
TILE_BUDGET = 28 * 1024 * 1024


def _row_tile(n, bytes_per_row, fixed_bytes, budget=TILE_BUDGET):
    for tm in ROW_TILES:
        if n % tm == 0 and 2 * (tm * bytes_per_row + fixed_bytes) <= budget:
            return tm
    raise ValueError(f"no row tile for {n}")


WIDE_BUDGET = 38 * 1024 * 1024


def _mm(a, w, *, name, tn=512, nt=False, pre=None, post=None, extras=(), out_dtype=F32, norm_w=None, rider=None):
    assert norm_w is None or (rider is None and out_dtype == F32)
    a_list = list(a) if isinstance(a, (list, tuple)) else [a]
    w_list = list(w) if isinstance(w, (list, tuple)) else [w]
    n_a, n_ex = len(a_list), len(extras)
    n = a_list[0].shape[0]
    shard = w_list[0].shape[2] if w_list[0].ndim == 3 else None
    assert shard is None or (not nt and n_a == 1 and shard % tn == 0)
    m = w_list[0].shape[0] * shard if shard else w_list[0].shape[0] if nt else w_list[0].shape[1]
    tn = min(tn, m)
    size = lambda dt: jnp.dtype(dt).itemsize
    per_row = (sum(x.shape[1] * size(x.dtype) for x in a_list) + m * size(out_dtype)
               + sum(m * size(e.dtype) for e in extras) + (2 * m if norm_w is not None else 0))
    tm = _row_tile(n, per_row, sum(x.size * size(x.dtype) for x in w_list) // 2, WIDE_BUDGET)
    n_norm = 0 if norm_w is None else 1

    def body(*refs):
        a_refs, w_refs, ex_refs = refs[:n_a], refs[n_a:2 * n_a], refs[2 * n_a:2 * n_a + n_ex]
        o_ref = refs[2 * n_a + n_ex + n_norm]
        avs = [(a_ref[...] if pre is None else pre(a_ref[...])).astype(BF16) for a_ref in a_refs]
        for c0 in range(0, m, tn):
            r = None
            for av, w_ref in zip(avs, w_refs):
                if shard:
                    term = _dot(av, w_ref[c0 // shard, :, c0 % shard:c0 % shard + tn])
                else:
                    term = _dot_nt(av, w_ref[c0:c0 + tn, :]) if nt else _dot(av, w_ref[:, c0:c0 + tn])
                r = term if r is None else r + term
            if post is not None:
                r = post(r, *[e[:, c0:c0 + tn] for e in ex_refs])
            o_ref[:, c0:c0 + tn] = r.astype(out_dtype)
        if n_norm:
            x = o_ref[...]
            scale = lax.rsqrt(jnp.mean(x * x, axis=-1, keepdims=True) + EPS)
            refs[2 * n_a + n_ex + 2][...] = (x * scale * refs[2 * n_a + n_ex][...]).astype(BF16)

    a_specs = [pl.BlockSpec((tm, x.shape[1]), lambda i: (i, 0)) for x in a_list]
    w_specs = [pl.BlockSpec(x.shape, lambda i, nd=x.ndim: (0,) * nd, pipeline_mode=pl.Buffered(1)) for x in w_list]
    blk = pl.BlockSpec((tm, m), lambda i: (i, 0))
    vec = [pl.BlockSpec((1, m), lambda i: (0, 0))] * n_norm
    grid = (n // tm,)
    ride = _Ride(rider, body, 2 * n_a + n_ex + n_norm, 1 + n_norm, 0, grid)
    outs = pl.pallas_call(
        ride.body, name=name, grid=grid,
        in_specs=a_specs + w_specs + [blk] * n_ex + vec + ride.in_specs,
        out_specs=[blk] * (1 + n_norm) + ride.out_specs,
        out_shape=[jax.ShapeDtypeStruct((n, m), out_dtype)] + [jax.ShapeDtypeStruct((n, m), BF16)] * n_norm + ride.out_shape,
        scratch_shapes=ride.scratch, compiler_params=_params(*ride.semantics(("parallel",))),
    )(*a_list, *w_list, *extras, *([norm_w] * n_norm), *ride.args)
    if n_norm:
        return outs[0], outs[1]
    return (outs[0], outs[1:]) if rider else outs[0]


def _mm_fanout(a, ws, *, name, tn=512):
    n, k = a.shape
    ms = [w.shape[0] for w in ws]
    tm = _row_tile(n, k * 2 + 4 * sum(ms), sum(w.size for w in ws), WIDE_BUDGET)
    n_w = len(ws)

    def body(a_ref, *refs):
        av = a_ref[...]
        for w_ref, o_ref, m in zip(refs[:n_w], refs[n_w:], ms):
            for c0 in range(0, m, tn):
                o_ref[:, c0:c0 + tn] = _dot_nt(av, w_ref[c0:c0 + tn, :])

    return pl.pallas_call(
        body, name=name, grid=(n // tm,),
        in_specs=[pl.BlockSpec((tm, k), lambda i: (i, 0))]
        + [pl.BlockSpec(w.shape, lambda i: (0, 0), pipeline_mode=pl.Buffered(1)) for w in ws],
        out_specs=[pl.BlockSpec((tm, m), lambda i: (i, 0)) for m in ms],
        out_shape=[jax.ShapeDtypeStruct((n, m), F32) for m in ms],
        compiler_params=_params("parallel"),
    )(a, *ws)


def _mm_tn(a, g, *, name, tk, tn, pre=None, slab=None, rider=None):
    n, k = a.shape
    m = g.shape[1]
    tk, tn = min(tk, k), min(tn, m)
    tm = _row_tile(n, tk * jnp.dtype(a.dtype).itemsize + tn * jnp.dtype(g.dtype).itemsize, tk * tn * 4)
    steps = n // tm

    def body(a_ref, g_ref, o_ref, acc_ref):
        r = pl.program_id(2)

        @pl.when(r == 0)
        def _():
            acc_ref[...] = jnp.zeros_like(acc_ref)

        av = a_ref[...]
        if pre is not None:
            av = pre(av)
        if slab:
            for s in range(tn // slab):
                acc_ref[s] += _dot_tn(av, g_ref[:, s * slab:(s + 1) * slab])
        else:
            acc_ref[...] += _dot_tn(av, g_ref[...])

        @pl.when(r == steps - 1)
        def _():
            o_ref[...] = acc_ref[...].astype(BF16)

    if slab:
        block, out_spec = (tn // slab, tk, slab), pl.BlockSpec((tn // slab, tk, slab), lambda i, j, r: (j, i, 0))
        out_shape = jax.ShapeDtypeStruct((m // slab, k, slab), BF16)
    else:
        block, out_spec = (tk, tn), pl.BlockSpec((tk, tn), lambda i, j, r: (i, j))
        out_shape = jax.ShapeDtypeStruct((k, m), BF16)
    grid = (k // tk, m // tn, steps)
    ride = _Ride(rider, body, 2, 1, 1, grid)
    outs = pl.pallas_call(
        ride.body, name=name, grid=grid,
        in_specs=[pl.BlockSpec((tm, tk), lambda i, j, r: (r, i)), pl.BlockSpec((tm, tn), lambda i, j, r: (r, j))]
        + ride.in_specs,
        out_specs=[out_spec] + ride.out_specs, out_shape=[out_shape] + ride.out_shape,
        scratch_shapes=[pltpu.VMEM(block, F32)] + ride.scratch,
        compiler_params=_params(*ride.semantics(("parallel", "parallel", "arbitrary"))),
    )(a, g, *ride.args)
    return (outs[0], outs[1:]) if rider else outs[0]


def _mm_tn_cat(a_list, g, *, name):
    n, m = g.shape
    ks = [a.shape[1] for a in a_list]
    size = lambda x: jnp.dtype(x.dtype).itemsize
    tm = _row_tile(n, sum(a.shape[1] * size(a) for a in a_list) + m * size(g), sum(ks) * m * 4)
    steps, n_a = n // tm, len(a_list)

    def body(*refs):
        g_ref, o_ref, acc_ref = refs[n_a], refs[n_a + 1], refs[n_a + 2]
        r = pl.program_id(0)

        @pl.when(r == 0)
        def _():
            acc_ref[...] = jnp.zeros_like(acc_ref)

        gv, k0 = g_ref[...], 0
        for a_ref, k in zip(refs[:n_a], ks):
            acc_ref[k0:k0 + k, :] += _dot_tn(a_ref[...], gv)
            k0 += k

        @pl.when(r == steps - 1)
        def _():
            o_ref[...] = acc_ref[...].astype(BF16)

    return pl.pallas_call(
        body, name=name, grid=(steps,),
        in_specs=[pl.BlockSpec((tm, k), lambda r: (r, 0)) for k in ks] + [pl.BlockSpec((tm, m), lambda r: (r, 0))],
        out_specs=pl.BlockSpec((sum(ks), m), lambda r: (0, 0)),
        out_shape=jax.ShapeDtypeStruct((sum(ks), m), BF16),
        scratch_shapes=[pltpu.VMEM((sum(ks), m), F32)],
        compiler_params=_params("arbitrary"),
    )(*a_list, g)


def _mm_rms_bwd(a, w, h, w_norm, dres, *, name):
    n, k = a.shape
    d = h.shape[1]
    slabs, _, ks = w.shape
    tm = _row_tile(n, k * jnp.dtype(a.dtype).itemsize + d * (4 + 4 + 4 + 2), d * k, WIDE_BUDGET)

    def body(a_ref, w_ref, h_ref, wn_ref, dres_ref, dx_ref, dxb_ref, dw_ref):
        @pl.when(pl.program_id(0) == 0)
        def _():
            dw_ref[...] = jnp.zeros_like(dw_ref)

        dyv = None
        for s in range(slabs):
            part = _dot_nt(a_ref[:, s * ks:(s + 1) * ks], w_ref[s])
            dyv = part if dyv is None else dyv + part
        x = h_ref[...]
        r = lax.rsqrt(jnp.mean(x * x, axis=-1, keepdims=True) + EPS)
        g = dyv * wn_ref[...]
        dx = r * (g - x * (r * r) * jnp.mean(g * x, axis=-1, keepdims=True)) + dres_ref[...]
        dx_ref[...] = dx
        dxb_ref[...] = dx.astype(BF16)
        dw_ref[...] += jnp.sum(dyv * x * r, axis=0, keepdims=True)

    row = pl.BlockSpec((tm, d), lambda i: (i, 0))
    vec = pl.BlockSpec((1, d), lambda i: (0, 0))
    return pl.pallas_call(
        body, name=name, grid=(n // tm,),
        in_specs=[pl.BlockSpec((tm, k), lambda i: (i, 0)),
                  pl.BlockSpec(w.shape, lambda i: (0, 0, 0), pipeline_mode=pl.Buffered(1)), row, vec, row],
        out_specs=[row, row, vec],
        out_shape=[jax.ShapeDtypeStruct((n, d), F32), jax.ShapeDtypeStruct((n, d), BF16), jax.ShapeDtypeStruct((1, d), F32)],
        compiler_params=_params("arbitrary"),
    )(a, w, h, w_norm, dres)


def _embed_norm(x, meta, w, *, name, rider=None):
    bsz, seq, d = x.shape
    t = seq + CHUNK
    nc = t // CHUNK

    def body(x_ref, meta_ref, w_ref, h_ref, hn_ref):
        j = pl.program_id(0)
        first = jnp.concatenate([jnp.zeros((PAD, d), F32), meta_ref[...]], axis=0)
        for e in range(bsz):
            h = jnp.where(j == 0, first, x_ref[e])
            r = lax.rsqrt(jnp.mean(h * h, axis=-1, keepdims=True) + EPS)
            h_ref[e] = h
            hn_ref[e] = (h * r * w_ref[...]).astype(BF16)

    row = pl.BlockSpec((bsz, CHUNK, d), lambda j: (0, j, 0))
    grid = (nc,)
    ride = _Ride(rider, body, 3, 2, 0, grid)
    outs = pl.pallas_call(
        ride.body, name=name, grid=grid,
        in_specs=[pl.BlockSpec((bsz, CHUNK, d), lambda j: (0, jnp.maximum(j - 1, 0), 0)),
                  pl.BlockSpec((N_META, d), lambda j: (0, 0)), pl.BlockSpec((1, d), lambda j: (0, 0))] + ride.in_specs,
        out_specs=[row, row] + ride.out_specs,
        out_shape=[jax.ShapeDtypeStruct((bsz, t, d), F32), jax.ShapeDtypeStruct((bsz, t, d), BF16)] + ride.out_shape,
        scratch_shapes=ride.scratch, compiler_params=_params(*ride.semantics(("parallel",))),
    )(x, meta, w, *ride.args)
    return outs[:2], outs[2:]


def _final_norm_loss(h2, target, w, *, name):
    bsz, t, d = h2.shape
    nc = t // CHUNK

    def body(h_ref, t_ref, w_ref, dh_ref, dhb_ref, loss_ref, dw_ref):
        j = pl.program_id(0)

        @pl.when(j == 0)
        def _():
            loss_ref[...] = jnp.zeros_like(loss_ref)
            dw_ref[...] = jnp.zeros_like(dw_ref)

        wv = w_ref[...]
        for e in range(bsz):
            x = h_ref[e]
            r = lax.rsqrt(jnp.mean(x * x, axis=-1, keepdims=True) + EPS)
            diff = jnp.where(j > 0, x * r * wv - t_ref[e], 0.0)
            loss_ref[...] += _sum_all(diff * diff) * (0.5 / d)
            dy = diff * (1.0 / d)
            g = dy * wv
            dh = r * (g - x * (r * r) * jnp.mean(g * x, axis=-1, keepdims=True))
            dh_ref[e] = dh
            dhb_ref[e] = dh.astype(BF16)
            dw_ref[...] += jnp.sum(dy * x * r, axis=0, keepdims=True)

    row = pl.BlockSpec((bsz, CHUNK, d), lambda j: (0, j, 0))
    return pl.pallas_call(
        body, name=name, grid=(nc,),
        in_specs=[row, pl.BlockSpec((bsz, CHUNK, d), lambda j: (0, jnp.maximum(j - 1, 0), 0)),
                  pl.BlockSpec((1, d), lambda j: (0, 0))],
        out_specs=[row, row, pl.BlockSpec((8, LANES), lambda j: (0, 0)), pl.BlockSpec((1, d), lambda j: (0, 0))],
        out_shape=[jax.ShapeDtypeStruct((bsz, t, d), F32), jax.ShapeDtypeStruct((bsz, t, d), BF16),
                   jax.ShapeDtypeStruct((8, LANES), F32), jax.ShapeDtypeStruct((1, d), F32)],
        compiler_params=_params("arbitrary"),
    )(h2, target, w)


def _pool_masks(j, transposed):
    r = lax.broadcasted_iota(jnp.int32, (CHUNK, 2 * CHUNK), 0)
    c = lax.broadcasted_iota(jnp.int32, (CHUNK, 2 * CHUNK), 1)
    masks = []
    for w in POOL_WINDOWS:
        if transposed:
            m = (c >= r) & (c < r + w)
        else:
            s = c - CHUNK
            m = (s <= r) & (s > r - w) & (s + j * CHUNK >= 0)
        masks.append(m.astype(F32))
    return masks


POOL_TERMS = 2


def _pool_count(t_global, w):
    return jnp.clip(t_global - PAD + 1, 1, w).astype(F32)


def _pool_fwd(u, pool_w, pool_scale, *, name):
    bsz, t, _ = u.shape
    nc = t // CHUNK

    def body(prev_ref, cur_ref, pw_ref, sc_ref, o_ref):
        j = pl.program_id(0)
        masks = _pool_masks(j, False)
        tg = j * CHUNK + lax.broadcasted_iota(jnp.int32, (CHUNK, 1), 0)
        count = [_pool_count(tg, w) for w in POOL_WINDOWS]
        units = [(e, gi) for e in range(bsz) for gi in range(len(POOL_WINDOWS))]
        sl = lambda gi: pl.ds(gi * POOL_GROUP, POOL_GROUP)
        cur = {(e, gi): cur_ref[e, :, sl(gi)] for e, gi in units}
        both = {(e, gi): jnp.concatenate([prev_ref[e, :, sl(gi)], cur[e, gi]], axis=0) for e, gi in units}
        win = {(e, gi): _dot_exact(masks[gi], both[e, gi], POOL_TERMS) for e, gi in units}
        pooled = {(e, gi): win[e, gi] / count[gi] - cur[e, gi] for e, gi in units}
        mixed = {(e, gi): _dot(pooled[e, gi], pw_ref[gi]) for e, gi in units}
        for e, gi in units:
            o_ref[e, :, sl(gi)] = (mixed[e, gi] * sc_ref[:, sl(gi)]).astype(BF16)

    blk = lambda f: pl.BlockSpec((bsz, CHUNK, D_POOL), f)
    return pl.pallas_call(
        body, name=name, grid=(nc,),
        in_specs=[blk(lambda j: (0, jnp.maximum(j - 1, 0), 0)), blk(lambda j: (0, j, 0)),
                  pl.BlockSpec((4, POOL_GROUP, POOL_GROUP), lambda j: (0, 0, 0)),
                  pl.BlockSpec((1, D_POOL), lambda j: (0, 0))],
        out_specs=blk(lambda j: (0, j, 0)), out_shape=jax.ShapeDtypeStruct(u.shape, BF16),
        compiler_params=_params("parallel"),
    )(u, u, pool_w, pool_scale)


def _pool_bwd(u, dyp, pool_w, pool_scale, *, name):
    bsz, t, _ = u.shape
    nc = t // CHUNK

    def body(prev_ref, cur_ref, dy_ref, dyn_ref, pw_ref, sc_ref, du_ref, dpw_ref, dsc_ref):
        j = pl.program_id(0)

        @pl.when(j == 0)
        def _():
            dpw_ref[...] = jnp.zeros_like(dpw_ref)
            dsc_ref[...] = jnp.zeros_like(dsc_ref)

        fwd = _pool_masks(j, False)
        bwd = _pool_masks(j, True)
        tg = j * CHUNK + lax.broadcasted_iota(jnp.int32, (CHUNK, 1), 0)
        count = [_pool_count(tg, w) for w in POOL_WINDOWS]
        count_next = [_pool_count(tg + CHUNK, w) for w in POOL_WINDOWS]
        has_next = j < nc - 1
        groups = range(len(POOL_WINDOWS))
        units = [(e, gi) for e in range(bsz) for gi in groups]
        sl = lambda gi: pl.ds(gi * POOL_GROUP, POOL_GROUP)
        cur = {(e, gi): cur_ref[e, :, sl(gi)] for e, gi in units}
        both = {(e, gi): jnp.concatenate([prev_ref[e, :, sl(gi)], cur[e, gi]], axis=0) for e, gi in units}
        win = {(e, gi): _dot_exact(fwd[gi], both[e, gi], POOL_TERMS) for e, gi in units}
        pooled = {(e, gi): win[e, gi] / count[gi] - cur[e, gi] for e, gi in units}
        dy = {(e, gi): dy_ref[e, :, sl(gi)] for e, gi in units}
        mixed = {(e, gi): _dot(pooled[e, gi], pw_ref[gi]) for e, gi in units}
        dm = {(e, gi): dy[e, gi] * sc_ref[:, sl(gi)] for e, gi in units}
        dm_next = {(e, gi): jnp.where(has_next, dyn_ref[e, :, sl(gi)], 0.0) * sc_ref[:, sl(gi)] for e, gi in units}
        dpw = {(e, gi): _dot_tn(pooled[e, gi], dm[e, gi]) for e, gi in units}
        dpooled = {(e, gi): _dot_nt(dm[e, gi], pw_ref[gi]) for e, gi in units}
        dpooled_next = {(e, gi): _dot_nt(dm_next[e, gi], pw_ref[gi]) for e, gi in units}
        spread = {(e, gi): jnp.concatenate([dpooled[e, gi] / count[gi], dpooled_next[e, gi] / count_next[gi]], axis=0)
                  for e, gi in units}
        back = {(e, gi): _dot_exact(bwd[gi], spread[e, gi], POOL_TERMS) for e, gi in units}
        for e, gi in units:
            du_ref[e, :, sl(gi)] = (back[e, gi] - dpooled[e, gi]).astype(BF16)
        for gi in groups:
            dsc, dw = None, None
            for e in range(bsz):
                term = jnp.sum(dy[e, gi] * mixed[e, gi], axis=0, keepdims=True)
                dsc = term if dsc is None else dsc + term
                dw = dpw[e, gi] if dw is None else dw + dpw[e, gi]
            dsc_ref[:, sl(gi)] += dsc
            dpw_ref[gi] += dw

    blk = lambda f: pl.BlockSpec((bsz, CHUNK, D_POOL), f)
    return pl.pallas_call(
        body, name=name, grid=(nc,),
        in_specs=[blk(lambda j: (0, jnp.maximum(j - 1, 0), 0)), blk(lambda j: (0, j, 0)),
                  blk(lambda j: (0, j, 0)), blk(lambda j: (0, jnp.minimum(j + 1, nc - 1), 0)),
                  pl.BlockSpec((4, POOL_GROUP, POOL_GROUP), lambda j: (0, 0, 0)),
                  pl.BlockSpec((1, D_POOL), lambda j: (0, 0))],
        out_specs=[blk(lambda j: (0, j, 0)), pl.BlockSpec((4, POOL_GROUP, POOL_GROUP), lambda j: (0, 0, 0)),
                   pl.BlockSpec((1, D_POOL), lambda j: (0, 0))],
        out_shape=[jax.ShapeDtypeStruct(u.shape, BF16), jax.ShapeDtypeStruct((4, POOL_GROUP, POOL_GROUP), F32),
                   jax.ShapeDtypeStruct((1, D_POOL), F32)],
        compiler_params=_params("arbitrary"),
    )(u, u, dyp, dyp, pool_w, pool_scale)


CONV_SLAB = 512


def _conv_taps(tail, cur, keep_tail):
    ext = jnp.concatenate([jnp.where(keep_tail, tail, 0.0), cur], axis=0)
    return [(pltpu.roll(ext, CONV_W - 1 - k, 0) if k < CONV_W - 1 else ext)[8:] for k in range(CONV_W)]


def _conv_pre(taps, w_ref, b_ref, sl):
    acc = b_ref[:, sl]
    for k in range(CONV_W):
        acc = acc + w_ref[k:k + 1, sl] * taps[k]
    return acc


def _proj_conv(hn, w, conv_w, conv_b, *, name):
    n, d = hn.shape
    c = w.shape[0]
    assert PAD >= CONV_W - 1
    tm = _row_tile(n, d * 2 + c * (4 + 2), c * d, WIDE_BUDGET)

    def body(hn_ref, w_ref, cw_ref, cb_ref, xbc_ref, xc_ref, tail_ref):
        @pl.when(pl.program_id(0) == 0)
        def _():
            tail_ref[...] = jnp.zeros_like(tail_ref)

        av = hn_ref[...]
        starts = list(range(0, c, CONV_SLAB))

        def project(c0):
            xbc_ref[:, pl.ds(c0, CONV_SLAB)] = _dot_nt(av, w_ref[c0:c0 + CONV_SLAB, :])

        def convolve(c0):
            sl = pl.ds(c0, CONV_SLAB)
            xb = xbc_ref[:, sl]
            pre = _conv_pre(_conv_taps(tail_ref[:, sl], xb, True), cw_ref, cb_ref, sl)
            xc_ref[:, sl] = (pre * _sigmoid(pre)).astype(BF16)
            tail_ref[:, sl] = xb[tm - 8:, :]

        project(starts[0])
        for c0, c_next in zip(starts, starts[1:] + [None]):
            if c_next is not None:
                project(c_next)
            convolve(c0)

    row = lambda width: pl.BlockSpec((tm, width), lambda i: (i, 0))
    return pl.pallas_call(
        body, name=name, grid=(n // tm,),
        in_specs=[row(d), pl.BlockSpec(w.shape, lambda i: (0, 0), pipeline_mode=pl.Buffered(1)),
                  pl.BlockSpec((CONV_W, c), lambda i: (0, 0)), pl.BlockSpec((1, c), lambda i: (0, 0))],
        out_specs=[row(c), row(c)],
        out_shape=[jax.ShapeDtypeStruct((n, c), F32), jax.ShapeDtypeStruct((n, c), BF16)],
        scratch_shapes=[pltpu.VMEM((8, c), F32)],
        compiler_params=_params("arbitrary"),
    )(hn, w, conv_w, conv_b)


def _conv_bwd(xbc, dxs, db, dc, conv_w, conv_b, *, name):
    bsz, t, c = xbc.shape
    nc = t // CHUNK
    halo = 16
    rows = CHUNK + halo

    def body(tail_ref, cur_ref, head_ref, dxs_ref, db_ref, dc_ref, dxs_head, db_head, dc_head, w_ref, b_ref,
             dx_ref, dwb_ref):
        j = pl.program_id(1)

        @pl.when(j == 0)
        def _():
            dwb_ref[...] = jnp.zeros_like(dwb_ref)

        has_prev, has_next = j > 0, j < nc - 1
        for c0 in range(0, c, CONV_SLAB):
            sl = pl.ds(c0, CONV_SLAB)
            if c0 < D_SSM:
                dxc, dxc_next = dxs_ref[0, :, sl], dxs_head[0, :, sl]
            elif c0 < D_SSM + D_POOL:
                dxc, dxc_next = db_ref[0], db_head[0]
            else:
                dxc, dxc_next = dc_ref[0], dc_head[0]
            dxc = jnp.concatenate([dxc.astype(F32), jnp.where(has_next, dxc_next.astype(F32), 0.0)], axis=0)
            ext = jnp.concatenate([jnp.where(has_prev, tail_ref[0, :, sl], 0.0), cur_ref[0, :, sl],
                                   jnp.where(has_next, head_ref[0, :, sl], 0.0)], axis=0)
            taps = [(pltpu.roll(ext, CONV_W - 1 - k, 0) if k < CONV_W - 1 else ext)[8:] for k in range(CONV_W)]
            pre = _conv_pre(taps, w_ref, b_ref, sl)
            s = _sigmoid(pre)
            dpre = dxc * (s * (1.0 + pre * (1.0 - s)))
            acc = w_ref[CONV_W - 1:CONV_W, sl] * dpre[:CHUNK]
            for k in range(CONV_W - 1):
                up = CONV_W - 1 - k
                acc = acc + w_ref[k:k + 1, sl] * pltpu.roll(dpre, rows - up, 0)[:CHUNK]
            dx_ref[0, :, sl] = acc.astype(BF16)
            for k in range(CONV_W):
                dwb_ref[0, k:k + 1, sl] += jnp.sum(dpre[:CHUNK] * taps[k][:CHUNK], axis=0, keepdims=True)
            dwb_ref[0, CONV_W:CONV_W + 1, sl] += jnp.sum(dpre[:CHUNK], axis=0, keepdims=True)

    assert CONV_SLAB == D_POOL and D_SSM % CONV_SLAB == 0
    row = lambda width: pl.BlockSpec((1, CHUNK, width), lambda b, j: (b, j, 0))
    nxt = lambda width: pl.BlockSpec(
        (1, halo, width), lambda b, j: (b, jnp.minimum((j + 1) * (CHUNK // halo), t // halo - 1), 0))
    return pl.pallas_call(
        body, name=name, grid=(bsz, nc),
        in_specs=[pl.BlockSpec((1, 8, c), lambda b, j: (b, jnp.maximum(j * (CHUNK // 8) - 1, 0), 0)), row(c), nxt(c),
                  row(D_SSM), row(D_POOL), row(D_POOL), nxt(D_SSM), nxt(D_POOL), nxt(D_POOL),
                  pl.BlockSpec((CONV_W, c), lambda b, j: (0, 0)), pl.BlockSpec((1, c), lambda b, j: (0, 0))],
        out_specs=[row(c), pl.BlockSpec((1, 8, c), lambda b, j: (b, 0, 0))],
        out_shape=[jax.ShapeDtypeStruct(xbc.shape, BF16), jax.ShapeDtypeStruct((bsz, 8, c), F32)],
        compiler_params=_params("parallel", "arbitrary"),
    )(xbc, xbc, xbc, dxs, db, dc, dxs, db, dc, conv_w, conv_b)


def _dt_valid(j):
    lane = lax.broadcasted_iota(jnp.int32, (CHUNK, LANES), 1)
    row = lax.broadcasted_iota(jnp.int32, (CHUNK, LANES), 0)
    return (lane < HPG) & ((j > 0) | (row >= PAD))


def _proj_uz_dt(hn, wu, wz, wdt, dtb, alog, nc, *, name):
    n, d = hn.shape
    tm = _pick(n, (768, 384, 128))
    per_tile = tm // CHUNK
    widths = (wu.shape[0], wz.shape[0], wdt.shape[0])

    def body(hn_ref, wu_ref, wz_ref, wdt_ref, dtb_ref, alog_ref, u_ref, z_ref, dtr_ref, dt_ref, acs_ref, tr_ref):
        i = pl.program_id(0)
        av = hn_ref[...]
        for w_ref, o_ref, m in zip((wu_ref, wz_ref, wdt_ref), (u_ref, z_ref, dtr_ref), widths):
            for c0 in range(0, m, 512):
                o_ref[:, c0:c0 + 512] = _dot_nt(av, w_ref[c0:c0 + 512, :])
        row = lax.broadcasted_iota(jnp.int32, (CHUNK, LANES), 0)
        lane = lax.broadcasted_iota(jnp.int32, (CHUNK, LANES), 1)
        tril = (row >= lane).astype(F32)
        units = [(cc, g) for cc in range(per_tile) for g in range(N_GROUPS)]
        at = lambda cc, g: (pl.ds(cc * CHUNK, CHUNK), pl.ds(g * LANES, LANES))
        valid = [(lane < HPG) & (((i * per_tile + cc) % nc > 0) | (row >= PAD)) for cc in range(per_tile)]
        dt = {(cc, g): jnp.where(valid[cc], _softplus(dtr_ref[at(cc, g)] + dtb_ref[g]), 0.0) for cc, g in units}
        acs = {(cc, g): _dot_exact(tril, dt[cc, g] * -jnp.exp(alog_ref[g])) for cc, g in units}
        for cc, g in units:
            dt_ref[at(cc, g)] = dt[cc, g]
            acs_ref[at(cc, g)] = acs[cc, g]
            tr_ref[cc, g, 0:8, :] = dt[cc, g].T[0:8]
            tr_ref[cc, g, 8:16, :] = acs[cc, g].T[0:8]

    row_blk = lambda width: pl.BlockSpec((tm, width), lambda i: (i, 0))
    whole = lambda w: pl.BlockSpec(w.shape, lambda i: (0, 0), pipeline_mode=pl.Buffered(1))
    const = pl.BlockSpec((N_GROUPS, 1, LANES), lambda i: (0, 0, 0))
    return pl.pallas_call(
        body, name=name, grid=(n // tm,),
        in_specs=[row_blk(d), whole(wu), whole(wz), whole(wdt), const, const],
        out_specs=[row_blk(widths[0]), row_blk(widths[1])] + [row_blk(D_DT)] * 3
        + [pl.BlockSpec((per_tile, N_GROUPS, 16, LANES), lambda i: (i, 0, 0, 0))],
        out_shape=[jax.ShapeDtypeStruct((n, widths[0]), F32), jax.ShapeDtypeStruct((n, widths[1]), F32)]
        + [jax.ShapeDtypeStruct((n, D_DT), F32)] * 3 + [jax.ShapeDtypeStruct((n // CHUNK, N_GROUPS, 16, LANES), F32)],
        compiler_params=_params("parallel"),
    )(hn, wu, wz, wdt, dtb, alog)


def _ssd_decay(dt, acs, tr):
    lane = lax.broadcasted_iota(jnp.int32, (CHUNK, LANES), 1)
    row = lax.broadcasted_iota(jnp.int32, (CHUNK, LANES), 0)
    return dict(lane=lane, row=row, dt=dt, causal=row >= lane, acs=acs, acs_t=tr[8:16], dt_t=tr[0:8],
                aend=acs[CHUNK - 1:CHUNK, :])


def _ssd_specs(bsz, nc, rev):
    ch = (lambda j: nc - 1 - j) if rev else (lambda j: j)
    return dict(
        xs=pl.BlockSpec((bsz, CHUNK, GW), lambda g, j: (0, ch(j), g)),
        bm=pl.BlockSpec((bsz, CHUNK, D_STATE), lambda g, j: (0, ch(j), D_SSM // D_STATE + g)),
        cm=pl.BlockSpec((bsz, CHUNK, D_STATE), lambda g, j: (0, ch(j), D_SSM // D_STATE + N_GROUPS + g)),
        lane_blk=pl.BlockSpec((bsz, CHUNK, LANES), lambda g, j: (0, ch(j), g)),
        grp_const=pl.BlockSpec((1, 1, LANES), lambda g, j: (g, 0, 0)),
        grp_vec=pl.BlockSpec((1, GW), lambda g, j: (0, g)),
        state=pl.BlockSpec((bsz, 1, D_STATE, GW), lambda g, j: (0, ch(j), 0, g)),
        tr=pl.BlockSpec((bsz, 1, 1, 16, LANES), lambda g, j: (0, ch(j), g, 0, 0)),
    )


def _ssd_fwd(xc, dt, acs, tr, z, dskip, normw, *, name, rider=None):
    bsz, t, _ = xc.shape
    nc = t // CHUNK
    sp = _ssd_specs(bsz, nc, False)

    def body(xs_ref, b_ref, c_ref, dt_ref, acs_ref, tr_ref, z_ref, dsk_ref, nw_ref, yn_ref, y_ref, sp_ref, s_ref):
        j = pl.program_id(1)

        @pl.when(j == 0)
        def _():
            s_ref[...] = jnp.zeros_like(s_ref)

        ex = range(bsz)
        units = [(e, r) for e in ex for r in range(HPG)]
        full = lambda v: jnp.broadcast_to(v, (CHUNK, LANES))
        pair = lambda r: pl.ds((r // 2) * LANES, LANES)
        q = [_ssd_decay(dt_ref[e], acs_ref[e], tr_ref[e, 0, 0]) for e in ex]
        for e in ex:
            sp_ref[e, 0] = s_ref[e]
        bm, cm = [b_ref[e] for e in ex], [c_ref[e] for e in ex]
        cb = [_dot_nt(cm[e], bm[e]) for e in ex]
        low = q[0]["lane"] < HEAD_DIM
        col = {(e, r): full(q[e]["acs"][:, r:r + 1]) for e, r in units}
        aend = {(e, r): q[e]["aend"][:, r:r + 1] for e, r in units}
        decay = {(e, r): jnp.exp(jnp.where(q[e]["causal"], col[e, r] - q[e]["acs_t"][r:r + 1, :], -jnp.inf))
                 for e, r in units}
        mp = {(e, r): cb[e] * decay[e, r] * q[e]["dt_t"][r:r + 1, :] for e, r in units}
        ce = {(e, r): cm[e] * jnp.exp(col[e, r]) for e, r in units}
        bk = {(e, r): bm[e] * (jnp.exp(aend[e, r] - col[e, r]) * full(q[e]["dt"][:, r:r + 1])) for e, r in units}
        xp = {(e, r): xs_ref[e, :, pair(r)] for e, r in units}
        s_old = {(e, r): s_ref[e, :, pair(r)] for e, r in units}
        y_h = {u: _dot(mp[u], xp[u]) + _dot(ce[u], s_old[u]) for u in units}
        s_h = {u: jnp.exp(aend[u]) * s_old[u] + _dot_tn(bk[u], xp[u]) for u in units}
        for e in ex:
            for r in range(0, HPG, 2):
                y_ref[e, :, pair(r)] = jnp.where(low, y_h[e, r], y_h[e, r + 1])
                s_ref[e, :, pair(r)] = jnp.where(low, s_h[e, r], s_h[e, r + 1])
        y = [y_ref[e] + dsk_ref[...] * xs_ref[e] for e in ex]
        zz = [z_ref[e] for e in ex]
        yg = [y[e] * (zz[e] * _sigmoid(zz[e])) for e in ex]
        rstd = [lax.rsqrt(jnp.mean(yg[e] * yg[e], axis=-1, keepdims=True) + EPS) for e in ex]
        for e in ex:
            y_ref[e] = y[e]
            yn_ref[e] = (yg[e] * rstd[e] * nw_ref[...]).astype(BF16)

    grid = (N_GROUPS, nc)
    ride = _Ride(rider, body, 9, 3, 1, grid)
    outs = pl.pallas_call(
        ride.body, name=name, grid=grid,
        in_specs=[sp["xs"], sp["bm"], sp["cm"], sp["lane_blk"], sp["lane_blk"], sp["tr"], sp["xs"],
                  sp["grp_vec"], sp["grp_vec"]] + ride.in_specs,
        out_specs=[sp["xs"], sp["xs"], sp["state"]] + ride.out_specs,
        out_shape=[jax.ShapeDtypeStruct((bsz, t, D_SSM), BF16), jax.ShapeDtypeStruct((bsz, t, D_SSM), F32),
                   jax.ShapeDtypeStruct((bsz, nc, D_STATE, D_SSM), F32)] + ride.out_shape,
        scratch_shapes=[pltpu.VMEM((bsz, D_STATE, GW), F32)] + ride.scratch,
        compiler_params=_params(*ride.semantics(("parallel", "arbitrary"))),
    )(xc, xc, xc, dt, acs, tr, z, dskip, normw, *ride.args)
    return outs[:3], outs[3:]


def _ssd_bwd(xc, dtr, dt, acs, tr, z, ypre, sprev, dyn, dtb, alog, dskip, normw, *, name, rider=None):
    bsz, t, _ = xc.shape
    nc = t // CHUNK
    sp = _ssd_specs(bsz, nc, True)

    def body(xs_ref, b_ref, c_ref, dtr_ref, dt_ref, acs_ref, tr_ref, z_ref, y_ref, sp_ref, dyn_ref, dtb_ref, alog_ref,
             dsk_ref, nw_ref, dz_ref, dxs_ref, db_ref, dc_ref, ddt_ref, dnw_ref, dsm_ref, ds_ref):
        j = pl.program_id(1)

        @pl.when(j == 0)
        def _():
            ds_ref[...] = jnp.zeros_like(ds_ref)
            dnw_ref[...] = jnp.zeros_like(dnw_ref)
            dsm_ref[...] = jnp.zeros_like(dsm_ref)

        ex = range(bsz)
        heads = range(HPG)
        units = [(e, r) for e in ex for r in heads]
        q = [_ssd_decay(dt_ref[e], acs_ref[e], tr_ref[e, 0, 0]) for e in ex]
        a = -jnp.exp(alog_ref[0])
        valid = _dt_valid(nc - 1 - j)
        lane, row = q[0]["lane"], q[0]["row"]
        lane1 = lane[0:1, :]
        nw = nw_ref[...]
        y, zz, dyn = [y_ref[e] for e in ex], [z_ref[e] for e in ex], [dyn_ref[e] for e in ex]
        sz = [_sigmoid(zz[e]) for e in ex]
        sil = [zz[e] * sz[e] for e in ex]
        yg = [y[e] * sil[e] for e in ex]
        rstd = [lax.rsqrt(jnp.mean(yg[e] * yg[e], axis=-1, keepdims=True) + EPS) for e in ex]
        gn = [dyn[e] * nw for e in ex]
        dyg = [rstd[e] * (gn[e] - yg[e] * (rstd[e] * rstd[e]) * jnp.mean(gn[e] * yg[e], axis=-1, keepdims=True))
               for e in ex]
        dy = [dyg[e] * sil[e] for e in ex]
        xs = [xs_ref[e] for e in ex]
        for e in ex:
            dnw_ref[e] += jnp.sum(dyn[e] * yg[e] * rstd[e], axis=0, keepdims=True)
            dz_ref[e] = (dyg[e] * y[e] * (sz[e] * (1.0 + zz[e] * (1.0 - sz[e])))).astype(BF16)
        dskip_cols = [jnp.sum(dy[e] * xs[e], axis=0, keepdims=True) for e in ex]

        bm, cm = [b_ref[e] for e in ex], [c_ref[e] for e in ex]
        cb = [_dot_nt(cm[e], bm[e]) for e in ex]
        zero = jnp.zeros((CHUNK, LANES), F32)
        full = lambda v: jnp.broadcast_to(v, (CHUNK, LANES))
        low = lane < HEAD_DIM
        half = [low if r % 2 == 0 else ~low for r in heads]
        sl = lambda v, r: v[:, (r // 2) * LANES:(r // 2 + 1) * LANES]
        pair = lambda r: pl.ds((r // 2) * LANES, LANES)
        col = {(e, r): full(q[e]["acs"][:, r:r + 1]) for e, r in units}
        dt_col = {(e, r): full(q[e]["dt"][:, r:r + 1]) for e, r in units}
        aend = {(e, r): q[e]["aend"][:, r:r + 1] for e, r in units}
        dt_row = {(e, r): q[e]["dt_t"][r:r + 1, :] for e, r in units}
        decay = {(e, r): jnp.exp(jnp.where(q[e]["causal"], col[e, r] - q[e]["acs_t"][r:r + 1, :], -jnp.inf))
                 for e, r in units}
        ea = {u: jnp.exp(col[u]) for u in units}
        dte = {u: jnp.exp(aend[u] - col[u]) for u in units}
        ed = {u: jnp.exp(aend[u]) for u in units}
        k = {u: dte[u] * dt_col[u] for u in units}
        mp = {(e, r): cb[e] * decay[e, r] * dt_row[e, r] for e, r in units}
        xp = {(e, r): sl(xs[e], r) for e, r in units}
        dym = {(e, r): jnp.where(half[r], sl(dy[e], r), 0.0) for e, r in units}
        s_old = {(e, r): sp_ref[e, 0, :, pair(r)] for e, r in units}
        ds_old = {(e, r): ds_ref[e, :, pair(r)] for e, r in units}
        dsm = {(e, r): jnp.where(half[r], ds_old[e, r], 0.0) for e, r in units}
        gmat = {u: _dot_nt(dym[u], xp[u]) for u in units}
        t1 = {u: _dot_nt(dym[u], s_old[u]) for u in units}
        dbs = {u: _dot_nt(xp[u], dsm[u]) for u in units}
        dx = {(e, r): _dot_tn(mp[e, r], dym[e, r]) + _dot(bm[e] * k[e, r], dsm[e, r]) for e, r in units}
        ds = {(e, r): _dot_tn(cm[e] * ea[e, r], dym[e, r]) for e, r in units}
        gd = {u: gmat[u] * decay[u] for u in units}
        w0 = {(e, r): gd[e, r] * cb[e] for e, r in units}
        cs0 = {u: jnp.sum(w0[u], axis=0, keepdims=True) for u in units}
        rs = {u: jnp.sum(w0[u] * dt_row[u], axis=1, keepdims=True) for u in units}
        qv = {(e, r): jnp.sum(cm[e] * t1[e, r], axis=1, keepdims=True) for e, r in units}
        dk = {(e, r): jnp.sum(bm[e] * dbs[e, r], axis=1, keepdims=True) for e, r in units}
        ddte = {u: dk[u] * dt_col[u] for u in units}
        d_aend = {u: _sum_all(dsm[u] * s_old[u]) * ed[u] + _sum_all(ddte[u][:, 0:1] * dte[u][:, 0:1]) for u in units}
        last_row = row == CHUNK - 1
        dacs_col = {u: rs[u] + qv[u] * ea[u] - ddte[u] * dte[u] + jnp.where(last_row, d_aend[u], 0.0) for u in units}
        triu = (lane >= row).astype(F32)
        for e in ex:
            dcb, dc_acc, db_acc = zero, zero, zero
            dacs, dacs_t, ddt, ddt_t = zero, zero, zero, zero
            dskip_row = jnp.zeros((1, LANES), F32)
            for r in heads:
                u = (e, r)
                dcb = dcb + gd[u] * dt_row[u]
                dc_acc = dc_acc + ea[u] * t1[u]
                db_acc = db_acc + k[u] * dbs[u]
                dacs = jnp.where(lane == r, dacs_col[u], dacs)
                ddt = jnp.where(lane == r, dk[u] * dte[u], ddt)
                dacs_t = jnp.where(row == r, -cs0[u] * dt_row[u], dacs_t)
                ddt_t = jnp.where(row == r, cs0[u], ddt_t)
                dsk = _sum_all(jnp.where(half[r][0:1, :], sl(dskip_cols[e], r), 0.0))
                dskip_row = dskip_row + jnp.where(lane1 == r, dsk, 0.0)
            for r in range(0, HPG, 2):
                dxs_ref[e, :, pair(r)] = (dx[e, r] + dx[e, r + 1] + sl(dy[e], r) * dsk_ref[:, pair(r)]).astype(BF16)
                ed_pair = jnp.where(lane1 < HEAD_DIM, ed[e, r], ed[e, r + 1])
                ds_ref[e, :, pair(r)] = ds[e, r] + ds[e, r + 1] + ed_pair * ds_old[e, r]
            dacs = dacs + dacs_t.T
            ddt = ddt + ddt_t.T
            dda = _dot_exact(triu, dacs)
            ddt = ddt + dda * a
            da = jnp.sum(dda * q[e]["dt"], axis=0, keepdims=True)
            draw = jnp.where(valid, ddt * _sigmoid(dtr_ref[e] + dtb_ref[0]), 0.0)
            ddt_ref[e] = draw.astype(BF16)
            dsm_ref[e, 0, 0:1, :] += dskip_row
            dsm_ref[e, 0, 1:2, :] += da * a
            dsm_ref[e, 0, 2:3, :] += jnp.sum(draw, axis=0, keepdims=True)
            dc_ref[e] = (dc_acc + _dot(dcb, bm[e])).astype(BF16)
            db_ref[e] = (db_acc + _dot_tn(dcb, cm[e])).astype(BF16)

    grp_out = pl.BlockSpec((bsz, CHUNK, D_STATE), lambda g, j: (0, nc - 1 - j, g))
    grid = (N_GROUPS, nc)
    ride = _Ride(rider, body, 15, 7, 1, grid)
    outs = pl.pallas_call(
        ride.body, name=name, grid=grid,
        in_specs=[sp["xs"], sp["bm"], sp["cm"], sp["lane_blk"], sp["lane_blk"], sp["lane_blk"], sp["tr"], sp["xs"],
                  sp["xs"], sp["state"], sp["xs"], sp["grp_const"], sp["grp_const"], sp["grp_vec"], sp["grp_vec"]]
        + ride.in_specs,
        out_specs=[sp["xs"], sp["xs"], grp_out, grp_out, sp["lane_blk"],
                   pl.BlockSpec((bsz, 1, GW), lambda g, j: (0, 0, g)),
                   pl.BlockSpec((bsz, 1, 8, LANES), lambda g, j: (0, g, 0, 0))] + ride.out_specs,
        out_shape=[jax.ShapeDtypeStruct((bsz, t, D_SSM), BF16), jax.ShapeDtypeStruct((bsz, t, D_SSM), BF16),
                   jax.ShapeDtypeStruct((bsz, t, N_GROUPS * D_STATE), BF16),
                   jax.ShapeDtypeStruct((bsz, t, N_GROUPS * D_STATE), BF16),
                   jax.ShapeDtypeStruct((bsz, t, D_DT), BF16), jax.ShapeDtypeStruct((bsz, 1, D_SSM), F32),
                   jax.ShapeDtypeStruct((bsz, N_GROUPS, 8, LANES), F32)] + ride.out_shape,
        scratch_shapes=[pltpu.VMEM((bsz, D_STATE, GW), F32)] + ride.scratch,
        compiler_params=_params(*ride.semantics(("parallel", "arbitrary"))),
    )(xc, xc, xc, dtr, dt, acs, tr, z, ypre, sprev, dyn, dtb, alog, dskip, normw, *ride.args)
    return outs[:7], outs[7:]


def _input_grad(dhn, h0, w, dres, seq, *, name):
    bsz, t, d = h0.shape
    nc = t // CHUNK

    def body(dy_ref, h_ref, w_ref, dres_ref, gx_ref, head_ref, dw_ref):
        j = pl.program_id(0)

        @pl.when(j == 0)
        def _():
            dw_ref[...] = jnp.zeros_like(dw_ref)

        for e in range(bsz):
            x, dyv = h_ref[e], dy_ref[e]
            r = lax.rsqrt(jnp.mean(x * x, axis=-1, keepdims=True) + EPS)
            g = dyv * w_ref[...]
            dx = r * (g - x * (r * r) * jnp.mean(g * x, axis=-1, keepdims=True)) + dres_ref[e]
            dw_ref[...] += jnp.sum(dyv * x * r, axis=0, keepdims=True)
            gx_ref[e] = dx

        @pl.when(j == 0)
        def _():
            head_ref[...] = gx_ref[...]

    row = pl.BlockSpec((bsz, CHUNK, d), lambda j: (0, j, 0))
    return pl.pallas_call(
        body, name=name, grid=(nc,),
        in_specs=[row, row, pl.BlockSpec((1, d), lambda j: (0, 0)), row],
        out_specs=[pl.BlockSpec((bsz, CHUNK, d), lambda j: (0, jnp.maximum(j - 1, 0), 0)),
                   pl.BlockSpec((bsz, CHUNK, d), lambda j: (0, 0, 0)), pl.BlockSpec((1, d), lambda j: (0, 0))],
        out_shape=[jax.ShapeDtypeStruct((bsz, seq, d), F32), jax.ShapeDtypeStruct((bsz, CHUNK, d), F32),
                   jax.ShapeDtypeStruct((1, d), F32)],
        compiler_params=_params("arbitrary"),
    )(dhn, h0, w, dres)


def _remote(src, dst, send_sem, recv_sem, dev):
    return pltpu.make_async_remote_copy(src_ref=src, dst_ref=dst, send_sem=send_sem, recv_sem=recv_sem,
                                        device_id=dev, device_id_type=MESH)


def _position():
    return lax.axis_index("x"), lax.axis_index("y"), lax.axis_index("c")


def _other_chips(pos):
    x, y, _ = pos
    return [(1 - x, y), (x, 1 - y), (1 - x, 1 - y)]


class _Gather:
    def __init__(self, arrs):
        n = len(arrs)
        self.args, self.n_in, self.n_out = list(arrs), n, n
        self.split = [a.ndim == 2 and a.shape[1] % (2 * LANES) == 0 for a in arrs]
        self.out_shape = [jax.ShapeDtypeStruct((4,) + a.shape, a.dtype) for a in arrs]
        self.scratch = [pltpu.SemaphoreType.DMA((3 * n,)), pltpu.SemaphoreType.DMA((3 * n,)),
                        pltpu.SemaphoreType.DMA((n,)), pltpu.SemaphoreType.DMA((3 * n,)),
                        pltpu.SemaphoreType.DMA((3 * n,))]

    def _copies(self, pos, ins, outs, sems):
        send_sems, recv_sems, loc_sems, pass_send_sems, pass_recv_sems = sems
        x, y, c = pos
        me, sibling = 2 * x + y, (x, y, 1 - c)
        local = [pltpu.make_async_copy(ins[i], outs[i].at[me], loc_sems.at[i]) for i in range(self.n_in)]
        sends, recvs, passes, pass_recvs = [], [], [], []
        for i in range(self.n_in):
            half = self.args[i].shape[1] // 2 if self.split[i] else None
            for k, (px, py) in enumerate(_other_chips(pos)):
                them = 2 * px + py
                sems_k = (send_sems.at[3 * i + k], recv_sems.at[3 * i + k], (px, py, c))
                if half is None:
                    sends.append(_remote(ins[i], outs[i].at[me], *sems_k))
                    recvs.append(_remote(ins[i], outs[i].at[them], *sems_k))
                    passes.append(None)
                    continue
                mine = pl.ds(pl.multiple_of(c * half, LANES), half)
                other = pl.ds(pl.multiple_of((1 - c) * half, LANES), half)
                sends.append(_remote(ins[i].at[:, mine], outs[i].at[me, :, mine], *sems_k))
                recvs.append(_remote(ins[i].at[:, mine], outs[i].at[them, :, mine], *sems_k))
                pass_k = (pass_send_sems.at[3 * i + k], pass_recv_sems.at[3 * i + k], sibling)
                passes.append(_remote(outs[i].at[them, :, mine], outs[i].at[them, :, mine], *pass_k))
                pass_recvs.append(_remote(outs[i].at[them, :, other], outs[i].at[them, :, other], *pass_k))
        return local, sends, recvs, passes, pass_recvs

    def start(self, pos, ins, outs, sems):
        local, sends = self._copies(pos, ins, outs, sems)[:2]
        for cp in local + sends:
            cp.start()

    def relay(self, pos, ins, outs, sems):
        _, _, recvs, passes, _ = self._copies(pos, ins, outs, sems)
        for cp, onward in zip(recvs, passes):
            if onward is not None:
                cp.wait_recv()
                onward.start()

    def finish(self, pos, ins, outs, sems):
        local, sends, recvs, passes, pass_recvs = self._copies(pos, ins, outs, sems)
        for cp, onward in zip(recvs, passes):
            if onward is None:
                cp.wait_recv()
        for cp in pass_recvs:
            cp.wait_recv()
        for cp in sends + [p for p in passes if p is not None]:
            cp.wait_send()
        for cp in local:
            cp.wait()


class _Exchange:
    FLIPS = [(fx, fy, fc) for fx in (0, 1) for fy in (0, 1) for fc in (0, 1)][1:]

    def __init__(self, big, small=None):
        n = len(big)
        self.n_big, self.has_small = n, small is not None
        self.args = list(big) + ([small] if self.has_small else [])
        self.n_in = self.n_out = len(self.args)
        self.out_shape = [jax.ShapeDtypeStruct(a.shape, a.dtype) for a in big]
        self.scratch = [pltpu.SemaphoreType.DMA((max(3 * n, 1),)), pltpu.SemaphoreType.DMA((max(3 * n, 1),)),
                        pltpu.SemaphoreType.DMA((n + 1,))]
        if self.has_small:
            self.out_shape.append(jax.ShapeDtypeStruct((8,) + small.shape, small.dtype))
            self.scratch += [pltpu.SemaphoreType.DMA((7,)), pltpu.SemaphoreType.DMA((7,))]

    def _copies(self, pos, ins, outs, sems):
        x, y, c = pos
        me, me8 = 2 * x + y, 4 * x + 2 * y + c
        local, sends, recvs = [], [], []
        for i in range(self.n_big):
            local.append(pltpu.make_async_copy(ins[i].at[me], outs[i].at[me], sems[2].at[i]))
            for k, (px, py) in enumerate(_other_chips(pos)):
                sems_k = (sems[0].at[3 * i + k], sems[1].at[3 * i + k], (px, py, c))
                sends.append(_remote(ins[i].at[2 * px + py], outs[i].at[me], *sems_k))
                recvs.append(_remote(ins[i].at[me], outs[i].at[2 * px + py], *sems_k))
        if self.has_small:
            small, landed = ins[self.n_big], outs[self.n_big]
            local.append(pltpu.make_async_copy(small, landed.at[me8], sems[2].at[self.n_big]))
            for k, (fx, fy, fc) in enumerate(self.FLIPS):
                peer = (x ^ fx, y ^ fy, c ^ fc)
                sems_k = (sems[3].at[k], sems[4].at[k], peer)
                sends.append(_remote(small, landed.at[me8], *sems_k))
                recvs.append(_remote(small, landed.at[4 * peer[0] + 2 * peer[1] + peer[2]], *sems_k))
        return local, sends, recvs, [None] * len(recvs), []

    start = _Gather.start
    relay = _Gather.relay
    finish = _Gather.finish


class _Swap:
    def __init__(self, arrs):
        n = len(arrs)
        self.args, self.n_in, self.n_out = list(arrs), n, n
        self.out_shape = [jax.ShapeDtypeStruct(a.shape, a.dtype) for a in arrs]
        self.scratch = [pltpu.SemaphoreType.DMA((n,)), pltpu.SemaphoreType.DMA((n,))]

    def _copies(self, pos, ins, outs, sems):
        x, y, c = pos
        both = [_remote(ins[i], outs[i], sems[0].at[i], sems[1].at[i], (x, y, 1 - c)) for i in range(self.n_in)]
        return [], both, both, [None] * len(both), []

    start = _Gather.start
    relay = _Gather.relay
    finish = _Gather.finish


def _comm(rider, *, name):
    a, b = rider.n_in, rider.n_in + rider.n_out

    def body(*refs):
        pos = _position()
        rider.start(pos, refs[:a], refs[a:b], refs[b:])
        rider.relay(pos, refs[:a], refs[a:b], refs[b:])
        rider.finish(pos, refs[:a], refs[a:b], refs[b:])

    return pl.pallas_call(body, name=name, in_specs=[ANY] * rider.n_in, out_specs=[ANY] * rider.n_out,
                          out_shape=rider.out_shape, scratch_shapes=rider.scratch)(*rider.args)


class _Ride:
    RELAY_AT = 0.8

    def __init__(self, rider, body, n_in, n_out, n_scratch, grid):
        self.rider = rider
        self.args = rider.args if rider else []
        self.in_specs = [ANY] * rider.n_in if rider else []
        self.out_specs = [ANY] * rider.n_out if rider else []
        self.out_shape = rider.out_shape if rider else []
        self.scratch = rider.scratch if rider else []
        self.body = self._wrap(body, n_in, n_out, n_scratch, grid) if rider else body

    def semantics(self, sem):
        return ("arbitrary",) * len(sem) if self.rider else sem

    def _wrap(self, body, n_in, n_out, n_scratch, grid):
        rider = self.rider
        a = n_in
        b = a + rider.n_in
        c = b + n_out
        d = c + rider.n_out
        e = d + n_scratch

        def wrapped(*refs):
            pos = _position()
            ids = [pl.program_id(i) for i in range(len(grid))]
            step, total = 0, 1
            for i, g in zip(ids, grid):
                step, total = step * g + i, total * g

            @pl.when(step == 0)
            def _():
                rider.start(pos, refs[a:b], refs[c:d], refs[e:])

            body(*refs[:a], *refs[b:c], *refs[d:e])

            @pl.when(step == int(self.RELAY_AT * (total - 1)))
            def _():
                rider.relay(pos, refs[a:b], refs[c:d], refs[e:])

            @pl.when(step == total - 1)
            def _():
                rider.finish(pos, refs[a:b], refs[c:d], refs[e:])

        return wrapped


def _elementwise_tiles(r, c):
    if r % 8 == 0 and r * c > 65536:
        tm = _pick(r, (256, 128, 64, 16, 8))
        return (tm, c), r // tm, lambda i: (i, 0)
    if r % 8 and c % 256 == 0 and r * c > 65536:
        return (r, 256), c // 256, lambda i: (0, i)
    return (r, c), 1, lambda i: (0, 0)


def _chip_sum(landed, *, name):
    _, r, c = landed.shape
    blk, steps, at = _elementwise_tiles(r, c)

    def body(land_ref, o_ref):
        acc = land_ref[0].astype(F32)
        for jchip in range(1, 4):
            acc = acc + land_ref[jchip].astype(F32)
        o_ref[...] = acc

    return pl.pallas_call(
        body, name=name, grid=(steps,), in_specs=[pl.BlockSpec((4,) + blk, lambda i: (0,) + at(i))],
        out_specs=pl.BlockSpec(blk, at), out_shape=jax.ShapeDtypeStruct((r, c), F32),
        compiler_params=_params("parallel"),
    )(landed)


def _device_sum(parts, *, name):
    _, r, c = parts.shape

    def body(p_ref, o_ref):
        acc = p_ref[0]
        for d in range(1, 8):
            acc = acc + p_ref[d]
        o_ref[...] = acc

    return pl.pallas_call(body, name=name, out_shape=jax.ShapeDtypeStruct((r, c), F32))(parts)


def _adamw_math(w, g, m, v):
    m = ADAM_B1 * m + (1.0 - ADAM_B1) * g
    v = ADAM_B2 * v + (1.0 - ADAM_B2) * (g * g)
    m_hat = m / (1.0 - ADAM_B1 ** ADAM_STEP)
    v_hat = v / (1.0 - ADAM_B2 ** ADAM_STEP)
    return -ADAM_LR * (m_hat / (jnp.sqrt(v_hat) + ADAM_EPS) + ADAM_WD * w), m, v


def _adamw(w, g_parts, m, v, *, name):
    r, c = w.shape
    shape, steps, at = _elementwise_tiles(r, c)
    n_g = len(g_parts)

    def body(*refs):
        w_ref, m_ref, v_ref = refs[n_g:n_g + 3]
        g_ref, d_ref, nm_ref, nv_ref = refs[n_g + 3:]
        g = refs[0][...]
        for p in refs[1:n_g]:
            g = g + p[...]
        g_ref[...] = g
        d_ref[...], nm_ref[...], nv_ref[...] = _adamw_math(w_ref[...], g, m_ref[...], v_ref[...])

    blk = pl.BlockSpec(shape, at)
    return pl.pallas_call(
        body, name=name, grid=(steps,), in_specs=[blk] * (n_g + 3), out_specs=[blk] * 4,
        out_shape=[jax.ShapeDtypeStruct((r, c), F32)] * 4, compiler_params=_params("parallel"),
    )(*g_parts, w, m, v)


def _pad_heads(v):
    return jnp.pad(v.reshape(N_GROUPS, 1, HPG), ((0, 0), (0, 0), (0, LANES - HPG)))


def _unpad_heads(v):
    return v[:, :HPG].reshape(1, N_HEADS)


_SMALL_EARLY = [("pool_w", (512, 128)), ("pool_scale", (1, 512)), ("conv_w", (4, D_XBC)), ("conv_b", (1, D_XBC)),
                ("dt_bias", (1, N_HEADS)), ("a_log", (1, N_HEADS)), ("d_skip", (1, N_HEADS)), ("ssm_norm_w", (1, D_SSM)),
                ("norm_ffn_w", (1, 1024)), ("norm_f_w", (1, 1024))]
_SMALL_LATE = [("norm_mix_w", (1, 1024)), ("meta", (N_META, 1024)), ("loss", (1, 1))]


def _pack_small(grads, layout):
    rows = []
    for nm, shape in layout:
        flat = grads[nm].reshape(-1)
        rows.append(jnp.pad(flat, (0, (-flat.size) % LANES)).reshape(-1, LANES))
    packed = jnp.concatenate(rows, axis=0)
    return jnp.pad(packed, ((0, (-packed.shape[0]) % 8), (0, 0)))


def _unpack_small(packed, layout):
    out, r0 = {}, 0
    for nm, shape in layout:
        size = shape[0] * shape[1]
        nrow = -(-size // LANES)
        out[nm] = packed[r0:r0 + nrow].reshape(-1)[:size].reshape(shape)
        r0 += nrow
    return out


def kernel(x, meta, norm_mix_w, w_in, pool_w, pool_scale, conv_w, conv_b, dt_bias, a_log, d_skip, ssm_norm_w, w_out, norm_ffn_w, w_ff1, w_ff2, norm_f_w, loss_target, m_meta, m_norm_mix_w, m_w_in, m_pool_w, m_pool_scale, m_conv_w, m_conv_b, m_dt_bias, m_a_log, m_d_skip, m_ssm_norm_w, m_w_out, m_norm_ffn_w, m_w_ff1, m_w_ff2, m_norm_f_w, v_meta, v_norm_mix_w, v_w_in, v_pool_w, v_pool_scale, v_conv_w, v_conv_b, v_dt_bias, v_a_log, v_d_skip, v_ssm_norm_w, v_w_out, v_norm_ffn_w, v_w_ff1, v_w_ff2, v_norm_f_w):
    bsz, seq, d = x.shape
    t = seq + CHUNK
    n = bsz * t
    chip = 2 * lax.axis_index("x") + lax.axis_index("y")
    d_in = w_in.shape[2] * 4

    g_conv, g_meta = _comm(_Gather([conv_w[0], meta]), name="gather_small")
    convw = g_conv.transpose(1, 0, 2).reshape(CONV_W, D_XBC)
    meta_full = g_meta.transpose(1, 0, 2).reshape(N_META, d)
    (h0, hn1), (g_in,) = _embed_norm(x, meta_full, norm_mix_w, name="embed_norm",
                                     rider=_Gather([w_in[0].T.astype(BF16)]))
    h0f, hn1 = h0.reshape(n, d), hn1.reshape(n, d)
    late_weights = _Gather([w_out[0].astype(BF16), w_ff1[0].astype(BF16), w_ff2[0].astype(BF16)])
    win = g_in.reshape(d_in, d)
    wu, wz = win[:D_POOL], win[D_POOL:D_POOL + D_SSM]
    wx = win[D_POOL + D_SSM:D_POOL + D_SSM + D_XBC]
    wdt = jnp.pad(win[D_POOL + D_SSM + D_XBC:].reshape(N_GROUPS, HPG, d),
                  ((0, 0), (0, LANES - HPG), (0, 0))).reshape(D_DT, d)
    dtb, alog = _pad_heads(dt_bias), _pad_heads(a_log)
    dskip = jnp.repeat(d_skip, HEAD_DIM, axis=1)
    poolw = pool_w[0]

    u, z, dtr, dt_, acs_, tr_ = _proj_uz_dt(hn1, wu, wz, wdt, dtb, alog, t // CHUNK, name="proj_uzdt")
    xbc, xc = _proj_conv(hn1, wx, convw, conv_b, name="proj_xbc")
    ypool = _pool_fwd(u.reshape(bsz, t, D_POOL), poolw, pool_scale, name="pool_fwd")
    xbc3 = xbc.reshape(bsz, t, D_XBC)
    xc = xc.reshape(bsz, t, D_XBC)
    z3, dtr3 = z.reshape(bsz, t, D_SSM), dtr.reshape(bsz, t, D_DT)
    dt3, acs3 = dt_.reshape(bsz, t, D_DT), acs_.reshape(bsz, t, D_DT)
    tr3 = tr_.reshape(bsz, t // CHUNK, N_GROUPS, 16, LANES)
    (yn, ypre, sprev), (g_out, g_ff1, g_ff2) = _ssd_fwd(xc, dt3, acs3, tr3, z3, dskip, ssm_norm_w, name="ssd_fwd",
                                                        rider=late_weights)
    wo = g_out.reshape(D_POOL + D_SSM, d)
    wo_p, wo_s = wo[:D_POOL], wo[D_POOL:]
    w1 = g_ff1
    w2 = g_ff2.reshape(D_FF, d)
    ypool_f, yn_f = ypool.reshape(n, D_POOL), yn.reshape(n, D_SSM)
    add = lambda r, e: r + e
    h1, hn2 = _mm([ypool_f, yn_f], [wo_p, wo_s], name="out_proj", post=add, extras=(h0f,), norm_w=norm_ffn_w)
    act = _mm(hn2, w1, name="ff1", out_dtype=BF16)
    relu2 = lambda a: jnp.square(jnp.maximum(a, 0))
    h2 = _mm(act, w2, name="ff2", pre=relu2, post=add, extras=(h1,))
    dh2, dh2b, loss_acc, d_norm_f = _final_norm_loss(h2.reshape(bsz, t, d), loss_target, norm_f_w.reshape(1, d),
                                                     name="loss")

    dh2f, dh2bf = dh2.reshape(n, d), dh2b.reshape(n, d)
    dact = _mm(dh2bf, w2, name="ff2_bwd", nt=True, post=lambda r, a: r * (2.0 * jnp.maximum(a, 0).astype(F32)),
               extras=(act,), out_dtype=BF16)
    d_w2 = _mm_tn(act, dh2bf, name="ff2_dw", tk=2048, tn=1024, pre=relu2)
    d_w1 = _mm_tn(hn2, dact, name="ff1_dw", tk=1024, tn=2048, slab=D_FF // 4)
    dh1, dh1b, d_norm_ffn = _mm_rms_bwd(dact, w1, h1, norm_ffn_w, dh2f, name="ff1_bwd")
    dypool, dyn = _mm_fanout(dh1b, [wo_p, wo_s], name="out_proj_bwd")
    d_wo = _mm_tn_cat([ypool_f, yn_f], dh1b, name="out_proj_dw")
    big_late = [d_wo.reshape(4, (D_POOL + D_SSM) // 4, d),
                d_w1, d_w2.reshape(4, D_FF // 4, d)]
    (dz, dxs, dbm, dcm, ddtr, d_nw, d_heads), landed_late = _ssd_bwd(
        xc, dtr3, dt3, acs3, tr3, z3, ypre, sprev, dyn.reshape(bsz, t, D_SSM), dtb, alog, dskip, ssm_norm_w, name="ssd_bwd",
        rider=_Exchange(big_late))
    dxbc, d_convwb = _conv_bwd(xbc3, dxs, dbm, dcm, convw, conv_b, name="conv_bwd")
    du, d_poolw, d_poolsc = _pool_bwd(u.reshape(bsz, t, D_POOL), dypool.reshape(bsz, t, D_POOL), poolw, pool_scale,
                                      name="pool_bwd")
    duf, dzf, dxbcf, ddtrf = du.reshape(n, D_POOL), dz.reshape(n, D_SSM), dxbc.reshape(n, D_XBC), ddtr.reshape(n, D_DT)
    heads = jnp.sum(d_heads, axis=0)
    small_early = _pack_small({
        "pool_w": d_poolw, "pool_scale": d_poolsc,
        "conv_w": jnp.sum(d_convwb[:, :CONV_W], axis=0), "conv_b": jnp.sum(d_convwb[:, CONV_W:CONV_W + 1], axis=0),
        "dt_bias": _unpad_heads(heads[:, 2]), "a_log": _unpad_heads(heads[:, 1]), "d_skip": _unpad_heads(heads[:, 0]),
        "ssm_norm_w": jnp.sum(d_nw, axis=0), "norm_ffn_w": d_norm_ffn, "norm_f_w": d_norm_f}, _SMALL_EARLY)
    d_wuzdt = _mm_tn_cat([duf, dzf, ddtrf], hn1, name="proj_uzdt_dw")
    d_wx, (early_all,) = _mm_tn(dxbcf, hn1, name="proj_xbc_dw", tk=1280, tn=1024, rider=_Exchange([], small_early))
    d_wdt = d_wuzdt[D_POOL + D_SSM:].reshape(N_GROUPS, LANES, d)[:, :HPG].reshape(N_HEADS, d)
    d_win = jnp.concatenate([d_wuzdt[:D_POOL + D_SSM], d_wx, d_wdt], axis=0)
    big_in = d_win.reshape(4, d_in // 4, d)
    dhn1, (landed_in,) = _mm([duf, dzf, dxbcf, ddtrf], [wu, wz, wx, wdt], name="proj_bwd",
                             rider=_Exchange([big_in]))
    grad_x, d_head_rows, d_norm_mix = _input_grad(
        dhn1.reshape(bsz, t, d), h0, norm_mix_w, dh1.reshape(bsz, t, d), seq, name="input_grad")

    landed = [landed_in] + list(landed_late)
    small_late = _pack_small({"norm_mix_w": d_norm_mix, "meta": jnp.sum(d_head_rows[:, PAD:], axis=0),
                              "loss": loss_acc[0:1, 0:1]}, _SMALL_LATE)
    (late_all,) = _comm(_Exchange([], small_late), name="exchange_small")
    mine = [_chip_sum(l, name=f"chip_sum_{i}") for i, l in enumerate(landed)]
    theirs = _comm(_Swap(mine), name="swap_cores")
    gsmall = {**_unpack_small(_device_sum(early_all, name="device_sum_early"), _SMALL_EARLY),
              **_unpack_small(_device_sum(late_all, name="device_sum_late"), _SMALL_LATE)}
    gsmall["conv_w"] = lax.dynamic_slice_in_dim(gsmall["conv_w"], chip * (D_XBC // 4), D_XBC // 4, axis=1)
    gsmall["meta"] = lax.dynamic_slice_in_dim(gsmall["meta"], chip * (d // 4), d // 4, axis=1)
    loss = gsmall["loss"][0, 0]

    given = dict(meta=(meta, m_meta, v_meta), norm_mix_w=(norm_mix_w, m_norm_mix_w, v_norm_mix_w),
                 w_in=(w_in, m_w_in, v_w_in), pool_w=(pool_w, m_pool_w, v_pool_w),
                 pool_scale=(pool_scale, m_pool_scale, v_pool_scale), conv_w=(conv_w, m_conv_w, v_conv_w),
                 conv_b=(conv_b, m_conv_b, v_conv_b), dt_bias=(dt_bias, m_dt_bias, v_dt_bias),
                 a_log=(a_log, m_a_log, v_a_log), d_skip=(d_skip, m_d_skip, v_d_skip),
                 ssm_norm_w=(ssm_norm_w, m_ssm_norm_w, v_ssm_norm_w), w_out=(w_out, m_w_out, v_w_out),
                 norm_ffn_w=(norm_ffn_w, m_norm_ffn_w, v_norm_ffn_w), w_ff1=(w_ff1, m_w_ff1, v_w_ff1),
                 w_ff2=(w_ff2, m_w_ff2, v_w_ff2), norm_f_w=(norm_f_w, m_norm_f_w, v_norm_f_w))
    big_names = ["w_in", "w_out", "w_ff1", "w_ff2"]
    results = {}
    for nm, (w, m, v) in given.items():
        if nm in big_names:
            i = big_names.index(nm)
            parts, shape2 = (mine[i], theirs[i]), mine[i].shape
        else:
            parts, shape2 = (gsmall[nm],), gsmall[nm].shape
        if nm == "w_in":
            outs = _adamw(w[0].T, parts, m[0].T, v[0].T, name=f"adamw_{nm}")
            results[nm] = [o.T[None] for o in outs]
        else:
            outs = _adamw(w.reshape(shape2), parts, m.reshape(shape2), v.reshape(shape2), name=f"adamw_{nm}")
            results[nm] = [o.reshape(w.shape) for o in outs]
    order = list(given)
    return (loss, grad_x, *[results[nm][0] for nm in order], *[results[nm][1] for nm in order],
            *[results[nm][2] for nm in order], *[results[nm][3] for nm in order])
```

```python
import jax
import jax.numpy as jnp
from jax import lax
from jax.experimental import pallas as pl
from jax.experimental.pallas import tpu as pltpu

F32 = jnp.float32
BF16 = jnp.bfloat16
MESH = pl.DeviceIdType.MESH
ANY = pl.BlockSpec(memory_space=pl.ANY)

D_MODEL = 1024
N_META = 16
CHUNK = 128
PAD = CHUNK - N_META
POOL_WINDOWS = (2, 4, 8, 16)
D_POOL = 512
POOL_GROUP = 128
D_SSM = 1536
N_HEADS = 24
N_GROUPS = 4
HPG = 6
HEAD_DIM = 64
D_STATE = 128
GW = HPG * HEAD_DIM
D_XBC = D_SSM + 2 * N_GROUPS * D_STATE
D_DT = N_GROUPS * 128
D_FF = 4096
CONV_W = 4
EPS = 1e-5
LANES = 128
VMEM_LIMIT = 56 * 1024 * 1024

ADAM_LR, ADAM_B1, ADAM_B2, ADAM_EPS, ADAM_WD, ADAM_STEP = 0.001, 0.9, 0.999, 1e-08, 0.01, 10


def _params(*sem):
    return pltpu.CompilerParams(dimension_semantics=sem, vmem_limit_bytes=VMEM_LIMIT)


def _pick(n, cands):
    for c in cands:
        if n % c == 0:
            return c
    raise ValueError(f"no block size for {n}")


def _dot(a, b):
    return jnp.dot(a.astype(BF16), b.astype(BF16), preferred_element_type=F32)


def _dot_nt(a, b):
    return lax.dot_general(a.astype(BF16), b.astype(BF16), (((1,), (1,)), ((), ())), preferred_element_type=F32)


def _dot_tn(a, b):
    return lax.dot_general(a.astype(BF16), b.astype(BF16), (((0,), (0,)), ((), ())), preferred_element_type=F32)


def _dot_exact(mask, x, terms=3):
    m = mask.astype(BF16)
    dot = lambda t: jnp.dot(m, t, preferred_element_type=F32)
    hi = x.astype(BF16)
    r1 = x - hi.astype(F32)
    mid = r1.astype(BF16)
    if terms == 2:
        return dot(hi) + dot(mid)
    lo = (r1 - mid.astype(F32)).astype(BF16)
    return dot(hi) + dot(mid) + dot(lo)


def _sigmoid(x):
    return 1.0 / (1.0 + jnp.exp(-x))


def _softplus(x):
    return jnp.maximum(x, 0.0) + jnp.log1p(jnp.exp(-jnp.abs(x)))


def _sum_all(x):
    return jnp.sum(jnp.sum(x, axis=1, keepdims=True), axis=0, keepdims=True)


ROW_TILES = (1056, 768, 704, 512, 384, 256, 128)
TILE_BUDGET = 28 * 1024 * 1024


def _row_tile(n, bytes_per_row, fixed_bytes, budget=TILE_BUDGET):
    for tm in ROW_TILES:
        if n % tm == 0 and 2 * (tm * bytes_per_row + fixed_bytes) <= budget:
            return tm
    raise ValueError(f"no row tile for {n}")


WIDE_BUDGET = 38 * 1024 * 1024


def _mm(a, w, *, name, tn=512, nt=False, pre=None, post=None, extras=(), out_dtype=F32, norm_w=None, rider=None):
    assert norm_w is None or (rider is None and out_dtype == F32)
    a_list = list(a) if isinstance(a, (list, tuple)) else [a]
    w_list = list(w) if isinstance(w, (list, tuple)) else [w]
    n_a, n_ex = len(a_list), len(extras)
    n = a_list[0].shape[0]
    shard = w_list[0].shape[2] if w_list[0].ndim == 3 else None
    assert shard is None or (not nt and n_a == 1 and shard % tn == 0)
    m = w_list[0].shape[0] * shard if shard else w_list[0].shape[0] if nt else w_list[0].shape[1]
    tn = min(tn, m)
    size = lambda dt: jnp.dtype(dt).itemsize
    per_row = (sum(x.shape[1] * size(x.dtype) for x in a_list) + m * size(out_dtype)
               + sum(m * size(e.dtype) for e in extras) + (2 * m if norm_w is not None else 0))
    tm = _row_tile(n, per_row, sum(x.size * size(x.dtype) for x in w_list) // 2, WIDE_BUDGET)
    n_norm = 0 if norm_w is None else 1

    def body(*refs):
        a_refs, w_refs, ex_refs = refs[:n_a], refs[n_a:2 * n_a], refs[2 * n_a:2 * n_a + n_ex]
        o_ref = refs[2 * n_a + n_ex + n_norm]
        avs = [(a_ref[...] if pre is None else pre(a_ref[...])).astype(BF16) for a_ref in a_refs]
        for c0 in range(0, m, tn):
            r = None
            for av, w_ref in zip(avs, w_refs):
                if shard:
                    term = _dot(av, w_ref[c0 // shard, :, c0 % shard:c0 % shard + tn])
                else:
                    term = _dot_nt(av, w_ref[c0:c0 + tn, :]) if nt else _dot(av, w_ref[:, c0:c0 + tn])
                r = term if r is None else r + term
            if post is not None:
                r = post(r, *[e[:, c0:c0 + tn] for e in ex_refs])
            o_ref[:, c0:c0 + tn] = r.astype(out_dtype)
        if n_norm:
            x = o_ref[...]
            scale = lax.rsqrt(jnp.mean(x * x, axis=-1, keepdims=True) + EPS)
            refs[2 * n_a + n_ex + 2][...] = (x * scale * refs[2 * n_a + n_ex][...]).astype(BF16)

    a_specs = [pl.BlockSpec((tm, x.shape[1]), lambda i: (i, 0)) for x in a_list]
    w_specs = [pl.BlockSpec(x.shape, lambda i, nd=x.ndim: (0,) * nd, pipeline_mode=pl.Buffered(1)) for x in w_list]
    blk = pl.BlockSpec((tm, m), lambda i: (i, 0))
    vec = [pl.BlockSpec((1, m), lambda i: (0, 0))] * n_norm
    grid = (n // tm,)
    ride = _Ride(rider, body, 2 * n_a + n_ex + n_norm, 1 + n_norm, 0, grid)
    outs = pl.pallas_call(
        ride.body, name=name, grid=grid,
        in_specs=a_specs + w_specs + [blk] * n_ex + vec + ride.in_specs,
        out_specs=[blk] * (1 + n_norm) + ride.out_specs,
        out_shape=[jax.ShapeDtypeStruct((n, m), out_dtype)] + [jax.ShapeDtypeStruct((n, m), BF16)] * n_norm + ride.out_shape,
        scratch_shapes=ride.scratch, compiler_params=_params(*ride.semantics(("parallel",))),
    )(*a_list, *w_list, *extras, *([norm_w] * n_norm), *ride.args)
    if n_norm:
        return outs[0], outs[1]
    return (outs[0], outs[1:]) if rider else outs[0]


def _mm_fanout(a, ws, *, name, tn=512):
    n, k = a.shape
    ms = [w.shape[0] for w in ws]
    tm = _row_tile(n, k * 2 + 4 * sum(ms), sum(w.size for w in ws), WIDE_BUDGET)
    n_w = len(ws)

    def body(a_ref, *refs):
        av = a_ref[...]
        for w_ref, o_ref, m in zip(refs[:n_w], refs[n_w:], ms):
            for c0 in range(0, m, tn):
                o_ref[:, c0:c0 + tn] = _dot_nt(av, w_ref[c0:c0 + tn, :])

    return pl.pallas_call(
        body, name=name, grid=(n // tm,),
        in_specs=[pl.BlockSpec((tm, k), lambda i: (i, 0))]
        + [pl.BlockSpec(w.shape, lambda i: (0, 0), pipeline_mode=pl.Buffered(1)) for w in ws],
        out_specs=[pl.BlockSpec((tm, m), lambda i: (i, 0)) for m in ms],
        out_shape=[jax.ShapeDtypeStruct((n, m), F32) for m in ms],
        compiler_params=_params("parallel"),
    )(a, *ws)


def _mm_tn(a, g, *, name, tk, tn, pre=None, slab=None, rider=None):
    n, k = a.shape
    m = g.shape[1]
    tk, tn = min(tk, k), min(tn, m)
    tm = _row_tile(n, tk * jnp.dtype(a.dtype).itemsize + tn * jnp.dtype(g.dtype).itemsize, tk * tn * 4)
    steps = n // tm

    def body(a_ref, g_ref, o_ref, acc_ref):
        r = pl.program_id(2)

        @pl.when(r == 0)
        def _():
            acc_ref[...] = jnp.zeros_like(acc_ref)

        av = a_ref[...]
        if pre is not None:
            av = pre(av)
        if slab:
            for s in range(tn // slab):
                acc_ref[s] += _dot_tn(av, g_ref[:, s * slab:(s + 1) * slab])
        else:
            acc_ref[...] += _dot_tn(av, g_ref[...])

        @pl.when(r == steps - 1)
        def _():
            o_ref[...] = acc_ref[...].astype(BF16)

    if slab:
        block, out_spec = (tn // slab, tk, slab), pl.BlockSpec((tn // slab, tk, slab), lambda i, j, r: (j, i, 0))
        out_shape = jax.ShapeDtypeStruct((m // slab, k, slab), BF16)
    else:
        block, out_spec = (tk, tn), pl.BlockSpec((tk, tn), lambda i, j, r: (i, j))
        out_shape = jax.ShapeDtypeStruct((k, m), BF16)
    grid = (k // tk, m // tn, steps)
    ride = _Ride(rider, body, 2, 1, 1, grid)
    outs = pl.pallas_call(
        ride.body, name=name, grid=grid,
        in_specs=[pl.BlockSpec((tm, tk), lambda i, j, r: (r, i)), pl.BlockSpec((tm, tn), lambda i, j, r: (r, j))]
        + ride.in_specs,
        out_specs=[out_spec] + ride.out_specs, out_shape=[out_shape] + ride.out_shape,
        scratch_shapes=[pltpu.VMEM(block, F32)] + ride.scratch,
        compiler_params=_params(*ride.semantics(("parallel", "parallel", "arbitrary"))),
    )(a, g, *ride.args)
    return (outs[0], outs[1:]) if rider else outs[0]


def _mm_tn_cat(a_list, g, *, name):
    n, m = g.shape
    ks = [a.shape[1] for a in a_list]
    size = lambda x: jnp.dtype(x.dtype).itemsize
    tm = _row_tile(n, sum(a.shape[1] * size(a) for a in a_list) + m * size(g), sum(ks) * m * 4)
    steps, n_a = n // tm, len(a_list)

    def body(*refs):
        g_ref, o_ref, acc_ref = refs[n_a], refs[n_a + 1], refs[n_a + 2]
        r = pl.program_id(0)

        @pl.when(r == 0)
        def _():
            acc_ref[...] = jnp.zeros_like(acc_ref)

        gv, k0 = g_ref[...], 0
        for a_ref, k in zip(refs[:n_a], ks):
            acc_ref[k0:k0 + k, :] += _dot_tn(a_ref[...], gv)
            k0 += k

        @pl.when(r == steps - 1)
        def _():
            o_ref[...] = acc_ref[...].astype(BF16)

    return pl.pallas_call(
        body, name=name, grid=(steps,),
        in_specs=[pl.BlockSpec((tm, k), lambda r: (r, 0)) for k in ks] + [pl.BlockSpec((tm, m), lambda r: (r, 0))],
        out_specs=pl.BlockSpec((sum(ks), m), lambda r: (0, 0)),
        out_shape=jax.ShapeDtypeStruct((sum(ks), m), BF16),
        scratch_shapes=[pltpu.VMEM((sum(ks), m), F32)],
        compiler_params=_params("arbitrary"),
    )(*a_list, g)


def _mm_rms_bwd(a, w, h, w_norm, dres, *, name):
    n, k = a.shape
    d = h.shape[1]
    slabs, _, ks = w.shape
    tm = _row_tile(n, k * jnp.dtype(a.dtype).itemsize + d * (4 + 4 + 4 + 2), d * k, WIDE_BUDGET)

    def body(a_ref, w_ref, h_ref, wn_ref, dres_ref, dx_ref, dxb_ref, dw_ref):
        @pl.when(pl.program_id(0) == 0)
        def _():
            dw_ref[...] = jnp.zeros_like(dw_ref)

        dyv = None
        for s in range(slabs):
            part = _dot_nt(a_ref[:, s * ks:(s + 1) * ks], w_ref[s])
            dyv = part if dyv is None else dyv + part
        x = h_ref[...]
        r = lax.rsqrt(jnp.mean(x * x, axis=-1, keepdims=True) + EPS)
        g = dyv * wn_ref[...]
        dx = r * (g - x * (r * r) * jnp.mean(g * x, axis=-1, keepdims=True)) + dres_ref[...]
        dx_ref[...] = dx
        dxb_ref[...] = dx.astype(BF16)
        dw_ref[...] += jnp.sum(dyv * x * r, axis=0, keepdims=True)

    row = pl.BlockSpec((tm, d), lambda i: (i, 0))
    vec = pl.BlockSpec((1, d), lambda i: (0, 0))
    return pl.pallas_call(
        body, name=name, grid=(n // tm,),
        in_specs=[pl.BlockSpec((tm, k), lambda i: (i, 0)),
                  pl.BlockSpec(w.shape, lambda i: (0, 0, 0), pipeline_mode=pl.Buffered(1)), row, vec, row],
        out_specs=[row, row, vec],
        out_shape=[jax.ShapeDtypeStruct((n, d), F32), jax.ShapeDtypeStruct((n, d), BF16), jax.ShapeDtypeStruct((1, d), F32)],
        compiler_params=_params("arbitrary"),
    )(a, w, h, w_norm, dres)


def _embed_norm(x, meta, w, *, name, rider=None):
    bsz, seq, d = x.shape
    t = seq + CHUNK
    nc = t // CHUNK

    def body(x_ref, meta_ref, w_ref, h_ref, hn_ref):
        j = pl.program_id(0)
        first = jnp.concatenate([jnp.zeros((PAD, d), F32), meta_ref[...]], axis=0)
        for e in range(bsz):
            h = jnp.where(j == 0, first, x_ref[e])
            r = lax.rsqrt(jnp.mean(h * h, axis=-1, keepdims=True) + EPS)
            h_ref[e] = h
            hn_ref[e] = (h * r * w_ref[...]).astype(BF16)

    row = pl.BlockSpec((bsz, CHUNK, d), lambda j: (0, j, 0))
    grid = (nc,)
    ride = _Ride(rider, body, 3, 2, 0, grid)
    outs = pl.pallas_call(
        ride.body, name=name, grid=grid,
        in_specs=[pl.BlockSpec((bsz, CHUNK, d), lambda j: (0, jnp.maximum(j - 1, 0), 0)),
                  pl.BlockSpec((N_META, d), lambda j: (0, 0)), pl.BlockSpec((1, d), lambda j: (0, 0))] + ride.in_specs,
        out_specs=[row, row] + ride.out_specs,
        out_shape=[jax.ShapeDtypeStruct((bsz, t, d), F32), jax.ShapeDtypeStruct((bsz, t, d), BF16)] + ride.out_shape,
        scratch_shapes=ride.scratch, compiler_params=_params(*ride.semantics(("parallel",))),
    )(x, meta, w, *ride.args)
    return outs[:2], outs[2:]


def _final_norm_loss(h2, target, w, *, name):
    bsz, t, d = h2.shape
    nc = t // CHUNK

    def body(h_ref, t_ref, w_ref, dh_ref, dhb_ref, loss_ref, dw_ref):
        j = pl.program_id(0)

        @pl.when(j == 0)
        def _():
            loss_ref[...] = jnp.zeros_like(loss_ref)
            dw_ref[...] = jnp.zeros_like(dw_ref)

        wv = w_ref[...]
        for e in range(bsz):
            x = h_ref[e]
            r = lax.rsqrt(jnp.mean(x * x, axis=-1, keepdims=True) + EPS)
            diff = jnp.where(j > 0, x * r * wv - t_ref[e], 0.0)
            loss_ref[...] += _sum_all(diff * diff) * (0.5 / d)
            dy = diff * (1.0 / d)
            g = dy * wv
            dh = r * (g - x * (r * r) * jnp.mean(g * x, axis=-1, keepdims=True))
            dh_ref[e] = dh
            dhb_ref[e] = dh.astype(BF16)
            dw_ref[...] += jnp.sum(dy * x * r, axis=0, keepdims=True)

    row = pl.BlockSpec((bsz, CHUNK, d), lambda j: (0, j, 0))
    return pl.pallas_call(
        body, name=name, grid=(nc,),
        in_specs=[row, pl.BlockSpec((bsz, CHUNK, d), lambda j: (0, jnp.maximum(j - 1, 0), 0)),
                  pl.BlockSpec((1, d), lambda j: (0, 0))],
        out_specs=[row, row, pl.BlockSpec((8, LANES), lambda j: (0, 0)), pl.BlockSpec((1, d), lambda j: (0, 0))],
        out_shape=[jax.ShapeDtypeStruct((bsz, t, d), F32), jax.ShapeDtypeStruct((bsz, t, d), BF16),
                   jax.ShapeDtypeStruct((8, LANES), F32), jax.ShapeDtypeStruct((1, d), F32)],
        compiler_params=_params("arbitrary"),
    )(h2, target, w)


def _pool_masks(j, transposed):
    r = lax.broadcasted_iota(jnp.int32, (CHUNK, 2 * CHUNK), 0)
    c = lax.broadcasted_iota(jnp.int32, (CHUNK, 2 * CHUNK), 1)
    masks = []
    for w in POOL_WINDOWS:
        if transposed:
            m = (c >= r) & (c < r + w)
        else:
            s = c - CHUNK
            m = (s <= r) & (s > r - w) & (s + j * CHUNK >= 0)
        masks.append(m.astype(F32))
    return masks


POOL_TERMS = 2


def _pool_count(t_global, w):
    return jnp.clip(t_global - PAD + 1, 1, w).astype(F32)


def _pool_fwd(u, pool_w, pool_scale, *, name):
    bsz, t, _ = u.shape
    nc = t // CHUNK

    def body(prev_ref, cur_ref, pw_ref, sc_ref, o_ref):
        j = pl.program_id(0)
        masks = _pool_masks(j, False)
        tg = j * CHUNK + lax.broadcasted_iota(jnp.int32, (CHUNK, 1), 0)
        count = [_pool_count(tg, w) for w in POOL_WINDOWS]
        units = [(e, gi) for e in range(bsz) for gi in range(len(POOL_WINDOWS))]
        sl = lambda gi: pl.ds(gi * POOL_GROUP, POOL_GROUP)
        cur = {(e, gi): cur_ref[e, :, sl(gi)] for e, gi in units}
        both = {(e, gi): jnp.concatenate([prev_ref[e, :, sl(gi)], cur[e, gi]], axis=0) for e, gi in units}
        win = {(e, gi): _dot_exact(masks[gi], both[e, gi], POOL_TERMS) for e, gi in units}
        pooled = {(e, gi): win[e, gi] / count[gi] - cur[e, gi] for e, gi in units}
        mixed = {(e, gi): _dot(pooled[e, gi], pw_ref[gi]) for e, gi in units}
        for e, gi in units:
            o_ref[e, :, sl(gi)] = (mixed[e, gi] * sc_ref[:, sl(gi)]).astype(BF16)

    blk = lambda f: pl.BlockSpec((bsz, CHUNK, D_POOL), f)
    return pl.pallas_call(
        body, name=name, grid=(nc,),
        in_specs=[blk(lambda j: (0, jnp.maximum(j - 1, 0), 0)), blk(lambda j: (0, j, 0)),
                  pl.BlockSpec((4, POOL_GROUP, POOL_GROUP), lambda j: (0, 0, 0)),
                  pl.BlockSpec((1, D_POOL), lambda j: (0, 0))],
        out_specs=blk(lambda j: (0, j, 0)), out_shape=jax.ShapeDtypeStruct(u.shape, BF16),
        compiler_params=_params("parallel"),
    )(u, u, pool_w, pool_scale)


def _pool_bwd(u, dyp, pool_w, pool_scale, *, name):
    bsz, t, _ = u.shape
    nc = t // CHUNK

    def body(prev_ref, cur_ref, dy_ref, dyn_ref, pw_ref, sc_ref, du_ref, dpw_ref, dsc_ref):
        j = pl.program_id(0)

        @pl.when(j == 0)
        def _():
            dpw_ref[...] = jnp.zeros_like(dpw_ref)
            dsc_ref[...] = jnp.zeros_like(dsc_ref)

        fwd = _pool_masks(j, False)
        bwd = _pool_masks(j, True)
        tg = j * CHUNK + lax.broadcasted_iota(jnp.int32, (CHUNK, 1), 0)
        count = [_pool_count(tg, w) for w in POOL_WINDOWS]
        count_next = [_pool_count(tg + CHUNK, w) for w in POOL_WINDOWS]
        has_next = j < nc - 1
        groups = range(len(POOL_WINDOWS))
        units = [(e, gi) for e in range(bsz) for gi in groups]
        sl = lambda gi: pl.ds(gi * POOL_GROUP, POOL_GROUP)
        cur = {(e, gi): cur_ref[e, :, sl(gi)] for e, gi in units}
        both = {(e, gi): jnp.concatenate([prev_ref[e, :, sl(gi)], cur[e, gi]], axis=0) for e, gi in units}
        win = {(e, gi): _dot_exact(fwd[gi], both[e, gi], POOL_TERMS) for e, gi in units}
        pooled = {(e, gi): win[e, gi] / count[gi] - cur[e, gi] for e, gi in units}
        dy = {(e, gi): dy_ref[e, :, sl(gi)] for e, gi in units}
        mixed = {(e, gi): _dot(pooled[e, gi], pw_ref[gi]) for e, gi in units}
        dm = {(e, gi): dy[e, gi] * sc_ref[:, sl(gi)] for e, gi in units}
        dm_next = {(e, gi): jnp.where(has_next, dyn_ref[e, :, sl(gi)], 0.0) * sc_ref[:, sl(gi)] for e, gi in units}
        dpw = {(e, gi): _dot_tn(pooled[e, gi], dm[e, gi]) for e, gi in units}
        dpooled = {(e, gi): _dot_nt(dm[e, gi], pw_ref[gi]) for e, gi in units}
        dpooled_next = {(e, gi): _dot_nt(dm_next[e, gi], pw_ref[gi]) for e, gi in units}
        spread = {(e, gi): jnp.concatenate([dpooled[e, gi] / count[gi], dpooled_next[e, gi] / count_next[gi]], axis=0)
                  for e, gi in units}
        back = {(e, gi): _dot_exact(bwd[gi], spread[e, gi], POOL_TERMS) for e, gi in units}
        for e, gi in units:
            du_ref[e, :, sl(gi)] = (back[e, gi] - dpooled[e, gi]).astype(BF16)
        for gi in groups:
            dsc, dw = None, None
            for e in range(bsz):
                term = jnp.sum(dy[e, gi] * mixed[e, gi], axis=0, keepdims=True)
                dsc = term if dsc is None else dsc + term
                dw = dpw[e, gi] if dw is None else dw + dpw[e, gi]
            dsc_ref[:, sl(gi)] += dsc
            dpw_ref[gi] += dw

    blk = lambda f: pl.BlockSpec((bsz, CHUNK, D_POOL), f)
    return pl.pallas_call(
        body, name=name, grid=(nc,),
        in_specs=[blk(lambda j: (0, jnp.maximum(j - 1, 0), 0)), blk(lambda j: (0, j, 0)),
                  blk(lambda j: (0, j, 0)), blk(lambda j: (0, jnp.minimum(j + 1, nc - 1), 0)),
                  pl.BlockSpec((4, POOL_GROUP, POOL_GROUP), lambda j: (0, 0, 0)),
                  pl.BlockSpec((1, D_POOL), lambda j: (0, 0))],
        out_specs=[blk(lambda j: (0, j, 0)), pl.BlockSpec((4, POOL_GROUP, POOL_GROUP), lambda j: (0, 0, 0)),
                   pl.BlockSpec((1, D_POOL), lambda j: (0, 0))],
        out_shape=[jax.ShapeDtypeStruct(u.shape, BF16), jax.ShapeDtypeStruct((4, POOL_GROUP, POOL_GROUP), F32),
                   jax.ShapeDtypeStruct((1, D_POOL), F32)],
        compiler_params=_params("arbitrary"),
    )(u, u, dyp, dyp, pool_w, pool_scale)


CONV_SLAB = 512


def _conv_taps(tail, cur, keep_tail):
    ext = jnp.concatenate([jnp.where(keep_tail, tail, 0.0), cur], axis=0)
    return [(pltpu.roll(ext, CONV_W - 1 - k, 0) if k < CONV_W - 1 else ext)[8:] for k in range(CONV_W)]


def _conv_pre(taps, w_ref, b_ref, sl):
    acc = b_ref[:, sl]
    for k in range(CONV_W):
        acc = acc + w_ref[k:k + 1, sl] * taps[k]
    return acc


def _proj_conv(hn, w, conv_w, conv_b, *, name):
    n, d = hn.shape
    c = w.shape[0]
    assert PAD >= CONV_W - 1
    tm = _row_tile(n, d * 2 + c * (4 + 2), c * d, WIDE_BUDGET)

    def body(hn_ref, w_ref, cw_ref, cb_ref, xbc_ref, xc_ref, tail_ref):
        @pl.when(pl.program_id(0) == 0)
        def _():
            tail_ref[...] = jnp.zeros_like(tail_ref)

        av = hn_ref[...]
        starts = list(range(0, c, CONV_SLAB))

        def project(c0):
            xbc_ref[:, pl.ds(c0, CONV_SLAB)] = _dot_nt(av, w_ref[c0:c0 + CONV_SLAB, :])

        def convolve(c0):
            sl = pl.ds(c0, CONV_SLAB)
            xb = xbc_ref[:, sl]
            pre = _conv_pre(_conv_taps(tail_ref[:, sl], xb, True), cw_ref, cb_ref, sl)
            xc_ref[:, sl] = (pre * _sigmoid(pre)).astype(BF16)
            tail_ref[:, sl] = xb[tm - 8:, :]

        project(starts[0])
        for c0, c_next in zip(starts, starts[1:] + [None]):
            if c_next is not None:
                project(c_next)
            convolve(c0)

    row = lambda width: pl.BlockSpec((tm, width), lambda i: (i, 0))
    return pl.pallas_call(
        body, name=name, grid=(n // tm,),
        in_specs=[row(d), pl.BlockSpec(w.shape, lambda i: (0, 0), pipeline_mode=pl.Buffered(1)),
                  pl.BlockSpec((CONV_W, c), lambda i: (0, 0)), pl.BlockSpec((1, c), lambda i: (0, 0))],
        out_specs=[row(c), row(c)],
        out_shape=[jax.ShapeDtypeStruct((n, c), F32), jax.ShapeDtypeStruct((n, c), BF16)],
        scratch_shapes=[pltpu.VMEM((8, c), F32)],
        compiler_params=_params("arbitrary"),
    )(hn, w, conv_w, conv_b)


def _conv_bwd(xbc, dxs, db, dc, conv_w, conv_b, *, name):
    bsz, t, c = xbc.shape
    tile = _pick(t, (3 * CHUNK, CHUNK))
    nc = t // tile
    halo = 16
    rows = tile + halo

    def body(tail_ref, cur_ref, head_ref, dxs_ref, db_ref, dc_ref, dxs_head, db_head, dc_head, w_ref, b_ref,
             dx_ref, dwb_ref):
        j = pl.program_id(1)

        @pl.when(j == 0)
        def _():
            dwb_ref[...] = jnp.zeros_like(dwb_ref)

        has_prev, has_next = j > 0, j < nc - 1
        for c0 in range(0, c, CONV_SLAB):
            sl = pl.ds(c0, CONV_SLAB)
            if c0 < D_SSM:
                dxc, dxc_next = dxs_ref[0, :, sl], dxs_head[0, :, sl]
            elif c0 < D_SSM + D_POOL:
                dxc, dxc_next = db_ref[0], db_head[0]
            else:
                dxc, dxc_next = dc_ref[0], dc_head[0]
            dxc = jnp.concatenate([dxc.astype(F32), jnp.where(has_next, dxc_next.astype(F32), 0.0)], axis=0)
            ext = jnp.concatenate([jnp.where(has_prev, tail_ref[0, :, sl], 0.0), cur_ref[0, :, sl],
                                   jnp.where(has_next, head_ref[0, :, sl], 0.0)], axis=0)
            taps = [(pltpu.roll(ext, CONV_W - 1 - k, 0) if k < CONV_W - 1 else ext)[8:] for k in range(CONV_W)]
            pre = _conv_pre(taps, w_ref, b_ref, sl)
            s = _sigmoid(pre)
            dpre = dxc * (s * (1.0 + pre * (1.0 - s)))
            acc = w_ref[CONV_W - 1:CONV_W, sl] * dpre[:tile]
            for k in range(CONV_W - 1):
                up = CONV_W - 1 - k
                acc = acc + w_ref[k:k + 1, sl] * pltpu.roll(dpre, rows - up, 0)[:tile]
            dx_ref[0, :, sl] = acc.astype(BF16)
            for k in range(CONV_W):
                dwb_ref[0, k:k + 1, sl] += jnp.sum(dpre[:tile] * taps[k][:tile], axis=0, keepdims=True)
            dwb_ref[0, CONV_W:CONV_W + 1, sl] += jnp.sum(dpre[:tile], axis=0, keepdims=True)

    assert CONV_SLAB == D_POOL and D_SSM % CONV_SLAB == 0
    row = lambda width: pl.BlockSpec((1, tile, width), lambda b, j: (b, j, 0))
    nxt = lambda width: pl.BlockSpec(
        (1, halo, width), lambda b, j: (b, jnp.minimum((j + 1) * (tile // halo), t // halo - 1), 0))
    return pl.pallas_call(
        body, name=name, grid=(bsz, nc),
        in_specs=[pl.BlockSpec((1, 8, c), lambda b, j: (b, jnp.maximum(j * (tile // 8) - 1, 0), 0)), row(c), nxt(c),
                  row(D_SSM), row(D_POOL), row(D_POOL), nxt(D_SSM), nxt(D_POOL), nxt(D_POOL),
                  pl.BlockSpec((CONV_W, c), lambda b, j: (0, 0)), pl.BlockSpec((1, c), lambda b, j: (0, 0))],
        out_specs=[row(c), pl.BlockSpec((1, 8, c), lambda b, j: (b, 0, 0))],
        out_shape=[jax.ShapeDtypeStruct(xbc.shape, BF16), jax.ShapeDtypeStruct((bsz, 8, c), F32)],
        compiler_params=_params("parallel", "arbitrary"),
    )(xbc, xbc, xbc, dxs, db, dc, dxs, db, dc, conv_w, conv_b)


def _dt_valid(j):
    lane = lax.broadcasted_iota(jnp.int32, (CHUNK, LANES), 1)
    row = lax.broadcasted_iota(jnp.int32, (CHUNK, LANES), 0)
    return (lane < HPG) & ((j > 0) | (row >= PAD))


def _proj_uz_dt(hn, wu, wz, wdt, dtb, alog, nc, *, name):
    n, d = hn.shape
    tm = _pick(n, (768, 384, 128))
    per_tile = tm // CHUNK
    widths = (wu.shape[0], wz.shape[0], wdt.shape[0])

    def body(hn_ref, wu_ref, wz_ref, wdt_ref, dtb_ref, alog_ref, u_ref, z_ref, dtr_ref, dt_ref, acs_ref, tr_ref):
        i = pl.program_id(0)
        av = hn_ref[...]
        for w_ref, o_ref, m in zip((wu_ref, wz_ref, wdt_ref), (u_ref, z_ref, dtr_ref), widths):
            for c0 in range(0, m, 512):
                o_ref[:, c0:c0 + 512] = _dot_nt(av, w_ref[c0:c0 + 512, :])
        row = lax.broadcasted_iota(jnp.int32, (CHUNK, LANES), 0)
        lane = lax.broadcasted_iota(jnp.int32, (CHUNK, LANES), 1)
        tril = (row >= lane).astype(F32)
        units = [(cc, g) for cc in range(per_tile) for g in range(N_GROUPS)]
        at = lambda cc, g: (pl.ds(cc * CHUNK, CHUNK), pl.ds(g * LANES, LANES))
        valid = [(lane < HPG) & (((i * per_tile + cc) % nc > 0) | (row >= PAD)) for cc in range(per_tile)]
        dt = {(cc, g): jnp.where(valid[cc], _softplus(dtr_ref[at(cc, g)] + dtb_ref[g]), 0.0) for cc, g in units}
        acs = {(cc, g): _dot_exact(tril, dt[cc, g] * -jnp.exp(alog_ref[g])) for cc, g in units}
        for cc, g in units:
            dt_ref[at(cc, g)] = dt[cc, g]
            acs_ref[at(cc, g)] = acs[cc, g]
            tr_ref[cc, g, 0:8, :] = dt[cc, g].T[0:8]
            tr_ref[cc, g, 8:16, :] = acs[cc, g].T[0:8]

    row_blk = lambda width: pl.BlockSpec((tm, width), lambda i: (i, 0))
    whole = lambda w: pl.BlockSpec(w.shape, lambda i: (0, 0), pipeline_mode=pl.Buffered(1))
    const = pl.BlockSpec((N_GROUPS, 1, LANES), lambda i: (0, 0, 0))
    return pl.pallas_call(
        body, name=name, grid=(n // tm,),
        in_specs=[row_blk(d), whole(wu), whole(wz), whole(wdt), const, const],
        out_specs=[row_blk(widths[0]), row_blk(widths[1])] + [row_blk(D_DT)] * 3
        + [pl.BlockSpec((per_tile, N_GROUPS, 16, LANES), lambda i: (i, 0, 0, 0))],
        out_shape=[jax.ShapeDtypeStruct((n, widths[0]), F32), jax.ShapeDtypeStruct((n, widths[1]), F32)]
        + [jax.ShapeDtypeStruct((n, D_DT), F32)] * 3 + [jax.ShapeDtypeStruct((n // CHUNK, N_GROUPS, 16, LANES), F32)],
        compiler_params=_params("parallel"),
    )(hn, wu, wz, wdt, dtb, alog)


def _ssd_decay(dt, acs, tr):
    lane = lax.broadcasted_iota(jnp.int32, (CHUNK, LANES), 1)
    row = lax.broadcasted_iota(jnp.int32, (CHUNK, LANES), 0)
    return dict(lane=lane, row=row, dt=dt, causal=row >= lane, acs=acs, acs_t=tr[8:16], dt_t=tr[0:8],
                aend=acs[CHUNK - 1:CHUNK, :])


def _ssd_specs(bsz, nc, rev):
    ch = (lambda j: nc - 1 - j) if rev else (lambda j: j)
    return dict(
        xs=pl.BlockSpec((bsz, CHUNK, GW), lambda g, j: (0, ch(j), g)),
        bm=pl.BlockSpec((bsz, CHUNK, D_STATE), lambda g, j: (0, ch(j), D_SSM // D_STATE + g)),
        cm=pl.BlockSpec((bsz, CHUNK, D_STATE), lambda g, j: (0, ch(j), D_SSM // D_STATE + N_GROUPS + g)),
        lane_blk=pl.BlockSpec((bsz, CHUNK, LANES), lambda g, j: (0, ch(j), g)),
        grp_const=pl.BlockSpec((1, 1, LANES), lambda g, j: (g, 0, 0)),
        grp_vec=pl.BlockSpec((1, GW), lambda g, j: (0, g)),
        state=pl.BlockSpec((bsz, 1, D_STATE, GW), lambda g, j: (0, ch(j), 0, g)),
        tr=pl.BlockSpec((bsz, 1, 1, 16, LANES), lambda g, j: (0, ch(j), g, 0, 0)),
    )


def _ssd_fwd(xc, dt, acs, tr, z, dskip, normw, *, name, rider=None):
    bsz, t, _ = xc.shape
    nc = t // CHUNK
    sp = _ssd_specs(bsz, nc, False)

    def body(xs_ref, b_ref, c_ref, dt_ref, acs_ref, tr_ref, z_ref, dsk_ref, nw_ref, yn_ref, y_ref, sp_ref, s_ref):
        j = pl.program_id(1)

        @pl.when(j == 0)
        def _():
            s_ref[...] = jnp.zeros_like(s_ref)

        ex = range(bsz)
        units = [(e, r) for e in ex for r in range(HPG)]
        full = lambda v: jnp.broadcast_to(v, (CHUNK, LANES))
        pair = lambda r: pl.ds((r // 2) * LANES, LANES)
        q = [_ssd_decay(dt_ref[e], acs_ref[e], tr_ref[e, 0, 0]) for e in ex]
        for e in ex:
            sp_ref[e, 0] = s_ref[e]
        bm, cm = [b_ref[e] for e in ex], [c_ref[e] for e in ex]
        cb = [_dot_nt(cm[e], bm[e]) for e in ex]
        low = q[0]["lane"] < HEAD_DIM
        col = {(e, r): full(q[e]["acs"][:, r:r + 1]) for e, r in units}
        aend = {(e, r): q[e]["aend"][:, r:r + 1] for e, r in units}
        decay = {(e, r): jnp.exp(jnp.where(q[e]["causal"], col[e, r] - q[e]["acs_t"][r:r + 1, :], -jnp.inf))
                 for e, r in units}
        mp = {(e, r): cb[e] * decay[e, r] * q[e]["dt_t"][r:r + 1, :] for e, r in units}
        ce = {(e, r): cm[e] * jnp.exp(col[e, r]) for e, r in units}
        bk = {(e, r): bm[e] * (jnp.exp(aend[e, r] - col[e, r]) * full(q[e]["dt"][:, r:r + 1])) for e, r in units}
        xp = {(e, r): xs_ref[e, :, pair(r)] for e, r in units}
        s_old = {(e, r): s_ref[e, :, pair(r)] for e, r in units}
        y_h = {u: _dot(mp[u], xp[u]) + _dot(ce[u], s_old[u]) for u in units}
        s_h = {u: jnp.exp(aend[u]) * s_old[u] + _dot_tn(bk[u], xp[u]) for u in units}
        for e in ex:
            for r in range(0, HPG, 2):
                y_ref[e, :, pair(r)] = jnp.where(low, y_h[e, r], y_h[e, r + 1])
                s_ref[e, :, pair(r)] = jnp.where(low, s_h[e, r], s_h[e, r + 1])
        y = [y_ref[e] + dsk_ref[...] * xs_ref[e] for e in ex]
        zz = [z_ref[e] for e in ex]
        yg = [y[e] * (zz[e] * _sigmoid(zz[e])) for e in ex]
        rstd = [lax.rsqrt(jnp.mean(yg[e] * yg[e], axis=-1, keepdims=True) + EPS) for e in ex]
        for e in ex:
            y_ref[e] = y[e]
            yn_ref[e] = (yg[e] * rstd[e] * nw_ref[...]).astype(BF16)

    grid = (N_GROUPS, nc)
    ride = _Ride(rider, body, 9, 3, 1, grid)
    outs = pl.pallas_call(
        ride.body, name=name, grid=grid,
        in_specs=[sp["xs"], sp["bm"], sp["cm"], sp["lane_blk"], sp["lane_blk"], sp["tr"], sp["xs"],
                  sp["grp_vec"], sp["grp_vec"]] + ride.in_specs,
        out_specs=[sp["xs"], sp["xs"], sp["state"]] + ride.out_specs,
        out_shape=[jax.ShapeDtypeStruct((bsz, t, D_SSM), BF16), jax.ShapeDtypeStruct((bsz, t, D_SSM), F32),
                   jax.ShapeDtypeStruct((bsz, nc, D_STATE, D_SSM), F32)] + ride.out_shape,
        scratch_shapes=[pltpu.VMEM((bsz, D_STATE, GW), F32)] + ride.scratch,
        compiler_params=_params(*ride.semantics(("parallel", "arbitrary"))),
    )(xc, xc, xc, dt, acs, tr, z, dskip, normw, *ride.args)
    return outs[:3], outs[3:]


def _ssd_bwd(xc, dtr, dt, acs, tr, z, ypre, sprev, dyn, dtb, alog, dskip, normw, *, name, rider=None):
    bsz, t, _ = xc.shape
    nc = t // CHUNK
    sp = _ssd_specs(bsz, nc, True)

    def body(xs_ref, b_ref, c_ref, dtr_ref, dt_ref, acs_ref, tr_ref, z_ref, y_ref, sp_ref, dyn_ref, dtb_ref, alog_ref,
             dsk_ref, nw_ref, dz_ref, dxs_ref, db_ref, dc_ref, ddt_ref, dnw_ref, dsm_ref, ds_ref):
        j = pl.program_id(1)

        @pl.when(j == 0)
        def _():
            ds_ref[...] = jnp.zeros_like(ds_ref)
            dnw_ref[...] = jnp.zeros_like(dnw_ref)
            dsm_ref[...] = jnp.zeros_like(dsm_ref)

        ex = range(bsz)
        heads = range(HPG)
        units = [(e, r) for e in ex for r in heads]
        q = [_ssd_decay(dt_ref[e], acs_ref[e], tr_ref[e, 0, 0]) for e in ex]
        a = -jnp.exp(alog_ref[0])
        valid = _dt_valid(nc - 1 - j)
        lane, row = q[0]["lane"], q[0]["row"]
        lane1 = lane[0:1, :]
        nw = nw_ref[...]
        y, zz, dyn = [y_ref[e] for e in ex], [z_ref[e] for e in ex], [dyn_ref[e] for e in ex]
        sz = [_sigmoid(zz[e]) for e in ex]
        sil = [zz[e] * sz[e] for e in ex]
        yg = [y[e] * sil[e] for e in ex]
        rstd = [lax.rsqrt(jnp.mean(yg[e] * yg[e], axis=-1, keepdims=True) + EPS) for e in ex]
        gn = [dyn[e] * nw for e in ex]
        dyg = [rstd[e] * (gn[e] - yg[e] * (rstd[e] * rstd[e]) * jnp.mean(gn[e] * yg[e], axis=-1, keepdims=True))
               for e in ex]
        dy = [dyg[e] * sil[e] for e in ex]
        xs = [xs_ref[e] for e in ex]
        for e in ex:
            dnw_ref[e] += jnp.sum(dyn[e] * yg[e] * rstd[e], axis=0, keepdims=True)
            dz_ref[e] = (dyg[e] * y[e] * (sz[e] * (1.0 + zz[e] * (1.0 - sz[e])))).astype(BF16)
        dskip_cols = [jnp.sum(dy[e] * xs[e], axis=0, keepdims=True) for e in ex]

        bm, cm = [b_ref[e] for e in ex], [c_ref[e] for e in ex]
        cb = [_dot_nt(cm[e], bm[e]) for e in ex]
        zero = jnp.zeros((CHUNK, LANES), F32)
        full = lambda v: jnp.broadcast_to(v, (CHUNK, LANES))
        low = lane < HEAD_DIM
        half = [low if r % 2 == 0 else ~low for r in heads]
        sl = lambda v, r: v[:, (r // 2) * LANES:(r // 2 + 1) * LANES]
        pair = lambda r: pl.ds((r // 2) * LANES, LANES)
        col = {(e, r): full(q[e]["acs"][:, r:r + 1]) for e, r in units}
        dt_col = {(e, r): full(q[e]["dt"][:, r:r + 1]) for e, r in units}
        aend = {(e, r): q[e]["aend"][:, r:r + 1] for e, r in units}
        dt_row = {(e, r): q[e]["dt_t"][r:r + 1, :] for e, r in units}
        decay = {(e, r): jnp.exp(jnp.where(q[e]["causal"], col[e, r] - q[e]["acs_t"][r:r + 1, :], -jnp.inf))
                 for e, r in units}
        ea = {u: jnp.exp(col[u]) for u in units}
        dte = {u: jnp.exp(aend[u] - col[u]) for u in units}
        ed = {u: jnp.exp(aend[u]) for u in units}
        k = {u: dte[u] * dt_col[u] for u in units}
        mp = {(e, r): cb[e] * decay[e, r] * dt_row[e, r] for e, r in units}
        xp = {(e, r): sl(xs[e], r) for e, r in units}
        dym = {(e, r): jnp.where(half[r], sl(dy[e], r), 0.0) for e, r in units}
        s_old = {(e, r): sp_ref[e, 0, :, pair(r)] for e, r in units}
        ds_old = {(e, r): ds_ref[e, :, pair(r)] for e, r in units}
        dsm = {(e, r): jnp.where(half[r], ds_old[e, r], 0.0) for e, r in units}
        gmat = {u: _dot_nt(dym[u], xp[u]) for u in units}
        t1 = {u: _dot_nt(dym[u], s_old[u]) for u in units}
        dbs = {u: _dot_nt(xp[u], dsm[u]) for u in units}
        dx = {(e, r): _dot_tn(mp[e, r], dym[e, r]) + _dot(bm[e] * k[e, r], dsm[e, r]) for e, r in units}
        ds = {(e, r): _dot_tn(cm[e] * ea[e, r], dym[e, r]) for e, r in units}
        gd = {u: gmat[u] * decay[u] for u in units}
        w0 = {(e, r): gd[e, r] * cb[e] for e, r in units}
        cs0 = {u: jnp.sum(w0[u], axis=0, keepdims=True) for u in units}
        rs = {u: jnp.sum(w0[u] * dt_row[u], axis=1, keepdims=True) for u in units}
        qv = {(e, r): jnp.sum(cm[e] * t1[e, r], axis=1, keepdims=True) for e, r in units}
        dk = {(e, r): jnp.sum(bm[e] * dbs[e, r], axis=1, keepdims=True) for e, r in units}
        ddte = {u: dk[u] * dt_col[u] for u in units}
        d_aend = {u: _sum_all(dsm[u] * s_old[u]) * ed[u] + _sum_all(ddte[u][:, 0:1] * dte[u][:, 0:1]) for u in units}
        last_row = row == CHUNK - 1
        dacs_col = {u: rs[u] + qv[u] * ea[u] - ddte[u] * dte[u] + jnp.where(last_row, d_aend[u], 0.0) for u in units}
        triu = (lane >= row).astype(F32)
        for e in ex:
            dcb, dc_acc, db_acc = zero, zero, zero
            dacs, dacs_t, ddt, ddt_t = zero, zero, zero, zero
            dskip_row = jnp.zeros((1, LANES), F32)
            for r in heads:
                u = (e, r)
                dcb = dcb + gd[u] * dt_row[u]
                dc_acc = dc_acc + ea[u] * t1[u]
                db_acc = db_acc + k[u] * dbs[u]
                dacs = jnp.where(lane == r, dacs_col[u], dacs)
                ddt = jnp.where(lane == r, dk[u] * dte[u], ddt)
                dacs_t = jnp.where(row == r, -cs0[u] * dt_row[u], dacs_t)
                ddt_t = jnp.where(row == r, cs0[u], ddt_t)
                dsk = _sum_all(jnp.where(half[r][0:1, :], sl(dskip_cols[e], r), 0.0))
                dskip_row = dskip_row + jnp.where(lane1 == r, dsk, 0.0)
            for r in range(0, HPG, 2):
                dxs_ref[e, :, pair(r)] = (dx[e, r] + dx[e, r + 1] + sl(dy[e], r) * dsk_ref[:, pair(r)]).astype(BF16)
                ed_pair = jnp.where(lane1 < HEAD_DIM, ed[e, r], ed[e, r + 1])
                ds_ref[e, :, pair(r)] = ds[e, r] + ds[e, r + 1] + ed_pair * ds_old[e, r]
            dacs = dacs + dacs_t.T
            ddt = ddt + ddt_t.T
            dda = _dot_exact(triu, dacs)
            ddt = ddt + dda * a
            da = jnp.sum(dda * q[e]["dt"], axis=0, keepdims=True)
            draw = jnp.where(valid, ddt * _sigmoid(dtr_ref[e] + dtb_ref[0]), 0.0)
            ddt_ref[e] = draw.astype(BF16)
            dsm_ref[e, 0, 0:1, :] += dskip_row
            dsm_ref[e, 0, 1:2, :] += da * a
            dsm_ref[e, 0, 2:3, :] += jnp.sum(draw, axis=0, keepdims=True)
            dc_ref[e] = (dc_acc + _dot(dcb, bm[e])).astype(BF16)
            db_ref[e] = (db_acc + _dot_tn(dcb, cm[e])).astype(BF16)

    grp_out = pl.BlockSpec((bsz, CHUNK, D_STATE), lambda g, j: (0, nc - 1 - j, g))
    grid = (N_GROUPS, nc)
    ride = _Ride(rider, body, 15, 7, 1, grid)
    outs = pl.pallas_call(
        ride.body, name=name, grid=grid,
        in_specs=[sp["xs"], sp["bm"], sp["cm"], sp["lane_blk"], sp["lane_blk"], sp["lane_blk"], sp["tr"], sp["xs"],
                  sp["xs"], sp["state"], sp["xs"], sp["grp_const"], sp["grp_const"], sp["grp_vec"], sp["grp_vec"]]
        + ride.in_specs,
        out_specs=[sp["xs"], sp["xs"], grp_out, grp_out, sp["lane_blk"],
                   pl.BlockSpec((bsz, 1, GW), lambda g, j: (0, 0, g)),
                   pl.BlockSpec((bsz, 1, 8, LANES), lambda g, j: (0, g, 0, 0))] + ride.out_specs,
        out_shape=[jax.ShapeDtypeStruct((bsz, t, D_SSM), BF16), jax.ShapeDtypeStruct((bsz, t, D_SSM), BF16),
                   jax.ShapeDtypeStruct((bsz, t, N_GROUPS * D_STATE), BF16),
                   jax.ShapeDtypeStruct((bsz, t, N_GROUPS * D_STATE), BF16),
                   jax.ShapeDtypeStruct((bsz, t, D_DT), BF16), jax.ShapeDtypeStruct((bsz, 1, D_SSM), F32),
                   jax.ShapeDtypeStruct((bsz, N_GROUPS, 8, LANES), F32)] + ride.out_shape,
        scratch_shapes=[pltpu.VMEM((bsz, D_STATE, GW), F32)] + ride.scratch,
        compiler_params=_params(*ride.semantics(("parallel", "arbitrary"))),
    )(xc, xc, xc, dtr, dt, acs, tr, z, ypre, sprev, dyn, dtb, alog, dskip, normw, *ride.args)
    return outs[:7], outs[7:]


def _input_grad(dhn, h0, w, dres, seq, *, name):
    bsz, t, d = h0.shape
    nc = t // CHUNK

    def body(dy_ref, h_ref, w_ref, dres_ref, gx_ref, head_ref, dw_ref):
        j = pl.program_id(0)

        @pl.when(j == 0)
        def _():
            dw_ref[...] = jnp.zeros_like(dw_ref)

        for e in range(bsz):
            x, dyv = h_ref[e], dy_ref[e]
            r = lax.rsqrt(jnp.mean(x * x, axis=-1, keepdims=True) + EPS)
            g = dyv * w_ref[...]
            dx = r * (g - x * (r * r) * jnp.mean(g * x, axis=-1, keepdims=True)) + dres_ref[e]
            dw_ref[...] += jnp.sum(dyv * x * r, axis=0, keepdims=True)
            gx_ref[e] = dx

        @pl.when(j == 0)
        def _():
            head_ref[...] = gx_ref[...]

    row = pl.BlockSpec((bsz, CHUNK, d), lambda j: (0, j, 0))
    return pl.pallas_call(
        body, name=name, grid=(nc,),
        in_specs=[row, row, pl.BlockSpec((1, d), lambda j: (0, 0)), row],
        out_specs=[pl.BlockSpec((bsz, CHUNK, d), lambda j: (0, jnp.maximum(j - 1, 0), 0)),
                   pl.BlockSpec((bsz, CHUNK, d), lambda j: (0, 0, 0)), pl.BlockSpec((1, d), lambda j: (0, 0))],
        out_shape=[jax.ShapeDtypeStruct((bsz, seq, d), F32), jax.ShapeDtypeStruct((bsz, CHUNK, d), F32),
                   jax.ShapeDtypeStruct((1, d), F32)],
        compiler_params=_params("arbitrary"),
    )(dhn, h0, w, dres)


def _remote(src, dst, send_sem, recv_sem, dev):
    return pltpu.make_async_remote_copy(src_ref=src, dst_ref=dst, send_sem=send_sem, recv_sem=recv_sem,
                                        device_id=dev, device_id_type=MESH)


def _position():
    return lax.axis_index("x"), lax.axis_index("y"), lax.axis_index("c")


def _other_chips(pos):
    x, y, _ = pos
    return [(1 - x, y), (x, 1 - y), (1 - x, 1 - y)]


class _Gather:
    def __init__(self, arrs):
        n = len(arrs)
        self.args, self.n_in, self.n_out = list(arrs), n, n
        self.split = [a.ndim == 2 and a.shape[1] % (2 * LANES) == 0 for a in arrs]
        self.out_shape = [jax.ShapeDtypeStruct((4,) + a.shape, a.dtype) for a in arrs]
        self.scratch = [pltpu.SemaphoreType.DMA((3 * n,)), pltpu.SemaphoreType.DMA((3 * n,)),
                        pltpu.SemaphoreType.DMA((n,)), pltpu.SemaphoreType.DMA((3 * n,)),
                        pltpu.SemaphoreType.DMA((3 * n,))]

    def _copies(self, pos, ins, outs, sems):
        send_sems, recv_sems, loc_sems, pass_send_sems, pass_recv_sems = sems
        x, y, c = pos
        me, sibling = 2 * x + y, (x, y, 1 - c)
        local = [pltpu.make_async_copy(ins[i], outs[i].at[me], loc_sems.at[i]) for i in range(self.n_in)]
        sends, recvs, passes, pass_recvs = [], [], [], []
        for i in range(self.n_in):
            half = self.args[i].shape[1] // 2 if self.split[i] else None
            for k, (px, py) in enumerate(_other_chips(pos)):
                them = 2 * px + py
                sems_k = (send_sems.at[3 * i + k], recv_sems.at[3 * i + k], (px, py, c))
                if half is None:
                    sends.append(_remote(ins[i], outs[i].at[me], *sems_k))
                    recvs.append(_remote(ins[i], outs[i].at[them], *sems_k))
                    passes.append(None)
                    continue
                mine = pl.ds(pl.multiple_of(c * half, LANES), half)
                other = pl.ds(pl.multiple_of((1 - c) * half, LANES), half)
                sends.append(_remote(ins[i].at[:, mine], outs[i].at[me, :, mine], *sems_k))
                recvs.append(_remote(ins[i].at[:, mine], outs[i].at[them, :, mine], *sems_k))
                pass_k = (pass_send_sems.at[3 * i + k], pass_recv_sems.at[3 * i + k], sibling)
                passes.append(_remote(outs[i].at[them, :, mine], outs[i].at[them, :, mine], *pass_k))
                pass_recvs.append(_remote(outs[i].at[them, :, other], outs[i].at[them, :, other], *pass_k))
        return local, sends, recvs, passes, pass_recvs

    def start(self, pos, ins, outs, sems):
        local, sends = self._copies(pos, ins, outs, sems)[:2]
        for cp in local + sends:
            cp.start()

    def relay(self, pos, ins, outs, sems):
        _, _, recvs, passes, _ = self._copies(pos, ins, outs, sems)
        for cp, onward in zip(recvs, passes):
            if onward is not None:
                cp.wait_recv()
                onward.start()

    def finish(self, pos, ins, outs, sems):
        local, sends, recvs, passes, pass_recvs = self._copies(pos, ins, outs, sems)
        for cp, onward in zip(recvs, passes):
            if onward is None:
                cp.wait_recv()
        for cp in pass_recvs:
            cp.wait_recv()
        for cp in sends + [p for p in passes if p is not None]:
            cp.wait_send()
        for cp in local:
            cp.wait()


class _Exchange:
    FLIPS = [(fx, fy, fc) for fx in (0, 1) for fy in (0, 1) for fc in (0, 1)][1:]

    def __init__(self, big, small=None):
        n = len(big)
        self.n_big, self.has_small = n, small is not None
        self.args = list(big) + ([small] if self.has_small else [])
        self.n_in = self.n_out = len(self.args)
        self.out_shape = [jax.ShapeDtypeStruct(a.shape, a.dtype) for a in big]
        self.scratch = [pltpu.SemaphoreType.DMA((max(3 * n, 1),)), pltpu.SemaphoreType.DMA((max(3 * n, 1),)),
                        pltpu.SemaphoreType.DMA((n + 1,))]
        if self.has_small:
            self.out_shape.append(jax.ShapeDtypeStruct((8,) + small.shape, small.dtype))
            self.scratch += [pltpu.SemaphoreType.DMA((7,)), pltpu.SemaphoreType.DMA((7,))]

    def _copies(self, pos, ins, outs, sems):
        x, y, c = pos
        me, me8 = 2 * x + y, 4 * x + 2 * y + c
        local, sends, recvs = [], [], []
        for i in range(self.n_big):
            local.append(pltpu.make_async_copy(ins[i].at[me], outs[i].at[me], sems[2].at[i]))
            for k, (px, py) in enumerate(_other_chips(pos)):
                sems_k = (sems[0].at[3 * i + k], sems[1].at[3 * i + k], (px, py, c))
                sends.append(_remote(ins[i].at[2 * px + py], outs[i].at[me], *sems_k))
                recvs.append(_remote(ins[i].at[me], outs[i].at[2 * px + py], *sems_k))
        if self.has_small:
            small, landed = ins[self.n_big], outs[self.n_big]
            local.append(pltpu.make_async_copy(small, landed.at[me8], sems[2].at[self.n_big]))
            for k, (fx, fy, fc) in enumerate(self.FLIPS):
                peer = (x ^ fx, y ^ fy, c ^ fc)
                sems_k = (sems[3].at[k], sems[4].at[k], peer)
                sends.append(_remote(small, landed.at[me8], *sems_k))
                recvs.append(_remote(small, landed.at[4 * peer[0] + 2 * peer[1] + peer[2]], *sems_k))
        return local, sends, recvs, [None] * len(recvs), []

    start = _Gather.start
    relay = _Gather.relay
    finish = _Gather.finish


class _Swap:
    def __init__(self, arrs):
        n = len(arrs)
        self.args, self.n_in, self.n_out = list(arrs), n, n
        self.out_shape = [jax.ShapeDtypeStruct(a.shape, a.dtype) for a in arrs]
        self.scratch = [pltpu.SemaphoreType.DMA((n,)), pltpu.SemaphoreType.DMA((n,))]

    def _copies(self, pos, ins, outs, sems):
        x, y, c = pos
        both = [_remote(ins[i], outs[i], sems[0].at[i], sems[1].at[i], (x, y, 1 - c)) for i in range(self.n_in)]
        return [], both, both, [None] * len(both), []

    start = _Gather.start
    relay = _Gather.relay
    finish = _Gather.finish


def _comm(rider, *, name):
    a, b = rider.n_in, rider.n_in + rider.n_out

    def body(*refs):
        pos = _position()
        rider.start(pos, refs[:a], refs[a:b], refs[b:])
        rider.relay(pos, refs[:a], refs[a:b], refs[b:])
        rider.finish(pos, refs[:a], refs[a:b], refs[b:])

    return pl.pallas_call(body, name=name, in_specs=[ANY] * rider.n_in, out_specs=[ANY] * rider.n_out,
                          out_shape=rider.out_shape, scratch_shapes=rider.scratch)(*rider.args)


class _Ride:
    RELAY_AT = 0.8

    def __init__(self, rider, body, n_in, n_out, n_scratch, grid):
        self.rider = rider
        self.args = rider.args if rider else []
        self.in_specs = [ANY] * rider.n_in if rider else []
        self.out_specs = [ANY] * rider.n_out if rider else []
        self.out_shape = rider.out_shape if rider else []
        self.scratch = rider.scratch if rider else []
        self.body = self._wrap(body, n_in, n_out, n_scratch, grid) if rider else body

    def semantics(self, sem):
        return ("arbitrary",) * len(sem) if self.rider else sem

    def _wrap(self, body, n_in, n_out, n_scratch, grid):
        rider = self.rider
        a = n_in
        b = a + rider.n_in
        c = b + n_out
        d = c + rider.n_out
        e = d + n_scratch

        def wrapped(*refs):
            pos = _position()
            ids = [pl.program_id(i) for i in range(len(grid))]
            step, total = 0, 1
            for i, g in zip(ids, grid):
                step, total = step * g + i, total * g

            @pl.when(step == 0)
            def _():
                rider.start(pos, refs[a:b], refs[c:d], refs[e:])

            body(*refs[:a], *refs[b:c], *refs[d:e])

            @pl.when(step == int(self.RELAY_AT * (total - 1)))
            def _():
                rider.relay(pos, refs[a:b], refs[c:d], refs[e:])

            @pl.when(step == total - 1)
            def _():
                rider.finish(pos, refs[a:b], refs[c:d], refs[e:])

        return wrapped


def _elementwise_tiles(r, c):
    if r % 8 == 0 and r * c > 65536:
        tm = _pick(r, (256, 128, 64, 16, 8))
        return (tm, c), r // tm, lambda i: (i, 0)
    if r % 8 and c % 256 == 0 and r * c > 65536:
        return (r, 256), c // 256, lambda i: (0, i)
    return (r, c), 1, lambda i: (0, 0)


def _chip_sum(landed, *, name):
    _, r, c = landed.shape
    blk, steps, at = _elementwise_tiles(r, c)

    def body(land_ref, o_ref):
        acc = land_ref[0].astype(F32)
        for jchip in range(1, 4):
            acc = acc + land_ref[jchip].astype(F32)
        o_ref[...] = acc

    return pl.pallas_call(
        body, name=name, grid=(steps,), in_specs=[pl.BlockSpec((4,) + blk, lambda i: (0,) + at(i))],
        out_specs=pl.BlockSpec(blk, at), out_shape=jax.ShapeDtypeStruct((r, c), F32),
        compiler_params=_params("parallel"),
    )(landed)


def _device_sum(parts, *, name):
    _, r, c = parts.shape

    def body(p_ref, o_ref):
        acc = p_ref[0]
        for d in range(1, 8):
            acc = acc + p_ref[d]
        o_ref[...] = acc

    return pl.pallas_call(body, name=name, out_shape=jax.ShapeDtypeStruct((r, c), F32))(parts)


def _adamw_math(w, g, m, v):
    m = ADAM_B1 * m + (1.0 - ADAM_B1) * g
    v = ADAM_B2 * v + (1.0 - ADAM_B2) * (g * g)
    m_hat = m / (1.0 - ADAM_B1 ** ADAM_STEP)
    v_hat = v / (1.0 - ADAM_B2 ** ADAM_STEP)
    return -ADAM_LR * (m_hat / (jnp.sqrt(v_hat) + ADAM_EPS) + ADAM_WD * w), m, v


def _adamw(w, g_parts, m, v, *, name):
    r, c = w.shape
    shape, steps, at = _elementwise_tiles(r, c)
    n_g = len(g_parts)

    def body(*refs):
        w_ref, m_ref, v_ref = refs[n_g:n_g + 3]
        g_ref, d_ref, nm_ref, nv_ref = refs[n_g + 3:]
        g = refs[0][...]
        for p in refs[1:n_g]:
            g = g + p[...]
        g_ref[...] = g
        d_ref[...], nm_ref[...], nv_ref[...] = _adamw_math(w_ref[...], g, m_ref[...], v_ref[...])

    blk = pl.BlockSpec(shape, at)
    return pl.pallas_call(
        body, name=name, grid=(steps,), in_specs=[blk] * (n_g + 3), out_specs=[blk] * 4,
        out_shape=[jax.ShapeDtypeStruct((r, c), F32)] * 4, compiler_params=_params("parallel"),
    )(*g_parts, w, m, v)


def _pad_heads(v):
    return jnp.pad(v.reshape(N_GROUPS, 1, HPG), ((0, 0), (0, 0), (0, LANES - HPG)))


def _unpad_heads(v):
    return v[:, :HPG].reshape(1, N_HEADS)


_SMALL_EARLY = [("pool_w", (512, 128)), ("pool_scale", (1, 512)), ("conv_w", (4, D_XBC)), ("conv_b", (1, D_XBC)),
                ("dt_bias", (1, N_HEADS)), ("a_log", (1, N_HEADS)), ("d_skip", (1, N_HEADS)), ("ssm_norm_w", (1, D_SSM)),
                ("norm_ffn_w", (1, 1024)), ("norm_f_w", (1, 1024))]
_SMALL_LATE = [("norm_mix_w", (1, 1024)), ("meta", (N_META, 1024)), ("loss", (1, 1))]


def _pack_small(grads, layout):
    rows = []
    for nm, shape in layout:
        flat = grads[nm].reshape(-1)
        rows.append(jnp.pad(flat, (0, (-flat.size) % LANES)).reshape(-1, LANES))
    packed = jnp.concatenate(rows, axis=0)
    return jnp.pad(packed, ((0, (-packed.shape[0]) % 8), (0, 0)))


def _unpack_small(packed, layout):
    out, r0 = {}, 0
    for nm, shape in layout:
        size = shape[0] * shape[1]
        nrow = -(-size // LANES)
        out[nm] = packed[r0:r0 + nrow].reshape(-1)[:size].reshape(shape)
        r0 += nrow
    return out


def kernel(x, meta, norm_mix_w, w_in, pool_w, pool_scale, conv_w, conv_b, dt_bias, a_log, d_skip, ssm_norm_w, w_out, norm_ffn_w, w_ff1, w_ff2, norm_f_w, loss_target, m_meta, m_norm_mix_w, m_w_in, m_pool_w, m_pool_scale, m_conv_w, m_conv_b, m_dt_bias, m_a_log, m_d_skip, m_ssm_norm_w, m_w_out, m_norm_ffn_w, m_w_ff1, m_w_ff2, m_norm_f_w, v_meta, v_norm_mix_w, v_w_in, v_pool_w, v_pool_scale, v_conv_w, v_conv_b, v_dt_bias, v_a_log, v_d_skip, v_ssm_norm_w, v_w_out, v_norm_ffn_w, v_w_ff1, v_w_ff2, v_norm_f_w):
    bsz, seq, d = x.shape
    t = seq + CHUNK
    n = bsz * t
    chip = 2 * lax.axis_index("x") + lax.axis_index("y")
    d_in = w_in.shape[2] * 4

    g_conv, g_meta = _comm(_Gather([conv_w[0], meta]), name="gather_small")
    convw = g_conv.transpose(1, 0, 2).reshape(CONV_W, D_XBC)
    meta_full = g_meta.transpose(1, 0, 2).reshape(N_META, d)
    (h0, hn1), (g_in,) = _embed_norm(x, meta_full, norm_mix_w, name="embed_norm",
                                     rider=_Gather([w_in[0].T.astype(BF16)]))
    h0f, hn1 = h0.reshape(n, d), hn1.reshape(n, d)
    late_weights = _Gather([w_out[0].astype(BF16), w_ff1[0].astype(BF16), w_ff2[0].astype(BF16)])
    win = g_in.reshape(d_in, d)
    wu, wz = win[:D_POOL], win[D_POOL:D_POOL + D_SSM]
    wx = win[D_POOL + D_SSM:D_POOL + D_SSM + D_XBC]
    wdt = jnp.pad(win[D_POOL + D_SSM + D_XBC:].reshape(N_GROUPS, HPG, d),
                  ((0, 0), (0, LANES - HPG), (0, 0))).reshape(D_DT, d)
    dtb, alog = _pad_heads(dt_bias), _pad_heads(a_log)
    dskip = jnp.repeat(d_skip, HEAD_DIM, axis=1)
    poolw = pool_w[0]

    u, z, dtr, dt_, acs_, tr_ = _proj_uz_dt(hn1, wu, wz, wdt, dtb, alog, t // CHUNK, name="proj_uzdt")
    xbc, xc = _proj_conv(hn1, wx, convw, conv_b, name="proj_xbc")
    ypool = _pool_fwd(u.reshape(bsz, t, D_POOL), poolw, pool_scale, name="pool_fwd")
    xbc3 = xbc.reshape(bsz, t, D_XBC)
    xc = xc.reshape(bsz, t, D_XBC)
    z3, dtr3 = z.reshape(bsz, t, D_SSM), dtr.reshape(bsz, t, D_DT)
    dt3, acs3 = dt_.reshape(bsz, t, D_DT), acs_.reshape(bsz, t, D_DT)
    tr3 = tr_.reshape(bsz, t // CHUNK, N_GROUPS, 16, LANES)
    (yn, ypre, sprev), (g_out, g_ff1, g_ff2) = _ssd_fwd(xc, dt3, acs3, tr3, z3, dskip, ssm_norm_w, name="ssd_fwd",
                                                        rider=late_weights)
    wo = g_out.reshape(D_POOL + D_SSM, d)
    wo_p, wo_s = wo[:D_POOL], wo[D_POOL:]
    w1 = g_ff1
    w2 = g_ff2.reshape(D_FF, d)
    ypool_f, yn_f = ypool.reshape(n, D_POOL), yn.reshape(n, D_SSM)
    add = lambda r, e: r + e
    h1, hn2 = _mm([ypool_f, yn_f], [wo_p, wo_s], name="out_proj", post=add, extras=(h0f,), norm_w=norm_ffn_w)
    act = _mm(hn2, w1, name="ff1", out_dtype=BF16)
    relu2 = lambda a: jnp.square(jnp.maximum(a, 0))
    h2 = _mm(act, w2, name="ff2", pre=relu2, post=add, extras=(h1,))
    dh2, dh2b, loss_acc, d_norm_f = _final_norm_loss(h2.reshape(bsz, t, d), loss_target, norm_f_w.reshape(1, d),
                                                     name="loss")

    dh2f, dh2bf = dh2.reshape(n, d), dh2b.reshape(n, d)
    dact = _mm(dh2bf, w2, name="ff2_bwd", nt=True, post=lambda r, a: r * (2.0 * jnp.maximum(a, 0).astype(F32)),
               extras=(act,), out_dtype=BF16)
    d_w2 = _mm_tn(act, dh2bf, name="ff2_dw", tk=2048, tn=1024, pre=relu2)
    d_w1 = _mm_tn(hn2, dact, name="ff1_dw", tk=1024, tn=2048, slab=D_FF // 4)
    dh1, dh1b, d_norm_ffn = _mm_rms_bwd(dact, w1, h1, norm_ffn_w, dh2f, name="ff1_bwd")
    dypool, dyn = _mm_fanout(dh1b, [wo_p, wo_s], name="out_proj_bwd")
    d_wo = _mm_tn_cat([ypool_f, yn_f], dh1b, name="out_proj_dw")
    big_late = [d_wo.reshape(4, (D_POOL + D_SSM) // 4, d),
                d_w1, d_w2.reshape(4, D_FF // 4, d)]
    (dz, dxs, dbm, dcm, ddtr, d_nw, d_heads), landed_late = _ssd_bwd(
        xc, dtr3, dt3, acs3, tr3, z3, ypre, sprev, dyn.reshape(bsz, t, D_SSM), dtb, alog, dskip, ssm_norm_w, name="ssd_bwd",
        rider=_Exchange(big_late))
    dxbc, d_convwb = _conv_bwd(xbc3, dxs, dbm, dcm, convw, conv_b, name="conv_bwd")
    du, d_poolw, d_poolsc = _pool_bwd(u.reshape(bsz, t, D_POOL), dypool.reshape(bsz, t, D_POOL), poolw, pool_scale,
                                      name="pool_bwd")
    duf, dzf, dxbcf, ddtrf = du.reshape(n, D_POOL), dz.reshape(n, D_SSM), dxbc.reshape(n, D_XBC), ddtr.reshape(n, D_DT)
    heads = jnp.sum(d_heads, axis=0)
    small_early = _pack_small({
        "pool_w": d_poolw, "pool_scale": d_poolsc,
        "conv_w": jnp.sum(d_convwb[:, :CONV_W], axis=0), "conv_b": jnp.sum(d_convwb[:, CONV_W:CONV_W + 1], axis=0),
        "dt_bias": _unpad_heads(heads[:, 2]), "a_log": _unpad_heads(heads[:, 1]), "d_skip": _unpad_heads(heads[:, 0]),
        "ssm_norm_w": jnp.sum(d_nw, axis=0), "norm_ffn_w": d_norm_ffn, "norm_f_w": d_norm_f}, _SMALL_EARLY)
    d_wuzdt = _mm_tn_cat([duf, dzf, ddtrf], hn1, name="proj_uzdt_dw")
    d_wx, (early_all,) = _mm_tn(dxbcf, hn1, name="proj_xbc_dw", tk=1280, tn=1024, rider=_Exchange([], small_early))
    d_wdt = d_wuzdt[D_POOL + D_SSM:].reshape(N_GROUPS, LANES, d)[:, :HPG].reshape(N_HEADS, d)
    d_win = jnp.concatenate([d_wuzdt[:D_POOL + D_SSM], d_wx, d_wdt], axis=0)
    big_in = d_win.reshape(4, d_in // 4, d)
    dhn1, (landed_in,) = _mm([duf, dzf, dxbcf, ddtrf], [wu, wz, wx, wdt], name="proj_bwd",
                             rider=_Exchange([big_in]))
    grad_x, d_head_rows, d_norm_mix = _input_grad(
        dhn1.reshape(bsz, t, d), h0, norm_mix_w, dh1.reshape(bsz, t, d), seq, name="input_grad")

    landed = [landed_in] + list(landed_late)
    small_late = _pack_small({"norm_mix_w": d_norm_mix, "meta": jnp.sum(d_head_rows[:, PAD:], axis=0),
                              "loss": loss_acc[0:1, 0:1]}, _SMALL_LATE)
    (late_all,) = _comm(_Exchange([], small_late), name="exchange_small")
    mine = [_chip_sum(l, name=f"chip_sum_{i}") for i, l in enumerate(landed)]
    theirs = _comm(_Swap(mine), name="swap_cores")
    gsmall = {**_unpack_small(_device_sum(early_all, name="device_sum_early"), _SMALL_EARLY),
              **_unpack_small(_device_sum(late_all, name="device_sum_late"), _SMALL_LATE)}
    gsmall["conv_w"] = lax.dynamic_slice_in_dim(gsmall["conv_w"], chip * (D_XBC // 4), D_XBC // 4, axis=1)
    gsmall["meta"] = lax.dynamic_slice_in_dim(gsmall["meta"], chip * (d // 4), d // 4, axis=1)
    loss = gsmall["loss"][0, 0]

    given = dict(meta=(meta, m_meta, v_meta), norm_mix_w=(norm_mix_w, m_norm_mix_w, v_norm_mix_w),
                 w_in=(w_in, m_w_in, v_w_in), pool_w=(pool_w, m_pool_w, v_pool_w),
                 pool_scale=(pool_scale, m_pool_scale, v_pool_scale), conv_w=(conv_w, m_conv_w, v_conv_w),
                 conv_b=(conv_b, m_conv_b, v_conv_b), dt_bias=(dt_bias, m_dt_bias, v_dt_bias),
                 a_log=(a_log, m_a_log, v_a_log), d_skip=(d_skip, m_d_skip, v_d_skip),
                 ssm_norm_w=(ssm_norm_w, m_ssm_norm_w, v_ssm_norm_w), w_out=(w_out, m_w_out, v_w_out),
                 norm_ffn_w=(norm_ffn_w, m_norm_ffn_w, v_norm_ffn_w), w_ff1=(w_ff1, m_w_ff1, v_w_ff1),
                 w_ff2=(w_ff2, m_w_ff2, v_w_ff2), norm_f_w=(norm_f_w, m_norm_f_w, v_norm_f_w))
    big_names = ["w_in", "w_out", "w_ff1", "w_ff2"]
    results = {}
    for nm, (w, m, v) in given.items():
        if nm in big_names:
            i = big_names.index(nm)
            parts, shape2 = (mine[i], theirs[i]), mine[i].shape
        else:
            parts, shape2 = (gsmall[nm],), gsmall[nm].shape
        if nm == "w_in":
            outs = _adamw(w[0].T, parts, m[0].T, v[0].T, name=f"adamw_{nm}")
            results[nm] = [o.T[None] for o in outs]
        else:
            outs = _adamw(w.reshape(shape2), parts, m.reshape(shape2), v.reshape(shape2), name=f"adamw_{nm}")
            results[nm] = [o.reshape(w.shape) for o in outs]
    order = list(given)
    return (loss, grad_x, *[results[nm][0] for nm in order], *[results[nm][1] for nm in order],
            *[results[nm][2] for nm in order], *[results[nm][3] for nm in order])
```

```python
import jax
import jax.numpy as jnp
from jax import lax
from jax.experimental import pallas as pl
from jax.experimental.pallas import tpu as pltpu

F32 = jnp.float32
BF16 = jnp.bfloat16
MESH = pl.DeviceIdType.MESH
ANY = pl.BlockSpec(memory_space=pl.ANY)

D_MODEL = 1024
N_META = 16
CHUNK = 128
PAD = CHUNK - N_META
POOL_WINDOWS = (2, 4, 8, 16)
D_POOL = 512
POOL_GROUP = 128
D_SSM = 1536
N_HEADS = 24
N_GROUPS = 4
HPG = 6
HEAD_DIM = 64
D_STATE = 128
GW = HPG * HEAD_DIM
D_XBC = D_SSM + 2 * N_GROUPS * D_STATE
D_DT = N_GROUPS * 128
D_FF = 4096
CONV_W = 4
EPS = 1e-5
LANES = 128
VMEM_LIMIT = 56 * 1024 * 1024

ADAM_LR, ADAM_B1, ADAM_B2, ADAM_EPS, ADAM_WD, ADAM_STEP = 0.001, 0.9, 0.999, 1e-08, 0.01, 10


def _params(*sem):
    return pltpu.CompilerParams(dimension_semantics=sem, vmem_limit_bytes=VMEM_LIMIT)


def _pick(n, cands):
    for c in cands:
        if n % c == 0:
            return c
    raise ValueError(f"no block size for {n}")


def _dot(a, b):
    return jnp.dot(a.astype(BF16), b.astype(BF16), preferred_element_type=F32)


def _dot_nt(a, b):
    return lax.dot_general(a.astype(BF16), b.astype(BF16), (((1,), (1,)), ((), ())), preferred_element_type=F32)


def _dot_tn(a, b):
    return lax.dot_general(a.astype(BF16), b.astype(BF16), (((0,), (0,)), ((), ())), preferred_element_type=F32)


def _dot_exact(mask, x, terms=3):
    m = mask.astype(BF16)
    dot = lambda t: jnp.dot(m, t, preferred_element_type=F32)
    hi = x.astype(BF16)
    r1 = x - hi.astype(F32)
    mid = r1.astype(BF16)
    if terms == 2:
        return dot(hi) + dot(mid)
    lo = (r1 - mid.astype(F32)).astype(BF16)
    return dot(hi) + dot(mid) + dot(lo)


def _sigmoid(x):
    return 1.0 / (1.0 + jnp.exp(-x))


def _softplus(x):
    return jnp.maximum(x, 0.0) + jnp.log1p(jnp.exp(-jnp.abs(x)))


def _sum_all(x):
    return jnp.sum(jnp.sum(x, axis=1, keepdims=True), axis=0, keepdims=True)


ROW_TILES = (1056, 768, 704, 512, 384, 256, 128)
TILE_BUDGET = 28 * 1024 * 1024


def _row_tile(n, bytes_per_row, fixed_bytes, budget=TILE_BUDGET):
    for tm in ROW_TILES:
        if n % tm == 0 and 2 * (tm * bytes_per_row + fixed_bytes) <= budget:
            return tm
    raise ValueError(f"no row tile for {n}")


WIDE_BUDGET = 38 * 1024 * 1024


def _mm(a, w, *, name, tn=512, nt=False, pre=None, post=None, extras=(), out_dtype=F32, norm_w=None, rider=None):
    assert norm_w is None or (rider is None and out_dtype == F32)
    a_list = list(a) if isinstance(a, (list, tuple)) else [a]
    w_list = list(w) if isinstance(w, (list, tuple)) else [w]
    n_a, n_ex = len(a_list), len(extras)
    n = a_list[0].shape[0]
    shard = w_list[0].shape[2] if w_list[0].ndim == 3 else None
    assert shard is None or (not nt and n_a == 1 and shard % tn == 0)
    m = w_list[0].shape[0] * shard if shard else w_list[0].shape[0] if nt else w_list[0].shape[1]
    tn = min(tn, m)
    size = lambda dt: jnp.dtype(dt).itemsize
    per_row = (sum(x.shape[1] * size(x.dtype) for x in a_list) + m * size(out_dtype)
               + sum(m * size(e.dtype) for e in extras) + (2 * m if norm_w is not None else 0))
    tm = _row_tile(n, per_row, sum(x.size * size(x.dtype) for x in w_list) // 2, WIDE_BUDGET)
    n_norm = 0 if norm_w is None else 1

    def body(*refs):
        a_refs, w_refs, ex_refs = refs[:n_a], refs[n_a:2 * n_a], refs[2 * n_a:2 * n_a + n_ex]
        o_ref = refs[2 * n_a + n_ex + n_norm]
        avs = [(a_ref[...] if pre is None else pre(a_ref[...])).astype(BF16) for a_ref in a_refs]
        for c0 in range(0, m, tn):
            r = None
            for av, w_ref in zip(avs, w_refs):
                if shard:
                    term = _dot(av, w_ref[c0 // shard, :, c0 % shard:c0 % shard + tn])
                else:
                    term = _dot_nt(av, w_ref[c0:c0 + tn, :]) if nt else _dot(av, w_ref[:, c0:c0 + tn])
                r = term if r is None else r + term
            if post is not None:
                r = post(r, *[e[:, c0:c0 + tn] for e in ex_refs])
            o_ref[:, c0:c0 + tn] = r.astype(out_dtype)
        if n_norm:
            x = o_ref[...]
            scale = lax.rsqrt(jnp.mean(x * x, axis=-1, keepdims=True) + EPS)
            refs[2 * n_a + n_ex + 2][...] = (x * scale * refs[2 * n_a + n_ex][...]).astype(BF16)

    a_specs = [pl.BlockSpec((tm, x.shape[1]), lambda i: (i, 0)) for x in a_list]
    w_specs = [pl.BlockSpec(x.shape, lambda i, nd=x.ndim: (0,) * nd, pipeline_mode=pl.Buffered(1)) for x in w_list]
    blk = pl.BlockSpec((tm, m), lambda i: (i, 0))
    vec = [pl.BlockSpec((1, m), lambda i: (0, 0))] * n_norm
    grid = (n // tm,)
    ride = _Ride(rider, body, 2 * n_a + n_ex + n_norm, 1 + n_norm, 0, grid)
    outs = pl.pallas_call(
        ride.body, name=name, grid=grid,
        in_specs=a_specs + w_specs + [blk] * n_ex + vec + ride.in_specs,
        out_specs=[blk] * (1 + n_norm) + ride.out_specs,
        out_shape=[jax.ShapeDtypeStruct((n, m), out_dtype)] + [jax.ShapeDtypeStruct((n, m), BF16)] * n_norm + ride.out_shape,
        scratch_shapes=ride.scratch, compiler_params=_params(*ride.semantics(("parallel",))),
    )(*a_list, *w_list, *extras, *([norm_w] * n_norm), *ride.args)
    if n_norm:
        return outs[0], outs[1]
    return (outs[0], outs[1:]) if rider else outs[0]


def _mm_fanout(a, ws, *, name, tn=512):
    n, k = a.shape
    ms = [w.shape[0] for w in ws]
    tm = _row_tile(n, k * 2 + 4 * sum(ms), sum(w.size for w in ws), WIDE_BUDGET)
    n_w = len(ws)

    def body(a_ref, *refs):
        av = a_ref[...]
        for w_ref, o_ref, m in zip(refs[:n_w], refs[n_w:], ms):
            for c0 in range(0, m, tn):
                o_ref[:, c0:c0 + tn] = _dot_nt(av, w_ref[c0:c0 + tn, :])

    return pl.pallas_call(
        body, name=name, grid=(n // tm,),
        in_specs=[pl.BlockSpec((tm, k), lambda i: (i, 0))]
        + [pl.BlockSpec(w.shape, lambda i: (0, 0), pipeline_mode=pl.Buffered(1)) for w in ws],
        out_specs=[pl.BlockSpec((tm, m), lambda i: (i, 0)) for m in ms],
        out_shape=[jax.ShapeDtypeStruct((n, m), F32) for m in ms],
        compiler_params=_params("parallel"),
    )(a, *ws)


def _mm_tn(a, g, *, name, tk, tn, pre=None, slab=None, rider=None):
    n, k = a.shape
    m = g.shape[1]
    tk, tn = min(tk, k), min(tn, m)
    tm = _row_tile(n, tk * jnp.dtype(a.dtype).itemsize + tn * jnp.dtype(g.dtype).itemsize, tk * tn * 4)
    steps = n // tm

    def body(a_ref, g_ref, o_ref, acc_ref):
        r = pl.program_id(2)

        @pl.when(r == 0)
        def _():
            acc_ref[...] = jnp.zeros_like(acc_ref)

        av = a_ref[...]
        if pre is not None:
            av = pre(av)
        if slab:
            for s in range(tn // slab):
                acc_ref[s] += _dot_tn(av, g_ref[:, s * slab:(s + 1) * slab])
        else:
            acc_ref[...] += _dot_tn(av, g_ref[...])

        @pl.when(r == steps - 1)
        def _():
            o_ref[...] = acc_ref[...].astype(BF16)

    if slab:
        block, out_spec = (tn // slab, tk, slab), pl.BlockSpec((tn // slab, tk, slab), lambda i, j, r: (j, i, 0))
        out_shape = jax.ShapeDtypeStruct((m // slab, k, slab), BF16)
    else:
        block, out_spec = (tk, tn), pl.BlockSpec((tk, tn), lambda i, j, r: (i, j))
        out_shape = jax.ShapeDtypeStruct((k, m), BF16)
    grid = (k // tk, m // tn, steps)
    ride = _Ride(rider, body, 2, 1, 1, grid)
    outs = pl.pallas_call(
        ride.body, name=name, grid=grid,
        in_specs=[pl.BlockSpec((tm, tk), lambda i, j, r: (r, i)), pl.BlockSpec((tm, tn), lambda i, j, r: (r, j))]
        + ride.in_specs,
        out_specs=[out_spec] + ride.out_specs, out_shape=[out_shape] + ride.out_shape,
        scratch_shapes=[pltpu.VMEM(block, F32)] + ride.scratch,
        compiler_params=_params(*ride.semantics(("parallel", "parallel", "arbitrary"))),
    )(a, g, *ride.args)
    return (outs[0], outs[1:]) if rider else outs[0]


def _mm_tn_cat(a_list, g, *, name):
    n, m = g.shape
    ks = [a.shape[1] for a in a_list]
    size = lambda x: jnp.dtype(x.dtype).itemsize
    tm = _row_tile(n, sum(a.shape[1] * size(a) for a in a_list) + m * size(g), sum(ks) * m * 4)
    steps, n_a = n // tm, len(a_list)

    def body(*refs):
        g_ref, o_ref, acc_ref = refs[n_a], refs[n_a + 1], refs[n_a + 2]
        r = pl.program_id(0)

        @pl.when(r == 0)
        def _():
            acc_ref[...] = jnp.zeros_like(acc_ref)

        gv, k0 = g_ref[...], 0
        for a_ref, k in zip(refs[:n_a], ks):
            acc_ref[k0:k0 + k, :] += _dot_tn(a_ref[...], gv)
            k0 += k

        @pl.when(r == steps - 1)
        def _():
            o_ref[...] = acc_ref[...].astype(BF16)

    return pl.pallas_call(
        body, name=name, grid=(steps,),
        in_specs=[pl.BlockSpec((tm, k), lambda r: (r, 0)) for k in ks] + [pl.BlockSpec((tm, m), lambda r: (r, 0))],
        out_specs=pl.BlockSpec((sum(ks), m), lambda r: (0, 0)),
        out_shape=jax.ShapeDtypeStruct((sum(ks), m), BF16),
        scratch_shapes=[pltpu.VMEM((sum(ks), m), F32)],
        compiler_params=_params("arbitrary"),
    )(*a_list, g)


def _mm_rms_bwd(a, w, h, w_norm, dres, *, name):
    n, k = a.shape
    d = h.shape[1]
    slabs, _, ks = w.shape
    tm = _row_tile(n, k * jnp.dtype(a.dtype).itemsize + d * (4 + 4 + 4 + 2), d * k, WIDE_BUDGET)

    def body(a_ref, w_ref, h_ref, wn_ref, dres_ref, dx_ref, dxb_ref, dw_ref):
        @pl.when(pl.program_id(0) == 0)
        def _():
            dw_ref[...] = jnp.zeros_like(dw_ref)

        dyv = None
        for s in range(slabs):
            part = _dot_nt(a_ref[:, s * ks:(s + 1) * ks], w_ref[s])
            dyv = part if dyv is None else dyv + part
        x = h_ref[...]
        r = lax.rsqrt(jnp.mean(x * x, axis=-1, keepdims=True) + EPS)
        g = dyv * wn_ref[...]
        dx = r * (g - x * (r * r) * jnp.mean(g * x, axis=-1, keepdims=True)) + dres_ref[...]
        dx_ref[...] = dx
        dxb_ref[...] = dx.astype(BF16)
        dw_ref[...] += jnp.sum(dyv * x * r, axis=0, keepdims=True)

    row = pl.BlockSpec((tm, d), lambda i: (i, 0))
    vec = pl.BlockSpec((1, d), lambda i: (0, 0))
    return pl.pallas_call(
        body, name=name, grid=(n // tm,),
        in_specs=[pl.BlockSpec((tm, k), lambda i: (i, 0)),
                  pl.BlockSpec(w.shape, lambda i: (0, 0, 0), pipeline_mode=pl.Buffered(1)), row, vec, row],
        out_specs=[row, row, vec],
        out_shape=[jax.ShapeDtypeStruct((n, d), F32), jax.ShapeDtypeStruct((n, d), BF16), jax.ShapeDtypeStruct((1, d), F32)],
        compiler_params=_params("arbitrary"),
    )(a, w, h, w_norm, dres)


def _embed_norm(x, meta, w, *, name, rider=None):
    bsz, seq, d = x.shape
    t = seq + CHUNK
    nc = t // CHUNK

    def body(x_ref, meta_ref, w_ref, h_ref, hn_ref):
        j = pl.program_id(0)
        first = jnp.concatenate([jnp.zeros((PAD, d), F32), meta_ref[...]], axis=0)
        for e in range(bsz):
            h = jnp.where(j == 0, first, x_ref[e])
            r = lax.rsqrt(jnp.mean(h * h, axis=-1, keepdims=True) + EPS)
            h_ref[e] = h
            hn_ref[e] = (h * r * w_ref[...]).astype(BF16)

    row = pl.BlockSpec((bsz, CHUNK, d), lambda j: (0, j, 0))
    grid = (nc,)
    ride = _Ride(rider, body, 3, 2, 0, grid)
    outs = pl.pallas_call(
        ride.body, name=name, grid=grid,
        in_specs=[pl.BlockSpec((bsz, CHUNK, d), lambda j: (0, jnp.maximum(j - 1, 0), 0)),
                  pl.BlockSpec((N_META, d), lambda j: (0, 0)), pl.BlockSpec((1, d), lambda j: (0, 0))] + ride.in_specs,
        out_specs=[row, row] + ride.out_specs,
        out_shape=[jax.ShapeDtypeStruct((bsz, t, d), F32), jax.ShapeDtypeStruct((bsz, t, d), BF16)] + ride.out_shape,
        scratch_shapes=ride.scratch, compiler_params=_params(*ride.semantics(("parallel",))),
    )(x, meta, w, *ride.args)
    return outs[:2], outs[2:]


def _final_norm_loss(h2, target, w, *, name):
    bsz, t, d = h2.shape
    nc = t // CHUNK

    def body(h_ref, t_ref, w_ref, dh_ref, dhb_ref, loss_ref, dw_ref):
        j = pl.program_id(0)

        @pl.when(j == 0)
        def _():
            loss_ref[...] = jnp.zeros_like(loss_ref)
            dw_ref[...] = jnp.zeros_like(dw_ref)

        wv = w_ref[...]
        for e in range(bsz):
            x = h_ref[e]
            r = lax.rsqrt(jnp.mean(x * x, axis=-1, keepdims=True) + EPS)
            diff = jnp.where(j > 0, x * r * wv - t_ref[e], 0.0)
            loss_ref[...] += _sum_all(diff * diff) * (0.5 / d)
            dy = diff * (1.0 / d)
            g = dy * wv
            dh = r * (g - x * (r * r) * jnp.mean(g * x, axis=-1, keepdims=True))
            dh_ref[e] = dh
            dhb_ref[e] = dh.astype(BF16)
            dw_ref[...] += jnp.sum(dy * x * r, axis=0, keepdims=True)

    row = pl.BlockSpec((bsz, CHUNK, d), lambda j: (0, j, 0))
    return pl.pallas_call(
        body, name=name, grid=(nc,),
        in_specs=[row, pl.BlockSpec((bsz, CHUNK, d), lambda j: (0, jnp.maximum(j - 1, 0), 0)),
                  pl.BlockSpec((1, d), lambda j: (0, 0))],
        out_specs=[row, row, pl.BlockSpec((8, LANES), lambda j: (0, 0)), pl.BlockSpec((1, d), lambda j: (0, 0))],
        out_shape=[jax.ShapeDtypeStruct((bsz, t, d), F32), jax.ShapeDtypeStruct((bsz, t, d), BF16),
                   jax.ShapeDtypeStruct((8, LANES), F32), jax.ShapeDtypeStruct((1, d), F32)],
        compiler_params=_params("arbitrary"),
    )(h2, target, w)


def _pool_masks(j, transposed):
    r = lax.broadcasted_iota(jnp.int32, (CHUNK, 2 * CHUNK), 0)
    c = lax.broadcasted_iota(jnp.int32, (CHUNK, 2 * CHUNK), 1)
    masks = []
    for w in POOL_WINDOWS:
        if transposed:
            m = (c >= r) & (c < r + w)
        else:
            s = c - CHUNK
            m = (s <= r) & (s > r - w) & (s + j * CHUNK >= 0)
        masks.append(m.astype(F32))
    return masks


POOL_TERMS = 2


def _pool_count(t_global, w):
    return jnp.clip(t_global - PAD + 1, 1, w).astype(F32)


def _pool_fwd(u, pool_w, pool_scale, *, name):
    bsz, t, _ = u.shape
    nc = t // CHUNK

    def body(prev_ref, cur_ref, pw_ref, sc_ref, o_ref):
        j = pl.program_id(0)
        masks = _pool_masks(j, False)
        tg = j * CHUNK + lax.broadcasted_iota(jnp.int32, (CHUNK, 1), 0)
        count = [_pool_count(tg, w) for w in POOL_WINDOWS]
        units = [(e, gi) for e in range(bsz) for gi in range(len(POOL_WINDOWS))]
        sl = lambda gi: pl.ds(gi * POOL_GROUP, POOL_GROUP)
        cur = {(e, gi): cur_ref[e, :, sl(gi)] for e, gi in units}
        both = {(e, gi): jnp.concatenate([prev_ref[e, :, sl(gi)], cur[e, gi]], axis=0) for e, gi in units}
        win = {(e, gi): _dot_exact(masks[gi], both[e, gi], POOL_TERMS) for e, gi in units}
        pooled = {(e, gi): win[e, gi] / count[gi] - cur[e, gi] for e, gi in units}
        mixed = {(e, gi): _dot(pooled[e, gi], pw_ref[gi]) for e, gi in units}
        for e, gi in units:
            o_ref[e, :, sl(gi)] = (mixed[e, gi] * sc_ref[:, sl(gi)]).astype(BF16)

    blk = lambda f: pl.BlockSpec((bsz, CHUNK, D_POOL), f)
    return pl.pallas_call(
        body, name=name, grid=(nc,),
        in_specs=[blk(lambda j: (0, jnp.maximum(j - 1, 0), 0)), blk(lambda j: (0, j, 0)),
                  pl.BlockSpec((4, POOL_GROUP, POOL_GROUP), lambda j: (0, 0, 0)),
                  pl.BlockSpec((1, D_POOL), lambda j: (0, 0))],
        out_specs=blk(lambda j: (0, j, 0)), out_shape=jax.ShapeDtypeStruct(u.shape, BF16),
        compiler_params=_params("parallel"),
    )(u, u, pool_w, pool_scale)


def _pool_bwd(u, dyp, pool_w, pool_scale, *, name):
    bsz, t, _ = u.shape
    nc = t // CHUNK

    def body(prev_ref, cur_ref, dy_ref, dyn_ref, pw_ref, sc_ref, du_ref, dpw_ref, dsc_ref):
        j = pl.program_id(0)

        @pl.when(j == 0)
        def _():
            dpw_ref[...] = jnp.zeros_like(dpw_ref)
            dsc_ref[...] = jnp.zeros_like(dsc_ref)

        fwd = _pool_masks(j, False)
        bwd = _pool_masks(j, True)
        tg = j * CHUNK + lax.broadcasted_iota(jnp.int32, (CHUNK, 1), 0)
        count = [_pool_count(tg, w) for w in POOL_WINDOWS]
        count_next = [_pool_count(tg + CHUNK, w) for w in POOL_WINDOWS]
        has_next = j < nc - 1
        groups = range(len(POOL_WINDOWS))
        units = [(e, gi) for e in range(bsz) for gi in groups]
        sl = lambda gi: pl.ds(gi * POOL_GROUP, POOL_GROUP)
        cur = {(e, gi): cur_ref[e, :, sl(gi)] for e, gi in units}
        both = {(e, gi): jnp.concatenate([prev_ref[e, :, sl(gi)], cur[e, gi]], axis=0) for e, gi in units}
        win = {(e, gi): _dot_exact(fwd[gi], both[e, gi], POOL_TERMS) for e, gi in units}
        pooled = {(e, gi): win[e, gi] / count[gi] - cur[e, gi] for e, gi in units}
        dy = {(e, gi): dy_ref[e, :, sl(gi)] for e, gi in units}
        mixed = {(e, gi): _dot(pooled[e, gi], pw_ref[gi]) for e, gi in units}
        dm = {(e, gi): dy[e, gi] * sc_ref[:, sl(gi)] for e, gi in units}
        dm_next = {(e, gi): jnp.where(has_next, dyn_ref[e, :, sl(gi)], 0.0) * sc_ref[:, sl(gi)] for e, gi in units}
        dpw = {(e, gi): _dot_tn(pooled[e, gi], dm[e, gi]) for e, gi in units}
        dpooled = {(e, gi): _dot_nt(dm[e, gi], pw_ref[gi]) for e, gi in units}
        dpooled_next = {(e, gi): _dot_nt(dm_next[e, gi], pw_ref[gi]) for e, gi in units}
        spread = {(e, gi): jnp.concatenate([dpooled[e, gi] / count[gi], dpooled_next[e, gi] / count_next[gi]], axis=0)
                  for e, gi in units}
        back = {(e, gi): _dot_exact(bwd[gi], spread[e, gi], POOL_TERMS) for e, gi in units}
        for e, gi in units:
            du_ref[e, :, sl(gi)] = (back[e, gi] - dpooled[e, gi]).astype(BF16)
        for gi in groups:
            dsc, dw = None, None
            for e in range(bsz):
                term = jnp.sum(dy[e, gi] * mixed[e, gi], axis=0, keepdims=True)
                dsc = term if dsc is None else dsc + term
                dw = dpw[e, gi] if dw is None else dw + dpw[e, gi]
            dsc_ref[:, sl(gi)] += dsc
            dpw_ref[gi] += dw

    blk = lambda f: pl.BlockSpec((bsz, CHUNK, D_POOL), f)
    return pl.pallas_call(
        body, name=name, grid=(nc,),
        in_specs=[blk(lambda j: (0, jnp.maximum(j - 1, 0), 0)), blk(lambda j: (0, j, 0)),
                  blk(lambda j: (0, j, 0)), blk(lambda j: (0, jnp.minimum(j + 1, nc - 1), 0)),
                  pl.BlockSpec((4, POOL_GROUP, POOL_GROUP), lambda j: (0, 0, 0)),
                  pl.BlockSpec((1, D_POOL), lambda j: (0, 0))],
        out_specs=[blk(lambda j: (0, j, 0)), pl.BlockSpec((4, POOL_GROUP, POOL_GROUP), lambda j: (0, 0, 0)),
                   pl.BlockSpec((1, D_POOL), lambda j: (0, 0))],
        out_shape=[jax.ShapeDtypeStruct(u.shape, BF16), jax.ShapeDtypeStruct((4, POOL_GROUP, POOL_GROUP), F32),
                   jax.ShapeDtypeStruct((1, D_POOL), F32)],
        compiler_params=_params("arbitrary"),
    )(u, u, dyp, dyp, pool_w, pool_scale)


CONV_SLAB = 512


def _conv_taps(tail, cur, keep_tail):
    ext = jnp.concatenate([jnp.where(keep_tail, tail, 0.0), cur], axis=0)
    return [(pltpu.roll(ext, CONV_W - 1 - k, 0) if k < CONV_W - 1 else ext)[8:] for k in range(CONV_W)]


def _conv_pre(taps, w_ref, b_ref, sl):
    acc = b_ref[:, sl]
    for k in range(CONV_W):
        acc = acc + w_ref[k:k + 1, sl] * taps[k]
    return acc


def _proj_conv(hn, w, conv_w, conv_b, *, name):
    n, d = hn.shape
    c = w.shape[0]
    assert PAD >= CONV_W - 1
    tm = _row_tile(n, d * 2 + c * (4 + 2), c * d, WIDE_BUDGET)

    def body(hn_ref, w_ref, cw_ref, cb_ref, xbc_ref, xc_ref, tail_ref):
        @pl.when(pl.program_id(0) == 0)
        def _():
            tail_ref[...] = jnp.zeros_like(tail_ref)

        av = hn_ref[...]
        starts = list(range(0, c, CONV_SLAB))

        def project(c0):
            xbc_ref[:, pl.ds(c0, CONV_SLAB)] = _dot_nt(av, w_ref[c0:c0 + CONV_SLAB, :])

        def convolve(c0):
            sl = pl.ds(c0, CONV_SLAB)
            xb = xbc_ref[:, sl]
            pre = _conv_pre(_conv_taps(tail_ref[:, sl], xb, True), cw_ref, cb_ref, sl)
            xc_ref[:, sl] = (pre * _sigmoid(pre)).astype(BF16)
            tail_ref[:, sl] = xb[tm - 8:, :]

        project(starts[0])
        for c0, c_next in zip(starts, starts[1:] + [None]):
            if c_next is not None:
                project(c_next)
            convolve(c0)

    row = lambda width: pl.BlockSpec((tm, width), lambda i: (i, 0))
    return pl.pallas_call(
        body, name=name, grid=(n // tm,),
        in_specs=[row(d), pl.BlockSpec(w.shape, lambda i: (0, 0), pipeline_mode=pl.Buffered(1)),
                  pl.BlockSpec((CONV_W, c), lambda i: (0, 0)), pl.BlockSpec((1, c), lambda i: (0, 0))],
        out_specs=[row(c), row(c)],
        out_shape=[jax.ShapeDtypeStruct((n, c), F32), jax.ShapeDtypeStruct((n, c), BF16)],
        scratch_shapes=[pltpu.VMEM((8, c), F32)],
        compiler_params=_params("arbitrary"),
    )(hn, w, conv_w, conv_b)


def _conv_bwd(xbc, dxs, db, dc, conv_w, conv_b, *, name, rider=None):
    bsz, t, c = xbc.shape
    tile = _pick(t, (3 * CHUNK, CHUNK))
    nc = t // tile
    halo = 16
    rows = tile + halo

    def body(tail_ref, cur_ref, head_ref, dxs_ref, db_ref, dc_ref, dxs_head, db_head, dc_head, w_ref, b_ref,
             dx_ref, dwb_ref):
        j = pl.program_id(1)

        @pl.when(j == 0)
        def _():
            dwb_ref[...] = jnp.zeros_like(dwb_ref)

        has_prev, has_next = j > 0, j < nc - 1
        for c0 in range(0, c, CONV_SLAB):
            sl = pl.ds(c0, CONV_SLAB)
            if c0 < D_SSM:
                dxc, dxc_next = dxs_ref[0, :, sl], dxs_head[0, :, sl]
            elif c0 < D_SSM + D_POOL:
                dxc, dxc_next = db_ref[0], db_head[0]
            else:
                dxc, dxc_next = dc_ref[0], dc_head[0]
            dxc = jnp.concatenate([dxc.astype(F32), jnp.where(has_next, dxc_next.astype(F32), 0.0)], axis=0)
            ext = jnp.concatenate([jnp.where(has_prev, tail_ref[0, :, sl], 0.0), cur_ref[0, :, sl],
                                   jnp.where(has_next, head_ref[0, :, sl], 0.0)], axis=0)
            taps = [(pltpu.roll(ext, CONV_W - 1 - k, 0) if k < CONV_W - 1 else ext)[8:] for k in range(CONV_W)]
            pre = _conv_pre(taps, w_ref, b_ref, sl)
            s = _sigmoid(pre)
            dpre = dxc * (s * (1.0 + pre * (1.0 - s)))
            acc = w_ref[CONV_W - 1:CONV_W, sl] * dpre[:tile]
            for k in range(CONV_W - 1):
                up = CONV_W - 1 - k
                acc = acc + w_ref[k:k + 1, sl] * pltpu.roll(dpre, rows - up, 0)[:tile]
            dx_ref[0, :, sl] = acc.astype(BF16)
            for k in range(CONV_W):
                dwb_ref[0, k:k + 1, sl] += jnp.sum(dpre[:tile] * taps[k][:tile], axis=0, keepdims=True)
            dwb_ref[0, CONV_W:CONV_W + 1, sl] += jnp.sum(dpre[:tile], axis=0, keepdims=True)

    assert CONV_SLAB == D_POOL and D_SSM % CONV_SLAB == 0
    row = lambda width: pl.BlockSpec((1, tile, width), lambda b, j: (b, j, 0))
    nxt = lambda width: pl.BlockSpec(
        (1, halo, width), lambda b, j: (b, jnp.minimum((j + 1) * (tile // halo), t // halo - 1), 0))
    grid = (bsz, nc)
    ride = _Ride(rider, body, 11, 2, 0, grid)
    outs = pl.pallas_call(
        ride.body, name=name, grid=grid,
        in_specs=[pl.BlockSpec((1, 8, c), lambda b, j: (b, jnp.maximum(j * (tile // 8) - 1, 0), 0)), row(c), nxt(c),
                  row(D_SSM), row(D_POOL), row(D_POOL), nxt(D_SSM), nxt(D_POOL), nxt(D_POOL),
                  pl.BlockSpec((CONV_W, c), lambda b, j: (0, 0)), pl.BlockSpec((1, c), lambda b, j: (0, 0))]
        + ride.in_specs,
        out_specs=[row(c), pl.BlockSpec((1, 8, c), lambda b, j: (b, 0, 0))] + ride.out_specs,
        out_shape=[jax.ShapeDtypeStruct(xbc.shape, BF16), jax.ShapeDtypeStruct((bsz, 8, c), F32)] + ride.out_shape,
        scratch_shapes=ride.scratch, compiler_params=_params(*ride.semantics(("parallel", "arbitrary"))),
    )(xbc, xbc, xbc, dxs, db, dc, dxs, db, dc, conv_w, conv_b, *ride.args)
    return outs[:2], outs[2:]


def _dt_valid(j):
    lane = lax.broadcasted_iota(jnp.int32, (CHUNK, LANES), 1)
    row = lax.broadcasted_iota(jnp.int32, (CHUNK, LANES), 0)
    return (lane < HPG) & ((j > 0) | (row >= PAD))


def _proj_uz_dt(hn, wu, wz, wdt, dtb, alog, nc, *, name):
    n, d = hn.shape
    tm = _pick(n, (768, 384, 128))
    per_tile = tm // CHUNK
    widths = (wu.shape[0], wz.shape[0], wdt.shape[0])

    def body(hn_ref, wu_ref, wz_ref, wdt_ref, dtb_ref, alog_ref, u_ref, z_ref, dtr_ref, dt_ref, acs_ref, tr_ref):
        i = pl.program_id(0)
        av = hn_ref[...]
        for w_ref, o_ref, m in zip((wu_ref, wz_ref, wdt_ref), (u_ref, z_ref, dtr_ref), widths):
            for c0 in range(0, m, 512):
                o_ref[:, c0:c0 + 512] = _dot_nt(av, w_ref[c0:c0 + 512, :])
        row = lax.broadcasted_iota(jnp.int32, (CHUNK, LANES), 0)
        lane = lax.broadcasted_iota(jnp.int32, (CHUNK, LANES), 1)
        tril = (row >= lane).astype(F32)
        units = [(cc, g) for cc in range(per_tile) for g in range(N_GROUPS)]
        at = lambda cc, g: (pl.ds(cc * CHUNK, CHUNK), pl.ds(g * LANES, LANES))
        valid = [(lane < HPG) & (((i * per_tile + cc) % nc > 0) | (row >= PAD)) for cc in range(per_tile)]
        dt = {(cc, g): jnp.where(valid[cc], _softplus(dtr_ref[at(cc, g)] + dtb_ref[g]), 0.0) for cc, g in units}
        acs = {(cc, g): _dot_exact(tril, dt[cc, g] * -jnp.exp(alog_ref[g])) for cc, g in units}
        for cc, g in units:
            dt_ref[at(cc, g)] = dt[cc, g]
            acs_ref[at(cc, g)] = acs[cc, g]
            tr_ref[cc, g, 0:8, :] = dt[cc, g].T[0:8]
            tr_ref[cc, g, 8:16, :] = acs[cc, g].T[0:8]

    row_blk = lambda width: pl.BlockSpec((tm, width), lambda i: (i, 0))
    whole = lambda w: pl.BlockSpec(w.shape, lambda i: (0, 0), pipeline_mode=pl.Buffered(1))
    const = pl.BlockSpec((N_GROUPS, 1, LANES), lambda i: (0, 0, 0))
    return pl.pallas_call(
        body, name=name, grid=(n // tm,),
        in_specs=[row_blk(d), whole(wu), whole(wz), whole(wdt), const, const],
        out_specs=[row_blk(widths[0]), row_blk(widths[1])] + [row_blk(D_DT)] * 3
        + [pl.BlockSpec((per_tile, N_GROUPS, 16, LANES), lambda i: (i, 0, 0, 0))],
        out_shape=[jax.ShapeDtypeStruct((n, widths[0]), F32), jax.ShapeDtypeStruct((n, widths[1]), F32)]
        + [jax.ShapeDtypeStruct((n, D_DT), F32)] * 3 + [jax.ShapeDtypeStruct((n // CHUNK, N_GROUPS, 16, LANES), F32)],
        compiler_params=_params("parallel"),
    )(hn, wu, wz, wdt, dtb, alog)


def _ssd_decay(dt, acs, tr):
    lane = lax.broadcasted_iota(jnp.int32, (CHUNK, LANES), 1)
    row = lax.broadcasted_iota(jnp.int32, (CHUNK, LANES), 0)
    return dict(lane=lane, row=row, dt=dt, causal=row >= lane, acs=acs, acs_t=tr[8:16], dt_t=tr[0:8],
                aend=acs[CHUNK - 1:CHUNK, :])


def _ssd_specs(bsz, nc, rev):
    ch = (lambda j: nc - 1 - j) if rev else (lambda j: j)
    return dict(
        xs=pl.BlockSpec((bsz, CHUNK, GW), lambda g, j: (0, ch(j), g)),
        bm=pl.BlockSpec((bsz, CHUNK, D_STATE), lambda g, j: (0, ch(j), D_SSM // D_STATE + g)),
        cm=pl.BlockSpec((bsz, CHUNK, D_STATE), lambda g, j: (0, ch(j), D_SSM // D_STATE + N_GROUPS + g)),
        lane_blk=pl.BlockSpec((bsz, CHUNK, LANES), lambda g, j: (0, ch(j), g)),
        grp_const=pl.BlockSpec((1, 1, LANES), lambda g, j: (g, 0, 0)),
        grp_vec=pl.BlockSpec((1, GW), lambda g, j: (0, g)),
        state=pl.BlockSpec((bsz, 1, D_STATE, GW), lambda g, j: (0, ch(j), 0, g)),
        tr=pl.BlockSpec((bsz, 1, 1, 16, LANES), lambda g, j: (0, ch(j), g, 0, 0)),
    )


def _ssd_fwd(xc, dt, acs, tr, z, dskip, normw, *, name, rider=None):
    bsz, t, _ = xc.shape
    nc = t // CHUNK
    sp = _ssd_specs(bsz, nc, False)

    def body(xs_ref, b_ref, c_ref, dt_ref, acs_ref, tr_ref, z_ref, dsk_ref, nw_ref, yn_ref, y_ref, sp_ref, s_ref):
        j = pl.program_id(1)

        @pl.when(j == 0)
        def _():
            s_ref[...] = jnp.zeros_like(s_ref)

        ex = range(bsz)
        units = [(e, r) for e in ex for r in range(HPG)]
        full = lambda v: jnp.broadcast_to(v, (CHUNK, LANES))
        pair = lambda r: pl.ds((r // 2) * LANES, LANES)
        q = [_ssd_decay(dt_ref[e], acs_ref[e], tr_ref[e, 0, 0]) for e in ex]
        for e in ex:
            sp_ref[e, 0] = s_ref[e]
        bm, cm = [b_ref[e] for e in ex], [c_ref[e] for e in ex]
        cb = [_dot_nt(cm[e], bm[e]) for e in ex]
        low = q[0]["lane"] < HEAD_DIM
        col = {(e, r): full(q[e]["acs"][:, r:r + 1]) for e, r in units}
        aend = {(e, r): q[e]["aend"][:, r:r + 1] for e, r in units}
        decay = {(e, r): jnp.exp(jnp.where(q[e]["causal"], col[e, r] - q[e]["acs_t"][r:r + 1, :], -jnp.inf))
                 for e, r in units}
        mp = {(e, r): cb[e] * decay[e, r] * q[e]["dt_t"][r:r + 1, :] for e, r in units}
        ce = {(e, r): cm[e] * jnp.exp(col[e, r]) for e, r in units}
        bk = {(e, r): bm[e] * (jnp.exp(aend[e, r] - col[e, r]) * full(q[e]["dt"][:, r:r + 1])) for e, r in units}
        xp = {(e, r): xs_ref[e, :, pair(r)] for e, r in units}
        s_old = {(e, r): s_ref[e, :, pair(r)] for e, r in units}
        y_h = {u: _dot(mp[u], xp[u]) + _dot(ce[u], s_old[u]) for u in units}
        s_h = {u: jnp.exp(aend[u]) * s_old[u] + _dot_tn(bk[u], xp[u]) for u in units}
        for e in ex:
            for r in range(0, HPG, 2):
                y_ref[e, :, pair(r)] = jnp.where(low, y_h[e, r], y_h[e, r + 1])
                s_ref[e, :, pair(r)] = jnp.where(low, s_h[e, r], s_h[e, r + 1])
        y = [y_ref[e] + dsk_ref[...] * xs_ref[e] for e in ex]
        zz = [z_ref[e] for e in ex]
        yg = [y[e] * (zz[e] * _sigmoid(zz[e])) for e in ex]
        rstd = [lax.rsqrt(jnp.mean(yg[e] * yg[e], axis=-1, keepdims=True) + EPS) for e in ex]
        for e in ex:
            y_ref[e] = y[e]
            yn_ref[e] = (yg[e] * rstd[e] * nw_ref[...]).astype(BF16)

    grid = (N_GROUPS, nc)
    ride = _Ride(rider, body, 9, 3, 1, grid)
    outs = pl.pallas_call(
        ride.body, name=name, grid=grid,
        in_specs=[sp["xs"], sp["bm"], sp["cm"], sp["lane_blk"], sp["lane_blk"], sp["tr"], sp["xs"],
                  sp["grp_vec"], sp["grp_vec"]] + ride.in_specs,
        out_specs=[sp["xs"], sp["xs"], sp["state"]] + ride.out_specs,
        out_shape=[jax.ShapeDtypeStruct((bsz, t, D_SSM), BF16), jax.ShapeDtypeStruct((bsz, t, D_SSM), F32),
                   jax.ShapeDtypeStruct((bsz, nc, D_STATE, D_SSM), F32)] + ride.out_shape,
        scratch_shapes=[pltpu.VMEM((bsz, D_STATE, GW), F32)] + ride.scratch,
        compiler_params=_params(*ride.semantics(("parallel", "arbitrary"))),
    )(xc, xc, xc, dt, acs, tr, z, dskip, normw, *ride.args)
    return outs[:3], outs[3:]


def _ssd_bwd(xc, dtr, dt, acs, tr, z, ypre, sprev, dyn, dtb, alog, dskip, normw, *, name, rider=None):
    bsz, t, _ = xc.shape
    nc = t // CHUNK
    sp = _ssd_specs(bsz, nc, True)

    def body(xs_ref, b_ref, c_ref, dtr_ref, dt_ref, acs_ref, tr_ref, z_ref, y_ref, sp_ref, dyn_ref, dtb_ref, alog_ref,
             dsk_ref, nw_ref, dz_ref, dxs_ref, db_ref, dc_ref, ddt_ref, dnw_ref, dsm_ref, ds_ref):
        j = pl.program_id(1)

        @pl.when(j == 0)
        def _():
            ds_ref[...] = jnp.zeros_like(ds_ref)
            dnw_ref[...] = jnp.zeros_like(dnw_ref)
            dsm_ref[...] = jnp.zeros_like(dsm_ref)

        ex = range(bsz)
        heads = range(HPG)
        units = [(e, r) for e in ex for r in heads]
        q = [_ssd_decay(dt_ref[e], acs_ref[e], tr_ref[e, 0, 0]) for e in ex]
        a = -jnp.exp(alog_ref[0])
        valid = _dt_valid(nc - 1 - j)
        lane, row = q[0]["lane"], q[0]["row"]
        lane1 = lane[0:1, :]
        nw = nw_ref[...]
        y, zz, dyn = [y_ref[e] for e in ex], [z_ref[e] for e in ex], [dyn_ref[e] for e in ex]
        sz = [_sigmoid(zz[e]) for e in ex]
        sil = [zz[e] * sz[e] for e in ex]
        yg = [y[e] * sil[e] for e in ex]
        rstd = [lax.rsqrt(jnp.mean(yg[e] * yg[e], axis=-1, keepdims=True) + EPS) for e in ex]
        gn = [dyn[e] * nw for e in ex]
        dyg = [rstd[e] * (gn[e] - yg[e] * (rstd[e] * rstd[e]) * jnp.mean(gn[e] * yg[e], axis=-1, keepdims=True))
               for e in ex]
        dy = [dyg[e] * sil[e] for e in ex]
        xs = [xs_ref[e] for e in ex]
        for e in ex:
            dnw_ref[e] += jnp.sum(dyn[e] * yg[e] * rstd[e], axis=0, keepdims=True)
            dz_ref[e] = (dyg[e] * y[e] * (sz[e] * (1.0 + zz[e] * (1.0 - sz[e])))).astype(BF16)
        dskip_cols = [jnp.sum(dy[e] * xs[e], axis=0, keepdims=True) for e in ex]

        bm, cm = [b_ref[e] for e in ex], [c_ref[e] for e in ex]
        cb = [_dot_nt(cm[e], bm[e]) for e in ex]
        zero = jnp.zeros((CHUNK, LANES), F32)
        full = lambda v: jnp.broadcast_to(v, (CHUNK, LANES))
        low = lane < HEAD_DIM
        half = [low if r % 2 == 0 else ~low for r in heads]
        sl = lambda v, r: v[:, (r // 2) * LANES:(r // 2 + 1) * LANES]
        pair = lambda r: pl.ds((r // 2) * LANES, LANES)
        col = {(e, r): full(q[e]["acs"][:, r:r + 1]) for e, r in units}
        dt_col = {(e, r): full(q[e]["dt"][:, r:r + 1]) for e, r in units}
        aend = {(e, r): q[e]["aend"][:, r:r + 1] for e, r in units}
        dt_row = {(e, r): q[e]["dt_t"][r:r + 1, :] for e, r in units}
        decay = {(e, r): jnp.exp(jnp.where(q[e]["causal"], col[e, r] - q[e]["acs_t"][r:r + 1, :], -jnp.inf))
                 for e, r in units}
        ea = {u: jnp.exp(col[u]) for u in units}
        dte = {u: jnp.exp(aend[u] - col[u]) for u in units}
        ed = {u: jnp.exp(aend[u]) for u in units}
        k = {u: dte[u] * dt_col[u] for u in units}
        mp = {(e, r): cb[e] * decay[e, r] * dt_row[e, r] for e, r in units}
        xp = {(e, r): sl(xs[e], r) for e, r in units}
        dym = {(e, r): jnp.where(half[r], sl(dy[e], r), 0.0) for e, r in units}
        s_old = {(e, r): sp_ref[e, 0, :, pair(r)] for e, r in units}
        ds_old = {(e, r): ds_ref[e, :, pair(r)] for e, r in units}
        dsm = {(e, r): jnp.where(half[r], ds_old[e, r], 0.0) for e, r in units}
        gmat = {u: _dot_nt(dym[u], xp[u]) for u in units}
        t1 = {u: _dot_nt(dym[u], s_old[u]) for u in units}
        dbs = {u: _dot_nt(xp[u], dsm[u]) for u in units}
        dx = {(e, r): _dot_tn(mp[e, r], dym[e, r]) + _dot(bm[e] * k[e, r], dsm[e, r]) for e, r in units}
        ds = {(e, r): _dot_tn(cm[e] * ea[e, r], dym[e, r]) for e, r in units}
        gd = {u: gmat[u] * decay[u] for u in units}
        w0 = {(e, r): gd[e, r] * cb[e] for e, r in units}
        cs0 = {u: jnp.sum(w0[u], axis=0, keepdims=True) for u in units}
        rs = {u: jnp.sum(w0[u] * dt_row[u], axis=1, keepdims=True) for u in units}
        qv = {(e, r): jnp.sum(cm[e] * t1[e, r], axis=1, keepdims=True) for e, r in units}
        dk = {(e, r): jnp.sum(bm[e] * dbs[e, r], axis=1, keepdims=True) for e, r in units}
        ddte = {u: dk[u] * dt_col[u] for u in units}
        d_aend = {u: _sum_all(dsm[u] * s_old[u]) * ed[u] + _sum_all(ddte[u][:, 0:1] * dte[u][:, 0:1]) for u in units}
        last_row = row == CHUNK - 1
        dacs_col = {u: rs[u] + qv[u] * ea[u] - ddte[u] * dte[u] + jnp.where(last_row, d_aend[u], 0.0) for u in units}
        triu = (lane >= row).astype(F32)
        for e in ex:
            dcb, dc_acc, db_acc = zero, zero, zero
            dacs, dacs_t, ddt, ddt_t = zero, zero, zero, zero
            dskip_row = jnp.zeros((1, LANES), F32)
            for r in heads:
                u = (e, r)
                dcb = dcb + gd[u] * dt_row[u]
                dc_acc = dc_acc + ea[u] * t1[u]
                db_acc = db_acc + k[u] * dbs[u]
                dacs = jnp.where(lane == r, dacs_col[u], dacs)
                ddt = jnp.where(lane == r, dk[u] * dte[u], ddt)
                dacs_t = jnp.where(row == r, -cs0[u] * dt_row[u], dacs_t)
                ddt_t = jnp.where(row == r, cs0[u], ddt_t)
                dsk = _sum_all(jnp.where(half[r][0:1, :], sl(dskip_cols[e], r), 0.0))
                dskip_row = dskip_row + jnp.where(lane1 == r, dsk, 0.0)
            for r in range(0, HPG, 2):
                dxs_ref[e, :, pair(r)] = (dx[e, r] + dx[e, r + 1] + sl(dy[e], r) * dsk_ref[:, pair(r)]).astype(BF16)
                ed_pair = jnp.where(lane1 < HEAD_DIM, ed[e, r], ed[e, r + 1])
                ds_ref[e, :, pair(r)] = ds[e, r] + ds[e, r + 1] + ed_pair * ds_old[e, r]
            dacs = dacs + dacs_t.T
            ddt = ddt + ddt_t.T
            dda = _dot_exact(triu, dacs)
            ddt = ddt + dda * a
            da = jnp.sum(dda * q[e]["dt"], axis=0, keepdims=True)
            draw = jnp.where(valid, ddt * _sigmoid(dtr_ref[e] + dtb_ref[0]), 0.0)
            ddt_ref[e] = draw.astype(BF16)
            dsm_ref[e, 0, 0:1, :] += dskip_row
            dsm_ref[e, 0, 1:2, :] += da * a
            dsm_ref[e, 0, 2:3, :] += jnp.sum(draw, axis=0, keepdims=True)
            dc_ref[e] = (dc_acc + _dot(dcb, bm[e])).astype(BF16)
            db_ref[e] = (db_acc + _dot_tn(dcb, cm[e])).astype(BF16)

    grp_out = pl.BlockSpec((bsz, CHUNK, D_STATE), lambda g, j: (0, nc - 1 - j, g))
    grid = (N_GROUPS, nc)
    ride = _Ride(rider, body, 15, 7, 1, grid)
    outs = pl.pallas_call(
        ride.body, name=name, grid=grid,
        in_specs=[sp["xs"], sp["bm"], sp["cm"], sp["lane_blk"], sp["lane_blk"], sp["lane_blk"], sp["tr"], sp["xs"],
                  sp["xs"], sp["state"], sp["xs"], sp["grp_const"], sp["grp_const"], sp["grp_vec"], sp["grp_vec"]]
        + ride.in_specs,
        out_specs=[sp["xs"], sp["xs"], grp_out, grp_out, sp["lane_blk"],
                   pl.BlockSpec((bsz, 1, GW), lambda g, j: (0, 0, g)),
                   pl.BlockSpec((bsz, 1, 8, LANES), lambda g, j: (0, g, 0, 0))] + ride.out_specs,
        out_shape=[jax.ShapeDtypeStruct((bsz, t, D_SSM), BF16), jax.ShapeDtypeStruct((bsz, t, D_SSM), BF16),
                   jax.ShapeDtypeStruct((bsz, t, N_GROUPS * D_STATE), BF16),
                   jax.ShapeDtypeStruct((bsz, t, N_GROUPS * D_STATE), BF16),
                   jax.ShapeDtypeStruct((bsz, t, D_DT), BF16), jax.ShapeDtypeStruct((bsz, 1, D_SSM), F32),
                   jax.ShapeDtypeStruct((bsz, N_GROUPS, 8, LANES), F32)] + ride.out_shape,
        scratch_shapes=[pltpu.VMEM((bsz, D_STATE, GW), F32)] + ride.scratch,
        compiler_params=_params(*ride.semantics(("parallel", "arbitrary"))),
    )(xc, xc, xc, dtr, dt, acs, tr, z, ypre, sprev, dyn, dtb, alog, dskip, normw, *ride.args)
    return outs[:7], outs[7:]


def _input_grad(dhn, h0, w, dres, seq, *, name):
    bsz, t, d = h0.shape
    nc = t // CHUNK

    def body(dy_ref, h_ref, w_ref, dres_ref, gx_ref, head_ref, dw_ref):
        j = pl.program_id(0)

        @pl.when(j == 0)
        def _():
            dw_ref[...] = jnp.zeros_like(dw_ref)

        for e in range(bsz):
            x, dyv = h_ref[e], dy_ref[e]
            r = lax.rsqrt(jnp.mean(x * x, axis=-1, keepdims=True) + EPS)
            g = dyv * w_ref[...]
            dx = r * (g - x * (r * r) * jnp.mean(g * x, axis=-1, keepdims=True)) + dres_ref[e]
            dw_ref[...] += jnp.sum(dyv * x * r, axis=0, keepdims=True)
            gx_ref[e] = dx

        @pl.when(j == 0)
        def _():
            head_ref[...] = gx_ref[...]

    row = pl.BlockSpec((bsz, CHUNK, d), lambda j: (0, j, 0))
    return pl.pallas_call(
        body, name=name, grid=(nc,),
        in_specs=[row, row, pl.BlockSpec((1, d), lambda j: (0, 0)), row],
        out_specs=[pl.BlockSpec((bsz, CHUNK, d), lambda j: (0, jnp.maximum(j - 1, 0), 0)),
                   pl.BlockSpec((bsz, CHUNK, d), lambda j: (0, 0, 0)), pl.BlockSpec((1, d), lambda j: (0, 0))],
        out_shape=[jax.ShapeDtypeStruct((bsz, seq, d), F32), jax.ShapeDtypeStruct((bsz, CHUNK, d), F32),
                   jax.ShapeDtypeStruct((1, d), F32)],
        compiler_params=_params("arbitrary"),
    )(dhn, h0, w, dres)


def _remote(src, dst, send_sem, recv_sem, dev):
    return pltpu.make_async_remote_copy(src_ref=src, dst_ref=dst, send_sem=send_sem, recv_sem=recv_sem,
                                        device_id=dev, device_id_type=MESH)


def _position():
    return lax.axis_index("x"), lax.axis_index("y"), lax.axis_index("c")


def _other_chips(pos):
    x, y, _ = pos
    return [(1 - x, y), (x, 1 - y), (1 - x, 1 - y)]


class _Gather:
    def __init__(self, arrs):
        n = len(arrs)
        self.args, self.n_in, self.n_out = list(arrs), n, n
        self.split = [a.ndim == 2 and a.shape[1] % (2 * LANES) == 0 for a in arrs]
        self.out_shape = [jax.ShapeDtypeStruct((4,) + a.shape, a.dtype) for a in arrs]
        self.scratch = [pltpu.SemaphoreType.DMA((3 * n,)), pltpu.SemaphoreType.DMA((3 * n,)),
                        pltpu.SemaphoreType.DMA((n,)), pltpu.SemaphoreType.DMA((3 * n,)),
                        pltpu.SemaphoreType.DMA((3 * n,))]

    def _copies(self, pos, ins, outs, sems):
        send_sems, recv_sems, loc_sems, pass_send_sems, pass_recv_sems = sems
        x, y, c = pos
        me, sibling = 2 * x + y, (x, y, 1 - c)
        local = [pltpu.make_async_copy(ins[i], outs[i].at[me], loc_sems.at[i]) for i in range(self.n_in)]
        sends, recvs, passes, pass_recvs = [], [], [], []
        for i in range(self.n_in):
            half = self.args[i].shape[1] // 2 if self.split[i] else None
            for k, (px, py) in enumerate(_other_chips(pos)):
                them = 2 * px + py
                sems_k = (send_sems.at[3 * i + k], recv_sems.at[3 * i + k], (px, py, c))
                if half is None:
                    sends.append(_remote(ins[i], outs[i].at[me], *sems_k))
                    recvs.append(_remote(ins[i], outs[i].at[them], *sems_k))
                    passes.append(None)
                    continue
                mine = pl.ds(pl.multiple_of(c * half, LANES), half)
                other = pl.ds(pl.multiple_of((1 - c) * half, LANES), half)
                sends.append(_remote(ins[i].at[:, mine], outs[i].at[me, :, mine], *sems_k))
                recvs.append(_remote(ins[i].at[:, mine], outs[i].at[them, :, mine], *sems_k))
                pass_k = (pass_send_sems.at[3 * i + k], pass_recv_sems.at[3 * i + k], sibling)
                passes.append(_remote(outs[i].at[them, :, mine], outs[i].at[them, :, mine], *pass_k))
                pass_recvs.append(_remote(outs[i].at[them, :, other], outs[i].at[them, :, other], *pass_k))
        return local, sends, recvs, passes, pass_recvs

    def start(self, pos, ins, outs, sems):
        local, sends = self._copies(pos, ins, outs, sems)[:2]
        for cp in local + sends:
            cp.start()

    def relay(self, pos, ins, outs, sems):
        _, _, recvs, passes, _ = self._copies(pos, ins, outs, sems)
        for cp, onward in zip(recvs, passes):
            if onward is not None:
                cp.wait_recv()
                onward.start()

    def finish(self, pos, ins, outs, sems):
        local, sends, recvs, passes, pass_recvs = self._copies(pos, ins, outs, sems)
        for cp, onward in zip(recvs, passes):
            if onward is None:
                cp.wait_recv()
        for cp in pass_recvs:
            cp.wait_recv()
        for cp in sends + [p for p in passes if p is not None]:
            cp.wait_send()
        for cp in local:
            cp.wait()


class _Exchange:
    FLIPS = [(fx, fy, fc) for fx in (0, 1) for fy in (0, 1) for fc in (0, 1)][1:]

    def __init__(self, big, small=None):
        n = len(big)
        self.n_big, self.has_small = n, small is not None
        self.args = list(big) + ([small] if self.has_small else [])
        self.n_in = self.n_out = len(self.args)
        self.out_shape = [jax.ShapeDtypeStruct(a.shape, a.dtype) for a in big]
        self.scratch = [pltpu.SemaphoreType.DMA((max(3 * n, 1),)), pltpu.SemaphoreType.DMA((max(3 * n, 1),)),
                        pltpu.SemaphoreType.DMA((n + 1,))]
        if self.has_small:
            self.out_shape.append(jax.ShapeDtypeStruct((8,) + small.shape, small.dtype))
            self.scratch += [pltpu.SemaphoreType.DMA((7,)), pltpu.SemaphoreType.DMA((7,))]

    def _copies(self, pos, ins, outs, sems):
        x, y, c = pos
        me, me8 = 2 * x + y, 4 * x + 2 * y + c
        local, sends, recvs = [], [], []
        for i in range(self.n_big):
            local.append(pltpu.make_async_copy(ins[i].at[me], outs[i].at[me], sems[2].at[i]))
            for k, (px, py) in enumerate(_other_chips(pos)):
                sems_k = (sems[0].at[3 * i + k], sems[1].at[3 * i + k], (px, py, c))
                sends.append(_remote(ins[i].at[2 * px + py], outs[i].at[me], *sems_k))
                recvs.append(_remote(ins[i].at[me], outs[i].at[2 * px + py], *sems_k))
        if self.has_small:
            small, landed = ins[self.n_big], outs[self.n_big]
            local.append(pltpu.make_async_copy(small, landed.at[me8], sems[2].at[self.n_big]))
            for k, (fx, fy, fc) in enumerate(self.FLIPS):
                peer = (x ^ fx, y ^ fy, c ^ fc)
                sems_k = (sems[3].at[k], sems[4].at[k], peer)
                sends.append(_remote(small, landed.at[me8], *sems_k))
                recvs.append(_remote(small, landed.at[4 * peer[0] + 2 * peer[1] + peer[2]], *sems_k))
        return local, sends, recvs, [None] * len(recvs), []

    start = _Gather.start
    relay = _Gather.relay
    finish = _Gather.finish


class _Swap:
    def __init__(self, arrs):
        n = len(arrs)
        self.args, self.n_in, self.n_out = list(arrs), n, n
        self.out_shape = [jax.ShapeDtypeStruct(a.shape, a.dtype) for a in arrs]
        self.scratch = [pltpu.SemaphoreType.DMA((n,)), pltpu.SemaphoreType.DMA((n,))]

    def _copies(self, pos, ins, outs, sems):
        x, y, c = pos
        both = [_remote(ins[i], outs[i], sems[0].at[i], sems[1].at[i], (x, y, 1 - c)) for i in range(self.n_in)]
        return [], both, both, [None] * len(both), []

    start = _Gather.start
    relay = _Gather.relay
    finish = _Gather.finish


def _comm(rider, *, name):
    a, b = rider.n_in, rider.n_in + rider.n_out

    def body(*refs):
        pos = _position()
        rider.start(pos, refs[:a], refs[a:b], refs[b:])
        rider.relay(pos, refs[:a], refs[a:b], refs[b:])
        rider.finish(pos, refs[:a], refs[a:b], refs[b:])

    return pl.pallas_call(body, name=name, in_specs=[ANY] * rider.n_in, out_specs=[ANY] * rider.n_out,
                          out_shape=rider.out_shape, scratch_shapes=rider.scratch)(*rider.args)


class _Ride:
    RELAY_AT = 0.8

    def __init__(self, rider, body, n_in, n_out, n_scratch, grid):
        self.rider = rider
        self.args = rider.args if rider else []
        self.in_specs = [ANY] * rider.n_in if rider else []
        self.out_specs = [ANY] * rider.n_out if rider else []
        self.out_shape = rider.out_shape if rider else []
        self.scratch = rider.scratch if rider else []
        self.body = self._wrap(body, n_in, n_out, n_scratch, grid) if rider else body

    def semantics(self, sem):
        return ("arbitrary",) * len(sem) if self.rider else sem

    def _wrap(self, body, n_in, n_out, n_scratch, grid):
        rider = self.rider
        a = n_in
        b = a + rider.n_in
        c = b + n_out
        d = c + rider.n_out
        e = d + n_scratch

        def wrapped(*refs):
            pos = _position()
            ids = [pl.program_id(i) for i in range(len(grid))]
            step, total = 0, 1
            for i, g in zip(ids, grid):
                step, total = step * g + i, total * g

            @pl.when(step == 0)
            def _():
                rider.start(pos, refs[a:b], refs[c:d], refs[e:])

            body(*refs[:a], *refs[b:c], *refs[d:e])

            @pl.when(step == int(self.RELAY_AT * (total - 1)))
            def _():
                rider.relay(pos, refs[a:b], refs[c:d], refs[e:])

            @pl.when(step == total - 1)
            def _():
                rider.finish(pos, refs[a:b], refs[c:d], refs[e:])

        return wrapped


def _elementwise_tiles(r, c):
    if r % 8 == 0 and r * c > 65536:
        tm = _pick(r, (256, 128, 64, 16, 8))
        return (tm, c), r // tm, lambda i: (i, 0)
    if r % 8 and c % 256 == 0 and r * c > 65536:
        return (r, 256), c // 256, lambda i: (0, i)
    return (r, c), 1, lambda i: (0, 0)


def _chip_sum(landed, *, name):
    _, r, c = landed.shape
    blk, steps, at = _elementwise_tiles(r, c)

    def body(land_ref, o_ref):
        acc = land_ref[0].astype(F32)
        for jchip in range(1, 4):
            acc = acc + land_ref[jchip].astype(F32)
        o_ref[...] = acc

    return pl.pallas_call(
        body, name=name, grid=(steps,), in_specs=[pl.BlockSpec((4,) + blk, lambda i: (0,) + at(i))],
        out_specs=pl.BlockSpec(blk, at), out_shape=jax.ShapeDtypeStruct((r, c), F32),
        compiler_params=_params("parallel"),
    )(landed)


def _device_sum(parts, *, name):
    _, r, c = parts.shape

    def body(p_ref, o_ref):
        acc = p_ref[0]
        for d in range(1, 8):
            acc = acc + p_ref[d]
        o_ref[...] = acc

    return pl.pallas_call(body, name=name, out_shape=jax.ShapeDtypeStruct((r, c), F32))(parts)


def _adamw_math(w, g, m, v):
    m = ADAM_B1 * m + (1.0 - ADAM_B1) * g
    v = ADAM_B2 * v + (1.0 - ADAM_B2) * (g * g)
    m_hat = m / (1.0 - ADAM_B1 ** ADAM_STEP)
    v_hat = v / (1.0 - ADAM_B2 ** ADAM_STEP)
    return -ADAM_LR * (m_hat / (jnp.sqrt(v_hat) + ADAM_EPS) + ADAM_WD * w), m, v


def _adamw(w, g_parts, m, v, *, name):
    r, c = w.shape
    shape, steps, at = _elementwise_tiles(r, c)
    n_g = len(g_parts)

    def body(*refs):
        w_ref, m_ref, v_ref = refs[n_g:n_g + 3]
        g_ref, d_ref, nm_ref, nv_ref = refs[n_g + 3:]
        g = refs[0][...]
        for p in refs[1:n_g]:
            g = g + p[...]
        g_ref[...] = g
        d_ref[...], nm_ref[...], nv_ref[...] = _adamw_math(w_ref[...], g, m_ref[...], v_ref[...])

    blk = pl.BlockSpec(shape, at)
    return pl.pallas_call(
        body, name=name, grid=(steps,), in_specs=[blk] * (n_g + 3), out_specs=[blk] * 4,
        out_shape=[jax.ShapeDtypeStruct((r, c), F32)] * 4, compiler_params=_params("parallel"),
    )(*g_parts, w, m, v)


def _pad_heads(v):
    return jnp.pad(v.reshape(N_GROUPS, 1, HPG), ((0, 0), (0, 0), (0, LANES - HPG)))


def _unpad_heads(v):
    return v[:, :HPG].reshape(1, N_HEADS)


_SMALL_EARLY = [("pool_w", (512, 128)), ("pool_scale", (1, 512)), ("conv_w", (4, D_XBC)), ("conv_b", (1, D_XBC)),
                ("dt_bias", (1, N_HEADS)), ("a_log", (1, N_HEADS)), ("d_skip", (1, N_HEADS)), ("ssm_norm_w", (1, D_SSM)),
                ("norm_ffn_w", (1, 1024)), ("norm_f_w", (1, 1024))]
_SMALL_LATE = [("norm_mix_w", (1, 1024)), ("meta", (N_META, 1024)), ("loss", (1, 1))]


def _pack_small(grads, layout):
    rows = []
    for nm, shape in layout:
        flat = grads[nm].reshape(-1)
        rows.append(jnp.pad(flat, (0, (-flat.size) % LANES)).reshape(-1, LANES))
    packed = jnp.concatenate(rows, axis=0)
    return jnp.pad(packed, ((0, (-packed.shape[0]) % 8), (0, 0)))


def _unpack_small(packed, layout):
    out, r0 = {}, 0
    for nm, shape in layout:
        size = shape[0] * shape[1]
        nrow = -(-size // LANES)
        out[nm] = packed[r0:r0 + nrow].reshape(-1)[:size].reshape(shape)
        r0 += nrow
    return out


def kernel(x, meta, norm_mix_w, w_in, pool_w, pool_scale, conv_w, conv_b, dt_bias, a_log, d_skip, ssm_norm_w, w_out, norm_ffn_w, w_ff1, w_ff2, norm_f_w, loss_target, m_meta, m_norm_mix_w, m_w_in, m_pool_w, m_pool_scale, m_conv_w, m_conv_b, m_dt_bias, m_a_log, m_d_skip, m_ssm_norm_w, m_w_out, m_norm_ffn_w, m_w_ff1, m_w_ff2, m_norm_f_w, v_meta, v_norm_mix_w, v_w_in, v_pool_w, v_pool_scale, v_conv_w, v_conv_b, v_dt_bias, v_a_log, v_d_skip, v_ssm_norm_w, v_w_out, v_norm_ffn_w, v_w_ff1, v_w_ff2, v_norm_f_w):
    bsz, seq, d = x.shape
    t = seq + CHUNK
    n = bsz * t
    chip = 2 * lax.axis_index("x") + lax.axis_index("y")
    d_in = w_in.shape[2] * 4

    g_conv, g_meta = _comm(_Gather([conv_w[0], meta]), name="gather_small")
    convw = g_conv.transpose(1, 0, 2).reshape(CONV_W, D_XBC)
    meta_full = g_meta.transpose(1, 0, 2).reshape(N_META, d)
    (h0, hn1), (g_in,) = _embed_norm(x, meta_full, norm_mix_w, name="embed_norm",
                                     rider=_Gather([w_in[0].T.astype(BF16)]))
    h0f, hn1 = h0.reshape(n, d), hn1.reshape(n, d)
    late_weights = _Gather([w_out[0].astype(BF16), w_ff1[0].astype(BF16), w_ff2[0].astype(BF16)])
    win = g_in.reshape(d_in, d)
    wu, wz = win[:D_POOL], win[D_POOL:D_POOL + D_SSM]
    wx = win[D_POOL + D_SSM:D_POOL + D_SSM + D_XBC]
    wdt = jnp.pad(win[D_POOL + D_SSM + D_XBC:].reshape(N_GROUPS, HPG, d),
                  ((0, 0), (0, LANES - HPG), (0, 0))).reshape(D_DT, d)
    dtb, alog = _pad_heads(dt_bias), _pad_heads(a_log)
    dskip = jnp.repeat(d_skip, HEAD_DIM, axis=1)
    poolw = pool_w[0]

    u, z, dtr, dt_, acs_, tr_ = _proj_uz_dt(hn1, wu, wz, wdt, dtb, alog, t // CHUNK, name="proj_uzdt")
    xbc, xc = _proj_conv(hn1, wx, convw, conv_b, name="proj_xbc")
    ypool = _pool_fwd(u.reshape(bsz, t, D_POOL), poolw, pool_scale, name="pool_fwd")
    xbc3 = xbc.reshape(bsz, t, D_XBC)
    xc = xc.reshape(bsz, t, D_XBC)
    z3, dtr3 = z.reshape(bsz, t, D_SSM), dtr.reshape(bsz, t, D_DT)
    dt3, acs3 = dt_.reshape(bsz, t, D_DT), acs_.reshape(bsz, t, D_DT)
    tr3 = tr_.reshape(bsz, t // CHUNK, N_GROUPS, 16, LANES)
    (yn, ypre, sprev), (g_out, g_ff1, g_ff2) = _ssd_fwd(xc, dt3, acs3, tr3, z3, dskip, ssm_norm_w, name="ssd_fwd",
                                                        rider=late_weights)
    wo = g_out.reshape(D_POOL + D_SSM, d)
    wo_p, wo_s = wo[:D_POOL], wo[D_POOL:]
    w1 = g_ff1
    w2 = g_ff2.reshape(D_FF, d)
    ypool_f, yn_f = ypool.reshape(n, D_POOL), yn.reshape(n, D_SSM)
    add = lambda r, e: r + e
    h1, hn2 = _mm([ypool_f, yn_f], [wo_p, wo_s], name="out_proj", post=add, extras=(h0f,), norm_w=norm_ffn_w)
    act = _mm(hn2, w1, name="ff1", out_dtype=BF16)
    relu2 = lambda a: jnp.square(jnp.maximum(a, 0))
    h2 = _mm(act, w2, name="ff2", pre=relu2, post=add, extras=(h1,))
    dh2, dh2b, loss_acc, d_norm_f = _final_norm_loss(h2.reshape(bsz, t, d), loss_target, norm_f_w.reshape(1, d),
                                                     name="loss")

    dh2f, dh2bf = dh2.reshape(n, d), dh2b.reshape(n, d)
    dact = _mm(dh2bf, w2, name="ff2_bwd", nt=True, post=lambda r, a: r * (2.0 * jnp.maximum(a, 0).astype(F32)),
               extras=(act,), out_dtype=BF16)
    d_w2 = _mm_tn(act, dh2bf, name="ff2_dw", tk=2048, tn=1024, pre=relu2)
    d_w1 = _mm_tn(hn2, dact, name="ff1_dw", tk=1024, tn=2048, slab=D_FF // 4)
    dh1, dh1b, d_norm_ffn = _mm_rms_bwd(dact, w1, h1, norm_ffn_w, dh2f, name="ff1_bwd")
    dypool, dyn = _mm_fanout(dh1b, [wo_p, wo_s], name="out_proj_bwd")
    d_wo = _mm_tn_cat([ypool_f, yn_f], dh1b, name="out_proj_dw")
    big_late = [d_wo.reshape(4, (D_POOL + D_SSM) // 4, d),
                d_w1, d_w2.reshape(4, D_FF // 4, d)]
    (dz, dxs, dbm, dcm, ddtr, d_nw, d_heads), landed_late = _ssd_bwd(
        xc, dtr3, dt3, acs3, tr3, z3, ypre, sprev, dyn.reshape(bsz, t, D_SSM), dtb, alog, dskip, ssm_norm_w, name="ssd_bwd",
        rider=_Exchange(big_late))
    mine_late = [_chip_sum(l, name=f"chip_sum_{i + 1}") for i, l in enumerate(landed_late)]
    (dxbc, d_convwb), theirs_late = _conv_bwd(xbc3, dxs, dbm, dcm, convw, conv_b, name="conv_bwd",
                                               rider=_Swap(mine_late))
    du, d_poolw, d_poolsc = _pool_bwd(u.reshape(bsz, t, D_POOL), dypool.reshape(bsz, t, D_POOL), poolw, pool_scale,
                                      name="pool_bwd")
    duf, dzf, dxbcf, ddtrf = du.reshape(n, D_POOL), dz.reshape(n, D_SSM), dxbc.reshape(n, D_XBC), ddtr.reshape(n, D_DT)
    heads = jnp.sum(d_heads, axis=0)
    small_early = _pack_small({
        "pool_w": d_poolw, "pool_scale": d_poolsc,
        "conv_w": jnp.sum(d_convwb[:, :CONV_W], axis=0), "conv_b": jnp.sum(d_convwb[:, CONV_W:CONV_W + 1], axis=0),
        "dt_bias": _unpad_heads(heads[:, 2]), "a_log": _unpad_heads(heads[:, 1]), "d_skip": _unpad_heads(heads[:, 0]),
        "ssm_norm_w": jnp.sum(d_nw, axis=0), "norm_ffn_w": d_norm_ffn, "norm_f_w": d_norm_f}, _SMALL_EARLY)
    d_wuzdt = _mm_tn_cat([duf, dzf, ddtrf], hn1, name="proj_uzdt_dw")
    d_wx, (early_all,) = _mm_tn(dxbcf, hn1, name="proj_xbc_dw", tk=1280, tn=1024, rider=_Exchange([], small_early))
    d_wdt = d_wuzdt[D_POOL + D_SSM:].reshape(N_GROUPS, LANES, d)[:, :HPG].reshape(N_HEADS, d)
    d_win = jnp.concatenate([d_wuzdt[:D_POOL + D_SSM], d_wx, d_wdt], axis=0)
    big_in = d_win.reshape(4, d_in // 4, d)
    dhn1, (landed_in,) = _mm([duf, dzf, dxbcf, ddtrf], [wu, wz, wx, wdt], name="proj_bwd",
                             rider=_Exchange([big_in]))
    grad_x, d_head_rows, d_norm_mix = _input_grad(
        dhn1.reshape(bsz, t, d), h0, norm_mix_w, dh1.reshape(bsz, t, d), seq, name="input_grad")

    small_late = _pack_small({"norm_mix_w": d_norm_mix, "meta": jnp.sum(d_head_rows[:, PAD:], axis=0),
                              "loss": loss_acc[0:1, 0:1]}, _SMALL_LATE)
    (late_all,) = _comm(_Exchange([], small_late), name="exchange_small")
    mine_in = _chip_sum(landed_in, name="chip_sum_0")
    (theirs_in,) = _comm(_Swap([mine_in]), name="swap_cores")
    mine, theirs = [mine_in] + mine_late, [theirs_in] + list(theirs_late)
    gsmall = {**_unpack_small(_device_sum(early_all, name="device_sum_early"), _SMALL_EARLY),
              **_unpack_small(_device_sum(late_all, name="device_sum_late"), _SMALL_LATE)}
    gsmall["conv_w"] = lax.dynamic_slice_in_dim(gsmall["conv_w"], chip * (D_XBC // 4), D_XBC // 4, axis=1)
    gsmall["meta"] = lax.dynamic_slice_in_dim(gsmall["meta"], chip * (d // 4), d // 4, axis=1)
    loss = gsmall["loss"][0, 0]

    given = dict(meta=(meta, m_meta, v_meta), norm_mix_w=(norm_mix_w, m_norm_mix_w, v_norm_mix_w),
                 w_in=(w_in, m_w_in, v_w_in), pool_w=(pool_w, m_pool_w, v_pool_w),
                 pool_scale=(pool_scale, m_pool_scale, v_pool_scale), conv_w=(conv_w, m_conv_w, v_conv_w),
                 conv_b=(conv_b, m_conv_b, v_conv_b), dt_bias=(dt_bias, m_dt_bias, v_dt_bias),
                 a_log=(a_log, m_a_log, v_a_log), d_skip=(d_skip, m_d_skip, v_d_skip),
                 ssm_norm_w=(ssm_norm_w, m_ssm_norm_w, v_ssm_norm_w), w_out=(w_out, m_w_out, v_w_out),
                 norm_ffn_w=(norm_ffn_w, m_norm_ffn_w, v_norm_ffn_w), w_ff1=(w_ff1, m_w_ff1, v_w_ff1),
                 w_ff2=(w_ff2, m_w_ff2, v_w_ff2), norm_f_w=(norm_f_w, m_norm_f_w, v_norm_f_w))
    big_names = ["w_in", "w_out", "w_ff1", "w_ff2"]
    results = {}
    for nm, (w, m, v) in given.items():
        if nm in big_names:
            i = big_names.index(nm)
            parts, shape2 = (mine[i], theirs[i]), mine[i].shape
        else:
            parts, shape2 = (gsmall[nm],), gsmall[nm].shape
        if nm == "w_in":
            outs = _adamw(w[0].T, parts, m[0].T, v[0].T, name=f"adamw_{nm}")
            results[nm] = [o.T[None] for o in outs]
        else:
            outs = _adamw(w.reshape(shape2), parts, m.reshape(shape2), v.reshape(shape2), name=f"adamw_{nm}")
            results[nm] = [o.reshape(w.shape) for o in outs]
    order = list(given)
    return (loss, grad_x, *[results[nm][0] for nm in order], *[results[nm][1] for nm in order],
            *[results[nm][2] for nm in order], *[results[nm][3] for nm in order])
```

```python
import jax
import jax.numpy as jnp
from jax import lax
from jax.experimental import pallas as pl
from jax.experimental.pallas import tpu as pltpu

F32 = jnp.float32
BF16 = jnp.bfloat16
MESH = pl.DeviceIdType.MESH
ANY = pl.BlockSpec(memory_space=pl.ANY)

D_MODEL = 1024
N_META = 16
CHUNK = 128
PAD = CHUNK - N_META
POOL_WINDOWS = (2, 4, 8, 16)
D_POOL = 512
POOL_GROUP = 128
D_SSM = 1536
N_HEADS = 24
N_GROUPS = 4
HPG = 6
HEAD_DIM = 64
D_STATE = 128
GW = HPG * HEAD_DIM
D_XBC = D_SSM + 2 * N_GROUPS * D_STATE
D_DT = N_GROUPS * 128
D_FF = 4096
CONV_W = 4
EPS = 1e-5
LANES = 128
VMEM_LIMIT = 56 * 1024 * 1024

ADAM_LR, ADAM_B1, ADAM_B2, ADAM_EPS, ADAM_WD, ADAM_STEP = 0.001, 0.9, 0.999, 1e-08, 0.01, 10


def _params(*sem):
    return pltpu.CompilerParams(dimension_semantics=sem, vmem_limit_bytes=VMEM_LIMIT)


def _pick(n, cands):
    for c in cands:
        if n % c == 0:
            return c
    raise ValueError(f"no block size for {n}")


def _dot(a, b):
    return jnp.dot(a.astype(BF16), b.astype(BF16), preferred_element_type=F32)


def _dot_nt(a, b):
    return lax.dot_general(a.astype(BF16), b.astype(BF16), (((1,), (1,)), ((), ())), preferred_element_type=F32)


def _dot_tn(a, b):
    return lax.dot_general(a.astype(BF16), b.astype(BF16), (((0,), (0,)), ((), ())), preferred_element_type=F32)


def _dot_exact(mask, x, terms=3):
    m = mask.astype(BF16)
    dot = lambda t: jnp.dot(m, t, preferred_element_type=F32)
    hi = x.astype(BF16)
    r1 = x - hi.astype(F32)
    mid = r1.astype(BF16)
    if terms == 2:
        return dot(hi) + dot(mid)
    lo = (r1 - mid.astype(F32)).astype(BF16)
    return dot(hi) + dot(mid) + dot(lo)


def _sigmoid(x):
    return 1.0 / (1.0 + jnp.exp(-x))


def _softplus(x):
    return jnp.maximum(x, 0.0) + jnp.log1p(jnp.exp(-jnp.abs(x)))


def _sum_all(x):
    return jnp.sum(jnp.sum(x, axis=1, keepdims=True), axis=0, keepdims=True)


ROW_TILES = (1056, 768, 704, 512, 384, 256, 128)
TILE_BUDGET = 28 * 1024 * 1024


def _row_tile(n, bytes_per_row, fixed_bytes, budget=TILE_BUDGET):
    for tm in ROW_TILES:
        if n % tm == 0 and 2 * (tm * bytes_per_row + fixed_bytes) <= budget:
            return tm
    raise ValueError(f"no row tile for {n}")


WIDE_BUDGET = 38 * 1024 * 1024


def _mm(a, w, *, name, tn=512, nt=False, pre=None, post=None, extras=(), out_dtype=F32, norm_w=None, rider=None):
    assert norm_w is None or (rider is None and out_dtype == F32)
    a_list = list(a) if isinstance(a, (list, tuple)) else [a]
    w_list = list(w) if isinstance(w, (list, tuple)) else [w]
    n_a, n_ex = len(a_list), len(extras)
    n = a_list[0].shape[0]
    shard = w_list[0].shape[2] if w_list[0].ndim == 3 else None
    assert shard is None or (not nt and n_a == 1 and shard % tn == 0)
    m = w_list[0].shape[0] * shard if shard else w_list[0].shape[0] if nt else w_list[0].shape[1]
    tn = min(tn, m)
    size = lambda dt: jnp.dtype(dt).itemsize
    per_row = (sum(x.shape[1] * size(x.dtype) for x in a_list) + m * size(out_dtype)
               + sum(m * size(e.dtype) for e in extras) + (2 * m if norm_w is not None else 0))
    tm = _row_tile(n, per_row, sum(x.size * size(x.dtype) for x in w_list) // 2, WIDE_BUDGET)
    n_norm = 0 if norm_w is None else 1

    def body(*refs):
        a_refs, w_refs, ex_refs = refs[:n_a], refs[n_a:2 * n_a], refs[2 * n_a:2 * n_a + n_ex]
        o_ref = refs[2 * n_a + n_ex + n_norm]
        avs = [(a_ref[...] if pre is None else pre(a_ref[...])).astype(BF16) for a_ref in a_refs]
        for c0 in range(0, m, tn):
            r = None
            for av, w_ref in zip(avs, w_refs):
                if shard:
                    term = _dot(av, w_ref[c0 // shard, :, c0 % shard:c0 % shard + tn])
                else:
                    term = _dot_nt(av, w_ref[c0:c0 + tn, :]) if nt else _dot(av, w_ref[:, c0:c0 + tn])
                r = term if r is None else r + term
            if post is not None:
                r = post(r, *[e[:, c0:c0 + tn] for e in ex_refs])
            o_ref[:, c0:c0 + tn] = r.astype(out_dtype)
        if n_norm:
            x = o_ref[...]
            scale = lax.rsqrt(jnp.mean(x * x, axis=-1, keepdims=True) + EPS)
            refs[2 * n_a + n_ex + 2][...] = (x * scale * refs[2 * n_a + n_ex][...]).astype(BF16)

    a_specs = [pl.BlockSpec((tm, x.shape[1]), lambda i: (i, 0)) for x in a_list]
    w_specs = [pl.BlockSpec(x.shape, lambda i, nd=x.ndim: (0,) * nd, pipeline_mode=pl.Buffered(1)) for x in w_list]
    blk = pl.BlockSpec((tm, m), lambda i: (i, 0))
    vec = [pl.BlockSpec((1, m), lambda i: (0, 0))] * n_norm
    grid = (n // tm,)
    ride = _Ride(rider, body, 2 * n_a + n_ex + n_norm, 1 + n_norm, 0, grid)
    outs = pl.pallas_call(
        ride.body, name=name, grid=grid,
        in_specs=a_specs + w_specs + [blk] * n_ex + vec + ride.in_specs,
        out_specs=[blk] * (1 + n_norm) + ride.out_specs,
        out_shape=[jax.ShapeDtypeStruct((n, m), out_dtype)] + [jax.ShapeDtypeStruct((n, m), BF16)] * n_norm + ride.out_shape,
        scratch_shapes=ride.scratch, compiler_params=_params(*ride.semantics(("parallel",))),
    )(*a_list, *w_list, *extras, *([norm_w] * n_norm), *ride.args)
    if n_norm:
        return outs[0], outs[1]
    return (outs[0], outs[1:]) if rider else outs[0]


def _mm_fanout(a, ws, *, name, tn=512):
    n, k = a.shape
    ms = [w.shape[0] for w in ws]
    tm = _row_tile(n, k * 2 + 4 * sum(ms), sum(w.size for w in ws), WIDE_BUDGET)
    n_w = len(ws)

    def body(a_ref, *refs):
        av = a_ref[...]
        for w_ref, o_ref, m in zip(refs[:n_w], refs[n_w:], ms):
            for c0 in range(0, m, tn):
                o_ref[:, c0:c0 + tn] = _dot_nt(av, w_ref[c0:c0 + tn, :])

    return pl.pallas_call(
        body, name=name, grid=(n // tm,),
        in_specs=[pl.BlockSpec((tm, k), lambda i: (i, 0))]
        + [pl.BlockSpec(w.shape, lambda i: (0, 0), pipeline_mode=pl.Buffered(1)) for w in ws],
        out_specs=[pl.BlockSpec((tm, m), lambda i: (i, 0)) for m in ms],
        out_shape=[jax.ShapeDtypeStruct((n, m), F32) for m in ms],
        compiler_params=_params("parallel"),
    )(a, *ws)


def _mm_tn(a, g, *, name, tk, tn, pre=None, slab=None, rider=None):
    n, k = a.shape
    m = g.shape[1]
    tk, tn = min(tk, k), min(tn, m)
    tm = _row_tile(n, tk * jnp.dtype(a.dtype).itemsize + tn * jnp.dtype(g.dtype).itemsize, tk * tn * 4)
    steps = n // tm

    def body(a_ref, g_ref, o_ref, acc_ref):
        r = pl.program_id(2)

        @pl.when(r == 0)
        def _():
            acc_ref[...] = jnp.zeros_like(acc_ref)

        av = a_ref[...]
        if pre is not None:
            av = pre(av)
        if slab:
            for s in range(tn // slab):
                acc_ref[s] += _dot_tn(av, g_ref[:, s * slab:(s + 1) * slab])
        else:
            acc_ref[...] += _dot_tn(av, g_ref[...])

        @pl.when(r == steps - 1)
        def _():
            o_ref[...] = acc_ref[...].astype(BF16)

    if slab:
        block, out_spec = (tn // slab, tk, slab), pl.BlockSpec((tn // slab, tk, slab), lambda i, j, r: (j, i, 0))
        out_shape = jax.ShapeDtypeStruct((m // slab, k, slab), BF16)
    else:
        block, out_spec = (tk, tn), pl.BlockSpec((tk, tn), lambda i, j, r: (i, j))
        out_shape = jax.ShapeDtypeStruct((k, m), BF16)
    grid = (k // tk, m // tn, steps)
    ride = _Ride(rider, body, 2, 1, 1, grid)
    outs = pl.pallas_call(
        ride.body, name=name, grid=grid,
        in_specs=[pl.BlockSpec((tm, tk), lambda i, j, r: (r, i)), pl.BlockSpec((tm, tn), lambda i, j, r: (r, j))]
        + ride.in_specs,
        out_specs=[out_spec] + ride.out_specs, out_shape=[out_shape] + ride.out_shape,
        scratch_shapes=[pltpu.VMEM(block, F32)] + ride.scratch,
        compiler_params=_params(*ride.semantics(("parallel", "parallel", "arbitrary"))),
    )(a, g, *ride.args)
    return (outs[0], outs[1:]) if rider else outs[0]


def _mm_tn_cat(a_list, g, *, name):
    n, m = g.shape
    ks = [a.shape[1] for a in a_list]
    size = lambda x: jnp.dtype(x.dtype).itemsize
    tm = _row_tile(n, sum(a.shape[1] * size(a) for a in a_list) + m * size(g), sum(ks) * m * 4)
    steps, n_a = n // tm, len(a_list)

    def body(*refs):
        g_ref, o_ref, acc_ref = refs[n_a], refs[n_a + 1], refs[n_a + 2]
        r = pl.program_id(0)

        @pl.when(r == 0)
        def _():
            acc_ref[...] = jnp.zeros_like(acc_ref)

        gv, k0 = g_ref[...], 0
        for a_ref, k in zip(refs[:n_a], ks):
            acc_ref[k0:k0 + k, :] += _dot_tn(a_ref[...], gv)
            k0 += k

        @pl.when(r == steps - 1)
        def _():
            o_ref[...] = acc_ref[...].astype(BF16)

    return pl.pallas_call(
        body, name=name, grid=(steps,),
        in_specs=[pl.BlockSpec((tm, k), lambda r: (r, 0)) for k in ks] + [pl.BlockSpec((tm, m), lambda r: (r, 0))],
        out_specs=pl.BlockSpec((sum(ks), m), lambda r: (0, 0)),
        out_shape=jax.ShapeDtypeStruct((sum(ks), m), BF16),
        scratch_shapes=[pltpu.VMEM((sum(ks), m), F32)],
        compiler_params=_params("arbitrary"),
    )(*a_list, g)


def _mm_rms_bwd(a, w, h, w_norm, dres, *, name):
    n, k = a.shape
    d = h.shape[1]
    slabs, _, ks = w.shape
    tm = _row_tile(n, k * jnp.dtype(a.dtype).itemsize + d * (4 + 4 + 4 + 2), d * k, WIDE_BUDGET)

    def body(a_ref, w_ref, h_ref, wn_ref, dres_ref, dx_ref, dxb_ref, dw_ref):
        @pl.when(pl.program_id(0) == 0)
        def _():
            dw_ref[...] = jnp.zeros_like(dw_ref)

        dyv = None
        for s in range(slabs):
            part = _dot_nt(a_ref[:, s * ks:(s + 1) * ks], w_ref[s])
            dyv = part if dyv is None else dyv + part
        x = h_ref[...]
        r = lax.rsqrt(jnp.mean(x * x, axis=-1, keepdims=True) + EPS)
        g = dyv * wn_ref[...]
        dx = r * (g - x * (r * r) * jnp.mean(g * x, axis=-1, keepdims=True)) + dres_ref[...]
        dx_ref[...] = dx
        dxb_ref[...] = dx.astype(BF16)
        dw_ref[...] += jnp.sum(dyv * x * r, axis=0, keepdims=True)

    row = pl.BlockSpec((tm, d), lambda i: (i, 0))
    vec = pl.BlockSpec((1, d), lambda i: (0, 0))
    return pl.pallas_call(
        body, name=name, grid=(n // tm,),
        in_specs=[pl.BlockSpec((tm, k), lambda i: (i, 0)),
                  pl.BlockSpec(w.shape, lambda i: (0, 0, 0), pipeline_mode=pl.Buffered(1)), row, vec, row],
        out_specs=[row, row, vec],
        out_shape=[jax.ShapeDtypeStruct((n, d), F32), jax.ShapeDtypeStruct((n, d), BF16), jax.ShapeDtypeStruct((1, d), F32)],
        compiler_params=_params("arbitrary"),
    )(a, w, h, w_norm, dres)


def _embed_norm(x, meta, w, *, name, rider=None):
    bsz, seq, d = x.shape
    t = seq + CHUNK
    nc = t // CHUNK

    def body(x_ref, meta_ref, w_ref, h_ref, hn_ref):
        j = pl.program_id(0)
        first = jnp.concatenate([jnp.zeros((PAD, d), F32), meta_ref[...]], axis=0)
        for e in range(bsz):
            h = jnp.where(j == 0, first, x_ref[e])
            r = lax.rsqrt(jnp.mean(h * h, axis=-1, keepdims=True) + EPS)
            h_ref[e] = h
            hn_ref[e] = (h * r * w_ref[...]).astype(BF16)

    row = pl.BlockSpec((bsz, CHUNK, d), lambda j: (0, j, 0))
    grid = (nc,)
    ride = _Ride(rider, body, 3, 2, 0, grid)
    outs = pl.pallas_call(
        ride.body, name=name, grid=grid,
        in_specs=[pl.BlockSpec((bsz, CHUNK, d), lambda j: (0, jnp.maximum(j - 1, 0), 0)),
                  pl.BlockSpec((N_META, d), lambda j: (0, 0)), pl.BlockSpec((1, d), lambda j: (0, 0))] + ride.in_specs,
        out_specs=[row, row] + ride.out_specs,
        out_shape=[jax.ShapeDtypeStruct((bsz, t, d), F32), jax.ShapeDtypeStruct((bsz, t, d), BF16)] + ride.out_shape,
        scratch_shapes=ride.scratch, compiler_params=_params(*ride.semantics(("parallel",))),
    )(x, meta, w, *ride.args)
    return outs[:2], outs[2:]


def _final_norm_loss(h2, target, w, *, name):
    bsz, t, d = h2.shape
    nc = t // CHUNK

    def body(h_ref, t_ref, w_ref, dh_ref, dhb_ref, loss_ref, dw_ref):
        j = pl.program_id(0)

        @pl.when(j == 0)
        def _():
            loss_ref[...] = jnp.zeros_like(loss_ref)
            dw_ref[...] = jnp.zeros_like(dw_ref)

        wv = w_ref[...]
        for e in range(bsz):
            x = h_ref[e]
            r = lax.rsqrt(jnp.mean(x * x, axis=-1, keepdims=True) + EPS)
            diff = jnp.where(j > 0, x * r * wv - t_ref[e], 0.0)
            loss_ref[...] += _sum_all(diff * diff) * (0.5 / d)
            dy = diff * (1.0 / d)
            g = dy * wv
            dh = r * (g - x * (r * r) * jnp.mean(g * x, axis=-1, keepdims=True))
            dh_ref[e] = dh
            dhb_ref[e] = dh.astype(BF16)
            dw_ref[...] += jnp.sum(dy * x * r, axis=0, keepdims=True)

    row = pl.BlockSpec((bsz, CHUNK, d), lambda j: (0, j, 0))
    return pl.pallas_call(
        body, name=name, grid=(nc,),
        in_specs=[row, pl.BlockSpec((bsz, CHUNK, d), lambda j: (0, jnp.maximum(j - 1, 0), 0)),
                  pl.BlockSpec((1, d), lambda j: (0, 0))],
        out_specs=[row, row, pl.BlockSpec((8, LANES), lambda j: (0, 0)), pl.BlockSpec((1, d), lambda j: (0, 0))],
        out_shape=[jax.ShapeDtypeStruct((bsz, t, d), F32), jax.ShapeDtypeStruct((bsz, t, d), BF16),
                   jax.ShapeDtypeStruct((8, LANES), F32), jax.ShapeDtypeStruct((1, d), F32)],
        compiler_params=_params("arbitrary"),
    )(h2, target, w)


def _pool_masks(j, transposed):
    r = lax.broadcasted_iota(jnp.int32, (CHUNK, 2 * CHUNK), 0)
    c = lax.broadcasted_iota(jnp.int32, (CHUNK, 2 * CHUNK), 1)
    masks = []
    for w in POOL_WINDOWS:
        if transposed:
            m = (c >= r) & (c < r + w)
        else:
            s = c - CHUNK
            m = (s <= r) & (s > r - w) & (s + j * CHUNK >= 0)
        masks.append(m.astype(F32))
    return masks


POOL_TERMS = 2


def _pool_count(t_global, w):
    return jnp.clip(t_global - PAD + 1, 1, w).astype(F32)


def _pool_fwd(u, pool_w, pool_scale, *, name):
    bsz, t, _ = u.shape
    nc = t // CHUNK

    def body(prev_ref, cur_ref, pw_ref, sc_ref, o_ref):
        j = pl.program_id(0)
        masks = _pool_masks(j, False)
        tg = j * CHUNK + lax.broadcasted_iota(jnp.int32, (CHUNK, 1), 0)
        count = [_pool_count(tg, w) for w in POOL_WINDOWS]
        units = [(e, gi) for e in range(bsz) for gi in range(len(POOL_WINDOWS))]
        sl = lambda gi: pl.ds(gi * POOL_GROUP, POOL_GROUP)
        cur = {(e, gi): cur_ref[e, :, sl(gi)] for e, gi in units}
        both = {(e, gi): jnp.concatenate([prev_ref[e, :, sl(gi)], cur[e, gi]], axis=0) for e, gi in units}
        win = {(e, gi): _dot_exact(masks[gi], both[e, gi], POOL_TERMS) for e, gi in units}
        pooled = {(e, gi): win[e, gi] / count[gi] - cur[e, gi] for e, gi in units}
        mixed = {(e, gi): _dot(pooled[e, gi], pw_ref[gi]) for e, gi in units}
        for e, gi in units:
            o_ref[e, :, sl(gi)] = (mixed[e, gi] * sc_ref[:, sl(gi)]).astype(BF16)

    blk = lambda f: pl.BlockSpec((bsz, CHUNK, D_POOL), f)
    return pl.pallas_call(
        body, name=name, grid=(nc,),
        in_specs=[blk(lambda j: (0, jnp.maximum(j - 1, 0), 0)), blk(lambda j: (0, j, 0)),
                  pl.BlockSpec((4, POOL_GROUP, POOL_GROUP), lambda j: (0, 0, 0)),
                  pl.BlockSpec((1, D_POOL), lambda j: (0, 0))],
        out_specs=blk(lambda j: (0, j, 0)), out_shape=jax.ShapeDtypeStruct(u.shape, BF16),
        compiler_params=_params("parallel"),
    )(u, u, pool_w, pool_scale)


def _pool_bwd(u, dyp, pool_w, pool_scale, *, name):
    bsz, t, _ = u.shape
    nc = t // CHUNK

    def body(prev_ref, cur_ref, dy_ref, dyn_ref, pw_ref, sc_ref, du_ref, dpw_ref, dsc_ref):
        j = pl.program_id(0)

        @pl.when(j == 0)
        def _():
            dpw_ref[...] = jnp.zeros_like(dpw_ref)
            dsc_ref[...] = jnp.zeros_like(dsc_ref)

        fwd = _pool_masks(j, False)
        bwd = _pool_masks(j, True)
        tg = j * CHUNK + lax.broadcasted_iota(jnp.int32, (CHUNK, 1), 0)
        count = [_pool_count(tg, w) for w in POOL_WINDOWS]
        count_next = [_pool_count(tg + CHUNK, w) for w in POOL_WINDOWS]
        has_next = j < nc - 1
        groups = range(len(POOL_WINDOWS))
        units = [(e, gi) for e in range(bsz) for gi in groups]
        sl = lambda gi: pl.ds(gi * POOL_GROUP, POOL_GROUP)
        cur = {(e, gi): cur_ref[e, :, sl(gi)] for e, gi in units}
        both = {(e, gi): jnp.concatenate([prev_ref[e, :, sl(gi)], cur[e, gi]], axis=0) for e, gi in units}
        win = {(e, gi): _dot_exact(fwd[gi], both[e, gi], POOL_TERMS) for e, gi in units}
        pooled = {(e, gi): win[e, gi] / count[gi] - cur[e, gi] for e, gi in units}
        dy = {(e, gi): dy_ref[e, :, sl(gi)] for e, gi in units}
        mixed = {(e, gi): _dot(pooled[e, gi], pw_ref[gi]) for e, gi in units}
        dm = {(e, gi): dy[e, gi] * sc_ref[:, sl(gi)] for e, gi in units}
        dm_next = {(e, gi): jnp.where(has_next, dyn_ref[e, :, sl(gi)], 0.0) * sc_ref[:, sl(gi)] for e, gi in units}
        dpw = {(e, gi): _dot_tn(pooled[e, gi], dm[e, gi]) for e, gi in units}
        dpooled = {(e, gi): _dot_nt(dm[e, gi], pw_ref[gi]) for e, gi in units}
        dpooled_next = {(e, gi): _dot_nt(dm_next[e, gi], pw_ref[gi]) for e, gi in units}
        spread = {(e, gi): jnp.concatenate([dpooled[e, gi] / count[gi], dpooled_next[e, gi] / count_next[gi]], axis=0)
                  for e, gi in units}
        back = {(e, gi): _dot_exact(bwd[gi], spread[e, gi], POOL_TERMS) for e, gi in units}
        for e, gi in units:
            du_ref[e, :, sl(gi)] = (back[e, gi] - dpooled[e, gi]).astype(BF16)
        for gi in groups:
            dsc, dw = None, None
            for e in range(bsz):
                term = jnp.sum(dy[e, gi] * mixed[e, gi], axis=0, keepdims=True)
                dsc = term if dsc is None else dsc + term
                dw = dpw[e, gi] if dw is None else dw + dpw[e, gi]
            dsc_ref[:, sl(gi)] += dsc
            dpw_ref[gi] += dw

    blk = lambda f: pl.BlockSpec((bsz, CHUNK, D_POOL), f)
    return pl.pallas_call(
        body, name=name, grid=(nc,),
        in_specs=[blk(lambda j: (0, jnp.maximum(j - 1, 0), 0)), blk(lambda j: (0, j, 0)),
                  blk(lambda j: (0, j, 0)), blk(lambda j: (0, jnp.minimum(j + 1, nc - 1), 0)),
                  pl.BlockSpec((4, POOL_GROUP, POOL_GROUP), lambda j: (0, 0, 0)),
                  pl.BlockSpec((1, D_POOL), lambda j: (0, 0))],
        out_specs=[blk(lambda j: (0, j, 0)), pl.BlockSpec((4, POOL_GROUP, POOL_GROUP), lambda j: (0, 0, 0)),
                   pl.BlockSpec((1, D_POOL), lambda j: (0, 0))],
        out_shape=[jax.ShapeDtypeStruct(u.shape, BF16), jax.ShapeDtypeStruct((4, POOL_GROUP, POOL_GROUP), F32),
                   jax.ShapeDtypeStruct((1, D_POOL), F32)],
        compiler_params=_params("arbitrary"),
    )(u, u, dyp, dyp, pool_w, pool_scale)


CONV_SLAB = 512


def _conv_taps(tail, cur, keep_tail):
    ext = jnp.concatenate([jnp.where(keep_tail, tail, 0.0), cur], axis=0)
    return [(pltpu.roll(ext, CONV_W - 1 - k, 0) if k < CONV_W - 1 else ext)[8:] for k in range(CONV_W)]


def _conv_pre(taps, w_ref, b_ref, sl):
    acc = b_ref[:, sl]
    for k in range(CONV_W):
        acc = acc + w_ref[k:k + 1, sl] * taps[k]
    return acc


def _proj_conv(hn, w, conv_w, conv_b, *, name):
    n, d = hn.shape
    c = w.shape[0]
    assert PAD >= CONV_W - 1
    tm = _row_tile(n, d * 2 + c * (4 + 2), c * d, WIDE_BUDGET)

    def body(hn_ref, w_ref, cw_ref, cb_ref, xbc_ref, xc_ref, tail_ref):
        @pl.when(pl.program_id(0) == 0)
        def _():
            tail_ref[...] = jnp.zeros_like(tail_ref)

        av = hn_ref[...]
        starts = list(range(0, c, CONV_SLAB))

        def project(c0):
            xbc_ref[:, pl.ds(c0, CONV_SLAB)] = _dot_nt(av, w_ref[c0:c0 + CONV_SLAB, :])

        def convolve(c0):
            sl = pl.ds(c0, CONV_SLAB)
            xb = xbc_ref[:, sl]
            pre = _conv_pre(_conv_taps(tail_ref[:, sl], xb, True), cw_ref, cb_ref, sl)
            xc_ref[:, sl] = (pre * _sigmoid(pre)).astype(BF16)
            tail_ref[:, sl] = xb[tm - 8:, :]

        project(starts[0])
        for c0, c_next in zip(starts, starts[1:] + [None]):
            if c_next is not None:
                project(c_next)
            convolve(c0)

    row = lambda width: pl.BlockSpec((tm, width), lambda i: (i, 0))
    return pl.pallas_call(
        body, name=name, grid=(n // tm,),
        in_specs=[row(d), pl.BlockSpec(w.shape, lambda i: (0, 0), pipeline_mode=pl.Buffered(1)),
                  pl.BlockSpec((CONV_W, c), lambda i: (0, 0)), pl.BlockSpec((1, c), lambda i: (0, 0))],
        out_specs=[row(c), row(c)],
        out_shape=[jax.ShapeDtypeStruct((n, c), F32), jax.ShapeDtypeStruct((n, c), BF16)],
        scratch_shapes=[pltpu.VMEM((8, c), F32)],
        compiler_params=_params("arbitrary"),
    )(hn, w, conv_w, conv_b)


def _conv_bwd(xbc, dxs, db, dc, conv_w, conv_b, *, name, rider=None):
    bsz, t, c = xbc.shape
    tile = _pick(t, (3 * CHUNK, CHUNK))
    nc = t // tile
    halo = 16
    rows = tile + halo

    def body(tail_ref, cur_ref, head_ref, dxs_ref, db_ref, dc_ref, dxs_head, db_head, dc_head, w_ref, b_ref,
             dx_ref, dwb_ref):
        j = pl.program_id(1)

        @pl.when(j == 0)
        def _():
            dwb_ref[...] = jnp.zeros_like(dwb_ref)

        has_prev, has_next = j > 0, j < nc - 1
        for c0 in range(0, c, CONV_SLAB):
            sl = pl.ds(c0, CONV_SLAB)
            if c0 < D_SSM:
                dxc, dxc_next = dxs_ref[0, :, sl], dxs_head[0, :, sl]
            elif c0 < D_SSM + D_POOL:
                dxc, dxc_next = db_ref[0], db_head[0]
            else:
                dxc, dxc_next = dc_ref[0], dc_head[0]
            dxc = jnp.concatenate([dxc.astype(F32), jnp.where(has_next, dxc_next.astype(F32), 0.0)], axis=0)
            ext = jnp.concatenate([jnp.where(has_prev, tail_ref[0, :, sl], 0.0), cur_ref[0, :, sl],
                                   jnp.where(has_next, head_ref[0, :, sl], 0.0)], axis=0)
            taps = [(pltpu.roll(ext, CONV_W - 1 - k, 0) if k < CONV_W - 1 else ext)[8:] for k in range(CONV_W)]
            pre = _conv_pre(taps, w_ref, b_ref, sl)
            s = _sigmoid(pre)
            dpre = dxc * (s * (1.0 + pre * (1.0 - s)))
            acc = w_ref[CONV_W - 1:CONV_W, sl] * dpre[:tile]
            for k in range(CONV_W - 1):
                up = CONV_W - 1 - k
                acc = acc + w_ref[k:k + 1, sl] * pltpu.roll(dpre, rows - up, 0)[:tile]
            dx_ref[0, :, sl] = acc.astype(BF16)
            for k in range(CONV_W):
                dwb_ref[0, k:k + 1, sl] += jnp.sum(dpre[:tile] * taps[k][:tile], axis=0, keepdims=True)
            dwb_ref[0, CONV_W:CONV_W + 1, sl] += jnp.sum(dpre[:tile], axis=0, keepdims=True)

    assert CONV_SLAB == D_POOL and D_SSM % CONV_SLAB == 0
    row = lambda width: pl.BlockSpec((1, tile, width), lambda b, j: (b, j, 0))
    nxt = lambda width: pl.BlockSpec(
        (1, halo, width), lambda b, j: (b, jnp.minimum((j + 1) * (tile // halo), t // halo - 1), 0))
    grid = (bsz, nc)
    ride = _Ride(rider, body, 11, 2, 0, grid)
    outs = pl.pallas_call(
        ride.body, name=name, grid=grid,
        in_specs=[pl.BlockSpec((1, 8, c), lambda b, j: (b, jnp.maximum(j * (tile // 8) - 1, 0), 0)), row(c), nxt(c),
                  row(D_SSM), row(D_POOL), row(D_POOL), nxt(D_SSM), nxt(D_POOL), nxt(D_POOL),
                  pl.BlockSpec((CONV_W, c), lambda b, j: (0, 0)), pl.BlockSpec((1, c), lambda b, j: (0, 0))]
        + ride.in_specs,
        out_specs=[row(c), pl.BlockSpec((1, 8, c), lambda b, j: (b, 0, 0))] + ride.out_specs,
        out_shape=[jax.ShapeDtypeStruct(xbc.shape, BF16), jax.ShapeDtypeStruct((bsz, 8, c), F32)] + ride.out_shape,
        scratch_shapes=ride.scratch, compiler_params=_params(*ride.semantics(("parallel", "arbitrary"))),
    )(xbc, xbc, xbc, dxs, db, dc, dxs, db, dc, conv_w, conv_b, *ride.args)
    return outs[:2], outs[2:]


def _dt_valid(j):
    lane = lax.broadcasted_iota(jnp.int32, (CHUNK, LANES), 1)
    row = lax.broadcasted_iota(jnp.int32, (CHUNK, LANES), 0)
    return (lane < HPG) & ((j > 0) | (row >= PAD))


def _proj_uz_dt(hn, wu, wz, wdt, dtb, alog, nc, *, name):
    n, d = hn.shape
    tm = _pick(n, (768, 384, 128))
    per_tile = tm // CHUNK
    widths = (wu.shape[0], wz.shape[0], wdt.shape[0])

    def body(hn_ref, wu_ref, wz_ref, wdt_ref, dtb_ref, alog_ref, u_ref, z_ref, dtr_ref, dt_ref, acs_ref, tr_ref):
        i = pl.program_id(0)
        av = hn_ref[...]
        for w_ref, o_ref, m in zip((wu_ref, wz_ref, wdt_ref), (u_ref, z_ref, dtr_ref), widths):
            for c0 in range(0, m, 512):
                o_ref[:, c0:c0 + 512] = _dot_nt(av, w_ref[c0:c0 + 512, :])
        row = lax.broadcasted_iota(jnp.int32, (CHUNK, LANES), 0)
        lane = lax.broadcasted_iota(jnp.int32, (CHUNK, LANES), 1)
        tril = (row >= lane).astype(F32)
        units = [(cc, g) for cc in range(per_tile) for g in range(N_GROUPS)]
        at = lambda cc, g: (pl.ds(cc * CHUNK, CHUNK), pl.ds(g * LANES, LANES))
        valid = [(lane < HPG) & (((i * per_tile + cc) % nc > 0) | (row >= PAD)) for cc in range(per_tile)]
        dt = {(cc, g): jnp.where(valid[cc], _softplus(dtr_ref[at(cc, g)] + dtb_ref[g]), 0.0) for cc, g in units}
        acs = {(cc, g): _dot_exact(tril, dt[cc, g] * -jnp.exp(alog_ref[g])) for cc, g in units}
        for cc, g in units:
            dt_ref[at(cc, g)] = dt[cc, g]
            acs_ref[at(cc, g)] = acs[cc, g]
            tr_ref[cc, g, 0:8, :] = dt[cc, g].T[0:8]
            tr_ref[cc, g, 8:16, :] = acs[cc, g].T[0:8]

    row_blk = lambda width: pl.BlockSpec((tm, width), lambda i: (i, 0))
    whole = lambda w: pl.BlockSpec(w.shape, lambda i: (0, 0), pipeline_mode=pl.Buffered(1))
    const = pl.BlockSpec((N_GROUPS, 1, LANES), lambda i: (0, 0, 0))
    return pl.pallas_call(
        body, name=name, grid=(n // tm,),
        in_specs=[row_blk(d), whole(wu), whole(wz), whole(wdt), const, const],
        out_specs=[row_blk(widths[0]), row_blk(widths[1])] + [row_blk(D_DT)] * 3
        + [pl.BlockSpec((per_tile, N_GROUPS, 16, LANES), lambda i: (i, 0, 0, 0))],
        out_shape=[jax.ShapeDtypeStruct((n, widths[0]), F32), jax.ShapeDtypeStruct((n, widths[1]), F32)]
        + [jax.ShapeDtypeStruct((n, D_DT), F32)] * 3 + [jax.ShapeDtypeStruct((n // CHUNK, N_GROUPS, 16, LANES), F32)],
        compiler_params=_params("parallel"),
    )(hn, wu, wz, wdt, dtb, alog)


def _ssd_decay(dt, acs, tr):
    lane = lax.broadcasted_iota(jnp.int32, (CHUNK, LANES), 1)
    row = lax.broadcasted_iota(jnp.int32, (CHUNK, LANES), 0)
    return dict(lane=lane, row=row, dt=dt, causal=row >= lane, acs=acs, acs_t=tr[8:16], dt_t=tr[0:8],
                aend=acs[CHUNK - 1:CHUNK, :])


def _ssd_specs(bsz, nc, rev):
    ch = (lambda j: nc - 1 - j) if rev else (lambda j: j)
    return dict(
        xs=pl.BlockSpec((bsz, CHUNK, GW), lambda g, j: (0, ch(j), g)),
        bm=pl.BlockSpec((bsz, CHUNK, D_STATE), lambda g, j: (0, ch(j), D_SSM // D_STATE + g)),
        cm=pl.BlockSpec((bsz, CHUNK, D_STATE), lambda g, j: (0, ch(j), D_SSM // D_STATE + N_GROUPS + g)),
        lane_blk=pl.BlockSpec((bsz, CHUNK, LANES), lambda g, j: (0, ch(j), g)),
        grp_const=pl.BlockSpec((1, 1, LANES), lambda g, j: (g, 0, 0)),
        grp_vec=pl.BlockSpec((1, GW), lambda g, j: (0, g)),
        state=pl.BlockSpec((bsz, 1, D_STATE, GW), lambda g, j: (0, ch(j), 0, g)),
        tr=pl.BlockSpec((bsz, 1, 1, 16, LANES), lambda g, j: (0, ch(j), g, 0, 0)),
    )


def _ssd_fwd(xc, dt, acs, tr, z, dskip, normw, *, name, rider=None):
    bsz, t, _ = xc.shape
    nc = t // CHUNK
    sp = _ssd_specs(bsz, nc, False)

    def body(xs_ref, b_ref, c_ref, dt_ref, acs_ref, tr_ref, z_ref, dsk_ref, nw_ref, yn_ref, y_ref, sp_ref, s_ref):
        j = pl.program_id(1)

        @pl.when(j == 0)
        def _():
            s_ref[...] = jnp.zeros_like(s_ref)

        ex = range(bsz)
        units = [(e, r) for e in ex for r in range(HPG)]
        full = lambda v: jnp.broadcast_to(v, (CHUNK, LANES))
        pair = lambda r: pl.ds((r // 2) * LANES, LANES)
        q = [_ssd_decay(dt_ref[e], acs_ref[e], tr_ref[e, 0, 0]) for e in ex]
        for e in ex:
            sp_ref[e, 0] = s_ref[e]
        bm, cm = [b_ref[e] for e in ex], [c_ref[e] for e in ex]
        cb = [_dot_nt(cm[e], bm[e]) for e in ex]
        low = q[0]["lane"] < HEAD_DIM
        col = {(e, r): full(q[e]["acs"][:, r:r + 1]) for e, r in units}
        aend = {(e, r): q[e]["aend"][:, r:r + 1] for e, r in units}
        decay = {(e, r): jnp.exp(jnp.where(q[e]["causal"], col[e, r] - q[e]["acs_t"][r:r + 1, :], -jnp.inf))
                 for e, r in units}
        mp = {(e, r): cb[e] * decay[e, r] * q[e]["dt_t"][r:r + 1, :] for e, r in units}
        ce = {(e, r): cm[e] * jnp.exp(col[e, r]) for e, r in units}
        bk = {(e, r): bm[e] * (jnp.exp(aend[e, r] - col[e, r]) * full(q[e]["dt"][:, r:r + 1])) for e, r in units}
        xp = {(e, r): xs_ref[e, :, pair(r)] for e, r in units}
        s_old = {(e, r): s_ref[e, :, pair(r)] for e, r in units}
        y_h = {u: _dot(mp[u], xp[u]) + _dot(ce[u], s_old[u]) for u in units}
        s_h = {u: jnp.exp(aend[u]) * s_old[u] + _dot_tn(bk[u], xp[u]) for u in units}
        for e in ex:
            for r in range(0, HPG, 2):
                y_ref[e, :, pair(r)] = jnp.where(low, y_h[e, r], y_h[e, r + 1])
                s_ref[e, :, pair(r)] = jnp.where(low, s_h[e, r], s_h[e, r + 1])
        y = [y_ref[e] + dsk_ref[...] * xs_ref[e] for e in ex]
        zz = [z_ref[e] for e in ex]
        yg = [y[e] * (zz[e] * _sigmoid(zz[e])) for e in ex]
        rstd = [lax.rsqrt(jnp.mean(yg[e] * yg[e], axis=-1, keepdims=True) + EPS) for e in ex]
        for e in ex:
            y_ref[e] = y[e]
            yn_ref[e] = (yg[e] * rstd[e] * nw_ref[...]).astype(BF16)

    grid = (N_GROUPS, nc)
    ride = _Ride(rider, body, 9, 3, 1, grid)
    outs = pl.pallas_call(
        ride.body, name=name, grid=grid,
        in_specs=[sp["xs"], sp["bm"], sp["cm"], sp["lane_blk"], sp["lane_blk"], sp["tr"], sp["xs"],
                  sp["grp_vec"], sp["grp_vec"]] + ride.in_specs,
        out_specs=[sp["xs"], sp["xs"], sp["state"]] + ride.out_specs,
        out_shape=[jax.ShapeDtypeStruct((bsz, t, D_SSM), BF16), jax.ShapeDtypeStruct((bsz, t, D_SSM), F32),
                   jax.ShapeDtypeStruct((bsz, nc, D_STATE, D_SSM), F32)] + ride.out_shape,
        scratch_shapes=[pltpu.VMEM((bsz, D_STATE, GW), F32)] + ride.scratch,
        compiler_params=_params(*ride.semantics(("parallel", "arbitrary"))),
    )(xc, xc, xc, dt, acs, tr, z, dskip, normw, *ride.args)
    return outs[:3], outs[3:]


def _ssd_bwd(xc, dtr, dt, acs, tr, z, ypre, sprev, dyn, dtb, alog, dskip, normw, *, name, rider=None):
    bsz, t, _ = xc.shape
    nc = t // CHUNK
    sp = _ssd_specs(bsz, nc, True)

    def body(xs_ref, b_ref, c_ref, dtr_ref, dt_ref, acs_ref, tr_ref, z_ref, y_ref, sp_ref, dyn_ref, dtb_ref, alog_ref,
             dsk_ref, nw_ref, dz_ref, dxs_ref, db_ref, dc_ref, ddt_ref, dnw_ref, dsm_ref, ds_ref):
        j = pl.program_id(1)

        @pl.when(j == 0)
        def _():
            ds_ref[...] = jnp.zeros_like(ds_ref)
            dnw_ref[...] = jnp.zeros_like(dnw_ref)
            dsm_ref[...] = jnp.zeros_like(dsm_ref)

        ex = range(bsz)
        heads = range(HPG)
        units = [(e, r) for e in ex for r in heads]
        q = [_ssd_decay(dt_ref[e], acs_ref[e], tr_ref[e, 0, 0]) for e in ex]
        a = -jnp.exp(alog_ref[0])
        valid = _dt_valid(nc - 1 - j)
        lane, row = q[0]["lane"], q[0]["row"]
        lane1 = lane[0:1, :]
        nw = nw_ref[...]
        y, zz, dyn = [y_ref[e] for e in ex], [z_ref[e] for e in ex], [dyn_ref[e] for e in ex]
        sz = [_sigmoid(zz[e]) for e in ex]
        sil = [zz[e] * sz[e] for e in ex]
        yg = [y[e] * sil[e] for e in ex]
        rstd = [lax.rsqrt(jnp.mean(yg[e] * yg[e], axis=-1, keepdims=True) + EPS) for e in ex]
        gn = [dyn[e] * nw for e in ex]
        dyg = [rstd[e] * (gn[e] - yg[e] * (rstd[e] * rstd[e]) * jnp.mean(gn[e] * yg[e], axis=-1, keepdims=True))
               for e in ex]
        dy = [dyg[e] * sil[e] for e in ex]
        xs = [xs_ref[e] for e in ex]
        for e in ex:
            dnw_ref[e] += jnp.sum(dyn[e] * yg[e] * rstd[e], axis=0, keepdims=True)
            dz_ref[e] = (dyg[e] * y[e] * (sz[e] * (1.0 + zz[e] * (1.0 - sz[e])))).astype(BF16)
        dskip_cols = [jnp.sum(dy[e] * xs[e], axis=0, keepdims=True) for e in ex]

        bm, cm = [b_ref[e] for e in ex], [c_ref[e] for e in ex]
        cb = [_dot_nt(cm[e], bm[e]) for e in ex]
        zero = jnp.zeros((CHUNK, LANES), F32)
        full = lambda v: jnp.broadcast_to(v, (CHUNK, LANES))
        low = lane < HEAD_DIM
        half = [low if r % 2 == 0 else ~low for r in heads]
        sl = lambda v, r: v[:, (r // 2) * LANES:(r // 2 + 1) * LANES]
        pair = lambda r: pl.ds((r // 2) * LANES, LANES)
        col = {(e, r): full(q[e]["acs"][:, r:r + 1]) for e, r in units}
        dt_col = {(e, r): full(q[e]["dt"][:, r:r + 1]) for e, r in units}
        aend = {(e, r): q[e]["aend"][:, r:r + 1] for e, r in units}
        dt_row = {(e, r): q[e]["dt_t"][r:r + 1, :] for e, r in units}
        decay = {(e, r): jnp.exp(jnp.where(q[e]["causal"], col[e, r] - q[e]["acs_t"][r:r + 1, :], -jnp.inf))
                 for e, r in units}
        ea = {u: jnp.exp(col[u]) for u in units}
        dte = {u: jnp.exp(aend[u] - col[u]) for u in units}
        ed = {u: jnp.exp(aend[u]) for u in units}
        k = {u: dte[u] * dt_col[u] for u in units}
        mp = {(e, r): cb[e] * decay[e, r] * dt_row[e, r] for e, r in units}
        xp = {(e, r): sl(xs[e], r) for e, r in units}
        dym = {(e, r): jnp.where(half[r], sl(dy[e], r), 0.0) for e, r in units}
        s_old = {(e, r): sp_ref[e, 0, :, pair(r)] for e, r in units}
        ds_old = {(e, r): ds_ref[e, :, pair(r)] for e, r in units}
        dsm = {(e, r): jnp.where(half[r], ds_old[e, r], 0.0) for e, r in units}
        gmat = {u: _dot_nt(dym[u], xp[u]) for u in units}
        t1 = {u: _dot_nt(dym[u], s_old[u]) for u in units}
        dbs = {u: _dot_nt(xp[u], dsm[u]) for u in units}
        dx = {(e, r): _dot_tn(mp[e, r], dym[e, r]) + _dot(bm[e] * k[e, r], dsm[e, r]) for e, r in units}
        ds = {(e, r): _dot_tn(cm[e] * ea[e, r], dym[e, r]) for e, r in units}
        gd = {u: gmat[u] * decay[u] for u in units}
        w0 = {(e, r): gd[e, r] * cb[e] for e, r in units}
        cs0 = {u: jnp.sum(w0[u], axis=0, keepdims=True) for u in units}
        rs = {u: jnp.sum(w0[u] * dt_row[u], axis=1, keepdims=True) for u in units}
        qv = {(e, r): jnp.sum(cm[e] * t1[e, r], axis=1, keepdims=True) for e, r in units}
        dk = {(e, r): jnp.sum(bm[e] * dbs[e, r], axis=1, keepdims=True) for e, r in units}
        ddte = {u: dk[u] * dt_col[u] for u in units}
        d_aend = {u: _sum_all(dsm[u] * s_old[u]) * ed[u] + _sum_all(ddte[u][:, 0:1] * dte[u][:, 0:1]) for u in units}
        last_row = row == CHUNK - 1
        dacs_col = {u: rs[u] + qv[u] * ea[u] - ddte[u] * dte[u] + jnp.where(last_row, d_aend[u], 0.0) for u in units}
        triu = (lane >= row).astype(F32)
        for e in ex:
            dcb, dc_acc, db_acc = zero, zero, zero
            dacs, dacs_t, ddt, ddt_t = zero, zero, zero, zero
            dskip_row = jnp.zeros((1, LANES), F32)
            for r in heads:
                u = (e, r)
                dcb = dcb + gd[u] * dt_row[u]
                dc_acc = dc_acc + ea[u] * t1[u]
                db_acc = db_acc + k[u] * dbs[u]
                dacs = jnp.where(lane == r, dacs_col[u], dacs)
                ddt = jnp.where(lane == r, dk[u] * dte[u], ddt)
                dacs_t = jnp.where(row == r, -cs0[u] * dt_row[u], dacs_t)
                ddt_t = jnp.where(row == r, cs0[u], ddt_t)
                dsk = _sum_all(jnp.where(half[r][0:1, :], sl(dskip_cols[e], r), 0.0))
                dskip_row = dskip_row + jnp.where(lane1 == r, dsk, 0.0)
            for r in range(0, HPG, 2):
                dxs_ref[e, :, pair(r)] = (dx[e, r] + dx[e, r + 1] + sl(dy[e], r) * dsk_ref[:, pair(r)]).astype(BF16)
                ed_pair = jnp.where(lane1 < HEAD_DIM, ed[e, r], ed[e, r + 1])
                ds_ref[e, :, pair(r)] = ds[e, r] + ds[e, r + 1] + ed_pair * ds_old[e, r]
            dacs = dacs + dacs_t.T
            ddt = ddt + ddt_t.T
            dda = _dot_exact(triu, dacs)
            ddt = ddt + dda * a
            da = jnp.sum(dda * q[e]["dt"], axis=0, keepdims=True)
            draw = jnp.where(valid, ddt * _sigmoid(dtr_ref[e] + dtb_ref[0]), 0.0)
            ddt_ref[e] = draw.astype(BF16)
            dsm_ref[e, 0, 0:1, :] += dskip_row
            dsm_ref[e, 0, 1:2, :] += da * a
            dsm_ref[e, 0, 2:3, :] += jnp.sum(draw, axis=0, keepdims=True)
            dc_ref[e] = (dc_acc + _dot(dcb, bm[e])).astype(BF16)
            db_ref[e] = (db_acc + _dot_tn(dcb, cm[e])).astype(BF16)

    grp_out = pl.BlockSpec((bsz, CHUNK, D_STATE), lambda g, j: (0, nc - 1 - j, g))
    grid = (N_GROUPS, nc)
    ride = _Ride(rider, body, 15, 7, 1, grid)
    outs = pl.pallas_call(
        ride.body, name=name, grid=grid,
        in_specs=[sp["xs"], sp["bm"], sp["cm"], sp["lane_blk"], sp["lane_blk"], sp["lane_blk"], sp["tr"], sp["xs"],
                  sp["xs"], sp["state"], sp["xs"], sp["grp_const"], sp["grp_const"], sp["grp_vec"], sp["grp_vec"]]
        + ride.in_specs,
        out_specs=[sp["xs"], sp["xs"], grp_out, grp_out, sp["lane_blk"],
                   pl.BlockSpec((bsz, 1, GW), lambda g, j: (0, 0, g)),
                   pl.BlockSpec((bsz, 1, 8, LANES), lambda g, j: (0, g, 0, 0))] + ride.out_specs,
        out_shape=[jax.ShapeDtypeStruct((bsz, t, D_SSM), BF16), jax.ShapeDtypeStruct((bsz, t, D_SSM), BF16),
                   jax.ShapeDtypeStruct((bsz, t, N_GROUPS * D_STATE), BF16),
                   jax.ShapeDtypeStruct((bsz, t, N_GROUPS * D_STATE), BF16),
                   jax.ShapeDtypeStruct((bsz, t, D_DT), BF16), jax.ShapeDtypeStruct((bsz, 1, D_SSM), F32),
                   jax.ShapeDtypeStruct((bsz, N_GROUPS, 8, LANES), F32)] + ride.out_shape,
        scratch_shapes=[pltpu.VMEM((bsz, D_STATE, GW), F32)] + ride.scratch,
        compiler_params=_params(*ride.semantics(("parallel", "arbitrary"))),
    )(xc, xc, xc, dtr, dt, acs, tr, z, ypre, sprev, dyn, dtb, alog, dskip, normw, *ride.args)
    return outs[:7], outs[7:]


def _input_grad(dhn, h0, w, dres, seq, *, name):
    bsz, t, d = h0.shape
    nc = t // CHUNK

    def body(dy_ref, h_ref, w_ref, dres_ref, gx_ref, head_ref, dw_ref):
        j = pl.program_id(0)

        @pl.when(j == 0)
        def _():
            dw_ref[...] = jnp.zeros_like(dw_ref)

        for e in range(bsz):
            x, dyv = h_ref[e], dy_ref[e]
            r = lax.rsqrt(jnp.mean(x * x, axis=-1, keepdims=True) + EPS)
            g = dyv * w_ref[...]
            dx = r * (g - x * (r * r) * jnp.mean(g * x, axis=-1, keepdims=True)) + dres_ref[e]
            dw_ref[...] += jnp.sum(dyv * x * r, axis=0, keepdims=True)
            gx_ref[e] = dx

        @pl.when(j == 0)
        def _():
            head_ref[...] = gx_ref[...]

    row = pl.BlockSpec((bsz, CHUNK, d), lambda j: (0, j, 0))
    return pl.pallas_call(
        body, name=name, grid=(nc,),
        in_specs=[row, row, pl.BlockSpec((1, d), lambda j: (0, 0)), row],
        out_specs=[pl.BlockSpec((bsz, CHUNK, d), lambda j: (0, jnp.maximum(j - 1, 0), 0)),
                   pl.BlockSpec((bsz, CHUNK, d), lambda j: (0, 0, 0)), pl.BlockSpec((1, d), lambda j: (0, 0))],
        out_shape=[jax.ShapeDtypeStruct((bsz, seq, d), F32), jax.ShapeDtypeStruct((bsz, CHUNK, d), F32),
                   jax.ShapeDtypeStruct((1, d), F32)],
        compiler_params=_params("arbitrary"),
    )(dhn, h0, w, dres)


def _remote(src, dst, send_sem, recv_sem, dev):
    return pltpu.make_async_remote_copy(src_ref=src, dst_ref=dst, send_sem=send_sem, recv_sem=recv_sem,
                                        device_id=dev, device_id_type=MESH)


def _position():
    return lax.axis_index("x"), lax.axis_index("y"), lax.axis_index("c")


def _other_chips(pos):
    x, y, _ = pos
    return [(1 - x, y), (x, 1 - y), (1 - x, 1 - y)]


class _Gather:
    def __init__(self, arrs, relay_at=None):
        n = len(arrs)
        self.args, self.n_in, self.n_out = list(arrs), n, n
        if relay_at is not None:
            self.relay_at = relay_at
        self.split = [a.ndim == 2 and a.shape[1] % (2 * LANES) == 0 for a in arrs]
        self.out_shape = [jax.ShapeDtypeStruct((4,) + a.shape, a.dtype) for a in arrs]
        self.scratch = [pltpu.SemaphoreType.DMA((3 * n,)), pltpu.SemaphoreType.DMA((3 * n,)),
                        pltpu.SemaphoreType.DMA((n,)), pltpu.SemaphoreType.DMA((3 * n,)),
                        pltpu.SemaphoreType.DMA((3 * n,))]

    def _copies(self, pos, ins, outs, sems):
        send_sems, recv_sems, loc_sems, pass_send_sems, pass_recv_sems = sems
        x, y, c = pos
        me, sibling = 2 * x + y, (x, y, 1 - c)
        local = [pltpu.make_async_copy(ins[i], outs[i].at[me], loc_sems.at[i]) for i in range(self.n_in)]
        sends, recvs, passes, pass_recvs = [], [], [], []
        for i in range(self.n_in):
            half = self.args[i].shape[1] // 2 if self.split[i] else None
            for k, (px, py) in enumerate(_other_chips(pos)):
                them = 2 * px + py
                sems_k = (send_sems.at[3 * i + k], recv_sems.at[3 * i + k], (px, py, c))
                if half is None:
                    sends.append(_remote(ins[i], outs[i].at[me], *sems_k))
                    recvs.append(_remote(ins[i], outs[i].at[them], *sems_k))
                    passes.append(None)
                    continue
                mine = pl.ds(pl.multiple_of(c * half, LANES), half)
                other = pl.ds(pl.multiple_of((1 - c) * half, LANES), half)
                sends.append(_remote(ins[i].at[:, mine], outs[i].at[me, :, mine], *sems_k))
                recvs.append(_remote(ins[i].at[:, mine], outs[i].at[them, :, mine], *sems_k))
                pass_k = (pass_send_sems.at[3 * i + k], pass_recv_sems.at[3 * i + k], sibling)
                passes.append(_remote(outs[i].at[them, :, mine], outs[i].at[them, :, mine], *pass_k))
                pass_recvs.append(_remote(outs[i].at[them, :, other], outs[i].at[them, :, other], *pass_k))
        return local, sends, recvs, passes, pass_recvs

    def start(self, pos, ins, outs, sems):
        local, sends = self._copies(pos, ins, outs, sems)[:2]
        for cp in local + sends:
            cp.start()

    def relay(self, pos, ins, outs, sems):
        _, _, recvs, passes, _ = self._copies(pos, ins, outs, sems)
        for cp, onward in zip(recvs, passes):
            if onward is not None:
                cp.wait_recv()
                onward.start()

    def finish(self, pos, ins, outs, sems):
        local, sends, recvs, passes, pass_recvs = self._copies(pos, ins, outs, sems)
        for cp, onward in zip(recvs, passes):
            if onward is None:
                cp.wait_recv()
        for cp in pass_recvs:
            cp.wait_recv()
        for cp in sends + [p for p in passes if p is not None]:
            cp.wait_send()
        for cp in local:
            cp.wait()


class _Exchange:
    FLIPS = [(fx, fy, fc) for fx in (0, 1) for fy in (0, 1) for fc in (0, 1)][1:]

    def __init__(self, big, small=None):
        n = len(big)
        self.n_big, self.has_small = n, small is not None
        self.args = list(big) + ([small] if self.has_small else [])
        self.n_in = self.n_out = len(self.args)
        self.out_shape = [jax.ShapeDtypeStruct(a.shape, a.dtype) for a in big]
        self.scratch = [pltpu.SemaphoreType.DMA((max(3 * n, 1),)), pltpu.SemaphoreType.DMA((max(3 * n, 1),)),
                        pltpu.SemaphoreType.DMA((n + 1,))]
        if self.has_small:
            self.out_shape.append(jax.ShapeDtypeStruct((8,) + small.shape, small.dtype))
            self.scratch += [pltpu.SemaphoreType.DMA((7,)), pltpu.SemaphoreType.DMA((7,))]

    def _copies(self, pos, ins, outs, sems):
        x, y, c = pos
        me, me8 = 2 * x + y, 4 * x + 2 * y + c
        local, sends, recvs = [], [], []
        for i in range(self.n_big):
            local.append(pltpu.make_async_copy(ins[i].at[me], outs[i].at[me], sems[2].at[i]))
            for k, (px, py) in enumerate(_other_chips(pos)):
                sems_k = (sems[0].at[3 * i + k], sems[1].at[3 * i + k], (px, py, c))
                sends.append(_remote(ins[i].at[2 * px + py], outs[i].at[me], *sems_k))
                recvs.append(_remote(ins[i].at[me], outs[i].at[2 * px + py], *sems_k))
        if self.has_small:
            small, landed = ins[self.n_big], outs[self.n_big]
            local.append(pltpu.make_async_copy(small, landed.at[me8], sems[2].at[self.n_big]))
            for k, (fx, fy, fc) in enumerate(self.FLIPS):
                peer = (x ^ fx, y ^ fy, c ^ fc)
                sems_k = (sems[3].at[k], sems[4].at[k], peer)
                sends.append(_remote(small, landed.at[me8], *sems_k))
                recvs.append(_remote(small, landed.at[4 * peer[0] + 2 * peer[1] + peer[2]], *sems_k))
        return local, sends, recvs, [None] * len(recvs), []

    start = _Gather.start
    relay = _Gather.relay
    finish = _Gather.finish


class _Swap:
    def __init__(self, arrs):
        n = len(arrs)
        self.args, self.n_in, self.n_out = list(arrs), n, n
        self.out_shape = [jax.ShapeDtypeStruct(a.shape, a.dtype) for a in arrs]
        self.scratch = [pltpu.SemaphoreType.DMA((n,)), pltpu.SemaphoreType.DMA((n,))]

    def _copies(self, pos, ins, outs, sems):
        x, y, c = pos
        both = [_remote(ins[i], outs[i], sems[0].at[i], sems[1].at[i], (x, y, 1 - c)) for i in range(self.n_in)]
        return [], both, both, [None] * len(both), []

    start = _Gather.start
    relay = _Gather.relay
    finish = _Gather.finish


def _comm(rider, *, name):
    a, b = rider.n_in, rider.n_in + rider.n_out

    def body(*refs):
        pos = _position()
        rider.start(pos, refs[:a], refs[a:b], refs[b:])
        rider.relay(pos, refs[:a], refs[a:b], refs[b:])
        rider.finish(pos, refs[:a], refs[a:b], refs[b:])

    return pl.pallas_call(body, name=name, in_specs=[ANY] * rider.n_in, out_specs=[ANY] * rider.n_out,
                          out_shape=rider.out_shape, scratch_shapes=rider.scratch)(*rider.args)


class _Ride:
    RELAY_AT = 0.8

    def __init__(self, rider, body, n_in, n_out, n_scratch, grid):
        self.rider = rider
        self.args = rider.args if rider else []
        self.in_specs = [ANY] * rider.n_in if rider else []
        self.out_specs = [ANY] * rider.n_out if rider else []
        self.out_shape = rider.out_shape if rider else []
        self.scratch = rider.scratch if rider else []
        self.body = self._wrap(body, n_in, n_out, n_scratch, grid) if rider else body

    def semantics(self, sem):
        return ("arbitrary",) * len(sem) if self.rider else sem

    def _wrap(self, body, n_in, n_out, n_scratch, grid):
        rider = self.rider
        a = n_in
        b = a + rider.n_in
        c = b + n_out
        d = c + rider.n_out
        e = d + n_scratch

        def wrapped(*refs):
            pos = _position()
            ids = [pl.program_id(i) for i in range(len(grid))]
            step, total = 0, 1
            for i, g in zip(ids, grid):
                step, total = step * g + i, total * g

            @pl.when(step == 0)
            def _():
                rider.start(pos, refs[a:b], refs[c:d], refs[e:])

            body(*refs[:a], *refs[b:c], *refs[d:e])

            @pl.when(step == int(getattr(rider, "relay_at", self.RELAY_AT) * (total - 1)))
            def _():
                rider.relay(pos, refs[a:b], refs[c:d], refs[e:])

            @pl.when(step == total - 1)
            def _():
                rider.finish(pos, refs[a:b], refs[c:d], refs[e:])

        return wrapped


def _elementwise_tiles(r, c):
    if r % 8 == 0 and r * c > 65536:
        tm = _pick(r, (256, 128, 64, 16, 8))
        return (tm, c), r // tm, lambda i: (i, 0)
    if r % 8 and c % 256 == 0 and r * c > 65536:
        return (r, 256), c // 256, lambda i: (0, i)
    return (r, c), 1, lambda i: (0, 0)


def _chip_sum(landed, *, name):
    _, r, c = landed.shape
    blk, steps, at = _elementwise_tiles(r, c)

    def body(land_ref, o_ref):
        acc = land_ref[0].astype(F32)
        for jchip in range(1, 4):
            acc = acc + land_ref[jchip].astype(F32)
        o_ref[...] = acc

    return pl.pallas_call(
        body, name=name, grid=(steps,), in_specs=[pl.BlockSpec((4,) + blk, lambda i: (0,) + at(i))],
        out_specs=pl.BlockSpec(blk, at), out_shape=jax.ShapeDtypeStruct((r, c), F32),
        compiler_params=_params("parallel"),
    )(landed)


def _device_sum(parts, *, name):
    _, r, c = parts.shape

    def body(p_ref, o_ref):
        acc = p_ref[0]
        for d in range(1, 8):
            acc = acc + p_ref[d]
        o_ref[...] = acc

    return pl.pallas_call(body, name=name, out_shape=jax.ShapeDtypeStruct((r, c), F32))(parts)


def _adamw_math(w, g, m, v):
    m = ADAM_B1 * m + (1.0 - ADAM_B1) * g
    v = ADAM_B2 * v + (1.0 - ADAM_B2) * (g * g)
    m_hat = m / (1.0 - ADAM_B1 ** ADAM_STEP)
    v_hat = v / (1.0 - ADAM_B2 ** ADAM_STEP)
    return -ADAM_LR * (m_hat / (jnp.sqrt(v_hat) + ADAM_EPS) + ADAM_WD * w), m, v


def _adamw(w, g_parts, m, v, *, name):
    r, c = w.shape
    shape, steps, at = _elementwise_tiles(r, c)
    n_g = len(g_parts)

    def body(*refs):
        w_ref, m_ref, v_ref = refs[n_g:n_g + 3]
        g_ref, d_ref, nm_ref, nv_ref = refs[n_g + 3:]
        g = refs[0][...]
        for p in refs[1:n_g]:
            g = g + p[...]
        g_ref[...] = g
        d_ref[...], nm_ref[...], nv_ref[...] = _adamw_math(w_ref[...], g, m_ref[...], v_ref[...])

    blk = pl.BlockSpec(shape, at)
    return pl.pallas_call(
        body, name=name, grid=(steps,), in_specs=[blk] * (n_g + 3), out_specs=[blk] * 4,
        out_shape=[jax.ShapeDtypeStruct((r, c), F32)] * 4, compiler_params=_params("parallel"),
    )(*g_parts, w, m, v)


def _pad_heads(v):
    return jnp.pad(v.reshape(N_GROUPS, 1, HPG), ((0, 0), (0, 0), (0, LANES - HPG)))


def _unpad_heads(v):
    return v[:, :HPG].reshape(1, N_HEADS)


_SMALL_EARLY = [("pool_w", (512, 128)), ("pool_scale", (1, 512)), ("conv_w", (4, D_XBC)), ("conv_b", (1, D_XBC)),
                ("dt_bias", (1, N_HEADS)), ("a_log", (1, N_HEADS)), ("d_skip", (1, N_HEADS)), ("ssm_norm_w", (1, D_SSM)),
                ("norm_ffn_w", (1, 1024)), ("norm_f_w", (1, 1024))]
_SMALL_LATE = [("norm_mix_w", (1, 1024)), ("meta", (N_META, 1024)), ("loss", (1, 1))]


def _pack_small(grads, layout):
    rows = []
    for nm, shape in layout:
        flat = grads[nm].reshape(-1)
        rows.append(jnp.pad(flat, (0, (-flat.size) % LANES)).reshape(-1, LANES))
    packed = jnp.concatenate(rows, axis=0)
    return jnp.pad(packed, ((0, (-packed.shape[0]) % 8), (0, 0)))


def _unpack_small(packed, layout):
    out, r0 = {}, 0
    for nm, shape in layout:
        size = shape[0] * shape[1]
        nrow = -(-size // LANES)
        out[nm] = packed[r0:r0 + nrow].reshape(-1)[:size].reshape(shape)
        r0 += nrow
    return out


def kernel(x, meta, norm_mix_w, w_in, pool_w, pool_scale, conv_w, conv_b, dt_bias, a_log, d_skip, ssm_norm_w, w_out, norm_ffn_w, w_ff1, w_ff2, norm_f_w, loss_target, m_meta, m_norm_mix_w, m_w_in, m_pool_w, m_pool_scale, m_conv_w, m_conv_b, m_dt_bias, m_a_log, m_d_skip, m_ssm_norm_w, m_w_out, m_norm_ffn_w, m_w_ff1, m_w_ff2, m_norm_f_w, v_meta, v_norm_mix_w, v_w_in, v_pool_w, v_pool_scale, v_conv_w, v_conv_b, v_dt_bias, v_a_log, v_d_skip, v_ssm_norm_w, v_w_out, v_norm_ffn_w, v_w_ff1, v_w_ff2, v_norm_f_w):
    bsz, seq, d = x.shape
    t = seq + CHUNK
    n = bsz * t
    chip = 2 * lax.axis_index("x") + lax.axis_index("y")
    d_in = w_in.shape[2] * 4

    g_conv, g_meta = _comm(_Gather([conv_w[0], meta]), name="gather_small")
    convw = g_conv.transpose(1, 0, 2).reshape(CONV_W, D_XBC)
    meta_full = g_meta.transpose(1, 0, 2).reshape(N_META, d)
    (h0, hn1), (g_in,) = _embed_norm(x, meta_full, norm_mix_w, name="embed_norm",
                                     rider=_Gather([w_in[0].T.astype(BF16)], relay_at=1.0))
    h0f, hn1 = h0.reshape(n, d), hn1.reshape(n, d)
    late_weights = _Gather([w_out[0].astype(BF16), w_ff1[0].astype(BF16), w_ff2[0].astype(BF16)])
    win = g_in.reshape(d_in, d)
    wu, wz = win[:D_POOL], win[D_POOL:D_POOL + D_SSM]
    wx = win[D_POOL + D_SSM:D_POOL + D_SSM + D_XBC]
    wdt = jnp.pad(win[D_POOL + D_SSM + D_XBC:].reshape(N_GROUPS, HPG, d),
                  ((0, 0), (0, LANES - HPG), (0, 0))).reshape(D_DT, d)
    dtb, alog = _pad_heads(dt_bias), _pad_heads(a_log)
    dskip = jnp.repeat(d_skip, HEAD_DIM, axis=1)
    poolw = pool_w[0]

    u, z, dtr, dt_, acs_, tr_ = _proj_uz_dt(hn1, wu, wz, wdt, dtb, alog, t // CHUNK, name="proj_uzdt")
    xbc, xc = _proj_conv(hn1, wx, convw, conv_b, name="proj_xbc")
    ypool = _pool_fwd(u.reshape(bsz, t, D_POOL), poolw, pool_scale, name="pool_fwd")
    xbc3 = xbc.reshape(bsz, t, D_XBC)
    xc = xc.reshape(bsz, t, D_XBC)
    z3, dtr3 = z.reshape(bsz, t, D_SSM), dtr.reshape(bsz, t, D_DT)
    dt3, acs3 = dt_.reshape(bsz, t, D_DT), acs_.reshape(bsz, t, D_DT)
    tr3 = tr_.reshape(bsz, t // CHUNK, N_GROUPS, 16, LANES)
    (yn, ypre, sprev), (g_out, g_ff1, g_ff2) = _ssd_fwd(xc, dt3, acs3, tr3, z3, dskip, ssm_norm_w, name="ssd_fwd",
                                                        rider=late_weights)
    wo = g_out.reshape(D_POOL + D_SSM, d)
    wo_p, wo_s = wo[:D_POOL], wo[D_POOL:]
    w1 = g_ff1
    w2 = g_ff2.reshape(D_FF, d)
    ypool_f, yn_f = ypool.reshape(n, D_POOL), yn.reshape(n, D_SSM)
    add = lambda r, e: r + e
    h1, hn2 = _mm([ypool_f, yn_f], [wo_p, wo_s], name="out_proj", post=add, extras=(h0f,), norm_w=norm_ffn_w)
    act = _mm(hn2, w1, name="ff1", out_dtype=BF16)
    relu2 = lambda a: jnp.square(jnp.maximum(a, 0))
    h2 = _mm(act, w2, name="ff2", pre=relu2, post=add, extras=(h1,))
    dh2, dh2b, loss_acc, d_norm_f = _final_norm_loss(h2.reshape(bsz, t, d), loss_target, norm_f_w.reshape(1, d),
                                                     name="loss")

    dh2f, dh2bf = dh2.reshape(n, d), dh2b.reshape(n, d)
    dact = _mm(dh2bf, w2, name="ff2_bwd", nt=True, post=lambda r, a: r * (2.0 * jnp.maximum(a, 0).astype(F32)),
               extras=(act,), out_dtype=BF16)
    d_w2 = _mm_tn(act, dh2bf, name="ff2_dw", tk=2048, tn=1024, pre=relu2)
    d_w1 = _mm_tn(hn2, dact, name="ff1_dw", tk=1024, tn=2048, slab=D_FF // 4)
    dh1, dh1b, d_norm_ffn = _mm_rms_bwd(dact, w1, h1, norm_ffn_w, dh2f, name="ff1_bwd")
    dypool, dyn = _mm_fanout(dh1b, [wo_p, wo_s], name="out_proj_bwd")
    d_wo = _mm_tn_cat([ypool_f, yn_f], dh1b, name="out_proj_dw")
    big_late = [d_wo.reshape(4, (D_POOL + D_SSM) // 4, d),
                d_w1, d_w2.reshape(4, D_FF // 4, d)]
    (dz, dxs, dbm, dcm, ddtr, d_nw, d_heads), landed_late = _ssd_bwd(
        xc, dtr3, dt3, acs3, tr3, z3, ypre, sprev, dyn.reshape(bsz, t, D_SSM), dtb, alog, dskip, ssm_norm_w, name="ssd_bwd",
        rider=_Exchange(big_late))
    mine_late = [_chip_sum(l, name=f"chip_sum_{i + 1}") for i, l in enumerate(landed_late)]
    (dxbc, d_convwb), theirs_late = _conv_bwd(xbc3, dxs, dbm, dcm, convw, conv_b, name="conv_bwd",
                                               rider=_Swap(mine_late))
    du, d_poolw, d_poolsc = _pool_bwd(u.reshape(bsz, t, D_POOL), dypool.reshape(bsz, t, D_POOL), poolw, pool_scale,
                                      name="pool_bwd")
    duf, dzf, dxbcf, ddtrf = du.reshape(n, D_POOL), dz.reshape(n, D_SSM), dxbc.reshape(n, D_XBC), ddtr.reshape(n, D_DT)
    heads = jnp.sum(d_heads, axis=0)
    small_early = _pack_small({
        "pool_w": d_poolw, "pool_scale": d_poolsc,
        "conv_w": jnp.sum(d_convwb[:, :CONV_W], axis=0), "conv_b": jnp.sum(d_convwb[:, CONV_W:CONV_W + 1], axis=0),
        "dt_bias": _unpad_heads(heads[:, 2]), "a_log": _unpad_heads(heads[:, 1]), "d_skip": _unpad_heads(heads[:, 0]),
        "ssm_norm_w": jnp.sum(d_nw, axis=0), "norm_ffn_w": d_norm_ffn, "norm_f_w": d_norm_f}, _SMALL_EARLY)
    d_wuzdt = _mm_tn_cat([duf, dzf, ddtrf], hn1, name="proj_uzdt_dw")
    d_wx, (early_all,) = _mm_tn(dxbcf, hn1, name="proj_xbc_dw", tk=1280, tn=1024, rider=_Exchange([], small_early))
    d_wdt = d_wuzdt[D_POOL + D_SSM:].reshape(N_GROUPS, LANES, d)[:, :HPG].reshape(N_HEADS, d)
    d_win = jnp.concatenate([d_wuzdt[:D_POOL + D_SSM], d_wx, d_wdt], axis=0)
    big_in = d_win.reshape(4, d_in // 4, d)
    dhn1, (landed_in,) = _mm([duf, dzf, dxbcf, ddtrf], [wu, wz, wx, wdt], name="proj_bwd",
                             rider=_Exchange([big_in]))
    grad_x, d_head_rows, d_norm_mix = _input_grad(
        dhn1.reshape(bsz, t, d), h0, norm_mix_w, dh1.reshape(bsz, t, d), seq, name="input_grad")

    small_late = _pack_small({"norm_mix_w": d_norm_mix, "meta": jnp.sum(d_head_rows[:, PAD:], axis=0),
                              "loss": loss_acc[0:1, 0:1]}, _SMALL_LATE)
    (late_all,) = _comm(_Exchange([], small_late), name="exchange_small")
    mine_in = _chip_sum(landed_in, name="chip_sum_0")
    (theirs_in,) = _comm(_Swap([mine_in]), name="swap_cores")
    mine, theirs = [mine_in] + mine_late, [theirs_in] + list(theirs_late)
    gsmall = {**_unpack_small(_device_sum(early_all, name="device_sum_early"), _SMALL_EARLY),
              **_unpack_small(_device_sum(late_all, name="device_sum_late"), _SMALL_LATE)}
    gsmall["conv_w"] = lax.dynamic_slice_in_dim(gsmall["conv_w"], chip * (D_XBC // 4), D_XBC // 4, axis=1)
    gsmall["meta"] = lax.dynamic_slice_in_dim(gsmall["meta"], chip * (d // 4), d // 4, axis=1)
    loss = gsmall["loss"][0, 0]

    given = dict(meta=(meta, m_meta, v_meta), norm_mix_w=(norm_mix_w, m_norm_mix_w, v_norm_mix_w),
                 w_in=(w_in, m_w_in, v_w_in), pool_w=(pool_w, m_pool_w, v_pool_w),
                 pool_scale=(pool_scale, m_pool_scale, v_pool_scale), conv_w=(conv_w, m_conv_w, v_conv_w),
                 conv_b=(conv_b, m_conv_b, v_conv_b), dt_bias=(dt_bias, m_dt_bias, v_dt_bias),
                 a_log=(a_log, m_a_log, v_a_log), d_skip=(d_skip, m_d_skip, v_d_skip),
                 ssm_norm_w=(ssm_norm_w, m_ssm_norm_w, v_ssm_norm_w), w_out=(w_out, m_w_out, v_w_out),
                 norm_ffn_w=(norm_ffn_w, m_norm_ffn_w, v_norm_ffn_w), w_ff1=(w_ff1, m_w_ff1, v_w_ff1),
                 w_ff2=(w_ff2, m_w_ff2, v_w_ff2), norm_f_w=(norm_f_w, m_norm_f_w, v_norm_f_w))
    big_names = ["w_in", "w_out", "w_ff1", "w_ff2"]
    results = {}
    for nm, (w, m, v) in given.items():
        if nm in big_names:
            i = big_names.index(nm)
            parts, shape2 = (mine[i], theirs[i]), mine[i].shape
        else:
            parts, shape2 = (gsmall[nm],), gsmall[nm].shape
        if nm == "w_in":
            outs = _adamw(w[0].T, parts, m[0].T, v[0].T, name=f"adamw_{nm}")
            results[nm] = [o.T[None] for o in outs]
        else:
            outs = _adamw(w.reshape(shape2), parts, m.reshape(shape2), v.reshape(shape2), name=f"adamw_{nm}")
            results[nm] = [o.reshape(w.shape) for o in outs]
    order = list(given)
    return (loss, grad_x, *[results[nm][0] for nm in order], *[results[nm][1] for nm in order],
            *[results[nm][2] for nm in order], *[results[nm][3] for nm in order])
```

```python
import jax
import jax.numpy as jnp
from jax import lax
from jax.experimental import pallas as pl
from jax.experimental.pallas import tpu as pltpu

F32 = jnp.float32
BF16 = jnp.bfloat16
MESH = pl.DeviceIdType.MESH
ANY = pl.BlockSpec(memory_space=pl.ANY)

D_MODEL = 1024
N_META = 16
CHUNK = 128
PAD = CHUNK - N_META
POOL_WINDOWS = (2, 4, 8, 16)
D_POOL = 512
POOL_GROUP = 128
D_SSM = 1536
N_HEADS = 24
N_GROUPS = 4
HPG = 6
HEAD_DIM = 64
D_STATE = 128
GW = HPG * HEAD_DIM
D_XBC = D_SSM + 2 * N_GROUPS * D_STATE
D_DT = N_GROUPS * 128
D_FF = 4096
CONV_W = 4
EPS = 1e-5
LANES = 128
VMEM_LIMIT = 56 * 1024 * 1024

ADAM_LR, ADAM_B1, ADAM_B2, ADAM_EPS, ADAM_WD, ADAM_STEP = 0.001, 0.9, 0.999, 1e-08, 0.01, 10


def _params(*sem):
    return pltpu.CompilerParams(dimension_semantics=sem, vmem_limit_bytes=VMEM_LIMIT)


def _pick(n, cands):
    for c in cands:
        if n % c == 0:
            return c
    raise ValueError(f"no block size for {n}")


def _dot(a, b):
    return jnp.dot(a.astype(BF16), b.astype(BF16), preferred_element_type=F32)


def _dot_nt(a, b):
    return lax.dot_general(a.astype(BF16), b.astype(BF16), (((1,), (1,)), ((), ())), preferred_element_type=F32)


def _dot_tn(a, b):
    return lax.dot_general(a.astype(BF16), b.astype(BF16), (((0,), (0,)), ((), ())), preferred_element_type=F32)


def _dot_exact(mask, x, terms=3):
    m = mask.astype(BF16)
    dot = lambda t: jnp.dot(m, t, preferred_element_type=F32)
    hi = x.astype(BF16)
    r1 = x - hi.astype(F32)
    mid = r1.astype(BF16)
    if terms == 2:
        return dot(hi) + dot(mid)
    lo = (r1 - mid.astype(F32)).astype(BF16)
    return dot(hi) + dot(mid) + dot(lo)


def _sigmoid(x):
    return 1.0 / (1.0 + jnp.exp(-x))


def _softplus(x):
    return jnp.maximum(x, 0.0) + jnp.log1p(jnp.exp(-jnp.abs(x)))


def _sum_all(x):
    return jnp.sum(jnp.sum(x, axis=1, keepdims=True), axis=0, keepdims=True)


ROW_TILES = (1056, 768, 704, 512, 384, 256, 128)
TILE_BUDGET = 28 * 1024 * 1024


def _row_tile(n, bytes_per_row, fixed_bytes, budget=TILE_BUDGET):
    for tm in ROW_TILES:
        if n % tm == 0 and 2 * (tm * bytes_per_row + fixed_bytes) <= budget:
            return tm
    raise ValueError(f"no row tile for {n}")


WIDE_BUDGET = 38 * 1024 * 1024


def _mm(a, w, *, name, tn=512, nt=False, pre=None, post=None, extras=(), out_dtype=F32, norm_w=None, rider=None):
    assert norm_w is None or (rider is None and out_dtype == F32)
    a_list = list(a) if isinstance(a, (list, tuple)) else [a]
    w_list = list(w) if isinstance(w, (list, tuple)) else [w]
    n_a, n_ex = len(a_list), len(extras)
    n = a_list[0].shape[0]
    shard = w_list[0].shape[2] if w_list[0].ndim == 3 else None
    assert shard is None or (not nt and n_a == 1 and shard % tn == 0)
    m = w_list[0].shape[0] * shard if shard else w_list[0].shape[0] if nt else w_list[0].shape[1]
    tn = min(tn, m)
    size = lambda dt: jnp.dtype(dt).itemsize
    per_row = (sum(x.shape[1] * size(x.dtype) for x in a_list) + m * size(out_dtype)
               + sum(m * size(e.dtype) for e in extras) + (2 * m if norm_w is not None else 0))
    tm = _row_tile(n, per_row, sum(x.size * size(x.dtype) for x in w_list) // 2, WIDE_BUDGET)
    n_norm = 0 if norm_w is None else 1

    def body(*refs):
        a_refs, w_refs, ex_refs = refs[:n_a], refs[n_a:2 * n_a], refs[2 * n_a:2 * n_a + n_ex]
        o_ref = refs[2 * n_a + n_ex + n_norm]
        avs = [(a_ref[...] if pre is None else pre(a_ref[...])).astype(BF16) for a_ref in a_refs]
        for c0 in range(0, m, tn):
            r = None
            for av, w_ref in zip(avs, w_refs):
                if shard:
                    term = _dot(av, w_ref[c0 // shard, :, c0 % shard:c0 % shard + tn])
                else:
                    term = _dot_nt(av, w_ref[c0:c0 + tn, :]) if nt else _dot(av, w_ref[:, c0:c0 + tn])
                r = term if r is None else r + term
            if post is not None:
                r = post(r, *[e[:, c0:c0 + tn] for e in ex_refs])
            o_ref[:, c0:c0 + tn] = r.astype(out_dtype)
        if n_norm:
            x = o_ref[...]
            scale = lax.rsqrt(jnp.mean(x * x, axis=-1, keepdims=True) + EPS)
            refs[2 * n_a + n_ex + 2][...] = (x * scale * refs[2 * n_a + n_ex][...]).astype(BF16)

    a_specs = [pl.BlockSpec((tm, x.shape[1]), lambda i: (i, 0)) for x in a_list]
    w_specs = [pl.BlockSpec(x.shape, lambda i, nd=x.ndim: (0,) * nd, pipeline_mode=pl.Buffered(1)) for x in w_list]
    blk = pl.BlockSpec((tm, m), lambda i: (i, 0))
    vec = [pl.BlockSpec((1, m), lambda i: (0, 0))] * n_norm
    grid = (n // tm,)
    ride = _Ride(rider, body, 2 * n_a + n_ex + n_norm, 1 + n_norm, 0, grid)
    outs = pl.pallas_call(
        ride.body, name=name, grid=grid,
        in_specs=a_specs + w_specs + [blk] * n_ex + vec + ride.in_specs,
        out_specs=[blk] * (1 + n_norm) + ride.out_specs,
        out_shape=[jax.ShapeDtypeStruct((n, m), out_dtype)] + [jax.ShapeDtypeStruct((n, m), BF16)] * n_norm + ride.out_shape,
        scratch_shapes=ride.scratch, compiler_params=_params(*ride.semantics(("parallel",))),
    )(*a_list, *w_list, *extras, *([norm_w] * n_norm), *ride.args)
    if n_norm:
        return outs[0], outs[1]
    return (outs[0], outs[1:]) if rider else outs[0]


def _mm_fanout(a, ws, *, name, tn=512):
    n, k = a.shape
    ms = [w.shape[0] for w in ws]
    tm = _row_tile(n, k * 2 + 4 * sum(ms), sum(w.size for w in ws), WIDE_BUDGET)
    n_w = len(ws)

    def body(a_ref, *refs):
        av = a_ref[...]
        for w_ref, o_ref, m in zip(refs[:n_w], refs[n_w:], ms):
            for c0 in range(0, m, tn):
                o_ref[:, c0:c0 + tn] = _dot_nt(av, w_ref[c0:c0 + tn, :])

    return pl.pallas_call(
        body, name=name, grid=(n // tm,),
        in_specs=[pl.BlockSpec((tm, k), lambda i: (i, 0))]
        + [pl.BlockSpec(w.shape, lambda i: (0, 0), pipeline_mode=pl.Buffered(1)) for w in ws],
        out_specs=[pl.BlockSpec((tm, m), lambda i: (i, 0)) for m in ms],
        out_shape=[jax.ShapeDtypeStruct((n, m), F32) for m in ms],
        compiler_params=_params("parallel"),
    )(a, *ws)


def _mm_tn(a, g, *, name, tk, tn, pre=None, slab=None, rider=None):
    n, k = a.shape
    m = g.shape[1]
    tk, tn = min(tk, k), min(tn, m)
    tm = _row_tile(n, tk * jnp.dtype(a.dtype).itemsize + tn * jnp.dtype(g.dtype).itemsize, tk * tn * 4)
    steps = n // tm

    def body(a_ref, g_ref, o_ref, acc_ref):
        r = pl.program_id(2)

        @pl.when(r == 0)
        def _():
            acc_ref[...] = jnp.zeros_like(acc_ref)

        av = a_ref[...]
        if pre is not None:
            av = pre(av)
        if slab:
            for s in range(tn // slab):
                acc_ref[s] += _dot_tn(av, g_ref[:, s * slab:(s + 1) * slab])
        else:
            acc_ref[...] += _dot_tn(av, g_ref[...])

        @pl.when(r == steps - 1)
        def _():
            o_ref[...] = acc_ref[...].astype(BF16)

    if slab:
        block, out_spec = (tn // slab, tk, slab), pl.BlockSpec((tn // slab, tk, slab), lambda i, j, r: (j, i, 0))
        out_shape = jax.ShapeDtypeStruct((m // slab, k, slab), BF16)
    else:
        block, out_spec = (tk, tn), pl.BlockSpec((tk, tn), lambda i, j, r: (i, j))
        out_shape = jax.ShapeDtypeStruct((k, m), BF16)
    grid = (k // tk, m // tn, steps)
    ride = _Ride(rider, body, 2, 1, 1, grid)
    outs = pl.pallas_call(
        ride.body, name=name, grid=grid,
        in_specs=[pl.BlockSpec((tm, tk), lambda i, j, r: (r, i)), pl.BlockSpec((tm, tn), lambda i, j, r: (r, j))]
        + ride.in_specs,
        out_specs=[out_spec] + ride.out_specs, out_shape=[out_shape] + ride.out_shape,
        scratch_shapes=[pltpu.VMEM(block, F32)] + ride.scratch,
        compiler_params=_params(*ride.semantics(("parallel", "parallel", "arbitrary"))),
    )(a, g, *ride.args)
    return (outs[0], outs[1:]) if rider else outs[0]


def _mm_tn_cat(a_list, g, *, name):
    n, m = g.shape
    ks = [a.shape[1] for a in a_list]
    size = lambda x: jnp.dtype(x.dtype).itemsize
    tm = _row_tile(n, sum(a.shape[1] * size(a) for a in a_list) + m * size(g), sum(ks) * m * 4)
    steps, n_a = n // tm, len(a_list)

    def body(*refs):
        g_ref, o_ref, acc_ref = refs[n_a], refs[n_a + 1], refs[n_a + 2]
        r = pl.program_id(0)

        @pl.when(r == 0)
        def _():
            acc_ref[...] = jnp.zeros_like(acc_ref)

        gv, k0 = g_ref[...], 0
        for a_ref, k in zip(refs[:n_a], ks):
            acc_ref[k0:k0 + k, :] += _dot_tn(a_ref[...], gv)
            k0 += k

        @pl.when(r == steps - 1)
        def _():
            o_ref[...] = acc_ref[...].astype(BF16)

    return pl.pallas_call(
        body, name=name, grid=(steps,),
        in_specs=[pl.BlockSpec((tm, k), lambda r: (r, 0)) for k in ks] + [pl.BlockSpec((tm, m), lambda r: (r, 0))],
        out_specs=pl.BlockSpec((sum(ks), m), lambda r: (0, 0)),
        out_shape=jax.ShapeDtypeStruct((sum(ks), m), BF16),
        scratch_shapes=[pltpu.VMEM((sum(ks), m), F32)],
        compiler_params=_params("arbitrary"),
    )(*a_list, g)


def _mm_rms_bwd(a, w, h, w_norm, dres, *, name):
    n, k = a.shape
    d = h.shape[1]
    slabs, _, ks = w.shape
    tm = _row_tile(n, k * jnp.dtype(a.dtype).itemsize + d * (4 + 4 + 4 + 2), d * k, WIDE_BUDGET)

    def body(a_ref, w_ref, h_ref, wn_ref, dres_ref, dx_ref, dxb_ref, dw_ref):
        @pl.when(pl.program_id(0) == 0)
        def _():
            dw_ref[...] = jnp.zeros_like(dw_ref)

        dyv = None
        for s in range(slabs):
            part = _dot_nt(a_ref[:, s * ks:(s + 1) * ks], w_ref[s])
            dyv = part if dyv is None else dyv + part
        x = h_ref[...]
        r = lax.rsqrt(jnp.mean(x * x, axis=-1, keepdims=True) + EPS)
        g = dyv * wn_ref[...]
        dx = r * (g - x * (r * r) * jnp.mean(g * x, axis=-1, keepdims=True)) + dres_ref[...]
        dx_ref[...] = dx
        dxb_ref[...] = dx.astype(BF16)
        dw_ref[...] += jnp.sum(dyv * x * r, axis=0, keepdims=True)

    row = pl.BlockSpec((tm, d), lambda i: (i, 0))
    vec = pl.BlockSpec((1, d), lambda i: (0, 0))
    return pl.pallas_call(
        body, name=name, grid=(n // tm,),
        in_specs=[pl.BlockSpec((tm, k), lambda i: (i, 0)),
                  pl.BlockSpec(w.shape, lambda i: (0, 0, 0), pipeline_mode=pl.Buffered(1)), row, vec, row],
        out_specs=[row, row, vec],
        out_shape=[jax.ShapeDtypeStruct((n, d), F32), jax.ShapeDtypeStruct((n, d), BF16), jax.ShapeDtypeStruct((1, d), F32)],
        compiler_params=_params("arbitrary"),
    )(a, w, h, w_norm, dres)


def _embed_norm(x, meta, w, *, name, rider=None):
    bsz, seq, d = x.shape
    t = seq + CHUNK
    nc = t // CHUNK

    def body(x_ref, meta_ref, w_ref, h_ref, hn_ref):
        j = pl.program_id(0)
        first = jnp.concatenate([jnp.zeros((PAD, d), F32), meta_ref[...]], axis=0)
        for e in range(bsz):
            h = jnp.where(j == 0, first, x_ref[e])
            r = lax.rsqrt(jnp.mean(h * h, axis=-1, keepdims=True) + EPS)
            h_ref[e] = h
            hn_ref[e] = (h * r * w_ref[...]).astype(BF16)

    row = pl.BlockSpec((bsz, CHUNK, d), lambda j: (0, j, 0))
    grid = (nc,)
    ride = _Ride(rider, body, 3, 2, 0, grid)
    outs = pl.pallas_call(
        ride.body, name=name, grid=grid,
        in_specs=[pl.BlockSpec((bsz, CHUNK, d), lambda j: (0, jnp.maximum(j - 1, 0), 0)),
                  pl.BlockSpec((N_META, d), lambda j: (0, 0)), pl.BlockSpec((1, d), lambda j: (0, 0))] + ride.in_specs,
        out_specs=[row, row] + ride.out_specs,
        out_shape=[jax.ShapeDtypeStruct((bsz, t, d), F32), jax.ShapeDtypeStruct((bsz, t, d), BF16)] + ride.out_shape,
        scratch_shapes=ride.scratch, compiler_params=_params(*ride.semantics(("parallel",))),
    )(x, meta, w, *ride.args)
    return outs[:2], outs[2:]


def _ff2_loss(act, w2, h1, target, w, nc, *, name):
    n, f = act.shape
    d = h1.shape[1]
    tm = _pick(n, (768, 384, 128))
    per_tile = tm // CHUNK

    def body(act_ref, w2_ref, h1_ref, tgt_ref, w_ref, dh_ref, dhb_ref, loss_ref, dw_ref, tbuf_ref, sems):
        i = pl.program_id(0)

        @pl.when(i == 0)
        def _():
            loss_ref[...] = jnp.zeros_like(loss_ref)
            dw_ref[...] = jnp.zeros_like(dw_ref)

        chunks = [i * per_tile + cc for cc in range(per_tile)]
        fetch = [pltpu.make_async_copy(
            tgt_ref.at[c // nc, pl.ds(pl.multiple_of(jnp.maximum(c % nc - 1, 0) * CHUNK, CHUNK), CHUNK)],
            tbuf_ref.at[pl.ds(cc * CHUNK, CHUNK)], sems.at[cc]) for cc, c in enumerate(chunks)]
        for cc, c in enumerate(chunks):
            @pl.when(c % nc > 0)
            def _(cc=cc):
                fetch[cc].start()

            @pl.when(c % nc == 0)
            def _(cc=cc):
                tbuf_ref[pl.ds(cc * CHUNK, CHUNK), :] = jnp.zeros((CHUNK, d), F32)

        av = jnp.square(jnp.maximum(act_ref[...], 0)).astype(BF16)
        for c0 in range(0, d, 512):
            dh_ref[:, c0:c0 + 512] = h1_ref[:, c0:c0 + 512] + _dot(av, w2_ref[:, c0:c0 + 512])
        wv = w_ref[...]
        for cc, c in enumerate(chunks):
            @pl.when(c % nc > 0)
            def _(cc=cc):
                fetch[cc].wait()

            rows = pl.ds(cc * CHUNK, CHUNK)
            x = dh_ref[rows, :]
            r = lax.rsqrt(jnp.mean(x * x, axis=-1, keepdims=True) + EPS)
            diff = jnp.where(c % nc > 0, x * r * wv - tbuf_ref[rows, :], 0.0)
            loss_ref[...] += _sum_all(diff * diff) * (0.5 / d)
            dy = diff * (1.0 / d)
            g = dy * wv
            dh = r * (g - x * (r * r) * jnp.mean(g * x, axis=-1, keepdims=True))
            dh_ref[rows, :] = dh
            dhb_ref[rows, :] = dh.astype(BF16)
            dw_ref[...] += jnp.sum(dy * x * r, axis=0, keepdims=True)

    row = lambda width: pl.BlockSpec((tm, width), lambda i: (i, 0))
    return pl.pallas_call(
        body, name=name, grid=(n // tm,),
        in_specs=[row(f), pl.BlockSpec(w2.shape, lambda i: (0, 0), pipeline_mode=pl.Buffered(1)), row(d), ANY,
                  pl.BlockSpec((1, d), lambda i: (0, 0))],
        out_specs=[row(d), row(d), pl.BlockSpec((8, LANES), lambda i: (0, 0)), pl.BlockSpec((1, d), lambda i: (0, 0))],
        out_shape=[jax.ShapeDtypeStruct((n, d), F32), jax.ShapeDtypeStruct((n, d), BF16),
                   jax.ShapeDtypeStruct((8, LANES), F32), jax.ShapeDtypeStruct((1, d), F32)],
        scratch_shapes=[pltpu.VMEM((tm, d), F32), pltpu.SemaphoreType.DMA((per_tile,))],
        compiler_params=_params("arbitrary"),
    )(act, w2, h1, target, w)


def _pool_masks(j, transposed):
    r = lax.broadcasted_iota(jnp.int32, (CHUNK, 2 * CHUNK), 0)
    c = lax.broadcasted_iota(jnp.int32, (CHUNK, 2 * CHUNK), 1)
    masks = []
    for w in POOL_WINDOWS:
        if transposed:
            m = (c >= r) & (c < r + w)
        else:
            s = c - CHUNK
            m = (s <= r) & (s > r - w) & (s + j * CHUNK >= 0)
        masks.append(m.astype(F32))
    return masks


POOL_TERMS = 2


def _pool_count(t_global, w):
    return jnp.clip(t_global - PAD + 1, 1, w).astype(F32)


def _pool_fwd(u, pool_w, pool_scale, *, name):
    bsz, t, _ = u.shape
    nc = t // CHUNK

    def body(prev_ref, cur_ref, pw_ref, sc_ref, o_ref):
        j = pl.program_id(0)
        masks = _pool_masks(j, False)
        tg = j * CHUNK + lax.broadcasted_iota(jnp.int32, (CHUNK, 1), 0)
        count = [_pool_count(tg, w) for w in POOL_WINDOWS]
        units = [(e, gi) for e in range(bsz) for gi in range(len(POOL_WINDOWS))]
        sl = lambda gi: pl.ds(gi * POOL_GROUP, POOL_GROUP)
        cur = {(e, gi): cur_ref[e, :, sl(gi)] for e, gi in units}
        both = {(e, gi): jnp.concatenate([prev_ref[e, :, sl(gi)], cur[e, gi]], axis=0) for e, gi in units}
        win = {(e, gi): _dot_exact(masks[gi], both[e, gi], POOL_TERMS) for e, gi in units}
        pooled = {(e, gi): win[e, gi] / count[gi] - cur[e, gi] for e, gi in units}
        mixed = {(e, gi): _dot(pooled[e, gi], pw_ref[gi]) for e, gi in units}
        for e, gi in units:
            o_ref[e, :, sl(gi)] = (mixed[e, gi] * sc_ref[:, sl(gi)]).astype(BF16)

    blk = lambda f: pl.BlockSpec((bsz, CHUNK, D_POOL), f)
    return pl.pallas_call(
        body, name=name, grid=(nc,),
        in_specs=[blk(lambda j: (0, jnp.maximum(j - 1, 0), 0)), blk(lambda j: (0, j, 0)),
                  pl.BlockSpec((4, POOL_GROUP, POOL_GROUP), lambda j: (0, 0, 0)),
                  pl.BlockSpec((1, D_POOL), lambda j: (0, 0))],
        out_specs=blk(lambda j: (0, j, 0)), out_shape=jax.ShapeDtypeStruct(u.shape, BF16),
        compiler_params=_params("parallel"),
    )(u, u, pool_w, pool_scale)


def _pool_bwd(u, dyp, pool_w, pool_scale, *, name):
    bsz, t, _ = u.shape
    nc = t // CHUNK

    def body(prev_ref, cur_ref, dy_ref, dyn_ref, pw_ref, sc_ref, du_ref, dpw_ref, dsc_ref):
        j = pl.program_id(0)

        @pl.when(j == 0)
        def _():
            dpw_ref[...] = jnp.zeros_like(dpw_ref)
            dsc_ref[...] = jnp.zeros_like(dsc_ref)

        fwd = _pool_masks(j, False)
        bwd = _pool_masks(j, True)
        tg = j * CHUNK + lax.broadcasted_iota(jnp.int32, (CHUNK, 1), 0)
        count = [_pool_count(tg, w) for w in POOL_WINDOWS]
        count_next = [_pool_count(tg + CHUNK, w) for w in POOL_WINDOWS]
        has_next = j < nc - 1
        groups = range(len(POOL_WINDOWS))
        units = [(e, gi) for e in range(bsz) for gi in groups]
        sl = lambda gi: pl.ds(gi * POOL_GROUP, POOL_GROUP)
        cur = {(e, gi): cur_ref[e, :, sl(gi)] for e, gi in units}
        both = {(e, gi): jnp.concatenate([prev_ref[e, :, sl(gi)], cur[e, gi]], axis=0) for e, gi in units}
        win = {(e, gi): _dot_exact(fwd[gi], both[e, gi], POOL_TERMS) for e, gi in units}
        pooled = {(e, gi): win[e, gi] / count[gi] - cur[e, gi] for e, gi in units}
        dy = {(e, gi): dy_ref[e, :, sl(gi)] for e, gi in units}
        mixed = {(e, gi): _dot(pooled[e, gi], pw_ref[gi]) for e, gi in units}
        dm = {(e, gi): dy[e, gi] * sc_ref[:, sl(gi)] for e, gi in units}
        dm_next = {(e, gi): jnp.where(has_next, dyn_ref[e, :, sl(gi)], 0.0) * sc_ref[:, sl(gi)] for e, gi in units}
        dpw = {(e, gi): _dot_tn(pooled[e, gi], dm[e, gi]) for e, gi in units}
        dpooled = {(e, gi): _dot_nt(dm[e, gi], pw_ref[gi]) for e, gi in units}
        dpooled_next = {(e, gi): _dot_nt(dm_next[e, gi], pw_ref[gi]) for e, gi in units}
        spread = {(e, gi): jnp.concatenate([dpooled[e, gi] / count[gi], dpooled_next[e, gi] / count_next[gi]], axis=0)
                  for e, gi in units}
        back = {(e, gi): _dot_exact(bwd[gi], spread[e, gi], POOL_TERMS) for e, gi in units}
        for e, gi in units:
            du_ref[e, :, sl(gi)] = (back[e, gi] - dpooled[e, gi]).astype(BF16)
        for gi in groups:
            dsc, dw = None, None
            for e in range(bsz):
                term = jnp.sum(dy[e, gi] * mixed[e, gi], axis=0, keepdims=True)
                dsc = term if dsc is None else dsc + term
                dw = dpw[e, gi] if dw is None else dw + dpw[e, gi]
            dsc_ref[:, sl(gi)] += dsc
            dpw_ref[gi] += dw

    blk = lambda f: pl.BlockSpec((bsz, CHUNK, D_POOL), f)
    return pl.pallas_call(
        body, name=name, grid=(nc,),
        in_specs=[blk(lambda j: (0, jnp.maximum(j - 1, 0), 0)), blk(lambda j: (0, j, 0)),
                  blk(lambda j: (0, j, 0)), blk(lambda j: (0, jnp.minimum(j + 1, nc - 1), 0)),
                  pl.BlockSpec((4, POOL_GROUP, POOL_GROUP), lambda j: (0, 0, 0)),
                  pl.BlockSpec((1, D_POOL), lambda j: (0, 0))],
        out_specs=[blk(lambda j: (0, j, 0)), pl.BlockSpec((4, POOL_GROUP, POOL_GROUP), lambda j: (0, 0, 0)),
                   pl.BlockSpec((1, D_POOL), lambda j: (0, 0))],
        out_shape=[jax.ShapeDtypeStruct(u.shape, BF16), jax.ShapeDtypeStruct((4, POOL_GROUP, POOL_GROUP), F32),
                   jax.ShapeDtypeStruct((1, D_POOL), F32)],
        compiler_params=_params("arbitrary"),
    )(u, u, dyp, dyp, pool_w, pool_scale)


CONV_SLAB = 512


def _conv_taps(tail, cur, keep_tail):
    ext = jnp.concatenate([jnp.where(keep_tail, tail, 0.0), cur], axis=0)
    return [(pltpu.roll(ext, CONV_W - 1 - k, 0) if k < CONV_W - 1 else ext)[8:] for k in range(CONV_W)]


def _conv_pre(taps, w_ref, b_ref, sl):
    acc = b_ref[:, sl]
    for k in range(CONV_W):
        acc = acc + w_ref[k:k + 1, sl] * taps[k]
    return acc


def _proj_conv(hn, w, conv_w, conv_b, *, name):
    n, d = hn.shape
    c = w.shape[0]
    assert PAD >= CONV_W - 1
    tm = _row_tile(n, d * 2 + c * (4 + 2), c * d, WIDE_BUDGET)

    def body(hn_ref, w_ref, cw_ref, cb_ref, xbc_ref, xc_ref, tail_ref):
        @pl.when(pl.program_id(0) == 0)
        def _():
            tail_ref[...] = jnp.zeros_like(tail_ref)

        av = hn_ref[...]
        starts = list(range(0, c, CONV_SLAB))

        def project(c0):
            xbc_ref[:, pl.ds(c0, CONV_SLAB)] = _dot_nt(av, w_ref[c0:c0 + CONV_SLAB, :])

        def convolve(c0):
            sl = pl.ds(c0, CONV_SLAB)
            xb = xbc_ref[:, sl]
            pre = _conv_pre(_conv_taps(tail_ref[:, sl], xb, True), cw_ref, cb_ref, sl)
            xc_ref[:, sl] = (pre * _sigmoid(pre)).astype(BF16)
            tail_ref[:, sl] = xb[tm - 8:, :]

        project(starts[0])
        for c0, c_next in zip(starts, starts[1:] + [None]):
            if c_next is not None:
                project(c_next)
            convolve(c0)

    row = lambda width: pl.BlockSpec((tm, width), lambda i: (i, 0))
    return pl.pallas_call(
        body, name=name, grid=(n // tm,),
        in_specs=[row(d), pl.BlockSpec(w.shape, lambda i: (0, 0), pipeline_mode=pl.Buffered(1)),
                  pl.BlockSpec((CONV_W, c), lambda i: (0, 0)), pl.BlockSpec((1, c), lambda i: (0, 0))],
        out_specs=[row(c), row(c)],
        out_shape=[jax.ShapeDtypeStruct((n, c), F32), jax.ShapeDtypeStruct((n, c), BF16)],
        scratch_shapes=[pltpu.VMEM((8, c), F32)],
        compiler_params=_params("arbitrary"),
    )(hn, w, conv_w, conv_b)


def _conv_bwd(xbc, dxs, db, dc, conv_w, conv_b, *, name, rider=None):
    bsz, t, c = xbc.shape
    tile = _pick(t, (3 * CHUNK, CHUNK))
    nc = t // tile
    halo = 16
    rows = tile + halo

    def body(tail_ref, cur_ref, head_ref, dxs_ref, db_ref, dc_ref, dxs_head, db_head, dc_head, w_ref, b_ref,
             dx_ref, dwb_ref):
        j = pl.program_id(1)

        @pl.when(j == 0)
        def _():
            dwb_ref[...] = jnp.zeros_like(dwb_ref)

        has_prev, has_next = j > 0, j < nc - 1
        for c0 in range(0, c, CONV_SLAB):
            sl = pl.ds(c0, CONV_SLAB)
            if c0 < D_SSM:
                dxc, dxc_next = dxs_ref[0, :, sl], dxs_head[0, :, sl]
            elif c0 < D_SSM + D_POOL:
                dxc, dxc_next = db_ref[0], db_head[0]
            else:
                dxc, dxc_next = dc_ref[0], dc_head[0]
            dxc = jnp.concatenate([dxc.astype(F32), jnp.where(has_next, dxc_next.astype(F32), 0.0)], axis=0)
            ext = jnp.concatenate([jnp.where(has_prev, tail_ref[0, :, sl], 0.0), cur_ref[0, :, sl],
                                   jnp.where(has_next, head_ref[0, :, sl], 0.0)], axis=0)
            taps = [(pltpu.roll(ext, CONV_W - 1 - k, 0) if k < CONV_W - 1 else ext)[8:] for k in range(CONV_W)]
            pre = _conv_pre(taps, w_ref, b_ref, sl)
            s = _sigmoid(pre)
            dpre = dxc * (s * (1.0 + pre * (1.0 - s)))
            acc = w_ref[CONV_W - 1:CONV_W, sl] * dpre[:tile]
            for k in range(CONV_W - 1):
                up = CONV_W - 1 - k
                acc = acc + w_ref[k:k + 1, sl] * pltpu.roll(dpre, rows - up, 0)[:tile]
            dx_ref[0, :, sl] = acc.astype(BF16)
            for k in range(CONV_W):
                dwb_ref[0, k:k + 1, sl] += jnp.sum(dpre[:tile] * taps[k][:tile], axis=0, keepdims=True)
            dwb_ref[0, CONV_W:CONV_W + 1, sl] += jnp.sum(dpre[:tile], axis=0, keepdims=True)

    assert CONV_SLAB == D_POOL and D_SSM % CONV_SLAB == 0
    row = lambda width: pl.BlockSpec((1, tile, width), lambda b, j: (b, j, 0))
    nxt = lambda width: pl.BlockSpec(
        (1, halo, width), lambda b, j: (b, jnp.minimum((j + 1) * (tile // halo), t // halo - 1), 0))
    grid = (bsz, nc)
    ride = _Ride(rider, body, 11, 2, 0, grid)
    outs = pl.pallas_call(
        ride.body, name=name, grid=grid,
        in_specs=[pl.BlockSpec((1, 8, c), lambda b, j: (b, jnp.maximum(j * (tile // 8) - 1, 0), 0)), row(c), nxt(c),
                  row(D_SSM), row(D_POOL), row(D_POOL), nxt(D_SSM), nxt(D_POOL), nxt(D_POOL),
                  pl.BlockSpec((CONV_W, c), lambda b, j: (0, 0)), pl.BlockSpec((1, c), lambda b, j: (0, 0))]
        + ride.in_specs,
        out_specs=[row(c), pl.BlockSpec((1, 8, c), lambda b, j: (b, 0, 0))] + ride.out_specs,
        out_shape=[jax.ShapeDtypeStruct(xbc.shape, BF16), jax.ShapeDtypeStruct((bsz, 8, c), F32)] + ride.out_shape,
        scratch_shapes=ride.scratch, compiler_params=_params(*ride.semantics(("parallel", "arbitrary"))),
    )(xbc, xbc, xbc, dxs, db, dc, dxs, db, dc, conv_w, conv_b, *ride.args)
    return outs[:2], outs[2:]


def _dt_valid(j):
    lane = lax.broadcasted_iota(jnp.int32, (CHUNK, LANES), 1)
    row = lax.broadcasted_iota(jnp.int32, (CHUNK, LANES), 0)
    return (lane < HPG) & ((j > 0) | (row >= PAD))


def _proj_uz_dt(hn, wu, wz, wdt, dtb, alog, nc, *, name):
    n, d = hn.shape
    tm = _pick(n, (768, 384, 128))
    per_tile = tm // CHUNK
    widths = (wu.shape[0], wz.shape[0], wdt.shape[0])

    def body(hn_ref, wu_ref, wz_ref, wdt_ref, dtb_ref, alog_ref, u_ref, z_ref, dtr_ref, dt_ref, acs_ref, tr_ref):
        i = pl.program_id(0)
        av = hn_ref[...]
        for w_ref, o_ref, m in zip((wu_ref, wz_ref, wdt_ref), (u_ref, z_ref, dtr_ref), widths):
            for c0 in range(0, m, 512):
                o_ref[:, c0:c0 + 512] = _dot_nt(av, w_ref[c0:c0 + 512, :])
        row = lax.broadcasted_iota(jnp.int32, (CHUNK, LANES), 0)
        lane = lax.broadcasted_iota(jnp.int32, (CHUNK, LANES), 1)
        tril = (row >= lane).astype(F32)
        units = [(cc, g) for cc in range(per_tile) for g in range(N_GROUPS)]
        at = lambda cc, g: (pl.ds(cc * CHUNK, CHUNK), pl.ds(g * LANES, LANES))
        valid = [(lane < HPG) & (((i * per_tile + cc) % nc > 0) | (row >= PAD)) for cc in range(per_tile)]
        dt = {(cc, g): jnp.where(valid[cc], _softplus(dtr_ref[at(cc, g)] + dtb_ref[g]), 0.0) for cc, g in units}
        acs = {(cc, g): _dot_exact(tril, dt[cc, g] * -jnp.exp(alog_ref[g])) for cc, g in units}
        for cc, g in units:
            dt_ref[at(cc, g)] = dt[cc, g]
            acs_ref[at(cc, g)] = acs[cc, g]
            tr_ref[cc, g, 0:8, :] = dt[cc, g].T[0:8]
            tr_ref[cc, g, 8:16, :] = acs[cc, g].T[0:8]

    row_blk = lambda width: pl.BlockSpec((tm, width), lambda i: (i, 0))
    whole = lambda w: pl.BlockSpec(w.shape, lambda i: (0, 0), pipeline_mode=pl.Buffered(1))
    const = pl.BlockSpec((N_GROUPS, 1, LANES), lambda i: (0, 0, 0))
    return pl.pallas_call(
        body, name=name, grid=(n // tm,),
        in_specs=[row_blk(d), whole(wu), whole(wz), whole(wdt), const, const],
        out_specs=[row_blk(widths[0]), row_blk(widths[1])] + [row_blk(D_DT)] * 3
        + [pl.BlockSpec((per_tile, N_GROUPS, 16, LANES), lambda i: (i, 0, 0, 0))],
        out_shape=[jax.ShapeDtypeStruct((n, widths[0]), F32), jax.ShapeDtypeStruct((n, widths[1]), F32)]
        + [jax.ShapeDtypeStruct((n, D_DT), F32)] * 3 + [jax.ShapeDtypeStruct((n // CHUNK, N_GROUPS, 16, LANES), F32)],
        compiler_params=_params("parallel"),
    )(hn, wu, wz, wdt, dtb, alog)


def _ssd_decay(dt, acs, tr):
    lane = lax.broadcasted_iota(jnp.int32, (CHUNK, LANES), 1)
    row = lax.broadcasted_iota(jnp.int32, (CHUNK, LANES), 0)
    return dict(lane=lane, row=row, dt=dt, causal=row >= lane, acs=acs, acs_t=tr[8:16], dt_t=tr[0:8],
                aend=acs[CHUNK - 1:CHUNK, :])


def _ssd_specs(bsz, nc, rev):
    ch = (lambda j: nc - 1 - j) if rev else (lambda j: j)
    return dict(
        xs=pl.BlockSpec((bsz, CHUNK, GW), lambda g, j: (0, ch(j), g)),
        bm=pl.BlockSpec((bsz, CHUNK, D_STATE), lambda g, j: (0, ch(j), D_SSM // D_STATE + g)),
        cm=pl.BlockSpec((bsz, CHUNK, D_STATE), lambda g, j: (0, ch(j), D_SSM // D_STATE + N_GROUPS + g)),
        lane_blk=pl.BlockSpec((bsz, CHUNK, LANES), lambda g, j: (0, ch(j), g)),
        grp_const=pl.BlockSpec((1, 1, LANES), lambda g, j: (g, 0, 0)),
        grp_vec=pl.BlockSpec((1, GW), lambda g, j: (0, g)),
        state=pl.BlockSpec((bsz, 1, D_STATE, GW), lambda g, j: (0, ch(j), 0, g)),
        tr=pl.BlockSpec((bsz, 1, 1, 16, LANES), lambda g, j: (0, ch(j), g, 0, 0)),
    )


def _ssd_fwd(xc, dt, acs, tr, z, dskip, normw, *, name, rider=None):
    bsz, t, _ = xc.shape
    nc = t // CHUNK
    sp = _ssd_specs(bsz, nc, False)

    def body(xs_ref, b_ref, c_ref, dt_ref, acs_ref, tr_ref, z_ref, dsk_ref, nw_ref, yn_ref, y_ref, sp_ref, s_ref):
        j = pl.program_id(1)

        @pl.when(j == 0)
        def _():
            s_ref[...] = jnp.zeros_like(s_ref)

        ex = range(bsz)
        units = [(e, r) for e in ex for r in range(HPG)]
        full = lambda v: jnp.broadcast_to(v, (CHUNK, LANES))
        pair = lambda r: pl.ds((r // 2) * LANES, LANES)
        q = [_ssd_decay(dt_ref[e], acs_ref[e], tr_ref[e, 0, 0]) for e in ex]
        for e in ex:
            sp_ref[e, 0] = s_ref[e]
        bm, cm = [b_ref[e] for e in ex], [c_ref[e] for e in ex]
        cb = [_dot_nt(cm[e], bm[e]) for e in ex]
        low = q[0]["lane"] < HEAD_DIM
        col = {(e, r): full(q[e]["acs"][:, r:r + 1]) for e, r in units}
        aend = {(e, r): q[e]["aend"][:, r:r + 1] for e, r in units}
        decay = {(e, r): jnp.exp(jnp.where(q[e]["causal"], col[e, r] - q[e]["acs_t"][r:r + 1, :], -jnp.inf))
                 for e, r in units}
        mp = {(e, r): cb[e] * decay[e, r] * q[e]["dt_t"][r:r + 1, :] for e, r in units}
        ce = {(e, r): cm[e] * jnp.exp(col[e, r]) for e, r in units}
        bk = {(e, r): bm[e] * (jnp.exp(aend[e, r] - col[e, r]) * full(q[e]["dt"][:, r:r + 1])) for e, r in units}
        xp = {(e, r): xs_ref[e, :, pair(r)] for e, r in units}
        s_old = {(e, r): s_ref[e, :, pair(r)] for e, r in units}
        y_h = {u: _dot(mp[u], xp[u]) + _dot(ce[u], s_old[u]) for u in units}
        s_h = {u: jnp.exp(aend[u]) * s_old[u] + _dot_tn(bk[u], xp[u]) for u in units}
        for e in ex:
            for r in range(0, HPG, 2):
                y_ref[e, :, pair(r)] = jnp.where(low, y_h[e, r], y_h[e, r + 1])
                s_ref[e, :, pair(r)] = jnp.where(low, s_h[e, r], s_h[e, r + 1])
        y = [y_ref[e] + dsk_ref[...] * xs_ref[e] for e in ex]
        zz = [z_ref[e] for e in ex]
        yg = [y[e] * (zz[e] * _sigmoid(zz[e])) for e in ex]
        rstd = [lax.rsqrt(jnp.mean(yg[e] * yg[e], axis=-1, keepdims=True) + EPS) for e in ex]
        for e in ex:
            y_ref[e] = y[e]
            yn_ref[e] = (yg[e] * rstd[e] * nw_ref[...]).astype(BF16)

    grid = (N_GROUPS, nc)
    ride = _Ride(rider, body, 9, 3, 1, grid)
    outs = pl.pallas_call(
        ride.body, name=name, grid=grid,
        in_specs=[sp["xs"], sp["bm"], sp["cm"], sp["lane_blk"], sp["lane_blk"], sp["tr"], sp["xs"],
                  sp["grp_vec"], sp["grp_vec"]] + ride.in_specs,
        out_specs=[sp["xs"], sp["xs"], sp["state"]] + ride.out_specs,
        out_shape=[jax.ShapeDtypeStruct((bsz, t, D_SSM), BF16), jax.ShapeDtypeStruct((bsz, t, D_SSM), F32),
                   jax.ShapeDtypeStruct((bsz, nc, D_STATE, D_SSM), F32)] + ride.out_shape,
        scratch_shapes=[pltpu.VMEM((bsz, D_STATE, GW), F32)] + ride.scratch,
        compiler_params=_params(*ride.semantics(("parallel", "arbitrary"))),
    )(xc, xc, xc, dt, acs, tr, z, dskip, normw, *ride.args)
    return outs[:3], outs[3:]


def _ssd_bwd(xc, dtr, dt, acs, tr, z, ypre, sprev, dyn, dtb, alog, dskip, normw, *, name, rider=None):
    bsz, t, _ = xc.shape
    nc = t // CHUNK
    sp = _ssd_specs(bsz, nc, True)

    def body(xs_ref, b_ref, c_ref, dtr_ref, dt_ref, acs_ref, tr_ref, z_ref, y_ref, sp_ref, dyn_ref, dtb_ref, alog_ref,
             dsk_ref, nw_ref, dz_ref, dxs_ref, db_ref, dc_ref, ddt_ref, dnw_ref, dsm_ref, ds_ref):
        j = pl.program_id(1)

        @pl.when(j == 0)
        def _():
            ds_ref[...] = jnp.zeros_like(ds_ref)
            dnw_ref[...] = jnp.zeros_like(dnw_ref)
            dsm_ref[...] = jnp.zeros_like(dsm_ref)

        ex = range(bsz)
        heads = range(HPG)
        units = [(e, r) for e in ex for r in heads]
        q = [_ssd_decay(dt_ref[e], acs_ref[e], tr_ref[e, 0, 0]) for e in ex]
        a = -jnp.exp(alog_ref[0])
        valid = _dt_valid(nc - 1 - j)
        lane, row = q[0]["lane"], q[0]["row"]
        lane1 = lane[0:1, :]
        nw = nw_ref[...]
        y, zz, dyn = [y_ref[e] for e in ex], [z_ref[e] for e in ex], [dyn_ref[e] for e in ex]
        sz = [_sigmoid(zz[e]) for e in ex]
        sil = [zz[e] * sz[e] for e in ex]
        yg = [y[e] * sil[e] for e in ex]
        rstd = [lax.rsqrt(jnp.mean(yg[e] * yg[e], axis=-1, keepdims=True) + EPS) for e in ex]
        gn = [dyn[e] * nw for e in ex]
        dyg = [rstd[e] * (gn[e] - yg[e] * (rstd[e] * rstd[e]) * jnp.mean(gn[e] * yg[e], axis=-1, keepdims=True))
               for e in ex]
        dy = [dyg[e] * sil[e] for e in ex]
        xs = [xs_ref[e] for e in ex]
        for e in ex:
            dnw_ref[e] += jnp.sum(dyn[e] * yg[e] * rstd[e], axis=0, keepdims=True)
            dz_ref[e] = (dyg[e] * y[e] * (sz[e] * (1.0 + zz[e] * (1.0 - sz[e])))).astype(BF16)
        dskip_cols = [jnp.sum(dy[e] * xs[e], axis=0, keepdims=True) for e in ex]

        bm, cm = [b_ref[e] for e in ex], [c_ref[e] for e in ex]
        cb = [_dot_nt(cm[e], bm[e]) for e in ex]
        zero = jnp.zeros((CHUNK, LANES), F32)
        full = lambda v: jnp.broadcast_to(v, (CHUNK, LANES))
        low = lane < HEAD_DIM
        half = [low if r % 2 == 0 else ~low for r in heads]
        sl = lambda v, r: v[:, (r // 2) * LANES:(r // 2 + 1) * LANES]
        pair = lambda r: pl.ds((r // 2) * LANES, LANES)
        col = {(e, r): full(q[e]["acs"][:, r:r + 1]) for e, r in units}
        dt_col = {(e, r): full(q[e]["dt"][:, r:r + 1]) for e, r in units}
        aend = {(e, r): q[e]["aend"][:, r:r + 1] for e, r in units}
        dt_row = {(e, r): q[e]["dt_t"][r:r + 1, :] for e, r in units}
        decay = {(e, r): jnp.exp(jnp.where(q[e]["causal"], col[e, r] - q[e]["acs_t"][r:r + 1, :], -jnp.inf))
                 for e, r in units}
        ea = {u: jnp.exp(col[u]) for u in units}
        dte = {u: jnp.exp(aend[u] - col[u]) for u in units}
        ed = {u: jnp.exp(aend[u]) for u in units}
        k = {u: dte[u] * dt_col[u] for u in units}
        mp = {(e, r): cb[e] * decay[e, r] * dt_row[e, r] for e, r in units}
        xp = {(e, r): sl(xs[e], r) for e, r in units}
        dym = {(e, r): jnp.where(half[r], sl(dy[e], r), 0.0) for e, r in units}
        s_old = {(e, r): sp_ref[e, 0, :, pair(r)] for e, r in units}
        ds_old = {(e, r): ds_ref[e, :, pair(r)] for e, r in units}
        dsm = {(e, r): jnp.where(half[r], ds_old[e, r], 0.0) for e, r in units}
        gmat = {u: _dot_nt(dym[u], xp[u]) for u in units}
        t1 = {u: _dot_nt(dym[u], s_old[u]) for u in units}
        dbs = {u: _dot_nt(xp[u], dsm[u]) for u in units}
        dx = {(e, r): _dot_tn(mp[e, r], dym[e, r]) + _dot(bm[e] * k[e, r], dsm[e, r]) for e, r in units}
        ds = {(e, r): _dot_tn(cm[e] * ea[e, r], dym[e, r]) for e, r in units}
        gd = {u: gmat[u] * decay[u] for u in units}
        w0 = {(e, r): gd[e, r] * cb[e] for e, r in units}
        cs0 = {u: jnp.sum(w0[u], axis=0, keepdims=True) for u in units}
        rs = {u: jnp.sum(w0[u] * dt_row[u], axis=1, keepdims=True) for u in units}
        qv = {(e, r): jnp.sum(cm[e] * t1[e, r], axis=1, keepdims=True) for e, r in units}
        dk = {(e, r): jnp.sum(bm[e] * dbs[e, r], axis=1, keepdims=True) for e, r in units}
        ddte = {u: dk[u] * dt_col[u] for u in units}
        d_aend = {u: _sum_all(dsm[u] * s_old[u]) * ed[u] + _sum_all(ddte[u][:, 0:1] * dte[u][:, 0:1]) for u in units}
        last_row = row == CHUNK - 1
        dacs_col = {u: rs[u] + qv[u] * ea[u] - ddte[u] * dte[u] + jnp.where(last_row, d_aend[u], 0.0) for u in units}
        triu = (lane >= row).astype(F32)
        for e in ex:
            dcb, dc_acc, db_acc = zero, zero, zero
            dacs, dacs_t, ddt, ddt_t = zero, zero, zero, zero
            dskip_row = jnp.zeros((1, LANES), F32)
            for r in heads:
                u = (e, r)
                dcb = dcb + gd[u] * dt_row[u]
                dc_acc = dc_acc + ea[u] * t1[u]
                db_acc = db_acc + k[u] * dbs[u]
                dacs = jnp.where(lane == r, dacs_col[u], dacs)
                ddt = jnp.where(lane == r, dk[u] * dte[u], ddt)
                dacs_t = jnp.where(row == r, -cs0[u] * dt_row[u], dacs_t)
                ddt_t = jnp.where(row == r, cs0[u], ddt_t)
                dsk = _sum_all(jnp.where(half[r][0:1, :], sl(dskip_cols[e], r), 0.0))
                dskip_row = dskip_row + jnp.where(lane1 == r, dsk, 0.0)
            for r in range(0, HPG, 2):
                dxs_ref[e, :, pair(r)] = (dx[e, r] + dx[e, r + 1] + sl(dy[e], r) * dsk_ref[:, pair(r)]).astype(BF16)
                ed_pair = jnp.where(lane1 < HEAD_DIM, ed[e, r], ed[e, r + 1])
                ds_ref[e, :, pair(r)] = ds[e, r] + ds[e, r + 1] + ed_pair * ds_old[e, r]
            dacs = dacs + dacs_t.T
            ddt = ddt + ddt_t.T
            dda = _dot_exact(triu, dacs)
            ddt = ddt + dda * a
            da = jnp.sum(dda * q[e]["dt"], axis=0, keepdims=True)
            draw = jnp.where(valid, ddt * _sigmoid(dtr_ref[e] + dtb_ref[0]), 0.0)
            ddt_ref[e] = draw.astype(BF16)
            dsm_ref[e, 0, 0:1, :] += dskip_row
            dsm_ref[e, 0, 1:2, :] += da * a
            dsm_ref[e, 0, 2:3, :] += jnp.sum(draw, axis=0, keepdims=True)
            dc_ref[e] = (dc_acc + _dot(dcb, bm[e])).astype(BF16)
            db_ref[e] = (db_acc + _dot_tn(dcb, cm[e])).astype(BF16)

    grp_out = pl.BlockSpec((bsz, CHUNK, D_STATE), lambda g, j: (0, nc - 1 - j, g))
    grid = (N_GROUPS, nc)
    ride = _Ride(rider, body, 15, 7, 1, grid)
    outs = pl.pallas_call(
        ride.body, name=name, grid=grid,
        in_specs=[sp["xs"], sp["bm"], sp["cm"], sp["lane_blk"], sp["lane_blk"], sp["lane_blk"], sp["tr"], sp["xs"],
                  sp["xs"], sp["state"], sp["xs"], sp["grp_const"], sp["grp_const"], sp["grp_vec"], sp["grp_vec"]]
        + ride.in_specs,
        out_specs=[sp["xs"], sp["xs"], grp_out, grp_out, sp["lane_blk"],
                   pl.BlockSpec((bsz, 1, GW), lambda g, j: (0, 0, g)),
                   pl.BlockSpec((bsz, 1, 8, LANES), lambda g, j: (0, g, 0, 0))] + ride.out_specs,
        out_shape=[jax.ShapeDtypeStruct((bsz, t, D_SSM), BF16), jax.ShapeDtypeStruct((bsz, t, D_SSM), BF16),
                   jax.ShapeDtypeStruct((bsz, t, N_GROUPS * D_STATE), BF16),
                   jax.ShapeDtypeStruct((bsz, t, N_GROUPS * D_STATE), BF16),
                   jax.ShapeDtypeStruct((bsz, t, D_DT), BF16), jax.ShapeDtypeStruct((bsz, 1, D_SSM), F32),
                   jax.ShapeDtypeStruct((bsz, N_GROUPS, 8, LANES), F32)] + ride.out_shape,
        scratch_shapes=[pltpu.VMEM((bsz, D_STATE, GW), F32)] + ride.scratch,
        compiler_params=_params(*ride.semantics(("parallel", "arbitrary"))),
    )(xc, xc, xc, dtr, dt, acs, tr, z, ypre, sprev, dyn, dtb, alog, dskip, normw, *ride.args)
    return outs[:7], outs[7:]


def _input_grad(dhn, h0, w, dres, seq, *, name):
    bsz, t, d = h0.shape
    nc = t // CHUNK

    def body(dy_ref, h_ref, w_ref, dres_ref, gx_ref, head_ref, dw_ref):
        j = pl.program_id(0)

        @pl.when(j == 0)
        def _():
            dw_ref[...] = jnp.zeros_like(dw_ref)

        for e in range(bsz):
            x, dyv = h_ref[e], dy_ref[e]
            r = lax.rsqrt(jnp.mean(x * x, axis=-1, keepdims=True) + EPS)
            g = dyv * w_ref[...]
            dx = r * (g - x * (r * r) * jnp.mean(g * x, axis=-1, keepdims=True)) + dres_ref[e]
            dw_ref[...] += jnp.sum(dyv * x * r, axis=0, keepdims=True)
            gx_ref[e] = dx

        @pl.when(j == 0)
        def _():
            head_ref[...] = gx_ref[...]

    row = pl.BlockSpec((bsz, CHUNK, d), lambda j: (0, j, 0))
    return pl.pallas_call(
        body, name=name, grid=(nc,),
        in_specs=[row, row, pl.BlockSpec((1, d), lambda j: (0, 0)), row],
        out_specs=[pl.BlockSpec((bsz, CHUNK, d), lambda j: (0, jnp.maximum(j - 1, 0), 0)),
                   pl.BlockSpec((bsz, CHUNK, d), lambda j: (0, 0, 0)), pl.BlockSpec((1, d), lambda j: (0, 0))],
        out_shape=[jax.ShapeDtypeStruct((bsz, seq, d), F32), jax.ShapeDtypeStruct((bsz, CHUNK, d), F32),
                   jax.ShapeDtypeStruct((1, d), F32)],
        compiler_params=_params("arbitrary"),
    )(dhn, h0, w, dres)


def _remote(src, dst, send_sem, recv_sem, dev):
    return pltpu.make_async_remote_copy(src_ref=src, dst_ref=dst, send_sem=send_sem, recv_sem=recv_sem,
                                        device_id=dev, device_id_type=MESH)


def _position():
    return lax.axis_index("x"), lax.axis_index("y"), lax.axis_index("c")


def _other_chips(pos):
    x, y, _ = pos
    return [(1 - x, y), (x, 1 - y), (1 - x, 1 - y)]


class _Gather:
    def __init__(self, arrs):
        n = len(arrs)
        self.args, self.n_in, self.n_out = list(arrs), n, n
        self.split = [a.ndim == 2 and a.shape[1] % (2 * LANES) == 0 for a in arrs]
        self.out_shape = [jax.ShapeDtypeStruct((4,) + a.shape, a.dtype) for a in arrs]
        self.scratch = [pltpu.SemaphoreType.DMA((3 * n,)), pltpu.SemaphoreType.DMA((3 * n,)),
                        pltpu.SemaphoreType.DMA((n,)), pltpu.SemaphoreType.DMA((3 * n,)),
                        pltpu.SemaphoreType.DMA((3 * n,))]

    def _copies(self, pos, ins, outs, sems):
        send_sems, recv_sems, loc_sems, pass_send_sems, pass_recv_sems = sems
        x, y, c = pos
        me, sibling = 2 * x + y, (x, y, 1 - c)
        local = [pltpu.make_async_copy(ins[i], outs[i].at[me], loc_sems.at[i]) for i in range(self.n_in)]
        sends, recvs, passes, pass_recvs = [], [], [], []
        for i in range(self.n_in):
            half = self.args[i].shape[1] // 2 if self.split[i] else None
            for k, (px, py) in enumerate(_other_chips(pos)):
                them = 2 * px + py
                sems_k = (send_sems.at[3 * i + k], recv_sems.at[3 * i + k], (px, py, c))
                if half is None:
                    sends.append(_remote(ins[i], outs[i].at[me], *sems_k))
                    recvs.append(_remote(ins[i], outs[i].at[them], *sems_k))
                    passes.append(None)
                    continue
                mine = pl.ds(pl.multiple_of(c * half, LANES), half)
                other = pl.ds(pl.multiple_of((1 - c) * half, LANES), half)
                sends.append(_remote(ins[i].at[:, mine], outs[i].at[me, :, mine], *sems_k))
                recvs.append(_remote(ins[i].at[:, mine], outs[i].at[them, :, mine], *sems_k))
                pass_k = (pass_send_sems.at[3 * i + k], pass_recv_sems.at[3 * i + k], sibling)
                passes.append(_remote(outs[i].at[them, :, mine], outs[i].at[them, :, mine], *pass_k))
                pass_recvs.append(_remote(outs[i].at[them, :, other], outs[i].at[them, :, other], *pass_k))
        return local, sends, recvs, passes, pass_recvs

    def start(self, pos, ins, outs, sems):
        local, sends = self._copies(pos, ins, outs, sems)[:2]
        for cp in local + sends:
            cp.start()

    def relay(self, pos, ins, outs, sems):
        _, _, recvs, passes, _ = self._copies(pos, ins, outs, sems)
        for cp, onward in zip(recvs, passes):
            if onward is not None:
                cp.wait_recv()
                onward.start()

    def finish(self, pos, ins, outs, sems):
        local, sends, recvs, passes, pass_recvs = self._copies(pos, ins, outs, sems)
        for cp, onward in zip(recvs, passes):
            if onward is None:
                cp.wait_recv()
        for cp in pass_recvs:
            cp.wait_recv()
        for cp in sends + [p for p in passes if p is not None]:
            cp.wait_send()
        for cp in local:
            cp.wait()


class _Exchange:
    FLIPS = [(fx, fy, fc) for fx in (0, 1) for fy in (0, 1) for fc in (0, 1)][1:]

    def __init__(self, big, small=None):
        n = len(big)
        self.n_big, self.has_small = n, small is not None
        self.args = list(big) + ([small] if self.has_small else [])
        self.n_in = self.n_out = len(self.args)
        self.out_shape = [jax.ShapeDtypeStruct(a.shape, a.dtype) for a in big]
        self.scratch = [pltpu.SemaphoreType.DMA((max(3 * n, 1),)), pltpu.SemaphoreType.DMA((max(3 * n, 1),)),
                        pltpu.SemaphoreType.DMA((n + 1,))]
        if self.has_small:
            self.out_shape.append(jax.ShapeDtypeStruct((8,) + small.shape, small.dtype))
            self.scratch += [pltpu.SemaphoreType.DMA((7,)), pltpu.SemaphoreType.DMA((7,))]

    def _copies(self, pos, ins, outs, sems):
        x, y, c = pos
        me, me8 = 2 * x + y, 4 * x + 2 * y + c
        local, sends, recvs = [], [], []
        for i in range(self.n_big):
            local.append(pltpu.make_async_copy(ins[i].at[me], outs[i].at[me], sems[2].at[i]))
            for k, (px, py) in enumerate(_other_chips(pos)):
                sems_k = (sems[0].at[3 * i + k], sems[1].at[3 * i + k], (px, py, c))
                sends.append(_remote(ins[i].at[2 * px + py], outs[i].at[me], *sems_k))
                recvs.append(_remote(ins[i].at[me], outs[i].at[2 * px + py], *sems_k))
        if self.has_small:
            small, landed = ins[self.n_big], outs[self.n_big]
            local.append(pltpu.make_async_copy(small, landed.at[me8], sems[2].at[self.n_big]))
            for k, (fx, fy, fc) in enumerate(self.FLIPS):
                peer = (x ^ fx, y ^ fy, c ^ fc)
                sems_k = (sems[3].at[k], sems[4].at[k], peer)
                sends.append(_remote(small, landed.at[me8], *sems_k))
                recvs.append(_remote(small, landed.at[4 * peer[0] + 2 * peer[1] + peer[2]], *sems_k))
        return local, sends, recvs, [None] * len(recvs), []

    start = _Gather.start
    relay = _Gather.relay
    finish = _Gather.finish


class _Swap:
    def __init__(self, arrs):
        n = len(arrs)
        self.args, self.n_in, self.n_out = list(arrs), n, n
        self.out_shape = [jax.ShapeDtypeStruct(a.shape, a.dtype) for a in arrs]
        self.scratch = [pltpu.SemaphoreType.DMA((n,)), pltpu.SemaphoreType.DMA((n,))]

    def _copies(self, pos, ins, outs, sems):
        x, y, c = pos
        both = [_remote(ins[i], outs[i], sems[0].at[i], sems[1].at[i], (x, y, 1 - c)) for i in range(self.n_in)]
        return [], both, both, [None] * len(both), []

    start = _Gather.start
    relay = _Gather.relay
    finish = _Gather.finish


def _comm(rider, *, name):
    a, b = rider.n_in, rider.n_in + rider.n_out

    def body(*refs):
        pos = _position()
        rider.start(pos, refs[:a], refs[a:b], refs[b:])
        rider.relay(pos, refs[:a], refs[a:b], refs[b:])
        rider.finish(pos, refs[:a], refs[a:b], refs[b:])

    return pl.pallas_call(body, name=name, in_specs=[ANY] * rider.n_in, out_specs=[ANY] * rider.n_out,
                          out_shape=rider.out_shape, scratch_shapes=rider.scratch)(*rider.args)


class _Ride:
    RELAY_AT = 0.8

    def __init__(self, rider, body, n_in, n_out, n_scratch, grid):
        self.rider = rider
        self.args = rider.args if rider else []
        self.in_specs = [ANY] * rider.n_in if rider else []
        self.out_specs = [ANY] * rider.n_out if rider else []
        self.out_shape = rider.out_shape if rider else []
        self.scratch = rider.scratch if rider else []
        self.body = self._wrap(body, n_in, n_out, n_scratch, grid) if rider else body

    def semantics(self, sem):
        return ("arbitrary",) * len(sem) if self.rider else sem

    def _wrap(self, body, n_in, n_out, n_scratch, grid):
        rider = self.rider
        a = n_in
        b = a + rider.n_in
        c = b + n_out
        d = c + rider.n_out
        e = d + n_scratch

        def wrapped(*refs):
            pos = _position()
            ids = [pl.program_id(i) for i in range(len(grid))]
            step, total = 0, 1
            for i, g in zip(ids, grid):
                step, total = step * g + i, total * g

            @pl.when(step == 0)
            def _():
                rider.start(pos, refs[a:b], refs[c:d], refs[e:])

            body(*refs[:a], *refs[b:c], *refs[d:e])

            @pl.when(step == int(self.RELAY_AT * (total - 1)))
            def _():
                rider.relay(pos, refs[a:b], refs[c:d], refs[e:])

            @pl.when(step == total - 1)
            def _():
                rider.finish(pos, refs[a:b], refs[c:d], refs[e:])

        return wrapped


def _elementwise_tiles(r, c):
    if r % 8 == 0 and r * c > 65536:
        tm = _pick(r, (256, 128, 64, 16, 8))
        return (tm, c), r // tm, lambda i: (i, 0)
    if r % 8 and c % 256 == 0 and r * c > 65536:
        return (r, 256), c // 256, lambda i: (0, i)
    return (r, c), 1, lambda i: (0, 0)


def _chip_sum(landed, *, name):
    _, r, c = landed.shape
    blk, steps, at = _elementwise_tiles(r, c)

    def body(land_ref, o_ref):
        acc = land_ref[0].astype(F32)
        for jchip in range(1, 4):
            acc = acc + land_ref[jchip].astype(F32)
        o_ref[...] = acc

    return pl.pallas_call(
        body, name=name, grid=(steps,), in_specs=[pl.BlockSpec((4,) + blk, lambda i: (0,) + at(i))],
        out_specs=pl.BlockSpec(blk, at), out_shape=jax.ShapeDtypeStruct((r, c), F32),
        compiler_params=_params("parallel"),
    )(landed)


def _device_sum(parts, *, name):
    _, r, c = parts.shape

    def body(p_ref, o_ref):
        acc = p_ref[0]
        for d in range(1, 8):
            acc = acc + p_ref[d]
        o_ref[...] = acc

    return pl.pallas_call(body, name=name, out_shape=jax.ShapeDtypeStruct((r, c), F32))(parts)


def _adamw_math(w, g, m, v):
    m = ADAM_B1 * m + (1.0 - ADAM_B1) * g
    v = ADAM_B2 * v + (1.0 - ADAM_B2) * (g * g)
    m_hat = m / (1.0 - ADAM_B1 ** ADAM_STEP)
    v_hat = v / (1.0 - ADAM_B2 ** ADAM_STEP)
    return -ADAM_LR * (m_hat / (jnp.sqrt(v_hat) + ADAM_EPS) + ADAM_WD * w), m, v


def _adamw(w, g_parts, m, v, *, name):
    r, c = w.shape
    shape, steps, at = _elementwise_tiles(r, c)
    n_g = len(g_parts)

    def body(*refs):
        w_ref, m_ref, v_ref = refs[n_g:n_g + 3]
        g_ref, d_ref, nm_ref, nv_ref = refs[n_g + 3:]
        g = refs[0][...]
        for p in refs[1:n_g]:
            g = g + p[...]
        g_ref[...] = g
        d_ref[...], nm_ref[...], nv_ref[...] = _adamw_math(w_ref[...], g, m_ref[...], v_ref[...])

    blk = pl.BlockSpec(shape, at)
    return pl.pallas_call(
        body, name=name, grid=(steps,), in_specs=[blk] * (n_g + 3), out_specs=[blk] * 4,
        out_shape=[jax.ShapeDtypeStruct((r, c), F32)] * 4, compiler_params=_params("parallel"),
    )(*g_parts, w, m, v)


def _pad_heads(v):
    return jnp.pad(v.reshape(N_GROUPS, 1, HPG), ((0, 0), (0, 0), (0, LANES - HPG)))


def _unpad_heads(v):
    return v[:, :HPG].reshape(1, N_HEADS)


_SMALL_EARLY = [("pool_w", (512, 128)), ("pool_scale", (1, 512)), ("conv_w", (4, D_XBC)), ("conv_b", (1, D_XBC)),
                ("dt_bias", (1, N_HEADS)), ("a_log", (1, N_HEADS)), ("d_skip", (1, N_HEADS)), ("ssm_norm_w", (1, D_SSM)),
                ("norm_ffn_w", (1, 1024)), ("norm_f_w", (1, 1024))]
_SMALL_LATE = [("norm_mix_w", (1, 1024)), ("meta", (N_META, 1024)), ("loss", (1, 1))]


def _pack_small(grads, layout):
    rows = []
    for nm, shape in layout:
        flat = grads[nm].reshape(-1)
        rows.append(jnp.pad(flat, (0, (-flat.size) % LANES)).reshape(-1, LANES))
    packed = jnp.concatenate(rows, axis=0)
    return jnp.pad(packed, ((0, (-packed.shape[0]) % 8), (0, 0)))


def _unpack_small(packed, layout):
    out, r0 = {}, 0
    for nm, shape in layout:
        size = shape[0] * shape[1]
        nrow = -(-size // LANES)
        out[nm] = packed[r0:r0 + nrow].reshape(-1)[:size].reshape(shape)
        r0 += nrow
    return out


def kernel(x, meta, norm_mix_w, w_in, pool_w, pool_scale, conv_w, conv_b, dt_bias, a_log, d_skip, ssm_norm_w, w_out, norm_ffn_w, w_ff1, w_ff2, norm_f_w, loss_target, m_meta, m_norm_mix_w, m_w_in, m_pool_w, m_pool_scale, m_conv_w, m_conv_b, m_dt_bias, m_a_log, m_d_skip, m_ssm_norm_w, m_w_out, m_norm_ffn_w, m_w_ff1, m_w_ff2, m_norm_f_w, v_meta, v_norm_mix_w, v_w_in, v_pool_w, v_pool_scale, v_conv_w, v_conv_b, v_dt_bias, v_a_log, v_d_skip, v_ssm_norm_w, v_w_out, v_norm_ffn_w, v_w_ff1, v_w_ff2, v_norm_f_w):
    bsz, seq, d = x.shape
    t = seq + CHUNK
    n = bsz * t
    chip = 2 * lax.axis_index("x") + lax.axis_index("y")
    d_in = w_in.shape[2] * 4

    g_conv, g_meta = _comm(_Gather([conv_w[0], meta]), name="gather_small")
    convw = g_conv.transpose(1, 0, 2).reshape(CONV_W, D_XBC)
    meta_full = g_meta.transpose(1, 0, 2).reshape(N_META, d)
    (h0, hn1), (g_in,) = _embed_norm(x, meta_full, norm_mix_w, name="embed_norm",
                                     rider=_Gather([w_in[0].T.astype(BF16)]))
    h0f, hn1 = h0.reshape(n, d), hn1.reshape(n, d)
    late_weights = _Gather([w_out[0].astype(BF16), w_ff1[0].astype(BF16), w_ff2[0].astype(BF16)])
    win = g_in.reshape(d_in, d)
    wu, wz = win[:D_POOL], win[D_POOL:D_POOL + D_SSM]
    wx = win[D_POOL + D_SSM:D_POOL + D_SSM + D_XBC]
    wdt = jnp.pad(win[D_POOL + D_SSM + D_XBC:].reshape(N_GROUPS, HPG, d),
                  ((0, 0), (0, LANES - HPG), (0, 0))).reshape(D_DT, d)
    dtb, alog = _pad_heads(dt_bias), _pad_heads(a_log)
    dskip = jnp.repeat(d_skip, HEAD_DIM, axis=1)
    poolw = pool_w[0]

    u, z, dtr, dt_, acs_, tr_ = _proj_uz_dt(hn1, wu, wz, wdt, dtb, alog, t // CHUNK, name="proj_uzdt")
    xbc, xc = _proj_conv(hn1, wx, convw, conv_b, name="proj_xbc")
    ypool = _pool_fwd(u.reshape(bsz, t, D_POOL), poolw, pool_scale, name="pool_fwd")
    xbc3 = xbc.reshape(bsz, t, D_XBC)
    xc = xc.reshape(bsz, t, D_XBC)
    z3, dtr3 = z.reshape(bsz, t, D_SSM), dtr.reshape(bsz, t, D_DT)
    dt3, acs3 = dt_.reshape(bsz, t, D_DT), acs_.reshape(bsz, t, D_DT)
    tr3 = tr_.reshape(bsz, t // CHUNK, N_GROUPS, 16, LANES)
    (yn, ypre, sprev), (g_out, g_ff1, g_ff2) = _ssd_fwd(xc, dt3, acs3, tr3, z3, dskip, ssm_norm_w, name="ssd_fwd",
                                                        rider=late_weights)
    wo = g_out.reshape(D_POOL + D_SSM, d)
    wo_p, wo_s = wo[:D_POOL], wo[D_POOL:]
    w1 = g_ff1
    w2 = g_ff2.reshape(D_FF, d)
    ypool_f, yn_f = ypool.reshape(n, D_POOL), yn.reshape(n, D_SSM)
    add = lambda r, e: r + e
    h1, hn2 = _mm([ypool_f, yn_f], [wo_p, wo_s], name="out_proj", post=add, extras=(h0f,), norm_w=norm_ffn_w)
    act = _mm(hn2, w1, name="ff1", out_dtype=BF16)
    relu2 = lambda a: jnp.square(jnp.maximum(a, 0))
    dh2f, dh2bf, loss_acc, d_norm_f = _ff2_loss(act, w2, h1, loss_target, norm_f_w.reshape(1, d), t // CHUNK, name="ff2")

    dact = _mm(dh2bf, w2, name="ff2_bwd", nt=True, post=lambda r, a: r * (2.0 * jnp.maximum(a, 0).astype(F32)),
               extras=(act,), out_dtype=BF16)
    d_w2 = _mm_tn(act, dh2bf, name="ff2_dw", tk=2048, tn=1024, pre=relu2)
    d_w1 = _mm_tn(hn2, dact, name="ff1_dw", tk=1024, tn=2048, slab=D_FF // 4)
    dh1, dh1b, d_norm_ffn = _mm_rms_bwd(dact, w1, h1, norm_ffn_w, dh2f, name="ff1_bwd")
    dypool, dyn = _mm_fanout(dh1b, [wo_p, wo_s], name="out_proj_bwd")
    d_wo = _mm_tn_cat([ypool_f, yn_f], dh1b, name="out_proj_dw")
    big_late = [d_wo.reshape(4, (D_POOL + D_SSM) // 4, d),
                d_w1, d_w2.reshape(4, D_FF // 4, d)]
    (dz, dxs, dbm, dcm, ddtr, d_nw, d_heads), landed_late = _ssd_bwd(
        xc, dtr3, dt3, acs3, tr3, z3, ypre, sprev, dyn.reshape(bsz, t, D_SSM), dtb, alog, dskip, ssm_norm_w, name="ssd_bwd",
        rider=_Exchange(big_late))
    mine_late = [_chip_sum(l, name=f"chip_sum_{i + 1}") for i, l in enumerate(landed_late)]
    (dxbc, d_convwb), theirs_late = _conv_bwd(xbc3, dxs, dbm, dcm, convw, conv_b, name="conv_bwd",
                                               rider=_Swap(mine_late))
    du, d_poolw, d_poolsc = _pool_bwd(u.reshape(bsz, t, D_POOL), dypool.reshape(bsz, t, D_POOL), poolw, pool_scale,
                                      name="pool_bwd")
    duf, dzf, dxbcf, ddtrf = du.reshape(n, D_POOL), dz.reshape(n, D_SSM), dxbc.reshape(n, D_XBC), ddtr.reshape(n, D_DT)
    heads = jnp.sum(d_heads, axis=0)
    small_early = _pack_small({
        "pool_w": d_poolw, "pool_scale": d_poolsc,
        "conv_w": jnp.sum(d_convwb[:, :CONV_W], axis=0), "conv_b": jnp.sum(d_convwb[:, CONV_W:CONV_W + 1], axis=0),
        "dt_bias": _unpad_heads(heads[:, 2]), "a_log": _unpad_heads(heads[:, 1]), "d_skip": _unpad_heads(heads[:, 0]),
        "ssm_norm_w": jnp.sum(d_nw, axis=0), "norm_ffn_w": d_norm_ffn, "norm_f_w": d_norm_f}, _SMALL_EARLY)
    d_wuzdt = _mm_tn_cat([duf, dzf, ddtrf], hn1, name="proj_uzdt_dw")
    d_wx, (early_all,) = _mm_tn(dxbcf, hn1, name="proj_xbc_dw", tk=1280, tn=1024, rider=_Exchange([], small_early))
    d_wdt = d_wuzdt[D_POOL + D_SSM:].reshape(N_GROUPS, LANES, d)[:, :HPG].reshape(N_HEADS, d)
    d_win = jnp.concatenate([d_wuzdt[:D_POOL + D_SSM], d_wx, d_wdt], axis=0)
    big_in = d_win.reshape(4, d_in // 4, d)
    dhn1, (landed_in,) = _mm([duf, dzf, dxbcf, ddtrf], [wu, wz, wx, wdt], name="proj_bwd",
                             rider=_Exchange([big_in]))
    grad_x, d_head_rows, d_norm_mix = _input_grad(
        dhn1.reshape(bsz, t, d), h0, norm_mix_w, dh1.reshape(bsz, t, d), seq, name="input_grad")

    small_late = _pack_small({"norm_mix_w": d_norm_mix, "meta": jnp.sum(d_head_rows[:, PAD:], axis=0),
                              "loss": loss_acc[0:1, 0:1]}, _SMALL_LATE)
    (late_all,) = _comm(_Exchange([], small_late), name="exchange_small")
    mine_in = _chip_sum(landed_in, name="chip_sum_0")
    (theirs_in,) = _comm(_Swap([mine_in]), name="swap_cores")
    mine, theirs = [mine_in] + mine_late, [theirs_in] + list(theirs_late)
    gsmall = {**_unpack_small(_device_sum(early_all, name="device_sum_early"), _SMALL_EARLY),
              **_unpack_small(_device_sum(late_all, name="device_sum_late"), _SMALL_LATE)}
    gsmall["conv_w"] = lax.dynamic_slice_in_dim(gsmall["conv_w"], chip * (D_XBC // 4), D_XBC // 4, axis=1)
    gsmall["meta"] = lax.dynamic_slice_in_dim(gsmall["meta"], chip * (d // 4), d // 4, axis=1)
    loss = gsmall["loss"][0, 0]

    given = dict(meta=(meta, m_meta, v_meta), norm_mix_w=(norm_mix_w, m_norm_mix_w, v_norm_mix_w),
                 w_in=(w_in, m_w_in, v_w_in), pool_w=(pool_w, m_pool_w, v_pool_w),
                 pool_scale=(pool_scale, m_pool_scale, v_pool_scale), conv_w=(conv_w, m_conv_w, v_conv_w),
                 conv_b=(conv_b, m_conv_b, v_conv_b), dt_bias=(dt_bias, m_dt_bias, v_dt_bias),
                 a_log=(a_log, m_a_log, v_a_log), d_skip=(d_skip, m_d_skip, v_d_skip),
                 ssm_norm_w=(ssm_norm_w, m_ssm_norm_w, v_ssm_norm_w), w_out=(w_out, m_w_out, v_w_out),
                 norm_ffn_w=(norm_ffn_w, m_norm_ffn_w, v_norm_ffn_w), w_ff1=(w_ff1, m_w_ff1, v_w_ff1),
                 w_ff2=(w_ff2, m_w_ff2, v_w_ff2), norm_f_w=(norm_f_w, m_norm_f_w, v_norm_f_w))
    big_names = ["w_in", "w_out", "w_ff1", "w_ff2"]
    results = {}
    for nm, (w, m, v) in given.items():
        if nm in big_names:
            i = big_names.index(nm)
            parts, shape2 = (mine[i], theirs[i]), mine[i].shape
        else:
            parts, shape2 = (gsmall[nm],), gsmall[nm].shape
        if nm == "w_in":
            outs = _adamw(w[0].T, parts, m[0].T, v[0].T, name=f"adamw_{nm}")
            results[nm] = [o.T[None] for o in outs]
        else:
            outs = _adamw(w.reshape(shape2), parts, m.reshape(shape2), v.reshape(shape2), name=f"adamw_{nm}")
            results[nm] = [o.reshape(w.shape) for o in outs]
    order = list(given)
    return (loss, grad_x, *[results[nm][0] for nm in order], *[results[nm][1] for nm in order],
            *[results[nm][2] for nm in order], *[results[nm][3] for nm in order])
```

```python
import jax
import jax.numpy as jnp
from jax import lax
from jax.experimental import pallas as pl
from jax.experimental.pallas import tpu as pltpu

F32 = jnp.float32
BF16 = jnp.bfloat16
MESH = pl.DeviceIdType.MESH
ANY = pl.BlockSpec(memory_space=pl.ANY)

D_MODEL = 1024
N_META = 16
CHUNK = 128
PAD = CHUNK - N_META
POOL_WINDOWS = (2, 4, 8, 16)
D_POOL = 512
POOL_GROUP = 128
D_SSM = 1536
N_HEADS = 24
N_GROUPS = 4
HPG = 6
HEAD_DIM = 64
D_STATE = 128
GW = HPG * HEAD_DIM
D_XBC = D_SSM + 2 * N_GROUPS * D_STATE
D_DT = N_GROUPS * 128
D_FF = 4096
CONV_W = 4
EPS = 1e-5
LANES = 128
VMEM_LIMIT = 56 * 1024 * 1024

ADAM_LR, ADAM_B1, ADAM_B2, ADAM_EPS, ADAM_WD, ADAM_STEP = 0.001, 0.9, 0.999, 1e-08, 0.01, 10


def _params(*sem):
    return pltpu.CompilerParams(dimension_semantics=sem, vmem_limit_bytes=VMEM_LIMIT)


def _pick(n, cands):
    for c in cands:
        if n % c == 0:
            return c
    raise ValueError(f"no block size for {n}")


def _dot(a, b):
    return jnp.dot(a.astype(BF16), b.astype(BF16), preferred_element_type=F32)


def _dot_nt(a, b):
    return lax.dot_general(a.astype(BF16), b.astype(BF16), (((1,), (1,)), ((), ())), preferred_element_type=F32)


def _dot_tn(a, b):
    return lax.dot_general(a.astype(BF16), b.astype(BF16), (((0,), (0,)), ((), ())), preferred_element_type=F32)


def _dot_exact(mask, x, terms=3):
    m = mask.astype(BF16)
    dot = lambda t: jnp.dot(m, t, preferred_element_type=F32)
    hi = x.astype(BF16)
    r1 = x - hi.astype(F32)
    mid = r1.astype(BF16)
    if terms == 2:
        return dot(hi) + dot(mid)
    lo = (r1 - mid.astype(F32)).astype(BF16)
    return dot(hi) + dot(mid) + dot(lo)


def _sigmoid(x):
    return 1.0 / (1.0 + jnp.exp(-x))


def _softplus(x):
    return jnp.maximum(x, 0.0) + jnp.log1p(jnp.exp(-jnp.abs(x)))


def _sum_all(x):
    return jnp.sum(jnp.sum(x, axis=1, keepdims=True), axis=0, keepdims=True)


ROW_TILES = (1056, 768, 704, 512, 384, 256, 128)
TILE_BUDGET = 28 * 1024 * 1024


def _row_tile(n, bytes_per_row, fixed_bytes, budget=TILE_BUDGET):
    for tm in ROW_TILES:
        if n % tm == 0 and 2 * (tm * bytes_per_row + fixed_bytes) <= budget:
            return tm
    raise ValueError(f"no row tile for {n}")


WIDE_BUDGET = 38 * 1024 * 1024


def _mm(a, w, *, name, tn=512, nt=False, pre=None, post=None, extras=(), out_dtype=F32, norm_w=None, rider=None):
    assert norm_w is None or (rider is None and out_dtype == F32)
    a_list = list(a) if isinstance(a, (list, tuple)) else [a]
    w_list = list(w) if isinstance(w, (list, tuple)) else [w]
    n_a, n_ex = len(a_list), len(extras)
    n = a_list[0].shape[0]
    shard = w_list[0].shape[2] if w_list[0].ndim == 3 else None
    assert shard is None or (not nt and n_a == 1 and shard % tn == 0)
    m = w_list[0].shape[0] * shard if shard else w_list[0].shape[0] if nt else w_list[0].shape[1]
    tn = min(tn, m)
    size = lambda dt: jnp.dtype(dt).itemsize
    per_row = (sum(x.shape[1] * size(x.dtype) for x in a_list) + m * size(out_dtype)
               + sum(m * size(e.dtype) for e in extras) + (2 * m if norm_w is not None else 0))
    tm = _row_tile(n, per_row, sum(x.size * size(x.dtype) for x in w_list) // 2, WIDE_BUDGET)
    n_norm = 0 if norm_w is None else 1

    def body(*refs):
        a_refs, w_refs, ex_refs = refs[:n_a], refs[n_a:2 * n_a], refs[2 * n_a:2 * n_a + n_ex]
        o_ref = refs[2 * n_a + n_ex + n_norm]
        avs = [(a_ref[...] if pre is None else pre(a_ref[...])).astype(BF16) for a_ref in a_refs]
        for c0 in range(0, m, tn):
            r = None
            for av, w_ref in zip(avs, w_refs):
                if shard:
                    term = _dot(av, w_ref[c0 // shard, :, c0 % shard:c0 % shard + tn])
                else:
                    term = _dot_nt(av, w_ref[c0:c0 + tn, :]) if nt else _dot(av, w_ref[:, c0:c0 + tn])
                r = term if r is None else r + term
            if post is not None:
                r = post(r, *[e[:, c0:c0 + tn] for e in ex_refs])
            o_ref[:, c0:c0 + tn] = r.astype(out_dtype)
        if n_norm:
            x = o_ref[...]
            scale = lax.rsqrt(jnp.mean(x * x, axis=-1, keepdims=True) + EPS)
            refs[2 * n_a + n_ex + 2][...] = (x * scale * refs[2 * n_a + n_ex][...]).astype(BF16)

    a_specs = [pl.BlockSpec((tm, x.shape[1]), lambda i: (i, 0)) for x in a_list]
    w_specs = [pl.BlockSpec(x.shape, lambda i, nd=x.ndim: (0,) * nd, pipeline_mode=pl.Buffered(1)) for x in w_list]
    blk = pl.BlockSpec((tm, m), lambda i: (i, 0))
    vec = [pl.BlockSpec((1, m), lambda i: (0, 0))] * n_norm
    grid = (n // tm,)
    ride = _Ride(rider, body, 2 * n_a + n_ex + n_norm, 1 + n_norm, 0, grid)
    outs = pl.pallas_call(
        ride.body, name=name, grid=grid,
        in_specs=a_specs + w_specs + [blk] * n_ex + vec + ride.in_specs,
        out_specs=[blk] * (1 + n_norm) + ride.out_specs,
        out_shape=[jax.ShapeDtypeStruct((n, m), out_dtype)] + [jax.ShapeDtypeStruct((n, m), BF16)] * n_norm + ride.out_shape,
        scratch_shapes=ride.scratch, compiler_params=_params(*ride.semantics(("parallel",))),
    )(*a_list, *w_list, *extras, *([norm_w] * n_norm), *ride.args)
    if n_norm:
        return outs[0], outs[1]
    return (outs[0], outs[1:]) if rider else outs[0]


def _mm_fanout(a, ws, *, name, tn=512):
    n, k = a.shape
    ms = [w.shape[0] for w in ws]
    tm = _row_tile(n, k * 2 + 4 * sum(ms), sum(w.size for w in ws), WIDE_BUDGET)
    n_w = len(ws)

    def body(a_ref, *refs):
        av = a_ref[...]
        for w_ref, o_ref, m in zip(refs[:n_w], refs[n_w:], ms):
            for c0 in range(0, m, tn):
                o_ref[:, c0:c0 + tn] = _dot_nt(av, w_ref[c0:c0 + tn, :])

    return pl.pallas_call(
        body, name=name, grid=(n // tm,),
        in_specs=[pl.BlockSpec((tm, k), lambda i: (i, 0))]
        + [pl.BlockSpec(w.shape, lambda i: (0, 0), pipeline_mode=pl.Buffered(1)) for w in ws],
        out_specs=[pl.BlockSpec((tm, m), lambda i: (i, 0)) for m in ms],
        out_shape=[jax.ShapeDtypeStruct((n, m), F32) for m in ms],
        compiler_params=_params("parallel"),
    )(a, *ws)


def _mm_tn(a, g, *, name, tk, tn, pre=None, slab=None, rider=None):
    n, k = a.shape
    m = g.shape[1]
    tk, tn = min(tk, k), min(tn, m)
    tm = _row_tile(n, tk * jnp.dtype(a.dtype).itemsize + tn * jnp.dtype(g.dtype).itemsize, tk * tn * 4)
    steps = n // tm

    def body(a_ref, g_ref, o_ref, acc_ref):
        r = pl.program_id(2)

        @pl.when(r == 0)
        def _():
            acc_ref[...] = jnp.zeros_like(acc_ref)

        av = a_ref[...]
        if pre is not None:
            av = pre(av)
        if slab:
            for s in range(tn // slab):
                acc_ref[s] += _dot_tn(av, g_ref[:, s * slab:(s + 1) * slab])
        else:
            acc_ref[...] += _dot_tn(av, g_ref[...])

        @pl.when(r == steps - 1)
        def _():
            o_ref[...] = acc_ref[...].astype(BF16)

    if slab:
        block, out_spec = (tn // slab, tk, slab), pl.BlockSpec((tn // slab, tk, slab), lambda i, j, r: (j, i, 0))
        out_shape = jax.ShapeDtypeStruct((m // slab, k, slab), BF16)
    else:
        block, out_spec = (tk, tn), pl.BlockSpec((tk, tn), lambda i, j, r: (i, j))
        out_shape = jax.ShapeDtypeStruct((k, m), BF16)
    grid = (k // tk, m // tn, steps)
    ride = _Ride(rider, body, 2, 1, 1, grid)
    outs = pl.pallas_call(
        ride.body, name=name, grid=grid,
        in_specs=[pl.BlockSpec((tm, tk), lambda i, j, r: (r, i)), pl.BlockSpec((tm, tn), lambda i, j, r: (r, j))]
        + ride.in_specs,
        out_specs=[out_spec] + ride.out_specs, out_shape=[out_shape] + ride.out_shape,
        scratch_shapes=[pltpu.VMEM(block, F32)] + ride.scratch,
        compiler_params=_params(*ride.semantics(("parallel", "parallel", "arbitrary"))),
    )(a, g, *ride.args)
    return (outs[0], outs[1:]) if rider else outs[0]


def _mm_tn_cat(a_list, g, *, name, budget=TILE_BUDGET):
    n, m = g.shape
    ks = [a.shape[1] for a in a_list]
    size = lambda x: jnp.dtype(x.dtype).itemsize
    tm = _row_tile(n, sum(a.shape[1] * size(a) for a in a_list) + m * size(g), sum(ks) * m * 4, budget)
    steps, n_a = n // tm, len(a_list)

    def body(*refs):
        g_ref, o_ref, acc_ref = refs[n_a], refs[n_a + 1], refs[n_a + 2]
        r = pl.program_id(0)

        @pl.when(r == 0)
        def _():
            acc_ref[...] = jnp.zeros_like(acc_ref)

        gv, k0 = g_ref[...], 0
        for a_ref, k in zip(refs[:n_a], ks):
            acc_ref[k0:k0 + k, :] += _dot_tn(a_ref[...], gv)
            k0 += k

        @pl.when(r == steps - 1)
        def _():
            o_ref[...] = acc_ref[...].astype(BF16)

    return pl.pallas_call(
        body, name=name, grid=(steps,),
        in_specs=[pl.BlockSpec((tm, k), lambda r: (r, 0)) for k in ks] + [pl.BlockSpec((tm, m), lambda r: (r, 0))],
        out_specs=pl.BlockSpec((sum(ks), m), lambda r: (0, 0)),
        out_shape=jax.ShapeDtypeStruct((sum(ks), m), BF16),
        scratch_shapes=[pltpu.VMEM((sum(ks), m), F32)],
        compiler_params=_params("arbitrary"),
    )(*a_list, g)


def _mm_rms_bwd(a, w, h, w_norm, dres, *, name):
    n, k = a.shape
    d = h.shape[1]
    slabs, _, ks = w.shape
    tm = _row_tile(n, k * jnp.dtype(a.dtype).itemsize + d * (4 + 4 + 4 + 2), d * k, WIDE_BUDGET)

    def body(a_ref, w_ref, h_ref, wn_ref, dres_ref, dx_ref, dxb_ref, dw_ref):
        @pl.when(pl.program_id(0) == 0)
        def _():
            dw_ref[...] = jnp.zeros_like(dw_ref)

        dyv = None
        for s in range(slabs):
            part = _dot_nt(a_ref[:, s * ks:(s + 1) * ks], w_ref[s])
            dyv = part if dyv is None else dyv + part
        x = h_ref[...]
        r = lax.rsqrt(jnp.mean(x * x, axis=-1, keepdims=True) + EPS)
        g = dyv * wn_ref[...]
        dx = r * (g - x * (r * r) * jnp.mean(g * x, axis=-1, keepdims=True)) + dres_ref[...]
        dx_ref[...] = dx
        dxb_ref[...] = dx.astype(BF16)
        dw_ref[...] += jnp.sum(dyv * x * r, axis=0, keepdims=True)

    row = pl.BlockSpec((tm, d), lambda i: (i, 0))
    vec = pl.BlockSpec((1, d), lambda i: (0, 0))
    return pl.pallas_call(
        body, name=name, grid=(n // tm,),
        in_specs=[pl.BlockSpec((tm, k), lambda i: (i, 0)),
                  pl.BlockSpec(w.shape, lambda i: (0, 0, 0), pipeline_mode=pl.Buffered(1)), row, vec, row],
        out_specs=[row, row, vec],
        out_shape=[jax.ShapeDtypeStruct((n, d), F32), jax.ShapeDtypeStruct((n, d), BF16), jax.ShapeDtypeStruct((1, d), F32)],
        compiler_params=_params("arbitrary"),
    )(a, w, h, w_norm, dres)


def _embed_norm(x, meta, w, *, name, rider=None):
    bsz, seq, d = x.shape
    t = seq + CHUNK
    nc = t // CHUNK

    def body(x_ref, meta_ref, w_ref, h_ref, hn_ref):
        j = pl.program_id(0)
        first = jnp.concatenate([jnp.zeros((PAD, d), F32), meta_ref[...]], axis=0)
        for e in range(bsz):
            h = jnp.where(j == 0, first, x_ref[e])
            r = lax.rsqrt(jnp.mean(h * h, axis=-1, keepdims=True) + EPS)
            h_ref[e] = h
            hn_ref[e] = (h * r * w_ref[...]).astype(BF16)

    row = pl.BlockSpec((bsz, CHUNK, d), lambda j: (0, j, 0))
    grid = (nc,)
    ride = _Ride(rider, body, 3, 2, 0, grid)
    outs = pl.pallas_call(
        ride.body, name=name, grid=grid,
        in_specs=[pl.BlockSpec((bsz, CHUNK, d), lambda j: (0, jnp.maximum(j - 1, 0), 0)),
                  pl.BlockSpec((N_META, d), lambda j: (0, 0)), pl.BlockSpec((1, d), lambda j: (0, 0))] + ride.in_specs,
        out_specs=[row, row] + ride.out_specs,
        out_shape=[jax.ShapeDtypeStruct((bsz, t, d), F32), jax.ShapeDtypeStruct((bsz, t, d), BF16)] + ride.out_shape,
        scratch_shapes=ride.scratch, compiler_params=_params(*ride.semantics(("parallel",))),
    )(x, meta, w, *ride.args)
    return outs[:2], outs[2:]


def _ff2_loss(act, w2, h1, target, w, nc, *, name):
    n, f = act.shape
    d = h1.shape[1]
    tm = _pick(n, (768, 384, 128))
    per_tile = tm // CHUNK

    def body(act_ref, w2_ref, h1_ref, tgt_ref, w_ref, dh_ref, dhb_ref, loss_ref, dw_ref, tbuf_ref, sems):
        i = pl.program_id(0)

        @pl.when(i == 0)
        def _():
            loss_ref[...] = jnp.zeros_like(loss_ref)
            dw_ref[...] = jnp.zeros_like(dw_ref)

        chunks = [i * per_tile + cc for cc in range(per_tile)]
        fetch = [pltpu.make_async_copy(
            tgt_ref.at[c // nc, pl.ds(pl.multiple_of(jnp.maximum(c % nc - 1, 0) * CHUNK, CHUNK), CHUNK)],
            tbuf_ref.at[pl.ds(cc * CHUNK, CHUNK)], sems.at[cc]) for cc, c in enumerate(chunks)]
        for cc, c in enumerate(chunks):
            @pl.when(c % nc > 0)
            def _(cc=cc):
                fetch[cc].start()

            @pl.when(c % nc == 0)
            def _(cc=cc):
                tbuf_ref[pl.ds(cc * CHUNK, CHUNK), :] = jnp.zeros((CHUNK, d), F32)

        av = jnp.square(jnp.maximum(act_ref[...], 0)).astype(BF16)
        for c0 in range(0, d, 512):
            dh_ref[:, c0:c0 + 512] = h1_ref[:, c0:c0 + 512] + _dot(av, w2_ref[:, c0:c0 + 512])
        wv = w_ref[...]
        for cc, c in enumerate(chunks):
            @pl.when(c % nc > 0)
            def _(cc=cc):
                fetch[cc].wait()

            rows = pl.ds(cc * CHUNK, CHUNK)
            x = dh_ref[rows, :]
            r = lax.rsqrt(jnp.mean(x * x, axis=-1, keepdims=True) + EPS)
            diff = jnp.where(c % nc > 0, x * r * wv - tbuf_ref[rows, :], 0.0)
            loss_ref[...] += _sum_all(diff * diff) * (0.5 / d)
            dy = diff * (1.0 / d)
            g = dy * wv
            dh = r * (g - x * (r * r) * jnp.mean(g * x, axis=-1, keepdims=True))
            dh_ref[rows, :] = dh
            dhb_ref[rows, :] = dh.astype(BF16)
            dw_ref[...] += jnp.sum(dy * x * r, axis=0, keepdims=True)

    row = lambda width: pl.BlockSpec((tm, width), lambda i: (i, 0))
    return pl.pallas_call(
        body, name=name, grid=(n // tm,),
        in_specs=[row(f), pl.BlockSpec(w2.shape, lambda i: (0, 0), pipeline_mode=pl.Buffered(1)), row(d), ANY,
                  pl.BlockSpec((1, d), lambda i: (0, 0))],
        out_specs=[row(d), row(d), pl.BlockSpec((8, LANES), lambda i: (0, 0)), pl.BlockSpec((1, d), lambda i: (0, 0))],
        out_shape=[jax.ShapeDtypeStruct((n, d), F32), jax.ShapeDtypeStruct((n, d), BF16),
                   jax.ShapeDtypeStruct((8, LANES), F32), jax.ShapeDtypeStruct((1, d), F32)],
        scratch_shapes=[pltpu.VMEM((tm, d), F32), pltpu.SemaphoreType.DMA((per_tile,))],
        compiler_params=_params("arbitrary"),
    )(act, w2, h1, target, w)


def _pool_masks(j, transposed):
    r = lax.broadcasted_iota(jnp.int32, (CHUNK, 2 * CHUNK), 0)
    c = lax.broadcasted_iota(jnp.int32, (CHUNK, 2 * CHUNK), 1)
    masks = []
    for w in POOL_WINDOWS:
        if transposed:
            m = (c >= r) & (c < r + w)
        else:
            s = c - CHUNK
            m = (s <= r) & (s > r - w) & (s + j * CHUNK >= 0)
        masks.append(m.astype(F32))
    return masks


POOL_TERMS = 2


def _pool_count(t_global, w):
    return jnp.clip(t_global - PAD + 1, 1, w).astype(F32)


def _pool_fwd(u, pool_w, pool_scale, *, name):
    bsz, t, _ = u.shape
    nc = t // CHUNK

    def body(prev_ref, cur_ref, pw_ref, sc_ref, o_ref):
        j = pl.program_id(0)
        masks = _pool_masks(j, False)
        tg = j * CHUNK + lax.broadcasted_iota(jnp.int32, (CHUNK, 1), 0)
        count = [_pool_count(tg, w) for w in POOL_WINDOWS]
        units = [(e, gi) for e in range(bsz) for gi in range(len(POOL_WINDOWS))]
        sl = lambda gi: pl.ds(gi * POOL_GROUP, POOL_GROUP)
        cur = {(e, gi): cur_ref[e, :, sl(gi)] for e, gi in units}
        both = {(e, gi): jnp.concatenate([prev_ref[e, :, sl(gi)], cur[e, gi]], axis=0) for e, gi in units}
        win = {(e, gi): _dot_exact(masks[gi], both[e, gi], POOL_TERMS) for e, gi in units}
        pooled = {(e, gi): win[e, gi] / count[gi] - cur[e, gi] for e, gi in units}
        mixed = {(e, gi): _dot(pooled[e, gi], pw_ref[gi]) for e, gi in units}
        for e, gi in units:
            o_ref[e, :, sl(gi)] = (mixed[e, gi] * sc_ref[:, sl(gi)]).astype(BF16)

    blk = lambda f: pl.BlockSpec((bsz, CHUNK, D_POOL), f)
    return pl.pallas_call(
        body, name=name, grid=(nc,),
        in_specs=[blk(lambda j: (0, jnp.maximum(j - 1, 0), 0)), blk(lambda j: (0, j, 0)),
                  pl.BlockSpec((4, POOL_GROUP, POOL_GROUP), lambda j: (0, 0, 0)),
                  pl.BlockSpec((1, D_POOL), lambda j: (0, 0))],
        out_specs=blk(lambda j: (0, j, 0)), out_shape=jax.ShapeDtypeStruct(u.shape, BF16),
        compiler_params=_params("parallel"),
    )(u, u, pool_w, pool_scale)


def _pool_bwd(u, dyp, pool_w, pool_scale, *, name):
    bsz, t, _ = u.shape
    nc = t // CHUNK

    def body(prev_ref, cur_ref, dy_ref, dyn_ref, pw_ref, sc_ref, du_ref, dpw_ref, dsc_ref):
        j = pl.program_id(0)

        @pl.when(j == 0)
        def _():
            dpw_ref[...] = jnp.zeros_like(dpw_ref)
            dsc_ref[...] = jnp.zeros_like(dsc_ref)

        fwd = _pool_masks(j, False)
        bwd = _pool_masks(j, True)
        tg = j * CHUNK + lax.broadcasted_iota(jnp.int32, (CHUNK, 1), 0)
        count = [_pool_count(tg, w) for w in POOL_WINDOWS]
        count_next = [_pool_count(tg + CHUNK, w) for w in POOL_WINDOWS]
        has_next = j < nc - 1
        groups = range(len(POOL_WINDOWS))
        units = [(e, gi) for e in range(bsz) for gi in groups]
        sl = lambda gi: pl.ds(gi * POOL_GROUP, POOL_GROUP)
        cur = {(e, gi): cur_ref[e, :, sl(gi)] for e, gi in units}
        both = {(e, gi): jnp.concatenate([prev_ref[e, :, sl(gi)], cur[e, gi]], axis=0) for e, gi in units}
        win = {(e, gi): _dot_exact(fwd[gi], both[e, gi], POOL_TERMS) for e, gi in units}
        pooled = {(e, gi): win[e, gi] / count[gi] - cur[e, gi] for e, gi in units}
        dy = {(e, gi): dy_ref[e, :, sl(gi)] for e, gi in units}
        mixed = {(e, gi): _dot(pooled[e, gi], pw_ref[gi]) for e, gi in units}
        dm = {(e, gi): dy[e, gi] * sc_ref[:, sl(gi)] for e, gi in units}
        dm_next = {(e, gi): jnp.where(has_next, dyn_ref[e, :, sl(gi)], 0.0) * sc_ref[:, sl(gi)] for e, gi in units}
        dpw = {(e, gi): _dot_tn(pooled[e, gi], dm[e, gi]) for e, gi in units}
        dpooled = {(e, gi): _dot_nt(dm[e, gi], pw_ref[gi]) for e, gi in units}
        dpooled_next = {(e, gi): _dot_nt(dm_next[e, gi], pw_ref[gi]) for e, gi in units}
        spread = {(e, gi): jnp.concatenate([dpooled[e, gi] / count[gi], dpooled_next[e, gi] / count_next[gi]], axis=0)
                  for e, gi in units}
        back = {(e, gi): _dot_exact(bwd[gi], spread[e, gi], POOL_TERMS) for e, gi in units}
        for e, gi in units:
            du_ref[e, :, sl(gi)] = (back[e, gi] - dpooled[e, gi]).astype(BF16)
        for gi in groups:
            dsc, dw = None, None
            for e in range(bsz):
                term = jnp.sum(dy[e, gi] * mixed[e, gi], axis=0, keepdims=True)
                dsc = term if dsc is None else dsc + term
                dw = dpw[e, gi] if dw is None else dw + dpw[e, gi]
            dsc_ref[:, sl(gi)] += dsc
            dpw_ref[gi] += dw

    blk = lambda f: pl.BlockSpec((bsz, CHUNK, D_POOL), f)
    return pl.pallas_call(
        body, name=name, grid=(nc,),
        in_specs=[blk(lambda j: (0, jnp.maximum(j - 1, 0), 0)), blk(lambda j: (0, j, 0)),
                  blk(lambda j: (0, j, 0)), blk(lambda j: (0, jnp.minimum(j + 1, nc - 1), 0)),
                  pl.BlockSpec((4, POOL_GROUP, POOL_GROUP), lambda j: (0, 0, 0)),
                  pl.BlockSpec((1, D_POOL), lambda j: (0, 0))],
        out_specs=[blk(lambda j: (0, j, 0)), pl.BlockSpec((4, POOL_GROUP, POOL_GROUP), lambda j: (0, 0, 0)),
                   pl.BlockSpec((1, D_POOL), lambda j: (0, 0))],
        out_shape=[jax.ShapeDtypeStruct(u.shape, BF16), jax.ShapeDtypeStruct((4, POOL_GROUP, POOL_GROUP), F32),
                   jax.ShapeDtypeStruct((1, D_POOL), F32)],
        compiler_params=_params("arbitrary"),
    )(u, u, dyp, dyp, pool_w, pool_scale)


CONV_SLAB = 512


def _conv_taps(tail, cur, keep_tail):
    ext = jnp.concatenate([jnp.where(keep_tail, tail, 0.0), cur], axis=0)
    return [(pltpu.roll(ext, CONV_W - 1 - k, 0) if k < CONV_W - 1 else ext)[8:] for k in range(CONV_W)]


def _conv_pre(taps, w_ref, b_ref, sl):
    acc = b_ref[:, sl]
    for k in range(CONV_W):
        acc = acc + w_ref[k:k + 1, sl] * taps[k]
    return acc


def _proj_conv(hn, w, conv_w, conv_b, *, name):
    n, d = hn.shape
    c = w.shape[0]
    assert PAD >= CONV_W - 1
    tm = _row_tile(n, d * 2 + c * (4 + 2), c * d, WIDE_BUDGET)

    def body(hn_ref, w_ref, cw_ref, cb_ref, xbc_ref, xc_ref, tail_ref):
        @pl.when(pl.program_id(0) == 0)
        def _():
            tail_ref[...] = jnp.zeros_like(tail_ref)

        av = hn_ref[...]
        starts = list(range(0, c, CONV_SLAB))

        def project(c0):
            xbc_ref[:, pl.ds(c0, CONV_SLAB)] = _dot_nt(av, w_ref[c0:c0 + CONV_SLAB, :])

        def convolve(c0):
            sl = pl.ds(c0, CONV_SLAB)
            xb = xbc_ref[:, sl]
            pre = _conv_pre(_conv_taps(tail_ref[:, sl], xb, True), cw_ref, cb_ref, sl)
            xc_ref[:, sl] = (pre * _sigmoid(pre)).astype(BF16)
            tail_ref[:, sl] = xb[tm - 8:, :]

        project(starts[0])
        for c0, c_next in zip(starts, starts[1:] + [None]):
            if c_next is not None:
                project(c_next)
            convolve(c0)

    row = lambda width: pl.BlockSpec((tm, width), lambda i: (i, 0))
    return pl.pallas_call(
        body, name=name, grid=(n // tm,),
        in_specs=[row(d), pl.BlockSpec(w.shape, lambda i: (0, 0), pipeline_mode=pl.Buffered(1)),
                  pl.BlockSpec((CONV_W, c), lambda i: (0, 0)), pl.BlockSpec((1, c), lambda i: (0, 0))],
        out_specs=[row(c), row(c)],
        out_shape=[jax.ShapeDtypeStruct((n, c), F32), jax.ShapeDtypeStruct((n, c), BF16)],
        scratch_shapes=[pltpu.VMEM((8, c), F32)],
        compiler_params=_params("arbitrary"),
    )(hn, w, conv_w, conv_b)


def _conv_bwd(xbc, dxs, db, dc, conv_w, conv_b, *, name, rider=None):
    bsz, t, c = xbc.shape
    tile = _pick(t, (3 * CHUNK, CHUNK))
    nc = t // tile
    halo = 16
    rows = tile + halo

    def body(tail_ref, cur_ref, head_ref, dxs_ref, db_ref, dc_ref, dxs_head, db_head, dc_head, w_ref, b_ref,
             dx_ref, dwb_ref):
        j = pl.program_id(1)

        @pl.when(j == 0)
        def _():
            dwb_ref[...] = jnp.zeros_like(dwb_ref)

        has_prev, has_next = j > 0, j < nc - 1
        for c0 in range(0, c, CONV_SLAB):
            sl = pl.ds(c0, CONV_SLAB)
            if c0 < D_SSM:
                dxc, dxc_next = dxs_ref[0, :, sl], dxs_head[0, :, sl]
            elif c0 < D_SSM + D_POOL:
                dxc, dxc_next = db_ref[0], db_head[0]
            else:
                dxc, dxc_next = dc_ref[0], dc_head[0]
            dxc = jnp.concatenate([dxc.astype(F32), jnp.where(has_next, dxc_next.astype(F32), 0.0)], axis=0)
            ext = jnp.concatenate([jnp.where(has_prev, tail_ref[0, :, sl], 0.0), cur_ref[0, :, sl],
                                   jnp.where(has_next, head_ref[0, :, sl], 0.0)], axis=0)
            taps = [(pltpu.roll(ext, CONV_W - 1 - k, 0) if k < CONV_W - 1 else ext)[8:] for k in range(CONV_W)]
            pre = _conv_pre(taps, w_ref, b_ref, sl)
            s = _sigmoid(pre)
            dpre = dxc * (s * (1.0 + pre * (1.0 - s)))
            acc = w_ref[CONV_W - 1:CONV_W, sl] * dpre[:tile]
            for k in range(CONV_W - 1):
                up = CONV_W - 1 - k
                acc = acc + w_ref[k:k + 1, sl] * pltpu.roll(dpre, rows - up, 0)[:tile]
            dx_ref[0, :, sl] = acc.astype(BF16)
            for k in range(CONV_W):
                dwb_ref[0, k:k + 1, sl] += jnp.sum(dpre[:tile] * taps[k][:tile], axis=0, keepdims=True)
            dwb_ref[0, CONV_W:CONV_W + 1, sl] += jnp.sum(dpre[:tile], axis=0, keepdims=True)

    assert CONV_SLAB == D_POOL and D_SSM % CONV_SLAB == 0
    row = lambda width: pl.BlockSpec((1, tile, width), lambda b, j: (b, j, 0))
    nxt = lambda width: pl.BlockSpec(
        (1, halo, width), lambda b, j: (b, jnp.minimum((j + 1) * (tile // halo), t // halo - 1), 0))
    grid = (bsz, nc)
    ride = _Ride(rider, body, 11, 2, 0, grid)
    outs = pl.pallas_call(
        ride.body, name=name, grid=grid,
        in_specs=[pl.BlockSpec((1, 8, c), lambda b, j: (b, jnp.maximum(j * (tile // 8) - 1, 0), 0)), row(c), nxt(c),
                  row(D_SSM), row(D_POOL), row(D_POOL), nxt(D_SSM), nxt(D_POOL), nxt(D_POOL),
                  pl.BlockSpec((CONV_W, c), lambda b, j: (0, 0)), pl.BlockSpec((1, c), lambda b, j: (0, 0))]
        + ride.in_specs,
        out_specs=[row(c), pl.BlockSpec((1, 8, c), lambda b, j: (b, 0, 0))] + ride.out_specs,
        out_shape=[jax.ShapeDtypeStruct(xbc.shape, BF16), jax.ShapeDtypeStruct((bsz, 8, c), F32)] + ride.out_shape,
        scratch_shapes=ride.scratch, compiler_params=_params(*ride.semantics(("parallel", "arbitrary"))),
    )(xbc, xbc, xbc, dxs, db, dc, dxs, db, dc, conv_w, conv_b, *ride.args)
    return outs[:2], outs[2:]


def _dt_valid(j):
    lane = lax.broadcasted_iota(jnp.int32, (CHUNK, LANES), 1)
    row = lax.broadcasted_iota(jnp.int32, (CHUNK, LANES), 0)
    return (lane < HPG) & ((j > 0) | (row >= PAD))


def _proj_uz_dt(hn, wu, wz, wdt, dtb, alog, nc, *, name):
    n, d = hn.shape
    tm = _pick(n, (768, 384, 128))
    per_tile = tm // CHUNK
    widths = (wu.shape[0], wz.shape[0], wdt.shape[0])

    def body(hn_ref, wu_ref, wz_ref, wdt_ref, dtb_ref, alog_ref, u_ref, z_ref, dtr_ref, dt_ref, acs_ref, tr_ref):
        i = pl.program_id(0)
        av = hn_ref[...]
        for w_ref, o_ref, m in zip((wu_ref, wz_ref, wdt_ref), (u_ref, z_ref, dtr_ref), widths):
            for c0 in range(0, m, 512):
                o_ref[:, c0:c0 + 512] = _dot_nt(av, w_ref[c0:c0 + 512, :])
        row = lax.broadcasted_iota(jnp.int32, (CHUNK, LANES), 0)
        lane = lax.broadcasted_iota(jnp.int32, (CHUNK, LANES), 1)
        tril = (row >= lane).astype(F32)
        units = [(cc, g) for cc in range(per_tile) for g in range(N_GROUPS)]
        at = lambda cc, g: (pl.ds(cc * CHUNK, CHUNK), pl.ds(g * LANES, LANES))
        valid = [(lane < HPG) & (((i * per_tile + cc) % nc > 0) | (row >= PAD)) for cc in range(per_tile)]
        dt = {(cc, g): jnp.where(valid[cc], _softplus(dtr_ref[at(cc, g)] + dtb_ref[g]), 0.0) for cc, g in units}
        acs = {(cc, g): _dot_exact(tril, dt[cc, g] * -jnp.exp(alog_ref[g])) for cc, g in units}
        for cc, g in units:
            dt_ref[at(cc, g)] = dt[cc, g]
            acs_ref[at(cc, g)] = acs[cc, g]
            tr_ref[cc, g, 0:8, :] = dt[cc, g].T[0:8]
            tr_ref[cc, g, 8:16, :] = acs[cc, g].T[0:8]

    row_blk = lambda width: pl.BlockSpec((tm, width), lambda i: (i, 0))
    whole = lambda w: pl.BlockSpec(w.shape, lambda i: (0, 0), pipeline_mode=pl.Buffered(1))
    const = pl.BlockSpec((N_GROUPS, 1, LANES), lambda i: (0, 0, 0))
    return pl.pallas_call(
        body, name=name, grid=(n // tm,),
        in_specs=[row_blk(d), whole(wu), whole(wz), whole(wdt), const, const],
        out_specs=[row_blk(widths[0]), row_blk(widths[1])] + [row_blk(D_DT)] * 3
        + [pl.BlockSpec((per_tile, N_GROUPS, 16, LANES), lambda i: (i, 0, 0, 0))],
        out_shape=[jax.ShapeDtypeStruct((n, widths[0]), F32), jax.ShapeDtypeStruct((n, widths[1]), F32)]
        + [jax.ShapeDtypeStruct((n, D_DT), F32)] * 3 + [jax.ShapeDtypeStruct((n // CHUNK, N_GROUPS, 16, LANES), F32)],
        compiler_params=_params("parallel"),
    )(hn, wu, wz, wdt, dtb, alog)


def _ssd_decay(dt, acs, tr):
    lane = lax.broadcasted_iota(jnp.int32, (CHUNK, LANES), 1)
    row = lax.broadcasted_iota(jnp.int32, (CHUNK, LANES), 0)
    return dict(lane=lane, row=row, dt=dt, causal=row >= lane, acs=acs, acs_t=tr[8:16], dt_t=tr[0:8],
                aend=acs[CHUNK - 1:CHUNK, :])


def _ssd_specs(bsz, nc, rev):
    ch = (lambda j: nc - 1 - j) if rev else (lambda j: j)
    return dict(
        xs=pl.BlockSpec((bsz, CHUNK, GW), lambda g, j: (0, ch(j), g)),
        bm=pl.BlockSpec((bsz, CHUNK, D_STATE), lambda g, j: (0, ch(j), D_SSM // D_STATE + g)),
        cm=pl.BlockSpec((bsz, CHUNK, D_STATE), lambda g, j: (0, ch(j), D_SSM // D_STATE + N_GROUPS + g)),
        lane_blk=pl.BlockSpec((bsz, CHUNK, LANES), lambda g, j: (0, ch(j), g)),
        grp_const=pl.BlockSpec((1, 1, LANES), lambda g, j: (g, 0, 0)),
        grp_vec=pl.BlockSpec((1, GW), lambda g, j: (0, g)),
        state=pl.BlockSpec((bsz, 1, D_STATE, GW), lambda g, j: (0, ch(j), 0, g)),
        tr=pl.BlockSpec((bsz, 1, 1, 16, LANES), lambda g, j: (0, ch(j), g, 0, 0)),
    )


def _ssd_fwd(xc, dt, acs, tr, z, dskip, normw, *, name, rider=None):
    bsz, t, _ = xc.shape
    nc = t // CHUNK
    sp = _ssd_specs(bsz, nc, False)

    def body(xs_ref, b_ref, c_ref, dt_ref, acs_ref, tr_ref, z_ref, dsk_ref, nw_ref, yn_ref, y_ref, sp_ref, s_ref):
        j = pl.program_id(1)

        @pl.when(j == 0)
        def _():
            s_ref[...] = jnp.zeros_like(s_ref)

        ex = range(bsz)
        units = [(e, r) for e in ex for r in range(HPG)]
        full = lambda v: jnp.broadcast_to(v, (CHUNK, LANES))
        pair = lambda r: pl.ds((r // 2) * LANES, LANES)
        q = [_ssd_decay(dt_ref[e], acs_ref[e], tr_ref[e, 0, 0]) for e in ex]
        for e in ex:
            sp_ref[e, 0] = s_ref[e]
        bm, cm = [b_ref[e] for e in ex], [c_ref[e] for e in ex]
        cb = [_dot_nt(cm[e], bm[e]) for e in ex]
        low = q[0]["lane"] < HEAD_DIM
        col = {(e, r): full(q[e]["acs"][:, r:r + 1]) for e, r in units}
        aend = {(e, r): q[e]["aend"][:, r:r + 1] for e, r in units}
        decay = {(e, r): jnp.exp(jnp.where(q[e]["causal"], col[e, r] - q[e]["acs_t"][r:r + 1, :], -jnp.inf))
                 for e, r in units}
        mp = {(e, r): cb[e] * decay[e, r] * q[e]["dt_t"][r:r + 1, :] for e, r in units}
        ce = {(e, r): cm[e] * jnp.exp(col[e, r]) for e, r in units}
        bk = {(e, r): bm[e] * (jnp.exp(aend[e, r] - col[e, r]) * full(q[e]["dt"][:, r:r + 1])) for e, r in units}
        xp = {(e, r): xs_ref[e, :, pair(r)] for e, r in units}
        s_old = {(e, r): s_ref[e, :, pair(r)] for e, r in units}
        y_h = {u: _dot(mp[u], xp[u]) + _dot(ce[u], s_old[u]) for u in units}
        s_h = {u: jnp.exp(aend[u]) * s_old[u] + _dot_tn(bk[u], xp[u]) for u in units}
        for e in ex:
            for r in range(0, HPG, 2):
                y_ref[e, :, pair(r)] = jnp.where(low, y_h[e, r], y_h[e, r + 1])
                s_ref[e, :, pair(r)] = jnp.where(low, s_h[e, r], s_h[e, r + 1])
        y = [y_ref[e] + dsk_ref[...] * xs_ref[e] for e in ex]
        zz = [z_ref[e] for e in ex]
        yg = [y[e] * (zz[e] * _sigmoid(zz[e])) for e in ex]
        rstd = [lax.rsqrt(jnp.mean(yg[e] * yg[e], axis=-1, keepdims=True) + EPS) for e in ex]
        for e in ex:
            y_ref[e] = y[e]
            yn_ref[e] = (yg[e] * rstd[e] * nw_ref[...]).astype(BF16)

    grid = (N_GROUPS, nc)
    ride = _Ride(rider, body, 9, 3, 1, grid)
    outs = pl.pallas_call(
        ride.body, name=name, grid=grid,
        in_specs=[sp["xs"], sp["bm"], sp["cm"], sp["lane_blk"], sp["lane_blk"], sp["tr"], sp["xs"],
                  sp["grp_vec"], sp["grp_vec"]] + ride.in_specs,
        out_specs=[sp["xs"], sp["xs"], sp["state"]] + ride.out_specs,
        out_shape=[jax.ShapeDtypeStruct((bsz, t, D_SSM), BF16), jax.ShapeDtypeStruct((bsz, t, D_SSM), F32),
                   jax.ShapeDtypeStruct((bsz, nc, D_STATE, D_SSM), F32)] + ride.out_shape,
        scratch_shapes=[pltpu.VMEM((bsz, D_STATE, GW), F32)] + ride.scratch,
        compiler_params=_params(*ride.semantics(("parallel", "arbitrary"))),
    )(xc, xc, xc, dt, acs, tr, z, dskip, normw, *ride.args)
    return outs[:3], outs[3:]


def _ssd_bwd(xc, dtr, dt, acs, tr, z, ypre, sprev, dyn, dtb, alog, dskip, normw, *, name, rider=None):
    bsz, t, _ = xc.shape
    nc = t // CHUNK
    sp = _ssd_specs(bsz, nc, True)

    def body(xs_ref, b_ref, c_ref, dtr_ref, dt_ref, acs_ref, tr_ref, z_ref, y_ref, sp_ref, dyn_ref, dtb_ref, alog_ref,
             dsk_ref, nw_ref, dz_ref, dxs_ref, db_ref, dc_ref, ddt_ref, dnw_ref, dsm_ref, ds_ref):
        j = pl.program_id(1)

        @pl.when(j == 0)
        def _():
            ds_ref[...] = jnp.zeros_like(ds_ref)
            dnw_ref[...] = jnp.zeros_like(dnw_ref)
            dsm_ref[...] = jnp.zeros_like(dsm_ref)

        ex = range(bsz)
        heads = range(HPG)
        units = [(e, r) for e in ex for r in heads]
        q = [_ssd_decay(dt_ref[e], acs_ref[e], tr_ref[e, 0, 0]) for e in ex]
        a = -jnp.exp(alog_ref[0])
        valid = _dt_valid(nc - 1 - j)
        lane, row = q[0]["lane"], q[0]["row"]
        lane1 = lane[0:1, :]
        nw = nw_ref[...]
        y, zz, dyn = [y_ref[e] for e in ex], [z_ref[e] for e in ex], [dyn_ref[e] for e in ex]
        sz = [_sigmoid(zz[e]) for e in ex]
        sil = [zz[e] * sz[e] for e in ex]
        yg = [y[e] * sil[e] for e in ex]
        rstd = [lax.rsqrt(jnp.mean(yg[e] * yg[e], axis=-1, keepdims=True) + EPS) for e in ex]
        gn = [dyn[e] * nw for e in ex]
        dyg = [rstd[e] * (gn[e] - yg[e] * (rstd[e] * rstd[e]) * jnp.mean(gn[e] * yg[e], axis=-1, keepdims=True))
               for e in ex]
        dy = [dyg[e] * sil[e] for e in ex]
        xs = [xs_ref[e] for e in ex]
        for e in ex:
            dnw_ref[e] += jnp.sum(dyn[e] * yg[e] * rstd[e], axis=0, keepdims=True)
            dz_ref[e] = (dyg[e] * y[e] * (sz[e] * (1.0 + zz[e] * (1.0 - sz[e])))).astype(BF16)
        dskip_cols = [jnp.sum(dy[e] * xs[e], axis=0, keepdims=True) for e in ex]

        bm, cm = [b_ref[e] for e in ex], [c_ref[e] for e in ex]
        cb = [_dot_nt(cm[e], bm[e]) for e in ex]
        zero = jnp.zeros((CHUNK, LANES), F32)
        full = lambda v: jnp.broadcast_to(v, (CHUNK, LANES))
        low = lane < HEAD_DIM
        half = [low if r % 2 == 0 else ~low for r in heads]
        sl = lambda v, r: v[:, (r // 2) * LANES:(r // 2 + 1) * LANES]
        pair = lambda r: pl.ds((r // 2) * LANES, LANES)
        col = {(e, r): full(q[e]["acs"][:, r:r + 1]) for e, r in units}
        dt_col = {(e, r): full(q[e]["dt"][:, r:r + 1]) for e, r in units}
        aend = {(e, r): q[e]["aend"][:, r:r + 1] for e, r in units}
        dt_row = {(e, r): q[e]["dt_t"][r:r + 1, :] for e, r in units}
        decay = {(e, r): jnp.exp(jnp.where(q[e]["causal"], col[e, r] - q[e]["acs_t"][r:r + 1, :], -jnp.inf))
                 for e, r in units}
        ea = {u: jnp.exp(col[u]) for u in units}
        dte = {u: jnp.exp(aend[u] - col[u]) for u in units}
        ed = {u: jnp.exp(aend[u]) for u in units}
        k = {u: dte[u] * dt_col[u] for u in units}
        mp = {(e, r): cb[e] * decay[e, r] * dt_row[e, r] for e, r in units}
        xp = {(e, r): sl(xs[e], r) for e, r in units}
        dym = {(e, r): jnp.where(half[r], sl(dy[e], r), 0.0) for e, r in units}
        s_old = {(e, r): sp_ref[e, 0, :, pair(r)] for e, r in units}
        ds_old = {(e, r): ds_ref[e, :, pair(r)] for e, r in units}
        dsm = {(e, r): jnp.where(half[r], ds_old[e, r], 0.0) for e, r in units}
        gmat = {u: _dot_nt(dym[u], xp[u]) for u in units}
        t1 = {u: _dot_nt(dym[u], s_old[u]) for u in units}
        dbs = {u: _dot_nt(xp[u], dsm[u]) for u in units}
        dx = {(e, r): _dot_tn(mp[e, r], dym[e, r]) + _dot(bm[e] * k[e, r], dsm[e, r]) for e, r in units}
        ds = {(e, r): _dot_tn(cm[e] * ea[e, r], dym[e, r]) for e, r in units}
        gd = {u: gmat[u] * decay[u] for u in units}
        w0 = {(e, r): gd[e, r] * cb[e] for e, r in units}
        cs0 = {u: jnp.sum(w0[u], axis=0, keepdims=True) for u in units}
        rs = {u: jnp.sum(w0[u] * dt_row[u], axis=1, keepdims=True) for u in units}
        qv = {(e, r): jnp.sum(cm[e] * t1[e, r], axis=1, keepdims=True) for e, r in units}
        dk = {(e, r): jnp.sum(bm[e] * dbs[e, r], axis=1, keepdims=True) for e, r in units}
        ddte = {u: dk[u] * dt_col[u] for u in units}
        d_aend = {u: _sum_all(dsm[u] * s_old[u]) * ed[u] + _sum_all(ddte[u][:, 0:1] * dte[u][:, 0:1]) for u in units}
        last_row = row == CHUNK - 1
        dacs_col = {u: rs[u] + qv[u] * ea[u] - ddte[u] * dte[u] + jnp.where(last_row, d_aend[u], 0.0) for u in units}
        triu = (lane >= row).astype(F32)
        for e in ex:
            dcb, dc_acc, db_acc = zero, zero, zero
            dacs, dacs_t, ddt, ddt_t = zero, zero, zero, zero
            dskip_row = jnp.zeros((1, LANES), F32)
            for r in heads:
                u = (e, r)
                dcb = dcb + gd[u] * dt_row[u]
                dc_acc = dc_acc + ea[u] * t1[u]
                db_acc = db_acc + k[u] * dbs[u]
                dacs = jnp.where(lane == r, dacs_col[u], dacs)
                ddt = jnp.where(lane == r, dk[u] * dte[u], ddt)
                dacs_t = jnp.where(row == r, -cs0[u] * dt_row[u], dacs_t)
                ddt_t = jnp.where(row == r, cs0[u], ddt_t)
                dsk = _sum_all(jnp.where(half[r][0:1, :], sl(dskip_cols[e], r), 0.0))
                dskip_row = dskip_row + jnp.where(lane1 == r, dsk, 0.0)
            for r in range(0, HPG, 2):
                dxs_ref[e, :, pair(r)] = (dx[e, r] + dx[e, r + 1] + sl(dy[e], r) * dsk_ref[:, pair(r)]).astype(BF16)
                ed_pair = jnp.where(lane1 < HEAD_DIM, ed[e, r], ed[e, r + 1])
                ds_ref[e, :, pair(r)] = ds[e, r] + ds[e, r + 1] + ed_pair * ds_old[e, r]
            dacs = dacs + dacs_t.T
            ddt = ddt + ddt_t.T
            dda = _dot_exact(triu, dacs)
            ddt = ddt + dda * a
            da = jnp.sum(dda * q[e]["dt"], axis=0, keepdims=True)
            draw = jnp.where(valid, ddt * _sigmoid(dtr_ref[e] + dtb_ref[0]), 0.0)
            ddt_ref[e] = draw.astype(BF16)
            dsm_ref[e, 0, 0:1, :] += dskip_row
            dsm_ref[e, 0, 1:2, :] += da * a
            dsm_ref[e, 0, 2:3, :] += jnp.sum(draw, axis=0, keepdims=True)
            dc_ref[e] = (dc_acc + _dot(dcb, bm[e])).astype(BF16)
            db_ref[e] = (db_acc + _dot_tn(dcb, cm[e])).astype(BF16)

    grp_out = pl.BlockSpec((bsz, CHUNK, D_STATE), lambda g, j: (0, nc - 1 - j, g))
    grid = (N_GROUPS, nc)
    ride = _Ride(rider, body, 15, 7, 1, grid)
    outs = pl.pallas_call(
        ride.body, name=name, grid=grid,
        in_specs=[sp["xs"], sp["bm"], sp["cm"], sp["lane_blk"], sp["lane_blk"], sp["lane_blk"], sp["tr"], sp["xs"],
                  sp["xs"], sp["state"], sp["xs"], sp["grp_const"], sp["grp_const"], sp["grp_vec"], sp["grp_vec"]]
        + ride.in_specs,
        out_specs=[sp["xs"], sp["xs"], grp_out, grp_out, sp["lane_blk"],
                   pl.BlockSpec((bsz, 1, GW), lambda g, j: (0, 0, g)),
                   pl.BlockSpec((bsz, 1, 8, LANES), lambda g, j: (0, g, 0, 0))] + ride.out_specs,
        out_shape=[jax.ShapeDtypeStruct((bsz, t, D_SSM), BF16), jax.ShapeDtypeStruct((bsz, t, D_SSM), BF16),
                   jax.ShapeDtypeStruct((bsz, t, N_GROUPS * D_STATE), BF16),
                   jax.ShapeDtypeStruct((bsz, t, N_GROUPS * D_STATE), BF16),
                   jax.ShapeDtypeStruct((bsz, t, D_DT), BF16), jax.ShapeDtypeStruct((bsz, 1, D_SSM), F32),
                   jax.ShapeDtypeStruct((bsz, N_GROUPS, 8, LANES), F32)] + ride.out_shape,
        scratch_shapes=[pltpu.VMEM((bsz, D_STATE, GW), F32)] + ride.scratch,
        compiler_params=_params(*ride.semantics(("parallel", "arbitrary"))),
    )(xc, xc, xc, dtr, dt, acs, tr, z, ypre, sprev, dyn, dtb, alog, dskip, normw, *ride.args)
    return outs[:7], outs[7:]


def _input_grad(dhn, h0, w, dres, seq, *, name):
    bsz, t, d = h0.shape
    nc = t // CHUNK

    def body(dy_ref, h_ref, w_ref, dres_ref, gx_ref, head_ref, dw_ref):
        j = pl.program_id(0)

        @pl.when(j == 0)
        def _():
            dw_ref[...] = jnp.zeros_like(dw_ref)

        for e in range(bsz):
            x, dyv = h_ref[e], dy_ref[e]
            r = lax.rsqrt(jnp.mean(x * x, axis=-1, keepdims=True) + EPS)
            g = dyv * w_ref[...]
            dx = r * (g - x * (r * r) * jnp.mean(g * x, axis=-1, keepdims=True)) + dres_ref[e]
            dw_ref[...] += jnp.sum(dyv * x * r, axis=0, keepdims=True)
            gx_ref[e] = dx

        @pl.when(j == 0)
        def _():
            head_ref[...] = gx_ref[...]

    row = pl.BlockSpec((bsz, CHUNK, d), lambda j: (0, j, 0))
    return pl.pallas_call(
        body, name=name, grid=(nc,),
        in_specs=[row, row, pl.BlockSpec((1, d), lambda j: (0, 0)), row],
        out_specs=[pl.BlockSpec((bsz, CHUNK, d), lambda j: (0, jnp.maximum(j - 1, 0), 0)),
                   pl.BlockSpec((bsz, CHUNK, d), lambda j: (0, 0, 0)), pl.BlockSpec((1, d), lambda j: (0, 0))],
        out_shape=[jax.ShapeDtypeStruct((bsz, seq, d), F32), jax.ShapeDtypeStruct((bsz, CHUNK, d), F32),
                   jax.ShapeDtypeStruct((1, d), F32)],
        compiler_params=_params("arbitrary"),
    )(dhn, h0, w, dres)


def _remote(src, dst, send_sem, recv_sem, dev):
    return pltpu.make_async_remote_copy(src_ref=src, dst_ref=dst, send_sem=send_sem, recv_sem=recv_sem,
                                        device_id=dev, device_id_type=MESH)


def _position():
    return lax.axis_index("x"), lax.axis_index("y"), lax.axis_index("c")


def _other_chips(pos):
    x, y, _ = pos
    return [(1 - x, y), (x, 1 - y), (1 - x, 1 - y)]


class _Gather:
    def __init__(self, arrs):
        n = len(arrs)
        self.args, self.n_in, self.n_out = list(arrs), n, n
        self.split = [a.ndim == 2 and a.shape[1] % (2 * LANES) == 0 for a in arrs]
        self.out_shape = [jax.ShapeDtypeStruct((4,) + a.shape, a.dtype) for a in arrs]
        self.scratch = [pltpu.SemaphoreType.DMA((3 * n,)), pltpu.SemaphoreType.DMA((3 * n,)),
                        pltpu.SemaphoreType.DMA((n,)), pltpu.SemaphoreType.DMA((3 * n,)),
                        pltpu.SemaphoreType.DMA((3 * n,))]

    def _copies(self, pos, ins, outs, sems):
        send_sems, recv_sems, loc_sems, pass_send_sems, pass_recv_sems = sems
        x, y, c = pos
        me, sibling = 2 * x + y, (x, y, 1 - c)
        local = [pltpu.make_async_copy(ins[i], outs[i].at[me], loc_sems.at[i]) for i in range(self.n_in)]
        sends, recvs, passes, pass_recvs = [], [], [], []
        for i in range(self.n_in):
            half = self.args[i].shape[1] // 2 if self.split[i] else None
            for k, (px, py) in enumerate(_other_chips(pos)):
                them = 2 * px + py
                sems_k = (send_sems.at[3 * i + k], recv_sems.at[3 * i + k], (px, py, c))
                if half is None:
                    sends.append(_remote(ins[i], outs[i].at[me], *sems_k))
                    recvs.append(_remote(ins[i], outs[i].at[them], *sems_k))
                    passes.append(None)
                    continue
                mine = pl.ds(pl.multiple_of(c * half, LANES), half)
                other = pl.ds(pl.multiple_of((1 - c) * half, LANES), half)
                sends.append(_remote(ins[i].at[:, mine], outs[i].at[me, :, mine], *sems_k))
                recvs.append(_remote(ins[i].at[:, mine], outs[i].at[them, :, mine], *sems_k))
                pass_k = (pass_send_sems.at[3 * i + k], pass_recv_sems.at[3 * i + k], sibling)
                passes.append(_remote(outs[i].at[them, :, mine], outs[i].at[them, :, mine], *pass_k))
                pass_recvs.append(_remote(outs[i].at[them, :, other], outs[i].at[them, :, other], *pass_k))
        return local, sends, recvs, passes, pass_recvs

    def start(self, pos, ins, outs, sems):
        local, sends = self._copies(pos, ins, outs, sems)[:2]
        for cp in local + sends:
            cp.start()

    def relay(self, pos, ins, outs, sems):
        _, _, recvs, passes, _ = self._copies(pos, ins, outs, sems)
        for cp, onward in zip(recvs, passes):
            if onward is not None:
                cp.wait_recv()
                onward.start()

    def finish(self, pos, ins, outs, sems):
        local, sends, recvs, passes, pass_recvs = self._copies(pos, ins, outs, sems)
        for cp, onward in zip(recvs, passes):
            if onward is None:
                cp.wait_recv()
        for cp in pass_recvs:
            cp.wait_recv()
        for cp in sends + [p for p in passes if p is not None]:
            cp.wait_send()
        for cp in local:
            cp.wait()


class _Exchange:
    FLIPS = [(fx, fy, fc) for fx in (0, 1) for fy in (0, 1) for fc in (0, 1)][1:]

    def __init__(self, big, small=None):
        n = len(big)
        self.n_big, self.has_small = n, small is not None
        self.args = list(big) + ([small] if self.has_small else [])
        self.n_in = self.n_out = len(self.args)
        self.out_shape = [jax.ShapeDtypeStruct(a.shape, a.dtype) for a in big]
        self.scratch = [pltpu.SemaphoreType.DMA((max(3 * n, 1),)), pltpu.SemaphoreType.DMA((max(3 * n, 1),)),
                        pltpu.SemaphoreType.DMA((n + 1,))]
        if self.has_small:
            self.out_shape.append(jax.ShapeDtypeStruct((8,) + small.shape, small.dtype))
            self.scratch += [pltpu.SemaphoreType.DMA((7,)), pltpu.SemaphoreType.DMA((7,))]

    def _copies(self, pos, ins, outs, sems):
        x, y, c = pos
        me, me8 = 2 * x + y, 4 * x + 2 * y + c
        local, sends, recvs = [], [], []
        for i in range(self.n_big):
            local.append(pltpu.make_async_copy(ins[i].at[me], outs[i].at[me], sems[2].at[i]))
            for k, (px, py) in enumerate(_other_chips(pos)):
                sems_k = (sems[0].at[3 * i + k], sems[1].at[3 * i + k], (px, py, c))
                sends.append(_remote(ins[i].at[2 * px + py], outs[i].at[me], *sems_k))
                recvs.append(_remote(ins[i].at[me], outs[i].at[2 * px + py], *sems_k))
        if self.has_small:
            small, landed = ins[self.n_big], outs[self.n_big]
            local.append(pltpu.make_async_copy(small, landed.at[me8], sems[2].at[self.n_big]))
            for k, (fx, fy, fc) in enumerate(self.FLIPS):
                peer = (x ^ fx, y ^ fy, c ^ fc)
                sems_k = (sems[3].at[k], sems[4].at[k], peer)
                sends.append(_remote(small, landed.at[me8], *sems_k))
                recvs.append(_remote(small, landed.at[4 * peer[0] + 2 * peer[1] + peer[2]], *sems_k))
        return local, sends, recvs, [None] * len(recvs), []

    start = _Gather.start
    relay = _Gather.relay
    finish = _Gather.finish


class _Swap:
    def __init__(self, arrs):
        n = len(arrs)
        self.args, self.n_in, self.n_out = list(arrs), n, n
        self.out_shape = [jax.ShapeDtypeStruct(a.shape, a.dtype) for a in arrs]
        self.scratch = [pltpu.SemaphoreType.DMA((n,)), pltpu.SemaphoreType.DMA((n,))]

    def _copies(self, pos, ins, outs, sems):
        x, y, c = pos
        both = [_remote(ins[i], outs[i], sems[0].at[i], sems[1].at[i], (x, y, 1 - c)) for i in range(self.n_in)]
        return [], both, both, [None] * len(both), []

    start = _Gather.start
    relay = _Gather.relay
    finish = _Gather.finish


def _comm(rider, *, name):
    a, b = rider.n_in, rider.n_in + rider.n_out

    def body(*refs):
        pos = _position()
        rider.start(pos, refs[:a], refs[a:b], refs[b:])
        rider.relay(pos, refs[:a], refs[a:b], refs[b:])
        rider.finish(pos, refs[:a], refs[a:b], refs[b:])

    return pl.pallas_call(body, name=name, in_specs=[ANY] * rider.n_in, out_specs=[ANY] * rider.n_out,
                          out_shape=rider.out_shape, scratch_shapes=rider.scratch)(*rider.args)


class _Ride:
    RELAY_AT = 0.8

    def __init__(self, rider, body, n_in, n_out, n_scratch, grid):
        self.rider = rider
        self.args = rider.args if rider else []
        self.in_specs = [ANY] * rider.n_in if rider else []
        self.out_specs = [ANY] * rider.n_out if rider else []
        self.out_shape = rider.out_shape if rider else []
        self.scratch = rider.scratch if rider else []
        self.body = self._wrap(body, n_in, n_out, n_scratch, grid) if rider else body

    def semantics(self, sem):
        return ("arbitrary",) * len(sem) if self.rider else sem

    def _wrap(self, body, n_in, n_out, n_scratch, grid):
        rider = self.rider
        a = n_in
        b = a + rider.n_in
        c = b + n_out
        d = c + rider.n_out
        e = d + n_scratch

        def wrapped(*refs):
            pos = _position()
            ids = [pl.program_id(i) for i in range(len(grid))]
            step, total = 0, 1
            for i, g in zip(ids, grid):
                step, total = step * g + i, total * g

            @pl.when(step == 0)
            def _():
                rider.start(pos, refs[a:b], refs[c:d], refs[e:])

            body(*refs[:a], *refs[b:c], *refs[d:e])

            @pl.when(step == int(self.RELAY_AT * (total - 1)))
            def _():
                rider.relay(pos, refs[a:b], refs[c:d], refs[e:])

            @pl.when(step == total - 1)
            def _():
                rider.finish(pos, refs[a:b], refs[c:d], refs[e:])

        return wrapped


def _elementwise_tiles(r, c):
    if r % 8 == 0 and r * c > 65536:
        tm = _pick(r, (256, 128, 64, 16, 8))
        return (tm, c), r // tm, lambda i: (i, 0)
    if r % 8 and c % 256 == 0 and r * c > 65536:
        return (r, 256), c // 256, lambda i: (0, i)
    return (r, c), 1, lambda i: (0, 0)


def _chip_sum(landed, *, name):
    _, r, c = landed.shape
    blk, steps, at = _elementwise_tiles(r, c)

    def body(land_ref, o_ref):
        acc = land_ref[0].astype(F32)
        for jchip in range(1, 4):
            acc = acc + land_ref[jchip].astype(F32)
        o_ref[...] = acc

    return pl.pallas_call(
        body, name=name, grid=(steps,), in_specs=[pl.BlockSpec((4,) + blk, lambda i: (0,) + at(i))],
        out_specs=pl.BlockSpec(blk, at), out_shape=jax.ShapeDtypeStruct((r, c), F32),
        compiler_params=_params("parallel"),
    )(landed)


def _device_sum(parts, *, name):
    _, r, c = parts.shape

    def body(p_ref, o_ref):
        acc = p_ref[0]
        for d in range(1, 8):
            acc = acc + p_ref[d]
        o_ref[...] = acc

    return pl.pallas_call(body, name=name, out_shape=jax.ShapeDtypeStruct((r, c), F32))(parts)


def _adamw_math(w, g, m, v):
    m = ADAM_B1 * m + (1.0 - ADAM_B1) * g
    v = ADAM_B2 * v + (1.0 - ADAM_B2) * (g * g)
    m_hat = m / (1.0 - ADAM_B1 ** ADAM_STEP)
    v_hat = v / (1.0 - ADAM_B2 ** ADAM_STEP)
    return -ADAM_LR * (m_hat / (jnp.sqrt(v_hat) + ADAM_EPS) + ADAM_WD * w), m, v


def _adamw(w, g_parts, m, v, *, name):
    r, c = w.shape
    shape, steps, at = _elementwise_tiles(r, c)
    n_g = len(g_parts)

    def body(*refs):
        w_ref, m_ref, v_ref = refs[n_g:n_g + 3]
        g_ref, d_ref, nm_ref, nv_ref = refs[n_g + 3:]
        g = refs[0][...]
        for p in refs[1:n_g]:
            g = g + p[...]
        g_ref[...] = g
        d_ref[...], nm_ref[...], nv_ref[...] = _adamw_math(w_ref[...], g, m_ref[...], v_ref[...])

    blk = pl.BlockSpec(shape, at)
    return pl.pallas_call(
        body, name=name, grid=(steps,), in_specs=[blk] * (n_g + 3), out_specs=[blk] * 4,
        out_shape=[jax.ShapeDtypeStruct((r, c), F32)] * 4, compiler_params=_params("parallel"),
    )(*g_parts, w, m, v)


def _pad_heads(v):
    return jnp.pad(v.reshape(N_GROUPS, 1, HPG), ((0, 0), (0, 0), (0, LANES - HPG)))


def _unpad_heads(v):
    return v[:, :HPG].reshape(1, N_HEADS)


_SMALL_EARLY = [("pool_w", (512, 128)), ("pool_scale", (1, 512)), ("conv_w", (4, D_XBC)), ("conv_b", (1, D_XBC)),
                ("dt_bias", (1, N_HEADS)), ("a_log", (1, N_HEADS)), ("d_skip", (1, N_HEADS)), ("ssm_norm_w", (1, D_SSM)),
                ("norm_ffn_w", (1, 1024)), ("norm_f_w", (1, 1024))]
_SMALL_LATE = [("norm_mix_w", (1, 1024)), ("meta", (N_META, 1024)), ("loss", (1, 1))]


def _pack_small(grads, layout):
    rows = []
    for nm, shape in layout:
        flat = grads[nm].reshape(-1)
        rows.append(jnp.pad(flat, (0, (-flat.size) % LANES)).reshape(-1, LANES))
    packed = jnp.concatenate(rows, axis=0)
    return jnp.pad(packed, ((0, (-packed.shape[0]) % 8), (0, 0)))


def _unpack_small(packed, layout):
    out, r0 = {}, 0
    for nm, shape in layout:
        size = shape[0] * shape[1]
        nrow = -(-size // LANES)
        out[nm] = packed[r0:r0 + nrow].reshape(-1)[:size].reshape(shape)
        r0 += nrow
    return out


def kernel(x, meta, norm_mix_w, w_in, pool_w, pool_scale, conv_w, conv_b, dt_bias, a_log, d_skip, ssm_norm_w, w_out, norm_ffn_w, w_ff1, w_ff2, norm_f_w, loss_target, m_meta, m_norm_mix_w, m_w_in, m_pool_w, m_pool_scale, m_conv_w, m_conv_b, m_dt_bias, m_a_log, m_d_skip, m_ssm_norm_w, m_w_out, m_norm_ffn_w, m_w_ff1, m_w_ff2, m_norm_f_w, v_meta, v_norm_mix_w, v_w_in, v_pool_w, v_pool_scale, v_conv_w, v_conv_b, v_dt_bias, v_a_log, v_d_skip, v_ssm_norm_w, v_w_out, v_norm_ffn_w, v_w_ff1, v_w_ff2, v_norm_f_w):
    bsz, seq, d = x.shape
    t = seq + CHUNK
    n = bsz * t
    chip = 2 * lax.axis_index("x") + lax.axis_index("y")
    d_in = w_in.shape[2] * 4

    g_conv, g_meta = _comm(_Gather([conv_w[0], meta]), name="gather_small")
    convw = g_conv.transpose(1, 0, 2).reshape(CONV_W, D_XBC)
    meta_full = g_meta.transpose(1, 0, 2).reshape(N_META, d)
    (h0, hn1), (g_in,) = _embed_norm(x, meta_full, norm_mix_w, name="embed_norm",
                                     rider=_Gather([w_in[0].T.astype(BF16)]))
    h0f, hn1 = h0.reshape(n, d), hn1.reshape(n, d)
    late_weights = _Gather([w_out[0].astype(BF16), w_ff1[0].astype(BF16), w_ff2[0].astype(BF16)])
    win = g_in.reshape(d_in, d)
    wu, wz = win[:D_POOL], win[D_POOL:D_POOL + D_SSM]
    wx = win[D_POOL + D_SSM:D_POOL + D_SSM + D_XBC]
    wdt = jnp.pad(win[D_POOL + D_SSM + D_XBC:].reshape(N_GROUPS, HPG, d),
                  ((0, 0), (0, LANES - HPG), (0, 0))).reshape(D_DT, d)
    dtb, alog = _pad_heads(dt_bias), _pad_heads(a_log)
    dskip = jnp.repeat(d_skip, HEAD_DIM, axis=1)
    poolw = pool_w[0]

    u, z, dtr, dt_, acs_, tr_ = _proj_uz_dt(hn1, wu, wz, wdt, dtb, alog, t // CHUNK, name="proj_uzdt")
    xbc, xc = _proj_conv(hn1, wx, convw, conv_b, name="proj_xbc")
    ypool = _pool_fwd(u.reshape(bsz, t, D_POOL), poolw, pool_scale, name="pool_fwd")
    xbc3 = xbc.reshape(bsz, t, D_XBC)
    xc = xc.reshape(bsz, t, D_XBC)
    z3, dtr3 = z.reshape(bsz, t, D_SSM), dtr.reshape(bsz, t, D_DT)
    dt3, acs3 = dt_.reshape(bsz, t, D_DT), acs_.reshape(bsz, t, D_DT)
    tr3 = tr_.reshape(bsz, t // CHUNK, N_GROUPS, 16, LANES)
    (yn, ypre, sprev), (g_out, g_ff1, g_ff2) = _ssd_fwd(xc, dt3, acs3, tr3, z3, dskip, ssm_norm_w, name="ssd_fwd",
                                                        rider=late_weights)
    wo = g_out.reshape(D_POOL + D_SSM, d)
    wo_p, wo_s = wo[:D_POOL], wo[D_POOL:]
    w1 = g_ff1
    w2 = g_ff2.reshape(D_FF, d)
    ypool_f, yn_f = ypool.reshape(n, D_POOL), yn.reshape(n, D_SSM)
    add = lambda r, e: r + e
    h1, hn2 = _mm([ypool_f, yn_f], [wo_p, wo_s], name="out_proj", post=add, extras=(h0f,), norm_w=norm_ffn_w)
    act = _mm(hn2, w1, name="ff1", out_dtype=BF16)
    relu2 = lambda a: jnp.square(jnp.maximum(a, 0))
    dh2f, dh2bf, loss_acc, d_norm_f = _ff2_loss(act, w2, h1, loss_target, norm_f_w.reshape(1, d), t // CHUNK, name="ff2")

    dact = _mm(dh2bf, w2, name="ff2_bwd", nt=True, post=lambda r, a: r * (2.0 * jnp.maximum(a, 0).astype(F32)),
               extras=(act,), out_dtype=BF16)
    d_w2 = _mm_tn(act, dh2bf, name="ff2_dw", tk=2048, tn=1024, pre=relu2)
    d_w1 = _mm_tn(hn2, dact, name="ff1_dw", tk=1024, tn=2048, slab=D_FF // 4)
    dh1, dh1b, d_norm_ffn = _mm_rms_bwd(dact, w1, h1, norm_ffn_w, dh2f, name="ff1_bwd")
    dypool, dyn = _mm_fanout(dh1b, [wo_p, wo_s], name="out_proj_bwd")
    d_wo = _mm_tn_cat([ypool_f, yn_f], dh1b, name="out_proj_dw")
    big_late = [d_wo.reshape(4, (D_POOL + D_SSM) // 4, d),
                d_w1, d_w2.reshape(4, D_FF // 4, d)]
    (dz, dxs, dbm, dcm, ddtr, d_nw, d_heads), landed_late = _ssd_bwd(
        xc, dtr3, dt3, acs3, tr3, z3, ypre, sprev, dyn.reshape(bsz, t, D_SSM), dtb, alog, dskip, ssm_norm_w, name="ssd_bwd",
        rider=_Exchange(big_late))
    mine_late = [_chip_sum(l, name=f"chip_sum_{i + 1}") for i, l in enumerate(landed_late)]
    (dxbc, d_convwb), theirs_late = _conv_bwd(xbc3, dxs, dbm, dcm, convw, conv_b, name="conv_bwd",
                                               rider=_Swap(mine_late))
    du, d_poolw, d_poolsc = _pool_bwd(u.reshape(bsz, t, D_POOL), dypool.reshape(bsz, t, D_POOL), poolw, pool_scale,
                                      name="pool_bwd")
    duf, dzf, dxbcf, ddtrf = du.reshape(n, D_POOL), dz.reshape(n, D_SSM), dxbc.reshape(n, D_XBC), ddtr.reshape(n, D_DT)
    heads = jnp.sum(d_heads, axis=0)
    small_early = _pack_small({
        "pool_w": d_poolw, "pool_scale": d_poolsc,
        "conv_w": jnp.sum(d_convwb[:, :CONV_W], axis=0), "conv_b": jnp.sum(d_convwb[:, CONV_W:CONV_W + 1], axis=0),
        "dt_bias": _unpad_heads(heads[:, 2]), "a_log": _unpad_heads(heads[:, 1]), "d_skip": _unpad_heads(heads[:, 0]),
        "ssm_norm_w": jnp.sum(d_nw, axis=0), "norm_ffn_w": d_norm_ffn, "norm_f_w": d_norm_f}, _SMALL_EARLY)
    d_wuzdt = _mm_tn_cat([duf, dzf, ddtrf], hn1, name="proj_uzdt_dw", budget=WIDE_BUDGET)
    d_wx, (early_all,) = _mm_tn(dxbcf, hn1, name="proj_xbc_dw", tk=1280, tn=1024, rider=_Exchange([], small_early))
    d_wdt = d_wuzdt[D_POOL + D_SSM:].reshape(N_GROUPS, LANES, d)[:, :HPG].reshape(N_HEADS, d)
    d_win = jnp.concatenate([d_wuzdt[:D_POOL + D_SSM], d_wx, d_wdt], axis=0)
    big_in = d_win.reshape(4, d_in // 4, d)
    dhn1, (landed_in,) = _mm([duf, dzf, dxbcf, ddtrf], [wu, wz, wx, wdt], name="proj_bwd",
                             rider=_Exchange([big_in]))
    grad_x, d_head_rows, d_norm_mix = _input_grad(
        dhn1.reshape(bsz, t, d), h0, norm_mix_w, dh1.reshape(bsz, t, d), seq, name="input_grad")

    small_late = _pack_small({"norm_mix_w": d_norm_mix, "meta": jnp.sum(d_head_rows[:, PAD:], axis=0),
                              "loss": loss_acc[0:1, 0:1]}, _SMALL_LATE)
    (late_all,) = _comm(_Exchange([], small_late), name="exchange_small")
    mine_in = _chip_sum(landed_in, name="chip_sum_0")
    (theirs_in,) = _comm(_Swap([mine_in]), name="swap_cores")
    mine, theirs = [mine_in] + mine_late, [theirs_in] + list(theirs_late)
    gsmall = {**_unpack_small(_device_sum(early_all, name="device_sum_early"), _SMALL_EARLY),
              **_unpack_small(_device_sum(late_all, name="device_sum_late"), _SMALL_LATE)}
    gsmall["conv_w"] = lax.dynamic_slice_in_dim(gsmall["conv_w"], chip * (D_XBC // 4), D_XBC // 4, axis=1)
    gsmall["meta"] = lax.dynamic_slice_in_dim(gsmall["meta"], chip * (d // 4), d // 4, axis=1)
    loss = gsmall["loss"][0, 0]

    given = dict(meta=(meta, m_meta, v_meta), norm_mix_w=(norm_mix_w, m_norm_mix_w, v_norm_mix_w),
                 w_in=(w_in, m_w_in, v_w_in), pool_w=(pool_w, m_pool_w, v_pool_w),
                 pool_scale=(pool_scale, m_pool_scale, v_pool_scale), conv_w=(conv_w, m_conv_w, v_conv_w),
                 conv_b=(conv_b, m_conv_b, v_conv_b), dt_bias=(dt_bias, m_dt_bias, v_dt_bias),
                 a_log=(a_log, m_a_log, v_a_log), d_skip=(d_skip, m_d_skip, v_d_skip),
                 ssm_norm_w=(ssm_norm_w, m_ssm_norm_w, v_ssm_norm_w), w_out=(w_out, m_w_out, v_w_out),
                 norm_ffn_w=(norm_ffn_w, m_norm_ffn_w, v_norm_ffn_w), w_ff1=(w_ff1, m_w_ff1, v_w_ff1),
                 w_ff2=(w_ff2, m_w_ff2, v_w_ff2), norm_f_w=(norm_f_w, m_norm_f_w, v_norm_f_w))
    big_names = ["w_in", "w_out", "w_ff1", "w_ff2"]
    results = {}
    for nm, (w, m, v) in given.items():
        if nm in big_names:
            i = big_names.index(nm)
            parts, shape2 = (mine[i], theirs[i]), mine[i].shape
        else:
            parts, shape2 = (gsmall[nm],), gsmall[nm].shape
        if nm == "w_in":
            outs = _adamw(w[0].T, parts, m[0].T, v[0].T, name=f"adamw_{nm}")
            results[nm] = [o.T[None] for o in outs]
        else:
            outs = _adamw(w.reshape(shape2), parts, m.reshape(shape2), v.reshape(shape2), name=f"adamw_{nm}")
            results[nm] = [o.reshape(w.shape) for o in outs]
    order = list(given)
    return (loss, grad_x, *[results[nm][0] for nm in order], *[results[nm][1] for nm in order],
            *[results[nm][2] for nm in order], *[results[nm][3] for nm in order])
```

```python
import jax
import jax.numpy as jnp
from jax import lax
from jax.experimental import pallas as pl
from jax.experimental.pallas import tpu as pltpu

F32 = jnp.float32
BF16 = jnp.bfloat16
MESH = pl.DeviceIdType.MESH
ANY = pl.BlockSpec(memory_space=pl.ANY)

D_MODEL = 1024
N_META = 16
CHUNK = 128
PAD = CHUNK - N_META
POOL_WINDOWS = (2, 4, 8, 16)
D_POOL = 512
POOL_GROUP = 128
D_SSM = 1536
N_HEADS = 24
N_GROUPS = 4
HPG = 6
HEAD_DIM = 64
D_STATE = 128
GW = HPG * HEAD_DIM
D_XBC = D_SSM + 2 * N_GROUPS * D_STATE
D_DT = N_GROUPS * 128
D_FF = 4096
CONV_W = 4
EPS = 1e-5
LANES = 128
VMEM_LIMIT = 56 * 1024 * 1024

ADAM_LR, ADAM_B1, ADAM_B2, ADAM_EPS, ADAM_WD, ADAM_STEP = 0.001, 0.9, 0.999, 1e-08, 0.01, 10


def _params(*sem):
    return pltpu.CompilerParams(dimension_semantics=sem, vmem_limit_bytes=VMEM_LIMIT)


def _pick(n, cands):
    for c in cands:
        if n % c == 0:
            return c
    raise ValueError(f"no block size for {n}")


def _dot(a, b):
    return jnp.dot(a.astype(BF16), b.astype(BF16), preferred_element_type=F32)


def _dot_nt(a, b):
    return lax.dot_general(a.astype(BF16), b.astype(BF16), (((1,), (1,)), ((), ())), preferred_element_type=F32)


def _dot_tn(a, b):
    return lax.dot_general(a.astype(BF16), b.astype(BF16), (((0,), (0,)), ((), ())), preferred_element_type=F32)


def _dot_exact(mask, x, terms=3):
    m = mask.astype(BF16)
    dot = lambda t: jnp.dot(m, t, preferred_element_type=F32)
    hi = x.astype(BF16)
    r1 = x - hi.astype(F32)
    mid = r1.astype(BF16)
    if terms == 2:
        return dot(hi) + dot(mid)
    lo = (r1 - mid.astype(F32)).astype(BF16)
    return dot(hi) + dot(mid) + dot(lo)


def _sigmoid(x):
    return 1.0 / (1.0 + jnp.exp(-x))


def _softplus(x):
    return jnp.maximum(x, 0.0) + jnp.log1p(jnp.exp(-jnp.abs(x)))


def _sum_all(x):
    return jnp.sum(jnp.sum(x, axis=1, keepdims=True), axis=0, keepdims=True)


ROW_TILES = (1056, 768, 704, 512, 384, 256, 128)
TILE_BUDGET = 28 * 1024 * 1024


def _row_tile(n, bytes_per_row, fixed_bytes, budget=TILE_BUDGET):
    for tm in ROW_TILES:
        if n % tm == 0 and 2 * (tm * bytes_per_row + fixed_bytes) <= budget:
            return tm
    raise ValueError(f"no row tile for {n}")


WIDE_BUDGET = 38 * 1024 * 1024


def _mm(a, w, *, name, tn=512, nt=False, pre=None, post=None, extras=(), out_dtype=F32, norm_w=None, rider=None):
    assert norm_w is None or (rider is None and out_dtype == F32)
    a_list = list(a) if isinstance(a, (list, tuple)) else [a]
    w_list = list(w) if isinstance(w, (list, tuple)) else [w]
    n_a, n_ex = len(a_list), len(extras)
    n = a_list[0].shape[0]
    shard = w_list[0].shape[2] if w_list[0].ndim == 3 else None
    assert shard is None or (not nt and n_a == 1 and shard % tn == 0)
    m = w_list[0].shape[0] * shard if shard else w_list[0].shape[0] if nt else w_list[0].shape[1]
    tn = min(tn, m)
    size = lambda dt: jnp.dtype(dt).itemsize
    per_row = (sum(x.shape[1] * size(x.dtype) for x in a_list) + m * size(out_dtype)
               + sum(m * size(e.dtype) for e in extras) + (2 * m if norm_w is not None else 0))
    tm = _row_tile(n, per_row, sum(x.size * size(x.dtype) for x in w_list) // 2, WIDE_BUDGET)
    n_norm = 0 if norm_w is None else 1

    def body(*refs):
        a_refs, w_refs, ex_refs = refs[:n_a], refs[n_a:2 * n_a], refs[2 * n_a:2 * n_a + n_ex]
        o_ref = refs[2 * n_a + n_ex + n_norm]
        avs = [(a_ref[...] if pre is None else pre(a_ref[...])).astype(BF16) for a_ref in a_refs]
        for c0 in range(0, m, tn):
            r = None
            for av, w_ref in zip(avs, w_refs):
                if shard:
                    term = _dot(av, w_ref[c0 // shard, :, c0 % shard:c0 % shard + tn])
                else:
                    term = _dot_nt(av, w_ref[c0:c0 + tn, :]) if nt else _dot(av, w_ref[:, c0:c0 + tn])
                r = term if r is None else r + term
            if post is not None:
                r = post(r, *[e[:, c0:c0 + tn] for e in ex_refs])
            o_ref[:, c0:c0 + tn] = r.astype(out_dtype)
        if n_norm:
            x = o_ref[...]
            scale = lax.rsqrt(jnp.mean(x * x, axis=-1, keepdims=True) + EPS)
            refs[2 * n_a + n_ex + 2][...] = (x * scale * refs[2 * n_a + n_ex][...]).astype(BF16)

    a_specs = [pl.BlockSpec((tm, x.shape[1]), lambda i: (i, 0)) for x in a_list]
    w_specs = [pl.BlockSpec(x.shape, lambda i, nd=x.ndim: (0,) * nd, pipeline_mode=pl.Buffered(1)) for x in w_list]
    blk = pl.BlockSpec((tm, m), lambda i: (i, 0))
    vec = [pl.BlockSpec((1, m), lambda i: (0, 0))] * n_norm
    grid = (n // tm,)
    ride = _Ride(rider, body, 2 * n_a + n_ex + n_norm, 1 + n_norm, 0, grid)
    outs = pl.pallas_call(
        ride.body, name=name, grid=grid,
        in_specs=a_specs + w_specs + [blk] * n_ex + vec + ride.in_specs,
        out_specs=[blk] * (1 + n_norm) + ride.out_specs,
        out_shape=[jax.ShapeDtypeStruct((n, m), out_dtype)] + [jax.ShapeDtypeStruct((n, m), BF16)] * n_norm + ride.out_shape,
        scratch_shapes=ride.scratch, compiler_params=_params(*ride.semantics(("parallel",))),
    )(*a_list, *w_list, *extras, *([norm_w] * n_norm), *ride.args)
    if n_norm:
        return outs[0], outs[1]
    return (outs[0], outs[1:]) if rider else outs[0]


def _mm_fanout(a, ws, *, name, tn=512):
    n, k = a.shape
    ms = [w.shape[0] for w in ws]
    tm = _row_tile(n, k * 2 + 4 * sum(ms), sum(w.size for w in ws), WIDE_BUDGET)
    n_w = len(ws)

    def body(a_ref, *refs):
        av = a_ref[...]
        for w_ref, o_ref, m in zip(refs[:n_w], refs[n_w:], ms):
            for c0 in range(0, m, tn):
                o_ref[:, c0:c0 + tn] = _dot_nt(av, w_ref[c0:c0 + tn, :])

    return pl.pallas_call(
        body, name=name, grid=(n // tm,),
        in_specs=[pl.BlockSpec((tm, k), lambda i: (i, 0))]
        + [pl.BlockSpec(w.shape, lambda i: (0, 0), pipeline_mode=pl.Buffered(1)) for w in ws],
        out_specs=[pl.BlockSpec((tm, m), lambda i: (i, 0)) for m in ms],
        out_shape=[jax.ShapeDtypeStruct((n, m), F32) for m in ms],
        compiler_params=_params("parallel"),
    )(a, *ws)


def _mm_tn(a, g, *, name, tk, tn, pre=None, slab=None, rider=None):
    n, k = a.shape
    m = g.shape[1]
    tk, tn = min(tk, k), min(tn, m)
    tm = _row_tile(n, tk * jnp.dtype(a.dtype).itemsize + tn * jnp.dtype(g.dtype).itemsize, tk * tn * 4)
    steps = n // tm

    def body(a_ref, g_ref, o_ref, acc_ref):
        r = pl.program_id(2)

        @pl.when(r == 0)
        def _():
            acc_ref[...] = jnp.zeros_like(acc_ref)

        av = a_ref[...]
        if pre is not None:
            av = pre(av)
        if slab:
            for s in range(tn // slab):
                acc_ref[s] += _dot_tn(av, g_ref[:, s * slab:(s + 1) * slab])
        else:
            acc_ref[...] += _dot_tn(av, g_ref[...])

        @pl.when(r == steps - 1)
        def _():
            o_ref[...] = acc_ref[...].astype(BF16)

    if slab:
        block, out_spec = (tn // slab, tk, slab), pl.BlockSpec((tn // slab, tk, slab), lambda i, j, r: (j, i, 0))
        out_shape = jax.ShapeDtypeStruct((m // slab, k, slab), BF16)
    else:
        block, out_spec = (tk, tn), pl.BlockSpec((tk, tn), lambda i, j, r: (i, j))
        out_shape = jax.ShapeDtypeStruct((k, m), BF16)
    grid = (k // tk, m // tn, steps)
    ride = _Ride(rider, body, 2, 1, 1, grid)
    outs = pl.pallas_call(
        ride.body, name=name, grid=grid,
        in_specs=[pl.BlockSpec((tm, tk), lambda i, j, r: (r, i)), pl.BlockSpec((tm, tn), lambda i, j, r: (r, j))]
        + ride.in_specs,
        out_specs=[out_spec] + ride.out_specs, out_shape=[out_shape] + ride.out_shape,
        scratch_shapes=[pltpu.VMEM(block, F32)] + ride.scratch,
        compiler_params=_params(*ride.semantics(("parallel", "parallel", "arbitrary"))),
    )(a, g, *ride.args)
    return (outs[0], outs[1:]) if rider else outs[0]


def _mm_tn_cat(a_list, g, *, name, budget=TILE_BUDGET):
    n, m = g.shape
    ks = [a.shape[1] for a in a_list]
    size = lambda x: jnp.dtype(x.dtype).itemsize
    tm = _row_tile(n, sum(a.shape[1] * size(a) for a in a_list) + m * size(g), sum(ks) * m * 4, budget)
    steps, n_a = n // tm, len(a_list)

    def body(*refs):
        g_ref, o_ref, acc_ref = refs[n_a], refs[n_a + 1], refs[n_a + 2]
        r = pl.program_id(0)

        @pl.when(r == 0)
        def _():
            acc_ref[...] = jnp.zeros_like(acc_ref)

        gv, k0 = g_ref[...], 0
        for a_ref, k in zip(refs[:n_a], ks):
            acc_ref[k0:k0 + k, :] += _dot_tn(a_ref[...], gv)
            k0 += k

        @pl.when(r == steps - 1)
        def _():
            o_ref[...] = acc_ref[...].astype(BF16)

    return pl.pallas_call(
        body, name=name, grid=(steps,),
        in_specs=[pl.BlockSpec((tm, k), lambda r: (r, 0)) for k in ks] + [pl.BlockSpec((tm, m), lambda r: (r, 0))],
        out_specs=pl.BlockSpec((sum(ks), m), lambda r: (0, 0)),
        out_shape=jax.ShapeDtypeStruct((sum(ks), m), BF16),
        scratch_shapes=[pltpu.VMEM((sum(ks), m), F32)],
        compiler_params=_params("arbitrary"),
    )(*a_list, g)


def _mm_rms_bwd(a, w, h, w_norm, dres, *, name):
    n, k = a.shape
    d = h.shape[1]
    slabs, _, ks = w.shape
    tm = _row_tile(n, k * jnp.dtype(a.dtype).itemsize + d * (4 + 4 + 4 + 2), d * k, WIDE_BUDGET)

    def body(a_ref, w_ref, h_ref, wn_ref, dres_ref, dx_ref, dxb_ref, dw_ref):
        @pl.when(pl.program_id(0) == 0)
        def _():
            dw_ref[...] = jnp.zeros_like(dw_ref)

        dyv = None
        for s in range(slabs):
            part = _dot_nt(a_ref[:, s * ks:(s + 1) * ks], w_ref[s])
            dyv = part if dyv is None else dyv + part
        x = h_ref[...]
        r = lax.rsqrt(jnp.mean(x * x, axis=-1, keepdims=True) + EPS)
        g = dyv * wn_ref[...]
        dx = r * (g - x * (r * r) * jnp.mean(g * x, axis=-1, keepdims=True)) + dres_ref[...]
        dx_ref[...] = dx
        dxb_ref[...] = dx.astype(BF16)
        dw_ref[...] += jnp.sum(dyv * x * r, axis=0, keepdims=True)

    row = pl.BlockSpec((tm, d), lambda i: (i, 0))
    vec = pl.BlockSpec((1, d), lambda i: (0, 0))
    return pl.pallas_call(
        body, name=name, grid=(n // tm,),
        in_specs=[pl.BlockSpec((tm, k), lambda i: (i, 0)),
                  pl.BlockSpec(w.shape, lambda i: (0, 0, 0), pipeline_mode=pl.Buffered(1)), row, vec, row],
        out_specs=[row, row, vec],
        out_shape=[jax.ShapeDtypeStruct((n, d), F32), jax.ShapeDtypeStruct((n, d), BF16), jax.ShapeDtypeStruct((1, d), F32)],
        compiler_params=_params("arbitrary"),
    )(a, w, h, w_norm, dres)


def _embed_norm(x, meta, w, *, name, rider=None):
    bsz, seq, d = x.shape
    t = seq + CHUNK
    nc = t // CHUNK

    def body(x_ref, meta_ref, w_ref, h_ref, hn_ref):
        j = pl.program_id(0)
        first = jnp.concatenate([jnp.zeros((PAD, d), F32), meta_ref[...]], axis=0)
        for e in range(bsz):
            h = jnp.where(j == 0, first, x_ref[e])
            r = lax.rsqrt(jnp.mean(h * h, axis=-1, keepdims=True) + EPS)
            h_ref[e] = h
            hn_ref[e] = (h * r * w_ref[...]).astype(BF16)

    row = pl.BlockSpec((bsz, CHUNK, d), lambda j: (0, j, 0))
    grid = (nc,)
    ride = _Ride(rider, body, 3, 2, 0, grid)
    outs = pl.pallas_call(
        ride.body, name=name, grid=grid,
        in_specs=[pl.BlockSpec((bsz, CHUNK, d), lambda j: (0, jnp.maximum(j - 1, 0), 0)),
                  pl.BlockSpec((N_META, d), lambda j: (0, 0)), pl.BlockSpec((1, d), lambda j: (0, 0))] + ride.in_specs,
        out_specs=[row, row] + ride.out_specs,
        out_shape=[jax.ShapeDtypeStruct((bsz, t, d), F32), jax.ShapeDtypeStruct((bsz, t, d), BF16)] + ride.out_shape,
        scratch_shapes=ride.scratch, compiler_params=_params(*ride.semantics(("parallel",))),
    )(x, meta, w, *ride.args)
    return outs[:2], outs[2:]


def _ff2_loss(act, w2, h1, target, w, nc, *, name):
    n, f = act.shape
    d = h1.shape[1]
    tm = _pick(n, (768, 384, 128))
    per_tile = tm // CHUNK

    def body(act_ref, w2_ref, h1_ref, tgt_ref, w_ref, dh_ref, dhb_ref, loss_ref, dw_ref, tbuf_ref, sems):
        i = pl.program_id(0)

        @pl.when(i == 0)
        def _():
            loss_ref[...] = jnp.zeros_like(loss_ref)
            dw_ref[...] = jnp.zeros_like(dw_ref)

        chunks = [i * per_tile + cc for cc in range(per_tile)]
        fetch = [pltpu.make_async_copy(
            tgt_ref.at[c // nc, pl.ds(pl.multiple_of(jnp.maximum(c % nc - 1, 0) * CHUNK, CHUNK), CHUNK)],
            tbuf_ref.at[pl.ds(cc * CHUNK, CHUNK)], sems.at[cc]) for cc, c in enumerate(chunks)]
        for cc, c in enumerate(chunks):
            @pl.when(c % nc > 0)
            def _(cc=cc):
                fetch[cc].start()

            @pl.when(c % nc == 0)
            def _(cc=cc):
                tbuf_ref[pl.ds(cc * CHUNK, CHUNK), :] = jnp.zeros((CHUNK, d), F32)

        av = jnp.square(jnp.maximum(act_ref[...], 0)).astype(BF16)
        for c0 in range(0, d, 512):
            dh_ref[:, c0:c0 + 512] = h1_ref[:, c0:c0 + 512] + _dot(av, w2_ref[:, c0:c0 + 512])
        wv = w_ref[...]
        for cc, c in enumerate(chunks):
            @pl.when(c % nc > 0)
            def _(cc=cc):
                fetch[cc].wait()

            rows = pl.ds(cc * CHUNK, CHUNK)
            x = dh_ref[rows, :]
            r = lax.rsqrt(jnp.mean(x * x, axis=-1, keepdims=True) + EPS)
            diff = jnp.where(c % nc > 0, x * r * wv - tbuf_ref[rows, :], 0.0)
            loss_ref[...] += _sum_all(diff * diff) * (0.5 / d)
            dy = diff * (1.0 / d)
            g = dy * wv
            dh = r * (g - x * (r * r) * jnp.mean(g * x, axis=-1, keepdims=True))
            dh_ref[rows, :] = dh
            dhb_ref[rows, :] = dh.astype(BF16)
            dw_ref[...] += jnp.sum(dy * x * r, axis=0, keepdims=True)

    row = lambda width: pl.BlockSpec((tm, width), lambda i: (i, 0))
    return pl.pallas_call(
        body, name=name, grid=(n // tm,),
        in_specs=[row(f), pl.BlockSpec(w2.shape, lambda i: (0, 0), pipeline_mode=pl.Buffered(1)), row(d), ANY,
                  pl.BlockSpec((1, d), lambda i: (0, 0))],
        out_specs=[row(d), row(d), pl.BlockSpec((8, LANES), lambda i: (0, 0)), pl.BlockSpec((1, d), lambda i: (0, 0))],
        out_shape=[jax.ShapeDtypeStruct((n, d), F32), jax.ShapeDtypeStruct((n, d), BF16),
                   jax.ShapeDtypeStruct((8, LANES), F32), jax.ShapeDtypeStruct((1, d), F32)],
        scratch_shapes=[pltpu.VMEM((tm, d), F32), pltpu.SemaphoreType.DMA((per_tile,))],
        compiler_params=_params("arbitrary"),
    )(act, w2, h1, target, w)


def _pool_masks(j, transposed):
    r = lax.broadcasted_iota(jnp.int32, (CHUNK, 2 * CHUNK), 0)
    c = lax.broadcasted_iota(jnp.int32, (CHUNK, 2 * CHUNK), 1)
    masks = []
    for w in POOL_WINDOWS:
        if transposed:
            m = (c >= r) & (c < r + w)
        else:
            s = c - CHUNK
            m = (s <= r) & (s > r - w) & (s + j * CHUNK >= 0)
        masks.append(m.astype(F32))
    return masks


POOL_TERMS = 2


def _pool_count(t_global, w):
    return jnp.clip(t_global - PAD + 1, 1, w).astype(F32)


def _pool_fwd(u, pool_w, pool_scale, *, name):
    bsz, t, _ = u.shape
    nc = t // CHUNK

    def body(prev_ref, cur_ref, pw_ref, sc_ref, o_ref):
        j = pl.program_id(0)
        masks = _pool_masks(j, False)
        tg = j * CHUNK + lax.broadcasted_iota(jnp.int32, (CHUNK, 1), 0)
        count = [_pool_count(tg, w) for w in POOL_WINDOWS]
        units = [(e, gi) for e in range(bsz) for gi in range(len(POOL_WINDOWS))]
        sl = lambda gi: pl.ds(gi * POOL_GROUP, POOL_GROUP)
        cur = {(e, gi): cur_ref[e, :, sl(gi)] for e, gi in units}
        both = {(e, gi): jnp.concatenate([prev_ref[e, :, sl(gi)], cur[e, gi]], axis=0) for e, gi in units}
        win = {(e, gi): _dot_exact(masks[gi], both[e, gi], POOL_TERMS) for e, gi in units}
        pooled = {(e, gi): win[e, gi] / count[gi] - cur[e, gi] for e, gi in units}
        mixed = {(e, gi): _dot(pooled[e, gi], pw_ref[gi]) for e, gi in units}
        for e, gi in units:
            o_ref[e, :, sl(gi)] = (mixed[e, gi] * sc_ref[:, sl(gi)]).astype(BF16)

    blk = lambda f: pl.BlockSpec((bsz, CHUNK, D_POOL), f)
    return pl.pallas_call(
        body, name=name, grid=(nc,),
        in_specs=[blk(lambda j: (0, jnp.maximum(j - 1, 0), 0)), blk(lambda j: (0, j, 0)),
                  pl.BlockSpec((4, POOL_GROUP, POOL_GROUP), lambda j: (0, 0, 0)),
                  pl.BlockSpec((1, D_POOL), lambda j: (0, 0))],
        out_specs=blk(lambda j: (0, j, 0)), out_shape=jax.ShapeDtypeStruct(u.shape, BF16),
        compiler_params=_params("parallel"),
    )(u, u, pool_w, pool_scale)


def _pool_bwd(u, dyp, pool_w, pool_scale, *, name):
    bsz, t, _ = u.shape
    nc = t // CHUNK

    def body(prev_ref, cur_ref, dy_ref, dyn_ref, pw_ref, sc_ref, du_ref, dpw_ref, dsc_ref):
        j = pl.program_id(0)

        @pl.when(j == 0)
        def _():
            dpw_ref[...] = jnp.zeros_like(dpw_ref)
            dsc_ref[...] = jnp.zeros_like(dsc_ref)

        fwd = _pool_masks(j, False)
        bwd = _pool_masks(j, True)
        tg = j * CHUNK + lax.broadcasted_iota(jnp.int32, (CHUNK, 1), 0)
        count = [_pool_count(tg, w) for w in POOL_WINDOWS]
        count_next = [_pool_count(tg + CHUNK, w) for w in POOL_WINDOWS]
        has_next = j < nc - 1
        groups = range(len(POOL_WINDOWS))
        units = [(e, gi) for e in range(bsz) for gi in groups]
        sl = lambda gi: pl.ds(gi * POOL_GROUP, POOL_GROUP)
        cur = {(e, gi): cur_ref[e, :, sl(gi)] for e, gi in units}
        both = {(e, gi): jnp.concatenate([prev_ref[e, :, sl(gi)], cur[e, gi]], axis=0) for e, gi in units}
        win = {(e, gi): _dot_exact(fwd[gi], both[e, gi], POOL_TERMS) for e, gi in units}
        pooled = {(e, gi): win[e, gi] / count[gi] - cur[e, gi] for e, gi in units}
        dy = {(e, gi): dy_ref[e, :, sl(gi)] for e, gi in units}
        mixed = {(e, gi): _dot(pooled[e, gi], pw_ref[gi]) for e, gi in units}
        dm = {(e, gi): dy[e, gi] * sc_ref[:, sl(gi)] for e, gi in units}
        dm_next = {(e, gi): jnp.where(has_next, dyn_ref[e, :, sl(gi)], 0.0) * sc_ref[:, sl(gi)] for e, gi in units}
        dpw = {(e, gi): _dot_tn(pooled[e, gi], dm[e, gi]) for e, gi in units}
        dpooled = {(e, gi): _dot_nt(dm[e, gi], pw_ref[gi]) for e, gi in units}
        dpooled_next = {(e, gi): _dot_nt(dm_next[e, gi], pw_ref[gi]) for e, gi in units}
        spread = {(e, gi): jnp.concatenate([dpooled[e, gi] / count[gi], dpooled_next[e, gi] / count_next[gi]], axis=0)
                  for e, gi in units}
        back = {(e, gi): _dot_exact(bwd[gi], spread[e, gi], POOL_TERMS) for e, gi in units}
        for e, gi in units:
            du_ref[e, :, sl(gi)] = (back[e, gi] - dpooled[e, gi]).astype(BF16)
        for gi in groups:
            dsc, dw = None, None
            for e in range(bsz):
                term = jnp.sum(dy[e, gi] * mixed[e, gi], axis=0, keepdims=True)
                dsc = term if dsc is None else dsc + term
                dw = dpw[e, gi] if dw is None else dw + dpw[e, gi]
            dsc_ref[:, sl(gi)] += dsc
            dpw_ref[gi] += dw

    blk = lambda f: pl.BlockSpec((bsz, CHUNK, D_POOL), f)
    return pl.pallas_call(
        body, name=name, grid=(nc,),
        in_specs=[blk(lambda j: (0, jnp.maximum(j - 1, 0), 0)), blk(lambda j: (0, j, 0)),
                  blk(lambda j: (0, j, 0)), blk(lambda j: (0, jnp.minimum(j + 1, nc - 1), 0)),
                  pl.BlockSpec((4, POOL_GROUP, POOL_GROUP), lambda j: (0, 0, 0)),
                  pl.BlockSpec((1, D_POOL), lambda j: (0, 0))],
        out_specs=[blk(lambda j: (0, j, 0)), pl.BlockSpec((4, POOL_GROUP, POOL_GROUP), lambda j: (0, 0, 0)),
                   pl.BlockSpec((1, D_POOL), lambda j: (0, 0))],
        out_shape=[jax.ShapeDtypeStruct(u.shape, BF16), jax.ShapeDtypeStruct((4, POOL_GROUP, POOL_GROUP), F32),
                   jax.ShapeDtypeStruct((1, D_POOL), F32)],
        compiler_params=_params("arbitrary"),
    )(u, u, dyp, dyp, pool_w, pool_scale)


CONV_SLAB = 512


def _conv_taps(tail, cur, keep_tail):
    ext = jnp.concatenate([jnp.where(keep_tail, tail, 0.0), cur], axis=0)
    return [(pltpu.roll(ext, CONV_W - 1 - k, 0) if k < CONV_W - 1 else ext)[8:] for k in range(CONV_W)]


def _conv_pre(taps, w_ref, b_ref, sl):
    acc = b_ref[:, sl]
    for k in range(CONV_W):
        acc = acc + w_ref[k:k + 1, sl] * taps[k]
    return acc


def _proj_conv(hn, w, conv_w, conv_b, *, name):
    n, d = hn.shape
    c = w.shape[0]
    assert PAD >= CONV_W - 1
    tm = _row_tile(n, d * 2 + c * (4 + 2), c * d, WIDE_BUDGET)

    def body(hn_ref, w_ref, cw_ref, cb_ref, xbc_ref, xc_ref, tail_ref):
        @pl.when(pl.program_id(0) == 0)
        def _():
            tail_ref[...] = jnp.zeros_like(tail_ref)

        av = hn_ref[...]
        starts = list(range(0, c, CONV_SLAB))

        def project(c0):
            xbc_ref[:, pl.ds(c0, CONV_SLAB)] = _dot_nt(av, w_ref[c0:c0 + CONV_SLAB, :])

        def convolve(c0):
            sl = pl.ds(c0, CONV_SLAB)
            xb = xbc_ref[:, sl]
            pre = _conv_pre(_conv_taps(tail_ref[:, sl], xb, True), cw_ref, cb_ref, sl)
            xc_ref[:, sl] = (pre * _sigmoid(pre)).astype(BF16)
            tail_ref[:, sl] = xb[tm - 8:, :]

        project(starts[0])
        for c0, c_next in zip(starts, starts[1:] + [None]):
            if c_next is not None:
                project(c_next)
            convolve(c0)

    row = lambda width: pl.BlockSpec((tm, width), lambda i: (i, 0))
    return pl.pallas_call(
        body, name=name, grid=(n // tm,),
        in_specs=[row(d), pl.BlockSpec(w.shape, lambda i: (0, 0), pipeline_mode=pl.Buffered(1)),
                  pl.BlockSpec((CONV_W, c), lambda i: (0, 0)), pl.BlockSpec((1, c), lambda i: (0, 0))],
        out_specs=[row(c), row(c)],
        out_shape=[jax.ShapeDtypeStruct((n, c), F32), jax.ShapeDtypeStruct((n, c), BF16)],
        scratch_shapes=[pltpu.VMEM((8, c), F32)],
        compiler_params=_params("arbitrary"),
    )(hn, w, conv_w, conv_b)


def _conv_bwd(xbc, dxs, db, dc, conv_w, conv_b, *, name, rider=None):
    bsz, t, c = xbc.shape
    tile = _pick(t, (3 * CHUNK, CHUNK))
    nc = t // tile
    halo = 16
    rows = tile + halo

    def body(tail_ref, cur_ref, head_ref, dxs_ref, db_ref, dc_ref, dxs_head, db_head, dc_head, w_ref, b_ref,
             dx_ref, dwb_ref):
        j = pl.program_id(1)

        @pl.when(j == 0)
        def _():
            dwb_ref[...] = jnp.zeros_like(dwb_ref)

        has_prev, has_next = j > 0, j < nc - 1
        for c0 in range(0, c, CONV_SLAB):
            sl = pl.ds(c0, CONV_SLAB)
            if c0 < D_SSM:
                dxc, dxc_next = dxs_ref[0, :, sl], dxs_head[0, :, sl]
            elif c0 < D_SSM + D_POOL:
                dxc, dxc_next = db_ref[0], db_head[0]
            else:
                dxc, dxc_next = dc_ref[0], dc_head[0]
            dxc = jnp.concatenate([dxc.astype(F32), jnp.where(has_next, dxc_next.astype(F32), 0.0)], axis=0)
            ext = jnp.concatenate([jnp.where(has_prev, tail_ref[0, :, sl], 0.0), cur_ref[0, :, sl],
                                   jnp.where(has_next, head_ref[0, :, sl], 0.0)], axis=0)
            taps = [(pltpu.roll(ext, CONV_W - 1 - k, 0) if k < CONV_W - 1 else ext)[8:] for k in range(CONV_W)]
            pre = _conv_pre(taps, w_ref, b_ref, sl)
            s = _sigmoid(pre)
            dpre = dxc * (s * (1.0 + pre * (1.0 - s)))
            acc = w_ref[CONV_W - 1:CONV_W, sl] * dpre[:tile]
            for k in range(CONV_W - 1):
                up = CONV_W - 1 - k
                acc = acc + w_ref[k:k + 1, sl] * pltpu.roll(dpre, rows - up, 0)[:tile]
            dx_ref[0, :, sl] = acc.astype(BF16)
            for k in range(CONV_W):
                dwb_ref[0, k:k + 1, sl] += jnp.sum(dpre[:tile] * taps[k][:tile], axis=0, keepdims=True)
            dwb_ref[0, CONV_W:CONV_W + 1, sl] += jnp.sum(dpre[:tile], axis=0, keepdims=True)

    assert CONV_SLAB == D_POOL and D_SSM % CONV_SLAB == 0
    row = lambda width: pl.BlockSpec((1, tile, width), lambda b, j: (b, j, 0))
    nxt = lambda width: pl.BlockSpec(
        (1, halo, width), lambda b, j: (b, jnp.minimum((j + 1) * (tile // halo), t // halo - 1), 0))
    grid = (bsz, nc)
    ride = _Ride(rider, body, 11, 2, 0, grid)
    outs = pl.pallas_call(
        ride.body, name=name, grid=grid,
        in_specs=[pl.BlockSpec((1, 8, c), lambda b, j: (b, jnp.maximum(j * (tile // 8) - 1, 0), 0)), row(c), nxt(c),
                  row(D_SSM), row(D_POOL), row(D_POOL), nxt(D_SSM), nxt(D_POOL), nxt(D_POOL),
                  pl.BlockSpec((CONV_W, c), lambda b, j: (0, 0)), pl.BlockSpec((1, c), lambda b, j: (0, 0))]
        + ride.in_specs,
        out_specs=[row(c), pl.BlockSpec((1, 8, c), lambda b, j: (b, 0, 0))] + ride.out_specs,
        out_shape=[jax.ShapeDtypeStruct(xbc.shape, BF16), jax.ShapeDtypeStruct((bsz, 8, c), F32)] + ride.out_shape,
        scratch_shapes=ride.scratch, compiler_params=_params(*ride.semantics(("parallel", "arbitrary"))),
    )(xbc, xbc, xbc, dxs, db, dc, dxs, db, dc, conv_w, conv_b, *ride.args)
    return outs[:2], outs[2:]


def _dt_valid(j):
    lane = lax.broadcasted_iota(jnp.int32, (CHUNK, LANES), 1)
    row = lax.broadcasted_iota(jnp.int32, (CHUNK, LANES), 0)
    return (lane < HPG) & ((j > 0) | (row >= PAD))


def _proj_uz_dt(hn, wu, wz, wdt, dtb, alog, nc, *, name):
    n, d = hn.shape
    tm = _pick(n, (768, 384, 128))
    per_tile = tm // CHUNK
    widths = (wu.shape[0], wz.shape[0], wdt.shape[0])

    def body(hn_ref, wu_ref, wz_ref, wdt_ref, dtb_ref, alog_ref, u_ref, z_ref, dtr_ref, dt_ref, acs_ref, tr_ref):
        i = pl.program_id(0)
        av = hn_ref[...]
        for w_ref, o_ref, m in zip((wu_ref, wz_ref, wdt_ref), (u_ref, z_ref, dtr_ref), widths):
            for c0 in range(0, m, 512):
                o_ref[:, c0:c0 + 512] = _dot_nt(av, w_ref[c0:c0 + 512, :])
        row = lax.broadcasted_iota(jnp.int32, (CHUNK, LANES), 0)
        lane = lax.broadcasted_iota(jnp.int32, (CHUNK, LANES), 1)
        tril = (row >= lane).astype(F32)
        units = [(cc, g) for cc in range(per_tile) for g in range(N_GROUPS)]
        at = lambda cc, g: (pl.ds(cc * CHUNK, CHUNK), pl.ds(g * LANES, LANES))
        valid = [(lane < HPG) & (((i * per_tile + cc) % nc > 0) | (row >= PAD)) for cc in range(per_tile)]
        dt = {(cc, g): jnp.where(valid[cc], _softplus(dtr_ref[at(cc, g)] + dtb_ref[g]), 0.0) for cc, g in units}
        acs = {(cc, g): _dot_exact(tril, dt[cc, g] * -jnp.exp(alog_ref[g])) for cc, g in units}
        for cc, g in units:
            dt_ref[at(cc, g)] = dt[cc, g]
            acs_ref[at(cc, g)] = acs[cc, g]
            tr_ref[cc, g, 0:8, :] = dt[cc, g].T[0:8]
            tr_ref[cc, g, 8:16, :] = acs[cc, g].T[0:8]

    row_blk = lambda width: pl.BlockSpec((tm, width), lambda i: (i, 0))
    whole = lambda w: pl.BlockSpec(w.shape, lambda i: (0, 0), pipeline_mode=pl.Buffered(1))
    const = pl.BlockSpec((N_GROUPS, 1, LANES), lambda i: (0, 0, 0))
    return pl.pallas_call(
        body, name=name, grid=(n // tm,),
        in_specs=[row_blk(d), whole(wu), whole(wz), whole(wdt), const, const],
        out_specs=[row_blk(widths[0]), row_blk(widths[1])] + [row_blk(D_DT)] * 3
        + [pl.BlockSpec((per_tile, N_GROUPS, 16, LANES), lambda i: (i, 0, 0, 0))],
        out_shape=[jax.ShapeDtypeStruct((n, widths[0]), F32), jax.ShapeDtypeStruct((n, widths[1]), F32)]
        + [jax.ShapeDtypeStruct((n, D_DT), F32)] * 3 + [jax.ShapeDtypeStruct((n // CHUNK, N_GROUPS, 16, LANES), F32)],
        compiler_params=_params("parallel"),
    )(hn, wu, wz, wdt, dtb, alog)


def _ssd_decay(dt, acs, tr):
    lane = lax.broadcasted_iota(jnp.int32, (CHUNK, LANES), 1)
    row = lax.broadcasted_iota(jnp.int32, (CHUNK, LANES), 0)
    return dict(lane=lane, row=row, dt=dt, causal=row >= lane, acs=acs, acs_t=tr[8:16], dt_t=tr[0:8],
                aend=acs[CHUNK - 1:CHUNK, :])


def _ssd_specs(bsz, nc, rev):
    ch = (lambda j: nc - 1 - j) if rev else (lambda j: j)
    return dict(
        xs=pl.BlockSpec((bsz, CHUNK, GW), lambda g, j: (0, ch(j), g)),
        bm=pl.BlockSpec((bsz, CHUNK, D_STATE), lambda g, j: (0, ch(j), D_SSM // D_STATE + g)),
        cm=pl.BlockSpec((bsz, CHUNK, D_STATE), lambda g, j: (0, ch(j), D_SSM // D_STATE + N_GROUPS + g)),
        lane_blk=pl.BlockSpec((bsz, CHUNK, LANES), lambda g, j: (0, ch(j), g)),
        grp_const=pl.BlockSpec((1, 1, LANES), lambda g, j: (g, 0, 0)),
        grp_vec=pl.BlockSpec((1, GW), lambda g, j: (0, g)),
        state=pl.BlockSpec((bsz, 1, D_STATE, GW), lambda g, j: (0, ch(j), 0, g)),
        tr=pl.BlockSpec((bsz, 1, 1, 16, LANES), lambda g, j: (0, ch(j), g, 0, 0)),
    )


def _ssd_fwd(xc, dt, acs, tr, z, dskip, normw, *, name, rider=None):
    bsz, t, _ = xc.shape
    nc = t // CHUNK
    sp = _ssd_specs(bsz, nc, False)

    def body(xs_ref, b_ref, c_ref, dt_ref, acs_ref, tr_ref, z_ref, dsk_ref, nw_ref, yn_ref, y_ref, sp_ref, s_ref):
        j = pl.program_id(1)

        @pl.when(j == 0)
        def _():
            s_ref[...] = jnp.zeros_like(s_ref)

        ex = range(bsz)
        units = [(e, r) for e in ex for r in range(HPG)]
        full = lambda v: jnp.broadcast_to(v, (CHUNK, LANES))
        pair = lambda r: pl.ds((r // 2) * LANES, LANES)
        q = [_ssd_decay(dt_ref[e], acs_ref[e], tr_ref[e, 0, 0]) for e in ex]
        for e in ex:
            sp_ref[e, 0] = s_ref[e]
        bm, cm = [b_ref[e] for e in ex], [c_ref[e] for e in ex]
        cb = [_dot_nt(cm[e], bm[e]) for e in ex]
        low = q[0]["lane"] < HEAD_DIM
        col = {(e, r): full(q[e]["acs"][:, r:r + 1]) for e, r in units}
        aend = {(e, r): q[e]["aend"][:, r:r + 1] for e, r in units}
        decay = {(e, r): jnp.exp(jnp.where(q[e]["causal"], col[e, r] - q[e]["acs_t"][r:r + 1, :], -jnp.inf))
                 for e, r in units}
        mp = {(e, r): cb[e] * decay[e, r] * q[e]["dt_t"][r:r + 1, :] for e, r in units}
        ce = {(e, r): cm[e] * jnp.exp(col[e, r]) for e, r in units}
        bk = {(e, r): bm[e] * (jnp.exp(aend[e, r] - col[e, r]) * full(q[e]["dt"][:, r:r + 1])) for e, r in units}
        xp = {(e, r): xs_ref[e, :, pair(r)] for e, r in units}
        s_old = {(e, r): s_ref[e, :, pair(r)] for e, r in units}
        y_h = {u: _dot(mp[u], xp[u]) + _dot(ce[u], s_old[u]) for u in units}
        s_h = {u: jnp.exp(aend[u]) * s_old[u] + _dot_tn(bk[u], xp[u]) for u in units}
        for e in ex:
            for r in range(0, HPG, 2):
                y_ref[e, :, pair(r)] = jnp.where(low, y_h[e, r], y_h[e, r + 1])
                s_ref[e, :, pair(r)] = jnp.where(low, s_h[e, r], s_h[e, r + 1])
        y = [y_ref[e] + dsk_ref[...] * xs_ref[e] for e in ex]
        zz = [z_ref[e] for e in ex]
        yg = [y[e] * (zz[e] * _sigmoid(zz[e])) for e in ex]
        rstd = [lax.rsqrt(jnp.mean(yg[e] * yg[e], axis=-1, keepdims=True) + EPS) for e in ex]
        for e in ex:
            y_ref[e] = y[e]
            yn_ref[e] = (yg[e] * rstd[e] * nw_ref[...]).astype(BF16)

    grid = (N_GROUPS, nc)
    ride = _Ride(rider, body, 9, 3, 1, grid)
    outs = pl.pallas_call(
        ride.body, name=name, grid=grid,
        in_specs=[sp["xs"], sp["bm"], sp["cm"], sp["lane_blk"], sp["lane_blk"], sp["tr"], sp["xs"],
                  sp["grp_vec"], sp["grp_vec"]] + ride.in_specs,
        out_specs=[sp["xs"], sp["xs"], sp["state"]] + ride.out_specs,
        out_shape=[jax.ShapeDtypeStruct((bsz, t, D_SSM), BF16), jax.ShapeDtypeStruct((bsz, t, D_SSM), F32),
                   jax.ShapeDtypeStruct((bsz, nc, D_STATE, D_SSM), F32)] + ride.out_shape,
        scratch_shapes=[pltpu.VMEM((bsz, D_STATE, GW), F32)] + ride.scratch,
        compiler_params=_params(*ride.semantics(("parallel", "arbitrary"))),
    )(xc, xc, xc, dt, acs, tr, z, dskip, normw, *ride.args)
    return outs[:3], outs[3:]


def _ssd_bwd(xc, dtr, dt, acs, tr, z, ypre, sprev, dyn, dtb, alog, dskip, normw, *, name, rider=None):
    bsz, t, _ = xc.shape
    nc = t // CHUNK
    sp = _ssd_specs(bsz, nc, True)

    def body(xs_ref, b_ref, c_ref, dtr_ref, dt_ref, acs_ref, tr_ref, z_ref, y_ref, sp_ref, dyn_ref, dtb_ref, alog_ref,
             dsk_ref, nw_ref, dz_ref, dxs_ref, db_ref, dc_ref, ddt_ref, dnw_ref, dsm_ref, ds_ref):
        j = pl.program_id(1)

        @pl.when(j == 0)
        def _():
            ds_ref[...] = jnp.zeros_like(ds_ref)
            dnw_ref[...] = jnp.zeros_like(dnw_ref)
            dsm_ref[...] = jnp.zeros_like(dsm_ref)

        ex = range(bsz)
        heads = range(HPG)
        units = [(e, r) for e in ex for r in heads]
        q = [_ssd_decay(dt_ref[e], acs_ref[e], tr_ref[e, 0, 0]) for e in ex]
        a = -jnp.exp(alog_ref[0])
        valid = _dt_valid(nc - 1 - j)
        lane, row = q[0]["lane"], q[0]["row"]
        lane1 = lane[0:1, :]
        nw = nw_ref[...]
        y, zz, dyn = [y_ref[e] for e in ex], [z_ref[e] for e in ex], [dyn_ref[e] for e in ex]
        sz = [_sigmoid(zz[e]) for e in ex]
        sil = [zz[e] * sz[e] for e in ex]
        yg = [y[e] * sil[e] for e in ex]
        rstd = [lax.rsqrt(jnp.mean(yg[e] * yg[e], axis=-1, keepdims=True) + EPS) for e in ex]
        gn = [dyn[e] * nw for e in ex]
        dyg = [rstd[e] * (gn[e] - yg[e] * (rstd[e] * rstd[e]) * jnp.mean(gn[e] * yg[e], axis=-1, keepdims=True))
               for e in ex]
        dy = [dyg[e] * sil[e] for e in ex]
        xs = [xs_ref[e] for e in ex]
        for e in ex:
            dnw_ref[e] += jnp.sum(dyn[e] * yg[e] * rstd[e], axis=0, keepdims=True)
            dz_ref[e] = (dyg[e] * y[e] * (sz[e] * (1.0 + zz[e] * (1.0 - sz[e])))).astype(BF16)
        dskip_cols = [jnp.sum(dy[e] * xs[e], axis=0, keepdims=True) for e in ex]

        bm, cm = [b_ref[e] for e in ex], [c_ref[e] for e in ex]
        cb = [_dot_nt(cm[e], bm[e]) for e in ex]
        zero = jnp.zeros((CHUNK, LANES), F32)
        full = lambda v: jnp.broadcast_to(v, (CHUNK, LANES))
        low = lane < HEAD_DIM
        half = [low if r % 2 == 0 else ~low for r in heads]
        sl = lambda v, r: v[:, (r // 2) * LANES:(r // 2 + 1) * LANES]
        pair = lambda r: pl.ds((r // 2) * LANES, LANES)
        col = {(e, r): full(q[e]["acs"][:, r:r + 1]) for e, r in units}
        dt_col = {(e, r): full(q[e]["dt"][:, r:r + 1]) for e, r in units}
        aend = {(e, r): q[e]["aend"][:, r:r + 1] for e, r in units}
        dt_row = {(e, r): q[e]["dt_t"][r:r + 1, :] for e, r in units}
        decay = {(e, r): jnp.exp(jnp.where(q[e]["causal"], col[e, r] - q[e]["acs_t"][r:r + 1, :], -jnp.inf))
                 for e, r in units}
        ea = {u: jnp.exp(col[u]) for u in units}
        dte = {u: jnp.exp(aend[u] - col[u]) for u in units}
        ed = {u: jnp.exp(aend[u]) for u in units}
        k = {u: dte[u] * dt_col[u] for u in units}
        mp = {(e, r): cb[e] * decay[e, r] * dt_row[e, r] for e, r in units}
        xp = {(e, r): sl(xs[e], r) for e, r in units}
        dym = {(e, r): jnp.where(half[r], sl(dy[e], r), 0.0) for e, r in units}
        s_old = {(e, r): sp_ref[e, 0, :, pair(r)] for e, r in units}
        ds_old = {(e, r): ds_ref[e, :, pair(r)] for e, r in units}
        dsm = {(e, r): jnp.where(half[r], ds_old[e, r], 0.0) for e, r in units}
        gmat = {u: _dot_nt(dym[u], xp[u]) for u in units}
        t1 = {u: _dot_nt(dym[u], s_old[u]) for u in units}
        dbs = {u: _dot_nt(xp[u], dsm[u]) for u in units}
        dx = {(e, r): _dot_tn(mp[e, r], dym[e, r]) + _dot(bm[e] * k[e, r], dsm[e, r]) for e, r in units}
        ds = {(e, r): _dot_tn(cm[e] * ea[e, r], dym[e, r]) for e, r in units}
        gd = {u: gmat[u] * decay[u] for u in units}
        w0 = {(e, r): gd[e, r] * cb[e] for e, r in units}
        cs0 = {u: jnp.sum(w0[u], axis=0, keepdims=True) for u in units}
        rs = {u: jnp.sum(w0[u] * dt_row[u], axis=1, keepdims=True) for u in units}
        qv = {(e, r): jnp.sum(cm[e] * t1[e, r], axis=1, keepdims=True) for e, r in units}
        dk = {(e, r): jnp.sum(bm[e] * dbs[e, r], axis=1, keepdims=True) for e, r in units}
        ddte = {u: dk[u] * dt_col[u] for u in units}
        d_aend = {u: _sum_all(dsm[u] * s_old[u]) * ed[u] + _sum_all(ddte[u][:, 0:1] * dte[u][:, 0:1]) for u in units}
        last_row = row == CHUNK - 1
        dacs_col = {u: rs[u] + qv[u] * ea[u] - ddte[u] * dte[u] + jnp.where(last_row, d_aend[u], 0.0) for u in units}
        triu = (lane >= row).astype(F32)
        for e in ex:
            dcb, dc_acc, db_acc = zero, zero, zero
            dacs, dacs_t, ddt, ddt_t = zero, zero, zero, zero
            dskip_row = jnp.zeros((1, LANES), F32)
            for r in heads:
                u = (e, r)
                dcb = dcb + gd[u] * dt_row[u]
                dc_acc = dc_acc + ea[u] * t1[u]
                db_acc = db_acc + k[u] * dbs[u]
                dacs = jnp.where(lane == r, dacs_col[u], dacs)
                ddt = jnp.where(lane == r, dk[u] * dte[u], ddt)
                dacs_t = jnp.where(row == r, -cs0[u] * dt_row[u], dacs_t)
                ddt_t = jnp.where(row == r, cs0[u], ddt_t)
                dsk = _sum_all(jnp.where(half[r][0:1, :], sl(dskip_cols[e], r), 0.0))
                dskip_row = dskip_row + jnp.where(lane1 == r, dsk, 0.0)
            for r in range(0, HPG, 2):
                dxs_ref[e, :, pair(r)] = (dx[e, r] + dx[e, r + 1] + sl(dy[e], r) * dsk_ref[:, pair(r)]).astype(BF16)
                ed_pair = jnp.where(lane1 < HEAD_DIM, ed[e, r], ed[e, r + 1])
                ds_ref[e, :, pair(r)] = ds[e, r] + ds[e, r + 1] + ed_pair * ds_old[e, r]
            dacs = dacs + dacs_t.T
            ddt = ddt + ddt_t.T
            dda = _dot_exact(triu, dacs)
            ddt = ddt + dda * a
            da = jnp.sum(dda * q[e]["dt"], axis=0, keepdims=True)
            draw = jnp.where(valid, ddt * _sigmoid(dtr_ref[e] + dtb_ref[0]), 0.0)
            ddt_ref[e] = draw.astype(BF16)
            dsm_ref[e, 0, 0:1, :] += dskip_row
            dsm_ref[e, 0, 1:2, :] += da * a
            dsm_ref[e, 0, 2:3, :] += jnp.sum(draw, axis=0, keepdims=True)
            dc_ref[e] = (dc_acc + _dot(dcb, bm[e])).astype(BF16)
            db_ref[e] = (db_acc + _dot_tn(dcb, cm[e])).astype(BF16)

    grp_out = pl.BlockSpec((bsz, CHUNK, D_STATE), lambda g, j: (0, nc - 1 - j, g))
    grid = (N_GROUPS, nc)
    ride = _Ride(rider, body, 15, 7, 1, grid)
    outs = pl.pallas_call(
        ride.body, name=name, grid=grid,
        in_specs=[sp["xs"], sp["bm"], sp["cm"], sp["lane_blk"], sp["lane_blk"], sp["lane_blk"], sp["tr"], sp["xs"],
                  sp["xs"], sp["state"], sp["xs"], sp["grp_const"], sp["grp_const"], sp["grp_vec"], sp["grp_vec"]]
        + ride.in_specs,
        out_specs=[sp["xs"], sp["xs"], grp_out, grp_out, sp["lane_blk"],
                   pl.BlockSpec((bsz, 1, GW), lambda g, j: (0, 0, g)),
                   pl.BlockSpec((bsz, 1, 8, LANES), lambda g, j: (0, g, 0, 0))] + ride.out_specs,
        out_shape=[jax.ShapeDtypeStruct((bsz, t, D_SSM), BF16), jax.ShapeDtypeStruct((bsz, t, D_SSM), BF16),
                   jax.ShapeDtypeStruct((bsz, t, N_GROUPS * D_STATE), BF16),
                   jax.ShapeDtypeStruct((bsz, t, N_GROUPS * D_STATE), BF16),
                   jax.ShapeDtypeStruct((bsz, t, D_DT), BF16), jax.ShapeDtypeStruct((bsz, 1, D_SSM), F32),
                   jax.ShapeDtypeStruct((bsz, N_GROUPS, 8, LANES), F32)] + ride.out_shape,
        scratch_shapes=[pltpu.VMEM((bsz, D_STATE, GW), F32)] + ride.scratch,
        compiler_params=_params(*ride.semantics(("parallel", "arbitrary"))),
    )(xc, xc, xc, dtr, dt, acs, tr, z, ypre, sprev, dyn, dtb, alog, dskip, normw, *ride.args)
    return outs[:7], outs[7:]


def _input_grad(dhn, h0, w, dres, seq, *, name):
    bsz, t, d = h0.shape
    nc = t // CHUNK

    def body(dy_ref, h_ref, w_ref, dres_ref, gx_ref, head_ref, dw_ref):
        j = pl.program_id(0)

        @pl.when(j == 0)
        def _():
            dw_ref[...] = jnp.zeros_like(dw_ref)

        for e in range(bsz):
            x, dyv = h_ref[e], dy_ref[e]
            r = lax.rsqrt(jnp.mean(x * x, axis=-1, keepdims=True) + EPS)
            g = dyv * w_ref[...]
            dx = r * (g - x * (r * r) * jnp.mean(g * x, axis=-1, keepdims=True)) + dres_ref[e]
            dw_ref[...] += jnp.sum(dyv * x * r, axis=0, keepdims=True)
            gx_ref[e] = dx

        @pl.when(j == 0)
        def _():
            head_ref[...] = gx_ref[...]

    row = pl.BlockSpec((bsz, CHUNK, d), lambda j: (0, j, 0))
    return pl.pallas_call(
        body, name=name, grid=(nc,),
        in_specs=[row, row, pl.BlockSpec((1, d), lambda j: (0, 0)), row],
        out_specs=[pl.BlockSpec((bsz, CHUNK, d), lambda j: (0, jnp.maximum(j - 1, 0), 0)),
                   pl.BlockSpec((bsz, CHUNK, d), lambda j: (0, 0, 0)), pl.BlockSpec((1, d), lambda j: (0, 0))],
        out_shape=[jax.ShapeDtypeStruct((bsz, seq, d), F32), jax.ShapeDtypeStruct((bsz, CHUNK, d), F32),
                   jax.ShapeDtypeStruct((1, d), F32)],
        compiler_params=_params("arbitrary"),
    )(dhn, h0, w, dres)


def _remote(src, dst, send_sem, recv_sem, dev):
    return pltpu.make_async_remote_copy(src_ref=src, dst_ref=dst, send_sem=send_sem, recv_sem=recv_sem,
                                        device_id=dev, device_id_type=MESH)


def _position():
    return lax.axis_index("x"), lax.axis_index("y"), lax.axis_index("c")


def _other_chips(pos):
    x, y, _ = pos
    return [(1 - x, y), (x, 1 - y), (1 - x, 1 - y)]


class _Gather:
    def __init__(self, arrs):
        n = len(arrs)
        self.args, self.n_in, self.n_out = list(arrs), n, n
        self.split = [a.ndim == 2 and a.shape[1] % (2 * LANES) == 0 for a in arrs]
        self.out_shape = [jax.ShapeDtypeStruct((4,) + a.shape, a.dtype) for a in arrs]
        self.scratch = [pltpu.SemaphoreType.DMA((3 * n,)), pltpu.SemaphoreType.DMA((3 * n,)),
                        pltpu.SemaphoreType.DMA((n,)), pltpu.SemaphoreType.DMA((3 * n,)),
                        pltpu.SemaphoreType.DMA((3 * n,))]

    def _copies(self, pos, ins, outs, sems):
        send_sems, recv_sems, loc_sems, pass_send_sems, pass_recv_sems = sems
        x, y, c = pos
        me, sibling = 2 * x + y, (x, y, 1 - c)
        local = [pltpu.make_async_copy(ins[i], outs[i].at[me], loc_sems.at[i]) for i in range(self.n_in)]
        sends, recvs, passes, pass_recvs = [], [], [], []
        for i in range(self.n_in):
            half = self.args[i].shape[1] // 2 if self.split[i] else None
            for k, (px, py) in enumerate(_other_chips(pos)):
                them = 2 * px + py
                sems_k = (send_sems.at[3 * i + k], recv_sems.at[3 * i + k], (px, py, c))
                if half is None:
                    sends.append(_remote(ins[i], outs[i].at[me], *sems_k))
                    recvs.append(_remote(ins[i], outs[i].at[them], *sems_k))
                    passes.append(None)
                    continue
                mine = pl.ds(pl.multiple_of(c * half, LANES), half)
                other = pl.ds(pl.multiple_of((1 - c) * half, LANES), half)
                sends.append(_remote(ins[i].at[:, mine], outs[i].at[me, :, mine], *sems_k))
                recvs.append(_remote(ins[i].at[:, mine], outs[i].at[them, :, mine], *sems_k))
                pass_k = (pass_send_sems.at[3 * i + k], pass_recv_sems.at[3 * i + k], sibling)
                passes.append(_remote(outs[i].at[them, :, mine], outs[i].at[them, :, mine], *pass_k))
                pass_recvs.append(_remote(outs[i].at[them, :, other], outs[i].at[them, :, other], *pass_k))
        return local, sends, recvs, passes, pass_recvs

    def start(self, pos, ins, outs, sems):
        local, sends = self._copies(pos, ins, outs, sems)[:2]
        for cp in local + sends:
            cp.start()

    def relay(self, pos, ins, outs, sems):
        _, _, recvs, passes, _ = self._copies(pos, ins, outs, sems)
        for cp, onward in zip(recvs, passes):
            if onward is not None:
                cp.wait_recv()
                onward.start()

    def finish(self, pos, ins, outs, sems):
        local, sends, recvs, passes, pass_recvs = self._copies(pos, ins, outs, sems)
        for cp, onward in zip(recvs, passes):
            if onward is None:
                cp.wait_recv()
        for cp in pass_recvs:
            cp.wait_recv()
        for cp in sends + [p for p in passes if p is not None]:
            cp.wait_send()
        for cp in local:
            cp.wait()


class _Exchange:
    FLIPS = [(fx, fy, fc) for fx in (0, 1) for fy in (0, 1) for fc in (0, 1)][1:]

    def __init__(self, big, small=None):
        n = len(big)
        self.n_big, self.has_small = n, small is not None
        self.args = list(big) + ([small] if self.has_small else [])
        self.n_in = self.n_out = len(self.args)
        self.out_shape = [jax.ShapeDtypeStruct(a.shape, a.dtype) for a in big]
        self.scratch = [pltpu.SemaphoreType.DMA((max(3 * n, 1),)), pltpu.SemaphoreType.DMA((max(3 * n, 1),)),
                        pltpu.SemaphoreType.DMA((n + 1,))]
        if self.has_small:
            self.out_shape.append(jax.ShapeDtypeStruct((8,) + small.shape, small.dtype))
            self.scratch += [pltpu.SemaphoreType.DMA((7,)), pltpu.SemaphoreType.DMA((7,))]

    def _copies(self, pos, ins, outs, sems):
        x, y, c = pos
        me, me8 = 2 * x + y, 4 * x + 2 * y + c
        local, sends, recvs = [], [], []
        for i in range(self.n_big):
            local.append(pltpu.make_async_copy(ins[i].at[me], outs[i].at[me], sems[2].at[i]))
            for k, (px, py) in enumerate(_other_chips(pos)):
                sems_k = (sems[0].at[3 * i + k], sems[1].at[3 * i + k], (px, py, c))
                sends.append(_remote(ins[i].at[2 * px + py], outs[i].at[me], *sems_k))
                recvs.append(_remote(ins[i].at[me], outs[i].at[2 * px + py], *sems_k))
        if self.has_small:
            small, landed = ins[self.n_big], outs[self.n_big]
            local.append(pltpu.make_async_copy(small, landed.at[me8], sems[2].at[self.n_big]))
            for k, (fx, fy, fc) in enumerate(self.FLIPS):
                peer = (x ^ fx, y ^ fy, c ^ fc)
                sems_k = (sems[3].at[k], sems[4].at[k], peer)
                sends.append(_remote(small, landed.at[me8], *sems_k))
                recvs.append(_remote(small, landed.at[4 * peer[0] + 2 * peer[1] + peer[2]], *sems_k))
        return local, sends, recvs, [None] * len(recvs), []

    start = _Gather.start
    relay = _Gather.relay
    finish = _Gather.finish


class _Swap:
    def __init__(self, arrs):
        n = len(arrs)
        self.args, self.n_in, self.n_out = list(arrs), n, n
        self.out_shape = [jax.ShapeDtypeStruct(a.shape, a.dtype) for a in arrs]
        self.scratch = [pltpu.SemaphoreType.DMA((n,)), pltpu.SemaphoreType.DMA((n,))]

    def _copies(self, pos, ins, outs, sems):
        x, y, c = pos
        both = [_remote(ins[i], outs[i], sems[0].at[i], sems[1].at[i], (x, y, 1 - c)) for i in range(self.n_in)]
        return [], both, both, [None] * len(both), []

    start = _Gather.start
    relay = _Gather.relay
    finish = _Gather.finish


class _Both:
    def __init__(self, first, second):
        self.parts = (first, second)
        self.args = first.args + second.args
        self.n_in, self.n_out = first.n_in + second.n_in, first.n_out + second.n_out
        self.out_shape = first.out_shape + second.out_shape
        self.scratch = first.scratch + second.scratch

    def _each(self, method, pos, ins, outs, sems):
        i = o = s = 0
        for part in self.parts:
            getattr(part, method)(pos, ins[i:i + part.n_in], outs[o:o + part.n_out], sems[s:s + len(part.scratch)])
            i, o, s = i + part.n_in, o + part.n_out, s + len(part.scratch)

    def start(self, *a):
        self._each("start", *a)

    def relay(self, *a):
        self._each("relay", *a)

    def finish(self, *a):
        self._each("finish", *a)


def _comm(rider, *, name):
    a, b = rider.n_in, rider.n_in + rider.n_out

    def body(*refs):
        pos = _position()
        rider.start(pos, refs[:a], refs[a:b], refs[b:])
        rider.relay(pos, refs[:a], refs[a:b], refs[b:])
        rider.finish(pos, refs[:a], refs[a:b], refs[b:])

    return pl.pallas_call(body, name=name, in_specs=[ANY] * rider.n_in, out_specs=[ANY] * rider.n_out,
                          out_shape=rider.out_shape, scratch_shapes=rider.scratch)(*rider.args)


class _Ride:
    RELAY_AT = 0.8

    def __init__(self, rider, body, n_in, n_out, n_scratch, grid):
        self.rider = rider
        self.args = rider.args if rider else []
        self.in_specs = [ANY] * rider.n_in if rider else []
        self.out_specs = [ANY] * rider.n_out if rider else []
        self.out_shape = rider.out_shape if rider else []
        self.scratch = rider.scratch if rider else []
        self.body = self._wrap(body, n_in, n_out, n_scratch, grid) if rider else body

    def semantics(self, sem):
        return ("arbitrary",) * len(sem) if self.rider else sem

    def _wrap(self, body, n_in, n_out, n_scratch, grid):
        rider = self.rider
        a = n_in
        b = a + rider.n_in
        c = b + n_out
        d = c + rider.n_out
        e = d + n_scratch

        def wrapped(*refs):
            pos = _position()
            ids = [pl.program_id(i) for i in range(len(grid))]
            step, total = 0, 1
            for i, g in zip(ids, grid):
                step, total = step * g + i, total * g

            @pl.when(step == 0)
            def _():
                rider.start(pos, refs[a:b], refs[c:d], refs[e:])

            body(*refs[:a], *refs[b:c], *refs[d:e])

            @pl.when(step == int(self.RELAY_AT * (total - 1)))
            def _():
                rider.relay(pos, refs[a:b], refs[c:d], refs[e:])

            @pl.when(step == total - 1)
            def _():
                rider.finish(pos, refs[a:b], refs[c:d], refs[e:])

        return wrapped


def _elementwise_tiles(r, c):
    if r % 8 == 0 and r * c > 65536:
        tm = _pick(r, (256, 128, 64, 16, 8))
        return (tm, c), r // tm, lambda i: (i, 0)
    if r % 8 and c % 256 == 0 and r * c > 65536:
        return (r, 256), c // 256, lambda i: (0, i)
    return (r, c), 1, lambda i: (0, 0)


def _chip_sum(landed, *, name):
    _, r, c = landed.shape
    blk, steps, at = _elementwise_tiles(r, c)

    def body(land_ref, o_ref):
        acc = land_ref[0].astype(F32)
        for jchip in range(1, 4):
            acc = acc + land_ref[jchip].astype(F32)
        o_ref[...] = acc

    return pl.pallas_call(
        body, name=name, grid=(steps,), in_specs=[pl.BlockSpec((4,) + blk, lambda i: (0,) + at(i))],
        out_specs=pl.BlockSpec(blk, at), out_shape=jax.ShapeDtypeStruct((r, c), F32),
        compiler_params=_params("parallel"),
    )(landed)


def _device_sum(parts, *, name):
    _, r, c = parts.shape

    def body(p_ref, o_ref):
        acc = p_ref[0]
        for d in range(1, 8):
            acc = acc + p_ref[d]
        o_ref[...] = acc

    return pl.pallas_call(body, name=name, out_shape=jax.ShapeDtypeStruct((r, c), F32))(parts)


def _adamw_math(w, g, m, v):
    m = ADAM_B1 * m + (1.0 - ADAM_B1) * g
    v = ADAM_B2 * v + (1.0 - ADAM_B2) * (g * g)
    m_hat = m / (1.0 - ADAM_B1 ** ADAM_STEP)
    v_hat = v / (1.0 - ADAM_B2 ** ADAM_STEP)
    return -ADAM_LR * (m_hat / (jnp.sqrt(v_hat) + ADAM_EPS) + ADAM_WD * w), m, v


def _adamw(w, g_parts, m, v, *, name):
    r, c = w.shape
    shape, steps, at = _elementwise_tiles(r, c)
    n_g = len(g_parts)

    def body(*refs):
        w_ref, m_ref, v_ref = refs[n_g:n_g + 3]
        g_ref, d_ref, nm_ref, nv_ref = refs[n_g + 3:]
        g = refs[0][...]
        for p in refs[1:n_g]:
            g = g + p[...]
        g_ref[...] = g
        d_ref[...], nm_ref[...], nv_ref[...] = _adamw_math(w_ref[...], g, m_ref[...], v_ref[...])

    blk = pl.BlockSpec(shape, at)
    return pl.pallas_call(
        body, name=name, grid=(steps,), in_specs=[blk] * (n_g + 3), out_specs=[blk] * 4,
        out_shape=[jax.ShapeDtypeStruct((r, c), F32)] * 4, compiler_params=_params("parallel"),
    )(*g_parts, w, m, v)


def _pad_heads(v):
    return jnp.pad(v.reshape(N_GROUPS, 1, HPG), ((0, 0), (0, 0), (0, LANES - HPG)))


def _unpad_heads(v):
    return v[:, :HPG].reshape(1, N_HEADS)


_SMALL_EARLY = [("pool_w", (512, 128)), ("pool_scale", (1, 512)), ("conv_w", (4, D_XBC)), ("conv_b", (1, D_XBC)),
                ("dt_bias", (1, N_HEADS)), ("a_log", (1, N_HEADS)), ("d_skip", (1, N_HEADS)), ("ssm_norm_w", (1, D_SSM)),
                ("norm_ffn_w", (1, 1024)), ("norm_f_w", (1, 1024))]
_SMALL_LATE = [("norm_mix_w", (1, 1024)), ("meta", (N_META, 1024)), ("loss", (1, 1))]


def _pack_small(grads, layout):
    rows = []
    for nm, shape in layout:
        flat = grads[nm].reshape(-1)
        rows.append(jnp.pad(flat, (0, (-flat.size) % LANES)).reshape(-1, LANES))
    packed = jnp.concatenate(rows, axis=0)
    return jnp.pad(packed, ((0, (-packed.shape[0]) % 8), (0, 0)))


def _unpack_small(packed, layout):
    out, r0 = {}, 0
    for nm, shape in layout:
        size = shape[0] * shape[1]
        nrow = -(-size // LANES)
        out[nm] = packed[r0:r0 + nrow].reshape(-1)[:size].reshape(shape)
        r0 += nrow
    return out


def kernel(x, meta, norm_mix_w, w_in, pool_w, pool_scale, conv_w, conv_b, dt_bias, a_log, d_skip, ssm_norm_w, w_out, norm_ffn_w, w_ff1, w_ff2, norm_f_w, loss_target, m_meta, m_norm_mix_w, m_w_in, m_pool_w, m_pool_scale, m_conv_w, m_conv_b, m_dt_bias, m_a_log, m_d_skip, m_ssm_norm_w, m_w_out, m_norm_ffn_w, m_w_ff1, m_w_ff2, m_norm_f_w, v_meta, v_norm_mix_w, v_w_in, v_pool_w, v_pool_scale, v_conv_w, v_conv_b, v_dt_bias, v_a_log, v_d_skip, v_ssm_norm_w, v_w_out, v_norm_ffn_w, v_w_ff1, v_w_ff2, v_norm_f_w):
    bsz, seq, d = x.shape
    t = seq + CHUNK
    n = bsz * t
    chip = 2 * lax.axis_index("x") + lax.axis_index("y")
    d_in = w_in.shape[2] * 4

    g_conv, g_meta = _comm(_Gather([conv_w[0], meta]), name="gather_small")
    convw = g_conv.transpose(1, 0, 2).reshape(CONV_W, D_XBC)
    meta_full = g_meta.transpose(1, 0, 2).reshape(N_META, d)
    (h0, hn1), (g_in,) = _embed_norm(x, meta_full, norm_mix_w, name="embed_norm",
                                     rider=_Gather([w_in[0].T.astype(BF16)]))
    h0f, hn1 = h0.reshape(n, d), hn1.reshape(n, d)
    late_weights = _Gather([w_out[0].astype(BF16), w_ff1[0].astype(BF16), w_ff2[0].astype(BF16)])
    win = g_in.reshape(d_in, d)
    wu, wz = win[:D_POOL], win[D_POOL:D_POOL + D_SSM]
    wx = win[D_POOL + D_SSM:D_POOL + D_SSM + D_XBC]
    wdt = jnp.pad(win[D_POOL + D_SSM + D_XBC:].reshape(N_GROUPS, HPG, d),
                  ((0, 0), (0, LANES - HPG), (0, 0))).reshape(D_DT, d)
    dtb, alog = _pad_heads(dt_bias), _pad_heads(a_log)
    dskip = jnp.repeat(d_skip, HEAD_DIM, axis=1)
    poolw = pool_w[0]

    u, z, dtr, dt_, acs_, tr_ = _proj_uz_dt(hn1, wu, wz, wdt, dtb, alog, t // CHUNK, name="proj_uzdt")
    xbc, xc = _proj_conv(hn1, wx, convw, conv_b, name="proj_xbc")
    ypool = _pool_fwd(u.reshape(bsz, t, D_POOL), poolw, pool_scale, name="pool_fwd")
    xbc3 = xbc.reshape(bsz, t, D_XBC)
    xc = xc.reshape(bsz, t, D_XBC)
    z3, dtr3 = z.reshape(bsz, t, D_SSM), dtr.reshape(bsz, t, D_DT)
    dt3, acs3 = dt_.reshape(bsz, t, D_DT), acs_.reshape(bsz, t, D_DT)
    tr3 = tr_.reshape(bsz, t // CHUNK, N_GROUPS, 16, LANES)
    (yn, ypre, sprev), (g_out, g_ff1, g_ff2) = _ssd_fwd(xc, dt3, acs3, tr3, z3, dskip, ssm_norm_w, name="ssd_fwd",
                                                        rider=late_weights)
    wo = g_out.reshape(D_POOL + D_SSM, d)
    wo_p, wo_s = wo[:D_POOL], wo[D_POOL:]
    w1 = g_ff1
    w2 = g_ff2.reshape(D_FF, d)
    ypool_f, yn_f = ypool.reshape(n, D_POOL), yn.reshape(n, D_SSM)
    add = lambda r, e: r + e
    h1, hn2 = _mm([ypool_f, yn_f], [wo_p, wo_s], name="out_proj", post=add, extras=(h0f,), norm_w=norm_ffn_w)
    act = _mm(hn2, w1, name="ff1", out_dtype=BF16)
    relu2 = lambda a: jnp.square(jnp.maximum(a, 0))
    dh2f, dh2bf, loss_acc, d_norm_f = _ff2_loss(act, w2, h1, loss_target, norm_f_w.reshape(1, d), t // CHUNK, name="ff2")

    dact = _mm(dh2bf, w2, name="ff2_bwd", nt=True, post=lambda r, a: r * (2.0 * jnp.maximum(a, 0).astype(F32)),
               extras=(act,), out_dtype=BF16)
    d_w2 = _mm_tn(act, dh2bf, name="ff2_dw", tk=2048, tn=1024, pre=relu2)
    d_w1 = _mm_tn(hn2, dact, name="ff1_dw", tk=1024, tn=2048, slab=D_FF // 4)
    dh1, dh1b, d_norm_ffn = _mm_rms_bwd(dact, w1, h1, norm_ffn_w, dh2f, name="ff1_bwd")
    dypool, dyn = _mm_fanout(dh1b, [wo_p, wo_s], name="out_proj_bwd")
    d_wo = _mm_tn_cat([ypool_f, yn_f], dh1b, name="out_proj_dw")
    big_late = [d_wo.reshape(4, (D_POOL + D_SSM) // 4, d),
                d_w1, d_w2.reshape(4, D_FF // 4, d)]
    (dz, dxs, dbm, dcm, ddtr, d_nw, d_heads), landed_late = _ssd_bwd(
        xc, dtr3, dt3, acs3, tr3, z3, ypre, sprev, dyn.reshape(bsz, t, D_SSM), dtb, alog, dskip, ssm_norm_w, name="ssd_bwd",
        rider=_Exchange(big_late))
    mine_late = [_chip_sum(l, name=f"chip_sum_{i + 1}") for i, l in enumerate(landed_late)]
    (dxbc, d_convwb), theirs_late = _conv_bwd(xbc3, dxs, dbm, dcm, convw, conv_b, name="conv_bwd",
                                               rider=_Swap(mine_late))
    du, d_poolw, d_poolsc = _pool_bwd(u.reshape(bsz, t, D_POOL), dypool.reshape(bsz, t, D_POOL), poolw, pool_scale,
                                      name="pool_bwd")
    duf, dzf, dxbcf, ddtrf = du.reshape(n, D_POOL), dz.reshape(n, D_SSM), dxbc.reshape(n, D_XBC), ddtr.reshape(n, D_DT)
    heads = jnp.sum(d_heads, axis=0)
    small_early = _pack_small({
        "pool_w": d_poolw, "pool_scale": d_poolsc,
        "conv_w": jnp.sum(d_convwb[:, :CONV_W], axis=0), "conv_b": jnp.sum(d_convwb[:, CONV_W:CONV_W + 1], axis=0),
        "dt_bias": _unpad_heads(heads[:, 2]), "a_log": _unpad_heads(heads[:, 1]), "d_skip": _unpad_heads(heads[:, 0]),
        "ssm_norm_w": jnp.sum(d_nw, axis=0), "norm_ffn_w": d_norm_ffn, "norm_f_w": d_norm_f}, _SMALL_EARLY)
    d_wuzdt = _mm_tn_cat([duf, dzf, ddtrf], hn1, name="proj_uzdt_dw", budget=WIDE_BUDGET)
    d_wx, (early_all,) = _mm_tn(dxbcf, hn1, name="proj_xbc_dw", tk=1280, tn=1024, rider=_Exchange([], small_early))
    d_wdt = d_wuzdt[D_POOL + D_SSM:].reshape(N_GROUPS, LANES, d)[:, :HPG].reshape(N_HEADS, d)
    d_win = jnp.concatenate([d_wuzdt[:D_POOL + D_SSM], d_wx, d_wdt], axis=0)
    big_in = d_win.reshape(4, d_in // 4, d)
    dhn1, (landed_in,) = _mm([duf, dzf, dxbcf, ddtrf], [wu, wz, wx, wdt], name="proj_bwd",
                             rider=_Exchange([big_in]))
    grad_x, d_head_rows, d_norm_mix = _input_grad(
        dhn1.reshape(bsz, t, d), h0, norm_mix_w, dh1.reshape(bsz, t, d), seq, name="input_grad")

    small_late = _pack_small({"norm_mix_w": d_norm_mix, "meta": jnp.sum(d_head_rows[:, PAD:], axis=0),
                              "loss": loss_acc[0:1, 0:1]}, _SMALL_LATE)
    mine_in = _chip_sum(landed_in, name="chip_sum_0")
    late_all, theirs_in = _comm(_Both(_Exchange([], small_late), _Swap([mine_in])), name="exchange_tail")
    mine, theirs = [mine_in] + mine_late, [theirs_in] + list(theirs_late)
    gsmall = {**_unpack_small(_device_sum(early_all, name="device_sum_early"), _SMALL_EARLY),
              **_unpack_small(_device_sum(late_all, name="device_sum_late"), _SMALL_LATE)}
    gsmall["conv_w"] = lax.dynamic_slice_in_dim(gsmall["conv_w"], chip * (D_XBC // 4), D_XBC // 4, axis=1)
    gsmall["meta"] = lax.dynamic_slice_in_dim(gsmall["meta"], chip * (d // 4), d // 4, axis=1)
    loss = gsmall["loss"][0, 0]

    given = dict(meta=(meta, m_meta, v_meta), norm_mix_w=(norm_mix_w, m_norm_mix_w, v_norm_mix_w),
                 w_in=(w_in, m_w_in, v_w_in), pool_w=(pool_w, m_pool_w, v_pool_w),
                 pool_scale=(pool_scale, m_pool_scale, v_pool_scale), conv_w=(conv_w, m_conv_w, v_conv_w),
                 conv_b=(conv_b, m_conv_b, v_conv_b), dt_bias=(dt_bias, m_dt_bias, v_dt_bias),
                 a_log=(a_log, m_a_log, v_a_log), d_skip=(d_skip, m_d_skip, v_d_skip),
                 ssm_norm_w=(ssm_norm_w, m_ssm_norm_w, v_ssm_norm_w), w_out=(w_out, m_w_out, v_w_out),
                 norm_ffn_w=(norm_ffn_w, m_norm_ffn_w, v_norm_ffn_w), w_ff1=(w_ff1, m_w_ff1, v_w_ff1),
                 w_ff2=(w_ff2, m_w_ff2, v_w_ff2), norm_f_w=(norm_f_w, m_norm_f_w, v_norm_f_w))
    big_names = ["w_in", "w_out", "w_ff1", "w_ff2"]
    results = {}
    for nm, (w, m, v) in given.items():
        if nm in big_names:
            i = big_names.index(nm)
            parts, shape2 = (mine[i], theirs[i]), mine[i].shape
        else:
            parts, shape2 = (gsmall[nm],), gsmall[nm].shape
        if nm == "w_in":
            outs = _adamw(w[0].T, parts, m[0].T, v[0].T, name=f"adamw_{nm}")
            results[nm] = [o.T[None] for o in outs]
        else:
            outs = _adamw(w.reshape(shape2), parts, m.reshape(shape2), v.reshape(shape2), name=f"adamw_{nm}")
            results[nm] = [o.reshape(w.shape) for o in outs]
    order = list(given)
    return (loss, grad_x, *[results[nm][0] for nm in order], *[results[nm][1] for nm in order],
            *[results[nm][2] for nm in order], *[results[nm][3] for nm in order])
```
